```python
import math
import jax, jax.numpy as jnp
from jax import lax
import numpy as np

D_MODEL = 1024
BATCH = 8
SEQ = 8192
DEPTH = 1

CONV_DIM = D_MODEL
CONV_WIDTH = 3
HEAD_DIM = 64
ATTN_SLOTS = 8
WINDOWS = (128, 512, 2048)
DILATIONS = (1, 4, 16)
N_GROUPS = len(WINDOWS)
ATTN_HEADS = N_GROUPS * ATTN_SLOTS
ATTN_DIM = ATTN_HEADS * HEAD_DIM
ATTN_OUT = ATTN_SLOTS * HEAD_DIM
Q_BLOCK = 128
EPS = 1e-6
NEG_INF = -1e30

IN_SIZES = (CONV_DIM, CONV_DIM, CONV_DIM, CONV_DIM,
            ATTN_DIM, ATTN_DIM, ATTN_DIM,
            ATTN_OUT,
            D_MODEL, D_MODEL)
IN_TOTAL = sum(IN_SIZES)
IN_SPLITS = tuple(int(v) for v in np.cumsum(IN_SIZES)[:-1])

kernel_name = "hybrid_shortconv_dilated_swa_gated_merge"


def rms_norm(t, w):
    tf = t.astype(jnp.float32)
    tf = tf * lax.rsqrt(jnp.mean(tf * tf, axis=-1, keepdims=True) + EPS)
    return (tf * w.astype(jnp.float32)).astype(t.dtype)


def window_attend(q, k, v, w_sub):
    n, length, hd = q.shape
    blk = math.gcd(Q_BLOCK, length)
    nblk = length // blk
    span = blk + w_sub
    kp = jnp.pad(k, ((0, 0), (w_sub, 0), (0, 0)))
    vp = jnp.pad(v, ((0, 0), (w_sub, 0), (0, 0)))
    idx = jnp.arange(nblk)[:, None] * blk + jnp.arange(span)[None, :]
    kb = kp[:, idx]
    vb = vp[:, idx]
    qb = q.reshape(n, nblk, blk, hd)
    scores = jnp.einsum('nbqd,nbkd->nbqk', qb, kb, preferred_element_type=jnp.float32) * (hd ** -0.5)
    dist = jnp.arange(blk)[:, None] + w_sub - jnp.arange(span)[None, :]
    valid = (dist >= 0) & (dist <= w_sub)
    valid = valid[None, :, :] & (idx >= w_sub)[:, None, :]
    scores = jnp.where(valid, scores, NEG_INF)
    mx = jnp.max(scores, axis=-1)
    p = jnp.exp(scores - mx[..., None])
    s = jnp.sum(p, axis=-1)
    o = jnp.einsum('nbqk,nbkd->nbqd', p, vb.astype(jnp.float32))
    return o.reshape(n, length, hd), mx.reshape(n, length), s.reshape(n, length)


def dilated_attend(q, k, v, window, dilation):
    b, s_len, h, hd = q.shape
    length = s_len // dilation
    w_sub = window // dilation

    def to_sub(t):
        return t.reshape(b, length, dilation, h, hd).transpose(0, 2, 3, 1, 4).reshape(b * dilation * h, length, hd)

    o, m, s = window_attend(to_sub(q), to_sub(k), to_sub(v), w_sub)
    o = o.reshape(b, dilation, h, length, hd).transpose(0, 3, 1, 2, 4).reshape(b, s_len, h, hd)
    m = m.reshape(b, dilation, h, length).transpose(0, 3, 1, 2).reshape(b, s_len, h)
    s = s.reshape(b, dilation, h, length).transpose(0, 3, 1, 2).reshape(b, s_len, h)
    return o, m, s


def _fwd_setup_inputs(seed: int = 0) -> dict:
    key = jax.random.key(seed)
    ks = jax.random.split(key, 14)
    f32 = jnp.float32
    nrm = lambda k, shape: jax.random.normal(k, shape, f32)
    x = nrm(ks[0], (BATCH, SEQ, D_MODEL))
    c = nrm(ks[1], (BATCH, D_MODEL))
    w_ada = nrm(ks[2], (DEPTH, D_MODEL, 3 * D_MODEL)) * D_MODEL ** -0.5
    b_ada = 0.02 * nrm(ks[3], (DEPTH, 3 * D_MODEL))
    norm_w = 1.0 + 0.1 * nrm(ks[4], (DEPTH, D_MODEL))
    w_in = nrm(ks[5], (DEPTH, D_MODEL, IN_TOTAL)) * D_MODEL ** -0.5
    conv_w = nrm(ks[6], (DEPTH, CONV_WIDTH, CONV_DIM)) * CONV_WIDTH ** -0.5
    q_norm_w = 1.0 + 0.1 * nrm(ks[7], (DEPTH, HEAD_DIM))
    k_norm_w = 1.0 + 0.1 * nrm(ks[8], (DEPTH, HEAD_DIM))
    w_br_conv = nrm(ks[9], (DEPTH, CONV_DIM, D_MODEL)) * CONV_DIM ** -0.5
    w_br_attn = nrm(ks[10], (DEPTH, ATTN_OUT, D_MODEL)) * ATTN_OUT ** -0.5
    w_out = nrm(ks[11], (DEPTH, D_MODEL, D_MODEL)) * D_MODEL ** -0.5
    return {"x": x, "c": c, "w_ada": w_ada, "b_ada": b_ada, "norm_w": norm_w, "w_in": w_in,
            "conv_w": conv_w, "q_norm_w": q_norm_w, "k_norm_w": k_norm_w,
            "w_br_conv": w_br_conv, "w_br_attn": w_br_attn, "w_out": w_out}


def _fwd_reference(x, c, w_ada, b_ada, norm_w, w_in, conv_w, q_norm_w, k_norm_w, w_br_conv, w_br_attn, w_out):
    b, s_len, _ = x.shape
    for l in range(DEPTH):
        mod = jax.nn.silu(c) @ w_ada[l] + b_ada[l]
        shift, scale, gate = jnp.split(mod[:, None, :], 3, axis=-1)
        h = rms_norm(x, norm_w[l]) * (1 + scale) + shift

        proj = h @ w_in[l]
        b_a, c_a, x_a, z_a, q, k, v, z_b, g_a, g_b = jnp.split(proj, IN_SPLITS, axis=-1)

        u = c_a * x_a
        conv = lax.conv_general_dilated(u, conv_w[l][:, None, :], window_strides=(1,),
                                        padding=[(CONV_WIDTH - 1, 0)],
                                        dimension_numbers=('NWC', 'WIO', 'NWC'),
                                        feature_group_count=CONV_DIM)
        y_a = b_a * conv * jax.nn.silu(z_a)

        q = rms_norm(q.reshape(b, s_len, N_GROUPS, ATTN_SLOTS, HEAD_DIM), q_norm_w[l])
        k = rms_norm(k.reshape(b, s_len, N_GROUPS, ATTN_SLOTS, HEAD_DIM), k_norm_w[l])
        v = v.reshape(b, s_len, N_GROUPS, ATTN_SLOTS, HEAD_DIM)
        outs, maxes, sums = [], [], []
        for g in range(N_GROUPS):
            o_g, m_g, s_g = dilated_attend(q[:, :, g], k[:, :, g], v[:, :, g], WINDOWS[g], DILATIONS[g])
            outs.append(o_g)
            maxes.append(m_g)
            sums.append(s_g)
        o_all = jnp.stack(outs, axis=0)
        m_all = jnp.stack(maxes, axis=0)
        s_all = jnp.stack(sums, axis=0)
        wgt = jnp.exp(m_all - jnp.max(m_all, axis=0, keepdims=True))
        attn = jnp.sum(wgt[..., None] * o_all, axis=0) / jnp.sum(wgt * s_all, axis=0)[..., None]
        y_b = attn.reshape(b, s_len, ATTN_OUT).astype(x.dtype) * jax.nn.silu(z_b)

        merged = jax.nn.sigmoid(g_a) * (y_a @ w_br_conv[l]) + jax.nn.sigmoid(g_b) * (y_b @ w_br_attn[l])
        x = x + gate * (merged @ w_out[l])
    return x


import jax as _jax
import jax.numpy as _jnp

TWIN_FORMAT = 'train_step'
FWD_PARAMS = ['x', 'c', 'w_ada', 'b_ada', 'norm_w', 'w_in', 'conv_w', 'q_norm_w', 'k_norm_w', 'w_br_conv', 'w_br_attn', 'w_out']
TWIN_WEIGHTS = ['w_ada', 'b_ada', 'norm_w', 'w_in', 'conv_w', 'q_norm_w', 'k_norm_w', 'w_br_conv', 'w_br_attn', 'w_out']
TWIN_DIFF_INPUT = 'x'
TWIN_INPUTS = ['x', 'c', 'w_ada', 'b_ada', 'norm_w', 'w_in', 'conv_w', 'q_norm_w', 'k_norm_w', 'w_br_conv', 'w_br_attn', 'w_out', 'loss_target', 'm_w_ada', 'm_b_ada', 'm_norm_w', 'm_w_in', 'm_conv_w', 'm_q_norm_w', 'm_k_norm_w', 'm_w_br_conv', 'm_w_br_attn', 'm_w_out', 'v_w_ada', 'v_b_ada', 'v_norm_w', 'v_w_in', 'v_conv_w', 'v_q_norm_w', 'v_k_norm_w', 'v_w_br_conv', 'v_w_br_attn', 'v_w_out']
TWIN_OUTPUTS = ['loss', 'grad_x', 'grad_w_ada', 'grad_b_ada', 'grad_norm_w', 'grad_w_in', 'grad_conv_w', 'grad_q_norm_w', 'grad_k_norm_w', 'grad_w_br_conv', 'grad_w_br_attn', 'grad_w_out', 'delta_w_ada', 'delta_b_ada', 'delta_norm_w', 'delta_w_in', 'delta_conv_w', 'delta_q_norm_w', 'delta_k_norm_w', 'delta_w_br_conv', 'delta_w_br_attn', 'delta_w_out', 'new_m_w_ada', 'new_m_b_ada', 'new_m_norm_w', 'new_m_w_in', 'new_m_conv_w', 'new_m_q_norm_w', 'new_m_k_norm_w', 'new_m_w_br_conv', 'new_m_w_br_attn', 'new_m_w_out', 'new_v_w_ada', 'new_v_b_ada', 'new_v_norm_w', 'new_v_w_in', 'new_v_conv_w', 'new_v_q_norm_w', 'new_v_k_norm_w', 'new_v_w_br_conv', 'new_v_w_br_attn', 'new_v_w_out']
TWIN_LEAF_KINDS = {'loss': 'loss', 'grad_x': 'grad_x', 'grad_w_ada': 'grad_w', 'grad_b_ada': 'grad_w', 'grad_norm_w': 'grad_w', 'grad_w_in': 'grad_w', 'grad_conv_w': 'grad_w', 'grad_q_norm_w': 'grad_w', 'grad_k_norm_w': 'grad_w', 'grad_w_br_conv': 'grad_w', 'grad_w_br_attn': 'grad_w', 'grad_w_out': 'grad_w', 'delta_w_ada': 'delta_w', 'delta_b_ada': 'delta_w', 'delta_norm_w': 'delta_w', 'delta_w_in': 'delta_w', 'delta_conv_w': 'delta_w', 'delta_q_norm_w': 'delta_w', 'delta_k_norm_w': 'delta_w', 'delta_w_br_conv': 'delta_w', 'delta_w_br_attn': 'delta_w', 'delta_w_out': 'delta_w', 'new_m_w_ada': 'new_m', 'new_m_b_ada': 'new_m', 'new_m_norm_w': 'new_m', 'new_m_w_in': 'new_m', 'new_m_conv_w': 'new_m', 'new_m_q_norm_w': 'new_m', 'new_m_k_norm_w': 'new_m', 'new_m_w_br_conv': 'new_m', 'new_m_w_br_attn': 'new_m', 'new_m_w_out': 'new_m', 'new_v_w_ada': 'new_v', 'new_v_b_ada': 'new_v', 'new_v_norm_w': 'new_v', 'new_v_w_in': 'new_v', 'new_v_conv_w': 'new_v', 'new_v_q_norm_w': 'new_v', 'new_v_k_norm_w': 'new_v', 'new_v_w_br_conv': 'new_v', 'new_v_w_br_attn': 'new_v', 'new_v_w_out': 'new_v'}


def _forward(args):
    return _fwd_reference(*[args[k] for k in FWD_PARAMS])


def _output_shape():
    def fwd():
        inp = _fwd_setup_inputs(0)
        return _fwd_reference(*[inp[k] for k in FWD_PARAMS])
    out = _jax.eval_shape(fwd)
    return out.shape, out.dtype

N_MICROBATCH = 1
ADAM_LR = 0.001
ADAM_B1 = 0.9
ADAM_B2 = 0.999
ADAM_EPS = 1e-08
ADAM_WD = 0.01
ADAM_STEP = 10
PER_EXAMPLE_BATCH_AXIS = {'x': 0, 'c': 0, 'loss_target': 0}
SHARED_INPUTS = []
_WEIGHT_DTYPES = {'w_ada': _jnp.float32, 'b_ada': _jnp.float32, 'norm_w': _jnp.float32, 'w_in': _jnp.float32, 'conv_w': _jnp.float32, 'q_norm_w': _jnp.float32, 'k_norm_w': _jnp.float32, 'w_br_conv': _jnp.float32, 'w_br_attn': _jnp.float32, 'w_out': _jnp.float32}
MOMENT_SCALE = {'w_ada': 1.595101e+01, 'b_ada': 4.696185e+01, 'norm_w': 1.014671e+02, 'w_in': 2.576338e+00, 'conv_w': 1.702193e+01, 'q_norm_w': 1.537820e+00, 'k_norm_w': 1.528320e+00, 'w_br_conv': 1.293029e+00, 'w_br_attn': 1.284318e-01, 'w_out': 1.199815e+00}


def _to_microbatches(a, axis):
    t = _jnp.moveaxis(a, axis, 0)
    t = t.reshape((N_MICROBATCH, t.shape[0] // N_MICROBATCH) + t.shape[1:])
    return _jnp.moveaxis(t, 1, axis + 1)


def setup_inputs(seed: int = 0) -> dict:
    inp = _fwd_setup_inputs(seed)
    key = _jax.random.fold_in(_jax.random.key(seed), 7919)
    shape, _ = _output_shape()
    out = dict(inp)
    out["loss_target"] = _jax.random.normal(_jax.random.fold_in(key, 0), shape, _jnp.float32)
    for i, name in enumerate(TWIN_WEIGHTS):
        w = inp[name].astype(_jnp.float32)
        if MOMENT_SCALE is None:
            s = _jnp.sqrt(_jnp.mean(_jnp.square(w)) + 1e-30)
        else:
            s = MOMENT_SCALE[name]
        km, kv = _jax.random.split(_jax.random.fold_in(key, i + 1))
        out[name] = w
        out["m_" + name] = s * _jax.random.normal(km, w.shape, _jnp.float32)
        out["v_" + name] = (s * s) * _jax.random.uniform(kv, w.shape, _jnp.float32, 0.5, 1.5)
    if N_MICROBATCH > 1:
        for name, axis in PER_EXAMPLE_BATCH_AXIS.items():
            out[name] = _to_microbatches(out[name], axis)
    return {'x': out['x'], 'c': out['c'], 'w_ada': out['w_ada'], 'b_ada': out['b_ada'], 'norm_w': out['norm_w'], 'w_in': out['w_in'], 'conv_w': out['conv_w'], 'q_norm_w': out['q_norm_w'], 'k_norm_w': out['k_norm_w'], 'w_br_conv': out['w_br_conv'], 'w_br_attn': out['w_br_attn'], 'w_out': out['w_out'], 'loss_target': out['loss_target'], 'm_w_ada': out['m_w_ada'], 'm_b_ada': out['m_b_ada'], 'm_norm_w': out['m_norm_w'], 'm_w_in': out['m_w_in'], 'm_conv_w': out['m_conv_w'], 'm_q_norm_w': out['m_q_norm_w'], 'm_k_norm_w': out['m_k_norm_w'], 'm_w_br_conv': out['m_w_br_conv'], 'm_w_br_attn': out['m_w_br_attn'], 'm_w_out': out['m_w_out'], 'v_w_ada': out['v_w_ada'], 'v_b_ada': out['v_b_ada'], 'v_norm_w': out['v_norm_w'], 'v_w_in': out['v_w_in'], 'v_conv_w': out['v_conv_w'], 'v_q_norm_w': out['v_q_norm_w'], 'v_k_norm_w': out['v_k_norm_w'], 'v_w_br_conv': out['v_w_br_conv'], 'v_w_br_attn': out['v_w_br_attn'], 'v_w_out': out['v_w_out']}


def _loss(weights, diff, rest, loss_target):
    with _jax.named_scope("forward"):
        args = {**rest, TWIN_DIFF_INPUT: diff, **{k: w.astype(_WEIGHT_DTYPES[k]) for k, w in weights.items()}}
        y = _forward(args)
    with _jax.named_scope("loss_head"):
        err = _jnp.square(y.astype(_jnp.float32) - loss_target)
        return 0.5 * _jnp.sum(_jnp.mean(err, axis=-1)) if err.ndim else 0.5 * err


def _adamw(w, g, m, v):
    m = ADAM_B1 * m + (1.0 - ADAM_B1) * g
    v = ADAM_B2 * v + (1.0 - ADAM_B2) * _jnp.square(g)
    m_hat = m / (1.0 - ADAM_B1 ** ADAM_STEP)
    v_hat = v / (1.0 - ADAM_B2 ** ADAM_STEP)
    delta = -ADAM_LR * (m_hat / (_jnp.sqrt(v_hat) + ADAM_EPS) + ADAM_WD * w)
    return delta, m, v


def reference(x, c, w_ada, b_ada, norm_w, w_in, conv_w, q_norm_w, k_norm_w, w_br_conv, w_br_attn, w_out, loss_target, m_w_ada, m_b_ada, m_norm_w, m_w_in, m_conv_w, m_q_norm_w, m_k_norm_w, m_w_br_conv, m_w_br_attn, m_w_out, v_w_ada, v_b_ada, v_norm_w, v_w_in, v_conv_w, v_q_norm_w, v_k_norm_w, v_w_br_conv, v_w_br_attn, v_w_out):
    given = dict(x=x, c=c, w_ada=w_ada, b_ada=b_ada, norm_w=norm_w, w_in=w_in, conv_w=conv_w, q_norm_w=q_norm_w, k_norm_w=k_norm_w, w_br_conv=w_br_conv, w_br_attn=w_br_attn, w_out=w_out, loss_target=loss_target, m_w_ada=m_w_ada, m_b_ada=m_b_ada, m_norm_w=m_norm_w, m_w_in=m_w_in, m_conv_w=m_conv_w, m_q_norm_w=m_q_norm_w, m_k_norm_w=m_k_norm_w, m_w_br_conv=m_w_br_conv, m_w_br_attn=m_w_br_attn, m_w_out=m_w_out, v_w_ada=v_w_ada, v_b_ada=v_b_ada, v_norm_w=v_norm_w, v_w_in=v_w_in, v_conv_w=v_conv_w, v_q_norm_w=v_q_norm_w, v_k_norm_w=v_k_norm_w, v_w_br_conv=v_w_br_conv, v_w_br_attn=v_w_br_attn, v_w_out=v_w_out)
    weights = {n: given[n] for n in TWIN_WEIGHTS}
    shared = {n: given[n] for n in SHARED_INPUTS}
    per_example = {n: given[n] for n in ['x', 'c']}
    grad_fn = _jax.value_and_grad(_loss, argnums=(0, 1))

    def one_microbatch(ex, loss_target):
        ex = dict(ex)
        diff = ex.pop(TWIN_DIFF_INPUT)
        return grad_fn(weights, diff, {**shared, **ex}, loss_target)

    if N_MICROBATCH == 1:
        loss, (grad_w, grad_x) = one_microbatch(per_example, given["loss_target"])
    else:
        def body(carry, xs):
            loss_sum, grad_sum = carry
            l_k, (gw_k, gx_k) = one_microbatch(xs[0], xs[1])
            with _jax.named_scope("update"):
                return (loss_sum + l_k, _jax.tree.map(_jnp.add, grad_sum, gw_k)), gx_k

        init = (_jnp.zeros((), _jnp.float32), _jax.tree.map(_jnp.zeros_like, weights))
        (loss, grad_w), grad_x = _jax.lax.scan(body, init, (per_example, given["loss_target"]))
    with _jax.named_scope("update"):
        delta_w, new_m, new_v = {}, {}, {}
        for n in TWIN_WEIGHTS:
            delta_w[n], new_m[n], new_v[n] = _adamw(weights[n], grad_w[n], given["m_" + n], given["v_" + n])
    return (loss, grad_x, *[grad_w[n] for n in TWIN_WEIGHTS], *[delta_w[n] for n in TWIN_WEIGHTS],
            *[new_m[n] for n in TWIN_WEIGHTS], *[new_v[n] for n in TWIN_WEIGHTS])
```

```python
import jax
import jax.numpy as jnp
from jax import lax
from jax.experimental import pallas as pl
from jax.experimental.pallas import tpu as pltpu

F32, BF16 = jnp.float32, jnp.bfloat16
D = 1024
NIN = 11264
NDEV = 8
SHARD = NIN // NDEV
HD = 64
QB = 128
CB = 512
CB_Q, CB_K, CB_V, CB_ZB = 8, 11, 14, 17
DILATIONS = (1, 4, 16)
EPS = 1e-6
NEG = -1e30
HALO = 16
MESH = pl.DeviceIdType.MESH

ADAM_LR, ADAM_B1, ADAM_B2, ADAM_EPS, ADAM_WD, ADAM_STEP = 0.001, 0.9, 0.999, 1e-08, 0.01, 10

NT = (((1,), (1,)), ((), ()))
TN = (((0,), (0,)), ((), ()))


def _cp(sem, vmem_mb=48):
    return pltpu.CompilerParams(dimension_semantics=sem, vmem_limit_bytes=vmem_mb << 20)


def _silu(z):
    return z * jax.nn.sigmoid(z)


def _coords():
    return lax.axis_index("x"), lax.axis_index("y"), lax.axis_index("c")


def all_gather(arrs, name):
    n = len(arrs)

    def body(*refs):
        ins, outs = refs[:n], refs[n:2 * n]
        send_sems, recv_sems, local_sems = refs[2 * n:]
        x, y, c = _coords()
        me, sibling = (x, y, c), (x, y, 1 - c)
        chips = [(1 - x, y), (x, 1 - y), (1 - x, 1 - y)]

        def slot(a, dev):
            return outs[a].at[4 * dev[0] + 2 * dev[1] + dev[2]]

        def copy(a, k, block, to, src=None):
            return pltpu.make_async_remote_copy(
                src_ref=slot(a, block) if src is None else src, dst_ref=slot(a, block),
                send_sem=send_sems.at[a, k], recv_sem=recv_sems.at[a, k],
                device_id=to, device_id_type=MESH)

        mine = [pltpu.make_async_copy(ins[a], slot(a, me), local_sems.at[a]) for a in range(n)]
        for cp in mine:
            cp.start()
        first = []
        for a in range(n):
            first.append(copy(a, 0, me, sibling, src=ins[a]))
            first += [copy(a, 1 + j, me, (*chip, c), src=ins[a]) for j, chip in enumerate(chips)]
        for cp in first:
            cp.start()
        passed = []
        for j, chip in enumerate(chips):
            for a in range(n):
                copy(a, 1 + j, (*chip, c), me).wait_recv()
                fwd = copy(a, 4 + j, (*chip, c), sibling)
                fwd.start()
                passed.append(fwd)
        for a in range(n):
            copy(a, 0, sibling, me).wait_recv()
            for j, chip in enumerate(chips):
                copy(a, 4 + j, (*chip, 1 - c), me).wait_recv()
        for cp in first + passed:
            cp.wait_send()
        for cp in mine:
            cp.wait()

    any_spec = pl.BlockSpec(memory_space=pl.ANY)
    return pl.pallas_call(
        body, name=name,
        out_shape=[jax.ShapeDtypeStruct((NDEV,) + a.shape, a.dtype) for a in arrs],
        in_specs=[any_spec] * n, out_specs=[any_spec] * n,
        scratch_shapes=[pltpu.SemaphoreType.DMA((n, 7)), pltpu.SemaphoreType.DMA((n, 7)),
                        pltpu.SemaphoreType.DMA((n,))],
    )(*arrs)


def all_to_all(arrs, name):
    n = len(arrs)
    flips = [(fx, fy, fc) for fx in (0, 1) for fy in (0, 1) for fc in (0, 1)][1:]

    def body(*refs):
        ins, outs = refs[:n], refs[n:2 * n]
        send_sems, recv_sems, local_sems = refs[2 * n:]
        x, y, c = _coords()
        me = 4 * x + 2 * y + c

        def peer(f):
            return (1 - x if f[0] else x, 1 - y if f[1] else y, 1 - c if f[2] else c)

        def copies(a, k):
            p = peer(flips[k])
            pid = 4 * p[0] + 2 * p[1] + p[2]
            send = pltpu.make_async_remote_copy(
                src_ref=ins[a].at[pid], dst_ref=outs[a].at[me],
                send_sem=send_sems.at[a, k], recv_sem=recv_sems.at[a, k], device_id=p, device_id_type=MESH)
            recv = pltpu.make_async_remote_copy(
                src_ref=ins[a].at[pid], dst_ref=outs[a].at[pid],
                send_sem=send_sems.at[a, k], recv_sem=recv_sems.at[a, k], device_id=p, device_id_type=MESH)
            return send, recv

        mine = [pltpu.make_async_copy(ins[a].at[me], outs[a].at[me], local_sems.at[a]) for a in range(n)]
        for cp in mine:
            cp.start()
        pairs = [copies(a, k) for k in range(7) for a in range(n)]
        for send, _ in pairs:
            send.start()
        for _, recv in pairs:
            recv.wait_recv()
        for send, _ in pairs:
            send.wait_send()
        for cp in mine:
            cp.wait()

    any_spec = pl.BlockSpec(memory_space=pl.ANY)
    return pl.pallas_call(
        body, name=name,
        out_shape=[jax.ShapeDtypeStruct(a.shape, a.dtype) for a in arrs],
        in_specs=[any_spec] * n, out_specs=[any_spec] * n,
        scratch_shapes=[pltpu.SemaphoreType.DMA((n, 7)), pltpu.SemaphoreType.DMA((n, 7)),
                        pltpu.SemaphoreType.DMA((n,))],
    )(*arrs)


def ada_fwd(c_all, w_ada, b_cols):
    def body(c_ref, w_ref, b_ref, o_ref):
        a = _silu(c_ref[...]).astype(BF16)
        o_ref[...] = jnp.dot(a, w_ref[...].astype(BF16), preferred_element_type=F32) + b_ref[...]

    return pl.pallas_call(body, name="ada_fwd",
                          out_shape=jax.ShapeDtypeStruct((NDEV, w_ada.shape[1]), F32))(c_all, w_ada, b_cols)


def ada_bwd(c_all_t, dmod_cols):
    def body(c_ref, d_ref, o_ref):
        at = _silu(c_ref[...])
        acc = at[:, 0:1] * d_ref[0:1, :]
        for b in range(1, NDEV):
            acc = acc + at[:, b:b + 1] * d_ref[b:b + 1, :]
        o_ref[...] = acc

    return pl.pallas_call(body, name="ada_bwd",
                          out_shape=jax.ShapeDtypeStruct((D, dmod_cols.shape[1]), F32))(c_all_t, dmod_cols)


def sum_parts(parts):
    def body(p_ref, o_ref):
        acc = p_ref[0]
        for b in range(1, NDEV):
            acc = acc + p_ref[b]
        o_ref[...] = acc

    return pl.pallas_call(body, name="sum_parts",
                          out_shape=jax.ShapeDtypeStruct(parts.shape[1:], F32))(parts)


def adamw(parts, w, m, v, name, rows):
    n, r, ccols = parts.shape

    def body(p_ref, w_ref, m_ref, v_ref, g_ref, d_ref, nm_ref, nv_ref):
        g = p_ref[0].astype(F32)
        for b in range(1, n):
            g = g + p_ref[b].astype(F32)
        nm = ADAM_B1 * m_ref[...] + (1.0 - ADAM_B1) * g
        nv = ADAM_B2 * v_ref[...] + (1.0 - ADAM_B2) * (g * g)
        g_ref[...] = g
        nm_ref[...] = nm
        nv_ref[...] = nv
        m_hat = nm / (1.0 - ADAM_B1 ** ADAM_STEP)
        v_hat = nv / (1.0 - ADAM_B2 ** ADAM_STEP)
        d_ref[...] = -ADAM_LR * (m_hat / (jnp.sqrt(v_hat) + ADAM_EPS) + ADAM_WD * w_ref[...])

    blk = pl.BlockSpec((rows, ccols), lambda i: (i, 0))
    out = jax.ShapeDtypeStruct((r, ccols), F32)
    return pl.pallas_call(
        body, name=name, grid=(r // rows,),
        in_specs=[pl.BlockSpec((n, rows, ccols), lambda i: (0, i, 0)), blk, blk, blk],
        out_specs=[blk] * 4, out_shape=[out] * 4, compiler_params=_cp(("parallel",)))(parts, w, m, v)


def norm_fwd(x, nw, scale, shift, tm):
    s = x.shape[0]

    def body(x_ref, nw_ref, sc_ref, sh_ref, h_ref, ht_ref):
        xf = x_ref[...]
        r = lax.rsqrt(jnp.mean(xf * xf, axis=-1, keepdims=True) + EPS)
        h = (xf * r * nw_ref[...]) * (1.0 + sc_ref[...]) + sh_ref[...]
        h_ref[...] = h.astype(BF16)
        ht_ref[...] = h.T.astype(BF16)

    vec = pl.BlockSpec((1, D), lambda i: (0, 0))
    return pl.pallas_call(
        body, name="norm_fwd", grid=(s // tm,),
        in_specs=[pl.BlockSpec((tm, D), lambda i: (i, 0)), vec, vec, vec],
        out_specs=[pl.BlockSpec((tm, D), lambda i: (i, 0)), pl.BlockSpec((D, tm), lambda i: (0, i))],
        out_shape=[jax.ShapeDtypeStruct((s, D), BF16), jax.ShapeDtypeStruct((D, s), BF16)],
        compiler_params=_cp(("parallel",)))(x, nw, scale, shift)


def norm_bwd(dh, x, dy, nw, scale, tm):
    s = x.shape[0]

    def body(dh_ref, x_ref, dy_ref, nw_ref, sc_ref, gx_ref, st_ref):
        xf, g = x_ref[...], dh_ref[...]
        r = lax.rsqrt(jnp.mean(xf * xf, axis=-1, keepdims=True) + EPS)
        xh = xf * r
        dn = g * (1.0 + sc_ref[...])
        dxh = dn * nw_ref[...]
        gx_ref[...] = dy_ref[...] + r * (dxh - xh * jnp.mean(dxh * xh, axis=-1, keepdims=True))

        @pl.when(pl.program_id(0) == 0)
        def _():
            st_ref[...] = jnp.zeros_like(st_ref)

        st_ref[0:1, :] += jnp.sum(g, axis=0, keepdims=True)
        st_ref[1:2, :] += jnp.sum(g * xh * nw_ref[...], axis=0, keepdims=True)
        st_ref[2:3, :] += jnp.sum(dn * xh, axis=0, keepdims=True)

    vec = pl.BlockSpec((1, D), lambda i: (0, 0))
    row = pl.BlockSpec((tm, D), lambda i: (i, 0))
    return pl.pallas_call(
        body, name="norm_bwd", grid=(s // tm,),
        in_specs=[row, row, row, vec, vec],
        out_specs=[row, pl.BlockSpec((8, D), lambda i: (0, 0))],
        out_shape=[jax.ShapeDtypeStruct((s, D), F32), jax.ShapeDtypeStruct((8, D), F32)],
        compiler_params=_cp(("arbitrary",)))(dh, x, dy, nw, scale)


def proj_fwd(h, wg, tm):
    s = h.shape[0]

    def body(a_ref, w_ref, o_ref):
        o_ref[...] = jnp.dot(a_ref[...], w_ref[...], preferred_element_type=F32).astype(BF16)

    return pl.pallas_call(
        body, name="proj_fwd", grid=(NDEV, s // tm),
        in_specs=[pl.BlockSpec((tm, D), lambda j, i: (i, 0)),
                  pl.BlockSpec((None, D, SHARD), lambda j, i: (j, 0, 0))],
        out_specs=pl.BlockSpec((tm, SHARD), lambda j, i: (i, j)),
        out_shape=jax.ShapeDtypeStruct((s, NIN), BF16),
        compiler_params=_cp(("parallel", "parallel")))(h, wg)


def proj_bwd_x(dproj, wg, tm):
    s = dproj.shape[0]

    def body(a_ref, w_ref, o_ref):
        p = lax.dot_general(a_ref[...], w_ref[...], NT, preferred_element_type=F32)
        k = pl.program_id(1)

        @pl.when(k == 0)
        def _():
            o_ref[...] = p

        @pl.when(k > 0)
        def _():
            o_ref[...] += p

    return pl.pallas_call(
        body, name="proj_bwd_x", grid=(s // tm, NDEV),
        in_specs=[pl.BlockSpec((tm, SHARD), lambda i, k: (i, k)),
                  pl.BlockSpec((None, D, SHARD), lambda i, k: (k, 0, 0))],
        out_specs=pl.BlockSpec((tm, D), lambda i, k: (i, 0)),
        out_shape=jax.ShapeDtypeStruct((s, D), F32),
        compiler_params=_cp(("parallel", "arbitrary")))(dproj, wg)


def proj_bwd_w(ht, dproj, tk):
    s = dproj.shape[0]
    nk = s // tk

    def body(a_ref, b_ref, o_ref, acc_ref):
        k = pl.program_id(1)
        p = jnp.dot(a_ref[...], b_ref[...], preferred_element_type=F32)

        @pl.when(k == 0)
        def _():
            acc_ref[...] = p

        @pl.when(k > 0)
        def _():
            acc_ref[...] += p

        @pl.when(k == nk - 1)
        def _():
            o_ref[...] = acc_ref[...].astype(BF16)

    return pl.pallas_call(
        body, name="proj_bwd_w", grid=(NDEV, nk),
        in_specs=[pl.BlockSpec((D, tk), lambda j, k: (0, k)),
                  pl.BlockSpec((tk, SHARD), lambda j, k: (k, j))],
        out_specs=pl.BlockSpec((None, D, SHARD), lambda j, k: (j, 0, 0)),
        out_shape=jax.ShapeDtypeStruct((NDEV, D, SHARD), BF16),
        scratch_shapes=[pltpu.VMEM((D, SHARD), F32)],
        compiler_params=_cp(("parallel", "arbitrary")))(ht, dproj)


def matmul_tn(a, b, name, tk):
    s, m = a.shape
    n = b.shape[1]
    nk = s // tk

    def body(a_ref, b_ref, o_ref, acc_ref):
        k = pl.program_id(0)
        p = lax.dot_general(a_ref[...], b_ref[...], TN, preferred_element_type=F32)

        @pl.when(k == 0)
        def _():
            acc_ref[...] = p

        @pl.when(k > 0)
        def _():
            acc_ref[...] += p

        @pl.when(k == nk - 1)
        def _():
            o_ref[...] = acc_ref[...].astype(BF16)

    return pl.pallas_call(
        body, name=name, grid=(nk,),
        in_specs=[pl.BlockSpec((tk, m), lambda k: (k, 0)), pl.BlockSpec((tk, n), lambda k: (k, 0))],
        out_specs=pl.BlockSpec((m, n), lambda k: (0, 0)),
        out_shape=jax.ShapeDtypeStruct((m, n), BF16),
        scratch_shapes=[pltpu.VMEM((m, n), F32)],
        compiler_params=_cp(("arbitrary",)))(a, b)


LANES = 128


def to_residue_major(srcs, dil, tm, name):
    s = srcs[0][0].shape[0]
    n = len(srcs)
    rows = tm // dil

    def body(*refs):
        ins, outs, bufs = refs[:n], refs[n:2 * n], refs[2 * n:]
        for a in range(n):
            for k in range(CB // LANES):
                lanes = slice(k * LANES, (k + 1) * LANES)
                bufs[a][k] = ins[a][:, lanes].astype(F32)
                for r in range(dil):
                    outs[a][r, :, lanes] = bufs[a][k, pl.ds(r, rows, stride=dil), :].astype(outs[a].dtype)

    outs = pl.pallas_call(
        body, name=name, grid=(s // tm,),
        in_specs=[pl.BlockSpec((tm, CB), lambda i, cb=cb: (i, cb)) for _, cb in srcs],
        out_specs=[pl.BlockSpec((dil, rows, CB), lambda i: (0, i, 0))] * n,
        out_shape=[jax.ShapeDtypeStruct((dil, s // dil, CB), a.dtype) for a, _ in srcs],
        scratch_shapes=[pltpu.VMEM((CB // LANES, tm, LANES), F32)] * n,
        compiler_params=_cp(("parallel",)))(*[a for a, _ in srcs])
    return [o.reshape(s, CB) for o in outs]


def _from_residue_major(ref, buf, dil):
    if dil == 1:
        return ref[0].astype(F32)
    rows = ref.shape[1]
    for k in range(CB // LANES):
        for r in range(dil):
            buf[k, pl.ds(r, rows, stride=dil), :] = ref[r, :, k * LANES:(k + 1) * LANES].astype(F32)
    return jnp.concatenate([buf[k] for k in range(CB // LANES)], axis=1)


def qkv_grads_to_dproj(dproj, grads, tm):
    s = dproj.shape[0]
    flat = [(t.reshape(d, s // d, CB), d, 3 * kind + g)
            for g, d in enumerate(DILATIONS) for kind, t in enumerate(grads[g])]

    def body(*refs):
        ins, o_ref, buf = refs[1:1 + len(flat)], refs[1 + len(flat)], refs[2 + len(flat)]
        j = pl.program_id(1)
        for ref, (_, d, jj) in zip(ins, flat):
            @pl.when(j == jj)
            def _(ref=ref, d=d):
                o_ref[...] = _from_residue_major(ref, buf, d).astype(BF16)

    return pl.pallas_call(
        body, name="qkv_grads_to_dproj", grid=(s // tm, 9),
        in_specs=[pl.BlockSpec(memory_space=pl.ANY)]
                 + [pl.BlockSpec((d, tm // d, CB), lambda i, j: (0, i, 0)) for _, d, _ in flat],
        out_specs=pl.BlockSpec((tm, CB), lambda i, j: (i, CB_Q + j)),
        out_shape=jax.ShapeDtypeStruct((s, NIN), BF16),
        input_output_aliases={0: 0},
        scratch_shapes=[pltpu.VMEM((CB // LANES, tm, LANES), F32)],
        compiler_params=_cp(("arbitrary", "arbitrary")))(dproj, *[t for t, _, _ in flat])


def _lane_lo():
    return lax.broadcasted_iota(jnp.int32, (1, 2 * HD), 1) < HD


def _half_mean(t, lo):
    s_lo = jnp.sum(jnp.where(lo, t, 0.0), axis=-1, keepdims=True)
    s_all = jnp.sum(t, axis=-1, keepdims=True)
    return jnp.where(lo, s_lo, s_all - s_lo) * (1.0 / HD)


def _qk_norm(t, w, lo):
    r = lax.rsqrt(_half_mean(t * t, lo) + EPS)
    xh = t * r
    return xh, r, xh * w


def _norm_bwd_pair(dn, xh, r, w, lo):
    dxh = dn * w
    return r * (dxh - xh * _half_mean(dxh * xh, lo))


def _masks(other_ok):
    qi = lax.broadcasted_iota(jnp.int32, (QB, QB), 0)
    kj = lax.broadcasted_iota(jnp.int32, (QB, QB), 1)
    return (kj >= qi) & other_ok, kj <= qi


def attn_fwd(q_src, k_src, v_src, qw2, kw2, g, dil):
    s = q_src[0].shape[0]
    nb = s // dil // QB

    def body(q_ref, kp_ref, kc_ref, vp_ref, vc_ref, qw_ref, kw_ref, o_ref, l_ref):
        b = pl.program_id(1)
        lo = _lane_lo()
        m_prev, m_cur = _masks(b > 0)
        mask = jnp.concatenate([m_prev, m_cur], axis=1)
        for i in range(4):
            sl = slice(2 * HD * i, 2 * HD * (i + 1))
            _, _, qs = _qk_norm(q_ref[:, sl].astype(F32), qw_ref[...], lo)
            qs = qs * (HD ** -0.5)
            kk = jnp.concatenate([kp_ref[:, sl], kc_ref[:, sl]], axis=0).astype(F32)
            _, _, ks = _qk_norm(kk, kw_ref[...], lo)
            ks = ks.astype(BF16)
            vv = jnp.concatenate([vp_ref[:, sl], vc_ref[:, sl]], axis=0)
            outs, lses = [], []
            for hmask in (lo, ~lo):
                qh = jnp.where(hmask, qs, 0.0).astype(BF16)
                sc = lax.dot_general(qh, ks, NT, preferred_element_type=F32)
                sc = jnp.where(mask, sc, NEG)
                mx = jnp.max(sc, axis=-1, keepdims=True)
                p = jnp.exp(sc - mx)
                den = jnp.sum(p, axis=-1, keepdims=True)
                o = jnp.dot(p.astype(BF16), vv, preferred_element_type=F32)
                outs.append(o / den)
                lses.append(mx + jnp.log(den))
            o_ref[:, sl] = jnp.where(lo, outs[0], outs[1])
            l_ref[:, sl] = jnp.where(lo, lses[0], lses[1])

    cur = lambda cb: pl.BlockSpec((QB, CB), lambda r, b: (r * nb + b, cb))
    prev = lambda cb: pl.BlockSpec((QB, CB), lambda r, b: (r * nb + jnp.maximum(b - 1, 0), cb))
    vec = pl.BlockSpec((1, 2 * HD), lambda r, b: (0, 0))
    out = jax.ShapeDtypeStruct((s, CB), F32)
    return pl.pallas_call(
        body, name=f"attn_fwd_g{g}", grid=(dil, nb),
        in_specs=[cur(q_src[1]), prev(k_src[1]), cur(k_src[1]), prev(v_src[1]), cur(v_src[1]), vec, vec],
        out_specs=[cur(0)] * 2, out_shape=[out, out],
        compiler_params=_cp(("parallel", "parallel")))(
            q_src[0], k_src[0], k_src[0], v_src[0], v_src[0], qw2, kw2)


def attn_bwd_q(q_src, k_src, v_src, da, lse, dm, qw2, kw2, g, dil):
    s = q_src[0].shape[0]
    nb = s // dil // QB

    def body(q_ref, kp_ref, kc_ref, vp_ref, vc_ref, da_ref, l_ref, dm_ref, qw_ref, kw_ref, dq_ref, gw_ref):
        r, b = pl.program_id(0), pl.program_id(1)
        lo = _lane_lo()
        m_prev, m_cur = _masks(b > 0)
        mask = jnp.concatenate([m_prev, m_cur], axis=1)

        @pl.when((r == 0) & (b == 0))
        def _():
            gw_ref[...] = jnp.zeros_like(gw_ref)

        gw = jnp.zeros((1, 2 * HD), F32)
        for i in range(4):
            sl = slice(2 * HD * i, 2 * HD * (i + 1))
            xh, rs, qs = _qk_norm(q_ref[:, sl].astype(F32), qw_ref[...], lo)
            qs = qs * (HD ** -0.5)
            kk = jnp.concatenate([kp_ref[:, sl], kc_ref[:, sl]], axis=0).astype(F32)
            _, _, ks = _qk_norm(kk, kw_ref[...], lo)
            ks = ks.astype(BF16)
            vv = jnp.concatenate([vp_ref[:, sl], vc_ref[:, sl]], axis=0)
            da2, l2, d2 = da_ref[:, sl], l_ref[:, sl], dm_ref[:, sl]
            dqs = jnp.zeros((QB, 2 * HD), F32)
            for h, hmask in enumerate((lo, ~lo)):
                qh = jnp.where(hmask, qs, 0.0).astype(BF16)
                sc = lax.dot_general(qh, ks, NT, preferred_element_type=F32)
                sc = jnp.where(mask, sc, NEG)
                p = jnp.exp(sc - l2[:, HD * h:HD * h + 1])
                dah = jnp.where(hmask, da2, jnp.zeros_like(da2))
                dp = lax.dot_general(dah, vv, NT, preferred_element_type=F32)
                ds = p * (dp - d2[:, HD * h:HD * h + 1])
                dq_h = jnp.dot(ds.astype(BF16), ks, preferred_element_type=F32)
                dqs = dqs + jnp.where(hmask, dq_h, 0.0)
            dqs = dqs * (HD ** -0.5)
            gw = gw + jnp.sum(dqs * xh, axis=0, keepdims=True)
            dq_ref[:, sl] = _norm_bwd_pair(dqs, xh, rs, qw_ref[...], lo).astype(BF16)
        gw_ref[0:1, :] += gw

    cur = lambda cb: pl.BlockSpec((QB, CB), lambda r, b: (r * nb + b, cb))
    prev = lambda cb: pl.BlockSpec((QB, CB), lambda r, b: (r * nb + jnp.maximum(b - 1, 0), cb))
    vec = pl.BlockSpec((1, 2 * HD), lambda r, b: (0, 0))
    return pl.pallas_call(
        body, name=f"attn_bwd_q_g{g}", grid=(dil, nb),
        in_specs=[cur(q_src[1]), prev(k_src[1]), cur(k_src[1]), prev(v_src[1]), cur(v_src[1]),
                  cur(0), cur(0), cur(0), vec, vec],
        out_specs=[cur(0), pl.BlockSpec((8, 2 * HD), lambda r, b: (0, 0))],
        out_shape=[jax.ShapeDtypeStruct((s, CB), BF16), jax.ShapeDtypeStruct((8, 2 * HD), F32)],
        compiler_params=_cp(("arbitrary", "arbitrary")))(
            q_src[0], k_src[0], k_src[0], v_src[0], v_src[0], da, lse, dm, qw2, kw2)


def attn_bwd_kv(q_src, k_src, v_src, da, lse, dm, qw2, kw2, g, dil):
    s = q_src[0].shape[0]
    nb = s // dil // QB

    def body(k_ref, v_ref, qc_ref, qn_ref, dac_ref, dan_ref, lc_ref, ln_ref, dc_ref, dn_ref,
             qw_ref, kw_ref, dk_ref, dv_ref, gw_ref):
        r, cblk = pl.program_id(0), pl.program_id(1)

        @pl.when((r == 0) & (cblk == 0))
        def _():
            gw_ref[...] = jnp.zeros_like(gw_ref)

        lo = _lane_lo()
        m_next, m_cur = _masks(cblk < nb - 1)
        mask = jnp.concatenate([m_cur, m_next], axis=0)
        gw = jnp.zeros((1, 2 * HD), F32)
        for i in range(4):
            sl = slice(2 * HD * i, 2 * HD * (i + 1))
            qq = jnp.concatenate([qc_ref[:, sl], qn_ref[:, sl]], axis=0).astype(F32)
            _, _, qs = _qk_norm(qq, qw_ref[...], lo)
            qs = qs * (HD ** -0.5)
            xh, rs, ks = _qk_norm(k_ref[:, sl].astype(F32), kw_ref[...], lo)
            ks = ks.astype(BF16)
            vv = v_ref[:, sl]
            da2 = jnp.concatenate([dac_ref[:, sl], dan_ref[:, sl]], axis=0)
            l2 = jnp.concatenate([lc_ref[:, sl], ln_ref[:, sl]], axis=0)
            d2 = jnp.concatenate([dc_ref[:, sl], dn_ref[:, sl]], axis=0)
            dks = jnp.zeros((QB, 2 * HD), F32)
            dvv = jnp.zeros((QB, 2 * HD), F32)
            for h, hmask in enumerate((lo, ~lo)):
                qh = jnp.where(hmask, qs, 0.0).astype(BF16)
                sc = lax.dot_general(qh, ks, NT, preferred_element_type=F32)
                sc = jnp.where(mask, sc, NEG)
                p = jnp.exp(sc - l2[:, HD * h:HD * h + 1])
                dah = jnp.where(hmask, da2, jnp.zeros_like(da2))
                dp = lax.dot_general(dah, vv, NT, preferred_element_type=F32)
                ds = p * (dp - d2[:, HD * h:HD * h + 1])
                dvv = dvv + lax.dot_general(p.astype(BF16), dah, TN, preferred_element_type=F32)
                dks = dks + lax.dot_general(ds.astype(BF16), qh, TN, preferred_element_type=F32)
            gw = gw + jnp.sum(dks * xh, axis=0, keepdims=True)
            dk_ref[:, sl] = _norm_bwd_pair(dks, xh, rs, kw_ref[...], lo).astype(BF16)
            dv_ref[:, sl] = dvv.astype(BF16)
        gw_ref[0:1, :] += gw

    cur = lambda cb: pl.BlockSpec((QB, CB), lambda r, c: (r * nb + c, cb))
    nxt = lambda cb: pl.BlockSpec((QB, CB), lambda r, c: (r * nb + jnp.minimum(c + 1, nb - 1), cb))
    vec = pl.BlockSpec((1, 2 * HD), lambda r, c: (0, 0))
    out = jax.ShapeDtypeStruct((s, CB), BF16)
    return pl.pallas_call(
        body, name=f"attn_bwd_kv_g{g}", grid=(dil, nb),
        in_specs=[cur(k_src[1]), cur(v_src[1]), cur(q_src[1]), nxt(q_src[1]),
                  cur(0), nxt(0), cur(0), nxt(0), cur(0), nxt(0), vec, vec],
        out_specs=[cur(0), cur(0), pl.BlockSpec((8, 2 * HD), lambda r, c: (0, 0))],
        out_shape=[out, out, jax.ShapeDtypeStruct((8, 2 * HD), F32)],
        compiler_params=_cp(("arbitrary", "arbitrary")))(
            k_src[0], v_src[0], q_src[0], q_src[0], da, da, lse, lse, dm, dm, qw2, kw2)


def _conv_taps(u, u_prev, first):
    tm = u.shape[0]
    row = lax.broadcasted_iota(jnp.int32, (tm, 1), 0)
    up = jnp.where(first, 0.0, u_prev)
    u1 = jnp.where(row == 0, up[HALO - 1:HALO, :], pltpu.roll(u, 1, 0))
    u2 = jnp.where(row == 0, up[HALO - 2:HALO - 1, :],
                   jnp.where(row == 1, up[HALO - 1:HALO, :], pltpu.roll(u, 2, 0)))
    return u1, u2


def mid_fwd(proj, o_g, lse_g, conv_w, tm):
    s = proj.shape[0]
    hb = tm // HALO

    def body(ba_ref, ca_ref, xa_ref, za_ref, cah_ref, xah_ref, zb_ref,
             o0, o1, o2, l0, l1, l2, w_ref, ya_ref, yb_ref, at_ref, ls_ref, buf_o, buf_l):
        first = pl.program_id(0) == 0
        u = ca_ref[...].astype(F32) * xa_ref[...].astype(F32)
        u1, u2 = _conv_taps(u, cah_ref[...].astype(F32) * xah_ref[...].astype(F32), first)
        conv = w_ref[0:1, :] * u2 + w_ref[1:2, :] * u1 + w_ref[2:3, :] * u
        ya_ref[...] = (ba_ref[...].astype(F32) * conv * _silu(za_ref[...].astype(F32))).astype(BF16)
        ls = [_from_residue_major(l, buf_l.at[g], d) for g, (l, d) in enumerate(zip((l0, l1, l2), DILATIONS))]
        mx = jnp.maximum(jnp.maximum(ls[0], ls[1]), ls[2])
        es = [jnp.exp(l - mx) for l in ls]
        den = es[0] + es[1] + es[2]
        num = jnp.zeros_like(den)
        for e, o, d in zip(es, (o0, o1, o2), DILATIONS):
            num = num + e * _from_residue_major(o, buf_o, d)
        attn = num / den
        at_ref[...] = attn
        ls_ref[...] = mx + jnp.log(den)
        yb_ref[...] = (attn * _silu(zb_ref[...].astype(F32))).astype(BF16)

    col = lambda j: pl.BlockSpec((tm, D), lambda i: (i, j))
    halo = lambda j: pl.BlockSpec((HALO, D), lambda i: (jnp.maximum(i * hb - 1, 0), j))
    loc = pl.BlockSpec((tm, CB), lambda i: (i, 0))
    rm = [pl.BlockSpec((d, tm // d, CB), lambda i: (0, i, 0)) for d in DILATIONS]
    rm_view = lambda ts: [t.reshape(d, s // d, CB) for t, d in zip(ts, DILATIONS)]
    return pl.pallas_call(
        body, name="mid_fwd", grid=(s // tm,),
        in_specs=[col(0), col(1), col(2), col(3), halo(1), halo(2),
                  pl.BlockSpec((tm, CB), lambda i: (i, CB_ZB))] + rm + rm
                 + [pl.BlockSpec((3, D), lambda i: (0, 0))],
        out_specs=[pl.BlockSpec((tm, D), lambda i: (i, 0)), loc, loc, loc],
        out_shape=[jax.ShapeDtypeStruct((s, D), BF16), jax.ShapeDtypeStruct((s, CB), BF16),
                   jax.ShapeDtypeStruct((s, CB), F32), jax.ShapeDtypeStruct((s, CB), F32)],
        scratch_shapes=[pltpu.VMEM((CB // LANES, tm, LANES), F32), pltpu.VMEM((3, CB // LANES, tm, LANES), F32)],
        compiler_params=_cp(("parallel",)))(
            proj, proj, proj, proj, proj, proj, proj, *rm_view(o_g), *rm_view(lse_g), conv_w)


def mid_bwd(dproj, proj, dya, conv_w, tm):
    s = proj.shape[0]
    hb = tm // HALO
    nblk = s // tm
    last_h = s // HALO - 1

    def body(_, ba_ref, ca_ref, xa_ref, za_ref, cah_ref, xah_ref, ban_ref, zan_ref, dy_ref, dyn_ref, w_ref,
             o_ref, gw_ref):
        i = pl.program_id(0)
        ba, ca, xa, za = (t[...].astype(F32) for t in (ba_ref, ca_ref, xa_ref, za_ref))
        u = ca * xa
        u1, u2 = _conv_taps(u, cah_ref[...].astype(F32) * xah_ref[...].astype(F32), i == 0)
        w0, w1, w2 = w_ref[0:1, :], w_ref[1:2, :], w_ref[2:3, :]
        conv = w0 * u2 + w1 * u1 + w2 * u
        sg = jax.nn.sigmoid(za)
        sz = za * sg
        dy = dy_ref[...].astype(F32)
        dconv = dy * ba * sz
        dcn = dyn_ref[...].astype(F32) * ban_ref[...].astype(F32) * _silu(zan_ref[...].astype(F32))
        dcn = jnp.where(i == nblk - 1, 0.0, dcn)
        row = lax.broadcasted_iota(jnp.int32, (tm, 1), 0)
        d1 = jnp.where(row == tm - 1, dcn[0:1, :], pltpu.roll(dconv, tm - 1, 0))
        d2 = jnp.where(row == tm - 2, dcn[0:1, :],
                       jnp.where(row == tm - 1, dcn[1:2, :], pltpu.roll(dconv, tm - 2, 0)))
        du = w2 * dconv + w1 * d1 + w0 * d2
        o_ref[:, 0:D] = (dy * conv * sz).astype(BF16)
        o_ref[:, D:2 * D] = (du * xa).astype(BF16)
        o_ref[:, 2 * D:3 * D] = (du * ca).astype(BF16)
        o_ref[:, 3 * D:4 * D] = (dy * ba * conv * (sg * (1.0 + za * (1.0 - sg)))).astype(BF16)

        @pl.when(i == 0)
        def _():
            gw_ref[...] = jnp.zeros_like(gw_ref)

        gw_ref[0:1, :] += jnp.sum(dconv * u2, axis=0, keepdims=True)
        gw_ref[1:2, :] += jnp.sum(dconv * u1, axis=0, keepdims=True)
        gw_ref[2:3, :] += jnp.sum(dconv * u, axis=0, keepdims=True)

    col = lambda j: pl.BlockSpec((tm, D), lambda i: (i, j))
    halo_prev = lambda j: pl.BlockSpec((HALO, D), lambda i: (jnp.maximum(i * hb - 1, 0), j))
    halo_next = lambda j: pl.BlockSpec((HALO, D), lambda i: (jnp.minimum((i + 1) * hb, last_h), j))
    return pl.pallas_call(
        body, name="mid_bwd", grid=(nblk,),
        in_specs=[pl.BlockSpec(memory_space=pl.ANY), col(0), col(1), col(2), col(3),
                  halo_prev(1), halo_prev(2), halo_next(0), halo_next(3),
                  pl.BlockSpec((tm, D), lambda i: (i, 0)), halo_next(0),
                  pl.BlockSpec((3, D), lambda i: (0, 0))],
        out_specs=[pl.BlockSpec((tm, 4 * D), lambda i: (i, 0)), pl.BlockSpec((8, D), lambda i: (0, 0))],
        out_shape=[jax.ShapeDtypeStruct((s, NIN), BF16), jax.ShapeDtypeStruct((8, D), F32)],
        input_output_aliases={0: 0},
        compiler_params=_cp(("arbitrary",)))(dproj, proj, proj, proj, proj, proj, proj, proj, proj, dya, dya, conv_w)


def tail(proj, ya, yb, attn, x, target, gate, pa_w, pb_w, wo_w, tm):
    s = proj.shape[0]
    ni = s // tm
    bd = (lax.broadcasted_iota(jnp.int32, (CB, CB), 0) // HD
          == lax.broadcasted_iota(jnp.int32, (CB, CB), 1) // HD).astype(BF16)

    def body(ya_ref, yb_ref, ga_ref, gb_ref, zb_ref, at_ref, x_ref, t_ref, gate_ref, pa_ref, pb_ref, wo_ref,
             bd_ref, dp_ref, dy_ref, dya_ref, da_ref, dm_ref, mg_ref, do_ref, dpa_ref, dpb_ref, st_ref, pieces):
        i, j = pl.program_id(0), pl.program_id(1)

        @pl.when((i == 0) & (j == 0))
        def _():
            st_ref[...] = jnp.zeros_like(st_ref)

        @pl.when(j == 0)
        def _():
            gate_v = gate_ref[...]
            pa = jnp.dot(ya_ref[...], pa_ref[...], preferred_element_type=F32)
            pb = jnp.dot(yb_ref[...], pb_ref[...], preferred_element_type=F32)
            sa = jax.nn.sigmoid(ga_ref[...].astype(F32))
            sb = jax.nn.sigmoid(gb_ref[...].astype(F32))
            merged = (sa * pa + sb * pb).astype(BF16)
            mg_ref[...] = merged
            out = jnp.dot(merged, wo_ref[...], preferred_element_type=F32)
            err = x_ref[...] + gate_v * out - t_ref[...]
            dy = err * (1.0 / D)
            dy_ref[...] = dy
            st_ref[0:1, :] += jnp.sum(dy * out, axis=0, keepdims=True)
            st_ref[1:2, :] += jnp.sum(err * err, axis=0, keepdims=True)
            dout = (gate_v * dy).astype(BF16)
            do_ref[...] = dout
            dmg = lax.dot_general(dout, wo_ref[...], NT, preferred_element_type=F32)
            dpa = (dmg * sa).astype(BF16)
            dpb = (dmg * sb).astype(BF16)
            dpa_ref[...] = dpa
            dpb_ref[...] = dpb
            dga = (dmg * pa * sa * (1.0 - sa)).astype(BF16)
            dgb = (dmg * pb * sb * (1.0 - sb)).astype(BF16)
            pieces[1] = dga[:, :CB]
            pieces[2] = dga[:, CB:]
            pieces[3] = dgb[:, :CB]
            pieces[4] = dgb[:, CB:]
            dya_ref[...] = lax.dot_general(dpa, pa_ref[...], NT, preferred_element_type=F32).astype(BF16)
            dyb = lax.dot_general(dpb, pb_ref[...], NT, preferred_element_type=F32)
            zb = zb_ref[...].astype(F32)
            sg = jax.nn.sigmoid(zb)
            attn_v = at_ref[...]
            dattn = dyb * (zb * sg)
            da_ref[...] = dattn.astype(BF16)
            pieces[0] = (dyb * attn_v * (sg * (1.0 + zb * (1.0 - sg)))).astype(BF16)
            prod = dattn * attn_v
            hi = prod.astype(BF16)
            lo = (prod - hi.astype(F32)).astype(BF16)
            dm_ref[...] = (jnp.dot(hi, bd_ref[...], preferred_element_type=F32)
                           + jnp.dot(lo, bd_ref[...], preferred_element_type=F32))

        dp_ref[...] = pieces[j]

    row = lambda w: pl.BlockSpec((tm, w), lambda i, j: (i, 0))
    pcol = lambda w, jb: pl.BlockSpec((tm, w), lambda i, j: (i, jb))
    full = lambda a: pl.BlockSpec(a.shape, lambda i, j: (0, 0))
    return pl.pallas_call(
        body, name="tail", grid=(ni, 5),
        in_specs=[row(D), row(CB), pcol(D, 9), pcol(D, 10), pcol(CB, CB_ZB), row(CB), row(D), row(D),
                  pl.BlockSpec((1, D), lambda i, j: (0, 0)), full(pa_w), full(pb_w), full(wo_w), full(bd)],
        out_specs=[pl.BlockSpec((tm, CB), lambda i, j: (i, CB_ZB + j)),
                   row(D), row(D), row(CB), row(CB), row(D), row(D), row(D), row(D),
                   pl.BlockSpec((8, D), lambda i, j: (0, 0))],
        out_shape=[jax.ShapeDtypeStruct((s, NIN), BF16), jax.ShapeDtypeStruct((s, D), F32),
                   jax.ShapeDtypeStruct((s, D), BF16), jax.ShapeDtypeStruct((s, CB), BF16),
                   jax.ShapeDtypeStruct((s, CB), F32)] + [jax.ShapeDtypeStruct((s, D), BF16)] * 4
                  + [jax.ShapeDtypeStruct((8, D), F32)],
        scratch_shapes=[pltpu.VMEM((5, tm, CB), BF16)],
        compiler_params=_cp(("arbitrary", "arbitrary"), 56))(
            ya, yb, proj, proj, proj, attn, x, target, gate, pa_w, pb_w, wo_w, bd)


def _local_step(x, target, shift, scale, gate, norm_w, conv_w, qw, kw, wg, pa_w, pb_w, wo_w):
    qw2, kw2 = jnp.tile(qw, (1, 2)), jnp.tile(kw, (1, 2))
    h, ht = norm_fwd(x, norm_w, scale, shift, 512)
    proj = proj_fwd(h, wg, 1024)
    srcs = []
    for g, d in enumerate(DILATIONS):
        cols = [(proj, CB_Q + g), (proj, CB_K + g), (proj, CB_V + g)]
        srcs.append(cols if d == 1 else [(t, 0) for t in to_residue_major(cols, d, 512, f"qkv_rm_g{g}")])
    o_g, lse_g = zip(*[attn_fwd(*srcs[g], qw2, kw2, g, d) for g, d in enumerate(DILATIONS)])
    ya, yb, attn, lse = mid_fwd(proj, o_g, lse_g, conv_w, 512)
    dproj, dy, dya, da, dm, merged, dout, dpa, dpb, st_tail = tail(
        proj, ya, yb, attn, x, target, gate, pa_w, pb_w, wo_w, 256)
    g_wo = matmul_tn(merged, dout, "grad_w_out", 1024)
    g_pa = matmul_tn(ya, dpa, "grad_w_br_conv", 1024)
    g_pb = matmul_tn(yb, dpb, "grad_w_br_attn", 1024)
    dproj, st_conv = mid_bwd(dproj, proj, dya, conv_w, 512)
    gqw = jnp.zeros((1, 2 * HD), F32)
    gkw = jnp.zeros((1, 2 * HD), F32)
    grads = []
    for g, d in enumerate(DILATIONS):
        stats = (da, lse, dm) if d == 1 else to_residue_major([(da, 0), (lse, 0), (dm, 0)], d, 512, f"stats_rm_g{g}")
        dq, gq = attn_bwd_q(*srcs[g], *stats, qw2, kw2, g, d)
        dk, dv, gk = attn_bwd_kv(*srcs[g], *stats, qw2, kw2, g, d)
        grads.append((dq, dk, dv))
        gqw, gkw = gqw + gq[0:1], gkw + gk[0:1]
    dproj = qkv_grads_to_dproj(dproj, grads, 512)
    dh = proj_bwd_x(dproj, wg, 1024)
    g_win = proj_bwd_w(ht, dproj, 1024)
    grad_x, st_norm = norm_bwd(dh, x, dy, norm_w, scale, 512)
    dmod = jnp.concatenate([st_norm[0:1], st_norm[1:2], st_tail[0:1]], axis=1)
    loss_part = (0.5 / D) * jnp.sum(st_tail[1])
    small = dict(dmod=dmod, norm_w=st_norm[2:3], conv_w=st_conv[0:3],
                 q_norm_w=gqw[:, :HD] + gqw[:, HD:], k_norm_w=gkw[:, :HD] + gkw[:, HD:], loss=loss_part)
    return grad_x, small, (g_win, g_pa, g_pb, g_wo)


def kernel(x, c, w_ada, b_ada, norm_w, w_in, conv_w, q_norm_w, k_norm_w, w_br_conv, w_br_attn, w_out, loss_target, m_w_ada, m_b_ada, m_norm_w, m_w_in, m_conv_w, m_q_norm_w, m_k_norm_w, m_w_br_conv, m_w_br_attn, m_w_out, v_w_ada, v_b_ada, v_norm_w, v_w_in, v_conv_w, v_q_norm_w, v_k_norm_w, v_w_br_conv, v_w_br_attn, v_w_out):
    me = 4 * lax.axis_index("x") + 2 * lax.axis_index("y") + lax.axis_index("c")
    ncol = w_ada.shape[2]

    conv_pad = jnp.zeros((8, 128), F32).at[0:3].set(conv_w[0])
    wg, pa_g, pb_g, wo_g, c_all, conv_all = all_gather(
        [w_in[0].astype(BF16), w_br_conv[0].astype(BF16), w_br_attn[0].astype(BF16), w_out[0].astype(BF16),
         c, conv_pad], "gather_weights")
    pa_w = pa_g.reshape(D, D)
    wo_w = wo_g.reshape(D, D)
    pb_w = pb_g.transpose(1, 0, 2).reshape(CB, D)
    conv_full = conv_all[:, 0:3].transpose(1, 0, 2).reshape(3, D)
    c_all = c_all.reshape(NDEV, D)

    b_cols = lax.dynamic_slice(b_ada, (0, me * ncol), (1, ncol))
    mod_cols = ada_fwd(c_all, w_ada[0], b_cols)
    (mod_all,) = all_gather([mod_cols], "gather_mod")
    mod = lax.dynamic_index_in_dim(mod_all, me, axis=1, keepdims=False).reshape(1, 3 * D)
    shift, scale, gate = mod[:, 0:D], mod[:, D:2 * D], mod[:, 2 * D:3 * D]

    grad_x, small, (g_win, g_pa, g_pb, g_wo) = _local_step(
        x[0], loss_target[0], shift, scale, gate, norm_w, conv_full, q_norm_w, k_norm_w,
        wg, pa_w, pb_w, wo_w)

    packed = jnp.concatenate(
        [small["dmod"], small["norm_w"], small["conv_w"].reshape(1, 3 * D), small["q_norm_w"], small["k_norm_w"],
         jnp.full((1, 128), small["loss"], F32)], axis=1)
    (packed_all,) = all_gather([packed], "gather_small")
    tot = sum_parts(packed_all)
    loss = tot[0, 7 * D + 2 * HD]
    dmod_all = packed_all[:, 0, 0:3 * D]
    g_b_ada = tot[:, 0:3 * D]
    g_norm_w = tot[:, 3 * D:4 * D]
    g_conv = lax.dynamic_slice(tot[:, 4 * D:7 * D].reshape(3, D), (0, me * 128), (3, 128))
    g_qn = tot[:, 7 * D:7 * D + HD]
    g_kn = tot[:, 7 * D + HD:7 * D + 2 * HD]
    g_w_ada = ada_bwd(c_all.T, lax.dynamic_slice(dmod_all, (0, me * ncol), (NDEV, ncol)))

    g_pb_slabs = g_pb.reshape(CB, NDEV, 128).transpose(1, 0, 2)
    r_win, r_pa, r_pb, r_wo = all_to_all(
        [g_win, g_pa.reshape(NDEV, 128, D), g_pb_slabs, g_wo.reshape(NDEV, 128, D)], "scatter_grads")

    def upd(parts, w, m, v, name, rows):
        shape = w.shape
        w2, m2, v2 = (t.reshape(shape[-2:]) for t in (w, m, v))
        return [t.reshape(shape) for t in adamw(parts, w2, m2, v2, name, rows)]

    res = {
        "w_ada": upd(g_w_ada[None], w_ada, m_w_ada, v_w_ada, "adamw_w_ada", 256),
        "b_ada": upd(g_b_ada[None], b_ada, m_b_ada, v_b_ada, "adamw_b_ada", 1),
        "norm_w": upd(g_norm_w[None], norm_w, m_norm_w, v_norm_w, "adamw_norm_w", 1),
        "w_in": upd(r_win, w_in, m_w_in, v_w_in, "adamw_w_in", 128),
        "conv_w": upd(g_conv[None], conv_w, m_conv_w, v_conv_w, "adamw_conv_w", 3),
        "q_norm_w": upd(g_qn[None], q_norm_w, m_q_norm_w, v_q_norm_w, "adamw_q_norm_w", 1),
        "k_norm_w": upd(g_kn[None], k_norm_w, m_k_norm_w, v_k_norm_w, "adamw_k_norm_w", 1),
        "w_br_conv": upd(r_pa, w_br_conv, m_w_br_conv, v_w_br_conv, "adamw_w_br_conv", 128),
        "w_br_attn": upd(r_pb, w_br_attn, m_w_br_attn, v_w_br_attn, "adamw_w_br_attn", 512),
        "w_out": upd(r_wo, w_out, m_w_out, v_w_out, "adamw_w_out", 128),
    }
    names = ["w_ada", "b_ada", "norm_w", "w_in", "conv_w", "q_norm_w", "k_norm_w", "w_br_conv", "w_br_attn", "w_out"]
    return (loss, grad_x[None], *[res[n][0] for n in names], *[res[n][1] for n in names],
            *[res[n][2] for n in names], *[res[n][3] for n in names])
```

```python
import jax
import jax.numpy as jnp
from jax import lax
from jax.experimental import pallas as pl
from jax.experimental.pallas import tpu as pltpu

F32, BF16 = jnp.float32, jnp.bfloat16
D = 1024
NIN = 11264
NDEV = 8
SHARD = NIN // NDEV
HD = 64
QB = 128
CB = 512
CB_Q, CB_K, CB_V, CB_ZB = 8, 11, 14, 17
DILATIONS = (1, 4, 16)
EPS = 1e-6
NEG = -1e30
HALO = 16
MESH = pl.DeviceIdType.MESH

ADAM_LR, ADAM_B1, ADAM_B2, ADAM_EPS, ADAM_WD, ADAM_STEP = 0.001, 0.9, 0.999, 1e-08, 0.01, 10

NT = (((1,), (1,)), ((), ()))
TN = (((0,), (0,)), ((), ()))


def _cp(sem, vmem_mb=48):
    return pltpu.CompilerParams(dimension_semantics=sem, vmem_limit_bytes=vmem_mb << 20)


def _silu(z):
    return z * jax.nn.sigmoid(z)


def _coords():
    return lax.axis_index("x"), lax.axis_index("y"), lax.axis_index("c")


def all_gather(arrs, name):
    n = len(arrs)

    def body(*refs):
        ins, outs = refs[:n], refs[n:2 * n]
        send_sems, recv_sems, local_sems = refs[2 * n:]
        x, y, c = _coords()
        me, sibling = (x, y, c), (x, y, 1 - c)
        chips = [(1 - x, y), (x, 1 - y), (1 - x, 1 - y)]

        def slot(a, dev):
            return outs[a].at[4 * dev[0] + 2 * dev[1] + dev[2]]

        def copy(a, k, block, to, src=None):
            return pltpu.make_async_remote_copy(
                src_ref=slot(a, block) if src is None else src, dst_ref=slot(a, block),
                send_sem=send_sems.at[a, k], recv_sem=recv_sems.at[a, k],
                device_id=to, device_id_type=MESH)

        mine = [pltpu.make_async_copy(ins[a], slot(a, me), local_sems.at[a]) for a in range(n)]
        for cp in mine:
            cp.start()
        first = []
        for a in range(n):
            first.append(copy(a, 0, me, sibling, src=ins[a]))
            first += [copy(a, 1 + j, me, (*chip, c), src=ins[a]) for j, chip in enumerate(chips)]
        for cp in first:
            cp.start()
        passed = []
        for j, chip in enumerate(chips):
            for a in range(n):
                copy(a, 1 + j, (*chip, c), me).wait_recv()
                fwd = copy(a, 4 + j, (*chip, c), sibling)
                fwd.start()
                passed.append(fwd)
        for a in range(n):
            copy(a, 0, sibling, me).wait_recv()
            for j, chip in enumerate(chips):
                copy(a, 4 + j, (*chip, 1 - c), me).wait_recv()
        for cp in first + passed:
            cp.wait_send()
        for cp in mine:
            cp.wait()

    any_spec = pl.BlockSpec(memory_space=pl.ANY)
    return pl.pallas_call(
        body, name=name,
        out_shape=[jax.ShapeDtypeStruct((NDEV,) + a.shape, a.dtype) for a in arrs],
        in_specs=[any_spec] * n, out_specs=[any_spec] * n,
        scratch_shapes=[pltpu.SemaphoreType.DMA((n, 7)), pltpu.SemaphoreType.DMA((n, 7)),
                        pltpu.SemaphoreType.DMA((n,))],
    )(*arrs)


FLIPS = [(fx, fy, fc) for fx in (0, 1) for fy in (0, 1) for fc in (0, 1)][1:]


def _flip(dev, f):
    return tuple(1 - v if b else v for v, b in zip(dev, f))


def _dev_index(dev):
    return 4 * dev[0] + 2 * dev[1] + dev[2]


def gather_order(me_xyc):
    x, y, c = me_xyc
    chips = [(1 - x, y), (x, 1 - y), (1 - x, 1 - y)]
    devs = [(x, y, c), (x, y, 1 - c)] + [(*ch, c) for ch in chips] + [(*ch, 1 - c) for ch in chips]
    return jnp.stack([_dev_index(d) for d in devs]).astype(jnp.int32)


def scatter_order(me_xyc):
    devs = [_flip(me_xyc, f) for f in FLIPS] + [me_xyc]
    return jnp.stack([_dev_index(d) for d in devs]).astype(jnp.int32)


def ada_fwd(c_all, w_ada, b_cols):
    def body(c_ref, w_ref, b_ref, o_ref):
        a = _silu(c_ref[...]).astype(BF16)
        o_ref[...] = jnp.dot(a, w_ref[...].astype(BF16), preferred_element_type=F32) + b_ref[...]

    return pl.pallas_call(body, name="ada_fwd",
                          out_shape=jax.ShapeDtypeStruct((NDEV, w_ada.shape[1]), F32))(c_all, w_ada, b_cols)


def ada_bwd(c_all_t, dmod_cols):
    def body(c_ref, d_ref, o_ref):
        at = _silu(c_ref[...])
        acc = at[:, 0:1] * d_ref[0:1, :]
        for b in range(1, NDEV):
            acc = acc + at[:, b:b + 1] * d_ref[b:b + 1, :]
        o_ref[...] = acc

    return pl.pallas_call(body, name="ada_bwd",
                          out_shape=jax.ShapeDtypeStruct((D, dmod_cols.shape[1]), F32))(c_all_t, dmod_cols)


def sum_parts(parts):
    def body(p_ref, o_ref):
        acc = p_ref[0]
        for b in range(1, NDEV):
            acc = acc + p_ref[b]
        o_ref[...] = acc

    return pl.pallas_call(body, name="sum_parts",
                          out_shape=jax.ShapeDtypeStruct(parts.shape[1:], F32))(parts)


def adamw(parts, w, m, v, name, rows):
    n, r, ccols = parts.shape

    def body(p_ref, w_ref, m_ref, v_ref, g_ref, d_ref, nm_ref, nv_ref):
        g = p_ref[0].astype(F32)
        for b in range(1, n):
            g = g + p_ref[b].astype(F32)
        nm = ADAM_B1 * m_ref[...] + (1.0 - ADAM_B1) * g
        nv = ADAM_B2 * v_ref[...] + (1.0 - ADAM_B2) * (g * g)
        g_ref[...] = g
        nm_ref[...] = nm
        nv_ref[...] = nv
        m_hat = nm / (1.0 - ADAM_B1 ** ADAM_STEP)
        v_hat = nv / (1.0 - ADAM_B2 ** ADAM_STEP)
        d_ref[...] = -ADAM_LR * (m_hat / (jnp.sqrt(v_hat) + ADAM_EPS) + ADAM_WD * w_ref[...])

    blk = pl.BlockSpec((rows, ccols), lambda i: (i, 0))
    out = jax.ShapeDtypeStruct((r, ccols), F32)
    return pl.pallas_call(
        body, name=name, grid=(r // rows,),
        in_specs=[pl.BlockSpec((n, rows, ccols), lambda i: (0, i, 0)), blk, blk, blk],
        out_specs=[blk] * 4, out_shape=[out] * 4, compiler_params=_cp(("parallel",)))(parts, w, m, v)


def norm_fwd(x, nw, scale, shift, tm):
    s = x.shape[0]

    def body(x_ref, nw_ref, sc_ref, sh_ref, h_ref, ht_ref):
        xf = x_ref[...]
        r = lax.rsqrt(jnp.mean(xf * xf, axis=-1, keepdims=True) + EPS)
        h = (xf * r * nw_ref[...]) * (1.0 + sc_ref[...]) + sh_ref[...]
        h_ref[...] = h.astype(BF16)
        ht_ref[...] = h.T.astype(BF16)

    vec = pl.BlockSpec((1, D), lambda i: (0, 0))
    return pl.pallas_call(
        body, name="norm_fwd", grid=(s // tm,),
        in_specs=[pl.BlockSpec((tm, D), lambda i: (i, 0)), vec, vec, vec],
        out_specs=[pl.BlockSpec((tm, D), lambda i: (i, 0)), pl.BlockSpec((D, tm), lambda i: (0, i))],
        out_shape=[jax.ShapeDtypeStruct((s, D), BF16), jax.ShapeDtypeStruct((D, s), BF16)],
        compiler_params=_cp(("parallel",)))(x, nw, scale, shift)


def norm_bwd(dh, x, dy, nw, scale, tm):
    s = x.shape[0]

    def body(dh_ref, x_ref, dy_ref, nw_ref, sc_ref, gx_ref, st_ref):
        xf, g = x_ref[...], dh_ref[...]
        r = lax.rsqrt(jnp.mean(xf * xf, axis=-1, keepdims=True) + EPS)
        xh = xf * r
        dn = g * (1.0 + sc_ref[...])
        dxh = dn * nw_ref[...]
        gx_ref[...] = dy_ref[...] + r * (dxh - xh * jnp.mean(dxh * xh, axis=-1, keepdims=True))

        @pl.when(pl.program_id(0) == 0)
        def _():
            st_ref[...] = jnp.zeros_like(st_ref)

        st_ref[0:1, :] += jnp.sum(g, axis=0, keepdims=True)
        st_ref[1:2, :] += jnp.sum(g * xh * nw_ref[...], axis=0, keepdims=True)
        st_ref[2:3, :] += jnp.sum(dn * xh, axis=0, keepdims=True)

    vec = pl.BlockSpec((1, D), lambda i: (0, 0))
    row = pl.BlockSpec((tm, D), lambda i: (i, 0))
    return pl.pallas_call(
        body, name="norm_bwd", grid=(s // tm,),
        in_specs=[row, row, row, vec, vec],
        out_specs=[row, pl.BlockSpec((8, D), lambda i: (0, 0))],
        out_shape=[jax.ShapeDtypeStruct((s, D), F32), jax.ShapeDtypeStruct((8, D), F32)],
        compiler_params=_cp(("arbitrary",)))(dh, x, dy, nw, scale)


def proj_fwd_gather(h, w_shard, order, tm):
    s = h.shape[0]
    ni = s // tm

    def body(order_ref, h_ref, w_ref, o_ref, wg_ref, wbuf, send_sems, recv_sems, local_sem, load_sem):
        jj, i = pl.program_id(0), pl.program_id(1)
        x, y, c = _coords()
        me, sibling = (x, y, c), (x, y, 1 - c)
        chips = [(1 - x, y), (x, 1 - y), (1 - x, 1 - y)]

        def slot(dev):
            return wg_ref.at[_dev_index(dev)]

        def copy(k, block, to, src=None):
            return pltpu.make_async_remote_copy(
                src_ref=slot(block) if src is None else src, dst_ref=slot(block),
                send_sem=send_sems.at[k], recv_sem=recv_sems.at[k], device_id=to, device_id_type=MESH)

        mine = pltpu.make_async_copy(w_ref, slot(me), local_sem)
        first = [copy(0, me, sibling, src=w_ref)] + [copy(1 + j, me, (*ch, c), src=w_ref) for j, ch in enumerate(chips)]
        passed = [copy(4 + j, (*ch, c), sibling) for j, ch in enumerate(chips)]
        start = i == 0

        @pl.when(start & (jj == 0))
        def _():
            mine.start()
            for cp in first:
                cp.start()
            mine.wait()

        @pl.when(start & (jj == 1))
        def _():
            copy(0, sibling, me).wait_recv()

        for j, ch in enumerate(chips):
            @pl.when(start & (jj == 2 + j))
            def _(j=j, ch=ch):
                copy(1 + j, (*ch, c), me).wait_recv()
                passed[j].start()

            @pl.when(start & (jj == 5 + j))
            def _(j=j, ch=ch):
                copy(4 + j, (*ch, 1 - c), me).wait_recv()

        @pl.when(start)
        def _():
            load = pltpu.make_async_copy(wg_ref.at[order_ref[jj]], wbuf, load_sem)
            load.start()
            load.wait()

        o_ref[...] = jnp.dot(h_ref[...], wbuf[...], preferred_element_type=F32).astype(BF16)

        @pl.when((jj == NDEV - 1) & (i == ni - 1))
        def _():
            for cp in first + passed:
                cp.wait_send()

    any_spec = pl.BlockSpec(memory_space=pl.ANY)
    return pl.pallas_call(
        body, name="proj_fwd_gather",
        grid_spec=pltpu.PrefetchScalarGridSpec(
            num_scalar_prefetch=1, grid=(NDEV, ni),
            in_specs=[pl.BlockSpec((tm, D), lambda jj, i, o: (i, 0)), any_spec],
            out_specs=[pl.BlockSpec((tm, SHARD), lambda jj, i, o: (i, o[jj])), any_spec],
            scratch_shapes=[pltpu.VMEM((D, SHARD), BF16), pltpu.SemaphoreType.DMA((7,)),
                            pltpu.SemaphoreType.DMA((7,)), pltpu.SemaphoreType.DMA, pltpu.SemaphoreType.DMA]),
        out_shape=[jax.ShapeDtypeStruct((s, NIN), BF16), jax.ShapeDtypeStruct((NDEV, D, SHARD), BF16)],
        compiler_params=_cp(("arbitrary", "arbitrary")))(order, h, w_shard)


def proj_bwd(ht, dproj, wg, smalls, order, tt):
    s = dproj.shape[0]
    nk = s // tt
    n = len(smalls)

    def body(order_ref, ht_ref, dp_ref, w_ref, *rest):
        small_in = rest[:n]
        dh_ref, gw_ref, rwin_ref = rest[n:n + 3]
        small_out = rest[n + 3:2 * n + 3]
        acc, stage, send_sems, recv_sems, local_sems, stage_sems = rest[2 * n + 3:]
        t, k = pl.program_id(0), pl.program_id(1)
        me_xyc = _coords()
        me = _dev_index(me_xyc)
        peers = [_flip(me_xyc, f) for f in FLIPS]

        def exchange(a, kf, src_arr, dst_arr):
            pid = _dev_index(peers[kf])
            mk = lambda dst: pltpu.make_async_remote_copy(
                src_ref=src_arr.at[pid], dst_ref=dst, send_sem=send_sems.at[a, kf], recv_sem=recv_sems.at[a, kf],
                device_id=peers[kf], device_id_type=MESH)
            return mk(dst_arr.at[me]), mk(dst_arr.at[pid])

        small_pairs = [exchange(1 + a, kf, small_in[a], small_out[a]) for kf in range(7) for a in range(n)]
        small_own = [pltpu.make_async_copy(small_in[a].at[me], small_out[a].at[me], local_sems.at[1 + a])
                     for a in range(n)]
        win_pairs = [exchange(0, kf, gw_ref, rwin_ref) for kf in range(7)]
        win_own = pltpu.make_async_copy(gw_ref.at[me], rwin_ref.at[me], local_sems.at[0])

        def to_hbm(jj):
            slab = me if jj == 7 else _dev_index(peers[jj])
            return pltpu.make_async_copy(stage.at[jj % 2], gw_ref.at[slab], stage_sems.at[jj % 2])

        @pl.when((t == 0) & (k == 0))
        def _():
            for cp in small_own:
                cp.start()
            for send, _ in small_pairs:
                send.start()

        @pl.when(t < NDEV)
        def _():
            p = jnp.dot(ht_ref[...], dp_ref[...], preferred_element_type=F32)

            @pl.when(k == 0)
            def _():
                acc[...] = p

            @pl.when(k > 0)
            def _():
                acc[...] += p

        for jj in range(NDEV):
            @pl.when((t == jj) & (k == nk - 1))
            def _(jj=jj):
                stage[jj % 2] = acc[...].astype(BF16)
                to_hbm(jj).start()

            @pl.when((t == jj + 1) & (k == 0))
            def _(jj=jj):
                to_hbm(jj).wait()
                if jj < 7:
                    win_pairs[jj][0].start()
                else:
                    win_own.start()

        @pl.when(t >= NDEV)
        def _():
            p = lax.dot_general(dp_ref[...], w_ref[...], NT, preferred_element_type=F32)

            @pl.when(k == 0)
            def _():
                dh_ref[...] = p

            @pl.when(k > 0)
            def _():
                dh_ref[...] += p

        @pl.when((t == 2 * NDEV - 1) & (k == nk - 1))
        def _():
            for _, recv in win_pairs + small_pairs:
                recv.wait_recv()
            for send, _ in win_pairs + small_pairs:
                send.wait_send()
            win_own.wait()
            for cp in small_own:
                cp.wait()

    any_spec = pl.BlockSpec(memory_space=pl.ANY)
    first = lambda t: t < NDEV
    outs = pl.pallas_call(
        body, name="proj_bwd",
        grid_spec=pltpu.PrefetchScalarGridSpec(
            num_scalar_prefetch=1, grid=(2 * NDEV, nk),
            in_specs=[pl.BlockSpec((D, tt), lambda t, k, o: (0, jnp.where(first(t), k, nk - 1))),
                      pl.BlockSpec((tt, SHARD), lambda t, k, o: (jnp.where(first(t), k, t - NDEV),
                                                                 jnp.where(first(t), o[jnp.minimum(t, NDEV - 1)], k))),
                      pl.BlockSpec((None, D, SHARD), lambda t, k, o: (jnp.where(first(t), 0, k), 0, 0))]
                     + [any_spec] * n,
            out_specs=[pl.BlockSpec((tt, D), lambda t, k, o: (jnp.where(first(t), 0, t - NDEV), 0))]
                      + [any_spec] * (2 + n),
            scratch_shapes=[pltpu.VMEM((D, SHARD), F32), pltpu.VMEM((2, D, SHARD), BF16),
                            pltpu.SemaphoreType.DMA((1 + n, 7)), pltpu.SemaphoreType.DMA((1 + n, 7)),
                            pltpu.SemaphoreType.DMA((1 + n,)), pltpu.SemaphoreType.DMA((2,))]),
        out_shape=[jax.ShapeDtypeStruct((s, D), F32), jax.ShapeDtypeStruct((NDEV, D, SHARD), BF16),
                   jax.ShapeDtypeStruct((NDEV, D, SHARD), BF16)]
                  + [jax.ShapeDtypeStruct(a.shape, a.dtype) for a in smalls],
        compiler_params=_cp(("arbitrary", "arbitrary"), 56))(order, ht, dproj, wg, *smalls)
    return outs[0], outs[2], outs[3:]


def matmul_tn(a, b, name, tk):
    s, m = a.shape
    n = b.shape[1]
    nk = s // tk

    def body(a_ref, b_ref, o_ref, acc_ref):
        k = pl.program_id(0)
        p = lax.dot_general(a_ref[...], b_ref[...], TN, preferred_element_type=F32)

        @pl.when(k == 0)
        def _():
            acc_ref[...] = p

        @pl.when(k > 0)
        def _():
            acc_ref[...] += p

        @pl.when(k == nk - 1)
        def _():
            o_ref[...] = acc_ref[...].astype(BF16)

    return pl.pallas_call(
        body, name=name, grid=(nk,),
        in_specs=[pl.BlockSpec((tk, m), lambda k: (k, 0)), pl.BlockSpec((tk, n), lambda k: (k, 0))],
        out_specs=pl.BlockSpec((m, n), lambda k: (0, 0)),
        out_shape=jax.ShapeDtypeStruct((m, n), BF16),
        scratch_shapes=[pltpu.VMEM((m, n), F32)],
        compiler_params=_cp(("arbitrary",)))(a, b)


LANES = 128


def to_residue_major(srcs, dil, tm, name):
    s = srcs[0][0].shape[0]
    n = len(srcs)
    rows = tm // dil

    def body(*refs):
        ins, outs, bufs = refs[:n], refs[n:2 * n], refs[2 * n:]
        for a in range(n):
            for k in range(CB // LANES):
                lanes = slice(k * LANES, (k + 1) * LANES)
                bufs[a][k] = ins[a][:, lanes].astype(F32)
                for r in range(dil):
                    outs[a][r, :, lanes] = bufs[a][k, pl.ds(r, rows, stride=dil), :].astype(outs[a].dtype)

    outs = pl.pallas_call(
        body, name=name, grid=(s // tm,),
        in_specs=[pl.BlockSpec((tm, CB), lambda i, cb=cb: (i, cb)) for _, cb in srcs],
        out_specs=[pl.BlockSpec((dil, rows, CB), lambda i: (0, i, 0))] * n,
        out_shape=[jax.ShapeDtypeStruct((dil, s // dil, CB), a.dtype) for a, _ in srcs],
        scratch_shapes=[pltpu.VMEM((CB // LANES, tm, LANES), F32)] * n,
        compiler_params=_cp(("parallel",)))(*[a for a, _ in srcs])
    return [o.reshape(s, CB) for o in outs]


def _from_residue_major(ref, buf, dil):
    if dil == 1:
        return ref[0].astype(F32)
    rows = ref.shape[1]
    for k in range(CB // LANES):
        for r in range(dil):
            buf[k, pl.ds(r, rows, stride=dil), :] = ref[r, :, k * LANES:(k + 1) * LANES].astype(F32)
    return jnp.concatenate([buf[k] for k in range(CB // LANES)], axis=1)


def qkv_grads_to_dproj(dproj, grads, tm):
    s = dproj.shape[0]
    flat = [(t.reshape(d, s // d, CB), d, 3 * kind + g)
            for g, d in enumerate(DILATIONS) for kind, t in enumerate(grads[g])]

    def body(*refs):
        ins, o_ref, buf = refs[1:1 + len(flat)], refs[1 + len(flat)], refs[2 + len(flat)]
        j = pl.program_id(1)
        for ref, (_, d, jj) in zip(ins, flat):
            @pl.when(j == jj)
            def _(ref=ref, d=d):
                o_ref[...] = _from_residue_major(ref, buf, d).astype(BF16)

    return pl.pallas_call(
        body, name="qkv_grads_to_dproj", grid=(s // tm, 9),
        in_specs=[pl.BlockSpec(memory_space=pl.ANY)]
                 + [pl.BlockSpec((d, tm // d, CB), lambda i, j: (0, i, 0)) for _, d, _ in flat],
        out_specs=pl.BlockSpec((tm, CB), lambda i, j: (i, CB_Q + j)),
        out_shape=jax.ShapeDtypeStruct((s, NIN), BF16),
        input_output_aliases={0: 0},
        scratch_shapes=[pltpu.VMEM((CB // LANES, tm, LANES), F32)],
        compiler_params=_cp(("arbitrary", "arbitrary")))(dproj, *[t for t, _, _ in flat])


def _lane_lo():
    return lax.broadcasted_iota(jnp.int32, (1, 2 * HD), 1) < HD


def _half_mean(t, lo):
    s_lo = jnp.sum(jnp.where(lo, t, 0.0), axis=-1, keepdims=True)
    s_all = jnp.sum(t, axis=-1, keepdims=True)
    return jnp.where(lo, s_lo, s_all - s_lo) * (1.0 / HD)


def _qk_norm(t, w, lo):
    r = lax.rsqrt(_half_mean(t * t, lo) + EPS)
    xh = t * r
    return xh, r, xh * w


def _norm_bwd_pair(dn, xh, r, w, lo):
    dxh = dn * w
    return r * (dxh - xh * _half_mean(dxh * xh, lo))


def _masks(other_ok):
    qi = lax.broadcasted_iota(jnp.int32, (QB, QB), 0)
    kj = lax.broadcasted_iota(jnp.int32, (QB, QB), 1)
    return (kj >= qi) & other_ok, kj <= qi


def attn_fwd(q_src, k_src, v_src, qw2, kw2, g, dil):
    s = q_src[0].shape[0]
    nb = s // dil // QB

    def body(q_ref, kp_ref, kc_ref, vp_ref, vc_ref, qw_ref, kw_ref, o_ref, l_ref):
        b = pl.program_id(1)
        lo = _lane_lo()
        m_prev, m_cur = _masks(b > 0)
        mask = jnp.concatenate([m_prev, m_cur], axis=1)
        for i in range(4):
            sl = slice(2 * HD * i, 2 * HD * (i + 1))
            _, _, qs = _qk_norm(q_ref[:, sl].astype(F32), qw_ref[...], lo)
            qs = qs * (HD ** -0.5)
            kk = jnp.concatenate([kp_ref[:, sl], kc_ref[:, sl]], axis=0).astype(F32)
            _, _, ks = _qk_norm(kk, kw_ref[...], lo)
            ks = ks.astype(BF16)
            vv = jnp.concatenate([vp_ref[:, sl], vc_ref[:, sl]], axis=0)
            outs, lses = [], []
            for hmask in (lo, ~lo):
                qh = jnp.where(hmask, qs, 0.0).astype(BF16)
                sc = lax.dot_general(qh, ks, NT, preferred_element_type=F32)
                sc = jnp.where(mask, sc, NEG)
                mx = jnp.max(sc, axis=-1, keepdims=True)
                p = jnp.exp(sc - mx)
                den = jnp.sum(p, axis=-1, keepdims=True)
                o = jnp.dot(p.astype(BF16), vv, preferred_element_type=F32)
                outs.append(o / den)
                lses.append(mx + jnp.log(den))
            o_ref[:, sl] = jnp.where(lo, outs[0], outs[1])
            l_ref[:, sl] = jnp.where(lo, lses[0], lses[1])

    cur = lambda cb: pl.BlockSpec((QB, CB), lambda r, b: (r * nb + b, cb))
    prev = lambda cb: pl.BlockSpec((QB, CB), lambda r, b: (r * nb + jnp.maximum(b - 1, 0), cb))
    vec = pl.BlockSpec((1, 2 * HD), lambda r, b: (0, 0))
    out = jax.ShapeDtypeStruct((s, CB), F32)
    return pl.pallas_call(
        body, name=f"attn_fwd_g{g}", grid=(dil, nb),
        in_specs=[cur(q_src[1]), prev(k_src[1]), cur(k_src[1]), prev(v_src[1]), cur(v_src[1]), vec, vec],
        out_specs=[cur(0)] * 2, out_shape=[out, out],
        compiler_params=_cp(("parallel", "parallel")))(
            q_src[0], k_src[0], k_src[0], v_src[0], v_src[0], qw2, kw2)


def attn_bwd_q(q_src, k_src, v_src, da, lse, dm, qw2, kw2, g, dil):
    s = q_src[0].shape[0]
    nb = s // dil // QB

    def body(q_ref, kp_ref, kc_ref, vp_ref, vc_ref, da_ref, l_ref, dm_ref, qw_ref, kw_ref, dq_ref, gw_ref):
        r, b = pl.program_id(0), pl.program_id(1)
        lo = _lane_lo()
        m_prev, m_cur = _masks(b > 0)
        mask = jnp.concatenate([m_prev, m_cur], axis=1)

        @pl.when((r == 0) & (b == 0))
        def _():
            gw_ref[...] = jnp.zeros_like(gw_ref)

        gw = jnp.zeros((1, 2 * HD), F32)
        for i in range(4):
            sl = slice(2 * HD * i, 2 * HD * (i + 1))
            xh, rs, qs = _qk_norm(q_ref[:, sl].astype(F32), qw_ref[...], lo)
            qs = qs * (HD ** -0.5)
            kk = jnp.concatenate([kp_ref[:, sl], kc_ref[:, sl]], axis=0).astype(F32)
            _, _, ks = _qk_norm(kk, kw_ref[...], lo)
            ks = ks.astype(BF16)
            vv = jnp.concatenate([vp_ref[:, sl], vc_ref[:, sl]], axis=0)
            da2, l2, d2 = da_ref[:, sl], l_ref[:, sl], dm_ref[:, sl]
            dqs = jnp.zeros((QB, 2 * HD), F32)
            for h, hmask in enumerate((lo, ~lo)):
                qh = jnp.where(hmask, qs, 0.0).astype(BF16)
                sc = lax.dot_general(qh, ks, NT, preferred_element_type=F32)
                sc = jnp.where(mask, sc, NEG)
                p = jnp.exp(sc - l2[:, HD * h:HD * h + 1])
                dah = jnp.where(hmask, da2, jnp.zeros_like(da2))
                dp = lax.dot_general(dah, vv, NT, preferred_element_type=F32)
                ds = p * (dp - d2[:, HD * h:HD * h + 1])
                dq_h = jnp.dot(ds.astype(BF16), ks, preferred_element_type=F32)
                dqs = dqs + jnp.where(hmask, dq_h, 0.0)
            dqs = dqs * (HD ** -0.5)
            gw = gw + jnp.sum(dqs * xh, axis=0, keepdims=True)
            dq_ref[:, sl] = _norm_bwd_pair(dqs, xh, rs, qw_ref[...], lo).astype(BF16)
        gw_ref[0:1, :] += gw

    cur = lambda cb: pl.BlockSpec((QB, CB), lambda r, b: (r * nb + b, cb))
    prev = lambda cb: pl.BlockSpec((QB, CB), lambda r, b: (r * nb + jnp.maximum(b - 1, 0), cb))
    vec = pl.BlockSpec((1, 2 * HD), lambda r, b: (0, 0))
    return pl.pallas_call(
        body, name=f"attn_bwd_q_g{g}", grid=(dil, nb),
        in_specs=[cur(q_src[1]), prev(k_src[1]), cur(k_src[1]), prev(v_src[1]), cur(v_src[1]),
                  cur(0), cur(0), cur(0), vec, vec],
        out_specs=[cur(0), pl.BlockSpec((8, 2 * HD), lambda r, b: (0, 0))],
        out_shape=[jax.ShapeDtypeStruct((s, CB), BF16), jax.ShapeDtypeStruct((8, 2 * HD), F32)],
        compiler_params=_cp(("arbitrary", "arbitrary")))(
            q_src[0], k_src[0], k_src[0], v_src[0], v_src[0], da, lse, dm, qw2, kw2)


def attn_bwd_kv(q_src, k_src, v_src, da, lse, dm, qw2, kw2, g, dil):
    s = q_src[0].shape[0]
    nb = s // dil // QB

    def body(k_ref, v_ref, qc_ref, qn_ref, dac_ref, dan_ref, lc_ref, ln_ref, dc_ref, dn_ref,
             qw_ref, kw_ref, dk_ref, dv_ref, gw_ref):
        r, cblk = pl.program_id(0), pl.program_id(1)

        @pl.when((r == 0) & (cblk == 0))
        def _():
            gw_ref[...] = jnp.zeros_like(gw_ref)

        lo = _lane_lo()
        m_next, m_cur = _masks(cblk < nb - 1)
        mask = jnp.concatenate([m_cur, m_next], axis=0)
        gw = jnp.zeros((1, 2 * HD), F32)
        for i in range(4):
            sl = slice(2 * HD * i, 2 * HD * (i + 1))
            qq = jnp.concatenate([qc_ref[:, sl], qn_ref[:, sl]], axis=0).astype(F32)
            _, _, qs = _qk_norm(qq, qw_ref[...], lo)
            qs = qs * (HD ** -0.5)
            xh, rs, ks = _qk_norm(k_ref[:, sl].astype(F32), kw_ref[...], lo)
            ks = ks.astype(BF16)
            vv = v_ref[:, sl]
            da2 = jnp.concatenate([dac_ref[:, sl], dan_ref[:, sl]], axis=0)
            l2 = jnp.concatenate([lc_ref[:, sl], ln_ref[:, sl]], axis=0)
            d2 = jnp.concatenate([dc_ref[:, sl], dn_ref[:, sl]], axis=0)
            dks = jnp.zeros((QB, 2 * HD), F32)
            dvv = jnp.zeros((QB, 2 * HD), F32)
            for h, hmask in enumerate((lo, ~lo)):
                qh = jnp.where(hmask, qs, 0.0).astype(BF16)
                sc = lax.dot_general(qh, ks, NT, preferred_element_type=F32)
                sc = jnp.where(mask, sc, NEG)
                p = jnp.exp(sc - l2[:, HD * h:HD * h + 1])
                dah = jnp.where(hmask, da2, jnp.zeros_like(da2))
                dp = lax.dot_general(dah, vv, NT, preferred_element_type=F32)
                ds = p * (dp - d2[:, HD * h:HD * h + 1])
                dvv = dvv + lax.dot_general(p.astype(BF16), dah, TN, preferred_element_type=F32)
                dks = dks + lax.dot_general(ds.astype(BF16), qh, TN, preferred_element_type=F32)
            gw = gw + jnp.sum(dks * xh, axis=0, keepdims=True)
            dk_ref[:, sl] = _norm_bwd_pair(dks, xh, rs, kw_ref[...], lo).astype(BF16)
            dv_ref[:, sl] = dvv.astype(BF16)
        gw_ref[0:1, :] += gw

    cur = lambda cb: pl.BlockSpec((QB, CB), lambda r, c: (r * nb + c, cb))
    nxt = lambda cb: pl.BlockSpec((QB, CB), lambda r, c: (r * nb + jnp.minimum(c + 1, nb - 1), cb))
    vec = pl.BlockSpec((1, 2 * HD), lambda r, c: (0, 0))
    out = jax.ShapeDtypeStruct((s, CB), BF16)
    return pl.pallas_call(
        body, name=f"attn_bwd_kv_g{g}", grid=(dil, nb),
        in_specs=[cur(k_src[1]), cur(v_src[1]), cur(q_src[1]), nxt(q_src[1]),
                  cur(0), nxt(0), cur(0), nxt(0), cur(0), nxt(0), vec, vec],
        out_specs=[cur(0), cur(0), pl.BlockSpec((8, 2 * HD), lambda r, c: (0, 0))],
        out_shape=[out, out, jax.ShapeDtypeStruct((8, 2 * HD), F32)],
        compiler_params=_cp(("arbitrary", "arbitrary")))(
            k_src[0], v_src[0], q_src[0], q_src[0], da, da, lse, lse, dm, dm, qw2, kw2)


def _conv_taps(u, u_prev, first):
    tm = u.shape[0]
    row = lax.broadcasted_iota(jnp.int32, (tm, 1), 0)
    up = jnp.where(first, 0.0, u_prev)
    u1 = jnp.where(row == 0, up[HALO - 1:HALO, :], pltpu.roll(u, 1, 0))
    u2 = jnp.where(row == 0, up[HALO - 2:HALO - 1, :],
                   jnp.where(row == 1, up[HALO - 1:HALO, :], pltpu.roll(u, 2, 0)))
    return u1, u2


def mid_fwd(proj, o_g, lse_g, conv_w, tm):
    s = proj.shape[0]
    hb = tm // HALO

    def body(ba_ref, ca_ref, xa_ref, za_ref, cah_ref, xah_ref, zb_ref,
             o0, o1, o2, l0, l1, l2, w_ref, ya_ref, yb_ref, at_ref, ls_ref, buf_o, buf_l):
        first = pl.program_id(0) == 0
        u = ca_ref[...].astype(F32) * xa_ref[...].astype(F32)
        u1, u2 = _conv_taps(u, cah_ref[...].astype(F32) * xah_ref[...].astype(F32), first)
        conv = w_ref[0:1, :] * u2 + w_ref[1:2, :] * u1 + w_ref[2:3, :] * u
        ya_ref[...] = (ba_ref[...].astype(F32) * conv * _silu(za_ref[...].astype(F32))).astype(BF16)
        ls = [_from_residue_major(l, buf_l.at[g], d) for g, (l, d) in enumerate(zip((l0, l1, l2), DILATIONS))]
        mx = jnp.maximum(jnp.maximum(ls[0], ls[1]), ls[2])
        es = [jnp.exp(l - mx) for l in ls]
        den = es[0] + es[1] + es[2]
        num = jnp.zeros_like(den)
        for e, o, d in zip(es, (o0, o1, o2), DILATIONS):
            num = num + e * _from_residue_major(o, buf_o, d)
        attn = num / den
        at_ref[...] = attn
        ls_ref[...] = mx + jnp.log(den)
        yb_ref[...] = (attn * _silu(zb_ref[...].astype(F32))).astype(BF16)

    col = lambda j: pl.BlockSpec((tm, D), lambda i: (i, j))
    halo = lambda j: pl.BlockSpec((HALO, D), lambda i: (jnp.maximum(i * hb - 1, 0), j))
    loc = pl.BlockSpec((tm, CB), lambda i: (i, 0))
    rm = [pl.BlockSpec((d, tm // d, CB), lambda i: (0, i, 0)) for d in DILATIONS]
    rm_view = lambda ts: [t.reshape(d, s // d, CB) for t, d in zip(ts, DILATIONS)]
    return pl.pallas_call(
        body, name="mid_fwd", grid=(s // tm,),
        in_specs=[col(0), col(1), col(2), col(3), halo(1), halo(2),
                  pl.BlockSpec((tm, CB), lambda i: (i, CB_ZB))] + rm + rm
                 + [pl.BlockSpec((3, D), lambda i: (0, 0))],
        out_specs=[pl.BlockSpec((tm, D), lambda i: (i, 0)), loc, loc, loc],
        out_shape=[jax.ShapeDtypeStruct((s, D), BF16), jax.ShapeDtypeStruct((s, CB), BF16),
                   jax.ShapeDtypeStruct((s, CB), F32), jax.ShapeDtypeStruct((s, CB), F32)],
        scratch_shapes=[pltpu.VMEM((CB // LANES, tm, LANES), F32), pltpu.VMEM((3, CB // LANES, tm, LANES), F32)],
        compiler_params=_cp(("parallel",)))(
            proj, proj, proj, proj, proj, proj, proj, *rm_view(o_g), *rm_view(lse_g), conv_w)


def mid_bwd(dproj, proj, dya, conv_w, tm):
    s = proj.shape[0]
    hb = tm // HALO
    nblk = s // tm
    last_h = s // HALO - 1

    def body(_, ba_ref, ca_ref, xa_ref, za_ref, cah_ref, xah_ref, ban_ref, zan_ref, dy_ref, dyn_ref, w_ref,
             o_ref, gw_ref):
        i = pl.program_id(0)
        ba, ca, xa, za = (t[...].astype(F32) for t in (ba_ref, ca_ref, xa_ref, za_ref))
        u = ca * xa
        u1, u2 = _conv_taps(u, cah_ref[...].astype(F32) * xah_ref[...].astype(F32), i == 0)
        w0, w1, w2 = w_ref[0:1, :], w_ref[1:2, :], w_ref[2:3, :]
        conv = w0 * u2 + w1 * u1 + w2 * u
        sg = jax.nn.sigmoid(za)
        sz = za * sg
        dy = dy_ref[...].astype(F32)
        dconv = dy * ba * sz
        dcn = dyn_ref[...].astype(F32) * ban_ref[...].astype(F32) * _silu(zan_ref[...].astype(F32))
        dcn = jnp.where(i == nblk - 1, 0.0, dcn)
        row = lax.broadcasted_iota(jnp.int32, (tm, 1), 0)
        d1 = jnp.where(row == tm - 1, dcn[0:1, :], pltpu.roll(dconv, tm - 1, 0))
        d2 = jnp.where(row == tm - 2, dcn[0:1, :],
                       jnp.where(row == tm - 1, dcn[1:2, :], pltpu.roll(dconv, tm - 2, 0)))
        du = w2 * dconv + w1 * d1 + w0 * d2
        o_ref[:, 0:D] = (dy * conv * sz).astype(BF16)
        o_ref[:, D:2 * D] = (du * xa).astype(BF16)
        o_ref[:, 2 * D:3 * D] = (du * ca).astype(BF16)
        o_ref[:, 3 * D:4 * D] = (dy * ba * conv * (sg * (1.0 + za * (1.0 - sg)))).astype(BF16)

        @pl.when(i == 0)
        def _():
            gw_ref[...] = jnp.zeros_like(gw_ref)

        gw_ref[0:1, :] += jnp.sum(dconv * u2, axis=0, keepdims=True)
        gw_ref[1:2, :] += jnp.sum(dconv * u1, axis=0, keepdims=True)
        gw_ref[2:3, :] += jnp.sum(dconv * u, axis=0, keepdims=True)

    col = lambda j: pl.BlockSpec((tm, D), lambda i: (i, j))
    halo_prev = lambda j: pl.BlockSpec((HALO, D), lambda i: (jnp.maximum(i * hb - 1, 0), j))
    halo_next = lambda j: pl.BlockSpec((HALO, D), lambda i: (jnp.minimum((i + 1) * hb, last_h), j))
    return pl.pallas_call(
        body, name="mid_bwd", grid=(nblk,),
        in_specs=[pl.BlockSpec(memory_space=pl.ANY), col(0), col(1), col(2), col(3),
                  halo_prev(1), halo_prev(2), halo_next(0), halo_next(3),
                  pl.BlockSpec((tm, D), lambda i: (i, 0)), halo_next(0),
                  pl.BlockSpec((3, D), lambda i: (0, 0))],
        out_specs=[pl.BlockSpec((tm, 4 * D), lambda i: (i, 0)), pl.BlockSpec((8, D), lambda i: (0, 0))],
        out_shape=[jax.ShapeDtypeStruct((s, NIN), BF16), jax.ShapeDtypeStruct((8, D), F32)],
        input_output_aliases={0: 0},
        compiler_params=_cp(("arbitrary",)))(dproj, proj, proj, proj, proj, proj, proj, proj, proj, dya, dya, conv_w)


def tail(proj, ya, yb, attn, x, target, gate, pa_w, pb_w, wo_w, tm):
    s = proj.shape[0]
    ni = s // tm
    bd = (lax.broadcasted_iota(jnp.int32, (CB, CB), 0) // HD
          == lax.broadcasted_iota(jnp.int32, (CB, CB), 1) // HD).astype(BF16)

    def body(ya_ref, yb_ref, ga_ref, gb_ref, zb_ref, at_ref, x_ref, t_ref, gate_ref, pa_ref, pb_ref, wo_ref,
             bd_ref, dp_ref, dy_ref, dya_ref, da_ref, dm_ref, mg_ref, do_ref, dpa_ref, dpb_ref, st_ref, pieces):
        i, j = pl.program_id(0), pl.program_id(1)

        @pl.when((i == 0) & (j == 0))
        def _():
            st_ref[...] = jnp.zeros_like(st_ref)

        @pl.when(j == 0)
        def _():
            gate_v = gate_ref[...]
            pa = jnp.dot(ya_ref[...], pa_ref[...], preferred_element_type=F32)
            pb = jnp.dot(yb_ref[...], pb_ref[...], preferred_element_type=F32)
            sa = jax.nn.sigmoid(ga_ref[...].astype(F32))
            sb = jax.nn.sigmoid(gb_ref[...].astype(F32))
            merged = (sa * pa + sb * pb).astype(BF16)
            mg_ref[...] = merged
            out = jnp.dot(merged, wo_ref[...], preferred_element_type=F32)
            err = x_ref[...] + gate_v * out - t_ref[...]
            dy = err * (1.0 / D)
            dy_ref[...] = dy
            st_ref[0:1, :] += jnp.sum(dy * out, axis=0, keepdims=True)
            st_ref[1:2, :] += jnp.sum(err * err, axis=0, keepdims=True)
            dout = (gate_v * dy).astype(BF16)
            do_ref[...] = dout
            dmg = lax.dot_general(dout, wo_ref[...], NT, preferred_element_type=F32)
            dpa = (dmg * sa).astype(BF16)
            dpb = (dmg * sb).astype(BF16)
            dpa_ref[...] = dpa
            dpb_ref[...] = dpb
            dga = (dmg * pa * sa * (1.0 - sa)).astype(BF16)
            dgb = (dmg * pb * sb * (1.0 - sb)).astype(BF16)
            pieces[1] = dga[:, :CB]
            pieces[2] = dga[:, CB:]
            pieces[3] = dgb[:, :CB]
            pieces[4] = dgb[:, CB:]
            dya_ref[...] = lax.dot_general(dpa, pa_ref[...], NT, preferred_element_type=F32).astype(BF16)
            dyb = lax.dot_general(dpb, pb_ref[...], NT, preferred_element_type=F32)
            zb = zb_ref[...].astype(F32)
            sg = jax.nn.sigmoid(zb)
            attn_v = at_ref[...]
            dattn = dyb * (zb * sg)
            da_ref[...] = dattn.astype(BF16)
            pieces[0] = (dyb * attn_v * (sg * (1.0 + zb * (1.0 - sg)))).astype(BF16)
            prod = dattn * attn_v
            hi = prod.astype(BF16)
            lo = (prod - hi.astype(F32)).astype(BF16)
            dm_ref[...] = (jnp.dot(hi, bd_ref[...], preferred_element_type=F32)
                           + jnp.dot(lo, bd_ref[...], preferred_element_type=F32))

        dp_ref[...] = pieces[j]

    row = lambda w: pl.BlockSpec((tm, w), lambda i, j: (i, 0))
    pcol = lambda w, jb: pl.BlockSpec((tm, w), lambda i, j: (i, jb))
    full = lambda a: pl.BlockSpec(a.shape, lambda i, j: (0, 0))
    return pl.pallas_call(
        body, name="tail", grid=(ni, 5),
        in_specs=[row(D), row(CB), pcol(D, 9), pcol(D, 10), pcol(CB, CB_ZB), row(CB), row(D), row(D),
                  pl.BlockSpec((1, D), lambda i, j: (0, 0)), full(pa_w), full(pb_w), full(wo_w), full(bd)],
        out_specs=[pl.BlockSpec((tm, CB), lambda i, j: (i, CB_ZB + j)),
                   row(D), row(D), row(CB), row(CB), row(D), row(D), row(D), row(D),
                   pl.BlockSpec((8, D), lambda i, j: (0, 0))],
        out_shape=[jax.ShapeDtypeStruct((s, NIN), BF16), jax.ShapeDtypeStruct((s, D), F32),
                   jax.ShapeDtypeStruct((s, D), BF16), jax.ShapeDtypeStruct((s, CB), BF16),
                   jax.ShapeDtypeStruct((s, CB), F32)] + [jax.ShapeDtypeStruct((s, D), BF16)] * 4
                  + [jax.ShapeDtypeStruct((8, D), F32)],
        scratch_shapes=[pltpu.VMEM((5, tm, CB), BF16)],
        compiler_params=_cp(("arbitrary", "arbitrary"), 56))(
            ya, yb, proj, proj, proj, attn, x, target, gate, pa_w, pb_w, wo_w, bd)


def _local_step(x, target, shift, scale, gate, norm_w, conv_w, qw, kw, w_shard, pa_w, pb_w, wo_w, me_xyc):
    qw2, kw2 = jnp.tile(qw, (1, 2)), jnp.tile(kw, (1, 2))
    h, ht = norm_fwd(x, norm_w, scale, shift, 512)
    proj, wg = proj_fwd_gather(h, w_shard, gather_order(me_xyc), 1024)
    srcs = []
    for g, d in enumerate(DILATIONS):
        cols = [(proj, CB_Q + g), (proj, CB_K + g), (proj, CB_V + g)]
        srcs.append(cols if d == 1 else [(t, 0) for t in to_residue_major(cols, d, 512, f"qkv_rm_g{g}")])
    o_g, lse_g = zip(*[attn_fwd(*srcs[g], qw2, kw2, g, d) for g, d in enumerate(DILATIONS)])
    ya, yb, attn, lse = mid_fwd(proj, o_g, lse_g, conv_w, 512)
    dproj, dy, dya, da, dm, merged, dout, dpa, dpb, st_tail = tail(
        proj, ya, yb, attn, x, target, gate, pa_w, pb_w, wo_w, 256)
    g_wo = matmul_tn(merged, dout, "grad_w_out", 1024)
    g_pa = matmul_tn(ya, dpa, "grad_w_br_conv", 1024)
    g_pb = matmul_tn(yb, dpb, "grad_w_br_attn", 1024)
    dproj, st_conv = mid_bwd(dproj, proj, dya, conv_w, 512)
    gqw = jnp.zeros((1, 2 * HD), F32)
    gkw = jnp.zeros((1, 2 * HD), F32)
    grads = []
    for g, d in enumerate(DILATIONS):
        stats = (da, lse, dm) if d == 1 else to_residue_major([(da, 0), (lse, 0), (dm, 0)], d, 512, f"stats_rm_g{g}")
        dq, gq = attn_bwd_q(*srcs[g], *stats, qw2, kw2, g, d)
        dk, dv, gk = attn_bwd_kv(*srcs[g], *stats, qw2, kw2, g, d)
        grads.append((dq, dk, dv))
        gqw, gkw = gqw + gq[0:1], gkw + gk[0:1]
    dproj = qkv_grads_to_dproj(dproj, grads, 512)
    slabs = [g_pa.reshape(NDEV, 128, D), g_pb.reshape(CB, NDEV, 128).transpose(1, 0, 2), g_wo.reshape(NDEV, 128, D)]
    dh, r_win, (r_pa, r_pb, r_wo) = proj_bwd(ht, dproj, wg, slabs, scatter_order(me_xyc), 1024)
    grad_x, st_norm = norm_bwd(dh, x, dy, norm_w, scale, 512)
    dmod = jnp.concatenate([st_norm[0:1], st_norm[1:2], st_tail[0:1]], axis=1)
    loss_part = (0.5 / D) * jnp.sum(st_tail[1])
    small = dict(dmod=dmod, norm_w=st_norm[2:3], conv_w=st_conv[0:3],
                 q_norm_w=gqw[:, :HD] + gqw[:, HD:], k_norm_w=gkw[:, :HD] + gkw[:, HD:], loss=loss_part)
    return grad_x, small, (r_win, r_pa, r_pb, r_wo)


def kernel(x, c, w_ada, b_ada, norm_w, w_in, conv_w, q_norm_w, k_norm_w, w_br_conv, w_br_attn, w_out, loss_target, m_w_ada, m_b_ada, m_norm_w, m_w_in, m_conv_w, m_q_norm_w, m_k_norm_w, m_w_br_conv, m_w_br_attn, m_w_out, v_w_ada, v_b_ada, v_norm_w, v_w_in, v_conv_w, v_q_norm_w, v_k_norm_w, v_w_br_conv, v_w_br_attn, v_w_out):
    me_xyc = (lax.axis_index("x"), lax.axis_index("y"), lax.axis_index("c"))
    me = _dev_index(me_xyc)
    ncol = w_ada.shape[2]

    conv_pad = jnp.zeros((8, 128), F32).at[0:3].set(conv_w[0])
    pa_g, pb_g, wo_g, c_all, conv_all = all_gather(
        [w_br_conv[0].astype(BF16), w_br_attn[0].astype(BF16), w_out[0].astype(BF16), c, conv_pad],
        "gather_weights")
    pa_w = pa_g.reshape(D, D)
    wo_w = wo_g.reshape(D, D)
    pb_w = pb_g.transpose(1, 0, 2).reshape(CB, D)
    conv_full = conv_all[:, 0:3].transpose(1, 0, 2).reshape(3, D)
    c_all = c_all.reshape(NDEV, D)

    b_cols = lax.dynamic_slice(b_ada, (0, me * ncol), (1, ncol))
    mod_cols = ada_fwd(c_all, w_ada[0], b_cols)
    (mod_all,) = all_gather([mod_cols], "gather_mod")
    mod = lax.dynamic_index_in_dim(mod_all, me, axis=1, keepdims=False).reshape(1, 3 * D)
    shift, scale, gate = mod[:, 0:D], mod[:, D:2 * D], mod[:, 2 * D:3 * D]

    grad_x, small, (r_win, r_pa, r_pb, r_wo) = _local_step(
        x[0], loss_target[0], shift, scale, gate, norm_w, conv_full, q_norm_w, k_norm_w,
        w_in[0].astype(BF16), pa_w, pb_w, wo_w, me_xyc)

    packed = jnp.concatenate(
        [small["dmod"], small["norm_w"], small["conv_w"].reshape(1, 3 * D), small["q_norm_w"], small["k_norm_w"],
         jnp.full((1, 128), small["loss"], F32)], axis=1)
    (packed_all,) = all_gather([packed], "gather_small")
    tot = sum_parts(packed_all)
    loss = tot[0, 7 * D + 2 * HD]
    dmod_all = packed_all[:, 0, 0:3 * D]
    g_b_ada = tot[:, 0:3 * D]
    g_norm_w = tot[:, 3 * D:4 * D]
    g_conv = lax.dynamic_slice(tot[:, 4 * D:7 * D].reshape(3, D), (0, me * 128), (3, 128))
    g_qn = tot[:, 7 * D:7 * D + HD]
    g_kn = tot[:, 7 * D + HD:7 * D + 2 * HD]
    g_w_ada = ada_bwd(c_all.T, lax.dynamic_slice(dmod_all, (0, me * ncol), (NDEV, ncol)))

    def upd(parts, w, m, v, name, rows):
        shape = w.shape
        w2, m2, v2 = (t.reshape(shape[-2:]) for t in (w, m, v))
        return [t.reshape(shape) for t in adamw(parts, w2, m2, v2, name, rows)]

    res = {
        "w_ada": upd(g_w_ada[None], w_ada, m_w_ada, v_w_ada, "adamw_w_ada", 256),
        "b_ada": upd(g_b_ada[None], b_ada, m_b_ada, v_b_ada, "adamw_b_ada", 1),
        "norm_w": upd(g_norm_w[None], norm_w, m_norm_w, v_norm_w, "adamw_norm_w", 1),
        "w_in": upd(r_win, w_in, m_w_in, v_w_in, "adamw_w_in", 128),
        "conv_w": upd(g_conv[None], conv_w, m_conv_w, v_conv_w, "adamw_conv_w", 3),
        "q_norm_w": upd(g_qn[None], q_norm_w, m_q_norm_w, v_q_norm_w, "adamw_q_norm_w", 1),
        "k_norm_w": upd(g_kn[None], k_norm_w, m_k_norm_w, v_k_norm_w, "adamw_k_norm_w", 1),
        "w_br_conv": upd(r_pa, w_br_conv, m_w_br_conv, v_w_br_conv, "adamw_w_br_conv", 128),
        "w_br_attn": upd(r_pb, w_br_attn, m_w_br_attn, v_w_br_attn, "adamw_w_br_attn", 512),
        "w_out": upd(r_wo, w_out, m_w_out, v_w_out, "adamw_w_out", 128),
    }
    names = ["w_ada", "b_ada", "norm_w", "w_in", "conv_w", "q_norm_w", "k_norm_w", "w_br_conv", "w_br_attn", "w_out"]
    return (loss, grad_x[None], *[res[n][0] for n in names], *[res[n][1] for n in names],
            *[res[n][2] for n in names], *[res[n][3] for n in names])
```

```python
import jax
import jax.numpy as jnp
from jax import lax
from jax.experimental import pallas as pl
from jax.experimental.pallas import tpu as pltpu

F32, BF16 = jnp.float32, jnp.bfloat16
D = 1024
NIN = 11264
NDEV = 8
SHARD = NIN // NDEV
HD = 64
NH = 8
QB = 128
CB = 512
CB_Q, CB_K, CB_V, CB_ZB = 8, 11, 14, 17
DILATIONS = (1, 4, 16)
EPS = 1e-6
NEG = -1e30
HALO = 16
LANES = 128
MESH = pl.DeviceIdType.MESH

ADAM_LR, ADAM_B1, ADAM_B2, ADAM_EPS, ADAM_WD, ADAM_STEP = 0.001, 0.9, 0.999, 1e-08, 0.01, 10

NT = (((1,), (1,)), ((), ()))
TN = (((0,), (0,)), ((), ()))


def _cp(sem, vmem_mb=48):
    return pltpu.CompilerParams(dimension_semantics=sem, vmem_limit_bytes=vmem_mb << 20)


def _silu(z):
    return z * jax.nn.sigmoid(z)


def _coords():
    return lax.axis_index("x"), lax.axis_index("y"), lax.axis_index("c")


def all_gather(arrs, name):
    n = len(arrs)

    def body(*refs):
        ins, outs = refs[:n], refs[n:2 * n]
        send_sems, recv_sems, local_sems = refs[2 * n:]
        x, y, c = _coords()
        me, sibling = (x, y, c), (x, y, 1 - c)
        chips = [(1 - x, y), (x, 1 - y), (1 - x, 1 - y)]

        def slot(a, dev):
            return outs[a].at[4 * dev[0] + 2 * dev[1] + dev[2]]

        def copy(a, k, block, to, src=None):
            return pltpu.make_async_remote_copy(
                src_ref=slot(a, block) if src is None else src, dst_ref=slot(a, block),
                send_sem=send_sems.at[a, k], recv_sem=recv_sems.at[a, k],
                device_id=to, device_id_type=MESH)

        mine = [pltpu.make_async_copy(ins[a], slot(a, me), local_sems.at[a]) for a in range(n)]
        for cp in mine:
            cp.start()
        first = []
        for a in range(n):
            first.append(copy(a, 0, me, sibling, src=ins[a]))
            first += [copy(a, 1 + j, me, (*chip, c), src=ins[a]) for j, chip in enumerate(chips)]
        for cp in first:
            cp.start()
        passed = []
        for j, chip in enumerate(chips):
            for a in range(n):
                copy(a, 1 + j, (*chip, c), me).wait_recv()
                fwd = copy(a, 4 + j, (*chip, c), sibling)
                fwd.start()
                passed.append(fwd)
        for a in range(n):
            copy(a, 0, sibling, me).wait_recv()
            for j, chip in enumerate(chips):
                copy(a, 4 + j, (*chip, 1 - c), me).wait_recv()
        for cp in first + passed:
            cp.wait_send()
        for cp in mine:
            cp.wait()

    any_spec = pl.BlockSpec(memory_space=pl.ANY)
    return pl.pallas_call(
        body, name=name,
        out_shape=[jax.ShapeDtypeStruct((NDEV,) + a.shape, a.dtype) for a in arrs],
        in_specs=[any_spec] * n, out_specs=[any_spec] * n,
        scratch_shapes=[pltpu.SemaphoreType.DMA((n, 7)), pltpu.SemaphoreType.DMA((n, 7)),
                        pltpu.SemaphoreType.DMA((n,))],
    )(*arrs)


FLIPS = [(fx, fy, fc) for fx in (0, 1) for fy in (0, 1) for fc in (0, 1)][1:]


def _flip(dev, f):
    return tuple(1 - v if b else v for v, b in zip(dev, f))


def _dev_index(dev):
    return 4 * dev[0] + 2 * dev[1] + dev[2]


def gather_order(me_xyc):
    x, y, c = me_xyc
    chips = [(1 - x, y), (x, 1 - y), (1 - x, 1 - y)]
    devs = [(x, y, c), (x, y, 1 - c)] + [(*ch, c) for ch in chips] + [(*ch, 1 - c) for ch in chips]
    return jnp.stack([_dev_index(d) for d in devs]).astype(jnp.int32)


def scatter_order(me_xyc):
    devs = [_flip(me_xyc, f) for f in FLIPS] + [me_xyc]
    return jnp.stack([_dev_index(d) for d in devs]).astype(jnp.int32)


def ada_fwd(c_all, w_ada, b_cols):
    def body(c_ref, w_ref, b_ref, o_ref):
        a = _silu(c_ref[...]).astype(BF16)
        o_ref[...] = jnp.dot(a, w_ref[...].astype(BF16), preferred_element_type=F32) + b_ref[...]

    return pl.pallas_call(body, name="ada_fwd",
                          out_shape=jax.ShapeDtypeStruct((NDEV, w_ada.shape[1]), F32))(c_all, w_ada, b_cols)


def ada_bwd(c_all_t, dmod_cols):
    def body(c_ref, d_ref, o_ref):
        at = _silu(c_ref[...])
        acc = at[:, 0:1] * d_ref[0:1, :]
        for b in range(1, NDEV):
            acc = acc + at[:, b:b + 1] * d_ref[b:b + 1, :]
        o_ref[...] = acc

    return pl.pallas_call(body, name="ada_bwd",
                          out_shape=jax.ShapeDtypeStruct((D, dmod_cols.shape[1]), F32))(c_all_t, dmod_cols)


def sum_parts(parts):
    def body(p_ref, o_ref):
        acc = p_ref[0]
        for b in range(1, NDEV):
            acc = acc + p_ref[b]
        o_ref[...] = acc

    return pl.pallas_call(body, name="sum_parts",
                          out_shape=jax.ShapeDtypeStruct(parts.shape[1:], F32))(parts)


def adamw(parts, w, m, v, name, rows):
    n, r, ccols = parts.shape

    def body(p_ref, w_ref, m_ref, v_ref, g_ref, d_ref, nm_ref, nv_ref):
        g = p_ref[0].astype(F32)
        for b in range(1, n):
            g = g + p_ref[b].astype(F32)
        nm = ADAM_B1 * m_ref[...] + (1.0 - ADAM_B1) * g
        nv = ADAM_B2 * v_ref[...] + (1.0 - ADAM_B2) * (g * g)
        g_ref[...] = g
        nm_ref[...] = nm
        nv_ref[...] = nv
        m_hat = nm / (1.0 - ADAM_B1 ** ADAM_STEP)
        v_hat = nv / (1.0 - ADAM_B2 ** ADAM_STEP)
        d_ref[...] = -ADAM_LR * (m_hat / (jnp.sqrt(v_hat) + ADAM_EPS) + ADAM_WD * w_ref[...])

    blk = pl.BlockSpec((rows, ccols), lambda i: (i, 0))
    out = jax.ShapeDtypeStruct((r, ccols), F32)
    return pl.pallas_call(
        body, name=name, grid=(r // rows,),
        in_specs=[pl.BlockSpec((n, rows, ccols), lambda i: (0, i, 0)), blk, blk, blk],
        out_specs=[blk] * 4, out_shape=[out] * 4, compiler_params=_cp(("parallel",)))(parts, w, m, v)


def norm_fwd(x, nw, scale, shift, tm):
    s = x.shape[0]

    def body(x_ref, nw_ref, sc_ref, sh_ref, h_ref, ht_ref):
        xf = x_ref[...]
        r = lax.rsqrt(jnp.mean(xf * xf, axis=-1, keepdims=True) + EPS)
        h = (xf * r * nw_ref[...]) * (1.0 + sc_ref[...]) + sh_ref[...]
        h_ref[...] = h.astype(BF16)
        ht_ref[...] = h.T.astype(BF16)

    vec = pl.BlockSpec((1, D), lambda i: (0, 0))
    return pl.pallas_call(
        body, name="norm_fwd", grid=(s // tm,),
        in_specs=[pl.BlockSpec((tm, D), lambda i: (i, 0)), vec, vec, vec],
        out_specs=[pl.BlockSpec((tm, D), lambda i: (i, 0)), pl.BlockSpec((D, tm), lambda i: (0, i))],
        out_shape=[jax.ShapeDtypeStruct((s, D), BF16), jax.ShapeDtypeStruct((D, s), BF16)],
        compiler_params=_cp(("parallel",)))(x, nw, scale, shift)


def norm_bwd(dh, x, dy, nw, scale, tm):
    s = x.shape[0]

    def body(dh_ref, x_ref, dy_ref, nw_ref, sc_ref, gx_ref, st_ref):
        xf, g = x_ref[...], dh_ref[...]
        r = lax.rsqrt(jnp.mean(xf * xf, axis=-1, keepdims=True) + EPS)
        xh = xf * r
        dn = g * (1.0 + sc_ref[...])
        dxh = dn * nw_ref[...]
        gx_ref[...] = dy_ref[...] + r * (dxh - xh * jnp.mean(dxh * xh, axis=-1, keepdims=True))

        @pl.when(pl.program_id(0) == 0)
        def _():
            st_ref[...] = jnp.zeros_like(st_ref)

        st_ref[0:1, :] += jnp.sum(g, axis=0, keepdims=True)
        st_ref[1:2, :] += jnp.sum(g * xh * nw_ref[...], axis=0, keepdims=True)
        st_ref[2:3, :] += jnp.sum(dn * xh, axis=0, keepdims=True)

    vec = pl.BlockSpec((1, D), lambda i: (0, 0))
    row = pl.BlockSpec((tm, D), lambda i: (i, 0))
    return pl.pallas_call(
        body, name="norm_bwd", grid=(s // tm,),
        in_specs=[row, row, row, vec, vec],
        out_specs=[row, pl.BlockSpec((8, D), lambda i: (0, 0))],
        out_shape=[jax.ShapeDtypeStruct((s, D), F32), jax.ShapeDtypeStruct((8, D), F32)],
        compiler_params=_cp(("arbitrary",)))(dh, x, dy, nw, scale)


def proj_fwd_gather(h, w_shard, order, tm):
    s = h.shape[0]
    ni = s // tm

    def body(order_ref, h_ref, w_ref, o_ref, wg_ref, wbuf, send_sems, recv_sems, local_sem, load_sem):
        jj, i = pl.program_id(0), pl.program_id(1)
        x, y, c = _coords()
        me, sibling = (x, y, c), (x, y, 1 - c)
        chips = [(1 - x, y), (x, 1 - y), (1 - x, 1 - y)]

        def slot(dev):
            return wg_ref.at[_dev_index(dev)]

        def copy(k, block, to, src=None):
            return pltpu.make_async_remote_copy(
                src_ref=slot(block) if src is None else src, dst_ref=slot(block),
                send_sem=send_sems.at[k], recv_sem=recv_sems.at[k], device_id=to, device_id_type=MESH)

        mine = pltpu.make_async_copy(w_ref, slot(me), local_sem)
        first = [copy(0, me, sibling, src=w_ref)] + [copy(1 + j, me, (*ch, c), src=w_ref) for j, ch in enumerate(chips)]
        passed = [copy(4 + j, (*ch, c), sibling) for j, ch in enumerate(chips)]
        start = i == 0

        @pl.when(start & (jj == 0))
        def _():
            mine.start()
            for cp in first:
                cp.start()
            mine.wait()

        @pl.when(start & (jj == 1))
        def _():
            copy(0, sibling, me).wait_recv()

        for j, ch in enumerate(chips):
            @pl.when(start & (jj == 2 + j))
            def _(j=j, ch=ch):
                copy(1 + j, (*ch, c), me).wait_recv()
                passed[j].start()

            @pl.when(start & (jj == 5 + j))
            def _(j=j, ch=ch):
                copy(4 + j, (*ch, 1 - c), me).wait_recv()

        @pl.when(start)
        def _():
            load = pltpu.make_async_copy(wg_ref.at[order_ref[jj]], wbuf, load_sem)
            load.start()
            load.wait()

        o_ref[...] = jnp.dot(h_ref[...], wbuf[...], preferred_element_type=F32).astype(BF16)

        @pl.when((jj == NDEV - 1) & (i == ni - 1))
        def _():
            for cp in first + passed:
                cp.wait_send()

    any_spec = pl.BlockSpec(memory_space=pl.ANY)
    return pl.pallas_call(
        body, name="proj_fwd_gather",
        grid_spec=pltpu.PrefetchScalarGridSpec(
            num_scalar_prefetch=1, grid=(NDEV, ni),
            in_specs=[pl.BlockSpec((tm, D), lambda jj, i, o: (i, 0)), any_spec],
            out_specs=[pl.BlockSpec((tm, SHARD), lambda jj, i, o: (i, o[jj])), any_spec],
            scratch_shapes=[pltpu.VMEM((D, SHARD), BF16), pltpu.SemaphoreType.DMA((7,)),
                            pltpu.SemaphoreType.DMA((7,)), pltpu.SemaphoreType.DMA, pltpu.SemaphoreType.DMA]),
        out_shape=[jax.ShapeDtypeStruct((s, NIN), BF16), jax.ShapeDtypeStruct((NDEV, D, SHARD), BF16)],
        compiler_params=_cp(("arbitrary", "arbitrary")))(order, h, w_shard)


def proj_bwd(ht, dproj, wg, smalls, order, tt):
    s = dproj.shape[0]
    nk = s // tt
    n = len(smalls)

    def body(order_ref, ht_ref, dp_ref, w_ref, *rest):
        small_in = rest[:n]
        dh_ref, gw_ref, rwin_ref = rest[n:n + 3]
        small_out = rest[n + 3:2 * n + 3]
        acc, stage, send_sems, recv_sems, local_sems, stage_sems = rest[2 * n + 3:]
        t, k = pl.program_id(0), pl.program_id(1)
        me_xyc = _coords()
        me = _dev_index(me_xyc)
        peers = [_flip(me_xyc, f) for f in FLIPS]

        def exchange(a, kf, src_arr, dst_arr):
            pid = _dev_index(peers[kf])
            mk = lambda dst: pltpu.make_async_remote_copy(
                src_ref=src_arr.at[pid], dst_ref=dst, send_sem=send_sems.at[a, kf], recv_sem=recv_sems.at[a, kf],
                device_id=peers[kf], device_id_type=MESH)
            return mk(dst_arr.at[me]), mk(dst_arr.at[pid])

        small_pairs = [exchange(1 + a, kf, small_in[a], small_out[a]) for kf in range(7) for a in range(n)]
        small_own = [pltpu.make_async_copy(small_in[a].at[me], small_out[a].at[me], local_sems.at[1 + a])
                     for a in range(n)]
        win_pairs = [exchange(0, kf, gw_ref, rwin_ref) for kf in range(7)]
        win_own = pltpu.make_async_copy(gw_ref.at[me], rwin_ref.at[me], local_sems.at[0])

        def to_hbm(jj):
            slab = me if jj == 7 else _dev_index(peers[jj])
            return pltpu.make_async_copy(stage.at[jj % 2], gw_ref.at[slab], stage_sems.at[jj % 2])

        @pl.when((t == 0) & (k == 0))
        def _():
            for cp in small_own:
                cp.start()
            for send, _ in small_pairs:
                send.start()

        @pl.when(t < NDEV)
        def _():
            p = jnp.dot(ht_ref[...], dp_ref[...], preferred_element_type=F32)

            @pl.when(k == 0)
            def _():
                acc[...] = p

            @pl.when(k > 0)
            def _():
                acc[...] += p

        for jj in range(NDEV):
            @pl.when((t == jj) & (k == nk - 1))
            def _(jj=jj):
                stage[jj % 2] = acc[...].astype(BF16)
                to_hbm(jj).start()

            @pl.when((t == jj + 1) & (k == 0))
            def _(jj=jj):
                to_hbm(jj).wait()
                if jj < 7:
                    win_pairs[jj][0].start()
                else:
                    win_own.start()

        @pl.when(t >= NDEV)
        def _():
            p = lax.dot_general(dp_ref[...], w_ref[...], NT, preferred_element_type=F32)

            @pl.when(k == 0)
            def _():
                dh_ref[...] = p

            @pl.when(k > 0)
            def _():
                dh_ref[...] += p

        @pl.when((t == 2 * NDEV - 1) & (k == nk - 1))
        def _():
            for _, recv in win_pairs + small_pairs:
                recv.wait_recv()
            for send, _ in win_pairs + small_pairs:
                send.wait_send()
            win_own.wait()
            for cp in small_own:
                cp.wait()

    any_spec = pl.BlockSpec(memory_space=pl.ANY)
    first = lambda t: t < NDEV
    outs = pl.pallas_call(
        body, name="proj_bwd",
        grid_spec=pltpu.PrefetchScalarGridSpec(
            num_scalar_prefetch=1, grid=(2 * NDEV, nk),
            in_specs=[pl.BlockSpec((D, tt), lambda t, k, o: (0, jnp.where(first(t), k, nk - 1))),
                      pl.BlockSpec((tt, SHARD), lambda t, k, o: (jnp.where(first(t), k, t - NDEV),
                                                                 jnp.where(first(t), o[jnp.minimum(t, NDEV - 1)], k))),
                      pl.BlockSpec((None, D, SHARD), lambda t, k, o: (jnp.where(first(t), 0, k), 0, 0))]
                     + [any_spec] * n,
            out_specs=[pl.BlockSpec((tt, D), lambda t, k, o: (jnp.where(first(t), 0, t - NDEV), 0))]
                      + [any_spec] * (2 + n),
            scratch_shapes=[pltpu.VMEM((D, SHARD), F32), pltpu.VMEM((2, D, SHARD), BF16),
                            pltpu.SemaphoreType.DMA((1 + n, 7)), pltpu.SemaphoreType.DMA((1 + n, 7)),
                            pltpu.SemaphoreType.DMA((1 + n,)), pltpu.SemaphoreType.DMA((2,))]),
        out_shape=[jax.ShapeDtypeStruct((s, D), F32), jax.ShapeDtypeStruct((NDEV, D, SHARD), BF16),
                   jax.ShapeDtypeStruct((NDEV, D, SHARD), BF16)]
                  + [jax.ShapeDtypeStruct(a.shape, a.dtype) for a in smalls],
        compiler_params=_cp(("arbitrary", "arbitrary"), 56))(order, ht, dproj, wg, *smalls)
    return outs[0], outs[2], outs[3:]


def matmul_tn(a, b, name, tk):
    s, m = a.shape
    n = b.shape[1]
    nk = s // tk

    def body(a_ref, b_ref, o_ref, acc_ref):
        k = pl.program_id(0)
        p = lax.dot_general(a_ref[...], b_ref[...], TN, preferred_element_type=F32)

        @pl.when(k == 0)
        def _():
            acc_ref[...] = p

        @pl.when(k > 0)
        def _():
            acc_ref[...] += p

        @pl.when(k == nk - 1)
        def _():
            o_ref[...] = acc_ref[...].astype(BF16)

    return pl.pallas_call(
        body, name=name, grid=(nk,),
        in_specs=[pl.BlockSpec((tk, m), lambda k: (k, 0)), pl.BlockSpec((tk, n), lambda k: (k, 0))],
        out_specs=pl.BlockSpec((m, n), lambda k: (0, 0)),
        out_shape=jax.ShapeDtypeStruct((m, n), BF16),
        scratch_shapes=[pltpu.VMEM((m, n), F32)],
        compiler_params=_cp(("arbitrary",)))(a, b)


def _head_matrices():
    lane = lax.broadcasted_iota(jnp.int32, (CB, CB), 0)
    col = lax.broadcasted_iota(jnp.int32, (CB, CB), 1)
    same = (lane // HD == col // HD).astype(BF16)
    lane_c = lax.broadcasted_iota(jnp.int32, (CB, LANES), 0)
    col_c = lax.broadcasted_iota(jnp.int32, (CB, LANES), 1)
    total = (lane_c // HD == col_c).astype(BF16)
    pick = (lane_c == col_c * HD).astype(BF16)
    return same, total, pick


def _dot_hilo(x, m_ref):
    hi = x.astype(BF16)
    lo = (x - hi.astype(F32)).astype(BF16)
    return (jnp.dot(hi, m_ref[...], preferred_element_type=F32)
            + jnp.dot(lo, m_ref[...], preferred_element_type=F32))


def _to_residue_major(val, buf, out_ref, dil):
    rows = out_ref.shape[1]
    for k in range(val.shape[1] // LANES):
        lanes = slice(k * LANES, (k + 1) * LANES)
        buf[k] = val[:, lanes]
        for r in range(dil):
            out_ref[r, :, lanes] = buf.at[k][pl.ds(r, rows, stride=dil), :].astype(out_ref.dtype)


def _from_residue_major(ref, buf, dil):
    if dil == 1:
        return ref[0].astype(F32)
    rows = ref.shape[1]
    for k in range(CB // LANES):
        for r in range(dil):
            buf.at[k][pl.ds(r, rows, stride=dil), :] = ref[r, :, k * LANES:(k + 1) * LANES].astype(F32)
    return jnp.concatenate([buf[k] for k in range(CB // LANES)], axis=1)


def qkv_prep(proj, qw8, kw8, same, tm):
    s = proj.shape[0]
    items = []
    for g, d in enumerate(DILATIONS):
        items += [(g, "q", CB_Q + g, d), (g, "k", CB_K + g, d)] + ([(g, "v", CB_V + g, d)] if d > 1 else [])
    n = len(items)

    def body(*refs):
        ins, (qw_ref, kw_ref, same_ref), outs, buf = refs[:n], refs[n:n + 3], refs[n + 3:2 * n + 3], refs[-1]
        for idx, (_, kind, _, dil) in enumerate(items):
            val = ins[idx][...].astype(F32)
            if kind != "v":
                r = lax.rsqrt(_dot_hilo(val * val, same_ref) * (1.0 / HD) + EPS)
                val = val * r * (qw_ref if kind == "q" else kw_ref)[...]
            if dil == 1:
                outs[idx][0] = val.astype(BF16)
            else:
                _to_residue_major(val, buf, outs[idx], dil)

    full = lambda a: pl.BlockSpec(a.shape, lambda i: (0, 0))
    outs = pl.pallas_call(
        body, name="qkv_prep", grid=(s // tm,),
        in_specs=[pl.BlockSpec((tm, CB), lambda i, cb=cb: (i, cb)) for _, _, cb, _ in items]
                 + [full(qw8), full(kw8), full(same)],
        out_specs=[pl.BlockSpec((d, tm // d, CB), lambda i: (0, i, 0)) for _, _, _, d in items],
        out_shape=[jax.ShapeDtypeStruct((d, s // d, CB), BF16) for _, _, _, d in items],
        scratch_shapes=[pltpu.VMEM((CB // LANES, tm, LANES), F32)],
        compiler_params=_cp(("parallel",)))(*([proj] * n), qw8 * (HD ** -0.5), kw8, same)
    srcs = [[None, None, (proj, CB_V + g)] for g in range(len(DILATIONS))]
    for (g, kind, _, _), o in zip(items, outs):
        srcs[g]["qkv".index(kind)] = (o.reshape(s, CB), 0)
    return srcs


def stats_prep(da, lc, dc, g, dil, tm):
    s = da.shape[0]
    rows = tm // dil

    def body(da_ref, lc_ref, dc_ref, dap_ref, lcp_ref, dcp_ref, lt_ref, dt_ref, buf):
        if dil == 1:
            dap_ref[0] = da_ref[...]
        else:
            _to_residue_major(da_ref[...].astype(F32), buf, dap_ref, dil)
        for src, dst, dst_t in ((lc_ref, lcp_ref, lt_ref), (dc_ref, dcp_ref, dt_ref)):
            buf[0] = src[...]
            for r in range(dil):
                piece = buf.at[0][pl.ds(r, rows, stride=dil), :] if dil > 1 else buf[0]
                dst[r] = piece
                dst_t[r] = piece.T[0:NH, :]

    row = lambda w: pl.BlockSpec((tm, w), lambda i: (i, 0))
    rm = lambda w: pl.BlockSpec((dil, rows, w), lambda i: (0, i, 0))
    tr = pl.BlockSpec((dil, NH, rows), lambda i: (0, 0, i))
    length = s // dil
    dap, lcp, dcp, lt, dt = pl.pallas_call(
        body, name=f"stats_prep_g{g}", grid=(s // tm,),
        in_specs=[row(CB), row(LANES), row(LANES)],
        out_specs=[rm(CB), rm(LANES), rm(LANES), tr, tr],
        out_shape=[jax.ShapeDtypeStruct((dil, length, CB), BF16)]
                  + [jax.ShapeDtypeStruct((dil, length, LANES), F32)] * 2
                  + [jax.ShapeDtypeStruct((dil, NH, length), F32)] * 2,
        scratch_shapes=[pltpu.VMEM((CB // LANES, tm, LANES), F32)],
        compiler_params=_cp(("parallel",)))(da, lc, dc)
    return (dap.reshape(s, CB), lcp.reshape(s, LANES), dcp.reshape(s, LANES),
            lt.reshape(dil * NH, length), dt.reshape(dil * NH, length))


def qkv_grads_to_dproj(dproj, proj, grads, qw8, kw8, same, tm):
    s = dproj.shape[0]
    flat = [(t.reshape(d, s // d, CB), d, kind, 3 * kind + g)
            for g, d in enumerate(DILATIONS) for kind, t in enumerate(grads[g])]
    nf = len(flat)

    def body(*refs):
        p_ref, ins = refs[1], refs[2:2 + nf]
        qw_ref, kw_ref, same_ref, o_ref, gw_ref, buf = refs[2 + nf:]
        i, j = pl.program_id(0), pl.program_id(1)

        @pl.when((i == 0) & (j == 0))
        def _():
            gw_ref[...] = jnp.zeros_like(gw_ref)

        for ref, (_, d, kind, jj) in zip(ins, flat):
            @pl.when(j == jj)
            def _(ref=ref, d=d, kind=kind):
                dn = _from_residue_major(ref, buf, d)
                if kind == 2:
                    o_ref[...] = dn.astype(BF16)
                    return
                t = p_ref[...].astype(F32)
                r = lax.rsqrt(_dot_hilo(t * t, same_ref) * (1.0 / HD) + EPS)
                xh = t * r
                gw_ref[kind:kind + 1, :] += jnp.sum(dn * xh, axis=0, keepdims=True)
                dxh = dn * (qw_ref if kind == 0 else kw_ref)[...]
                mean = _dot_hilo(dxh * xh, same_ref) * (1.0 / HD)
                o_ref[...] = (r * (dxh - xh * mean)).astype(BF16)

    full = lambda a: pl.BlockSpec(a.shape, lambda i, j: (0, 0))
    return pl.pallas_call(
        body, name="qkv_grads_to_dproj", grid=(s // tm, 9),
        in_specs=[pl.BlockSpec(memory_space=pl.ANY), pl.BlockSpec((tm, CB), lambda i, j: (i, CB_Q + j))]
                 + [pl.BlockSpec((d, tm // d, CB), lambda i, j: (0, i, 0)) for _, d, _, _ in flat]
                 + [full(qw8), full(kw8), full(same)],
        out_specs=[pl.BlockSpec((tm, CB), lambda i, j: (i, CB_Q + j)), pl.BlockSpec((8, CB), lambda i, j: (0, 0))],
        out_shape=[jax.ShapeDtypeStruct((s, NIN), BF16), jax.ShapeDtypeStruct((8, CB), F32)],
        input_output_aliases={0: 0},
        scratch_shapes=[pltpu.VMEM((CB // LANES, tm, LANES), F32)],
        compiler_params=_cp(("arbitrary", "arbitrary")))(dproj, proj, *[t for t, _, _, _ in flat], qw8, kw8, same)


def _lane_lo():
    return lax.broadcasted_iota(jnp.int32, (1, 2 * HD), 1) < HD


def _masks(other_ok):
    qi = lax.broadcasted_iota(jnp.int32, (QB, QB), 0)
    kj = lax.broadcasted_iota(jnp.int32, (QB, QB), 1)
    return (kj >= qi) & other_ok, kj <= qi


def attn_fwd(q_src, k_src, v_src, g, dil):
    s = q_src[0].shape[0]
    nb = s // dil // QB

    def body(q_ref, kp_ref, kc_ref, vp_ref, vc_ref, o_ref, l_ref):
        b = pl.program_id(1)
        lo = _lane_lo()
        m_prev, m_cur = _masks(b > 0)
        mask = jnp.concatenate([m_prev, m_cur], axis=1)
        ones = jnp.ones((2 * QB, 2 * HD), BF16)
        for i in range(NH // 2):
            sl = slice(2 * HD * i, 2 * HD * (i + 1))
            qs = q_ref[:, sl]
            ks = jnp.concatenate([kp_ref[:, sl], kc_ref[:, sl]], axis=0)
            vv = jnp.concatenate([vp_ref[:, sl], vc_ref[:, sl]], axis=0)
            outs, lses = [], []
            for hmask in (lo, ~lo):
                qh = jnp.where(hmask, qs, jnp.zeros_like(qs))
                sc = lax.dot_general(qh, ks, NT, preferred_element_type=F32)
                sc = jnp.where(mask, sc, NEG)
                mx = jnp.max(sc, axis=-1, keepdims=True)
                p = jnp.exp(sc - mx).astype(BF16)
                den = jnp.dot(p, ones, preferred_element_type=F32)
                outs.append(jnp.dot(p, vv, preferred_element_type=F32) / den)
                lses.append(mx + jnp.log(den))
            o_ref[:, sl] = jnp.where(lo, outs[0], outs[1])
            l_ref[:, sl] = jnp.where(lo, lses[0], lses[1])

    cur = lambda cb: pl.BlockSpec((QB, CB), lambda r, b: (r * nb + b, cb))
    prev = lambda cb: pl.BlockSpec((QB, CB), lambda r, b: (r * nb + jnp.maximum(b - 1, 0), cb))
    out = jax.ShapeDtypeStruct((s, CB), F32)
    return pl.pallas_call(
        body, name=f"attn_fwd_g{g}", grid=(dil, nb),
        in_specs=[cur(q_src[1]), prev(k_src[1]), cur(k_src[1]), prev(v_src[1]), cur(v_src[1])],
        out_specs=[cur(0)] * 2, out_shape=[out, out],
        compiler_params=_cp(("parallel", "parallel")))(q_src[0], k_src[0], k_src[0], v_src[0], v_src[0])


def attn_bwd_q(q_src, k_src, v_src, da, lc, dc, g, dil):
    s = q_src[0].shape[0]
    nb = s // dil // QB

    def body(q_ref, kp_ref, kc_ref, vp_ref, vc_ref, da_ref, l_ref, d_ref, dq_ref):
        b = pl.program_id(1)
        lo = _lane_lo()
        m_prev, m_cur = _masks(b > 0)
        mask = jnp.concatenate([m_prev, m_cur], axis=1)
        for i in range(NH // 2):
            sl = slice(2 * HD * i, 2 * HD * (i + 1))
            qs = q_ref[:, sl]
            ks = jnp.concatenate([kp_ref[:, sl], kc_ref[:, sl]], axis=0)
            vv = jnp.concatenate([vp_ref[:, sl], vc_ref[:, sl]], axis=0)
            da2 = da_ref[:, sl]
            dqs = jnp.zeros((QB, 2 * HD), F32)
            for h, hmask in enumerate((lo, ~lo)):
                head = 2 * i + h
                qh = jnp.where(hmask, qs, jnp.zeros_like(qs))
                sc = lax.dot_general(qh, ks, NT, preferred_element_type=F32)
                sc = jnp.where(mask, sc, NEG)
                p = jnp.exp(sc - l_ref[:, head:head + 1])
                dah = jnp.where(hmask, da2, jnp.zeros_like(da2))
                dp = lax.dot_general(dah, vv, NT, preferred_element_type=F32)
                ds = p * (dp - d_ref[:, head:head + 1])
                dq_h = jnp.dot(ds.astype(BF16), ks, preferred_element_type=F32)
                dqs = dqs + jnp.where(hmask, dq_h, 0.0)
            dq_ref[:, sl] = (dqs * (HD ** -0.5)).astype(BF16)

    cur = lambda cb, w=CB: pl.BlockSpec((QB, w), lambda r, b: (r * nb + b, cb))
    prev = lambda cb: pl.BlockSpec((QB, CB), lambda r, b: (r * nb + jnp.maximum(b - 1, 0), cb))
    return pl.pallas_call(
        body, name=f"attn_bwd_q_g{g}", grid=(dil, nb),
        in_specs=[cur(q_src[1]), prev(k_src[1]), cur(k_src[1]), prev(v_src[1]), cur(v_src[1]),
                  cur(0), cur(0, LANES), cur(0, LANES)],
        out_specs=cur(0), out_shape=jax.ShapeDtypeStruct((s, CB), BF16),
        compiler_params=_cp(("parallel", "parallel")))(
            q_src[0], k_src[0], k_src[0], v_src[0], v_src[0], da, lc, dc)


def attn_bwd_kv(q_src, k_src, v_src, da, lt, dt, g, dil):
    s = q_src[0].shape[0]
    nb = s // dil // QB

    def body(k_ref, v_ref, qc_ref, qn_ref, dac_ref, dan_ref, lc_ref, ln_ref, dc_ref, dn_ref, dk_ref, dv_ref):
        cblk = pl.program_id(1)
        lo = _lane_lo()
        kj = lax.broadcasted_iota(jnp.int32, (QB, QB), 0)
        qi = lax.broadcasted_iota(jnp.int32, (QB, QB), 1)
        mask = jnp.concatenate([kj <= qi, (kj >= qi) & (cblk < nb - 1)], axis=1)
        lrow = jnp.concatenate([lc_ref[...], ln_ref[...]], axis=1)
        drow = jnp.concatenate([dc_ref[...], dn_ref[...]], axis=1)
        for i in range(NH // 2):
            sl = slice(2 * HD * i, 2 * HD * (i + 1))
            qq = jnp.concatenate([qc_ref[:, sl], qn_ref[:, sl]], axis=0)
            da2 = jnp.concatenate([dac_ref[:, sl], dan_ref[:, sl]], axis=0)
            ks, vv = k_ref[:, sl], v_ref[:, sl]
            dks = jnp.zeros((QB, 2 * HD), F32)
            dvv = jnp.zeros((QB, 2 * HD), F32)
            for h, hmask in enumerate((lo, ~lo)):
                head = 2 * i + h
                qh = jnp.where(hmask, qq, jnp.zeros_like(qq))
                dah = jnp.where(hmask, da2, jnp.zeros_like(da2))
                sc = lax.dot_general(ks, qh, NT, preferred_element_type=F32)
                sc = jnp.where(mask, sc, NEG)
                p = jnp.exp(sc - lrow[head:head + 1, :])
                dp = lax.dot_general(vv, dah, NT, preferred_element_type=F32)
                ds = p * (dp - drow[head:head + 1, :])
                dvv = dvv + jnp.dot(p.astype(BF16), dah, preferred_element_type=F32)
                dks = dks + jnp.dot(ds.astype(BF16), qh, preferred_element_type=F32)
            dk_ref[:, sl] = dks.astype(BF16)
            dv_ref[:, sl] = dvv.astype(BF16)

    cur = lambda cb: pl.BlockSpec((QB, CB), lambda r, c: (r * nb + c, cb))
    nxt = lambda cb: pl.BlockSpec((QB, CB), lambda r, c: (r * nb + jnp.minimum(c + 1, nb - 1), cb))
    t_cur = pl.BlockSpec((NH, QB), lambda r, c: (r, c))
    t_nxt = pl.BlockSpec((NH, QB), lambda r, c: (r, jnp.minimum(c + 1, nb - 1)))
    out = jax.ShapeDtypeStruct((s, CB), BF16)
    return pl.pallas_call(
        body, name=f"attn_bwd_kv_g{g}", grid=(dil, nb),
        in_specs=[cur(k_src[1]), cur(v_src[1]), cur(q_src[1]), nxt(q_src[1]),
                  cur(0), nxt(0), t_cur, t_nxt, t_cur, t_nxt],
        out_specs=[cur(0), cur(0)], out_shape=[out, out],
        compiler_params=_cp(("parallel", "parallel")))(
            k_src[0], v_src[0], q_src[0], q_src[0], da, da, lt, lt, dt, dt)


def _conv_taps(u, u_prev, first):
    tm = u.shape[0]
    row = lax.broadcasted_iota(jnp.int32, (tm, 1), 0)
    up = jnp.where(first, 0.0, u_prev)
    u1 = jnp.where(row == 0, up[HALO - 1:HALO, :], pltpu.roll(u, 1, 0))
    u2 = jnp.where(row == 0, up[HALO - 2:HALO - 1, :],
                   jnp.where(row == 1, up[HALO - 1:HALO, :], pltpu.roll(u, 2, 0)))
    return u1, u2


def mid_fwd(proj, o_g, lse_g, conv_w, pick, tm):
    s = proj.shape[0]
    hb = tm // HALO

    def body(ba_ref, ca_ref, xa_ref, za_ref, cah_ref, xah_ref, zb_ref,
             o0, o1, o2, l0, l1, l2, w_ref, pick_ref, ya_ref, yb_ref, at_ref, lc_ref, buf_o, buf_l):
        first = pl.program_id(0) == 0
        u = ca_ref[...].astype(F32) * xa_ref[...].astype(F32)
        u1, u2 = _conv_taps(u, cah_ref[...].astype(F32) * xah_ref[...].astype(F32), first)
        conv = w_ref[0:1, :] * u2 + w_ref[1:2, :] * u1 + w_ref[2:3, :] * u
        ya_ref[...] = (ba_ref[...].astype(F32) * conv * _silu(za_ref[...].astype(F32))).astype(BF16)
        ls = [_from_residue_major(l, buf_l.at[g], d) for g, (l, d) in enumerate(zip((l0, l1, l2), DILATIONS))]
        mx = jnp.maximum(jnp.maximum(ls[0], ls[1]), ls[2])
        es = [jnp.exp(l - mx) for l in ls]
        den = es[0] + es[1] + es[2]
        num = jnp.zeros_like(den)
        for e, o, d in zip(es, (o0, o1, o2), DILATIONS):
            num = num + e * _from_residue_major(o, buf_o, d)
        attn = num / den
        at_ref[...] = attn
        lc_ref[...] = _dot_hilo(mx + jnp.log(den), pick_ref)
        yb_ref[...] = (attn * _silu(zb_ref[...].astype(F32))).astype(BF16)

    col = lambda j: pl.BlockSpec((tm, D), lambda i: (i, j))
    halo = lambda j: pl.BlockSpec((HALO, D), lambda i: (jnp.maximum(i * hb - 1, 0), j))
    loc = lambda w: pl.BlockSpec((tm, w), lambda i: (i, 0))
    rm = [pl.BlockSpec((d, tm // d, CB), lambda i: (0, i, 0)) for d in DILATIONS]
    rm_view = lambda ts: [t.reshape(d, s // d, CB) for t, d in zip(ts, DILATIONS)]
    return pl.pallas_call(
        body, name="mid_fwd", grid=(s // tm,),
        in_specs=[col(0), col(1), col(2), col(3), halo(1), halo(2),
                  pl.BlockSpec((tm, CB), lambda i: (i, CB_ZB))] + rm + rm
                 + [pl.BlockSpec((3, D), lambda i: (0, 0)), pl.BlockSpec(pick.shape, lambda i: (0, 0))],
        out_specs=[loc(D), loc(CB), loc(CB), loc(LANES)],
        out_shape=[jax.ShapeDtypeStruct((s, D), BF16), jax.ShapeDtypeStruct((s, CB), BF16),
                   jax.ShapeDtypeStruct((s, CB), F32), jax.ShapeDtypeStruct((s, LANES), F32)],
        scratch_shapes=[pltpu.VMEM((CB // LANES, tm, LANES), F32), pltpu.VMEM((3, CB // LANES, tm, LANES), F32)],
        compiler_params=_cp(("parallel",)))(
            proj, proj, proj, proj, proj, proj, proj, *rm_view(o_g), *rm_view(lse_g), conv_w, pick)


def mid_bwd(dproj, proj, dya, conv_w, tm):
    s = proj.shape[0]
    hb = tm // HALO
    nblk = s // tm
    last_h = s // HALO - 1

    def body(_, ba_ref, ca_ref, xa_ref, za_ref, cah_ref, xah_ref, ban_ref, zan_ref, dy_ref, dyn_ref, w_ref,
             o_ref, gw_ref):
        i = pl.program_id(0)
        ba, ca, xa, za = (t[...].astype(F32) for t in (ba_ref, ca_ref, xa_ref, za_ref))
        u = ca * xa
        u1, u2 = _conv_taps(u, cah_ref[...].astype(F32) * xah_ref[...].astype(F32), i == 0)
        w0, w1, w2 = w_ref[0:1, :], w_ref[1:2, :], w_ref[2:3, :]
        conv = w0 * u2 + w1 * u1 + w2 * u
        sg = jax.nn.sigmoid(za)
        sz = za * sg
        dy = dy_ref[...].astype(F32)
        dconv = dy * ba * sz
        dcn = dyn_ref[...].astype(F32) * ban_ref[...].astype(F32) * _silu(zan_ref[...].astype(F32))
        dcn = jnp.where(i == nblk - 1, 0.0, dcn)
        row = lax.broadcasted_iota(jnp.int32, (tm, 1), 0)
        d1 = jnp.where(row == tm - 1, dcn[0:1, :], pltpu.roll(dconv, tm - 1, 0))
        d2 = jnp.where(row == tm - 2, dcn[0:1, :],
                       jnp.where(row == tm - 1, dcn[1:2, :], pltpu.roll(dconv, tm - 2, 0)))
        du = w2 * dconv + w1 * d1 + w0 * d2
        o_ref[:, 0:D] = (dy * conv * sz).astype(BF16)
        o_ref[:, D:2 * D] = (du * xa).astype(BF16)
        o_ref[:, 2 * D:3 * D] = (du * ca).astype(BF16)
        o_ref[:, 3 * D:4 * D] = (dy * ba * conv * (sg * (1.0 + za * (1.0 - sg)))).astype(BF16)

        @pl.when(i == 0)
        def _():
            gw_ref[...] = jnp.zeros_like(gw_ref)

        gw_ref[0:1, :] += jnp.sum(dconv * u2, axis=0, keepdims=True)
        gw_ref[1:2, :] += jnp.sum(dconv * u1, axis=0, keepdims=True)
        gw_ref[2:3, :] += jnp.sum(dconv * u, axis=0, keepdims=True)

    col = lambda j: pl.BlockSpec((tm, D), lambda i: (i, j))
    halo_prev = lambda j: pl.BlockSpec((HALO, D), lambda i: (jnp.maximum(i * hb - 1, 0), j))
    halo_next = lambda j: pl.BlockSpec((HALO, D), lambda i: (jnp.minimum((i + 1) * hb, last_h), j))
    return pl.pallas_call(
        body, name="mid_bwd", grid=(nblk,),
        in_specs=[pl.BlockSpec(memory_space=pl.ANY), col(0), col(1), col(2), col(3),
                  halo_prev(1), halo_prev(2), halo_next(0), halo_next(3),
                  pl.BlockSpec((tm, D), lambda i: (i, 0)), halo_next(0),
                  pl.BlockSpec((3, D), lambda i: (0, 0))],
        out_specs=[pl.BlockSpec((tm, 4 * D), lambda i: (i, 0)), pl.BlockSpec((8, D), lambda i: (0, 0))],
        out_shape=[jax.ShapeDtypeStruct((s, NIN), BF16), jax.ShapeDtypeStruct((8, D), F32)],
        input_output_aliases={0: 0},
        compiler_params=_cp(("arbitrary",)))(dproj, proj, proj, proj, proj, proj, proj, proj, proj, dya, dya, conv_w)


def tail(proj, ya, yb, attn, x, target, gate, pa_w, pb_w, wo_w, total, tm):
    s = proj.shape[0]
    ni = s // tm

    def body(ya_ref, yb_ref, ga_ref, gb_ref, zb_ref, at_ref, x_ref, t_ref, gate_ref, pa_ref, pb_ref, wo_ref,
             tot_ref, dp_ref, dy_ref, dya_ref, da_ref, dc_ref, mg_ref, do_ref, dpa_ref, dpb_ref, st_ref, pieces):
        i, j = pl.program_id(0), pl.program_id(1)

        @pl.when((i == 0) & (j == 0))
        def _():
            st_ref[...] = jnp.zeros_like(st_ref)

        @pl.when(j == 0)
        def _():
            gate_v = gate_ref[...]
            pa = jnp.dot(ya_ref[...], pa_ref[...], preferred_element_type=F32)
            pb = jnp.dot(yb_ref[...], pb_ref[...], preferred_element_type=F32)
            sa = jax.nn.sigmoid(ga_ref[...].astype(F32))
            sb = jax.nn.sigmoid(gb_ref[...].astype(F32))
            merged = (sa * pa + sb * pb).astype(BF16)
            mg_ref[...] = merged
            out = jnp.dot(merged, wo_ref[...], preferred_element_type=F32)
            err = x_ref[...] + gate_v * out - t_ref[...]
            dy = err * (1.0 / D)
            dy_ref[...] = dy
            st_ref[0:1, :] += jnp.sum(dy * out, axis=0, keepdims=True)
            st_ref[1:2, :] += jnp.sum(err * err, axis=0, keepdims=True)
            dout = (gate_v * dy).astype(BF16)
            do_ref[...] = dout
            dmg = lax.dot_general(dout, wo_ref[...], NT, preferred_element_type=F32)
            dpa = (dmg * sa).astype(BF16)
            dpb = (dmg * sb).astype(BF16)
            dpa_ref[...] = dpa
            dpb_ref[...] = dpb
            dga = (dmg * pa * sa * (1.0 - sa)).astype(BF16)
            dgb = (dmg * pb * sb * (1.0 - sb)).astype(BF16)
            pieces[1] = dga[:, :CB]
            pieces[2] = dga[:, CB:]
            pieces[3] = dgb[:, :CB]
            pieces[4] = dgb[:, CB:]
            dya_ref[...] = lax.dot_general(dpa, pa_ref[...], NT, preferred_element_type=F32).astype(BF16)
            dyb = lax.dot_general(dpb, pb_ref[...], NT, preferred_element_type=F32)
            zb = zb_ref[...].astype(F32)
            sg = jax.nn.sigmoid(zb)
            attn_v = at_ref[...]
            dattn = dyb * (zb * sg)
            da_ref[...] = dattn.astype(BF16)
            pieces[0] = (dyb * attn_v * (sg * (1.0 + zb * (1.0 - sg)))).astype(BF16)
            dc_ref[...] = _dot_hilo(dattn * attn_v, tot_ref)

        dp_ref[...] = pieces[j]

    row = lambda w: pl.BlockSpec((tm, w), lambda i, j: (i, 0))
    pcol = lambda w, jb: pl.BlockSpec((tm, w), lambda i, j: (i, jb))
    full = lambda a: pl.BlockSpec(a.shape, lambda i, j: (0, 0))
    return pl.pallas_call(
        body, name="tail", grid=(ni, 5),
        in_specs=[row(D), row(CB), pcol(D, 9), pcol(D, 10), pcol(CB, CB_ZB), row(CB), row(D), row(D),
                  pl.BlockSpec((1, D), lambda i, j: (0, 0)), full(pa_w), full(pb_w), full(wo_w), full(total)],
        out_specs=[pl.BlockSpec((tm, CB), lambda i, j: (i, CB_ZB + j)),
                   row(D), row(D), row(CB), row(LANES), row(D), row(D), row(D), row(D),
                   pl.BlockSpec((8, D), lambda i, j: (0, 0))],
        out_shape=[jax.ShapeDtypeStruct((s, NIN), BF16), jax.ShapeDtypeStruct((s, D), F32),
                   jax.ShapeDtypeStruct((s, D), BF16), jax.ShapeDtypeStruct((s, CB), BF16),
                   jax.ShapeDtypeStruct((s, LANES), F32)] + [jax.ShapeDtypeStruct((s, D), BF16)] * 4
                  + [jax.ShapeDtypeStruct((8, D), F32)],
        scratch_shapes=[pltpu.VMEM((5, tm, CB), BF16)],
        compiler_params=_cp(("arbitrary", "arbitrary"), 56))(
            ya, yb, proj, proj, proj, attn, x, target, gate, pa_w, pb_w, wo_w, total)


def _local_step(x, target, shift, scale, gate, norm_w, conv_w, qw, kw, w_shard, pa_w, pb_w, wo_w, me_xyc):
    qw8, kw8 = jnp.tile(qw, (1, NH)), jnp.tile(kw, (1, NH))
    same, total, pick = _head_matrices()
    h, ht = norm_fwd(x, norm_w, scale, shift, 512)
    proj, wg = proj_fwd_gather(h, w_shard, gather_order(me_xyc), 1024)
    srcs = qkv_prep(proj, qw8, kw8, same, 512)
    o_g, lse_g = zip(*[attn_fwd(*srcs[g], g, d) for g, d in enumerate(DILATIONS)])
    ya, yb, attn, lc = mid_fwd(proj, o_g, lse_g, conv_w, pick, 512)
    dproj, dy, dya, da, dc, merged, dout, dpa, dpb, st_tail = tail(
        proj, ya, yb, attn, x, target, gate, pa_w, pb_w, wo_w, total, 256)
    g_wo = matmul_tn(merged, dout, "grad_w_out", 1024)
    g_pa = matmul_tn(ya, dpa, "grad_w_br_conv", 1024)
    g_pb = matmul_tn(yb, dpb, "grad_w_br_attn", 1024)
    dproj, st_conv = mid_bwd(dproj, proj, dya, conv_w, 512)
    grads = []
    for g, d in enumerate(DILATIONS):
        da_p, lc_p, dc_p, lt, dt = stats_prep(da, lc, dc, g, d, 2048)
        dq = attn_bwd_q(*srcs[g], da_p, lc_p, dc_p, g, d)
        dk, dv = attn_bwd_kv(*srcs[g], da_p, lt, dt, g, d)
        grads.append((dq, dk, dv))
    dproj, gw_qk = qkv_grads_to_dproj(dproj, proj, grads, qw8, kw8, same, 512)
    slabs = [g_pa.reshape(NDEV, 128, D), g_pb.reshape(CB, NDEV, 128).transpose(1, 0, 2), g_wo.reshape(NDEV, 128, D)]
    dh, r_win, (r_pa, r_pb, r_wo) = proj_bwd(ht, dproj, wg, slabs, scatter_order(me_xyc), 1024)
    grad_x, st_norm = norm_bwd(dh, x, dy, norm_w, scale, 512)
    dmod = jnp.concatenate([st_norm[0:1], st_norm[1:2], st_tail[0:1]], axis=1)
    loss_part = (0.5 / D) * jnp.sum(st_tail[1])
    gw_heads = gw_qk[0:2].reshape(2, NH, HD).sum(axis=1)
    small = dict(dmod=dmod, norm_w=st_norm[2:3], conv_w=st_conv[0:3],
                 q_norm_w=gw_heads[0:1], k_norm_w=gw_heads[1:2], loss=loss_part)
    return grad_x, small, (r_win, r_pa, r_pb, r_wo)


def kernel(x, c, w_ada, b_ada, norm_w, w_in, conv_w, q_norm_w, k_norm_w, w_br_conv, w_br_attn, w_out, loss_target, m_w_ada, m_b_ada, m_norm_w, m_w_in, m_conv_w, m_q_norm_w, m_k_norm_w, m_w_br_conv, m_w_br_attn, m_w_out, v_w_ada, v_b_ada, v_norm_w, v_w_in, v_conv_w, v_q_norm_w, v_k_norm_w, v_w_br_conv, v_w_br_attn, v_w_out):
    me_xyc = (lax.axis_index("x"), lax.axis_index("y"), lax.axis_index("c"))
    me = _dev_index(me_xyc)
    ncol = w_ada.shape[2]

    conv_pad = jnp.zeros((8, 128), F32).at[0:3].set(conv_w[0])
    pa_g, pb_g, wo_g, c_all, conv_all = all_gather(
        [w_br_conv[0].astype(BF16), w_br_attn[0].astype(BF16), w_out[0].astype(BF16), c, conv_pad],
        "gather_weights")
    pa_w = pa_g.reshape(D, D)
    wo_w = wo_g.reshape(D, D)
    pb_w = pb_g.transpose(1, 0, 2).reshape(CB, D)
    conv_full = conv_all[:, 0:3].transpose(1, 0, 2).reshape(3, D)
    c_all = c_all.reshape(NDEV, D)

    b_cols = lax.dynamic_slice(b_ada, (0, me * ncol), (1, ncol))
    mod_cols = ada_fwd(c_all, w_ada[0], b_cols)
    (mod_all,) = all_gather([mod_cols], "gather_mod")
    mod = lax.dynamic_index_in_dim(mod_all, me, axis=1, keepdims=False).reshape(1, 3 * D)
    shift, scale, gate = mod[:, 0:D], mod[:, D:2 * D], mod[:, 2 * D:3 * D]

    grad_x, small, (r_win, r_pa, r_pb, r_wo) = _local_step(
        x[0], loss_target[0], shift, scale, gate, norm_w, conv_full, q_norm_w, k_norm_w,
        w_in[0].astype(BF16), pa_w, pb_w, wo_w, me_xyc)

    packed = jnp.concatenate(
        [small["dmod"], small["norm_w"], small["conv_w"].reshape(1, 3 * D), small["q_norm_w"], small["k_norm_w"],
         jnp.full((1, 128), small["loss"], F32)], axis=1)
    (packed_all,) = all_gather([packed], "gather_small")
    tot = sum_parts(packed_all)
    loss = tot[0, 7 * D + 2 * HD]
    dmod_all = packed_all[:, 0, 0:3 * D]
    g_b_ada = tot[:, 0:3 * D]
    g_norm_w = tot[:, 3 * D:4 * D]
    g_conv = lax.dynamic_slice(tot[:, 4 * D:7 * D].reshape(3, D), (0, me * 128), (3, 128))
    g_qn = tot[:, 7 * D:7 * D + HD]
    g_kn = tot[:, 7 * D + HD:7 * D + 2 * HD]
    g_w_ada = ada_bwd(c_all.T, lax.dynamic_slice(dmod_all, (0, me * ncol), (NDEV, ncol)))

    def upd(parts, w, m, v, name, rows):
        shape = w.shape
        w2, m2, v2 = (t.reshape(shape[-2:]) for t in (w, m, v))
        return [t.reshape(shape) for t in adamw(parts, w2, m2, v2, name, rows)]

    res = {
        "w_ada": upd(g_w_ada[None], w_ada, m_w_ada, v_w_ada, "adamw_w_ada", 256),
        "b_ada": upd(g_b_ada[None], b_ada, m_b_ada, v_b_ada, "adamw_b_ada", 1),
        "norm_w": upd(g_norm_w[None], norm_w, m_norm_w, v_norm_w, "adamw_norm_w", 1),
        "w_in": upd(r_win, w_in, m_w_in, v_w_in, "adamw_w_in", 128),
        "conv_w": upd(g_conv[None], conv_w, m_conv_w, v_conv_w, "adamw_conv_w", 3),
        "q_norm_w": upd(g_qn[None], q_norm_w, m_q_norm_w, v_q_norm_w, "adamw_q_norm_w", 1),
        "k_norm_w": upd(g_kn[None], k_norm_w, m_k_norm_w, v_k_norm_w, "adamw_k_norm_w", 1),
        "w_br_conv": upd(r_pa, w_br_conv, m_w_br_conv, v_w_br_conv, "adamw_w_br_conv", 128),
        "w_br_attn": upd(r_pb, w_br_attn, m_w_br_attn, v_w_br_attn, "adamw_w_br_attn", 512),
        "w_out": upd(r_wo, w_out, m_w_out, v_w_out, "adamw_w_out", 128),
    }
    names = ["w_ada", "b_ada", "norm_w", "w_in", "conv_w", "q_norm_w", "k_norm_w", "w_br_conv", "w_br_attn", "w_out"]
    return (loss, grad_x[None], *[res[n][0] for n in names], *[res[n][1] for n in names],
            *[res[n][2] for n in names], *[res[n][3] for n in names])
```

```python
import jax
import jax.numpy as jnp
from jax import lax
from jax.experimental import pallas as pl
from jax.experimental.pallas import tpu as pltpu

F32, BF16 = jnp.float32, jnp.bfloat16
D = 1024
NIN = 11264
NDEV = 8
SHARD = NIN // NDEV
HD = 64
NH = 8
QB = 128
CB = 512
CB_Q, CB_K, CB_V, CB_ZB = 8, 11, 14, 17
DILATIONS = (1, 4, 16)
EPS = 1e-6
NEG = -1e30
HALO = 16
LANES = 128
MESH = pl.DeviceIdType.MESH

ADAM_LR, ADAM_B1, ADAM_B2, ADAM_EPS, ADAM_WD, ADAM_STEP = 0.001, 0.9, 0.999, 1e-08, 0.01, 10

NT = (((1,), (1,)), ((), ()))
TN = (((0,), (0,)), ((), ()))


def _cp(sem, vmem_mb=48):
    return pltpu.CompilerParams(dimension_semantics=sem, vmem_limit_bytes=vmem_mb << 20)


def _silu(z):
    return z * jax.nn.sigmoid(z)


def _coords():
    return lax.axis_index("x"), lax.axis_index("y"), lax.axis_index("c")


def all_gather(arrs, name):
    n = len(arrs)

    def body(*refs):
        ins, outs = refs[:n], refs[n:2 * n]
        send_sems, recv_sems, local_sems = refs[2 * n:]
        x, y, c = _coords()
        me, sibling = (x, y, c), (x, y, 1 - c)
        chips = [(1 - x, y), (x, 1 - y), (1 - x, 1 - y)]

        def slot(a, dev):
            return outs[a].at[4 * dev[0] + 2 * dev[1] + dev[2]]

        def copy(a, k, block, to, src=None):
            return pltpu.make_async_remote_copy(
                src_ref=slot(a, block) if src is None else src, dst_ref=slot(a, block),
                send_sem=send_sems.at[a, k], recv_sem=recv_sems.at[a, k],
                device_id=to, device_id_type=MESH)

        mine = [pltpu.make_async_copy(ins[a], slot(a, me), local_sems.at[a]) for a in range(n)]
        for cp in mine:
            cp.start()
        first = []
        for a in range(n):
            first.append(copy(a, 0, me, sibling, src=ins[a]))
            first += [copy(a, 1 + j, me, (*chip, c), src=ins[a]) for j, chip in enumerate(chips)]
        for cp in first:
            cp.start()
        passed = []
        for j, chip in enumerate(chips):
            for a in range(n):
                copy(a, 1 + j, (*chip, c), me).wait_recv()
                fwd = copy(a, 4 + j, (*chip, c), sibling)
                fwd.start()
                passed.append(fwd)
        for a in range(n):
            copy(a, 0, sibling, me).wait_recv()
            for j, chip in enumerate(chips):
                copy(a, 4 + j, (*chip, 1 - c), me).wait_recv()
        for cp in first + passed:
            cp.wait_send()
        for cp in mine:
            cp.wait()

    any_spec = pl.BlockSpec(memory_space=pl.ANY)
    return pl.pallas_call(
        body, name=name,
        out_shape=[jax.ShapeDtypeStruct((NDEV,) + a.shape, a.dtype) for a in arrs],
        in_specs=[any_spec] * n, out_specs=[any_spec] * n,
        scratch_shapes=[pltpu.SemaphoreType.DMA((n, 7)), pltpu.SemaphoreType.DMA((n, 7)),
                        pltpu.SemaphoreType.DMA((n,))],
    )(*arrs)


FLIPS = [(fx, fy, fc) for fx in (0, 1) for fy in (0, 1) for fc in (0, 1)][1:]


def _flip(dev, f):
    return tuple(1 - v if b else v for v, b in zip(dev, f))


def _dev_index(dev):
    return 4 * dev[0] + 2 * dev[1] + dev[2]


def gather_order(me_xyc):
    x, y, c = me_xyc
    chips = [(1 - x, y), (x, 1 - y), (1 - x, 1 - y)]
    devs = [(x, y, c), (x, y, 1 - c)] + [(*ch, c) for ch in chips] + [(*ch, 1 - c) for ch in chips]
    return jnp.stack([_dev_index(d) for d in devs]).astype(jnp.int32)


def scatter_order(me_xyc):
    devs = [_flip(me_xyc, f) for f in FLIPS] + [me_xyc]
    return jnp.stack([_dev_index(d) for d in devs]).astype(jnp.int32)


def ada_fwd(c_all, w_ada, b_cols):
    def body(c_ref, w_ref, b_ref, o_ref):
        a = _silu(c_ref[...]).astype(BF16)
        o_ref[...] = jnp.dot(a, w_ref[...].astype(BF16), preferred_element_type=F32) + b_ref[...]

    return pl.pallas_call(body, name="ada_fwd",
                          out_shape=jax.ShapeDtypeStruct((NDEV, w_ada.shape[1]), F32))(c_all, w_ada, b_cols)


def ada_bwd(c_all_t, dmod_cols):
    def body(c_ref, d_ref, o_ref):
        at = _silu(c_ref[...])
        acc = at[:, 0:1] * d_ref[0:1, :]
        for b in range(1, NDEV):
            acc = acc + at[:, b:b + 1] * d_ref[b:b + 1, :]
        o_ref[...] = acc

    return pl.pallas_call(body, name="ada_bwd",
                          out_shape=jax.ShapeDtypeStruct((D, dmod_cols.shape[1]), F32))(c_all_t, dmod_cols)


def sum_parts(parts):
    def body(p_ref, o_ref):
        acc = p_ref[0]
        for b in range(1, NDEV):
            acc = acc + p_ref[b]
        o_ref[...] = acc

    return pl.pallas_call(body, name="sum_parts",
                          out_shape=jax.ShapeDtypeStruct(parts.shape[1:], F32))(parts)


def adamw(parts, w, m, v, name, rows):
    n, r, ccols = parts.shape

    def body(p_ref, w_ref, m_ref, v_ref, g_ref, d_ref, nm_ref, nv_ref):
        g = p_ref[0].astype(F32)
        for b in range(1, n):
            g = g + p_ref[b].astype(F32)
        nm = ADAM_B1 * m_ref[...] + (1.0 - ADAM_B1) * g
        nv = ADAM_B2 * v_ref[...] + (1.0 - ADAM_B2) * (g * g)
        g_ref[...] = g
        nm_ref[...] = nm
        nv_ref[...] = nv
        m_hat = nm / (1.0 - ADAM_B1 ** ADAM_STEP)
        v_hat = nv / (1.0 - ADAM_B2 ** ADAM_STEP)
        d_ref[...] = -ADAM_LR * (m_hat / (jnp.sqrt(v_hat) + ADAM_EPS) + ADAM_WD * w_ref[...])

    blk = pl.BlockSpec((rows, ccols), lambda i: (i, 0))
    out = jax.ShapeDtypeStruct((r, ccols), F32)
    return pl.pallas_call(
        body, name=name, grid=(r // rows,),
        in_specs=[pl.BlockSpec((n, rows, ccols), lambda i: (0, i, 0)), blk, blk, blk],
        out_specs=[blk] * 4, out_shape=[out] * 4, compiler_params=_cp(("parallel",)))(parts, w, m, v)


def norm_fwd(x, nw, scale, shift, tm):
    s = x.shape[0]

    def body(x_ref, nw_ref, sc_ref, sh_ref, h_ref, ht_ref):
        xf = x_ref[...]
        r = lax.rsqrt(jnp.mean(xf * xf, axis=-1, keepdims=True) + EPS)
        h = (xf * r * nw_ref[...]) * (1.0 + sc_ref[...]) + sh_ref[...]
        h_ref[...] = h.astype(BF16)
        ht_ref[...] = h.T.astype(BF16)

    vec = pl.BlockSpec((1, D), lambda i: (0, 0))
    return pl.pallas_call(
        body, name="norm_fwd", grid=(s // tm,),
        in_specs=[pl.BlockSpec((tm, D), lambda i: (i, 0)), vec, vec, vec],
        out_specs=[pl.BlockSpec((tm, D), lambda i: (i, 0)), pl.BlockSpec((D, tm), lambda i: (0, i))],
        out_shape=[jax.ShapeDtypeStruct((s, D), BF16), jax.ShapeDtypeStruct((D, s), BF16)],
        compiler_params=_cp(("parallel",)))(x, nw, scale, shift)


def norm_bwd(dh, x, dy, nw, scale, tm):
    s = x.shape[0]

    def body(dh_ref, x_ref, dy_ref, nw_ref, sc_ref, gx_ref, st_ref):
        xf, g = x_ref[...], dh_ref[...]
        r = lax.rsqrt(jnp.mean(xf * xf, axis=-1, keepdims=True) + EPS)
        xh = xf * r
        dn = g * (1.0 + sc_ref[...])
        dxh = dn * nw_ref[...]
        gx_ref[...] = dy_ref[...] + r * (dxh - xh * jnp.mean(dxh * xh, axis=-1, keepdims=True))

        @pl.when(pl.program_id(0) == 0)
        def _():
            st_ref[...] = jnp.zeros_like(st_ref)

        st_ref[0:1, :] += jnp.sum(g, axis=0, keepdims=True)
        st_ref[1:2, :] += jnp.sum(g * xh * nw_ref[...], axis=0, keepdims=True)
        st_ref[2:3, :] += jnp.sum(dn * xh, axis=0, keepdims=True)

    vec = pl.BlockSpec((1, D), lambda i: (0, 0))
    row = pl.BlockSpec((tm, D), lambda i: (i, 0))
    return pl.pallas_call(
        body, name="norm_bwd", grid=(s // tm,),
        in_specs=[row, row, row, vec, vec],
        out_specs=[row, pl.BlockSpec((8, D), lambda i: (0, 0))],
        out_shape=[jax.ShapeDtypeStruct((s, D), F32), jax.ShapeDtypeStruct((8, D), F32)],
        compiler_params=_cp(("arbitrary",)))(dh, x, dy, nw, scale)


def proj_fwd_gather(h, w_shard, order, tm):
    s = h.shape[0]
    ni = s // tm

    def body(order_ref, h_ref, w_ref, o_ref, wg_ref, wbuf, send_sems, recv_sems, local_sem, load_sem):
        jj, i = pl.program_id(0), pl.program_id(1)
        x, y, c = _coords()
        me, sibling = (x, y, c), (x, y, 1 - c)
        chips = [(1 - x, y), (x, 1 - y), (1 - x, 1 - y)]

        def slot(dev):
            return wg_ref.at[_dev_index(dev)]

        def copy(k, block, to, src=None):
            return pltpu.make_async_remote_copy(
                src_ref=slot(block) if src is None else src, dst_ref=slot(block),
                send_sem=send_sems.at[k], recv_sem=recv_sems.at[k], device_id=to, device_id_type=MESH)

        mine = pltpu.make_async_copy(w_ref, slot(me), local_sem)
        first = [copy(0, me, sibling, src=w_ref)] + [copy(1 + j, me, (*ch, c), src=w_ref) for j, ch in enumerate(chips)]
        passed = [copy(4 + j, (*ch, c), sibling) for j, ch in enumerate(chips)]
        start = i == 0

        @pl.when(start & (jj == 0))
        def _():
            mine.start()
            for cp in first:
                cp.start()
            mine.wait()

        @pl.when(start & (jj == 1))
        def _():
            copy(0, sibling, me).wait_recv()

        for j, ch in enumerate(chips):
            @pl.when(start & (jj == 2 + j))
            def _(j=j, ch=ch):
                copy(1 + j, (*ch, c), me).wait_recv()
                passed[j].start()

            @pl.when(start & (jj == 5 + j))
            def _(j=j, ch=ch):
                copy(4 + j, (*ch, 1 - c), me).wait_recv()

        @pl.when(start)
        def _():
            load = pltpu.make_async_copy(wg_ref.at[order_ref[jj]], wbuf, load_sem)
            load.start()
            load.wait()

        o_ref[...] = jnp.dot(h_ref[...], wbuf[...], preferred_element_type=F32).astype(BF16)

        @pl.when((jj == NDEV - 1) & (i == ni - 1))
        def _():
            for cp in first + passed:
                cp.wait_send()

    any_spec = pl.BlockSpec(memory_space=pl.ANY)
    return pl.pallas_call(
        body, name="proj_fwd_gather",
        grid_spec=pltpu.PrefetchScalarGridSpec(
            num_scalar_prefetch=1, grid=(NDEV, ni),
            in_specs=[pl.BlockSpec((tm, D), lambda jj, i, o: (i, 0)), any_spec],
            out_specs=[pl.BlockSpec((tm, SHARD), lambda jj, i, o: (i, o[jj])), any_spec],
            scratch_shapes=[pltpu.VMEM((D, SHARD), BF16), pltpu.SemaphoreType.DMA((7,)),
                            pltpu.SemaphoreType.DMA((7,)), pltpu.SemaphoreType.DMA, pltpu.SemaphoreType.DMA]),
        out_shape=[jax.ShapeDtypeStruct((s, NIN), BF16), jax.ShapeDtypeStruct((NDEV, D, SHARD), BF16)],
        compiler_params=_cp(("arbitrary", "arbitrary")))(order, h, w_shard)


def proj_bwd(ht, dproj, wg, smalls, order, tt):
    s = dproj.shape[0]
    nk = s // tt
    n = len(smalls)

    def body(order_ref, ht_ref, dp_ref, w_ref, *rest):
        small_in = rest[:n]
        dh_ref, gw_ref, rwin_ref = rest[n:n + 3]
        small_out = rest[n + 3:2 * n + 3]
        acc, stage, send_sems, recv_sems, local_sems, stage_sems = rest[2 * n + 3:]
        t, k = pl.program_id(0), pl.program_id(1)
        me_xyc = _coords()
        me = _dev_index(me_xyc)
        peers = [_flip(me_xyc, f) for f in FLIPS]

        def exchange(a, kf, src_arr, dst_arr):
            pid = _dev_index(peers[kf])
            mk = lambda dst: pltpu.make_async_remote_copy(
                src_ref=src_arr.at[pid], dst_ref=dst, send_sem=send_sems.at[a, kf], recv_sem=recv_sems.at[a, kf],
                device_id=peers[kf], device_id_type=MESH)
            return mk(dst_arr.at[me]), mk(dst_arr.at[pid])

        small_pairs = [exchange(1 + a, kf, small_in[a], small_out[a]) for kf in range(7) for a in range(n)]
        small_own = [pltpu.make_async_copy(small_in[a].at[me], small_out[a].at[me], local_sems.at[1 + a])
                     for a in range(n)]
        win_pairs = [exchange(0, kf, gw_ref, rwin_ref) for kf in range(7)]
        win_own = pltpu.make_async_copy(gw_ref.at[me], rwin_ref.at[me], local_sems.at[0])

        def to_hbm(jj):
            slab = me if jj == 7 else _dev_index(peers[jj])
            return pltpu.make_async_copy(stage.at[jj % 2], gw_ref.at[slab], stage_sems.at[jj % 2])

        @pl.when((t == 0) & (k == 0))
        def _():
            for cp in small_own:
                cp.start()
            for send, _ in small_pairs:
                send.start()

        @pl.when(t < NDEV)
        def _():
            p = jnp.dot(ht_ref[...], dp_ref[...], preferred_element_type=F32)

            @pl.when(k == 0)
            def _():
                acc[...] = p

            @pl.when(k > 0)
            def _():
                acc[...] += p

        for jj in range(NDEV):
            @pl.when((t == jj) & (k == nk - 1))
            def _(jj=jj):
                stage[jj % 2] = acc[...].astype(BF16)
                to_hbm(jj).start()

            @pl.when((t == jj + 1) & (k == 0))
            def _(jj=jj):
                to_hbm(jj).wait()
                if jj < 7:
                    win_pairs[jj][0].start()
                else:
                    win_own.start()

        @pl.when(t >= NDEV)
        def _():
            p = lax.dot_general(dp_ref[...], w_ref[...], NT, preferred_element_type=F32)

            @pl.when(k == 0)
            def _():
                dh_ref[...] = p

            @pl.when(k > 0)
            def _():
                dh_ref[...] += p

        @pl.when((t == 2 * NDEV - 1) & (k == nk - 1))
        def _():
            for _, recv in win_pairs + small_pairs:
                recv.wait_recv()
            for send, _ in win_pairs + small_pairs:
                send.wait_send()
            win_own.wait()
            for cp in small_own:
                cp.wait()

    any_spec = pl.BlockSpec(memory_space=pl.ANY)
    first = lambda t: t < NDEV
    outs = pl.pallas_call(
        body, name="proj_bwd",
        grid_spec=pltpu.PrefetchScalarGridSpec(
            num_scalar_prefetch=1, grid=(2 * NDEV, nk),
            in_specs=[pl.BlockSpec((D, tt), lambda t, k, o: (0, jnp.where(first(t), k, nk - 1))),
                      pl.BlockSpec((tt, SHARD), lambda t, k, o: (jnp.where(first(t), k, t - NDEV),
                                                                 jnp.where(first(t), o[jnp.minimum(t, NDEV - 1)], k))),
                      pl.BlockSpec((None, D, SHARD), lambda t, k, o: (jnp.where(first(t), 0, k), 0, 0))]
                     + [any_spec] * n,
            out_specs=[pl.BlockSpec((tt, D), lambda t, k, o: (jnp.where(first(t), 0, t - NDEV), 0))]
                      + [any_spec] * (2 + n),
            scratch_shapes=[pltpu.VMEM((D, SHARD), F32), pltpu.VMEM((2, D, SHARD), BF16),
                            pltpu.SemaphoreType.DMA((1 + n, 7)), pltpu.SemaphoreType.DMA((1 + n, 7)),
                            pltpu.SemaphoreType.DMA((1 + n,)), pltpu.SemaphoreType.DMA((2,))]),
        out_shape=[jax.ShapeDtypeStruct((s, D), F32), jax.ShapeDtypeStruct((NDEV, D, SHARD), BF16),
                   jax.ShapeDtypeStruct((NDEV, D, SHARD), BF16)]
                  + [jax.ShapeDtypeStruct(a.shape, a.dtype) for a in smalls],
        compiler_params=_cp(("arbitrary", "arbitrary"), 56))(order, ht, dproj, wg, *smalls)
    return outs[0], outs[2], outs[3:]


def matmul_tn(a, b, name, tk):
    s, m = a.shape
    n = b.shape[1]
    nk = s // tk

    def body(a_ref, b_ref, o_ref, acc_ref):
        k = pl.program_id(0)
        p = lax.dot_general(a_ref[...], b_ref[...], TN, preferred_element_type=F32)

        @pl.when(k == 0)
        def _():
            acc_ref[...] = p

        @pl.when(k > 0)
        def _():
            acc_ref[...] += p

        @pl.when(k == nk - 1)
        def _():
            o_ref[...] = acc_ref[...].astype(BF16)

    return pl.pallas_call(
        body, name=name, grid=(nk,),
        in_specs=[pl.BlockSpec((tk, m), lambda k: (k, 0)), pl.BlockSpec((tk, n), lambda k: (k, 0))],
        out_specs=pl.BlockSpec((m, n), lambda k: (0, 0)),
        out_shape=jax.ShapeDtypeStruct((m, n), BF16),
        scratch_shapes=[pltpu.VMEM((m, n), F32)],
        compiler_params=_cp(("arbitrary",)))(a, b)


def _head_matrices():
    lane = lax.broadcasted_iota(jnp.int32, (CB, CB), 0)
    col = lax.broadcasted_iota(jnp.int32, (CB, CB), 1)
    same = (lane // HD == col // HD).astype(BF16)
    lane_c = lax.broadcasted_iota(jnp.int32, (CB, LANES), 0)
    col_c = lax.broadcasted_iota(jnp.int32, (CB, LANES), 1)
    total = (lane_c // HD == col_c).astype(BF16)
    pick = (lane_c == col_c * HD).astype(BF16)
    return same, total, pick


def _head_sum(x, m_ref):
    return jnp.dot(x.astype(BF16), m_ref[...], preferred_element_type=F32)


def _dot_hilo(x, m_ref):
    hi = x.astype(BF16)
    lo = (x - hi.astype(F32)).astype(BF16)
    return (jnp.dot(hi, m_ref[...], preferred_element_type=F32)
            + jnp.dot(lo, m_ref[...], preferred_element_type=F32))


def _to_residue_major(val, buf, out_ref, dil):
    rows = out_ref.shape[1]
    for k in range(val.shape[1] // LANES):
        lanes = slice(k * LANES, (k + 1) * LANES)
        buf[k] = val[:, lanes]
        for r in range(dil):
            out_ref[r, :, lanes] = buf.at[k][pl.ds(r, rows, stride=dil), :].astype(out_ref.dtype)


def _from_residue_major(ref, buf, dil):
    if dil == 1:
        return ref[0].astype(F32)
    rows = ref.shape[1]
    for k in range(CB // LANES):
        for r in range(dil):
            buf.at[k][pl.ds(r, rows, stride=dil), :] = ref[r, :, k * LANES:(k + 1) * LANES].astype(F32)
    return jnp.concatenate([buf[k] for k in range(CB // LANES)], axis=1)


def qkv_prep(proj, qw8, kw8, same, tm):
    s = proj.shape[0]
    items = []
    for g, d in enumerate(DILATIONS):
        items += [(g, "q", CB_Q + g, d), (g, "k", CB_K + g, d)] + ([(g, "v", CB_V + g, d)] if d > 1 else [])
    n = len(items)

    def body(*refs):
        ins, (qw_ref, kw_ref, same_ref), outs, buf = refs[:n], refs[n:n + 3], refs[n + 3:2 * n + 3], refs[-1]
        for idx, (_, kind, _, dil) in enumerate(items):
            val = ins[idx][...].astype(F32)
            if kind != "v":
                r = lax.rsqrt(_head_sum(val * val, same_ref) * (1.0 / HD) + EPS)
                val = val * r * (qw_ref if kind == "q" else kw_ref)[...]
            if dil == 1:
                outs[idx][0] = val.astype(BF16)
            else:
                _to_residue_major(val, buf, outs[idx], dil)

    full = lambda a: pl.BlockSpec(a.shape, lambda i: (0, 0))
    outs = pl.pallas_call(
        body, name="qkv_prep", grid=(s // tm,),
        in_specs=[pl.BlockSpec((tm, CB), lambda i, cb=cb: (i, cb)) for _, _, cb, _ in items]
                 + [full(qw8), full(kw8), full(same)],
        out_specs=[pl.BlockSpec((d, tm // d, CB), lambda i: (0, i, 0)) for _, _, _, d in items],
        out_shape=[jax.ShapeDtypeStruct((d, s // d, CB), BF16) for _, _, _, d in items],
        scratch_shapes=[pltpu.VMEM((CB // LANES, tm, LANES), F32)],
        compiler_params=_cp(("parallel",)))(*([proj] * n), qw8 * (HD ** -0.5), kw8, same)
    srcs = [[None, None, (proj, CB_V + g)] for g in range(len(DILATIONS))]
    for (g, kind, _, _), o in zip(items, outs):
        srcs[g]["qkv".index(kind)] = (o.reshape(s, CB), 0)
    return srcs


def stats_prep(da, lc, dc, g, dil, tm):
    s = da.shape[0]
    rows = tm // dil

    def body(da_ref, lc_ref, dc_ref, dap_ref, lcp_ref, dcp_ref, lt_ref, dt_ref, buf):
        if dil == 1:
            dap_ref[0] = da_ref[...]
        else:
            _to_residue_major(da_ref[...].astype(F32), buf, dap_ref, dil)
        for src, dst, dst_t in ((lc_ref, lcp_ref, lt_ref), (dc_ref, dcp_ref, dt_ref)):
            buf[0] = src[...]
            for r in range(dil):
                piece = buf.at[0][pl.ds(r, rows, stride=dil), :] if dil > 1 else buf[0]
                dst[r] = piece
                dst_t[r] = piece.T[0:NH, :]

    row = lambda w: pl.BlockSpec((tm, w), lambda i: (i, 0))
    rm = lambda w: pl.BlockSpec((dil, rows, w), lambda i: (0, i, 0))
    tr = pl.BlockSpec((dil, NH, rows), lambda i: (0, 0, i))
    length = s // dil
    dap, lcp, dcp, lt, dt = pl.pallas_call(
        body, name=f"stats_prep_g{g}", grid=(s // tm,),
        in_specs=[row(CB), row(LANES), row(LANES)],
        out_specs=[rm(CB), rm(LANES), rm(LANES), tr, tr],
        out_shape=[jax.ShapeDtypeStruct((dil, length, CB), BF16)]
                  + [jax.ShapeDtypeStruct((dil, length, LANES), F32)] * 2
                  + [jax.ShapeDtypeStruct((dil, NH, length), F32)] * 2,
        scratch_shapes=[pltpu.VMEM((CB // LANES, tm, LANES), F32)],
        compiler_params=_cp(("parallel",)))(da, lc, dc)
    return (dap.reshape(s, CB), lcp.reshape(s, LANES), dcp.reshape(s, LANES),
            lt.reshape(dil * NH, length), dt.reshape(dil * NH, length))


def qkv_grads_to_dproj(dproj, proj, grads, qw8, kw8, same, tm):
    s = dproj.shape[0]
    flat = [(t.reshape(d, s // d, CB), d, kind, 3 * kind + g)
            for g, d in enumerate(DILATIONS) for kind, t in enumerate(grads[g])]
    nf = len(flat)

    def body(*refs):
        p_ref, ins = refs[1], refs[2:2 + nf]
        qw_ref, kw_ref, same_ref, o_ref, gw_ref, buf = refs[2 + nf:]
        i, j = pl.program_id(0), pl.program_id(1)

        @pl.when((i == 0) & (j == 0))
        def _():
            gw_ref[...] = jnp.zeros_like(gw_ref)

        for ref, (_, d, kind, jj) in zip(ins, flat):
            @pl.when(j == jj)
            def _(ref=ref, d=d, kind=kind):
                dn = _from_residue_major(ref, buf, d)
                if kind == 2:
                    o_ref[...] = dn.astype(BF16)
                    return
                t = p_ref[...].astype(F32)
                r = lax.rsqrt(_head_sum(t * t, same_ref) * (1.0 / HD) + EPS)
                xh = t * r
                gw_ref[kind:kind + 1, :] += jnp.sum(dn * xh, axis=0, keepdims=True)
                dxh = dn * (qw_ref if kind == 0 else kw_ref)[...]
                mean = _head_sum(dxh * xh, same_ref) * (1.0 / HD)
                o_ref[...] = (r * (dxh - xh * mean)).astype(BF16)

    full = lambda a: pl.BlockSpec(a.shape, lambda i, j: (0, 0))
    return pl.pallas_call(
        body, name="qkv_grads_to_dproj", grid=(s // tm, 9),
        in_specs=[pl.BlockSpec(memory_space=pl.ANY), pl.BlockSpec((tm, CB), lambda i, j: (i, CB_Q + j))]
                 + [pl.BlockSpec((d, tm // d, CB), lambda i, j: (0, i, 0)) for _, d, _, _ in flat]
                 + [full(qw8), full(kw8), full(same)],
        out_specs=[pl.BlockSpec((tm, CB), lambda i, j: (i, CB_Q + j)), pl.BlockSpec((8, CB), lambda i, j: (0, 0))],
        out_shape=[jax.ShapeDtypeStruct((s, NIN), BF16), jax.ShapeDtypeStruct((8, CB), F32)],
        input_output_aliases={0: 0},
        scratch_shapes=[pltpu.VMEM((CB // LANES, tm, LANES), F32)],
        compiler_params=_cp(("arbitrary", "arbitrary")))(dproj, proj, *[t for t, _, _, _ in flat], qw8, kw8, same)


def _lane_lo():
    return lax.broadcasted_iota(jnp.int32, (1, 2 * HD), 1) < HD


def _masks(other_ok):
    qi = lax.broadcasted_iota(jnp.int32, (QB, QB), 0)
    kj = lax.broadcasted_iota(jnp.int32, (QB, QB), 1)
    return (kj >= qi) & other_ok, kj <= qi


def attn_fwd(q_src, k_src, v_src, g, dil):
    s = q_src[0].shape[0]
    nb = s // dil // QB

    def body(q_ref, kp_ref, kc_ref, vp_ref, vc_ref, o_ref, l_ref):
        b = pl.program_id(1)
        lo = _lane_lo()
        m_prev, m_cur = _masks(b > 0)
        mask = jnp.concatenate([m_prev, m_cur], axis=1)
        for i in range(NH // 2):
            sl = slice(2 * HD * i, 2 * HD * (i + 1))
            qs = q_ref[:, sl]
            ks = jnp.concatenate([kp_ref[:, sl], kc_ref[:, sl]], axis=0)
            vv = jnp.concatenate([vp_ref[:, sl], vc_ref[:, sl]], axis=0)
            outs, lses = [], []
            for hmask in (lo, ~lo):
                qh = jnp.where(hmask, qs, jnp.zeros_like(qs))
                sc = lax.dot_general(qh, ks, NT, preferred_element_type=F32)
                sc = jnp.where(mask, sc, NEG)
                mx = jnp.max(sc, axis=-1, keepdims=True)
                p = jnp.exp(sc - mx)
                den = jnp.sum(p, axis=-1, keepdims=True)
                outs.append(jnp.dot(p.astype(BF16), vv, preferred_element_type=F32) * (1.0 / den))
                lses.append(jnp.broadcast_to(mx + jnp.log(den), (QB, 2 * HD)))
            o_ref[:, sl] = jnp.where(lo, outs[0], outs[1])
            l_ref[:, sl] = jnp.where(lo, lses[0], lses[1])

    cur = lambda cb: pl.BlockSpec((QB, CB), lambda r, b: (r * nb + b, cb))
    prev = lambda cb: pl.BlockSpec((QB, CB), lambda r, b: (r * nb + jnp.maximum(b - 1, 0), cb))
    out = jax.ShapeDtypeStruct((s, CB), F32)
    return pl.pallas_call(
        body, name=f"attn_fwd_g{g}", grid=(dil, nb),
        in_specs=[cur(q_src[1]), prev(k_src[1]), cur(k_src[1]), prev(v_src[1]), cur(v_src[1])],
        out_specs=[cur(0)] * 2, out_shape=[out, out],
        compiler_params=_cp(("parallel", "parallel")))(q_src[0], k_src[0], k_src[0], v_src[0], v_src[0])


def attn_bwd_q(q_src, k_src, v_src, da, lc, dc, g, dil):
    s = q_src[0].shape[0]
    nb = s // dil // QB

    def body(q_ref, kp_ref, kc_ref, vp_ref, vc_ref, da_ref, l_ref, d_ref, dq_ref):
        b = pl.program_id(1)
        lo = _lane_lo()
        m_prev, m_cur = _masks(b > 0)
        mask = jnp.concatenate([m_prev, m_cur], axis=1)
        for i in range(NH // 2):
            sl = slice(2 * HD * i, 2 * HD * (i + 1))
            qs = q_ref[:, sl]
            ks = jnp.concatenate([kp_ref[:, sl], kc_ref[:, sl]], axis=0)
            vv = jnp.concatenate([vp_ref[:, sl], vc_ref[:, sl]], axis=0)
            da2 = da_ref[:, sl]
            dqs = jnp.zeros((QB, 2 * HD), F32)
            for h, hmask in enumerate((lo, ~lo)):
                head = 2 * i + h
                qh = jnp.where(hmask, qs, jnp.zeros_like(qs))
                sc = lax.dot_general(qh, ks, NT, preferred_element_type=F32)
                sc = jnp.where(mask, sc, NEG)
                p = jnp.exp(sc - l_ref[:, head:head + 1])
                dah = jnp.where(hmask, da2, jnp.zeros_like(da2))
                dp = lax.dot_general(dah, vv, NT, preferred_element_type=F32)
                ds = p * (dp - d_ref[:, head:head + 1])
                dq_h = jnp.dot(ds.astype(BF16), ks, preferred_element_type=F32)
                dqs = dqs + jnp.where(hmask, dq_h, 0.0)
            dq_ref[:, sl] = (dqs * (HD ** -0.5)).astype(BF16)

    cur = lambda cb, w=CB: pl.BlockSpec((QB, w), lambda r, b: (r * nb + b, cb))
    prev = lambda cb: pl.BlockSpec((QB, CB), lambda r, b: (r * nb + jnp.maximum(b - 1, 0), cb))
    return pl.pallas_call(
        body, name=f"attn_bwd_q_g{g}", grid=(dil, nb),
        in_specs=[cur(q_src[1]), prev(k_src[1]), cur(k_src[1]), prev(v_src[1]), cur(v_src[1]),
                  cur(0), cur(0, LANES), cur(0, LANES)],
        out_specs=cur(0), out_shape=jax.ShapeDtypeStruct((s, CB), BF16),
        compiler_params=_cp(("parallel", "parallel")))(
            q_src[0], k_src[0], k_src[0], v_src[0], v_src[0], da, lc, dc)


def attn_bwd_kv(q_src, k_src, v_src, da, lt, dt, g, dil):
    s = q_src[0].shape[0]
    nb = s // dil // QB

    def body(k_ref, v_ref, qc_ref, qn_ref, dac_ref, dan_ref, lc_ref, ln_ref, dc_ref, dn_ref, dk_ref, dv_ref):
        cblk = pl.program_id(1)
        lo = _lane_lo()
        kj = lax.broadcasted_iota(jnp.int32, (QB, QB), 0)
        qi = lax.broadcasted_iota(jnp.int32, (QB, QB), 1)
        mask = jnp.concatenate([kj <= qi, (kj >= qi) & (cblk < nb - 1)], axis=1)
        lrow = jnp.concatenate([lc_ref[...], ln_ref[...]], axis=1)
        drow = jnp.concatenate([dc_ref[...], dn_ref[...]], axis=1)
        for i in range(NH // 2):
            sl = slice(2 * HD * i, 2 * HD * (i + 1))
            qq = jnp.concatenate([qc_ref[:, sl], qn_ref[:, sl]], axis=0)
            da2 = jnp.concatenate([dac_ref[:, sl], dan_ref[:, sl]], axis=0)
            ks, vv = k_ref[:, sl], v_ref[:, sl]
            dks = jnp.zeros((QB, 2 * HD), F32)
            dvv = jnp.zeros((QB, 2 * HD), F32)
            for h, hmask in enumerate((lo, ~lo)):
                head = 2 * i + h
                qh = jnp.where(hmask, qq, jnp.zeros_like(qq))
                dah = jnp.where(hmask, da2, jnp.zeros_like(da2))
                sc = lax.dot_general(ks, qh, NT, preferred_element_type=F32)
                sc = jnp.where(mask, sc, NEG)
                p = jnp.exp(sc - lrow[head:head + 1, :])
                dp = lax.dot_general(vv, dah, NT, preferred_element_type=F32)
                ds = p * (dp - drow[head:head + 1, :])
                dvv = dvv + jnp.dot(p.astype(BF16), dah, preferred_element_type=F32)
                dks = dks + jnp.dot(ds.astype(BF16), qh, preferred_element_type=F32)
            dk_ref[:, sl] = dks.astype(BF16)
            dv_ref[:, sl] = dvv.astype(BF16)

    cur = lambda cb: pl.BlockSpec((QB, CB), lambda r, c: (r * nb + c, cb))
    nxt = lambda cb: pl.BlockSpec((QB, CB), lambda r, c: (r * nb + jnp.minimum(c + 1, nb - 1), cb))
    t_cur = pl.BlockSpec((NH, QB), lambda r, c: (r, c))
    t_nxt = pl.BlockSpec((NH, QB), lambda r, c: (r, jnp.minimum(c + 1, nb - 1)))
    out = jax.ShapeDtypeStruct((s, CB), BF16)
    return pl.pallas_call(
        body, name=f"attn_bwd_kv_g{g}", grid=(dil, nb),
        in_specs=[cur(k_src[1]), cur(v_src[1]), cur(q_src[1]), nxt(q_src[1]),
                  cur(0), nxt(0), t_cur, t_nxt, t_cur, t_nxt],
        out_specs=[cur(0), cur(0)], out_shape=[out, out],
        compiler_params=_cp(("parallel", "parallel")))(
            k_src[0], v_src[0], q_src[0], q_src[0], da, da, lt, lt, dt, dt)


def _conv_taps(u, u_prev, first):
    tm = u.shape[0]
    row = lax.broadcasted_iota(jnp.int32, (tm, 1), 0)
    up = jnp.where(first, 0.0, u_prev)
    u1 = jnp.where(row == 0, up[HALO - 1:HALO, :], pltpu.roll(u, 1, 0))
    u2 = jnp.where(row == 0, up[HALO - 2:HALO - 1, :],
                   jnp.where(row == 1, up[HALO - 1:HALO, :], pltpu.roll(u, 2, 0)))
    return u1, u2


def mid_fwd(proj, o_g, lse_g, conv_w, pick, tm):
    s = proj.shape[0]
    hb = tm // HALO

    def body(ba_ref, ca_ref, xa_ref, za_ref, cah_ref, xah_ref, zb_ref,
             o0, o1, o2, l0, l1, l2, w_ref, pick_ref, ya_ref, yb_ref, at_ref, lc_ref, buf_o, buf_l):
        first = pl.program_id(0) == 0
        u = ca_ref[...].astype(F32) * xa_ref[...].astype(F32)
        u1, u2 = _conv_taps(u, cah_ref[...].astype(F32) * xah_ref[...].astype(F32), first)
        conv = w_ref[0:1, :] * u2 + w_ref[1:2, :] * u1 + w_ref[2:3, :] * u
        ya_ref[...] = (ba_ref[...].astype(F32) * conv * _silu(za_ref[...].astype(F32))).astype(BF16)
        ls = [_from_residue_major(l, buf_l.at[g], d) for g, (l, d) in enumerate(zip((l0, l1, l2), DILATIONS))]
        mx = jnp.maximum(jnp.maximum(ls[0], ls[1]), ls[2])
        es = [jnp.exp(l - mx) for l in ls]
        den = es[0] + es[1] + es[2]
        num = jnp.zeros_like(den)
        for e, o, d in zip(es, (o0, o1, o2), DILATIONS):
            num = num + e * _from_residue_major(o, buf_o, d)
        attn = num / den
        at_ref[...] = attn
        lc_ref[...] = _dot_hilo(mx + jnp.log(den), pick_ref)
        yb_ref[...] = (attn * _silu(zb_ref[...].astype(F32))).astype(BF16)

    col = lambda j: pl.BlockSpec((tm, D), lambda i: (i, j))
    halo = lambda j: pl.BlockSpec((HALO, D), lambda i: (jnp.maximum(i * hb - 1, 0), j))
    loc = lambda w: pl.BlockSpec((tm, w), lambda i: (i, 0))
    rm = [pl.BlockSpec((d, tm // d, CB), lambda i: (0, i, 0)) for d in DILATIONS]
    rm_view = lambda ts: [t.reshape(d, s // d, CB) for t, d in zip(ts, DILATIONS)]
    return pl.pallas_call(
        body, name="mid_fwd", grid=(s // tm,),
        in_specs=[col(0), col(1), col(2), col(3), halo(1), halo(2),
                  pl.BlockSpec((tm, CB), lambda i: (i, CB_ZB))] + rm + rm
                 + [pl.BlockSpec((3, D), lambda i: (0, 0)), pl.BlockSpec(pick.shape, lambda i: (0, 0))],
        out_specs=[loc(D), loc(CB), loc(CB), loc(LANES)],
        out_shape=[jax.ShapeDtypeStruct((s, D), BF16), jax.ShapeDtypeStruct((s, CB), BF16),
                   jax.ShapeDtypeStruct((s, CB), F32), jax.ShapeDtypeStruct((s, LANES), F32)],
        scratch_shapes=[pltpu.VMEM((CB // LANES, tm, LANES), F32), pltpu.VMEM((3, CB // LANES, tm, LANES), F32)],
        compiler_params=_cp(("parallel",)))(
            proj, proj, proj, proj, proj, proj, proj, *rm_view(o_g), *rm_view(lse_g), conv_w, pick)


def mid_bwd(dproj, proj, dya, conv_w, tm):
    s = proj.shape[0]
    hb = tm // HALO
    nblk = s // tm
    last_h = s // HALO - 1

    def body(_, ba_ref, ca_ref, xa_ref, za_ref, cah_ref, xah_ref, ban_ref, zan_ref, dy_ref, dyn_ref, w_ref,
             o_ref, gw_ref):
        i = pl.program_id(0)
        ba, ca, xa, za = (t[...].astype(F32) for t in (ba_ref, ca_ref, xa_ref, za_ref))
        u = ca * xa
        u1, u2 = _conv_taps(u, cah_ref[...].astype(F32) * xah_ref[...].astype(F32), i == 0)
        w0, w1, w2 = w_ref[0:1, :], w_ref[1:2, :], w_ref[2:3, :]
        conv = w0 * u2 + w1 * u1 + w2 * u
        sg = jax.nn.sigmoid(za)
        sz = za * sg
        dy = dy_ref[...].astype(F32)
        dconv = dy * ba * sz
        dcn = dyn_ref[...].astype(F32) * ban_ref[...].astype(F32) * _silu(zan_ref[...].astype(F32))
        dcn = jnp.where(i == nblk - 1, 0.0, dcn)
        row = lax.broadcasted_iota(jnp.int32, (tm, 1), 0)
        d1 = jnp.where(row == tm - 1, dcn[0:1, :], pltpu.roll(dconv, tm - 1, 0))
        d2 = jnp.where(row == tm - 2, dcn[0:1, :],
                       jnp.where(row == tm - 1, dcn[1:2, :], pltpu.roll(dconv, tm - 2, 0)))
        du = w2 * dconv + w1 * d1 + w0 * d2
        o_ref[:, 0:D] = (dy * conv * sz).astype(BF16)
        o_ref[:, D:2 * D] = (du * xa).astype(BF16)
        o_ref[:, 2 * D:3 * D] = (du * ca).astype(BF16)
        o_ref[:, 3 * D:4 * D] = (dy * ba * conv * (sg * (1.0 + za * (1.0 - sg)))).astype(BF16)

        @pl.when(i == 0)
        def _():
            gw_ref[...] = jnp.zeros_like(gw_ref)

        gw_ref[0:1, :] += jnp.sum(dconv * u2, axis=0, keepdims=True)
        gw_ref[1:2, :] += jnp.sum(dconv * u1, axis=0, keepdims=True)
        gw_ref[2:3, :] += jnp.sum(dconv * u, axis=0, keepdims=True)

    col = lambda j: pl.BlockSpec((tm, D), lambda i: (i, j))
    halo_prev = lambda j: pl.BlockSpec((HALO, D), lambda i: (jnp.maximum(i * hb - 1, 0), j))
    halo_next = lambda j: pl.BlockSpec((HALO, D), lambda i: (jnp.minimum((i + 1) * hb, last_h), j))
    return pl.pallas_call(
        body, name="mid_bwd", grid=(nblk,),
        in_specs=[pl.BlockSpec(memory_space=pl.ANY), col(0), col(1), col(2), col(3),
                  halo_prev(1), halo_prev(2), halo_next(0), halo_next(3),
                  pl.BlockSpec((tm, D), lambda i: (i, 0)), halo_next(0),
                  pl.BlockSpec((3, D), lambda i: (0, 0))],
        out_specs=[pl.BlockSpec((tm, 4 * D), lambda i: (i, 0)), pl.BlockSpec((8, D), lambda i: (0, 0))],
        out_shape=[jax.ShapeDtypeStruct((s, NIN), BF16), jax.ShapeDtypeStruct((8, D), F32)],
        input_output_aliases={0: 0},
        compiler_params=_cp(("arbitrary",)))(dproj, proj, proj, proj, proj, proj, proj, proj, proj, dya, dya, conv_w)


def tail(proj, ya, yb, attn, x, target, gate, pa_w, pb_w, wo_w, total, tm):
    s = proj.shape[0]
    ni = s // tm

    def body(ya_ref, yb_ref, ga_ref, gb_ref, zb_ref, at_ref, x_ref, t_ref, gate_ref, pa_ref, pb_ref, wo_ref,
             tot_ref, dp_ref, dy_ref, dya_ref, da_ref, dc_ref, mg_ref, do_ref, dpa_ref, dpb_ref, st_ref, pieces):
        i, j = pl.program_id(0), pl.program_id(1)

        @pl.when((i == 0) & (j == 0))
        def _():
            st_ref[...] = jnp.zeros_like(st_ref)

        @pl.when(j == 0)
        def _():
            gate_v = gate_ref[...]
            pa = jnp.dot(ya_ref[...], pa_ref[...], preferred_element_type=F32)
            pb = jnp.dot(yb_ref[...], pb_ref[...], preferred_element_type=F32)
            sa = jax.nn.sigmoid(ga_ref[...].astype(F32))
            sb = jax.nn.sigmoid(gb_ref[...].astype(F32))
            merged = (sa * pa + sb * pb).astype(BF16)
            mg_ref[...] = merged
            out = jnp.dot(merged, wo_ref[...], preferred_element_type=F32)
            err = x_ref[...] + gate_v * out - t_ref[...]
            dy = err * (1.0 / D)
            dy_ref[...] = dy
            st_ref[0:1, :] += jnp.sum(dy * out, axis=0, keepdims=True)
            st_ref[1:2, :] += jnp.sum(err * err, axis=0, keepdims=True)
            dout = (gate_v * dy).astype(BF16)
            do_ref[...] = dout
            dmg = lax.dot_general(dout, wo_ref[...], NT, preferred_element_type=F32)
            dpa = (dmg * sa).astype(BF16)
            dpb = (dmg * sb).astype(BF16)
            dpa_ref[...] = dpa
            dpb_ref[...] = dpb
            dga = (dmg * pa * sa * (1.0 - sa)).astype(BF16)
            dgb = (dmg * pb * sb * (1.0 - sb)).astype(BF16)
            pieces[1] = dga[:, :CB]
            pieces[2] = dga[:, CB:]
            pieces[3] = dgb[:, :CB]
            pieces[4] = dgb[:, CB:]
            dya_ref[...] = lax.dot_general(dpa, pa_ref[...], NT, preferred_element_type=F32).astype(BF16)
            dyb = lax.dot_general(dpb, pb_ref[...], NT, preferred_element_type=F32)
            zb = zb_ref[...].astype(F32)
            sg = jax.nn.sigmoid(zb)
            attn_v = at_ref[...]
            dattn = dyb * (zb * sg)
            da_ref[...] = dattn.astype(BF16)
            pieces[0] = (dyb * attn_v * (sg * (1.0 + zb * (1.0 - sg)))).astype(BF16)
            dc_ref[...] = _dot_hilo(dattn * attn_v, tot_ref)

        dp_ref[...] = pieces[j]

    row = lambda w: pl.BlockSpec((tm, w), lambda i, j: (i, 0))
    pcol = lambda w, jb: pl.BlockSpec((tm, w), lambda i, j: (i, jb))
    full = lambda a: pl.BlockSpec(a.shape, lambda i, j: (0, 0))
    return pl.pallas_call(
        body, name="tail", grid=(ni, 5),
        in_specs=[row(D), row(CB), pcol(D, 9), pcol(D, 10), pcol(CB, CB_ZB), row(CB), row(D), row(D),
                  pl.BlockSpec((1, D), lambda i, j: (0, 0)), full(pa_w), full(pb_w), full(wo_w), full(total)],
        out_specs=[pl.BlockSpec((tm, CB), lambda i, j: (i, CB_ZB + j)),
                   row(D), row(D), row(CB), row(LANES), row(D), row(D), row(D), row(D),
                   pl.BlockSpec((8, D), lambda i, j: (0, 0))],
        out_shape=[jax.ShapeDtypeStruct((s, NIN), BF16), jax.ShapeDtypeStruct((s, D), F32),
                   jax.ShapeDtypeStruct((s, D), BF16), jax.ShapeDtypeStruct((s, CB), BF16),
                   jax.ShapeDtypeStruct((s, LANES), F32)] + [jax.ShapeDtypeStruct((s, D), BF16)] * 4
                  + [jax.ShapeDtypeStruct((8, D), F32)],
        scratch_shapes=[pltpu.VMEM((5, tm, CB), BF16)],
        compiler_params=_cp(("arbitrary", "arbitrary"), 56))(
            ya, yb, proj, proj, proj, attn, x, target, gate, pa_w, pb_w, wo_w, total)


def _local_step(x, target, shift, scale, gate, norm_w, conv_w, qw, kw, w_shard, pa_w, pb_w, wo_w, me_xyc):
    qw8, kw8 = jnp.tile(qw, (1, NH)), jnp.tile(kw, (1, NH))
    same, total, pick = _head_matrices()
    h, ht = norm_fwd(x, norm_w, scale, shift, 512)
    proj, wg = proj_fwd_gather(h, w_shard, gather_order(me_xyc), 1024)
    srcs = qkv_prep(proj, qw8, kw8, same, 512)
    o_g, lse_g = zip(*[attn_fwd(*srcs[g], g, d) for g, d in enumerate(DILATIONS)])
    ya, yb, attn, lc = mid_fwd(proj, o_g, lse_g, conv_w, pick, 512)
    dproj, dy, dya, da, dc, merged, dout, dpa, dpb, st_tail = tail(
        proj, ya, yb, attn, x, target, gate, pa_w, pb_w, wo_w, total, 256)
    g_wo = matmul_tn(merged, dout, "grad_w_out", 1024)
    g_pa = matmul_tn(ya, dpa, "grad_w_br_conv", 1024)
    g_pb = matmul_tn(yb, dpb, "grad_w_br_attn", 1024)
    dproj, st_conv = mid_bwd(dproj, proj, dya, conv_w, 512)
    grads = []
    for g, d in enumerate(DILATIONS):
        da_p, lc_p, dc_p, lt, dt = stats_prep(da, lc, dc, g, d, 2048)
        dq = attn_bwd_q(*srcs[g], da_p, lc_p, dc_p, g, d)
        dk, dv = attn_bwd_kv(*srcs[g], da_p, lt, dt, g, d)
        grads.append((dq, dk, dv))
    dproj, gw_qk = qkv_grads_to_dproj(dproj, proj, grads, qw8, kw8, same, 512)
    slabs = [g_pa.reshape(NDEV, 128, D), g_pb.reshape(CB, NDEV, 128).transpose(1, 0, 2), g_wo.reshape(NDEV, 128, D)]
    dh, r_win, (r_pa, r_pb, r_wo) = proj_bwd(ht, dproj, wg, slabs, scatter_order(me_xyc), 1024)
    grad_x, st_norm = norm_bwd(dh, x, dy, norm_w, scale, 512)
    dmod = jnp.concatenate([st_norm[0:1], st_norm[1:2], st_tail[0:1]], axis=1)
    loss_part = (0.5 / D) * jnp.sum(st_tail[1])
    gw_heads = gw_qk[0:2].reshape(2, NH, HD).sum(axis=1)
    small = dict(dmod=dmod, norm_w=st_norm[2:3], conv_w=st_conv[0:3],
                 q_norm_w=gw_heads[0:1], k_norm_w=gw_heads[1:2], loss=loss_part)
    return grad_x, small, (r_win, r_pa, r_pb, r_wo)


def kernel(x, c, w_ada, b_ada, norm_w, w_in, conv_w, q_norm_w, k_norm_w, w_br_conv, w_br_attn, w_out, loss_target, m_w_ada, m_b_ada, m_norm_w, m_w_in, m_conv_w, m_q_norm_w, m_k_norm_w, m_w_br_conv, m_w_br_attn, m_w_out, v_w_ada, v_b_ada, v_norm_w, v_w_in, v_conv_w, v_q_norm_w, v_k_norm_w, v_w_br_conv, v_w_br_attn, v_w_out):
    me_xyc = (lax.axis_index("x"), lax.axis_index("y"), lax.axis_index("c"))
    me = _dev_index(me_xyc)
    ncol = w_ada.shape[2]

    conv_pad = jnp.zeros((8, 128), F32).at[0:3].set(conv_w[0])
    pa_g, pb_g, wo_g, c_all, conv_all = all_gather(
        [w_br_conv[0].astype(BF16), w_br_attn[0].astype(BF16), w_out[0].astype(BF16), c, conv_pad],
        "gather_weights")
    pa_w = pa_g.reshape(D, D)
    wo_w = wo_g.reshape(D, D)
    pb_w = pb_g.transpose(1, 0, 2).reshape(CB, D)
    conv_full = conv_all[:, 0:3].transpose(1, 0, 2).reshape(3, D)
    c_all = c_all.reshape(NDEV, D)

    b_cols = lax.dynamic_slice(b_ada, (0, me * ncol), (1, ncol))
    mod_cols = ada_fwd(c_all, w_ada[0], b_cols)
    (mod_all,) = all_gather([mod_cols], "gather_mod")
    mod = lax.dynamic_index_in_dim(mod_all, me, axis=1, keepdims=False).reshape(1, 3 * D)
    shift, scale, gate = mod[:, 0:D], mod[:, D:2 * D], mod[:, 2 * D:3 * D]

    grad_x, small, (r_win, r_pa, r_pb, r_wo) = _local_step(
        x[0], loss_target[0], shift, scale, gate, norm_w, conv_full, q_norm_w, k_norm_w,
        w_in[0].astype(BF16), pa_w, pb_w, wo_w, me_xyc)

    packed = jnp.concatenate(
        [small["dmod"], small["norm_w"], small["conv_w"].reshape(1, 3 * D), small["q_norm_w"], small["k_norm_w"],
         jnp.full((1, 128), small["loss"], F32)], axis=1)
    (packed_all,) = all_gather([packed], "gather_small")
    tot = sum_parts(packed_all)
    loss = tot[0, 7 * D + 2 * HD]
    dmod_all = packed_all[:, 0, 0:3 * D]
    g_b_ada = tot[:, 0:3 * D]
    g_norm_w = tot[:, 3 * D:4 * D]
    g_conv = lax.dynamic_slice(tot[:, 4 * D:7 * D].reshape(3, D), (0, me * 128), (3, 128))
    g_qn = tot[:, 7 * D:7 * D + HD]
    g_kn = tot[:, 7 * D + HD:7 * D + 2 * HD]
    g_w_ada = ada_bwd(c_all.T, lax.dynamic_slice(dmod_all, (0, me * ncol), (NDEV, ncol)))

    def upd(parts, w, m, v, name, rows):
        shape = w.shape
        w2, m2, v2 = (t.reshape(shape[-2:]) for t in (w, m, v))
        return [t.reshape(shape) for t in adamw(parts, w2, m2, v2, name, rows)]

    res = {
        "w_ada": upd(g_w_ada[None], w_ada, m_w_ada, v_w_ada, "adamw_w_ada", 256),
        "b_ada": upd(g_b_ada[None], b_ada, m_b_ada, v_b_ada, "adamw_b_ada", 1),
        "norm_w": upd(g_norm_w[None], norm_w, m_norm_w, v_norm_w, "adamw_norm_w", 1),
        "w_in": upd(r_win, w_in, m_w_in, v_w_in, "adamw_w_in", 128),
        "conv_w": upd(g_conv[None], conv_w, m_conv_w, v_conv_w, "adamw_conv_w", 3),
        "q_norm_w": upd(g_qn[None], q_norm_w, m_q_norm_w, v_q_norm_w, "adamw_q_norm_w", 1),
        "k_norm_w": upd(g_kn[None], k_norm_w, m_k_norm_w, v_k_norm_w, "adamw_k_norm_w", 1),
        "w_br_conv": upd(r_pa, w_br_conv, m_w_br_conv, v_w_br_conv, "adamw_w_br_conv", 128),
        "w_br_attn": upd(r_pb, w_br_attn, m_w_br_attn, v_w_br_attn, "adamw_w_br_attn", 512),
        "w_out": upd(r_wo, w_out, m_w_out, v_w_out, "adamw_w_out", 128),
    }
    names = ["w_ada", "b_ada", "norm_w", "w_in", "conv_w", "q_norm_w", "k_norm_w", "w_br_conv", "w_br_attn", "w_out"]
    return (loss, grad_x[None], *[res[n][0] for n in names], *[res[n][1] for n in names],
            *[res[n][2] for n in names], *[res[n][3] for n in names])
```

```python
import jax
import jax.numpy as jnp
from jax import lax
from jax.experimental import pallas as pl
from jax.experimental.pallas import tpu as pltpu

F32, BF16 = jnp.float32, jnp.bfloat16
D = 1024
NIN = 11264
NDEV = 8
SHARD = NIN // NDEV
HD = 64
NH = 8
QB = 128
CB = 512
CB_Q, CB_K, CB_V, CB_ZB = 8, 11, 14, 17
DILATIONS = (1, 4, 16)
EPS = 1e-6
NEG = -1e30
HALO = 16
LANES = 128
MESH = pl.DeviceIdType.MESH

ADAM_LR, ADAM_B1, ADAM_B2, ADAM_EPS, ADAM_WD, ADAM_STEP = 0.001, 0.9, 0.999, 1e-08, 0.01, 10

NT = (((1,), (1,)), ((), ()))
TN = (((0,), (0,)), ((), ()))


def _cp(sem, vmem_mb=48):
    return pltpu.CompilerParams(dimension_semantics=sem, vmem_limit_bytes=vmem_mb << 20)


def _silu(z):
    return z * jax.nn.sigmoid(z)


def _coords():
    return lax.axis_index("x"), lax.axis_index("y"), lax.axis_index("c")


def all_gather(arrs, name):
    n = len(arrs)

    def body(*refs):
        ins, outs = refs[:n], refs[n:2 * n]
        send_sems, recv_sems, local_sems = refs[2 * n:]
        x, y, c = _coords()
        me, sibling = (x, y, c), (x, y, 1 - c)
        chips = [(1 - x, y), (x, 1 - y), (1 - x, 1 - y)]

        def slot(a, dev):
            return outs[a].at[4 * dev[0] + 2 * dev[1] + dev[2]]

        def copy(a, k, block, to, src=None):
            return pltpu.make_async_remote_copy(
                src_ref=slot(a, block) if src is None else src, dst_ref=slot(a, block),
                send_sem=send_sems.at[a, k], recv_sem=recv_sems.at[a, k],
                device_id=to, device_id_type=MESH)

        mine = [pltpu.make_async_copy(ins[a], slot(a, me), local_sems.at[a]) for a in range(n)]
        for cp in mine:
            cp.start()
        first = []
        for a in range(n):
            first.append(copy(a, 0, me, sibling, src=ins[a]))
            first += [copy(a, 1 + j, me, (*chip, c), src=ins[a]) for j, chip in enumerate(chips)]
        for cp in first:
            cp.start()
        passed = []
        for j, chip in enumerate(chips):
            for a in range(n):
                copy(a, 1 + j, (*chip, c), me).wait_recv()
                fwd = copy(a, 4 + j, (*chip, c), sibling)
                fwd.start()
                passed.append(fwd)
        for a in range(n):
            copy(a, 0, sibling, me).wait_recv()
            for j, chip in enumerate(chips):
                copy(a, 4 + j, (*chip, 1 - c), me).wait_recv()
        for cp in first + passed:
            cp.wait_send()
        for cp in mine:
            cp.wait()

    any_spec = pl.BlockSpec(memory_space=pl.ANY)
    return pl.pallas_call(
        body, name=name,
        out_shape=[jax.ShapeDtypeStruct((NDEV,) + a.shape, a.dtype) for a in arrs],
        in_specs=[any_spec] * n, out_specs=[any_spec] * n,
        scratch_shapes=[pltpu.SemaphoreType.DMA((n, 7)), pltpu.SemaphoreType.DMA((n, 7)),
                        pltpu.SemaphoreType.DMA((n,))],
    )(*arrs)


FLIPS = [(fx, fy, fc) for fx in (0, 1) for fy in (0, 1) for fc in (0, 1)][1:]


def _flip(dev, f):
    return tuple(1 - v if b else v for v, b in zip(dev, f))


def _dev_index(dev):
    return 4 * dev[0] + 2 * dev[1] + dev[2]


def gather_order(me_xyc):
    x, y, c = me_xyc
    chips = [(1 - x, y), (x, 1 - y), (1 - x, 1 - y)]
    devs = [(x, y, c), (x, y, 1 - c)] + [(*ch, c) for ch in chips] + [(*ch, 1 - c) for ch in chips]
    return jnp.stack([_dev_index(d) for d in devs]).astype(jnp.int32)


def scatter_order(me_xyc):
    devs = [_flip(me_xyc, f) for f in FLIPS] + [me_xyc]
    return jnp.stack([_dev_index(d) for d in devs]).astype(jnp.int32)


def ada_fwd(c_all, w_ada, b_cols):
    def body(c_ref, w_ref, b_ref, o_ref):
        a = _silu(c_ref[...]).astype(BF16)
        o_ref[...] = jnp.dot(a, w_ref[...].astype(BF16), preferred_element_type=F32) + b_ref[...]

    return pl.pallas_call(body, name="ada_fwd",
                          out_shape=jax.ShapeDtypeStruct((NDEV, w_ada.shape[1]), F32))(c_all, w_ada, b_cols)


def ada_bwd(c_all_t, dmod_cols):
    def body(c_ref, d_ref, o_ref):
        at = _silu(c_ref[...])
        acc = at[:, 0:1] * d_ref[0:1, :]
        for b in range(1, NDEV):
            acc = acc + at[:, b:b + 1] * d_ref[b:b + 1, :]
        o_ref[...] = acc

    return pl.pallas_call(body, name="ada_bwd",
                          out_shape=jax.ShapeDtypeStruct((D, dmod_cols.shape[1]), F32))(c_all_t, dmod_cols)


def sum_parts(parts):
    def body(p_ref, o_ref):
        acc = p_ref[0]
        for b in range(1, NDEV):
            acc = acc + p_ref[b]
        o_ref[...] = acc

    return pl.pallas_call(body, name="sum_parts",
                          out_shape=jax.ShapeDtypeStruct(parts.shape[1:], F32))(parts)


def adamw(parts, w, m, v, name, rows):
    n, r, ccols = parts.shape

    def body(p_ref, w_ref, m_ref, v_ref, g_ref, d_ref, nm_ref, nv_ref):
        g = p_ref[0].astype(F32)
        for b in range(1, n):
            g = g + p_ref[b].astype(F32)
        nm = ADAM_B1 * m_ref[...] + (1.0 - ADAM_B1) * g
        nv = ADAM_B2 * v_ref[...] + (1.0 - ADAM_B2) * (g * g)
        g_ref[...] = g
        nm_ref[...] = nm
        nv_ref[...] = nv
        m_hat = nm / (1.0 - ADAM_B1 ** ADAM_STEP)
        v_hat = nv / (1.0 - ADAM_B2 ** ADAM_STEP)
        d_ref[...] = -ADAM_LR * (m_hat / (jnp.sqrt(v_hat) + ADAM_EPS) + ADAM_WD * w_ref[...])

    blk = pl.BlockSpec((rows, ccols), lambda i: (i, 0))
    out = jax.ShapeDtypeStruct((r, ccols), F32)
    return pl.pallas_call(
        body, name=name, grid=(r // rows,),
        in_specs=[pl.BlockSpec((n, rows, ccols), lambda i: (0, i, 0)), blk, blk, blk],
        out_specs=[blk] * 4, out_shape=[out] * 4, compiler_params=_cp(("parallel",)))(parts, w, m, v)


def norm_fwd(x, nw, scale, shift, tm):
    s = x.shape[0]

    def body(x_ref, nw_ref, sc_ref, sh_ref, h_ref, ht_ref):
        xf = x_ref[...]
        r = lax.rsqrt(jnp.mean(xf * xf, axis=-1, keepdims=True) + EPS)
        h = (xf * r * nw_ref[...]) * (1.0 + sc_ref[...]) + sh_ref[...]
        h_ref[...] = h.astype(BF16)
        ht_ref[...] = h.T.astype(BF16)

    vec = pl.BlockSpec((1, D), lambda i: (0, 0))
    return pl.pallas_call(
        body, name="norm_fwd", grid=(s // tm,),
        in_specs=[pl.BlockSpec((tm, D), lambda i: (i, 0)), vec, vec, vec],
        out_specs=[pl.BlockSpec((tm, D), lambda i: (i, 0)), pl.BlockSpec((D, tm), lambda i: (0, i))],
        out_shape=[jax.ShapeDtypeStruct((s, D), BF16), jax.ShapeDtypeStruct((D, s), BF16)],
        compiler_params=_cp(("parallel",)))(x, nw, scale, shift)


def norm_bwd(dh, x, dy, nw, scale, tm):
    s = x.shape[0]

    def body(dh_ref, x_ref, dy_ref, nw_ref, sc_ref, gx_ref, st_ref):
        xf, g = x_ref[...], dh_ref[...]
        r = lax.rsqrt(jnp.mean(xf * xf, axis=-1, keepdims=True) + EPS)
        xh = xf * r
        dn = g * (1.0 + sc_ref[...])
        dxh = dn * nw_ref[...]
        gx_ref[...] = dy_ref[...] + r * (dxh - xh * jnp.mean(dxh * xh, axis=-1, keepdims=True))

        @pl.when(pl.program_id(0) == 0)
        def _():
            st_ref[...] = jnp.zeros_like(st_ref)

        st_ref[0:1, :] += jnp.sum(g, axis=0, keepdims=True)
        st_ref[1:2, :] += jnp.sum(g * xh * nw_ref[...], axis=0, keepdims=True)
        st_ref[2:3, :] += jnp.sum(dn * xh, axis=0, keepdims=True)

    vec = pl.BlockSpec((1, D), lambda i: (0, 0))
    row = pl.BlockSpec((tm, D), lambda i: (i, 0))
    return pl.pallas_call(
        body, name="norm_bwd", grid=(s // tm,),
        in_specs=[row, row, row, vec, vec],
        out_specs=[row, pl.BlockSpec((8, D), lambda i: (0, 0))],
        out_shape=[jax.ShapeDtypeStruct((s, D), F32), jax.ShapeDtypeStruct((8, D), F32)],
        compiler_params=_cp(("arbitrary",)))(dh, x, dy, nw, scale)


def proj_fwd_gather(h, w_shard, order, tm):
    s = h.shape[0]
    ni = s // tm

    def body(order_ref, h_ref, w_ref, o_ref, wg_ref, wbuf, send_sems, recv_sems, local_sem, load_sem):
        jj, i = pl.program_id(0), pl.program_id(1)
        x, y, c = _coords()
        me, sibling = (x, y, c), (x, y, 1 - c)
        chips = [(1 - x, y), (x, 1 - y), (1 - x, 1 - y)]

        def slot(dev):
            return wg_ref.at[_dev_index(dev)]

        def copy(k, block, to, src=None):
            return pltpu.make_async_remote_copy(
                src_ref=slot(block) if src is None else src, dst_ref=slot(block),
                send_sem=send_sems.at[k], recv_sem=recv_sems.at[k], device_id=to, device_id_type=MESH)

        mine = pltpu.make_async_copy(w_ref, slot(me), local_sem)
        first = [copy(0, me, sibling, src=w_ref)] + [copy(1 + j, me, (*ch, c), src=w_ref) for j, ch in enumerate(chips)]
        passed = [copy(4 + j, (*ch, c), sibling) for j, ch in enumerate(chips)]
        start = i == 0

        @pl.when(start & (jj == 0))
        def _():
            mine.start()
            for cp in first:
                cp.start()
            mine.wait()

        @pl.when(start & (jj == 1))
        def _():
            copy(0, sibling, me).wait_recv()

        for j, ch in enumerate(chips):
            @pl.when(start & (jj == 2 + j))
            def _(j=j, ch=ch):
                copy(1 + j, (*ch, c), me).wait_recv()
                passed[j].start()

            @pl.when(start & (jj == 5 + j))
            def _(j=j, ch=ch):
                copy(4 + j, (*ch, 1 - c), me).wait_recv()

        @pl.when(start)
        def _():
            load = pltpu.make_async_copy(wg_ref.at[order_ref[jj]], wbuf, load_sem)
            load.start()
            load.wait()

        o_ref[...] = jnp.dot(h_ref[...], wbuf[...], preferred_element_type=F32).astype(BF16)

        @pl.when((jj == NDEV - 1) & (i == ni - 1))
        def _():
            for cp in first + passed:
                cp.wait_send()

    any_spec = pl.BlockSpec(memory_space=pl.ANY)
    return pl.pallas_call(
        body, name="proj_fwd_gather",
        grid_spec=pltpu.PrefetchScalarGridSpec(
            num_scalar_prefetch=1, grid=(NDEV, ni),
            in_specs=[pl.BlockSpec((tm, D), lambda jj, i, o: (i, 0)), any_spec],
            out_specs=[pl.BlockSpec((tm, SHARD), lambda jj, i, o: (i, o[jj])), any_spec],
            scratch_shapes=[pltpu.VMEM((D, SHARD), BF16), pltpu.SemaphoreType.DMA((7,)),
                            pltpu.SemaphoreType.DMA((7,)), pltpu.SemaphoreType.DMA, pltpu.SemaphoreType.DMA]),
        out_shape=[jax.ShapeDtypeStruct((s, NIN), BF16), jax.ShapeDtypeStruct((NDEV, D, SHARD), BF16)],
        compiler_params=_cp(("arbitrary", "arbitrary")))(order, h, w_shard)


def proj_bwd(ht, dproj, wg, smalls, order, tt):
    s = dproj.shape[0]
    nk = s // tt
    n = len(smalls)

    def body(order_ref, ht_ref, dp_ref, w_ref, *rest):
        small_in = rest[:n]
        dh_ref, gw_ref, rwin_ref = rest[n:n + 3]
        small_out = rest[n + 3:2 * n + 3]
        acc, stage, send_sems, recv_sems, local_sems, stage_sems = rest[2 * n + 3:]
        t, k = pl.program_id(0), pl.program_id(1)
        me_xyc = _coords()
        me = _dev_index(me_xyc)
        peers = [_flip(me_xyc, f) for f in FLIPS]

        def exchange(a, kf, src_arr, dst_arr):
            pid = _dev_index(peers[kf])
            mk = lambda dst: pltpu.make_async_remote_copy(
                src_ref=src_arr.at[pid], dst_ref=dst, send_sem=send_sems.at[a, kf], recv_sem=recv_sems.at[a, kf],
                device_id=peers[kf], device_id_type=MESH)
            return mk(dst_arr.at[me]), mk(dst_arr.at[pid])

        small_pairs = [exchange(1 + a, kf, small_in[a], small_out[a]) for kf in range(7) for a in range(n)]
        small_own = [pltpu.make_async_copy(small_in[a].at[me], small_out[a].at[me], local_sems.at[1 + a])
                     for a in range(n)]
        win_pairs = [exchange(0, kf, gw_ref, rwin_ref) for kf in range(7)]
        win_own = pltpu.make_async_copy(gw_ref.at[me], rwin_ref.at[me], local_sems.at[0])

        def to_hbm(jj):
            slab = me if jj == 7 else _dev_index(peers[jj])
            return pltpu.make_async_copy(stage.at[jj % 2], gw_ref.at[slab], stage_sems.at[jj % 2])

        @pl.when((t == 0) & (k == 0))
        def _():
            for cp in small_own:
                cp.start()
            for send, _ in small_pairs:
                send.start()

        @pl.when(t < NDEV)
        def _():
            p = jnp.dot(ht_ref[...], dp_ref[...], preferred_element_type=F32)

            @pl.when(k == 0)
            def _():
                acc[...] = p

            @pl.when(k > 0)
            def _():
                acc[...] += p

        for jj in range(NDEV):
            @pl.when((t == jj) & (k == nk - 1))
            def _(jj=jj):
                stage[jj % 2] = acc[...].astype(BF16)
                to_hbm(jj).start()

            @pl.when((t == jj + 1) & (k == 0))
            def _(jj=jj):
                to_hbm(jj).wait()
                if jj < 7:
                    win_pairs[jj][0].start()
                else:
                    win_own.start()

        @pl.when(t >= NDEV)
        def _():
            p = lax.dot_general(dp_ref[...], w_ref[...], NT, preferred_element_type=F32)

            @pl.when(k == 0)
            def _():
                dh_ref[...] = p

            @pl.when(k > 0)
            def _():
                dh_ref[...] += p

        @pl.when((t == 2 * NDEV - 1) & (k == nk - 1))
        def _():
            for _, recv in win_pairs + small_pairs:
                recv.wait_recv()
            for send, _ in win_pairs + small_pairs:
                send.wait_send()
            win_own.wait()
            for cp in small_own:
                cp.wait()

    any_spec = pl.BlockSpec(memory_space=pl.ANY)
    first = lambda t: t < NDEV
    outs = pl.pallas_call(
        body, name="proj_bwd",
        grid_spec=pltpu.PrefetchScalarGridSpec(
            num_scalar_prefetch=1, grid=(2 * NDEV, nk),
            in_specs=[pl.BlockSpec((D, tt), lambda t, k, o: (0, jnp.where(first(t), k, nk - 1))),
                      pl.BlockSpec((tt, SHARD), lambda t, k, o: (jnp.where(first(t), k, t - NDEV),
                                                                 jnp.where(first(t), o[jnp.minimum(t, NDEV - 1)], k))),
                      pl.BlockSpec((None, D, SHARD), lambda t, k, o: (jnp.where(first(t), 0, k), 0, 0))]
                     + [any_spec] * n,
            out_specs=[pl.BlockSpec((tt, D), lambda t, k, o: (jnp.where(first(t), 0, t - NDEV), 0))]
                      + [any_spec] * (2 + n),
            scratch_shapes=[pltpu.VMEM((D, SHARD), F32), pltpu.VMEM((2, D, SHARD), BF16),
                            pltpu.SemaphoreType.DMA((1 + n, 7)), pltpu.SemaphoreType.DMA((1 + n, 7)),
                            pltpu.SemaphoreType.DMA((1 + n,)), pltpu.SemaphoreType.DMA((2,))]),
        out_shape=[jax.ShapeDtypeStruct((s, D), F32), jax.ShapeDtypeStruct((NDEV, D, SHARD), BF16),
                   jax.ShapeDtypeStruct((NDEV, D, SHARD), BF16)]
                  + [jax.ShapeDtypeStruct(a.shape, a.dtype) for a in smalls],
        compiler_params=_cp(("arbitrary", "arbitrary"), 56))(order, ht, dproj, wg, *smalls)
    return outs[0], outs[2], outs[3:]


def matmul_tn(a, b, name, tk):
    s, m = a.shape
    n = b.shape[1]
    nk = s // tk

    def body(a_ref, b_ref, o_ref, acc_ref):
        k = pl.program_id(0)
        p = lax.dot_general(a_ref[...], b_ref[...], TN, preferred_element_type=F32)

        @pl.when(k == 0)
        def _():
            acc_ref[...] = p

        @pl.when(k > 0)
        def _():
            acc_ref[...] += p

        @pl.when(k == nk - 1)
        def _():
            o_ref[...] = acc_ref[...].astype(BF16)

    return pl.pallas_call(
        body, name=name, grid=(nk,),
        in_specs=[pl.BlockSpec((tk, m), lambda k: (k, 0)), pl.BlockSpec((tk, n), lambda k: (k, 0))],
        out_specs=pl.BlockSpec((m, n), lambda k: (0, 0)),
        out_shape=jax.ShapeDtypeStruct((m, n), BF16),
        scratch_shapes=[pltpu.VMEM((m, n), F32)],
        compiler_params=_cp(("arbitrary",)))(a, b)


def _head_matrices():
    lane = lax.broadcasted_iota(jnp.int32, (CB, CB), 0)
    col = lax.broadcasted_iota(jnp.int32, (CB, CB), 1)
    same = (lane // HD == col // HD).astype(BF16)
    lane_c = lax.broadcasted_iota(jnp.int32, (CB, LANES), 0)
    col_c = lax.broadcasted_iota(jnp.int32, (CB, LANES), 1)
    total = (lane_c // HD == col_c).astype(BF16)
    pick = (lane_c == col_c * HD).astype(BF16)
    return same, total, pick


def _head_sum(x, m_ref):
    return jnp.dot(x.astype(BF16), m_ref[...], preferred_element_type=F32)


def _dot_hilo(x, m_ref):
    hi = x.astype(BF16)
    lo = (x - hi.astype(F32)).astype(BF16)
    return (jnp.dot(hi, m_ref[...], preferred_element_type=F32)
            + jnp.dot(lo, m_ref[...], preferred_element_type=F32))


def _to_residue_major(val, buf, out_ref, dil):
    rows = out_ref.shape[1]
    for k in range(val.shape[1] // LANES):
        lanes = slice(k * LANES, (k + 1) * LANES)
        buf[k] = val[:, lanes]
        for r in range(dil):
            out_ref[r, :, lanes] = buf.at[k][pl.ds(r, rows, stride=dil), :].astype(out_ref.dtype)


def _from_residue_major(ref, buf, dil):
    if dil == 1:
        return ref[0].astype(F32)
    rows = ref.shape[1]
    for k in range(CB // LANES):
        for r in range(dil):
            buf.at[k][pl.ds(r, rows, stride=dil), :] = ref[r, :, k * LANES:(k + 1) * LANES].astype(F32)
    return jnp.concatenate([buf[k] for k in range(CB // LANES)], axis=1)


def qkv_prep(proj, qw8, kw8, same, tm):
    s = proj.shape[0]
    items = []
    for g, d in enumerate(DILATIONS):
        items += [(g, "q", CB_Q + g, d), (g, "k", CB_K + g, d)] + ([(g, "v", CB_V + g, d)] if d > 1 else [])
    n = len(items)

    def body(*refs):
        ins, (qw_ref, kw_ref, same_ref), outs, buf = refs[:n], refs[n:n + 3], refs[n + 3:2 * n + 3], refs[-1]
        for idx, (_, kind, _, dil) in enumerate(items):
            val = ins[idx][...].astype(F32)
            if kind != "v":
                r = lax.rsqrt(_head_sum(val * val, same_ref) * (1.0 / HD) + EPS)
                val = val * r * (qw_ref if kind == "q" else kw_ref)[...]
            if dil == 1:
                outs[idx][0] = val.astype(BF16)
            else:
                _to_residue_major(val, buf, outs[idx], dil)

    full = lambda a: pl.BlockSpec(a.shape, lambda i: (0, 0))
    outs = pl.pallas_call(
        body, name="qkv_prep", grid=(s // tm,),
        in_specs=[pl.BlockSpec((tm, CB), lambda i, cb=cb: (i, cb)) for _, _, cb, _ in items]
                 + [full(qw8), full(kw8), full(same)],
        out_specs=[pl.BlockSpec((d, tm // d, CB), lambda i: (0, i, 0)) for _, _, _, d in items],
        out_shape=[jax.ShapeDtypeStruct((d, s // d, CB), BF16) for _, _, _, d in items],
        scratch_shapes=[pltpu.VMEM((CB // LANES, tm, LANES), F32)],
        compiler_params=_cp(("parallel",)))(*([proj] * n), qw8 * (HD ** -0.5), kw8, same)
    srcs = [[None, None, (proj, CB_V + g)] for g in range(len(DILATIONS))]
    for (g, kind, _, _), o in zip(items, outs):
        srcs[g]["qkv".index(kind)] = (o.reshape(s, CB), 0)
    return srcs


def stats_prep(da, lc, dc, g, dil, tm):
    s = da.shape[0]
    rows = tm // dil

    def body(da_ref, lc_ref, dc_ref, dap_ref, lcp_ref, dcp_ref, lt_ref, dt_ref, buf):
        if dil == 1:
            dap_ref[0] = da_ref[...]
        else:
            _to_residue_major(da_ref[...].astype(F32), buf, dap_ref, dil)
        for src, dst, dst_t in ((lc_ref, lcp_ref, lt_ref), (dc_ref, dcp_ref, dt_ref)):
            buf[0] = src[...]
            for r in range(dil):
                piece = buf.at[0][pl.ds(r, rows, stride=dil), :] if dil > 1 else buf[0]
                dst[r] = piece
                dst_t[r] = piece.T[0:NH, :]

    row = lambda w: pl.BlockSpec((tm, w), lambda i: (i, 0))
    rm = lambda w: pl.BlockSpec((dil, rows, w), lambda i: (0, i, 0))
    tr = pl.BlockSpec((dil, NH, rows), lambda i: (0, 0, i))
    length = s // dil
    dap, lcp, dcp, lt, dt = pl.pallas_call(
        body, name=f"stats_prep_g{g}", grid=(s // tm,),
        in_specs=[row(CB), row(LANES), row(LANES)],
        out_specs=[rm(CB), rm(LANES), rm(LANES), tr, tr],
        out_shape=[jax.ShapeDtypeStruct((dil, length, CB), BF16)]
                  + [jax.ShapeDtypeStruct((dil, length, LANES), F32)] * 2
                  + [jax.ShapeDtypeStruct((dil, NH, length), F32)] * 2,
        scratch_shapes=[pltpu.VMEM((CB // LANES, tm, LANES), F32)],
        compiler_params=_cp(("parallel",)))(da, lc, dc)
    return (dap.reshape(s, CB), lcp.reshape(s, LANES), dcp.reshape(s, LANES),
            lt.reshape(dil * NH, length), dt.reshape(dil * NH, length))


def qkv_grads_to_dproj(dproj, proj, grads, qw8, kw8, same, tm):
    s = dproj.shape[0]
    ni = s // tm
    flat = [(t.reshape(d, s // d, CB), d, kind, 3 * kind + g)
            for g, d in enumerate(DILATIONS) for kind, t in enumerate(grads[g])]
    nf = len(flat)
    nraw = 2 * len(DILATIONS)

    def body(*refs):
        dp_hbm, raws, ins = refs[nraw + nf + 4], refs[1:1 + nraw], refs[1 + nraw:1 + nraw + nf]
        qw_ref, kw_ref, same_ref = refs[1 + nraw + nf:4 + nraw + nf]
        gw_ref, stage, buf, sems = refs[5 + nraw + nf:]
        i = pl.program_id(0)
        slot = i % 2

        def slab(step, sl):
            return pltpu.make_async_copy(
                stage.at[sl], dp_hbm.at[pl.ds(pl.multiple_of(step * tm, tm), tm), pl.ds(CB_Q * CB, 9 * CB)],
                sems.at[sl])

        @pl.when(i == 0)
        def _():
            gw_ref[...] = jnp.zeros_like(gw_ref)

        @pl.when(i >= 2)
        def _():
            slab(i - 2, slot).wait()

        for ref, (_, d, kind, jj) in zip(ins, flat):
            cols = slice(jj * CB, (jj + 1) * CB)
            dn = _from_residue_major(ref, buf, d)
            if kind == 2:
                stage[slot, :, cols] = dn.astype(BF16)
                continue
            t = raws[jj][...].astype(F32)
            r = lax.rsqrt(_head_sum(t * t, same_ref) * (1.0 / HD) + EPS)
            xh = t * r
            gw_ref[kind:kind + 1, :] += jnp.sum(dn * xh, axis=0, keepdims=True)
            dxh = dn * (qw_ref if kind == 0 else kw_ref)[...]
            mean = _head_sum(dxh * xh, same_ref) * (1.0 / HD)
            stage[slot, :, cols] = (r * (dxh - xh * mean)).astype(BF16)
        slab(i, slot).start()

        @pl.when(i == ni - 1)
        def _():
            slab(i - 1, 1 - slot).wait()
            slab(i, slot).wait()

    full = lambda a: pl.BlockSpec(a.shape, lambda i: (0, 0))
    any_spec = pl.BlockSpec(memory_space=pl.ANY)
    return pl.pallas_call(
        body, name="qkv_grads_to_dproj", grid=(ni,),
        in_specs=[any_spec] + [pl.BlockSpec((tm, CB), lambda i, jb=jb: (i, CB_Q + jb)) for jb in range(nraw)]
                 + [pl.BlockSpec((d, tm // d, CB), lambda i: (0, i, 0)) for _, d, _, _ in flat]
                 + [full(qw8), full(kw8), full(same)],
        out_specs=[any_spec, pl.BlockSpec((8, CB), lambda i: (0, 0))],
        out_shape=[jax.ShapeDtypeStruct((s, NIN), BF16), jax.ShapeDtypeStruct((8, CB), F32)],
        input_output_aliases={0: 0},
        scratch_shapes=[pltpu.VMEM((2, tm, 9 * CB), BF16), pltpu.VMEM((CB // LANES, tm, LANES), F32),
                        pltpu.SemaphoreType.DMA((2,))],
        compiler_params=_cp(("arbitrary",)))(
            dproj, *([proj] * nraw), *[t for t, _, _, _ in flat], qw8, kw8, same)


def _lane_lo():
    return lax.broadcasted_iota(jnp.int32, (1, 2 * HD), 1) < HD


def _masks(other_ok):
    qi = lax.broadcasted_iota(jnp.int32, (QB, QB), 0)
    kj = lax.broadcasted_iota(jnp.int32, (QB, QB), 1)
    return (kj >= qi) & other_ok, kj <= qi


SUB = 4


def _attn_specs(nb, dil):
    steps = nb // SUB
    main = lambda cb, w=CB: pl.BlockSpec((SUB * QB, w), lambda r, s: (r * steps + s, cb))
    prev = lambda cb: pl.BlockSpec((QB, CB), lambda r, s: (jnp.maximum(r * nb + SUB * s - 1, 0), cb))
    nxt = lambda cb: pl.BlockSpec((QB, CB), lambda r, s: (jnp.minimum(r * nb + SUB * (s + 1), dil * nb - 1), cb))
    return main, prev, nxt


def attn_fwd(q_src, k_src, v_src, g, dil):
    s = q_src[0].shape[0]
    nb = s // dil // QB
    main, prev, _ = _attn_specs(nb, dil)

    def body(q_ref, kp_ref, k_ref, vp_ref, v_ref, o_ref, l_ref, kbuf, vbuf):
        step = pl.program_id(1)
        kbuf[0:QB], kbuf[QB:] = kp_ref[...], k_ref[...]
        vbuf[0:QB], vbuf[QB:] = vp_ref[...], v_ref[...]
        lo = _lane_lo()

        def block(j, carry):
            r0 = pl.multiple_of(j * QB, QB)
            rows, krows = pl.ds(r0, QB), pl.ds(r0, 2 * QB)
            m_prev, m_cur = _masks(step * SUB + j > 0)
            mask = jnp.concatenate([m_prev, m_cur], axis=1)
            for i in range(NH // 2):
                sl = slice(2 * HD * i, 2 * HD * (i + 1))
                qs, ks, vv = q_ref[rows, sl], kbuf[krows, sl], vbuf[krows, sl]
                outs, lses = [], []
                for hmask in (lo, ~lo):
                    qh = jnp.where(hmask, qs, jnp.zeros_like(qs))
                    sc = lax.dot_general(qh, ks, NT, preferred_element_type=F32)
                    sc = jnp.where(mask, sc, NEG)
                    mx = jnp.max(sc, axis=-1, keepdims=True)
                    p = jnp.exp(sc - mx)
                    den = jnp.sum(p, axis=-1, keepdims=True)
                    outs.append(jnp.dot(p.astype(BF16), vv, preferred_element_type=F32) * (1.0 / den))
                    lses.append(jnp.broadcast_to(mx + jnp.log(den), (QB, 2 * HD)))
                o_ref[rows, sl] = jnp.where(lo, outs[0], outs[1])
                l_ref[rows, sl] = jnp.where(lo, lses[0], lses[1])
            return carry

        lax.fori_loop(0, SUB, block, 0)

    out = jax.ShapeDtypeStruct((s, CB), F32)
    return pl.pallas_call(
        body, name=f"attn_fwd_g{g}", grid=(dil, nb // SUB),
        in_specs=[main(q_src[1]), prev(k_src[1]), main(k_src[1]), prev(v_src[1]), main(v_src[1])],
        out_specs=[main(0)] * 2, out_shape=[out, out],
        scratch_shapes=[pltpu.VMEM(((SUB + 1) * QB, CB), BF16)] * 2,
        compiler_params=_cp(("parallel", "parallel")))(q_src[0], k_src[0], k_src[0], v_src[0], v_src[0])


def attn_bwd_q(q_src, k_src, v_src, da, lc, dc, g, dil):
    s = q_src[0].shape[0]
    nb = s // dil // QB
    main, prev, _ = _attn_specs(nb, dil)

    def body(q_ref, kp_ref, k_ref, vp_ref, v_ref, da_ref, l_ref, d_ref, dq_ref, kbuf, vbuf):
        step = pl.program_id(1)
        kbuf[0:QB], kbuf[QB:] = kp_ref[...], k_ref[...]
        vbuf[0:QB], vbuf[QB:] = vp_ref[...], v_ref[...]
        lo = _lane_lo()

        def block(j, carry):
            r0 = pl.multiple_of(j * QB, QB)
            rows, krows = pl.ds(r0, QB), pl.ds(r0, 2 * QB)
            m_prev, m_cur = _masks(step * SUB + j > 0)
            mask = jnp.concatenate([m_prev, m_cur], axis=1)
            lcols, dcols = l_ref[rows, :], d_ref[rows, :]
            for i in range(NH // 2):
                sl = slice(2 * HD * i, 2 * HD * (i + 1))
                qs, ks, vv, da2 = q_ref[rows, sl], kbuf[krows, sl], vbuf[krows, sl], da_ref[rows, sl]
                dqs = jnp.zeros((QB, 2 * HD), F32)
                for h, hmask in enumerate((lo, ~lo)):
                    head = 2 * i + h
                    qh = jnp.where(hmask, qs, jnp.zeros_like(qs))
                    sc = lax.dot_general(qh, ks, NT, preferred_element_type=F32)
                    sc = jnp.where(mask, sc, NEG)
                    p = jnp.exp(sc - lcols[:, head:head + 1])
                    dah = jnp.where(hmask, da2, jnp.zeros_like(da2))
                    dp = lax.dot_general(dah, vv, NT, preferred_element_type=F32)
                    ds = p * (dp - dcols[:, head:head + 1])
                    dq_h = jnp.dot(ds.astype(BF16), ks, preferred_element_type=F32)
                    dqs = dqs + jnp.where(hmask, dq_h, 0.0)
                dq_ref[rows, sl] = (dqs * (HD ** -0.5)).astype(BF16)
            return carry

        lax.fori_loop(0, SUB, block, 0)

    return pl.pallas_call(
        body, name=f"attn_bwd_q_g{g}", grid=(dil, nb // SUB),
        in_specs=[main(q_src[1]), prev(k_src[1]), main(k_src[1]), prev(v_src[1]), main(v_src[1]),
                  main(0), main(0, LANES), main(0, LANES)],
        out_specs=main(0), out_shape=jax.ShapeDtypeStruct((s, CB), BF16),
        scratch_shapes=[pltpu.VMEM(((SUB + 1) * QB, CB), BF16)] * 2,
        compiler_params=_cp(("parallel", "parallel")))(
            q_src[0], k_src[0], k_src[0], v_src[0], v_src[0], da, lc, dc)


def attn_bwd_kv(q_src, k_src, v_src, da, lt, dt, g, dil):
    s = q_src[0].shape[0]
    nb = s // dil // QB
    main, _, nxt = _attn_specs(nb, dil)

    def body(k_ref, v_ref, q_ref, qn_ref, da_ref, dan_ref, l_ref, ln_ref, d_ref, dn_ref, dk_ref, dv_ref,
             qbuf, dabuf, lbuf, dbuf):
        step = pl.program_id(1)
        qbuf[0:SUB * QB], qbuf[SUB * QB:] = q_ref[...], qn_ref[...]
        dabuf[0:SUB * QB], dabuf[SUB * QB:] = da_ref[...], dan_ref[...]
        for c in range(SUB):
            lbuf[c], dbuf[c] = l_ref[:, c * QB:(c + 1) * QB], d_ref[:, c * QB:(c + 1) * QB]
        lbuf[SUB], dbuf[SUB] = ln_ref[...], dn_ref[...]
        lo = _lane_lo()
        kj = lax.broadcasted_iota(jnp.int32, (QB, QB), 0)
        qi = lax.broadcasted_iota(jnp.int32, (QB, QB), 1)

        def block(j, carry):
            r0 = pl.multiple_of(j * QB, QB)
            rows, qrows = pl.ds(r0, QB), pl.ds(r0, 2 * QB)
            mask = jnp.concatenate([kj <= qi, (kj >= qi) & (step * SUB + j < nb - 1)], axis=1)
            lrow = jnp.concatenate([lbuf[j], lbuf[j + 1]], axis=1)
            drow = jnp.concatenate([dbuf[j], dbuf[j + 1]], axis=1)
            for i in range(NH // 2):
                sl = slice(2 * HD * i, 2 * HD * (i + 1))
                qq, da2, ks, vv = qbuf[qrows, sl], dabuf[qrows, sl], k_ref[rows, sl], v_ref[rows, sl]
                dks = jnp.zeros((QB, 2 * HD), F32)
                dvv = jnp.zeros((QB, 2 * HD), F32)
                for h, hmask in enumerate((lo, ~lo)):
                    head = 2 * i + h
                    qh = jnp.where(hmask, qq, jnp.zeros_like(qq))
                    dah = jnp.where(hmask, da2, jnp.zeros_like(da2))
                    sc = lax.dot_general(ks, qh, NT, preferred_element_type=F32)
                    sc = jnp.where(mask, sc, NEG)
                    p = jnp.exp(sc - lrow[head:head + 1, :])
                    dp = lax.dot_general(vv, dah, NT, preferred_element_type=F32)
                    ds = p * (dp - drow[head:head + 1, :])
                    dvv = dvv + jnp.dot(p.astype(BF16), dah, preferred_element_type=F32)
                    dks = dks + jnp.dot(ds.astype(BF16), qh, preferred_element_type=F32)
                dk_ref[rows, sl] = dks.astype(BF16)
                dv_ref[rows, sl] = dvv.astype(BF16)
            return carry

        lax.fori_loop(0, SUB, block, 0)

    steps = nb // SUB
    t_main = pl.BlockSpec((NH, SUB * QB), lambda r, s: (r, s))
    t_nxt = pl.BlockSpec((NH, QB), lambda r, s: (r, jnp.minimum(SUB * (s + 1), nb - 1)))
    out = jax.ShapeDtypeStruct((s, CB), BF16)
    return pl.pallas_call(
        body, name=f"attn_bwd_kv_g{g}", grid=(dil, steps),
        in_specs=[main(k_src[1]), main(v_src[1]), main(q_src[1]), nxt(q_src[1]),
                  main(0), nxt(0), t_main, t_nxt, t_main, t_nxt],
        out_specs=[main(0), main(0)], out_shape=[out, out],
        scratch_shapes=[pltpu.VMEM(((SUB + 1) * QB, CB), BF16)] * 2 + [pltpu.VMEM((SUB + 1, NH, QB), F32)] * 2,
        compiler_params=_cp(("parallel", "parallel")))(
            k_src[0], v_src[0], q_src[0], q_src[0], da, da, lt, lt, dt, dt)


def _conv_taps(u, u_prev, first):
    tm = u.shape[0]
    row = lax.broadcasted_iota(jnp.int32, (tm, 1), 0)
    up = jnp.where(first, 0.0, u_prev)
    u1 = jnp.where(row == 0, up[HALO - 1:HALO, :], pltpu.roll(u, 1, 0))
    u2 = jnp.where(row == 0, up[HALO - 2:HALO - 1, :],
                   jnp.where(row == 1, up[HALO - 1:HALO, :], pltpu.roll(u, 2, 0)))
    return u1, u2


def mid_fwd(proj, o_g, lse_g, conv_w, pick, tm):
    s = proj.shape[0]
    hb = tm // HALO

    def body(ba_ref, ca_ref, xa_ref, za_ref, cah_ref, xah_ref, zb_ref,
             o0, o1, o2, l0, l1, l2, w_ref, pick_ref, ya_ref, yb_ref, at_ref, lc_ref, buf_o, buf_l):
        first = pl.program_id(0) == 0
        u = ca_ref[...].astype(F32) * xa_ref[...].astype(F32)
        u1, u2 = _conv_taps(u, cah_ref[...].astype(F32) * xah_ref[...].astype(F32), first)
        conv = w_ref[0:1, :] * u2 + w_ref[1:2, :] * u1 + w_ref[2:3, :] * u
        ya_ref[...] = (ba_ref[...].astype(F32) * conv * _silu(za_ref[...].astype(F32))).astype(BF16)
        ls = [_from_residue_major(l, buf_l.at[g], d) for g, (l, d) in enumerate(zip((l0, l1, l2), DILATIONS))]
        mx = jnp.maximum(jnp.maximum(ls[0], ls[1]), ls[2])
        es = [jnp.exp(l - mx) for l in ls]
        den = es[0] + es[1] + es[2]
        num = jnp.zeros_like(den)
        for e, o, d in zip(es, (o0, o1, o2), DILATIONS):
            num = num + e * _from_residue_major(o, buf_o, d)
        attn = num / den
        at_ref[...] = attn
        lc_ref[...] = _dot_hilo(mx + jnp.log(den), pick_ref)
        yb_ref[...] = (attn * _silu(zb_ref[...].astype(F32))).astype(BF16)

    col = lambda j: pl.BlockSpec((tm, D), lambda i: (i, j))
    halo = lambda j: pl.BlockSpec((HALO, D), lambda i: (jnp.maximum(i * hb - 1, 0), j))
    loc = lambda w: pl.BlockSpec((tm, w), lambda i: (i, 0))
    rm = [pl.BlockSpec((d, tm // d, CB), lambda i: (0, i, 0)) for d in DILATIONS]
    rm_view = lambda ts: [t.reshape(d, s // d, CB) for t, d in zip(ts, DILATIONS)]
    return pl.pallas_call(
        body, name="mid_fwd", grid=(s // tm,),
        in_specs=[col(0), col(1), col(2), col(3), halo(1), halo(2),
                  pl.BlockSpec((tm, CB), lambda i: (i, CB_ZB))] + rm + rm
                 + [pl.BlockSpec((3, D), lambda i: (0, 0)), pl.BlockSpec(pick.shape, lambda i: (0, 0))],
        out_specs=[loc(D), loc(CB), loc(CB), loc(LANES)],
        out_shape=[jax.ShapeDtypeStruct((s, D), BF16), jax.ShapeDtypeStruct((s, CB), BF16),
                   jax.ShapeDtypeStruct((s, CB), F32), jax.ShapeDtypeStruct((s, LANES), F32)],
        scratch_shapes=[pltpu.VMEM((CB // LANES, tm, LANES), F32), pltpu.VMEM((3, CB // LANES, tm, LANES), F32)],
        compiler_params=_cp(("parallel",)))(
            proj, proj, proj, proj, proj, proj, proj, *rm_view(o_g), *rm_view(lse_g), conv_w, pick)


def mid_bwd(dproj, proj, dya, conv_w, tm):
    s = proj.shape[0]
    hb = tm // HALO
    nblk = s // tm
    last_h = s // HALO - 1

    def body(_, ba_ref, ca_ref, xa_ref, za_ref, cah_ref, xah_ref, ban_ref, zan_ref, dy_ref, dyn_ref, w_ref,
             o_ref, gw_ref):
        i = pl.program_id(0)
        ba, ca, xa, za = (t[...].astype(F32) for t in (ba_ref, ca_ref, xa_ref, za_ref))
        u = ca * xa
        u1, u2 = _conv_taps(u, cah_ref[...].astype(F32) * xah_ref[...].astype(F32), i == 0)
        w0, w1, w2 = w_ref[0:1, :], w_ref[1:2, :], w_ref[2:3, :]
        conv = w0 * u2 + w1 * u1 + w2 * u
        sg = jax.nn.sigmoid(za)
        sz = za * sg
        dy = dy_ref[...].astype(F32)
        dconv = dy * ba * sz
        dcn = dyn_ref[...].astype(F32) * ban_ref[...].astype(F32) * _silu(zan_ref[...].astype(F32))
        dcn = jnp.where(i == nblk - 1, 0.0, dcn)
        row = lax.broadcasted_iota(jnp.int32, (tm, 1), 0)
        d1 = jnp.where(row == tm - 1, dcn[0:1, :], pltpu.roll(dconv, tm - 1, 0))
        d2 = jnp.where(row == tm - 2, dcn[0:1, :],
                       jnp.where(row == tm - 1, dcn[1:2, :], pltpu.roll(dconv, tm - 2, 0)))
        du = w2 * dconv + w1 * d1 + w0 * d2
        o_ref[:, 0:D] = (dy * conv * sz).astype(BF16)
        o_ref[:, D:2 * D] = (du * xa).astype(BF16)
        o_ref[:, 2 * D:3 * D] = (du * ca).astype(BF16)
        o_ref[:, 3 * D:4 * D] = (dy * ba * conv * (sg * (1.0 + za * (1.0 - sg)))).astype(BF16)

        @pl.when(i == 0)
        def _():
            gw_ref[...] = jnp.zeros_like(gw_ref)

        gw_ref[0:1, :] += jnp.sum(dconv * u2, axis=0, keepdims=True)
        gw_ref[1:2, :] += jnp.sum(dconv * u1, axis=0, keepdims=True)
        gw_ref[2:3, :] += jnp.sum(dconv * u, axis=0, keepdims=True)

    col = lambda j: pl.BlockSpec((tm, D), lambda i: (i, j))
    halo_prev = lambda j: pl.BlockSpec((HALO, D), lambda i: (jnp.maximum(i * hb - 1, 0), j))
    halo_next = lambda j: pl.BlockSpec((HALO, D), lambda i: (jnp.minimum((i + 1) * hb, last_h), j))
    return pl.pallas_call(
        body, name="mid_bwd", grid=(nblk,),
        in_specs=[pl.BlockSpec(memory_space=pl.ANY), col(0), col(1), col(2), col(3),
                  halo_prev(1), halo_prev(2), halo_next(0), halo_next(3),
                  pl.BlockSpec((tm, D), lambda i: (i, 0)), halo_next(0),
                  pl.BlockSpec((3, D), lambda i: (0, 0))],
        out_specs=[pl.BlockSpec((tm, 4 * D), lambda i: (i, 0)), pl.BlockSpec((8, D), lambda i: (0, 0))],
        out_shape=[jax.ShapeDtypeStruct((s, NIN), BF16), jax.ShapeDtypeStruct((8, D), F32)],
        input_output_aliases={0: 0},
        compiler_params=_cp(("arbitrary",)))(dproj, proj, proj, proj, proj, proj, proj, proj, proj, dya, dya, conv_w)


def tail(proj, ya, yb, attn, x, target, gate, pa_w, pb_w, wo_w, total, tm):
    s = proj.shape[0]
    ni = s // tm
    ncol = NIN - CB_ZB * CB

    def body(ya_ref, yb_ref, ga_ref, gb_ref, zb_ref, at_ref, x_ref, t_ref, gate_ref, pa_ref, pb_ref, wo_ref,
             tot_ref, dp_hbm, dy_ref, dya_ref, da_ref, dc_ref, mg_ref, do_ref, dpa_ref, dpb_ref, st_ref,
             stage, sems):
        i = pl.program_id(0)
        slot = i % 2

        def slab(step, sl):
            return pltpu.make_async_copy(
                stage.at[sl], dp_hbm.at[pl.ds(pl.multiple_of(step * tm, tm), tm), pl.ds(CB_ZB * CB, ncol)],
                sems.at[sl])

        @pl.when(i == 0)
        def _():
            st_ref[...] = jnp.zeros_like(st_ref)

        @pl.when(i >= 2)
        def _():
            slab(i - 2, slot).wait()

        gate_v = gate_ref[...]
        pa = jnp.dot(ya_ref[...], pa_ref[...], preferred_element_type=F32)
        pb = jnp.dot(yb_ref[...], pb_ref[...], preferred_element_type=F32)
        sa = jax.nn.sigmoid(ga_ref[...].astype(F32))
        sb = jax.nn.sigmoid(gb_ref[...].astype(F32))
        merged = (sa * pa + sb * pb).astype(BF16)
        mg_ref[...] = merged
        out = jnp.dot(merged, wo_ref[...], preferred_element_type=F32)
        err = x_ref[...] + gate_v * out - t_ref[...]
        dy = err * (1.0 / D)
        dy_ref[...] = dy
        st_ref[0:1, :] += jnp.sum(dy * out, axis=0, keepdims=True)
        st_ref[1:2, :] += jnp.sum(err * err, axis=0, keepdims=True)
        dout = (gate_v * dy).astype(BF16)
        do_ref[...] = dout
        dmg = lax.dot_general(dout, wo_ref[...], NT, preferred_element_type=F32)
        dpa = (dmg * sa).astype(BF16)
        dpb = (dmg * sb).astype(BF16)
        dpa_ref[...] = dpa
        dpb_ref[...] = dpb
        stage[slot, :, CB:CB + D] = (dmg * pa * sa * (1.0 - sa)).astype(BF16)
        stage[slot, :, CB + D:] = (dmg * pb * sb * (1.0 - sb)).astype(BF16)
        dya_ref[...] = lax.dot_general(dpa, pa_ref[...], NT, preferred_element_type=F32).astype(BF16)
        dyb = lax.dot_general(dpb, pb_ref[...], NT, preferred_element_type=F32)
        zb = zb_ref[...].astype(F32)
        sg = jax.nn.sigmoid(zb)
        attn_v = at_ref[...]
        dattn = dyb * (zb * sg)
        da_ref[...] = dattn.astype(BF16)
        stage[slot, :, 0:CB] = (dyb * attn_v * (sg * (1.0 + zb * (1.0 - sg)))).astype(BF16)
        dc_ref[...] = _dot_hilo(dattn * attn_v, tot_ref)

        slab(i, slot).start()

        @pl.when(i == ni - 1)
        def _():
            slab(i - 1, 1 - slot).wait()
            slab(i, slot).wait()

    row = lambda w: pl.BlockSpec((tm, w), lambda i: (i, 0))
    pcol = lambda w, jb: pl.BlockSpec((tm, w), lambda i: (i, jb))
    full = lambda a: pl.BlockSpec(a.shape, lambda i: (0, 0))
    return pl.pallas_call(
        body, name="tail", grid=(ni,),
        in_specs=[row(D), row(CB), pcol(D, 9), pcol(D, 10), pcol(CB, CB_ZB), row(CB), row(D), row(D),
                  pl.BlockSpec((1, D), lambda i: (0, 0)), full(pa_w), full(pb_w), full(wo_w), full(total)],
        out_specs=[pl.BlockSpec(memory_space=pl.ANY),
                   row(D), row(D), row(CB), row(LANES), row(D), row(D), row(D), row(D),
                   pl.BlockSpec((8, D), lambda i: (0, 0))],
        out_shape=[jax.ShapeDtypeStruct((s, NIN), BF16), jax.ShapeDtypeStruct((s, D), F32),
                   jax.ShapeDtypeStruct((s, D), BF16), jax.ShapeDtypeStruct((s, CB), BF16),
                   jax.ShapeDtypeStruct((s, LANES), F32)] + [jax.ShapeDtypeStruct((s, D), BF16)] * 4
                  + [jax.ShapeDtypeStruct((8, D), F32)],
        scratch_shapes=[pltpu.VMEM((2, tm, ncol), BF16), pltpu.SemaphoreType.DMA((2,))],
        compiler_params=_cp(("arbitrary",), 56))(
            ya, yb, proj, proj, proj, attn, x, target, gate, pa_w, pb_w, wo_w, total)


def _local_step(x, target, shift, scale, gate, norm_w, conv_w, qw, kw, w_shard, pa_w, pb_w, wo_w, me_xyc):
    qw8, kw8 = jnp.tile(qw, (1, NH)), jnp.tile(kw, (1, NH))
    same, total, pick = _head_matrices()
    h, ht = norm_fwd(x, norm_w, scale, shift, 512)
    proj, wg = proj_fwd_gather(h, w_shard, gather_order(me_xyc), 1024)
    srcs = qkv_prep(proj, qw8, kw8, same, 512)
    o_g, lse_g = zip(*[attn_fwd(*srcs[g], g, d) for g, d in enumerate(DILATIONS)])
    ya, yb, attn, lc = mid_fwd(proj, o_g, lse_g, conv_w, pick, 512)
    dproj, dy, dya, da, dc, merged, dout, dpa, dpb, st_tail = tail(
        proj, ya, yb, attn, x, target, gate, pa_w, pb_w, wo_w, total, 256)
    g_wo = matmul_tn(merged, dout, "grad_w_out", 1024)
    g_pa = matmul_tn(ya, dpa, "grad_w_br_conv", 1024)
    g_pb = matmul_tn(yb, dpb, "grad_w_br_attn", 1024)
    dproj, st_conv = mid_bwd(dproj, proj, dya, conv_w, 512)
    grads = []
    for g, d in enumerate(DILATIONS):
        da_p, lc_p, dc_p, lt, dt = stats_prep(da, lc, dc, g, d, 2048)
        dq = attn_bwd_q(*srcs[g], da_p, lc_p, dc_p, g, d)
        dk, dv = attn_bwd_kv(*srcs[g], da_p, lt, dt, g, d)
        grads.append((dq, dk, dv))
    dproj, gw_qk = qkv_grads_to_dproj(dproj, proj, grads, qw8, kw8, same, 512)
    slabs = [g_pa.reshape(NDEV, 128, D), g_pb.reshape(CB, NDEV, 128).transpose(1, 0, 2), g_wo.reshape(NDEV, 128, D)]
    dh, r_win, (r_pa, r_pb, r_wo) = proj_bwd(ht, dproj, wg, slabs, scatter_order(me_xyc), 1024)
    grad_x, st_norm = norm_bwd(dh, x, dy, norm_w, scale, 512)
    dmod = jnp.concatenate([st_norm[0:1], st_norm[1:2], st_tail[0:1]], axis=1)
    loss_part = (0.5 / D) * jnp.sum(st_tail[1])
    gw_heads = gw_qk[0:2].reshape(2, NH, HD).sum(axis=1)
    small = dict(dmod=dmod, norm_w=st_norm[2:3], conv_w=st_conv[0:3],
                 q_norm_w=gw_heads[0:1], k_norm_w=gw_heads[1:2], loss=loss_part)
    return grad_x, small, (r_win, r_pa, r_pb, r_wo)


def kernel(x, c, w_ada, b_ada, norm_w, w_in, conv_w, q_norm_w, k_norm_w, w_br_conv, w_br_attn, w_out, loss_target, m_w_ada, m_b_ada, m_norm_w, m_w_in, m_conv_w, m_q_norm_w, m_k_norm_w, m_w_br_conv, m_w_br_attn, m_w_out, v_w_ada, v_b_ada, v_norm_w, v_w_in, v_conv_w, v_q_norm_w, v_k_norm_w, v_w_br_conv, v_w_br_attn, v_w_out):
    me_xyc = (lax.axis_index("x"), lax.axis_index("y"), lax.axis_index("c"))
    me = _dev_index(me_xyc)
    ncol = w_ada.shape[2]

    conv_pad = jnp.zeros((8, 128), F32).at[0:3].set(conv_w[0])
    pa_g, pb_g, wo_g, c_all, conv_all = all_gather(
        [w_br_conv[0].astype(BF16), w_br_attn[0].astype(BF16), w_out[0].astype(BF16), c, conv_pad],
        "gather_weights")
    pa_w = pa_g.reshape(D, D)
    wo_w = wo_g.reshape(D, D)
    pb_w = pb_g.transpose(1, 0, 2).reshape(CB, D)
    conv_full = conv_all[:, 0:3].transpose(1, 0, 2).reshape(3, D)
    c_all = c_all.reshape(NDEV, D)

    b_cols = lax.dynamic_slice(b_ada, (0, me * ncol), (1, ncol))
    mod_cols = ada_fwd(c_all, w_ada[0], b_cols)
    (mod_all,) = all_gather([mod_cols], "gather_mod")
    mod = lax.dynamic_index_in_dim(mod_all, me, axis=1, keepdims=False).reshape(1, 3 * D)
    shift, scale, gate = mod[:, 0:D], mod[:, D:2 * D], mod[:, 2 * D:3 * D]

    grad_x, small, (r_win, r_pa, r_pb, r_wo) = _local_step(
        x[0], loss_target[0], shift, scale, gate, norm_w, conv_full, q_norm_w, k_norm_w,
        w_in[0].astype(BF16), pa_w, pb_w, wo_w, me_xyc)

    packed = jnp.concatenate(
        [small["dmod"], small["norm_w"], small["conv_w"].reshape(1, 3 * D), small["q_norm_w"], small["k_norm_w"],
         jnp.full((1, 128), small["loss"], F32)], axis=1)
    (packed_all,) = all_gather([packed], "gather_small")
    tot = sum_parts(packed_all)
    loss = tot[0, 7 * D + 2 * HD]
    dmod_all = packed_all[:, 0, 0:3 * D]
    g_b_ada = tot[:, 0:3 * D]
    g_norm_w = tot[:, 3 * D:4 * D]
    g_conv = lax.dynamic_slice(tot[:, 4 * D:7 * D].reshape(3, D), (0, me * 128), (3, 128))
    g_qn = tot[:, 7 * D:7 * D + HD]
    g_kn = tot[:, 7 * D + HD:7 * D + 2 * HD]
    g_w_ada = ada_bwd(c_all.T, lax.dynamic_slice(dmod_all, (0, me * ncol), (NDEV, ncol)))

    def upd(parts, w, m, v, name, rows):
        shape = w.shape
        w2, m2, v2 = (t.reshape(shape[-2:]) for t in (w, m, v))
        return [t.reshape(shape) for t in adamw(parts, w2, m2, v2, name, rows)]

    res = {
        "w_ada": upd(g_w_ada[None], w_ada, m_w_ada, v_w_ada, "adamw_w_ada", 256),
        "b_ada": upd(g_b_ada[None], b_ada, m_b_ada, v_b_ada, "adamw_b_ada", 1),
        "norm_w": upd(g_norm_w[None], norm_w, m_norm_w, v_norm_w, "adamw_norm_w", 1),
        "w_in": upd(r_win, w_in, m_w_in, v_w_in, "adamw_w_in", 128),
        "conv_w": upd(g_conv[None], conv_w, m_conv_w, v_conv_w, "adamw_conv_w", 3),
        "q_norm_w": upd(g_qn[None], q_norm_w, m_q_norm_w, v_q_norm_w, "adamw_q_norm_w", 1),
        "k_norm_w": upd(g_kn[None], k_norm_w, m_k_norm_w, v_k_norm_w, "adamw_k_norm_w", 1),
        "w_br_conv": upd(r_pa, w_br_conv, m_w_br_conv, v_w_br_conv, "adamw_w_br_conv", 128),
        "w_br_attn": upd(r_pb, w_br_attn, m_w_br_attn, v_w_br_attn, "adamw_w_br_attn", 512),
        "w_out": upd(r_wo, w_out, m_w_out, v_w_out, "adamw_w_out", 128),
    }
    names = ["w_ada", "b_ada", "norm_w", "w_in", "conv_w", "q_norm_w", "k_norm_w", "w_br_conv", "w_br_attn", "w_out"]
    return (loss, grad_x[None], *[res[n][0] for n in names], *[res[n][1] for n in names],
            *[res[n][2] for n in names], *[res[n][3] for n in names])
```

```python
import jax
import jax.numpy as jnp
from jax import lax
from jax.experimental import pallas as pl
from jax.experimental.pallas import tpu as pltpu

F32, BF16 = jnp.float32, jnp.bfloat16
D = 1024
NIN = 11264
NDEV = 8
SHARD = NIN // NDEV
HD = 64
NH = 8
QB = 128
CB = 512
CB_Q, CB_K, CB_V, CB_ZB = 8, 11, 14, 17
DILATIONS = (1, 4, 16)
EPS = 1e-6
NEG = -1e30
HALO = 16
LANES = 128
MESH = pl.DeviceIdType.MESH

ADAM_LR, ADAM_B1, ADAM_B2, ADAM_EPS, ADAM_WD, ADAM_STEP = 0.001, 0.9, 0.999, 1e-08, 0.01, 10

NT = (((1,), (1,)), ((), ()))
TN = (((0,), (0,)), ((), ()))


def _cp(sem, vmem_mb=48):
    return pltpu.CompilerParams(dimension_semantics=sem, vmem_limit_bytes=vmem_mb << 20)


def _silu(z):
    return z * jax.nn.sigmoid(z)


def _coords():
    return lax.axis_index("x"), lax.axis_index("y"), lax.axis_index("c")


def all_gather(arrs, name):
    n = len(arrs)

    def body(*refs):
        ins, outs = refs[:n], refs[n:2 * n]
        send_sems, recv_sems, local_sems = refs[2 * n:]
        x, y, c = _coords()
        me, sibling = (x, y, c), (x, y, 1 - c)
        chips = [(1 - x, y), (x, 1 - y), (1 - x, 1 - y)]

        def slot(a, dev):
            return outs[a].at[4 * dev[0] + 2 * dev[1] + dev[2]]

        def copy(a, k, block, to, src=None):
            return pltpu.make_async_remote_copy(
                src_ref=slot(a, block) if src is None else src, dst_ref=slot(a, block),
                send_sem=send_sems.at[a, k], recv_sem=recv_sems.at[a, k],
                device_id=to, device_id_type=MESH)

        mine = [pltpu.make_async_copy(ins[a], slot(a, me), local_sems.at[a]) for a in range(n)]
        for cp in mine:
            cp.start()
        first = []
        for a in range(n):
            first.append(copy(a, 0, me, sibling, src=ins[a]))
            first += [copy(a, 1 + j, me, (*chip, c), src=ins[a]) for j, chip in enumerate(chips)]
        for cp in first:
            cp.start()
        passed = []
        for j, chip in enumerate(chips):
            for a in range(n):
                copy(a, 1 + j, (*chip, c), me).wait_recv()
                fwd = copy(a, 4 + j, (*chip, c), sibling)
                fwd.start()
                passed.append(fwd)
        for a in range(n):
            copy(a, 0, sibling, me).wait_recv()
            for j, chip in enumerate(chips):
                copy(a, 4 + j, (*chip, 1 - c), me).wait_recv()
        for cp in first + passed:
            cp.wait_send()
        for cp in mine:
            cp.wait()

    any_spec = pl.BlockSpec(memory_space=pl.ANY)
    return pl.pallas_call(
        body, name=name,
        out_shape=[jax.ShapeDtypeStruct((NDEV,) + a.shape, a.dtype) for a in arrs],
        in_specs=[any_spec] * n, out_specs=[any_spec] * n,
        scratch_shapes=[pltpu.SemaphoreType.DMA((n, 7)), pltpu.SemaphoreType.DMA((n, 7)),
                        pltpu.SemaphoreType.DMA((n,))],
    )(*arrs)


FLIPS = [(fx, fy, fc) for fx in (0, 1) for fy in (0, 1) for fc in (0, 1)][1:]


def _flip(dev, f):
    return tuple(1 - v if b else v for v, b in zip(dev, f))


def _dev_index(dev):
    return 4 * dev[0] + 2 * dev[1] + dev[2]


def gather_order(me_xyc):
    x, y, c = me_xyc
    chips = [(1 - x, y), (x, 1 - y), (1 - x, 1 - y)]
    devs = [(x, y, c), (x, y, 1 - c)] + [(*ch, c) for ch in chips] + [(*ch, 1 - c) for ch in chips]
    return jnp.stack([_dev_index(d) for d in devs]).astype(jnp.int32)


def scatter_order(me_xyc):
    devs = [_flip(me_xyc, f) for f in FLIPS] + [me_xyc]
    return jnp.stack([_dev_index(d) for d in devs]).astype(jnp.int32)


def ada_fwd(c_all, w_ada, b_cols):
    def body(c_ref, w_ref, b_ref, o_ref):
        a = _silu(c_ref[...]).astype(BF16)
        o_ref[...] = jnp.dot(a, w_ref[...].astype(BF16), preferred_element_type=F32) + b_ref[...]

    return pl.pallas_call(body, name="ada_fwd",
                          out_shape=jax.ShapeDtypeStruct((NDEV, w_ada.shape[1]), F32))(c_all, w_ada, b_cols)


def ada_bwd(c_all_t, dmod_cols):
    def body(c_ref, d_ref, o_ref):
        at = _silu(c_ref[...])
        acc = at[:, 0:1] * d_ref[0:1, :]
        for b in range(1, NDEV):
            acc = acc + at[:, b:b + 1] * d_ref[b:b + 1, :]
        o_ref[...] = acc

    return pl.pallas_call(body, name="ada_bwd",
                          out_shape=jax.ShapeDtypeStruct((D, dmod_cols.shape[1]), F32))(c_all_t, dmod_cols)


def sum_parts(parts):
    def body(p_ref, o_ref):
        acc = p_ref[0]
        for b in range(1, NDEV):
            acc = acc + p_ref[b]
        o_ref[...] = acc

    return pl.pallas_call(body, name="sum_parts",
                          out_shape=jax.ShapeDtypeStruct(parts.shape[1:], F32))(parts)


def adamw(parts, w, m, v, name, rows):
    n, r, ccols = parts.shape

    def body(p_ref, w_ref, m_ref, v_ref, g_ref, d_ref, nm_ref, nv_ref):
        g = p_ref[0].astype(F32)
        for b in range(1, n):
            g = g + p_ref[b].astype(F32)
        nm = ADAM_B1 * m_ref[...] + (1.0 - ADAM_B1) * g
        nv = ADAM_B2 * v_ref[...] + (1.0 - ADAM_B2) * (g * g)
        g_ref[...] = g
        nm_ref[...] = nm
        nv_ref[...] = nv
        m_hat = nm / (1.0 - ADAM_B1 ** ADAM_STEP)
        v_hat = nv / (1.0 - ADAM_B2 ** ADAM_STEP)
        d_ref[...] = -ADAM_LR * (m_hat / (jnp.sqrt(v_hat) + ADAM_EPS) + ADAM_WD * w_ref[...])

    blk = pl.BlockSpec((rows, ccols), lambda i: (i, 0))
    out = jax.ShapeDtypeStruct((r, ccols), F32)
    return pl.pallas_call(
        body, name=name, grid=(r // rows,),
        in_specs=[pl.BlockSpec((n, rows, ccols), lambda i: (0, i, 0)), blk, blk, blk],
        out_specs=[blk] * 4, out_shape=[out] * 4, compiler_params=_cp(("parallel",)))(parts, w, m, v)


def norm_fwd(x, nw, scale, shift, tm):
    s = x.shape[0]

    def body(x_ref, nw_ref, sc_ref, sh_ref, h_ref, ht_ref):
        xf = x_ref[...]
        r = lax.rsqrt(jnp.mean(xf * xf, axis=-1, keepdims=True) + EPS)
        h = (xf * r * nw_ref[...]) * (1.0 + sc_ref[...]) + sh_ref[...]
        h_ref[...] = h.astype(BF16)
        ht_ref[...] = h.T.astype(BF16)

    vec = pl.BlockSpec((1, D), lambda i: (0, 0))
    return pl.pallas_call(
        body, name="norm_fwd", grid=(s // tm,),
        in_specs=[pl.BlockSpec((tm, D), lambda i: (i, 0)), vec, vec, vec],
        out_specs=[pl.BlockSpec((tm, D), lambda i: (i, 0)), pl.BlockSpec((D, tm), lambda i: (0, i))],
        out_shape=[jax.ShapeDtypeStruct((s, D), BF16), jax.ShapeDtypeStruct((D, s), BF16)],
        compiler_params=_cp(("parallel",)))(x, nw, scale, shift)


def norm_bwd(dh, x, dy, nw, scale, tm):
    s = x.shape[0]

    def body(dh_ref, x_ref, dy_ref, nw_ref, sc_ref, gx_ref, st_ref):
        xf, g = x_ref[...], dh_ref[...]
        r = lax.rsqrt(jnp.mean(xf * xf, axis=-1, keepdims=True) + EPS)
        xh = xf * r
        dn = g * (1.0 + sc_ref[...])
        dxh = dn * nw_ref[...]
        gx_ref[...] = dy_ref[...] + r * (dxh - xh * jnp.mean(dxh * xh, axis=-1, keepdims=True))

        @pl.when(pl.program_id(0) == 0)
        def _():
            st_ref[...] = jnp.zeros_like(st_ref)

        st_ref[0:1, :] += jnp.sum(g, axis=0, keepdims=True)
        st_ref[1:2, :] += jnp.sum(g * xh * nw_ref[...], axis=0, keepdims=True)
        st_ref[2:3, :] += jnp.sum(dn * xh, axis=0, keepdims=True)

    vec = pl.BlockSpec((1, D), lambda i: (0, 0))
    row = pl.BlockSpec((tm, D), lambda i: (i, 0))
    return pl.pallas_call(
        body, name="norm_bwd", grid=(s // tm,),
        in_specs=[row, row, row, vec, vec],
        out_specs=[row, pl.BlockSpec((8, D), lambda i: (0, 0))],
        out_shape=[jax.ShapeDtypeStruct((s, D), F32), jax.ShapeDtypeStruct((8, D), F32)],
        compiler_params=_cp(("arbitrary",)))(dh, x, dy, nw, scale)


def proj_fwd_gather(h, w_shard, order, tm):
    s = h.shape[0]
    ni = s // tm

    def body(order_ref, h_ref, w_ref, o_ref, wg_ref, wbuf, send_sems, recv_sems, local_sem, load_sem):
        jj, i = pl.program_id(0), pl.program_id(1)
        x, y, c = _coords()
        me, sibling = (x, y, c), (x, y, 1 - c)
        chips = [(1 - x, y), (x, 1 - y), (1 - x, 1 - y)]

        def slot(dev):
            return wg_ref.at[_dev_index(dev)]

        def copy(k, block, to, src=None):
            return pltpu.make_async_remote_copy(
                src_ref=slot(block) if src is None else src, dst_ref=slot(block),
                send_sem=send_sems.at[k], recv_sem=recv_sems.at[k], device_id=to, device_id_type=MESH)

        mine = pltpu.make_async_copy(w_ref, slot(me), local_sem)
        first = [copy(0, me, sibling, src=w_ref)] + [copy(1 + j, me, (*ch, c), src=w_ref) for j, ch in enumerate(chips)]
        passed = [copy(4 + j, (*ch, c), sibling) for j, ch in enumerate(chips)]
        start = i == 0

        @pl.when(start & (jj == 0))
        def _():
            mine.start()
            for cp in first:
                cp.start()
            mine.wait()

        @pl.when(start & (jj == 1))
        def _():
            copy(0, sibling, me).wait_recv()

        for j, ch in enumerate(chips):
            @pl.when(start & (jj == 2 + j))
            def _(j=j, ch=ch):
                copy(1 + j, (*ch, c), me).wait_recv()
                passed[j].start()

            @pl.when(start & (jj == 5 + j))
            def _(j=j, ch=ch):
                copy(4 + j, (*ch, 1 - c), me).wait_recv()

        @pl.when(start)
        def _():
            load = pltpu.make_async_copy(wg_ref.at[order_ref[jj]], wbuf, load_sem)
            load.start()
            load.wait()

        o_ref[...] = jnp.dot(h_ref[...], wbuf[...], preferred_element_type=F32).astype(BF16)

        @pl.when((jj == NDEV - 1) & (i == ni - 1))
        def _():
            for cp in first + passed:
                cp.wait_send()

    any_spec = pl.BlockSpec(memory_space=pl.ANY)
    return pl.pallas_call(
        body, name="proj_fwd_gather",
        grid_spec=pltpu.PrefetchScalarGridSpec(
            num_scalar_prefetch=1, grid=(NDEV, ni),
            in_specs=[pl.BlockSpec((tm, D), lambda jj, i, o: (i, 0)), any_spec],
            out_specs=[pl.BlockSpec((tm, SHARD), lambda jj, i, o: (i, o[jj])), any_spec],
            scratch_shapes=[pltpu.VMEM((D, SHARD), BF16), pltpu.SemaphoreType.DMA((7,)),
                            pltpu.SemaphoreType.DMA((7,)), pltpu.SemaphoreType.DMA, pltpu.SemaphoreType.DMA]),
        out_shape=[jax.ShapeDtypeStruct((s, NIN), BF16), jax.ShapeDtypeStruct((NDEV, D, SHARD), BF16)],
        compiler_params=_cp(("arbitrary", "arbitrary")))(order, h, w_shard)


def proj_bwd(ht, dproj, wg, smalls, order, tt):
    s = dproj.shape[0]
    nk = s // tt
    n = len(smalls)

    def body(order_ref, ht_ref, dp_ref, w_ref, *rest):
        small_in = rest[:n]
        dh_ref, gw_ref, rwin_ref = rest[n:n + 3]
        small_out = rest[n + 3:2 * n + 3]
        acc, stage, send_sems, recv_sems, local_sems, stage_sems = rest[2 * n + 3:]
        t, k = pl.program_id(0), pl.program_id(1)
        me_xyc = _coords()
        me = _dev_index(me_xyc)
        peers = [_flip(me_xyc, f) for f in FLIPS]

        def exchange(a, kf, src_arr, dst_arr):
            pid = _dev_index(peers[kf])
            mk = lambda dst: pltpu.make_async_remote_copy(
                src_ref=src_arr.at[pid], dst_ref=dst, send_sem=send_sems.at[a, kf], recv_sem=recv_sems.at[a, kf],
                device_id=peers[kf], device_id_type=MESH)
            return mk(dst_arr.at[me]), mk(dst_arr.at[pid])

        small_pairs = [exchange(1 + a, kf, small_in[a], small_out[a]) for kf in range(7) for a in range(n)]
        small_own = [pltpu.make_async_copy(small_in[a].at[me], small_out[a].at[me], local_sems.at[1 + a])
                     for a in range(n)]
        win_pairs = [exchange(0, kf, gw_ref, rwin_ref) for kf in range(7)]
        win_own = pltpu.make_async_copy(gw_ref.at[me], rwin_ref.at[me], local_sems.at[0])

        def to_hbm(jj):
            slab = me if jj == 7 else _dev_index(peers[jj])
            return pltpu.make_async_copy(stage.at[jj % 2], gw_ref.at[slab], stage_sems.at[jj % 2])

        @pl.when((t == 0) & (k == 0))
        def _():
            for cp in small_own:
                cp.start()
            for send, _ in small_pairs:
                send.start()

        @pl.when(t < NDEV)
        def _():
            p = jnp.dot(ht_ref[...], dp_ref[...], preferred_element_type=F32)

            @pl.when(k == 0)
            def _():
                acc[...] = p

            @pl.when(k > 0)
            def _():
                acc[...] += p

        for jj in range(NDEV):
            @pl.when((t == jj) & (k == nk - 1))
            def _(jj=jj):
                stage[jj % 2] = acc[...].astype(BF16)
                to_hbm(jj).start()

            @pl.when((t == jj + 1) & (k == 0))
            def _(jj=jj):
                to_hbm(jj).wait()
                if jj < 7:
                    win_pairs[jj][0].start()
                else:
                    win_own.start()

        @pl.when(t >= NDEV)
        def _():
            p = lax.dot_general(dp_ref[...], w_ref[...], NT, preferred_element_type=F32)

            @pl.when(k == 0)
            def _():
                dh_ref[...] = p

            @pl.when(k > 0)
            def _():
                dh_ref[...] += p

        @pl.when((t == 2 * NDEV - 1) & (k == nk - 1))
        def _():
            for _, recv in win_pairs + small_pairs:
                recv.wait_recv()
            for send, _ in win_pairs + small_pairs:
                send.wait_send()
            win_own.wait()
            for cp in small_own:
                cp.wait()

    any_spec = pl.BlockSpec(memory_space=pl.ANY)
    first = lambda t: t < NDEV
    outs = pl.pallas_call(
        body, name="proj_bwd",
        grid_spec=pltpu.PrefetchScalarGridSpec(
            num_scalar_prefetch=1, grid=(2 * NDEV, nk),
            in_specs=[pl.BlockSpec((D, tt), lambda t, k, o: (0, jnp.where(first(t), k, nk - 1))),
                      pl.BlockSpec((tt, SHARD), lambda t, k, o: (jnp.where(first(t), k, t - NDEV),
                                                                 jnp.where(first(t), o[jnp.minimum(t, NDEV - 1)], k))),
                      pl.BlockSpec((None, D, SHARD), lambda t, k, o: (jnp.where(first(t), 0, k), 0, 0))]
                     + [any_spec] * n,
            out_specs=[pl.BlockSpec((tt, D), lambda t, k, o: (jnp.where(first(t), 0, t - NDEV), 0))]
                      + [any_spec] * (2 + n),
            scratch_shapes=[pltpu.VMEM((D, SHARD), F32), pltpu.VMEM((2, D, SHARD), BF16),
                            pltpu.SemaphoreType.DMA((1 + n, 7)), pltpu.SemaphoreType.DMA((1 + n, 7)),
                            pltpu.SemaphoreType.DMA((1 + n,)), pltpu.SemaphoreType.DMA((2,))]),
        out_shape=[jax.ShapeDtypeStruct((s, D), F32), jax.ShapeDtypeStruct((NDEV, D, SHARD), BF16),
                   jax.ShapeDtypeStruct((NDEV, D, SHARD), BF16)]
                  + [jax.ShapeDtypeStruct(a.shape, a.dtype) for a in smalls],
        compiler_params=_cp(("arbitrary", "arbitrary"), 56))(order, ht, dproj, wg, *smalls)
    return outs[0], outs[2], outs[3:]


def matmul_tn(a, b, name, tk):
    s, m = a.shape
    n = b.shape[1]
    nk = s // tk

    def body(a_ref, b_ref, o_ref, acc_ref):
        k = pl.program_id(0)
        p = lax.dot_general(a_ref[...], b_ref[...], TN, preferred_element_type=F32)

        @pl.when(k == 0)
        def _():
            acc_ref[...] = p

        @pl.when(k > 0)
        def _():
            acc_ref[...] += p

        @pl.when(k == nk - 1)
        def _():
            o_ref[...] = acc_ref[...].astype(BF16)

    return pl.pallas_call(
        body, name=name, grid=(nk,),
        in_specs=[pl.BlockSpec((tk, m), lambda k: (k, 0)), pl.BlockSpec((tk, n), lambda k: (k, 0))],
        out_specs=pl.BlockSpec((m, n), lambda k: (0, 0)),
        out_shape=jax.ShapeDtypeStruct((m, n), BF16),
        scratch_shapes=[pltpu.VMEM((m, n), F32)],
        compiler_params=_cp(("arbitrary",)))(a, b)


def _head_matrices():
    lane = lax.broadcasted_iota(jnp.int32, (CB, CB), 0)
    col = lax.broadcasted_iota(jnp.int32, (CB, CB), 1)
    same = (lane // HD == col // HD).astype(BF16)
    lane_c = lax.broadcasted_iota(jnp.int32, (CB, LANES), 0)
    col_c = lax.broadcasted_iota(jnp.int32, (CB, LANES), 1)
    total = (lane_c // HD == col_c).astype(BF16)
    pick = (lane_c == col_c * HD).astype(BF16)
    return same, total, pick


def _head_sum(x, m_ref):
    return jnp.dot(x.astype(BF16), m_ref[...], preferred_element_type=F32)


def _dot_hilo(x, m_ref):
    hi = x.astype(BF16)
    lo = (x - hi.astype(F32)).astype(BF16)
    return (jnp.dot(hi, m_ref[...], preferred_element_type=F32)
            + jnp.dot(lo, m_ref[...], preferred_element_type=F32))


def _to_residue_major(val, buf, out_ref, dil):
    rows = out_ref.shape[1]
    for k in range(val.shape[1] // LANES):
        lanes = slice(k * LANES, (k + 1) * LANES)
        buf[k] = val[:, lanes]
        for r in range(dil):
            out_ref[r, :, lanes] = buf.at[k][pl.ds(r, rows, stride=dil), :].astype(out_ref.dtype)


def _from_residue_major(ref, buf, dil):
    if dil == 1:
        return ref[0].astype(F32)
    rows = ref.shape[1]
    for k in range(CB // LANES):
        for r in range(dil):
            buf.at[k][pl.ds(r, rows, stride=dil), :] = ref[r, :, k * LANES:(k + 1) * LANES].astype(F32)
    return jnp.concatenate([buf[k] for k in range(CB // LANES)], axis=1)


def qkv_prep(proj, qw8, kw8, same, tm):
    s = proj.shape[0]
    items = []
    for g, d in enumerate(DILATIONS):
        items += [(g, "q", CB_Q + g, d), (g, "k", CB_K + g, d)] + ([(g, "v", CB_V + g, d)] if d > 1 else [])
    n = len(items)

    def body(*refs):
        ins, (qw_ref, kw_ref, same_ref), outs, buf = refs[:n], refs[n:n + 3], refs[n + 3:2 * n + 3], refs[-1]
        for idx, (_, kind, _, dil) in enumerate(items):
            val = ins[idx][...].astype(F32)
            if kind != "v":
                r = lax.rsqrt(_head_sum(val * val, same_ref) * (1.0 / HD) + EPS)
                val = val * r * (qw_ref if kind == "q" else kw_ref)[...]
            if dil == 1:
                outs[idx][0] = val.astype(BF16)
            else:
                _to_residue_major(val, buf, outs[idx], dil)

    full = lambda a: pl.BlockSpec(a.shape, lambda i: (0, 0))
    outs = pl.pallas_call(
        body, name="qkv_prep", grid=(s // tm,),
        in_specs=[pl.BlockSpec((tm, CB), lambda i, cb=cb: (i, cb)) for _, _, cb, _ in items]
                 + [full(qw8), full(kw8), full(same)],
        out_specs=[pl.BlockSpec((d, tm // d, CB), lambda i: (0, i, 0)) for _, _, _, d in items],
        out_shape=[jax.ShapeDtypeStruct((d, s // d, CB), BF16) for _, _, _, d in items],
        scratch_shapes=[pltpu.VMEM((CB // LANES, tm, LANES), F32)],
        compiler_params=_cp(("parallel",)))(*([proj] * n), qw8 * (HD ** -0.5), kw8, same)
    srcs = [[None, None, (proj, CB_V + g)] for g in range(len(DILATIONS))]
    for (g, kind, _, _), o in zip(items, outs):
        srcs[g]["qkv".index(kind)] = (o.reshape(s, CB), 0)
    return srcs


def stats_prep(da, lc, dc, g, dil, tm):
    s = da.shape[0]
    rows = tm // dil

    def body(da_ref, lc_ref, dc_ref, dap_ref, lcp_ref, dcp_ref, lt_ref, dt_ref, buf):
        if dil == 1:
            dap_ref[0] = da_ref[...]
        else:
            _to_residue_major(da_ref[...].astype(F32), buf, dap_ref, dil)
        for src, dst, dst_t in ((lc_ref, lcp_ref, lt_ref), (dc_ref, dcp_ref, dt_ref)):
            buf[0] = src[...]
            for r in range(dil):
                piece = buf.at[0][pl.ds(r, rows, stride=dil), :] if dil > 1 else buf[0]
                dst[r] = piece
                dst_t[r] = piece.T[0:NH, :]

    row = lambda w: pl.BlockSpec((tm, w), lambda i: (i, 0))
    rm = lambda w: pl.BlockSpec((dil, rows, w), lambda i: (0, i, 0))
    tr = pl.BlockSpec((dil, NH, rows), lambda i: (0, 0, i))
    length = s // dil
    dap, lcp, dcp, lt, dt = pl.pallas_call(
        body, name=f"stats_prep_g{g}", grid=(s // tm,),
        in_specs=[row(CB), row(LANES), row(LANES)],
        out_specs=[rm(CB), rm(LANES), rm(LANES), tr, tr],
        out_shape=[jax.ShapeDtypeStruct((dil, length, CB), BF16)]
                  + [jax.ShapeDtypeStruct((dil, length, LANES), F32)] * 2
                  + [jax.ShapeDtypeStruct((dil, NH, length), F32)] * 2,
        scratch_shapes=[pltpu.VMEM((CB // LANES, tm, LANES), F32)],
        compiler_params=_cp(("parallel",)))(da, lc, dc)
    return (dap.reshape(s, CB), lcp.reshape(s, LANES), dcp.reshape(s, LANES),
            lt.reshape(dil * NH, length), dt.reshape(dil * NH, length))


def qkv_grads_to_dproj(dproj, proj, grads, qw8, kw8, same, tm):
    s = dproj.shape[0]
    ni = s // tm
    flat = [(t.reshape(d, s // d, CB), d, kind, 3 * kind + g)
            for g, d in enumerate(DILATIONS) for kind, t in enumerate(grads[g])]
    nf = len(flat)
    nraw = 2 * len(DILATIONS)

    def body(*refs):
        dp_hbm, raws, ins = refs[nraw + nf + 4], refs[1:1 + nraw], refs[1 + nraw:1 + nraw + nf]
        qw_ref, kw_ref, same_ref = refs[1 + nraw + nf:4 + nraw + nf]
        gw_ref, stage, buf, sems = refs[5 + nraw + nf:]
        i = pl.program_id(0)
        slot = i % 2

        def slab(step, sl):
            return pltpu.make_async_copy(
                stage.at[sl], dp_hbm.at[pl.ds(pl.multiple_of(step * tm, tm), tm), pl.ds(CB_Q * CB, 9 * CB)],
                sems.at[sl])

        @pl.when(i == 0)
        def _():
            gw_ref[...] = jnp.zeros_like(gw_ref)

        @pl.when(i >= 2)
        def _():
            slab(i - 2, slot).wait()

        for ref, (_, d, kind, jj) in zip(ins, flat):
            cols = slice(jj * CB, (jj + 1) * CB)
            dn = _from_residue_major(ref, buf, d)
            if kind == 2:
                stage[slot, :, cols] = dn.astype(BF16)
                continue
            t = raws[jj][...].astype(F32)
            r = lax.rsqrt(_head_sum(t * t, same_ref) * (1.0 / HD) + EPS)
            xh = t * r
            gw_ref[kind:kind + 1, :] += jnp.sum(dn * xh, axis=0, keepdims=True)
            dxh = dn * (qw_ref if kind == 0 else kw_ref)[...]
            mean = _head_sum(dxh * xh, same_ref) * (1.0 / HD)
            stage[slot, :, cols] = (r * (dxh - xh * mean)).astype(BF16)
        slab(i, slot).start()

        @pl.when(i == ni - 1)
        def _():
            slab(i - 1, 1 - slot).wait()
            slab(i, slot).wait()

    full = lambda a: pl.BlockSpec(a.shape, lambda i: (0, 0))
    any_spec = pl.BlockSpec(memory_space=pl.ANY)
    return pl.pallas_call(
        body, name="qkv_grads_to_dproj", grid=(ni,),
        in_specs=[any_spec] + [pl.BlockSpec((tm, CB), lambda i, jb=jb: (i, CB_Q + jb)) for jb in range(nraw)]
                 + [pl.BlockSpec((d, tm // d, CB), lambda i: (0, i, 0)) for _, d, _, _ in flat]
                 + [full(qw8), full(kw8), full(same)],
        out_specs=[any_spec, pl.BlockSpec((8, CB), lambda i: (0, 0))],
        out_shape=[jax.ShapeDtypeStruct((s, NIN), BF16), jax.ShapeDtypeStruct((8, CB), F32)],
        input_output_aliases={0: 0},
        scratch_shapes=[pltpu.VMEM((2, tm, 9 * CB), BF16), pltpu.VMEM((CB // LANES, tm, LANES), F32),
                        pltpu.SemaphoreType.DMA((2,))],
        compiler_params=_cp(("arbitrary",)))(
            dproj, *([proj] * nraw), *[t for t, _, _, _ in flat], qw8, kw8, same)


def _lane_lo():
    return lax.broadcasted_iota(jnp.int32, (1, 2 * HD), 1) < HD


def _stack_heads(t, lo):
    zero = jnp.zeros_like(t)
    return jnp.concatenate([jnp.where(lo, t, zero), jnp.where(lo, zero, t)], axis=0)


def _masks(other_ok):
    qi = lax.broadcasted_iota(jnp.int32, (QB, QB), 0)
    kj = lax.broadcasted_iota(jnp.int32, (QB, QB), 1)
    return (kj >= qi) & other_ok, kj <= qi


SUB = 4


def _attn_specs(nb, dil):
    steps = nb // SUB
    main = lambda cb, w=CB: pl.BlockSpec((SUB * QB, w), lambda r, s: (r * steps + s, cb))
    prev = lambda cb: pl.BlockSpec((QB, CB), lambda r, s: (jnp.maximum(r * nb + SUB * s - 1, 0), cb))
    nxt = lambda cb: pl.BlockSpec((QB, CB), lambda r, s: (jnp.minimum(r * nb + SUB * (s + 1), dil * nb - 1), cb))
    return main, prev, nxt


def attn_fwd(q_src, k_src, v_src, g, dil):
    s = q_src[0].shape[0]
    nb = s // dil // QB
    main, prev, _ = _attn_specs(nb, dil)

    def body(q_ref, kp_ref, k_ref, vp_ref, v_ref, o_ref, l_ref, kbuf, vbuf):
        step = pl.program_id(1)
        kbuf[0:QB], kbuf[QB:] = kp_ref[...], k_ref[...]
        vbuf[0:QB], vbuf[QB:] = vp_ref[...], v_ref[...]
        lo = _lane_lo()

        def block(j, carry):
            r0 = pl.multiple_of(j * QB, QB)
            rows, krows = pl.ds(r0, QB), pl.ds(r0, 2 * QB)
            m_prev, m_cur = _masks(step * SUB + j > 0)
            mask = jnp.concatenate([m_prev, m_cur], axis=1)
            mask = jnp.concatenate([mask, mask], axis=0)
            for i in range(NH // 2):
                sl = slice(2 * HD * i, 2 * HD * (i + 1))
                qs, ks, vv = q_ref[rows, sl], kbuf[krows, sl], vbuf[krows, sl]
                sc = lax.dot_general(_stack_heads(qs, lo), ks, NT, preferred_element_type=F32)
                sc = jnp.where(mask, sc, NEG)
                mx = jnp.max(sc, axis=-1, keepdims=True)
                p = jnp.exp(sc - mx)
                den = jnp.sum(p, axis=-1, keepdims=True)
                o = jnp.dot(p.astype(BF16), vv, preferred_element_type=F32) * (1.0 / den)
                lse = jnp.broadcast_to(mx + jnp.log(den), (2 * QB, 2 * HD))
                o_ref[rows, sl] = jnp.where(lo, o[:QB], o[QB:])
                l_ref[rows, sl] = jnp.where(lo, lse[:QB], lse[QB:])
            return carry

        lax.fori_loop(0, SUB, block, 0)

    out = jax.ShapeDtypeStruct((s, CB), F32)
    return pl.pallas_call(
        body, name=f"attn_fwd_g{g}", grid=(dil, nb // SUB),
        in_specs=[main(q_src[1]), prev(k_src[1]), main(k_src[1]), prev(v_src[1]), main(v_src[1])],
        out_specs=[main(0)] * 2, out_shape=[out, out],
        scratch_shapes=[pltpu.VMEM(((SUB + 1) * QB, CB), BF16)] * 2,
        compiler_params=_cp(("parallel", "parallel")))(q_src[0], k_src[0], k_src[0], v_src[0], v_src[0])


def attn_bwd_q(q_src, k_src, v_src, da, lc, dc, g, dil):
    s = q_src[0].shape[0]
    nb = s // dil // QB
    main, prev, _ = _attn_specs(nb, dil)

    def body(q_ref, kp_ref, k_ref, vp_ref, v_ref, da_ref, l_ref, d_ref, dq_ref, kbuf, vbuf):
        step = pl.program_id(1)
        kbuf[0:QB], kbuf[QB:] = kp_ref[...], k_ref[...]
        vbuf[0:QB], vbuf[QB:] = vp_ref[...], v_ref[...]
        lo = _lane_lo()

        def block(j, carry):
            r0 = pl.multiple_of(j * QB, QB)
            rows, krows = pl.ds(r0, QB), pl.ds(r0, 2 * QB)
            m_prev, m_cur = _masks(step * SUB + j > 0)
            mask = jnp.concatenate([m_prev, m_cur], axis=1)
            mask = jnp.concatenate([mask, mask], axis=0)
            lcols, dcols = l_ref[rows, :], d_ref[rows, :]
            for i in range(NH // 2):
                sl = slice(2 * HD * i, 2 * HD * (i + 1))
                qs, ks, vv, da2 = q_ref[rows, sl], kbuf[krows, sl], vbuf[krows, sl], da_ref[rows, sl]
                pair = lambda t: jnp.concatenate([t[:, 2 * i:2 * i + 1], t[:, 2 * i + 1:2 * i + 2]], axis=0)
                sc = lax.dot_general(_stack_heads(qs, lo), ks, NT, preferred_element_type=F32)
                sc = jnp.where(mask, sc, NEG)
                p = jnp.exp(sc - pair(lcols))
                dp = lax.dot_general(_stack_heads(da2, lo), vv, NT, preferred_element_type=F32)
                ds = p * (dp - pair(dcols))
                dq = jnp.dot(ds.astype(BF16), ks, preferred_element_type=F32)
                dq_ref[rows, sl] = (jnp.where(lo, dq[:QB], dq[QB:]) * (HD ** -0.5)).astype(BF16)
            return carry

        lax.fori_loop(0, SUB, block, 0)

    return pl.pallas_call(
        body, name=f"attn_bwd_q_g{g}", grid=(dil, nb // SUB),
        in_specs=[main(q_src[1]), prev(k_src[1]), main(k_src[1]), prev(v_src[1]), main(v_src[1]),
                  main(0), main(0, LANES), main(0, LANES)],
        out_specs=main(0), out_shape=jax.ShapeDtypeStruct((s, CB), BF16),
        scratch_shapes=[pltpu.VMEM(((SUB + 1) * QB, CB), BF16)] * 2,
        compiler_params=_cp(("parallel", "parallel")))(
            q_src[0], k_src[0], k_src[0], v_src[0], v_src[0], da, lc, dc)


def attn_bwd_kv(q_src, k_src, v_src, da, lt, dt, g, dil):
    s = q_src[0].shape[0]
    nb = s // dil // QB
    main, _, nxt = _attn_specs(nb, dil)

    def body(k_ref, v_ref, q_ref, qn_ref, da_ref, dan_ref, l_ref, ln_ref, d_ref, dn_ref, dk_ref, dv_ref,
             qbuf, dabuf, lbuf, dbuf):
        step = pl.program_id(1)
        qbuf[0:SUB * QB], qbuf[SUB * QB:] = q_ref[...], qn_ref[...]
        dabuf[0:SUB * QB], dabuf[SUB * QB:] = da_ref[...], dan_ref[...]
        for c in range(SUB):
            lbuf[c], dbuf[c] = l_ref[:, c * QB:(c + 1) * QB], d_ref[:, c * QB:(c + 1) * QB]
        lbuf[SUB], dbuf[SUB] = ln_ref[...], dn_ref[...]
        lo = _lane_lo()
        kj = lax.broadcasted_iota(jnp.int32, (QB, QB), 0)
        qi = lax.broadcasted_iota(jnp.int32, (QB, QB), 1)

        def block(j, carry):
            r0 = pl.multiple_of(j * QB, QB)
            rows, qrows = pl.ds(r0, QB), pl.ds(r0, 2 * QB)
            mask = jnp.concatenate([kj <= qi, (kj >= qi) & (step * SUB + j < nb - 1)], axis=1)
            mask = jnp.concatenate([mask, mask], axis=1)
            lrow = jnp.concatenate([lbuf[j], lbuf[j + 1]], axis=1)
            drow = jnp.concatenate([dbuf[j], dbuf[j + 1]], axis=1)
            for i in range(NH // 2):
                sl = slice(2 * HD * i, 2 * HD * (i + 1))
                q2, da2 = _stack_heads(qbuf[qrows, sl], lo), _stack_heads(dabuf[qrows, sl], lo)
                ks, vv = k_ref[rows, sl], v_ref[rows, sl]
                pair = lambda t: jnp.concatenate([t[2 * i:2 * i + 1, :], t[2 * i + 1:2 * i + 2, :]], axis=1)
                sc = lax.dot_general(ks, q2, NT, preferred_element_type=F32)
                sc = jnp.where(mask, sc, NEG)
                p = jnp.exp(sc - pair(lrow))
                dp = lax.dot_general(vv, da2, NT, preferred_element_type=F32)
                ds = p * (dp - pair(drow))
                dv_ref[rows, sl] = jnp.dot(p.astype(BF16), da2, preferred_element_type=F32).astype(BF16)
                dk_ref[rows, sl] = jnp.dot(ds.astype(BF16), q2, preferred_element_type=F32).astype(BF16)
            return carry

        lax.fori_loop(0, SUB, block, 0)

    steps = nb // SUB
    t_main = pl.BlockSpec((NH, SUB * QB), lambda r, s: (r, s))
    t_nxt = pl.BlockSpec((NH, QB), lambda r, s: (r, jnp.minimum(SUB * (s + 1), nb - 1)))
    out = jax.ShapeDtypeStruct((s, CB), BF16)
    return pl.pallas_call(
        body, name=f"attn_bwd_kv_g{g}", grid=(dil, steps),
        in_specs=[main(k_src[1]), main(v_src[1]), main(q_src[1]), nxt(q_src[1]),
                  main(0), nxt(0), t_main, t_nxt, t_main, t_nxt],
        out_specs=[main(0), main(0)], out_shape=[out, out],
        scratch_shapes=[pltpu.VMEM(((SUB + 1) * QB, CB), BF16)] * 2 + [pltpu.VMEM((SUB + 1, NH, QB), F32)] * 2,
        compiler_params=_cp(("parallel", "parallel")))(
            k_src[0], v_src[0], q_src[0], q_src[0], da, da, lt, lt, dt, dt)


def _conv_taps(u, u_prev, first):
    tm = u.shape[0]
    row = lax.broadcasted_iota(jnp.int32, (tm, 1), 0)
    up = jnp.where(first, 0.0, u_prev)
    u1 = jnp.where(row == 0, up[HALO - 1:HALO, :], pltpu.roll(u, 1, 0))
    u2 = jnp.where(row == 0, up[HALO - 2:HALO - 1, :],
                   jnp.where(row == 1, up[HALO - 1:HALO, :], pltpu.roll(u, 2, 0)))
    return u1, u2


def mid_fwd(proj, o_g, lse_g, conv_w, pick, tm):
    s = proj.shape[0]
    hb = tm // HALO

    def body(ba_ref, ca_ref, xa_ref, za_ref, cah_ref, xah_ref, zb_ref,
             o0, o1, o2, l0, l1, l2, w_ref, pick_ref, ya_ref, yb_ref, at_ref, lc_ref, buf_o, buf_l):
        first = pl.program_id(0) == 0
        u = ca_ref[...].astype(F32) * xa_ref[...].astype(F32)
        u1, u2 = _conv_taps(u, cah_ref[...].astype(F32) * xah_ref[...].astype(F32), first)
        conv = w_ref[0:1, :] * u2 + w_ref[1:2, :] * u1 + w_ref[2:3, :] * u
        ya_ref[...] = (ba_ref[...].astype(F32) * conv * _silu(za_ref[...].astype(F32))).astype(BF16)
        ls = [_from_residue_major(l, buf_l.at[g], d) for g, (l, d) in enumerate(zip((l0, l1, l2), DILATIONS))]
        mx = jnp.maximum(jnp.maximum(ls[0], ls[1]), ls[2])
        es = [jnp.exp(l - mx) for l in ls]
        den = es[0] + es[1] + es[2]
        num = jnp.zeros_like(den)
        for e, o, d in zip(es, (o0, o1, o2), DILATIONS):
            num = num + e * _from_residue_major(o, buf_o, d)
        attn = num / den
        at_ref[...] = attn
        lc_ref[...] = _dot_hilo(mx + jnp.log(den), pick_ref)
        yb_ref[...] = (attn * _silu(zb_ref[...].astype(F32))).astype(BF16)

    col = lambda j: pl.BlockSpec((tm, D), lambda i: (i, j))
    halo = lambda j: pl.BlockSpec((HALO, D), lambda i: (jnp.maximum(i * hb - 1, 0), j))
    loc = lambda w: pl.BlockSpec((tm, w), lambda i: (i, 0))
    rm = [pl.BlockSpec((d, tm // d, CB), lambda i: (0, i, 0)) for d in DILATIONS]
    rm_view = lambda ts: [t.reshape(d, s // d, CB) for t, d in zip(ts, DILATIONS)]
    return pl.pallas_call(
        body, name="mid_fwd", grid=(s // tm,),
        in_specs=[col(0), col(1), col(2), col(3), halo(1), halo(2),
                  pl.BlockSpec((tm, CB), lambda i: (i, CB_ZB))] + rm + rm
                 + [pl.BlockSpec((3, D), lambda i: (0, 0)), pl.BlockSpec(pick.shape, lambda i: (0, 0))],
        out_specs=[loc(D), loc(CB), loc(CB), loc(LANES)],
        out_shape=[jax.ShapeDtypeStruct((s, D), BF16), jax.ShapeDtypeStruct((s, CB), BF16),
                   jax.ShapeDtypeStruct((s, CB), F32), jax.ShapeDtypeStruct((s, LANES), F32)],
        scratch_shapes=[pltpu.VMEM((CB // LANES, tm, LANES), F32), pltpu.VMEM((3, CB // LANES, tm, LANES), F32)],
        compiler_params=_cp(("parallel",)))(
            proj, proj, proj, proj, proj, proj, proj, *rm_view(o_g), *rm_view(lse_g), conv_w, pick)


def mid_bwd(dproj, proj, dya, conv_w, tm):
    s = proj.shape[0]
    hb = tm // HALO
    nblk = s // tm
    last_h = s // HALO - 1

    def body(_, ba_ref, ca_ref, xa_ref, za_ref, cah_ref, xah_ref, ban_ref, zan_ref, dy_ref, dyn_ref, w_ref,
             o_ref, gw_ref):
        i = pl.program_id(0)
        ba, ca, xa, za = (t[...].astype(F32) for t in (ba_ref, ca_ref, xa_ref, za_ref))
        u = ca * xa
        u1, u2 = _conv_taps(u, cah_ref[...].astype(F32) * xah_ref[...].astype(F32), i == 0)
        w0, w1, w2 = w_ref[0:1, :], w_ref[1:2, :], w_ref[2:3, :]
        conv = w0 * u2 + w1 * u1 + w2 * u
        sg = jax.nn.sigmoid(za)
        sz = za * sg
        dy = dy_ref[...].astype(F32)
        dconv = dy * ba * sz
        dcn = dyn_ref[...].astype(F32) * ban_ref[...].astype(F32) * _silu(zan_ref[...].astype(F32))
        dcn = jnp.where(i == nblk - 1, 0.0, dcn)
        row = lax.broadcasted_iota(jnp.int32, (tm, 1), 0)
        d1 = jnp.where(row == tm - 1, dcn[0:1, :], pltpu.roll(dconv, tm - 1, 0))
        d2 = jnp.where(row == tm - 2, dcn[0:1, :],
                       jnp.where(row == tm - 1, dcn[1:2, :], pltpu.roll(dconv, tm - 2, 0)))
        du = w2 * dconv + w1 * d1 + w0 * d2
        o_ref[:, 0:D] = (dy * conv * sz).astype(BF16)
        o_ref[:, D:2 * D] = (du * xa).astype(BF16)
        o_ref[:, 2 * D:3 * D] = (du * ca).astype(BF16)
        o_ref[:, 3 * D:4 * D] = (dy * ba * conv * (sg * (1.0 + za * (1.0 - sg)))).astype(BF16)

        @pl.when(i == 0)
        def _():
            gw_ref[...] = jnp.zeros_like(gw_ref)

        gw_ref[0:1, :] += jnp.sum(dconv * u2, axis=0, keepdims=True)
        gw_ref[1:2, :] += jnp.sum(dconv * u1, axis=0, keepdims=True)
        gw_ref[2:3, :] += jnp.sum(dconv * u, axis=0, keepdims=True)

    col = lambda j: pl.BlockSpec((tm, D), lambda i: (i, j))
    halo_prev = lambda j: pl.BlockSpec((HALO, D), lambda i: (jnp.maximum(i * hb - 1, 0), j))
    halo_next = lambda j: pl.BlockSpec((HALO, D), lambda i: (jnp.minimum((i + 1) * hb, last_h), j))
    return pl.pallas_call(
        body, name="mid_bwd", grid=(nblk,),
        in_specs=[pl.BlockSpec(memory_space=pl.ANY), col(0), col(1), col(2), col(3),
                  halo_prev(1), halo_prev(2), halo_next(0), halo_next(3),
                  pl.BlockSpec((tm, D), lambda i: (i, 0)), halo_next(0),
                  pl.BlockSpec((3, D), lambda i: (0, 0))],
        out_specs=[pl.BlockSpec((tm, 4 * D), lambda i: (i, 0)), pl.BlockSpec((8, D), lambda i: (0, 0))],
        out_shape=[jax.ShapeDtypeStruct((s, NIN), BF16), jax.ShapeDtypeStruct((8, D), F32)],
        input_output_aliases={0: 0},
        compiler_params=_cp(("arbitrary",)))(dproj, proj, proj, proj, proj, proj, proj, proj, proj, dya, dya, conv_w)


def tail(proj, ya, yb, attn, x, target, gate, pa_w, pb_w, wo_w, total, tm):
    s = proj.shape[0]
    ni = s // tm
    ncol = NIN - CB_ZB * CB

    def body(ya_ref, yb_ref, ga_ref, gb_ref, zb_ref, at_ref, x_ref, t_ref, gate_ref, pa_ref, pb_ref, wo_ref,
             tot_ref, dp_hbm, dy_ref, dya_ref, da_ref, dc_ref, mg_ref, do_ref, dpa_ref, dpb_ref, st_ref,
             stage, sems):
        i = pl.program_id(0)
        slot = i % 2

        def slab(step, sl):
            return pltpu.make_async_copy(
                stage.at[sl], dp_hbm.at[pl.ds(pl.multiple_of(step * tm, tm), tm), pl.ds(CB_ZB * CB, ncol)],
                sems.at[sl])

        @pl.when(i == 0)
        def _():
            st_ref[...] = jnp.zeros_like(st_ref)

        @pl.when(i >= 2)
        def _():
            slab(i - 2, slot).wait()

        gate_v = gate_ref[...]
        pa = jnp.dot(ya_ref[...], pa_ref[...], preferred_element_type=F32)
        pb = jnp.dot(yb_ref[...], pb_ref[...], preferred_element_type=F32)
        sa = jax.nn.sigmoid(ga_ref[...].astype(F32))
        sb = jax.nn.sigmoid(gb_ref[...].astype(F32))
        merged = (sa * pa + sb * pb).astype(BF16)
        mg_ref[...] = merged
        out = jnp.dot(merged, wo_ref[...], preferred_element_type=F32)
        err = x_ref[...] + gate_v * out - t_ref[...]
        dy = err * (1.0 / D)
        dy_ref[...] = dy
        st_ref[0:1, :] += jnp.sum(dy * out, axis=0, keepdims=True)
        st_ref[1:2, :] += jnp.sum(err * err, axis=0, keepdims=True)
        dout = (gate_v * dy).astype(BF16)
        do_ref[...] = dout
        dmg = lax.dot_general(dout, wo_ref[...], NT, preferred_element_type=F32)
        dpa = (dmg * sa).astype(BF16)
        dpb = (dmg * sb).astype(BF16)
        dpa_ref[...] = dpa
        dpb_ref[...] = dpb
        stage[slot, :, CB:CB + D] = (dmg * pa * sa * (1.0 - sa)).astype(BF16)
        stage[slot, :, CB + D:] = (dmg * pb * sb * (1.0 - sb)).astype(BF16)
        dya_ref[...] = lax.dot_general(dpa, pa_ref[...], NT, preferred_element_type=F32).astype(BF16)
        dyb = lax.dot_general(dpb, pb_ref[...], NT, preferred_element_type=F32)
        zb = zb_ref[...].astype(F32)
        sg = jax.nn.sigmoid(zb)
        attn_v = at_ref[...]
        dattn = dyb * (zb * sg)
        da_ref[...] = dattn.astype(BF16)
        stage[slot, :, 0:CB] = (dyb * attn_v * (sg * (1.0 + zb * (1.0 - sg)))).astype(BF16)
        dc_ref[...] = _dot_hilo(dattn * attn_v, tot_ref)

        slab(i, slot).start()

        @pl.when(i == ni - 1)
        def _():
            slab(i - 1, 1 - slot).wait()
            slab(i, slot).wait()

    row = lambda w: pl.BlockSpec((tm, w), lambda i: (i, 0))
    pcol = lambda w, jb: pl.BlockSpec((tm, w), lambda i: (i, jb))
    full = lambda a: pl.BlockSpec(a.shape, lambda i: (0, 0))
    return pl.pallas_call(
        body, name="tail", grid=(ni,),
        in_specs=[row(D), row(CB), pcol(D, 9), pcol(D, 10), pcol(CB, CB_ZB), row(CB), row(D), row(D),
                  pl.BlockSpec((1, D), lambda i: (0, 0)), full(pa_w), full(pb_w), full(wo_w), full(total)],
        out_specs=[pl.BlockSpec(memory_space=pl.ANY),
                   row(D), row(D), row(CB), row(LANES), row(D), row(D), row(D), row(D),
                   pl.BlockSpec((8, D), lambda i: (0, 0))],
        out_shape=[jax.ShapeDtypeStruct((s, NIN), BF16), jax.ShapeDtypeStruct((s, D), F32),
                   jax.ShapeDtypeStruct((s, D), BF16), jax.ShapeDtypeStruct((s, CB), BF16),
                   jax.ShapeDtypeStruct((s, LANES), F32)] + [jax.ShapeDtypeStruct((s, D), BF16)] * 4
                  + [jax.ShapeDtypeStruct((8, D), F32)],
        scratch_shapes=[pltpu.VMEM((2, tm, ncol), BF16), pltpu.SemaphoreType.DMA((2,))],
        compiler_params=_cp(("arbitrary",), 56))(
            ya, yb, proj, proj, proj, attn, x, target, gate, pa_w, pb_w, wo_w, total)


def _local_step(x, target, shift, scale, gate, norm_w, conv_w, qw, kw, w_shard, pa_w, pb_w, wo_w, me_xyc):
    qw8, kw8 = jnp.tile(qw, (1, NH)), jnp.tile(kw, (1, NH))
    same, total, pick = _head_matrices()
    h, ht = norm_fwd(x, norm_w, scale, shift, 512)
    proj, wg = proj_fwd_gather(h, w_shard, gather_order(me_xyc), 1024)
    srcs = qkv_prep(proj, qw8, kw8, same, 512)
    o_g, lse_g = zip(*[attn_fwd(*srcs[g], g, d) for g, d in enumerate(DILATIONS)])
    ya, yb, attn, lc = mid_fwd(proj, o_g, lse_g, conv_w, pick, 512)
    dproj, dy, dya, da, dc, merged, dout, dpa, dpb, st_tail = tail(
        proj, ya, yb, attn, x, target, gate, pa_w, pb_w, wo_w, total, 256)
    g_wo = matmul_tn(merged, dout, "grad_w_out", 1024)
    g_pa = matmul_tn(ya, dpa, "grad_w_br_conv", 1024)
    g_pb = matmul_tn(yb, dpb, "grad_w_br_attn", 1024)
    dproj, st_conv = mid_bwd(dproj, proj, dya, conv_w, 512)
    grads = []
    for g, d in enumerate(DILATIONS):
        da_p, lc_p, dc_p, lt, dt = stats_prep(da, lc, dc, g, d, 2048)
        dq = attn_bwd_q(*srcs[g], da_p, lc_p, dc_p, g, d)
        dk, dv = attn_bwd_kv(*srcs[g], da_p, lt, dt, g, d)
        grads.append((dq, dk, dv))
    dproj, gw_qk = qkv_grads_to_dproj(dproj, proj, grads, qw8, kw8, same, 512)
    slabs = [g_pa.reshape(NDEV, 128, D), g_pb.reshape(CB, NDEV, 128).transpose(1, 0, 2), g_wo.reshape(NDEV, 128, D)]
    dh, r_win, (r_pa, r_pb, r_wo) = proj_bwd(ht, dproj, wg, slabs, scatter_order(me_xyc), 1024)
    grad_x, st_norm = norm_bwd(dh, x, dy, norm_w, scale, 512)
    dmod = jnp.concatenate([st_norm[0:1], st_norm[1:2], st_tail[0:1]], axis=1)
    loss_part = (0.5 / D) * jnp.sum(st_tail[1])
    gw_heads = gw_qk[0:2].reshape(2, NH, HD).sum(axis=1)
    small = dict(dmod=dmod, norm_w=st_norm[2:3], conv_w=st_conv[0:3],
                 q_norm_w=gw_heads[0:1], k_norm_w=gw_heads[1:2], loss=loss_part)
    return grad_x, small, (r_win, r_pa, r_pb, r_wo)


def kernel(x, c, w_ada, b_ada, norm_w, w_in, conv_w, q_norm_w, k_norm_w, w_br_conv, w_br_attn, w_out, loss_target, m_w_ada, m_b_ada, m_norm_w, m_w_in, m_conv_w, m_q_norm_w, m_k_norm_w, m_w_br_conv, m_w_br_attn, m_w_out, v_w_ada, v_b_ada, v_norm_w, v_w_in, v_conv_w, v_q_norm_w, v_k_norm_w, v_w_br_conv, v_w_br_attn, v_w_out):
    me_xyc = (lax.axis_index("x"), lax.axis_index("y"), lax.axis_index("c"))
    me = _dev_index(me_xyc)
    ncol = w_ada.shape[2]

    conv_pad = jnp.zeros((8, 128), F32).at[0:3].set(conv_w[0])
    pa_g, pb_g, wo_g, c_all, conv_all = all_gather(
        [w_br_conv[0].astype(BF16), w_br_attn[0].astype(BF16), w_out[0].astype(BF16), c, conv_pad],
        "gather_weights")
    pa_w = pa_g.reshape(D, D)
    wo_w = wo_g.reshape(D, D)
    pb_w = pb_g.transpose(1, 0, 2).reshape(CB, D)
    conv_full = conv_all[:, 0:3].transpose(1, 0, 2).reshape(3, D)
    c_all = c_all.reshape(NDEV, D)

    b_cols = lax.dynamic_slice(b_ada, (0, me * ncol), (1, ncol))
    mod_cols = ada_fwd(c_all, w_ada[0], b_cols)
    (mod_all,) = all_gather([mod_cols], "gather_mod")
    mod = lax.dynamic_index_in_dim(mod_all, me, axis=1, keepdims=False).reshape(1, 3 * D)
    shift, scale, gate = mod[:, 0:D], mod[:, D:2 * D], mod[:, 2 * D:3 * D]

    grad_x, small, (r_win, r_pa, r_pb, r_wo) = _local_step(
        x[0], loss_target[0], shift, scale, gate, norm_w, conv_full, q_norm_w, k_norm_w,
        w_in[0].astype(BF16), pa_w, pb_w, wo_w, me_xyc)

    packed = jnp.concatenate(
        [small["dmod"], small["norm_w"], small["conv_w"].reshape(1, 3 * D), small["q_norm_w"], small["k_norm_w"],
         jnp.full((1, 128), small["loss"], F32)], axis=1)
    (packed_all,) = all_gather([packed], "gather_small")
    tot = sum_parts(packed_all)
    loss = tot[0, 7 * D + 2 * HD]
    dmod_all = packed_all[:, 0, 0:3 * D]
    g_b_ada = tot[:, 0:3 * D]
    g_norm_w = tot[:, 3 * D:4 * D]
    g_conv = lax.dynamic_slice(tot[:, 4 * D:7 * D].reshape(3, D), (0, me * 128), (3, 128))
    g_qn = tot[:, 7 * D:7 * D + HD]
    g_kn = tot[:, 7 * D + HD:7 * D + 2 * HD]
    g_w_ada = ada_bwd(c_all.T, lax.dynamic_slice(dmod_all, (0, me * ncol), (NDEV, ncol)))

    def upd(parts, w, m, v, name, rows):
        shape = w.shape
        w2, m2, v2 = (t.reshape(shape[-2:]) for t in (w, m, v))
        return [t.reshape(shape) for t in adamw(parts, w2, m2, v2, name, rows)]

    res = {
        "w_ada": upd(g_w_ada[None], w_ada, m_w_ada, v_w_ada, "adamw_w_ada", 256),
        "b_ada": upd(g_b_ada[None], b_ada, m_b_ada, v_b_ada, "adamw_b_ada", 1),
        "norm_w": upd(g_norm_w[None], norm_w, m_norm_w, v_norm_w, "adamw_norm_w", 1),
        "w_in": upd(r_win, w_in, m_w_in, v_w_in, "adamw_w_in", 128),
        "conv_w": upd(g_conv[None], conv_w, m_conv_w, v_conv_w, "adamw_conv_w", 3),
        "q_norm_w": upd(g_qn[None], q_norm_w, m_q_norm_w, v_q_norm_w, "adamw_q_norm_w", 1),
        "k_norm_w": upd(g_kn[None], k_norm_w, m_k_norm_w, v_k_norm_w, "adamw_k_norm_w", 1),
        "w_br_conv": upd(r_pa, w_br_conv, m_w_br_conv, v_w_br_conv, "adamw_w_br_conv", 128),
        "w_br_attn": upd(r_pb, w_br_attn, m_w_br_attn, v_w_br_attn, "adamw_w_br_attn", 512),
        "w_out": upd(r_wo, w_out, m_w_out, v_w_out, "adamw_w_out", 128),
    }
    names = ["w_ada", "b_ada", "norm_w", "w_in", "conv_w", "q_norm_w", "k_norm_w", "w_br_conv", "w_br_attn", "w_out"]
    return (loss, grad_x[None], *[res[n][0] for n in names], *[res[n][1] for n in names],
            *[res[n][2] for n in names], *[res[n][3] for n in names])
```

```python
import jax
import jax.numpy as jnp
from jax import lax
from jax.experimental import pallas as pl
from jax.experimental.pallas import tpu as pltpu

F32, BF16 = jnp.float32, jnp.bfloat16
D = 1024
NIN = 11264
NDEV = 8
SHARD = NIN // NDEV
HD = 64
NH = 8
QB = 128
CB = 512
CB_Q, CB_K, CB_V, CB_ZB = 8, 11, 14, 17
DILATIONS = (1, 4, 16)
EPS = 1e-6
NEG = -1e30
HALO = 16
LANES = 128
MESH = pl.DeviceIdType.MESH

ADAM_LR, ADAM_B1, ADAM_B2, ADAM_EPS, ADAM_WD, ADAM_STEP = 0.001, 0.9, 0.999, 1e-08, 0.01, 10

NT = (((1,), (1,)), ((), ()))
TN = (((0,), (0,)), ((), ()))


def _cp(sem, vmem_mb=48):
    return pltpu.CompilerParams(dimension_semantics=sem, vmem_limit_bytes=vmem_mb << 20)


def _silu(z):
    return z * jax.nn.sigmoid(z)


def _coords():
    return lax.axis_index("x"), lax.axis_index("y"), lax.axis_index("c")


def all_gather(arrs, name):
    n = len(arrs)

    def body(*refs):
        ins, outs = refs[:n], refs[n:2 * n]
        send_sems, recv_sems, local_sems = refs[2 * n:]
        x, y, c = _coords()
        me, sibling = (x, y, c), (x, y, 1 - c)
        chips = [(1 - x, y), (x, 1 - y), (1 - x, 1 - y)]

        def slot(a, dev):
            return outs[a].at[4 * dev[0] + 2 * dev[1] + dev[2]]

        def copy(a, k, block, to, src=None):
            return pltpu.make_async_remote_copy(
                src_ref=slot(a, block) if src is None else src, dst_ref=slot(a, block),
                send_sem=send_sems.at[a, k], recv_sem=recv_sems.at[a, k],
                device_id=to, device_id_type=MESH)

        mine = [pltpu.make_async_copy(ins[a], slot(a, me), local_sems.at[a]) for a in range(n)]
        for cp in mine:
            cp.start()
        first = []
        for a in range(n):
            first.append(copy(a, 0, me, sibling, src=ins[a]))
            first += [copy(a, 1 + j, me, (*chip, c), src=ins[a]) for j, chip in enumerate(chips)]
        for cp in first:
            cp.start()
        passed = []
        for j, chip in enumerate(chips):
            for a in range(n):
                copy(a, 1 + j, (*chip, c), me).wait_recv()
                fwd = copy(a, 4 + j, (*chip, c), sibling)
                fwd.start()
                passed.append(fwd)
        for a in range(n):
            copy(a, 0, sibling, me).wait_recv()
            for j, chip in enumerate(chips):
                copy(a, 4 + j, (*chip, 1 - c), me).wait_recv()
        for cp in first + passed:
            cp.wait_send()
        for cp in mine:
            cp.wait()

    any_spec = pl.BlockSpec(memory_space=pl.ANY)
    return pl.pallas_call(
        body, name=name,
        out_shape=[jax.ShapeDtypeStruct((NDEV,) + a.shape, a.dtype) for a in arrs],
        in_specs=[any_spec] * n, out_specs=[any_spec] * n,
        scratch_shapes=[pltpu.SemaphoreType.DMA((n, 7)), pltpu.SemaphoreType.DMA((n, 7)),
                        pltpu.SemaphoreType.DMA((n,))],
    )(*arrs)


FLIPS = [(fx, fy, fc) for fx in (0, 1) for fy in (0, 1) for fc in (0, 1)][1:]


def _flip(dev, f):
    return tuple(1 - v if b else v for v, b in zip(dev, f))


def _dev_index(dev):
    return 4 * dev[0] + 2 * dev[1] + dev[2]


def gather_order(me_xyc):
    x, y, c = me_xyc
    chips = [(1 - x, y), (x, 1 - y), (1 - x, 1 - y)]
    devs = [(x, y, c), (x, y, 1 - c)] + [(*ch, c) for ch in chips] + [(*ch, 1 - c) for ch in chips]
    return jnp.stack([_dev_index(d) for d in devs]).astype(jnp.int32)


def scatter_order(me_xyc):
    devs = [_flip(me_xyc, f) for f in FLIPS] + [me_xyc]
    return jnp.stack([_dev_index(d) for d in devs]).astype(jnp.int32)


def ada_fwd(c_all, w_ada, b_cols):
    def body(c_ref, w_ref, b_ref, o_ref):
        a = _silu(c_ref[...]).astype(BF16)
        o_ref[...] = jnp.dot(a, w_ref[...].astype(BF16), preferred_element_type=F32) + b_ref[...]

    return pl.pallas_call(body, name="ada_fwd",
                          out_shape=jax.ShapeDtypeStruct((NDEV, w_ada.shape[1]), F32))(c_all, w_ada, b_cols)


def ada_bwd(c_all_t, dmod_cols):
    def body(c_ref, d_ref, o_ref):
        at = _silu(c_ref[...])
        acc = at[:, 0:1] * d_ref[0:1, :]
        for b in range(1, NDEV):
            acc = acc + at[:, b:b + 1] * d_ref[b:b + 1, :]
        o_ref[...] = acc

    return pl.pallas_call(body, name="ada_bwd",
                          out_shape=jax.ShapeDtypeStruct((D, dmod_cols.shape[1]), F32))(c_all_t, dmod_cols)


def sum_parts(parts):
    def body(p_ref, o_ref):
        acc = p_ref[0]
        for b in range(1, NDEV):
            acc = acc + p_ref[b]
        o_ref[...] = acc

    return pl.pallas_call(body, name="sum_parts",
                          out_shape=jax.ShapeDtypeStruct(parts.shape[1:], F32))(parts)


def adamw(parts, w, m, v, name, rows):
    n, r, ccols = parts.shape

    def body(p_ref, w_ref, m_ref, v_ref, g_ref, d_ref, nm_ref, nv_ref):
        g = p_ref[0].astype(F32)
        for b in range(1, n):
            g = g + p_ref[b].astype(F32)
        nm = ADAM_B1 * m_ref[...] + (1.0 - ADAM_B1) * g
        nv = ADAM_B2 * v_ref[...] + (1.0 - ADAM_B2) * (g * g)
        g_ref[...] = g
        nm_ref[...] = nm
        nv_ref[...] = nv
        m_hat = nm / (1.0 - ADAM_B1 ** ADAM_STEP)
        v_hat = nv / (1.0 - ADAM_B2 ** ADAM_STEP)
        d_ref[...] = -ADAM_LR * (m_hat / (jnp.sqrt(v_hat) + ADAM_EPS) + ADAM_WD * w_ref[...])

    blk = pl.BlockSpec((rows, ccols), lambda i: (i, 0))
    out = jax.ShapeDtypeStruct((r, ccols), F32)
    return pl.pallas_call(
        body, name=name, grid=(r // rows,),
        in_specs=[pl.BlockSpec((n, rows, ccols), lambda i: (0, i, 0)), blk, blk, blk],
        out_specs=[blk] * 4, out_shape=[out] * 4, compiler_params=_cp(("parallel",)))(parts, w, m, v)


def norm_fwd(x, nw, scale, shift, tm):
    s = x.shape[0]

    def body(x_ref, nw_ref, sc_ref, sh_ref, h_ref, ht_ref):
        xf = x_ref[...]
        r = lax.rsqrt(jnp.mean(xf * xf, axis=-1, keepdims=True) + EPS)
        h = (xf * r * nw_ref[...]) * (1.0 + sc_ref[...]) + sh_ref[...]
        h_ref[...] = h.astype(BF16)
        ht_ref[...] = h.T.astype(BF16)

    vec = pl.BlockSpec((1, D), lambda i: (0, 0))
    return pl.pallas_call(
        body, name="norm_fwd", grid=(s // tm,),
        in_specs=[pl.BlockSpec((tm, D), lambda i: (i, 0)), vec, vec, vec],
        out_specs=[pl.BlockSpec((tm, D), lambda i: (i, 0)), pl.BlockSpec((D, tm), lambda i: (0, i))],
        out_shape=[jax.ShapeDtypeStruct((s, D), BF16), jax.ShapeDtypeStruct((D, s), BF16)],
        compiler_params=_cp(("parallel",)))(x, nw, scale, shift)


def norm_bwd(dh, x, dy, nw, scale, tm):
    s = x.shape[0]

    def body(dh_ref, x_ref, dy_ref, nw_ref, sc_ref, gx_ref, st_ref):
        xf, g = x_ref[...], dh_ref[...]
        r = lax.rsqrt(jnp.mean(xf * xf, axis=-1, keepdims=True) + EPS)
        xh = xf * r
        dn = g * (1.0 + sc_ref[...])
        dxh = dn * nw_ref[...]
        gx_ref[...] = dy_ref[...] + r * (dxh - xh * jnp.mean(dxh * xh, axis=-1, keepdims=True))

        @pl.when(pl.program_id(0) == 0)
        def _():
            st_ref[...] = jnp.zeros_like(st_ref)

        st_ref[0:1, :] += jnp.sum(g, axis=0, keepdims=True)
        st_ref[1:2, :] += jnp.sum(g * xh * nw_ref[...], axis=0, keepdims=True)
        st_ref[2:3, :] += jnp.sum(dn * xh, axis=0, keepdims=True)

    vec = pl.BlockSpec((1, D), lambda i: (0, 0))
    row = pl.BlockSpec((tm, D), lambda i: (i, 0))
    return pl.pallas_call(
        body, name="norm_bwd", grid=(s // tm,),
        in_specs=[row, row, row, vec, vec],
        out_specs=[row, pl.BlockSpec((8, D), lambda i: (0, 0))],
        out_shape=[jax.ShapeDtypeStruct((s, D), F32), jax.ShapeDtypeStruct((8, D), F32)],
        compiler_params=_cp(("arbitrary",)))(dh, x, dy, nw, scale)


def proj_fwd_gather(h, w_shard, order, tm):
    s = h.shape[0]
    ni = s // tm

    def body(order_ref, h_ref, w_ref, o_ref, wg_ref, wbuf, send_sems, recv_sems, local_sem, load_sem):
        jj, i = pl.program_id(0), pl.program_id(1)
        x, y, c = _coords()
        me, sibling = (x, y, c), (x, y, 1 - c)
        chips = [(1 - x, y), (x, 1 - y), (1 - x, 1 - y)]

        def slot(dev):
            return wg_ref.at[_dev_index(dev)]

        def copy(k, block, to, src=None):
            return pltpu.make_async_remote_copy(
                src_ref=slot(block) if src is None else src, dst_ref=slot(block),
                send_sem=send_sems.at[k], recv_sem=recv_sems.at[k], device_id=to, device_id_type=MESH)

        mine = pltpu.make_async_copy(w_ref, slot(me), local_sem)
        first = [copy(0, me, sibling, src=w_ref)] + [copy(1 + j, me, (*ch, c), src=w_ref) for j, ch in enumerate(chips)]
        passed = [copy(4 + j, (*ch, c), sibling) for j, ch in enumerate(chips)]
        start = i == 0

        @pl.when(start & (jj == 0))
        def _():
            mine.start()
            for cp in first:
                cp.start()
            mine.wait()

        @pl.when(start & (jj == 1))
        def _():
            copy(0, sibling, me).wait_recv()

        for j, ch in enumerate(chips):
            @pl.when(start & (jj == 2 + j))
            def _(j=j, ch=ch):
                copy(1 + j, (*ch, c), me).wait_recv()
                passed[j].start()

            @pl.when(start & (jj == 5 + j))
            def _(j=j, ch=ch):
                copy(4 + j, (*ch, 1 - c), me).wait_recv()

        @pl.when(start)
        def _():
            load = pltpu.make_async_copy(wg_ref.at[order_ref[jj]], wbuf, load_sem)
            load.start()
            load.wait()

        o_ref[...] = jnp.dot(h_ref[...], wbuf[...], preferred_element_type=F32).astype(BF16)

        @pl.when((jj == NDEV - 1) & (i == ni - 1))
        def _():
            for cp in first + passed:
                cp.wait_send()

    any_spec = pl.BlockSpec(memory_space=pl.ANY)
    return pl.pallas_call(
        body, name="proj_fwd_gather",
        grid_spec=pltpu.PrefetchScalarGridSpec(
            num_scalar_prefetch=1, grid=(NDEV, ni),
            in_specs=[pl.BlockSpec((tm, D), lambda jj, i, o: (i, 0)), any_spec],
            out_specs=[pl.BlockSpec((tm, SHARD), lambda jj, i, o: (i, o[jj])), any_spec],
            scratch_shapes=[pltpu.VMEM((D, SHARD), BF16), pltpu.SemaphoreType.DMA((7,)),
                            pltpu.SemaphoreType.DMA((7,)), pltpu.SemaphoreType.DMA, pltpu.SemaphoreType.DMA]),
        out_shape=[jax.ShapeDtypeStruct((s, NIN), BF16), jax.ShapeDtypeStruct((NDEV, D, SHARD), BF16)],
        compiler_params=_cp(("arbitrary", "arbitrary")))(order, h, w_shard)


def proj_bwd(ht, dproj, wg, smalls, order, tt):
    s = dproj.shape[0]
    nk = s // tt
    n = len(smalls)

    def body(order_ref, ht_ref, dp_ref, w_ref, *rest):
        small_in = rest[:n]
        dh_ref, gw_ref, rwin_ref = rest[n:n + 3]
        small_out = rest[n + 3:2 * n + 3]
        acc, stage, send_sems, recv_sems, local_sems, stage_sems = rest[2 * n + 3:]
        t, k = pl.program_id(0), pl.program_id(1)
        me_xyc = _coords()
        me = _dev_index(me_xyc)
        peers = [_flip(me_xyc, f) for f in FLIPS]

        def exchange(a, kf, src_arr, dst_arr):
            pid = _dev_index(peers[kf])
            mk = lambda dst: pltpu.make_async_remote_copy(
                src_ref=src_arr.at[pid], dst_ref=dst, send_sem=send_sems.at[a, kf], recv_sem=recv_sems.at[a, kf],
                device_id=peers[kf], device_id_type=MESH)
            return mk(dst_arr.at[me]), mk(dst_arr.at[pid])

        small_pairs = [exchange(1 + a, kf, small_in[a], small_out[a]) for kf in range(7) for a in range(n)]
        small_own = [pltpu.make_async_copy(small_in[a].at[me], small_out[a].at[me], local_sems.at[1 + a])
                     for a in range(n)]
        win_pairs = [exchange(0, kf, gw_ref, rwin_ref) for kf in range(7)]
        win_own = pltpu.make_async_copy(gw_ref.at[me], rwin_ref.at[me], local_sems.at[0])

        def to_hbm(jj):
            slab = me if jj == 7 else _dev_index(peers[jj])
            return pltpu.make_async_copy(stage.at[jj % 2], gw_ref.at[slab], stage_sems.at[jj % 2])

        @pl.when((t == 0) & (k == 0))
        def _():
            for cp in small_own:
                cp.start()
            for send, _ in small_pairs:
                send.start()

        @pl.when(t < NDEV)
        def _():
            p = jnp.dot(ht_ref[...], dp_ref[...], preferred_element_type=F32)

            @pl.when(k == 0)
            def _():
                acc[...] = p

            @pl.when(k > 0)
            def _():
                acc[...] += p

        for jj in range(NDEV):
            @pl.when((t == jj) & (k == nk - 1))
            def _(jj=jj):
                stage[jj % 2] = acc[...].astype(BF16)
                to_hbm(jj).start()

            @pl.when((t == jj + 1) & (k == 0))
            def _(jj=jj):
                to_hbm(jj).wait()
                if jj < 7:
                    win_pairs[jj][0].start()
                else:
                    win_own.start()

        @pl.when(t >= NDEV)
        def _():
            p = lax.dot_general(dp_ref[...], w_ref[...], NT, preferred_element_type=F32)

            @pl.when(k == 0)
            def _():
                dh_ref[...] = p

            @pl.when(k > 0)
            def _():
                dh_ref[...] += p

        @pl.when((t == 2 * NDEV - 1) & (k == nk - 1))
        def _():
            for _, recv in win_pairs + small_pairs:
                recv.wait_recv()
            for send, _ in win_pairs + small_pairs:
                send.wait_send()
            win_own.wait()
            for cp in small_own:
                cp.wait()

    any_spec = pl.BlockSpec(memory_space=pl.ANY)
    first = lambda t: t < NDEV
    outs = pl.pallas_call(
        body, name="proj_bwd",
        grid_spec=pltpu.PrefetchScalarGridSpec(
            num_scalar_prefetch=1, grid=(2 * NDEV, nk),
            in_specs=[pl.BlockSpec((D, tt), lambda t, k, o: (0, jnp.where(first(t), k, nk - 1))),
                      pl.BlockSpec((tt, SHARD), lambda t, k, o: (jnp.where(first(t), k, t - NDEV),
                                                                 jnp.where(first(t), o[jnp.minimum(t, NDEV - 1)], k))),
                      pl.BlockSpec((None, D, SHARD), lambda t, k, o: (jnp.where(first(t), 0, k), 0, 0))]
                     + [any_spec] * n,
            out_specs=[pl.BlockSpec((tt, D), lambda t, k, o: (jnp.where(first(t), 0, t - NDEV), 0))]
                      + [any_spec] * (2 + n),
            scratch_shapes=[pltpu.VMEM((D, SHARD), F32), pltpu.VMEM((2, D, SHARD), BF16),
                            pltpu.SemaphoreType.DMA((1 + n, 7)), pltpu.SemaphoreType.DMA((1 + n, 7)),
                            pltpu.SemaphoreType.DMA((1 + n,)), pltpu.SemaphoreType.DMA((2,))]),
        out_shape=[jax.ShapeDtypeStruct((s, D), F32), jax.ShapeDtypeStruct((NDEV, D, SHARD), BF16),
                   jax.ShapeDtypeStruct((NDEV, D, SHARD), BF16)]
                  + [jax.ShapeDtypeStruct(a.shape, a.dtype) for a in smalls],
        compiler_params=_cp(("arbitrary", "arbitrary"), 56))(order, ht, dproj, wg, *smalls)
    return outs[0], outs[2], outs[3:]


def matmul_tn(a, b, name, tk):
    s, m = a.shape
    n = b.shape[1]
    nk = s // tk

    def body(a_ref, b_ref, o_ref, acc_ref):
        k = pl.program_id(0)
        p = lax.dot_general(a_ref[...], b_ref[...], TN, preferred_element_type=F32)

        @pl.when(k == 0)
        def _():
            acc_ref[...] = p

        @pl.when(k > 0)
        def _():
            acc_ref[...] += p

        @pl.when(k == nk - 1)
        def _():
            o_ref[...] = acc_ref[...].astype(BF16)

    return pl.pallas_call(
        body, name=name, grid=(nk,),
        in_specs=[pl.BlockSpec((tk, m), lambda k: (k, 0)), pl.BlockSpec((tk, n), lambda k: (k, 0))],
        out_specs=pl.BlockSpec((m, n), lambda k: (0, 0)),
        out_shape=jax.ShapeDtypeStruct((m, n), BF16),
        scratch_shapes=[pltpu.VMEM((m, n), F32)],
        compiler_params=_cp(("arbitrary",)))(a, b)


def _head_matrices():
    lane = lax.broadcasted_iota(jnp.int32, (CB, CB), 0)
    col = lax.broadcasted_iota(jnp.int32, (CB, CB), 1)
    same = (lane // HD == col // HD).astype(BF16)
    lane_c = lax.broadcasted_iota(jnp.int32, (CB, LANES), 0)
    col_c = lax.broadcasted_iota(jnp.int32, (CB, LANES), 1)
    total = (lane_c // HD == col_c).astype(BF16)
    pick = (lane_c == col_c * HD).astype(BF16)
    return same, total, pick


def _head_sum(x, m_ref):
    return jnp.dot(x.astype(BF16), m_ref[...], preferred_element_type=F32)


def _dot_hilo(x, m_ref):
    hi = x.astype(BF16)
    lo = (x - hi.astype(F32)).astype(BF16)
    return (jnp.dot(hi, m_ref[...], preferred_element_type=F32)
            + jnp.dot(lo, m_ref[...], preferred_element_type=F32))


def _to_residue_major(val, buf, out_ref, dil):
    rows = out_ref.shape[1]
    for k in range(val.shape[1] // LANES):
        lanes = slice(k * LANES, (k + 1) * LANES)
        buf[k] = val[:, lanes]
        for r in range(dil):
            out_ref[r, :, lanes] = buf.at[k][pl.ds(r, rows, stride=dil), :].astype(out_ref.dtype)


def _from_residue_major(ref, buf, dil):
    if dil == 1:
        return ref[0].astype(F32)
    rows = ref.shape[1]
    for k in range(CB // LANES):
        for r in range(dil):
            buf.at[k][pl.ds(r, rows, stride=dil), :] = ref[r, :, k * LANES:(k + 1) * LANES].astype(F32)
    return jnp.concatenate([buf[k] for k in range(CB // LANES)], axis=1)


def qkv_prep(proj, qw8, kw8, same, tm):
    s = proj.shape[0]
    items = []
    for g, d in enumerate(DILATIONS):
        items += [(g, "q", CB_Q + g, d), (g, "k", CB_K + g, d)] + ([(g, "v", CB_V + g, d)] if d > 1 else [])
    n = len(items)

    def body(*refs):
        ins, (qw_ref, kw_ref, same_ref), outs, buf = refs[:n], refs[n:n + 3], refs[n + 3:2 * n + 3], refs[-1]
        for idx, (_, kind, _, dil) in enumerate(items):
            val = ins[idx][...].astype(F32)
            if kind != "v":
                r = lax.rsqrt(_head_sum(val * val, same_ref) * (1.0 / HD) + EPS)
                val = val * r * (qw_ref if kind == "q" else kw_ref)[...]
            if dil == 1:
                outs[idx][0] = val.astype(BF16)
            else:
                _to_residue_major(val, buf, outs[idx], dil)

    full = lambda a: pl.BlockSpec(a.shape, lambda i: (0, 0))
    outs = pl.pallas_call(
        body, name="qkv_prep", grid=(s // tm,),
        in_specs=[pl.BlockSpec((tm, CB), lambda i, cb=cb: (i, cb)) for _, _, cb, _ in items]
                 + [full(qw8), full(kw8), full(same)],
        out_specs=[pl.BlockSpec((d, tm // d, CB), lambda i: (0, i, 0)) for _, _, _, d in items],
        out_shape=[jax.ShapeDtypeStruct((d, s // d, CB), BF16) for _, _, _, d in items],
        scratch_shapes=[pltpu.VMEM((CB // LANES, tm, LANES), F32)],
        compiler_params=_cp(("parallel",)))(*([proj] * n), qw8 * (HD ** -0.5), kw8, same)
    srcs = [[None, None, (proj, CB_V + g)] for g in range(len(DILATIONS))]
    for (g, kind, _, _), o in zip(items, outs):
        srcs[g]["qkv".index(kind)] = (o.reshape(s, CB), 0)
    return srcs


def stats_prep(da, lc, dc, g, dil, tm):
    s = da.shape[0]
    rows = tm // dil

    def body(da_ref, lc_ref, dc_ref, dap_ref, lcp_ref, dcp_ref, lt_ref, dt_ref, buf):
        if dil == 1:
            dap_ref[0] = da_ref[...]
        else:
            _to_residue_major(da_ref[...].astype(F32), buf, dap_ref, dil)
        for src, dst, dst_t in ((lc_ref, lcp_ref, lt_ref), (dc_ref, dcp_ref, dt_ref)):
            buf[0] = src[...]
            for r in range(dil):
                piece = buf.at[0][pl.ds(r, rows, stride=dil), :] if dil > 1 else buf[0]
                dst[r] = piece
                dst_t[r] = piece.T[0:NH, :]

    row = lambda w: pl.BlockSpec((tm, w), lambda i: (i, 0))
    rm = lambda w: pl.BlockSpec((dil, rows, w), lambda i: (0, i, 0))
    tr = pl.BlockSpec((dil, NH, rows), lambda i: (0, 0, i))
    length = s // dil
    dap, lcp, dcp, lt, dt = pl.pallas_call(
        body, name=f"stats_prep_g{g}", grid=(s // tm,),
        in_specs=[row(CB), row(LANES), row(LANES)],
        out_specs=[rm(CB), rm(LANES), rm(LANES), tr, tr],
        out_shape=[jax.ShapeDtypeStruct((dil, length, CB), BF16)]
                  + [jax.ShapeDtypeStruct((dil, length, LANES), F32)] * 2
                  + [jax.ShapeDtypeStruct((dil, NH, length), F32)] * 2,
        scratch_shapes=[pltpu.VMEM((CB // LANES, tm, LANES), F32)],
        compiler_params=_cp(("parallel",)))(da, lc, dc)
    return (dap.reshape(s, CB), lcp.reshape(s, LANES), dcp.reshape(s, LANES),
            lt.reshape(dil * NH, length), dt.reshape(dil * NH, length))


def qkv_grads_to_dproj(dproj, proj, grads, qw8, kw8, same, tm):
    s = dproj.shape[0]
    ni = s // tm
    flat = [(t.reshape(d, s // d, CB), d, kind, 3 * kind + g)
            for g, d in enumerate(DILATIONS) for kind, t in enumerate(grads[g])]
    nf = len(flat)
    nraw = 2 * len(DILATIONS)

    def body(*refs):
        dp_hbm, raws, ins = refs[nraw + nf + 4], refs[1:1 + nraw], refs[1 + nraw:1 + nraw + nf]
        qw_ref, kw_ref, same_ref = refs[1 + nraw + nf:4 + nraw + nf]
        gw_ref, stage, buf, sems = refs[5 + nraw + nf:]
        i = pl.program_id(0)
        slot = i % 2

        def slab(step, sl):
            return pltpu.make_async_copy(
                stage.at[sl], dp_hbm.at[pl.ds(pl.multiple_of(step * tm, tm), tm), pl.ds(CB_Q * CB, 9 * CB)],
                sems.at[sl])

        @pl.when(i == 0)
        def _():
            gw_ref[...] = jnp.zeros_like(gw_ref)

        @pl.when(i >= 2)
        def _():
            slab(i - 2, slot).wait()

        for ref, (_, d, kind, jj) in zip(ins, flat):
            cols = slice(jj * CB, (jj + 1) * CB)
            dn = _from_residue_major(ref, buf, d)
            if kind == 2:
                stage[slot, :, cols] = dn.astype(BF16)
                continue
            t = raws[jj][...].astype(F32)
            r = lax.rsqrt(_head_sum(t * t, same_ref) * (1.0 / HD) + EPS)
            xh = t * r
            gw_ref[kind:kind + 1, :] += jnp.sum(dn * xh, axis=0, keepdims=True)
            dxh = dn * (qw_ref if kind == 0 else kw_ref)[...]
            mean = _head_sum(dxh * xh, same_ref) * (1.0 / HD)
            stage[slot, :, cols] = (r * (dxh - xh * mean)).astype(BF16)
        slab(i, slot).start()

        @pl.when(i == ni - 1)
        def _():
            slab(i - 1, 1 - slot).wait()
            slab(i, slot).wait()

    full = lambda a: pl.BlockSpec(a.shape, lambda i: (0, 0))
    any_spec = pl.BlockSpec(memory_space=pl.ANY)
    return pl.pallas_call(
        body, name="qkv_grads_to_dproj", grid=(ni,),
        in_specs=[any_spec] + [pl.BlockSpec((tm, CB), lambda i, jb=jb: (i, CB_Q + jb)) for jb in range(nraw)]
                 + [pl.BlockSpec((d, tm // d, CB), lambda i: (0, i, 0)) for _, d, _, _ in flat]
                 + [full(qw8), full(kw8), full(same)],
        out_specs=[any_spec, pl.BlockSpec((8, CB), lambda i: (0, 0))],
        out_shape=[jax.ShapeDtypeStruct((s, NIN), BF16), jax.ShapeDtypeStruct((8, CB), F32)],
        input_output_aliases={0: 0},
        scratch_shapes=[pltpu.VMEM((2, tm, 9 * CB), BF16), pltpu.VMEM((CB // LANES, tm, LANES), F32),
                        pltpu.SemaphoreType.DMA((2,))],
        compiler_params=_cp(("arbitrary",)))(
            dproj, *([proj] * nraw), *[t for t, _, _, _ in flat], qw8, kw8, same)


def _lane_lo():
    return lax.broadcasted_iota(jnp.int32, (1, 2 * HD), 1) < HD


def _stack_heads(t, lo):
    zero = jnp.zeros_like(t)
    return jnp.concatenate([jnp.where(lo, t, zero), jnp.where(lo, zero, t)], axis=0)


def _masks(other_ok):
    qi = lax.broadcasted_iota(jnp.int32, (QB, QB), 0)
    kj = lax.broadcasted_iota(jnp.int32, (QB, QB), 1)
    return (kj >= qi) & other_ok, kj <= qi


SUB = 4


def _attn_specs(nb, dil):
    steps = nb // SUB
    main = lambda cb, w=CB: pl.BlockSpec((SUB * QB, w), lambda r, s: (r * steps + s, cb))
    prev = lambda cb: pl.BlockSpec((QB, CB), lambda r, s: (jnp.maximum(r * nb + SUB * s - 1, 0), cb))
    nxt = lambda cb: pl.BlockSpec((QB, CB), lambda r, s: (jnp.minimum(r * nb + SUB * (s + 1), dil * nb - 1), cb))
    return main, prev, nxt


def attn_fwd(q_src, k_src, v_src, g, dil):
    s = q_src[0].shape[0]
    nb = s // dil // QB
    main, prev, _ = _attn_specs(nb, dil)

    def body(q_ref, kp_ref, k_ref, vp_ref, v_ref, o_ref, l_ref, kbuf, vbuf):
        step = pl.program_id(1)
        kbuf[0:QB], kbuf[QB:] = kp_ref[...], k_ref[...]
        vbuf[0:QB], vbuf[QB:] = vp_ref[...], v_ref[...]
        lo = _lane_lo()

        def block(j, carry):
            r0 = pl.multiple_of(j * QB, QB)
            rows, krows = pl.ds(r0, QB), pl.ds(r0, 2 * QB)
            m_prev, m_cur = _masks(step * SUB + j > 0)
            mask = jnp.concatenate([m_prev, m_cur], axis=1)
            mask = jnp.concatenate([mask, mask], axis=0)
            for i in range(NH // 2):
                sl = slice(2 * HD * i, 2 * HD * (i + 1))
                qs, ks, vv = q_ref[rows, sl], kbuf[krows, sl], vbuf[krows, sl]
                sc = lax.dot_general(_stack_heads(qs, lo), ks, NT, preferred_element_type=F32)
                sc = jnp.where(mask, sc, NEG)
                mx = jnp.max(sc, axis=-1, keepdims=True)
                p = jnp.exp(sc - mx)
                den = jnp.sum(p, axis=-1, keepdims=True)
                o = jnp.dot(p.astype(BF16), vv, preferred_element_type=F32) * (1.0 / den)
                lse = jnp.broadcast_to(mx + jnp.log(den), (2 * QB, 2 * HD))
                o_ref[rows, sl] = jnp.where(lo, o[:QB], o[QB:])
                l_ref[rows, sl] = jnp.where(lo, lse[:QB], lse[QB:])
            return carry

        lax.fori_loop(0, SUB, block, 0, unroll=True)

    out = jax.ShapeDtypeStruct((s, CB), F32)
    return pl.pallas_call(
        body, name=f"attn_fwd_g{g}", grid=(dil, nb // SUB),
        in_specs=[main(q_src[1]), prev(k_src[1]), main(k_src[1]), prev(v_src[1]), main(v_src[1])],
        out_specs=[main(0)] * 2, out_shape=[out, out],
        scratch_shapes=[pltpu.VMEM(((SUB + 1) * QB, CB), BF16)] * 2,
        compiler_params=_cp(("parallel", "parallel")))(q_src[0], k_src[0], k_src[0], v_src[0], v_src[0])


def attn_bwd_q(q_src, k_src, v_src, da, lc, dc, g, dil):
    s = q_src[0].shape[0]
    nb = s // dil // QB
    main, prev, _ = _attn_specs(nb, dil)

    def body(q_ref, kp_ref, k_ref, vp_ref, v_ref, da_ref, l_ref, d_ref, dq_ref, kbuf, vbuf):
        step = pl.program_id(1)
        kbuf[0:QB], kbuf[QB:] = kp_ref[...], k_ref[...]
        vbuf[0:QB], vbuf[QB:] = vp_ref[...], v_ref[...]
        lo = _lane_lo()

        def block(j, carry):
            r0 = pl.multiple_of(j * QB, QB)
            rows, krows = pl.ds(r0, QB), pl.ds(r0, 2 * QB)
            m_prev, m_cur = _masks(step * SUB + j > 0)
            mask = jnp.concatenate([m_prev, m_cur], axis=1)
            mask = jnp.concatenate([mask, mask], axis=0)
            lcols, dcols = l_ref[rows, :], d_ref[rows, :]
            for i in range(NH // 2):
                sl = slice(2 * HD * i, 2 * HD * (i + 1))
                qs, ks, vv, da2 = q_ref[rows, sl], kbuf[krows, sl], vbuf[krows, sl], da_ref[rows, sl]
                pair = lambda t: jnp.concatenate([t[:, 2 * i:2 * i + 1], t[:, 2 * i + 1:2 * i + 2]], axis=0)
                sc = lax.dot_general(_stack_heads(qs, lo), ks, NT, preferred_element_type=F32)
                sc = jnp.where(mask, sc, NEG)
                p = jnp.exp(sc - pair(lcols))
                dp = lax.dot_general(_stack_heads(da2, lo), vv, NT, preferred_element_type=F32)
                ds = p * (dp - pair(dcols))
                dq = jnp.dot(ds.astype(BF16), ks, preferred_element_type=F32)
                dq_ref[rows, sl] = (jnp.where(lo, dq[:QB], dq[QB:]) * (HD ** -0.5)).astype(BF16)
            return carry

        lax.fori_loop(0, SUB, block, 0, unroll=True)

    return pl.pallas_call(
        body, name=f"attn_bwd_q_g{g}", grid=(dil, nb // SUB),
        in_specs=[main(q_src[1]), prev(k_src[1]), main(k_src[1]), prev(v_src[1]), main(v_src[1]),
                  main(0), main(0, LANES), main(0, LANES)],
        out_specs=main(0), out_shape=jax.ShapeDtypeStruct((s, CB), BF16),
        scratch_shapes=[pltpu.VMEM(((SUB + 1) * QB, CB), BF16)] * 2,
        compiler_params=_cp(("parallel", "parallel")))(
            q_src[0], k_src[0], k_src[0], v_src[0], v_src[0], da, lc, dc)


def attn_bwd_kv(q_src, k_src, v_src, da, lt, dt, g, dil):
    s = q_src[0].shape[0]
    nb = s // dil // QB
    main, _, nxt = _attn_specs(nb, dil)

    def body(k_ref, v_ref, q_ref, qn_ref, da_ref, dan_ref, l_ref, ln_ref, d_ref, dn_ref, dk_ref, dv_ref,
             qbuf, dabuf, lbuf, dbuf):
        step = pl.program_id(1)
        qbuf[0:SUB * QB], qbuf[SUB * QB:] = q_ref[...], qn_ref[...]
        dabuf[0:SUB * QB], dabuf[SUB * QB:] = da_ref[...], dan_ref[...]
        for c in range(SUB):
            lbuf[c], dbuf[c] = l_ref[:, c * QB:(c + 1) * QB], d_ref[:, c * QB:(c + 1) * QB]
        lbuf[SUB], dbuf[SUB] = ln_ref[...], dn_ref[...]
        lo = _lane_lo()
        kj = lax.broadcasted_iota(jnp.int32, (QB, QB), 0)
        qi = lax.broadcasted_iota(jnp.int32, (QB, QB), 1)

        def block(j, carry):
            r0 = pl.multiple_of(j * QB, QB)
            rows, qrows = pl.ds(r0, QB), pl.ds(r0, 2 * QB)
            mask = jnp.concatenate([kj <= qi, (kj >= qi) & (step * SUB + j < nb - 1)], axis=1)
            mask = jnp.concatenate([mask, mask], axis=1)
            lrow = jnp.concatenate([lbuf[j], lbuf[j + 1]], axis=1)
            drow = jnp.concatenate([dbuf[j], dbuf[j + 1]], axis=1)
            for i in range(NH // 2):
                sl = slice(2 * HD * i, 2 * HD * (i + 1))
                q2, da2 = _stack_heads(qbuf[qrows, sl], lo), _stack_heads(dabuf[qrows, sl], lo)
                ks, vv = k_ref[rows, sl], v_ref[rows, sl]
                pair = lambda t: jnp.concatenate([t[2 * i:2 * i + 1, :], t[2 * i + 1:2 * i + 2, :]], axis=1)
                sc = lax.dot_general(ks, q2, NT, preferred_element_type=F32)
                sc = jnp.where(mask, sc, NEG)
                p = jnp.exp(sc - pair(lrow))
                dp = lax.dot_general(vv, da2, NT, preferred_element_type=F32)
                ds = p * (dp - pair(drow))
                dv_ref[rows, sl] = jnp.dot(p.astype(BF16), da2, preferred_element_type=F32).astype(BF16)
                dk_ref[rows, sl] = jnp.dot(ds.astype(BF16), q2, preferred_element_type=F32).astype(BF16)
            return carry

        lax.fori_loop(0, SUB, block, 0, unroll=True)

    steps = nb // SUB
    t_main = pl.BlockSpec((NH, SUB * QB), lambda r, s: (r, s))
    t_nxt = pl.BlockSpec((NH, QB), lambda r, s: (r, jnp.minimum(SUB * (s + 1), nb - 1)))
    out = jax.ShapeDtypeStruct((s, CB), BF16)
    return pl.pallas_call(
        body, name=f"attn_bwd_kv_g{g}", grid=(dil, steps),
        in_specs=[main(k_src[1]), main(v_src[1]), main(q_src[1]), nxt(q_src[1]),
                  main(0), nxt(0), t_main, t_nxt, t_main, t_nxt],
        out_specs=[main(0), main(0)], out_shape=[out, out],
        scratch_shapes=[pltpu.VMEM(((SUB + 1) * QB, CB), BF16)] * 2 + [pltpu.VMEM((SUB + 1, NH, QB), F32)] * 2,
        compiler_params=_cp(("parallel", "parallel")))(
            k_src[0], v_src[0], q_src[0], q_src[0], da, da, lt, lt, dt, dt)


def _conv_taps(u, u_prev, first):
    tm = u.shape[0]
    row = lax.broadcasted_iota(jnp.int32, (tm, 1), 0)
    up = jnp.where(first, 0.0, u_prev)
    u1 = jnp.where(row == 0, up[HALO - 1:HALO, :], pltpu.roll(u, 1, 0))
    u2 = jnp.where(row == 0, up[HALO - 2:HALO - 1, :],
                   jnp.where(row == 1, up[HALO - 1:HALO, :], pltpu.roll(u, 2, 0)))
    return u1, u2


def mid_fwd(proj, o_g, lse_g, conv_w, pick, tm):
    s = proj.shape[0]
    hb = tm // HALO

    def body(ba_ref, ca_ref, xa_ref, za_ref, cah_ref, xah_ref, zb_ref,
             o0, o1, o2, l0, l1, l2, w_ref, pick_ref, ya_ref, yb_ref, at_ref, lc_ref, buf_o, buf_l):
        first = pl.program_id(0) == 0
        u = ca_ref[...].astype(F32) * xa_ref[...].astype(F32)
        u1, u2 = _conv_taps(u, cah_ref[...].astype(F32) * xah_ref[...].astype(F32), first)
        conv = w_ref[0:1, :] * u2 + w_ref[1:2, :] * u1 + w_ref[2:3, :] * u
        ya_ref[...] = (ba_ref[...].astype(F32) * conv * _silu(za_ref[...].astype(F32))).astype(BF16)
        ls = [_from_residue_major(l, buf_l.at[g], d) for g, (l, d) in enumerate(zip((l0, l1, l2), DILATIONS))]
        mx = jnp.maximum(jnp.maximum(ls[0], ls[1]), ls[2])
        es = [jnp.exp(l - mx) for l in ls]
        den = es[0] + es[1] + es[2]
        num = jnp.zeros_like(den)
        for e, o, d in zip(es, (o0, o1, o2), DILATIONS):
            num = num + e * _from_residue_major(o, buf_o, d)
        attn = num / den
        at_ref[...] = attn
        lc_ref[...] = _dot_hilo(mx + jnp.log(den), pick_ref)
        yb_ref[...] = (attn * _silu(zb_ref[...].astype(F32))).astype(BF16)

    col = lambda j: pl.BlockSpec((tm, D), lambda i: (i, j))
    halo = lambda j: pl.BlockSpec((HALO, D), lambda i: (jnp.maximum(i * hb - 1, 0), j))
    loc = lambda w: pl.BlockSpec((tm, w), lambda i: (i, 0))
    rm = [pl.BlockSpec((d, tm // d, CB), lambda i: (0, i, 0)) for d in DILATIONS]
    rm_view = lambda ts: [t.reshape(d, s // d, CB) for t, d in zip(ts, DILATIONS)]
    return pl.pallas_call(
        body, name="mid_fwd", grid=(s // tm,),
        in_specs=[col(0), col(1), col(2), col(3), halo(1), halo(2),
                  pl.BlockSpec((tm, CB), lambda i: (i, CB_ZB))] + rm + rm
                 + [pl.BlockSpec((3, D), lambda i: (0, 0)), pl.BlockSpec(pick.shape, lambda i: (0, 0))],
        out_specs=[loc(D), loc(CB), loc(CB), loc(LANES)],
        out_shape=[jax.ShapeDtypeStruct((s, D), BF16), jax.ShapeDtypeStruct((s, CB), BF16),
                   jax.ShapeDtypeStruct((s, CB), F32), jax.ShapeDtypeStruct((s, LANES), F32)],
        scratch_shapes=[pltpu.VMEM((CB // LANES, tm, LANES), F32), pltpu.VMEM((3, CB // LANES, tm, LANES), F32)],
        compiler_params=_cp(("parallel",)))(
            proj, proj, proj, proj, proj, proj, proj, *rm_view(o_g), *rm_view(lse_g), conv_w, pick)


def mid_bwd(dproj, proj, dya, conv_w, tm):
    s = proj.shape[0]
    hb = tm // HALO
    nblk = s // tm
    last_h = s // HALO - 1

    def body(_, ba_ref, ca_ref, xa_ref, za_ref, cah_ref, xah_ref, ban_ref, zan_ref, dy_ref, dyn_ref, w_ref,
             o_ref, gw_ref):
        i = pl.program_id(0)
        ba, ca, xa, za = (t[...].astype(F32) for t in (ba_ref, ca_ref, xa_ref, za_ref))
        u = ca * xa
        u1, u2 = _conv_taps(u, cah_ref[...].astype(F32) * xah_ref[...].astype(F32), i == 0)
        w0, w1, w2 = w_ref[0:1, :], w_ref[1:2, :], w_ref[2:3, :]
        conv = w0 * u2 + w1 * u1 + w2 * u
        sg = jax.nn.sigmoid(za)
        sz = za * sg
        dy = dy_ref[...].astype(F32)
        dconv = dy * ba * sz
        dcn = dyn_ref[...].astype(F32) * ban_ref[...].astype(F32) * _silu(zan_ref[...].astype(F32))
        dcn = jnp.where(i == nblk - 1, 0.0, dcn)
        row = lax.broadcasted_iota(jnp.int32, (tm, 1), 0)
        d1 = jnp.where(row == tm - 1, dcn[0:1, :], pltpu.roll(dconv, tm - 1, 0))
        d2 = jnp.where(row == tm - 2, dcn[0:1, :],
                       jnp.where(row == tm - 1, dcn[1:2, :], pltpu.roll(dconv, tm - 2, 0)))
        du = w2 * dconv + w1 * d1 + w0 * d2
        o_ref[:, 0:D] = (dy * conv * sz).astype(BF16)
        o_ref[:, D:2 * D] = (du * xa).astype(BF16)
        o_ref[:, 2 * D:3 * D] = (du * ca).astype(BF16)
        o_ref[:, 3 * D:4 * D] = (dy * ba * conv * (sg * (1.0 + za * (1.0 - sg)))).astype(BF16)

        @pl.when(i == 0)
        def _():
            gw_ref[...] = jnp.zeros_like(gw_ref)

        gw_ref[0:1, :] += jnp.sum(dconv * u2, axis=0, keepdims=True)
        gw_ref[1:2, :] += jnp.sum(dconv * u1, axis=0, keepdims=True)
        gw_ref[2:3, :] += jnp.sum(dconv * u, axis=0, keepdims=True)

    col = lambda j: pl.BlockSpec((tm, D), lambda i: (i, j))
    halo_prev = lambda j: pl.BlockSpec((HALO, D), lambda i: (jnp.maximum(i * hb - 1, 0), j))
    halo_next = lambda j: pl.BlockSpec((HALO, D), lambda i: (jnp.minimum((i + 1) * hb, last_h), j))
    return pl.pallas_call(
        body, name="mid_bwd", grid=(nblk,),
        in_specs=[pl.BlockSpec(memory_space=pl.ANY), col(0), col(1), col(2), col(3),
                  halo_prev(1), halo_prev(2), halo_next(0), halo_next(3),
                  pl.BlockSpec((tm, D), lambda i: (i, 0)), halo_next(0),
                  pl.BlockSpec((3, D), lambda i: (0, 0))],
        out_specs=[pl.BlockSpec((tm, 4 * D), lambda i: (i, 0)), pl.BlockSpec((8, D), lambda i: (0, 0))],
        out_shape=[jax.ShapeDtypeStruct((s, NIN), BF16), jax.ShapeDtypeStruct((8, D), F32)],
        input_output_aliases={0: 0},
        compiler_params=_cp(("arbitrary",)))(dproj, proj, proj, proj, proj, proj, proj, proj, proj, dya, dya, conv_w)


def tail(proj, ya, yb, attn, x, target, gate, pa_w, pb_w, wo_w, total, tm):
    s = proj.shape[0]
    ni = s // tm
    ncol = NIN - CB_ZB * CB

    def body(ya_ref, yb_ref, ga_ref, gb_ref, zb_ref, at_ref, x_ref, t_ref, gate_ref, pa_ref, pb_ref, wo_ref,
             tot_ref, dp_hbm, dy_ref, dya_ref, da_ref, dc_ref, mg_ref, do_ref, dpa_ref, dpb_ref, st_ref,
             stage, sems):
        i = pl.program_id(0)
        slot = i % 2

        def slab(step, sl):
            return pltpu.make_async_copy(
                stage.at[sl], dp_hbm.at[pl.ds(pl.multiple_of(step * tm, tm), tm), pl.ds(CB_ZB * CB, ncol)],
                sems.at[sl])

        @pl.when(i == 0)
        def _():
            st_ref[...] = jnp.zeros_like(st_ref)

        @pl.when(i >= 2)
        def _():
            slab(i - 2, slot).wait()

        gate_v = gate_ref[...]
        pa = jnp.dot(ya_ref[...], pa_ref[...], preferred_element_type=F32)
        pb = jnp.dot(yb_ref[...], pb_ref[...], preferred_element_type=F32)
        sa = jax.nn.sigmoid(ga_ref[...].astype(F32))
        sb = jax.nn.sigmoid(gb_ref[...].astype(F32))
        merged = (sa * pa + sb * pb).astype(BF16)
        mg_ref[...] = merged
        out = jnp.dot(merged, wo_ref[...], preferred_element_type=F32)
        err = x_ref[...] + gate_v * out - t_ref[...]
        dy = err * (1.0 / D)
        dy_ref[...] = dy
        st_ref[0:1, :] += jnp.sum(dy * out, axis=0, keepdims=True)
        st_ref[1:2, :] += jnp.sum(err * err, axis=0, keepdims=True)
        dout = (gate_v * dy).astype(BF16)
        do_ref[...] = dout
        dmg = lax.dot_general(dout, wo_ref[...], NT, preferred_element_type=F32)
        dpa = (dmg * sa).astype(BF16)
        dpb = (dmg * sb).astype(BF16)
        dpa_ref[...] = dpa
        dpb_ref[...] = dpb
        stage[slot, :, CB:CB + D] = (dmg * pa * sa * (1.0 - sa)).astype(BF16)
        stage[slot, :, CB + D:] = (dmg * pb * sb * (1.0 - sb)).astype(BF16)
        dya_ref[...] = lax.dot_general(dpa, pa_ref[...], NT, preferred_element_type=F32).astype(BF16)
        dyb = lax.dot_general(dpb, pb_ref[...], NT, preferred_element_type=F32)
        zb = zb_ref[...].astype(F32)
        sg = jax.nn.sigmoid(zb)
        attn_v = at_ref[...]
        dattn = dyb * (zb * sg)
        da_ref[...] = dattn.astype(BF16)
        stage[slot, :, 0:CB] = (dyb * attn_v * (sg * (1.0 + zb * (1.0 - sg)))).astype(BF16)
        dc_ref[...] = _dot_hilo(dattn * attn_v, tot_ref)

        slab(i, slot).start()

        @pl.when(i == ni - 1)
        def _():
            slab(i - 1, 1 - slot).wait()
            slab(i, slot).wait()

    row = lambda w: pl.BlockSpec((tm, w), lambda i: (i, 0))
    pcol = lambda w, jb: pl.BlockSpec((tm, w), lambda i: (i, jb))
    full = lambda a: pl.BlockSpec(a.shape, lambda i: (0, 0))
    return pl.pallas_call(
        body, name="tail", grid=(ni,),
        in_specs=[row(D), row(CB), pcol(D, 9), pcol(D, 10), pcol(CB, CB_ZB), row(CB), row(D), row(D),
                  pl.BlockSpec((1, D), lambda i: (0, 0)), full(pa_w), full(pb_w), full(wo_w), full(total)],
        out_specs=[pl.BlockSpec(memory_space=pl.ANY),
                   row(D), row(D), row(CB), row(LANES), row(D), row(D), row(D), row(D),
                   pl.BlockSpec((8, D), lambda i: (0, 0))],
        out_shape=[jax.ShapeDtypeStruct((s, NIN), BF16), jax.ShapeDtypeStruct((s, D), F32),
                   jax.ShapeDtypeStruct((s, D), BF16), jax.ShapeDtypeStruct((s, CB), BF16),
                   jax.ShapeDtypeStruct((s, LANES), F32)] + [jax.ShapeDtypeStruct((s, D), BF16)] * 4
                  + [jax.ShapeDtypeStruct((8, D), F32)],
        scratch_shapes=[pltpu.VMEM((2, tm, ncol), BF16), pltpu.SemaphoreType.DMA((2,))],
        compiler_params=_cp(("arbitrary",), 56))(
            ya, yb, proj, proj, proj, attn, x, target, gate, pa_w, pb_w, wo_w, total)


def _local_step(x, target, shift, scale, gate, norm_w, conv_w, qw, kw, w_shard, pa_w, pb_w, wo_w, me_xyc):
    qw8, kw8 = jnp.tile(qw, (1, NH)), jnp.tile(kw, (1, NH))
    same, total, pick = _head_matrices()
    h, ht = norm_fwd(x, norm_w, scale, shift, 512)
    proj, wg = proj_fwd_gather(h, w_shard, gather_order(me_xyc), 1024)
    srcs = qkv_prep(proj, qw8, kw8, same, 512)
    o_g, lse_g = zip(*[attn_fwd(*srcs[g], g, d) for g, d in enumerate(DILATIONS)])
    ya, yb, attn, lc = mid_fwd(proj, o_g, lse_g, conv_w, pick, 512)
    dproj, dy, dya, da, dc, merged, dout, dpa, dpb, st_tail = tail(
        proj, ya, yb, attn, x, target, gate, pa_w, pb_w, wo_w, total, 256)
    g_wo = matmul_tn(merged, dout, "grad_w_out", 1024)
    g_pa = matmul_tn(ya, dpa, "grad_w_br_conv", 1024)
    g_pb = matmul_tn(yb, dpb, "grad_w_br_attn", 1024)
    dproj, st_conv = mid_bwd(dproj, proj, dya, conv_w, 512)
    grads = []
    for g, d in enumerate(DILATIONS):
        da_p, lc_p, dc_p, lt, dt = stats_prep(da, lc, dc, g, d, 2048)
        dq = attn_bwd_q(*srcs[g], da_p, lc_p, dc_p, g, d)
        dk, dv = attn_bwd_kv(*srcs[g], da_p, lt, dt, g, d)
        grads.append((dq, dk, dv))
    dproj, gw_qk = qkv_grads_to_dproj(dproj, proj, grads, qw8, kw8, same, 512)
    slabs = [g_pa.reshape(NDEV, 128, D), g_pb.reshape(CB, NDEV, 128).transpose(1, 0, 2), g_wo.reshape(NDEV, 128, D)]
    dh, r_win, (r_pa, r_pb, r_wo) = proj_bwd(ht, dproj, wg, slabs, scatter_order(me_xyc), 1024)
    grad_x, st_norm = norm_bwd(dh, x, dy, norm_w, scale, 512)
    dmod = jnp.concatenate([st_norm[0:1], st_norm[1:2], st_tail[0:1]], axis=1)
    loss_part = (0.5 / D) * jnp.sum(st_tail[1])
    gw_heads = gw_qk[0:2].reshape(2, NH, HD).sum(axis=1)
    small = dict(dmod=dmod, norm_w=st_norm[2:3], conv_w=st_conv[0:3],
                 q_norm_w=gw_heads[0:1], k_norm_w=gw_heads[1:2], loss=loss_part)
    return grad_x, small, (r_win, r_pa, r_pb, r_wo)


def kernel(x, c, w_ada, b_ada, norm_w, w_in, conv_w, q_norm_w, k_norm_w, w_br_conv, w_br_attn, w_out, loss_target, m_w_ada, m_b_ada, m_norm_w, m_w_in, m_conv_w, m_q_norm_w, m_k_norm_w, m_w_br_conv, m_w_br_attn, m_w_out, v_w_ada, v_b_ada, v_norm_w, v_w_in, v_conv_w, v_q_norm_w, v_k_norm_w, v_w_br_conv, v_w_br_attn, v_w_out):
    me_xyc = (lax.axis_index("x"), lax.axis_index("y"), lax.axis_index("c"))
    me = _dev_index(me_xyc)
    ncol = w_ada.shape[2]

    conv_pad = jnp.zeros((8, 128), F32).at[0:3].set(conv_w[0])
    pa_g, pb_g, wo_g, c_all, conv_all = all_gather(
        [w_br_conv[0].astype(BF16), w_br_attn[0].astype(BF16), w_out[0].astype(BF16), c, conv_pad],
        "gather_weights")
    pa_w = pa_g.reshape(D, D)
    wo_w = wo_g.reshape(D, D)
    pb_w = pb_g.transpose(1, 0, 2).reshape(CB, D)
    conv_full = conv_all[:, 0:3].transpose(1, 0, 2).reshape(3, D)
    c_all = c_all.reshape(NDEV, D)

    b_cols = lax.dynamic_slice(b_ada, (0, me * ncol), (1, ncol))
    mod_cols = ada_fwd(c_all, w_ada[0], b_cols)
    (mod_all,) = all_gather([mod_cols], "gather_mod")
    mod = lax.dynamic_index_in_dim(mod_all, me, axis=1, keepdims=False).reshape(1, 3 * D)
    shift, scale, gate = mod[:, 0:D], mod[:, D:2 * D], mod[:, 2 * D:3 * D]

    grad_x, small, (r_win, r_pa, r_pb, r_wo) = _local_step(
        x[0], loss_target[0], shift, scale, gate, norm_w, conv_full, q_norm_w, k_norm_w,
        w_in[0].astype(BF16), pa_w, pb_w, wo_w, me_xyc)

    packed = jnp.concatenate(
        [small["dmod"], small["norm_w"], small["conv_w"].reshape(1, 3 * D), small["q_norm_w"], small["k_norm_w"],
         jnp.full((1, 128), small["loss"], F32)], axis=1)
    (packed_all,) = all_gather([packed], "gather_small")
    tot = sum_parts(packed_all)
    loss = tot[0, 7 * D + 2 * HD]
    dmod_all = packed_all[:, 0, 0:3 * D]
    g_b_ada = tot[:, 0:3 * D]
    g_norm_w = tot[:, 3 * D:4 * D]
    g_conv = lax.dynamic_slice(tot[:, 4 * D:7 * D].reshape(3, D), (0, me * 128), (3, 128))
    g_qn = tot[:, 7 * D:7 * D + HD]
    g_kn = tot[:, 7 * D + HD:7 * D + 2 * HD]
    g_w_ada = ada_bwd(c_all.T, lax.dynamic_slice(dmod_all, (0, me * ncol), (NDEV, ncol)))

    def upd(parts, w, m, v, name, rows):
        shape = w.shape
        w2, m2, v2 = (t.reshape(shape[-2:]) for t in (w, m, v))
        return [t.reshape(shape) for t in adamw(parts, w2, m2, v2, name, rows)]

    res = {
        "w_ada": upd(g_w_ada[None], w_ada, m_w_ada, v_w_ada, "adamw_w_ada", 256),
        "b_ada": upd(g_b_ada[None], b_ada, m_b_ada, v_b_ada, "adamw_b_ada", 1),
        "norm_w": upd(g_norm_w[None], norm_w, m_norm_w, v_norm_w, "adamw_norm_w", 1),
        "w_in": upd(r_win, w_in, m_w_in, v_w_in, "adamw_w_in", 128),
        "conv_w": upd(g_conv[None], conv_w, m_conv_w, v_conv_w, "adamw_conv_w", 3),
        "q_norm_w": upd(g_qn[None], q_norm_w, m_q_norm_w, v_q_norm_w, "adamw_q_norm_w", 1),
        "k_norm_w": upd(g_kn[None], k_norm_w, m_k_norm_w, v_k_norm_w, "adamw_k_norm_w", 1),
        "w_br_conv": upd(r_pa, w_br_conv, m_w_br_conv, v_w_br_conv, "adamw_w_br_conv", 128),
        "w_br_attn": upd(r_pb, w_br_attn, m_w_br_attn, v_w_br_attn, "adamw_w_br_attn", 512),
        "w_out": upd(r_wo, w_out, m_w_out, v_w_out, "adamw_w_out", 128),
    }
    names = ["w_ada", "b_ada", "norm_w", "w_in", "conv_w", "q_norm_w", "k_norm_w", "w_br_conv", "w_br_attn", "w_out"]
    return (loss, grad_x[None], *[res[n][0] for n in names], *[res[n][1] for n in names],
            *[res[n][2] for n in names], *[res[n][3] for n in names])
```

```python
import jax
import jax.numpy as jnp
from jax import lax
from jax.experimental import pallas as pl
from jax.experimental.pallas import tpu as pltpu

F32, BF16 = jnp.float32, jnp.bfloat16
D = 1024
NIN = 11264
NDEV = 8
SHARD = NIN // NDEV
HD = 64
NH = 8
QB = 128
CB = 512
CB_Q, CB_K, CB_V, CB_ZB = 8, 11, 14, 17
DILATIONS = (1, 4, 16)
EPS = 1e-6
NEG = -1e30
HALO = 16
LANES = 128
MESH = pl.DeviceIdType.MESH

ADAM_LR, ADAM_B1, ADAM_B2, ADAM_EPS, ADAM_WD, ADAM_STEP = 0.001, 0.9, 0.999, 1e-08, 0.01, 10

NT = (((1,), (1,)), ((), ()))
TN = (((0,), (0,)), ((), ()))


def _cp(sem, vmem_mb=48):
    return pltpu.CompilerParams(dimension_semantics=sem, vmem_limit_bytes=vmem_mb << 20)


def _silu(z):
    return z * jax.nn.sigmoid(z)


def _coords():
    return lax.axis_index("x"), lax.axis_index("y"), lax.axis_index("c")


def all_gather(arrs, name):
    n = len(arrs)

    def body(*refs):
        ins, outs = refs[:n], refs[n:2 * n]
        send_sems, recv_sems, local_sems = refs[2 * n:]
        x, y, c = _coords()
        me, sibling = (x, y, c), (x, y, 1 - c)
        chips = [(1 - x, y), (x, 1 - y), (1 - x, 1 - y)]

        def slot(a, dev):
            return outs[a].at[4 * dev[0] + 2 * dev[1] + dev[2]]

        def copy(a, k, block, to, src=None):
            return pltpu.make_async_remote_copy(
                src_ref=slot(a, block) if src is None else src, dst_ref=slot(a, block),
                send_sem=send_sems.at[a, k], recv_sem=recv_sems.at[a, k],
                device_id=to, device_id_type=MESH)

        mine = [pltpu.make_async_copy(ins[a], slot(a, me), local_sems.at[a]) for a in range(n)]
        for cp in mine:
            cp.start()
        first = []
        for a in range(n):
            first.append(copy(a, 0, me, sibling, src=ins[a]))
            first += [copy(a, 1 + j, me, (*chip, c), src=ins[a]) for j, chip in enumerate(chips)]
        for cp in first:
            cp.start()
        passed = []
        for j, chip in enumerate(chips):
            for a in range(n):
                copy(a, 1 + j, (*chip, c), me).wait_recv()
                fwd = copy(a, 4 + j, (*chip, c), sibling)
                fwd.start()
                passed.append(fwd)
        for a in range(n):
            copy(a, 0, sibling, me).wait_recv()
            for j, chip in enumerate(chips):
                copy(a, 4 + j, (*chip, 1 - c), me).wait_recv()
        for cp in first + passed:
            cp.wait_send()
        for cp in mine:
            cp.wait()

    any_spec = pl.BlockSpec(memory_space=pl.ANY)
    return pl.pallas_call(
        body, name=name,
        out_shape=[jax.ShapeDtypeStruct((NDEV,) + a.shape, a.dtype) for a in arrs],
        in_specs=[any_spec] * n, out_specs=[any_spec] * n,
        scratch_shapes=[pltpu.SemaphoreType.DMA((n, 7)), pltpu.SemaphoreType.DMA((n, 7)),
                        pltpu.SemaphoreType.DMA((n,))],
    )(*arrs)


FLIPS = [(fx, fy, fc) for fx in (0, 1) for fy in (0, 1) for fc in (0, 1)][1:]


def _flip(dev, f):
    return tuple(1 - v if b else v for v, b in zip(dev, f))


def _dev_index(dev):
    return 4 * dev[0] + 2 * dev[1] + dev[2]


def _chip_order(x, y, c):
    xor = lambda a, b: a + b - 2 * a * b
    return [(xor(x, 1 - c), xor(y, c)), (xor(x, c), xor(y, 1 - c)), (1 - x, 1 - y)]


def gather_order(me_xyc):
    x, y, c = me_xyc
    chips = _chip_order(x, y, c)
    devs = ([(x, y, c), (x, y, 1 - c)] + [(*ch, c) for ch in chips]
            + [(*chips[1], 1 - c), (*chips[0], 1 - c), (*chips[2], 1 - c)])
    return jnp.stack([_dev_index(d) for d in devs]).astype(jnp.int32)


def scatter_order(me_xyc):
    devs = [_flip(me_xyc, f) for f in FLIPS] + [me_xyc]
    return jnp.stack([_dev_index(d) for d in devs]).astype(jnp.int32)


def ada_fwd(c_all, w_ada, b_cols):
    def body(c_ref, w_ref, b_ref, o_ref):
        a = _silu(c_ref[...]).astype(BF16)
        o_ref[...] = jnp.dot(a, w_ref[...].astype(BF16), preferred_element_type=F32) + b_ref[...]

    return pl.pallas_call(body, name="ada_fwd",
                          out_shape=jax.ShapeDtypeStruct((NDEV, w_ada.shape[1]), F32))(c_all, w_ada, b_cols)


def ada_bwd(c_all_t, dmod_cols):
    def body(c_ref, d_ref, o_ref):
        at = _silu(c_ref[...])
        acc = at[:, 0:1] * d_ref[0:1, :]
        for b in range(1, NDEV):
            acc = acc + at[:, b:b + 1] * d_ref[b:b + 1, :]
        o_ref[...] = acc

    return pl.pallas_call(body, name="ada_bwd",
                          out_shape=jax.ShapeDtypeStruct((D, dmod_cols.shape[1]), F32))(c_all_t, dmod_cols)


def sum_parts(parts):
    def body(p_ref, o_ref):
        acc = p_ref[0]
        for b in range(1, NDEV):
            acc = acc + p_ref[b]
        o_ref[...] = acc

    return pl.pallas_call(body, name="sum_parts",
                          out_shape=jax.ShapeDtypeStruct(parts.shape[1:], F32))(parts)


def adamw(parts, w, m, v, name, rows):
    n, r, ccols = parts.shape

    def body(p_ref, w_ref, m_ref, v_ref, g_ref, d_ref, nm_ref, nv_ref):
        g = p_ref[0].astype(F32)
        for b in range(1, n):
            g = g + p_ref[b].astype(F32)
        nm = ADAM_B1 * m_ref[...] + (1.0 - ADAM_B1) * g
        nv = ADAM_B2 * v_ref[...] + (1.0 - ADAM_B2) * (g * g)
        g_ref[...] = g
        nm_ref[...] = nm
        nv_ref[...] = nv
        m_hat = nm / (1.0 - ADAM_B1 ** ADAM_STEP)
        v_hat = nv / (1.0 - ADAM_B2 ** ADAM_STEP)
        d_ref[...] = -ADAM_LR * (m_hat / (jnp.sqrt(v_hat) + ADAM_EPS) + ADAM_WD * w_ref[...])

    blk = pl.BlockSpec((rows, ccols), lambda i: (i, 0))
    out = jax.ShapeDtypeStruct((r, ccols), F32)
    return pl.pallas_call(
        body, name=name, grid=(r // rows,),
        in_specs=[pl.BlockSpec((n, rows, ccols), lambda i: (0, i, 0)), blk, blk, blk],
        out_specs=[blk] * 4, out_shape=[out] * 4, compiler_params=_cp(("parallel",)))(parts, w, m, v)


def norm_fwd(x, nw, scale, shift, tm):
    s = x.shape[0]

    def body(x_ref, nw_ref, sc_ref, sh_ref, h_ref, ht_ref):
        xf = x_ref[...]
        r = lax.rsqrt(jnp.mean(xf * xf, axis=-1, keepdims=True) + EPS)
        h = (xf * r * nw_ref[...]) * (1.0 + sc_ref[...]) + sh_ref[...]
        h_ref[...] = h.astype(BF16)
        ht_ref[...] = h.T.astype(BF16)

    vec = pl.BlockSpec((1, D), lambda i: (0, 0))
    return pl.pallas_call(
        body, name="norm_fwd", grid=(s // tm,),
        in_specs=[pl.BlockSpec((tm, D), lambda i: (i, 0)), vec, vec, vec],
        out_specs=[pl.BlockSpec((tm, D), lambda i: (i, 0)), pl.BlockSpec((D, tm), lambda i: (0, i))],
        out_shape=[jax.ShapeDtypeStruct((s, D), BF16), jax.ShapeDtypeStruct((D, s), BF16)],
        compiler_params=_cp(("parallel",)))(x, nw, scale, shift)


def norm_bwd(dh, x, dy, nw, scale, tm):
    s = x.shape[0]

    def body(dh_ref, x_ref, dy_ref, nw_ref, sc_ref, gx_ref, st_ref):
        xf, g = x_ref[...], dh_ref[...]
        r = lax.rsqrt(jnp.mean(xf * xf, axis=-1, keepdims=True) + EPS)
        xh = xf * r
        dn = g * (1.0 + sc_ref[...])
        dxh = dn * nw_ref[...]
        gx_ref[...] = dy_ref[...] + r * (dxh - xh * jnp.mean(dxh * xh, axis=-1, keepdims=True))

        @pl.when(pl.program_id(0) == 0)
        def _():
            st_ref[...] = jnp.zeros_like(st_ref)

        st_ref[0:1, :] += jnp.sum(g, axis=0, keepdims=True)
        st_ref[1:2, :] += jnp.sum(g * xh * nw_ref[...], axis=0, keepdims=True)
        st_ref[2:3, :] += jnp.sum(dn * xh, axis=0, keepdims=True)

    vec = pl.BlockSpec((1, D), lambda i: (0, 0))
    row = pl.BlockSpec((tm, D), lambda i: (i, 0))
    return pl.pallas_call(
        body, name="norm_bwd", grid=(s // tm,),
        in_specs=[row, row, row, vec, vec],
        out_specs=[row, pl.BlockSpec((8, D), lambda i: (0, 0))],
        out_shape=[jax.ShapeDtypeStruct((s, D), F32), jax.ShapeDtypeStruct((8, D), F32)],
        compiler_params=_cp(("arbitrary",)))(dh, x, dy, nw, scale)


def proj_fwd_gather(h, w_shard, extras, order, tm):
    s = h.shape[0]
    ni = s // tm
    n = 1 + len(extras)
    mid = ni // 2

    def body(order_ref, h_ref, *refs):
        ins, o_ref, outs = refs[:n], refs[n], refs[n + 1:2 * n + 1]
        wbuf, send_sems, recv_sems, local_sems, load_sems = refs[2 * n + 1:]
        jj, i = pl.program_id(0), pl.program_id(1)
        x, y, c = _coords()
        me, sibling = (x, y, c), (x, y, 1 - c)
        chips = _chip_order(x, y, c)
        relayed = [(*chips[1], 1 - c), (*chips[0], 1 - c), (*chips[2], 1 - c)]

        def slot(a, dev):
            return outs[a].at[_dev_index(dev)]

        def copy(a, k, block, to, src=None):
            return pltpu.make_async_remote_copy(
                src_ref=slot(a, block) if src is None else src, dst_ref=slot(a, block),
                send_sem=send_sems.at[a, k], recv_sem=recv_sems.at[a, k], device_id=to, device_id_type=MESH)

        mine = [pltpu.make_async_copy(ins[a], slot(a, me), local_sems.at[a]) for a in range(n)]
        to_sibling = [copy(a, 0, me, sibling, src=ins[a]) for a in range(n)]
        to_chip = [[copy(a, 1 + j, me, (*ch, c), src=ins[a]) for a in range(n)] for j, ch in enumerate(chips)]
        passed = [[copy(a, 4 + j, (*ch, c), sibling) for a in range(n)] for j, ch in enumerate(chips)]

        def load(row):
            return pltpu.make_async_copy(outs[0].at[order_ref[row]], wbuf.at[row % 2], load_sems.at[row % 2])

        @pl.when((jj == 0) & (i == 0))
        def _():
            for cp in mine:
                cp.start()
            to_sibling[0].start()
            to_chip[0][0].start()
            pltpu.make_async_copy(ins[0], wbuf.at[0], load_sems.at[0]).start()

        @pl.when((jj == 1) & (i == 0))
        def _():
            to_chip[1][0].start()
            to_chip[2][0].start()
            for a in range(1, n):
                to_sibling[a].start()
                for j in range(3):
                    to_chip[j][a].start()

        @pl.when((jj == 0) & (i == mid))
        def _():
            copy(0, 0, sibling, me).wait_recv()

        for j, ch in enumerate(chips):
            @pl.when((jj == 1 + j) & (i == mid))
            def _(j=j, ch=ch):
                copy(0, 1 + j, (*ch, c), me).wait_recv()
                passed[j][0].start()

            @pl.when((jj == 4 + j) & (i == mid))
            def _(j=j):
                copy(0, 4 + j, relayed[j], me).wait_recv()

            @pl.when((jj == 3 + j) & (i == 0))
            def _(j=j, ch=ch):
                for a in range(1, n):
                    copy(a, 1 + j, (*ch, c), me).wait_recv()
                    passed[j][a].start()

        @pl.when((jj < NDEV - 1) & (i == mid))
        def _():
            load(jj + 1).start()

        @pl.when(i == 0)
        def _():
            load(jj).wait()

        o_ref[...] = jnp.dot(h_ref[...], wbuf[jj % 2], preferred_element_type=F32).astype(BF16)

        @pl.when((jj == NDEV - 1) & (i == ni - 1))
        def _():
            for a in range(1, n):
                copy(a, 0, sibling, me).wait_recv()
                for j in range(3):
                    copy(a, 4 + j, relayed[j], me).wait_recv()
            for a in range(n):
                mine[a].wait()
                to_sibling[a].wait_send()
                for j in range(3):
                    to_chip[j][a].wait_send()
                    passed[j][a].wait_send()

    any_spec = pl.BlockSpec(memory_space=pl.ANY)
    outs = pl.pallas_call(
        body, name="proj_fwd_gather",
        grid_spec=pltpu.PrefetchScalarGridSpec(
            num_scalar_prefetch=1, grid=(NDEV, ni),
            in_specs=[pl.BlockSpec((tm, D), lambda jj, i, o: (i, 0))] + [any_spec] * n,
            out_specs=[pl.BlockSpec((tm, SHARD), lambda jj, i, o: (i, o[jj]))] + [any_spec] * n,
            scratch_shapes=[pltpu.VMEM((2, D, SHARD), BF16), pltpu.SemaphoreType.DMA((n, 7)),
                            pltpu.SemaphoreType.DMA((n, 7)), pltpu.SemaphoreType.DMA((n,)),
                            pltpu.SemaphoreType.DMA((2,))]),
        out_shape=[jax.ShapeDtypeStruct((s, NIN), BF16), jax.ShapeDtypeStruct((NDEV, D, SHARD), BF16)]
                  + [jax.ShapeDtypeStruct((NDEV,) + e.shape, e.dtype) for e in extras],
        compiler_params=_cp(("arbitrary", "arbitrary")))(order, h, w_shard, *extras)
    return outs[0], outs[1], outs[2:]


def proj_bwd(ht, dproj, wg, smalls, order, tt):
    s = dproj.shape[0]
    nk = s // tt
    n = len(smalls)

    def body(order_ref, ht_ref, dp_ref, w_ref, *rest):
        small_in = rest[:n]
        dh_ref, gw_ref, rwin_ref = rest[n:n + 3]
        small_out = rest[n + 3:2 * n + 3]
        acc, stage, send_sems, recv_sems, local_sems, stage_sems = rest[2 * n + 3:]
        t, k = pl.program_id(0), pl.program_id(1)
        me_xyc = _coords()
        me = _dev_index(me_xyc)
        peers = [_flip(me_xyc, f) for f in FLIPS]

        def exchange(a, kf, src_arr, dst_arr):
            pid = _dev_index(peers[kf])
            mk = lambda dst: pltpu.make_async_remote_copy(
                src_ref=src_arr.at[pid], dst_ref=dst, send_sem=send_sems.at[a, kf], recv_sem=recv_sems.at[a, kf],
                device_id=peers[kf], device_id_type=MESH)
            return mk(dst_arr.at[me]), mk(dst_arr.at[pid])

        small_pairs = [exchange(1 + a, kf, small_in[a], small_out[a]) for kf in range(7) for a in range(n)]
        small_own = [pltpu.make_async_copy(small_in[a].at[me], small_out[a].at[me], local_sems.at[1 + a])
                     for a in range(n)]
        win_pairs = [exchange(0, kf, gw_ref, rwin_ref) for kf in range(7)]
        win_own = pltpu.make_async_copy(gw_ref.at[me], rwin_ref.at[me], local_sems.at[0])

        def to_hbm(jj):
            slab = me if jj == 7 else _dev_index(peers[jj])
            return pltpu.make_async_copy(stage.at[jj % 2], gw_ref.at[slab], stage_sems.at[jj % 2])

        @pl.when((t == 0) & (k == 0))
        def _():
            for cp in small_own:
                cp.start()
            for send, _ in small_pairs:
                send.start()

        @pl.when(t < NDEV)
        def _():
            p = jnp.dot(ht_ref[...], dp_ref[...], preferred_element_type=F32)

            @pl.when(k == 0)
            def _():
                acc[...] = p

            @pl.when(k > 0)
            def _():
                acc[...] += p

        for jj in range(NDEV):
            @pl.when((t == jj) & (k == nk - 1))
            def _(jj=jj):
                stage[jj % 2] = acc[...].astype(BF16)
                to_hbm(jj).start()

            @pl.when((t == jj + 1) & (k == 1))
            def _(jj=jj):
                to_hbm(jj).wait()
                if jj < 7:
                    win_pairs[jj][0].start()
                else:
                    win_own.start()

        @pl.when(t >= NDEV)
        def _():
            p = lax.dot_general(dp_ref[...], w_ref[...], NT, preferred_element_type=F32)

            @pl.when(k == 0)
            def _():
                dh_ref[...] = p

            @pl.when(k > 0)
            def _():
                dh_ref[...] += p

        @pl.when((t == 2 * NDEV - 1) & (k == nk - 1))
        def _():
            for _, recv in win_pairs + small_pairs:
                recv.wait_recv()
            for send, _ in win_pairs + small_pairs:
                send.wait_send()
            win_own.wait()
            for cp in small_own:
                cp.wait()

    any_spec = pl.BlockSpec(memory_space=pl.ANY)
    first = lambda t: t < NDEV
    outs = pl.pallas_call(
        body, name="proj_bwd",
        grid_spec=pltpu.PrefetchScalarGridSpec(
            num_scalar_prefetch=1, grid=(2 * NDEV, nk),
            in_specs=[pl.BlockSpec((D, tt), lambda t, k, o: (0, jnp.where(first(t), k, nk - 1))),
                      pl.BlockSpec((tt, SHARD), lambda t, k, o: (jnp.where(first(t), k, t - NDEV),
                                                                 jnp.where(first(t), o[jnp.minimum(t, NDEV - 1)], k))),
                      pl.BlockSpec((None, D, SHARD), lambda t, k, o: (jnp.where(first(t), 0, k), 0, 0))]
                     + [any_spec] * n,
            out_specs=[pl.BlockSpec((tt, D), lambda t, k, o: (jnp.where(first(t), 0, t - NDEV), 0))]
                      + [any_spec] * (2 + n),
            scratch_shapes=[pltpu.VMEM((D, SHARD), F32), pltpu.VMEM((2, D, SHARD), BF16),
                            pltpu.SemaphoreType.DMA((1 + n, 7)), pltpu.SemaphoreType.DMA((1 + n, 7)),
                            pltpu.SemaphoreType.DMA((1 + n,)), pltpu.SemaphoreType.DMA((2,))]),
        out_shape=[jax.ShapeDtypeStruct((s, D), F32), jax.ShapeDtypeStruct((NDEV, D, SHARD), BF16),
                   jax.ShapeDtypeStruct((NDEV, D, SHARD), BF16)]
                  + [jax.ShapeDtypeStruct(a.shape, a.dtype) for a in smalls],
        compiler_params=_cp(("arbitrary", "arbitrary"), 56))(order, ht, dproj, wg, *smalls)
    return outs[0], outs[2], outs[3:]


def matmul_tn(a, b, name, tk):
    s, m = a.shape
    n = b.shape[1]
    nk = s // tk

    def body(a_ref, b_ref, o_ref, acc_ref):
        k = pl.program_id(0)
        p = lax.dot_general(a_ref[...], b_ref[...], TN, preferred_element_type=F32)

        @pl.when(k == 0)
        def _():
            acc_ref[...] = p

        @pl.when(k > 0)
        def _():
            acc_ref[...] += p

        @pl.when(k == nk - 1)
        def _():
            o_ref[...] = acc_ref[...].astype(BF16)

    return pl.pallas_call(
        body, name=name, grid=(nk,),
        in_specs=[pl.BlockSpec((tk, m), lambda k: (k, 0)), pl.BlockSpec((tk, n), lambda k: (k, 0))],
        out_specs=pl.BlockSpec((m, n), lambda k: (0, 0)),
        out_shape=jax.ShapeDtypeStruct((m, n), BF16),
        scratch_shapes=[pltpu.VMEM((m, n), F32)],
        compiler_params=_cp(("arbitrary",)))(a, b)


def _head_matrices():
    lane = lax.broadcasted_iota(jnp.int32, (CB, CB), 0)
    col = lax.broadcasted_iota(jnp.int32, (CB, CB), 1)
    same = (lane // HD == col // HD).astype(BF16)
    lane_c = lax.broadcasted_iota(jnp.int32, (CB, LANES), 0)
    col_c = lax.broadcasted_iota(jnp.int32, (CB, LANES), 1)
    total = (lane_c // HD == col_c).astype(BF16)
    pick = (lane_c == col_c * HD).astype(BF16)
    return same, total, pick


def _head_sum(x, m_ref):
    return jnp.dot(x.astype(BF16), m_ref[...], preferred_element_type=F32)


def _dot_hilo(x, m_ref):
    hi = x.astype(BF16)
    lo = (x - hi.astype(F32)).astype(BF16)
    return (jnp.dot(hi, m_ref[...], preferred_element_type=F32)
            + jnp.dot(lo, m_ref[...], preferred_element_type=F32))


def _to_residue_major(val, buf, out_ref, dil):
    rows = out_ref.shape[1]
    for k in range(val.shape[1] // LANES):
        lanes = slice(k * LANES, (k + 1) * LANES)
        buf[k] = val[:, lanes]
        for r in range(dil):
            out_ref[r, :, lanes] = buf.at[k][pl.ds(r, rows, stride=dil), :].astype(out_ref.dtype)


def _from_residue_major(ref, buf, dil):
    if dil == 1:
        return ref[0].astype(F32)
    rows = ref.shape[1]
    for k in range(CB // LANES):
        for r in range(dil):
            buf.at[k][pl.ds(r, rows, stride=dil), :] = ref[r, :, k * LANES:(k + 1) * LANES].astype(F32)
    return jnp.concatenate([buf[k] for k in range(CB // LANES)], axis=1)


def qkv_prep(proj, qw8, kw8, same, tm):
    s = proj.shape[0]
    items = []
    for g, d in enumerate(DILATIONS):
        items += [(g, "q", CB_Q + g, d), (g, "k", CB_K + g, d)] + ([(g, "v", CB_V + g, d)] if d > 1 else [])
    n = len(items)

    def body(*refs):
        ins, (qw_ref, kw_ref, same_ref), outs, buf = refs[:n], refs[n:n + 3], refs[n + 3:2 * n + 3], refs[-1]
        for idx, (_, kind, _, dil) in enumerate(items):
            val = ins[idx][...].astype(F32)
            if kind != "v":
                r = lax.rsqrt(_head_sum(val * val, same_ref) * (1.0 / HD) + EPS)
                val = val * r * (qw_ref if kind == "q" else kw_ref)[...]
            if dil == 1:
                outs[idx][0] = val.astype(BF16)
            else:
                _to_residue_major(val, buf, outs[idx], dil)

    full = lambda a: pl.BlockSpec(a.shape, lambda i: (0, 0))
    outs = pl.pallas_call(
        body, name="qkv_prep", grid=(s // tm,),
        in_specs=[pl.BlockSpec((tm, CB), lambda i, cb=cb: (i, cb)) for _, _, cb, _ in items]
                 + [full(qw8), full(kw8), full(same)],
        out_specs=[pl.BlockSpec((d, tm // d, CB), lambda i: (0, i, 0)) for _, _, _, d in items],
        out_shape=[jax.ShapeDtypeStruct((d, s // d, CB), BF16) for _, _, _, d in items],
        scratch_shapes=[pltpu.VMEM((CB // LANES, tm, LANES), F32)],
        compiler_params=_cp(("parallel",)))(*([proj] * n), qw8 * (HD ** -0.5), kw8, same)
    srcs = [[None, None, (proj, CB_V + g)] for g in range(len(DILATIONS))]
    for (g, kind, _, _), o in zip(items, outs):
        srcs[g]["qkv".index(kind)] = (o.reshape(s, CB), 0)
    return srcs


def stats_prep(da, lc, dc, g, dil, tm):
    s = da.shape[0]
    rows = tm // dil

    def body(da_ref, lc_ref, dc_ref, dap_ref, lcp_ref, dcp_ref, lt_ref, dt_ref, buf):
        if dil == 1:
            dap_ref[0] = da_ref[...]
        else:
            _to_residue_major(da_ref[...].astype(F32), buf, dap_ref, dil)
        for src, dst, dst_t in ((lc_ref, lcp_ref, lt_ref), (dc_ref, dcp_ref, dt_ref)):
            buf[0] = src[...]
            for r in range(dil):
                piece = buf.at[0][pl.ds(r, rows, stride=dil), :] if dil > 1 else buf[0]
                dst[r] = piece
                dst_t[r] = piece.T[0:NH, :]

    row = lambda w: pl.BlockSpec((tm, w), lambda i: (i, 0))
    rm = lambda w: pl.BlockSpec((dil, rows, w), lambda i: (0, i, 0))
    tr = pl.BlockSpec((dil, NH, rows), lambda i: (0, 0, i))
    length = s // dil
    dap, lcp, dcp, lt, dt = pl.pallas_call(
        body, name=f"stats_prep_g{g}", grid=(s // tm,),
        in_specs=[row(CB), row(LANES), row(LANES)],
        out_specs=[rm(CB), rm(LANES), rm(LANES), tr, tr],
        out_shape=[jax.ShapeDtypeStruct((dil, length, CB), BF16)]
                  + [jax.ShapeDtypeStruct((dil, length, LANES), F32)] * 2
                  + [jax.ShapeDtypeStruct((dil, NH, length), F32)] * 2,
        scratch_shapes=[pltpu.VMEM((CB // LANES, tm, LANES), F32)],
        compiler_params=_cp(("parallel",)))(da, lc, dc)
    return (dap.reshape(s, CB), lcp.reshape(s, LANES), dcp.reshape(s, LANES),
            lt.reshape(dil * NH, length), dt.reshape(dil * NH, length))


def qkv_grads_to_dproj(dproj, proj, grads, qw8, kw8, same, tm):
    s = dproj.shape[0]
    ni = s // tm
    flat = [(t.reshape(d, s // d, CB), d, kind, 3 * kind + g)
            for g, d in enumerate(DILATIONS) for kind, t in enumerate(grads[g])]
    nf = len(flat)
    nraw = 2 * len(DILATIONS)

    def body(*refs):
        dp_hbm, raws, ins = refs[nraw + nf + 4], refs[1:1 + nraw], refs[1 + nraw:1 + nraw + nf]
        qw_ref, kw_ref, same_ref = refs[1 + nraw + nf:4 + nraw + nf]
        gw_ref, stage, buf, sems = refs[5 + nraw + nf:]
        i = pl.program_id(0)
        slot = i % 2

        def slab(step, sl):
            return pltpu.make_async_copy(
                stage.at[sl], dp_hbm.at[pl.ds(pl.multiple_of(step * tm, tm), tm), pl.ds(CB_Q * CB, 9 * CB)],
                sems.at[sl])

        @pl.when(i == 0)
        def _():
            gw_ref[...] = jnp.zeros_like(gw_ref)

        @pl.when(i >= 2)
        def _():
            slab(i - 2, slot).wait()

        for ref, (_, d, kind, jj) in zip(ins, flat):
            cols = slice(jj * CB, (jj + 1) * CB)
            dn = _from_residue_major(ref, buf, d)
            if kind == 2:
                stage[slot, :, cols] = dn.astype(BF16)
                continue
            t = raws[jj][...].astype(F32)
            r = lax.rsqrt(_head_sum(t * t, same_ref) * (1.0 / HD) + EPS)
            xh = t * r
            gw_ref[kind:kind + 1, :] += jnp.sum(dn * xh, axis=0, keepdims=True)
            dxh = dn * (qw_ref if kind == 0 else kw_ref)[...]
            mean = _head_sum(dxh * xh, same_ref) * (1.0 / HD)
            stage[slot, :, cols] = (r * (dxh - xh * mean)).astype(BF16)
        slab(i, slot).start()

        @pl.when(i == ni - 1)
        def _():
            slab(i - 1, 1 - slot).wait()
            slab(i, slot).wait()

    full = lambda a: pl.BlockSpec(a.shape, lambda i: (0, 0))
    any_spec = pl.BlockSpec(memory_space=pl.ANY)
    return pl.pallas_call(
        body, name="qkv_grads_to_dproj", grid=(ni,),
        in_specs=[any_spec] + [pl.BlockSpec((tm, CB), lambda i, jb=jb: (i, CB_Q + jb)) for jb in range(nraw)]
                 + [pl.BlockSpec((d, tm // d, CB), lambda i: (0, i, 0)) for _, d, _, _ in flat]
                 + [full(qw8), full(kw8), full(same)],
        out_specs=[any_spec, pl.BlockSpec((8, CB), lambda i: (0, 0))],
        out_shape=[jax.ShapeDtypeStruct((s, NIN), BF16), jax.ShapeDtypeStruct((8, CB), F32)],
        input_output_aliases={0: 0},
        scratch_shapes=[pltpu.VMEM((2, tm, 9 * CB), BF16), pltpu.VMEM((CB // LANES, tm, LANES), F32),
                        pltpu.SemaphoreType.DMA((2,))],
        compiler_params=_cp(("arbitrary",)))(
            dproj, *([proj] * nraw), *[t for t, _, _, _ in flat], qw8, kw8, same)


def _lane_lo():
    return lax.broadcasted_iota(jnp.int32, (1, 2 * HD), 1) < HD


def _stack_heads(t, lo):
    zero = jnp.zeros_like(t)
    return jnp.concatenate([jnp.where(lo, t, zero), jnp.where(lo, zero, t)], axis=0)


def _masks(other_ok):
    qi = lax.broadcasted_iota(jnp.int32, (QB, QB), 0)
    kj = lax.broadcasted_iota(jnp.int32, (QB, QB), 1)
    return (kj >= qi) & other_ok, kj <= qi


SUB = 4


def _attn_specs(nb, dil):
    steps = nb // SUB
    main = lambda cb, w=CB: pl.BlockSpec((SUB * QB, w), lambda r, s: (r * steps + s, cb))
    prev = lambda cb: pl.BlockSpec((QB, CB), lambda r, s: (jnp.maximum(r * nb + SUB * s - 1, 0), cb))
    nxt = lambda cb: pl.BlockSpec((QB, CB), lambda r, s: (jnp.minimum(r * nb + SUB * (s + 1), dil * nb - 1), cb))
    return main, prev, nxt


def attn_fwd(q_src, k_src, v_src, g, dil):
    s = q_src[0].shape[0]
    nb = s // dil // QB
    main, prev, _ = _attn_specs(nb, dil)

    def body(q_ref, kp_ref, k_ref, vp_ref, v_ref, o_ref, l_ref, kbuf, vbuf):
        step = pl.program_id(1)
        kbuf[0:QB], kbuf[QB:] = kp_ref[...], k_ref[...]
        vbuf[0:QB], vbuf[QB:] = vp_ref[...], v_ref[...]
        lo = _lane_lo()

        def block(j, carry):
            r0 = pl.multiple_of(j * QB, QB)
            rows, krows = pl.ds(r0, QB), pl.ds(r0, 2 * QB)
            m_prev, m_cur = _masks(step * SUB + j > 0)
            mask = jnp.concatenate([m_prev, m_cur], axis=1)
            mask = jnp.concatenate([mask, mask], axis=0)
            for i in range(NH // 2):
                sl = slice(2 * HD * i, 2 * HD * (i + 1))
                qs, ks, vv = q_ref[rows, sl], kbuf[krows, sl], vbuf[krows, sl]
                sc = lax.dot_general(_stack_heads(qs, lo), ks, NT, preferred_element_type=F32)
                sc = jnp.where(mask, sc, NEG)
                mx = jnp.max(sc, axis=-1, keepdims=True)
                p = jnp.exp(sc - mx)
                den = jnp.sum(p, axis=-1, keepdims=True)
                o = jnp.dot(p.astype(BF16), vv, preferred_element_type=F32) * (1.0 / den)
                lse = jnp.broadcast_to(mx + jnp.log(den), (2 * QB, 2 * HD))
                o_ref[rows, sl] = jnp.where(lo, o[:QB], o[QB:])
                l_ref[rows, sl] = jnp.where(lo, lse[:QB], lse[QB:])
            return carry

        lax.fori_loop(0, SUB, block, 0, unroll=True)

    out = jax.ShapeDtypeStruct((s, CB), F32)
    return pl.pallas_call(
        body, name=f"attn_fwd_g{g}", grid=(dil, nb // SUB),
        in_specs=[main(q_src[1]), prev(k_src[1]), main(k_src[1]), prev(v_src[1]), main(v_src[1])],
        out_specs=[main(0)] * 2, out_shape=[out, out],
        scratch_shapes=[pltpu.VMEM(((SUB + 1) * QB, CB), BF16)] * 2,
        compiler_params=_cp(("parallel", "parallel")))(q_src[0], k_src[0], k_src[0], v_src[0], v_src[0])


def attn_bwd_q(q_src, k_src, v_src, da, lc, dc, g, dil):
    s = q_src[0].shape[0]
    nb = s // dil // QB
    main, prev, _ = _attn_specs(nb, dil)

    def body(q_ref, kp_ref, k_ref, vp_ref, v_ref, da_ref, l_ref, d_ref, dq_ref, kbuf, vbuf):
        step = pl.program_id(1)
        kbuf[0:QB], kbuf[QB:] = kp_ref[...], k_ref[...]
        vbuf[0:QB], vbuf[QB:] = vp_ref[...], v_ref[...]
        lo = _lane_lo()

        def block(j, carry):
            r0 = pl.multiple_of(j * QB, QB)
            rows, krows = pl.ds(r0, QB), pl.ds(r0, 2 * QB)
            m_prev, m_cur = _masks(step * SUB + j > 0)
            mask = jnp.concatenate([m_prev, m_cur], axis=1)
            mask = jnp.concatenate([mask, mask], axis=0)
            lcols, dcols = l_ref[rows, :], d_ref[rows, :]
            for i in range(NH // 2):
                sl = slice(2 * HD * i, 2 * HD * (i + 1))
                qs, ks, vv, da2 = q_ref[rows, sl], kbuf[krows, sl], vbuf[krows, sl], da_ref[rows, sl]
                pair = lambda t: jnp.concatenate([t[:, 2 * i:2 * i + 1], t[:, 2 * i + 1:2 * i + 2]], axis=0)
                sc = lax.dot_general(_stack_heads(qs, lo), ks, NT, preferred_element_type=F32)
                sc = jnp.where(mask, sc, NEG)
                p = jnp.exp(sc - pair(lcols))
                dp = lax.dot_general(_stack_heads(da2, lo), vv, NT, preferred_element_type=F32)
                ds = p * (dp - pair(dcols))
                dq = jnp.dot(ds.astype(BF16), ks, preferred_element_type=F32)
                dq_ref[rows, sl] = (jnp.where(lo, dq[:QB], dq[QB:]) * (HD ** -0.5)).astype(BF16)
            return carry

        lax.fori_loop(0, SUB, block, 0, unroll=True)

    return pl.pallas_call(
        body, name=f"attn_bwd_q_g{g}", grid=(dil, nb // SUB),
        in_specs=[main(q_src[1]), prev(k_src[1]), main(k_src[1]), prev(v_src[1]), main(v_src[1]),
                  main(0), main(0, LANES), main(0, LANES)],
        out_specs=main(0), out_shape=jax.ShapeDtypeStruct((s, CB), BF16),
        scratch_shapes=[pltpu.VMEM(((SUB + 1) * QB, CB), BF16)] * 2,
        compiler_params=_cp(("parallel", "parallel")))(
            q_src[0], k_src[0], k_src[0], v_src[0], v_src[0], da, lc, dc)


def attn_bwd_kv(q_src, k_src, v_src, da, lt, dt, g, dil):
    s = q_src[0].shape[0]
    nb = s // dil // QB
    main, _, nxt = _attn_specs(nb, dil)

    def body(k_ref, v_ref, q_ref, qn_ref, da_ref, dan_ref, l_ref, ln_ref, d_ref, dn_ref, dk_ref, dv_ref,
             qbuf, dabuf, lbuf, dbuf):
        step = pl.program_id(1)
        qbuf[0:SUB * QB], qbuf[SUB * QB:] = q_ref[...], qn_ref[...]
        dabuf[0:SUB * QB], dabuf[SUB * QB:] = da_ref[...], dan_ref[...]
        for c in range(SUB):
            lbuf[c], dbuf[c] = l_ref[:, c * QB:(c + 1) * QB], d_ref[:, c * QB:(c + 1) * QB]
        lbuf[SUB], dbuf[SUB] = ln_ref[...], dn_ref[...]
        lo = _lane_lo()
        kj = lax.broadcasted_iota(jnp.int32, (QB, QB), 0)
        qi = lax.broadcasted_iota(jnp.int32, (QB, QB), 1)

        def block(j, carry):
            r0 = pl.multiple_of(j * QB, QB)
            rows, qrows = pl.ds(r0, QB), pl.ds(r0, 2 * QB)
            mask = jnp.concatenate([kj <= qi, (kj >= qi) & (step * SUB + j < nb - 1)], axis=1)
            mask = jnp.concatenate([mask, mask], axis=1)
            lrow = jnp.concatenate([lbuf[j], lbuf[j + 1]], axis=1)
            drow = jnp.concatenate([dbuf[j], dbuf[j + 1]], axis=1)
            for i in range(NH // 2):
                sl = slice(2 * HD * i, 2 * HD * (i + 1))
                q2, da2 = _stack_heads(qbuf[qrows, sl], lo), _stack_heads(dabuf[qrows, sl], lo)
                ks, vv = k_ref[rows, sl], v_ref[rows, sl]
                pair = lambda t: jnp.concatenate([t[2 * i:2 * i + 1, :], t[2 * i + 1:2 * i + 2, :]], axis=1)
                sc = lax.dot_general(ks, q2, NT, preferred_element_type=F32)
                sc = jnp.where(mask, sc, NEG)
                p = jnp.exp(sc - pair(lrow))
                dp = lax.dot_general(vv, da2, NT, preferred_element_type=F32)
                ds = p * (dp - pair(drow))
                dv_ref[rows, sl] = jnp.dot(p.astype(BF16), da2, preferred_element_type=F32).astype(BF16)
                dk_ref[rows, sl] = jnp.dot(ds.astype(BF16), q2, preferred_element_type=F32).astype(BF16)
            return carry

        lax.fori_loop(0, SUB, block, 0, unroll=True)

    steps = nb // SUB
    t_main = pl.BlockSpec((NH, SUB * QB), lambda r, s: (r, s))
    t_nxt = pl.BlockSpec((NH, QB), lambda r, s: (r, jnp.minimum(SUB * (s + 1), nb - 1)))
    out = jax.ShapeDtypeStruct((s, CB), BF16)
    return pl.pallas_call(
        body, name=f"attn_bwd_kv_g{g}", grid=(dil, steps),
        in_specs=[main(k_src[1]), main(v_src[1]), main(q_src[1]), nxt(q_src[1]),
                  main(0), nxt(0), t_main, t_nxt, t_main, t_nxt],
        out_specs=[main(0), main(0)], out_shape=[out, out],
        scratch_shapes=[pltpu.VMEM(((SUB + 1) * QB, CB), BF16)] * 2 + [pltpu.VMEM((SUB + 1, NH, QB), F32)] * 2,
        compiler_params=_cp(("parallel", "parallel")))(
            k_src[0], v_src[0], q_src[0], q_src[0], da, da, lt, lt, dt, dt)


def _conv_taps(u, u_prev, first):
    tm = u.shape[0]
    row = lax.broadcasted_iota(jnp.int32, (tm, 1), 0)
    up = jnp.where(first, 0.0, u_prev)
    u1 = jnp.where(row == 0, up[HALO - 1:HALO, :], pltpu.roll(u, 1, 0))
    u2 = jnp.where(row == 0, up[HALO - 2:HALO - 1, :],
                   jnp.where(row == 1, up[HALO - 1:HALO, :], pltpu.roll(u, 2, 0)))
    return u1, u2


def mid_fwd(proj, o_g, lse_g, conv_w, pick, tm):
    s = proj.shape[0]
    hb = tm // HALO

    def body(ba_ref, ca_ref, xa_ref, za_ref, cah_ref, xah_ref, zb_ref,
             o0, o1, o2, l0, l1, l2, w_ref, pick_ref, ya_ref, yb_ref, at_ref, lc_ref, buf_o, buf_l):
        first = pl.program_id(0) == 0
        u = ca_ref[...].astype(F32) * xa_ref[...].astype(F32)
        u1, u2 = _conv_taps(u, cah_ref[...].astype(F32) * xah_ref[...].astype(F32), first)
        conv = w_ref[0:1, :] * u2 + w_ref[1:2, :] * u1 + w_ref[2:3, :] * u
        ya_ref[...] = (ba_ref[...].astype(F32) * conv * _silu(za_ref[...].astype(F32))).astype(BF16)
        ls = [_from_residue_major(l, buf_l.at[g], d) for g, (l, d) in enumerate(zip((l0, l1, l2), DILATIONS))]
        mx = jnp.maximum(jnp.maximum(ls[0], ls[1]), ls[2])
        es = [jnp.exp(l - mx) for l in ls]
        den = es[0] + es[1] + es[2]
        num = jnp.zeros_like(den)
        for e, o, d in zip(es, (o0, o1, o2), DILATIONS):
            num = num + e * _from_residue_major(o, buf_o, d)
        attn = num / den
        at_ref[...] = attn
        lc_ref[...] = _dot_hilo(mx + jnp.log(den), pick_ref)
        yb_ref[...] = (attn * _silu(zb_ref[...].astype(F32))).astype(BF16)

    col = lambda j: pl.BlockSpec((tm, D), lambda i: (i, j))
    halo = lambda j: pl.BlockSpec((HALO, D), lambda i: (jnp.maximum(i * hb - 1, 0), j))
    loc = lambda w: pl.BlockSpec((tm, w), lambda i: (i, 0))
    rm = [pl.BlockSpec((d, tm // d, CB), lambda i: (0, i, 0)) for d in DILATIONS]
    rm_view = lambda ts: [t.reshape(d, s // d, CB) for t, d in zip(ts, DILATIONS)]
    return pl.pallas_call(
        body, name="mid_fwd", grid=(s // tm,),
        in_specs=[col(0), col(1), col(2), col(3), halo(1), halo(2),
                  pl.BlockSpec((tm, CB), lambda i: (i, CB_ZB))] + rm + rm
                 + [pl.BlockSpec((3, D), lambda i: (0, 0)), pl.BlockSpec(pick.shape, lambda i: (0, 0))],
        out_specs=[loc(D), loc(CB), loc(CB), loc(LANES)],
        out_shape=[jax.ShapeDtypeStruct((s, D), BF16), jax.ShapeDtypeStruct((s, CB), BF16),
                   jax.ShapeDtypeStruct((s, CB), F32), jax.ShapeDtypeStruct((s, LANES), F32)],
        scratch_shapes=[pltpu.VMEM((CB // LANES, tm, LANES), F32), pltpu.VMEM((3, CB // LANES, tm, LANES), F32)],
        compiler_params=_cp(("parallel",)))(
            proj, proj, proj, proj, proj, proj, proj, *rm_view(o_g), *rm_view(lse_g), conv_w, pick)


def mid_bwd(dproj, proj, dya, conv_w, tm):
    s = proj.shape[0]
    hb = tm // HALO
    nblk = s // tm
    last_h = s // HALO - 1

    def body(_, ba_ref, ca_ref, xa_ref, za_ref, cah_ref, xah_ref, ban_ref, zan_ref, dy_ref, dyn_ref, w_ref,
             o_ref, gw_ref):
        i = pl.program_id(0)
        ba, ca, xa, za = (t[...].astype(F32) for t in (ba_ref, ca_ref, xa_ref, za_ref))
        u = ca * xa
        u1, u2 = _conv_taps(u, cah_ref[...].astype(F32) * xah_ref[...].astype(F32), i == 0)
        w0, w1, w2 = w_ref[0:1, :], w_ref[1:2, :], w_ref[2:3, :]
        conv = w0 * u2 + w1 * u1 + w2 * u
        sg = jax.nn.sigmoid(za)
        sz = za * sg
        dy = dy_ref[...].astype(F32)
        dconv = dy * ba * sz
        dcn = dyn_ref[...].astype(F32) * ban_ref[...].astype(F32) * _silu(zan_ref[...].astype(F32))
        dcn = jnp.where(i == nblk - 1, 0.0, dcn)
        row = lax.broadcasted_iota(jnp.int32, (tm, 1), 0)
        d1 = jnp.where(row == tm - 1, dcn[0:1, :], pltpu.roll(dconv, tm - 1, 0))
        d2 = jnp.where(row == tm - 2, dcn[0:1, :],
                       jnp.where(row == tm - 1, dcn[1:2, :], pltpu.roll(dconv, tm - 2, 0)))
        du = w2 * dconv + w1 * d1 + w0 * d2
        o_ref[:, 0:D] = (dy * conv * sz).astype(BF16)
        o_ref[:, D:2 * D] = (du * xa).astype(BF16)
        o_ref[:, 2 * D:3 * D] = (du * ca).astype(BF16)
        o_ref[:, 3 * D:4 * D] = (dy * ba * conv * (sg * (1.0 + za * (1.0 - sg)))).astype(BF16)

        @pl.when(i == 0)
        def _():
            gw_ref[...] = jnp.zeros_like(gw_ref)

        gw_ref[0:1, :] += jnp.sum(dconv * u2, axis=0, keepdims=True)
        gw_ref[1:2, :] += jnp.sum(dconv * u1, axis=0, keepdims=True)
        gw_ref[2:3, :] += jnp.sum(dconv * u, axis=0, keepdims=True)

    col = lambda j: pl.BlockSpec((tm, D), lambda i: (i, j))
    halo_prev = lambda j: pl.BlockSpec((HALO, D), lambda i: (jnp.maximum(i * hb - 1, 0), j))
    halo_next = lambda j: pl.BlockSpec((HALO, D), lambda i: (jnp.minimum((i + 1) * hb, last_h), j))
    return pl.pallas_call(
        body, name="mid_bwd", grid=(nblk,),
        in_specs=[pl.BlockSpec(memory_space=pl.ANY), col(0), col(1), col(2), col(3),
                  halo_prev(1), halo_prev(2), halo_next(0), halo_next(3),
                  pl.BlockSpec((tm, D), lambda i: (i, 0)), halo_next(0),
                  pl.BlockSpec((3, D), lambda i: (0, 0))],
        out_specs=[pl.BlockSpec((tm, 4 * D), lambda i: (i, 0)), pl.BlockSpec((8, D), lambda i: (0, 0))],
        out_shape=[jax.ShapeDtypeStruct((s, NIN), BF16), jax.ShapeDtypeStruct((8, D), F32)],
        input_output_aliases={0: 0},
        compiler_params=_cp(("arbitrary",)))(dproj, proj, proj, proj, proj, proj, proj, proj, proj, dya, dya, conv_w)


def tail(proj, ya, yb, attn, x, target, gate, pa_w, pb_w, wo_w, total, tm):
    s = proj.shape[0]
    ni = s // tm
    ncol = NIN - CB_ZB * CB

    def body(ya_ref, yb_ref, ga_ref, gb_ref, zb_ref, at_ref, x_ref, t_ref, gate_ref, pa_ref, pb_ref, wo_ref,
             tot_ref, dp_hbm, dy_ref, dya_ref, da_ref, dc_ref, mg_ref, do_ref, dpa_ref, dpb_ref, st_ref,
             stage, sems):
        i = pl.program_id(0)
        slot = i % 2

        def slab(step, sl):
            return pltpu.make_async_copy(
                stage.at[sl], dp_hbm.at[pl.ds(pl.multiple_of(step * tm, tm), tm), pl.ds(CB_ZB * CB, ncol)],
                sems.at[sl])

        @pl.when(i == 0)
        def _():
            st_ref[...] = jnp.zeros_like(st_ref)

        @pl.when(i >= 2)
        def _():
            slab(i - 2, slot).wait()

        gate_v = gate_ref[...]
        pa = jnp.dot(ya_ref[...], pa_ref[...], preferred_element_type=F32)
        pb = jnp.dot(yb_ref[...], pb_ref[...], preferred_element_type=F32)
        sa = jax.nn.sigmoid(ga_ref[...].astype(F32))
        sb = jax.nn.sigmoid(gb_ref[...].astype(F32))
        merged = (sa * pa + sb * pb).astype(BF16)
        mg_ref[...] = merged
        out = jnp.dot(merged, wo_ref[...], preferred_element_type=F32)
        err = x_ref[...] + gate_v * out - t_ref[...]
        dy = err * (1.0 / D)
        dy_ref[...] = dy
        st_ref[0:1, :] += jnp.sum(dy * out, axis=0, keepdims=True)
        st_ref[1:2, :] += jnp.sum(err * err, axis=0, keepdims=True)
        dout = (gate_v * dy).astype(BF16)
        do_ref[...] = dout
        dmg = lax.dot_general(dout, wo_ref[...], NT, preferred_element_type=F32)
        dpa = (dmg * sa).astype(BF16)
        dpb = (dmg * sb).astype(BF16)
        dpa_ref[...] = dpa
        dpb_ref[...] = dpb
        stage[slot, :, CB:CB + D] = (dmg * pa * sa * (1.0 - sa)).astype(BF16)
        stage[slot, :, CB + D:] = (dmg * pb * sb * (1.0 - sb)).astype(BF16)
        dya_ref[...] = lax.dot_general(dpa, pa_ref[...], NT, preferred_element_type=F32).astype(BF16)
        dyb = lax.dot_general(dpb, pb_ref[...], NT, preferred_element_type=F32)
        zb = zb_ref[...].astype(F32)
        sg = jax.nn.sigmoid(zb)
        attn_v = at_ref[...]
        dattn = dyb * (zb * sg)
        da_ref[...] = dattn.astype(BF16)
        stage[slot, :, 0:CB] = (dyb * attn_v * (sg * (1.0 + zb * (1.0 - sg)))).astype(BF16)
        dc_ref[...] = _dot_hilo(dattn * attn_v, tot_ref)

        slab(i, slot).start()

        @pl.when(i == ni - 1)
        def _():
            slab(i - 1, 1 - slot).wait()
            slab(i, slot).wait()

    row = lambda w: pl.BlockSpec((tm, w), lambda i: (i, 0))
    pcol = lambda w, jb: pl.BlockSpec((tm, w), lambda i: (i, jb))
    full = lambda a: pl.BlockSpec(a.shape, lambda i: (0, 0))
    return pl.pallas_call(
        body, name="tail", grid=(ni,),
        in_specs=[row(D), row(CB), pcol(D, 9), pcol(D, 10), pcol(CB, CB_ZB), row(CB), row(D), row(D),
                  pl.BlockSpec((1, D), lambda i: (0, 0)), full(pa_w), full(pb_w), full(wo_w), full(total)],
        out_specs=[pl.BlockSpec(memory_space=pl.ANY),
                   row(D), row(D), row(CB), row(LANES), row(D), row(D), row(D), row(D),
                   pl.BlockSpec((8, D), lambda i: (0, 0))],
        out_shape=[jax.ShapeDtypeStruct((s, NIN), BF16), jax.ShapeDtypeStruct((s, D), F32),
                   jax.ShapeDtypeStruct((s, D), BF16), jax.ShapeDtypeStruct((s, CB), BF16),
                   jax.ShapeDtypeStruct((s, LANES), F32)] + [jax.ShapeDtypeStruct((s, D), BF16)] * 4
                  + [jax.ShapeDtypeStruct((8, D), F32)],
        scratch_shapes=[pltpu.VMEM((2, tm, ncol), BF16), pltpu.SemaphoreType.DMA((2,))],
        compiler_params=_cp(("arbitrary",), 56))(
            ya, yb, proj, proj, proj, attn, x, target, gate, pa_w, pb_w, wo_w, total)


def _local_step(x, target, shift, scale, gate, norm_w, conv_w, qw, kw, w_shard, small_shards, me_xyc):
    qw8, kw8 = jnp.tile(qw, (1, NH)), jnp.tile(kw, (1, NH))
    same, total, pick = _head_matrices()
    h, ht = norm_fwd(x, norm_w, scale, shift, 512)
    proj, wg, (pa_g, pb_g, wo_g) = proj_fwd_gather(h, w_shard, small_shards, gather_order(me_xyc), 1024)
    pa_w, wo_w = pa_g.reshape(D, D), wo_g.reshape(D, D)
    pb_w = pb_g.transpose(1, 0, 2).reshape(CB, D)
    srcs = qkv_prep(proj, qw8, kw8, same, 512)
    o_g, lse_g = zip(*[attn_fwd(*srcs[g], g, d) for g, d in enumerate(DILATIONS)])
    ya, yb, attn, lc = mid_fwd(proj, o_g, lse_g, conv_w, pick, 512)
    dproj, dy, dya, da, dc, merged, dout, dpa, dpb, st_tail = tail(
        proj, ya, yb, attn, x, target, gate, pa_w, pb_w, wo_w, total, 256)
    g_wo = matmul_tn(merged, dout, "grad_w_out", 1024)
    g_pa = matmul_tn(ya, dpa, "grad_w_br_conv", 1024)
    g_pb = matmul_tn(yb, dpb, "grad_w_br_attn", 1024)
    dproj, st_conv = mid_bwd(dproj, proj, dya, conv_w, 512)
    grads = []
    for g, d in enumerate(DILATIONS):
        da_p, lc_p, dc_p, lt, dt = stats_prep(da, lc, dc, g, d, 2048)
        dq = attn_bwd_q(*srcs[g], da_p, lc_p, dc_p, g, d)
        dk, dv = attn_bwd_kv(*srcs[g], da_p, lt, dt, g, d)
        grads.append((dq, dk, dv))
    dproj, gw_qk = qkv_grads_to_dproj(dproj, proj, grads, qw8, kw8, same, 512)
    slabs = [g_pa.reshape(NDEV, 128, D), g_pb.reshape(CB, NDEV, 128).transpose(1, 0, 2), g_wo.reshape(NDEV, 128, D)]
    dh, r_win, (r_pa, r_pb, r_wo) = proj_bwd(ht, dproj, wg, slabs, scatter_order(me_xyc), 1024)
    grad_x, st_norm = norm_bwd(dh, x, dy, norm_w, scale, 512)
    dmod = jnp.concatenate([st_norm[0:1], st_norm[1:2], st_tail[0:1]], axis=1)
    loss_part = (0.5 / D) * jnp.sum(st_tail[1])
    gw_heads = gw_qk[0:2].reshape(2, NH, HD).sum(axis=1)
    small = dict(dmod=dmod, norm_w=st_norm[2:3], conv_w=st_conv[0:3],
                 q_norm_w=gw_heads[0:1], k_norm_w=gw_heads[1:2], loss=loss_part)
    return grad_x, small, (r_win, r_pa, r_pb, r_wo)


def kernel(x, c, w_ada, b_ada, norm_w, w_in, conv_w, q_norm_w, k_norm_w, w_br_conv, w_br_attn, w_out, loss_target, m_w_ada, m_b_ada, m_norm_w, m_w_in, m_conv_w, m_q_norm_w, m_k_norm_w, m_w_br_conv, m_w_br_attn, m_w_out, v_w_ada, v_b_ada, v_norm_w, v_w_in, v_conv_w, v_q_norm_w, v_k_norm_w, v_w_br_conv, v_w_br_attn, v_w_out):
    me_xyc = (lax.axis_index("x"), lax.axis_index("y"), lax.axis_index("c"))
    me = _dev_index(me_xyc)
    ncol = w_ada.shape[2]

    conv_pad = jnp.zeros((8, 128), F32).at[0:3].set(conv_w[0])
    c_all, conv_all = all_gather([c, conv_pad], "gather_cond")
    conv_full = conv_all[:, 0:3].transpose(1, 0, 2).reshape(3, D)
    c_all = c_all.reshape(NDEV, D)

    b_cols = lax.dynamic_slice(b_ada, (0, me * ncol), (1, ncol))
    mod_cols = ada_fwd(c_all, w_ada[0], b_cols)
    (mod_all,) = all_gather([mod_cols], "gather_mod")
    mod = lax.dynamic_index_in_dim(mod_all, me, axis=1, keepdims=False).reshape(1, 3 * D)
    shift, scale, gate = mod[:, 0:D], mod[:, D:2 * D], mod[:, 2 * D:3 * D]

    grad_x, small, (r_win, r_pa, r_pb, r_wo) = _local_step(
        x[0], loss_target[0], shift, scale, gate, norm_w, conv_full, q_norm_w, k_norm_w,
        w_in[0].astype(BF16), [w_br_conv[0].astype(BF16), w_br_attn[0].astype(BF16), w_out[0].astype(BF16)], me_xyc)

    packed = jnp.concatenate(
        [small["dmod"], small["norm_w"], small["conv_w"].reshape(1, 3 * D), small["q_norm_w"], small["k_norm_w"],
         jnp.full((1, 128), small["loss"], F32)], axis=1)
    (packed_all,) = all_gather([packed], "gather_small")
    tot = sum_parts(packed_all)
    loss = tot[0, 7 * D + 2 * HD]
    dmod_all = packed_all[:, 0, 0:3 * D]
    g_b_ada = tot[:, 0:3 * D]
    g_norm_w = tot[:, 3 * D:4 * D]
    g_conv = lax.dynamic_slice(tot[:, 4 * D:7 * D].reshape(3, D), (0, me * 128), (3, 128))
    g_qn = tot[:, 7 * D:7 * D + HD]
    g_kn = tot[:, 7 * D + HD:7 * D + 2 * HD]
    g_w_ada = ada_bwd(c_all.T, lax.dynamic_slice(dmod_all, (0, me * ncol), (NDEV, ncol)))

    def upd(parts, w, m, v, name, rows):
        shape = w.shape
        w2, m2, v2 = (t.reshape(shape[-2:]) for t in (w, m, v))
        return [t.reshape(shape) for t in adamw(parts, w2, m2, v2, name, rows)]

    res = {
        "w_ada": upd(g_w_ada[None], w_ada, m_w_ada, v_w_ada, "adamw_w_ada", 256),
        "b_ada": upd(g_b_ada[None], b_ada, m_b_ada, v_b_ada, "adamw_b_ada", 1),
        "norm_w": upd(g_norm_w[None], norm_w, m_norm_w, v_norm_w, "adamw_norm_w", 1),
        "w_in": upd(r_win, w_in, m_w_in, v_w_in, "adamw_w_in", 128),
        "conv_w": upd(g_conv[None], conv_w, m_conv_w, v_conv_w, "adamw_conv_w", 3),
        "q_norm_w": upd(g_qn[None], q_norm_w, m_q_norm_w, v_q_norm_w, "adamw_q_norm_w", 1),
        "k_norm_w": upd(g_kn[None], k_norm_w, m_k_norm_w, v_k_norm_w, "adamw_k_norm_w", 1),
        "w_br_conv": upd(r_pa, w_br_conv, m_w_br_conv, v_w_br_conv, "adamw_w_br_conv", 128),
        "w_br_attn": upd(r_pb, w_br_attn, m_w_br_attn, v_w_br_attn, "adamw_w_br_attn", 512),
        "w_out": upd(r_wo, w_out, m_w_out, v_w_out, "adamw_w_out", 128),
    }
    names = ["w_ada", "b_ada", "norm_w", "w_in", "conv_w", "q_norm_w", "k_norm_w", "w_br_conv", "w_br_attn", "w_out"]
    return (loss, grad_x[None], *[res[n][0] for n in names], *[res[n][1] for n in names],
            *[res[n][2] for n in names], *[res[n][3] for n in names])
```

```python
import jax
import jax.numpy as jnp
from jax import lax
from jax.experimental import pallas as pl
from jax.experimental.pallas import tpu as pltpu

F32, BF16 = jnp.float32, jnp.bfloat16
D = 1024
NIN = 11264
NDEV = 8
SHARD = NIN // NDEV
HD = 64
NH = 8
QB = 128
CB = 512
CB_Q, CB_K, CB_V, CB_ZB = 8, 11, 14, 17
DILATIONS = (1, 4, 16)
EPS = 1e-6
NEG = -1e30
HALO = 16
LANES = 128
MESH = pl.DeviceIdType.MESH

ADAM_LR, ADAM_B1, ADAM_B2, ADAM_EPS, ADAM_WD, ADAM_STEP = 0.001, 0.9, 0.999, 1e-08, 0.01, 10

NT = (((1,), (1,)), ((), ()))
TN = (((0,), (0,)), ((), ()))


def _cp(sem, vmem_mb=48):
    return pltpu.CompilerParams(dimension_semantics=sem, vmem_limit_bytes=vmem_mb << 20)


def _silu(z):
    return z * jax.nn.sigmoid(z)


def _coords():
    return lax.axis_index("x"), lax.axis_index("y"), lax.axis_index("c")


def all_gather(arrs, name):
    n = len(arrs)

    def body(*refs):
        ins, outs = refs[:n], refs[n:2 * n]
        send_sems, recv_sems, local_sems = refs[2 * n:]
        x, y, c = _coords()
        me, sibling = (x, y, c), (x, y, 1 - c)
        chips = [(1 - x, y), (x, 1 - y), (1 - x, 1 - y)]

        def slot(a, dev):
            return outs[a].at[4 * dev[0] + 2 * dev[1] + dev[2]]

        def copy(a, k, block, to, src=None):
            return pltpu.make_async_remote_copy(
                src_ref=slot(a, block) if src is None else src, dst_ref=slot(a, block),
                send_sem=send_sems.at[a, k], recv_sem=recv_sems.at[a, k],
                device_id=to, device_id_type=MESH)

        mine = [pltpu.make_async_copy(ins[a], slot(a, me), local_sems.at[a]) for a in range(n)]
        for cp in mine:
            cp.start()
        first = []
        for a in range(n):
            first.append(copy(a, 0, me, sibling, src=ins[a]))
            first += [copy(a, 1 + j, me, (*chip, c), src=ins[a]) for j, chip in enumerate(chips)]
        for cp in first:
            cp.start()
        passed = []
        for j, chip in enumerate(chips):
            for a in range(n):
                copy(a, 1 + j, (*chip, c), me).wait_recv()
                fwd = copy(a, 4 + j, (*chip, c), sibling)
                fwd.start()
                passed.append(fwd)
        for a in range(n):
            copy(a, 0, sibling, me).wait_recv()
            for j, chip in enumerate(chips):
                copy(a, 4 + j, (*chip, 1 - c), me).wait_recv()
        for cp in first + passed:
            cp.wait_send()
        for cp in mine:
            cp.wait()

    any_spec = pl.BlockSpec(memory_space=pl.ANY)
    return pl.pallas_call(
        body, name=name,
        out_shape=[jax.ShapeDtypeStruct((NDEV,) + a.shape, a.dtype) for a in arrs],
        in_specs=[any_spec] * n, out_specs=[any_spec] * n,
        scratch_shapes=[pltpu.SemaphoreType.DMA((n, 7)), pltpu.SemaphoreType.DMA((n, 7)),
                        pltpu.SemaphoreType.DMA((n,))],
    )(*arrs)


FLIPS = [(fx, fy, fc) for fx in (0, 1) for fy in (0, 1) for fc in (0, 1)][1:]


def _flip(dev, f):
    return tuple(1 - v if b else v for v, b in zip(dev, f))


def _dev_index(dev):
    return 4 * dev[0] + 2 * dev[1] + dev[2]


def _chip_order(x, y, c):
    xor = lambda a, b: a + b - 2 * a * b
    return [(xor(x, 1 - c), xor(y, c)), (xor(x, c), xor(y, 1 - c)), (1 - x, 1 - y)]


def gather_order(me_xyc):
    x, y, c = me_xyc
    chips = _chip_order(x, y, c)
    devs = [(x, y, c), (x, y, 1 - c), (*chips[0], c), (*chips[1], c),
            (*chips[1], 1 - c), (*chips[0], 1 - c), (*chips[2], c), (*chips[2], 1 - c)]
    return jnp.stack([_dev_index(d) for d in devs]).astype(jnp.int32)


def scatter_order(me_xyc):
    devs = [_flip(me_xyc, f) for f in FLIPS] + [me_xyc]
    return jnp.stack([_dev_index(d) for d in devs]).astype(jnp.int32)


def ada_fwd(c_all, w_ada, b_cols):
    def body(c_ref, w_ref, b_ref, o_ref):
        a = _silu(c_ref[...]).astype(BF16)
        o_ref[...] = jnp.dot(a, w_ref[...].astype(BF16), preferred_element_type=F32) + b_ref[...]

    return pl.pallas_call(body, name="ada_fwd",
                          out_shape=jax.ShapeDtypeStruct((NDEV, w_ada.shape[1]), F32))(c_all, w_ada, b_cols)


def ada_bwd(c_all_t, dmod_cols):
    def body(c_ref, d_ref, o_ref):
        at = _silu(c_ref[...])
        acc = at[:, 0:1] * d_ref[0:1, :]
        for b in range(1, NDEV):
            acc = acc + at[:, b:b + 1] * d_ref[b:b + 1, :]
        o_ref[...] = acc

    return pl.pallas_call(body, name="ada_bwd",
                          out_shape=jax.ShapeDtypeStruct((D, dmod_cols.shape[1]), F32))(c_all_t, dmod_cols)


def sum_parts(parts):
    def body(p_ref, o_ref):
        acc = p_ref[0]
        for b in range(1, NDEV):
            acc = acc + p_ref[b]
        o_ref[...] = acc

    return pl.pallas_call(body, name="sum_parts",
                          out_shape=jax.ShapeDtypeStruct(parts.shape[1:], F32))(parts)


def adamw(parts, w, m, v, name, rows):
    n, r, ccols = parts.shape

    def body(p_ref, w_ref, m_ref, v_ref, g_ref, d_ref, nm_ref, nv_ref):
        g = p_ref[0].astype(F32)
        for b in range(1, n):
            g = g + p_ref[b].astype(F32)
        nm = ADAM_B1 * m_ref[...] + (1.0 - ADAM_B1) * g
        nv = ADAM_B2 * v_ref[...] + (1.0 - ADAM_B2) * (g * g)
        g_ref[...] = g
        nm_ref[...] = nm
        nv_ref[...] = nv
        m_hat = nm / (1.0 - ADAM_B1 ** ADAM_STEP)
        v_hat = nv / (1.0 - ADAM_B2 ** ADAM_STEP)
        d_ref[...] = -ADAM_LR * (m_hat / (jnp.sqrt(v_hat) + ADAM_EPS) + ADAM_WD * w_ref[...])

    blk = pl.BlockSpec((rows, ccols), lambda i: (i, 0))
    out = jax.ShapeDtypeStruct((r, ccols), F32)
    return pl.pallas_call(
        body, name=name, grid=(r // rows,),
        in_specs=[pl.BlockSpec((n, rows, ccols), lambda i: (0, i, 0)), blk, blk, blk],
        out_specs=[blk] * 4, out_shape=[out] * 4, compiler_params=_cp(("parallel",)))(parts, w, m, v)


def norm_fwd(x, nw, scale, shift, tm):
    s = x.shape[0]

    def body(x_ref, nw_ref, sc_ref, sh_ref, h_ref, ht_ref):
        xf = x_ref[...]
        r = lax.rsqrt(jnp.mean(xf * xf, axis=-1, keepdims=True) + EPS)
        h = (xf * r * nw_ref[...]) * (1.0 + sc_ref[...]) + sh_ref[...]
        h_ref[...] = h.astype(BF16)
        ht_ref[...] = h.T.astype(BF16)

    vec = pl.BlockSpec((1, D), lambda i: (0, 0))
    return pl.pallas_call(
        body, name="norm_fwd", grid=(s // tm,),
        in_specs=[pl.BlockSpec((tm, D), lambda i: (i, 0)), vec, vec, vec],
        out_specs=[pl.BlockSpec((tm, D), lambda i: (i, 0)), pl.BlockSpec((D, tm), lambda i: (0, i))],
        out_shape=[jax.ShapeDtypeStruct((s, D), BF16), jax.ShapeDtypeStruct((D, s), BF16)],
        compiler_params=_cp(("parallel",)))(x, nw, scale, shift)


def norm_bwd(dh, x, dy, nw, scale, tm):
    s = x.shape[0]

    def body(dh_ref, x_ref, dy_ref, nw_ref, sc_ref, gx_ref, st_ref):
        xf, g = x_ref[...], dh_ref[...]
        r = lax.rsqrt(jnp.mean(xf * xf, axis=-1, keepdims=True) + EPS)
        xh = xf * r
        dn = g * (1.0 + sc_ref[...])
        dxh = dn * nw_ref[...]
        gx_ref[...] = dy_ref[...] + r * (dxh - xh * jnp.mean(dxh * xh, axis=-1, keepdims=True))

        @pl.when(pl.program_id(0) == 0)
        def _():
            st_ref[...] = jnp.zeros_like(st_ref)

        st_ref[0:1, :] += jnp.sum(g, axis=0, keepdims=True)
        st_ref[1:2, :] += jnp.sum(g * xh * nw_ref[...], axis=0, keepdims=True)
        st_ref[2:3, :] += jnp.sum(dn * xh, axis=0, keepdims=True)

    vec = pl.BlockSpec((1, D), lambda i: (0, 0))
    row = pl.BlockSpec((tm, D), lambda i: (i, 0))
    return pl.pallas_call(
        body, name="norm_bwd", grid=(s // tm,),
        in_specs=[row, row, row, vec, vec],
        out_specs=[row, pl.BlockSpec((8, D), lambda i: (0, 0))],
        out_shape=[jax.ShapeDtypeStruct((s, D), F32), jax.ShapeDtypeStruct((8, D), F32)],
        compiler_params=_cp(("arbitrary",)))(dh, x, dy, nw, scale)


def proj_fwd_gather(h, w_shard, extras, order, tm):
    s = h.shape[0]
    ni = s // tm
    n = 1 + len(extras)
    mid = ni - 2

    def body(order_ref, h_ref, *refs):
        ins, o_ref, outs = refs[:n], refs[n], refs[n + 1:2 * n + 1]
        wbuf, send_sems, recv_sems, local_sems, load_sems = refs[2 * n + 1:]
        jj, i = pl.program_id(0), pl.program_id(1)
        x, y, c = _coords()
        me, sibling = (x, y, c), (x, y, 1 - c)
        chips = _chip_order(x, y, c)
        relayed = [(*chips[1], 1 - c), (*chips[0], 1 - c), (*chips[2], 1 - c)]

        def slot(a, dev):
            return outs[a].at[_dev_index(dev)]

        def copy(a, k, block, to, src=None):
            return pltpu.make_async_remote_copy(
                src_ref=slot(a, block) if src is None else src, dst_ref=slot(a, block),
                send_sem=send_sems.at[a, k], recv_sem=recv_sems.at[a, k], device_id=to, device_id_type=MESH)

        mine = [pltpu.make_async_copy(ins[a], slot(a, me), local_sems.at[a]) for a in range(n)]
        to_sibling = [copy(a, 0, me, sibling, src=ins[a]) for a in range(n)]
        to_chip = [[copy(a, 1 + j, me, (*ch, c), src=ins[a]) for a in range(n)] for j, ch in enumerate(chips)]
        passed = [[copy(a, 4 + j, (*ch, c), sibling) for a in range(n)] for j, ch in enumerate(chips)]

        def load(row):
            return pltpu.make_async_copy(outs[0].at[order_ref[row]], wbuf.at[row % 2], load_sems.at[row % 2])

        @pl.when((jj == 0) & (i == 0))
        def _():
            for cp in mine:
                cp.start()
            to_sibling[0].start()
            to_chip[0][0].start()
            pltpu.make_async_copy(ins[0], wbuf.at[0], load_sems.at[0]).start()

        for j in (1, 2):
            @pl.when((jj == j) & (i == 0))
            def _(j=j):
                to_chip[j][0].start()

        @pl.when((jj == 4) & (i == 0))
        def _():
            for a in range(1, n):
                to_sibling[a].start()
                for j in range(3):
                    to_chip[j][a].start()

        direct = {2: 0, 3: 1, 6: 2}
        relay = {4: 0, 5: 1, 7: 2}

        @pl.when((jj == 0) & (i == mid))
        def _():
            copy(0, 0, sibling, me).wait_recv()

        for row, j in direct.items():
            @pl.when((jj == row - 1) & (i == mid))
            def _(j=j):
                copy(0, 1 + j, (*chips[j], c), me).wait_recv()
                passed[j][0].start()

        for row, j in relay.items():
            @pl.when((jj == row - 1) & (i == mid))
            def _(j=j):
                copy(0, 4 + j, relayed[j], me).wait_recv()

        @pl.when((jj == NDEV - 1) & (i == 0))
        def _():
            for j in range(3):
                for a in range(1, n):
                    copy(a, 1 + j, (*chips[j], c), me).wait_recv()
                    passed[j][a].start()

        @pl.when((jj < NDEV - 1) & (i == mid))
        def _():
            load(jj + 1).start()

        @pl.when(i == 0)
        def _():
            load(jj).wait()

        o_ref[...] = jnp.dot(h_ref[...], wbuf[jj % 2], preferred_element_type=F32).astype(BF16)

        @pl.when((jj == NDEV - 1) & (i == ni - 1))
        def _():
            for a in range(1, n):
                copy(a, 0, sibling, me).wait_recv()
                for j in range(3):
                    copy(a, 4 + j, relayed[j], me).wait_recv()
            for a in range(n):
                mine[a].wait()
                to_sibling[a].wait_send()
                for j in range(3):
                    to_chip[j][a].wait_send()
                    passed[j][a].wait_send()

    any_spec = pl.BlockSpec(memory_space=pl.ANY)
    outs = pl.pallas_call(
        body, name="proj_fwd_gather",
        grid_spec=pltpu.PrefetchScalarGridSpec(
            num_scalar_prefetch=1, grid=(NDEV, ni),
            in_specs=[pl.BlockSpec((tm, D), lambda jj, i, o: (i, 0))] + [any_spec] * n,
            out_specs=[pl.BlockSpec((tm, SHARD), lambda jj, i, o: (i, o[jj]))] + [any_spec] * n,
            scratch_shapes=[pltpu.VMEM((2, D, SHARD), BF16), pltpu.SemaphoreType.DMA((n, 7)),
                            pltpu.SemaphoreType.DMA((n, 7)), pltpu.SemaphoreType.DMA((n,)),
                            pltpu.SemaphoreType.DMA((2,))]),
        out_shape=[jax.ShapeDtypeStruct((s, NIN), BF16), jax.ShapeDtypeStruct((NDEV, D, SHARD), BF16)]
                  + [jax.ShapeDtypeStruct((NDEV,) + e.shape, e.dtype) for e in extras],
        compiler_params=_cp(("arbitrary", "arbitrary")))(order, h, w_shard, *extras)
    return outs[0], outs[1], outs[2:]


def proj_bwd(ht, dproj, wg, smalls, order, tt):
    s = dproj.shape[0]
    nk = s // tt
    n = len(smalls)

    def body(order_ref, ht_ref, dp_ref, w_ref, *rest):
        small_in = rest[:n]
        dh_ref, gw_ref, rwin_ref = rest[n:n + 3]
        small_out = rest[n + 3:2 * n + 3]
        acc, stage, send_sems, recv_sems, local_sems, stage_sems = rest[2 * n + 3:]
        t, k = pl.program_id(0), pl.program_id(1)
        me_xyc = _coords()
        me = _dev_index(me_xyc)
        peers = [_flip(me_xyc, f) for f in FLIPS]

        def exchange(a, kf, src_arr, dst_arr):
            pid = _dev_index(peers[kf])
            mk = lambda dst: pltpu.make_async_remote_copy(
                src_ref=src_arr.at[pid], dst_ref=dst, send_sem=send_sems.at[a, kf], recv_sem=recv_sems.at[a, kf],
                device_id=peers[kf], device_id_type=MESH)
            return mk(dst_arr.at[me]), mk(dst_arr.at[pid])

        small_pairs = [exchange(1 + a, kf, small_in[a], small_out[a]) for kf in range(7) for a in range(n)]
        small_own = [pltpu.make_async_copy(small_in[a].at[me], small_out[a].at[me], local_sems.at[1 + a])
                     for a in range(n)]
        win_pairs = [exchange(0, kf, gw_ref, rwin_ref) for kf in range(7)]
        win_own = pltpu.make_async_copy(gw_ref.at[me], rwin_ref.at[me], local_sems.at[0])

        def to_hbm(jj):
            slab = me if jj == 7 else _dev_index(peers[jj])
            return pltpu.make_async_copy(stage.at[jj % 2], gw_ref.at[slab], stage_sems.at[jj % 2])

        @pl.when((t == 0) & (k == 0))
        def _():
            for cp in small_own:
                cp.start()
            for send, _ in small_pairs:
                send.start()

        @pl.when(t < NDEV)
        def _():
            p = jnp.dot(ht_ref[...], dp_ref[...], preferred_element_type=F32)

            @pl.when(k == 0)
            def _():
                acc[...] = p

            @pl.when(k > 0)
            def _():
                acc[...] += p

        for jj in range(NDEV):
            @pl.when((t == jj) & (k == nk - 1))
            def _(jj=jj):
                stage[jj % 2] = acc[...].astype(BF16)
                to_hbm(jj).start()

            @pl.when((t == jj + 1) & (k == 1))
            def _(jj=jj):
                to_hbm(jj).wait()
                if jj < 7:
                    win_pairs[jj][0].start()
                else:
                    win_own.start()

        @pl.when(t >= NDEV)
        def _():
            p = lax.dot_general(dp_ref[...], w_ref[...], NT, preferred_element_type=F32)

            @pl.when(k == 0)
            def _():
                dh_ref[...] = p

            @pl.when(k > 0)
            def _():
                dh_ref[...] += p

        @pl.when((t == 2 * NDEV - 1) & (k == nk - 1))
        def _():
            for _, recv in win_pairs + small_pairs:
                recv.wait_recv()
            for send, _ in win_pairs + small_pairs:
                send.wait_send()
            win_own.wait()
            for cp in small_own:
                cp.wait()

    any_spec = pl.BlockSpec(memory_space=pl.ANY)
    first = lambda t: t < NDEV
    outs = pl.pallas_call(
        body, name="proj_bwd",
        grid_spec=pltpu.PrefetchScalarGridSpec(
            num_scalar_prefetch=1, grid=(2 * NDEV, nk),
            in_specs=[pl.BlockSpec((D, tt), lambda t, k, o: (0, jnp.where(first(t), k, nk - 1))),
                      pl.BlockSpec((tt, SHARD), lambda t, k, o: (jnp.where(first(t), k, t - NDEV),
                                                                 jnp.where(first(t), o[jnp.minimum(t, NDEV - 1)], k))),
                      pl.BlockSpec((None, D, SHARD), lambda t, k, o: (jnp.where(first(t), 0, k), 0, 0))]
                     + [any_spec] * n,
            out_specs=[pl.BlockSpec((tt, D), lambda t, k, o: (jnp.where(first(t), 0, t - NDEV), 0))]
                      + [any_spec] * (2 + n),
            scratch_shapes=[pltpu.VMEM((D, SHARD), F32), pltpu.VMEM((2, D, SHARD), BF16),
                            pltpu.SemaphoreType.DMA((1 + n, 7)), pltpu.SemaphoreType.DMA((1 + n, 7)),
                            pltpu.SemaphoreType.DMA((1 + n,)), pltpu.SemaphoreType.DMA((2,))]),
        out_shape=[jax.ShapeDtypeStruct((s, D), F32), jax.ShapeDtypeStruct((NDEV, D, SHARD), BF16),
                   jax.ShapeDtypeStruct((NDEV, D, SHARD), BF16)]
                  + [jax.ShapeDtypeStruct(a.shape, a.dtype) for a in smalls],
        compiler_params=_cp(("arbitrary", "arbitrary"), 56))(order, ht, dproj, wg, *smalls)
    return outs[0], outs[2], outs[3:]


def matmul_tn(a, b, name, tk):
    s, m = a.shape
    n = b.shape[1]
    nk = s // tk

    def body(a_ref, b_ref, o_ref, acc_ref):
        k = pl.program_id(0)
        p = lax.dot_general(a_ref[...], b_ref[...], TN, preferred_element_type=F32)

        @pl.when(k == 0)
        def _():
            acc_ref[...] = p

        @pl.when(k > 0)
        def _():
            acc_ref[...] += p

        @pl.when(k == nk - 1)
        def _():
            o_ref[...] = acc_ref[...].astype(BF16)

    return pl.pallas_call(
        body, name=name, grid=(nk,),
        in_specs=[pl.BlockSpec((tk, m), lambda k: (k, 0)), pl.BlockSpec((tk, n), lambda k: (k, 0))],
        out_specs=pl.BlockSpec((m, n), lambda k: (0, 0)),
        out_shape=jax.ShapeDtypeStruct((m, n), BF16),
        scratch_shapes=[pltpu.VMEM((m, n), F32)],
        compiler_params=_cp(("arbitrary",)))(a, b)


def _head_matrices():
    lane = lax.broadcasted_iota(jnp.int32, (CB, CB), 0)
    col = lax.broadcasted_iota(jnp.int32, (CB, CB), 1)
    same = (lane // HD == col // HD).astype(BF16)
    lane_c = lax.broadcasted_iota(jnp.int32, (CB, LANES), 0)
    col_c = lax.broadcasted_iota(jnp.int32, (CB, LANES), 1)
    total = (lane_c // HD == col_c).astype(BF16)
    pick = (lane_c == col_c * HD).astype(BF16)
    return same, total, pick


def _head_sum(x, m_ref):
    return jnp.dot(x.astype(BF16), m_ref[...], preferred_element_type=F32)


def _dot_hilo(x, m_ref):
    hi = x.astype(BF16)
    lo = (x - hi.astype(F32)).astype(BF16)
    return (jnp.dot(hi, m_ref[...], preferred_element_type=F32)
            + jnp.dot(lo, m_ref[...], preferred_element_type=F32))


def _to_residue_major(val, buf, out_ref, dil):
    rows = out_ref.shape[1]
    for k in range(val.shape[1] // LANES):
        lanes = slice(k * LANES, (k + 1) * LANES)
        buf[k] = val[:, lanes]
        for r in range(dil):
            out_ref[r, :, lanes] = buf.at[k][pl.ds(r, rows, stride=dil), :].astype(out_ref.dtype)


def _from_residue_major(ref, buf, dil):
    if dil == 1:
        return ref[0].astype(F32)
    rows = ref.shape[1]
    for k in range(CB // LANES):
        for r in range(dil):
            buf.at[k][pl.ds(r, rows, stride=dil), :] = ref[r, :, k * LANES:(k + 1) * LANES].astype(F32)
    return jnp.concatenate([buf[k] for k in range(CB // LANES)], axis=1)


def qkv_prep(proj, qw8, kw8, same, tm):
    s = proj.shape[0]
    items = []
    for g, d in enumerate(DILATIONS):
        items += [(g, "q", CB_Q + g, d), (g, "k", CB_K + g, d)] + ([(g, "v", CB_V + g, d)] if d > 1 else [])
    n = len(items)

    def body(*refs):
        ins, (qw_ref, kw_ref, same_ref), outs, buf = refs[:n], refs[n:n + 3], refs[n + 3:2 * n + 3], refs[-1]
        for idx, (_, kind, _, dil) in enumerate(items):
            val = ins[idx][...].astype(F32)
            if kind != "v":
                r = lax.rsqrt(_head_sum(val * val, same_ref) * (1.0 / HD) + EPS)
                val = val * r * (qw_ref if kind == "q" else kw_ref)[...]
            if dil == 1:
                outs[idx][0] = val.astype(BF16)
            else:
                _to_residue_major(val, buf, outs[idx], dil)

    full = lambda a: pl.BlockSpec(a.shape, lambda i: (0, 0))
    outs = pl.pallas_call(
        body, name="qkv_prep", grid=(s // tm,),
        in_specs=[pl.BlockSpec((tm, CB), lambda i, cb=cb: (i, cb)) for _, _, cb, _ in items]
                 + [full(qw8), full(kw8), full(same)],
        out_specs=[pl.BlockSpec((d, tm // d, CB), lambda i: (0, i, 0)) for _, _, _, d in items],
        out_shape=[jax.ShapeDtypeStruct((d, s // d, CB), BF16) for _, _, _, d in items],
        scratch_shapes=[pltpu.VMEM((CB // LANES, tm, LANES), F32)],
        compiler_params=_cp(("parallel",)))(*([proj] * n), qw8 * (HD ** -0.5), kw8, same)
    srcs = [[None, None, (proj, CB_V + g)] for g in range(len(DILATIONS))]
    for (g, kind, _, _), o in zip(items, outs):
        srcs[g]["qkv".index(kind)] = (o.reshape(s, CB), 0)
    return srcs


def stats_prep(da, lc, dc, g, dil, tm):
    s = da.shape[0]
    rows = tm // dil

    def body(da_ref, lc_ref, dc_ref, dap_ref, lcp_ref, dcp_ref, lt_ref, dt_ref, buf):
        if dil == 1:
            dap_ref[0] = da_ref[...]
        else:
            _to_residue_major(da_ref[...].astype(F32), buf, dap_ref, dil)
        for src, dst, dst_t in ((lc_ref, lcp_ref, lt_ref), (dc_ref, dcp_ref, dt_ref)):
            buf[0] = src[...]
            for r in range(dil):
                piece = buf.at[0][pl.ds(r, rows, stride=dil), :] if dil > 1 else buf[0]
                dst[r] = piece
                dst_t[r] = piece.T[0:NH, :]

    row = lambda w: pl.BlockSpec((tm, w), lambda i: (i, 0))
    rm = lambda w: pl.BlockSpec((dil, rows, w), lambda i: (0, i, 0))
    tr = pl.BlockSpec((dil, NH, rows), lambda i: (0, 0, i))
    length = s // dil
    dap, lcp, dcp, lt, dt = pl.pallas_call(
        body, name=f"stats_prep_g{g}", grid=(s // tm,),
        in_specs=[row(CB), row(LANES), row(LANES)],
        out_specs=[rm(CB), rm(LANES), rm(LANES), tr, tr],
        out_shape=[jax.ShapeDtypeStruct((dil, length, CB), BF16)]
                  + [jax.ShapeDtypeStruct((dil, length, LANES), F32)] * 2
                  + [jax.ShapeDtypeStruct((dil, NH, length), F32)] * 2,
        scratch_shapes=[pltpu.VMEM((CB // LANES, tm, LANES), F32)],
        compiler_params=_cp(("parallel",)))(da, lc, dc)
    return (dap.reshape(s, CB), lcp.reshape(s, LANES), dcp.reshape(s, LANES),
            lt.reshape(dil * NH, length), dt.reshape(dil * NH, length))


def qkv_grads_to_dproj(dproj, proj, grads, qw8, kw8, same, tm):
    s = dproj.shape[0]
    ni = s // tm
    flat = [(t.reshape(d, s // d, CB), d, kind, 3 * kind + g)
            for g, d in enumerate(DILATIONS) for kind, t in enumerate(grads[g])]
    nf = len(flat)
    nraw = 2 * len(DILATIONS)

    def body(*refs):
        dp_hbm, raws, ins = refs[nraw + nf + 4], refs[1:1 + nraw], refs[1 + nraw:1 + nraw + nf]
        qw_ref, kw_ref, same_ref = refs[1 + nraw + nf:4 + nraw + nf]
        gw_ref, stage, buf, sems = refs[5 + nraw + nf:]
        i = pl.program_id(0)
        slot = i % 2

        def slab(step, sl):
            return pltpu.make_async_copy(
                stage.at[sl], dp_hbm.at[pl.ds(pl.multiple_of(step * tm, tm), tm), pl.ds(CB_Q * CB, 9 * CB)],
                sems.at[sl])

        @pl.when(i == 0)
        def _():
            gw_ref[...] = jnp.zeros_like(gw_ref)

        @pl.when(i >= 2)
        def _():
            slab(i - 2, slot).wait()

        for ref, (_, d, kind, jj) in zip(ins, flat):
            cols = slice(jj * CB, (jj + 1) * CB)
            dn = _from_residue_major(ref, buf, d)
            if kind == 2:
                stage[slot, :, cols] = dn.astype(BF16)
                continue
            t = raws[jj][...].astype(F32)
            r = lax.rsqrt(_head_sum(t * t, same_ref) * (1.0 / HD) + EPS)
            xh = t * r
            gw_ref[kind:kind + 1, :] += jnp.sum(dn * xh, axis=0, keepdims=True)
            dxh = dn * (qw_ref if kind == 0 else kw_ref)[...]
            mean = _head_sum(dxh * xh, same_ref) * (1.0 / HD)
            stage[slot, :, cols] = (r * (dxh - xh * mean)).astype(BF16)
        slab(i, slot).start()

        @pl.when(i == ni - 1)
        def _():
            slab(i - 1, 1 - slot).wait()
            slab(i, slot).wait()

    full = lambda a: pl.BlockSpec(a.shape, lambda i: (0, 0))
    any_spec = pl.BlockSpec(memory_space=pl.ANY)
    return pl.pallas_call(
        body, name="qkv_grads_to_dproj", grid=(ni,),
        in_specs=[any_spec] + [pl.BlockSpec((tm, CB), lambda i, jb=jb: (i, CB_Q + jb)) for jb in range(nraw)]
                 + [pl.BlockSpec((d, tm // d, CB), lambda i: (0, i, 0)) for _, d, _, _ in flat]
                 + [full(qw8), full(kw8), full(same)],
        out_specs=[any_spec, pl.BlockSpec((8, CB), lambda i: (0, 0))],
        out_shape=[jax.ShapeDtypeStruct((s, NIN), BF16), jax.ShapeDtypeStruct((8, CB), F32)],
        input_output_aliases={0: 0},
        scratch_shapes=[pltpu.VMEM((2, tm, 9 * CB), BF16), pltpu.VMEM((CB // LANES, tm, LANES), F32),
                        pltpu.SemaphoreType.DMA((2,))],
        compiler_params=_cp(("arbitrary",)))(
            dproj, *([proj] * nraw), *[t for t, _, _, _ in flat], qw8, kw8, same)


def _lane_lo():
    return lax.broadcasted_iota(jnp.int32, (1, 2 * HD), 1) < HD


def _stack_heads(t, lo):
    zero = jnp.zeros_like(t)
    return jnp.concatenate([jnp.where(lo, t, zero), jnp.where(lo, zero, t)], axis=0)


def _masks(other_ok):
    qi = lax.broadcasted_iota(jnp.int32, (QB, QB), 0)
    kj = lax.broadcasted_iota(jnp.int32, (QB, QB), 1)
    return (kj >= qi) & other_ok, kj <= qi


SUB = 4


def _attn_specs(nb, dil):
    steps = nb // SUB
    main = lambda cb, w=CB: pl.BlockSpec((SUB * QB, w), lambda r, s: (r * steps + s, cb))
    prev = lambda cb: pl.BlockSpec((QB, CB), lambda r, s: (jnp.maximum(r * nb + SUB * s - 1, 0), cb))
    nxt = lambda cb: pl.BlockSpec((QB, CB), lambda r, s: (jnp.minimum(r * nb + SUB * (s + 1), dil * nb - 1), cb))
    return main, prev, nxt


def attn_fwd(q_src, k_src, v_src, g, dil):
    s = q_src[0].shape[0]
    nb = s // dil // QB
    main, prev, _ = _attn_specs(nb, dil)

    def body(q_ref, kp_ref, k_ref, vp_ref, v_ref, o_ref, l_ref, kbuf, vbuf):
        step = pl.program_id(1)
        kbuf[0:QB], kbuf[QB:] = kp_ref[...], k_ref[...]
        vbuf[0:QB], vbuf[QB:] = vp_ref[...], v_ref[...]
        lo = _lane_lo()

        def block(j, carry):
            r0 = pl.multiple_of(j * QB, QB)
            rows, krows = pl.ds(r0, QB), pl.ds(r0, 2 * QB)
            m_prev, m_cur = _masks(step * SUB + j > 0)
            mask = jnp.concatenate([m_prev, m_cur], axis=1)
            mask = jnp.concatenate([mask, mask], axis=0)
            for i in range(NH // 2):
                sl = slice(2 * HD * i, 2 * HD * (i + 1))
                qs, ks, vv = q_ref[rows, sl], kbuf[krows, sl], vbuf[krows, sl]
                sc = lax.dot_general(_stack_heads(qs, lo), ks, NT, preferred_element_type=F32)
                sc = jnp.where(mask, sc, NEG)
                mx = jnp.max(sc, axis=-1, keepdims=True)
                p = jnp.exp(sc - mx)
                den = jnp.sum(p, axis=-1, keepdims=True)
                o = jnp.dot(p.astype(BF16), vv, preferred_element_type=F32) * (1.0 / den)
                lse = jnp.broadcast_to(mx + jnp.log(den), (2 * QB, 2 * HD))
                o_ref[rows, sl] = jnp.where(lo, o[:QB], o[QB:])
                l_ref[rows, sl] = jnp.where(lo, lse[:QB], lse[QB:])
            return carry

        lax.fori_loop(0, SUB, block, 0, unroll=True)

    out = jax.ShapeDtypeStruct((s, CB), F32)
    return pl.pallas_call(
        body, name=f"attn_fwd_g{g}", grid=(dil, nb // SUB),
        in_specs=[main(q_src[1]), prev(k_src[1]), main(k_src[1]), prev(v_src[1]), main(v_src[1])],
        out_specs=[main(0)] * 2, out_shape=[out, out],
        scratch_shapes=[pltpu.VMEM(((SUB + 1) * QB, CB), BF16)] * 2,
        compiler_params=_cp(("parallel", "parallel")))(q_src[0], k_src[0], k_src[0], v_src[0], v_src[0])


def attn_bwd_q(q_src, k_src, v_src, da, lc, dc, g, dil):
    s = q_src[0].shape[0]
    nb = s // dil // QB
    main, prev, _ = _attn_specs(nb, dil)

    def body(q_ref, kp_ref, k_ref, vp_ref, v_ref, da_ref, l_ref, d_ref, dq_ref, kbuf, vbuf):
        step = pl.program_id(1)
        kbuf[0:QB], kbuf[QB:] = kp_ref[...], k_ref[...]
        vbuf[0:QB], vbuf[QB:] = vp_ref[...], v_ref[...]
        lo = _lane_lo()

        def block(j, carry):
            r0 = pl.multiple_of(j * QB, QB)
            rows, krows = pl.ds(r0, QB), pl.ds(r0, 2 * QB)
            m_prev, m_cur = _masks(step * SUB + j > 0)
            mask = jnp.concatenate([m_prev, m_cur], axis=1)
            mask = jnp.concatenate([mask, mask], axis=0)
            lcols, dcols = l_ref[rows, :], d_ref[rows, :]
            for i in range(NH // 2):
                sl = slice(2 * HD * i, 2 * HD * (i + 1))
                qs, ks, vv, da2 = q_ref[rows, sl], kbuf[krows, sl], vbuf[krows, sl], da_ref[rows, sl]
                pair = lambda t: jnp.concatenate([t[:, 2 * i:2 * i + 1], t[:, 2 * i + 1:2 * i + 2]], axis=0)
                sc = lax.dot_general(_stack_heads(qs, lo), ks, NT, preferred_element_type=F32)
                sc = jnp.where(mask, sc, NEG)
                p = jnp.exp(sc - pair(lcols))
                dp = lax.dot_general(_stack_heads(da2, lo), vv, NT, preferred_element_type=F32)
                ds = p * (dp - pair(dcols))
                dq = jnp.dot(ds.astype(BF16), ks, preferred_element_type=F32)
                dq_ref[rows, sl] = (jnp.where(lo, dq[:QB], dq[QB:]) * (HD ** -0.5)).astype(BF16)
            return carry

        lax.fori_loop(0, SUB, block, 0, unroll=True)

    return pl.pallas_call(
        body, name=f"attn_bwd_q_g{g}", grid=(dil, nb // SUB),
        in_specs=[main(q_src[1]), prev(k_src[1]), main(k_src[1]), prev(v_src[1]), main(v_src[1]),
                  main(0), main(0, LANES), main(0, LANES)],
        out_specs=main(0), out_shape=jax.ShapeDtypeStruct((s, CB), BF16),
        scratch_shapes=[pltpu.VMEM(((SUB + 1) * QB, CB), BF16)] * 2,
        compiler_params=_cp(("parallel", "parallel")))(
            q_src[0], k_src[0], k_src[0], v_src[0], v_src[0], da, lc, dc)


def attn_bwd_kv(q_src, k_src, v_src, da, lt, dt, g, dil):
    s = q_src[0].shape[0]
    nb = s // dil // QB
    main, _, nxt = _attn_specs(nb, dil)

    def body(k_ref, v_ref, q_ref, qn_ref, da_ref, dan_ref, l_ref, ln_ref, d_ref, dn_ref, dk_ref, dv_ref,
             qbuf, dabuf, lbuf, dbuf):
        step = pl.program_id(1)
        qbuf[0:SUB * QB], qbuf[SUB * QB:] = q_ref[...], qn_ref[...]
        dabuf[0:SUB * QB], dabuf[SUB * QB:] = da_ref[...], dan_ref[...]
        for c in range(SUB):
            lbuf[c], dbuf[c] = l_ref[:, c * QB:(c + 1) * QB], d_ref[:, c * QB:(c + 1) * QB]
        lbuf[SUB], dbuf[SUB] = ln_ref[...], dn_ref[...]
        lo = _lane_lo()
        kj = lax.broadcasted_iota(jnp.int32, (QB, QB), 0)
        qi = lax.broadcasted_iota(jnp.int32, (QB, QB), 1)

        def block(j, carry):
            r0 = pl.multiple_of(j * QB, QB)
            rows, qrows = pl.ds(r0, QB), pl.ds(r0, 2 * QB)
            mask = jnp.concatenate([kj <= qi, (kj >= qi) & (step * SUB + j < nb - 1)], axis=1)
            mask = jnp.concatenate([mask, mask], axis=1)
            lrow = jnp.concatenate([lbuf[j], lbuf[j + 1]], axis=1)
            drow = jnp.concatenate([dbuf[j], dbuf[j + 1]], axis=1)
            for i in range(NH // 2):
                sl = slice(2 * HD * i, 2 * HD * (i + 1))
                q2, da2 = _stack_heads(qbuf[qrows, sl], lo), _stack_heads(dabuf[qrows, sl], lo)
                ks, vv = k_ref[rows, sl], v_ref[rows, sl]
                pair = lambda t: jnp.concatenate([t[2 * i:2 * i + 1, :], t[2 * i + 1:2 * i + 2, :]], axis=1)
                sc = lax.dot_general(ks, q2, NT, preferred_element_type=F32)
                sc = jnp.where(mask, sc, NEG)
                p = jnp.exp(sc - pair(lrow))
                dp = lax.dot_general(vv, da2, NT, preferred_element_type=F32)
                ds = p * (dp - pair(drow))
                dv_ref[rows, sl] = jnp.dot(p.astype(BF16), da2, preferred_element_type=F32).astype(BF16)
                dk_ref[rows, sl] = jnp.dot(ds.astype(BF16), q2, preferred_element_type=F32).astype(BF16)
            return carry

        lax.fori_loop(0, SUB, block, 0, unroll=True)

    steps = nb // SUB
    t_main = pl.BlockSpec((NH, SUB * QB), lambda r, s: (r, s))
    t_nxt = pl.BlockSpec((NH, QB), lambda r, s: (r, jnp.minimum(SUB * (s + 1), nb - 1)))
    out = jax.ShapeDtypeStruct((s, CB), BF16)
    return pl.pallas_call(
        body, name=f"attn_bwd_kv_g{g}", grid=(dil, steps),
        in_specs=[main(k_src[1]), main(v_src[1]), main(q_src[1]), nxt(q_src[1]),
                  main(0), nxt(0), t_main, t_nxt, t_main, t_nxt],
        out_specs=[main(0), main(0)], out_shape=[out, out],
        scratch_shapes=[pltpu.VMEM(((SUB + 1) * QB, CB), BF16)] * 2 + [pltpu.VMEM((SUB + 1, NH, QB), F32)] * 2,
        compiler_params=_cp(("parallel", "parallel")))(
            k_src[0], v_src[0], q_src[0], q_src[0], da, da, lt, lt, dt, dt)


def _conv_taps(u, u_prev, first):
    tm = u.shape[0]
    row = lax.broadcasted_iota(jnp.int32, (tm, 1), 0)
    up = jnp.where(first, 0.0, u_prev)
    u1 = jnp.where(row == 0, up[HALO - 1:HALO, :], pltpu.roll(u, 1, 0))
    u2 = jnp.where(row == 0, up[HALO - 2:HALO - 1, :],
                   jnp.where(row == 1, up[HALO - 1:HALO, :], pltpu.roll(u, 2, 0)))
    return u1, u2


def mid_fwd(proj, o_g, lse_g, conv_w, pick, tm):
    s = proj.shape[0]
    hb = tm // HALO

    def body(ba_ref, ca_ref, xa_ref, za_ref, cah_ref, xah_ref, zb_ref,
             o0, o1, o2, l0, l1, l2, w_ref, pick_ref, ya_ref, yb_ref, at_ref, lc_ref, buf_o, buf_l):
        first = pl.program_id(0) == 0
        u = ca_ref[...].astype(F32) * xa_ref[...].astype(F32)
        u1, u2 = _conv_taps(u, cah_ref[...].astype(F32) * xah_ref[...].astype(F32), first)
        conv = w_ref[0:1, :] * u2 + w_ref[1:2, :] * u1 + w_ref[2:3, :] * u
        ya_ref[...] = (ba_ref[...].astype(F32) * conv * _silu(za_ref[...].astype(F32))).astype(BF16)
        ls = [_from_residue_major(l, buf_l.at[g], d) for g, (l, d) in enumerate(zip((l0, l1, l2), DILATIONS))]
        mx = jnp.maximum(jnp.maximum(ls[0], ls[1]), ls[2])
        es = [jnp.exp(l - mx) for l in ls]
        den = es[0] + es[1] + es[2]
        num = jnp.zeros_like(den)
        for e, o, d in zip(es, (o0, o1, o2), DILATIONS):
            num = num + e * _from_residue_major(o, buf_o, d)
        attn = num / den
        at_ref[...] = attn
        lc_ref[...] = _dot_hilo(mx + jnp.log(den), pick_ref)
        yb_ref[...] = (attn * _silu(zb_ref[...].astype(F32))).astype(BF16)

    col = lambda j: pl.BlockSpec((tm, D), lambda i: (i, j))
    halo = lambda j: pl.BlockSpec((HALO, D), lambda i: (jnp.maximum(i * hb - 1, 0), j))
    loc = lambda w: pl.BlockSpec((tm, w), lambda i: (i, 0))
    rm = [pl.BlockSpec((d, tm // d, CB), lambda i: (0, i, 0)) for d in DILATIONS]
    rm_view = lambda ts: [t.reshape(d, s // d, CB) for t, d in zip(ts, DILATIONS)]
    return pl.pallas_call(
        body, name="mid_fwd", grid=(s // tm,),
        in_specs=[col(0), col(1), col(2), col(3), halo(1), halo(2),
                  pl.BlockSpec((tm, CB), lambda i: (i, CB_ZB))] + rm + rm
                 + [pl.BlockSpec((3, D), lambda i: (0, 0)), pl.BlockSpec(pick.shape, lambda i: (0, 0))],
        out_specs=[loc(D), loc(CB), loc(CB), loc(LANES)],
        out_shape=[jax.ShapeDtypeStruct((s, D), BF16), jax.ShapeDtypeStruct((s, CB), BF16),
                   jax.ShapeDtypeStruct((s, CB), F32), jax.ShapeDtypeStruct((s, LANES), F32)],
        scratch_shapes=[pltpu.VMEM((CB // LANES, tm, LANES), F32), pltpu.VMEM((3, CB // LANES, tm, LANES), F32)],
        compiler_params=_cp(("parallel",)))(
            proj, proj, proj, proj, proj, proj, proj, *rm_view(o_g), *rm_view(lse_g), conv_w, pick)


def mid_bwd(dproj, proj, dya, conv_w, tm):
    s = proj.shape[0]
    hb = tm // HALO
    nblk = s // tm
    last_h = s // HALO - 1

    def body(_, ba_ref, ca_ref, xa_ref, za_ref, cah_ref, xah_ref, ban_ref, zan_ref, dy_ref, dyn_ref, w_ref,
             o_ref, gw_ref):
        i = pl.program_id(0)
        ba, ca, xa, za = (t[...].astype(F32) for t in (ba_ref, ca_ref, xa_ref, za_ref))
        u = ca * xa
        u1, u2 = _conv_taps(u, cah_ref[...].astype(F32) * xah_ref[...].astype(F32), i == 0)
        w0, w1, w2 = w_ref[0:1, :], w_ref[1:2, :], w_ref[2:3, :]
        conv = w0 * u2 + w1 * u1 + w2 * u
        sg = jax.nn.sigmoid(za)
        sz = za * sg
        dy = dy_ref[...].astype(F32)
        dconv = dy * ba * sz
        dcn = dyn_ref[...].astype(F32) * ban_ref[...].astype(F32) * _silu(zan_ref[...].astype(F32))
        dcn = jnp.where(i == nblk - 1, 0.0, dcn)
        row = lax.broadcasted_iota(jnp.int32, (tm, 1), 0)
        d1 = jnp.where(row == tm - 1, dcn[0:1, :], pltpu.roll(dconv, tm - 1, 0))
        d2 = jnp.where(row == tm - 2, dcn[0:1, :],
                       jnp.where(row == tm - 1, dcn[1:2, :], pltpu.roll(dconv, tm - 2, 0)))
        du = w2 * dconv + w1 * d1 + w0 * d2
        o_ref[:, 0:D] = (dy * conv * sz).astype(BF16)
        o_ref[:, D:2 * D] = (du * xa).astype(BF16)
        o_ref[:, 2 * D:3 * D] = (du * ca).astype(BF16)
        o_ref[:, 3 * D:4 * D] = (dy * ba * conv * (sg * (1.0 + za * (1.0 - sg)))).astype(BF16)

        @pl.when(i == 0)
        def _():
            gw_ref[...] = jnp.zeros_like(gw_ref)

        gw_ref[0:1, :] += jnp.sum(dconv * u2, axis=0, keepdims=True)
        gw_ref[1:2, :] += jnp.sum(dconv * u1, axis=0, keepdims=True)
        gw_ref[2:3, :] += jnp.sum(dconv * u, axis=0, keepdims=True)

    col = lambda j: pl.BlockSpec((tm, D), lambda i: (i, j))
    halo_prev = lambda j: pl.BlockSpec((HALO, D), lambda i: (jnp.maximum(i * hb - 1, 0), j))
    halo_next = lambda j: pl.BlockSpec((HALO, D), lambda i: (jnp.minimum((i + 1) * hb, last_h), j))
    return pl.pallas_call(
        body, name="mid_bwd", grid=(nblk,),
        in_specs=[pl.BlockSpec(memory_space=pl.ANY), col(0), col(1), col(2), col(3),
                  halo_prev(1), halo_prev(2), halo_next(0), halo_next(3),
                  pl.BlockSpec((tm, D), lambda i: (i, 0)), halo_next(0),
                  pl.BlockSpec((3, D), lambda i: (0, 0))],
        out_specs=[pl.BlockSpec((tm, 4 * D), lambda i: (i, 0)), pl.BlockSpec((8, D), lambda i: (0, 0))],
        out_shape=[jax.ShapeDtypeStruct((s, NIN), BF16), jax.ShapeDtypeStruct((8, D), F32)],
        input_output_aliases={0: 0},
        compiler_params=_cp(("arbitrary",)))(dproj, proj, proj, proj, proj, proj, proj, proj, proj, dya, dya, conv_w)


def tail(proj, ya, yb, attn, x, target, gate, pa_w, pb_w, wo_w, total, tm):
    s = proj.shape[0]
    ni = s // tm
    ncol = NIN - CB_ZB * CB

    def body(ya_ref, yb_ref, ga_ref, gb_ref, zb_ref, at_ref, x_ref, t_ref, gate_ref, pa_ref, pb_ref, wo_ref,
             tot_ref, dp_hbm, dy_ref, dya_ref, da_ref, dc_ref, mg_ref, do_ref, dpa_ref, dpb_ref, st_ref,
             stage, sems):
        i = pl.program_id(0)
        slot = i % 2

        def slab(step, sl):
            return pltpu.make_async_copy(
                stage.at[sl], dp_hbm.at[pl.ds(pl.multiple_of(step * tm, tm), tm), pl.ds(CB_ZB * CB, ncol)],
                sems.at[sl])

        @pl.when(i == 0)
        def _():
            st_ref[...] = jnp.zeros_like(st_ref)

        @pl.when(i >= 2)
        def _():
            slab(i - 2, slot).wait()

        gate_v = gate_ref[...]
        pa = jnp.dot(ya_ref[...], pa_ref[...], preferred_element_type=F32)
        pb = jnp.dot(yb_ref[...], pb_ref[...], preferred_element_type=F32)
        sa = jax.nn.sigmoid(ga_ref[...].astype(F32))
        sb = jax.nn.sigmoid(gb_ref[...].astype(F32))
        merged = (sa * pa + sb * pb).astype(BF16)
        mg_ref[...] = merged
        out = jnp.dot(merged, wo_ref[...], preferred_element_type=F32)
        err = x_ref[...] + gate_v * out - t_ref[...]
        dy = err * (1.0 / D)
        dy_ref[...] = dy
        st_ref[0:1, :] += jnp.sum(dy * out, axis=0, keepdims=True)
        st_ref[1:2, :] += jnp.sum(err * err, axis=0, keepdims=True)
        dout = (gate_v * dy).astype(BF16)
        do_ref[...] = dout
        dmg = lax.dot_general(dout, wo_ref[...], NT, preferred_element_type=F32)
        dpa = (dmg * sa).astype(BF16)
        dpb = (dmg * sb).astype(BF16)
        dpa_ref[...] = dpa
        dpb_ref[...] = dpb
        stage[slot, :, CB:CB + D] = (dmg * pa * sa * (1.0 - sa)).astype(BF16)
        stage[slot, :, CB + D:] = (dmg * pb * sb * (1.0 - sb)).astype(BF16)
        dya_ref[...] = lax.dot_general(dpa, pa_ref[...], NT, preferred_element_type=F32).astype(BF16)
        dyb = lax.dot_general(dpb, pb_ref[...], NT, preferred_element_type=F32)
        zb = zb_ref[...].astype(F32)
        sg = jax.nn.sigmoid(zb)
        attn_v = at_ref[...]
        dattn = dyb * (zb * sg)
        da_ref[...] = dattn.astype(BF16)
        stage[slot, :, 0:CB] = (dyb * attn_v * (sg * (1.0 + zb * (1.0 - sg)))).astype(BF16)
        dc_ref[...] = _dot_hilo(dattn * attn_v, tot_ref)

        slab(i, slot).start()

        @pl.when(i == ni - 1)
        def _():
            slab(i - 1, 1 - slot).wait()
            slab(i, slot).wait()

    row = lambda w: pl.BlockSpec((tm, w), lambda i: (i, 0))
    pcol = lambda w, jb: pl.BlockSpec((tm, w), lambda i: (i, jb))
    full = lambda a: pl.BlockSpec(a.shape, lambda i: (0, 0))
    return pl.pallas_call(
        body, name="tail", grid=(ni,),
        in_specs=[row(D), row(CB), pcol(D, 9), pcol(D, 10), pcol(CB, CB_ZB), row(CB), row(D), row(D),
                  pl.BlockSpec((1, D), lambda i: (0, 0)), full(pa_w), full(pb_w), full(wo_w), full(total)],
        out_specs=[pl.BlockSpec(memory_space=pl.ANY),
                   row(D), row(D), row(CB), row(LANES), row(D), row(D), row(D), row(D),
                   pl.BlockSpec((8, D), lambda i: (0, 0))],
        out_shape=[jax.ShapeDtypeStruct((s, NIN), BF16), jax.ShapeDtypeStruct((s, D), F32),
                   jax.ShapeDtypeStruct((s, D), BF16), jax.ShapeDtypeStruct((s, CB), BF16),
                   jax.ShapeDtypeStruct((s, LANES), F32)] + [jax.ShapeDtypeStruct((s, D), BF16)] * 4
                  + [jax.ShapeDtypeStruct((8, D), F32)],
        scratch_shapes=[pltpu.VMEM((2, tm, ncol), BF16), pltpu.SemaphoreType.DMA((2,))],
        compiler_params=_cp(("arbitrary",), 56))(
            ya, yb, proj, proj, proj, attn, x, target, gate, pa_w, pb_w, wo_w, total)


def _local_step(x, target, shift, scale, gate, norm_w, conv_w, qw, kw, w_shard, small_shards, me_xyc):
    qw8, kw8 = jnp.tile(qw, (1, NH)), jnp.tile(kw, (1, NH))
    same, total, pick = _head_matrices()
    h, ht = norm_fwd(x, norm_w, scale, shift, 512)
    proj, wg, (pa_g, pb_g, wo_g) = proj_fwd_gather(h, w_shard, small_shards, gather_order(me_xyc), 1024)
    pa_w, wo_w = pa_g.reshape(D, D), wo_g.reshape(D, D)
    pb_w = pb_g.transpose(1, 0, 2).reshape(CB, D)
    srcs = qkv_prep(proj, qw8, kw8, same, 512)
    o_g, lse_g = zip(*[attn_fwd(*srcs[g], g, d) for g, d in enumerate(DILATIONS)])
    ya, yb, attn, lc = mid_fwd(proj, o_g, lse_g, conv_w, pick, 512)
    dproj, dy, dya, da, dc, merged, dout, dpa, dpb, st_tail = tail(
        proj, ya, yb, attn, x, target, gate, pa_w, pb_w, wo_w, total, 256)
    g_wo = matmul_tn(merged, dout, "grad_w_out", 1024)
    g_pa = matmul_tn(ya, dpa, "grad_w_br_conv", 1024)
    g_pb = matmul_tn(yb, dpb, "grad_w_br_attn", 1024)
    dproj, st_conv = mid_bwd(dproj, proj, dya, conv_w, 512)
    grads = []
    for g, d in enumerate(DILATIONS):
        da_p, lc_p, dc_p, lt, dt = stats_prep(da, lc, dc, g, d, 2048)
        dq = attn_bwd_q(*srcs[g], da_p, lc_p, dc_p, g, d)
        dk, dv = attn_bwd_kv(*srcs[g], da_p, lt, dt, g, d)
        grads.append((dq, dk, dv))
    dproj, gw_qk = qkv_grads_to_dproj(dproj, proj, grads, qw8, kw8, same, 512)
    slabs = [g_pa.reshape(NDEV, 128, D), g_pb.reshape(CB, NDEV, 128).transpose(1, 0, 2), g_wo.reshape(NDEV, 128, D)]
    dh, r_win, (r_pa, r_pb, r_wo) = proj_bwd(ht, dproj, wg, slabs, scatter_order(me_xyc), 1024)
    grad_x, st_norm = norm_bwd(dh, x, dy, norm_w, scale, 512)
    dmod = jnp.concatenate([st_norm[0:1], st_norm[1:2], st_tail[0:1]], axis=1)
    loss_part = (0.5 / D) * jnp.sum(st_tail[1])
    gw_heads = gw_qk[0:2].reshape(2, NH, HD).sum(axis=1)
    small = dict(dmod=dmod, norm_w=st_norm[2:3], conv_w=st_conv[0:3],
                 q_norm_w=gw_heads[0:1], k_norm_w=gw_heads[1:2], loss=loss_part)
    return grad_x, small, (r_win, r_pa, r_pb, r_wo)


def kernel(x, c, w_ada, b_ada, norm_w, w_in, conv_w, q_norm_w, k_norm_w, w_br_conv, w_br_attn, w_out, loss_target, m_w_ada, m_b_ada, m_norm_w, m_w_in, m_conv_w, m_q_norm_w, m_k_norm_w, m_w_br_conv, m_w_br_attn, m_w_out, v_w_ada, v_b_ada, v_norm_w, v_w_in, v_conv_w, v_q_norm_w, v_k_norm_w, v_w_br_conv, v_w_br_attn, v_w_out):
    me_xyc = (lax.axis_index("x"), lax.axis_index("y"), lax.axis_index("c"))
    me = _dev_index(me_xyc)
    ncol = w_ada.shape[2]

    conv_pad = jnp.zeros((8, 128), F32).at[0:3].set(conv_w[0])
    c_all, conv_all = all_gather([c, conv_pad], "gather_cond")
    conv_full = conv_all[:, 0:3].transpose(1, 0, 2).reshape(3, D)
    c_all = c_all.reshape(NDEV, D)

    b_cols = lax.dynamic_slice(b_ada, (0, me * ncol), (1, ncol))
    mod_cols = ada_fwd(c_all, w_ada[0], b_cols)
    (mod_all,) = all_gather([mod_cols], "gather_mod")
    mod = lax.dynamic_index_in_dim(mod_all, me, axis=1, keepdims=False).reshape(1, 3 * D)
    shift, scale, gate = mod[:, 0:D], mod[:, D:2 * D], mod[:, 2 * D:3 * D]

    grad_x, small, (r_win, r_pa, r_pb, r_wo) = _local_step(
        x[0], loss_target[0], shift, scale, gate, norm_w, conv_full, q_norm_w, k_norm_w,
        w_in[0].astype(BF16), [w_br_conv[0].astype(BF16), w_br_attn[0].astype(BF16), w_out[0].astype(BF16)], me_xyc)

    packed = jnp.concatenate(
        [small["dmod"], small["norm_w"], small["conv_w"].reshape(1, 3 * D), small["q_norm_w"], small["k_norm_w"],
         jnp.full((1, 128), small["loss"], F32)], axis=1)
    (packed_all,) = all_gather([packed], "gather_small")
    tot = sum_parts(packed_all)
    loss = tot[0, 7 * D + 2 * HD]
    dmod_all = packed_all[:, 0, 0:3 * D]
    g_b_ada = tot[:, 0:3 * D]
    g_norm_w = tot[:, 3 * D:4 * D]
    g_conv = lax.dynamic_slice(tot[:, 4 * D:7 * D].reshape(3, D), (0, me * 128), (3, 128))
    g_qn = tot[:, 7 * D:7 * D + HD]
    g_kn = tot[:, 7 * D + HD:7 * D + 2 * HD]
    g_w_ada = ada_bwd(c_all.T, lax.dynamic_slice(dmod_all, (0, me * ncol), (NDEV, ncol)))

    def upd(parts, w, m, v, name, rows):
        shape = w.shape
        w2, m2, v2 = (t.reshape(shape[-2:]) for t in (w, m, v))
        return [t.reshape(shape) for t in adamw(parts, w2, m2, v2, name, rows)]

    res = {
        "w_ada": upd(g_w_ada[None], w_ada, m_w_ada, v_w_ada, "adamw_w_ada", 256),
        "b_ada": upd(g_b_ada[None], b_ada, m_b_ada, v_b_ada, "adamw_b_ada", 1),
        "norm_w": upd(g_norm_w[None], norm_w, m_norm_w, v_norm_w, "adamw_norm_w", 1),
        "w_in": upd(r_win, w_in, m_w_in, v_w_in, "adamw_w_in", 128),
        "conv_w": upd(g_conv[None], conv_w, m_conv_w, v_conv_w, "adamw_conv_w", 3),
        "q_norm_w": upd(g_qn[None], q_norm_w, m_q_norm_w, v_q_norm_w, "adamw_q_norm_w", 1),
        "k_norm_w": upd(g_kn[None], k_norm_w, m_k_norm_w, v_k_norm_w, "adamw_k_norm_w", 1),
        "w_br_conv": upd(r_pa, w_br_conv, m_w_br_conv, v_w_br_conv, "adamw_w_br_conv", 128),
        "w_br_attn": upd(r_pb, w_br_attn, m_w_br_attn, v_w_br_attn, "adamw_w_br_attn", 512),
        "w_out": upd(r_wo, w_out, m_w_out, v_w_out, "adamw_w_out", 128),
    }
    names = ["w_ada", "b_ada", "norm_w", "w_in", "conv_w", "q_norm_w", "k_norm_w", "w_br_conv", "w_br_attn", "w_out"]
    return (loss, grad_x[None], *[res[n][0] for n in names], *[res[n][1] for n in names],
            *[res[n][2] for n in names], *[res[n][3] for n in names])
```

```python
import jax
import jax.numpy as jnp
from jax import lax
from jax.experimental import pallas as pl
from jax.experimental.pallas import tpu as pltpu

F32, BF16 = jnp.float32, jnp.bfloat16
D = 1024
NIN = 11264
NDEV = 8
SHARD = NIN // NDEV
HD = 64
NH = 8
QB = 128
CB = 512
CB_Q, CB_K, CB_V, CB_ZB = 8, 11, 14, 17
DILATIONS = (1, 4, 16)
EPS = 1e-6
NEG = -1e30
HALO = 16
LANES = 128
MESH = pl.DeviceIdType.MESH

ADAM_LR, ADAM_B1, ADAM_B2, ADAM_EPS, ADAM_WD, ADAM_STEP = 0.001, 0.9, 0.999, 1e-08, 0.01, 10

NT = (((1,), (1,)), ((), ()))
TN = (((0,), (0,)), ((), ()))


def _cp(sem, vmem_mb=48):
    return pltpu.CompilerParams(dimension_semantics=sem, vmem_limit_bytes=vmem_mb << 20)


def _silu(z):
    return z * jax.nn.sigmoid(z)


def _coords():
    return lax.axis_index("x"), lax.axis_index("y"), lax.axis_index("c")


def all_gather(arrs, name):
    n = len(arrs)

    def body(*refs):
        ins, outs = refs[:n], refs[n:2 * n]
        send_sems, recv_sems, local_sems = refs[2 * n:]
        x, y, c = _coords()
        me, sibling = (x, y, c), (x, y, 1 - c)
        chips = [(1 - x, y), (x, 1 - y), (1 - x, 1 - y)]

        def slot(a, dev):
            return outs[a].at[4 * dev[0] + 2 * dev[1] + dev[2]]

        def copy(a, k, block, to, src=None):
            return pltpu.make_async_remote_copy(
                src_ref=slot(a, block) if src is None else src, dst_ref=slot(a, block),
                send_sem=send_sems.at[a, k], recv_sem=recv_sems.at[a, k],
                device_id=to, device_id_type=MESH)

        mine = [pltpu.make_async_copy(ins[a], slot(a, me), local_sems.at[a]) for a in range(n)]
        for cp in mine:
            cp.start()
        first = []
        for a in range(n):
            first.append(copy(a, 0, me, sibling, src=ins[a]))
            first += [copy(a, 1 + j, me, (*chip, c), src=ins[a]) for j, chip in enumerate(chips)]
        for cp in first:
            cp.start()
        passed = []
        for j, chip in enumerate(chips):
            for a in range(n):
                copy(a, 1 + j, (*chip, c), me).wait_recv()
                fwd = copy(a, 4 + j, (*chip, c), sibling)
                fwd.start()
                passed.append(fwd)
        for a in range(n):
            copy(a, 0, sibling, me).wait_recv()
            for j, chip in enumerate(chips):
                copy(a, 4 + j, (*chip, 1 - c), me).wait_recv()
        for cp in first + passed:
            cp.wait_send()
        for cp in mine:
            cp.wait()

    any_spec = pl.BlockSpec(memory_space=pl.ANY)
    return pl.pallas_call(
        body, name=name,
        out_shape=[jax.ShapeDtypeStruct((NDEV,) + a.shape, a.dtype) for a in arrs],
        in_specs=[any_spec] * n, out_specs=[any_spec] * n,
        scratch_shapes=[pltpu.SemaphoreType.DMA((n, 7)), pltpu.SemaphoreType.DMA((n, 7)),
                        pltpu.SemaphoreType.DMA((n,))],
    )(*arrs)


FLIPS = [(fx, fy, fc) for fx in (0, 1) for fy in (0, 1) for fc in (0, 1)][1:]


def _flip(dev, f):
    return tuple(1 - v if b else v for v, b in zip(dev, f))


def _dev_index(dev):
    return 4 * dev[0] + 2 * dev[1] + dev[2]


def _chip_order(x, y, c):
    xor = lambda a, b: a + b - 2 * a * b
    return [(xor(x, 1 - c), xor(y, c)), (xor(x, c), xor(y, 1 - c)), (1 - x, 1 - y)]


def gather_order(me_xyc):
    x, y, c = me_xyc
    chips = _chip_order(x, y, c)
    devs = [(x, y, c), (x, y, 1 - c), (*chips[0], c), (*chips[1], c),
            (*chips[1], 1 - c), (*chips[0], 1 - c), (*chips[2], c), (*chips[2], 1 - c)]
    return jnp.stack([_dev_index(d) for d in devs]).astype(jnp.int32)


def scatter_order(me_xyc):
    devs = [_flip(me_xyc, f) for f in FLIPS] + [me_xyc]
    return jnp.stack([_dev_index(d) for d in devs]).astype(jnp.int32)


def ada_fwd(c_all, w_ada, b_cols):
    def body(c_ref, w_ref, b_ref, o_ref):
        a = _silu(c_ref[...]).astype(BF16)
        o_ref[...] = jnp.dot(a, w_ref[...].astype(BF16), preferred_element_type=F32) + b_ref[...]

    return pl.pallas_call(body, name="ada_fwd",
                          out_shape=jax.ShapeDtypeStruct((NDEV, w_ada.shape[1]), F32))(c_all, w_ada, b_cols)


def ada_bwd(c_all_t, dmod_cols):
    def body(c_ref, d_ref, o_ref):
        at = _silu(c_ref[...])
        acc = at[:, 0:1] * d_ref[0:1, :]
        for b in range(1, NDEV):
            acc = acc + at[:, b:b + 1] * d_ref[b:b + 1, :]
        o_ref[...] = acc

    return pl.pallas_call(body, name="ada_bwd",
                          out_shape=jax.ShapeDtypeStruct((D, dmod_cols.shape[1]), F32))(c_all_t, dmod_cols)


def sum_parts(parts):
    def body(p_ref, o_ref):
        acc = p_ref[0]
        for b in range(1, NDEV):
            acc = acc + p_ref[b]
        o_ref[...] = acc

    return pl.pallas_call(body, name="sum_parts",
                          out_shape=jax.ShapeDtypeStruct(parts.shape[1:], F32))(parts)


def adamw(parts, w, m, v, name, rows):
    n, r, ccols = parts.shape

    def body(p_ref, w_ref, m_ref, v_ref, g_ref, d_ref, nm_ref, nv_ref):
        g = p_ref[0].astype(F32)
        for b in range(1, n):
            g = g + p_ref[b].astype(F32)
        nm = ADAM_B1 * m_ref[...] + (1.0 - ADAM_B1) * g
        nv = ADAM_B2 * v_ref[...] + (1.0 - ADAM_B2) * (g * g)
        g_ref[...] = g
        nm_ref[...] = nm
        nv_ref[...] = nv
        m_hat = nm / (1.0 - ADAM_B1 ** ADAM_STEP)
        v_hat = nv / (1.0 - ADAM_B2 ** ADAM_STEP)
        d_ref[...] = -ADAM_LR * (m_hat / (jnp.sqrt(v_hat) + ADAM_EPS) + ADAM_WD * w_ref[...])

    blk = pl.BlockSpec((rows, ccols), lambda i: (i, 0))
    out = jax.ShapeDtypeStruct((r, ccols), F32)
    return pl.pallas_call(
        body, name=name, grid=(r // rows,),
        in_specs=[pl.BlockSpec((n, rows, ccols), lambda i: (0, i, 0)), blk, blk, blk],
        out_specs=[blk] * 4, out_shape=[out] * 4, compiler_params=_cp(("parallel",)))(parts, w, m, v)


def norm_fwd(x, nw, scale, shift, tm):
    s = x.shape[0]

    def body(x_ref, nw_ref, sc_ref, sh_ref, h_ref, ht_ref):
        xf = x_ref[...]
        r = lax.rsqrt(jnp.mean(xf * xf, axis=-1, keepdims=True) + EPS)
        h = (xf * r * nw_ref[...]) * (1.0 + sc_ref[...]) + sh_ref[...]
        h_ref[...] = h.astype(BF16)
        ht_ref[...] = h.T.astype(BF16)

    vec = pl.BlockSpec((1, D), lambda i: (0, 0))
    return pl.pallas_call(
        body, name="norm_fwd", grid=(s // tm,),
        in_specs=[pl.BlockSpec((tm, D), lambda i: (i, 0)), vec, vec, vec],
        out_specs=[pl.BlockSpec((tm, D), lambda i: (i, 0)), pl.BlockSpec((D, tm), lambda i: (0, i))],
        out_shape=[jax.ShapeDtypeStruct((s, D), BF16), jax.ShapeDtypeStruct((D, s), BF16)],
        compiler_params=_cp(("parallel",)))(x, nw, scale, shift)


def norm_bwd(dh, x, dy, nw, scale, tm):
    s = x.shape[0]

    def body(dh_ref, x_ref, dy_ref, nw_ref, sc_ref, gx_ref, st_ref):
        xf, g = x_ref[...], dh_ref[...]
        r = lax.rsqrt(jnp.mean(xf * xf, axis=-1, keepdims=True) + EPS)
        xh = xf * r
        dn = g * (1.0 + sc_ref[...])
        dxh = dn * nw_ref[...]
        gx_ref[...] = dy_ref[...] + r * (dxh - xh * jnp.mean(dxh * xh, axis=-1, keepdims=True))

        @pl.when(pl.program_id(0) == 0)
        def _():
            st_ref[...] = jnp.zeros_like(st_ref)

        st_ref[0:1, :] += jnp.sum(g, axis=0, keepdims=True)
        st_ref[1:2, :] += jnp.sum(g * xh * nw_ref[...], axis=0, keepdims=True)
        st_ref[2:3, :] += jnp.sum(dn * xh, axis=0, keepdims=True)

    vec = pl.BlockSpec((1, D), lambda i: (0, 0))
    row = pl.BlockSpec((tm, D), lambda i: (i, 0))
    return pl.pallas_call(
        body, name="norm_bwd", grid=(s // tm,),
        in_specs=[row, row, row, vec, vec],
        out_specs=[row, pl.BlockSpec((8, D), lambda i: (0, 0))],
        out_shape=[jax.ShapeDtypeStruct((s, D), F32), jax.ShapeDtypeStruct((8, D), F32)],
        compiler_params=_cp(("arbitrary",)))(dh, x, dy, nw, scale)


def proj_fwd_gather(h, w_shard, extras, order, tm):
    s = h.shape[0]
    ni = s // tm
    n = 1 + len(extras)
    mid = ni - 2

    def body(order_ref, h_ref, *refs):
        ins, o_ref, outs = refs[:n], refs[n], refs[n + 1:2 * n + 1]
        wbuf, send_sems, recv_sems, local_sems, load_sems = refs[2 * n + 1:]
        jj, i = pl.program_id(0), pl.program_id(1)
        x, y, c = _coords()
        me, sibling = (x, y, c), (x, y, 1 - c)
        chips = _chip_order(x, y, c)
        relayed = [(*chips[1], 1 - c), (*chips[0], 1 - c), (*chips[2], 1 - c)]

        def slot(a, dev):
            return outs[a].at[_dev_index(dev)]

        def copy(a, k, block, to, src=None):
            return pltpu.make_async_remote_copy(
                src_ref=slot(a, block) if src is None else src, dst_ref=slot(a, block),
                send_sem=send_sems.at[a, k], recv_sem=recv_sems.at[a, k], device_id=to, device_id_type=MESH)

        mine = [pltpu.make_async_copy(ins[a], slot(a, me), local_sems.at[a]) for a in range(n)]
        to_sibling = [copy(a, 0, me, sibling, src=ins[a]) for a in range(n)]
        to_chip = [[copy(a, 1 + j, me, (*chips[j], c), src=ins[a]) for a in range(n)] for j in range(2)]
        onward = [copy(a, 3, (*chips[1], c), (*chips[0], c)) for a in range(n)]
        passed = [[copy(a, 4 + j, (*ch, c), sibling) for a in range(n)] for j, ch in enumerate(chips)]
        sends = lambda a: [to_sibling[a], to_chip[0][a], to_chip[1][a], onward[a]] + [passed[j][a] for j in range(3)]

        def arrived(a, j):
            copy(a, 1 + j, (*chips[j], c), me).wait_recv()

        def load(row):
            return pltpu.make_async_copy(outs[0].at[order_ref[row]], wbuf.at[row % 2], load_sems.at[row % 2])

        @pl.when((jj == 0) & (i == 0))
        def _():
            for cp in mine:
                cp.start()
            to_sibling[0].start()
            to_chip[0][0].start()
            pltpu.make_async_copy(ins[0], wbuf.at[0], load_sems.at[0]).start()

        @pl.when((jj == 1) & (i == 0))
        def _():
            to_chip[1][0].start()

        @pl.when((jj == 4) & (i == 0))
        def _():
            for a in range(1, n):
                to_sibling[a].start()
                to_chip[0][a].start()
                to_chip[1][a].start()

        direct = {2: 0, 3: 1, 6: 2}
        relay = {4: 0, 5: 1, 7: 2}

        @pl.when((jj == 0) & (i == mid))
        def _():
            copy(0, 0, sibling, me).wait_recv()

        for row, j in direct.items():
            @pl.when((jj == row - 1) & (i == mid))
            def _(j=j):
                arrived(0, j)
                passed[j][0].start()
                if j == 1:
                    onward[0].start()

        for row, j in relay.items():
            @pl.when((jj == row - 1) & (i == mid))
            def _(j=j):
                copy(0, 4 + j, relayed[j], me).wait_recv()

        @pl.when((jj == NDEV - 1) & (i == 0))
        def _():
            for a in range(1, n):
                arrived(a, 1)
                onward[a].start()
                passed[1][a].start()
                arrived(a, 0)
                passed[0][a].start()

        @pl.when((jj < NDEV - 1) & (i == mid))
        def _():
            load(jj + 1).start()

        @pl.when(i == 0)
        def _():
            load(jj).wait()

        o_ref[...] = jnp.dot(h_ref[...], wbuf[jj % 2], preferred_element_type=F32).astype(BF16)

        @pl.when((jj == NDEV - 1) & (i == ni - 1))
        def _():
            for a in range(1, n):
                arrived(a, 2)
                passed[2][a].start()
            for a in range(1, n):
                copy(a, 0, sibling, me).wait_recv()
                for j in range(3):
                    copy(a, 4 + j, relayed[j], me).wait_recv()
            for a in range(n):
                mine[a].wait()
                for cp in sends(a):
                    cp.wait_send()

    any_spec = pl.BlockSpec(memory_space=pl.ANY)
    outs = pl.pallas_call(
        body, name="proj_fwd_gather",
        grid_spec=pltpu.PrefetchScalarGridSpec(
            num_scalar_prefetch=1, grid=(NDEV, ni),
            in_specs=[pl.BlockSpec((tm, D), lambda jj, i, o: (i, 0))] + [any_spec] * n,
            out_specs=[pl.BlockSpec((tm, SHARD), lambda jj, i, o: (i, o[jj]))] + [any_spec] * n,
            scratch_shapes=[pltpu.VMEM((2, D, SHARD), BF16), pltpu.SemaphoreType.DMA((n, 7)),
                            pltpu.SemaphoreType.DMA((n, 7)), pltpu.SemaphoreType.DMA((n,)),
                            pltpu.SemaphoreType.DMA((2,))]),
        out_shape=[jax.ShapeDtypeStruct((s, NIN), BF16), jax.ShapeDtypeStruct((NDEV, D, SHARD), BF16)]
                  + [jax.ShapeDtypeStruct((NDEV,) + e.shape, e.dtype) for e in extras],
        compiler_params=_cp(("arbitrary", "arbitrary")))(order, h, w_shard, *extras)
    return outs[0], outs[1], outs[2:]


def proj_bwd(ht, dproj, wg, smalls, order, tt):
    s = dproj.shape[0]
    nk = s // tt
    n = len(smalls)

    def body(order_ref, ht_ref, dp_ref, w_ref, *rest):
        small_in = rest[:n]
        dh_ref, gw_ref, rwin_ref = rest[n:n + 3]
        small_out = rest[n + 3:2 * n + 3]
        acc, stage, send_sems, recv_sems, local_sems, stage_sems = rest[2 * n + 3:]
        t, k = pl.program_id(0), pl.program_id(1)
        me_xyc = _coords()
        me = _dev_index(me_xyc)
        peers = [_flip(me_xyc, f) for f in FLIPS]

        def exchange(a, kf, src_arr, dst_arr):
            pid = _dev_index(peers[kf])
            mk = lambda dst: pltpu.make_async_remote_copy(
                src_ref=src_arr.at[pid], dst_ref=dst, send_sem=send_sems.at[a, kf], recv_sem=recv_sems.at[a, kf],
                device_id=peers[kf], device_id_type=MESH)
            return mk(dst_arr.at[me]), mk(dst_arr.at[pid])

        small_pairs = [exchange(1 + a, kf, small_in[a], small_out[a]) for kf in range(7) for a in range(n)]
        small_own = [pltpu.make_async_copy(small_in[a].at[me], small_out[a].at[me], local_sems.at[1 + a])
                     for a in range(n)]
        win_pairs = [exchange(0, kf, gw_ref, rwin_ref) for kf in range(7)]
        win_own = pltpu.make_async_copy(gw_ref.at[me], rwin_ref.at[me], local_sems.at[0])

        def to_hbm(jj):
            slab = me if jj == 7 else _dev_index(peers[jj])
            return pltpu.make_async_copy(stage.at[jj % 2], gw_ref.at[slab], stage_sems.at[jj % 2])

        @pl.when((t == 0) & (k == 0))
        def _():
            for cp in small_own:
                cp.start()
            for send, _ in small_pairs:
                send.start()

        @pl.when(t < NDEV)
        def _():
            p = jnp.dot(ht_ref[...], dp_ref[...], preferred_element_type=F32)

            @pl.when(k == 0)
            def _():
                acc[...] = p

            @pl.when(k > 0)
            def _():
                acc[...] += p

        for jj in range(NDEV):
            @pl.when((t == jj) & (k == nk - 1))
            def _(jj=jj):
                stage[jj % 2] = acc[...].astype(BF16)
                to_hbm(jj).start()

            @pl.when((t == jj + 1) & (k == 1))
            def _(jj=jj):
                to_hbm(jj).wait()
                if jj < 7:
                    win_pairs[jj][0].start()
                else:
                    win_own.start()

        @pl.when(t >= NDEV)
        def _():
            p = lax.dot_general(dp_ref[...], w_ref[...], NT, preferred_element_type=F32)

            @pl.when(k == 0)
            def _():
                dh_ref[...] = p

            @pl.when(k > 0)
            def _():
                dh_ref[...] += p

        @pl.when((t == 2 * NDEV - 1) & (k == nk - 1))
        def _():
            for _, recv in win_pairs + small_pairs:
                recv.wait_recv()
            for send, _ in win_pairs + small_pairs:
                send.wait_send()
            win_own.wait()
            for cp in small_own:
                cp.wait()

    any_spec = pl.BlockSpec(memory_space=pl.ANY)
    first = lambda t: t < NDEV
    outs = pl.pallas_call(
        body, name="proj_bwd",
        grid_spec=pltpu.PrefetchScalarGridSpec(
            num_scalar_prefetch=1, grid=(2 * NDEV, nk),
            in_specs=[pl.BlockSpec((D, tt), lambda t, k, o: (0, jnp.where(first(t), k, nk - 1))),
                      pl.BlockSpec((tt, SHARD), lambda t, k, o: (jnp.where(first(t), k, t - NDEV),
                                                                 jnp.where(first(t), o[jnp.minimum(t, NDEV - 1)], k))),
                      pl.BlockSpec((None, D, SHARD), lambda t, k, o: (jnp.where(first(t), 0, k), 0, 0))]
                     + [any_spec] * n,
            out_specs=[pl.BlockSpec((tt, D), lambda t, k, o: (jnp.where(first(t), 0, t - NDEV), 0))]
                      + [any_spec] * (2 + n),
            scratch_shapes=[pltpu.VMEM((D, SHARD), F32), pltpu.VMEM((2, D, SHARD), BF16),
                            pltpu.SemaphoreType.DMA((1 + n, 7)), pltpu.SemaphoreType.DMA((1 + n, 7)),
                            pltpu.SemaphoreType.DMA((1 + n,)), pltpu.SemaphoreType.DMA((2,))]),
        out_shape=[jax.ShapeDtypeStruct((s, D), F32), jax.ShapeDtypeStruct((NDEV, D, SHARD), BF16),
                   jax.ShapeDtypeStruct((NDEV, D, SHARD), BF16)]
                  + [jax.ShapeDtypeStruct(a.shape, a.dtype) for a in smalls],
        compiler_params=_cp(("arbitrary", "arbitrary"), 56))(order, ht, dproj, wg, *smalls)
    return outs[0], outs[2], outs[3:]


def matmul_tn(a, b, name, tk):
    s, m = a.shape
    n = b.shape[1]
    nk = s // tk

    def body(a_ref, b_ref, o_ref, acc_ref):
        k = pl.program_id(0)
        p = lax.dot_general(a_ref[...], b_ref[...], TN, preferred_element_type=F32)

        @pl.when(k == 0)
        def _():
            acc_ref[...] = p

        @pl.when(k > 0)
        def _():
            acc_ref[...] += p

        @pl.when(k == nk - 1)
        def _():
            o_ref[...] = acc_ref[...].astype(BF16)

    return pl.pallas_call(
        body, name=name, grid=(nk,),
        in_specs=[pl.BlockSpec((tk, m), lambda k: (k, 0)), pl.BlockSpec((tk, n), lambda k: (k, 0))],
        out_specs=pl.BlockSpec((m, n), lambda k: (0, 0)),
        out_shape=jax.ShapeDtypeStruct((m, n), BF16),
        scratch_shapes=[pltpu.VMEM((m, n), F32)],
        compiler_params=_cp(("arbitrary",)))(a, b)


def _head_matrices():
    lane = lax.broadcasted_iota(jnp.int32, (CB, CB), 0)
    col = lax.broadcasted_iota(jnp.int32, (CB, CB), 1)
    same = (lane // HD == col // HD).astype(BF16)
    lane_c = lax.broadcasted_iota(jnp.int32, (CB, LANES), 0)
    col_c = lax.broadcasted_iota(jnp.int32, (CB, LANES), 1)
    total = (lane_c // HD == col_c).astype(BF16)
    pick = (lane_c == col_c * HD).astype(BF16)
    return same, total, pick


def _head_sum(x, m_ref):
    return jnp.dot(x.astype(BF16), m_ref[...], preferred_element_type=F32)


def _dot_hilo(x, m_ref):
    hi = x.astype(BF16)
    lo = (x - hi.astype(F32)).astype(BF16)
    return (jnp.dot(hi, m_ref[...], preferred_element_type=F32)
            + jnp.dot(lo, m_ref[...], preferred_element_type=F32))


def _to_residue_major(val, buf, out_ref, dil):
    rows = out_ref.shape[1]
    for k in range(val.shape[1] // LANES):
        lanes = slice(k * LANES, (k + 1) * LANES)
        buf[k] = val[:, lanes]
        for r in range(dil):
            out_ref[r, :, lanes] = buf.at[k][pl.ds(r, rows, stride=dil), :].astype(out_ref.dtype)


def _from_residue_major(ref, buf, dil):
    if dil == 1:
        return ref[0].astype(F32)
    rows = ref.shape[1]
    for k in range(CB // LANES):
        for r in range(dil):
            buf.at[k][pl.ds(r, rows, stride=dil), :] = ref[r, :, k * LANES:(k + 1) * LANES].astype(F32)
    return jnp.concatenate([buf[k] for k in range(CB // LANES)], axis=1)


def qkv_prep(proj, qw8, kw8, same, tm):
    s = proj.shape[0]
    items = []
    for g, d in enumerate(DILATIONS):
        items += [(g, "q", CB_Q + g, d), (g, "k", CB_K + g, d)] + ([(g, "v", CB_V + g, d)] if d > 1 else [])
    n = len(items)

    def body(*refs):
        ins, (qw_ref, kw_ref, same_ref), outs, buf = refs[:n], refs[n:n + 3], refs[n + 3:2 * n + 3], refs[-1]
        for idx, (_, kind, _, dil) in enumerate(items):
            val = ins[idx][...].astype(F32)
            if kind != "v":
                r = lax.rsqrt(_head_sum(val * val, same_ref) * (1.0 / HD) + EPS)
                val = val * r * (qw_ref if kind == "q" else kw_ref)[...]
            if dil == 1:
                outs[idx][0] = val.astype(BF16)
            else:
                _to_residue_major(val, buf, outs[idx], dil)

    full = lambda a: pl.BlockSpec(a.shape, lambda i: (0, 0))
    outs = pl.pallas_call(
        body, name="qkv_prep", grid=(s // tm,),
        in_specs=[pl.BlockSpec((tm, CB), lambda i, cb=cb: (i, cb)) for _, _, cb, _ in items]
                 + [full(qw8), full(kw8), full(same)],
        out_specs=[pl.BlockSpec((d, tm // d, CB), lambda i: (0, i, 0)) for _, _, _, d in items],
        out_shape=[jax.ShapeDtypeStruct((d, s // d, CB), BF16) for _, _, _, d in items],
        scratch_shapes=[pltpu.VMEM((CB // LANES, tm, LANES), F32)],
        compiler_params=_cp(("parallel",)))(*([proj] * n), qw8 * (HD ** -0.5), kw8, same)
    srcs = [[None, None, (proj, CB_V + g)] for g in range(len(DILATIONS))]
    for (g, kind, _, _), o in zip(items, outs):
        srcs[g]["qkv".index(kind)] = (o.reshape(s, CB), 0)
    return srcs


def stats_prep(da, lc, dc, g, dil, tm):
    s = da.shape[0]
    rows = tm // dil

    def body(da_ref, lc_ref, dc_ref, dap_ref, lcp_ref, dcp_ref, lt_ref, dt_ref, buf):
        if dil == 1:
            dap_ref[0] = da_ref[...]
        else:
            _to_residue_major(da_ref[...].astype(F32), buf, dap_ref, dil)
        for src, dst, dst_t in ((lc_ref, lcp_ref, lt_ref), (dc_ref, dcp_ref, dt_ref)):
            buf[0] = src[...]
            for r in range(dil):
                piece = buf.at[0][pl.ds(r, rows, stride=dil), :] if dil > 1 else buf[0]
                dst[r] = piece
                dst_t[r] = piece.T[0:NH, :]

    row = lambda w: pl.BlockSpec((tm, w), lambda i: (i, 0))
    rm = lambda w: pl.BlockSpec((dil, rows, w), lambda i: (0, i, 0))
    tr = pl.BlockSpec((dil, NH, rows), lambda i: (0, 0, i))
    length = s // dil
    dap, lcp, dcp, lt, dt = pl.pallas_call(
        body, name=f"stats_prep_g{g}", grid=(s // tm,),
        in_specs=[row(CB), row(LANES), row(LANES)],
        out_specs=[rm(CB), rm(LANES), rm(LANES), tr, tr],
        out_shape=[jax.ShapeDtypeStruct((dil, length, CB), BF16)]
                  + [jax.ShapeDtypeStruct((dil, length, LANES), F32)] * 2
                  + [jax.ShapeDtypeStruct((dil, NH, length), F32)] * 2,
        scratch_shapes=[pltpu.VMEM((CB // LANES, tm, LANES), F32)],
        compiler_params=_cp(("parallel",)))(da, lc, dc)
    return (dap.reshape(s, CB), lcp.reshape(s, LANES), dcp.reshape(s, LANES),
            lt.reshape(dil * NH, length), dt.reshape(dil * NH, length))


def qkv_grads_to_dproj(dproj, proj, grads, qw8, kw8, same, tm):
    s = dproj.shape[0]
    ni = s // tm
    flat = [(t.reshape(d, s // d, CB), d, kind, 3 * kind + g)
            for g, d in enumerate(DILATIONS) for kind, t in enumerate(grads[g])]
    nf = len(flat)
    nraw = 2 * len(DILATIONS)

    def body(*refs):
        dp_hbm, raws, ins = refs[nraw + nf + 4], refs[1:1 + nraw], refs[1 + nraw:1 + nraw + nf]
        qw_ref, kw_ref, same_ref = refs[1 + nraw + nf:4 + nraw + nf]
        gw_ref, stage, buf, sems = refs[5 + nraw + nf:]
        i = pl.program_id(0)
        slot = i % 2

        def slab(step, sl):
            return pltpu.make_async_copy(
                stage.at[sl], dp_hbm.at[pl.ds(pl.multiple_of(step * tm, tm), tm), pl.ds(CB_Q * CB, 9 * CB)],
                sems.at[sl])

        @pl.when(i == 0)
        def _():
            gw_ref[...] = jnp.zeros_like(gw_ref)

        @pl.when(i >= 2)
        def _():
            slab(i - 2, slot).wait()

        for ref, (_, d, kind, jj) in zip(ins, flat):
            cols = slice(jj * CB, (jj + 1) * CB)
            dn = _from_residue_major(ref, buf, d)
            if kind == 2:
                stage[slot, :, cols] = dn.astype(BF16)
                continue
            t = raws[jj][...].astype(F32)
            r = lax.rsqrt(_head_sum(t * t, same_ref) * (1.0 / HD) + EPS)
            xh = t * r
            gw_ref[kind:kind + 1, :] += jnp.sum(dn * xh, axis=0, keepdims=True)
            dxh = dn * (qw_ref if kind == 0 else kw_ref)[...]
            mean = _head_sum(dxh * xh, same_ref) * (1.0 / HD)
            stage[slot, :, cols] = (r * (dxh - xh * mean)).astype(BF16)
        slab(i, slot).start()

        @pl.when(i == ni - 1)
        def _():
            slab(i - 1, 1 - slot).wait()
            slab(i, slot).wait()

    full = lambda a: pl.BlockSpec(a.shape, lambda i: (0, 0))
    any_spec = pl.BlockSpec(memory_space=pl.ANY)
    return pl.pallas_call(
        body, name="qkv_grads_to_dproj", grid=(ni,),
        in_specs=[any_spec] + [pl.BlockSpec((tm, CB), lambda i, jb=jb: (i, CB_Q + jb)) for jb in range(nraw)]
                 + [pl.BlockSpec((d, tm // d, CB), lambda i: (0, i, 0)) for _, d, _, _ in flat]
                 + [full(qw8), full(kw8), full(same)],
        out_specs=[any_spec, pl.BlockSpec((8, CB), lambda i: (0, 0))],
        out_shape=[jax.ShapeDtypeStruct((s, NIN), BF16), jax.ShapeDtypeStruct((8, CB), F32)],
        input_output_aliases={0: 0},
        scratch_shapes=[pltpu.VMEM((2, tm, 9 * CB), BF16), pltpu.VMEM((CB // LANES, tm, LANES), F32),
                        pltpu.SemaphoreType.DMA((2,))],
        compiler_params=_cp(("arbitrary",)))(
            dproj, *([proj] * nraw), *[t for t, _, _, _ in flat], qw8, kw8, same)


def _lane_lo():
    return lax.broadcasted_iota(jnp.int32, (1, 2 * HD), 1) < HD


def _stack_heads(t, lo):
    zero = jnp.zeros_like(t)
    return jnp.concatenate([jnp.where(lo, t, zero), jnp.where(lo, zero, t)], axis=0)


def _masks(other_ok):
    qi = lax.broadcasted_iota(jnp.int32, (QB, QB), 0)
    kj = lax.broadcasted_iota(jnp.int32, (QB, QB), 1)
    return (kj >= qi) & other_ok, kj <= qi


SUB = 4


def _attn_specs(nb, dil):
    steps = nb // SUB
    main = lambda cb, w=CB: pl.BlockSpec((SUB * QB, w), lambda r, s: (r * steps + s, cb))
    prev = lambda cb: pl.BlockSpec((QB, CB), lambda r, s: (jnp.maximum(r * nb + SUB * s - 1, 0), cb))
    nxt = lambda cb: pl.BlockSpec((QB, CB), lambda r, s: (jnp.minimum(r * nb + SUB * (s + 1), dil * nb - 1), cb))
    return main, prev, nxt


def attn_fwd(q_src, k_src, v_src, g, dil):
    s = q_src[0].shape[0]
    nb = s // dil // QB
    main, prev, _ = _attn_specs(nb, dil)

    def body(q_ref, kp_ref, k_ref, vp_ref, v_ref, o_ref, l_ref, kbuf, vbuf):
        step = pl.program_id(1)
        kbuf[0:QB], kbuf[QB:] = kp_ref[...], k_ref[...]
        vbuf[0:QB], vbuf[QB:] = vp_ref[...], v_ref[...]
        lo = _lane_lo()

        def block(j, carry):
            r0 = pl.multiple_of(j * QB, QB)
            rows, krows = pl.ds(r0, QB), pl.ds(r0, 2 * QB)
            m_prev, m_cur = _masks(step * SUB + j > 0)
            mask = jnp.concatenate([m_prev, m_cur], axis=1)
            mask = jnp.concatenate([mask, mask], axis=0)
            for i in range(NH // 2):
                sl = slice(2 * HD * i, 2 * HD * (i + 1))
                qs, ks, vv = q_ref[rows, sl], kbuf[krows, sl], vbuf[krows, sl]
                sc = lax.dot_general(_stack_heads(qs, lo), ks, NT, preferred_element_type=F32)
                sc = jnp.where(mask, sc, NEG)
                mx = jnp.max(sc, axis=-1, keepdims=True)
                p = jnp.exp(sc - mx)
                den = jnp.sum(p, axis=-1, keepdims=True)
                o = jnp.dot(p.astype(BF16), vv, preferred_element_type=F32) * (1.0 / den)
                lse = jnp.broadcast_to(mx + jnp.log(den), (2 * QB, 2 * HD))
                o_ref[rows, sl] = jnp.where(lo, o[:QB], o[QB:])
                l_ref[rows, sl] = jnp.where(lo, lse[:QB], lse[QB:])
            return carry

        lax.fori_loop(0, SUB, block, 0, unroll=True)

    out = jax.ShapeDtypeStruct((s, CB), F32)
    return pl.pallas_call(
        body, name=f"attn_fwd_g{g}", grid=(dil, nb // SUB),
        in_specs=[main(q_src[1]), prev(k_src[1]), main(k_src[1]), prev(v_src[1]), main(v_src[1])],
        out_specs=[main(0)] * 2, out_shape=[out, out],
        scratch_shapes=[pltpu.VMEM(((SUB + 1) * QB, CB), BF16)] * 2,
        compiler_params=_cp(("parallel", "parallel")))(q_src[0], k_src[0], k_src[0], v_src[0], v_src[0])


def attn_bwd_q(q_src, k_src, v_src, da, lc, dc, g, dil):
    s = q_src[0].shape[0]
    nb = s // dil // QB
    main, prev, _ = _attn_specs(nb, dil)

    def body(q_ref, kp_ref, k_ref, vp_ref, v_ref, da_ref, l_ref, d_ref, dq_ref, kbuf, vbuf):
        step = pl.program_id(1)
        kbuf[0:QB], kbuf[QB:] = kp_ref[...], k_ref[...]
        vbuf[0:QB], vbuf[QB:] = vp_ref[...], v_ref[...]
        lo = _lane_lo()

        def block(j, carry):
            r0 = pl.multiple_of(j * QB, QB)
            rows, krows = pl.ds(r0, QB), pl.ds(r0, 2 * QB)
            m_prev, m_cur = _masks(step * SUB + j > 0)
            mask = jnp.concatenate([m_prev, m_cur], axis=1)
            mask = jnp.concatenate([mask, mask], axis=0)
            lcols, dcols = l_ref[rows, :], d_ref[rows, :]
            for i in range(NH // 2):
                sl = slice(2 * HD * i, 2 * HD * (i + 1))
                qs, ks, vv, da2 = q_ref[rows, sl], kbuf[krows, sl], vbuf[krows, sl], da_ref[rows, sl]
                pair = lambda t: jnp.concatenate([t[:, 2 * i:2 * i + 1], t[:, 2 * i + 1:2 * i + 2]], axis=0)
                sc = lax.dot_general(_stack_heads(qs, lo), ks, NT, preferred_element_type=F32)
                sc = jnp.where(mask, sc, NEG)
                p = jnp.exp(sc - pair(lcols))
                dp = lax.dot_general(_stack_heads(da2, lo), vv, NT, preferred_element_type=F32)
                ds = p * (dp - pair(dcols))
                dq = jnp.dot(ds.astype(BF16), ks, preferred_element_type=F32)
                dq_ref[rows, sl] = (jnp.where(lo, dq[:QB], dq[QB:]) * (HD ** -0.5)).astype(BF16)
            return carry

        lax.fori_loop(0, SUB, block, 0, unroll=True)

    return pl.pallas_call(
        body, name=f"attn_bwd_q_g{g}", grid=(dil, nb // SUB),
        in_specs=[main(q_src[1]), prev(k_src[1]), main(k_src[1]), prev(v_src[1]), main(v_src[1]),
                  main(0), main(0, LANES), main(0, LANES)],
        out_specs=main(0), out_shape=jax.ShapeDtypeStruct((s, CB), BF16),
        scratch_shapes=[pltpu.VMEM(((SUB + 1) * QB, CB), BF16)] * 2,
        compiler_params=_cp(("parallel", "parallel")))(
            q_src[0], k_src[0], k_src[0], v_src[0], v_src[0], da, lc, dc)


def attn_bwd_kv(q_src, k_src, v_src, da, lt, dt, g, dil):
    s = q_src[0].shape[0]
    nb = s // dil // QB
    main, _, nxt = _attn_specs(nb, dil)

    def body(k_ref, v_ref, q_ref, qn_ref, da_ref, dan_ref, l_ref, ln_ref, d_ref, dn_ref, dk_ref, dv_ref,
             qbuf, dabuf, lbuf, dbuf):
        step = pl.program_id(1)
        qbuf[0:SUB * QB], qbuf[SUB * QB:] = q_ref[...], qn_ref[...]
        dabuf[0:SUB * QB], dabuf[SUB * QB:] = da_ref[...], dan_ref[...]
        for c in range(SUB):
            lbuf[c], dbuf[c] = l_ref[:, c * QB:(c + 1) * QB], d_ref[:, c * QB:(c + 1) * QB]
        lbuf[SUB], dbuf[SUB] = ln_ref[...], dn_ref[...]
        lo = _lane_lo()
        kj = lax.broadcasted_iota(jnp.int32, (QB, QB), 0)
        qi = lax.broadcasted_iota(jnp.int32, (QB, QB), 1)

        def block(j, carry):
            r0 = pl.multiple_of(j * QB, QB)
            rows, qrows = pl.ds(r0, QB), pl.ds(r0, 2 * QB)
            mask = jnp.concatenate([kj <= qi, (kj >= qi) & (step * SUB + j < nb - 1)], axis=1)
            mask = jnp.concatenate([mask, mask], axis=1)
            lrow = jnp.concatenate([lbuf[j], lbuf[j + 1]], axis=1)
            drow = jnp.concatenate([dbuf[j], dbuf[j + 1]], axis=1)
            for i in range(NH // 2):
                sl = slice(2 * HD * i, 2 * HD * (i + 1))
                q2, da2 = _stack_heads(qbuf[qrows, sl], lo), _stack_heads(dabuf[qrows, sl], lo)
                ks, vv = k_ref[rows, sl], v_ref[rows, sl]
                pair = lambda t: jnp.concatenate([t[2 * i:2 * i + 1, :], t[2 * i + 1:2 * i + 2, :]], axis=1)
                sc = lax.dot_general(ks, q2, NT, preferred_element_type=F32)
                sc = jnp.where(mask, sc, NEG)
                p = jnp.exp(sc - pair(lrow))
                dp = lax.dot_general(vv, da2, NT, preferred_element_type=F32)
                ds = p * (dp - pair(drow))
                dv_ref[rows, sl] = jnp.dot(p.astype(BF16), da2, preferred_element_type=F32).astype(BF16)
                dk_ref[rows, sl] = jnp.dot(ds.astype(BF16), q2, preferred_element_type=F32).astype(BF16)
            return carry

        lax.fori_loop(0, SUB, block, 0, unroll=True)

    steps = nb // SUB
    t_main = pl.BlockSpec((NH, SUB * QB), lambda r, s: (r, s))
    t_nxt = pl.BlockSpec((NH, QB), lambda r, s: (r, jnp.minimum(SUB * (s + 1), nb - 1)))
    out = jax.ShapeDtypeStruct((s, CB), BF16)
    return pl.pallas_call(
        body, name=f"attn_bwd_kv_g{g}", grid=(dil, steps),
        in_specs=[main(k_src[1]), main(v_src[1]), main(q_src[1]), nxt(q_src[1]),
                  main(0), nxt(0), t_main, t_nxt, t_main, t_nxt],
        out_specs=[main(0), main(0)], out_shape=[out, out],
        scratch_shapes=[pltpu.VMEM(((SUB + 1) * QB, CB), BF16)] * 2 + [pltpu.VMEM((SUB + 1, NH, QB), F32)] * 2,
        compiler_params=_cp(("parallel", "parallel")))(
            k_src[0], v_src[0], q_src[0], q_src[0], da, da, lt, lt, dt, dt)


def _conv_taps(u, u_prev, first):
    tm = u.shape[0]
    row = lax.broadcasted_iota(jnp.int32, (tm, 1), 0)
    up = jnp.where(first, 0.0, u_prev)
    u1 = jnp.where(row == 0, up[HALO - 1:HALO, :], pltpu.roll(u, 1, 0))
    u2 = jnp.where(row == 0, up[HALO - 2:HALO - 1, :],
                   jnp.where(row == 1, up[HALO - 1:HALO, :], pltpu.roll(u, 2, 0)))
    return u1, u2


def mid_fwd(proj, o_g, lse_g, conv_w, pick, tm):
    s = proj.shape[0]
    hb = tm // HALO

    def body(ba_ref, ca_ref, xa_ref, za_ref, cah_ref, xah_ref, zb_ref,
             o0, o1, o2, l0, l1, l2, w_ref, pick_ref, ya_ref, yb_ref, at_ref, lc_ref, buf_o, buf_l):
        first = pl.program_id(0) == 0
        u = ca_ref[...].astype(F32) * xa_ref[...].astype(F32)
        u1, u2 = _conv_taps(u, cah_ref[...].astype(F32) * xah_ref[...].astype(F32), first)
        conv = w_ref[0:1, :] * u2 + w_ref[1:2, :] * u1 + w_ref[2:3, :] * u
        ya_ref[...] = (ba_ref[...].astype(F32) * conv * _silu(za_ref[...].astype(F32))).astype(BF16)
        ls = [_from_residue_major(l, buf_l.at[g], d) for g, (l, d) in enumerate(zip((l0, l1, l2), DILATIONS))]
        mx = jnp.maximum(jnp.maximum(ls[0], ls[1]), ls[2])
        es = [jnp.exp(l - mx) for l in ls]
        den = es[0] + es[1] + es[2]
        num = jnp.zeros_like(den)
        for e, o, d in zip(es, (o0, o1, o2), DILATIONS):
            num = num + e * _from_residue_major(o, buf_o, d)
        attn = num / den
        at_ref[...] = attn
        lc_ref[...] = _dot_hilo(mx + jnp.log(den), pick_ref)
        yb_ref[...] = (attn * _silu(zb_ref[...].astype(F32))).astype(BF16)

    col = lambda j: pl.BlockSpec((tm, D), lambda i: (i, j))
    halo = lambda j: pl.BlockSpec((HALO, D), lambda i: (jnp.maximum(i * hb - 1, 0), j))
    loc = lambda w: pl.BlockSpec((tm, w), lambda i: (i, 0))
    rm = [pl.BlockSpec((d, tm // d, CB), lambda i: (0, i, 0)) for d in DILATIONS]
    rm_view = lambda ts: [t.reshape(d, s // d, CB) for t, d in zip(ts, DILATIONS)]
    return pl.pallas_call(
        body, name="mid_fwd", grid=(s // tm,),
        in_specs=[col(0), col(1), col(2), col(3), halo(1), halo(2),
                  pl.BlockSpec((tm, CB), lambda i: (i, CB_ZB))] + rm + rm
                 + [pl.BlockSpec((3, D), lambda i: (0, 0)), pl.BlockSpec(pick.shape, lambda i: (0, 0))],
        out_specs=[loc(D), loc(CB), loc(CB), loc(LANES)],
        out_shape=[jax.ShapeDtypeStruct((s, D), BF16), jax.ShapeDtypeStruct((s, CB), BF16),
                   jax.ShapeDtypeStruct((s, CB), F32), jax.ShapeDtypeStruct((s, LANES), F32)],
        scratch_shapes=[pltpu.VMEM((CB // LANES, tm, LANES), F32), pltpu.VMEM((3, CB // LANES, tm, LANES), F32)],
        compiler_params=_cp(("parallel",)))(
            proj, proj, proj, proj, proj, proj, proj, *rm_view(o_g), *rm_view(lse_g), conv_w, pick)


def mid_bwd(dproj, proj, dya, conv_w, tm):
    s = proj.shape[0]
    hb = tm // HALO
    nblk = s // tm
    last_h = s // HALO - 1

    def body(_, ba_ref, ca_ref, xa_ref, za_ref, cah_ref, xah_ref, ban_ref, zan_ref, dy_ref, dyn_ref, w_ref,
             o_ref, gw_ref):
        i = pl.program_id(0)
        ba, ca, xa, za = (t[...].astype(F32) for t in (ba_ref, ca_ref, xa_ref, za_ref))
        u = ca * xa
        u1, u2 = _conv_taps(u, cah_ref[...].astype(F32) * xah_ref[...].astype(F32), i == 0)
        w0, w1, w2 = w_ref[0:1, :], w_ref[1:2, :], w_ref[2:3, :]
        conv = w0 * u2 + w1 * u1 + w2 * u
        sg = jax.nn.sigmoid(za)
        sz = za * sg
        dy = dy_ref[...].astype(F32)
        dconv = dy * ba * sz
        dcn = dyn_ref[...].astype(F32) * ban_ref[...].astype(F32) * _silu(zan_ref[...].astype(F32))
        dcn = jnp.where(i == nblk - 1, 0.0, dcn)
        row = lax.broadcasted_iota(jnp.int32, (tm, 1), 0)
        d1 = jnp.where(row == tm - 1, dcn[0:1, :], pltpu.roll(dconv, tm - 1, 0))
        d2 = jnp.where(row == tm - 2, dcn[0:1, :],
                       jnp.where(row == tm - 1, dcn[1:2, :], pltpu.roll(dconv, tm - 2, 0)))
        du = w2 * dconv + w1 * d1 + w0 * d2
        o_ref[:, 0:D] = (dy * conv * sz).astype(BF16)
        o_ref[:, D:2 * D] = (du * xa).astype(BF16)
        o_ref[:, 2 * D:3 * D] = (du * ca).astype(BF16)
        o_ref[:, 3 * D:4 * D] = (dy * ba * conv * (sg * (1.0 + za * (1.0 - sg)))).astype(BF16)

        @pl.when(i == 0)
        def _():
            gw_ref[...] = jnp.zeros_like(gw_ref)

        gw_ref[0:1, :] += jnp.sum(dconv * u2, axis=0, keepdims=True)
        gw_ref[1:2, :] += jnp.sum(dconv * u1, axis=0, keepdims=True)
        gw_ref[2:3, :] += jnp.sum(dconv * u, axis=0, keepdims=True)

    col = lambda j: pl.BlockSpec((tm, D), lambda i: (i, j))
    halo_prev = lambda j: pl.BlockSpec((HALO, D), lambda i: (jnp.maximum(i * hb - 1, 0), j))
    halo_next = lambda j: pl.BlockSpec((HALO, D), lambda i: (jnp.minimum((i + 1) * hb, last_h), j))
    return pl.pallas_call(
        body, name="mid_bwd", grid=(nblk,),
        in_specs=[pl.BlockSpec(memory_space=pl.ANY), col(0), col(1), col(2), col(3),
                  halo_prev(1), halo_prev(2), halo_next(0), halo_next(3),
                  pl.BlockSpec((tm, D), lambda i: (i, 0)), halo_next(0),
                  pl.BlockSpec((3, D), lambda i: (0, 0))],
        out_specs=[pl.BlockSpec((tm, 4 * D), lambda i: (i, 0)), pl.BlockSpec((8, D), lambda i: (0, 0))],
        out_shape=[jax.ShapeDtypeStruct((s, NIN), BF16), jax.ShapeDtypeStruct((8, D), F32)],
        input_output_aliases={0: 0},
        compiler_params=_cp(("arbitrary",)))(dproj, proj, proj, proj, proj, proj, proj, proj, proj, dya, dya, conv_w)


def tail(proj, ya, yb, attn, x, target, gate, pa_w, pb_w, wo_w, total, tm):
    s = proj.shape[0]
    ni = s // tm
    ncol = NIN - CB_ZB * CB

    def body(ya_ref, yb_ref, ga_ref, gb_ref, zb_ref, at_ref, x_ref, t_ref, gate_ref, pa_ref, pb_ref, wo_ref,
             tot_ref, dp_hbm, dy_ref, dya_ref, da_ref, dc_ref, mg_ref, do_ref, dpa_ref, dpb_ref, st_ref,
             stage, sems):
        i = pl.program_id(0)
        slot = i % 2

        def slab(step, sl):
            return pltpu.make_async_copy(
                stage.at[sl], dp_hbm.at[pl.ds(pl.multiple_of(step * tm, tm), tm), pl.ds(CB_ZB * CB, ncol)],
                sems.at[sl])

        @pl.when(i == 0)
        def _():
            st_ref[...] = jnp.zeros_like(st_ref)

        @pl.when(i >= 2)
        def _():
            slab(i - 2, slot).wait()

        gate_v = gate_ref[...]
        pa = jnp.dot(ya_ref[...], pa_ref[...], preferred_element_type=F32)
        pb = jnp.dot(yb_ref[...], pb_ref[...], preferred_element_type=F32)
        sa = jax.nn.sigmoid(ga_ref[...].astype(F32))
        sb = jax.nn.sigmoid(gb_ref[...].astype(F32))
        merged = (sa * pa + sb * pb).astype(BF16)
        mg_ref[...] = merged
        out = jnp.dot(merged, wo_ref[...], preferred_element_type=F32)
        err = x_ref[...] + gate_v * out - t_ref[...]
        dy = err * (1.0 / D)
        dy_ref[...] = dy
        st_ref[0:1, :] += jnp.sum(dy * out, axis=0, keepdims=True)
        st_ref[1:2, :] += jnp.sum(err * err, axis=0, keepdims=True)
        dout = (gate_v * dy).astype(BF16)
        do_ref[...] = dout
        dmg = lax.dot_general(dout, wo_ref[...], NT, preferred_element_type=F32)
        dpa = (dmg * sa).astype(BF16)
        dpb = (dmg * sb).astype(BF16)
        dpa_ref[...] = dpa
        dpb_ref[...] = dpb
        stage[slot, :, CB:CB + D] = (dmg * pa * sa * (1.0 - sa)).astype(BF16)
        stage[slot, :, CB + D:] = (dmg * pb * sb * (1.0 - sb)).astype(BF16)
        dya_ref[...] = lax.dot_general(dpa, pa_ref[...], NT, preferred_element_type=F32).astype(BF16)
        dyb = lax.dot_general(dpb, pb_ref[...], NT, preferred_element_type=F32)
        zb = zb_ref[...].astype(F32)
        sg = jax.nn.sigmoid(zb)
        attn_v = at_ref[...]
        dattn = dyb * (zb * sg)
        da_ref[...] = dattn.astype(BF16)
        stage[slot, :, 0:CB] = (dyb * attn_v * (sg * (1.0 + zb * (1.0 - sg)))).astype(BF16)
        dc_ref[...] = _dot_hilo(dattn * attn_v, tot_ref)

        slab(i, slot).start()

        @pl.when(i == ni - 1)
        def _():
            slab(i - 1, 1 - slot).wait()
            slab(i, slot).wait()

    row = lambda w: pl.BlockSpec((tm, w), lambda i: (i, 0))
    pcol = lambda w, jb: pl.BlockSpec((tm, w), lambda i: (i, jb))
    full = lambda a: pl.BlockSpec(a.shape, lambda i: (0, 0))
    return pl.pallas_call(
        body, name="tail", grid=(ni,),
        in_specs=[row(D), row(CB), pcol(D, 9), pcol(D, 10), pcol(CB, CB_ZB), row(CB), row(D), row(D),
                  pl.BlockSpec((1, D), lambda i: (0, 0)), full(pa_w), full(pb_w), full(wo_w), full(total)],
        out_specs=[pl.BlockSpec(memory_space=pl.ANY),
                   row(D), row(D), row(CB), row(LANES), row(D), row(D), row(D), row(D),
                   pl.BlockSpec((8, D), lambda i: (0, 0))],
        out_shape=[jax.ShapeDtypeStruct((s, NIN), BF16), jax.ShapeDtypeStruct((s, D), F32),
                   jax.ShapeDtypeStruct((s, D), BF16), jax.ShapeDtypeStruct((s, CB), BF16),
                   jax.ShapeDtypeStruct((s, LANES), F32)] + [jax.ShapeDtypeStruct((s, D), BF16)] * 4
                  + [jax.ShapeDtypeStruct((8, D), F32)],
        scratch_shapes=[pltpu.VMEM((2, tm, ncol), BF16), pltpu.SemaphoreType.DMA((2,))],
        compiler_params=_cp(("arbitrary",), 56))(
            ya, yb, proj, proj, proj, attn, x, target, gate, pa_w, pb_w, wo_w, total)


def _local_step(x, target, shift, scale, gate, norm_w, conv_w, qw, kw, w_shard, small_shards, me_xyc):
    qw8, kw8 = jnp.tile(qw, (1, NH)), jnp.tile(kw, (1, NH))
    same, total, pick = _head_matrices()
    h, ht = norm_fwd(x, norm_w, scale, shift, 512)
    proj, wg, (pa_g, pb_g, wo_g) = proj_fwd_gather(h, w_shard, small_shards, gather_order(me_xyc), 1024)
    pa_w, wo_w = pa_g.reshape(D, D), wo_g.reshape(D, D)
    pb_w = pb_g.transpose(1, 0, 2).reshape(CB, D)
    srcs = qkv_prep(proj, qw8, kw8, same, 512)
    o_g, lse_g = zip(*[attn_fwd(*srcs[g], g, d) for g, d in enumerate(DILATIONS)])
    ya, yb, attn, lc = mid_fwd(proj, o_g, lse_g, conv_w, pick, 512)
    dproj, dy, dya, da, dc, merged, dout, dpa, dpb, st_tail = tail(
        proj, ya, yb, attn, x, target, gate, pa_w, pb_w, wo_w, total, 256)
    g_wo = matmul_tn(merged, dout, "grad_w_out", 1024)
    g_pa = matmul_tn(ya, dpa, "grad_w_br_conv", 1024)
    g_pb = matmul_tn(yb, dpb, "grad_w_br_attn", 1024)
    dproj, st_conv = mid_bwd(dproj, proj, dya, conv_w, 512)
    grads = []
    for g, d in enumerate(DILATIONS):
        da_p, lc_p, dc_p, lt, dt = stats_prep(da, lc, dc, g, d, 2048)
        dq = attn_bwd_q(*srcs[g], da_p, lc_p, dc_p, g, d)
        dk, dv = attn_bwd_kv(*srcs[g], da_p, lt, dt, g, d)
        grads.append((dq, dk, dv))
    dproj, gw_qk = qkv_grads_to_dproj(dproj, proj, grads, qw8, kw8, same, 512)
    slabs = [g_pa.reshape(NDEV, 128, D), g_pb.reshape(CB, NDEV, 128).transpose(1, 0, 2), g_wo.reshape(NDEV, 128, D)]
    dh, r_win, (r_pa, r_pb, r_wo) = proj_bwd(ht, dproj, wg, slabs, scatter_order(me_xyc), 1024)
    grad_x, st_norm = norm_bwd(dh, x, dy, norm_w, scale, 512)
    dmod = jnp.concatenate([st_norm[0:1], st_norm[1:2], st_tail[0:1]], axis=1)
    loss_part = (0.5 / D) * jnp.sum(st_tail[1])
    gw_heads = gw_qk[0:2].reshape(2, NH, HD).sum(axis=1)
    small = dict(dmod=dmod, norm_w=st_norm[2:3], conv_w=st_conv[0:3],
                 q_norm_w=gw_heads[0:1], k_norm_w=gw_heads[1:2], loss=loss_part)
    return grad_x, small, (r_win, r_pa, r_pb, r_wo)


def kernel(x, c, w_ada, b_ada, norm_w, w_in, conv_w, q_norm_w, k_norm_w, w_br_conv, w_br_attn, w_out, loss_target, m_w_ada, m_b_ada, m_norm_w, m_w_in, m_conv_w, m_q_norm_w, m_k_norm_w, m_w_br_conv, m_w_br_attn, m_w_out, v_w_ada, v_b_ada, v_norm_w, v_w_in, v_conv_w, v_q_norm_w, v_k_norm_w, v_w_br_conv, v_w_br_attn, v_w_out):
    me_xyc = (lax.axis_index("x"), lax.axis_index("y"), lax.axis_index("c"))
    me = _dev_index(me_xyc)
    ncol = w_ada.shape[2]

    conv_pad = jnp.zeros((8, 128), F32).at[0:3].set(conv_w[0])
    c_all, conv_all = all_gather([c, conv_pad], "gather_cond")
    conv_full = conv_all[:, 0:3].transpose(1, 0, 2).reshape(3, D)
    c_all = c_all.reshape(NDEV, D)

    b_cols = lax.dynamic_slice(b_ada, (0, me * ncol), (1, ncol))
    mod_cols = ada_fwd(c_all, w_ada[0], b_cols)
    (mod_all,) = all_gather([mod_cols], "gather_mod")
    mod = lax.dynamic_index_in_dim(mod_all, me, axis=1, keepdims=False).reshape(1, 3 * D)
    shift, scale, gate = mod[:, 0:D], mod[:, D:2 * D], mod[:, 2 * D:3 * D]

    grad_x, small, (r_win, r_pa, r_pb, r_wo) = _local_step(
        x[0], loss_target[0], shift, scale, gate, norm_w, conv_full, q_norm_w, k_norm_w,
        w_in[0].astype(BF16), [w_br_conv[0].astype(BF16), w_br_attn[0].astype(BF16), w_out[0].astype(BF16)], me_xyc)

    packed = jnp.concatenate(
        [small["dmod"], small["norm_w"], small["conv_w"].reshape(1, 3 * D), small["q_norm_w"], small["k_norm_w"],
         jnp.full((1, 128), small["loss"], F32)], axis=1)
    (packed_all,) = all_gather([packed], "gather_small")
    tot = sum_parts(packed_all)
    loss = tot[0, 7 * D + 2 * HD]
    dmod_all = packed_all[:, 0, 0:3 * D]
    g_b_ada = tot[:, 0:3 * D]
    g_norm_w = tot[:, 3 * D:4 * D]
    g_conv = lax.dynamic_slice(tot[:, 4 * D:7 * D].reshape(3, D), (0, me * 128), (3, 128))
    g_qn = tot[:, 7 * D:7 * D + HD]
    g_kn = tot[:, 7 * D + HD:7 * D + 2 * HD]
    g_w_ada = ada_bwd(c_all.T, lax.dynamic_slice(dmod_all, (0, me * ncol), (NDEV, ncol)))

    def upd(parts, w, m, v, name, rows):
        shape = w.shape
        w2, m2, v2 = (t.reshape(shape[-2:]) for t in (w, m, v))
        return [t.reshape(shape) for t in adamw(parts, w2, m2, v2, name, rows)]

    res = {
        "w_ada": upd(g_w_ada[None], w_ada, m_w_ada, v_w_ada, "adamw_w_ada", 256),
        "b_ada": upd(g_b_ada[None], b_ada, m_b_ada, v_b_ada, "adamw_b_ada", 1),
        "norm_w": upd(g_norm_w[None], norm_w, m_norm_w, v_norm_w, "adamw_norm_w", 1),
        "w_in": upd(r_win, w_in, m_w_in, v_w_in, "adamw_w_in", 128),
        "conv_w": upd(g_conv[None], conv_w, m_conv_w, v_conv_w, "adamw_conv_w", 3),
        "q_norm_w": upd(g_qn[None], q_norm_w, m_q_norm_w, v_q_norm_w, "adamw_q_norm_w", 1),
        "k_norm_w": upd(g_kn[None], k_norm_w, m_k_norm_w, v_k_norm_w, "adamw_k_norm_w", 1),
        "w_br_conv": upd(r_pa, w_br_conv, m_w_br_conv, v_w_br_conv, "adamw_w_br_conv", 128),
        "w_br_attn": upd(r_pb, w_br_attn, m_w_br_attn, v_w_br_attn, "adamw_w_br_attn", 512),
        "w_out": upd(r_wo, w_out, m_w_out, v_w_out, "adamw_w_out", 128),
    }
    names = ["w_ada", "b_ada", "norm_w", "w_in", "conv_w", "q_norm_w", "k_norm_w", "w_br_conv", "w_br_attn", "w_out"]
    return (loss, grad_x[None], *[res[n][0] for n in names], *[res[n][1] for n in names],
            *[res[n][2] for n in names], *[res[n][3] for n in names])
```

```python
import jax
import jax.numpy as jnp
from jax import lax
from jax.experimental import pallas as pl
from jax.experimental.pallas import tpu as pltpu

F32, BF16 = jnp.float32, jnp.bfloat16
D = 1024
NIN = 11264
NDEV = 8
SHARD = NIN // NDEV
HD = 64
NH = 8
QB = 128
CB = 512
CB_Q, CB_K, CB_V, CB_ZB = 8, 11, 14, 17
DILATIONS = (1, 4, 16)
EPS = 1e-6
NEG = -1e30
HALO = 16
LANES = 128
MESH = pl.DeviceIdType.MESH

ADAM_LR, ADAM_B1, ADAM_B2, ADAM_EPS, ADAM_WD, ADAM_STEP = 0.001, 0.9, 0.999, 1e-08, 0.01, 10

NT = (((1,), (1,)), ((), ()))
TN = (((0,), (0,)), ((), ()))


def _cp(sem, vmem_mb=48):
    return pltpu.CompilerParams(dimension_semantics=sem, vmem_limit_bytes=vmem_mb << 20)


def _silu(z):
    return z * jax.nn.sigmoid(z)


def _coords():
    return lax.axis_index("x"), lax.axis_index("y"), lax.axis_index("c")


def all_gather(arrs, name):
    n = len(arrs)

    def body(*refs):
        ins, outs = refs[:n], refs[n:2 * n]
        send_sems, recv_sems, local_sems = refs[2 * n:]
        x, y, c = _coords()
        me, sibling = (x, y, c), (x, y, 1 - c)
        chips = [(1 - x, y), (x, 1 - y), (1 - x, 1 - y)]

        def slot(a, dev):
            return outs[a].at[4 * dev[0] + 2 * dev[1] + dev[2]]

        def copy(a, k, block, to, src=None):
            return pltpu.make_async_remote_copy(
                src_ref=slot(a, block) if src is None else src, dst_ref=slot(a, block),
                send_sem=send_sems.at[a, k], recv_sem=recv_sems.at[a, k],
                device_id=to, device_id_type=MESH)

        mine = [pltpu.make_async_copy(ins[a], slot(a, me), local_sems.at[a]) for a in range(n)]
        for cp in mine:
            cp.start()
        first = []
        for a in range(n):
            first.append(copy(a, 0, me, sibling, src=ins[a]))
            first += [copy(a, 1 + j, me, (*chip, c), src=ins[a]) for j, chip in enumerate(chips)]
        for cp in first:
            cp.start()
        passed = []
        for j, chip in enumerate(chips):
            for a in range(n):
                copy(a, 1 + j, (*chip, c), me).wait_recv()
                fwd = copy(a, 4 + j, (*chip, c), sibling)
                fwd.start()
                passed.append(fwd)
        for a in range(n):
            copy(a, 0, sibling, me).wait_recv()
            for j, chip in enumerate(chips):
                copy(a, 4 + j, (*chip, 1 - c), me).wait_recv()
        for cp in first + passed:
            cp.wait_send()
        for cp in mine:
            cp.wait()

    any_spec = pl.BlockSpec(memory_space=pl.ANY)
    return pl.pallas_call(
        body, name=name,
        out_shape=[jax.ShapeDtypeStruct((NDEV,) + a.shape, a.dtype) for a in arrs],
        in_specs=[any_spec] * n, out_specs=[any_spec] * n,
        scratch_shapes=[pltpu.SemaphoreType.DMA((n, 7)), pltpu.SemaphoreType.DMA((n, 7)),
                        pltpu.SemaphoreType.DMA((n,))],
    )(*arrs)


FLIPS = [(fx, fy, fc) for fx in (0, 1) for fy in (0, 1) for fc in (0, 1)][1:]


def _flip(dev, f):
    return tuple(1 - v if b else v for v, b in zip(dev, f))


def _dev_index(dev):
    return 4 * dev[0] + 2 * dev[1] + dev[2]


def _chip_order(x, y, c):
    xor = lambda a, b: a + b - 2 * a * b
    return [(xor(x, 1 - c), xor(y, c)), (xor(x, c), xor(y, 1 - c)), (1 - x, 1 - y)]


def _chip_index(chip):
    return 2 * chip[0] + chip[1]


PAIR = 2 * SHARD
CHIP_FLIPS = [(0, 1), (1, 0), (1, 1), (0, 0)]


def gather_order(me_xyc):
    x, y, c = me_xyc
    return jnp.stack([_chip_index(ch) for ch in [(x, y)] + _chip_order(x, y, c)]).astype(jnp.int32)


def scatter_order(me_xyc):
    x, y, _ = me_xyc
    chips = [_flip((x, y), f) for f in CHIP_FLIPS]
    return jnp.stack([_chip_index(ch) for ch in chips]).astype(jnp.int32)


def ada_fwd(c_all, w_ada, b_cols):
    def body(c_ref, w_ref, b_ref, o_ref):
        a = _silu(c_ref[...]).astype(BF16)
        o_ref[...] = jnp.dot(a, w_ref[...].astype(BF16), preferred_element_type=F32) + b_ref[...]

    return pl.pallas_call(body, name="ada_fwd",
                          out_shape=jax.ShapeDtypeStruct((NDEV, w_ada.shape[1]), F32))(c_all, w_ada, b_cols)


def ada_bwd(c_all_t, dmod_cols):
    def body(c_ref, d_ref, o_ref):
        at = _silu(c_ref[...])
        acc = at[:, 0:1] * d_ref[0:1, :]
        for b in range(1, NDEV):
            acc = acc + at[:, b:b + 1] * d_ref[b:b + 1, :]
        o_ref[...] = acc

    return pl.pallas_call(body, name="ada_bwd",
                          out_shape=jax.ShapeDtypeStruct((D, dmod_cols.shape[1]), F32))(c_all_t, dmod_cols)


def sum_parts(parts):
    def body(p_ref, o_ref):
        acc = p_ref[0]
        for b in range(1, NDEV):
            acc = acc + p_ref[b]
        o_ref[...] = acc

    return pl.pallas_call(body, name="sum_parts",
                          out_shape=jax.ShapeDtypeStruct(parts.shape[1:], F32))(parts)


def adamw(parts, w, m, v, name, rows):
    n, r, ccols = parts.shape

    def body(p_ref, w_ref, m_ref, v_ref, g_ref, d_ref, nm_ref, nv_ref):
        g = p_ref[0].astype(F32)
        for b in range(1, n):
            g = g + p_ref[b].astype(F32)
        nm = ADAM_B1 * m_ref[...] + (1.0 - ADAM_B1) * g
        nv = ADAM_B2 * v_ref[...] + (1.0 - ADAM_B2) * (g * g)
        g_ref[...] = g
        nm_ref[...] = nm
        nv_ref[...] = nv
        m_hat = nm / (1.0 - ADAM_B1 ** ADAM_STEP)
        v_hat = nv / (1.0 - ADAM_B2 ** ADAM_STEP)
        d_ref[...] = -ADAM_LR * (m_hat / (jnp.sqrt(v_hat) + ADAM_EPS) + ADAM_WD * w_ref[...])

    blk = pl.BlockSpec((rows, ccols), lambda i: (i, 0))
    out = jax.ShapeDtypeStruct((r, ccols), F32)
    return pl.pallas_call(
        body, name=name, grid=(r // rows,),
        in_specs=[pl.BlockSpec((n, rows, ccols), lambda i: (0, i, 0)), blk, blk, blk],
        out_specs=[blk] * 4, out_shape=[out] * 4, compiler_params=_cp(("parallel",)))(parts, w, m, v)


def norm_fwd(x, nw, scale, shift, tm):
    s = x.shape[0]

    def body(x_ref, nw_ref, sc_ref, sh_ref, h_ref, ht_ref):
        xf = x_ref[...]
        r = lax.rsqrt(jnp.mean(xf * xf, axis=-1, keepdims=True) + EPS)
        h = (xf * r * nw_ref[...]) * (1.0 + sc_ref[...]) + sh_ref[...]
        h_ref[...] = h.astype(BF16)
        ht_ref[...] = h.T.astype(BF16)

    vec = pl.BlockSpec((1, D), lambda i: (0, 0))
    return pl.pallas_call(
        body, name="norm_fwd", grid=(s // tm,),
        in_specs=[pl.BlockSpec((tm, D), lambda i: (i, 0)), vec, vec, vec],
        out_specs=[pl.BlockSpec((tm, D), lambda i: (i, 0)), pl.BlockSpec((D, tm), lambda i: (0, i))],
        out_shape=[jax.ShapeDtypeStruct((s, D), BF16), jax.ShapeDtypeStruct((D, s), BF16)],
        compiler_params=_cp(("parallel",)))(x, nw, scale, shift)


def norm_bwd(dh, x, dy, nw, scale, tm):
    s = x.shape[0]

    def body(dh_ref, x_ref, dy_ref, nw_ref, sc_ref, gx_ref, st_ref):
        xf, g = x_ref[...], dh_ref[...]
        r = lax.rsqrt(jnp.mean(xf * xf, axis=-1, keepdims=True) + EPS)
        xh = xf * r
        dn = g * (1.0 + sc_ref[...])
        dxh = dn * nw_ref[...]
        gx_ref[...] = dy_ref[...] + r * (dxh - xh * jnp.mean(dxh * xh, axis=-1, keepdims=True))

        @pl.when(pl.program_id(0) == 0)
        def _():
            st_ref[...] = jnp.zeros_like(st_ref)

        st_ref[0:1, :] += jnp.sum(g, axis=0, keepdims=True)
        st_ref[1:2, :] += jnp.sum(g * xh * nw_ref[...], axis=0, keepdims=True)
        st_ref[2:3, :] += jnp.sum(dn * xh, axis=0, keepdims=True)

    vec = pl.BlockSpec((1, D), lambda i: (0, 0))
    row = pl.BlockSpec((tm, D), lambda i: (i, 0))
    return pl.pallas_call(
        body, name="norm_bwd", grid=(s // tm,),
        in_specs=[row, row, row, vec, vec],
        out_specs=[row, pl.BlockSpec((8, D), lambda i: (0, 0))],
        out_shape=[jax.ShapeDtypeStruct((s, D), F32), jax.ShapeDtypeStruct((8, D), F32)],
        compiler_params=_cp(("arbitrary",)))(dh, x, dy, nw, scale)


def proj_fwd_gather(h, w_shard, extras, order, tm):
    s = h.shape[0]
    ni = s // tm
    n = 1 + len(extras)
    late = ni - 2
    split = 6 * 256

    def body(order_ref, h_ref, *refs):
        ins, o_ref, outs = refs[:n], refs[n], refs[n + 1:2 * n + 1]
        wbuf, send_sems, recv_sems, local_sems, load_sems = refs[2 * n + 1:]
        jj, i = pl.program_id(0), pl.program_id(1)
        x, y, c = _coords()
        me, sibling = (x, y, c), (x, y, 1 - c)
        chips = _chip_order(x, y, c)
        relayed = [(*chips[1], 1 - c), (*chips[0], 1 - c), (*chips[2], 1 - c)]

        def slot(a, dev):
            if a == 0:
                return outs[0].at[_chip_index(dev), :, pl.ds(pl.multiple_of(dev[2] * SHARD, LANES), SHARD)]
            return outs[a].at[_dev_index(dev)]

        def copy(a, k, block, to, src=None):
            return pltpu.make_async_remote_copy(
                src_ref=slot(a, block) if src is None else src, dst_ref=slot(a, block),
                send_sem=send_sems.at[a, k], recv_sem=recv_sems.at[a, k], device_id=to, device_id_type=MESH)

        mine = [pltpu.make_async_copy(ins[a], slot(a, me), local_sems.at[a]) for a in range(n)]
        to_sibling = [copy(a, 0, me, sibling, src=ins[a]) for a in range(n)]
        to_chip = [[copy(a, 1 + j, me, (*chips[j], c), src=ins[a]) for a in range(n)] for j in range(2)]
        onward = [copy(a, 3, (*chips[1], c), (*chips[0], c)) for a in range(n)]
        passed = [[copy(a, 4 + j, (*ch, c), sibling) for a in range(n)] for j, ch in enumerate(chips)]
        sends = lambda a: [to_sibling[a], to_chip[0][a], to_chip[1][a], onward[a]] + [passed[j][a] for j in range(3)]

        def arrived(a, j):
            copy(a, 1 + j, (*chips[j], c), me).wait_recv()

        def load(row):
            return pltpu.make_async_copy(outs[0].at[order_ref[row]], wbuf.at[row % 2], load_sems.at[row % 2])

        @pl.when((jj == 0) & (i == 0))
        def _():
            for cp in mine:
                cp.start()
            to_sibling[0].start()
            to_chip[0][0].start()
            to_chip[1][0].start()
            mine[0].wait()
            copy(0, 0, sibling, me).wait_recv()
            load(0).start()

        @pl.when((jj == 1) & (i == 0))
        def _():
            for a in range(1, n):
                to_sibling[a].start()
                to_chip[0][a].start()
                to_chip[1][a].start()

        @pl.when((jj == 0) & (i == late))
        def _():
            arrived(0, 0)
            passed[0][0].start()
            arrived(0, 1)
            passed[1][0].start()
            onward[0].start()
            copy(0, 5, relayed[1], me).wait_recv()

        @pl.when((jj == 1) & (i == late))
        def _():
            copy(0, 4, relayed[0], me).wait_recv()

        @pl.when((jj == 2) & (i == late))
        def _():
            arrived(0, 2)
            passed[2][0].start()
            copy(0, 6, relayed[2], me).wait_recv()

        @pl.when((jj == 3) & (i == 0))
        def _():
            for a in range(1, n):
                arrived(a, 1)
                onward[a].start()
                passed[1][a].start()
                arrived(a, 0)
                passed[0][a].start()

        @pl.when((jj < 3) & (i == late))
        def _():
            load(jj + 1).start()

        @pl.when(i == 0)
        def _():
            load(jj).wait()

        w = wbuf.at[jj % 2]
        o_ref[:, :split] = jnp.dot(h_ref[...], w[:, :split], preferred_element_type=F32).astype(BF16)
        o_ref[:, split:] = jnp.dot(h_ref[...], w[:, split:], preferred_element_type=F32).astype(BF16)

        @pl.when((jj == 3) & (i == ni - 1))
        def _():
            for a in range(1, n):
                arrived(a, 2)
                passed[2][a].start()
            for a in range(1, n):
                copy(a, 0, sibling, me).wait_recv()
                for j in range(3):
                    copy(a, 4 + j, relayed[j], me).wait_recv()
            for a in range(n):
                if a > 0:
                    mine[a].wait()
                for cp in sends(a):
                    cp.wait_send()

    any_spec = pl.BlockSpec(memory_space=pl.ANY)
    outs = pl.pallas_call(
        body, name="proj_fwd_gather",
        grid_spec=pltpu.PrefetchScalarGridSpec(
            num_scalar_prefetch=1, grid=(NDEV // 2, ni),
            in_specs=[pl.BlockSpec((tm, D), lambda jj, i, o: (i, 0))] + [any_spec] * n,
            out_specs=[pl.BlockSpec((tm, PAIR), lambda jj, i, o: (i, o[jj]))] + [any_spec] * n,
            scratch_shapes=[pltpu.VMEM((2, D, PAIR), BF16), pltpu.SemaphoreType.DMA((n, 7)),
                            pltpu.SemaphoreType.DMA((n, 7)), pltpu.SemaphoreType.DMA((n,)),
                            pltpu.SemaphoreType.DMA((2,))]),
        out_shape=[jax.ShapeDtypeStruct((s, NIN), BF16), jax.ShapeDtypeStruct((NDEV // 2, D, PAIR), BF16)]
                  + [jax.ShapeDtypeStruct((NDEV,) + e.shape, e.dtype) for e in extras],
        compiler_params=_cp(("arbitrary", "arbitrary"), 56))(order, h, w_shard, *extras)
    return outs[0], outs[1], outs[2:]


def proj_bwd(ht, dproj, wg, smalls, order, tt):
    s = dproj.shape[0]
    nk = s // tt
    nchip = NDEV // 2
    assert nk % nchip == 0
    split = 6 * 256
    n = len(smalls)

    def body(order_ref, ht_ref, dp_ref, w_ref, *rest):
        small_in = rest[:n]
        dh_ref, gw_ref, rwin_ref = rest[n:n + 3]
        small_out = rest[n + 3:2 * n + 3]
        acc, stage, send_sems, recv_sems, local_sems, stage_sems = rest[2 * n + 3:]
        t, k = pl.program_id(0), pl.program_id(1)
        me_xyc = _coords()
        me = _dev_index(me_xyc)
        peers = [_flip(me_xyc, f) for f in FLIPS]
        row_chips = [_flip(me_xyc[:2], f) for f in CHIP_FLIPS]

        def exchange(a, kf, src_arr, dst_arr):
            pid = _dev_index(peers[kf])
            mk = lambda dst: pltpu.make_async_remote_copy(
                src_ref=src_arr.at[pid], dst_ref=dst, send_sem=send_sems.at[a, kf], recv_sem=recv_sems.at[a, kf],
                device_id=peers[kf], device_id_type=MESH)
            return mk(dst_arr.at[me]), mk(dst_arr.at[pid])

        small_pairs = [exchange(1 + a, kf, small_in[a], small_out[a]) for kf in range(7) for a in range(n)]
        small_own = [pltpu.make_async_copy(small_in[a].at[me], small_out[a].at[me], local_sems.at[1 + a])
                     for a in range(n)]
        win_pairs = [exchange(0, kf, gw_ref, rwin_ref) for kf in range(7)]
        win_own = pltpu.make_async_copy(gw_ref.at[me], rwin_ref.at[me], local_sems.at[0])

        def to_hbm(row, core):
            slab = _dev_index((*row_chips[row], core))
            return pltpu.make_async_copy(stage.at[:, core * SHARD:(core + 1) * SHARD], gw_ref.at[slab],
                                         stage_sems.at[core])

        def flip_index(row, fc):
            return FLIPS.index((*CHIP_FLIPS[row], fc))

        @pl.when((t == 0) & (k == 0))
        def _():
            for cp in small_own:
                cp.start()
            for send, _ in small_pairs:
                send.start()

        @pl.when(t < nchip)
        def _():
            for cols in (slice(0, split), slice(split, PAIR)):
                p = jnp.dot(ht_ref[...], dp_ref[:, cols], preferred_element_type=F32)

                @pl.when(k == 0)
                def _():
                    acc[:, cols] = p

                @pl.when(k > 0)
                def _():
                    acc[:, cols] += p

        for row in range(nchip):
            @pl.when((t == row) & (k == nk - 1))
            def _(row=row):
                stage[...] = acc[...].astype(BF16)
                to_hbm(row, 0).start()
                to_hbm(row, 1).start()

            @pl.when((t == row + 1) & (k == 1))
            def _(row=row):
                to_hbm(row, 0).wait()
                to_hbm(row, 1).wait()
                if row < nchip - 1:
                    win_pairs[flip_index(row, 0)][0].start()
                    win_pairs[flip_index(row, 1)][0].start()
                else:
                    win_pairs[flip_index(row, 1)][0].start()
                    win_own.start()

        @pl.when(t >= nchip)
        def _():
            kk = k % nchip
            p = lax.dot_general(dp_ref[...], w_ref[...], NT, preferred_element_type=F32)

            @pl.when(kk == 0)
            def _():
                dh_ref[...] = p

            @pl.when(kk > 0)
            def _():
                dh_ref[...] += p

        @pl.when((t == 2 * nchip - 1) & (k == nk - 1))
        def _():
            for _, recv in win_pairs + small_pairs:
                recv.wait_recv()
            for send, _ in win_pairs + small_pairs:
                send.wait_send()
            win_own.wait()
            for cp in small_own:
                cp.wait()

    any_spec = pl.BlockSpec(memory_space=pl.ANY)
    first = lambda t: t < nchip
    block = lambda t, k: ((t - nchip) * nk + k) // nchip
    outs = pl.pallas_call(
        body, name="proj_bwd",
        grid_spec=pltpu.PrefetchScalarGridSpec(
            num_scalar_prefetch=1, grid=(2 * nchip, nk),
            in_specs=[pl.BlockSpec((D, tt), lambda t, k, o: (0, jnp.where(first(t), k, nk - 1))),
                      pl.BlockSpec((tt, PAIR), lambda t, k, o: (jnp.where(first(t), k, block(t, k)),
                                                                jnp.where(first(t), o[jnp.minimum(t, nchip - 1)],
                                                                          k % nchip))),
                      pl.BlockSpec((None, D, PAIR), lambda t, k, o: (jnp.where(first(t), 0, k % nchip), 0, 0))]
                     + [any_spec] * n,
            out_specs=[pl.BlockSpec((tt, D), lambda t, k, o: (jnp.where(first(t), 0, block(t, k)), 0))]
                      + [any_spec] * (2 + n),
            scratch_shapes=[pltpu.VMEM((D, PAIR), F32), pltpu.VMEM((D, PAIR), BF16),
                            pltpu.SemaphoreType.DMA((1 + n, 7)), pltpu.SemaphoreType.DMA((1 + n, 7)),
                            pltpu.SemaphoreType.DMA((1 + n,)), pltpu.SemaphoreType.DMA((2,))]),
        out_shape=[jax.ShapeDtypeStruct((s, D), F32), jax.ShapeDtypeStruct((NDEV, D, SHARD), BF16),
                   jax.ShapeDtypeStruct((NDEV, D, SHARD), BF16)]
                  + [jax.ShapeDtypeStruct(a.shape, a.dtype) for a in smalls],
        compiler_params=_cp(("arbitrary", "arbitrary"), 56))(order, ht, dproj, wg, *smalls)
    return outs[0], outs[2], outs[3:]


def matmul_tn(a, b, name, tk):
    s, m = a.shape
    n = b.shape[1]
    nk = s // tk

    def body(a_ref, b_ref, o_ref, acc_ref):
        k = pl.program_id(0)
        p = lax.dot_general(a_ref[...], b_ref[...], TN, preferred_element_type=F32)

        @pl.when(k == 0)
        def _():
            acc_ref[...] = p

        @pl.when(k > 0)
        def _():
            acc_ref[...] += p

        @pl.when(k == nk - 1)
        def _():
            o_ref[...] = acc_ref[...].astype(BF16)

    return pl.pallas_call(
        body, name=name, grid=(nk,),
        in_specs=[pl.BlockSpec((tk, m), lambda k: (k, 0)), pl.BlockSpec((tk, n), lambda k: (k, 0))],
        out_specs=pl.BlockSpec((m, n), lambda k: (0, 0)),
        out_shape=jax.ShapeDtypeStruct((m, n), BF16),
        scratch_shapes=[pltpu.VMEM((m, n), F32)],
        compiler_params=_cp(("arbitrary",)))(a, b)


def _head_matrices():
    lane = lax.broadcasted_iota(jnp.int32, (CB, CB), 0)
    col = lax.broadcasted_iota(jnp.int32, (CB, CB), 1)
    same = (lane // HD == col // HD).astype(BF16)
    lane_c = lax.broadcasted_iota(jnp.int32, (CB, LANES), 0)
    col_c = lax.broadcasted_iota(jnp.int32, (CB, LANES), 1)
    total = (lane_c // HD == col_c).astype(BF16)
    pick = (lane_c == col_c * HD).astype(BF16)
    return same, total, pick


def _head_sum(x, m_ref):
    return jnp.dot(x.astype(BF16), m_ref[...], preferred_element_type=F32)


def _dot_hilo(x, m_ref):
    hi = x.astype(BF16)
    lo = (x - hi.astype(F32)).astype(BF16)
    return (jnp.dot(hi, m_ref[...], preferred_element_type=F32)
            + jnp.dot(lo, m_ref[...], preferred_element_type=F32))


def _to_residue_major(val, buf, out_ref, dil):
    rows = out_ref.shape[1]
    for k in range(val.shape[1] // LANES):
        lanes = slice(k * LANES, (k + 1) * LANES)
        buf[k] = val[:, lanes]
        for r in range(dil):
            out_ref[r, :, lanes] = buf.at[k][pl.ds(r, rows, stride=dil), :].astype(out_ref.dtype)


def _from_residue_major(ref, buf, dil):
    if dil == 1:
        return ref[0].astype(F32)
    rows = ref.shape[1]
    for k in range(CB // LANES):
        for r in range(dil):
            buf.at[k][pl.ds(r, rows, stride=dil), :] = ref[r, :, k * LANES:(k + 1) * LANES].astype(F32)
    return jnp.concatenate([buf[k] for k in range(CB // LANES)], axis=1)


def qkv_prep(proj, qw8, kw8, same, tm):
    s = proj.shape[0]
    items = []
    for g, d in enumerate(DILATIONS):
        items += [(g, "q", CB_Q + g, d), (g, "k", CB_K + g, d)] + ([(g, "v", CB_V + g, d)] if d > 1 else [])
    n = len(items)

    def body(*refs):
        ins, (qw_ref, kw_ref, same_ref), outs, buf = refs[:n], refs[n:n + 3], refs[n + 3:2 * n + 3], refs[-1]
        for idx, (_, kind, _, dil) in enumerate(items):
            val = ins[idx][...].astype(F32)
            if kind != "v":
                r = lax.rsqrt(_head_sum(val * val, same_ref) * (1.0 / HD) + EPS)
                val = val * r * (qw_ref if kind == "q" else kw_ref)[...]
            if dil == 1:
                outs[idx][0] = val.astype(BF16)
            else:
                _to_residue_major(val, buf, outs[idx], dil)

    full = lambda a: pl.BlockSpec(a.shape, lambda i: (0, 0))
    outs = pl.pallas_call(
        body, name="qkv_prep", grid=(s // tm,),
        in_specs=[pl.BlockSpec((tm, CB), lambda i, cb=cb: (i, cb)) for _, _, cb, _ in items]
                 + [full(qw8), full(kw8), full(same)],
        out_specs=[pl.BlockSpec((d, tm // d, CB), lambda i: (0, i, 0)) for _, _, _, d in items],
        out_shape=[jax.ShapeDtypeStruct((d, s // d, CB), BF16) for _, _, _, d in items],
        scratch_shapes=[pltpu.VMEM((CB // LANES, tm, LANES), F32)],
        compiler_params=_cp(("parallel",)))(*([proj] * n), qw8 * (HD ** -0.5), kw8, same)
    srcs = [[None, None, (proj, CB_V + g)] for g in range(len(DILATIONS))]
    for (g, kind, _, _), o in zip(items, outs):
        srcs[g]["qkv".index(kind)] = (o.reshape(s, CB), 0)
    return srcs


def stats_prep(da, lc, dc, g, dil, tm):
    s = da.shape[0]
    rows = tm // dil

    def body(da_ref, lc_ref, dc_ref, dap_ref, lcp_ref, dcp_ref, lt_ref, dt_ref, buf):
        if dil == 1:
            dap_ref[0] = da_ref[...]
        else:
            _to_residue_major(da_ref[...].astype(F32), buf, dap_ref, dil)
        for src, dst, dst_t in ((lc_ref, lcp_ref, lt_ref), (dc_ref, dcp_ref, dt_ref)):
            buf[0] = src[...]
            for r in range(dil):
                piece = buf.at[0][pl.ds(r, rows, stride=dil), :] if dil > 1 else buf[0]
                dst[r] = piece
                dst_t[r] = piece.T[0:NH, :]

    row = lambda w: pl.BlockSpec((tm, w), lambda i: (i, 0))
    rm = lambda w: pl.BlockSpec((dil, rows, w), lambda i: (0, i, 0))
    tr = pl.BlockSpec((dil, NH, rows), lambda i: (0, 0, i))
    length = s // dil
    dap, lcp, dcp, lt, dt = pl.pallas_call(
        body, name=f"stats_prep_g{g}", grid=(s // tm,),
        in_specs=[row(CB), row(LANES), row(LANES)],
        out_specs=[rm(CB), rm(LANES), rm(LANES), tr, tr],
        out_shape=[jax.ShapeDtypeStruct((dil, length, CB), BF16)]
                  + [jax.ShapeDtypeStruct((dil, length, LANES), F32)] * 2
                  + [jax.ShapeDtypeStruct((dil, NH, length), F32)] * 2,
        scratch_shapes=[pltpu.VMEM((CB // LANES, tm, LANES), F32)],
        compiler_params=_cp(("parallel",)))(da, lc, dc)
    return (dap.reshape(s, CB), lcp.reshape(s, LANES), dcp.reshape(s, LANES),
            lt.reshape(dil * NH, length), dt.reshape(dil * NH, length))


def qkv_grads_to_dproj(dproj, proj, grads, qw8, kw8, same, tm):
    s = dproj.shape[0]
    ni = s // tm
    flat = [(t.reshape(d, s // d, CB), d, kind, 3 * kind + g)
            for g, d in enumerate(DILATIONS) for kind, t in enumerate(grads[g])]
    nf = len(flat)
    nraw = 2 * len(DILATIONS)

    def body(*refs):
        dp_hbm, raws, ins = refs[nraw + nf + 4], refs[1:1 + nraw], refs[1 + nraw:1 + nraw + nf]
        qw_ref, kw_ref, same_ref = refs[1 + nraw + nf:4 + nraw + nf]
        gw_ref, stage, buf, sems = refs[5 + nraw + nf:]
        i = pl.program_id(0)
        slot = i % 2

        def slab(step, sl):
            return pltpu.make_async_copy(
                stage.at[sl], dp_hbm.at[pl.ds(pl.multiple_of(step * tm, tm), tm), pl.ds(CB_Q * CB, 9 * CB)],
                sems.at[sl])

        @pl.when(i == 0)
        def _():
            gw_ref[...] = jnp.zeros_like(gw_ref)

        @pl.when(i >= 2)
        def _():
            slab(i - 2, slot).wait()

        for ref, (_, d, kind, jj) in zip(ins, flat):
            cols = slice(jj * CB, (jj + 1) * CB)
            dn = _from_residue_major(ref, buf, d)
            if kind == 2:
                stage[slot, :, cols] = dn.astype(BF16)
                continue
            t = raws[jj][...].astype(F32)
            r = lax.rsqrt(_head_sum(t * t, same_ref) * (1.0 / HD) + EPS)
            xh = t * r
            gw_ref[kind:kind + 1, :] += jnp.sum(dn * xh, axis=0, keepdims=True)
            dxh = dn * (qw_ref if kind == 0 else kw_ref)[...]
            mean = _head_sum(dxh * xh, same_ref) * (1.0 / HD)
            stage[slot, :, cols] = (r * (dxh - xh * mean)).astype(BF16)
        slab(i, slot).start()

        @pl.when(i == ni - 1)
        def _():
            slab(i - 1, 1 - slot).wait()
            slab(i, slot).wait()

    full = lambda a: pl.BlockSpec(a.shape, lambda i: (0, 0))
    any_spec = pl.BlockSpec(memory_space=pl.ANY)
    return pl.pallas_call(
        body, name="qkv_grads_to_dproj", grid=(ni,),
        in_specs=[any_spec] + [pl.BlockSpec((tm, CB), lambda i, jb=jb: (i, CB_Q + jb)) for jb in range(nraw)]
                 + [pl.BlockSpec((d, tm // d, CB), lambda i: (0, i, 0)) for _, d, _, _ in flat]
                 + [full(qw8), full(kw8), full(same)],
        out_specs=[any_spec, pl.BlockSpec((8, CB), lambda i: (0, 0))],
        out_shape=[jax.ShapeDtypeStruct((s, NIN), BF16), jax.ShapeDtypeStruct((8, CB), F32)],
        input_output_aliases={0: 0},
        scratch_shapes=[pltpu.VMEM((2, tm, 9 * CB), BF16), pltpu.VMEM((CB // LANES, tm, LANES), F32),
                        pltpu.SemaphoreType.DMA((2,))],
        compiler_params=_cp(("arbitrary",)))(
            dproj, *([proj] * nraw), *[t for t, _, _, _ in flat], qw8, kw8, same)


def _lane_lo():
    return lax.broadcasted_iota(jnp.int32, (1, 2 * HD), 1) < HD


def _stack_heads(t, lo):
    zero = jnp.zeros_like(t)
    return jnp.concatenate([jnp.where(lo, t, zero), jnp.where(lo, zero, t)], axis=0)


def _masks(other_ok):
    qi = lax.broadcasted_iota(jnp.int32, (QB, QB), 0)
    kj = lax.broadcasted_iota(jnp.int32, (QB, QB), 1)
    return (kj >= qi) & other_ok, kj <= qi


SUB = 4


def _attn_specs(nb, dil):
    steps = nb // SUB
    main = lambda cb, w=CB: pl.BlockSpec((SUB * QB, w), lambda r, s: (r * steps + s, cb))
    prev = lambda cb: pl.BlockSpec((QB, CB), lambda r, s: (jnp.maximum(r * nb + SUB * s - 1, 0), cb))
    nxt = lambda cb: pl.BlockSpec((QB, CB), lambda r, s: (jnp.minimum(r * nb + SUB * (s + 1), dil * nb - 1), cb))
    return main, prev, nxt


def attn_fwd(q_src, k_src, v_src, g, dil):
    s = q_src[0].shape[0]
    nb = s // dil // QB
    main, prev, _ = _attn_specs(nb, dil)

    def body(q_ref, kp_ref, k_ref, vp_ref, v_ref, o_ref, l_ref, kbuf, vbuf):
        step = pl.program_id(1)
        kbuf[0:QB], kbuf[QB:] = kp_ref[...], k_ref[...]
        vbuf[0:QB], vbuf[QB:] = vp_ref[...], v_ref[...]
        lo = _lane_lo()

        def block(j, carry):
            r0 = pl.multiple_of(j * QB, QB)
            rows, krows = pl.ds(r0, QB), pl.ds(r0, 2 * QB)
            m_prev, m_cur = _masks(step * SUB + j > 0)
            mask = jnp.concatenate([m_prev, m_cur], axis=1)
            mask = jnp.concatenate([mask, mask], axis=0)
            for i in range(NH // 2):
                sl = slice(2 * HD * i, 2 * HD * (i + 1))
                qs, ks, vv = q_ref[rows, sl], kbuf[krows, sl], vbuf[krows, sl]
                sc = lax.dot_general(_stack_heads(qs, lo), ks, NT, preferred_element_type=F32)
                sc = jnp.where(mask, sc, NEG)
                mx = jnp.max(sc, axis=-1, keepdims=True)
                p = jnp.exp(sc - mx)
                den = jnp.sum(p, axis=-1, keepdims=True)
                o = jnp.dot(p.astype(BF16), vv, preferred_element_type=F32) * (1.0 / den)
                lse = jnp.broadcast_to(mx + jnp.log(den), (2 * QB, 2 * HD))
                o_ref[rows, sl] = jnp.where(lo, o[:QB], o[QB:])
                l_ref[rows, sl] = jnp.where(lo, lse[:QB], lse[QB:])
            return carry

        lax.fori_loop(0, SUB, block, 0, unroll=True)

    out = jax.ShapeDtypeStruct((s, CB), F32)
    return pl.pallas_call(
        body, name=f"attn_fwd_g{g}", grid=(dil, nb // SUB),
        in_specs=[main(q_src[1]), prev(k_src[1]), main(k_src[1]), prev(v_src[1]), main(v_src[1])],
        out_specs=[main(0)] * 2, out_shape=[out, out],
        scratch_shapes=[pltpu.VMEM(((SUB + 1) * QB, CB), BF16)] * 2,
        compiler_params=_cp(("parallel", "parallel")))(q_src[0], k_src[0], k_src[0], v_src[0], v_src[0])


def attn_bwd_q(q_src, k_src, v_src, da, lc, dc, g, dil):
    s = q_src[0].shape[0]
    nb = s // dil // QB
    main, prev, _ = _attn_specs(nb, dil)

    def body(q_ref, kp_ref, k_ref, vp_ref, v_ref, da_ref, l_ref, d_ref, dq_ref, kbuf, vbuf):
        step = pl.program_id(1)
        kbuf[0:QB], kbuf[QB:] = kp_ref[...], k_ref[...]
        vbuf[0:QB], vbuf[QB:] = vp_ref[...], v_ref[...]
        lo = _lane_lo()

        def block(j, carry):
            r0 = pl.multiple_of(j * QB, QB)
            rows, krows = pl.ds(r0, QB), pl.ds(r0, 2 * QB)
            m_prev, m_cur = _masks(step * SUB + j > 0)
            mask = jnp.concatenate([m_prev, m_cur], axis=1)
            mask = jnp.concatenate([mask, mask], axis=0)
            lcols, dcols = l_ref[rows, :], d_ref[rows, :]
            for i in range(NH // 2):
                sl = slice(2 * HD * i, 2 * HD * (i + 1))
                qs, ks, vv, da2 = q_ref[rows, sl], kbuf[krows, sl], vbuf[krows, sl], da_ref[rows, sl]
                pair = lambda t: jnp.concatenate([t[:, 2 * i:2 * i + 1], t[:, 2 * i + 1:2 * i + 2]], axis=0)
                sc = lax.dot_general(_stack_heads(qs, lo), ks, NT, preferred_element_type=F32)
                sc = jnp.where(mask, sc, NEG)
                p = jnp.exp(sc - pair(lcols))
                dp = lax.dot_general(_stack_heads(da2, lo), vv, NT, preferred_element_type=F32)
                ds = p * (dp - pair(dcols))
                dq = jnp.dot(ds.astype(BF16), ks, preferred_element_type=F32)
                dq_ref[rows, sl] = (jnp.where(lo, dq[:QB], dq[QB:]) * (HD ** -0.5)).astype(BF16)
            return carry

        lax.fori_loop(0, SUB, block, 0, unroll=True)

    return pl.pallas_call(
        body, name=f"attn_bwd_q_g{g}", grid=(dil, nb // SUB),
        in_specs=[main(q_src[1]), prev(k_src[1]), main(k_src[1]), prev(v_src[1]), main(v_src[1]),
                  main(0), main(0, LANES), main(0, LANES)],
        out_specs=main(0), out_shape=jax.ShapeDtypeStruct((s, CB), BF16),
        scratch_shapes=[pltpu.VMEM(((SUB + 1) * QB, CB), BF16)] * 2,
        compiler_params=_cp(("parallel", "parallel")))(
            q_src[0], k_src[0], k_src[0], v_src[0], v_src[0], da, lc, dc)


def attn_bwd_kv(q_src, k_src, v_src, da, lt, dt, g, dil):
    s = q_src[0].shape[0]
    nb = s // dil // QB
    main, _, nxt = _attn_specs(nb, dil)

    def body(k_ref, v_ref, q_ref, qn_ref, da_ref, dan_ref, l_ref, ln_ref, d_ref, dn_ref, dk_ref, dv_ref,
             qbuf, dabuf, lbuf, dbuf):
        step = pl.program_id(1)
        qbuf[0:SUB * QB], qbuf[SUB * QB:] = q_ref[...], qn_ref[...]
        dabuf[0:SUB * QB], dabuf[SUB * QB:] = da_ref[...], dan_ref[...]
        for c in range(SUB):
            lbuf[c], dbuf[c] = l_ref[:, c * QB:(c + 1) * QB], d_ref[:, c * QB:(c + 1) * QB]
        lbuf[SUB], dbuf[SUB] = ln_ref[...], dn_ref[...]
        lo = _lane_lo()
        kj = lax.broadcasted_iota(jnp.int32, (QB, QB), 0)
        qi = lax.broadcasted_iota(jnp.int32, (QB, QB), 1)

        def block(j, carry):
            r0 = pl.multiple_of(j * QB, QB)
            rows, qrows = pl.ds(r0, QB), pl.ds(r0, 2 * QB)
            mask = jnp.concatenate([kj <= qi, (kj >= qi) & (step * SUB + j < nb - 1)], axis=1)
            mask = jnp.concatenate([mask, mask], axis=1)
            lrow = jnp.concatenate([lbuf[j], lbuf[j + 1]], axis=1)
            drow = jnp.concatenate([dbuf[j], dbuf[j + 1]], axis=1)
            for i in range(NH // 2):
                sl = slice(2 * HD * i, 2 * HD * (i + 1))
                q2, da2 = _stack_heads(qbuf[qrows, sl], lo), _stack_heads(dabuf[qrows, sl], lo)
                ks, vv = k_ref[rows, sl], v_ref[rows, sl]
                pair = lambda t: jnp.concatenate([t[2 * i:2 * i + 1, :], t[2 * i + 1:2 * i + 2, :]], axis=1)
                sc = lax.dot_general(ks, q2, NT, preferred_element_type=F32)
                sc = jnp.where(mask, sc, NEG)
                p = jnp.exp(sc - pair(lrow))
                dp = lax.dot_general(vv, da2, NT, preferred_element_type=F32)
                ds = p * (dp - pair(drow))
                dv_ref[rows, sl] = jnp.dot(p.astype(BF16), da2, preferred_element_type=F32).astype(BF16)
                dk_ref[rows, sl] = jnp.dot(ds.astype(BF16), q2, preferred_element_type=F32).astype(BF16)
            return carry

        lax.fori_loop(0, SUB, block, 0, unroll=True)

    steps = nb // SUB
    t_main = pl.BlockSpec((NH, SUB * QB), lambda r, s: (r, s))
    t_nxt = pl.BlockSpec((NH, QB), lambda r, s: (r, jnp.minimum(SUB * (s + 1), nb - 1)))
    out = jax.ShapeDtypeStruct((s, CB), BF16)
    return pl.pallas_call(
        body, name=f"attn_bwd_kv_g{g}", grid=(dil, steps),
        in_specs=[main(k_src[1]), main(v_src[1]), main(q_src[1]), nxt(q_src[1]),
                  main(0), nxt(0), t_main, t_nxt, t_main, t_nxt],
        out_specs=[main(0), main(0)], out_shape=[out, out],
        scratch_shapes=[pltpu.VMEM(((SUB + 1) * QB, CB), BF16)] * 2 + [pltpu.VMEM((SUB + 1, NH, QB), F32)] * 2,
        compiler_params=_cp(("parallel", "parallel")))(
            k_src[0], v_src[0], q_src[0], q_src[0], da, da, lt, lt, dt, dt)


def _conv_taps(u, u_prev, first):
    tm = u.shape[0]
    row = lax.broadcasted_iota(jnp.int32, (tm, 1), 0)
    up = jnp.where(first, 0.0, u_prev)
    u1 = jnp.where(row == 0, up[HALO - 1:HALO, :], pltpu.roll(u, 1, 0))
    u2 = jnp.where(row == 0, up[HALO - 2:HALO - 1, :],
                   jnp.where(row == 1, up[HALO - 1:HALO, :], pltpu.roll(u, 2, 0)))
    return u1, u2


def mid_fwd(proj, o_g, lse_g, conv_w, pick, tm):
    s = proj.shape[0]
    hb = tm // HALO

    def body(ba_ref, ca_ref, xa_ref, za_ref, cah_ref, xah_ref, zb_ref,
             o0, o1, o2, l0, l1, l2, w_ref, pick_ref, ya_ref, yb_ref, at_ref, lc_ref, buf_o, buf_l):
        first = pl.program_id(0) == 0
        u = ca_ref[...].astype(F32) * xa_ref[...].astype(F32)
        u1, u2 = _conv_taps(u, cah_ref[...].astype(F32) * xah_ref[...].astype(F32), first)
        conv = w_ref[0:1, :] * u2 + w_ref[1:2, :] * u1 + w_ref[2:3, :] * u
        ya_ref[...] = (ba_ref[...].astype(F32) * conv * _silu(za_ref[...].astype(F32))).astype(BF16)
        ls = [_from_residue_major(l, buf_l.at[g], d) for g, (l, d) in enumerate(zip((l0, l1, l2), DILATIONS))]
        mx = jnp.maximum(jnp.maximum(ls[0], ls[1]), ls[2])
        es = [jnp.exp(l - mx) for l in ls]
        den = es[0] + es[1] + es[2]
        num = jnp.zeros_like(den)
        for e, o, d in zip(es, (o0, o1, o2), DILATIONS):
            num = num + e * _from_residue_major(o, buf_o, d)
        attn = num / den
        at_ref[...] = attn
        lc_ref[...] = _dot_hilo(mx + jnp.log(den), pick_ref)
        yb_ref[...] = (attn * _silu(zb_ref[...].astype(F32))).astype(BF16)

    col = lambda j: pl.BlockSpec((tm, D), lambda i: (i, j))
    halo = lambda j: pl.BlockSpec((HALO, D), lambda i: (jnp.maximum(i * hb - 1, 0), j))
    loc = lambda w: pl.BlockSpec((tm, w), lambda i: (i, 0))
    rm = [pl.BlockSpec((d, tm // d, CB), lambda i: (0, i, 0)) for d in DILATIONS]
    rm_view = lambda ts: [t.reshape(d, s // d, CB) for t, d in zip(ts, DILATIONS)]
    return pl.pallas_call(
        body, name="mid_fwd", grid=(s // tm,),
        in_specs=[col(0), col(1), col(2), col(3), halo(1), halo(2),
                  pl.BlockSpec((tm, CB), lambda i: (i, CB_ZB))] + rm + rm
                 + [pl.BlockSpec((3, D), lambda i: (0, 0)), pl.BlockSpec(pick.shape, lambda i: (0, 0))],
        out_specs=[loc(D), loc(CB), loc(CB), loc(LANES)],
        out_shape=[jax.ShapeDtypeStruct((s, D), BF16), jax.ShapeDtypeStruct((s, CB), BF16),
                   jax.ShapeDtypeStruct((s, CB), F32), jax.ShapeDtypeStruct((s, LANES), F32)],
        scratch_shapes=[pltpu.VMEM((CB // LANES, tm, LANES), F32), pltpu.VMEM((3, CB // LANES, tm, LANES), F32)],
        compiler_params=_cp(("parallel",)))(
            proj, proj, proj, proj, proj, proj, proj, *rm_view(o_g), *rm_view(lse_g), conv_w, pick)


def mid_bwd(dproj, proj, dya, conv_w, tm):
    s = proj.shape[0]
    hb = tm // HALO
    nblk = s // tm
    last_h = s // HALO - 1

    def body(_, ba_ref, ca_ref, xa_ref, za_ref, cah_ref, xah_ref, ban_ref, zan_ref, dy_ref, dyn_ref, w_ref,
             o_ref, gw_ref):
        i = pl.program_id(0)
        ba, ca, xa, za = (t[...].astype(F32) for t in (ba_ref, ca_ref, xa_ref, za_ref))
        u = ca * xa
        u1, u2 = _conv_taps(u, cah_ref[...].astype(F32) * xah_ref[...].astype(F32), i == 0)
        w0, w1, w2 = w_ref[0:1, :], w_ref[1:2, :], w_ref[2:3, :]
        conv = w0 * u2 + w1 * u1 + w2 * u
        sg = jax.nn.sigmoid(za)
        sz = za * sg
        dy = dy_ref[...].astype(F32)
        dconv = dy * ba * sz
        dcn = dyn_ref[...].astype(F32) * ban_ref[...].astype(F32) * _silu(zan_ref[...].astype(F32))
        dcn = jnp.where(i == nblk - 1, 0.0, dcn)
        row = lax.broadcasted_iota(jnp.int32, (tm, 1), 0)
        d1 = jnp.where(row == tm - 1, dcn[0:1, :], pltpu.roll(dconv, tm - 1, 0))
        d2 = jnp.where(row == tm - 2, dcn[0:1, :],
                       jnp.where(row == tm - 1, dcn[1:2, :], pltpu.roll(dconv, tm - 2, 0)))
        du = w2 * dconv + w1 * d1 + w0 * d2
        o_ref[:, 0:D] = (dy * conv * sz).astype(BF16)
        o_ref[:, D:2 * D] = (du * xa).astype(BF16)
        o_ref[:, 2 * D:3 * D] = (du * ca).astype(BF16)
        o_ref[:, 3 * D:4 * D] = (dy * ba * conv * (sg * (1.0 + za * (1.0 - sg)))).astype(BF16)

        @pl.when(i == 0)
        def _():
            gw_ref[...] = jnp.zeros_like(gw_ref)

        gw_ref[0:1, :] += jnp.sum(dconv * u2, axis=0, keepdims=True)
        gw_ref[1:2, :] += jnp.sum(dconv * u1, axis=0, keepdims=True)
        gw_ref[2:3, :] += jnp.sum(dconv * u, axis=0, keepdims=True)

    col = lambda j: pl.BlockSpec((tm, D), lambda i: (i, j))
    halo_prev = lambda j: pl.BlockSpec((HALO, D), lambda i: (jnp.maximum(i * hb - 1, 0), j))
    halo_next = lambda j: pl.BlockSpec((HALO, D), lambda i: (jnp.minimum((i + 1) * hb, last_h), j))
    return pl.pallas_call(
        body, name="mid_bwd", grid=(nblk,),
        in_specs=[pl.BlockSpec(memory_space=pl.ANY), col(0), col(1), col(2), col(3),
                  halo_prev(1), halo_prev(2), halo_next(0), halo_next(3),
                  pl.BlockSpec((tm, D), lambda i: (i, 0)), halo_next(0),
                  pl.BlockSpec((3, D), lambda i: (0, 0))],
        out_specs=[pl.BlockSpec((tm, 4 * D), lambda i: (i, 0)), pl.BlockSpec((8, D), lambda i: (0, 0))],
        out_shape=[jax.ShapeDtypeStruct((s, NIN), BF16), jax.ShapeDtypeStruct((8, D), F32)],
        input_output_aliases={0: 0},
        compiler_params=_cp(("arbitrary",)))(dproj, proj, proj, proj, proj, proj, proj, proj, proj, dya, dya, conv_w)


def tail(proj, ya, yb, attn, x, target, gate, pa_w, pb_w, wo_w, total, tm):
    s = proj.shape[0]
    ni = s // tm
    ncol = NIN - CB_ZB * CB

    def body(ya_ref, yb_ref, ga_ref, gb_ref, zb_ref, at_ref, x_ref, t_ref, gate_ref, pa_ref, pb_ref, wo_ref,
             tot_ref, dp_hbm, dy_ref, dya_ref, da_ref, dc_ref, mg_ref, do_ref, dpa_ref, dpb_ref, st_ref,
             stage, sems):
        i = pl.program_id(0)
        slot = i % 2

        def slab(step, sl):
            return pltpu.make_async_copy(
                stage.at[sl], dp_hbm.at[pl.ds(pl.multiple_of(step * tm, tm), tm), pl.ds(CB_ZB * CB, ncol)],
                sems.at[sl])

        @pl.when(i == 0)
        def _():
            st_ref[...] = jnp.zeros_like(st_ref)

        @pl.when(i >= 2)
        def _():
            slab(i - 2, slot).wait()

        gate_v = gate_ref[...]
        pa = jnp.dot(ya_ref[...], pa_ref[...], preferred_element_type=F32)
        pb = jnp.dot(yb_ref[...], pb_ref[...], preferred_element_type=F32)
        sa = jax.nn.sigmoid(ga_ref[...].astype(F32))
        sb = jax.nn.sigmoid(gb_ref[...].astype(F32))
        merged = (sa * pa + sb * pb).astype(BF16)
        mg_ref[...] = merged
        out = jnp.dot(merged, wo_ref[...], preferred_element_type=F32)
        err = x_ref[...] + gate_v * out - t_ref[...]
        dy = err * (1.0 / D)
        dy_ref[...] = dy
        st_ref[0:1, :] += jnp.sum(dy * out, axis=0, keepdims=True)
        st_ref[1:2, :] += jnp.sum(err * err, axis=0, keepdims=True)
        dout = (gate_v * dy).astype(BF16)
        do_ref[...] = dout
        dmg = lax.dot_general(dout, wo_ref[...], NT, preferred_element_type=F32)
        dpa = (dmg * sa).astype(BF16)
        dpb = (dmg * sb).astype(BF16)
        dpa_ref[...] = dpa
        dpb_ref[...] = dpb
        stage[slot, :, CB:CB + D] = (dmg * pa * sa * (1.0 - sa)).astype(BF16)
        stage[slot, :, CB + D:] = (dmg * pb * sb * (1.0 - sb)).astype(BF16)
        dya_ref[...] = lax.dot_general(dpa, pa_ref[...], NT, preferred_element_type=F32).astype(BF16)
        dyb = lax.dot_general(dpb, pb_ref[...], NT, preferred_element_type=F32)
        zb = zb_ref[...].astype(F32)
        sg = jax.nn.sigmoid(zb)
        attn_v = at_ref[...]
        dattn = dyb * (zb * sg)
        da_ref[...] = dattn.astype(BF16)
        stage[slot, :, 0:CB] = (dyb * attn_v * (sg * (1.0 + zb * (1.0 - sg)))).astype(BF16)
        dc_ref[...] = _dot_hilo(dattn * attn_v, tot_ref)

        slab(i, slot).start()

        @pl.when(i == ni - 1)
        def _():
            slab(i - 1, 1 - slot).wait()
            slab(i, slot).wait()

    row = lambda w: pl.BlockSpec((tm, w), lambda i: (i, 0))
    pcol = lambda w, jb: pl.BlockSpec((tm, w), lambda i: (i, jb))
    full = lambda a: pl.BlockSpec(a.shape, lambda i: (0, 0))
    return pl.pallas_call(
        body, name="tail", grid=(ni,),
        in_specs=[row(D), row(CB), pcol(D, 9), pcol(D, 10), pcol(CB, CB_ZB), row(CB), row(D), row(D),
                  pl.BlockSpec((1, D), lambda i: (0, 0)), full(pa_w), full(pb_w), full(wo_w), full(total)],
        out_specs=[pl.BlockSpec(memory_space=pl.ANY),
                   row(D), row(D), row(CB), row(LANES), row(D), row(D), row(D), row(D),
                   pl.BlockSpec((8, D), lambda i: (0, 0))],
        out_shape=[jax.ShapeDtypeStruct((s, NIN), BF16), jax.ShapeDtypeStruct((s, D), F32),
                   jax.ShapeDtypeStruct((s, D), BF16), jax.ShapeDtypeStruct((s, CB), BF16),
                   jax.ShapeDtypeStruct((s, LANES), F32)] + [jax.ShapeDtypeStruct((s, D), BF16)] * 4
                  + [jax.ShapeDtypeStruct((8, D), F32)],
        scratch_shapes=[pltpu.VMEM((2, tm, ncol), BF16), pltpu.SemaphoreType.DMA((2,))],
        compiler_params=_cp(("arbitrary",), 56))(
            ya, yb, proj, proj, proj, attn, x, target, gate, pa_w, pb_w, wo_w, total)


def _local_step(x, target, shift, scale, gate, norm_w, conv_w, qw, kw, w_shard, small_shards, me_xyc):
    qw8, kw8 = jnp.tile(qw, (1, NH)), jnp.tile(kw, (1, NH))
    same, total, pick = _head_matrices()
    h, ht = norm_fwd(x, norm_w, scale, shift, 512)
    proj, wg, (pa_g, pb_g, wo_g) = proj_fwd_gather(h, w_shard, small_shards, gather_order(me_xyc), 1024)
    pa_w, wo_w = pa_g.reshape(D, D), wo_g.reshape(D, D)
    pb_w = pb_g.transpose(1, 0, 2).reshape(CB, D)
    srcs = qkv_prep(proj, qw8, kw8, same, 512)
    o_g, lse_g = zip(*[attn_fwd(*srcs[g], g, d) for g, d in enumerate(DILATIONS)])
    ya, yb, attn, lc = mid_fwd(proj, o_g, lse_g, conv_w, pick, 512)
    dproj, dy, dya, da, dc, merged, dout, dpa, dpb, st_tail = tail(
        proj, ya, yb, attn, x, target, gate, pa_w, pb_w, wo_w, total, 256)
    g_wo = matmul_tn(merged, dout, "grad_w_out", 1024)
    g_pa = matmul_tn(ya, dpa, "grad_w_br_conv", 1024)
    g_pb = matmul_tn(yb, dpb, "grad_w_br_attn", 1024)
    dproj, st_conv = mid_bwd(dproj, proj, dya, conv_w, 512)
    grads = []
    for g, d in enumerate(DILATIONS):
        da_p, lc_p, dc_p, lt, dt = stats_prep(da, lc, dc, g, d, 2048)
        dq = attn_bwd_q(*srcs[g], da_p, lc_p, dc_p, g, d)
        dk, dv = attn_bwd_kv(*srcs[g], da_p, lt, dt, g, d)
        grads.append((dq, dk, dv))
    dproj, gw_qk = qkv_grads_to_dproj(dproj, proj, grads, qw8, kw8, same, 512)
    slabs = [g_pa.reshape(NDEV, 128, D), g_pb.reshape(CB, NDEV, 128).transpose(1, 0, 2), g_wo.reshape(NDEV, 128, D)]
    dh, r_win, (r_pa, r_pb, r_wo) = proj_bwd(ht, dproj, wg, slabs, scatter_order(me_xyc), 512)
    grad_x, st_norm = norm_bwd(dh, x, dy, norm_w, scale, 512)
    dmod = jnp.concatenate([st_norm[0:1], st_norm[1:2], st_tail[0:1]], axis=1)
    loss_part = (0.5 / D) * jnp.sum(st_tail[1])
    gw_heads = gw_qk[0:2].reshape(2, NH, HD).sum(axis=1)
    small = dict(dmod=dmod, norm_w=st_norm[2:3], conv_w=st_conv[0:3],
                 q_norm_w=gw_heads[0:1], k_norm_w=gw_heads[1:2], loss=loss_part)
    return grad_x, small, (r_win, r_pa, r_pb, r_wo)


def kernel(x, c, w_ada, b_ada, norm_w, w_in, conv_w, q_norm_w, k_norm_w, w_br_conv, w_br_attn, w_out, loss_target, m_w_ada, m_b_ada, m_norm_w, m_w_in, m_conv_w, m_q_norm_w, m_k_norm_w, m_w_br_conv, m_w_br_attn, m_w_out, v_w_ada, v_b_ada, v_norm_w, v_w_in, v_conv_w, v_q_norm_w, v_k_norm_w, v_w_br_conv, v_w_br_attn, v_w_out):
    me_xyc = (lax.axis_index("x"), lax.axis_index("y"), lax.axis_index("c"))
    me = _dev_index(me_xyc)
    ncol = w_ada.shape[2]

    conv_pad = jnp.zeros((8, 128), F32).at[0:3].set(conv_w[0])
    c_all, conv_all = all_gather([c, conv_pad], "gather_cond")
    conv_full = conv_all[:, 0:3].transpose(1, 0, 2).reshape(3, D)
    c_all = c_all.reshape(NDEV, D)

    b_cols = lax.dynamic_slice(b_ada, (0, me * ncol), (1, ncol))
    mod_cols = ada_fwd(c_all, w_ada[0], b_cols)
    (mod_all,) = all_gather([mod_cols], "gather_mod")
    mod = lax.dynamic_index_in_dim(mod_all, me, axis=1, keepdims=False).reshape(1, 3 * D)
    shift, scale, gate = mod[:, 0:D], mod[:, D:2 * D], mod[:, 2 * D:3 * D]

    grad_x, small, (r_win, r_pa, r_pb, r_wo) = _local_step(
        x[0], loss_target[0], shift, scale, gate, norm_w, conv_full, q_norm_w, k_norm_w,
        w_in[0].astype(BF16), [w_br_conv[0].astype(BF16), w_br_attn[0].astype(BF16), w_out[0].astype(BF16)], me_xyc)

    packed = jnp.concatenate(
        [small["dmod"], small["norm_w"], small["conv_w"].reshape(1, 3 * D), small["q_norm_w"], small["k_norm_w"],
         jnp.full((1, 128), small["loss"], F32)], axis=1)
    (packed_all,) = all_gather([packed], "gather_small")
    tot = sum_parts(packed_all)
    loss = tot[0, 7 * D + 2 * HD]
    dmod_all = packed_all[:, 0, 0:3 * D]
    g_b_ada = tot[:, 0:3 * D]
    g_norm_w = tot[:, 3 * D:4 * D]
    g_conv = lax.dynamic_slice(tot[:, 4 * D:7 * D].reshape(3, D), (0, me * 128), (3, 128))
    g_qn = tot[:, 7 * D:7 * D + HD]
    g_kn = tot[:, 7 * D + HD:7 * D + 2 * HD]
    g_w_ada = ada_bwd(c_all.T, lax.dynamic_slice(dmod_all, (0, me * ncol), (NDEV, ncol)))

    def upd(parts, w, m, v, name, rows):
        shape = w.shape
        w2, m2, v2 = (t.reshape(shape[-2:]) for t in (w, m, v))
        return [t.reshape(shape) for t in adamw(parts, w2, m2, v2, name, rows)]

    res = {
        "w_ada": upd(g_w_ada[None], w_ada, m_w_ada, v_w_ada, "adamw_w_ada", 256),
        "b_ada": upd(g_b_ada[None], b_ada, m_b_ada, v_b_ada, "adamw_b_ada", 1),
        "norm_w": upd(g_norm_w[None], norm_w, m_norm_w, v_norm_w, "adamw_norm_w", 1),
        "w_in": upd(r_win, w_in, m_w_in, v_w_in, "adamw_w_in", 128),
        "conv_w": upd(g_conv[None], conv_w, m_conv_w, v_conv_w, "adamw_conv_w", 3),
        "q_norm_w": upd(g_qn[None], q_norm_w, m_q_norm_w, v_q_norm_w, "adamw_q_norm_w", 1),
        "k_norm_w": upd(g_kn[None], k_norm_w, m_k_norm_w, v_k_norm_w, "adamw_k_norm_w", 1),
        "w_br_conv": upd(r_pa, w_br_conv, m_w_br_conv, v_w_br_conv, "adamw_w_br_conv", 128),
        "w_br_attn": upd(r_pb, w_br_attn, m_w_br_attn, v_w_br_attn, "adamw_w_br_attn", 512),
        "w_out": upd(r_wo, w_out, m_w_out, v_w_out, "adamw_w_out", 128),
    }
    names = ["w_ada", "b_ada", "norm_w", "w_in", "conv_w", "q_norm_w", "k_norm_w", "w_br_conv", "w_br_attn", "w_out"]
    return (loss, grad_x[None], *[res[n][0] for n in names], *[res[n][1] for n in names],
            *[res[n][2] for n in names], *[res[n][3] for n in names])
```

```python
import jax
import jax.numpy as jnp
from jax import lax
from jax.experimental import pallas as pl
from jax.experimental.pallas import tpu as pltpu

F32, BF16 = jnp.float32, jnp.bfloat16
D = 1024
NIN = 11264
NDEV = 8
SHARD = NIN // NDEV
HD = 64
NH = 8
QB = 128
CB = 512
CB_Q, CB_K, CB_V, CB_ZB = 8, 11, 14, 17
DILATIONS = (1, 4, 16)
EPS = 1e-6
NEG = -1e30
HALO = 16
ROWS = 32
LANES = 128
MESH = pl.DeviceIdType.MESH

ADAM_LR, ADAM_B1, ADAM_B2, ADAM_EPS, ADAM_WD, ADAM_STEP = 0.001, 0.9, 0.999, 1e-08, 0.01, 10

NT = (((1,), (1,)), ((), ()))
TN = (((0,), (0,)), ((), ()))


def _cp(sem, vmem_mb=48):
    return pltpu.CompilerParams(dimension_semantics=sem, vmem_limit_bytes=vmem_mb << 20)


def _silu(z):
    return z * jax.nn.sigmoid(z)


def _coords():
    return lax.axis_index("x"), lax.axis_index("y"), lax.axis_index("c")


def all_gather(arrs, name):
    n = len(arrs)

    def body(*refs):
        ins, outs = refs[:n], refs[n:2 * n]
        send_sems, recv_sems, local_sems = refs[2 * n:]
        x, y, c = _coords()
        me, sibling = (x, y, c), (x, y, 1 - c)
        chips = [(1 - x, y), (x, 1 - y), (1 - x, 1 - y)]

        def slot(a, dev):
            return outs[a].at[4 * dev[0] + 2 * dev[1] + dev[2]]

        def copy(a, k, block, to, src=None):
            return pltpu.make_async_remote_copy(
                src_ref=slot(a, block) if src is None else src, dst_ref=slot(a, block),
                send_sem=send_sems.at[a, k], recv_sem=recv_sems.at[a, k],
                device_id=to, device_id_type=MESH)

        mine = [pltpu.make_async_copy(ins[a], slot(a, me), local_sems.at[a]) for a in range(n)]
        for cp in mine:
            cp.start()
        first = []
        for a in range(n):
            first.append(copy(a, 0, me, sibling, src=ins[a]))
            first += [copy(a, 1 + j, me, (*chip, c), src=ins[a]) for j, chip in enumerate(chips)]
        for cp in first:
            cp.start()
        passed = []
        for j, chip in enumerate(chips):
            for a in range(n):
                copy(a, 1 + j, (*chip, c), me).wait_recv()
                fwd = copy(a, 4 + j, (*chip, c), sibling)
                fwd.start()
                passed.append(fwd)
        for a in range(n):
            copy(a, 0, sibling, me).wait_recv()
            for j, chip in enumerate(chips):
                copy(a, 4 + j, (*chip, 1 - c), me).wait_recv()
        for cp in first + passed:
            cp.wait_send()
        for cp in mine:
            cp.wait()

    any_spec = pl.BlockSpec(memory_space=pl.ANY)
    return pl.pallas_call(
        body, name=name,
        out_shape=[jax.ShapeDtypeStruct((NDEV,) + a.shape, a.dtype) for a in arrs],
        in_specs=[any_spec] * n, out_specs=[any_spec] * n,
        scratch_shapes=[pltpu.SemaphoreType.DMA((n, 7)), pltpu.SemaphoreType.DMA((n, 7)),
                        pltpu.SemaphoreType.DMA((n,))],
    )(*arrs)


FLIPS = [(fx, fy, fc) for fx in (0, 1) for fy in (0, 1) for fc in (0, 1)][1:]


def _flip(dev, f):
    return tuple(1 - v if b else v for v, b in zip(dev, f))


def _dev_index(dev):
    return 4 * dev[0] + 2 * dev[1] + dev[2]


def _chip_order(x, y, c):
    xor = lambda a, b: a + b - 2 * a * b
    return [(xor(x, 1 - c), xor(y, c)), (xor(x, c), xor(y, 1 - c)), (1 - x, 1 - y)]


def gather_order(me_xyc):
    x, y, c = me_xyc
    chips = _chip_order(x, y, c)
    devs = [(x, y, c), (x, y, 1 - c), (*chips[0], c), (*chips[1], c),
            (*chips[1], 1 - c), (*chips[0], 1 - c), (*chips[2], c), (*chips[2], 1 - c)]
    return jnp.stack([_dev_index(d) for d in devs]).astype(jnp.int32)


def scatter_order(me_xyc):
    devs = [_flip(me_xyc, f) for f in FLIPS] + [me_xyc]
    return jnp.stack([_dev_index(d) for d in devs]).astype(jnp.int32)


def ada_fwd(c_all, w_ada, b_cols):
    def body(c_ref, w_ref, b_ref, o_ref):
        a = _silu(c_ref[...]).astype(BF16)
        o_ref[...] = jnp.dot(a, w_ref[...].astype(BF16), preferred_element_type=F32) + b_ref[...]

    return pl.pallas_call(body, name="ada_fwd",
                          out_shape=jax.ShapeDtypeStruct((NDEV, w_ada.shape[1]), F32))(c_all, w_ada, b_cols)


def ada_bwd(c_all_t, dmod_cols):
    def body(c_ref, d_ref, o_ref):
        at = _silu(c_ref[...])
        acc = at[:, 0:1] * d_ref[0:1, :]
        for b in range(1, NDEV):
            acc = acc + at[:, b:b + 1] * d_ref[b:b + 1, :]
        o_ref[...] = acc

    return pl.pallas_call(body, name="ada_bwd",
                          out_shape=jax.ShapeDtypeStruct((D, dmod_cols.shape[1]), F32))(c_all_t, dmod_cols)


def sum_parts(parts):
    def body(p_ref, o_ref):
        acc = p_ref[0]
        for b in range(1, NDEV):
            acc = acc + p_ref[b]
        o_ref[...] = acc

    return pl.pallas_call(body, name="sum_parts",
                          out_shape=jax.ShapeDtypeStruct(parts.shape[1:], F32))(parts)


def adamw(parts, w, m, v, name, rows):
    n, r, ccols = parts.shape

    def body(p_ref, w_ref, m_ref, v_ref, g_ref, d_ref, nm_ref, nv_ref):
        g = p_ref[0].astype(F32)
        for b in range(1, n):
            g = g + p_ref[b].astype(F32)
        nm = ADAM_B1 * m_ref[...] + (1.0 - ADAM_B1) * g
        nv = ADAM_B2 * v_ref[...] + (1.0 - ADAM_B2) * (g * g)
        g_ref[...] = g
        nm_ref[...] = nm
        nv_ref[...] = nv
        m_hat = nm / (1.0 - ADAM_B1 ** ADAM_STEP)
        v_hat = nv / (1.0 - ADAM_B2 ** ADAM_STEP)
        d_ref[...] = -ADAM_LR * (m_hat / (jnp.sqrt(v_hat) + ADAM_EPS) + ADAM_WD * w_ref[...])

    blk = pl.BlockSpec((rows, ccols), lambda i: (i, 0))
    out = jax.ShapeDtypeStruct((r, ccols), F32)
    return pl.pallas_call(
        body, name=name, grid=(r // rows,),
        in_specs=[pl.BlockSpec((n, rows, ccols), lambda i: (0, i, 0)), blk, blk, blk],
        out_specs=[blk] * 4, out_shape=[out] * 4, compiler_params=_cp(("parallel",)))(parts, w, m, v)


def norm_fwd(x, nw, scale, shift, tm):
    s = x.shape[0]

    def body(x_ref, nw_ref, sc_ref, sh_ref, h_ref, ht_ref):
        xf = x_ref[...]
        r = lax.rsqrt(jnp.mean(xf * xf, axis=-1, keepdims=True) + EPS)
        h = (xf * r * nw_ref[...]) * (1.0 + sc_ref[...]) + sh_ref[...]
        h_ref[...] = h.astype(BF16)
        ht_ref[...] = h.T.astype(BF16)

    vec = pl.BlockSpec((1, D), lambda i: (0, 0))
    return pl.pallas_call(
        body, name="norm_fwd", grid=(s // tm,),
        in_specs=[pl.BlockSpec((tm, D), lambda i: (i, 0)), vec, vec, vec],
        out_specs=[pl.BlockSpec((tm, D), lambda i: (i, 0)), pl.BlockSpec((D, tm), lambda i: (0, i))],
        out_shape=[jax.ShapeDtypeStruct((s, D), BF16), jax.ShapeDtypeStruct((D, s), BF16)],
        compiler_params=_cp(("parallel",)))(x, nw, scale, shift)


def norm_bwd(dh, x, dy, nw, scale, tm):
    s = x.shape[0]

    def body(dh_ref, x_ref, dy_ref, nw_ref, sc_ref, gx_ref, st_ref):
        @pl.when(pl.program_id(0) == 0)
        def _():
            st_ref[...] = jnp.zeros_like(st_ref)

        nw, sc1 = nw_ref[...], 1.0 + sc_ref[...]

        def chunk(c, sums):
            rows = pl.ds(pl.multiple_of(c * ROWS, ROWS), ROWS)
            xf, g = x_ref[rows, :], dh_ref[rows, :]
            r = lax.rsqrt(jnp.mean(xf * xf, axis=-1, keepdims=True) + EPS)
            xh = xf * r
            dn = g * sc1
            dxh = dn * nw
            gx_ref[rows, :] = dy_ref[rows, :] + r * (dxh - xh * jnp.mean(dxh * xh, axis=-1, keepdims=True))
            fold = lambda t: jnp.sum(t.reshape(ROWS // 8, 8, D), axis=0)
            return sums[0] + fold(g), sums[1] + fold(g * xh * nw), sums[2] + fold(dn * xh)

        zero = jnp.zeros((8, D), F32)
        sums = lax.fori_loop(0, tm // ROWS, chunk, (zero, zero, zero), unroll=True)
        for j in range(3):
            st_ref[j:j + 1, :] += jnp.sum(sums[j], axis=0, keepdims=True)

    vec = pl.BlockSpec((1, D), lambda i: (0, 0))
    row = pl.BlockSpec((tm, D), lambda i: (i, 0))
    return pl.pallas_call(
        body, name="norm_bwd", grid=(s // tm,),
        in_specs=[row, row, row, vec, vec],
        out_specs=[row, pl.BlockSpec((8, D), lambda i: (0, 0))],
        out_shape=[jax.ShapeDtypeStruct((s, D), F32), jax.ShapeDtypeStruct((8, D), F32)],
        compiler_params=_cp(("arbitrary",)))(dh, x, dy, nw, scale)


def proj_fwd_gather(h, w_shard, extras, order, tm):
    s = h.shape[0]
    ni = s // tm
    n = 1 + len(extras)
    mid = ni - 2

    def body(order_ref, h_ref, *refs):
        ins, o_ref, outs = refs[:n], refs[n], refs[n + 1:2 * n + 1]
        wbuf, send_sems, recv_sems, local_sems, load_sems = refs[2 * n + 1:]
        jj, i = pl.program_id(0), pl.program_id(1)
        x, y, c = _coords()
        me, sibling = (x, y, c), (x, y, 1 - c)
        chips = _chip_order(x, y, c)
        relayed = [(*chips[1], 1 - c), (*chips[0], 1 - c), (*chips[2], 1 - c)]

        def slot(a, dev):
            return outs[a].at[_dev_index(dev)]

        def copy(a, k, block, to, src=None):
            return pltpu.make_async_remote_copy(
                src_ref=slot(a, block) if src is None else src, dst_ref=slot(a, block),
                send_sem=send_sems.at[a, k], recv_sem=recv_sems.at[a, k], device_id=to, device_id_type=MESH)

        mine = [pltpu.make_async_copy(ins[a], slot(a, me), local_sems.at[a]) for a in range(n)]
        to_sibling = [copy(a, 0, me, sibling, src=ins[a]) for a in range(n)]
        to_chip = [[copy(a, 1 + j, me, (*chips[j], c), src=ins[a]) for a in range(n)] for j in range(2)]
        onward = [copy(a, 3, (*chips[1], c), (*chips[0], c)) for a in range(n)]
        passed = [[copy(a, 4 + j, (*ch, c), sibling) for a in range(n)] for j, ch in enumerate(chips)]
        sends = lambda a: [to_sibling[a], to_chip[0][a], to_chip[1][a], onward[a]] + [passed[j][a] for j in range(3)]

        def arrived(a, j):
            copy(a, 1 + j, (*chips[j], c), me).wait_recv()

        def load(row):
            return pltpu.make_async_copy(outs[0].at[order_ref[row]], wbuf.at[row % 2], load_sems.at[row % 2])

        @pl.when((jj == 0) & (i == 0))
        def _():
            for cp in mine:
                cp.start()
            to_sibling[0].start()
            to_chip[0][0].start()
            pltpu.make_async_copy(ins[0], wbuf.at[0], load_sems.at[0]).start()

        @pl.when((jj == 1) & (i == 0))
        def _():
            to_chip[1][0].start()

        @pl.when((jj == 4) & (i == 0))
        def _():
            for a in range(1, n):
                to_sibling[a].start()
                to_chip[0][a].start()
                to_chip[1][a].start()

        direct = {2: 0, 3: 1, 6: 2}
        relay = {4: 0, 5: 1, 7: 2}

        @pl.when((jj == 0) & (i == mid))
        def _():
            copy(0, 0, sibling, me).wait_recv()

        for row, j in direct.items():
            @pl.when((jj == row - 1) & (i == mid))
            def _(j=j):
                arrived(0, j)
                passed[j][0].start()
                if j == 1:
                    onward[0].start()

        for row, j in relay.items():
            @pl.when((jj == row - 1) & (i == mid))
            def _(j=j):
                copy(0, 4 + j, relayed[j], me).wait_recv()

        @pl.when((jj == NDEV - 1) & (i == 0))
        def _():
            for a in range(1, n):
                arrived(a, 1)
                onward[a].start()
                passed[1][a].start()
                arrived(a, 0)
                passed[0][a].start()

        @pl.when((jj < NDEV - 1) & (i == mid))
        def _():
            load(jj + 1).start()

        @pl.when(i == 0)
        def _():
            load(jj).wait()

        o_ref[...] = jnp.dot(h_ref[...], wbuf[jj % 2], preferred_element_type=F32).astype(BF16)

        @pl.when((jj == NDEV - 1) & (i == ni - 1))
        def _():
            for a in range(1, n):
                arrived(a, 2)
                passed[2][a].start()
            for a in range(1, n):
                copy(a, 0, sibling, me).wait_recv()
                for j in range(3):
                    copy(a, 4 + j, relayed[j], me).wait_recv()
            for a in range(n):
                mine[a].wait()
                for cp in sends(a):
                    cp.wait_send()

    any_spec = pl.BlockSpec(memory_space=pl.ANY)
    outs = pl.pallas_call(
        body, name="proj_fwd_gather",
        grid_spec=pltpu.PrefetchScalarGridSpec(
            num_scalar_prefetch=1, grid=(NDEV, ni),
            in_specs=[pl.BlockSpec((tm, D), lambda jj, i, o: (i, 0))] + [any_spec] * n,
            out_specs=[pl.BlockSpec((tm, SHARD), lambda jj, i, o: (i, o[jj]))] + [any_spec] * n,
            scratch_shapes=[pltpu.VMEM((2, D, SHARD), BF16), pltpu.SemaphoreType.DMA((n, 7)),
                            pltpu.SemaphoreType.DMA((n, 7)), pltpu.SemaphoreType.DMA((n,)),
                            pltpu.SemaphoreType.DMA((2,))]),
        out_shape=[jax.ShapeDtypeStruct((s, NIN), BF16), jax.ShapeDtypeStruct((NDEV, D, SHARD), BF16)]
                  + [jax.ShapeDtypeStruct((NDEV,) + e.shape, e.dtype) for e in extras],
        compiler_params=_cp(("arbitrary", "arbitrary")))(order, h, w_shard, *extras)
    return outs[0], outs[1], outs[2:]


def proj_bwd(ht, dproj, wg, smalls, order, tt):
    s = dproj.shape[0]
    nk = s // tt
    n = len(smalls)

    def body(order_ref, ht_ref, dp_ref, w_ref, *rest):
        small_in = rest[:n]
        dh_ref, gw_ref, rwin_ref = rest[n:n + 3]
        small_out = rest[n + 3:2 * n + 3]
        acc, stage, send_sems, recv_sems, local_sems, stage_sems = rest[2 * n + 3:]
        t, k = pl.program_id(0), pl.program_id(1)
        me_xyc = _coords()
        me = _dev_index(me_xyc)
        peers = [_flip(me_xyc, f) for f in FLIPS]

        def exchange(a, kf, src_arr, dst_arr):
            pid = _dev_index(peers[kf])
            mk = lambda dst: pltpu.make_async_remote_copy(
                src_ref=src_arr.at[pid], dst_ref=dst, send_sem=send_sems.at[a, kf], recv_sem=recv_sems.at[a, kf],
                device_id=peers[kf], device_id_type=MESH)
            return mk(dst_arr.at[me]), mk(dst_arr.at[pid])

        small_pairs = [exchange(1 + a, kf, small_in[a], small_out[a]) for kf in range(7) for a in range(n)]
        small_own = [pltpu.make_async_copy(small_in[a].at[me], small_out[a].at[me], local_sems.at[1 + a])
                     for a in range(n)]
        win_pairs = [exchange(0, kf, gw_ref, rwin_ref) for kf in range(7)]
        win_own = pltpu.make_async_copy(gw_ref.at[me], rwin_ref.at[me], local_sems.at[0])

        def to_hbm(jj):
            slab = me if jj == 7 else _dev_index(peers[jj])
            return pltpu.make_async_copy(stage.at[jj % 2], gw_ref.at[slab], stage_sems.at[jj % 2])

        @pl.when((t == 0) & (k == 0))
        def _():
            for cp in small_own:
                cp.start()
            for send, _ in small_pairs:
                send.start()

        @pl.when(t < NDEV)
        def _():
            p = jnp.dot(ht_ref[...], dp_ref[...], preferred_element_type=F32)

            @pl.when(k == 0)
            def _():
                acc[...] = p

            @pl.when(k > 0)
            def _():
                acc[...] += p

        for jj in range(NDEV):
            @pl.when((t == jj) & (k == nk - 1))
            def _(jj=jj):
                stage[jj % 2] = acc[...].astype(BF16)
                to_hbm(jj).start()

            @pl.when((t == jj + 1) & (k == 1))
            def _(jj=jj):
                to_hbm(jj).wait()
                if jj < 7:
                    win_pairs[jj][0].start()
                else:
                    win_own.start()

        @pl.when(t >= NDEV)
        def _():
            p = lax.dot_general(dp_ref[...], w_ref[...], NT, preferred_element_type=F32)

            @pl.when(k == 0)
            def _():
                dh_ref[...] = p

            @pl.when(k > 0)
            def _():
                dh_ref[...] += p

        @pl.when((t == 2 * NDEV - 1) & (k == nk - 1))
        def _():
            for _, recv in win_pairs + small_pairs:
                recv.wait_recv()
            for send, _ in win_pairs + small_pairs:
                send.wait_send()
            win_own.wait()
            for cp in small_own:
                cp.wait()

    any_spec = pl.BlockSpec(memory_space=pl.ANY)
    first = lambda t: t < NDEV
    outs = pl.pallas_call(
        body, name="proj_bwd",
        grid_spec=pltpu.PrefetchScalarGridSpec(
            num_scalar_prefetch=1, grid=(2 * NDEV, nk),
            in_specs=[pl.BlockSpec((D, tt), lambda t, k, o: (0, jnp.where(first(t), k, nk - 1))),
                      pl.BlockSpec((tt, SHARD), lambda t, k, o: (jnp.where(first(t), k, t - NDEV),
                                                                 jnp.where(first(t), o[jnp.minimum(t, NDEV - 1)], k))),
                      pl.BlockSpec((None, D, SHARD), lambda t, k, o: (jnp.where(first(t), 0, k), 0, 0))]
                     + [any_spec] * n,
            out_specs=[pl.BlockSpec((tt, D), lambda t, k, o: (jnp.where(first(t), 0, t - NDEV), 0))]
                      + [any_spec] * (2 + n),
            scratch_shapes=[pltpu.VMEM((D, SHARD), F32), pltpu.VMEM((2, D, SHARD), BF16),
                            pltpu.SemaphoreType.DMA((1 + n, 7)), pltpu.SemaphoreType.DMA((1 + n, 7)),
                            pltpu.SemaphoreType.DMA((1 + n,)), pltpu.SemaphoreType.DMA((2,))]),
        out_shape=[jax.ShapeDtypeStruct((s, D), F32), jax.ShapeDtypeStruct((NDEV, D, SHARD), BF16),
                   jax.ShapeDtypeStruct((NDEV, D, SHARD), BF16)]
                  + [jax.ShapeDtypeStruct(a.shape, a.dtype) for a in smalls],
        compiler_params=_cp(("arbitrary", "arbitrary"), 56))(order, ht, dproj, wg, *smalls)
    return outs[0], outs[2], outs[3:]


def matmul_tn(a, b, name, tk):
    s, m = a.shape
    n = b.shape[1]
    nk = s // tk

    def body(a_ref, b_ref, o_ref, acc_ref):
        k = pl.program_id(0)
        p = lax.dot_general(a_ref[...], b_ref[...], TN, preferred_element_type=F32)

        @pl.when(k == 0)
        def _():
            acc_ref[...] = p

        @pl.when(k > 0)
        def _():
            acc_ref[...] += p

        @pl.when(k == nk - 1)
        def _():
            o_ref[...] = acc_ref[...].astype(BF16)

    return pl.pallas_call(
        body, name=name, grid=(nk,),
        in_specs=[pl.BlockSpec((tk, m), lambda k: (k, 0)), pl.BlockSpec((tk, n), lambda k: (k, 0))],
        out_specs=pl.BlockSpec((m, n), lambda k: (0, 0)),
        out_shape=jax.ShapeDtypeStruct((m, n), BF16),
        scratch_shapes=[pltpu.VMEM((m, n), F32)],
        compiler_params=_cp(("arbitrary",)))(a, b)


def _head_matrices():
    lane = lax.broadcasted_iota(jnp.int32, (CB, CB), 0)
    col = lax.broadcasted_iota(jnp.int32, (CB, CB), 1)
    same = (lane // HD == col // HD).astype(BF16)
    lane_c = lax.broadcasted_iota(jnp.int32, (CB, LANES), 0)
    col_c = lax.broadcasted_iota(jnp.int32, (CB, LANES), 1)
    total = (lane_c // HD == col_c).astype(BF16)
    pick = (lane_c == col_c * HD).astype(BF16)
    return same, total, pick


def _head_sum(x, m_ref):
    return jnp.dot(x.astype(BF16), m_ref[...], preferred_element_type=F32)


def _dot_hilo(x, m_ref):
    hi = x.astype(BF16)
    lo = (x - hi.astype(F32)).astype(BF16)
    return (jnp.dot(hi, m_ref[...], preferred_element_type=F32)
            + jnp.dot(lo, m_ref[...], preferred_element_type=F32))


def _to_residue_major(val, buf, out_ref, dil):
    rows = out_ref.shape[1]
    for k in range(val.shape[1] // LANES):
        lanes = slice(k * LANES, (k + 1) * LANES)
        buf[k] = val[:, lanes]
        for r in range(dil):
            out_ref[r, :, lanes] = buf.at[k][pl.ds(r, rows, stride=dil), :].astype(out_ref.dtype)


def _from_residue_major(ref, buf, dil):
    if dil == 1:
        return ref[0].astype(F32)
    rows = ref.shape[1]
    for k in range(CB // LANES):
        for r in range(dil):
            buf.at[k][pl.ds(r, rows, stride=dil), :] = ref[r, :, k * LANES:(k + 1) * LANES].astype(F32)
    return jnp.concatenate([buf[k] for k in range(CB // LANES)], axis=1)


def qkv_prep(proj, qw8, kw8, same, tm):
    s = proj.shape[0]
    items = []
    for g, d in enumerate(DILATIONS):
        items += [(g, "q", CB_Q + g, d), (g, "k", CB_K + g, d)] + ([(g, "v", CB_V + g, d)] if d > 1 else [])
    n = len(items)

    def body(*refs):
        ins, (qw_ref, kw_ref, same_ref), outs, buf = refs[:n], refs[n:n + 3], refs[n + 3:2 * n + 3], refs[-1]
        for idx, (_, kind, _, dil) in enumerate(items):
            val = ins[idx][...].astype(F32)
            if kind != "v":
                r = lax.rsqrt(_head_sum(val * val, same_ref) * (1.0 / HD) + EPS)
                val = val * r * (qw_ref if kind == "q" else kw_ref)[...]
            if dil == 1:
                outs[idx][0] = val.astype(BF16)
            else:
                _to_residue_major(val, buf, outs[idx], dil)

    full = lambda a: pl.BlockSpec(a.shape, lambda i: (0, 0))
    outs = pl.pallas_call(
        body, name="qkv_prep", grid=(s // tm,),
        in_specs=[pl.BlockSpec((tm, CB), lambda i, cb=cb: (i, cb)) for _, _, cb, _ in items]
                 + [full(qw8), full(kw8), full(same)],
        out_specs=[pl.BlockSpec((d, tm // d, CB), lambda i: (0, i, 0)) for _, _, _, d in items],
        out_shape=[jax.ShapeDtypeStruct((d, s // d, CB), BF16) for _, _, _, d in items],
        scratch_shapes=[pltpu.VMEM((CB // LANES, tm, LANES), F32)],
        compiler_params=_cp(("parallel",)))(*([proj] * n), qw8 * (HD ** -0.5), kw8, same)
    srcs = [[None, None, (proj, CB_V + g)] for g in range(len(DILATIONS))]
    for (g, kind, _, _), o in zip(items, outs):
        srcs[g]["qkv".index(kind)] = (o.reshape(s, CB), 0)
    return srcs


def stats_prep(da, lc, dc, g, dil, tm):
    s = da.shape[0]
    rows = tm // dil

    def body(da_ref, lc_ref, dc_ref, dap_ref, lcp_ref, dcp_ref, lt_ref, dt_ref, buf):
        if dil == 1:
            dap_ref[0] = da_ref[...]
        else:
            _to_residue_major(da_ref[...].astype(F32), buf, dap_ref, dil)
        for src, dst, dst_t in ((lc_ref, lcp_ref, lt_ref), (dc_ref, dcp_ref, dt_ref)):
            buf[0] = src[...]
            for r in range(dil):
                piece = buf.at[0][pl.ds(r, rows, stride=dil), :] if dil > 1 else buf[0]
                dst[r] = piece
                dst_t[r] = piece.T[0:NH, :]

    row = lambda w: pl.BlockSpec((tm, w), lambda i: (i, 0))
    rm = lambda w: pl.BlockSpec((dil, rows, w), lambda i: (0, i, 0))
    tr = pl.BlockSpec((dil, NH, rows), lambda i: (0, 0, i))
    length = s // dil
    dap, lcp, dcp, lt, dt = pl.pallas_call(
        body, name=f"stats_prep_g{g}", grid=(s // tm,),
        in_specs=[row(CB), row(LANES), row(LANES)],
        out_specs=[rm(CB), rm(LANES), rm(LANES), tr, tr],
        out_shape=[jax.ShapeDtypeStruct((dil, length, CB), BF16)]
                  + [jax.ShapeDtypeStruct((dil, length, LANES), F32)] * 2
                  + [jax.ShapeDtypeStruct((dil, NH, length), F32)] * 2,
        scratch_shapes=[pltpu.VMEM((CB // LANES, tm, LANES), F32)],
        compiler_params=_cp(("parallel",)))(da, lc, dc)
    return (dap.reshape(s, CB), lcp.reshape(s, LANES), dcp.reshape(s, LANES),
            lt.reshape(dil * NH, length), dt.reshape(dil * NH, length))


def qkv_grads_to_dproj(dproj, proj, grads, qw8, kw8, same, tm):
    s = dproj.shape[0]
    ni = s // tm
    flat = [(t.reshape(d, s // d, CB), d, kind, 3 * kind + g)
            for g, d in enumerate(DILATIONS) for kind, t in enumerate(grads[g])]
    nf = len(flat)
    nraw = 2 * len(DILATIONS)

    def body(*refs):
        dp_hbm, raws, ins = refs[nraw + nf + 4], refs[1:1 + nraw], refs[1 + nraw:1 + nraw + nf]
        qw_ref, kw_ref, same_ref = refs[1 + nraw + nf:4 + nraw + nf]
        gw_ref, stage, buf, sems = refs[5 + nraw + nf:]
        i = pl.program_id(0)
        slot = i % 2

        def slab(step, sl):
            return pltpu.make_async_copy(
                stage.at[sl], dp_hbm.at[pl.ds(pl.multiple_of(step * tm, tm), tm), pl.ds(CB_Q * CB, 9 * CB)],
                sems.at[sl])

        @pl.when(i == 0)
        def _():
            gw_ref[...] = jnp.zeros_like(gw_ref)

        @pl.when(i >= 2)
        def _():
            slab(i - 2, slot).wait()

        for ref, (_, d, kind, jj) in zip(ins, flat):
            cols = slice(jj * CB, (jj + 1) * CB)
            dn = _from_residue_major(ref, buf, d)
            if kind == 2:
                stage[slot, :, cols] = dn.astype(BF16)
                continue
            t = raws[jj][...].astype(F32)
            r = lax.rsqrt(_head_sum(t * t, same_ref) * (1.0 / HD) + EPS)
            xh = t * r
            gw_ref[kind:kind + 1, :] += jnp.sum(dn * xh, axis=0, keepdims=True)
            dxh = dn * (qw_ref if kind == 0 else kw_ref)[...]
            mean = _head_sum(dxh * xh, same_ref) * (1.0 / HD)
            stage[slot, :, cols] = (r * (dxh - xh * mean)).astype(BF16)
        slab(i, slot).start()

        @pl.when(i == ni - 1)
        def _():
            slab(i - 1, 1 - slot).wait()
            slab(i, slot).wait()

    full = lambda a: pl.BlockSpec(a.shape, lambda i: (0, 0))
    any_spec = pl.BlockSpec(memory_space=pl.ANY)
    return pl.pallas_call(
        body, name="qkv_grads_to_dproj", grid=(ni,),
        in_specs=[any_spec] + [pl.BlockSpec((tm, CB), lambda i, jb=jb: (i, CB_Q + jb)) for jb in range(nraw)]
                 + [pl.BlockSpec((d, tm // d, CB), lambda i: (0, i, 0)) for _, d, _, _ in flat]
                 + [full(qw8), full(kw8), full(same)],
        out_specs=[any_spec, pl.BlockSpec((8, CB), lambda i: (0, 0))],
        out_shape=[jax.ShapeDtypeStruct((s, NIN), BF16), jax.ShapeDtypeStruct((8, CB), F32)],
        input_output_aliases={0: 0},
        scratch_shapes=[pltpu.VMEM((2, tm, 9 * CB), BF16), pltpu.VMEM((CB // LANES, tm, LANES), F32),
                        pltpu.SemaphoreType.DMA((2,))],
        compiler_params=_cp(("arbitrary",)))(
            dproj, *([proj] * nraw), *[t for t, _, _, _ in flat], qw8, kw8, same)


def _lane_lo():
    return lax.broadcasted_iota(jnp.int32, (1, 2 * HD), 1) < HD


def _stack_heads(t, lo):
    zero = jnp.zeros_like(t)
    return jnp.concatenate([jnp.where(lo, t, zero), jnp.where(lo, zero, t)], axis=0)


def _masks(other_ok):
    qi = lax.broadcasted_iota(jnp.int32, (QB, QB), 0)
    kj = lax.broadcasted_iota(jnp.int32, (QB, QB), 1)
    return (kj >= qi) & other_ok, kj <= qi


SUB = 4


def _attn_specs(nb, dil):
    steps = nb // SUB
    main = lambda cb, w=CB: pl.BlockSpec((SUB * QB, w), lambda r, s: (r * steps + s, cb))
    prev = lambda cb: pl.BlockSpec((QB, CB), lambda r, s: (jnp.maximum(r * nb + SUB * s - 1, 0), cb))
    nxt = lambda cb: pl.BlockSpec((QB, CB), lambda r, s: (jnp.minimum(r * nb + SUB * (s + 1), dil * nb - 1), cb))
    return main, prev, nxt


def attn_fwd(q_src, k_src, v_src, g, dil):
    s = q_src[0].shape[0]
    nb = s // dil // QB
    main, prev, _ = _attn_specs(nb, dil)

    def body(q_ref, kp_ref, k_ref, vp_ref, v_ref, o_ref, l_ref, kbuf, vbuf):
        step = pl.program_id(1)
        kbuf[0:QB], kbuf[QB:] = kp_ref[...], k_ref[...]
        vbuf[0:QB], vbuf[QB:] = vp_ref[...], v_ref[...]
        lo = _lane_lo()

        def block(j, carry):
            r0 = pl.multiple_of(j * QB, QB)
            rows, krows = pl.ds(r0, QB), pl.ds(r0, 2 * QB)
            m_prev, m_cur = _masks(step * SUB + j > 0)
            mask = jnp.concatenate([m_prev, m_cur], axis=1)
            mask = jnp.concatenate([mask, mask], axis=0)
            for i in range(NH // 2):
                sl = slice(2 * HD * i, 2 * HD * (i + 1))
                qs, ks, vv = q_ref[rows, sl], kbuf[krows, sl], vbuf[krows, sl]
                sc = lax.dot_general(_stack_heads(qs, lo), ks, NT, preferred_element_type=F32)
                sc = jnp.where(mask, sc, NEG)
                mx = jnp.max(sc, axis=-1, keepdims=True)
                p = jnp.exp(sc - mx)
                den = jnp.sum(p, axis=-1, keepdims=True)
                o = jnp.dot(p.astype(BF16), vv, preferred_element_type=F32) * (1.0 / den)
                lse = jnp.broadcast_to(mx + jnp.log(den), (2 * QB, 2 * HD))
                o_ref[rows, sl] = jnp.where(lo, o[:QB], o[QB:])
                l_ref[rows, sl] = jnp.where(lo, lse[:QB], lse[QB:])
            return carry

        lax.fori_loop(0, SUB, block, 0, unroll=True)

    out = jax.ShapeDtypeStruct((s, CB), F32)
    return pl.pallas_call(
        body, name=f"attn_fwd_g{g}", grid=(dil, nb // SUB),
        in_specs=[main(q_src[1]), prev(k_src[1]), main(k_src[1]), prev(v_src[1]), main(v_src[1])],
        out_specs=[main(0)] * 2, out_shape=[out, out],
        scratch_shapes=[pltpu.VMEM(((SUB + 1) * QB, CB), BF16)] * 2,
        compiler_params=_cp(("parallel", "parallel")))(q_src[0], k_src[0], k_src[0], v_src[0], v_src[0])


def attn_bwd_q(q_src, k_src, v_src, da, lc, dc, g, dil):
    s = q_src[0].shape[0]
    nb = s // dil // QB
    main, prev, _ = _attn_specs(nb, dil)

    def body(q_ref, kp_ref, k_ref, vp_ref, v_ref, da_ref, l_ref, d_ref, dq_ref, kbuf, vbuf):
        step = pl.program_id(1)
        kbuf[0:QB], kbuf[QB:] = kp_ref[...], k_ref[...]
        vbuf[0:QB], vbuf[QB:] = vp_ref[...], v_ref[...]
        lo = _lane_lo()

        def block(j, carry):
            r0 = pl.multiple_of(j * QB, QB)
            rows, krows = pl.ds(r0, QB), pl.ds(r0, 2 * QB)
            m_prev, m_cur = _masks(step * SUB + j > 0)
            mask = jnp.concatenate([m_prev, m_cur], axis=1)
            mask = jnp.concatenate([mask, mask], axis=0)
            lcols, dcols = l_ref[rows, :], d_ref[rows, :]
            for i in range(NH // 2):
                sl = slice(2 * HD * i, 2 * HD * (i + 1))
                qs, ks, vv, da2 = q_ref[rows, sl], kbuf[krows, sl], vbuf[krows, sl], da_ref[rows, sl]
                pair = lambda t: jnp.concatenate([t[:, 2 * i:2 * i + 1], t[:, 2 * i + 1:2 * i + 2]], axis=0)
                sc = lax.dot_general(_stack_heads(qs, lo), ks, NT, preferred_element_type=F32)
                sc = jnp.where(mask, sc, NEG)
                p = jnp.exp(sc - pair(lcols))
                dp = lax.dot_general(_stack_heads(da2, lo), vv, NT, preferred_element_type=F32)
                ds = p * (dp - pair(dcols))
                dq = jnp.dot(ds.astype(BF16), ks, preferred_element_type=F32)
                dq_ref[rows, sl] = (jnp.where(lo, dq[:QB], dq[QB:]) * (HD ** -0.5)).astype(BF16)
            return carry

        lax.fori_loop(0, SUB, block, 0, unroll=True)

    return pl.pallas_call(
        body, name=f"attn_bwd_q_g{g}", grid=(dil, nb // SUB),
        in_specs=[main(q_src[1]), prev(k_src[1]), main(k_src[1]), prev(v_src[1]), main(v_src[1]),
                  main(0), main(0, LANES), main(0, LANES)],
        out_specs=main(0), out_shape=jax.ShapeDtypeStruct((s, CB), BF16),
        scratch_shapes=[pltpu.VMEM(((SUB + 1) * QB, CB), BF16)] * 2,
        compiler_params=_cp(("parallel", "parallel")))(
            q_src[0], k_src[0], k_src[0], v_src[0], v_src[0], da, lc, dc)


def attn_bwd_kv(q_src, k_src, v_src, da, lt, dt, g, dil):
    s = q_src[0].shape[0]
    nb = s // dil // QB
    main, _, nxt = _attn_specs(nb, dil)

    def body(k_ref, v_ref, q_ref, qn_ref, da_ref, dan_ref, l_ref, ln_ref, d_ref, dn_ref, dk_ref, dv_ref,
             qbuf, dabuf, lbuf, dbuf):
        step = pl.program_id(1)
        qbuf[0:SUB * QB], qbuf[SUB * QB:] = q_ref[...], qn_ref[...]
        dabuf[0:SUB * QB], dabuf[SUB * QB:] = da_ref[...], dan_ref[...]
        for c in range(SUB):
            lbuf[c], dbuf[c] = l_ref[:, c * QB:(c + 1) * QB], d_ref[:, c * QB:(c + 1) * QB]
        lbuf[SUB], dbuf[SUB] = ln_ref[...], dn_ref[...]
        lo = _lane_lo()
        kj = lax.broadcasted_iota(jnp.int32, (QB, QB), 0)
        qi = lax.broadcasted_iota(jnp.int32, (QB, QB), 1)

        def block(j, carry):
            r0 = pl.multiple_of(j * QB, QB)
            rows, qrows = pl.ds(r0, QB), pl.ds(r0, 2 * QB)
            mask = jnp.concatenate([kj <= qi, (kj >= qi) & (step * SUB + j < nb - 1)], axis=1)
            mask = jnp.concatenate([mask, mask], axis=1)
            lrow = jnp.concatenate([lbuf[j], lbuf[j + 1]], axis=1)
            drow = jnp.concatenate([dbuf[j], dbuf[j + 1]], axis=1)
            for i in range(NH // 2):
                sl = slice(2 * HD * i, 2 * HD * (i + 1))
                q2, da2 = _stack_heads(qbuf[qrows, sl], lo), _stack_heads(dabuf[qrows, sl], lo)
                ks, vv = k_ref[rows, sl], v_ref[rows, sl]
                pair = lambda t: jnp.concatenate([t[2 * i:2 * i + 1, :], t[2 * i + 1:2 * i + 2, :]], axis=1)
                sc = lax.dot_general(ks, q2, NT, preferred_element_type=F32)
                sc = jnp.where(mask, sc, NEG)
                p = jnp.exp(sc - pair(lrow))
                dp = lax.dot_general(vv, da2, NT, preferred_element_type=F32)
                ds = p * (dp - pair(drow))
                dv_ref[rows, sl] = jnp.dot(p.astype(BF16), da2, preferred_element_type=F32).astype(BF16)
                dk_ref[rows, sl] = jnp.dot(ds.astype(BF16), q2, preferred_element_type=F32).astype(BF16)
            return carry

        lax.fori_loop(0, SUB, block, 0, unroll=True)

    steps = nb // SUB
    t_main = pl.BlockSpec((NH, SUB * QB), lambda r, s: (r, s))
    t_nxt = pl.BlockSpec((NH, QB), lambda r, s: (r, jnp.minimum(SUB * (s + 1), nb - 1)))
    out = jax.ShapeDtypeStruct((s, CB), BF16)
    return pl.pallas_call(
        body, name=f"attn_bwd_kv_g{g}", grid=(dil, steps),
        in_specs=[main(k_src[1]), main(v_src[1]), main(q_src[1]), nxt(q_src[1]),
                  main(0), nxt(0), t_main, t_nxt, t_main, t_nxt],
        out_specs=[main(0), main(0)], out_shape=[out, out],
        scratch_shapes=[pltpu.VMEM(((SUB + 1) * QB, CB), BF16)] * 2 + [pltpu.VMEM((SUB + 1, NH, QB), F32)] * 2,
        compiler_params=_cp(("parallel", "parallel")))(
            k_src[0], v_src[0], q_src[0], q_src[0], da, da, lt, lt, dt, dt)


def _conv_taps(u, u_prev, first):
    tm = u.shape[0]
    row = lax.broadcasted_iota(jnp.int32, (tm, 1), 0)
    up = jnp.where(first, 0.0, u_prev)
    u1 = jnp.where(row == 0, up[HALO - 1:HALO, :], pltpu.roll(u, 1, 0))
    u2 = jnp.where(row == 0, up[HALO - 2:HALO - 1, :],
                   jnp.where(row == 1, up[HALO - 1:HALO, :], pltpu.roll(u, 2, 0)))
    return u1, u2


def mid_fwd(proj, o_g, lse_g, conv_w, pick, tm):
    s = proj.shape[0]
    hb = tm // HALO

    def body(ba_ref, ca_ref, xa_ref, za_ref, cah_ref, xah_ref, zb_ref,
             o0, o1, o2, l0, l1, l2, w_ref, pick_ref, ya_ref, yb_ref, at_ref, lc_ref, buf_o, buf_l):
        first = pl.program_id(0) == 0
        u = ca_ref[...].astype(F32) * xa_ref[...].astype(F32)
        u1, u2 = _conv_taps(u, cah_ref[...].astype(F32) * xah_ref[...].astype(F32), first)
        conv = w_ref[0:1, :] * u2 + w_ref[1:2, :] * u1 + w_ref[2:3, :] * u
        ya_ref[...] = (ba_ref[...].astype(F32) * conv * _silu(za_ref[...].astype(F32))).astype(BF16)
        ls = [_from_residue_major(l, buf_l.at[g], d) for g, (l, d) in enumerate(zip((l0, l1, l2), DILATIONS))]
        mx = jnp.maximum(jnp.maximum(ls[0], ls[1]), ls[2])
        es = [jnp.exp(l - mx) for l in ls]
        den = es[0] + es[1] + es[2]
        num = jnp.zeros_like(den)
        for e, o, d in zip(es, (o0, o1, o2), DILATIONS):
            num = num + e * _from_residue_major(o, buf_o, d)
        attn = num / den
        at_ref[...] = attn
        lc_ref[...] = _dot_hilo(mx + jnp.log(den), pick_ref)
        yb_ref[...] = (attn * _silu(zb_ref[...].astype(F32))).astype(BF16)

    col = lambda j: pl.BlockSpec((tm, D), lambda i: (i, j))
    halo = lambda j: pl.BlockSpec((HALO, D), lambda i: (jnp.maximum(i * hb - 1, 0), j))
    loc = lambda w: pl.BlockSpec((tm, w), lambda i: (i, 0))
    rm = [pl.BlockSpec((d, tm // d, CB), lambda i: (0, i, 0)) for d in DILATIONS]
    rm_view = lambda ts: [t.reshape(d, s // d, CB) for t, d in zip(ts, DILATIONS)]
    return pl.pallas_call(
        body, name="mid_fwd", grid=(s // tm,),
        in_specs=[col(0), col(1), col(2), col(3), halo(1), halo(2),
                  pl.BlockSpec((tm, CB), lambda i: (i, CB_ZB))] + rm + rm
                 + [pl.BlockSpec((3, D), lambda i: (0, 0)), pl.BlockSpec(pick.shape, lambda i: (0, 0))],
        out_specs=[loc(D), loc(CB), loc(CB), loc(LANES)],
        out_shape=[jax.ShapeDtypeStruct((s, D), BF16), jax.ShapeDtypeStruct((s, CB), BF16),
                   jax.ShapeDtypeStruct((s, CB), F32), jax.ShapeDtypeStruct((s, LANES), F32)],
        scratch_shapes=[pltpu.VMEM((CB // LANES, tm, LANES), F32), pltpu.VMEM((3, CB // LANES, tm, LANES), F32)],
        compiler_params=_cp(("parallel",)))(
            proj, proj, proj, proj, proj, proj, proj, *rm_view(o_g), *rm_view(lse_g), conv_w, pick)


def mid_bwd(dproj, proj, dya, conv_w, tm):
    s = proj.shape[0]
    hb = tm // HALO
    nblk = s // tm
    last_h = s // HALO - 1

    def body(_, ba_ref, ca_ref, xa_ref, za_ref, cah_ref, xah_ref, ban_ref, zan_ref, dy_ref, dyn_ref, w_ref,
             o_ref, gw_ref):
        i = pl.program_id(0)
        ba, ca, xa, za = (t[...].astype(F32) for t in (ba_ref, ca_ref, xa_ref, za_ref))
        u = ca * xa
        u1, u2 = _conv_taps(u, cah_ref[...].astype(F32) * xah_ref[...].astype(F32), i == 0)
        w0, w1, w2 = w_ref[0:1, :], w_ref[1:2, :], w_ref[2:3, :]
        conv = w0 * u2 + w1 * u1 + w2 * u
        sg = jax.nn.sigmoid(za)
        sz = za * sg
        dy = dy_ref[...].astype(F32)
        dconv = dy * ba * sz
        dcn = dyn_ref[...].astype(F32) * ban_ref[...].astype(F32) * _silu(zan_ref[...].astype(F32))
        dcn = jnp.where(i == nblk - 1, 0.0, dcn)
        row = lax.broadcasted_iota(jnp.int32, (tm, 1), 0)
        d1 = jnp.where(row == tm - 1, dcn[0:1, :], pltpu.roll(dconv, tm - 1, 0))
        d2 = jnp.where(row == tm - 2, dcn[0:1, :],
                       jnp.where(row == tm - 1, dcn[1:2, :], pltpu.roll(dconv, tm - 2, 0)))
        du = w2 * dconv + w1 * d1 + w0 * d2
        o_ref[:, 0:D] = (dy * conv * sz).astype(BF16)
        o_ref[:, D:2 * D] = (du * xa).astype(BF16)
        o_ref[:, 2 * D:3 * D] = (du * ca).astype(BF16)
        o_ref[:, 3 * D:4 * D] = (dy * ba * conv * (sg * (1.0 + za * (1.0 - sg)))).astype(BF16)

        @pl.when(i == 0)
        def _():
            gw_ref[...] = jnp.zeros_like(gw_ref)

        gw_ref[0:1, :] += jnp.sum(dconv * u2, axis=0, keepdims=True)
        gw_ref[1:2, :] += jnp.sum(dconv * u1, axis=0, keepdims=True)
        gw_ref[2:3, :] += jnp.sum(dconv * u, axis=0, keepdims=True)

    col = lambda j: pl.BlockSpec((tm, D), lambda i: (i, j))
    halo_prev = lambda j: pl.BlockSpec((HALO, D), lambda i: (jnp.maximum(i * hb - 1, 0), j))
    halo_next = lambda j: pl.BlockSpec((HALO, D), lambda i: (jnp.minimum((i + 1) * hb, last_h), j))
    return pl.pallas_call(
        body, name="mid_bwd", grid=(nblk,),
        in_specs=[pl.BlockSpec(memory_space=pl.ANY), col(0), col(1), col(2), col(3),
                  halo_prev(1), halo_prev(2), halo_next(0), halo_next(3),
                  pl.BlockSpec((tm, D), lambda i: (i, 0)), halo_next(0),
                  pl.BlockSpec((3, D), lambda i: (0, 0))],
        out_specs=[pl.BlockSpec((tm, 4 * D), lambda i: (i, 0)), pl.BlockSpec((8, D), lambda i: (0, 0))],
        out_shape=[jax.ShapeDtypeStruct((s, NIN), BF16), jax.ShapeDtypeStruct((8, D), F32)],
        input_output_aliases={0: 0},
        compiler_params=_cp(("arbitrary",)))(dproj, proj, proj, proj, proj, proj, proj, proj, proj, dya, dya, conv_w)


def tail(proj, ya, yb, attn, x, target, gate, pa_w, pb_w, wo_w, total, tm):
    s = proj.shape[0]
    ni = s // tm
    ncol = NIN - CB_ZB * CB

    def body(ya_ref, yb_ref, ga_ref, gb_ref, zb_ref, at_ref, x_ref, t_ref, gate_ref, pa_ref, pb_ref, wo_ref,
             tot_ref, dp_hbm, dy_ref, dya_ref, da_ref, dc_ref, mg_ref, do_ref, dpa_ref, dpb_ref, st_ref,
             stage, sems):
        i = pl.program_id(0)
        slot = i % 2

        def slab(step, sl):
            return pltpu.make_async_copy(
                stage.at[sl], dp_hbm.at[pl.ds(pl.multiple_of(step * tm, tm), tm), pl.ds(CB_ZB * CB, ncol)],
                sems.at[sl])

        @pl.when(i == 0)
        def _():
            st_ref[...] = jnp.zeros_like(st_ref)

        @pl.when(i >= 2)
        def _():
            slab(i - 2, slot).wait()

        gate_v = gate_ref[...]
        pa = jnp.dot(ya_ref[...], pa_ref[...], preferred_element_type=F32)
        pb = jnp.dot(yb_ref[...], pb_ref[...], preferred_element_type=F32)
        sa = jax.nn.sigmoid(ga_ref[...].astype(F32))
        sb = jax.nn.sigmoid(gb_ref[...].astype(F32))
        merged = (sa * pa + sb * pb).astype(BF16)
        mg_ref[...] = merged
        out = jnp.dot(merged, wo_ref[...], preferred_element_type=F32)
        err = x_ref[...] + gate_v * out - t_ref[...]
        dy = err * (1.0 / D)
        dy_ref[...] = dy
        st_ref[0:1, :] += jnp.sum(dy * out, axis=0, keepdims=True)
        st_ref[1:2, :] += jnp.sum(err * err, axis=0, keepdims=True)
        dout = (gate_v * dy).astype(BF16)
        do_ref[...] = dout
        dmg = lax.dot_general(dout, wo_ref[...], NT, preferred_element_type=F32)
        dpa = (dmg * sa).astype(BF16)
        dpb = (dmg * sb).astype(BF16)
        dpa_ref[...] = dpa
        dpb_ref[...] = dpb
        stage[slot, :, CB:CB + D] = (dmg * pa * sa * (1.0 - sa)).astype(BF16)
        stage[slot, :, CB + D:] = (dmg * pb * sb * (1.0 - sb)).astype(BF16)
        dya_ref[...] = lax.dot_general(dpa, pa_ref[...], NT, preferred_element_type=F32).astype(BF16)
        dyb = lax.dot_general(dpb, pb_ref[...], NT, preferred_element_type=F32)
        zb = zb_ref[...].astype(F32)
        sg = jax.nn.sigmoid(zb)
        attn_v = at_ref[...]
        dattn = dyb * (zb * sg)
        da_ref[...] = dattn.astype(BF16)
        stage[slot, :, 0:CB] = (dyb * attn_v * (sg * (1.0 + zb * (1.0 - sg)))).astype(BF16)
        dc_ref[...] = _dot_hilo(dattn * attn_v, tot_ref)

        slab(i, slot).start()

        @pl.when(i == ni - 1)
        def _():
            slab(i - 1, 1 - slot).wait()
            slab(i, slot).wait()

    row = lambda w: pl.BlockSpec((tm, w), lambda i: (i, 0))
    pcol = lambda w, jb: pl.BlockSpec((tm, w), lambda i: (i, jb))
    full = lambda a: pl.BlockSpec(a.shape, lambda i: (0, 0))
    return pl.pallas_call(
        body, name="tail", grid=(ni,),
        in_specs=[row(D), row(CB), pcol(D, 9), pcol(D, 10), pcol(CB, CB_ZB), row(CB), row(D), row(D),
                  pl.BlockSpec((1, D), lambda i: (0, 0)), full(pa_w), full(pb_w), full(wo_w), full(total)],
        out_specs=[pl.BlockSpec(memory_space=pl.ANY),
                   row(D), row(D), row(CB), row(LANES), row(D), row(D), row(D), row(D),
                   pl.BlockSpec((8, D), lambda i: (0, 0))],
        out_shape=[jax.ShapeDtypeStruct((s, NIN), BF16), jax.ShapeDtypeStruct((s, D), F32),
                   jax.ShapeDtypeStruct((s, D), BF16), jax.ShapeDtypeStruct((s, CB), BF16),
                   jax.ShapeDtypeStruct((s, LANES), F32)] + [jax.ShapeDtypeStruct((s, D), BF16)] * 4
                  + [jax.ShapeDtypeStruct((8, D), F32)],
        scratch_shapes=[pltpu.VMEM((2, tm, ncol), BF16), pltpu.SemaphoreType.DMA((2,))],
        compiler_params=_cp(("arbitrary",), 56))(
            ya, yb, proj, proj, proj, attn, x, target, gate, pa_w, pb_w, wo_w, total)


def _local_step(x, target, shift, scale, gate, norm_w, conv_w, qw, kw, w_shard, small_shards, me_xyc):
    qw8, kw8 = jnp.tile(qw, (1, NH)), jnp.tile(kw, (1, NH))
    same, total, pick = _head_matrices()
    h, ht = norm_fwd(x, norm_w, scale, shift, 512)
    proj, wg, (pa_g, pb_g, wo_g) = proj_fwd_gather(h, w_shard, small_shards, gather_order(me_xyc), 1024)
    pa_w, wo_w = pa_g.reshape(D, D), wo_g.reshape(D, D)
    pb_w = pb_g.transpose(1, 0, 2).reshape(CB, D)
    srcs = qkv_prep(proj, qw8, kw8, same, 512)
    o_g, lse_g = zip(*[attn_fwd(*srcs[g], g, d) for g, d in enumerate(DILATIONS)])
    ya, yb, attn, lc = mid_fwd(proj, o_g, lse_g, conv_w, pick, 512)
    dproj, dy, dya, da, dc, merged, dout, dpa, dpb, st_tail = tail(
        proj, ya, yb, attn, x, target, gate, pa_w, pb_w, wo_w, total, 256)
    g_wo = matmul_tn(merged, dout, "grad_w_out", 1024)
    g_pa = matmul_tn(ya, dpa, "grad_w_br_conv", 1024)
    g_pb = matmul_tn(yb, dpb, "grad_w_br_attn", 1024)
    dproj, st_conv = mid_bwd(dproj, proj, dya, conv_w, 512)
    grads = []
    for g, d in enumerate(DILATIONS):
        da_p, lc_p, dc_p, lt, dt = stats_prep(da, lc, dc, g, d, 2048)
        dq = attn_bwd_q(*srcs[g], da_p, lc_p, dc_p, g, d)
        dk, dv = attn_bwd_kv(*srcs[g], da_p, lt, dt, g, d)
        grads.append((dq, dk, dv))
    dproj, gw_qk = qkv_grads_to_dproj(dproj, proj, grads, qw8, kw8, same, 512)
    slabs = [g_pa.reshape(NDEV, 128, D), g_pb.reshape(CB, NDEV, 128).transpose(1, 0, 2), g_wo.reshape(NDEV, 128, D)]
    dh, r_win, (r_pa, r_pb, r_wo) = proj_bwd(ht, dproj, wg, slabs, scatter_order(me_xyc), 1024)
    grad_x, st_norm = norm_bwd(dh, x, dy, norm_w, scale, 512)
    dmod = jnp.concatenate([st_norm[0:1], st_norm[1:2], st_tail[0:1]], axis=1)
    loss_part = (0.5 / D) * jnp.sum(st_tail[1])
    gw_heads = gw_qk[0:2].reshape(2, NH, HD).sum(axis=1)
    small = dict(dmod=dmod, norm_w=st_norm[2:3], conv_w=st_conv[0:3],
                 q_norm_w=gw_heads[0:1], k_norm_w=gw_heads[1:2], loss=loss_part)
    return grad_x, small, (r_win, r_pa, r_pb, r_wo)


def kernel(x, c, w_ada, b_ada, norm_w, w_in, conv_w, q_norm_w, k_norm_w, w_br_conv, w_br_attn, w_out, loss_target, m_w_ada, m_b_ada, m_norm_w, m_w_in, m_conv_w, m_q_norm_w, m_k_norm_w, m_w_br_conv, m_w_br_attn, m_w_out, v_w_ada, v_b_ada, v_norm_w, v_w_in, v_conv_w, v_q_norm_w, v_k_norm_w, v_w_br_conv, v_w_br_attn, v_w_out):
    me_xyc = (lax.axis_index("x"), lax.axis_index("y"), lax.axis_index("c"))
    me = _dev_index(me_xyc)
    ncol = w_ada.shape[2]

    conv_pad = jnp.zeros((8, 128), F32).at[0:3].set(conv_w[0])
    c_all, conv_all = all_gather([c, conv_pad], "gather_cond")
    conv_full = conv_all[:, 0:3].transpose(1, 0, 2).reshape(3, D)
    c_all = c_all.reshape(NDEV, D)

    b_cols = lax.dynamic_slice(b_ada, (0, me * ncol), (1, ncol))
    mod_cols = ada_fwd(c_all, w_ada[0], b_cols)
    (mod_all,) = all_gather([mod_cols], "gather_mod")
    mod = lax.dynamic_index_in_dim(mod_all, me, axis=1, keepdims=False).reshape(1, 3 * D)
    shift, scale, gate = mod[:, 0:D], mod[:, D:2 * D], mod[:, 2 * D:3 * D]

    grad_x, small, (r_win, r_pa, r_pb, r_wo) = _local_step(
        x[0], loss_target[0], shift, scale, gate, norm_w, conv_full, q_norm_w, k_norm_w,
        w_in[0].astype(BF16), [w_br_conv[0].astype(BF16), w_br_attn[0].astype(BF16), w_out[0].astype(BF16)], me_xyc)

    packed = jnp.concatenate(
        [small["dmod"], small["norm_w"], small["conv_w"].reshape(1, 3 * D), small["q_norm_w"], small["k_norm_w"],
         jnp.full((1, 128), small["loss"], F32)], axis=1)
    (packed_all,) = all_gather([packed], "gather_small")
    tot = sum_parts(packed_all)
    loss = tot[0, 7 * D + 2 * HD]
    dmod_all = packed_all[:, 0, 0:3 * D]
    g_b_ada = tot[:, 0:3 * D]
    g_norm_w = tot[:, 3 * D:4 * D]
    g_conv = lax.dynamic_slice(tot[:, 4 * D:7 * D].reshape(3, D), (0, me * 128), (3, 128))
    g_qn = tot[:, 7 * D:7 * D + HD]
    g_kn = tot[:, 7 * D + HD:7 * D + 2 * HD]
    g_w_ada = ada_bwd(c_all.T, lax.dynamic_slice(dmod_all, (0, me * ncol), (NDEV, ncol)))

    def upd(parts, w, m, v, name, rows):
        shape = w.shape
        w2, m2, v2 = (t.reshape(shape[-2:]) for t in (w, m, v))
        return [t.reshape(shape) for t in adamw(parts, w2, m2, v2, name, rows)]

    res = {
        "w_ada": upd(g_w_ada[None], w_ada, m_w_ada, v_w_ada, "adamw_w_ada", 256),
        "b_ada": upd(g_b_ada[None], b_ada, m_b_ada, v_b_ada, "adamw_b_ada", 1),
        "norm_w": upd(g_norm_w[None], norm_w, m_norm_w, v_norm_w, "adamw_norm_w", 1),
        "w_in": upd(r_win, w_in, m_w_in, v_w_in, "adamw_w_in", 128),
        "conv_w": upd(g_conv[None], conv_w, m_conv_w, v_conv_w, "adamw_conv_w", 3),
        "q_norm_w": upd(g_qn[None], q_norm_w, m_q_norm_w, v_q_norm_w, "adamw_q_norm_w", 1),
        "k_norm_w": upd(g_kn[None], k_norm_w, m_k_norm_w, v_k_norm_w, "adamw_k_norm_w", 1),
        "w_br_conv": upd(r_pa, w_br_conv, m_w_br_conv, v_w_br_conv, "adamw_w_br_conv", 128),
        "w_br_attn": upd(r_pb, w_br_attn, m_w_br_attn, v_w_br_attn, "adamw_w_br_attn", 512),
        "w_out": upd(r_wo, w_out, m_w_out, v_w_out, "adamw_w_out", 128),
    }
    names = ["w_ada", "b_ada", "norm_w", "w_in", "conv_w", "q_norm_w", "k_norm_w", "w_br_conv", "w_br_attn", "w_out"]
    return (loss, grad_x[None], *[res[n][0] for n in names], *[res[n][1] for n in names],
            *[res[n][2] for n in names], *[res[n][3] for n in names])
```

```python
import jax
import jax.numpy as jnp
from jax import lax
from jax.experimental import pallas as pl
from jax.experimental.pallas import tpu as pltpu

F32, BF16 = jnp.float32, jnp.bfloat16
D = 1024
NIN = 11264
NDEV = 8
SHARD = NIN // NDEV
HD = 64
NH = 8
QB = 128
CB = 512
CB_Q, CB_K, CB_V, CB_ZB = 8, 11, 14, 17
DILATIONS = (1, 4, 16)
EPS = 1e-6
NEG = -1e30
HALO = 16
ROWS = 32
LANES = 128
MESH = pl.DeviceIdType.MESH

ADAM_LR, ADAM_B1, ADAM_B2, ADAM_EPS, ADAM_WD, ADAM_STEP = 0.001, 0.9, 0.999, 1e-08, 0.01, 10

NT = (((1,), (1,)), ((), ()))
TN = (((0,), (0,)), ((), ()))


def _cp(sem, vmem_mb=48):
    return pltpu.CompilerParams(dimension_semantics=sem, vmem_limit_bytes=vmem_mb << 20)


def _silu(z):
    return z * jax.nn.sigmoid(z)


def _coords():
    return lax.axis_index("x"), lax.axis_index("y"), lax.axis_index("c")


FLIPS = [(fx, fy, fc) for fx in (0, 1) for fy in (0, 1) for fc in (0, 1)][1:]


def all_gather(arrs, name):
    n = len(arrs)

    def body(*refs):
        ins, outs = refs[:n], refs[n:2 * n]
        send_sems, recv_sems, local_sems = refs[2 * n:]
        me_xyc = _coords()
        me = _dev_index(me_xyc)
        peers = [_flip(me_xyc, f) for f in FLIPS]

        def copy(a, k, block):
            return pltpu.make_async_remote_copy(
                src_ref=ins[a], dst_ref=outs[a].at[block], send_sem=send_sems.at[a, k], recv_sem=recv_sems.at[a, k],
                device_id=peers[k], device_id_type=MESH)

        mine = [pltpu.make_async_copy(ins[a], outs[a].at[me], local_sems.at[a]) for a in range(n)]
        sends = [copy(a, k, me) for k in range(7) for a in range(n)]
        for cp in mine + sends:
            cp.start()
        for k in range(7):
            for a in range(n):
                copy(a, k, _dev_index(peers[k])).wait_recv()
        for cp in sends:
            cp.wait_send()
        for cp in mine:
            cp.wait()

    any_spec = pl.BlockSpec(memory_space=pl.ANY)
    return pl.pallas_call(
        body, name=name,
        out_shape=[jax.ShapeDtypeStruct((NDEV,) + a.shape, a.dtype) for a in arrs],
        in_specs=[any_spec] * n, out_specs=[any_spec] * n,
        scratch_shapes=[pltpu.SemaphoreType.DMA((n, 7)), pltpu.SemaphoreType.DMA((n, 7)),
                        pltpu.SemaphoreType.DMA((n,))],
    )(*arrs)


def _flip(dev, f):
    return tuple(1 - v if b else v for v, b in zip(dev, f))


def _dev_index(dev):
    return 4 * dev[0] + 2 * dev[1] + dev[2]


def _chip_order(x, y, c):
    xor = lambda a, b: a + b - 2 * a * b
    return [(xor(x, 1 - c), xor(y, c)), (xor(x, c), xor(y, 1 - c)), (1 - x, 1 - y)]


def gather_order(me_xyc):
    x, y, c = me_xyc
    chips = _chip_order(x, y, c)
    devs = [(x, y, c), (x, y, 1 - c), (*chips[0], c), (*chips[1], c),
            (*chips[1], 1 - c), (*chips[0], 1 - c), (*chips[2], c), (*chips[2], 1 - c)]
    return jnp.stack([_dev_index(d) for d in devs]).astype(jnp.int32)


def scatter_order(me_xyc):
    devs = [_flip(me_xyc, f) for f in FLIPS] + [me_xyc]
    return jnp.stack([_dev_index(d) for d in devs]).astype(jnp.int32)


def ada_fwd(c_all, w_ada, b_cols):
    def body(c_ref, w_ref, b_ref, o_ref):
        a = _silu(c_ref[...]).astype(BF16)
        o_ref[...] = jnp.dot(a, w_ref[...].astype(BF16), preferred_element_type=F32) + b_ref[...]

    return pl.pallas_call(body, name="ada_fwd",
                          out_shape=jax.ShapeDtypeStruct((NDEV, w_ada.shape[1]), F32))(c_all, w_ada, b_cols)


def ada_bwd(c_all_t, dmod_cols):
    def body(c_ref, d_ref, o_ref):
        at = _silu(c_ref[...])
        acc = at[:, 0:1] * d_ref[0:1, :]
        for b in range(1, NDEV):
            acc = acc + at[:, b:b + 1] * d_ref[b:b + 1, :]
        o_ref[...] = acc

    return pl.pallas_call(body, name="ada_bwd",
                          out_shape=jax.ShapeDtypeStruct((D, dmod_cols.shape[1]), F32))(c_all_t, dmod_cols)


def sum_parts(parts):
    def body(p_ref, o_ref):
        acc = p_ref[0]
        for b in range(1, NDEV):
            acc = acc + p_ref[b]
        o_ref[...] = acc

    return pl.pallas_call(body, name="sum_parts",
                          out_shape=jax.ShapeDtypeStruct(parts.shape[1:], F32))(parts)


def adamw(parts, w, m, v, name, rows):
    n, r, ccols = parts.shape

    def body(p_ref, w_ref, m_ref, v_ref, g_ref, d_ref, nm_ref, nv_ref):
        g = p_ref[0].astype(F32)
        for b in range(1, n):
            g = g + p_ref[b].astype(F32)
        nm = ADAM_B1 * m_ref[...] + (1.0 - ADAM_B1) * g
        nv = ADAM_B2 * v_ref[...] + (1.0 - ADAM_B2) * (g * g)
        g_ref[...] = g
        nm_ref[...] = nm
        nv_ref[...] = nv
        m_hat = nm / (1.0 - ADAM_B1 ** ADAM_STEP)
        v_hat = nv / (1.0 - ADAM_B2 ** ADAM_STEP)
        d_ref[...] = -ADAM_LR * (m_hat / (jnp.sqrt(v_hat) + ADAM_EPS) + ADAM_WD * w_ref[...])

    blk = pl.BlockSpec((rows, ccols), lambda i: (i, 0))
    out = jax.ShapeDtypeStruct((r, ccols), F32)
    return pl.pallas_call(
        body, name=name, grid=(r // rows,),
        in_specs=[pl.BlockSpec((n, rows, ccols), lambda i: (0, i, 0)), blk, blk, blk],
        out_specs=[blk] * 4, out_shape=[out] * 4, compiler_params=_cp(("parallel",)))(parts, w, m, v)


def norm_fwd(x, nw, scale, shift, tm):
    s = x.shape[0]

    def body(x_ref, nw_ref, sc_ref, sh_ref, h_ref, ht_ref):
        xf = x_ref[...]
        r = lax.rsqrt(jnp.mean(xf * xf, axis=-1, keepdims=True) + EPS)
        h = (xf * r * nw_ref[...]) * (1.0 + sc_ref[...]) + sh_ref[...]
        h_ref[...] = h.astype(BF16)
        ht_ref[...] = h.T.astype(BF16)

    vec = pl.BlockSpec((1, D), lambda i: (0, 0))
    return pl.pallas_call(
        body, name="norm_fwd", grid=(s // tm,),
        in_specs=[pl.BlockSpec((tm, D), lambda i: (i, 0)), vec, vec, vec],
        out_specs=[pl.BlockSpec((tm, D), lambda i: (i, 0)), pl.BlockSpec((D, tm), lambda i: (0, i))],
        out_shape=[jax.ShapeDtypeStruct((s, D), BF16), jax.ShapeDtypeStruct((D, s), BF16)],
        compiler_params=_cp(("parallel",)))(x, nw, scale, shift)


def norm_bwd(dh, x, dy, nw, scale, tm):
    s = x.shape[0]

    def body(dh_ref, x_ref, dy_ref, nw_ref, sc_ref, gx_ref, st_ref):
        @pl.when(pl.program_id(0) == 0)
        def _():
            st_ref[...] = jnp.zeros_like(st_ref)

        nw, sc1 = nw_ref[...], 1.0 + sc_ref[...]

        def chunk(c, sums):
            rows = pl.ds(pl.multiple_of(c * ROWS, ROWS), ROWS)
            xf, g = x_ref[rows, :], dh_ref[rows, :]
            r = lax.rsqrt(jnp.mean(xf * xf, axis=-1, keepdims=True) + EPS)
            xh = xf * r
            dn = g * sc1
            dxh = dn * nw
            gx_ref[rows, :] = dy_ref[rows, :] + r * (dxh - xh * jnp.mean(dxh * xh, axis=-1, keepdims=True))
            fold = lambda t: jnp.sum(t.reshape(ROWS // 8, 8, D), axis=0)
            return sums[0] + fold(g), sums[1] + fold(g * xh * nw), sums[2] + fold(dn * xh)

        zero = jnp.zeros((8, D), F32)
        sums = lax.fori_loop(0, tm // ROWS, chunk, (zero, zero, zero), unroll=True)
        for j in range(3):
            st_ref[j:j + 1, :] += jnp.sum(sums[j], axis=0, keepdims=True)

    vec = pl.BlockSpec((1, D), lambda i: (0, 0))
    row = pl.BlockSpec((tm, D), lambda i: (i, 0))
    return pl.pallas_call(
        body, name="norm_bwd", grid=(s // tm,),
        in_specs=[row, row, row, vec, vec],
        out_specs=[row, pl.BlockSpec((8, D), lambda i: (0, 0))],
        out_shape=[jax.ShapeDtypeStruct((s, D), F32), jax.ShapeDtypeStruct((8, D), F32)],
        compiler_params=_cp(("arbitrary",)))(dh, x, dy, nw, scale)


def proj_fwd_gather(h, w_shard, extras, order, tm):
    s = h.shape[0]
    ni = s // tm
    n = 1 + len(extras)
    mid = ni - 2

    def body(order_ref, h_ref, *refs):
        ins, o_ref, outs = refs[:n], refs[n], refs[n + 1:2 * n + 1]
        wbuf, send_sems, recv_sems, local_sems, load_sems = refs[2 * n + 1:]
        jj, i = pl.program_id(0), pl.program_id(1)
        x, y, c = _coords()
        me, sibling = (x, y, c), (x, y, 1 - c)
        chips = _chip_order(x, y, c)
        relayed = [(*chips[1], 1 - c), (*chips[0], 1 - c), (*chips[2], 1 - c)]

        def slot(a, dev):
            return outs[a].at[_dev_index(dev)]

        def copy(a, k, block, to, src=None):
            return pltpu.make_async_remote_copy(
                src_ref=slot(a, block) if src is None else src, dst_ref=slot(a, block),
                send_sem=send_sems.at[a, k], recv_sem=recv_sems.at[a, k], device_id=to, device_id_type=MESH)

        mine = [pltpu.make_async_copy(ins[a], slot(a, me), local_sems.at[a]) for a in range(n)]
        to_sibling = [copy(a, 0, me, sibling, src=ins[a]) for a in range(n)]
        to_chip = [[copy(a, 1 + j, me, (*chips[j], c), src=ins[a]) for a in range(n)] for j in range(2)]
        onward = [copy(a, 3, (*chips[1], c), (*chips[0], c)) for a in range(n)]
        passed = [[copy(a, 4 + j, (*ch, c), sibling) for a in range(n)] for j, ch in enumerate(chips)]
        sends = lambda a: [to_sibling[a], to_chip[0][a], to_chip[1][a], onward[a]] + [passed[j][a] for j in range(3)]

        def arrived(a, j):
            copy(a, 1 + j, (*chips[j], c), me).wait_recv()

        def load(row):
            return pltpu.make_async_copy(outs[0].at[order_ref[row]], wbuf.at[row % 2], load_sems.at[row % 2])

        @pl.when((jj == 0) & (i == 0))
        def _():
            for cp in mine:
                cp.start()
            to_sibling[0].start()
            to_chip[0][0].start()
            pltpu.make_async_copy(ins[0], wbuf.at[0], load_sems.at[0]).start()

        @pl.when((jj == 1) & (i == 0))
        def _():
            to_chip[1][0].start()

        @pl.when((jj == 4) & (i == 0))
        def _():
            for a in range(1, n):
                to_sibling[a].start()
                to_chip[0][a].start()
                to_chip[1][a].start()

        direct = {2: 0, 3: 1, 6: 2}
        relay = {4: 0, 5: 1, 7: 2}

        @pl.when((jj == 0) & (i == mid))
        def _():
            copy(0, 0, sibling, me).wait_recv()

        for row, j in direct.items():
            @pl.when((jj == row - 1) & (i == mid))
            def _(j=j):
                arrived(0, j)
                passed[j][0].start()
                if j == 1:
                    onward[0].start()

        for row, j in relay.items():
            @pl.when((jj == row - 1) & (i == mid))
            def _(j=j):
                copy(0, 4 + j, relayed[j], me).wait_recv()

        @pl.when((jj == NDEV - 1) & (i == 0))
        def _():
            for a in range(1, n):
                arrived(a, 1)
                onward[a].start()
                passed[1][a].start()
                arrived(a, 0)
                passed[0][a].start()

        @pl.when((jj < NDEV - 1) & (i == mid))
        def _():
            load(jj + 1).start()

        @pl.when(i == 0)
        def _():
            load(jj).wait()

        o_ref[...] = jnp.dot(h_ref[...], wbuf[jj % 2], preferred_element_type=F32).astype(BF16)

        @pl.when((jj == NDEV - 1) & (i == ni - 1))
        def _():
            for a in range(1, n):
                arrived(a, 2)
                passed[2][a].start()
            for a in range(1, n):
                copy(a, 0, sibling, me).wait_recv()
                for j in range(3):
                    copy(a, 4 + j, relayed[j], me).wait_recv()
            for a in range(n):
                mine[a].wait()
                for cp in sends(a):
                    cp.wait_send()

    any_spec = pl.BlockSpec(memory_space=pl.ANY)
    outs = pl.pallas_call(
        body, name="proj_fwd_gather",
        grid_spec=pltpu.PrefetchScalarGridSpec(
            num_scalar_prefetch=1, grid=(NDEV, ni),
            in_specs=[pl.BlockSpec((tm, D), lambda jj, i, o: (i, 0))] + [any_spec] * n,
            out_specs=[pl.BlockSpec((tm, SHARD), lambda jj, i, o: (i, o[jj]))] + [any_spec] * n,
            scratch_shapes=[pltpu.VMEM((2, D, SHARD), BF16), pltpu.SemaphoreType.DMA((n, 7)),
                            pltpu.SemaphoreType.DMA((n, 7)), pltpu.SemaphoreType.DMA((n,)),
                            pltpu.SemaphoreType.DMA((2,))]),
        out_shape=[jax.ShapeDtypeStruct((s, NIN), BF16), jax.ShapeDtypeStruct((NDEV, D, SHARD), BF16)]
                  + [jax.ShapeDtypeStruct((NDEV,) + e.shape, e.dtype) for e in extras],
        compiler_params=_cp(("arbitrary", "arbitrary")))(order, h, w_shard, *extras)
    return outs[0], outs[1], outs[2:]


def proj_bwd(ht, dproj, wg, smalls, order, tt):
    s = dproj.shape[0]
    nk = s // tt
    n = len(smalls)

    def body(order_ref, ht_ref, dp_ref, w_ref, *rest):
        small_in = rest[:n]
        dh_ref, gw_ref, rwin_ref = rest[n:n + 3]
        small_out = rest[n + 3:2 * n + 3]
        acc, stage, send_sems, recv_sems, local_sems, stage_sems = rest[2 * n + 3:]
        t, k = pl.program_id(0), pl.program_id(1)
        me_xyc = _coords()
        me = _dev_index(me_xyc)
        peers = [_flip(me_xyc, f) for f in FLIPS]

        def exchange(a, kf, src_arr, dst_arr):
            pid = _dev_index(peers[kf])
            mk = lambda dst: pltpu.make_async_remote_copy(
                src_ref=src_arr.at[pid], dst_ref=dst, send_sem=send_sems.at[a, kf], recv_sem=recv_sems.at[a, kf],
                device_id=peers[kf], device_id_type=MESH)
            return mk(dst_arr.at[me]), mk(dst_arr.at[pid])

        small_pairs = [exchange(1 + a, kf, small_in[a], small_out[a]) for kf in range(7) for a in range(n)]
        small_own = [pltpu.make_async_copy(small_in[a].at[me], small_out[a].at[me], local_sems.at[1 + a])
                     for a in range(n)]
        win_pairs = [exchange(0, kf, gw_ref, rwin_ref) for kf in range(7)]
        win_own = pltpu.make_async_copy(gw_ref.at[me], rwin_ref.at[me], local_sems.at[0])

        def to_hbm(jj):
            slab = me if jj == 7 else _dev_index(peers[jj])
            return pltpu.make_async_copy(stage.at[jj % 2], gw_ref.at[slab], stage_sems.at[jj % 2])

        @pl.when((t == 0) & (k == 0))
        def _():
            for cp in small_own:
                cp.start()
            for send, _ in small_pairs:
                send.start()

        @pl.when(t < NDEV)
        def _():
            p = jnp.dot(ht_ref[...], dp_ref[...], preferred_element_type=F32)

            @pl.when(k == 0)
            def _():
                acc[...] = p

            @pl.when(k > 0)
            def _():
                acc[...] += p

        for jj in range(NDEV):
            @pl.when((t == jj) & (k == nk - 1))
            def _(jj=jj):
                stage[jj % 2] = acc[...].astype(BF16)
                to_hbm(jj).start()

            @pl.when((t == jj + 1) & (k == 1))
            def _(jj=jj):
                to_hbm(jj).wait()
                if jj < 7:
                    win_pairs[jj][0].start()
                else:
                    win_own.start()

        @pl.when(t >= NDEV)
        def _():
            p = lax.dot_general(dp_ref[...], w_ref[...], NT, preferred_element_type=F32)

            @pl.when(k == 0)
            def _():
                dh_ref[...] = p

            @pl.when(k > 0)
            def _():
                dh_ref[...] += p

        @pl.when((t == 2 * NDEV - 1) & (k == nk - 1))
        def _():
            for _, recv in win_pairs + small_pairs:
                recv.wait_recv()
            for send, _ in win_pairs + small_pairs:
                send.wait_send()
            win_own.wait()
            for cp in small_own:
                cp.wait()

    any_spec = pl.BlockSpec(memory_space=pl.ANY)
    first = lambda t: t < NDEV
    outs = pl.pallas_call(
        body, name="proj_bwd",
        grid_spec=pltpu.PrefetchScalarGridSpec(
            num_scalar_prefetch=1, grid=(2 * NDEV, nk),
            in_specs=[pl.BlockSpec((D, tt), lambda t, k, o: (0, jnp.where(first(t), k, nk - 1))),
                      pl.BlockSpec((tt, SHARD), lambda t, k, o: (jnp.where(first(t), k, t - NDEV),
                                                                 jnp.where(first(t), o[jnp.minimum(t, NDEV - 1)], k))),
                      pl.BlockSpec((None, D, SHARD), lambda t, k, o: (jnp.where(first(t), 0, k), 0, 0))]
                     + [any_spec] * n,
            out_specs=[pl.BlockSpec((tt, D), lambda t, k, o: (jnp.where(first(t), 0, t - NDEV), 0))]
                      + [any_spec] * (2 + n),
            scratch_shapes=[pltpu.VMEM((D, SHARD), F32), pltpu.VMEM((2, D, SHARD), BF16),
                            pltpu.SemaphoreType.DMA((1 + n, 7)), pltpu.SemaphoreType.DMA((1 + n, 7)),
                            pltpu.SemaphoreType.DMA((1 + n,)), pltpu.SemaphoreType.DMA((2,))]),
        out_shape=[jax.ShapeDtypeStruct((s, D), F32), jax.ShapeDtypeStruct((NDEV, D, SHARD), BF16),
                   jax.ShapeDtypeStruct((NDEV, D, SHARD), BF16)]
                  + [jax.ShapeDtypeStruct(a.shape, a.dtype) for a in smalls],
        compiler_params=_cp(("arbitrary", "arbitrary"), 56))(order, ht, dproj, wg, *smalls)
    return outs[0], outs[2], outs[3:]


def matmul_tn(a, b, name, tk):
    s, m = a.shape
    n = b.shape[1]
    nk = s // tk

    def body(a_ref, b_ref, o_ref, acc_ref):
        k = pl.program_id(0)
        p = lax.dot_general(a_ref[...], b_ref[...], TN, preferred_element_type=F32)

        @pl.when(k == 0)
        def _():
            acc_ref[...] = p

        @pl.when(k > 0)
        def _():
            acc_ref[...] += p

        @pl.when(k == nk - 1)
        def _():
            o_ref[...] = acc_ref[...].astype(BF16)

    return pl.pallas_call(
        body, name=name, grid=(nk,),
        in_specs=[pl.BlockSpec((tk, m), lambda k: (k, 0)), pl.BlockSpec((tk, n), lambda k: (k, 0))],
        out_specs=pl.BlockSpec((m, n), lambda k: (0, 0)),
        out_shape=jax.ShapeDtypeStruct((m, n), BF16),
        scratch_shapes=[pltpu.VMEM((m, n), F32)],
        compiler_params=_cp(("arbitrary",)))(a, b)


def _head_matrices():
    lane = lax.broadcasted_iota(jnp.int32, (CB, CB), 0)
    col = lax.broadcasted_iota(jnp.int32, (CB, CB), 1)
    same = (lane // HD == col // HD).astype(BF16)
    lane_c = lax.broadcasted_iota(jnp.int32, (CB, LANES), 0)
    col_c = lax.broadcasted_iota(jnp.int32, (CB, LANES), 1)
    total = (lane_c // HD == col_c).astype(BF16)
    pick = (lane_c == col_c * HD).astype(BF16)
    return same, total, pick


def _head_sum(x, m_ref):
    return jnp.dot(x.astype(BF16), m_ref[...], preferred_element_type=F32)


def _dot_hilo(x, m_ref):
    hi = x.astype(BF16)
    lo = (x - hi.astype(F32)).astype(BF16)
    return (jnp.dot(hi, m_ref[...], preferred_element_type=F32)
            + jnp.dot(lo, m_ref[...], preferred_element_type=F32))


def _to_residue_major(val, buf, out_ref, dil):
    rows = out_ref.shape[1]
    for k in range(val.shape[1] // LANES):
        lanes = slice(k * LANES, (k + 1) * LANES)
        buf[k] = val[:, lanes]
        for r in range(dil):
            out_ref[r, :, lanes] = buf.at[k][pl.ds(r, rows, stride=dil), :].astype(out_ref.dtype)


def _from_residue_major(ref, buf, dil):
    if dil == 1:
        return ref[0].astype(F32)
    rows = ref.shape[1]
    for k in range(CB // LANES):
        for r in range(dil):
            buf.at[k][pl.ds(r, rows, stride=dil), :] = ref[r, :, k * LANES:(k + 1) * LANES].astype(F32)
    return jnp.concatenate([buf[k] for k in range(CB // LANES)], axis=1)


def qkv_prep(proj, qw8, kw8, same, tm):
    s = proj.shape[0]
    items = []
    for g, d in enumerate(DILATIONS):
        items += [(g, "q", CB_Q + g, d), (g, "k", CB_K + g, d)] + ([(g, "v", CB_V + g, d)] if d > 1 else [])
    n = len(items)

    def body(*refs):
        ins, (qw_ref, kw_ref, same_ref), outs, buf = refs[:n], refs[n:n + 3], refs[n + 3:2 * n + 3], refs[-1]
        for idx, (_, kind, _, dil) in enumerate(items):
            val = ins[idx][...].astype(F32)
            if kind != "v":
                r = lax.rsqrt(_head_sum(val * val, same_ref) * (1.0 / HD) + EPS)
                val = val * r * (qw_ref if kind == "q" else kw_ref)[...]
            if dil == 1:
                outs[idx][0] = val.astype(BF16)
            else:
                _to_residue_major(val, buf, outs[idx], dil)

    full = lambda a: pl.BlockSpec(a.shape, lambda i: (0, 0))
    outs = pl.pallas_call(
        body, name="qkv_prep", grid=(s // tm,),
        in_specs=[pl.BlockSpec((tm, CB), lambda i, cb=cb: (i, cb)) for _, _, cb, _ in items]
                 + [full(qw8), full(kw8), full(same)],
        out_specs=[pl.BlockSpec((d, tm // d, CB), lambda i: (0, i, 0)) for _, _, _, d in items],
        out_shape=[jax.ShapeDtypeStruct((d, s // d, CB), BF16) for _, _, _, d in items],
        scratch_shapes=[pltpu.VMEM((CB // LANES, tm, LANES), F32)],
        compiler_params=_cp(("parallel",)))(*([proj] * n), qw8 * (HD ** -0.5), kw8, same)
    srcs = [[None, None, (proj, CB_V + g)] for g in range(len(DILATIONS))]
    for (g, kind, _, _), o in zip(items, outs):
        srcs[g]["qkv".index(kind)] = (o.reshape(s, CB), 0)
    return srcs


def stats_prep(da, lc, dc, g, dil, tm):
    s = da.shape[0]
    rows = tm // dil

    def body(da_ref, lc_ref, dc_ref, dap_ref, lcp_ref, dcp_ref, lt_ref, dt_ref, buf):
        if dil == 1:
            dap_ref[0] = da_ref[...]
        else:
            _to_residue_major(da_ref[...].astype(F32), buf, dap_ref, dil)
        for src, dst, dst_t in ((lc_ref, lcp_ref, lt_ref), (dc_ref, dcp_ref, dt_ref)):
            buf[0] = src[...]
            for r in range(dil):
                piece = buf.at[0][pl.ds(r, rows, stride=dil), :] if dil > 1 else buf[0]
                dst[r] = piece
                dst_t[r] = piece.T[0:NH, :]

    row = lambda w: pl.BlockSpec((tm, w), lambda i: (i, 0))
    rm = lambda w: pl.BlockSpec((dil, rows, w), lambda i: (0, i, 0))
    tr = pl.BlockSpec((dil, NH, rows), lambda i: (0, 0, i))
    length = s // dil
    dap, lcp, dcp, lt, dt = pl.pallas_call(
        body, name=f"stats_prep_g{g}", grid=(s // tm,),
        in_specs=[row(CB), row(LANES), row(LANES)],
        out_specs=[rm(CB), rm(LANES), rm(LANES), tr, tr],
        out_shape=[jax.ShapeDtypeStruct((dil, length, CB), BF16)]
                  + [jax.ShapeDtypeStruct((dil, length, LANES), F32)] * 2
                  + [jax.ShapeDtypeStruct((dil, NH, length), F32)] * 2,
        scratch_shapes=[pltpu.VMEM((CB // LANES, tm, LANES), F32)],
        compiler_params=_cp(("parallel",)))(da, lc, dc)
    return (dap.reshape(s, CB), lcp.reshape(s, LANES), dcp.reshape(s, LANES),
            lt.reshape(dil * NH, length), dt.reshape(dil * NH, length))


def qkv_grads_to_dproj(dproj, proj, grads, qw8, kw8, same, tm):
    s = dproj.shape[0]
    ni = s // tm
    flat = [(t.reshape(d, s // d, CB), d, kind, 3 * kind + g)
            for g, d in enumerate(DILATIONS) for kind, t in enumerate(grads[g])]
    nf = len(flat)
    nraw = 2 * len(DILATIONS)

    def body(*refs):
        dp_hbm, raws, ins = refs[nraw + nf + 4], refs[1:1 + nraw], refs[1 + nraw:1 + nraw + nf]
        qw_ref, kw_ref, same_ref = refs[1 + nraw + nf:4 + nraw + nf]
        gw_ref, stage, buf, sems = refs[5 + nraw + nf:]
        i = pl.program_id(0)
        slot = i % 2

        def slab(step, sl):
            return pltpu.make_async_copy(
                stage.at[sl], dp_hbm.at[pl.ds(pl.multiple_of(step * tm, tm), tm), pl.ds(CB_Q * CB, 9 * CB)],
                sems.at[sl])

        @pl.when(i == 0)
        def _():
            gw_ref[...] = jnp.zeros_like(gw_ref)

        @pl.when(i >= 2)
        def _():
            slab(i - 2, slot).wait()

        for ref, (_, d, kind, jj) in zip(ins, flat):
            cols = slice(jj * CB, (jj + 1) * CB)
            dn = _from_residue_major(ref, buf, d)
            if kind == 2:
                stage[slot, :, cols] = dn.astype(BF16)
                continue
            t = raws[jj][...].astype(F32)
            r = lax.rsqrt(_head_sum(t * t, same_ref) * (1.0 / HD) + EPS)
            xh = t * r
            gw_ref[kind:kind + 1, :] += jnp.sum(dn * xh, axis=0, keepdims=True)
            dxh = dn * (qw_ref if kind == 0 else kw_ref)[...]
            mean = _head_sum(dxh * xh, same_ref) * (1.0 / HD)
            stage[slot, :, cols] = (r * (dxh - xh * mean)).astype(BF16)
        slab(i, slot).start()

        @pl.when(i == ni - 1)
        def _():
            slab(i - 1, 1 - slot).wait()
            slab(i, slot).wait()

    full = lambda a: pl.BlockSpec(a.shape, lambda i: (0, 0))
    any_spec = pl.BlockSpec(memory_space=pl.ANY)
    return pl.pallas_call(
        body, name="qkv_grads_to_dproj", grid=(ni,),
        in_specs=[any_spec] + [pl.BlockSpec((tm, CB), lambda i, jb=jb: (i, CB_Q + jb)) for jb in range(nraw)]
                 + [pl.BlockSpec((d, tm // d, CB), lambda i: (0, i, 0)) for _, d, _, _ in flat]
                 + [full(qw8), full(kw8), full(same)],
        out_specs=[any_spec, pl.BlockSpec((8, CB), lambda i: (0, 0))],
        out_shape=[jax.ShapeDtypeStruct((s, NIN), BF16), jax.ShapeDtypeStruct((8, CB), F32)],
        input_output_aliases={0: 0},
        scratch_shapes=[pltpu.VMEM((2, tm, 9 * CB), BF16), pltpu.VMEM((CB // LANES, tm, LANES), F32),
                        pltpu.SemaphoreType.DMA((2,))],
        compiler_params=_cp(("arbitrary",)))(
            dproj, *([proj] * nraw), *[t for t, _, _, _ in flat], qw8, kw8, same)


def _lane_lo():
    return lax.broadcasted_iota(jnp.int32, (1, 2 * HD), 1) < HD


def _stack_heads(t, lo):
    zero = jnp.zeros_like(t)
    return jnp.concatenate([jnp.where(lo, t, zero), jnp.where(lo, zero, t)], axis=0)


def _masks(other_ok):
    qi = lax.broadcasted_iota(jnp.int32, (QB, QB), 0)
    kj = lax.broadcasted_iota(jnp.int32, (QB, QB), 1)
    return (kj >= qi) & other_ok, kj <= qi


SUB = 4


def _attn_specs(nb, dil):
    steps = nb // SUB
    main = lambda cb, w=CB: pl.BlockSpec((SUB * QB, w), lambda r, s: (r * steps + s, cb))
    prev = lambda cb: pl.BlockSpec((QB, CB), lambda r, s: (jnp.maximum(r * nb + SUB * s - 1, 0), cb))
    nxt = lambda cb: pl.BlockSpec((QB, CB), lambda r, s: (jnp.minimum(r * nb + SUB * (s + 1), dil * nb - 1), cb))
    return main, prev, nxt


def attn_fwd(q_src, k_src, v_src, g, dil):
    s = q_src[0].shape[0]
    nb = s // dil // QB
    main, prev, _ = _attn_specs(nb, dil)

    def body(q_ref, kp_ref, k_ref, vp_ref, v_ref, o_ref, l_ref, kbuf, vbuf):
        step = pl.program_id(1)
        kbuf[0:QB], kbuf[QB:] = kp_ref[...], k_ref[...]
        vbuf[0:QB], vbuf[QB:] = vp_ref[...], v_ref[...]
        lo = _lane_lo()

        def block(j, carry):
            r0 = pl.multiple_of(j * QB, QB)
            rows, krows = pl.ds(r0, QB), pl.ds(r0, 2 * QB)
            m_prev, m_cur = _masks(step * SUB + j > 0)
            mask = jnp.concatenate([m_prev, m_cur], axis=1)
            mask = jnp.concatenate([mask, mask], axis=0)
            for i in range(NH // 2):
                sl = slice(2 * HD * i, 2 * HD * (i + 1))
                qs, ks, vv = q_ref[rows, sl], kbuf[krows, sl], vbuf[krows, sl]
                sc = lax.dot_general(_stack_heads(qs, lo), ks, NT, preferred_element_type=F32)
                sc = jnp.where(mask, sc, NEG)
                mx = jnp.max(sc, axis=-1, keepdims=True)
                p = jnp.exp(sc - mx)
                den = jnp.sum(p, axis=-1, keepdims=True)
                o = jnp.dot(p.astype(BF16), vv, preferred_element_type=F32) * (1.0 / den)
                lse = jnp.broadcast_to(mx + jnp.log(den), (2 * QB, 2 * HD))
                o_ref[rows, sl] = jnp.where(lo, o[:QB], o[QB:])
                l_ref[rows, sl] = jnp.where(lo, lse[:QB], lse[QB:])
            return carry

        lax.fori_loop(0, SUB, block, 0, unroll=True)

    out = jax.ShapeDtypeStruct((s, CB), F32)
    return pl.pallas_call(
        body, name=f"attn_fwd_g{g}", grid=(dil, nb // SUB),
        in_specs=[main(q_src[1]), prev(k_src[1]), main(k_src[1]), prev(v_src[1]), main(v_src[1])],
        out_specs=[main(0)] * 2, out_shape=[out, out],
        scratch_shapes=[pltpu.VMEM(((SUB + 1) * QB, CB), BF16)] * 2,
        compiler_params=_cp(("parallel", "parallel")))(q_src[0], k_src[0], k_src[0], v_src[0], v_src[0])


def attn_bwd_q(q_src, k_src, v_src, da, lc, dc, g, dil):
    s = q_src[0].shape[0]
    nb = s // dil // QB
    main, prev, _ = _attn_specs(nb, dil)

    def body(q_ref, kp_ref, k_ref, vp_ref, v_ref, da_ref, l_ref, d_ref, dq_ref, kbuf, vbuf):
        step = pl.program_id(1)
        kbuf[0:QB], kbuf[QB:] = kp_ref[...], k_ref[...]
        vbuf[0:QB], vbuf[QB:] = vp_ref[...], v_ref[...]
        lo = _lane_lo()

        def block(j, carry):
            r0 = pl.multiple_of(j * QB, QB)
            rows, krows = pl.ds(r0, QB), pl.ds(r0, 2 * QB)
            m_prev, m_cur = _masks(step * SUB + j > 0)
            mask = jnp.concatenate([m_prev, m_cur], axis=1)
            mask = jnp.concatenate([mask, mask], axis=0)
            lcols, dcols = l_ref[rows, :], d_ref[rows, :]
            for i in range(NH // 2):
                sl = slice(2 * HD * i, 2 * HD * (i + 1))
                qs, ks, vv, da2 = q_ref[rows, sl], kbuf[krows, sl], vbuf[krows, sl], da_ref[rows, sl]
                pair = lambda t: jnp.concatenate([t[:, 2 * i:2 * i + 1], t[:, 2 * i + 1:2 * i + 2]], axis=0)
                sc = lax.dot_general(_stack_heads(qs, lo), ks, NT, preferred_element_type=F32)
                sc = jnp.where(mask, sc, NEG)
                p = jnp.exp(sc - pair(lcols))
                dp = lax.dot_general(_stack_heads(da2, lo), vv, NT, preferred_element_type=F32)
                ds = p * (dp - pair(dcols))
                dq = jnp.dot(ds.astype(BF16), ks, preferred_element_type=F32)
                dq_ref[rows, sl] = (jnp.where(lo, dq[:QB], dq[QB:]) * (HD ** -0.5)).astype(BF16)
            return carry

        lax.fori_loop(0, SUB, block, 0, unroll=True)

    return pl.pallas_call(
        body, name=f"attn_bwd_q_g{g}", grid=(dil, nb // SUB),
        in_specs=[main(q_src[1]), prev(k_src[1]), main(k_src[1]), prev(v_src[1]), main(v_src[1]),
                  main(0), main(0, LANES), main(0, LANES)],
        out_specs=main(0), out_shape=jax.ShapeDtypeStruct((s, CB), BF16),
        scratch_shapes=[pltpu.VMEM(((SUB + 1) * QB, CB), BF16)] * 2,
        compiler_params=_cp(("parallel", "parallel")))(
            q_src[0], k_src[0], k_src[0], v_src[0], v_src[0], da, lc, dc)


def attn_bwd_kv(q_src, k_src, v_src, da, lt, dt, g, dil):
    s = q_src[0].shape[0]
    nb = s // dil // QB
    main, _, nxt = _attn_specs(nb, dil)

    def body(k_ref, v_ref, q_ref, qn_ref, da_ref, dan_ref, l_ref, ln_ref, d_ref, dn_ref, dk_ref, dv_ref,
             qbuf, dabuf, lbuf, dbuf):
        step = pl.program_id(1)
        qbuf[0:SUB * QB], qbuf[SUB * QB:] = q_ref[...], qn_ref[...]
        dabuf[0:SUB * QB], dabuf[SUB * QB:] = da_ref[...], dan_ref[...]
        for c in range(SUB):
            lbuf[c], dbuf[c] = l_ref[:, c * QB:(c + 1) * QB], d_ref[:, c * QB:(c + 1) * QB]
        lbuf[SUB], dbuf[SUB] = ln_ref[...], dn_ref[...]
        lo = _lane_lo()
        kj = lax.broadcasted_iota(jnp.int32, (QB, QB), 0)
        qi = lax.broadcasted_iota(jnp.int32, (QB, QB), 1)

        def block(j, carry):
            r0 = pl.multiple_of(j * QB, QB)
            rows, qrows = pl.ds(r0, QB), pl.ds(r0, 2 * QB)
            mask = jnp.concatenate([kj <= qi, (kj >= qi) & (step * SUB + j < nb - 1)], axis=1)
            mask = jnp.concatenate([mask, mask], axis=1)
            lrow = jnp.concatenate([lbuf[j], lbuf[j + 1]], axis=1)
            drow = jnp.concatenate([dbuf[j], dbuf[j + 1]], axis=1)
            for i in range(NH // 2):
                sl = slice(2 * HD * i, 2 * HD * (i + 1))
                q2, da2 = _stack_heads(qbuf[qrows, sl], lo), _stack_heads(dabuf[qrows, sl], lo)
                ks, vv = k_ref[rows, sl], v_ref[rows, sl]
                pair = lambda t: jnp.concatenate([t[2 * i:2 * i + 1, :], t[2 * i + 1:2 * i + 2, :]], axis=1)
                sc = lax.dot_general(ks, q2, NT, preferred_element_type=F32)
                sc = jnp.where(mask, sc, NEG)
                p = jnp.exp(sc - pair(lrow))
                dp = lax.dot_general(vv, da2, NT, preferred_element_type=F32)
                ds = p * (dp - pair(drow))
                dv_ref[rows, sl] = jnp.dot(p.astype(BF16), da2, preferred_element_type=F32).astype(BF16)
                dk_ref[rows, sl] = jnp.dot(ds.astype(BF16), q2, preferred_element_type=F32).astype(BF16)
            return carry

        lax.fori_loop(0, SUB, block, 0, unroll=True)

    steps = nb // SUB
    t_main = pl.BlockSpec((NH, SUB * QB), lambda r, s: (r, s))
    t_nxt = pl.BlockSpec((NH, QB), lambda r, s: (r, jnp.minimum(SUB * (s + 1), nb - 1)))
    out = jax.ShapeDtypeStruct((s, CB), BF16)
    return pl.pallas_call(
        body, name=f"attn_bwd_kv_g{g}", grid=(dil, steps),
        in_specs=[main(k_src[1]), main(v_src[1]), main(q_src[1]), nxt(q_src[1]),
                  main(0), nxt(0), t_main, t_nxt, t_main, t_nxt],
        out_specs=[main(0), main(0)], out_shape=[out, out],
        scratch_shapes=[pltpu.VMEM(((SUB + 1) * QB, CB), BF16)] * 2 + [pltpu.VMEM((SUB + 1, NH, QB), F32)] * 2,
        compiler_params=_cp(("parallel", "parallel")))(
            k_src[0], v_src[0], q_src[0], q_src[0], da, da, lt, lt, dt, dt)


def _conv_taps(u, u_prev, first):
    tm = u.shape[0]
    row = lax.broadcasted_iota(jnp.int32, (tm, 1), 0)
    up = jnp.where(first, 0.0, u_prev)
    u1 = jnp.where(row == 0, up[HALO - 1:HALO, :], pltpu.roll(u, 1, 0))
    u2 = jnp.where(row == 0, up[HALO - 2:HALO - 1, :],
                   jnp.where(row == 1, up[HALO - 1:HALO, :], pltpu.roll(u, 2, 0)))
    return u1, u2


def mid_fwd(proj, o_g, lse_g, conv_w, pick, tm):
    s = proj.shape[0]
    hb = tm // HALO

    def body(ba_ref, ca_ref, xa_ref, za_ref, cah_ref, xah_ref, zb_ref,
             o0, o1, o2, l0, l1, l2, w_ref, pick_ref, ya_ref, yb_ref, at_ref, lc_ref, buf_o, buf_l):
        first = pl.program_id(0) == 0
        u = ca_ref[...].astype(F32) * xa_ref[...].astype(F32)
        u1, u2 = _conv_taps(u, cah_ref[...].astype(F32) * xah_ref[...].astype(F32), first)
        conv = w_ref[0:1, :] * u2 + w_ref[1:2, :] * u1 + w_ref[2:3, :] * u
        ya_ref[...] = (ba_ref[...].astype(F32) * conv * _silu(za_ref[...].astype(F32))).astype(BF16)
        ls = [_from_residue_major(l, buf_l.at[g], d) for g, (l, d) in enumerate(zip((l0, l1, l2), DILATIONS))]
        mx = jnp.maximum(jnp.maximum(ls[0], ls[1]), ls[2])
        es = [jnp.exp(l - mx) for l in ls]
        den = es[0] + es[1] + es[2]
        num = jnp.zeros_like(den)
        for e, o, d in zip(es, (o0, o1, o2), DILATIONS):
            num = num + e * _from_residue_major(o, buf_o, d)
        attn = num / den
        at_ref[...] = attn
        lc_ref[...] = _dot_hilo(mx + jnp.log(den), pick_ref)
        yb_ref[...] = (attn * _silu(zb_ref[...].astype(F32))).astype(BF16)

    col = lambda j: pl.BlockSpec((tm, D), lambda i: (i, j))
    halo = lambda j: pl.BlockSpec((HALO, D), lambda i: (jnp.maximum(i * hb - 1, 0), j))
    loc = lambda w: pl.BlockSpec((tm, w), lambda i: (i, 0))
    rm = [pl.BlockSpec((d, tm // d, CB), lambda i: (0, i, 0)) for d in DILATIONS]
    rm_view = lambda ts: [t.reshape(d, s // d, CB) for t, d in zip(ts, DILATIONS)]
    return pl.pallas_call(
        body, name="mid_fwd", grid=(s // tm,),
        in_specs=[col(0), col(1), col(2), col(3), halo(1), halo(2),
                  pl.BlockSpec((tm, CB), lambda i: (i, CB_ZB))] + rm + rm
                 + [pl.BlockSpec((3, D), lambda i: (0, 0)), pl.BlockSpec(pick.shape, lambda i: (0, 0))],
        out_specs=[loc(D), loc(CB), loc(CB), loc(LANES)],
        out_shape=[jax.ShapeDtypeStruct((s, D), BF16), jax.ShapeDtypeStruct((s, CB), BF16),
                   jax.ShapeDtypeStruct((s, CB), F32), jax.ShapeDtypeStruct((s, LANES), F32)],
        scratch_shapes=[pltpu.VMEM((CB // LANES, tm, LANES), F32), pltpu.VMEM((3, CB // LANES, tm, LANES), F32)],
        compiler_params=_cp(("parallel",)))(
            proj, proj, proj, proj, proj, proj, proj, *rm_view(o_g), *rm_view(lse_g), conv_w, pick)


def mid_bwd(dproj, proj, dya, conv_w, tm):
    s = proj.shape[0]
    hb = tm // HALO
    nblk = s // tm
    last_h = s // HALO - 1

    def body(_, ba_ref, ca_ref, xa_ref, za_ref, cah_ref, xah_ref, ban_ref, zan_ref, dy_ref, dyn_ref, w_ref,
             o_ref, gw_ref):
        i = pl.program_id(0)
        ba, ca, xa, za = (t[...].astype(F32) for t in (ba_ref, ca_ref, xa_ref, za_ref))
        u = ca * xa
        u1, u2 = _conv_taps(u, cah_ref[...].astype(F32) * xah_ref[...].astype(F32), i == 0)
        w0, w1, w2 = w_ref[0:1, :], w_ref[1:2, :], w_ref[2:3, :]
        conv = w0 * u2 + w1 * u1 + w2 * u
        sg = jax.nn.sigmoid(za)
        sz = za * sg
        dy = dy_ref[...].astype(F32)
        dconv = dy * ba * sz
        dcn = dyn_ref[...].astype(F32) * ban_ref[...].astype(F32) * _silu(zan_ref[...].astype(F32))
        dcn = jnp.where(i == nblk - 1, 0.0, dcn)
        row = lax.broadcasted_iota(jnp.int32, (tm, 1), 0)
        d1 = jnp.where(row == tm - 1, dcn[0:1, :], pltpu.roll(dconv, tm - 1, 0))
        d2 = jnp.where(row == tm - 2, dcn[0:1, :],
                       jnp.where(row == tm - 1, dcn[1:2, :], pltpu.roll(dconv, tm - 2, 0)))
        du = w2 * dconv + w1 * d1 + w0 * d2
        o_ref[:, 0:D] = (dy * conv * sz).astype(BF16)
        o_ref[:, D:2 * D] = (du * xa).astype(BF16)
        o_ref[:, 2 * D:3 * D] = (du * ca).astype(BF16)
        o_ref[:, 3 * D:4 * D] = (dy * ba * conv * (sg * (1.0 + za * (1.0 - sg)))).astype(BF16)

        @pl.when(i == 0)
        def _():
            gw_ref[...] = jnp.zeros_like(gw_ref)

        gw_ref[0:1, :] += jnp.sum(dconv * u2, axis=0, keepdims=True)
        gw_ref[1:2, :] += jnp.sum(dconv * u1, axis=0, keepdims=True)
        gw_ref[2:3, :] += jnp.sum(dconv * u, axis=0, keepdims=True)

    col = lambda j: pl.BlockSpec((tm, D), lambda i: (i, j))
    halo_prev = lambda j: pl.BlockSpec((HALO, D), lambda i: (jnp.maximum(i * hb - 1, 0), j))
    halo_next = lambda j: pl.BlockSpec((HALO, D), lambda i: (jnp.minimum((i + 1) * hb, last_h), j))
    return pl.pallas_call(
        body, name="mid_bwd", grid=(nblk,),
        in_specs=[pl.BlockSpec(memory_space=pl.ANY), col(0), col(1), col(2), col(3),
                  halo_prev(1), halo_prev(2), halo_next(0), halo_next(3),
                  pl.BlockSpec((tm, D), lambda i: (i, 0)), halo_next(0),
                  pl.BlockSpec((3, D), lambda i: (0, 0))],
        out_specs=[pl.BlockSpec((tm, 4 * D), lambda i: (i, 0)), pl.BlockSpec((8, D), lambda i: (0, 0))],
        out_shape=[jax.ShapeDtypeStruct((s, NIN), BF16), jax.ShapeDtypeStruct((8, D), F32)],
        input_output_aliases={0: 0},
        compiler_params=_cp(("arbitrary",)))(dproj, proj, proj, proj, proj, proj, proj, proj, proj, dya, dya, conv_w)


def tail(proj, ya, yb, attn, x, target, gate, pa_w, pb_w, wo_w, total, tm):
    s = proj.shape[0]
    ni = s // tm
    ncol = NIN - CB_ZB * CB

    def body(ya_ref, yb_ref, ga_ref, gb_ref, zb_ref, at_ref, x_ref, t_ref, gate_ref, pa_ref, pb_ref, wo_ref,
             tot_ref, dp_hbm, dy_ref, dya_ref, da_ref, dc_ref, mg_ref, do_ref, dpa_ref, dpb_ref, st_ref,
             stage, sems):
        i = pl.program_id(0)
        slot = i % 2

        def slab(step, sl):
            return pltpu.make_async_copy(
                stage.at[sl], dp_hbm.at[pl.ds(pl.multiple_of(step * tm, tm), tm), pl.ds(CB_ZB * CB, ncol)],
                sems.at[sl])

        @pl.when(i == 0)
        def _():
            st_ref[...] = jnp.zeros_like(st_ref)

        @pl.when(i >= 2)
        def _():
            slab(i - 2, slot).wait()

        gate_v = gate_ref[...]
        pa = jnp.dot(ya_ref[...], pa_ref[...], preferred_element_type=F32)
        pb = jnp.dot(yb_ref[...], pb_ref[...], preferred_element_type=F32)
        sa = jax.nn.sigmoid(ga_ref[...].astype(F32))
        sb = jax.nn.sigmoid(gb_ref[...].astype(F32))
        merged = (sa * pa + sb * pb).astype(BF16)
        mg_ref[...] = merged
        out = jnp.dot(merged, wo_ref[...], preferred_element_type=F32)
        err = x_ref[...] + gate_v * out - t_ref[...]
        dy = err * (1.0 / D)
        dy_ref[...] = dy
        st_ref[0:1, :] += jnp.sum(dy * out, axis=0, keepdims=True)
        st_ref[1:2, :] += jnp.sum(err * err, axis=0, keepdims=True)
        dout = (gate_v * dy).astype(BF16)
        do_ref[...] = dout
        dmg = lax.dot_general(dout, wo_ref[...], NT, preferred_element_type=F32)
        dpa = (dmg * sa).astype(BF16)
        dpb = (dmg * sb).astype(BF16)
        dpa_ref[...] = dpa
        dpb_ref[...] = dpb
        stage[slot, :, CB:CB + D] = (dmg * pa * sa * (1.0 - sa)).astype(BF16)
        stage[slot, :, CB + D:] = (dmg * pb * sb * (1.0 - sb)).astype(BF16)
        dya_ref[...] = lax.dot_general(dpa, pa_ref[...], NT, preferred_element_type=F32).astype(BF16)
        dyb = lax.dot_general(dpb, pb_ref[...], NT, preferred_element_type=F32)
        zb = zb_ref[...].astype(F32)
        sg = jax.nn.sigmoid(zb)
        attn_v = at_ref[...]
        dattn = dyb * (zb * sg)
        da_ref[...] = dattn.astype(BF16)
        stage[slot, :, 0:CB] = (dyb * attn_v * (sg * (1.0 + zb * (1.0 - sg)))).astype(BF16)
        dc_ref[...] = _dot_hilo(dattn * attn_v, tot_ref)

        slab(i, slot).start()

        @pl.when(i == ni - 1)
        def _():
            slab(i - 1, 1 - slot).wait()
            slab(i, slot).wait()

    row = lambda w: pl.BlockSpec((tm, w), lambda i: (i, 0))
    pcol = lambda w, jb: pl.BlockSpec((tm, w), lambda i: (i, jb))
    full = lambda a: pl.BlockSpec(a.shape, lambda i: (0, 0))
    return pl.pallas_call(
        body, name="tail", grid=(ni,),
        in_specs=[row(D), row(CB), pcol(D, 9), pcol(D, 10), pcol(CB, CB_ZB), row(CB), row(D), row(D),
                  pl.BlockSpec((1, D), lambda i: (0, 0)), full(pa_w), full(pb_w), full(wo_w), full(total)],
        out_specs=[pl.BlockSpec(memory_space=pl.ANY),
                   row(D), row(D), row(CB), row(LANES), row(D), row(D), row(D), row(D),
                   pl.BlockSpec((8, D), lambda i: (0, 0))],
        out_shape=[jax.ShapeDtypeStruct((s, NIN), BF16), jax.ShapeDtypeStruct((s, D), F32),
                   jax.ShapeDtypeStruct((s, D), BF16), jax.ShapeDtypeStruct((s, CB), BF16),
                   jax.ShapeDtypeStruct((s, LANES), F32)] + [jax.ShapeDtypeStruct((s, D), BF16)] * 4
                  + [jax.ShapeDtypeStruct((8, D), F32)],
        scratch_shapes=[pltpu.VMEM((2, tm, ncol), BF16), pltpu.SemaphoreType.DMA((2,))],
        compiler_params=_cp(("arbitrary",), 56))(
            ya, yb, proj, proj, proj, attn, x, target, gate, pa_w, pb_w, wo_w, total)


def _local_step(x, target, shift, scale, gate, norm_w, conv_w, qw, kw, w_shard, small_shards, me_xyc):
    qw8, kw8 = jnp.tile(qw, (1, NH)), jnp.tile(kw, (1, NH))
    same, total, pick = _head_matrices()
    h, ht = norm_fwd(x, norm_w, scale, shift, 512)
    proj, wg, (pa_g, pb_g, wo_g) = proj_fwd_gather(h, w_shard, small_shards, gather_order(me_xyc), 1024)
    pa_w, wo_w = pa_g.reshape(D, D), wo_g.reshape(D, D)
    pb_w = pb_g.transpose(1, 0, 2).reshape(CB, D)
    srcs = qkv_prep(proj, qw8, kw8, same, 512)
    o_g, lse_g = zip(*[attn_fwd(*srcs[g], g, d) for g, d in enumerate(DILATIONS)])
    ya, yb, attn, lc = mid_fwd(proj, o_g, lse_g, conv_w, pick, 512)
    dproj, dy, dya, da, dc, merged, dout, dpa, dpb, st_tail = tail(
        proj, ya, yb, attn, x, target, gate, pa_w, pb_w, wo_w, total, 256)
    g_wo = matmul_tn(merged, dout, "grad_w_out", 1024)
    g_pa = matmul_tn(ya, dpa, "grad_w_br_conv", 1024)
    g_pb = matmul_tn(yb, dpb, "grad_w_br_attn", 1024)
    dproj, st_conv = mid_bwd(dproj, proj, dya, conv_w, 512)
    grads = []
    for g, d in enumerate(DILATIONS):
        da_p, lc_p, dc_p, lt, dt = stats_prep(da, lc, dc, g, d, 2048)
        dq = attn_bwd_q(*srcs[g], da_p, lc_p, dc_p, g, d)
        dk, dv = attn_bwd_kv(*srcs[g], da_p, lt, dt, g, d)
        grads.append((dq, dk, dv))
    dproj, gw_qk = qkv_grads_to_dproj(dproj, proj, grads, qw8, kw8, same, 512)
    slabs = [g_pa.reshape(NDEV, 128, D), g_pb.reshape(CB, NDEV, 128).transpose(1, 0, 2), g_wo.reshape(NDEV, 128, D)]
    dh, r_win, (r_pa, r_pb, r_wo) = proj_bwd(ht, dproj, wg, slabs, scatter_order(me_xyc), 1024)
    grad_x, st_norm = norm_bwd(dh, x, dy, norm_w, scale, 512)
    dmod = jnp.concatenate([st_norm[0:1], st_norm[1:2], st_tail[0:1]], axis=1)
    loss_part = (0.5 / D) * jnp.sum(st_tail[1])
    gw_heads = gw_qk[0:2].reshape(2, NH, HD).sum(axis=1)
    small = dict(dmod=dmod, norm_w=st_norm[2:3], conv_w=st_conv[0:3],
                 q_norm_w=gw_heads[0:1], k_norm_w=gw_heads[1:2], loss=loss_part)
    return grad_x, small, (r_win, r_pa, r_pb, r_wo)


def kernel(x, c, w_ada, b_ada, norm_w, w_in, conv_w, q_norm_w, k_norm_w, w_br_conv, w_br_attn, w_out, loss_target, m_w_ada, m_b_ada, m_norm_w, m_w_in, m_conv_w, m_q_norm_w, m_k_norm_w, m_w_br_conv, m_w_br_attn, m_w_out, v_w_ada, v_b_ada, v_norm_w, v_w_in, v_conv_w, v_q_norm_w, v_k_norm_w, v_w_br_conv, v_w_br_attn, v_w_out):
    me_xyc = (lax.axis_index("x"), lax.axis_index("y"), lax.axis_index("c"))
    me = _dev_index(me_xyc)
    ncol = w_ada.shape[2]

    conv_pad = jnp.zeros((8, 128), F32).at[0:3].set(conv_w[0])
    c_all, conv_all = all_gather([c, conv_pad], "gather_cond")
    conv_full = conv_all[:, 0:3].transpose(1, 0, 2).reshape(3, D)
    c_all = c_all.reshape(NDEV, D)

    b_cols = lax.dynamic_slice(b_ada, (0, me * ncol), (1, ncol))
    mod_cols = ada_fwd(c_all, w_ada[0], b_cols)
    (mod_all,) = all_gather([mod_cols], "gather_mod")
    mod = lax.dynamic_index_in_dim(mod_all, me, axis=1, keepdims=False).reshape(1, 3 * D)
    shift, scale, gate = mod[:, 0:D], mod[:, D:2 * D], mod[:, 2 * D:3 * D]

    grad_x, small, (r_win, r_pa, r_pb, r_wo) = _local_step(
        x[0], loss_target[0], shift, scale, gate, norm_w, conv_full, q_norm_w, k_norm_w,
        w_in[0].astype(BF16), [w_br_conv[0].astype(BF16), w_br_attn[0].astype(BF16), w_out[0].astype(BF16)], me_xyc)

    packed = jnp.concatenate(
        [small["dmod"], small["norm_w"], small["conv_w"].reshape(1, 3 * D), small["q_norm_w"], small["k_norm_w"],
         jnp.full((1, 128), small["loss"], F32)], axis=1)
    (packed_all,) = all_gather([packed], "gather_small")
    tot = sum_parts(packed_all)
    loss = tot[0, 7 * D + 2 * HD]
    dmod_all = packed_all[:, 0, 0:3 * D]
    g_b_ada = tot[:, 0:3 * D]
    g_norm_w = tot[:, 3 * D:4 * D]
    g_conv = lax.dynamic_slice(tot[:, 4 * D:7 * D].reshape(3, D), (0, me * 128), (3, 128))
    g_qn = tot[:, 7 * D:7 * D + HD]
    g_kn = tot[:, 7 * D + HD:7 * D + 2 * HD]
    g_w_ada = ada_bwd(c_all.T, lax.dynamic_slice(dmod_all, (0, me * ncol), (NDEV, ncol)))

    def upd(parts, w, m, v, name, rows):
        shape = w.shape
        w2, m2, v2 = (t.reshape(shape[-2:]) for t in (w, m, v))
        return [t.reshape(shape) for t in adamw(parts, w2, m2, v2, name, rows)]

    res = {
        "w_ada": upd(g_w_ada[None], w_ada, m_w_ada, v_w_ada, "adamw_w_ada", 256),
        "b_ada": upd(g_b_ada[None], b_ada, m_b_ada, v_b_ada, "adamw_b_ada", 1),
        "norm_w": upd(g_norm_w[None], norm_w, m_norm_w, v_norm_w, "adamw_norm_w", 1),
        "w_in": upd(r_win, w_in, m_w_in, v_w_in, "adamw_w_in", 128),
        "conv_w": upd(g_conv[None], conv_w, m_conv_w, v_conv_w, "adamw_conv_w", 3),
        "q_norm_w": upd(g_qn[None], q_norm_w, m_q_norm_w, v_q_norm_w, "adamw_q_norm_w", 1),
        "k_norm_w": upd(g_kn[None], k_norm_w, m_k_norm_w, v_k_norm_w, "adamw_k_norm_w", 1),
        "w_br_conv": upd(r_pa, w_br_conv, m_w_br_conv, v_w_br_conv, "adamw_w_br_conv", 128),
        "w_br_attn": upd(r_pb, w_br_attn, m_w_br_attn, v_w_br_attn, "adamw_w_br_attn", 512),
        "w_out": upd(r_wo, w_out, m_w_out, v_w_out, "adamw_w_out", 128),
    }
    names = ["w_ada", "b_ada", "norm_w", "w_in", "conv_w", "q_norm_w", "k_norm_w", "w_br_conv", "w_br_attn", "w_out"]
    return (loss, grad_x[None], *[res[n][0] for n in names], *[res[n][1] for n in names],
            *[res[n][2] for n in names], *[res[n][3] for n in names])
```

```python
import jax
import jax.numpy as jnp
from jax import lax
from jax.experimental import pallas as pl
from jax.experimental.pallas import tpu as pltpu

F32, BF16 = jnp.float32, jnp.bfloat16
D = 1024
NIN = 11264
NDEV = 8
SHARD = NIN // NDEV
HD = 64
NH = 8
QB = 128
CB = 512
CB_Q, CB_K, CB_V, CB_ZB = 8, 11, 14, 17
DILATIONS = (1, 4, 16)
EPS = 1e-6
NEG = -1e30
HALO = 16
ROWS = 32
LANES = 128
MESH = pl.DeviceIdType.MESH

ADAM_LR, ADAM_B1, ADAM_B2, ADAM_EPS, ADAM_WD, ADAM_STEP = 0.001, 0.9, 0.999, 1e-08, 0.01, 10

NT = (((1,), (1,)), ((), ()))
TN = (((0,), (0,)), ((), ()))


def _cp(sem, vmem_mb=48):
    return pltpu.CompilerParams(dimension_semantics=sem, vmem_limit_bytes=vmem_mb << 20)


def _silu(z):
    return z * jax.nn.sigmoid(z)


def _coords():
    return lax.axis_index("x"), lax.axis_index("y"), lax.axis_index("c")


FLIPS = [(fx, fy, fc) for fx in (0, 1) for fy in (0, 1) for fc in (0, 1)][1:]


def all_gather(arrs, name):
    n = len(arrs)

    def body(*refs):
        ins, outs = refs[:n], refs[n:2 * n]
        send_sems, recv_sems, local_sems = refs[2 * n:]
        me_xyc = _coords()
        me = _dev_index(me_xyc)
        peers = [_flip(me_xyc, f) for f in FLIPS]

        def copy(a, k, block):
            return pltpu.make_async_remote_copy(
                src_ref=ins[a], dst_ref=outs[a].at[block], send_sem=send_sems.at[a, k], recv_sem=recv_sems.at[a, k],
                device_id=peers[k], device_id_type=MESH)

        mine = [pltpu.make_async_copy(ins[a], outs[a].at[me], local_sems.at[a]) for a in range(n)]
        sends = [copy(a, k, me) for k in range(7) for a in range(n)]
        for cp in mine + sends:
            cp.start()
        for k in range(7):
            for a in range(n):
                copy(a, k, _dev_index(peers[k])).wait_recv()
        for cp in sends:
            cp.wait_send()
        for cp in mine:
            cp.wait()

    any_spec = pl.BlockSpec(memory_space=pl.ANY)
    return pl.pallas_call(
        body, name=name,
        out_shape=[jax.ShapeDtypeStruct((NDEV,) + a.shape, a.dtype) for a in arrs],
        in_specs=[any_spec] * n, out_specs=[any_spec] * n,
        scratch_shapes=[pltpu.SemaphoreType.DMA((n, 7)), pltpu.SemaphoreType.DMA((n, 7)),
                        pltpu.SemaphoreType.DMA((n,))],
    )(*arrs)


def _flip(dev, f):
    return tuple(1 - v if b else v for v, b in zip(dev, f))


def _dev_index(dev):
    return 4 * dev[0] + 2 * dev[1] + dev[2]


def _chip_order(x, y, c):
    xor = lambda a, b: a + b - 2 * a * b
    return [(xor(x, 1 - c), xor(y, c)), (xor(x, c), xor(y, 1 - c)), (1 - x, 1 - y)]


def gather_order(me_xyc):
    x, y, c = me_xyc
    chips = _chip_order(x, y, c)
    devs = [(x, y, c), (x, y, 1 - c), (*chips[0], c), (*chips[1], c),
            (*chips[1], 1 - c), (*chips[0], 1 - c), (*chips[2], c), (*chips[2], 1 - c)]
    return jnp.stack([_dev_index(d) for d in devs]).astype(jnp.int32)


def scatter_order(me_xyc):
    devs = [_flip(me_xyc, f) for f in FLIPS] + [me_xyc]
    return jnp.stack([_dev_index(d) for d in devs]).astype(jnp.int32)


def ada_fwd(c_all, w_ada, b_cols):
    def body(c_ref, w_ref, b_ref, o_ref):
        a = _silu(c_ref[...]).astype(BF16)
        o_ref[...] = jnp.dot(a, w_ref[...].astype(BF16), preferred_element_type=F32) + b_ref[...]

    return pl.pallas_call(body, name="ada_fwd",
                          out_shape=jax.ShapeDtypeStruct((NDEV, w_ada.shape[1]), F32))(c_all, w_ada, b_cols)


def ada_bwd(c_all_t, dmod_cols):
    def body(c_ref, d_ref, o_ref):
        at = _silu(c_ref[...])
        acc = at[:, 0:1] * d_ref[0:1, :]
        for b in range(1, NDEV):
            acc = acc + at[:, b:b + 1] * d_ref[b:b + 1, :]
        o_ref[...] = acc

    return pl.pallas_call(body, name="ada_bwd",
                          out_shape=jax.ShapeDtypeStruct((D, dmod_cols.shape[1]), F32))(c_all_t, dmod_cols)


def sum_parts(parts):
    def body(p_ref, o_ref):
        acc = p_ref[0]
        for b in range(1, NDEV):
            acc = acc + p_ref[b]
        o_ref[...] = acc

    return pl.pallas_call(body, name="sum_parts",
                          out_shape=jax.ShapeDtypeStruct(parts.shape[1:], F32))(parts)


def adamw(parts, w, m, v, name, rows):
    n, r, ccols = parts.shape

    def body(p_ref, w_ref, m_ref, v_ref, g_ref, d_ref, nm_ref, nv_ref):
        g = p_ref[0].astype(F32)
        for b in range(1, n):
            g = g + p_ref[b].astype(F32)
        nm = ADAM_B1 * m_ref[...] + (1.0 - ADAM_B1) * g
        nv = ADAM_B2 * v_ref[...] + (1.0 - ADAM_B2) * (g * g)
        g_ref[...] = g
        nm_ref[...] = nm
        nv_ref[...] = nv
        m_hat = nm / (1.0 - ADAM_B1 ** ADAM_STEP)
        v_hat = nv / (1.0 - ADAM_B2 ** ADAM_STEP)
        d_ref[...] = -ADAM_LR * (m_hat / (jnp.sqrt(v_hat) + ADAM_EPS) + ADAM_WD * w_ref[...])

    blk = pl.BlockSpec((rows, ccols), lambda i: (i, 0))
    out = jax.ShapeDtypeStruct((r, ccols), F32)
    return pl.pallas_call(
        body, name=name, grid=(r // rows,),
        in_specs=[pl.BlockSpec((n, rows, ccols), lambda i: (0, i, 0)), blk, blk, blk],
        out_specs=[blk] * 4, out_shape=[out] * 4, compiler_params=_cp(("parallel",)))(parts, w, m, v)


def norm_fwd(x, nw, scale, shift, tm):
    s = x.shape[0]

    def body(x_ref, nw_ref, sc_ref, sh_ref, h_ref, ht_ref):
        xf = x_ref[...]
        r = lax.rsqrt(jnp.mean(xf * xf, axis=-1, keepdims=True) + EPS)
        h = (xf * r * nw_ref[...]) * (1.0 + sc_ref[...]) + sh_ref[...]
        h_ref[...] = h.astype(BF16)
        ht_ref[...] = h.T.astype(BF16)

    vec = pl.BlockSpec((1, D), lambda i: (0, 0))
    return pl.pallas_call(
        body, name="norm_fwd", grid=(s // tm,),
        in_specs=[pl.BlockSpec((tm, D), lambda i: (i, 0)), vec, vec, vec],
        out_specs=[pl.BlockSpec((tm, D), lambda i: (i, 0)), pl.BlockSpec((D, tm), lambda i: (0, i))],
        out_shape=[jax.ShapeDtypeStruct((s, D), BF16), jax.ShapeDtypeStruct((D, s), BF16)],
        compiler_params=_cp(("parallel",)))(x, nw, scale, shift)


def norm_bwd(dh, x, dy, nw, scale, tm):
    s = x.shape[0]

    def body(dh_ref, x_ref, dy_ref, nw_ref, sc_ref, gx_ref, st_ref):
        @pl.when(pl.program_id(0) == 0)
        def _():
            st_ref[...] = jnp.zeros_like(st_ref)

        nw, sc1 = nw_ref[...], 1.0 + sc_ref[...]

        def chunk(c, sums):
            rows = pl.ds(pl.multiple_of(c * ROWS, ROWS), ROWS)
            xf, g = x_ref[rows, :], dh_ref[rows, :]
            r = lax.rsqrt(jnp.mean(xf * xf, axis=-1, keepdims=True) + EPS)
            xh = xf * r
            dn = g * sc1
            dxh = dn * nw
            gx_ref[rows, :] = dy_ref[rows, :] + r * (dxh - xh * jnp.mean(dxh * xh, axis=-1, keepdims=True))
            fold = lambda t: jnp.sum(t.reshape(ROWS // 8, 8, D), axis=0)
            return sums[0] + fold(g), sums[1] + fold(g * xh * nw), sums[2] + fold(dn * xh)

        zero = jnp.zeros((8, D), F32)
        sums = lax.fori_loop(0, tm // ROWS, chunk, (zero, zero, zero), unroll=True)
        for j in range(3):
            st_ref[j:j + 1, :] += jnp.sum(sums[j], axis=0, keepdims=True)

    vec = pl.BlockSpec((1, D), lambda i: (0, 0))
    row = pl.BlockSpec((tm, D), lambda i: (i, 0))
    return pl.pallas_call(
        body, name="norm_bwd", grid=(s // tm,),
        in_specs=[row, row, row, vec, vec],
        out_specs=[row, pl.BlockSpec((8, D), lambda i: (0, 0))],
        out_shape=[jax.ShapeDtypeStruct((s, D), F32), jax.ShapeDtypeStruct((8, D), F32)],
        compiler_params=_cp(("arbitrary",)))(dh, x, dy, nw, scale)


def proj_fwd_gather(h, w_shard, extras, order, tm):
    s = h.shape[0]
    ni = s // tm
    n = 1 + len(extras)
    mid = ni - 2

    def body(order_ref, h_ref, *refs):
        ins, o_ref, outs = refs[:n], refs[n], refs[n + 1:2 * n + 1]
        wbuf, send_sems, recv_sems, local_sems, load_sems = refs[2 * n + 1:]
        jj, i = pl.program_id(0), pl.program_id(1)
        x, y, c = _coords()
        me, sibling = (x, y, c), (x, y, 1 - c)
        chips = _chip_order(x, y, c)
        relayed = [(*chips[1], 1 - c), (*chips[0], 1 - c), (*chips[2], 1 - c)]

        def slot(a, dev):
            return outs[a].at[_dev_index(dev)]

        def copy(a, k, block, to, src=None):
            return pltpu.make_async_remote_copy(
                src_ref=slot(a, block) if src is None else src, dst_ref=slot(a, block),
                send_sem=send_sems.at[a, k], recv_sem=recv_sems.at[a, k], device_id=to, device_id_type=MESH)

        mine = [pltpu.make_async_copy(ins[a], slot(a, me), local_sems.at[a]) for a in range(n)]
        to_sibling = [copy(a, 0, me, sibling, src=ins[a]) for a in range(n)]
        to_chip = [[copy(a, 1 + j, me, (*chips[j], c), src=ins[a]) for a in range(n)] for j in range(2)]
        onward = [copy(a, 3, (*chips[1], c), (*chips[0], c)) for a in range(n)]
        passed = [[copy(a, 4 + j, (*ch, c), sibling) for a in range(n)] for j, ch in enumerate(chips)]
        sends = lambda a: [to_sibling[a], to_chip[0][a], to_chip[1][a], onward[a]] + [passed[j][a] for j in range(3)]

        def arrived(a, j):
            copy(a, 1 + j, (*chips[j], c), me).wait_recv()

        def load(row):
            return pltpu.make_async_copy(outs[0].at[order_ref[row]], wbuf.at[row % 2], load_sems.at[row % 2])

        @pl.when((jj == 0) & (i == 0))
        def _():
            for cp in mine:
                cp.start()
            to_sibling[0].start()
            to_chip[0][0].start()
            pltpu.make_async_copy(ins[0], wbuf.at[0], load_sems.at[0]).start()

        @pl.when((jj == 1) & (i == 0))
        def _():
            to_chip[1][0].start()

        @pl.when((jj == 4) & (i == 0))
        def _():
            for a in range(1, n):
                to_sibling[a].start()
                to_chip[0][a].start()
                to_chip[1][a].start()

        direct = {2: 0, 3: 1, 6: 2}
        relay = {4: 0, 5: 1, 7: 2}

        @pl.when((jj == 0) & (i == mid))
        def _():
            copy(0, 0, sibling, me).wait_recv()

        for row, j in direct.items():
            @pl.when((jj == row - 1) & (i == mid))
            def _(j=j):
                arrived(0, j)
                passed[j][0].start()
                if j == 1:
                    onward[0].start()

        for row, j in relay.items():
            @pl.when((jj == row - 1) & (i == mid))
            def _(j=j):
                copy(0, 4 + j, relayed[j], me).wait_recv()

        @pl.when((jj == NDEV - 1) & (i == 0))
        def _():
            for a in range(1, n):
                arrived(a, 1)
                onward[a].start()
                passed[1][a].start()
                arrived(a, 0)
                passed[0][a].start()

        @pl.when((jj < NDEV - 1) & (i == mid))
        def _():
            load(jj + 1).start()

        @pl.when(i == 0)
        def _():
            load(jj).wait()

        o_ref[...] = jnp.dot(h_ref[...], wbuf[jj % 2], preferred_element_type=F32).astype(BF16)

        @pl.when((jj == NDEV - 1) & (i == ni - 1))
        def _():
            for a in range(1, n):
                arrived(a, 2)
                passed[2][a].start()
            for a in range(1, n):
                copy(a, 0, sibling, me).wait_recv()
                for j in range(3):
                    copy(a, 4 + j, relayed[j], me).wait_recv()
            for a in range(n):
                mine[a].wait()
                for cp in sends(a):
                    cp.wait_send()

    any_spec = pl.BlockSpec(memory_space=pl.ANY)
    outs = pl.pallas_call(
        body, name="proj_fwd_gather",
        grid_spec=pltpu.PrefetchScalarGridSpec(
            num_scalar_prefetch=1, grid=(NDEV, ni),
            in_specs=[pl.BlockSpec((tm, D), lambda jj, i, o: (i, 0))] + [any_spec] * n,
            out_specs=[pl.BlockSpec((tm, SHARD), lambda jj, i, o: (i, o[jj]))] + [any_spec] * n,
            scratch_shapes=[pltpu.VMEM((2, D, SHARD), BF16), pltpu.SemaphoreType.DMA((n, 7)),
                            pltpu.SemaphoreType.DMA((n, 7)), pltpu.SemaphoreType.DMA((n,)),
                            pltpu.SemaphoreType.DMA((2,))]),
        out_shape=[jax.ShapeDtypeStruct((s, NIN), BF16), jax.ShapeDtypeStruct((NDEV, D, SHARD), BF16)]
                  + [jax.ShapeDtypeStruct((NDEV,) + e.shape, e.dtype) for e in extras],
        compiler_params=_cp(("arbitrary", "arbitrary")))(order, h, w_shard, *extras)
    return outs[0], outs[1], outs[2:]


def proj_bwd(ht, dproj, wg, smalls, order, tt):
    s = dproj.shape[0]
    nk = s // tt
    n = len(smalls)

    def body(order_ref, ht_ref, dp_ref, w_ref, *rest):
        small_in = rest[:n]
        dh_ref, gw_ref, rwin_ref = rest[n:n + 3]
        small_out = rest[n + 3:2 * n + 3]
        acc, stage, send_sems, recv_sems, local_sems, stage_sems = rest[2 * n + 3:]
        t, k = pl.program_id(0), pl.program_id(1)
        me_xyc = _coords()
        me = _dev_index(me_xyc)
        peers = [_flip(me_xyc, f) for f in FLIPS]

        def exchange(a, kf, src_arr, dst_arr):
            pid = _dev_index(peers[kf])
            mk = lambda dst: pltpu.make_async_remote_copy(
                src_ref=src_arr.at[pid], dst_ref=dst, send_sem=send_sems.at[a, kf], recv_sem=recv_sems.at[a, kf],
                device_id=peers[kf], device_id_type=MESH)
            return mk(dst_arr.at[me]), mk(dst_arr.at[pid])

        small_pairs = [exchange(1 + a, kf, small_in[a], small_out[a]) for kf in range(7) for a in range(n)]
        small_own = [pltpu.make_async_copy(small_in[a].at[me], small_out[a].at[me], local_sems.at[1 + a])
                     for a in range(n)]
        win_pairs = [exchange(0, kf, gw_ref, rwin_ref) for kf in range(7)]
        win_own = pltpu.make_async_copy(gw_ref.at[me], rwin_ref.at[me], local_sems.at[0])

        def to_hbm(jj):
            slab = me if jj == 7 else _dev_index(peers[jj])
            return pltpu.make_async_copy(stage.at[jj % 2], gw_ref.at[slab], stage_sems.at[jj % 2])

        @pl.when((t == 0) & (k == 0))
        def _():
            for cp in small_own:
                cp.start()
            for send, _ in small_pairs:
                send.start()

        @pl.when(t < NDEV)
        def _():
            p = jnp.dot(ht_ref[...], dp_ref[...], preferred_element_type=F32)

            @pl.when(k == 0)
            def _():
                acc[...] = p

            @pl.when(k > 0)
            def _():
                acc[...] += p

        for jj in range(NDEV):
            @pl.when((t == jj) & (k == nk - 1))
            def _(jj=jj):
                stage[jj % 2] = acc[...].astype(BF16)
                to_hbm(jj).start()

            @pl.when((t == jj + 1) & (k == 1))
            def _(jj=jj):
                to_hbm(jj).wait()
                if jj < 7:
                    win_pairs[jj][0].start()
                else:
                    win_own.start()

        @pl.when(t >= NDEV)
        def _():
            p = lax.dot_general(dp_ref[...], w_ref[...], NT, preferred_element_type=F32)

            @pl.when(k == 0)
            def _():
                dh_ref[...] = p

            @pl.when(k > 0)
            def _():
                dh_ref[...] += p

        @pl.when((t == 2 * NDEV - 1) & (k == nk - 1))
        def _():
            for _, recv in win_pairs + small_pairs:
                recv.wait_recv()
            for send, _ in win_pairs + small_pairs:
                send.wait_send()
            win_own.wait()
            for cp in small_own:
                cp.wait()

    any_spec = pl.BlockSpec(memory_space=pl.ANY)
    first = lambda t: t < NDEV
    outs = pl.pallas_call(
        body, name="proj_bwd",
        grid_spec=pltpu.PrefetchScalarGridSpec(
            num_scalar_prefetch=1, grid=(2 * NDEV, nk),
            in_specs=[pl.BlockSpec((D, tt), lambda t, k, o: (0, jnp.where(first(t), k, nk - 1))),
                      pl.BlockSpec((tt, SHARD), lambda t, k, o: (jnp.where(first(t), k, t - NDEV),
                                                                 jnp.where(first(t), o[jnp.minimum(t, NDEV - 1)], k))),
                      pl.BlockSpec((None, D, SHARD), lambda t, k, o: (jnp.where(first(t), 0, k), 0, 0))]
                     + [any_spec] * n,
            out_specs=[pl.BlockSpec((tt, D), lambda t, k, o: (jnp.where(first(t), 0, t - NDEV), 0))]
                      + [any_spec] * (2 + n),
            scratch_shapes=[pltpu.VMEM((D, SHARD), F32), pltpu.VMEM((2, D, SHARD), BF16),
                            pltpu.SemaphoreType.DMA((1 + n, 7)), pltpu.SemaphoreType.DMA((1 + n, 7)),
                            pltpu.SemaphoreType.DMA((1 + n,)), pltpu.SemaphoreType.DMA((2,))]),
        out_shape=[jax.ShapeDtypeStruct((s, D), F32), jax.ShapeDtypeStruct((NDEV, D, SHARD), BF16),
                   jax.ShapeDtypeStruct((NDEV, D, SHARD), BF16)]
                  + [jax.ShapeDtypeStruct(a.shape, a.dtype) for a in smalls],
        compiler_params=_cp(("arbitrary", "arbitrary"), 56))(order, ht, dproj, wg, *smalls)
    return outs[0], outs[2], outs[3:]


def matmul_tn(a, b, name, tk):
    s, m = a.shape
    n = b.shape[1]
    nk = s // tk

    def body(a_ref, b_ref, o_ref, acc_ref):
        k = pl.program_id(0)
        p = lax.dot_general(a_ref[...], b_ref[...], TN, preferred_element_type=F32)

        @pl.when(k == 0)
        def _():
            acc_ref[...] = p

        @pl.when(k > 0)
        def _():
            acc_ref[...] += p

        @pl.when(k == nk - 1)
        def _():
            o_ref[...] = acc_ref[...].astype(BF16)

    return pl.pallas_call(
        body, name=name, grid=(nk,),
        in_specs=[pl.BlockSpec((tk, m), lambda k: (k, 0)), pl.BlockSpec((tk, n), lambda k: (k, 0))],
        out_specs=pl.BlockSpec((m, n), lambda k: (0, 0)),
        out_shape=jax.ShapeDtypeStruct((m, n), BF16),
        scratch_shapes=[pltpu.VMEM((m, n), F32)],
        compiler_params=_cp(("arbitrary",)))(a, b)


def _head_matrices():
    lane = lax.broadcasted_iota(jnp.int32, (CB, CB), 0)
    col = lax.broadcasted_iota(jnp.int32, (CB, CB), 1)
    same = (lane // HD == col // HD).astype(BF16)
    lane_c = lax.broadcasted_iota(jnp.int32, (CB, LANES), 0)
    col_c = lax.broadcasted_iota(jnp.int32, (CB, LANES), 1)
    total = (lane_c // HD == col_c).astype(BF16)
    lane_e = lax.broadcasted_iota(jnp.int32, (LANES, CB), 0)
    col_e = lax.broadcasted_iota(jnp.int32, (LANES, CB), 1)
    expand = (lane_e == col_e // HD).astype(BF16)
    return same, total, expand


def _head_sum(x, m_ref):
    return jnp.dot(x.astype(BF16), m_ref[...], preferred_element_type=F32)


def _dot_hilo(x, m_ref):
    hi = x.astype(BF16)
    lo = (x - hi.astype(F32)).astype(BF16)
    return (jnp.dot(hi, m_ref[...], preferred_element_type=F32)
            + jnp.dot(lo, m_ref[...], preferred_element_type=F32))


def _to_residue_major(val, buf, out_ref, dil):
    rows = out_ref.shape[1]
    for k in range(val.shape[1] // LANES):
        lanes = slice(k * LANES, (k + 1) * LANES)
        buf[k] = val[:, lanes]
        for r in range(dil):
            out_ref[r, :, lanes] = buf.at[k][pl.ds(r, rows, stride=dil), :].astype(out_ref.dtype)


def _from_residue_major(ref, buf, dil):
    if dil == 1:
        return ref[0].astype(F32)
    rows, chunks = ref.shape[1], ref.shape[2] // LANES
    for k in range(chunks):
        for r in range(dil):
            buf.at[k][pl.ds(r, rows, stride=dil), :] = ref[r, :, k * LANES:(k + 1) * LANES].astype(F32)
    return jnp.concatenate([buf[k] for k in range(chunks)], axis=1)


def qkv_prep(proj, qw8, kw8, same, tm):
    s = proj.shape[0]
    items = []
    for g, d in enumerate(DILATIONS):
        items += [(g, "q", CB_Q + g, d), (g, "k", CB_K + g, d)] + ([(g, "v", CB_V + g, d)] if d > 1 else [])
    n = len(items)

    def body(*refs):
        ins, (qw_ref, kw_ref, same_ref), outs, buf = refs[:n], refs[n:n + 3], refs[n + 3:2 * n + 3], refs[-1]
        for idx, (_, kind, _, dil) in enumerate(items):
            val = ins[idx][...].astype(F32)
            if kind != "v":
                r = lax.rsqrt(_head_sum(val * val, same_ref) * (1.0 / HD) + EPS)
                val = val * r * (qw_ref if kind == "q" else kw_ref)[...]
            if dil == 1:
                outs[idx][0] = val.astype(BF16)
            else:
                _to_residue_major(val, buf, outs[idx], dil)

    full = lambda a: pl.BlockSpec(a.shape, lambda i: (0, 0))
    outs = pl.pallas_call(
        body, name="qkv_prep", grid=(s // tm,),
        in_specs=[pl.BlockSpec((tm, CB), lambda i, cb=cb: (i, cb)) for _, _, cb, _ in items]
                 + [full(qw8), full(kw8), full(same)],
        out_specs=[pl.BlockSpec((d, tm // d, CB), lambda i: (0, i, 0)) for _, _, _, d in items],
        out_shape=[jax.ShapeDtypeStruct((d, s // d, CB), BF16) for _, _, _, d in items],
        scratch_shapes=[pltpu.VMEM((CB // LANES, tm, LANES), F32)],
        compiler_params=_cp(("parallel",)))(*([proj] * n), qw8 * (HD ** -0.5), kw8, same)
    srcs = [[None, None, (proj, CB_V + g)] for g in range(len(DILATIONS))]
    for (g, kind, _, _), o in zip(items, outs):
        srcs[g]["qkv".index(kind)] = (o.reshape(s, CB), 0)
    return srcs


def stats_prep(da, lc, dc, g, dil, tm):
    s = da.shape[0]
    rows = tm // dil

    def body(da_ref, lc_ref, dc_ref, dap_ref, lcp_ref, dcp_ref, lt_ref, dt_ref, buf):
        if dil == 1:
            dap_ref[0] = da_ref[...]
        else:
            _to_residue_major(da_ref[...].astype(F32), buf, dap_ref, dil)
        for src, dst, dst_t in ((lc_ref, lcp_ref, lt_ref), (dc_ref, dcp_ref, dt_ref)):
            buf[0] = src[...]
            for r in range(dil):
                piece = buf.at[0][pl.ds(r, rows, stride=dil), :] if dil > 1 else buf[0]
                dst[r] = piece
                dst_t[r] = piece.T[0:NH, :]

    row = lambda w: pl.BlockSpec((tm, w), lambda i: (i, 0))
    rm = lambda w: pl.BlockSpec((dil, rows, w), lambda i: (0, i, 0))
    tr = pl.BlockSpec((dil, NH, rows), lambda i: (0, 0, i))
    length = s // dil
    dap, lcp, dcp, lt, dt = pl.pallas_call(
        body, name=f"stats_prep_g{g}", grid=(s // tm,),
        in_specs=[row(CB), row(LANES), row(LANES)],
        out_specs=[rm(CB), rm(LANES), rm(LANES), tr, tr],
        out_shape=[jax.ShapeDtypeStruct((dil, length, CB), BF16)]
                  + [jax.ShapeDtypeStruct((dil, length, LANES), F32)] * 2
                  + [jax.ShapeDtypeStruct((dil, NH, length), F32)] * 2,
        scratch_shapes=[pltpu.VMEM((CB // LANES, tm, LANES), F32)],
        compiler_params=_cp(("parallel",)))(da, lc, dc)
    return (dap.reshape(s, CB), lcp.reshape(s, LANES), dcp.reshape(s, LANES),
            lt.reshape(dil * NH, length), dt.reshape(dil * NH, length))


def qkv_grads_to_dproj(dproj, proj, grads, qw8, kw8, same, tm):
    s = dproj.shape[0]
    ni = s // tm
    flat = [(t.reshape(d, s // d, CB), d, kind, 3 * kind + g)
            for g, d in enumerate(DILATIONS) for kind, t in enumerate(grads[g])]
    nf = len(flat)
    nraw = 2 * len(DILATIONS)

    def body(*refs):
        dp_hbm, raws, ins = refs[nraw + nf + 4], refs[1:1 + nraw], refs[1 + nraw:1 + nraw + nf]
        qw_ref, kw_ref, same_ref = refs[1 + nraw + nf:4 + nraw + nf]
        gw_ref, stage, buf, sems = refs[5 + nraw + nf:]
        i = pl.program_id(0)
        slot = i % 2

        def slab(step, sl):
            return pltpu.make_async_copy(
                stage.at[sl], dp_hbm.at[pl.ds(pl.multiple_of(step * tm, tm), tm), pl.ds(CB_Q * CB, 9 * CB)],
                sems.at[sl])

        @pl.when(i == 0)
        def _():
            gw_ref[...] = jnp.zeros_like(gw_ref)

        @pl.when(i >= 2)
        def _():
            slab(i - 2, slot).wait()

        for ref, (_, d, kind, jj) in zip(ins, flat):
            cols = slice(jj * CB, (jj + 1) * CB)
            dn = _from_residue_major(ref, buf, d)
            if kind == 2:
                stage[slot, :, cols] = dn.astype(BF16)
                continue
            t = raws[jj][...].astype(F32)
            r = lax.rsqrt(_head_sum(t * t, same_ref) * (1.0 / HD) + EPS)
            xh = t * r
            gw_ref[kind:kind + 1, :] += jnp.sum(dn * xh, axis=0, keepdims=True)
            dxh = dn * (qw_ref if kind == 0 else kw_ref)[...]
            mean = _head_sum(dxh * xh, same_ref) * (1.0 / HD)
            stage[slot, :, cols] = (r * (dxh - xh * mean)).astype(BF16)
        slab(i, slot).start()

        @pl.when(i == ni - 1)
        def _():
            slab(i - 1, 1 - slot).wait()
            slab(i, slot).wait()

    full = lambda a: pl.BlockSpec(a.shape, lambda i: (0, 0))
    any_spec = pl.BlockSpec(memory_space=pl.ANY)
    return pl.pallas_call(
        body, name="qkv_grads_to_dproj", grid=(ni,),
        in_specs=[any_spec] + [pl.BlockSpec((tm, CB), lambda i, jb=jb: (i, CB_Q + jb)) for jb in range(nraw)]
                 + [pl.BlockSpec((d, tm // d, CB), lambda i: (0, i, 0)) for _, d, _, _ in flat]
                 + [full(qw8), full(kw8), full(same)],
        out_specs=[any_spec, pl.BlockSpec((8, CB), lambda i: (0, 0))],
        out_shape=[jax.ShapeDtypeStruct((s, NIN), BF16), jax.ShapeDtypeStruct((8, CB), F32)],
        input_output_aliases={0: 0},
        scratch_shapes=[pltpu.VMEM((2, tm, 9 * CB), BF16), pltpu.VMEM((CB // LANES, tm, LANES), F32),
                        pltpu.SemaphoreType.DMA((2,))],
        compiler_params=_cp(("arbitrary",)))(
            dproj, *([proj] * nraw), *[t for t, _, _, _ in flat], qw8, kw8, same)


def _lane_lo():
    return lax.broadcasted_iota(jnp.int32, (1, 2 * HD), 1) < HD


def _stack_heads(t, lo):
    zero = jnp.zeros_like(t)
    return jnp.concatenate([jnp.where(lo, t, zero), jnp.where(lo, zero, t)], axis=0)


def _masks(other_ok):
    qi = lax.broadcasted_iota(jnp.int32, (QB, QB), 0)
    kj = lax.broadcasted_iota(jnp.int32, (QB, QB), 1)
    return (kj >= qi) & other_ok, kj <= qi


SUB = 4


def _attn_specs(nb, dil):
    steps = nb // SUB
    main = lambda cb, w=CB: pl.BlockSpec((SUB * QB, w), lambda r, s: (r * steps + s, cb))
    prev = lambda cb: pl.BlockSpec((QB, CB), lambda r, s: (jnp.maximum(r * nb + SUB * s - 1, 0), cb))
    nxt = lambda cb: pl.BlockSpec((QB, CB), lambda r, s: (jnp.minimum(r * nb + SUB * (s + 1), dil * nb - 1), cb))
    return main, prev, nxt


def attn_fwd(q_src, k_src, v_src, g, dil):
    s = q_src[0].shape[0]
    nb = s // dil // QB
    main, prev, _ = _attn_specs(nb, dil)

    def body(q_ref, kp_ref, k_ref, vp_ref, v_ref, o_ref, l_ref, kbuf, vbuf):
        step = pl.program_id(1)
        kbuf[0:QB], kbuf[QB:] = kp_ref[...], k_ref[...]
        vbuf[0:QB], vbuf[QB:] = vp_ref[...], v_ref[...]
        lo = _lane_lo()
        head_lane = lax.broadcasted_iota(jnp.int32, (1, LANES), 1)

        def block(j, carry):
            r0 = pl.multiple_of(j * QB, QB)
            rows, krows = pl.ds(r0, QB), pl.ds(r0, 2 * QB)
            m_prev, m_cur = _masks(step * SUB + j > 0)
            mask = jnp.concatenate([m_prev, m_cur], axis=1)
            mask = jnp.concatenate([mask, mask], axis=0)
            lses = jnp.zeros((QB, LANES), F32)
            for i in range(NH // 2):
                sl = slice(2 * HD * i, 2 * HD * (i + 1))
                qs, ks, vv = q_ref[rows, sl], kbuf[krows, sl], vbuf[krows, sl]
                sc = lax.dot_general(_stack_heads(qs, lo), ks, NT, preferred_element_type=F32)
                sc = jnp.where(mask, sc, NEG)
                mx = jnp.max(sc, axis=-1, keepdims=True)
                p = jnp.exp(sc - mx)
                den = jnp.sum(p, axis=-1, keepdims=True)
                o = jnp.dot(p.astype(BF16), vv, preferred_element_type=F32) * (1.0 / den)
                lse = mx + jnp.log(den)
                o_ref[rows, sl] = jnp.where(lo, o[:QB], o[QB:]).astype(BF16)
                lses = jnp.where(head_lane == 2 * i, lse[:QB], jnp.where(head_lane == 2 * i + 1, lse[QB:], lses))
            l_ref[rows, :] = lses
            return carry

        lax.fori_loop(0, SUB, block, 0, unroll=True)

    return pl.pallas_call(
        body, name=f"attn_fwd_g{g}", grid=(dil, nb // SUB),
        in_specs=[main(q_src[1]), prev(k_src[1]), main(k_src[1]), prev(v_src[1]), main(v_src[1])],
        out_specs=[main(0), main(0, LANES)],
        out_shape=[jax.ShapeDtypeStruct((s, CB), BF16), jax.ShapeDtypeStruct((s, LANES), F32)],
        scratch_shapes=[pltpu.VMEM(((SUB + 1) * QB, CB), BF16)] * 2,
        compiler_params=_cp(("parallel", "parallel")))(q_src[0], k_src[0], k_src[0], v_src[0], v_src[0])


def attn_bwd_q(q_src, k_src, v_src, da, lc, dc, g, dil):
    s = q_src[0].shape[0]
    nb = s // dil // QB
    main, prev, _ = _attn_specs(nb, dil)

    def body(q_ref, kp_ref, k_ref, vp_ref, v_ref, da_ref, l_ref, d_ref, dq_ref, kbuf, vbuf):
        step = pl.program_id(1)
        kbuf[0:QB], kbuf[QB:] = kp_ref[...], k_ref[...]
        vbuf[0:QB], vbuf[QB:] = vp_ref[...], v_ref[...]
        lo = _lane_lo()

        def block(j, carry):
            r0 = pl.multiple_of(j * QB, QB)
            rows, krows = pl.ds(r0, QB), pl.ds(r0, 2 * QB)
            m_prev, m_cur = _masks(step * SUB + j > 0)
            mask = jnp.concatenate([m_prev, m_cur], axis=1)
            mask = jnp.concatenate([mask, mask], axis=0)
            lcols, dcols = l_ref[rows, :], d_ref[rows, :]
            for i in range(NH // 2):
                sl = slice(2 * HD * i, 2 * HD * (i + 1))
                qs, ks, vv, da2 = q_ref[rows, sl], kbuf[krows, sl], vbuf[krows, sl], da_ref[rows, sl]
                pair = lambda t: jnp.concatenate([t[:, 2 * i:2 * i + 1], t[:, 2 * i + 1:2 * i + 2]], axis=0)
                sc = lax.dot_general(_stack_heads(qs, lo), ks, NT, preferred_element_type=F32)
                sc = jnp.where(mask, sc, NEG)
                p = jnp.exp(sc - pair(lcols))
                dp = lax.dot_general(_stack_heads(da2, lo), vv, NT, preferred_element_type=F32)
                ds = p * (dp - pair(dcols))
                dq = jnp.dot(ds.astype(BF16), ks, preferred_element_type=F32)
                dq_ref[rows, sl] = (jnp.where(lo, dq[:QB], dq[QB:]) * (HD ** -0.5)).astype(BF16)
            return carry

        lax.fori_loop(0, SUB, block, 0, unroll=True)

    return pl.pallas_call(
        body, name=f"attn_bwd_q_g{g}", grid=(dil, nb // SUB),
        in_specs=[main(q_src[1]), prev(k_src[1]), main(k_src[1]), prev(v_src[1]), main(v_src[1]),
                  main(0), main(0, LANES), main(0, LANES)],
        out_specs=main(0), out_shape=jax.ShapeDtypeStruct((s, CB), BF16),
        scratch_shapes=[pltpu.VMEM(((SUB + 1) * QB, CB), BF16)] * 2,
        compiler_params=_cp(("parallel", "parallel")))(
            q_src[0], k_src[0], k_src[0], v_src[0], v_src[0], da, lc, dc)


def attn_bwd_kv(q_src, k_src, v_src, da, lt, dt, g, dil):
    s = q_src[0].shape[0]
    nb = s // dil // QB
    main, _, nxt = _attn_specs(nb, dil)

    def body(k_ref, v_ref, q_ref, qn_ref, da_ref, dan_ref, l_ref, ln_ref, d_ref, dn_ref, dk_ref, dv_ref,
             qbuf, dabuf, lbuf, dbuf):
        step = pl.program_id(1)
        qbuf[0:SUB * QB], qbuf[SUB * QB:] = q_ref[...], qn_ref[...]
        dabuf[0:SUB * QB], dabuf[SUB * QB:] = da_ref[...], dan_ref[...]
        for c in range(SUB):
            lbuf[c], dbuf[c] = l_ref[:, c * QB:(c + 1) * QB], d_ref[:, c * QB:(c + 1) * QB]
        lbuf[SUB], dbuf[SUB] = ln_ref[...], dn_ref[...]
        lo = _lane_lo()
        kj = lax.broadcasted_iota(jnp.int32, (QB, QB), 0)
        qi = lax.broadcasted_iota(jnp.int32, (QB, QB), 1)

        def block(j, carry):
            r0 = pl.multiple_of(j * QB, QB)
            rows, qrows = pl.ds(r0, QB), pl.ds(r0, 2 * QB)
            mask = jnp.concatenate([kj <= qi, (kj >= qi) & (step * SUB + j < nb - 1)], axis=1)
            mask = jnp.concatenate([mask, mask], axis=1)
            lrow = jnp.concatenate([lbuf[j], lbuf[j + 1]], axis=1)
            drow = jnp.concatenate([dbuf[j], dbuf[j + 1]], axis=1)
            for i in range(NH // 2):
                sl = slice(2 * HD * i, 2 * HD * (i + 1))
                q2, da2 = _stack_heads(qbuf[qrows, sl], lo), _stack_heads(dabuf[qrows, sl], lo)
                ks, vv = k_ref[rows, sl], v_ref[rows, sl]
                pair = lambda t: jnp.concatenate([t[2 * i:2 * i + 1, :], t[2 * i + 1:2 * i + 2, :]], axis=1)
                sc = lax.dot_general(ks, q2, NT, preferred_element_type=F32)
                sc = jnp.where(mask, sc, NEG)
                p = jnp.exp(sc - pair(lrow))
                dp = lax.dot_general(vv, da2, NT, preferred_element_type=F32)
                ds = p * (dp - pair(drow))
                dv_ref[rows, sl] = jnp.dot(p.astype(BF16), da2, preferred_element_type=F32).astype(BF16)
                dk_ref[rows, sl] = jnp.dot(ds.astype(BF16), q2, preferred_element_type=F32).astype(BF16)
            return carry

        lax.fori_loop(0, SUB, block, 0, unroll=True)

    steps = nb // SUB
    t_main = pl.BlockSpec((NH, SUB * QB), lambda r, s: (r, s))
    t_nxt = pl.BlockSpec((NH, QB), lambda r, s: (r, jnp.minimum(SUB * (s + 1), nb - 1)))
    out = jax.ShapeDtypeStruct((s, CB), BF16)
    return pl.pallas_call(
        body, name=f"attn_bwd_kv_g{g}", grid=(dil, steps),
        in_specs=[main(k_src[1]), main(v_src[1]), main(q_src[1]), nxt(q_src[1]),
                  main(0), nxt(0), t_main, t_nxt, t_main, t_nxt],
        out_specs=[main(0), main(0)], out_shape=[out, out],
        scratch_shapes=[pltpu.VMEM(((SUB + 1) * QB, CB), BF16)] * 2 + [pltpu.VMEM((SUB + 1, NH, QB), F32)] * 2,
        compiler_params=_cp(("parallel", "parallel")))(
            k_src[0], v_src[0], q_src[0], q_src[0], da, da, lt, lt, dt, dt)


def _conv_taps(u, u_prev, first):
    tm = u.shape[0]
    row = lax.broadcasted_iota(jnp.int32, (tm, 1), 0)
    up = jnp.where(first, 0.0, u_prev)
    u1 = jnp.where(row == 0, up[HALO - 1:HALO, :], pltpu.roll(u, 1, 0))
    u2 = jnp.where(row == 0, up[HALO - 2:HALO - 1, :],
                   jnp.where(row == 1, up[HALO - 1:HALO, :], pltpu.roll(u, 2, 0)))
    return u1, u2


def mid_fwd(proj, o_g, lse_g, conv_w, expand, tm):
    s = proj.shape[0]
    hb = tm // HALO

    def body(ba_ref, ca_ref, xa_ref, za_ref, cah_ref, xah_ref, zb_ref,
             o0, o1, o2, l0, l1, l2, w_ref, exp_ref, ya_ref, yb_ref, at_ref, lc_ref, buf_o, buf_l):
        first = pl.program_id(0) == 0
        u = ca_ref[...].astype(F32) * xa_ref[...].astype(F32)
        u1, u2 = _conv_taps(u, cah_ref[...].astype(F32) * xah_ref[...].astype(F32), first)
        conv = w_ref[0:1, :] * u2 + w_ref[1:2, :] * u1 + w_ref[2:3, :] * u
        ya_ref[...] = (ba_ref[...].astype(F32) * conv * _silu(za_ref[...].astype(F32))).astype(BF16)
        ls = [_from_residue_major(l, buf_l.at[g], d) for g, (l, d) in enumerate(zip((l0, l1, l2), DILATIONS))]
        mx = jnp.maximum(jnp.maximum(ls[0], ls[1]), ls[2])
        es = [jnp.exp(l - mx) for l in ls]
        den = es[0] + es[1] + es[2]
        attn = jnp.zeros((tm, CB), F32)
        for e, o, d in zip(es, (o0, o1, o2), DILATIONS):
            attn = attn + _dot_hilo(e / den, exp_ref) * _from_residue_major(o, buf_o, d)
        at_ref[...] = attn
        lc_ref[...] = mx + jnp.log(den)
        yb_ref[...] = (attn * _silu(zb_ref[...].astype(F32))).astype(BF16)

    col = lambda j: pl.BlockSpec((tm, D), lambda i: (i, j))
    halo = lambda j: pl.BlockSpec((HALO, D), lambda i: (jnp.maximum(i * hb - 1, 0), j))
    loc = lambda w: pl.BlockSpec((tm, w), lambda i: (i, 0))
    rm = lambda w: [pl.BlockSpec((d, tm // d, w), lambda i: (0, i, 0)) for d in DILATIONS]
    rm_view = lambda ts, w: [t.reshape(d, s // d, w) for t, d in zip(ts, DILATIONS)]
    return pl.pallas_call(
        body, name="mid_fwd", grid=(s // tm,),
        in_specs=[col(0), col(1), col(2), col(3), halo(1), halo(2),
                  pl.BlockSpec((tm, CB), lambda i: (i, CB_ZB))] + rm(CB) + rm(LANES)
                 + [pl.BlockSpec((3, D), lambda i: (0, 0)), pl.BlockSpec(expand.shape, lambda i: (0, 0))],
        out_specs=[loc(D), loc(CB), loc(CB), loc(LANES)],
        out_shape=[jax.ShapeDtypeStruct((s, D), BF16), jax.ShapeDtypeStruct((s, CB), BF16),
                   jax.ShapeDtypeStruct((s, CB), F32), jax.ShapeDtypeStruct((s, LANES), F32)],
        scratch_shapes=[pltpu.VMEM((CB // LANES, tm, LANES), F32), pltpu.VMEM((3, 1, tm, LANES), F32)],
        compiler_params=_cp(("parallel",)))(
            proj, proj, proj, proj, proj, proj, proj, *rm_view(o_g, CB), *rm_view(lse_g, LANES), conv_w, expand)


def mid_bwd(dproj, proj, dya, conv_w, tm):
    s = proj.shape[0]
    hb = tm // HALO
    nblk = s // tm
    last_h = s // HALO - 1

    def body(_, ba_ref, ca_ref, xa_ref, za_ref, cah_ref, xah_ref, ban_ref, zan_ref, dy_ref, dyn_ref, w_ref,
             o_ref, gw_ref):
        i = pl.program_id(0)
        ba, ca, xa, za = (t[...].astype(F32) for t in (ba_ref, ca_ref, xa_ref, za_ref))
        u = ca * xa
        u1, u2 = _conv_taps(u, cah_ref[...].astype(F32) * xah_ref[...].astype(F32), i == 0)
        w0, w1, w2 = w_ref[0:1, :], w_ref[1:2, :], w_ref[2:3, :]
        conv = w0 * u2 + w1 * u1 + w2 * u
        sg = jax.nn.sigmoid(za)
        sz = za * sg
        dy = dy_ref[...].astype(F32)
        dconv = dy * ba * sz
        dcn = dyn_ref[...].astype(F32) * ban_ref[...].astype(F32) * _silu(zan_ref[...].astype(F32))
        dcn = jnp.where(i == nblk - 1, 0.0, dcn)
        row = lax.broadcasted_iota(jnp.int32, (tm, 1), 0)
        d1 = jnp.where(row == tm - 1, dcn[0:1, :], pltpu.roll(dconv, tm - 1, 0))
        d2 = jnp.where(row == tm - 2, dcn[0:1, :],
                       jnp.where(row == tm - 1, dcn[1:2, :], pltpu.roll(dconv, tm - 2, 0)))
        du = w2 * dconv + w1 * d1 + w0 * d2
        o_ref[:, 0:D] = (dy * conv * sz).astype(BF16)
        o_ref[:, D:2 * D] = (du * xa).astype(BF16)
        o_ref[:, 2 * D:3 * D] = (du * ca).astype(BF16)
        o_ref[:, 3 * D:4 * D] = (dy * ba * conv * (sg * (1.0 + za * (1.0 - sg)))).astype(BF16)

        @pl.when(i == 0)
        def _():
            gw_ref[...] = jnp.zeros_like(gw_ref)

        gw_ref[0:1, :] += jnp.sum(dconv * u2, axis=0, keepdims=True)
        gw_ref[1:2, :] += jnp.sum(dconv * u1, axis=0, keepdims=True)
        gw_ref[2:3, :] += jnp.sum(dconv * u, axis=0, keepdims=True)

    col = lambda j: pl.BlockSpec((tm, D), lambda i: (i, j))
    halo_prev = lambda j: pl.BlockSpec((HALO, D), lambda i: (jnp.maximum(i * hb - 1, 0), j))
    halo_next = lambda j: pl.BlockSpec((HALO, D), lambda i: (jnp.minimum((i + 1) * hb, last_h), j))
    return pl.pallas_call(
        body, name="mid_bwd", grid=(nblk,),
        in_specs=[pl.BlockSpec(memory_space=pl.ANY), col(0), col(1), col(2), col(3),
                  halo_prev(1), halo_prev(2), halo_next(0), halo_next(3),
                  pl.BlockSpec((tm, D), lambda i: (i, 0)), halo_next(0),
                  pl.BlockSpec((3, D), lambda i: (0, 0))],
        out_specs=[pl.BlockSpec((tm, 4 * D), lambda i: (i, 0)), pl.BlockSpec((8, D), lambda i: (0, 0))],
        out_shape=[jax.ShapeDtypeStruct((s, NIN), BF16), jax.ShapeDtypeStruct((8, D), F32)],
        input_output_aliases={0: 0},
        compiler_params=_cp(("arbitrary",)))(dproj, proj, proj, proj, proj, proj, proj, proj, proj, dya, dya, conv_w)


def tail(proj, ya, yb, attn, x, target, gate, pa_w, pb_w, wo_w, total, tm):
    s = proj.shape[0]
    ni = s // tm
    ncol = NIN - CB_ZB * CB

    def body(ya_ref, yb_ref, ga_ref, gb_ref, zb_ref, at_ref, x_ref, t_ref, gate_ref, pa_ref, pb_ref, wo_ref,
             tot_ref, dp_hbm, dy_ref, dya_ref, da_ref, dc_ref, mg_ref, do_ref, dpa_ref, dpb_ref, st_ref,
             stage, sems):
        i = pl.program_id(0)
        slot = i % 2

        def slab(step, sl):
            return pltpu.make_async_copy(
                stage.at[sl], dp_hbm.at[pl.ds(pl.multiple_of(step * tm, tm), tm), pl.ds(CB_ZB * CB, ncol)],
                sems.at[sl])

        @pl.when(i == 0)
        def _():
            st_ref[...] = jnp.zeros_like(st_ref)

        @pl.when(i >= 2)
        def _():
            slab(i - 2, slot).wait()

        gate_v = gate_ref[...]
        pa = jnp.dot(ya_ref[...], pa_ref[...], preferred_element_type=F32)
        pb = jnp.dot(yb_ref[...], pb_ref[...], preferred_element_type=F32)
        sa = jax.nn.sigmoid(ga_ref[...].astype(F32))
        sb = jax.nn.sigmoid(gb_ref[...].astype(F32))
        merged = (sa * pa + sb * pb).astype(BF16)
        mg_ref[...] = merged
        out = jnp.dot(merged, wo_ref[...], preferred_element_type=F32)
        err = x_ref[...] + gate_v * out - t_ref[...]
        dy = err * (1.0 / D)
        dy_ref[...] = dy
        st_ref[0:1, :] += jnp.sum(dy * out, axis=0, keepdims=True)
        st_ref[1:2, :] += jnp.sum(err * err, axis=0, keepdims=True)
        dout = (gate_v * dy).astype(BF16)
        do_ref[...] = dout
        dmg = lax.dot_general(dout, wo_ref[...], NT, preferred_element_type=F32)
        dpa = (dmg * sa).astype(BF16)
        dpb = (dmg * sb).astype(BF16)
        dpa_ref[...] = dpa
        dpb_ref[...] = dpb
        stage[slot, :, CB:CB + D] = (dmg * pa * sa * (1.0 - sa)).astype(BF16)
        stage[slot, :, CB + D:] = (dmg * pb * sb * (1.0 - sb)).astype(BF16)
        dya_ref[...] = lax.dot_general(dpa, pa_ref[...], NT, preferred_element_type=F32).astype(BF16)
        dyb = lax.dot_general(dpb, pb_ref[...], NT, preferred_element_type=F32)
        zb = zb_ref[...].astype(F32)
        sg = jax.nn.sigmoid(zb)
        attn_v = at_ref[...]
        dattn = dyb * (zb * sg)
        da_ref[...] = dattn.astype(BF16)
        stage[slot, :, 0:CB] = (dyb * attn_v * (sg * (1.0 + zb * (1.0 - sg)))).astype(BF16)
        dc_ref[...] = _dot_hilo(dattn * attn_v, tot_ref)

        slab(i, slot).start()

        @pl.when(i == ni - 1)
        def _():
            slab(i - 1, 1 - slot).wait()
            slab(i, slot).wait()

    row = lambda w: pl.BlockSpec((tm, w), lambda i: (i, 0))
    pcol = lambda w, jb: pl.BlockSpec((tm, w), lambda i: (i, jb))
    full = lambda a: pl.BlockSpec(a.shape, lambda i: (0, 0))
    return pl.pallas_call(
        body, name="tail", grid=(ni,),
        in_specs=[row(D), row(CB), pcol(D, 9), pcol(D, 10), pcol(CB, CB_ZB), row(CB), row(D), row(D),
                  pl.BlockSpec((1, D), lambda i: (0, 0)), full(pa_w), full(pb_w), full(wo_w), full(total)],
        out_specs=[pl.BlockSpec(memory_space=pl.ANY),
                   row(D), row(D), row(CB), row(LANES), row(D), row(D), row(D), row(D),
                   pl.BlockSpec((8, D), lambda i: (0, 0))],
        out_shape=[jax.ShapeDtypeStruct((s, NIN), BF16), jax.ShapeDtypeStruct((s, D), F32),
                   jax.ShapeDtypeStruct((s, D), BF16), jax.ShapeDtypeStruct((s, CB), BF16),
                   jax.ShapeDtypeStruct((s, LANES), F32)] + [jax.ShapeDtypeStruct((s, D), BF16)] * 4
                  + [jax.ShapeDtypeStruct((8, D), F32)],
        scratch_shapes=[pltpu.VMEM((2, tm, ncol), BF16), pltpu.SemaphoreType.DMA((2,))],
        compiler_params=_cp(("arbitrary",), 56))(
            ya, yb, proj, proj, proj, attn, x, target, gate, pa_w, pb_w, wo_w, total)


def _local_step(x, target, shift, scale, gate, norm_w, conv_w, qw, kw, w_shard, small_shards, me_xyc):
    qw8, kw8 = jnp.tile(qw, (1, NH)), jnp.tile(kw, (1, NH))
    same, total, expand = _head_matrices()
    h, ht = norm_fwd(x, norm_w, scale, shift, 512)
    proj, wg, (pa_g, pb_g, wo_g) = proj_fwd_gather(h, w_shard, small_shards, gather_order(me_xyc), 1024)
    pa_w, wo_w = pa_g.reshape(D, D), wo_g.reshape(D, D)
    pb_w = pb_g.transpose(1, 0, 2).reshape(CB, D)
    srcs = qkv_prep(proj, qw8, kw8, same, 512)
    o_g, lse_g = zip(*[attn_fwd(*srcs[g], g, d) for g, d in enumerate(DILATIONS)])
    ya, yb, attn, lc = mid_fwd(proj, o_g, lse_g, conv_w, expand, 512)
    dproj, dy, dya, da, dc, merged, dout, dpa, dpb, st_tail = tail(
        proj, ya, yb, attn, x, target, gate, pa_w, pb_w, wo_w, total, 256)
    g_wo = matmul_tn(merged, dout, "grad_w_out", 1024)
    g_pa = matmul_tn(ya, dpa, "grad_w_br_conv", 1024)
    g_pb = matmul_tn(yb, dpb, "grad_w_br_attn", 1024)
    dproj, st_conv = mid_bwd(dproj, proj, dya, conv_w, 512)
    grads = []
    for g, d in enumerate(DILATIONS):
        da_p, lc_p, dc_p, lt, dt = stats_prep(da, lc, dc, g, d, 2048)
        dq = attn_bwd_q(*srcs[g], da_p, lc_p, dc_p, g, d)
        dk, dv = attn_bwd_kv(*srcs[g], da_p, lt, dt, g, d)
        grads.append((dq, dk, dv))
    dproj, gw_qk = qkv_grads_to_dproj(dproj, proj, grads, qw8, kw8, same, 512)
    slabs = [g_pa.reshape(NDEV, 128, D), g_pb.reshape(CB, NDEV, 128).transpose(1, 0, 2), g_wo.reshape(NDEV, 128, D)]
    dh, r_win, (r_pa, r_pb, r_wo) = proj_bwd(ht, dproj, wg, slabs, scatter_order(me_xyc), 1024)
    grad_x, st_norm = norm_bwd(dh, x, dy, norm_w, scale, 512)
    dmod = jnp.concatenate([st_norm[0:1], st_norm[1:2], st_tail[0:1]], axis=1)
    loss_part = (0.5 / D) * jnp.sum(st_tail[1])
    gw_heads = gw_qk[0:2].reshape(2, NH, HD).sum(axis=1)
    small = dict(dmod=dmod, norm_w=st_norm[2:3], conv_w=st_conv[0:3],
                 q_norm_w=gw_heads[0:1], k_norm_w=gw_heads[1:2], loss=loss_part)
    return grad_x, small, (r_win, r_pa, r_pb, r_wo)


def kernel(x, c, w_ada, b_ada, norm_w, w_in, conv_w, q_norm_w, k_norm_w, w_br_conv, w_br_attn, w_out, loss_target, m_w_ada, m_b_ada, m_norm_w, m_w_in, m_conv_w, m_q_norm_w, m_k_norm_w, m_w_br_conv, m_w_br_attn, m_w_out, v_w_ada, v_b_ada, v_norm_w, v_w_in, v_conv_w, v_q_norm_w, v_k_norm_w, v_w_br_conv, v_w_br_attn, v_w_out):
    me_xyc = (lax.axis_index("x"), lax.axis_index("y"), lax.axis_index("c"))
    me = _dev_index(me_xyc)
    ncol = w_ada.shape[2]

    conv_pad = jnp.zeros((8, 128), F32).at[0:3].set(conv_w[0])
    c_all, conv_all = all_gather([c, conv_pad], "gather_cond")
    conv_full = conv_all[:, 0:3].transpose(1, 0, 2).reshape(3, D)
    c_all = c_all.reshape(NDEV, D)

    b_cols = lax.dynamic_slice(b_ada, (0, me * ncol), (1, ncol))
    mod_cols = ada_fwd(c_all, w_ada[0], b_cols)
    (mod_all,) = all_gather([mod_cols], "gather_mod")
    mod = lax.dynamic_index_in_dim(mod_all, me, axis=1, keepdims=False).reshape(1, 3 * D)
    shift, scale, gate = mod[:, 0:D], mod[:, D:2 * D], mod[:, 2 * D:3 * D]

    grad_x, small, (r_win, r_pa, r_pb, r_wo) = _local_step(
        x[0], loss_target[0], shift, scale, gate, norm_w, conv_full, q_norm_w, k_norm_w,
        w_in[0].astype(BF16), [w_br_conv[0].astype(BF16), w_br_attn[0].astype(BF16), w_out[0].astype(BF16)], me_xyc)

    packed = jnp.concatenate(
        [small["dmod"], small["norm_w"], small["conv_w"].reshape(1, 3 * D), small["q_norm_w"], small["k_norm_w"],
         jnp.full((1, 128), small["loss"], F32)], axis=1)
    (packed_all,) = all_gather([packed], "gather_small")
    tot = sum_parts(packed_all)
    loss = tot[0, 7 * D + 2 * HD]
    dmod_all = packed_all[:, 0, 0:3 * D]
    g_b_ada = tot[:, 0:3 * D]
    g_norm_w = tot[:, 3 * D:4 * D]
    g_conv = lax.dynamic_slice(tot[:, 4 * D:7 * D].reshape(3, D), (0, me * 128), (3, 128))
    g_qn = tot[:, 7 * D:7 * D + HD]
    g_kn = tot[:, 7 * D + HD:7 * D + 2 * HD]
    g_w_ada = ada_bwd(c_all.T, lax.dynamic_slice(dmod_all, (0, me * ncol), (NDEV, ncol)))

    def upd(parts, w, m, v, name, rows):
        shape = w.shape
        w2, m2, v2 = (t.reshape(shape[-2:]) for t in (w, m, v))
        return [t.reshape(shape) for t in adamw(parts, w2, m2, v2, name, rows)]

    res = {
        "w_ada": upd(g_w_ada[None], w_ada, m_w_ada, v_w_ada, "adamw_w_ada", 256),
        "b_ada": upd(g_b_ada[None], b_ada, m_b_ada, v_b_ada, "adamw_b_ada", 1),
        "norm_w": upd(g_norm_w[None], norm_w, m_norm_w, v_norm_w, "adamw_norm_w", 1),
        "w_in": upd(r_win, w_in, m_w_in, v_w_in, "adamw_w_in", 128),
        "conv_w": upd(g_conv[None], conv_w, m_conv_w, v_conv_w, "adamw_conv_w", 3),
        "q_norm_w": upd(g_qn[None], q_norm_w, m_q_norm_w, v_q_norm_w, "adamw_q_norm_w", 1),
        "k_norm_w": upd(g_kn[None], k_norm_w, m_k_norm_w, v_k_norm_w, "adamw_k_norm_w", 1),
        "w_br_conv": upd(r_pa, w_br_conv, m_w_br_conv, v_w_br_conv, "adamw_w_br_conv", 128),
        "w_br_attn": upd(r_pb, w_br_attn, m_w_br_attn, v_w_br_attn, "adamw_w_br_attn", 512),
        "w_out": upd(r_wo, w_out, m_w_out, v_w_out, "adamw_w_out", 128),
    }
    names = ["w_ada", "b_ada", "norm_w", "w_in", "conv_w", "q_norm_w", "k_norm_w", "w_br_conv", "w_br_attn", "w_out"]
    return (loss, grad_x[None], *[res[n][0] for n in names], *[res[n][1] for n in names],
            *[res[n][2] for n in names], *[res[n][3] for n in names])
```

```python
import jax
import jax.numpy as jnp
from jax import lax
from jax.experimental import pallas as pl
from jax.experimental.pallas import tpu as pltpu

F32, BF16 = jnp.float32, jnp.bfloat16
D = 1024
NIN = 11264
NDEV = 8
SHARD = NIN // NDEV
HD = 64
NH = 8
QB = 128
CB = 512
CB_Q, CB_K, CB_V, CB_ZB = 8, 11, 14, 17
DILATIONS = (1, 4, 16)
EPS = 1e-6
NEG = -1e30
HALO = 16
LANES = 128
MESH = pl.DeviceIdType.MESH

ADAM_LR, ADAM_B1, ADAM_B2, ADAM_EPS, ADAM_WD, ADAM_STEP = 0.001, 0.9, 0.999, 1e-08, 0.01, 10

NT = (((1,), (1,)), ((), ()))
TN = (((0,), (0,)), ((), ()))


def _cp(sem, vmem_mb=48):
    return pltpu.CompilerParams(dimension_semantics=sem, vmem_limit_bytes=vmem_mb << 20)


def _silu(z):
    return z * jax.nn.sigmoid(z)


def _coords():
    return lax.axis_index("x"), lax.axis_index("y"), lax.axis_index("c")


FLIPS = [(fx, fy, fc) for fx in (0, 1) for fy in (0, 1) for fc in (0, 1)][1:]


def all_gather(arrs, name):
    n = len(arrs)

    def body(*refs):
        ins, outs = refs[:n], refs[n:2 * n]
        send_sems, recv_sems, local_sems = refs[2 * n:]
        me_xyc = _coords()
        me = _dev_index(me_xyc)
        peers = [_flip(me_xyc, f) for f in FLIPS]

        def copy(a, k, block):
            return pltpu.make_async_remote_copy(
                src_ref=ins[a], dst_ref=outs[a].at[block], send_sem=send_sems.at[a, k], recv_sem=recv_sems.at[a, k],
                device_id=peers[k], device_id_type=MESH)

        mine = [pltpu.make_async_copy(ins[a], outs[a].at[me], local_sems.at[a]) for a in range(n)]
        sends = [copy(a, k, me) for k in range(7) for a in range(n)]
        for cp in mine + sends:
            cp.start()
        for k in range(7):
            for a in range(n):
                copy(a, k, _dev_index(peers[k])).wait_recv()
        for cp in sends:
            cp.wait_send()
        for cp in mine:
            cp.wait()

    any_spec = pl.BlockSpec(memory_space=pl.ANY)
    return pl.pallas_call(
        body, name=name,
        out_shape=[jax.ShapeDtypeStruct((NDEV,) + a.shape, a.dtype) for a in arrs],
        in_specs=[any_spec] * n, out_specs=[any_spec] * n,
        scratch_shapes=[pltpu.SemaphoreType.DMA((n, 7)), pltpu.SemaphoreType.DMA((n, 7)),
                        pltpu.SemaphoreType.DMA((n,))],
    )(*arrs)


def _flip(dev, f):
    return tuple(1 - v if b else v for v, b in zip(dev, f))


def _dev_index(dev):
    return 4 * dev[0] + 2 * dev[1] + dev[2]


def _chip_order(x, y, c):
    xor = lambda a, b: a + b - 2 * a * b
    return [(xor(x, 1 - c), xor(y, c)), (xor(x, c), xor(y, 1 - c)), (1 - x, 1 - y)]


def gather_order(me_xyc):
    x, y, c = me_xyc
    chips = _chip_order(x, y, c)
    devs = [(x, y, c), (x, y, 1 - c), (*chips[0], c), (*chips[1], c),
            (*chips[1], 1 - c), (*chips[0], 1 - c), (*chips[2], c), (*chips[2], 1 - c)]
    return jnp.stack([_dev_index(d) for d in devs]).astype(jnp.int32)


def scatter_order(me_xyc):
    devs = [_flip(me_xyc, f) for f in FLIPS] + [me_xyc]
    return jnp.stack([_dev_index(d) for d in devs]).astype(jnp.int32)


def ada_fwd(c_all, w_ada, b_cols):
    def body(c_ref, w_ref, b_ref, o_ref):
        a = _silu(c_ref[...]).astype(BF16)
        o_ref[...] = jnp.dot(a, w_ref[...].astype(BF16), preferred_element_type=F32) + b_ref[...]

    return pl.pallas_call(body, name="ada_fwd",
                          out_shape=jax.ShapeDtypeStruct((NDEV, w_ada.shape[1]), F32))(c_all, w_ada, b_cols)


def ada_bwd(c_all_t, dmod_cols):
    def body(c_ref, d_ref, o_ref):
        at = _silu(c_ref[...])
        acc = at[:, 0:1] * d_ref[0:1, :]
        for b in range(1, NDEV):
            acc = acc + at[:, b:b + 1] * d_ref[b:b + 1, :]
        o_ref[...] = acc

    return pl.pallas_call(body, name="ada_bwd",
                          out_shape=jax.ShapeDtypeStruct((D, dmod_cols.shape[1]), F32))(c_all_t, dmod_cols)


def sum_parts(parts):
    def body(p_ref, o_ref):
        acc = p_ref[0]
        for b in range(1, NDEV):
            acc = acc + p_ref[b]
        o_ref[...] = acc

    return pl.pallas_call(body, name="sum_parts",
                          out_shape=jax.ShapeDtypeStruct(parts.shape[1:], F32))(parts)


def adamw(parts, w, m, v, name, rows):
    n, r, ccols = parts.shape

    def body(p_ref, w_ref, m_ref, v_ref, g_ref, d_ref, nm_ref, nv_ref):
        g = p_ref[0].astype(F32)
        for b in range(1, n):
            g = g + p_ref[b].astype(F32)
        nm = ADAM_B1 * m_ref[...] + (1.0 - ADAM_B1) * g
        nv = ADAM_B2 * v_ref[...] + (1.0 - ADAM_B2) * (g * g)
        g_ref[...] = g
        nm_ref[...] = nm
        nv_ref[...] = nv
        m_hat = nm / (1.0 - ADAM_B1 ** ADAM_STEP)
        v_hat = nv / (1.0 - ADAM_B2 ** ADAM_STEP)
        d_ref[...] = -ADAM_LR * (m_hat / (jnp.sqrt(v_hat) + ADAM_EPS) + ADAM_WD * w_ref[...])

    blk = pl.BlockSpec((rows, ccols), lambda i: (i, 0))
    out = jax.ShapeDtypeStruct((r, ccols), F32)
    return pl.pallas_call(
        body, name=name, grid=(r // rows,),
        in_specs=[pl.BlockSpec((n, rows, ccols), lambda i: (0, i, 0)), blk, blk, blk],
        out_specs=[blk] * 4, out_shape=[out] * 4, compiler_params=_cp(("parallel",)))(parts, w, m, v)


def norm_fwd(x, nw, scale, shift, tm):
    s = x.shape[0]

    def body(x_ref, nw_ref, sc_ref, sh_ref, h_ref, ht_ref):
        xf = x_ref[...]
        r = lax.rsqrt(jnp.mean(xf * xf, axis=-1, keepdims=True) + EPS)
        h = (xf * r * nw_ref[...]) * (1.0 + sc_ref[...]) + sh_ref[...]
        h_ref[...] = h.astype(BF16)
        ht_ref[...] = h.T.astype(BF16)

    vec = pl.BlockSpec((1, D), lambda i: (0, 0))
    return pl.pallas_call(
        body, name="norm_fwd", grid=(s // tm,),
        in_specs=[pl.BlockSpec((tm, D), lambda i: (i, 0)), vec, vec, vec],
        out_specs=[pl.BlockSpec((tm, D), lambda i: (i, 0)), pl.BlockSpec((D, tm), lambda i: (0, i))],
        out_shape=[jax.ShapeDtypeStruct((s, D), BF16), jax.ShapeDtypeStruct((D, s), BF16)],
        compiler_params=_cp(("parallel",)))(x, nw, scale, shift)


def proj_fwd_gather(h, w_shard, extras, order, tm):
    s = h.shape[0]
    ni = s // tm
    n = 1 + len(extras)
    mid = ni - 2

    def body(order_ref, h_ref, *refs):
        ins, o_ref, outs = refs[:n], refs[n], refs[n + 1:2 * n + 1]
        wbuf, send_sems, recv_sems, local_sems, load_sems = refs[2 * n + 1:]
        jj, i = pl.program_id(0), pl.program_id(1)
        x, y, c = _coords()
        me, sibling = (x, y, c), (x, y, 1 - c)
        chips = _chip_order(x, y, c)
        relayed = [(*chips[1], 1 - c), (*chips[0], 1 - c), (*chips[2], 1 - c)]

        def slot(a, dev):
            return outs[a].at[_dev_index(dev)]

        def copy(a, k, block, to, src=None):
            return pltpu.make_async_remote_copy(
                src_ref=slot(a, block) if src is None else src, dst_ref=slot(a, block),
                send_sem=send_sems.at[a, k], recv_sem=recv_sems.at[a, k], device_id=to, device_id_type=MESH)

        mine = [pltpu.make_async_copy(ins[a], slot(a, me), local_sems.at[a]) for a in range(n)]
        to_sibling = [copy(a, 0, me, sibling, src=ins[a]) for a in range(n)]
        to_chip = [[copy(a, 1 + j, me, (*chips[j], c), src=ins[a]) for a in range(n)] for j in range(2)]
        onward = [copy(a, 3, (*chips[1], c), (*chips[0], c)) for a in range(n)]
        passed = [[copy(a, 4 + j, (*ch, c), sibling) for a in range(n)] for j, ch in enumerate(chips)]
        sends = lambda a: [to_sibling[a], to_chip[0][a], to_chip[1][a], onward[a]] + [passed[j][a] for j in range(3)]

        def arrived(a, j):
            copy(a, 1 + j, (*chips[j], c), me).wait_recv()

        def load(row):
            return pltpu.make_async_copy(outs[0].at[order_ref[row]], wbuf.at[row % 2], load_sems.at[row % 2])

        @pl.when((jj == 0) & (i == 0))
        def _():
            for cp in mine:
                cp.start()
            to_sibling[0].start()
            to_chip[0][0].start()
            pltpu.make_async_copy(ins[0], wbuf.at[0], load_sems.at[0]).start()

        @pl.when((jj == 1) & (i == 0))
        def _():
            to_chip[1][0].start()

        @pl.when((jj == 4) & (i == 0))
        def _():
            for a in range(1, n):
                to_sibling[a].start()
                to_chip[0][a].start()
                to_chip[1][a].start()

        direct = {2: 0, 3: 1, 6: 2}
        relay = {4: 0, 5: 1, 7: 2}

        @pl.when((jj == 0) & (i == mid))
        def _():
            copy(0, 0, sibling, me).wait_recv()

        for row, j in direct.items():
            @pl.when((jj == row - 1) & (i == mid))
            def _(j=j):
                arrived(0, j)
                passed[j][0].start()
                if j == 1:
                    onward[0].start()

        for row, j in relay.items():
            @pl.when((jj == row - 1) & (i == mid))
            def _(j=j):
                copy(0, 4 + j, relayed[j], me).wait_recv()

        @pl.when((jj == NDEV - 1) & (i == 0))
        def _():
            for a in range(1, n):
                arrived(a, 1)
                onward[a].start()
                passed[1][a].start()
                arrived(a, 0)
                passed[0][a].start()

        @pl.when((jj < NDEV - 1) & (i == mid))
        def _():
            load(jj + 1).start()

        @pl.when(i == 0)
        def _():
            load(jj).wait()

        o_ref[...] = jnp.dot(h_ref[...], wbuf[jj % 2], preferred_element_type=F32).astype(BF16)

        @pl.when((jj == NDEV - 1) & (i == ni - 1))
        def _():
            for a in range(1, n):
                arrived(a, 2)
                passed[2][a].start()
            for a in range(1, n):
                copy(a, 0, sibling, me).wait_recv()
                for j in range(3):
                    copy(a, 4 + j, relayed[j], me).wait_recv()
            for a in range(n):
                mine[a].wait()
                for cp in sends(a):
                    cp.wait_send()

    any_spec = pl.BlockSpec(memory_space=pl.ANY)
    outs = pl.pallas_call(
        body, name="proj_fwd_gather",
        grid_spec=pltpu.PrefetchScalarGridSpec(
            num_scalar_prefetch=1, grid=(NDEV, ni),
            in_specs=[pl.BlockSpec((tm, D), lambda jj, i, o: (i, 0))] + [any_spec] * n,
            out_specs=[pl.BlockSpec((tm, SHARD), lambda jj, i, o: (i, o[jj]))] + [any_spec] * n,
            scratch_shapes=[pltpu.VMEM((2, D, SHARD), BF16), pltpu.SemaphoreType.DMA((n, 7)),
                            pltpu.SemaphoreType.DMA((n, 7)), pltpu.SemaphoreType.DMA((n,)),
                            pltpu.SemaphoreType.DMA((2,))]),
        out_shape=[jax.ShapeDtypeStruct((s, NIN), BF16), jax.ShapeDtypeStruct((NDEV, D, SHARD), BF16)]
                  + [jax.ShapeDtypeStruct((NDEV,) + e.shape, e.dtype) for e in extras],
        compiler_params=_cp(("arbitrary", "arbitrary")))(order, h, w_shard, *extras)
    return outs[0], outs[1], outs[2:]


def proj_bwd(ht, dproj, wg, smalls, order, x, dy, nw, scale, tt):
    s = dproj.shape[0]
    nk = s // tt
    n = len(smalls)
    rows_per_step = tt // nk
    last = 2 * NDEV

    def body(order_ref, ht_ref, dp_ref, w_ref, x_ref, dy_ref, nw_ref, sc_ref, *rest):
        small_in = rest[:n]
        gx_ref, st_ref, gw_ref, rwin_ref = rest[n:n + 4]
        small_out = rest[n + 4:2 * n + 4]
        acc, stage, dh, send_sems, recv_sems, local_sems, stage_sems = rest[2 * n + 4:]
        t, k = pl.program_id(0), pl.program_id(1)
        me_xyc = _coords()
        me = _dev_index(me_xyc)
        peers = [_flip(me_xyc, f) for f in FLIPS]

        def exchange(a, kf, src_arr, dst_arr):
            pid = _dev_index(peers[kf])
            mk = lambda dst: pltpu.make_async_remote_copy(
                src_ref=src_arr.at[pid], dst_ref=dst, send_sem=send_sems.at[a, kf], recv_sem=recv_sems.at[a, kf],
                device_id=peers[kf], device_id_type=MESH)
            return mk(dst_arr.at[me]), mk(dst_arr.at[pid])

        small_pairs = [exchange(1 + a, kf, small_in[a], small_out[a]) for kf in range(7) for a in range(n)]
        small_own = [pltpu.make_async_copy(small_in[a].at[me], small_out[a].at[me], local_sems.at[1 + a])
                     for a in range(n)]
        win_pairs = [exchange(0, kf, gw_ref, rwin_ref) for kf in range(7)]
        win_own = pltpu.make_async_copy(gw_ref.at[me], rwin_ref.at[me], local_sems.at[0])

        def to_hbm(jj):
            slab = me if jj == 7 else _dev_index(peers[jj])
            return pltpu.make_async_copy(stage.at[jj % 2], gw_ref.at[slab], stage_sems.at[jj % 2])

        @pl.when((t == 0) & (k == 0))
        def _():
            for cp in small_own:
                cp.start()
            for send, _ in small_pairs:
                send.start()

        @pl.when(t < NDEV)
        def _():
            p = jnp.dot(ht_ref[...], dp_ref[...], preferred_element_type=F32)

            @pl.when(k == 0)
            def _():
                acc[...] = p

            @pl.when(k > 0)
            def _():
                acc[...] += p

        for jj in range(NDEV):
            @pl.when((t == jj) & (k == nk - 1))
            def _(jj=jj):
                stage[jj % 2] = acc[...].astype(BF16)
                to_hbm(jj).start()

            @pl.when((t == jj + 1) & (k == 1))
            def _(jj=jj):
                to_hbm(jj).wait()
                if jj < 7:
                    win_pairs[jj][0].start()
                else:
                    win_own.start()

        def matmul_step():
            p = lax.dot_general(dp_ref[...], w_ref[...], NT, preferred_element_type=F32)
            slot = t % 2
            dh[slot] = jnp.where(k == 0, p, dh[slot] + p)

        def norm_step():
            g = dh.at[(t + 1) % 2][pl.ds(pl.multiple_of(k * rows_per_step, rows_per_step), rows_per_step), :]
            xf = x_ref[...]
            r = lax.rsqrt(jnp.mean(xf * xf, axis=-1, keepdims=True) + EPS)
            xh = xf * r
            dn = g * (1.0 + sc_ref[...])
            dxh = dn * nw_ref[...]
            gx_ref[...] = dy_ref[...] + r * (dxh - xh * jnp.mean(dxh * xh, axis=-1, keepdims=True))
            st_ref[0:1, :] += jnp.sum(g, axis=0, keepdims=True)
            st_ref[1:2, :] += jnp.sum(g * xh * nw_ref[...], axis=0, keepdims=True)
            st_ref[2:3, :] += jnp.sum(dn * xh, axis=0, keepdims=True)

        @pl.when((t == 0) & (k == 0))
        def _():
            st_ref[...] = jnp.zeros_like(st_ref)

        @pl.when(t == NDEV)
        def _():
            matmul_step()

        @pl.when((t > NDEV) & (t < last))
        def _():
            matmul_step()
            norm_step()

        @pl.when(t == last)
        def _():
            norm_step()

        @pl.when((t == last) & (k == nk - 1))
        def _():
            for _, recv in win_pairs + small_pairs:
                recv.wait_recv()
            for send, _ in win_pairs + small_pairs:
                send.wait_send()
            win_own.wait()
            for cp in small_own:
                cp.wait()

    any_spec = pl.BlockSpec(memory_space=pl.ANY)
    first = lambda t: t < NDEV
    slab = lambda t, k: jnp.where(t == last, NDEV - 1, k)
    chunk = pl.BlockSpec((rows_per_step, D), lambda t, k, o: (jnp.maximum((t - NDEV - 1) * nk + k, 0), 0))
    vec = pl.BlockSpec((1, D), lambda t, k, o: (0, 0))
    outs = pl.pallas_call(
        body, name="proj_bwd",
        grid_spec=pltpu.PrefetchScalarGridSpec(
            num_scalar_prefetch=1, grid=(last + 1, nk),
            in_specs=[pl.BlockSpec((D, tt), lambda t, k, o: (0, jnp.where(first(t), k, nk - 1))),
                      pl.BlockSpec((tt, SHARD), lambda t, k, o: (jnp.where(first(t), k, jnp.minimum(t, last - 1) - NDEV),
                                                                 jnp.where(first(t), o[jnp.minimum(t, NDEV - 1)],
                                                                           slab(t, k)))),
                      pl.BlockSpec((None, D, SHARD), lambda t, k, o: (jnp.where(first(t), 0, slab(t, k)), 0, 0)),
                      chunk, chunk, vec, vec]
                     + [any_spec] * n,
            out_specs=[chunk, pl.BlockSpec((8, D), lambda t, k, o: (0, 0))] + [any_spec] * (2 + n),
            scratch_shapes=[pltpu.VMEM((D, SHARD), F32), pltpu.VMEM((2, D, SHARD), BF16),
                            pltpu.VMEM((2, tt, D), F32),
                            pltpu.SemaphoreType.DMA((1 + n, 7)), pltpu.SemaphoreType.DMA((1 + n, 7)),
                            pltpu.SemaphoreType.DMA((1 + n,)), pltpu.SemaphoreType.DMA((2,))]),
        out_shape=[jax.ShapeDtypeStruct((s, D), F32), jax.ShapeDtypeStruct((8, D), F32),
                   jax.ShapeDtypeStruct((NDEV, D, SHARD), BF16), jax.ShapeDtypeStruct((NDEV, D, SHARD), BF16)]
                  + [jax.ShapeDtypeStruct(a.shape, a.dtype) for a in smalls],
        compiler_params=_cp(("arbitrary", "arbitrary"), 56))(order, ht, dproj, wg, x, dy, nw, scale, *smalls)
    return outs[0], outs[1], outs[3], outs[4:]


def matmul_tn(a, b, name, tk):
    s, m = a.shape
    n = b.shape[1]
    nk = s // tk

    def body(a_ref, b_ref, o_ref, acc_ref):
        k = pl.program_id(0)
        p = lax.dot_general(a_ref[...], b_ref[...], TN, preferred_element_type=F32)

        @pl.when(k == 0)
        def _():
            acc_ref[...] = p

        @pl.when(k > 0)
        def _():
            acc_ref[...] += p

        @pl.when(k == nk - 1)
        def _():
            o_ref[...] = acc_ref[...].astype(BF16)

    return pl.pallas_call(
        body, name=name, grid=(nk,),
        in_specs=[pl.BlockSpec((tk, m), lambda k: (k, 0)), pl.BlockSpec((tk, n), lambda k: (k, 0))],
        out_specs=pl.BlockSpec((m, n), lambda k: (0, 0)),
        out_shape=jax.ShapeDtypeStruct((m, n), BF16),
        scratch_shapes=[pltpu.VMEM((m, n), F32)],
        compiler_params=_cp(("arbitrary",)))(a, b)


def _head_matrices():
    lane = lax.broadcasted_iota(jnp.int32, (CB, CB), 0)
    col = lax.broadcasted_iota(jnp.int32, (CB, CB), 1)
    same = (lane // HD == col // HD).astype(BF16)
    lane_c = lax.broadcasted_iota(jnp.int32, (CB, LANES), 0)
    col_c = lax.broadcasted_iota(jnp.int32, (CB, LANES), 1)
    total = (lane_c // HD == col_c).astype(BF16)
    lane_e = lax.broadcasted_iota(jnp.int32, (LANES, CB), 0)
    col_e = lax.broadcasted_iota(jnp.int32, (LANES, CB), 1)
    expand = (lane_e == col_e // HD).astype(BF16)
    return same, total, expand


def _head_sum(x, m_ref):
    return jnp.dot(x.astype(BF16), m_ref[...], preferred_element_type=F32)


def _dot_hilo(x, m_ref):
    hi = x.astype(BF16)
    lo = (x - hi.astype(F32)).astype(BF16)
    return (jnp.dot(hi, m_ref[...], preferred_element_type=F32)
            + jnp.dot(lo, m_ref[...], preferred_element_type=F32))


def _to_residue_major(val, buf, out_ref, dil):
    rows = out_ref.shape[1]
    for k in range(val.shape[1] // LANES):
        lanes = slice(k * LANES, (k + 1) * LANES)
        buf[k] = val[:, lanes]
        for r in range(dil):
            out_ref[r, :, lanes] = buf.at[k][pl.ds(r, rows, stride=dil), :].astype(out_ref.dtype)


def _from_residue_major(ref, buf, dil):
    if dil == 1:
        return ref[0].astype(F32)
    rows, chunks = ref.shape[1], ref.shape[2] // LANES
    for k in range(chunks):
        for r in range(dil):
            buf.at[k][pl.ds(r, rows, stride=dil), :] = ref[r, :, k * LANES:(k + 1) * LANES].astype(F32)
    return jnp.concatenate([buf[k] for k in range(chunks)], axis=1)


def qkv_prep(proj, qw8, kw8, same, tm):
    s = proj.shape[0]
    items = []
    for g, d in enumerate(DILATIONS):
        items += [(g, "q", CB_Q + g, d), (g, "k", CB_K + g, d)] + ([(g, "v", CB_V + g, d)] if d > 1 else [])
    n = len(items)

    def body(*refs):
        ins, (qw_ref, kw_ref, same_ref), outs, buf = refs[:n], refs[n:n + 3], refs[n + 3:2 * n + 3], refs[-1]
        for idx, (_, kind, _, dil) in enumerate(items):
            val = ins[idx][...].astype(F32)
            if kind != "v":
                r = lax.rsqrt(_head_sum(val * val, same_ref) * (1.0 / HD) + EPS)
                val = val * r * (qw_ref if kind == "q" else kw_ref)[...]
            if dil == 1:
                outs[idx][0] = val.astype(BF16)
            else:
                _to_residue_major(val, buf, outs[idx], dil)

    full = lambda a: pl.BlockSpec(a.shape, lambda i: (0, 0))
    outs = pl.pallas_call(
        body, name="qkv_prep", grid=(s // tm,),
        in_specs=[pl.BlockSpec((tm, CB), lambda i, cb=cb: (i, cb)) for _, _, cb, _ in items]
                 + [full(qw8), full(kw8), full(same)],
        out_specs=[pl.BlockSpec((d, tm // d, CB), lambda i: (0, i, 0)) for _, _, _, d in items],
        out_shape=[jax.ShapeDtypeStruct((d, s // d, CB), BF16) for _, _, _, d in items],
        scratch_shapes=[pltpu.VMEM((CB // LANES, tm, LANES), F32)],
        compiler_params=_cp(("parallel",)))(*([proj] * n), qw8 * (HD ** -0.5), kw8, same)
    srcs = [[None, None, (proj, CB_V + g)] for g in range(len(DILATIONS))]
    for (g, kind, _, _), o in zip(items, outs):
        srcs[g]["qkv".index(kind)] = (o.reshape(s, CB), 0)
    return srcs


def stats_prep(da, lc, dc, g, dil, tm):
    s = da.shape[0]
    rows = tm // dil

    def body(da_ref, lc_ref, dc_ref, dap_ref, lcp_ref, dcp_ref, lt_ref, dt_ref, buf):
        if dil == 1:
            dap_ref[0] = da_ref[...]
        else:
            _to_residue_major(da_ref[...].astype(F32), buf, dap_ref, dil)
        for src, dst, dst_t in ((lc_ref, lcp_ref, lt_ref), (dc_ref, dcp_ref, dt_ref)):
            buf[0] = src[...]
            for r in range(dil):
                piece = buf.at[0][pl.ds(r, rows, stride=dil), :] if dil > 1 else buf[0]
                dst[r] = piece
                dst_t[r] = piece.T[0:NH, :]

    row = lambda w: pl.BlockSpec((tm, w), lambda i: (i, 0))
    rm = lambda w: pl.BlockSpec((dil, rows, w), lambda i: (0, i, 0))
    tr = pl.BlockSpec((dil, NH, rows), lambda i: (0, 0, i))
    length = s // dil
    dap, lcp, dcp, lt, dt = pl.pallas_call(
        body, name=f"stats_prep_g{g}", grid=(s // tm,),
        in_specs=[row(CB), row(LANES), row(LANES)],
        out_specs=[rm(CB), rm(LANES), rm(LANES), tr, tr],
        out_shape=[jax.ShapeDtypeStruct((dil, length, CB), BF16)]
                  + [jax.ShapeDtypeStruct((dil, length, LANES), F32)] * 2
                  + [jax.ShapeDtypeStruct((dil, NH, length), F32)] * 2,
        scratch_shapes=[pltpu.VMEM((CB // LANES, tm, LANES), F32)],
        compiler_params=_cp(("parallel",)))(da, lc, dc)
    return (dap.reshape(s, CB), lcp.reshape(s, LANES), dcp.reshape(s, LANES),
            lt.reshape(dil * NH, length), dt.reshape(dil * NH, length))


def qkv_grads_to_dproj(dproj, proj, grads, qw8, kw8, same, tm):
    s = dproj.shape[0]
    ni = s // tm
    flat = [(t.reshape(d, s // d, CB), d, kind, 3 * kind + g)
            for g, d in enumerate(DILATIONS) for kind, t in enumerate(grads[g])]
    nf = len(flat)
    nraw = 2 * len(DILATIONS)

    def body(*refs):
        dp_hbm, raws, ins = refs[nraw + nf + 4], refs[1:1 + nraw], refs[1 + nraw:1 + nraw + nf]
        qw_ref, kw_ref, same_ref = refs[1 + nraw + nf:4 + nraw + nf]
        gw_ref, stage, buf, sems = refs[5 + nraw + nf:]
        i = pl.program_id(0)
        slot = i % 2

        def slab(step, sl):
            return pltpu.make_async_copy(
                stage.at[sl], dp_hbm.at[pl.ds(pl.multiple_of(step * tm, tm), tm), pl.ds(CB_Q * CB, 9 * CB)],
                sems.at[sl])

        @pl.when(i == 0)
        def _():
            gw_ref[...] = jnp.zeros_like(gw_ref)

        @pl.when(i >= 2)
        def _():
            slab(i - 2, slot).wait()

        for ref, (_, d, kind, jj) in zip(ins, flat):
            cols = slice(jj * CB, (jj + 1) * CB)
            dn = _from_residue_major(ref, buf, d)
            if kind == 2:
                stage[slot, :, cols] = dn.astype(BF16)
                continue
            t = raws[jj][...].astype(F32)
            r = lax.rsqrt(_head_sum(t * t, same_ref) * (1.0 / HD) + EPS)
            xh = t * r
            gw_ref[kind:kind + 1, :] += jnp.sum(dn * xh, axis=0, keepdims=True)
            dxh = dn * (qw_ref if kind == 0 else kw_ref)[...]
            mean = _head_sum(dxh * xh, same_ref) * (1.0 / HD)
            stage[slot, :, cols] = (r * (dxh - xh * mean)).astype(BF16)
        slab(i, slot).start()

        @pl.when(i == ni - 1)
        def _():
            slab(i - 1, 1 - slot).wait()
            slab(i, slot).wait()

    full = lambda a: pl.BlockSpec(a.shape, lambda i: (0, 0))
    any_spec = pl.BlockSpec(memory_space=pl.ANY)
    return pl.pallas_call(
        body, name="qkv_grads_to_dproj", grid=(ni,),
        in_specs=[any_spec] + [pl.BlockSpec((tm, CB), lambda i, jb=jb: (i, CB_Q + jb)) for jb in range(nraw)]
                 + [pl.BlockSpec((d, tm // d, CB), lambda i: (0, i, 0)) for _, d, _, _ in flat]
                 + [full(qw8), full(kw8), full(same)],
        out_specs=[any_spec, pl.BlockSpec((8, CB), lambda i: (0, 0))],
        out_shape=[jax.ShapeDtypeStruct((s, NIN), BF16), jax.ShapeDtypeStruct((8, CB), F32)],
        input_output_aliases={0: 0},
        scratch_shapes=[pltpu.VMEM((2, tm, 9 * CB), BF16), pltpu.VMEM((CB // LANES, tm, LANES), F32),
                        pltpu.SemaphoreType.DMA((2,))],
        compiler_params=_cp(("arbitrary",)))(
            dproj, *([proj] * nraw), *[t for t, _, _, _ in flat], qw8, kw8, same)


def _lane_lo():
    return lax.broadcasted_iota(jnp.int32, (1, 2 * HD), 1) < HD


def _stack_heads(t, lo):
    zero = jnp.zeros_like(t)
    return jnp.concatenate([jnp.where(lo, t, zero), jnp.where(lo, zero, t)], axis=0)


def _masks(other_ok):
    qi = lax.broadcasted_iota(jnp.int32, (QB, QB), 0)
    kj = lax.broadcasted_iota(jnp.int32, (QB, QB), 1)
    return (kj >= qi) & other_ok, kj <= qi


SUB = 4


def _attn_specs(nb, dil):
    steps = nb // SUB
    main = lambda cb, w=CB: pl.BlockSpec((SUB * QB, w), lambda r, s: (r * steps + s, cb))
    prev = lambda cb: pl.BlockSpec((QB, CB), lambda r, s: (jnp.maximum(r * nb + SUB * s - 1, 0), cb))
    nxt = lambda cb: pl.BlockSpec((QB, CB), lambda r, s: (jnp.minimum(r * nb + SUB * (s + 1), dil * nb - 1), cb))
    return main, prev, nxt


def attn_fwd(q_src, k_src, v_src, g, dil):
    s = q_src[0].shape[0]
    nb = s // dil // QB
    main, prev, _ = _attn_specs(nb, dil)

    def body(q_ref, kp_ref, k_ref, vp_ref, v_ref, o_ref, l_ref, kbuf, vbuf):
        step = pl.program_id(1)
        kbuf[0:QB], kbuf[QB:] = kp_ref[...], k_ref[...]
        vbuf[0:QB], vbuf[QB:] = vp_ref[...], v_ref[...]
        lo = _lane_lo()
        head_lane = lax.broadcasted_iota(jnp.int32, (1, LANES), 1)

        def block(j, carry):
            r0 = pl.multiple_of(j * QB, QB)
            rows, krows = pl.ds(r0, QB), pl.ds(r0, 2 * QB)
            m_prev, m_cur = _masks(step * SUB + j > 0)
            mask = jnp.concatenate([m_prev, m_cur], axis=1)
            mask = jnp.concatenate([mask, mask], axis=0)
            lses = jnp.zeros((QB, LANES), F32)
            for i in range(NH // 2):
                sl = slice(2 * HD * i, 2 * HD * (i + 1))
                qs, ks, vv = q_ref[rows, sl], kbuf[krows, sl], vbuf[krows, sl]
                sc = lax.dot_general(_stack_heads(qs, lo), ks, NT, preferred_element_type=F32)
                sc = jnp.where(mask, sc, NEG)
                mx = jnp.max(sc, axis=-1, keepdims=True)
                p = jnp.exp(sc - mx)
                den = jnp.sum(p, axis=-1, keepdims=True)
                o = jnp.dot(p.astype(BF16), vv, preferred_element_type=F32) * (1.0 / den)
                lse = mx + jnp.log(den)
                o_ref[rows, sl] = jnp.where(lo, o[:QB], o[QB:]).astype(BF16)
                lses = jnp.where(head_lane == 2 * i, lse[:QB], jnp.where(head_lane == 2 * i + 1, lse[QB:], lses))
            l_ref[rows, :] = lses
            return carry

        lax.fori_loop(0, SUB, block, 0, unroll=True)

    return pl.pallas_call(
        body, name=f"attn_fwd_g{g}", grid=(dil, nb // SUB),
        in_specs=[main(q_src[1]), prev(k_src[1]), main(k_src[1]), prev(v_src[1]), main(v_src[1])],
        out_specs=[main(0), main(0, LANES)],
        out_shape=[jax.ShapeDtypeStruct((s, CB), BF16), jax.ShapeDtypeStruct((s, LANES), F32)],
        scratch_shapes=[pltpu.VMEM(((SUB + 1) * QB, CB), BF16)] * 2,
        compiler_params=_cp(("parallel", "parallel")))(q_src[0], k_src[0], k_src[0], v_src[0], v_src[0])


def attn_bwd_q(q_src, k_src, v_src, da, lc, dc, g, dil):
    s = q_src[0].shape[0]
    nb = s // dil // QB
    main, prev, _ = _attn_specs(nb, dil)

    def body(q_ref, kp_ref, k_ref, vp_ref, v_ref, da_ref, l_ref, d_ref, dq_ref, kbuf, vbuf):
        step = pl.program_id(1)
        kbuf[0:QB], kbuf[QB:] = kp_ref[...], k_ref[...]
        vbuf[0:QB], vbuf[QB:] = vp_ref[...], v_ref[...]
        lo = _lane_lo()

        def block(j, carry):
            r0 = pl.multiple_of(j * QB, QB)
            rows, krows = pl.ds(r0, QB), pl.ds(r0, 2 * QB)
            m_prev, m_cur = _masks(step * SUB + j > 0)
            mask = jnp.concatenate([m_prev, m_cur], axis=1)
            mask = jnp.concatenate([mask, mask], axis=0)
            lcols, dcols = l_ref[rows, :], d_ref[rows, :]
            for i in range(NH // 2):
                sl = slice(2 * HD * i, 2 * HD * (i + 1))
                qs, ks, vv, da2 = q_ref[rows, sl], kbuf[krows, sl], vbuf[krows, sl], da_ref[rows, sl]
                pair = lambda t: jnp.concatenate([t[:, 2 * i:2 * i + 1], t[:, 2 * i + 1:2 * i + 2]], axis=0)
                sc = lax.dot_general(_stack_heads(qs, lo), ks, NT, preferred_element_type=F32)
                sc = jnp.where(mask, sc, NEG)
                p = jnp.exp(sc - pair(lcols))
                dp = lax.dot_general(_stack_heads(da2, lo), vv, NT, preferred_element_type=F32)
                ds = p * (dp - pair(dcols))
                dq = jnp.dot(ds.astype(BF16), ks, preferred_element_type=F32)
                dq_ref[rows, sl] = (jnp.where(lo, dq[:QB], dq[QB:]) * (HD ** -0.5)).astype(BF16)
            return carry

        lax.fori_loop(0, SUB, block, 0, unroll=True)

    return pl.pallas_call(
        body, name=f"attn_bwd_q_g{g}", grid=(dil, nb // SUB),
        in_specs=[main(q_src[1]), prev(k_src[1]), main(k_src[1]), prev(v_src[1]), main(v_src[1]),
                  main(0), main(0, LANES), main(0, LANES)],
        out_specs=main(0), out_shape=jax.ShapeDtypeStruct((s, CB), BF16),
        scratch_shapes=[pltpu.VMEM(((SUB + 1) * QB, CB), BF16)] * 2,
        compiler_params=_cp(("parallel", "parallel")))(
            q_src[0], k_src[0], k_src[0], v_src[0], v_src[0], da, lc, dc)


def attn_bwd_kv(q_src, k_src, v_src, da, lt, dt, g, dil):
    s = q_src[0].shape[0]
    nb = s // dil // QB
    main, _, nxt = _attn_specs(nb, dil)

    def body(k_ref, v_ref, q_ref, qn_ref, da_ref, dan_ref, l_ref, ln_ref, d_ref, dn_ref, dk_ref, dv_ref,
             qbuf, dabuf, lbuf, dbuf):
        step = pl.program_id(1)
        qbuf[0:SUB * QB], qbuf[SUB * QB:] = q_ref[...], qn_ref[...]
        dabuf[0:SUB * QB], dabuf[SUB * QB:] = da_ref[...], dan_ref[...]
        for c in range(SUB):
            lbuf[c], dbuf[c] = l_ref[:, c * QB:(c + 1) * QB], d_ref[:, c * QB:(c + 1) * QB]
        lbuf[SUB], dbuf[SUB] = ln_ref[...], dn_ref[...]
        lo = _lane_lo()
        kj = lax.broadcasted_iota(jnp.int32, (QB, QB), 0)
        qi = lax.broadcasted_iota(jnp.int32, (QB, QB), 1)

        def block(j, carry):
            r0 = pl.multiple_of(j * QB, QB)
            rows, qrows = pl.ds(r0, QB), pl.ds(r0, 2 * QB)
            mask = jnp.concatenate([kj <= qi, (kj >= qi) & (step * SUB + j < nb - 1)], axis=1)
            mask = jnp.concatenate([mask, mask], axis=1)
            lrow = jnp.concatenate([lbuf[j], lbuf[j + 1]], axis=1)
            drow = jnp.concatenate([dbuf[j], dbuf[j + 1]], axis=1)
            for i in range(NH // 2):
                sl = slice(2 * HD * i, 2 * HD * (i + 1))
                q2, da2 = _stack_heads(qbuf[qrows, sl], lo), _stack_heads(dabuf[qrows, sl], lo)
                ks, vv = k_ref[rows, sl], v_ref[rows, sl]
                pair = lambda t: jnp.concatenate([t[2 * i:2 * i + 1, :], t[2 * i + 1:2 * i + 2, :]], axis=1)
                sc = lax.dot_general(ks, q2, NT, preferred_element_type=F32)
                sc = jnp.where(mask, sc, NEG)
                p = jnp.exp(sc - pair(lrow))
                dp = lax.dot_general(vv, da2, NT, preferred_element_type=F32)
                ds = p * (dp - pair(drow))
                dv_ref[rows, sl] = jnp.dot(p.astype(BF16), da2, preferred_element_type=F32).astype(BF16)
                dk_ref[rows, sl] = jnp.dot(ds.astype(BF16), q2, preferred_element_type=F32).astype(BF16)
            return carry

        lax.fori_loop(0, SUB, block, 0, unroll=True)

    steps = nb // SUB
    t_main = pl.BlockSpec((NH, SUB * QB), lambda r, s: (r, s))
    t_nxt = pl.BlockSpec((NH, QB), lambda r, s: (r, jnp.minimum(SUB * (s + 1), nb - 1)))
    out = jax.ShapeDtypeStruct((s, CB), BF16)
    return pl.pallas_call(
        body, name=f"attn_bwd_kv_g{g}", grid=(dil, steps),
        in_specs=[main(k_src[1]), main(v_src[1]), main(q_src[1]), nxt(q_src[1]),
                  main(0), nxt(0), t_main, t_nxt, t_main, t_nxt],
        out_specs=[main(0), main(0)], out_shape=[out, out],
        scratch_shapes=[pltpu.VMEM(((SUB + 1) * QB, CB), BF16)] * 2 + [pltpu.VMEM((SUB + 1, NH, QB), F32)] * 2,
        compiler_params=_cp(("parallel", "parallel")))(
            k_src[0], v_src[0], q_src[0], q_src[0], da, da, lt, lt, dt, dt)


def _conv_taps(u, u_prev, first):
    tm = u.shape[0]
    row = lax.broadcasted_iota(jnp.int32, (tm, 1), 0)
    up = jnp.where(first, 0.0, u_prev)
    u1 = jnp.where(row == 0, up[HALO - 1:HALO, :], pltpu.roll(u, 1, 0))
    u2 = jnp.where(row == 0, up[HALO - 2:HALO - 1, :],
                   jnp.where(row == 1, up[HALO - 1:HALO, :], pltpu.roll(u, 2, 0)))
    return u1, u2


def mid_fwd(proj, o_g, lse_g, conv_w, expand, tm):
    s = proj.shape[0]
    hb = tm // HALO

    def body(ba_ref, ca_ref, xa_ref, za_ref, cah_ref, xah_ref, zb_ref,
             o0, o1, o2, l0, l1, l2, w_ref, exp_ref, ya_ref, yb_ref, at_ref, lc_ref, buf_o, buf_l):
        first = pl.program_id(0) == 0
        u = ca_ref[...].astype(F32) * xa_ref[...].astype(F32)
        u1, u2 = _conv_taps(u, cah_ref[...].astype(F32) * xah_ref[...].astype(F32), first)
        conv = w_ref[0:1, :] * u2 + w_ref[1:2, :] * u1 + w_ref[2:3, :] * u
        ya_ref[...] = (ba_ref[...].astype(F32) * conv * _silu(za_ref[...].astype(F32))).astype(BF16)
        ls = [_from_residue_major(l, buf_l.at[g], d) for g, (l, d) in enumerate(zip((l0, l1, l2), DILATIONS))]
        mx = jnp.maximum(jnp.maximum(ls[0], ls[1]), ls[2])
        es = [jnp.exp(l - mx) for l in ls]
        den = es[0] + es[1] + es[2]
        attn = jnp.zeros((tm, CB), F32)
        for e, o, d in zip(es, (o0, o1, o2), DILATIONS):
            attn = attn + _dot_hilo(e / den, exp_ref) * _from_residue_major(o, buf_o, d)
        at_ref[...] = attn
        lc_ref[...] = mx + jnp.log(den)
        yb_ref[...] = (attn * _silu(zb_ref[...].astype(F32))).astype(BF16)

    col = lambda j: pl.BlockSpec((tm, D), lambda i: (i, j))
    halo = lambda j: pl.BlockSpec((HALO, D), lambda i: (jnp.maximum(i * hb - 1, 0), j))
    loc = lambda w: pl.BlockSpec((tm, w), lambda i: (i, 0))
    rm = lambda w: [pl.BlockSpec((d, tm // d, w), lambda i: (0, i, 0)) for d in DILATIONS]
    rm_view = lambda ts, w: [t.reshape(d, s // d, w) for t, d in zip(ts, DILATIONS)]
    return pl.pallas_call(
        body, name="mid_fwd", grid=(s // tm,),
        in_specs=[col(0), col(1), col(2), col(3), halo(1), halo(2),
                  pl.BlockSpec((tm, CB), lambda i: (i, CB_ZB))] + rm(CB) + rm(LANES)
                 + [pl.BlockSpec((3, D), lambda i: (0, 0)), pl.BlockSpec(expand.shape, lambda i: (0, 0))],
        out_specs=[loc(D), loc(CB), loc(CB), loc(LANES)],
        out_shape=[jax.ShapeDtypeStruct((s, D), BF16), jax.ShapeDtypeStruct((s, CB), BF16),
                   jax.ShapeDtypeStruct((s, CB), F32), jax.ShapeDtypeStruct((s, LANES), F32)],
        scratch_shapes=[pltpu.VMEM((CB // LANES, tm, LANES), F32), pltpu.VMEM((3, 1, tm, LANES), F32)],
        compiler_params=_cp(("parallel",)))(
            proj, proj, proj, proj, proj, proj, proj, *rm_view(o_g, CB), *rm_view(lse_g, LANES), conv_w, expand)


def mid_bwd(dproj, proj, dya, conv_w, tm):
    s = proj.shape[0]
    hb = tm // HALO
    nblk = s // tm
    last_h = s // HALO - 1

    def body(_, ba_ref, ca_ref, xa_ref, za_ref, cah_ref, xah_ref, ban_ref, zan_ref, dy_ref, dyn_ref, w_ref,
             o_ref, gw_ref):
        i = pl.program_id(0)
        ba, ca, xa, za = (t[...].astype(F32) for t in (ba_ref, ca_ref, xa_ref, za_ref))
        u = ca * xa
        u1, u2 = _conv_taps(u, cah_ref[...].astype(F32) * xah_ref[...].astype(F32), i == 0)
        w0, w1, w2 = w_ref[0:1, :], w_ref[1:2, :], w_ref[2:3, :]
        conv = w0 * u2 + w1 * u1 + w2 * u
        sg = jax.nn.sigmoid(za)
        sz = za * sg
        dy = dy_ref[...].astype(F32)
        dconv = dy * ba * sz
        dcn = dyn_ref[...].astype(F32) * ban_ref[...].astype(F32) * _silu(zan_ref[...].astype(F32))
        dcn = jnp.where(i == nblk - 1, 0.0, dcn)
        row = lax.broadcasted_iota(jnp.int32, (tm, 1), 0)
        d1 = jnp.where(row == tm - 1, dcn[0:1, :], pltpu.roll(dconv, tm - 1, 0))
        d2 = jnp.where(row == tm - 2, dcn[0:1, :],
                       jnp.where(row == tm - 1, dcn[1:2, :], pltpu.roll(dconv, tm - 2, 0)))
        du = w2 * dconv + w1 * d1 + w0 * d2
        o_ref[:, 0:D] = (dy * conv * sz).astype(BF16)
        o_ref[:, D:2 * D] = (du * xa).astype(BF16)
        o_ref[:, 2 * D:3 * D] = (du * ca).astype(BF16)
        o_ref[:, 3 * D:4 * D] = (dy * ba * conv * (sg * (1.0 + za * (1.0 - sg)))).astype(BF16)

        @pl.when(i == 0)
        def _():
            gw_ref[...] = jnp.zeros_like(gw_ref)

        gw_ref[0:1, :] += jnp.sum(dconv * u2, axis=0, keepdims=True)
        gw_ref[1:2, :] += jnp.sum(dconv * u1, axis=0, keepdims=True)
        gw_ref[2:3, :] += jnp.sum(dconv * u, axis=0, keepdims=True)

    col = lambda j: pl.BlockSpec((tm, D), lambda i: (i, j))
    halo_prev = lambda j: pl.BlockSpec((HALO, D), lambda i: (jnp.maximum(i * hb - 1, 0), j))
    halo_next = lambda j: pl.BlockSpec((HALO, D), lambda i: (jnp.minimum((i + 1) * hb, last_h), j))
    return pl.pallas_call(
        body, name="mid_bwd", grid=(nblk,),
        in_specs=[pl.BlockSpec(memory_space=pl.ANY), col(0), col(1), col(2), col(3),
                  halo_prev(1), halo_prev(2), halo_next(0), halo_next(3),
                  pl.BlockSpec((tm, D), lambda i: (i, 0)), halo_next(0),
                  pl.BlockSpec((3, D), lambda i: (0, 0))],
        out_specs=[pl.BlockSpec((tm, 4 * D), lambda i: (i, 0)), pl.BlockSpec((8, D), lambda i: (0, 0))],
        out_shape=[jax.ShapeDtypeStruct((s, NIN), BF16), jax.ShapeDtypeStruct((8, D), F32)],
        input_output_aliases={0: 0},
        compiler_params=_cp(("arbitrary",)))(dproj, proj, proj, proj, proj, proj, proj, proj, proj, dya, dya, conv_w)


def tail(proj, ya, yb, attn, x, target, gate, pa_w, pb_w, wo_w, total, tm):
    s = proj.shape[0]
    ni = s // tm
    ncol = NIN - CB_ZB * CB

    def body(ya_ref, yb_ref, ga_ref, gb_ref, zb_ref, at_ref, x_ref, t_ref, gate_ref, pa_ref, pb_ref, wo_ref,
             tot_ref, dp_hbm, dy_ref, dya_ref, da_ref, dc_ref, mg_ref, do_ref, dpa_ref, dpb_ref, st_ref,
             stage, sems):
        i = pl.program_id(0)
        slot = i % 2

        def slab(step, sl):
            return pltpu.make_async_copy(
                stage.at[sl], dp_hbm.at[pl.ds(pl.multiple_of(step * tm, tm), tm), pl.ds(CB_ZB * CB, ncol)],
                sems.at[sl])

        @pl.when(i == 0)
        def _():
            st_ref[...] = jnp.zeros_like(st_ref)

        @pl.when(i >= 2)
        def _():
            slab(i - 2, slot).wait()

        gate_v = gate_ref[...]
        pa = jnp.dot(ya_ref[...], pa_ref[...], preferred_element_type=F32)
        pb = jnp.dot(yb_ref[...], pb_ref[...], preferred_element_type=F32)
        sa = jax.nn.sigmoid(ga_ref[...].astype(F32))
        sb = jax.nn.sigmoid(gb_ref[...].astype(F32))
        merged = (sa * pa + sb * pb).astype(BF16)
        mg_ref[...] = merged
        out = jnp.dot(merged, wo_ref[...], preferred_element_type=F32)
        err = x_ref[...] + gate_v * out - t_ref[...]
        dy = err * (1.0 / D)
        dy_ref[...] = dy
        st_ref[0:1, :] += jnp.sum(dy * out, axis=0, keepdims=True)
        st_ref[1:2, :] += jnp.sum(err * err, axis=0, keepdims=True)
        dout = (gate_v * dy).astype(BF16)
        do_ref[...] = dout
        dmg = lax.dot_general(dout, wo_ref[...], NT, preferred_element_type=F32)
        dpa = (dmg * sa).astype(BF16)
        dpb = (dmg * sb).astype(BF16)
        dpa_ref[...] = dpa
        dpb_ref[...] = dpb
        stage[slot, :, CB:CB + D] = (dmg * pa * sa * (1.0 - sa)).astype(BF16)
        stage[slot, :, CB + D:] = (dmg * pb * sb * (1.0 - sb)).astype(BF16)
        dya_ref[...] = lax.dot_general(dpa, pa_ref[...], NT, preferred_element_type=F32).astype(BF16)
        dyb = lax.dot_general(dpb, pb_ref[...], NT, preferred_element_type=F32)
        zb = zb_ref[...].astype(F32)
        sg = jax.nn.sigmoid(zb)
        attn_v = at_ref[...]
        dattn = dyb * (zb * sg)
        da_ref[...] = dattn.astype(BF16)
        stage[slot, :, 0:CB] = (dyb * attn_v * (sg * (1.0 + zb * (1.0 - sg)))).astype(BF16)
        dc_ref[...] = _dot_hilo(dattn * attn_v, tot_ref)

        slab(i, slot).start()

        @pl.when(i == ni - 1)
        def _():
            slab(i - 1, 1 - slot).wait()
            slab(i, slot).wait()

    row = lambda w: pl.BlockSpec((tm, w), lambda i: (i, 0))
    pcol = lambda w, jb: pl.BlockSpec((tm, w), lambda i: (i, jb))
    full = lambda a: pl.BlockSpec(a.shape, lambda i: (0, 0))
    return pl.pallas_call(
        body, name="tail", grid=(ni,),
        in_specs=[row(D), row(CB), pcol(D, 9), pcol(D, 10), pcol(CB, CB_ZB), row(CB), row(D), row(D),
                  pl.BlockSpec((1, D), lambda i: (0, 0)), full(pa_w), full(pb_w), full(wo_w), full(total)],
        out_specs=[pl.BlockSpec(memory_space=pl.ANY),
                   row(D), row(D), row(CB), row(LANES), row(D), row(D), row(D), row(D),
                   pl.BlockSpec((8, D), lambda i: (0, 0))],
        out_shape=[jax.ShapeDtypeStruct((s, NIN), BF16), jax.ShapeDtypeStruct((s, D), F32),
                   jax.ShapeDtypeStruct((s, D), BF16), jax.ShapeDtypeStruct((s, CB), BF16),
                   jax.ShapeDtypeStruct((s, LANES), F32)] + [jax.ShapeDtypeStruct((s, D), BF16)] * 4
                  + [jax.ShapeDtypeStruct((8, D), F32)],
        scratch_shapes=[pltpu.VMEM((2, tm, ncol), BF16), pltpu.SemaphoreType.DMA((2,))],
        compiler_params=_cp(("arbitrary",), 56))(
            ya, yb, proj, proj, proj, attn, x, target, gate, pa_w, pb_w, wo_w, total)


def _local_step(x, target, shift, scale, gate, norm_w, conv_w, qw, kw, w_shard, small_shards, me_xyc):
    qw8, kw8 = jnp.tile(qw, (1, NH)), jnp.tile(kw, (1, NH))
    same, total, expand = _head_matrices()
    h, ht = norm_fwd(x, norm_w, scale, shift, 512)
    proj, wg, (pa_g, pb_g, wo_g) = proj_fwd_gather(h, w_shard, small_shards, gather_order(me_xyc), 1024)
    pa_w, wo_w = pa_g.reshape(D, D), wo_g.reshape(D, D)
    pb_w = pb_g.transpose(1, 0, 2).reshape(CB, D)
    srcs = qkv_prep(proj, qw8, kw8, same, 512)
    o_g, lse_g = zip(*[attn_fwd(*srcs[g], g, d) for g, d in enumerate(DILATIONS)])
    ya, yb, attn, lc = mid_fwd(proj, o_g, lse_g, conv_w, expand, 512)
    dproj, dy, dya, da, dc, merged, dout, dpa, dpb, st_tail = tail(
        proj, ya, yb, attn, x, target, gate, pa_w, pb_w, wo_w, total, 256)
    g_wo = matmul_tn(merged, dout, "grad_w_out", 1024)
    g_pa = matmul_tn(ya, dpa, "grad_w_br_conv", 1024)
    g_pb = matmul_tn(yb, dpb, "grad_w_br_attn", 1024)
    dproj, st_conv = mid_bwd(dproj, proj, dya, conv_w, 512)
    grads = []
    for g, d in enumerate(DILATIONS):
        da_p, lc_p, dc_p, lt, dt = stats_prep(da, lc, dc, g, d, 2048)
        dq = attn_bwd_q(*srcs[g], da_p, lc_p, dc_p, g, d)
        dk, dv = attn_bwd_kv(*srcs[g], da_p, lt, dt, g, d)
        grads.append((dq, dk, dv))
    dproj, gw_qk = qkv_grads_to_dproj(dproj, proj, grads, qw8, kw8, same, 512)
    slabs = [g_pa.reshape(NDEV, 128, D), g_pb.reshape(CB, NDEV, 128).transpose(1, 0, 2), g_wo.reshape(NDEV, 128, D)]
    grad_x, st_norm, r_win, (r_pa, r_pb, r_wo) = proj_bwd(
        ht, dproj, wg, slabs, scatter_order(me_xyc), x, dy, norm_w, scale, 1024)
    dmod = jnp.concatenate([st_norm[0:1], st_norm[1:2], st_tail[0:1]], axis=1)
    loss_part = (0.5 / D) * jnp.sum(st_tail[1])
    gw_heads = gw_qk[0:2].reshape(2, NH, HD).sum(axis=1)
    small = dict(dmod=dmod, norm_w=st_norm[2:3], conv_w=st_conv[0:3],
                 q_norm_w=gw_heads[0:1], k_norm_w=gw_heads[1:2], loss=loss_part)
    return grad_x, small, (r_win, r_pa, r_pb, r_wo)


def kernel(x, c, w_ada, b_ada, norm_w, w_in, conv_w, q_norm_w, k_norm_w, w_br_conv, w_br_attn, w_out, loss_target, m_w_ada, m_b_ada, m_norm_w, m_w_in, m_conv_w, m_q_norm_w, m_k_norm_w, m_w_br_conv, m_w_br_attn, m_w_out, v_w_ada, v_b_ada, v_norm_w, v_w_in, v_conv_w, v_q_norm_w, v_k_norm_w, v_w_br_conv, v_w_br_attn, v_w_out):
    me_xyc = (lax.axis_index("x"), lax.axis_index("y"), lax.axis_index("c"))
    me = _dev_index(me_xyc)
    ncol = w_ada.shape[2]

    conv_pad = jnp.zeros((8, 128), F32).at[0:3].set(conv_w[0])
    c_all, conv_all = all_gather([c, conv_pad], "gather_cond")
    conv_full = conv_all[:, 0:3].transpose(1, 0, 2).reshape(3, D)
    c_all = c_all.reshape(NDEV, D)

    b_cols = lax.dynamic_slice(b_ada, (0, me * ncol), (1, ncol))
    mod_cols = ada_fwd(c_all, w_ada[0], b_cols)
    (mod_all,) = all_gather([mod_cols], "gather_mod")
    mod = lax.dynamic_index_in_dim(mod_all, me, axis=1, keepdims=False).reshape(1, 3 * D)
    shift, scale, gate = mod[:, 0:D], mod[:, D:2 * D], mod[:, 2 * D:3 * D]

    grad_x, small, (r_win, r_pa, r_pb, r_wo) = _local_step(
        x[0], loss_target[0], shift, scale, gate, norm_w, conv_full, q_norm_w, k_norm_w,
        w_in[0].astype(BF16), [w_br_conv[0].astype(BF16), w_br_attn[0].astype(BF16), w_out[0].astype(BF16)], me_xyc)

    packed = jnp.concatenate(
        [small["dmod"], small["norm_w"], small["conv_w"].reshape(1, 3 * D), small["q_norm_w"], small["k_norm_w"],
         jnp.full((1, 128), small["loss"], F32)], axis=1)
    (packed_all,) = all_gather([packed], "gather_small")
    tot = sum_parts(packed_all)
    loss = tot[0, 7 * D + 2 * HD]
    dmod_all = packed_all[:, 0, 0:3 * D]
    g_b_ada = tot[:, 0:3 * D]
    g_norm_w = tot[:, 3 * D:4 * D]
    g_conv = lax.dynamic_slice(tot[:, 4 * D:7 * D].reshape(3, D), (0, me * 128), (3, 128))
    g_qn = tot[:, 7 * D:7 * D + HD]
    g_kn = tot[:, 7 * D + HD:7 * D + 2 * HD]
    g_w_ada = ada_bwd(c_all.T, lax.dynamic_slice(dmod_all, (0, me * ncol), (NDEV, ncol)))

    def upd(parts, w, m, v, name, rows):
        shape = w.shape
        w2, m2, v2 = (t.reshape(shape[-2:]) for t in (w, m, v))
        return [t.reshape(shape) for t in adamw(parts, w2, m2, v2, name, rows)]

    res = {
        "w_ada": upd(g_w_ada[None], w_ada, m_w_ada, v_w_ada, "adamw_w_ada", 256),
        "b_ada": upd(g_b_ada[None], b_ada, m_b_ada, v_b_ada, "adamw_b_ada", 1),
        "norm_w": upd(g_norm_w[None], norm_w, m_norm_w, v_norm_w, "adamw_norm_w", 1),
        "w_in": upd(r_win, w_in, m_w_in, v_w_in, "adamw_w_in", 128),
        "conv_w": upd(g_conv[None], conv_w, m_conv_w, v_conv_w, "adamw_conv_w", 3),
        "q_norm_w": upd(g_qn[None], q_norm_w, m_q_norm_w, v_q_norm_w, "adamw_q_norm_w", 1),
        "k_norm_w": upd(g_kn[None], k_norm_w, m_k_norm_w, v_k_norm_w, "adamw_k_norm_w", 1),
        "w_br_conv": upd(r_pa, w_br_conv, m_w_br_conv, v_w_br_conv, "adamw_w_br_conv", 128),
        "w_br_attn": upd(r_pb, w_br_attn, m_w_br_attn, v_w_br_attn, "adamw_w_br_attn", 512),
        "w_out": upd(r_wo, w_out, m_w_out, v_w_out, "adamw_w_out", 128),
    }
    names = ["w_ada", "b_ada", "norm_w", "w_in", "conv_w", "q_norm_w", "k_norm_w", "w_br_conv", "w_br_attn", "w_out"]
    return (loss, grad_x[None], *[res[n][0] for n in names], *[res[n][1] for n in names],
            *[res[n][2] for n in names], *[res[n][3] for n in names])
```

```python
import jax
import jax.numpy as jnp
from jax import lax
from jax.experimental import pallas as pl
from jax.experimental.pallas import tpu as pltpu

F32, BF16 = jnp.float32, jnp.bfloat16
D = 1024
NIN = 11264
NDEV = 8
SHARD = NIN // NDEV
HD = 64
NH = 8
QB = 128
CB = 512
CB_Q, CB_K, CB_V, CB_ZB = 8, 11, 14, 17
DILATIONS = (1, 4, 16)
EPS = 1e-6
NEG = -1e30
HALO = 16
LANES = 128
MESH = pl.DeviceIdType.MESH

ADAM_LR, ADAM_B1, ADAM_B2, ADAM_EPS, ADAM_WD, ADAM_STEP = 0.001, 0.9, 0.999, 1e-08, 0.01, 10

NT = (((1,), (1,)), ((), ()))
TN = (((0,), (0,)), ((), ()))


def _cp(sem, vmem_mb=48):
    return pltpu.CompilerParams(dimension_semantics=sem, vmem_limit_bytes=vmem_mb << 20)


def _silu(z):
    return z * jax.nn.sigmoid(z)


def _coords():
    return lax.axis_index("x"), lax.axis_index("y"), lax.axis_index("c")


FLIPS = [(fx, fy, fc) for fx in (0, 1) for fy in (0, 1) for fc in (0, 1)][1:]


def all_gather(arrs, name):
    n = len(arrs)

    def body(*refs):
        ins, outs = refs[:n], refs[n:2 * n]
        send_sems, recv_sems, local_sems = refs[2 * n:]
        me_xyc = _coords()
        me = _dev_index(me_xyc)
        peers = [_flip(me_xyc, f) for f in FLIPS]

        def copy(a, k, block):
            return pltpu.make_async_remote_copy(
                src_ref=ins[a], dst_ref=outs[a].at[block], send_sem=send_sems.at[a, k], recv_sem=recv_sems.at[a, k],
                device_id=peers[k], device_id_type=MESH)

        mine = [pltpu.make_async_copy(ins[a], outs[a].at[me], local_sems.at[a]) for a in range(n)]
        sends = [copy(a, k, me) for k in range(7) for a in range(n)]
        for cp in mine + sends:
            cp.start()
        for k in range(7):
            for a in range(n):
                copy(a, k, _dev_index(peers[k])).wait_recv()
        for cp in sends:
            cp.wait_send()
        for cp in mine:
            cp.wait()

    any_spec = pl.BlockSpec(memory_space=pl.ANY)
    return pl.pallas_call(
        body, name=name,
        out_shape=[jax.ShapeDtypeStruct((NDEV,) + a.shape, a.dtype) for a in arrs],
        in_specs=[any_spec] * n, out_specs=[any_spec] * n,
        scratch_shapes=[pltpu.SemaphoreType.DMA((n, 7)), pltpu.SemaphoreType.DMA((n, 7)),
                        pltpu.SemaphoreType.DMA((n,))],
    )(*arrs)


def _flip(dev, f):
    return tuple(1 - v if b else v for v, b in zip(dev, f))


def _dev_index(dev):
    return 4 * dev[0] + 2 * dev[1] + dev[2]


def _chip_order(x, y, c):
    xor = lambda a, b: a + b - 2 * a * b
    return [(xor(x, 1 - c), xor(y, c)), (xor(x, c), xor(y, 1 - c)), (1 - x, 1 - y)]


def gather_order(me_xyc):
    x, y, c = me_xyc
    chips = _chip_order(x, y, c)
    devs = [(x, y, c), (x, y, 1 - c), (*chips[0], c), (*chips[1], c),
            (*chips[1], 1 - c), (*chips[0], 1 - c), (*chips[2], c), (*chips[2], 1 - c)]
    return jnp.stack([_dev_index(d) for d in devs]).astype(jnp.int32)


def scatter_order(me_xyc):
    devs = [_flip(me_xyc, f) for f in FLIPS] + [me_xyc]
    return jnp.stack([_dev_index(d) for d in devs]).astype(jnp.int32)


def ada_fwd(c_all, w_ada, b_cols):
    def body(c_ref, w_ref, b_ref, o_ref):
        a = _silu(c_ref[...]).astype(BF16)
        o_ref[...] = jnp.dot(a, w_ref[...].astype(BF16), preferred_element_type=F32) + b_ref[...]

    return pl.pallas_call(body, name="ada_fwd",
                          out_shape=jax.ShapeDtypeStruct((NDEV, w_ada.shape[1]), F32))(c_all, w_ada, b_cols)


def ada_bwd(c_all_t, dmod_cols):
    def body(c_ref, d_ref, o_ref):
        at = _silu(c_ref[...])
        acc = at[:, 0:1] * d_ref[0:1, :]
        for b in range(1, NDEV):
            acc = acc + at[:, b:b + 1] * d_ref[b:b + 1, :]
        o_ref[...] = acc

    return pl.pallas_call(body, name="ada_bwd",
                          out_shape=jax.ShapeDtypeStruct((D, dmod_cols.shape[1]), F32))(c_all_t, dmod_cols)


def sum_parts(parts):
    def body(p_ref, o_ref):
        acc = p_ref[0]
        for b in range(1, NDEV):
            acc = acc + p_ref[b]
        o_ref[...] = acc

    return pl.pallas_call(body, name="sum_parts",
                          out_shape=jax.ShapeDtypeStruct(parts.shape[1:], F32))(parts)


def adamw(parts, w, m, v, name, rows):
    n, r, ccols = parts.shape

    def body(p_ref, w_ref, m_ref, v_ref, g_ref, d_ref, nm_ref, nv_ref):
        g = p_ref[0].astype(F32)
        for b in range(1, n):
            g = g + p_ref[b].astype(F32)
        nm = ADAM_B1 * m_ref[...] + (1.0 - ADAM_B1) * g
        nv = ADAM_B2 * v_ref[...] + (1.0 - ADAM_B2) * (g * g)
        g_ref[...] = g
        nm_ref[...] = nm
        nv_ref[...] = nv
        m_hat = nm / (1.0 - ADAM_B1 ** ADAM_STEP)
        v_hat = nv / (1.0 - ADAM_B2 ** ADAM_STEP)
        d_ref[...] = -ADAM_LR * (m_hat / (jnp.sqrt(v_hat) + ADAM_EPS) + ADAM_WD * w_ref[...])

    blk = pl.BlockSpec((rows, ccols), lambda i: (i, 0))
    out = jax.ShapeDtypeStruct((r, ccols), F32)
    return pl.pallas_call(
        body, name=name, grid=(r // rows,),
        in_specs=[pl.BlockSpec((n, rows, ccols), lambda i: (0, i, 0)), blk, blk, blk],
        out_specs=[blk] * 4, out_shape=[out] * 4, compiler_params=_cp(("parallel",)))(parts, w, m, v)


def proj_fwd_gather(x, nw, scale, shift, w_shard, extras, order, tm):
    s = x.shape[0]
    ni = s // tm
    n = 1 + len(extras)
    mid = ni - 2

    def body(order_ref, x_ref, nw_ref, sc_ref, sh_ref, *refs):
        ins, o_ref, ht_ref, outs = refs[:n], refs[n], refs[n + 1], refs[n + 2:2 * n + 2]
        h_all, wbuf, send_sems, recv_sems, local_sems, load_sems = refs[2 * n + 2:]
        jj, i = pl.program_id(0), pl.program_id(1)
        x, y, c = _coords()
        me, sibling = (x, y, c), (x, y, 1 - c)
        chips = _chip_order(x, y, c)
        relayed = [(*chips[1], 1 - c), (*chips[0], 1 - c), (*chips[2], 1 - c)]

        def slot(a, dev):
            return outs[a].at[_dev_index(dev)]

        def copy(a, k, block, to, src=None):
            return pltpu.make_async_remote_copy(
                src_ref=slot(a, block) if src is None else src, dst_ref=slot(a, block),
                send_sem=send_sems.at[a, k], recv_sem=recv_sems.at[a, k], device_id=to, device_id_type=MESH)

        mine = [pltpu.make_async_copy(ins[a], slot(a, me), local_sems.at[a]) for a in range(n)]
        to_sibling = [copy(a, 0, me, sibling, src=ins[a]) for a in range(n)]
        to_chip = [[copy(a, 1 + j, me, (*chips[j], c), src=ins[a]) for a in range(n)] for j in range(2)]
        onward = [copy(a, 3, (*chips[1], c), (*chips[0], c)) for a in range(n)]
        passed = [[copy(a, 4 + j, (*ch, c), sibling) for a in range(n)] for j, ch in enumerate(chips)]
        sends = lambda a: [to_sibling[a], to_chip[0][a], to_chip[1][a], onward[a]] + [passed[j][a] for j in range(3)]

        def arrived(a, j):
            copy(a, 1 + j, (*chips[j], c), me).wait_recv()

        def load(row):
            return pltpu.make_async_copy(outs[0].at[order_ref[row]], wbuf.at[row % 2], load_sems.at[row % 2])

        @pl.when((jj == 0) & (i == 0))
        def _():
            for cp in mine:
                cp.start()
            to_sibling[0].start()
            to_chip[0][0].start()
            pltpu.make_async_copy(ins[0], wbuf.at[0], load_sems.at[0]).start()

        @pl.when((jj == 1) & (i == 0))
        def _():
            to_chip[1][0].start()

        @pl.when((jj == 4) & (i == 0))
        def _():
            for a in range(1, n):
                to_sibling[a].start()
                to_chip[0][a].start()
                to_chip[1][a].start()

        direct = {2: 0, 3: 1, 6: 2}
        relay = {4: 0, 5: 1, 7: 2}

        @pl.when((jj == 0) & (i == mid))
        def _():
            copy(0, 0, sibling, me).wait_recv()

        for row, j in direct.items():
            @pl.when((jj == row - 1) & (i == mid))
            def _(j=j):
                arrived(0, j)
                passed[j][0].start()
                if j == 1:
                    onward[0].start()

        for row, j in relay.items():
            @pl.when((jj == row - 1) & (i == mid))
            def _(j=j):
                copy(0, 4 + j, relayed[j], me).wait_recv()

        @pl.when((jj == NDEV - 1) & (i == 0))
        def _():
            for a in range(1, n):
                arrived(a, 1)
                onward[a].start()
                passed[1][a].start()
                arrived(a, 0)
                passed[0][a].start()

        @pl.when((jj < NDEV - 1) & (i == mid))
        def _():
            load(jj + 1).start()

        @pl.when(i == 0)
        def _():
            load(jj).wait()

        @pl.when(jj == 0)
        def _():
            xf = x_ref[...]
            r = lax.rsqrt(jnp.mean(xf * xf, axis=-1, keepdims=True) + EPS)
            h = (xf * r * nw_ref[...]) * (1.0 + sc_ref[...]) + sh_ref[...]
            h_all[i] = h.astype(BF16)
            ht_ref[...] = h.T.astype(BF16)

        o_ref[...] = jnp.dot(h_all[i], wbuf[jj % 2], preferred_element_type=F32).astype(BF16)

        @pl.when((jj == NDEV - 1) & (i == ni - 1))
        def _():
            for a in range(1, n):
                arrived(a, 2)
                passed[2][a].start()
            for a in range(1, n):
                copy(a, 0, sibling, me).wait_recv()
                for j in range(3):
                    copy(a, 4 + j, relayed[j], me).wait_recv()
            for a in range(n):
                mine[a].wait()
                for cp in sends(a):
                    cp.wait_send()

    any_spec = pl.BlockSpec(memory_space=pl.ANY)
    vec = pl.BlockSpec((1, D), lambda jj, i, o: (0, 0))
    outs = pl.pallas_call(
        body, name="proj_fwd_gather",
        grid_spec=pltpu.PrefetchScalarGridSpec(
            num_scalar_prefetch=1, grid=(NDEV, ni),
            in_specs=[pl.BlockSpec((tm, D), lambda jj, i, o: (jnp.where(jj == 0, i, ni - 1), 0))] + [vec] * 3
                     + [any_spec] * n,
            out_specs=[pl.BlockSpec((tm, SHARD), lambda jj, i, o: (i, o[jj])),
                       pl.BlockSpec((D, tm), lambda jj, i, o: (0, jnp.where(jj == 0, i, ni - 1)))]
                      + [any_spec] * n,
            scratch_shapes=[pltpu.VMEM((ni, tm, D), BF16), pltpu.VMEM((2, D, SHARD), BF16),
                            pltpu.SemaphoreType.DMA((n, 7)), pltpu.SemaphoreType.DMA((n, 7)),
                            pltpu.SemaphoreType.DMA((n,)), pltpu.SemaphoreType.DMA((2,))]),
        out_shape=[jax.ShapeDtypeStruct((s, NIN), BF16), jax.ShapeDtypeStruct((D, s), BF16),
                   jax.ShapeDtypeStruct((NDEV, D, SHARD), BF16)]
                  + [jax.ShapeDtypeStruct((NDEV,) + e.shape, e.dtype) for e in extras],
        compiler_params=_cp(("arbitrary", "arbitrary"), 56))(order, x, nw, scale, shift, w_shard, *extras)
    return outs[0], outs[1], outs[2], outs[3:]


def proj_bwd(ht, dproj, wg, smalls, order, x, dy, nw, scale, tt):
    s = dproj.shape[0]
    nk = s // tt
    n = len(smalls)
    rows_per_step = tt // nk
    last = 2 * NDEV

    def body(order_ref, ht_ref, dp_ref, w_ref, x_ref, dy_ref, nw_ref, sc_ref, *rest):
        small_in = rest[:n]
        gx_ref, st_ref, gw_ref, rwin_ref = rest[n:n + 4]
        small_out = rest[n + 4:2 * n + 4]
        acc, stage, dh, send_sems, recv_sems, local_sems, stage_sems = rest[2 * n + 4:]
        t, k = pl.program_id(0), pl.program_id(1)
        me_xyc = _coords()
        me = _dev_index(me_xyc)
        peers = [_flip(me_xyc, f) for f in FLIPS]

        def exchange(a, kf, src_arr, dst_arr):
            pid = _dev_index(peers[kf])
            mk = lambda dst: pltpu.make_async_remote_copy(
                src_ref=src_arr.at[pid], dst_ref=dst, send_sem=send_sems.at[a, kf], recv_sem=recv_sems.at[a, kf],
                device_id=peers[kf], device_id_type=MESH)
            return mk(dst_arr.at[me]), mk(dst_arr.at[pid])

        small_pairs = [exchange(1 + a, kf, small_in[a], small_out[a]) for kf in range(7) for a in range(n)]
        small_own = [pltpu.make_async_copy(small_in[a].at[me], small_out[a].at[me], local_sems.at[1 + a])
                     for a in range(n)]
        win_pairs = [exchange(0, kf, gw_ref, rwin_ref) for kf in range(7)]
        win_own = pltpu.make_async_copy(gw_ref.at[me], rwin_ref.at[me], local_sems.at[0])

        def to_hbm(jj):
            slab = me if jj == 7 else _dev_index(peers[jj])
            return pltpu.make_async_copy(stage.at[jj % 2], gw_ref.at[slab], stage_sems.at[jj % 2])

        @pl.when((t == 0) & (k == 0))
        def _():
            for cp in small_own:
                cp.start()
            for send, _ in small_pairs:
                send.start()

        @pl.when(t < NDEV)
        def _():
            p = jnp.dot(ht_ref[...], dp_ref[...], preferred_element_type=F32)

            @pl.when(k == 0)
            def _():
                acc[...] = p

            @pl.when(k > 0)
            def _():
                acc[...] += p

        for jj in range(NDEV):
            @pl.when((t == jj) & (k == nk - 1))
            def _(jj=jj):
                stage[jj % 2] = acc[...].astype(BF16)
                to_hbm(jj).start()

            @pl.when((t == jj + 1) & (k == 1))
            def _(jj=jj):
                to_hbm(jj).wait()
                if jj < 7:
                    win_pairs[jj][0].start()
                else:
                    win_own.start()

        def matmul_step():
            p = lax.dot_general(dp_ref[...], w_ref[...], NT, preferred_element_type=F32)
            slot = t % 2
            dh[slot] = jnp.where(k == 0, p, dh[slot] + p)

        def norm_step():
            g = dh.at[(t + 1) % 2][pl.ds(pl.multiple_of(k * rows_per_step, rows_per_step), rows_per_step), :]
            xf = x_ref[...]
            r = lax.rsqrt(jnp.mean(xf * xf, axis=-1, keepdims=True) + EPS)
            xh = xf * r
            dn = g * (1.0 + sc_ref[...])
            dxh = dn * nw_ref[...]
            gx_ref[...] = dy_ref[...] + r * (dxh - xh * jnp.mean(dxh * xh, axis=-1, keepdims=True))
            st_ref[0:1, :] += jnp.sum(g, axis=0, keepdims=True)
            st_ref[1:2, :] += jnp.sum(g * xh * nw_ref[...], axis=0, keepdims=True)
            st_ref[2:3, :] += jnp.sum(dn * xh, axis=0, keepdims=True)

        @pl.when((t == 0) & (k == 0))
        def _():
            st_ref[...] = jnp.zeros_like(st_ref)

        @pl.when(t == NDEV)
        def _():
            matmul_step()

        @pl.when((t > NDEV) & (t < last))
        def _():
            matmul_step()
            norm_step()

        @pl.when(t == last)
        def _():
            norm_step()

        @pl.when((t == last) & (k == nk - 1))
        def _():
            for _, recv in win_pairs + small_pairs:
                recv.wait_recv()
            for send, _ in win_pairs + small_pairs:
                send.wait_send()
            win_own.wait()
            for cp in small_own:
                cp.wait()

    any_spec = pl.BlockSpec(memory_space=pl.ANY)
    first = lambda t: t < NDEV
    slab = lambda t, k: jnp.where(t == last, NDEV - 1, k)
    chunk = pl.BlockSpec((rows_per_step, D), lambda t, k, o: (jnp.maximum((t - NDEV - 1) * nk + k, 0), 0))
    vec = pl.BlockSpec((1, D), lambda t, k, o: (0, 0))
    outs = pl.pallas_call(
        body, name="proj_bwd",
        grid_spec=pltpu.PrefetchScalarGridSpec(
            num_scalar_prefetch=1, grid=(last + 1, nk),
            in_specs=[pl.BlockSpec((D, tt), lambda t, k, o: (0, jnp.where(first(t), k, nk - 1))),
                      pl.BlockSpec((tt, SHARD), lambda t, k, o: (jnp.where(first(t), k, jnp.minimum(t, last - 1) - NDEV),
                                                                 jnp.where(first(t), o[jnp.minimum(t, NDEV - 1)],
                                                                           slab(t, k)))),
                      pl.BlockSpec((None, D, SHARD), lambda t, k, o: (jnp.where(first(t), 0, slab(t, k)), 0, 0)),
                      chunk, chunk, vec, vec]
                     + [any_spec] * n,
            out_specs=[chunk, pl.BlockSpec((8, D), lambda t, k, o: (0, 0))] + [any_spec] * (2 + n),
            scratch_shapes=[pltpu.VMEM((D, SHARD), F32), pltpu.VMEM((2, D, SHARD), BF16),
                            pltpu.VMEM((2, tt, D), F32),
                            pltpu.SemaphoreType.DMA((1 + n, 7)), pltpu.SemaphoreType.DMA((1 + n, 7)),
                            pltpu.SemaphoreType.DMA((1 + n,)), pltpu.SemaphoreType.DMA((2,))]),
        out_shape=[jax.ShapeDtypeStruct((s, D), F32), jax.ShapeDtypeStruct((8, D), F32),
                   jax.ShapeDtypeStruct((NDEV, D, SHARD), BF16), jax.ShapeDtypeStruct((NDEV, D, SHARD), BF16)]
                  + [jax.ShapeDtypeStruct(a.shape, a.dtype) for a in smalls],
        compiler_params=_cp(("arbitrary", "arbitrary"), 56))(order, ht, dproj, wg, x, dy, nw, scale, *smalls)
    return outs[0], outs[1], outs[3], outs[4:]


def matmul_tn(a, b, name, tk):
    s, m = a.shape
    n = b.shape[1]
    nk = s // tk

    def body(a_ref, b_ref, o_ref, acc_ref):
        k = pl.program_id(0)
        p = lax.dot_general(a_ref[...], b_ref[...], TN, preferred_element_type=F32)

        @pl.when(k == 0)
        def _():
            acc_ref[...] = p

        @pl.when(k > 0)
        def _():
            acc_ref[...] += p

        @pl.when(k == nk - 1)
        def _():
            o_ref[...] = acc_ref[...].astype(BF16)

    return pl.pallas_call(
        body, name=name, grid=(nk,),
        in_specs=[pl.BlockSpec((tk, m), lambda k: (k, 0)), pl.BlockSpec((tk, n), lambda k: (k, 0))],
        out_specs=pl.BlockSpec((m, n), lambda k: (0, 0)),
        out_shape=jax.ShapeDtypeStruct((m, n), BF16),
        scratch_shapes=[pltpu.VMEM((m, n), F32)],
        compiler_params=_cp(("arbitrary",)))(a, b)


def _head_matrices():
    lane = lax.broadcasted_iota(jnp.int32, (CB, CB), 0)
    col = lax.broadcasted_iota(jnp.int32, (CB, CB), 1)
    same = (lane // HD == col // HD).astype(BF16)
    lane_c = lax.broadcasted_iota(jnp.int32, (CB, LANES), 0)
    col_c = lax.broadcasted_iota(jnp.int32, (CB, LANES), 1)
    total = (lane_c // HD == col_c).astype(BF16)
    lane_e = lax.broadcasted_iota(jnp.int32, (LANES, CB), 0)
    col_e = lax.broadcasted_iota(jnp.int32, (LANES, CB), 1)
    expand = (lane_e == col_e // HD).astype(BF16)
    return same, total, expand


def _head_sum(x, m_ref):
    return jnp.dot(x.astype(BF16), m_ref[...], preferred_element_type=F32)


def _dot_hilo(x, m_ref):
    hi = x.astype(BF16)
    lo = (x - hi.astype(F32)).astype(BF16)
    return (jnp.dot(hi, m_ref[...], preferred_element_type=F32)
            + jnp.dot(lo, m_ref[...], preferred_element_type=F32))


def _to_residue_major(val, buf, out_ref, dil):
    rows = out_ref.shape[1]
    for k in range(val.shape[1] // LANES):
        lanes = slice(k * LANES, (k + 1) * LANES)
        buf[k] = val[:, lanes]
        for r in range(dil):
            out_ref[r, :, lanes] = buf.at[k][pl.ds(r, rows, stride=dil), :].astype(out_ref.dtype)


def _from_residue_major(ref, buf, dil):
    if dil == 1:
        return ref[0].astype(F32)
    rows, chunks = ref.shape[1], ref.shape[2] // LANES
    for k in range(chunks):
        for r in range(dil):
            buf.at[k][pl.ds(r, rows, stride=dil), :] = ref[r, :, k * LANES:(k + 1) * LANES].astype(F32)
    return jnp.concatenate([buf[k] for k in range(chunks)], axis=1)


def qkv_prep(proj, qw8, kw8, same, tm):
    s = proj.shape[0]
    items = []
    for g, d in enumerate(DILATIONS):
        items += [(g, "q", CB_Q + g, d), (g, "k", CB_K + g, d)] + ([(g, "v", CB_V + g, d)] if d > 1 else [])
    n = len(items)

    def body(*refs):
        ins, (qw_ref, kw_ref, same_ref), outs, buf = refs[:n], refs[n:n + 3], refs[n + 3:2 * n + 3], refs[-1]
        for idx, (_, kind, _, dil) in enumerate(items):
            val = ins[idx][...].astype(F32)
            if kind != "v":
                r = lax.rsqrt(_head_sum(val * val, same_ref) * (1.0 / HD) + EPS)
                val = val * r * (qw_ref if kind == "q" else kw_ref)[...]
            if dil == 1:
                outs[idx][0] = val.astype(BF16)
            else:
                _to_residue_major(val, buf, outs[idx], dil)

    full = lambda a: pl.BlockSpec(a.shape, lambda i: (0, 0))
    outs = pl.pallas_call(
        body, name="qkv_prep", grid=(s // tm,),
        in_specs=[pl.BlockSpec((tm, CB), lambda i, cb=cb: (i, cb)) for _, _, cb, _ in items]
                 + [full(qw8), full(kw8), full(same)],
        out_specs=[pl.BlockSpec((d, tm // d, CB), lambda i: (0, i, 0)) for _, _, _, d in items],
        out_shape=[jax.ShapeDtypeStruct((d, s // d, CB), BF16) for _, _, _, d in items],
        scratch_shapes=[pltpu.VMEM((CB // LANES, tm, LANES), F32)],
        compiler_params=_cp(("parallel",)))(*([proj] * n), qw8 * (HD ** -0.5), kw8, same)
    srcs = [[None, None, (proj, CB_V + g)] for g in range(len(DILATIONS))]
    for (g, kind, _, _), o in zip(items, outs):
        srcs[g]["qkv".index(kind)] = (o.reshape(s, CB), 0)
    return srcs


def stats_prep(da, lc, dc, g, dil, tm):
    s = da.shape[0]
    rows = tm // dil

    def body(da_ref, lc_ref, dc_ref, dap_ref, lcp_ref, dcp_ref, lt_ref, dt_ref, buf):
        if dil == 1:
            dap_ref[0] = da_ref[...]
        else:
            _to_residue_major(da_ref[...].astype(F32), buf, dap_ref, dil)
        for src, dst, dst_t in ((lc_ref, lcp_ref, lt_ref), (dc_ref, dcp_ref, dt_ref)):
            buf[0] = src[...]
            for r in range(dil):
                piece = buf.at[0][pl.ds(r, rows, stride=dil), :] if dil > 1 else buf[0]
                dst[r] = piece
                dst_t[r] = piece.T[0:NH, :]

    row = lambda w: pl.BlockSpec((tm, w), lambda i: (i, 0))
    rm = lambda w: pl.BlockSpec((dil, rows, w), lambda i: (0, i, 0))
    tr = pl.BlockSpec((dil, NH, rows), lambda i: (0, 0, i))
    length = s // dil
    dap, lcp, dcp, lt, dt = pl.pallas_call(
        body, name=f"stats_prep_g{g}", grid=(s // tm,),
        in_specs=[row(CB), row(LANES), row(LANES)],
        out_specs=[rm(CB), rm(LANES), rm(LANES), tr, tr],
        out_shape=[jax.ShapeDtypeStruct((dil, length, CB), BF16)]
                  + [jax.ShapeDtypeStruct((dil, length, LANES), F32)] * 2
                  + [jax.ShapeDtypeStruct((dil, NH, length), F32)] * 2,
        scratch_shapes=[pltpu.VMEM((CB // LANES, tm, LANES), F32)],
        compiler_params=_cp(("parallel",)))(da, lc, dc)
    return (dap.reshape(s, CB), lcp.reshape(s, LANES), dcp.reshape(s, LANES),
            lt.reshape(dil * NH, length), dt.reshape(dil * NH, length))


def qkv_grads_to_dproj(dproj, proj, grads, qw8, kw8, same, tm):
    s = dproj.shape[0]
    ni = s // tm
    flat = [(t.reshape(d, s // d, CB), d, kind, 3 * kind + g)
            for g, d in enumerate(DILATIONS) for kind, t in enumerate(grads[g])]
    nf = len(flat)
    nraw = 2 * len(DILATIONS)

    def body(*refs):
        dp_hbm, raws, ins = refs[nraw + nf + 4], refs[1:1 + nraw], refs[1 + nraw:1 + nraw + nf]
        qw_ref, kw_ref, same_ref = refs[1 + nraw + nf:4 + nraw + nf]
        gw_ref, stage, buf, sems = refs[5 + nraw + nf:]
        i = pl.program_id(0)
        slot = i % 2

        def slab(step, sl):
            return pltpu.make_async_copy(
                stage.at[sl], dp_hbm.at[pl.ds(pl.multiple_of(step * tm, tm), tm), pl.ds(CB_Q * CB, 9 * CB)],
                sems.at[sl])

        @pl.when(i == 0)
        def _():
            gw_ref[...] = jnp.zeros_like(gw_ref)

        @pl.when(i >= 2)
        def _():
            slab(i - 2, slot).wait()

        for ref, (_, d, kind, jj) in zip(ins, flat):
            cols = slice(jj * CB, (jj + 1) * CB)
            dn = _from_residue_major(ref, buf, d)
            if kind == 2:
                stage[slot, :, cols] = dn.astype(BF16)
                continue
            t = raws[jj][...].astype(F32)
            r = lax.rsqrt(_head_sum(t * t, same_ref) * (1.0 / HD) + EPS)
            xh = t * r
            gw_ref[kind:kind + 1, :] += jnp.sum(dn * xh, axis=0, keepdims=True)
            dxh = dn * (qw_ref if kind == 0 else kw_ref)[...]
            mean = _head_sum(dxh * xh, same_ref) * (1.0 / HD)
            stage[slot, :, cols] = (r * (dxh - xh * mean)).astype(BF16)
        slab(i, slot).start()

        @pl.when(i == ni - 1)
        def _():
            slab(i - 1, 1 - slot).wait()
            slab(i, slot).wait()

    full = lambda a: pl.BlockSpec(a.shape, lambda i: (0, 0))
    any_spec = pl.BlockSpec(memory_space=pl.ANY)
    return pl.pallas_call(
        body, name="qkv_grads_to_dproj", grid=(ni,),
        in_specs=[any_spec] + [pl.BlockSpec((tm, CB), lambda i, jb=jb: (i, CB_Q + jb)) for jb in range(nraw)]
                 + [pl.BlockSpec((d, tm // d, CB), lambda i: (0, i, 0)) for _, d, _, _ in flat]
                 + [full(qw8), full(kw8), full(same)],
        out_specs=[any_spec, pl.BlockSpec((8, CB), lambda i: (0, 0))],
        out_shape=[jax.ShapeDtypeStruct((s, NIN), BF16), jax.ShapeDtypeStruct((8, CB), F32)],
        input_output_aliases={0: 0},
        scratch_shapes=[pltpu.VMEM((2, tm, 9 * CB), BF16), pltpu.VMEM((CB // LANES, tm, LANES), F32),
                        pltpu.SemaphoreType.DMA((2,))],
        compiler_params=_cp(("arbitrary",)))(
            dproj, *([proj] * nraw), *[t for t, _, _, _ in flat], qw8, kw8, same)


def _lane_lo():
    return lax.broadcasted_iota(jnp.int32, (1, 2 * HD), 1) < HD


def _stack_heads(t, lo):
    zero = jnp.zeros_like(t)
    return jnp.concatenate([jnp.where(lo, t, zero), jnp.where(lo, zero, t)], axis=0)


def _masks(other_ok):
    qi = lax.broadcasted_iota(jnp.int32, (QB, QB), 0)
    kj = lax.broadcasted_iota(jnp.int32, (QB, QB), 1)
    return (kj >= qi) & other_ok, kj <= qi


SUB = 4


def _attn_specs(nb, dil):
    steps = nb // SUB
    main = lambda cb, w=CB: pl.BlockSpec((SUB * QB, w), lambda r, s: (r * steps + s, cb))
    prev = lambda cb: pl.BlockSpec((QB, CB), lambda r, s: (jnp.maximum(r * nb + SUB * s - 1, 0), cb))
    nxt = lambda cb: pl.BlockSpec((QB, CB), lambda r, s: (jnp.minimum(r * nb + SUB * (s + 1), dil * nb - 1), cb))
    return main, prev, nxt


def attn_fwd(q_src, k_src, v_src, g, dil):
    s = q_src[0].shape[0]
    nb = s // dil // QB
    main, prev, _ = _attn_specs(nb, dil)

    def body(q_ref, kp_ref, k_ref, vp_ref, v_ref, o_ref, l_ref, kbuf, vbuf):
        step = pl.program_id(1)
        kbuf[0:QB], kbuf[QB:] = kp_ref[...], k_ref[...]
        vbuf[0:QB], vbuf[QB:] = vp_ref[...], v_ref[...]
        lo = _lane_lo()
        head_lane = lax.broadcasted_iota(jnp.int32, (1, LANES), 1)

        def block(j, carry):
            r0 = pl.multiple_of(j * QB, QB)
            rows, krows = pl.ds(r0, QB), pl.ds(r0, 2 * QB)
            m_prev, m_cur = _masks(step * SUB + j > 0)
            mask = jnp.concatenate([m_prev, m_cur], axis=1)
            mask = jnp.concatenate([mask, mask], axis=0)
            lses = jnp.zeros((QB, LANES), F32)
            for i in range(NH // 2):
                sl = slice(2 * HD * i, 2 * HD * (i + 1))
                qs, ks, vv = q_ref[rows, sl], kbuf[krows, sl], vbuf[krows, sl]
                sc = lax.dot_general(_stack_heads(qs, lo), ks, NT, preferred_element_type=F32)
                sc = jnp.where(mask, sc, NEG)
                mx = jnp.max(sc, axis=-1, keepdims=True)
                p = jnp.exp(sc - mx)
                den = jnp.sum(p, axis=-1, keepdims=True)
                o = jnp.dot(p.astype(BF16), vv, preferred_element_type=F32) * (1.0 / den)
                lse = mx + jnp.log(den)
                o_ref[rows, sl] = jnp.where(lo, o[:QB], o[QB:]).astype(BF16)
                lses = jnp.where(head_lane == 2 * i, lse[:QB], jnp.where(head_lane == 2 * i + 1, lse[QB:], lses))
            l_ref[rows, :] = lses
            return carry

        lax.fori_loop(0, SUB, block, 0, unroll=True)

    return pl.pallas_call(
        body, name=f"attn_fwd_g{g}", grid=(dil, nb // SUB),
        in_specs=[main(q_src[1]), prev(k_src[1]), main(k_src[1]), prev(v_src[1]), main(v_src[1])],
        out_specs=[main(0), main(0, LANES)],
        out_shape=[jax.ShapeDtypeStruct((s, CB), BF16), jax.ShapeDtypeStruct((s, LANES), F32)],
        scratch_shapes=[pltpu.VMEM(((SUB + 1) * QB, CB), BF16)] * 2,
        compiler_params=_cp(("parallel", "parallel")))(q_src[0], k_src[0], k_src[0], v_src[0], v_src[0])


def attn_bwd_q(q_src, k_src, v_src, da, lc, dc, g, dil):
    s = q_src[0].shape[0]
    nb = s // dil // QB
    main, prev, _ = _attn_specs(nb, dil)

    def body(q_ref, kp_ref, k_ref, vp_ref, v_ref, da_ref, l_ref, d_ref, dq_ref, kbuf, vbuf):
        step = pl.program_id(1)
        kbuf[0:QB], kbuf[QB:] = kp_ref[...], k_ref[...]
        vbuf[0:QB], vbuf[QB:] = vp_ref[...], v_ref[...]
        lo = _lane_lo()

        def block(j, carry):
            r0 = pl.multiple_of(j * QB, QB)
            rows, krows = pl.ds(r0, QB), pl.ds(r0, 2 * QB)
            m_prev, m_cur = _masks(step * SUB + j > 0)
            mask = jnp.concatenate([m_prev, m_cur], axis=1)
            mask = jnp.concatenate([mask, mask], axis=0)
            lcols, dcols = l_ref[rows, :], d_ref[rows, :]
            for i in range(NH // 2):
                sl = slice(2 * HD * i, 2 * HD * (i + 1))
                qs, ks, vv, da2 = q_ref[rows, sl], kbuf[krows, sl], vbuf[krows, sl], da_ref[rows, sl]
                pair = lambda t: jnp.concatenate([t[:, 2 * i:2 * i + 1], t[:, 2 * i + 1:2 * i + 2]], axis=0)
                sc = lax.dot_general(_stack_heads(qs, lo), ks, NT, preferred_element_type=F32)
                sc = jnp.where(mask, sc, NEG)
                p = jnp.exp(sc - pair(lcols))
                dp = lax.dot_general(_stack_heads(da2, lo), vv, NT, preferred_element_type=F32)
                ds = p * (dp - pair(dcols))
                dq = jnp.dot(ds.astype(BF16), ks, preferred_element_type=F32)
                dq_ref[rows, sl] = (jnp.where(lo, dq[:QB], dq[QB:]) * (HD ** -0.5)).astype(BF16)
            return carry

        lax.fori_loop(0, SUB, block, 0, unroll=True)

    return pl.pallas_call(
        body, name=f"attn_bwd_q_g{g}", grid=(dil, nb // SUB),
        in_specs=[main(q_src[1]), prev(k_src[1]), main(k_src[1]), prev(v_src[1]), main(v_src[1]),
                  main(0), main(0, LANES), main(0, LANES)],
        out_specs=main(0), out_shape=jax.ShapeDtypeStruct((s, CB), BF16),
        scratch_shapes=[pltpu.VMEM(((SUB + 1) * QB, CB), BF16)] * 2,
        compiler_params=_cp(("parallel", "parallel")))(
            q_src[0], k_src[0], k_src[0], v_src[0], v_src[0], da, lc, dc)


def attn_bwd_kv(q_src, k_src, v_src, da, lt, dt, g, dil):
    s = q_src[0].shape[0]
    nb = s // dil // QB
    main, _, nxt = _attn_specs(nb, dil)

    def body(k_ref, v_ref, q_ref, qn_ref, da_ref, dan_ref, l_ref, ln_ref, d_ref, dn_ref, dk_ref, dv_ref,
             qbuf, dabuf, lbuf, dbuf):
        step = pl.program_id(1)
        qbuf[0:SUB * QB], qbuf[SUB * QB:] = q_ref[...], qn_ref[...]
        dabuf[0:SUB * QB], dabuf[SUB * QB:] = da_ref[...], dan_ref[...]
        for c in range(SUB):
            lbuf[c], dbuf[c] = l_ref[:, c * QB:(c + 1) * QB], d_ref[:, c * QB:(c + 1) * QB]
        lbuf[SUB], dbuf[SUB] = ln_ref[...], dn_ref[...]
        lo = _lane_lo()
        kj = lax.broadcasted_iota(jnp.int32, (QB, QB), 0)
        qi = lax.broadcasted_iota(jnp.int32, (QB, QB), 1)

        def block(j, carry):
            r0 = pl.multiple_of(j * QB, QB)
            rows, qrows = pl.ds(r0, QB), pl.ds(r0, 2 * QB)
            mask = jnp.concatenate([kj <= qi, (kj >= qi) & (step * SUB + j < nb - 1)], axis=1)
            mask = jnp.concatenate([mask, mask], axis=1)
            lrow = jnp.concatenate([lbuf[j], lbuf[j + 1]], axis=1)
            drow = jnp.concatenate([dbuf[j], dbuf[j + 1]], axis=1)
            for i in range(NH // 2):
                sl = slice(2 * HD * i, 2 * HD * (i + 1))
                q2, da2 = _stack_heads(qbuf[qrows, sl], lo), _stack_heads(dabuf[qrows, sl], lo)
                ks, vv = k_ref[rows, sl], v_ref[rows, sl]
                pair = lambda t: jnp.concatenate([t[2 * i:2 * i + 1, :], t[2 * i + 1:2 * i + 2, :]], axis=1)
                sc = lax.dot_general(ks, q2, NT, preferred_element_type=F32)
                sc = jnp.where(mask, sc, NEG)
                p = jnp.exp(sc - pair(lrow))
                dp = lax.dot_general(vv, da2, NT, preferred_element_type=F32)
                ds = p * (dp - pair(drow))
                dv_ref[rows, sl] = jnp.dot(p.astype(BF16), da2, preferred_element_type=F32).astype(BF16)
                dk_ref[rows, sl] = jnp.dot(ds.astype(BF16), q2, preferred_element_type=F32).astype(BF16)
            return carry

        lax.fori_loop(0, SUB, block, 0, unroll=True)

    steps = nb // SUB
    t_main = pl.BlockSpec((NH, SUB * QB), lambda r, s: (r, s))
    t_nxt = pl.BlockSpec((NH, QB), lambda r, s: (r, jnp.minimum(SUB * (s + 1), nb - 1)))
    out = jax.ShapeDtypeStruct((s, CB), BF16)
    return pl.pallas_call(
        body, name=f"attn_bwd_kv_g{g}", grid=(dil, steps),
        in_specs=[main(k_src[1]), main(v_src[1]), main(q_src[1]), nxt(q_src[1]),
                  main(0), nxt(0), t_main, t_nxt, t_main, t_nxt],
        out_specs=[main(0), main(0)], out_shape=[out, out],
        scratch_shapes=[pltpu.VMEM(((SUB + 1) * QB, CB), BF16)] * 2 + [pltpu.VMEM((SUB + 1, NH, QB), F32)] * 2,
        compiler_params=_cp(("parallel", "parallel")))(
            k_src[0], v_src[0], q_src[0], q_src[0], da, da, lt, lt, dt, dt)


def _conv_taps(u, u_prev, first):
    tm = u.shape[0]
    row = lax.broadcasted_iota(jnp.int32, (tm, 1), 0)
    up = jnp.where(first, 0.0, u_prev)
    u1 = jnp.where(row == 0, up[HALO - 1:HALO, :], pltpu.roll(u, 1, 0))
    u2 = jnp.where(row == 0, up[HALO - 2:HALO - 1, :],
                   jnp.where(row == 1, up[HALO - 1:HALO, :], pltpu.roll(u, 2, 0)))
    return u1, u2


def mid_fwd(proj, o_g, lse_g, conv_w, expand, tm):
    s = proj.shape[0]
    hb = tm // HALO

    def body(ba_ref, ca_ref, xa_ref, za_ref, cah_ref, xah_ref, zb_ref,
             o0, o1, o2, l0, l1, l2, w_ref, exp_ref, ya_ref, yb_ref, at_ref, lc_ref, buf_o, buf_l):
        first = pl.program_id(0) == 0
        u = ca_ref[...].astype(F32) * xa_ref[...].astype(F32)
        u1, u2 = _conv_taps(u, cah_ref[...].astype(F32) * xah_ref[...].astype(F32), first)
        conv = w_ref[0:1, :] * u2 + w_ref[1:2, :] * u1 + w_ref[2:3, :] * u
        ya_ref[...] = (ba_ref[...].astype(F32) * conv * _silu(za_ref[...].astype(F32))).astype(BF16)
        ls = [_from_residue_major(l, buf_l.at[g], d) for g, (l, d) in enumerate(zip((l0, l1, l2), DILATIONS))]
        mx = jnp.maximum(jnp.maximum(ls[0], ls[1]), ls[2])
        es = [jnp.exp(l - mx) for l in ls]
        den = es[0] + es[1] + es[2]
        attn = jnp.zeros((tm, CB), F32)
        for e, o, d in zip(es, (o0, o1, o2), DILATIONS):
            attn = attn + _dot_hilo(e / den, exp_ref) * _from_residue_major(o, buf_o, d)
        at_ref[...] = attn
        lc_ref[...] = mx + jnp.log(den)
        yb_ref[...] = (attn * _silu(zb_ref[...].astype(F32))).astype(BF16)

    col = lambda j: pl.BlockSpec((tm, D), lambda i: (i, j))
    halo = lambda j: pl.BlockSpec((HALO, D), lambda i: (jnp.maximum(i * hb - 1, 0), j))
    loc = lambda w: pl.BlockSpec((tm, w), lambda i: (i, 0))
    rm = lambda w: [pl.BlockSpec((d, tm // d, w), lambda i: (0, i, 0)) for d in DILATIONS]
    rm_view = lambda ts, w: [t.reshape(d, s // d, w) for t, d in zip(ts, DILATIONS)]
    return pl.pallas_call(
        body, name="mid_fwd", grid=(s // tm,),
        in_specs=[col(0), col(1), col(2), col(3), halo(1), halo(2),
                  pl.BlockSpec((tm, CB), lambda i: (i, CB_ZB))] + rm(CB) + rm(LANES)
                 + [pl.BlockSpec((3, D), lambda i: (0, 0)), pl.BlockSpec(expand.shape, lambda i: (0, 0))],
        out_specs=[loc(D), loc(CB), loc(CB), loc(LANES)],
        out_shape=[jax.ShapeDtypeStruct((s, D), BF16), jax.ShapeDtypeStruct((s, CB), BF16),
                   jax.ShapeDtypeStruct((s, CB), F32), jax.ShapeDtypeStruct((s, LANES), F32)],
        scratch_shapes=[pltpu.VMEM((CB // LANES, tm, LANES), F32), pltpu.VMEM((3, 1, tm, LANES), F32)],
        compiler_params=_cp(("parallel",)))(
            proj, proj, proj, proj, proj, proj, proj, *rm_view(o_g, CB), *rm_view(lse_g, LANES), conv_w, expand)


def mid_bwd(dproj, proj, dya, conv_w, tm):
    s = proj.shape[0]
    hb = tm // HALO
    nblk = s // tm
    last_h = s // HALO - 1

    def body(_, ba_ref, ca_ref, xa_ref, za_ref, cah_ref, xah_ref, ban_ref, zan_ref, dy_ref, dyn_ref, w_ref,
             o_ref, gw_ref):
        i = pl.program_id(0)
        ba, ca, xa, za = (t[...].astype(F32) for t in (ba_ref, ca_ref, xa_ref, za_ref))
        u = ca * xa
        u1, u2 = _conv_taps(u, cah_ref[...].astype(F32) * xah_ref[...].astype(F32), i == 0)
        w0, w1, w2 = w_ref[0:1, :], w_ref[1:2, :], w_ref[2:3, :]
        conv = w0 * u2 + w1 * u1 + w2 * u
        sg = jax.nn.sigmoid(za)
        sz = za * sg
        dy = dy_ref[...].astype(F32)
        dconv = dy * ba * sz
        dcn = dyn_ref[...].astype(F32) * ban_ref[...].astype(F32) * _silu(zan_ref[...].astype(F32))
        dcn = jnp.where(i == nblk - 1, 0.0, dcn)
        row = lax.broadcasted_iota(jnp.int32, (tm, 1), 0)
        d1 = jnp.where(row == tm - 1, dcn[0:1, :], pltpu.roll(dconv, tm - 1, 0))
        d2 = jnp.where(row == tm - 2, dcn[0:1, :],
                       jnp.where(row == tm - 1, dcn[1:2, :], pltpu.roll(dconv, tm - 2, 0)))
        du = w2 * dconv + w1 * d1 + w0 * d2
        o_ref[:, 0:D] = (dy * conv * sz).astype(BF16)
        o_ref[:, D:2 * D] = (du * xa).astype(BF16)
        o_ref[:, 2 * D:3 * D] = (du * ca).astype(BF16)
        o_ref[:, 3 * D:4 * D] = (dy * ba * conv * (sg * (1.0 + za * (1.0 - sg)))).astype(BF16)

        @pl.when(i == 0)
        def _():
            gw_ref[...] = jnp.zeros_like(gw_ref)

        gw_ref[0:1, :] += jnp.sum(dconv * u2, axis=0, keepdims=True)
        gw_ref[1:2, :] += jnp.sum(dconv * u1, axis=0, keepdims=True)
        gw_ref[2:3, :] += jnp.sum(dconv * u, axis=0, keepdims=True)

    col = lambda j: pl.BlockSpec((tm, D), lambda i: (i, j))
    halo_prev = lambda j: pl.BlockSpec((HALO, D), lambda i: (jnp.maximum(i * hb - 1, 0), j))
    halo_next = lambda j: pl.BlockSpec((HALO, D), lambda i: (jnp.minimum((i + 1) * hb, last_h), j))
    return pl.pallas_call(
        body, name="mid_bwd", grid=(nblk,),
        in_specs=[pl.BlockSpec(memory_space=pl.ANY), col(0), col(1), col(2), col(3),
                  halo_prev(1), halo_prev(2), halo_next(0), halo_next(3),
                  pl.BlockSpec((tm, D), lambda i: (i, 0)), halo_next(0),
                  pl.BlockSpec((3, D), lambda i: (0, 0))],
        out_specs=[pl.BlockSpec((tm, 4 * D), lambda i: (i, 0)), pl.BlockSpec((8, D), lambda i: (0, 0))],
        out_shape=[jax.ShapeDtypeStruct((s, NIN), BF16), jax.ShapeDtypeStruct((8, D), F32)],
        input_output_aliases={0: 0},
        compiler_params=_cp(("arbitrary",)))(dproj, proj, proj, proj, proj, proj, proj, proj, proj, dya, dya, conv_w)


def tail(proj, ya, yb, attn, x, target, gate, pa_w, pb_w, wo_w, total, tm):
    s = proj.shape[0]
    ni = s // tm
    ncol = NIN - CB_ZB * CB

    def body(ya_ref, yb_ref, ga_ref, gb_ref, zb_ref, at_ref, x_ref, t_ref, gate_ref, pa_ref, pb_ref, wo_ref,
             tot_ref, dp_hbm, dy_ref, dya_ref, da_ref, dc_ref, mg_ref, do_ref, dpa_ref, dpb_ref, st_ref,
             stage, sems):
        i = pl.program_id(0)
        slot = i % 2

        def slab(step, sl):
            return pltpu.make_async_copy(
                stage.at[sl], dp_hbm.at[pl.ds(pl.multiple_of(step * tm, tm), tm), pl.ds(CB_ZB * CB, ncol)],
                sems.at[sl])

        @pl.when(i == 0)
        def _():
            st_ref[...] = jnp.zeros_like(st_ref)

        @pl.when(i >= 2)
        def _():
            slab(i - 2, slot).wait()

        gate_v = gate_ref[...]
        pa = jnp.dot(ya_ref[...], pa_ref[...], preferred_element_type=F32)
        pb = jnp.dot(yb_ref[...], pb_ref[...], preferred_element_type=F32)
        sa = jax.nn.sigmoid(ga_ref[...].astype(F32))
        sb = jax.nn.sigmoid(gb_ref[...].astype(F32))
        merged = (sa * pa + sb * pb).astype(BF16)
        mg_ref[...] = merged
        out = jnp.dot(merged, wo_ref[...], preferred_element_type=F32)
        err = x_ref[...] + gate_v * out - t_ref[...]
        dy = err * (1.0 / D)
        dy_ref[...] = dy
        st_ref[0:1, :] += jnp.sum(dy * out, axis=0, keepdims=True)
        st_ref[1:2, :] += jnp.sum(err * err, axis=0, keepdims=True)
        dout = (gate_v * dy).astype(BF16)
        do_ref[...] = dout
        dmg = lax.dot_general(dout, wo_ref[...], NT, preferred_element_type=F32)
        dpa = (dmg * sa).astype(BF16)
        dpb = (dmg * sb).astype(BF16)
        dpa_ref[...] = dpa
        dpb_ref[...] = dpb
        stage[slot, :, CB:CB + D] = (dmg * pa * sa * (1.0 - sa)).astype(BF16)
        stage[slot, :, CB + D:] = (dmg * pb * sb * (1.0 - sb)).astype(BF16)
        dya_ref[...] = lax.dot_general(dpa, pa_ref[...], NT, preferred_element_type=F32).astype(BF16)
        dyb = lax.dot_general(dpb, pb_ref[...], NT, preferred_element_type=F32)
        zb = zb_ref[...].astype(F32)
        sg = jax.nn.sigmoid(zb)
        attn_v = at_ref[...]
        dattn = dyb * (zb * sg)
        da_ref[...] = dattn.astype(BF16)
        stage[slot, :, 0:CB] = (dyb * attn_v * (sg * (1.0 + zb * (1.0 - sg)))).astype(BF16)
        dc_ref[...] = _dot_hilo(dattn * attn_v, tot_ref)

        slab(i, slot).start()

        @pl.when(i == ni - 1)
        def _():
            slab(i - 1, 1 - slot).wait()
            slab(i, slot).wait()

    row = lambda w: pl.BlockSpec((tm, w), lambda i: (i, 0))
    pcol = lambda w, jb: pl.BlockSpec((tm, w), lambda i: (i, jb))
    full = lambda a: pl.BlockSpec(a.shape, lambda i: (0, 0))
    return pl.pallas_call(
        body, name="tail", grid=(ni,),
        in_specs=[row(D), row(CB), pcol(D, 9), pcol(D, 10), pcol(CB, CB_ZB), row(CB), row(D), row(D),
                  pl.BlockSpec((1, D), lambda i: (0, 0)), full(pa_w), full(pb_w), full(wo_w), full(total)],
        out_specs=[pl.BlockSpec(memory_space=pl.ANY),
                   row(D), row(D), row(CB), row(LANES), row(D), row(D), row(D), row(D),
                   pl.BlockSpec((8, D), lambda i: (0, 0))],
        out_shape=[jax.ShapeDtypeStruct((s, NIN), BF16), jax.ShapeDtypeStruct((s, D), F32),
                   jax.ShapeDtypeStruct((s, D), BF16), jax.ShapeDtypeStruct((s, CB), BF16),
                   jax.ShapeDtypeStruct((s, LANES), F32)] + [jax.ShapeDtypeStruct((s, D), BF16)] * 4
                  + [jax.ShapeDtypeStruct((8, D), F32)],
        scratch_shapes=[pltpu.VMEM((2, tm, ncol), BF16), pltpu.SemaphoreType.DMA((2,))],
        compiler_params=_cp(("arbitrary",), 56))(
            ya, yb, proj, proj, proj, attn, x, target, gate, pa_w, pb_w, wo_w, total)


def _local_step(x, target, shift, scale, gate, norm_w, conv_w, qw, kw, w_shard, small_shards, me_xyc):
    qw8, kw8 = jnp.tile(qw, (1, NH)), jnp.tile(kw, (1, NH))
    same, total, expand = _head_matrices()
    proj, ht, wg, (pa_g, pb_g, wo_g) = proj_fwd_gather(
        x, norm_w, scale, shift, w_shard, small_shards, gather_order(me_xyc), 1024)
    pa_w, wo_w = pa_g.reshape(D, D), wo_g.reshape(D, D)
    pb_w = pb_g.transpose(1, 0, 2).reshape(CB, D)
    srcs = qkv_prep(proj, qw8, kw8, same, 512)
    o_g, lse_g = zip(*[attn_fwd(*srcs[g], g, d) for g, d in enumerate(DILATIONS)])
    ya, yb, attn, lc = mid_fwd(proj, o_g, lse_g, conv_w, expand, 512)
    dproj, dy, dya, da, dc, merged, dout, dpa, dpb, st_tail = tail(
        proj, ya, yb, attn, x, target, gate, pa_w, pb_w, wo_w, total, 256)
    g_wo = matmul_tn(merged, dout, "grad_w_out", 1024)
    g_pa = matmul_tn(ya, dpa, "grad_w_br_conv", 1024)
    g_pb = matmul_tn(yb, dpb, "grad_w_br_attn", 1024)
    dproj, st_conv = mid_bwd(dproj, proj, dya, conv_w, 512)
    grads = []
    for g, d in enumerate(DILATIONS):
        da_p, lc_p, dc_p, lt, dt = stats_prep(da, lc, dc, g, d, 2048)
        dq = attn_bwd_q(*srcs[g], da_p, lc_p, dc_p, g, d)
        dk, dv = attn_bwd_kv(*srcs[g], da_p, lt, dt, g, d)
        grads.append((dq, dk, dv))
    dproj, gw_qk = qkv_grads_to_dproj(dproj, proj, grads, qw8, kw8, same, 512)
    slabs = [g_pa.reshape(NDEV, 128, D), g_pb.reshape(CB, NDEV, 128).transpose(1, 0, 2), g_wo.reshape(NDEV, 128, D)]
    grad_x, st_norm, r_win, (r_pa, r_pb, r_wo) = proj_bwd(
        ht, dproj, wg, slabs, scatter_order(me_xyc), x, dy, norm_w, scale, 1024)
    dmod = jnp.concatenate([st_norm[0:1], st_norm[1:2], st_tail[0:1]], axis=1)
    loss_part = (0.5 / D) * jnp.sum(st_tail[1])
    gw_heads = gw_qk[0:2].reshape(2, NH, HD).sum(axis=1)
    small = dict(dmod=dmod, norm_w=st_norm[2:3], conv_w=st_conv[0:3],
                 q_norm_w=gw_heads[0:1], k_norm_w=gw_heads[1:2], loss=loss_part)
    return grad_x, small, (r_win, r_pa, r_pb, r_wo)


def kernel(x, c, w_ada, b_ada, norm_w, w_in, conv_w, q_norm_w, k_norm_w, w_br_conv, w_br_attn, w_out, loss_target, m_w_ada, m_b_ada, m_norm_w, m_w_in, m_conv_w, m_q_norm_w, m_k_norm_w, m_w_br_conv, m_w_br_attn, m_w_out, v_w_ada, v_b_ada, v_norm_w, v_w_in, v_conv_w, v_q_norm_w, v_k_norm_w, v_w_br_conv, v_w_br_attn, v_w_out):
    me_xyc = (lax.axis_index("x"), lax.axis_index("y"), lax.axis_index("c"))
    me = _dev_index(me_xyc)
    ncol = w_ada.shape[2]

    conv_pad = jnp.zeros((8, 128), F32).at[0:3].set(conv_w[0])
    c_all, conv_all = all_gather([c, conv_pad], "gather_cond")
    conv_full = conv_all[:, 0:3].transpose(1, 0, 2).reshape(3, D)
    c_all = c_all.reshape(NDEV, D)

    b_cols = lax.dynamic_slice(b_ada, (0, me * ncol), (1, ncol))
    mod_cols = ada_fwd(c_all, w_ada[0], b_cols)
    (mod_all,) = all_gather([mod_cols], "gather_mod")
    mod = lax.dynamic_index_in_dim(mod_all, me, axis=1, keepdims=False).reshape(1, 3 * D)
    shift, scale, gate = mod[:, 0:D], mod[:, D:2 * D], mod[:, 2 * D:3 * D]

    grad_x, small, (r_win, r_pa, r_pb, r_wo) = _local_step(
        x[0], loss_target[0], shift, scale, gate, norm_w, conv_full, q_norm_w, k_norm_w,
        w_in[0].astype(BF16), [w_br_conv[0].astype(BF16), w_br_attn[0].astype(BF16), w_out[0].astype(BF16)], me_xyc)

    packed = jnp.concatenate(
        [small["dmod"], small["norm_w"], small["conv_w"].reshape(1, 3 * D), small["q_norm_w"], small["k_norm_w"],
         jnp.full((1, 128), small["loss"], F32)], axis=1)
    (packed_all,) = all_gather([packed], "gather_small")
    tot = sum_parts(packed_all)
    loss = tot[0, 7 * D + 2 * HD]
    dmod_all = packed_all[:, 0, 0:3 * D]
    g_b_ada = tot[:, 0:3 * D]
    g_norm_w = tot[:, 3 * D:4 * D]
    g_conv = lax.dynamic_slice(tot[:, 4 * D:7 * D].reshape(3, D), (0, me * 128), (3, 128))
    g_qn = tot[:, 7 * D:7 * D + HD]
    g_kn = tot[:, 7 * D + HD:7 * D + 2 * HD]
    g_w_ada = ada_bwd(c_all.T, lax.dynamic_slice(dmod_all, (0, me * ncol), (NDEV, ncol)))

    def upd(parts, w, m, v, name, rows):
        shape = w.shape
        w2, m2, v2 = (t.reshape(shape[-2:]) for t in (w, m, v))
        return [t.reshape(shape) for t in adamw(parts, w2, m2, v2, name, rows)]

    res = {
        "w_ada": upd(g_w_ada[None], w_ada, m_w_ada, v_w_ada, "adamw_w_ada", 256),
        "b_ada": upd(g_b_ada[None], b_ada, m_b_ada, v_b_ada, "adamw_b_ada", 1),
        "norm_w": upd(g_norm_w[None], norm_w, m_norm_w, v_norm_w, "adamw_norm_w", 1),
        "w_in": upd(r_win, w_in, m_w_in, v_w_in, "adamw_w_in", 128),
        "conv_w": upd(g_conv[None], conv_w, m_conv_w, v_conv_w, "adamw_conv_w", 3),
        "q_norm_w": upd(g_qn[None], q_norm_w, m_q_norm_w, v_q_norm_w, "adamw_q_norm_w", 1),
        "k_norm_w": upd(g_kn[None], k_norm_w, m_k_norm_w, v_k_norm_w, "adamw_k_norm_w", 1),
        "w_br_conv": upd(r_pa, w_br_conv, m_w_br_conv, v_w_br_conv, "adamw_w_br_conv", 128),
        "w_br_attn": upd(r_pb, w_br_attn, m_w_br_attn, v_w_br_attn, "adamw_w_br_attn", 512),
        "w_out": upd(r_wo, w_out, m_w_out, v_w_out, "adamw_w_out", 128),
    }
    names = ["w_ada", "b_ada", "norm_w", "w_in", "conv_w", "q_norm_w", "k_norm_w", "w_br_conv", "w_br_attn", "w_out"]
    return (loss, grad_x[None], *[res[n][0] for n in names], *[res[n][1] for n in names],
            *[res[n][2] for n in names], *[res[n][3] for n in names])
```

```python
import jax
import jax.numpy as jnp
from jax import lax
from jax.experimental import pallas as pl
from jax.experimental.pallas import tpu as pltpu

F32, BF16 = jnp.float32, jnp.bfloat16
D = 1024
NIN = 11264
NDEV = 8
SHARD = NIN // NDEV
HD = 64
NH = 8
QB = 128
CB = 512
CB_Q, CB_K, CB_V, CB_ZB = 8, 11, 14, 17
DILATIONS = (1, 4, 16)
EPS = 1e-6
NEG = -1e30
HALO = 16
LANES = 128
MESH = pl.DeviceIdType.MESH

ADAM_LR, ADAM_B1, ADAM_B2, ADAM_EPS, ADAM_WD, ADAM_STEP = 0.001, 0.9, 0.999, 1e-08, 0.01, 10

NT = (((1,), (1,)), ((), ()))
TN = (((0,), (0,)), ((), ()))


def _cp(sem, vmem_mb=48):
    return pltpu.CompilerParams(dimension_semantics=sem, vmem_limit_bytes=vmem_mb << 20)


def _silu(z):
    return z * jax.nn.sigmoid(z)


def _coords():
    return lax.axis_index("x"), lax.axis_index("y"), lax.axis_index("c")


FLIPS = [(fx, fy, fc) for fx in (0, 1) for fy in (0, 1) for fc in (0, 1)][1:]


def all_gather(arrs, name):
    n = len(arrs)

    def body(*refs):
        ins, outs = refs[:n], refs[n:2 * n]
        send_sems, recv_sems, local_sems = refs[2 * n:]
        me_xyc = _coords()
        me = _dev_index(me_xyc)
        peers = [_flip(me_xyc, f) for f in FLIPS]

        def copy(a, k, block):
            return pltpu.make_async_remote_copy(
                src_ref=ins[a], dst_ref=outs[a].at[block], send_sem=send_sems.at[a, k], recv_sem=recv_sems.at[a, k],
                device_id=peers[k], device_id_type=MESH)

        mine = [pltpu.make_async_copy(ins[a], outs[a].at[me], local_sems.at[a]) for a in range(n)]
        sends = [copy(a, k, me) for k in range(7) for a in range(n)]
        for cp in mine + sends:
            cp.start()
        for k in range(7):
            for a in range(n):
                copy(a, k, _dev_index(peers[k])).wait_recv()
        for cp in sends:
            cp.wait_send()
        for cp in mine:
            cp.wait()

    any_spec = pl.BlockSpec(memory_space=pl.ANY)
    return pl.pallas_call(
        body, name=name,
        out_shape=[jax.ShapeDtypeStruct((NDEV,) + a.shape, a.dtype) for a in arrs],
        in_specs=[any_spec] * n, out_specs=[any_spec] * n,
        scratch_shapes=[pltpu.SemaphoreType.DMA((n, 7)), pltpu.SemaphoreType.DMA((n, 7)),
                        pltpu.SemaphoreType.DMA((n,))],
    )(*arrs)


def _flip(dev, f):
    return tuple(1 - v if b else v for v, b in zip(dev, f))


def _dev_index(dev):
    return 4 * dev[0] + 2 * dev[1] + dev[2]


def _chip_order(x, y, c):
    xor = lambda a, b: a + b - 2 * a * b
    return [(xor(x, 1 - c), xor(y, c)), (xor(x, c), xor(y, 1 - c)), (1 - x, 1 - y)]


def gather_order(me_xyc):
    x, y, c = me_xyc
    chips = _chip_order(x, y, c)
    devs = [(x, y, c), (x, y, 1 - c), (*chips[0], c), (*chips[1], c),
            (*chips[1], 1 - c), (*chips[0], 1 - c), (*chips[2], c), (*chips[2], 1 - c)]
    return jnp.stack([_dev_index(d) for d in devs]).astype(jnp.int32)


def scatter_order(me_xyc):
    devs = [_flip(me_xyc, f) for f in FLIPS] + [me_xyc]
    return jnp.stack([_dev_index(d) for d in devs]).astype(jnp.int32)


def ada_fwd(c_all, w_ada, b_cols):
    def body(c_ref, w_ref, b_ref, o_ref):
        a = _silu(c_ref[...]).astype(BF16)
        o_ref[...] = jnp.dot(a, w_ref[...].astype(BF16), preferred_element_type=F32) + b_ref[...]

    return pl.pallas_call(body, name="ada_fwd",
                          out_shape=jax.ShapeDtypeStruct((NDEV, w_ada.shape[1]), F32))(c_all, w_ada, b_cols)


def ada_bwd(c_all_t, dmod_cols):
    def body(c_ref, d_ref, o_ref):
        at = _silu(c_ref[...])
        acc = at[:, 0:1] * d_ref[0:1, :]
        for b in range(1, NDEV):
            acc = acc + at[:, b:b + 1] * d_ref[b:b + 1, :]
        o_ref[...] = acc

    return pl.pallas_call(body, name="ada_bwd",
                          out_shape=jax.ShapeDtypeStruct((D, dmod_cols.shape[1]), F32))(c_all_t, dmod_cols)


def sum_parts(parts):
    def body(p_ref, o_ref):
        acc = p_ref[0]
        for b in range(1, NDEV):
            acc = acc + p_ref[b]
        o_ref[...] = acc

    return pl.pallas_call(body, name="sum_parts",
                          out_shape=jax.ShapeDtypeStruct(parts.shape[1:], F32))(parts)


def adamw(parts, w, m, v, name, rows):
    n, r, ccols = parts.shape

    def body(p_ref, w_ref, m_ref, v_ref, g_ref, d_ref, nm_ref, nv_ref):
        g = p_ref[0].astype(F32)
        for b in range(1, n):
            g = g + p_ref[b].astype(F32)
        nm = ADAM_B1 * m_ref[...] + (1.0 - ADAM_B1) * g
        nv = ADAM_B2 * v_ref[...] + (1.0 - ADAM_B2) * (g * g)
        g_ref[...] = g
        nm_ref[...] = nm
        nv_ref[...] = nv
        m_hat = nm / (1.0 - ADAM_B1 ** ADAM_STEP)
        v_hat = nv / (1.0 - ADAM_B2 ** ADAM_STEP)
        d_ref[...] = -ADAM_LR * (m_hat / (jnp.sqrt(v_hat) + ADAM_EPS) + ADAM_WD * w_ref[...])

    blk = pl.BlockSpec((rows, ccols), lambda i: (i, 0))
    out = jax.ShapeDtypeStruct((r, ccols), F32)
    return pl.pallas_call(
        body, name=name, grid=(r // rows,),
        in_specs=[pl.BlockSpec((n, rows, ccols), lambda i: (0, i, 0)), blk, blk, blk],
        out_specs=[blk] * 4, out_shape=[out] * 4, compiler_params=_cp(("parallel",)))(parts, w, m, v)


def proj_fwd_gather(x, nw, scale, shift, w_shard, extras, order, tm):
    s = x.shape[0]
    ni = s // tm
    n = 1 + len(extras)
    mid = ni - 2

    def body(order_ref, x_ref, nw_ref, sc_ref, sh_ref, *refs):
        ins, o_ref, ht_ref, outs = refs[:n], refs[n], refs[n + 1], refs[n + 2:2 * n + 2]
        h_all, wbuf, send_sems, recv_sems, local_sems, load_sems = refs[2 * n + 2:]
        jj, i = pl.program_id(0), pl.program_id(1)
        x, y, c = _coords()
        me, sibling = (x, y, c), (x, y, 1 - c)
        chips = _chip_order(x, y, c)
        relayed = [(*chips[1], 1 - c), (*chips[0], 1 - c), (*chips[2], 1 - c)]

        def slot(a, dev):
            return outs[a].at[_dev_index(dev)]

        def copy(a, k, block, to, src=None):
            return pltpu.make_async_remote_copy(
                src_ref=slot(a, block) if src is None else src, dst_ref=slot(a, block),
                send_sem=send_sems.at[a, k], recv_sem=recv_sems.at[a, k], device_id=to, device_id_type=MESH)

        mine = [pltpu.make_async_copy(ins[a], slot(a, me), local_sems.at[a]) for a in range(n)]
        to_sibling = [copy(a, 0, me, sibling, src=ins[a]) for a in range(n)]
        to_chip = [[copy(a, 1 + j, me, (*chips[j], c), src=ins[a]) for a in range(n)] for j in range(2)]
        onward = [copy(a, 3, (*chips[1], c), (*chips[0], c)) for a in range(n)]
        passed = [[copy(a, 4 + j, (*ch, c), sibling) for a in range(n)] for j, ch in enumerate(chips)]
        sends = lambda a: [to_sibling[a], to_chip[0][a], to_chip[1][a], onward[a]] + [passed[j][a] for j in range(3)]

        def arrived(a, j):
            copy(a, 1 + j, (*chips[j], c), me).wait_recv()

        def load(row):
            return pltpu.make_async_copy(outs[0].at[order_ref[row]], wbuf.at[row % 2], load_sems.at[row % 2])

        @pl.when((jj == 0) & (i == 0))
        def _():
            for cp in mine:
                cp.start()
            to_sibling[0].start()
            to_chip[0][0].start()
            pltpu.make_async_copy(ins[0], wbuf.at[0], load_sems.at[0]).start()

        @pl.when((jj == 1) & (i == 0))
        def _():
            to_chip[1][0].start()

        @pl.when((jj == 4) & (i == 0))
        def _():
            for a in range(1, n):
                to_sibling[a].start()
                to_chip[0][a].start()
                to_chip[1][a].start()

        direct = {2: 0, 3: 1, 6: 2}
        relay = {4: 0, 5: 1, 7: 2}

        @pl.when((jj == 0) & (i == mid))
        def _():
            copy(0, 0, sibling, me).wait_recv()

        for row, j in direct.items():
            @pl.when((jj == row - 1) & (i == mid))
            def _(j=j):
                arrived(0, j)
                passed[j][0].start()
                if j == 1:
                    onward[0].start()

        for row, j in relay.items():
            @pl.when((jj == row - 1) & (i == mid))
            def _(j=j):
                copy(0, 4 + j, relayed[j], me).wait_recv()

        @pl.when((jj == NDEV - 1) & (i == 0))
        def _():
            for a in range(1, n):
                arrived(a, 1)
                onward[a].start()
                passed[1][a].start()
                arrived(a, 0)
                passed[0][a].start()

        @pl.when((jj < NDEV - 1) & (i == mid))
        def _():
            load(jj + 1).start()

        @pl.when(i == 0)
        def _():
            load(jj).wait()

        @pl.when(jj == 0)
        def _():
            xf = x_ref[...]
            r = lax.rsqrt(jnp.mean(xf * xf, axis=-1, keepdims=True) + EPS)
            h = (xf * r * nw_ref[...]) * (1.0 + sc_ref[...]) + sh_ref[...]
            h_all[i] = h.astype(BF16)
            ht_ref[...] = h.T.astype(BF16)

        o_ref[...] = jnp.dot(h_all[i], wbuf[jj % 2], preferred_element_type=F32).astype(BF16)

        @pl.when((jj == NDEV - 1) & (i == ni - 1))
        def _():
            for a in range(1, n):
                arrived(a, 2)
                passed[2][a].start()
            for a in range(1, n):
                copy(a, 0, sibling, me).wait_recv()
                for j in range(3):
                    copy(a, 4 + j, relayed[j], me).wait_recv()
            for a in range(n):
                mine[a].wait()
                for cp in sends(a):
                    cp.wait_send()

    any_spec = pl.BlockSpec(memory_space=pl.ANY)
    vec = pl.BlockSpec((1, D), lambda jj, i, o: (0, 0))
    outs = pl.pallas_call(
        body, name="proj_fwd_gather",
        grid_spec=pltpu.PrefetchScalarGridSpec(
            num_scalar_prefetch=1, grid=(NDEV, ni),
            in_specs=[pl.BlockSpec((tm, D), lambda jj, i, o: (jnp.where(jj == 0, i, ni - 1), 0))] + [vec] * 3
                     + [any_spec] * n,
            out_specs=[pl.BlockSpec((tm, SHARD), lambda jj, i, o: (i, o[jj])),
                       pl.BlockSpec((D, tm), lambda jj, i, o: (0, jnp.where(jj == 0, i, ni - 1)))]
                      + [any_spec] * n,
            scratch_shapes=[pltpu.VMEM((ni, tm, D), BF16), pltpu.VMEM((2, D, SHARD), BF16),
                            pltpu.SemaphoreType.DMA((n, 7)), pltpu.SemaphoreType.DMA((n, 7)),
                            pltpu.SemaphoreType.DMA((n,)), pltpu.SemaphoreType.DMA((2,))]),
        out_shape=[jax.ShapeDtypeStruct((s, NIN), BF16), jax.ShapeDtypeStruct((D, s), BF16),
                   jax.ShapeDtypeStruct((NDEV, D, SHARD), BF16)]
                  + [jax.ShapeDtypeStruct((NDEV,) + e.shape, e.dtype) for e in extras],
        compiler_params=_cp(("arbitrary", "arbitrary"), 56))(order, x, nw, scale, shift, w_shard, *extras)
    return outs[0], outs[1], outs[2], outs[3:]


def proj_bwd(ht, dproj, wg, smalls, order, x, dy, nw, scale, tt):
    s = dproj.shape[0]
    nk = s // tt
    n = len(smalls)
    rows_per_step = tt // nk
    last = 2 * NDEV

    def body(order_ref, ht_ref, dp_ref, w_ref, x_ref, dy_ref, nw_ref, sc_ref, *rest):
        small_in = rest[:n]
        gx_ref, st_ref, gw_ref, rwin_ref = rest[n:n + 4]
        small_out = rest[n + 4:2 * n + 4]
        acc, stage, dh, send_sems, recv_sems, local_sems, stage_sems = rest[2 * n + 4:]
        t, k = pl.program_id(0), pl.program_id(1)
        me_xyc = _coords()
        me = _dev_index(me_xyc)
        peers = [_flip(me_xyc, f) for f in FLIPS]

        def exchange(a, kf, src_arr, dst_arr):
            pid = _dev_index(peers[kf])
            mk = lambda dst: pltpu.make_async_remote_copy(
                src_ref=src_arr.at[pid], dst_ref=dst, send_sem=send_sems.at[a, kf], recv_sem=recv_sems.at[a, kf],
                device_id=peers[kf], device_id_type=MESH)
            return mk(dst_arr.at[me]), mk(dst_arr.at[pid])

        small_pairs = [exchange(1 + a, kf, small_in[a], small_out[a]) for kf in range(7) for a in range(n)]
        small_own = [pltpu.make_async_copy(small_in[a].at[me], small_out[a].at[me], local_sems.at[1 + a])
                     for a in range(n)]
        win_pairs = [exchange(0, kf, gw_ref, rwin_ref) for kf in range(7)]
        win_own = pltpu.make_async_copy(gw_ref.at[me], rwin_ref.at[me], local_sems.at[0])

        def to_hbm(jj):
            slab = me if jj == 7 else _dev_index(peers[jj])
            return pltpu.make_async_copy(stage.at[jj % 2], gw_ref.at[slab], stage_sems.at[jj % 2])

        @pl.when((t == 0) & (k == 0))
        def _():
            for cp in small_own:
                cp.start()
            for send, _ in small_pairs:
                send.start()

        @pl.when(t < NDEV)
        def _():
            p = jnp.dot(ht_ref[...], dp_ref[...], preferred_element_type=F32)

            @pl.when(k == 0)
            def _():
                acc[...] = p

            @pl.when(k > 0)
            def _():
                acc[...] += p

        for jj in range(NDEV):
            @pl.when((t == jj) & (k == nk - 1))
            def _(jj=jj):
                stage[jj % 2] = acc[...].astype(BF16)
                to_hbm(jj).start()

            @pl.when((t == jj + 1) & (k == 1))
            def _(jj=jj):
                to_hbm(jj).wait()
                if jj < 7:
                    win_pairs[jj][0].start()
                else:
                    win_own.start()

        def matmul_step():
            p = lax.dot_general(dp_ref[...], w_ref[...], NT, preferred_element_type=F32)
            slot = t % 2
            dh[slot] = jnp.where(k == 0, p, dh[slot] + p)

        def norm_step():
            g = dh.at[(t + 1) % 2][pl.ds(pl.multiple_of(k * rows_per_step, rows_per_step), rows_per_step), :]
            xf = x_ref[...]
            r = lax.rsqrt(jnp.mean(xf * xf, axis=-1, keepdims=True) + EPS)
            xh = xf * r
            dn = g * (1.0 + sc_ref[...])
            dxh = dn * nw_ref[...]
            gx_ref[...] = dy_ref[...] + r * (dxh - xh * jnp.mean(dxh * xh, axis=-1, keepdims=True))
            st_ref[0:1, :] += jnp.sum(g, axis=0, keepdims=True)
            st_ref[1:2, :] += jnp.sum(g * xh * nw_ref[...], axis=0, keepdims=True)
            st_ref[2:3, :] += jnp.sum(dn * xh, axis=0, keepdims=True)

        @pl.when((t == 0) & (k == 0))
        def _():
            st_ref[...] = jnp.zeros_like(st_ref)

        @pl.when(t == NDEV)
        def _():
            matmul_step()

        @pl.when((t > NDEV) & (t < last))
        def _():
            matmul_step()
            norm_step()

        @pl.when(t == last)
        def _():
            norm_step()

        @pl.when((t == last) & (k == nk - 1))
        def _():
            for _, recv in win_pairs + small_pairs:
                recv.wait_recv()
            for send, _ in win_pairs + small_pairs:
                send.wait_send()
            win_own.wait()
            for cp in small_own:
                cp.wait()

    any_spec = pl.BlockSpec(memory_space=pl.ANY)
    first = lambda t: t < NDEV
    slab = lambda t, k: jnp.where(t == last, NDEV - 1, k)
    chunk = pl.BlockSpec((rows_per_step, D), lambda t, k, o: (jnp.maximum((t - NDEV - 1) * nk + k, 0), 0))
    vec = pl.BlockSpec((1, D), lambda t, k, o: (0, 0))
    outs = pl.pallas_call(
        body, name="proj_bwd",
        grid_spec=pltpu.PrefetchScalarGridSpec(
            num_scalar_prefetch=1, grid=(last + 1, nk),
            in_specs=[pl.BlockSpec((D, tt), lambda t, k, o: (0, jnp.where(first(t), k, nk - 1))),
                      pl.BlockSpec((tt, SHARD), lambda t, k, o: (jnp.where(first(t), k, jnp.minimum(t, last - 1) - NDEV),
                                                                 jnp.where(first(t), o[jnp.minimum(t, NDEV - 1)],
                                                                           slab(t, k)))),
                      pl.BlockSpec((None, D, SHARD), lambda t, k, o: (jnp.where(first(t), 0, slab(t, k)), 0, 0)),
                      chunk, chunk, vec, vec]
                     + [any_spec] * n,
            out_specs=[chunk, pl.BlockSpec((8, D), lambda t, k, o: (0, 0))] + [any_spec] * (2 + n),
            scratch_shapes=[pltpu.VMEM((D, SHARD), F32), pltpu.VMEM((2, D, SHARD), BF16),
                            pltpu.VMEM((2, tt, D), F32),
                            pltpu.SemaphoreType.DMA((1 + n, 7)), pltpu.SemaphoreType.DMA((1 + n, 7)),
                            pltpu.SemaphoreType.DMA((1 + n,)), pltpu.SemaphoreType.DMA((2,))]),
        out_shape=[jax.ShapeDtypeStruct((s, D), F32), jax.ShapeDtypeStruct((8, D), F32),
                   jax.ShapeDtypeStruct((NDEV, D, SHARD), BF16), jax.ShapeDtypeStruct((NDEV, D, SHARD), BF16)]
                  + [jax.ShapeDtypeStruct(a.shape, a.dtype) for a in smalls],
        compiler_params=_cp(("arbitrary", "arbitrary"), 56))(order, ht, dproj, wg, x, dy, nw, scale, *smalls)
    return outs[0], outs[1], outs[3], outs[4:]


def matmul_tn(a, b, name, tk):
    s, m = a.shape
    n = b.shape[1]
    nk = s // tk

    def body(a_ref, b_ref, o_ref, acc_ref):
        k = pl.program_id(0)
        p = lax.dot_general(a_ref[...], b_ref[...], TN, preferred_element_type=F32)

        @pl.when(k == 0)
        def _():
            acc_ref[...] = p

        @pl.when(k > 0)
        def _():
            acc_ref[...] += p

        @pl.when(k == nk - 1)
        def _():
            o_ref[...] = acc_ref[...].astype(BF16)

    return pl.pallas_call(
        body, name=name, grid=(nk,),
        in_specs=[pl.BlockSpec((tk, m), lambda k: (k, 0)), pl.BlockSpec((tk, n), lambda k: (k, 0))],
        out_specs=pl.BlockSpec((m, n), lambda k: (0, 0)),
        out_shape=jax.ShapeDtypeStruct((m, n), BF16),
        scratch_shapes=[pltpu.VMEM((m, n), F32)],
        compiler_params=_cp(("arbitrary",)))(a, b)


def _head_matrices():
    lane = lax.broadcasted_iota(jnp.int32, (CB, CB), 0)
    col = lax.broadcasted_iota(jnp.int32, (CB, CB), 1)
    same = (lane // HD == col // HD).astype(BF16)
    lane_c = lax.broadcasted_iota(jnp.int32, (CB, LANES), 0)
    col_c = lax.broadcasted_iota(jnp.int32, (CB, LANES), 1)
    total = (lane_c // HD == col_c).astype(BF16)
    lane_e = lax.broadcasted_iota(jnp.int32, (LANES, CB), 0)
    col_e = lax.broadcasted_iota(jnp.int32, (LANES, CB), 1)
    expand = (lane_e == col_e // HD).astype(BF16)
    return same, total, expand


def _head_sum(x, m_ref):
    return jnp.dot(x.astype(BF16), m_ref[...], preferred_element_type=F32)


def _dot_hilo(x, m_ref):
    hi = x.astype(BF16)
    lo = (x - hi.astype(F32)).astype(BF16)
    return (jnp.dot(hi, m_ref[...], preferred_element_type=F32)
            + jnp.dot(lo, m_ref[...], preferred_element_type=F32))


def _to_residue_major(val, buf, out_ref, dil):
    rows = out_ref.shape[1]
    for k in range(val.shape[1] // LANES):
        lanes = slice(k * LANES, (k + 1) * LANES)
        buf[k] = val[:, lanes]
        for r in range(dil):
            out_ref[r, :, lanes] = buf.at[k][pl.ds(r, rows, stride=dil), :].astype(out_ref.dtype)


def _from_residue_major(ref, buf, dil):
    if dil == 1:
        return ref[0].astype(F32)
    rows, chunks = ref.shape[1], ref.shape[2] // LANES
    for k in range(chunks):
        for r in range(dil):
            buf.at[k][pl.ds(r, rows, stride=dil), :] = ref[r, :, k * LANES:(k + 1) * LANES].astype(F32)
    return jnp.concatenate([buf[k] for k in range(chunks)], axis=1)


def qkv_prep(proj, qw8, kw8, same, tm):
    s = proj.shape[0]
    items = []
    for g, d in enumerate(DILATIONS):
        items += [(g, "q", CB_Q + g, d), (g, "k", CB_K + g, d)] + ([(g, "v", CB_V + g, d)] if d > 1 else [])
    n = len(items)

    def body(*refs):
        ins, (qw_ref, kw_ref, same_ref), outs, buf = refs[:n], refs[n:n + 3], refs[n + 3:2 * n + 3], refs[-1]
        for idx, (_, kind, _, dil) in enumerate(items):
            val = ins[idx][...].astype(F32)
            if kind != "v":
                r = lax.rsqrt(_head_sum(val * val, same_ref) * (1.0 / HD) + EPS)
                val = val * r * (qw_ref if kind == "q" else kw_ref)[...]
            if dil == 1:
                outs[idx][0] = val.astype(BF16)
            else:
                _to_residue_major(val, buf, outs[idx], dil)

    full = lambda a: pl.BlockSpec(a.shape, lambda i: (0, 0))
    outs = pl.pallas_call(
        body, name="qkv_prep", grid=(s // tm,),
        in_specs=[pl.BlockSpec((tm, CB), lambda i, cb=cb: (i, cb)) for _, _, cb, _ in items]
                 + [full(qw8), full(kw8), full(same)],
        out_specs=[pl.BlockSpec((d, tm // d, CB), lambda i: (0, i, 0)) for _, _, _, d in items],
        out_shape=[jax.ShapeDtypeStruct((d, s // d, CB), BF16) for _, _, _, d in items],
        scratch_shapes=[pltpu.VMEM((CB // LANES, tm, LANES), F32)],
        compiler_params=_cp(("parallel",)))(*([proj] * n), qw8 * (HD ** -0.5), kw8, same)
    srcs = [[None, None, (proj, CB_V + g)] for g in range(len(DILATIONS))]
    for (g, kind, _, _), o in zip(items, outs):
        srcs[g]["qkv".index(kind)] = (o.reshape(s, CB), 0)
    return srcs


def stats_prep(da, lc, dc, g, dil, tm):
    s = da.shape[0]
    rows = tm // dil

    def body(da_ref, lc_ref, dc_ref, dap_ref, lcp_ref, dcp_ref, lt_ref, dt_ref, buf):
        if dil == 1:
            dap_ref[0] = da_ref[...]
        else:
            _to_residue_major(da_ref[...].astype(F32), buf, dap_ref, dil)
        for src, dst, dst_t in ((lc_ref, lcp_ref, lt_ref), (dc_ref, dcp_ref, dt_ref)):
            buf[0] = src[...]
            for r in range(dil):
                piece = buf.at[0][pl.ds(r, rows, stride=dil), :] if dil > 1 else buf[0]
                dst[r] = piece
                dst_t[r] = piece.T[0:NH, :]

    row = lambda w: pl.BlockSpec((tm, w), lambda i: (i, 0))
    rm = lambda w: pl.BlockSpec((dil, rows, w), lambda i: (0, i, 0))
    tr = pl.BlockSpec((dil, NH, rows), lambda i: (0, 0, i))
    length = s // dil
    dap, lcp, dcp, lt, dt = pl.pallas_call(
        body, name=f"stats_prep_g{g}", grid=(s // tm,),
        in_specs=[row(CB), row(LANES), row(LANES)],
        out_specs=[rm(CB), rm(LANES), rm(LANES), tr, tr],
        out_shape=[jax.ShapeDtypeStruct((dil, length, CB), BF16)]
                  + [jax.ShapeDtypeStruct((dil, length, LANES), F32)] * 2
                  + [jax.ShapeDtypeStruct((dil, NH, length), F32)] * 2,
        scratch_shapes=[pltpu.VMEM((CB // LANES, tm, LANES), F32)],
        compiler_params=_cp(("parallel",)))(da, lc, dc)
    return (dap.reshape(s, CB), lcp.reshape(s, LANES), dcp.reshape(s, LANES),
            lt.reshape(dil * NH, length), dt.reshape(dil * NH, length))


def qkv_grads_to_dproj(dproj, proj, grads, qw8, kw8, same, tm):
    s = dproj.shape[0]
    ni = s // tm
    flat = [(t.reshape(d, s // d, CB), d, kind, 3 * kind + g)
            for g, d in enumerate(DILATIONS) for kind, t in enumerate(grads[g])]
    nf = len(flat)
    nraw = 2 * len(DILATIONS)

    def body(*refs):
        dp_hbm, raws, ins = refs[nraw + nf + 4], refs[1:1 + nraw], refs[1 + nraw:1 + nraw + nf]
        qw_ref, kw_ref, same_ref = refs[1 + nraw + nf:4 + nraw + nf]
        gw_ref, stage, buf, sems = refs[5 + nraw + nf:]
        i = pl.program_id(0)
        slot = i % 2

        def slab(step, sl):
            return pltpu.make_async_copy(
                stage.at[sl], dp_hbm.at[pl.ds(pl.multiple_of(step * tm, tm), tm), pl.ds(CB_Q * CB, 9 * CB)],
                sems.at[sl])

        @pl.when(i == 0)
        def _():
            gw_ref[...] = jnp.zeros_like(gw_ref)

        @pl.when(i >= 2)
        def _():
            slab(i - 2, slot).wait()

        for ref, (_, d, kind, jj) in zip(ins, flat):
            cols = slice(jj * CB, (jj + 1) * CB)
            dn = _from_residue_major(ref, buf, d)
            if kind == 2:
                stage[slot, :, cols] = dn.astype(BF16)
                continue
            t = raws[jj][...].astype(F32)
            r = lax.rsqrt(_head_sum(t * t, same_ref) * (1.0 / HD) + EPS)
            xh = t * r
            gw_ref[kind:kind + 1, :] += jnp.sum(dn * xh, axis=0, keepdims=True)
            dxh = dn * (qw_ref if kind == 0 else kw_ref)[...]
            mean = _head_sum(dxh * xh, same_ref) * (1.0 / HD)
            stage[slot, :, cols] = (r * (dxh - xh * mean)).astype(BF16)
        slab(i, slot).start()

        @pl.when(i == ni - 1)
        def _():
            slab(i - 1, 1 - slot).wait()
            slab(i, slot).wait()

    full = lambda a: pl.BlockSpec(a.shape, lambda i: (0, 0))
    any_spec = pl.BlockSpec(memory_space=pl.ANY)
    return pl.pallas_call(
        body, name="qkv_grads_to_dproj", grid=(ni,),
        in_specs=[any_spec] + [pl.BlockSpec((tm, CB), lambda i, jb=jb: (i, CB_Q + jb)) for jb in range(nraw)]
                 + [pl.BlockSpec((d, tm // d, CB), lambda i: (0, i, 0)) for _, d, _, _ in flat]
                 + [full(qw8), full(kw8), full(same)],
        out_specs=[any_spec, pl.BlockSpec((8, CB), lambda i: (0, 0))],
        out_shape=[jax.ShapeDtypeStruct((s, NIN), BF16), jax.ShapeDtypeStruct((8, CB), F32)],
        input_output_aliases={0: 0},
        scratch_shapes=[pltpu.VMEM((2, tm, 9 * CB), BF16), pltpu.VMEM((CB // LANES, tm, LANES), F32),
                        pltpu.SemaphoreType.DMA((2,))],
        compiler_params=_cp(("arbitrary",)))(
            dproj, *([proj] * nraw), *[t for t, _, _, _ in flat], qw8, kw8, same)


def _lane_lo():
    return lax.broadcasted_iota(jnp.int32, (1, 2 * HD), 1) < HD


def _stack_heads(t, lo):
    zero = jnp.zeros_like(t)
    return jnp.concatenate([jnp.where(lo, t, zero), jnp.where(lo, zero, t)], axis=0)


def _masks(other_ok):
    qi = lax.broadcasted_iota(jnp.int32, (QB, QB), 0)
    kj = lax.broadcasted_iota(jnp.int32, (QB, QB), 1)
    return (kj >= qi) & other_ok, kj <= qi


SUB = 4


def _attn_specs(nb, dil):
    steps = nb // SUB
    main = lambda cb, w=CB: pl.BlockSpec((SUB * QB, w), lambda r, s: (r * steps + s, cb))
    prev = lambda cb: pl.BlockSpec((QB, CB), lambda r, s: (jnp.maximum(r * nb + SUB * s - 1, 0), cb))
    nxt = lambda cb: pl.BlockSpec((QB, CB), lambda r, s: (jnp.minimum(r * nb + SUB * (s + 1), dil * nb - 1), cb))
    return main, prev, nxt


def attn_fwd(q_src, k_src, v_src, g, dil):
    s = q_src[0].shape[0]
    nb = s // dil // QB
    main, prev, _ = _attn_specs(nb, dil)

    def body(q_ref, kp_ref, k_ref, vp_ref, v_ref, o_ref, l_ref, kbuf, vbuf):
        step = pl.program_id(1)
        kbuf[0:QB], kbuf[QB:] = kp_ref[...], k_ref[...]
        vbuf[0:QB], vbuf[QB:] = vp_ref[...], v_ref[...]
        lo = _lane_lo()
        head_lane = lax.broadcasted_iota(jnp.int32, (1, LANES), 1)

        def block(j, carry):
            r0 = pl.multiple_of(j * QB, QB)
            rows, krows = pl.ds(r0, QB), pl.ds(r0, 2 * QB)
            m_prev, m_cur = _masks(step * SUB + j > 0)
            mask = jnp.concatenate([m_prev, m_cur], axis=1)
            mask = jnp.concatenate([mask, mask], axis=0)
            lses = jnp.zeros((QB, LANES), F32)
            for i in range(NH // 2):
                sl = slice(2 * HD * i, 2 * HD * (i + 1))
                qs, ks, vv = q_ref[rows, sl], kbuf[krows, sl], vbuf[krows, sl]
                sc = lax.dot_general(_stack_heads(qs, lo), ks, NT, preferred_element_type=F32)
                sc = jnp.where(mask, sc, NEG)
                mx = jnp.max(sc, axis=-1, keepdims=True)
                p = jnp.exp(sc - mx)
                den = jnp.sum(p, axis=-1, keepdims=True)
                o = jnp.dot(p.astype(BF16), vv, preferred_element_type=F32) * (1.0 / den)
                lse = mx + jnp.log(den)
                o_ref[rows, sl] = jnp.where(lo, o[:QB], o[QB:]).astype(BF16)
                lses = jnp.where(head_lane == 2 * i, lse[:QB], jnp.where(head_lane == 2 * i + 1, lse[QB:], lses))
            l_ref[rows, :] = lses
            return carry

        lax.fori_loop(0, SUB, block, 0, unroll=True)

    return pl.pallas_call(
        body, name=f"attn_fwd_g{g}", grid=(dil, nb // SUB),
        in_specs=[main(q_src[1]), prev(k_src[1]), main(k_src[1]), prev(v_src[1]), main(v_src[1])],
        out_specs=[main(0), main(0, LANES)],
        out_shape=[jax.ShapeDtypeStruct((s, CB), BF16), jax.ShapeDtypeStruct((s, LANES), F32)],
        scratch_shapes=[pltpu.VMEM(((SUB + 1) * QB, CB), BF16)] * 2,
        compiler_params=_cp(("parallel", "parallel")))(q_src[0], k_src[0], k_src[0], v_src[0], v_src[0])


def attn_bwd_q(q_src, k_src, v_src, da, lc, dc, g, dil):
    s = q_src[0].shape[0]
    nb = s // dil // QB
    main, prev, _ = _attn_specs(nb, dil)

    def body(q_ref, kp_ref, k_ref, vp_ref, v_ref, da_ref, l_ref, d_ref, dq_ref, kbuf, vbuf):
        step = pl.program_id(1)
        kbuf[0:QB], kbuf[QB:] = kp_ref[...], k_ref[...]
        vbuf[0:QB], vbuf[QB:] = vp_ref[...], v_ref[...]
        lo = _lane_lo()

        def block(j, carry):
            r0 = pl.multiple_of(j * QB, QB)
            rows, krows = pl.ds(r0, QB), pl.ds(r0, 2 * QB)
            m_prev, m_cur = _masks(step * SUB + j > 0)
            mask = jnp.concatenate([m_prev, m_cur], axis=1)
            mask = jnp.concatenate([mask, mask], axis=0)
            lcols, dcols = l_ref[rows, :], d_ref[rows, :]
            for i in range(NH // 2):
                sl = slice(2 * HD * i, 2 * HD * (i + 1))
                qs, ks, vv, da2 = q_ref[rows, sl], kbuf[krows, sl], vbuf[krows, sl], da_ref[rows, sl]
                pair = lambda t: jnp.concatenate([t[:, 2 * i:2 * i + 1], t[:, 2 * i + 1:2 * i + 2]], axis=0)
                sc = lax.dot_general(_stack_heads(qs, lo), ks, NT, preferred_element_type=F32)
                sc = jnp.where(mask, sc, NEG)
                p = jnp.exp(sc - pair(lcols))
                dp = lax.dot_general(_stack_heads(da2, lo), vv, NT, preferred_element_type=F32)
                ds = p * (dp - pair(dcols))
                dq = jnp.dot(ds.astype(BF16), ks, preferred_element_type=F32)
                dq_ref[rows, sl] = (jnp.where(lo, dq[:QB], dq[QB:]) * (HD ** -0.5)).astype(BF16)
            return carry

        lax.fori_loop(0, SUB, block, 0, unroll=True)

    return pl.pallas_call(
        body, name=f"attn_bwd_q_g{g}", grid=(dil, nb // SUB),
        in_specs=[main(q_src[1]), prev(k_src[1]), main(k_src[1]), prev(v_src[1]), main(v_src[1]),
                  main(0), main(0, LANES), main(0, LANES)],
        out_specs=main(0), out_shape=jax.ShapeDtypeStruct((s, CB), BF16),
        scratch_shapes=[pltpu.VMEM(((SUB + 1) * QB, CB), BF16)] * 2,
        compiler_params=_cp(("parallel", "parallel")))(
            q_src[0], k_src[0], k_src[0], v_src[0], v_src[0], da, lc, dc)


def attn_bwd_kv(q_src, k_src, v_src, da, lt, dt, g, dil):
    s = q_src[0].shape[0]
    nb = s // dil // QB
    main, _, nxt = _attn_specs(nb, dil)

    def body(k_ref, v_ref, q_ref, qn_ref, da_ref, dan_ref, l_ref, ln_ref, d_ref, dn_ref, dk_ref, dv_ref,
             qbuf, dabuf, lbuf, dbuf):
        step = pl.program_id(1)
        qbuf[0:SUB * QB], qbuf[SUB * QB:] = q_ref[...], qn_ref[...]
        dabuf[0:SUB * QB], dabuf[SUB * QB:] = da_ref[...], dan_ref[...]
        for c in range(SUB):
            lbuf[c], dbuf[c] = l_ref[:, c * QB:(c + 1) * QB], d_ref[:, c * QB:(c + 1) * QB]
        lbuf[SUB], dbuf[SUB] = ln_ref[...], dn_ref[...]
        lo = _lane_lo()
        kj = lax.broadcasted_iota(jnp.int32, (QB, QB), 0)
        qi = lax.broadcasted_iota(jnp.int32, (QB, QB), 1)

        def block(j, carry):
            r0 = pl.multiple_of(j * QB, QB)
            rows, qrows = pl.ds(r0, QB), pl.ds(r0, 2 * QB)
            mask = jnp.concatenate([kj <= qi, (kj >= qi) & (step * SUB + j < nb - 1)], axis=1)
            mask = jnp.concatenate([mask, mask], axis=1)
            lrow = jnp.concatenate([lbuf[j], lbuf[j + 1]], axis=1)
            drow = jnp.concatenate([dbuf[j], dbuf[j + 1]], axis=1)
            for i in range(NH // 2):
                sl = slice(2 * HD * i, 2 * HD * (i + 1))
                q2, da2 = _stack_heads(qbuf[qrows, sl], lo), _stack_heads(dabuf[qrows, sl], lo)
                ks, vv = k_ref[rows, sl], v_ref[rows, sl]
                pair = lambda t: jnp.concatenate([t[2 * i:2 * i + 1, :], t[2 * i + 1:2 * i + 2, :]], axis=1)
                sc = lax.dot_general(ks, q2, NT, preferred_element_type=F32)
                sc = jnp.where(mask, sc, NEG)
                p = jnp.exp(sc - pair(lrow))
                dp = lax.dot_general(vv, da2, NT, preferred_element_type=F32)
                ds = p * (dp - pair(drow))
                dv_ref[rows, sl] = jnp.dot(p.astype(BF16), da2, preferred_element_type=F32).astype(BF16)
                dk_ref[rows, sl] = jnp.dot(ds.astype(BF16), q2, preferred_element_type=F32).astype(BF16)
            return carry

        lax.fori_loop(0, SUB, block, 0, unroll=True)

    steps = nb // SUB
    t_main = pl.BlockSpec((NH, SUB * QB), lambda r, s: (r, s))
    t_nxt = pl.BlockSpec((NH, QB), lambda r, s: (r, jnp.minimum(SUB * (s + 1), nb - 1)))
    out = jax.ShapeDtypeStruct((s, CB), BF16)
    return pl.pallas_call(
        body, name=f"attn_bwd_kv_g{g}", grid=(dil, steps),
        in_specs=[main(k_src[1]), main(v_src[1]), main(q_src[1]), nxt(q_src[1]),
                  main(0), nxt(0), t_main, t_nxt, t_main, t_nxt],
        out_specs=[main(0), main(0)], out_shape=[out, out],
        scratch_shapes=[pltpu.VMEM(((SUB + 1) * QB, CB), BF16)] * 2 + [pltpu.VMEM((SUB + 1, NH, QB), F32)] * 2,
        compiler_params=_cp(("parallel", "parallel")))(
            k_src[0], v_src[0], q_src[0], q_src[0], da, da, lt, lt, dt, dt)


def _conv_taps(u, u_prev, first):
    tm = u.shape[0]
    row = lax.broadcasted_iota(jnp.int32, (tm, 1), 0)
    up = jnp.where(first, 0.0, u_prev)
    u1 = jnp.where(row == 0, up[HALO - 1:HALO, :], pltpu.roll(u, 1, 0))
    u2 = jnp.where(row == 0, up[HALO - 2:HALO - 1, :],
                   jnp.where(row == 1, up[HALO - 1:HALO, :], pltpu.roll(u, 2, 0)))
    return u1, u2


def mid_fwd(proj, o_g, lse_g, conv_w, expand, tm):
    s = proj.shape[0]
    hb = tm // HALO

    def body(ba_ref, ca_ref, xa_ref, za_ref, cah_ref, xah_ref, zb_ref,
             o0, o1, o2, l0, l1, l2, w_ref, exp_ref, ya_ref, yb_ref, at_ref, lc_ref, buf_o, buf_l):
        first = pl.program_id(0) == 0
        u = ca_ref[...].astype(F32) * xa_ref[...].astype(F32)
        u1, u2 = _conv_taps(u, cah_ref[...].astype(F32) * xah_ref[...].astype(F32), first)
        conv = w_ref[0:1, :] * u2 + w_ref[1:2, :] * u1 + w_ref[2:3, :] * u
        ya_ref[...] = (ba_ref[...].astype(F32) * conv * _silu(za_ref[...].astype(F32))).astype(BF16)
        ls = [_from_residue_major(l, buf_l.at[g], d) for g, (l, d) in enumerate(zip((l0, l1, l2), DILATIONS))]
        mx = jnp.maximum(jnp.maximum(ls[0], ls[1]), ls[2])
        es = [jnp.exp(l - mx) for l in ls]
        den = es[0] + es[1] + es[2]
        attn = jnp.zeros((tm, CB), F32)
        for e, o, d in zip(es, (o0, o1, o2), DILATIONS):
            attn = attn + _dot_hilo(e / den, exp_ref) * _from_residue_major(o, buf_o, d)
        at_ref[...] = attn
        lc_ref[...] = mx + jnp.log(den)
        yb_ref[...] = (attn * _silu(zb_ref[...].astype(F32))).astype(BF16)

    col = lambda j: pl.BlockSpec((tm, D), lambda i: (i, j))
    halo = lambda j: pl.BlockSpec((HALO, D), lambda i: (jnp.maximum(i * hb - 1, 0), j))
    loc = lambda w: pl.BlockSpec((tm, w), lambda i: (i, 0))
    rm = lambda w: [pl.BlockSpec((d, tm // d, w), lambda i: (0, i, 0)) for d in DILATIONS]
    rm_view = lambda ts, w: [t.reshape(d, s // d, w) for t, d in zip(ts, DILATIONS)]
    return pl.pallas_call(
        body, name="mid_fwd", grid=(s // tm,),
        in_specs=[col(0), col(1), col(2), col(3), halo(1), halo(2),
                  pl.BlockSpec((tm, CB), lambda i: (i, CB_ZB))] + rm(CB) + rm(LANES)
                 + [pl.BlockSpec((3, D), lambda i: (0, 0)), pl.BlockSpec(expand.shape, lambda i: (0, 0))],
        out_specs=[loc(D), loc(CB), loc(CB), loc(LANES)],
        out_shape=[jax.ShapeDtypeStruct((s, D), BF16), jax.ShapeDtypeStruct((s, CB), BF16),
                   jax.ShapeDtypeStruct((s, CB), F32), jax.ShapeDtypeStruct((s, LANES), F32)],
        scratch_shapes=[pltpu.VMEM((CB // LANES, tm, LANES), F32), pltpu.VMEM((3, 1, tm, LANES), F32)],
        compiler_params=_cp(("parallel",)))(
            proj, proj, proj, proj, proj, proj, proj, *rm_view(o_g, CB), *rm_view(lse_g, LANES), conv_w, expand)


def tail(proj, ya, yb, attn, x, target, gate, pa_w, pb_w, wo_w, total, conv_w, tm):
    s = proj.shape[0]
    ni = s // tm
    hb = tm // HALO
    nlate = NIN - CB_ZB * CB
    nearly = 4 * D

    def body(ya_ref, yb_ref, ga_ref, gb_ref, zb_ref, at_ref, x_ref, t_ref, gate_ref, pa_ref, pb_ref, wo_ref,
             tot_ref, ba_ref, ca_ref, xa_ref, za_ref, cah_ref, xah_ref, cw_ref,
             dp_hbm, dy_ref, da_ref, dc_ref, mg_ref, do_ref, dpa_ref, dpb_ref, st_ref, gwc_ref,
             stage, dconv_next, sems):
        step = pl.program_id(0)
        i = ni - 1 - step
        slot = step % 2

        def slabs(at_step, sl):
            rows = pl.ds(pl.multiple_of((ni - 1 - at_step) * tm, tm), tm)
            return (pltpu.make_async_copy(stage.at[sl, :, 0:nearly], dp_hbm.at[rows, pl.ds(0, nearly)],
                                          sems.at[sl, 0]),
                    pltpu.make_async_copy(stage.at[sl, :, nearly:], dp_hbm.at[rows, pl.ds(CB_ZB * CB, nlate)],
                                          sems.at[sl, 1]))

        @pl.when(step == 0)
        def _():
            st_ref[...] = jnp.zeros_like(st_ref)
            gwc_ref[...] = jnp.zeros_like(gwc_ref)
            dconv_next[...] = jnp.zeros_like(dconv_next)

        @pl.when(step >= 2)
        def _():
            for cp in slabs(step - 2, slot):
                cp.wait()

        gate_v = gate_ref[...]
        pa = jnp.dot(ya_ref[...], pa_ref[...], preferred_element_type=F32)
        pb = jnp.dot(yb_ref[...], pb_ref[...], preferred_element_type=F32)
        sa = jax.nn.sigmoid(ga_ref[...].astype(F32))
        sb = jax.nn.sigmoid(gb_ref[...].astype(F32))
        merged = (sa * pa + sb * pb).astype(BF16)
        mg_ref[...] = merged
        out = jnp.dot(merged, wo_ref[...], preferred_element_type=F32)
        err = x_ref[...] + gate_v * out - t_ref[...]
        dy = err * (1.0 / D)
        dy_ref[...] = dy
        st_ref[0:1, :] += jnp.sum(dy * out, axis=0, keepdims=True)
        st_ref[1:2, :] += jnp.sum(err * err, axis=0, keepdims=True)
        dout = (gate_v * dy).astype(BF16)
        do_ref[...] = dout
        dmg = lax.dot_general(dout, wo_ref[...], NT, preferred_element_type=F32)
        dpa = (dmg * sa).astype(BF16)
        dpb = (dmg * sb).astype(BF16)
        dpa_ref[...] = dpa
        dpb_ref[...] = dpb
        late = nearly
        stage[slot, :, late + CB:late + CB + D] = (dmg * pa * sa * (1.0 - sa)).astype(BF16)
        stage[slot, :, late + CB + D:] = (dmg * pb * sb * (1.0 - sb)).astype(BF16)
        dya = lax.dot_general(dpa, pa_ref[...], NT, preferred_element_type=F32)
        dyb = lax.dot_general(dpb, pb_ref[...], NT, preferred_element_type=F32)
        zb = zb_ref[...].astype(F32)
        sg = jax.nn.sigmoid(zb)
        attn_v = at_ref[...]
        dattn = dyb * (zb * sg)
        da_ref[...] = dattn.astype(BF16)
        stage[slot, :, late:late + CB] = (dyb * attn_v * (sg * (1.0 + zb * (1.0 - sg)))).astype(BF16)
        dc_ref[...] = _dot_hilo(dattn * attn_v, tot_ref)

        ba, ca, xa, za = (t[...].astype(F32) for t in (ba_ref, ca_ref, xa_ref, za_ref))
        u = ca * xa
        u1, u2 = _conv_taps(u, cah_ref[...].astype(F32) * xah_ref[...].astype(F32), i == 0)
        w0, w1, w2 = cw_ref[0:1, :], cw_ref[1:2, :], cw_ref[2:3, :]
        conv = w0 * u2 + w1 * u1 + w2 * u
        sga = jax.nn.sigmoid(za)
        sza = za * sga
        dconv = dya * ba * sza
        dcn = dconv_next[...]
        rowi = lax.broadcasted_iota(jnp.int32, (tm, 1), 0)
        d1 = jnp.where(rowi == tm - 1, dcn[0:1, :], pltpu.roll(dconv, tm - 1, 0))
        d2 = jnp.where(rowi == tm - 2, dcn[0:1, :],
                       jnp.where(rowi == tm - 1, dcn[1:2, :], pltpu.roll(dconv, tm - 2, 0)))
        du = w2 * dconv + w1 * d1 + w0 * d2
        stage[slot, :, 0:D] = (dya * conv * sza).astype(BF16)
        stage[slot, :, D:2 * D] = (du * xa).astype(BF16)
        stage[slot, :, 2 * D:3 * D] = (du * ca).astype(BF16)
        stage[slot, :, 3 * D:4 * D] = (dya * ba * conv * (sga * (1.0 + za * (1.0 - sga)))).astype(BF16)
        gwc_ref[0:1, :] += jnp.sum(dconv * u2, axis=0, keepdims=True)
        gwc_ref[1:2, :] += jnp.sum(dconv * u1, axis=0, keepdims=True)
        gwc_ref[2:3, :] += jnp.sum(dconv * u, axis=0, keepdims=True)
        dconv_next[...] = dconv[0:8, :]

        for cp in slabs(step, slot):
            cp.start()

        @pl.when(step == ni - 1)
        def _():
            for cp in slabs(step - 1, 1 - slot) + slabs(step, slot):
                cp.wait()

    rev = lambda st: ni - 1 - st
    row = lambda w: pl.BlockSpec((tm, w), lambda st: (rev(st), 0))
    pcol = lambda w, jb: pl.BlockSpec((tm, w), lambda st: (rev(st), jb))
    halo = lambda jb: pl.BlockSpec((HALO, D), lambda st: (jnp.maximum(rev(st) * hb - 1, 0), jb))
    const = lambda a: pl.BlockSpec(a.shape, lambda st: (0, 0), pipeline_mode=pl.Buffered(1))
    acc = pl.BlockSpec((8, D), lambda st: (0, 0))
    return pl.pallas_call(
        body, name="tail", grid=(ni,),
        in_specs=[row(D), row(CB), pcol(D, 9), pcol(D, 10), pcol(CB, CB_ZB), row(CB), row(D), row(D),
                  pl.BlockSpec((1, D), lambda st: (0, 0)), const(pa_w), const(pb_w), const(wo_w), const(total),
                  pcol(D, 0), pcol(D, 1), pcol(D, 2), pcol(D, 3), halo(1), halo(2),
                  pl.BlockSpec((3, D), lambda st: (0, 0))],
        out_specs=[pl.BlockSpec(memory_space=pl.ANY),
                   row(D), row(CB), row(LANES), row(D), row(D), row(D), row(D), acc, acc],
        out_shape=[jax.ShapeDtypeStruct((s, NIN), BF16), jax.ShapeDtypeStruct((s, D), F32),
                   jax.ShapeDtypeStruct((s, CB), BF16), jax.ShapeDtypeStruct((s, LANES), F32)]
                  + [jax.ShapeDtypeStruct((s, D), BF16)] * 4 + [jax.ShapeDtypeStruct((8, D), F32)] * 2,
        scratch_shapes=[pltpu.VMEM((2, tm, nearly + nlate), BF16), pltpu.VMEM((8, D), F32),
                        pltpu.SemaphoreType.DMA((2, 2))],
        compiler_params=_cp(("arbitrary",), 60))(
            ya, yb, proj, proj, proj, attn, x, target, gate, pa_w, pb_w, wo_w, total,
            proj, proj, proj, proj, proj, proj, conv_w)


def _local_step(x, target, shift, scale, gate, norm_w, conv_w, qw, kw, w_shard, small_shards, me_xyc):
    qw8, kw8 = jnp.tile(qw, (1, NH)), jnp.tile(kw, (1, NH))
    same, total, expand = _head_matrices()
    proj, ht, wg, (pa_g, pb_g, wo_g) = proj_fwd_gather(
        x, norm_w, scale, shift, w_shard, small_shards, gather_order(me_xyc), 1024)
    pa_w, wo_w = pa_g.reshape(D, D), wo_g.reshape(D, D)
    pb_w = pb_g.transpose(1, 0, 2).reshape(CB, D)
    srcs = qkv_prep(proj, qw8, kw8, same, 512)
    o_g, lse_g = zip(*[attn_fwd(*srcs[g], g, d) for g, d in enumerate(DILATIONS)])
    ya, yb, attn, lc = mid_fwd(proj, o_g, lse_g, conv_w, expand, 512)
    dproj, dy, da, dc, merged, dout, dpa, dpb, st_tail, st_conv = tail(
        proj, ya, yb, attn, x, target, gate, pa_w, pb_w, wo_w, total, conv_w, 256)
    g_wo = matmul_tn(merged, dout, "grad_w_out", 1024)
    g_pa = matmul_tn(ya, dpa, "grad_w_br_conv", 1024)
    g_pb = matmul_tn(yb, dpb, "grad_w_br_attn", 1024)
    grads = []
    for g, d in enumerate(DILATIONS):
        da_p, lc_p, dc_p, lt, dt = stats_prep(da, lc, dc, g, d, 2048)
        dq = attn_bwd_q(*srcs[g], da_p, lc_p, dc_p, g, d)
        dk, dv = attn_bwd_kv(*srcs[g], da_p, lt, dt, g, d)
        grads.append((dq, dk, dv))
    dproj, gw_qk = qkv_grads_to_dproj(dproj, proj, grads, qw8, kw8, same, 512)
    slabs = [g_pa.reshape(NDEV, 128, D), g_pb.reshape(CB, NDEV, 128).transpose(1, 0, 2), g_wo.reshape(NDEV, 128, D)]
    grad_x, st_norm, r_win, (r_pa, r_pb, r_wo) = proj_bwd(
        ht, dproj, wg, slabs, scatter_order(me_xyc), x, dy, norm_w, scale, 1024)
    dmod = jnp.concatenate([st_norm[0:1], st_norm[1:2], st_tail[0:1]], axis=1)
    loss_part = (0.5 / D) * jnp.sum(st_tail[1])
    gw_heads = gw_qk[0:2].reshape(2, NH, HD).sum(axis=1)
    small = dict(dmod=dmod, norm_w=st_norm[2:3], conv_w=st_conv[0:3],
                 q_norm_w=gw_heads[0:1], k_norm_w=gw_heads[1:2], loss=loss_part)
    return grad_x, small, (r_win, r_pa, r_pb, r_wo)


def kernel(x, c, w_ada, b_ada, norm_w, w_in, conv_w, q_norm_w, k_norm_w, w_br_conv, w_br_attn, w_out, loss_target, m_w_ada, m_b_ada, m_norm_w, m_w_in, m_conv_w, m_q_norm_w, m_k_norm_w, m_w_br_conv, m_w_br_attn, m_w_out, v_w_ada, v_b_ada, v_norm_w, v_w_in, v_conv_w, v_q_norm_w, v_k_norm_w, v_w_br_conv, v_w_br_attn, v_w_out):
    me_xyc = (lax.axis_index("x"), lax.axis_index("y"), lax.axis_index("c"))
    me = _dev_index(me_xyc)
    ncol = w_ada.shape[2]

    conv_pad = jnp.zeros((8, 128), F32).at[0:3].set(conv_w[0])
    c_all, conv_all = all_gather([c, conv_pad], "gather_cond")
    conv_full = conv_all[:, 0:3].transpose(1, 0, 2).reshape(3, D)
    c_all = c_all.reshape(NDEV, D)

    b_cols = lax.dynamic_slice(b_ada, (0, me * ncol), (1, ncol))
    mod_cols = ada_fwd(c_all, w_ada[0], b_cols)
    (mod_all,) = all_gather([mod_cols], "gather_mod")
    mod = lax.dynamic_index_in_dim(mod_all, me, axis=1, keepdims=False).reshape(1, 3 * D)
    shift, scale, gate = mod[:, 0:D], mod[:, D:2 * D], mod[:, 2 * D:3 * D]

    grad_x, small, (r_win, r_pa, r_pb, r_wo) = _local_step(
        x[0], loss_target[0], shift, scale, gate, norm_w, conv_full, q_norm_w, k_norm_w,
        w_in[0].astype(BF16), [w_br_conv[0].astype(BF16), w_br_attn[0].astype(BF16), w_out[0].astype(BF16)], me_xyc)

    packed = jnp.concatenate(
        [small["dmod"], small["norm_w"], small["conv_w"].reshape(1, 3 * D), small["q_norm_w"], small["k_norm_w"],
         jnp.full((1, 128), small["loss"], F32)], axis=1)
    (packed_all,) = all_gather([packed], "gather_small")
    tot = sum_parts(packed_all)
    loss = tot[0, 7 * D + 2 * HD]
    dmod_all = packed_all[:, 0, 0:3 * D]
    g_b_ada = tot[:, 0:3 * D]
    g_norm_w = tot[:, 3 * D:4 * D]
    g_conv = lax.dynamic_slice(tot[:, 4 * D:7 * D].reshape(3, D), (0, me * 128), (3, 128))
    g_qn = tot[:, 7 * D:7 * D + HD]
    g_kn = tot[:, 7 * D + HD:7 * D + 2 * HD]
    g_w_ada = ada_bwd(c_all.T, lax.dynamic_slice(dmod_all, (0, me * ncol), (NDEV, ncol)))

    def upd(parts, w, m, v, name, rows):
        shape = w.shape
        w2, m2, v2 = (t.reshape(shape[-2:]) for t in (w, m, v))
        return [t.reshape(shape) for t in adamw(parts, w2, m2, v2, name, rows)]

    res = {
        "w_ada": upd(g_w_ada[None], w_ada, m_w_ada, v_w_ada, "adamw_w_ada", 256),
        "b_ada": upd(g_b_ada[None], b_ada, m_b_ada, v_b_ada, "adamw_b_ada", 1),
        "norm_w": upd(g_norm_w[None], norm_w, m_norm_w, v_norm_w, "adamw_norm_w", 1),
        "w_in": upd(r_win, w_in, m_w_in, v_w_in, "adamw_w_in", 128),
        "conv_w": upd(g_conv[None], conv_w, m_conv_w, v_conv_w, "adamw_conv_w", 3),
        "q_norm_w": upd(g_qn[None], q_norm_w, m_q_norm_w, v_q_norm_w, "adamw_q_norm_w", 1),
        "k_norm_w": upd(g_kn[None], k_norm_w, m_k_norm_w, v_k_norm_w, "adamw_k_norm_w", 1),
        "w_br_conv": upd(r_pa, w_br_conv, m_w_br_conv, v_w_br_conv, "adamw_w_br_conv", 128),
        "w_br_attn": upd(r_pb, w_br_attn, m_w_br_attn, v_w_br_attn, "adamw_w_br_attn", 512),
        "w_out": upd(r_wo, w_out, m_w_out, v_w_out, "adamw_w_out", 128),
    }
    names = ["w_ada", "b_ada", "norm_w", "w_in", "conv_w", "q_norm_w", "k_norm_w", "w_br_conv", "w_br_attn", "w_out"]
    return (loss, grad_x[None], *[res[n][0] for n in names], *[res[n][1] for n in names],
            *[res[n][2] for n in names], *[res[n][3] for n in names])
```

```python
import jax
import jax.numpy as jnp
from jax import lax
from jax.experimental import pallas as pl
from jax.experimental.pallas import tpu as pltpu

F32, BF16 = jnp.float32, jnp.bfloat16
D = 1024
NIN = 11264
NDEV = 8
SHARD = NIN // NDEV
HD = 64
NH = 8
QB = 128
CB = 512
CB_Q, CB_K, CB_V, CB_ZB = 8, 11, 14, 17
DILATIONS = (1, 4, 16)
EPS = 1e-6
NEG = -1e30
HALO = 16
LANES = 128
MESH = pl.DeviceIdType.MESH

ADAM_LR, ADAM_B1, ADAM_B2, ADAM_EPS, ADAM_WD, ADAM_STEP = 0.001, 0.9, 0.999, 1e-08, 0.01, 10

NT = (((1,), (1,)), ((), ()))
TN = (((0,), (0,)), ((), ()))


def _cp(sem, vmem_mb=48):
    return pltpu.CompilerParams(dimension_semantics=sem, vmem_limit_bytes=vmem_mb << 20)


def _silu(z):
    return z * jax.nn.sigmoid(z)


def _coords():
    return lax.axis_index("x"), lax.axis_index("y"), lax.axis_index("c")


FLIPS = [(fx, fy, fc) for fx in (0, 1) for fy in (0, 1) for fc in (0, 1)][1:]


def all_gather(arrs, name):
    n = len(arrs)

    def body(*refs):
        ins, outs = refs[:n], refs[n:2 * n]
        send_sems, recv_sems, local_sems = refs[2 * n:]
        me_xyc = _coords()
        me = _dev_index(me_xyc)
        peers = [_flip(me_xyc, f) for f in FLIPS]

        def copy(a, k, block):
            return pltpu.make_async_remote_copy(
                src_ref=ins[a], dst_ref=outs[a].at[block], send_sem=send_sems.at[a, k], recv_sem=recv_sems.at[a, k],
                device_id=peers[k], device_id_type=MESH)

        mine = [pltpu.make_async_copy(ins[a], outs[a].at[me], local_sems.at[a]) for a in range(n)]
        sends = [copy(a, k, me) for k in range(7) for a in range(n)]
        for cp in mine + sends:
            cp.start()
        for k in range(7):
            for a in range(n):
                copy(a, k, _dev_index(peers[k])).wait_recv()
        for cp in sends:
            cp.wait_send()
        for cp in mine:
            cp.wait()

    any_spec = pl.BlockSpec(memory_space=pl.ANY)
    return pl.pallas_call(
        body, name=name,
        out_shape=[jax.ShapeDtypeStruct((NDEV,) + a.shape, a.dtype) for a in arrs],
        in_specs=[any_spec] * n, out_specs=[any_spec] * n,
        scratch_shapes=[pltpu.SemaphoreType.DMA((n, 7)), pltpu.SemaphoreType.DMA((n, 7)),
                        pltpu.SemaphoreType.DMA((n,))],
    )(*arrs)


def _flip(dev, f):
    return tuple(1 - v if b else v for v, b in zip(dev, f))


def _dev_index(dev):
    return 4 * dev[0] + 2 * dev[1] + dev[2]


def _chip_order(x, y, c):
    xor = lambda a, b: a + b - 2 * a * b
    return [(xor(x, 1 - c), xor(y, c)), (xor(x, c), xor(y, 1 - c)), (1 - x, 1 - y)]


def gather_order(me_xyc):
    x, y, c = me_xyc
    chips = _chip_order(x, y, c)
    devs = [(x, y, c), (x, y, 1 - c), (*chips[0], c), (*chips[1], c),
            (*chips[1], 1 - c), (*chips[0], 1 - c), (*chips[2], c), (*chips[2], 1 - c)]
    return jnp.stack([_dev_index(d) for d in devs]).astype(jnp.int32)


def scatter_order(me_xyc):
    devs = [_flip(me_xyc, f) for f in FLIPS] + [me_xyc]
    return jnp.stack([_dev_index(d) for d in devs]).astype(jnp.int32)


def ada_fwd(c_all, w_ada, b_cols):
    def body(c_ref, w_ref, b_ref, o_ref):
        a = _silu(c_ref[...]).astype(BF16)
        o_ref[...] = jnp.dot(a, w_ref[...].astype(BF16), preferred_element_type=F32) + b_ref[...]

    return pl.pallas_call(body, name="ada_fwd",
                          out_shape=jax.ShapeDtypeStruct((NDEV, w_ada.shape[1]), F32))(c_all, w_ada, b_cols)


def ada_bwd(c_all_t, dmod_cols):
    def body(c_ref, d_ref, o_ref):
        at = _silu(c_ref[...])
        acc = at[:, 0:1] * d_ref[0:1, :]
        for b in range(1, NDEV):
            acc = acc + at[:, b:b + 1] * d_ref[b:b + 1, :]
        o_ref[...] = acc

    return pl.pallas_call(body, name="ada_bwd",
                          out_shape=jax.ShapeDtypeStruct((D, dmod_cols.shape[1]), F32))(c_all_t, dmod_cols)


def sum_parts(parts):
    def body(p_ref, o_ref):
        acc = p_ref[0]
        for b in range(1, NDEV):
            acc = acc + p_ref[b]
        o_ref[...] = acc

    return pl.pallas_call(body, name="sum_parts",
                          out_shape=jax.ShapeDtypeStruct(parts.shape[1:], F32))(parts)


def adamw(parts, w, m, v, name, rows):
    n, r, ccols = parts.shape

    def body(p_ref, w_ref, m_ref, v_ref, g_ref, d_ref, nm_ref, nv_ref):
        g = p_ref[0].astype(F32)
        for b in range(1, n):
            g = g + p_ref[b].astype(F32)
        nm = ADAM_B1 * m_ref[...] + (1.0 - ADAM_B1) * g
        nv = ADAM_B2 * v_ref[...] + (1.0 - ADAM_B2) * (g * g)
        g_ref[...] = g
        nm_ref[...] = nm
        nv_ref[...] = nv
        m_hat = nm / (1.0 - ADAM_B1 ** ADAM_STEP)
        v_hat = nv / (1.0 - ADAM_B2 ** ADAM_STEP)
        d_ref[...] = -ADAM_LR * (m_hat / (jnp.sqrt(v_hat) + ADAM_EPS) + ADAM_WD * w_ref[...])

    blk = pl.BlockSpec((rows, ccols), lambda i: (i, 0))
    out = jax.ShapeDtypeStruct((r, ccols), F32)
    return pl.pallas_call(
        body, name=name, grid=(r // rows,),
        in_specs=[pl.BlockSpec((n, rows, ccols), lambda i: (0, i, 0)), blk, blk, blk],
        out_specs=[blk] * 4, out_shape=[out] * 4, compiler_params=_cp(("parallel",)))(parts, w, m, v)


def proj_fwd_gather(x, nw, scale, shift, w_shard, extras, order, tm):
    s = x.shape[0]
    ni = s // tm
    n = 1 + len(extras)
    mid = ni - 2

    def body(order_ref, x_ref, nw_ref, sc_ref, sh_ref, *refs):
        ins, o_ref, ht_ref, outs = refs[:n], refs[n], refs[n + 1], refs[n + 2:2 * n + 2]
        h_all, wbuf, send_sems, recv_sems, local_sems, load_sems = refs[2 * n + 2:]
        jj, i = pl.program_id(0), pl.program_id(1)
        x, y, c = _coords()
        me, sibling = (x, y, c), (x, y, 1 - c)
        chips = _chip_order(x, y, c)
        relayed = [(*chips[1], 1 - c), (*chips[0], 1 - c), (*chips[2], 1 - c)]

        def slot(a, dev):
            return outs[a].at[_dev_index(dev)]

        def copy(a, k, block, to, src=None):
            return pltpu.make_async_remote_copy(
                src_ref=slot(a, block) if src is None else src, dst_ref=slot(a, block),
                send_sem=send_sems.at[a, k], recv_sem=recv_sems.at[a, k], device_id=to, device_id_type=MESH)

        mine = [pltpu.make_async_copy(ins[a], slot(a, me), local_sems.at[a]) for a in range(n)]
        to_sibling = [copy(a, 0, me, sibling, src=ins[a]) for a in range(n)]
        to_chip = [[copy(a, 1 + j, me, (*chips[j], c), src=ins[a]) for a in range(n)] for j in range(2)]
        onward = [copy(a, 3, (*chips[1], c), (*chips[0], c)) for a in range(n)]
        passed = [[copy(a, 4 + j, (*ch, c), sibling) for a in range(n)] for j, ch in enumerate(chips)]
        sends = lambda a: [to_sibling[a], to_chip[0][a], to_chip[1][a], onward[a]] + [passed[j][a] for j in range(3)]

        def arrived(a, j):
            copy(a, 1 + j, (*chips[j], c), me).wait_recv()

        def load(row):
            return pltpu.make_async_copy(outs[0].at[order_ref[row]], wbuf.at[row % 2], load_sems.at[row % 2])

        @pl.when((jj == 0) & (i == 0))
        def _():
            for cp in mine:
                cp.start()
            to_sibling[0].start()
            to_chip[0][0].start()
            pltpu.make_async_copy(ins[0], wbuf.at[0], load_sems.at[0]).start()

        @pl.when((jj == 1) & (i == 0))
        def _():
            to_chip[1][0].start()

        @pl.when((jj == 4) & (i == 0))
        def _():
            for a in range(1, n):
                to_sibling[a].start()
                to_chip[0][a].start()
                to_chip[1][a].start()

        direct = {2: 0, 3: 1, 6: 2}
        relay = {4: 0, 5: 1, 7: 2}

        @pl.when((jj == 0) & (i == mid))
        def _():
            copy(0, 0, sibling, me).wait_recv()

        for row, j in direct.items():
            @pl.when((jj == row - 1) & (i == mid))
            def _(j=j):
                arrived(0, j)
                passed[j][0].start()
                if j == 1:
                    onward[0].start()

        for row, j in relay.items():
            @pl.when((jj == row - 1) & (i == mid))
            def _(j=j):
                copy(0, 4 + j, relayed[j], me).wait_recv()

        @pl.when((jj == NDEV - 1) & (i == 0))
        def _():
            for a in range(1, n):
                arrived(a, 1)
                onward[a].start()
                passed[1][a].start()
                arrived(a, 0)
                passed[0][a].start()

        @pl.when((jj < NDEV - 1) & (i == mid))
        def _():
            load(jj + 1).start()

        @pl.when(i == 0)
        def _():
            load(jj).wait()

        @pl.when(jj == 0)
        def _():
            xf = x_ref[...]
            r = lax.rsqrt(jnp.mean(xf * xf, axis=-1, keepdims=True) + EPS)
            h = (xf * r * nw_ref[...]) * (1.0 + sc_ref[...]) + sh_ref[...]
            h_all[i] = h.astype(BF16)
            ht_ref[...] = h.T.astype(BF16)

        o_ref[...] = jnp.dot(h_all[i], wbuf[jj % 2], preferred_element_type=F32).astype(BF16)

        @pl.when((jj == NDEV - 1) & (i == ni - 1))
        def _():
            for a in range(1, n):
                arrived(a, 2)
                passed[2][a].start()
            for a in range(1, n):
                copy(a, 0, sibling, me).wait_recv()
                for j in range(3):
                    copy(a, 4 + j, relayed[j], me).wait_recv()
            for a in range(n):
                mine[a].wait()
                for cp in sends(a):
                    cp.wait_send()

    any_spec = pl.BlockSpec(memory_space=pl.ANY)
    vec = pl.BlockSpec((1, D), lambda jj, i, o: (0, 0))
    outs = pl.pallas_call(
        body, name="proj_fwd_gather",
        grid_spec=pltpu.PrefetchScalarGridSpec(
            num_scalar_prefetch=1, grid=(NDEV, ni),
            in_specs=[pl.BlockSpec((tm, D), lambda jj, i, o: (jnp.where(jj == 0, i, ni - 1), 0))] + [vec] * 3
                     + [any_spec] * n,
            out_specs=[pl.BlockSpec((tm, SHARD), lambda jj, i, o: (i, o[jj])),
                       pl.BlockSpec((D, tm), lambda jj, i, o: (0, jnp.where(jj == 0, i, ni - 1)))]
                      + [any_spec] * n,
            scratch_shapes=[pltpu.VMEM((ni, tm, D), BF16), pltpu.VMEM((2, D, SHARD), BF16),
                            pltpu.SemaphoreType.DMA((n, 7)), pltpu.SemaphoreType.DMA((n, 7)),
                            pltpu.SemaphoreType.DMA((n,)), pltpu.SemaphoreType.DMA((2,))]),
        out_shape=[jax.ShapeDtypeStruct((s, NIN), BF16), jax.ShapeDtypeStruct((D, s), BF16),
                   jax.ShapeDtypeStruct((NDEV, D, SHARD), BF16)]
                  + [jax.ShapeDtypeStruct((NDEV,) + e.shape, e.dtype) for e in extras],
        compiler_params=_cp(("arbitrary", "arbitrary"), 56))(order, x, nw, scale, shift, w_shard, *extras)
    return outs[0], outs[1], outs[2], outs[3:]


def proj_bwd(ht, dproj, wg, smalls, order, x, dy, nw, scale, tt):
    s = dproj.shape[0]
    nk = s // tt
    n = len(smalls)
    rows_per_step = tt // nk
    last = 2 * NDEV

    def body(order_ref, ht_ref, dp_ref, w_ref, x_ref, dy_ref, nw_ref, sc_ref, *rest):
        small_in = rest[:n]
        gx_ref, st_ref, gw_ref, rwin_ref = rest[n:n + 4]
        small_out = rest[n + 4:2 * n + 4]
        acc, stage, dh, send_sems, recv_sems, local_sems, stage_sems = rest[2 * n + 4:]
        t, k = pl.program_id(0), pl.program_id(1)
        me_xyc = _coords()
        me = _dev_index(me_xyc)
        peers = [_flip(me_xyc, f) for f in FLIPS]

        def exchange(a, kf, src_arr, dst_arr):
            pid = _dev_index(peers[kf])
            mk = lambda dst: pltpu.make_async_remote_copy(
                src_ref=src_arr.at[pid], dst_ref=dst, send_sem=send_sems.at[a, kf], recv_sem=recv_sems.at[a, kf],
                device_id=peers[kf], device_id_type=MESH)
            return mk(dst_arr.at[me]), mk(dst_arr.at[pid])

        small_pairs = [exchange(1 + a, kf, small_in[a], small_out[a]) for kf in range(7) for a in range(n)]
        small_own = [pltpu.make_async_copy(small_in[a].at[me], small_out[a].at[me], local_sems.at[1 + a])
                     for a in range(n)]
        win_pairs = [exchange(0, kf, gw_ref, rwin_ref) for kf in range(7)]
        win_own = pltpu.make_async_copy(gw_ref.at[me], rwin_ref.at[me], local_sems.at[0])

        def to_hbm(jj):
            slab = me if jj == 7 else _dev_index(peers[jj])
            return pltpu.make_async_copy(stage.at[jj % 2], gw_ref.at[slab], stage_sems.at[jj % 2])

        @pl.when((t == 0) & (k == 0))
        def _():
            for cp in small_own:
                cp.start()
            for send, _ in small_pairs:
                send.start()

        @pl.when(t < NDEV)
        def _():
            p = jnp.dot(ht_ref[...], dp_ref[...], preferred_element_type=F32)

            @pl.when(k == 0)
            def _():
                acc[...] = p

            @pl.when(k > 0)
            def _():
                acc[...] += p

        for jj in range(NDEV):
            @pl.when((t == jj) & (k == nk - 1))
            def _(jj=jj):
                stage[jj % 2] = acc[...].astype(BF16)
                to_hbm(jj).start()

            @pl.when((t == jj + 1) & (k == 1))
            def _(jj=jj):
                to_hbm(jj).wait()
                if jj < 7:
                    win_pairs[jj][0].start()
                else:
                    win_own.start()

        def matmul_step():
            p = lax.dot_general(dp_ref[...], w_ref[...], NT, preferred_element_type=F32)
            slot = t % 2
            dh[slot] = jnp.where(k == 0, p, dh[slot] + p)

        def norm_step():
            g = dh.at[(t + 1) % 2][pl.ds(pl.multiple_of(k * rows_per_step, rows_per_step), rows_per_step), :]
            xf = x_ref[...]
            r = lax.rsqrt(jnp.mean(xf * xf, axis=-1, keepdims=True) + EPS)
            xh = xf * r
            dn = g * (1.0 + sc_ref[...])
            dxh = dn * nw_ref[...]
            gx_ref[...] = dy_ref[...] + r * (dxh - xh * jnp.mean(dxh * xh, axis=-1, keepdims=True))
            st_ref[0:1, :] += jnp.sum(g, axis=0, keepdims=True)
            st_ref[1:2, :] += jnp.sum(g * xh * nw_ref[...], axis=0, keepdims=True)
            st_ref[2:3, :] += jnp.sum(dn * xh, axis=0, keepdims=True)

        @pl.when((t == 0) & (k == 0))
        def _():
            st_ref[...] = jnp.zeros_like(st_ref)

        @pl.when(t == NDEV)
        def _():
            matmul_step()

        @pl.when((t > NDEV) & (t < last))
        def _():
            matmul_step()
            norm_step()

        @pl.when(t == last)
        def _():
            norm_step()

        @pl.when((t == last) & (k == nk - 1))
        def _():
            for _, recv in win_pairs + small_pairs:
                recv.wait_recv()
            for send, _ in win_pairs + small_pairs:
                send.wait_send()
            win_own.wait()
            for cp in small_own:
                cp.wait()

    any_spec = pl.BlockSpec(memory_space=pl.ANY)
    first = lambda t: t < NDEV
    slab = lambda t, k: jnp.where(t == last, NDEV - 1, k)
    chunk = pl.BlockSpec((rows_per_step, D), lambda t, k, o: (jnp.maximum((t - NDEV - 1) * nk + k, 0), 0))
    vec = pl.BlockSpec((1, D), lambda t, k, o: (0, 0))
    outs = pl.pallas_call(
        body, name="proj_bwd",
        grid_spec=pltpu.PrefetchScalarGridSpec(
            num_scalar_prefetch=1, grid=(last + 1, nk),
            in_specs=[pl.BlockSpec((D, tt), lambda t, k, o: (0, jnp.where(first(t), k, nk - 1))),
                      pl.BlockSpec((tt, SHARD), lambda t, k, o: (jnp.where(first(t), k, jnp.minimum(t, last - 1) - NDEV),
                                                                 jnp.where(first(t), o[jnp.minimum(t, NDEV - 1)],
                                                                           slab(t, k)))),
                      pl.BlockSpec((None, D, SHARD), lambda t, k, o: (jnp.where(first(t), 0, slab(t, k)), 0, 0)),
                      chunk, chunk, vec, vec]
                     + [any_spec] * n,
            out_specs=[chunk, pl.BlockSpec((8, D), lambda t, k, o: (0, 0))] + [any_spec] * (2 + n),
            scratch_shapes=[pltpu.VMEM((D, SHARD), F32), pltpu.VMEM((2, D, SHARD), BF16),
                            pltpu.VMEM((2, tt, D), F32),
                            pltpu.SemaphoreType.DMA((1 + n, 7)), pltpu.SemaphoreType.DMA((1 + n, 7)),
                            pltpu.SemaphoreType.DMA((1 + n,)), pltpu.SemaphoreType.DMA((2,))]),
        out_shape=[jax.ShapeDtypeStruct((s, D), F32), jax.ShapeDtypeStruct((8, D), F32),
                   jax.ShapeDtypeStruct((NDEV, D, SHARD), BF16), jax.ShapeDtypeStruct((NDEV, D, SHARD), BF16)]
                  + [jax.ShapeDtypeStruct(a.shape, a.dtype) for a in smalls],
        compiler_params=_cp(("arbitrary", "arbitrary"), 56))(order, ht, dproj, wg, x, dy, nw, scale, *smalls)
    return outs[0], outs[1], outs[3], outs[4:]


def matmul_tn(a, b, name, tk):
    s, m = a.shape
    n = b.shape[1]
    nk = s // tk

    def body(a_ref, b_ref, o_ref, acc_ref):
        k = pl.program_id(0)
        p = lax.dot_general(a_ref[...], b_ref[...], TN, preferred_element_type=F32)

        @pl.when(k == 0)
        def _():
            acc_ref[...] = p

        @pl.when(k > 0)
        def _():
            acc_ref[...] += p

        @pl.when(k == nk - 1)
        def _():
            o_ref[...] = acc_ref[...].astype(BF16)

    return pl.pallas_call(
        body, name=name, grid=(nk,),
        in_specs=[pl.BlockSpec((tk, m), lambda k: (k, 0)), pl.BlockSpec((tk, n), lambda k: (k, 0))],
        out_specs=pl.BlockSpec((m, n), lambda k: (0, 0)),
        out_shape=jax.ShapeDtypeStruct((m, n), BF16),
        scratch_shapes=[pltpu.VMEM((m, n), F32)],
        compiler_params=_cp(("arbitrary",)))(a, b)


def _head_matrices():
    lane = lax.broadcasted_iota(jnp.int32, (CB, CB), 0)
    col = lax.broadcasted_iota(jnp.int32, (CB, CB), 1)
    same = (lane // HD == col // HD).astype(BF16)
    lane_c = lax.broadcasted_iota(jnp.int32, (CB, LANES), 0)
    col_c = lax.broadcasted_iota(jnp.int32, (CB, LANES), 1)
    total = (lane_c // HD == col_c).astype(BF16)
    lane_e = lax.broadcasted_iota(jnp.int32, (LANES, CB), 0)
    col_e = lax.broadcasted_iota(jnp.int32, (LANES, CB), 1)
    expand = (lane_e == col_e // HD).astype(BF16)
    return same, total, expand


def _head_sum(x, m_ref):
    return jnp.dot(x.astype(BF16), m_ref[...], preferred_element_type=F32)


def _dot_hilo(x, m_ref):
    hi = x.astype(BF16)
    lo = (x - hi.astype(F32)).astype(BF16)
    return (jnp.dot(hi, m_ref[...], preferred_element_type=F32)
            + jnp.dot(lo, m_ref[...], preferred_element_type=F32))


def _to_residue_major(val, buf, out_ref, dil):
    rows = out_ref.shape[1]
    for k in range(val.shape[1] // LANES):
        lanes = slice(k * LANES, (k + 1) * LANES)
        buf[k] = val[:, lanes]
        for r in range(dil):
            out_ref[r, :, lanes] = buf.at[k][pl.ds(r, rows, stride=dil), :].astype(out_ref.dtype)


def _from_residue_major(ref, buf, dil):
    if dil == 1:
        return ref[0].astype(F32)
    rows, chunks = ref.shape[1], ref.shape[2] // LANES
    for k in range(chunks):
        for r in range(dil):
            buf.at[k][pl.ds(r, rows, stride=dil), :] = ref[r, :, k * LANES:(k + 1) * LANES].astype(F32)
    return jnp.concatenate([buf[k] for k in range(chunks)], axis=1)


def qkv_prep(proj, qw8, kw8, same, tm):
    s = proj.shape[0]
    items = []
    for g, d in enumerate(DILATIONS):
        items += [(g, "q", CB_Q + g, d), (g, "k", CB_K + g, d)] + ([(g, "v", CB_V + g, d)] if d > 1 else [])
    n = len(items)

    def body(*refs):
        ins, (qw_ref, kw_ref, same_ref), outs, buf = refs[:n], refs[n:n + 3], refs[n + 3:2 * n + 3], refs[-1]
        for idx, (_, kind, _, dil) in enumerate(items):
            val = ins[idx][...].astype(F32)
            if kind != "v":
                r = lax.rsqrt(_head_sum(val * val, same_ref) * (1.0 / HD) + EPS)
                val = val * r * (qw_ref if kind == "q" else kw_ref)[...]
            if dil == 1:
                outs[idx][0] = val.astype(BF16)
            else:
                _to_residue_major(val, buf, outs[idx], dil)

    full = lambda a: pl.BlockSpec(a.shape, lambda i: (0, 0))
    outs = pl.pallas_call(
        body, name="qkv_prep", grid=(s // tm,),
        in_specs=[pl.BlockSpec((tm, CB), lambda i, cb=cb: (i, cb)) for _, _, cb, _ in items]
                 + [full(qw8), full(kw8), full(same)],
        out_specs=[pl.BlockSpec((d, tm // d, CB), lambda i: (0, i, 0)) for _, _, _, d in items],
        out_shape=[jax.ShapeDtypeStruct((d, s // d, CB), BF16) for _, _, _, d in items],
        scratch_shapes=[pltpu.VMEM((CB // LANES, tm, LANES), F32)],
        compiler_params=_cp(("parallel",)))(*([proj] * n), qw8 * (HD ** -0.5), kw8, same)
    srcs = [[None, None, (proj, CB_V + g)] for g in range(len(DILATIONS))]
    for (g, kind, _, _), o in zip(items, outs):
        srcs[g]["qkv".index(kind)] = (o.reshape(s, CB), 0)
    return srcs


def stats_prep(da, lc, dc, g, dil, tm):
    s = da.shape[0]
    rows = tm // dil

    def body(da_ref, lc_ref, dc_ref, dap_ref, lcp_ref, dcp_ref, lt_ref, dt_ref, buf):
        if dil == 1:
            dap_ref[0] = da_ref[...]
        else:
            _to_residue_major(da_ref[...].astype(F32), buf, dap_ref, dil)
        for src, dst, dst_t in ((lc_ref, lcp_ref, lt_ref), (dc_ref, dcp_ref, dt_ref)):
            buf[0] = src[...]
            for r in range(dil):
                piece = buf.at[0][pl.ds(r, rows, stride=dil), :] if dil > 1 else buf[0]
                dst[r] = piece
                dst_t[r] = piece.T[0:NH, :]

    row = lambda w: pl.BlockSpec((tm, w), lambda i: (i, 0))
    rm = lambda w: pl.BlockSpec((dil, rows, w), lambda i: (0, i, 0))
    tr = pl.BlockSpec((dil, NH, rows), lambda i: (0, 0, i))
    length = s // dil
    dap, lcp, dcp, lt, dt = pl.pallas_call(
        body, name=f"stats_prep_g{g}", grid=(s // tm,),
        in_specs=[row(CB), row(LANES), row(LANES)],
        out_specs=[rm(CB), rm(LANES), rm(LANES), tr, tr],
        out_shape=[jax.ShapeDtypeStruct((dil, length, CB), BF16)]
                  + [jax.ShapeDtypeStruct((dil, length, LANES), F32)] * 2
                  + [jax.ShapeDtypeStruct((dil, NH, length), F32)] * 2,
        scratch_shapes=[pltpu.VMEM((CB // LANES, tm, LANES), F32)],
        compiler_params=_cp(("parallel",)))(da, lc, dc)
    return (dap.reshape(s, CB), lcp.reshape(s, LANES), dcp.reshape(s, LANES),
            lt.reshape(dil * NH, length), dt.reshape(dil * NH, length))


def qkv_grads_to_dproj(dproj, proj, grads, qw8, kw8, same, tm):
    s = dproj.shape[0]
    ni = s // tm
    flat = [(t.reshape(d, s // d, CB), d, kind, 3 * kind + g)
            for g, d in enumerate(DILATIONS) for kind, t in enumerate(grads[g])]
    nf = len(flat)
    nraw = 2 * len(DILATIONS)

    def body(*refs):
        dp_hbm, raws, ins = refs[nraw + nf + 4], refs[1:1 + nraw], refs[1 + nraw:1 + nraw + nf]
        qw_ref, kw_ref, same_ref = refs[1 + nraw + nf:4 + nraw + nf]
        gw_ref, stage, buf, sems = refs[5 + nraw + nf:]
        i = pl.program_id(0)
        slot = i % 2

        def slab(step, sl):
            return pltpu.make_async_copy(
                stage.at[sl], dp_hbm.at[pl.ds(pl.multiple_of(step * tm, tm), tm), pl.ds(CB_Q * CB, 9 * CB)],
                sems.at[sl])

        @pl.when(i == 0)
        def _():
            gw_ref[...] = jnp.zeros_like(gw_ref)

        @pl.when(i >= 2)
        def _():
            slab(i - 2, slot).wait()

        for ref, (_, d, kind, jj) in zip(ins, flat):
            cols = slice(jj * CB, (jj + 1) * CB)
            dn = _from_residue_major(ref, buf, d)
            if kind == 2:
                stage[slot, :, cols] = dn.astype(BF16)
                continue
            t = raws[jj][...].astype(F32)
            r = lax.rsqrt(_head_sum(t * t, same_ref) * (1.0 / HD) + EPS)
            xh = t * r
            gw_ref[kind:kind + 1, :] += jnp.sum(dn * xh, axis=0, keepdims=True)
            dxh = dn * (qw_ref if kind == 0 else kw_ref)[...]
            mean = _head_sum(dxh * xh, same_ref) * (1.0 / HD)
            stage[slot, :, cols] = (r * (dxh - xh * mean)).astype(BF16)
        slab(i, slot).start()

        @pl.when(i == ni - 1)
        def _():
            slab(i - 1, 1 - slot).wait()
            slab(i, slot).wait()

    full = lambda a: pl.BlockSpec(a.shape, lambda i: (0, 0))
    any_spec = pl.BlockSpec(memory_space=pl.ANY)
    return pl.pallas_call(
        body, name="qkv_grads_to_dproj", grid=(ni,),
        in_specs=[any_spec] + [pl.BlockSpec((tm, CB), lambda i, jb=jb: (i, CB_Q + jb)) for jb in range(nraw)]
                 + [pl.BlockSpec((d, tm // d, CB), lambda i: (0, i, 0)) for _, d, _, _ in flat]
                 + [full(qw8), full(kw8), full(same)],
        out_specs=[any_spec, pl.BlockSpec((8, CB), lambda i: (0, 0))],
        out_shape=[jax.ShapeDtypeStruct((s, NIN), BF16), jax.ShapeDtypeStruct((8, CB), F32)],
        input_output_aliases={0: 0},
        scratch_shapes=[pltpu.VMEM((2, tm, 9 * CB), BF16), pltpu.VMEM((CB // LANES, tm, LANES), F32),
                        pltpu.SemaphoreType.DMA((2,))],
        compiler_params=_cp(("arbitrary",)))(
            dproj, *([proj] * nraw), *[t for t, _, _, _ in flat], qw8, kw8, same)


def _lane_lo():
    return lax.broadcasted_iota(jnp.int32, (1, 2 * HD), 1) < HD


def _stack_heads(t, lo):
    zero = jnp.zeros_like(t)
    return jnp.concatenate([jnp.where(lo, t, zero), jnp.where(lo, zero, t)], axis=0)


def _masks(other_ok):
    qi = lax.broadcasted_iota(jnp.int32, (QB, QB), 0)
    kj = lax.broadcasted_iota(jnp.int32, (QB, QB), 1)
    return (kj >= qi) & other_ok, kj <= qi


MAX_SUB = 8


def _attn_specs(nb, dil, sub):
    steps = nb // sub
    main = lambda cb, w=CB: pl.BlockSpec((sub * QB, w), lambda r, s: (r * steps + s, cb))
    prev = lambda cb: pl.BlockSpec((QB, CB), lambda r, s: (jnp.maximum(r * nb + sub * s - 1, 0), cb))
    nxt = lambda cb: pl.BlockSpec((QB, CB), lambda r, s: (jnp.minimum(r * nb + sub * (s + 1), dil * nb - 1), cb))
    return main, prev, nxt


def attn_fwd(q_src, k_src, v_src, g, dil):
    s = q_src[0].shape[0]
    nb = s // dil // QB
    sub = min(MAX_SUB, nb)
    main, prev, _ = _attn_specs(nb, dil, sub)

    def body(q_ref, kp_ref, k_ref, vp_ref, v_ref, o_ref, l_ref, kbuf, vbuf):
        step = pl.program_id(1)
        kbuf[0:QB], kbuf[QB:] = kp_ref[...], k_ref[...]
        vbuf[0:QB], vbuf[QB:] = vp_ref[...], v_ref[...]
        lo = _lane_lo()
        head_lane = lax.broadcasted_iota(jnp.int32, (1, LANES), 1)

        def block(j, carry):
            r0 = pl.multiple_of(j * QB, QB)
            rows, krows = pl.ds(r0, QB), pl.ds(r0, 2 * QB)
            m_prev, m_cur = _masks(step * sub + j > 0)
            mask = jnp.concatenate([m_prev, m_cur], axis=1)
            mask = jnp.concatenate([mask, mask], axis=0)
            lses = jnp.zeros((QB, LANES), F32)
            for i in range(NH // 2):
                sl = slice(2 * HD * i, 2 * HD * (i + 1))
                qs, ks, vv = q_ref[rows, sl], kbuf[krows, sl], vbuf[krows, sl]
                sc = lax.dot_general(_stack_heads(qs, lo), ks, NT, preferred_element_type=F32)
                sc = jnp.where(mask, sc, NEG)
                mx = jnp.max(sc, axis=-1, keepdims=True)
                p = jnp.exp(sc - mx)
                den = jnp.sum(p, axis=-1, keepdims=True)
                o = jnp.dot(p.astype(BF16), vv, preferred_element_type=F32) * (1.0 / den)
                lse = mx + jnp.log(den)
                o_ref[rows, sl] = jnp.where(lo, o[:QB], o[QB:]).astype(BF16)
                lses = jnp.where(head_lane == 2 * i, lse[:QB], jnp.where(head_lane == 2 * i + 1, lse[QB:], lses))
            l_ref[rows, :] = lses
            return carry

        lax.fori_loop(0, sub, block, 0, unroll=True)

    return pl.pallas_call(
        body, name=f"attn_fwd_g{g}", grid=(dil, nb // sub),
        in_specs=[main(q_src[1]), prev(k_src[1]), main(k_src[1]), prev(v_src[1]), main(v_src[1])],
        out_specs=[main(0), main(0, LANES)],
        out_shape=[jax.ShapeDtypeStruct((s, CB), BF16), jax.ShapeDtypeStruct((s, LANES), F32)],
        scratch_shapes=[pltpu.VMEM(((sub + 1) * QB, CB), BF16)] * 2,
        compiler_params=_cp(("parallel", "parallel")))(q_src[0], k_src[0], k_src[0], v_src[0], v_src[0])


def attn_bwd_q(q_src, k_src, v_src, da, lc, dc, g, dil):
    s = q_src[0].shape[0]
    nb = s // dil // QB
    sub = min(MAX_SUB, nb)
    main, prev, _ = _attn_specs(nb, dil, sub)

    def body(q_ref, kp_ref, k_ref, vp_ref, v_ref, da_ref, l_ref, d_ref, dq_ref, kbuf, vbuf):
        step = pl.program_id(1)
        kbuf[0:QB], kbuf[QB:] = kp_ref[...], k_ref[...]
        vbuf[0:QB], vbuf[QB:] = vp_ref[...], v_ref[...]
        lo = _lane_lo()

        def block(j, carry):
            r0 = pl.multiple_of(j * QB, QB)
            rows, krows = pl.ds(r0, QB), pl.ds(r0, 2 * QB)
            m_prev, m_cur = _masks(step * sub + j > 0)
            mask = jnp.concatenate([m_prev, m_cur], axis=1)
            mask = jnp.concatenate([mask, mask], axis=0)
            lcols, dcols = l_ref[rows, :], d_ref[rows, :]
            for i in range(NH // 2):
                sl = slice(2 * HD * i, 2 * HD * (i + 1))
                qs, ks, vv, da2 = q_ref[rows, sl], kbuf[krows, sl], vbuf[krows, sl], da_ref[rows, sl]
                pair = lambda t: jnp.concatenate([t[:, 2 * i:2 * i + 1], t[:, 2 * i + 1:2 * i + 2]], axis=0)
                sc = lax.dot_general(_stack_heads(qs, lo), ks, NT, preferred_element_type=F32)
                sc = jnp.where(mask, sc, NEG)
                p = jnp.exp(sc - pair(lcols))
                dp = lax.dot_general(_stack_heads(da2, lo), vv, NT, preferred_element_type=F32)
                ds = p * (dp - pair(dcols))
                dq = jnp.dot(ds.astype(BF16), ks, preferred_element_type=F32)
                dq_ref[rows, sl] = (jnp.where(lo, dq[:QB], dq[QB:]) * (HD ** -0.5)).astype(BF16)
            return carry

        lax.fori_loop(0, sub, block, 0, unroll=True)

    return pl.pallas_call(
        body, name=f"attn_bwd_q_g{g}", grid=(dil, nb // sub),
        in_specs=[main(q_src[1]), prev(k_src[1]), main(k_src[1]), prev(v_src[1]), main(v_src[1]),
                  main(0), main(0, LANES), main(0, LANES)],
        out_specs=main(0), out_shape=jax.ShapeDtypeStruct((s, CB), BF16),
        scratch_shapes=[pltpu.VMEM(((sub + 1) * QB, CB), BF16)] * 2,
        compiler_params=_cp(("parallel", "parallel")))(
            q_src[0], k_src[0], k_src[0], v_src[0], v_src[0], da, lc, dc)


def attn_bwd_kv(q_src, k_src, v_src, da, lt, dt, g, dil):
    s = q_src[0].shape[0]
    nb = s // dil // QB
    sub = min(MAX_SUB, nb)
    main, _, nxt = _attn_specs(nb, dil, sub)

    def body(k_ref, v_ref, q_ref, qn_ref, da_ref, dan_ref, l_ref, ln_ref, d_ref, dn_ref, dk_ref, dv_ref,
             qbuf, dabuf, lbuf, dbuf):
        step = pl.program_id(1)
        qbuf[0:sub * QB], qbuf[sub * QB:] = q_ref[...], qn_ref[...]
        dabuf[0:sub * QB], dabuf[sub * QB:] = da_ref[...], dan_ref[...]
        for c in range(sub):
            lbuf[c], dbuf[c] = l_ref[:, c * QB:(c + 1) * QB], d_ref[:, c * QB:(c + 1) * QB]
        lbuf[sub], dbuf[sub] = ln_ref[...], dn_ref[...]
        lo = _lane_lo()
        kj = lax.broadcasted_iota(jnp.int32, (QB, QB), 0)
        qi = lax.broadcasted_iota(jnp.int32, (QB, QB), 1)

        def block(j, carry):
            r0 = pl.multiple_of(j * QB, QB)
            rows, qrows = pl.ds(r0, QB), pl.ds(r0, 2 * QB)
            mask = jnp.concatenate([kj <= qi, (kj >= qi) & (step * sub + j < nb - 1)], axis=1)
            mask = jnp.concatenate([mask, mask], axis=1)
            lrow = jnp.concatenate([lbuf[j], lbuf[j + 1]], axis=1)
            drow = jnp.concatenate([dbuf[j], dbuf[j + 1]], axis=1)
            for i in range(NH // 2):
                sl = slice(2 * HD * i, 2 * HD * (i + 1))
                q2, da2 = _stack_heads(qbuf[qrows, sl], lo), _stack_heads(dabuf[qrows, sl], lo)
                ks, vv = k_ref[rows, sl], v_ref[rows, sl]
                pair = lambda t: jnp.concatenate([t[2 * i:2 * i + 1, :], t[2 * i + 1:2 * i + 2, :]], axis=1)
                sc = lax.dot_general(ks, q2, NT, preferred_element_type=F32)
                sc = jnp.where(mask, sc, NEG)
                p = jnp.exp(sc - pair(lrow))
                dp = lax.dot_general(vv, da2, NT, preferred_element_type=F32)
                ds = p * (dp - pair(drow))
                dv_ref[rows, sl] = jnp.dot(p.astype(BF16), da2, preferred_element_type=F32).astype(BF16)
                dk_ref[rows, sl] = jnp.dot(ds.astype(BF16), q2, preferred_element_type=F32).astype(BF16)
            return carry

        lax.fori_loop(0, sub, block, 0, unroll=True)

    steps = nb // sub
    t_main = pl.BlockSpec((NH, sub * QB), lambda r, s: (r, s))
    t_nxt = pl.BlockSpec((NH, QB), lambda r, s: (r, jnp.minimum(sub * (s + 1), nb - 1)))
    out = jax.ShapeDtypeStruct((s, CB), BF16)
    return pl.pallas_call(
        body, name=f"attn_bwd_kv_g{g}", grid=(dil, steps),
        in_specs=[main(k_src[1]), main(v_src[1]), main(q_src[1]), nxt(q_src[1]),
                  main(0), nxt(0), t_main, t_nxt, t_main, t_nxt],
        out_specs=[main(0), main(0)], out_shape=[out, out],
        scratch_shapes=[pltpu.VMEM(((sub + 1) * QB, CB), BF16)] * 2 + [pltpu.VMEM((sub + 1, NH, QB), F32)] * 2,
        compiler_params=_cp(("parallel", "parallel")))(
            k_src[0], v_src[0], q_src[0], q_src[0], da, da, lt, lt, dt, dt)


def _conv_taps(u, u_prev, first):
    tm = u.shape[0]
    row = lax.broadcasted_iota(jnp.int32, (tm, 1), 0)
    up = jnp.where(first, 0.0, u_prev)
    u1 = jnp.where(row == 0, up[HALO - 1:HALO, :], pltpu.roll(u, 1, 0))
    u2 = jnp.where(row == 0, up[HALO - 2:HALO - 1, :],
                   jnp.where(row == 1, up[HALO - 1:HALO, :], pltpu.roll(u, 2, 0)))
    return u1, u2


def mid_fwd(proj, o_g, lse_g, conv_w, expand, tm):
    s = proj.shape[0]
    hb = tm // HALO

    def body(ba_ref, ca_ref, xa_ref, za_ref, cah_ref, xah_ref, zb_ref,
             o0, o1, o2, l0, l1, l2, w_ref, exp_ref, ya_ref, yb_ref, at_ref, lc_ref, buf_o, buf_l):
        first = pl.program_id(0) == 0
        u = ca_ref[...].astype(F32) * xa_ref[...].astype(F32)
        u1, u2 = _conv_taps(u, cah_ref[...].astype(F32) * xah_ref[...].astype(F32), first)
        conv = w_ref[0:1, :] * u2 + w_ref[1:2, :] * u1 + w_ref[2:3, :] * u
        ya_ref[...] = (ba_ref[...].astype(F32) * conv * _silu(za_ref[...].astype(F32))).astype(BF16)
        ls = [_from_residue_major(l, buf_l.at[g], d) for g, (l, d) in enumerate(zip((l0, l1, l2), DILATIONS))]
        mx = jnp.maximum(jnp.maximum(ls[0], ls[1]), ls[2])
        es = [jnp.exp(l - mx) for l in ls]
        den = es[0] + es[1] + es[2]
        attn = jnp.zeros((tm, CB), F32)
        for e, o, d in zip(es, (o0, o1, o2), DILATIONS):
            attn = attn + _dot_hilo(e / den, exp_ref) * _from_residue_major(o, buf_o, d)
        at_ref[...] = attn
        lc_ref[...] = mx + jnp.log(den)
        yb_ref[...] = (attn * _silu(zb_ref[...].astype(F32))).astype(BF16)

    col = lambda j: pl.BlockSpec((tm, D), lambda i: (i, j))
    halo = lambda j: pl.BlockSpec((HALO, D), lambda i: (jnp.maximum(i * hb - 1, 0), j))
    loc = lambda w: pl.BlockSpec((tm, w), lambda i: (i, 0))
    rm = lambda w: [pl.BlockSpec((d, tm // d, w), lambda i: (0, i, 0)) for d in DILATIONS]
    rm_view = lambda ts, w: [t.reshape(d, s // d, w) for t, d in zip(ts, DILATIONS)]
    return pl.pallas_call(
        body, name="mid_fwd", grid=(s // tm,),
        in_specs=[col(0), col(1), col(2), col(3), halo(1), halo(2),
                  pl.BlockSpec((tm, CB), lambda i: (i, CB_ZB))] + rm(CB) + rm(LANES)
                 + [pl.BlockSpec((3, D), lambda i: (0, 0)), pl.BlockSpec(expand.shape, lambda i: (0, 0))],
        out_specs=[loc(D), loc(CB), loc(CB), loc(LANES)],
        out_shape=[jax.ShapeDtypeStruct((s, D), BF16), jax.ShapeDtypeStruct((s, CB), BF16),
                   jax.ShapeDtypeStruct((s, CB), F32), jax.ShapeDtypeStruct((s, LANES), F32)],
        scratch_shapes=[pltpu.VMEM((CB // LANES, tm, LANES), F32), pltpu.VMEM((3, 1, tm, LANES), F32)],
        compiler_params=_cp(("parallel",)))(
            proj, proj, proj, proj, proj, proj, proj, *rm_view(o_g, CB), *rm_view(lse_g, LANES), conv_w, expand)


def tail(proj, ya, yb, attn, x, target, gate, pa_w, pb_w, wo_w, total, conv_w, tm):
    s = proj.shape[0]
    ni = s // tm
    hb = tm // HALO
    nlate = NIN - CB_ZB * CB
    nearly = 4 * D

    def body(ya_ref, yb_ref, ga_ref, gb_ref, zb_ref, at_ref, x_ref, t_ref, gate_ref, pa_ref, pb_ref, wo_ref,
             tot_ref, ba_ref, ca_ref, xa_ref, za_ref, cah_ref, xah_ref, cw_ref,
             dp_hbm, dy_ref, da_ref, dc_ref, mg_ref, do_ref, dpa_ref, dpb_ref, st_ref, gwc_ref,
             stage, dconv_next, sems):
        step = pl.program_id(0)
        i = ni - 1 - step
        slot = step % 2

        def slabs(at_step, sl):
            rows = pl.ds(pl.multiple_of((ni - 1 - at_step) * tm, tm), tm)
            return (pltpu.make_async_copy(stage.at[sl, :, 0:nearly], dp_hbm.at[rows, pl.ds(0, nearly)],
                                          sems.at[sl, 0]),
                    pltpu.make_async_copy(stage.at[sl, :, nearly:], dp_hbm.at[rows, pl.ds(CB_ZB * CB, nlate)],
                                          sems.at[sl, 1]))

        @pl.when(step == 0)
        def _():
            st_ref[...] = jnp.zeros_like(st_ref)
            gwc_ref[...] = jnp.zeros_like(gwc_ref)
            dconv_next[...] = jnp.zeros_like(dconv_next)

        @pl.when(step >= 2)
        def _():
            for cp in slabs(step - 2, slot):
                cp.wait()

        gate_v = gate_ref[...]
        pa = jnp.dot(ya_ref[...], pa_ref[...], preferred_element_type=F32)
        pb = jnp.dot(yb_ref[...], pb_ref[...], preferred_element_type=F32)
        sa = jax.nn.sigmoid(ga_ref[...].astype(F32))
        sb = jax.nn.sigmoid(gb_ref[...].astype(F32))
        merged = (sa * pa + sb * pb).astype(BF16)
        mg_ref[...] = merged
        out = jnp.dot(merged, wo_ref[...], preferred_element_type=F32)
        err = x_ref[...] + gate_v * out - t_ref[...]
        dy = err * (1.0 / D)
        dy_ref[...] = dy
        st_ref[0:1, :] += jnp.sum(dy * out, axis=0, keepdims=True)
        st_ref[1:2, :] += jnp.sum(err * err, axis=0, keepdims=True)
        dout = (gate_v * dy).astype(BF16)
        do_ref[...] = dout
        dmg = lax.dot_general(dout, wo_ref[...], NT, preferred_element_type=F32)
        dpa = (dmg * sa).astype(BF16)
        dpb = (dmg * sb).astype(BF16)
        dpa_ref[...] = dpa
        dpb_ref[...] = dpb
        late = nearly
        stage[slot, :, late + CB:late + CB + D] = (dmg * pa * sa * (1.0 - sa)).astype(BF16)
        stage[slot, :, late + CB + D:] = (dmg * pb * sb * (1.0 - sb)).astype(BF16)
        dya = lax.dot_general(dpa, pa_ref[...], NT, preferred_element_type=F32)
        dyb = lax.dot_general(dpb, pb_ref[...], NT, preferred_element_type=F32)
        zb = zb_ref[...].astype(F32)
        sg = jax.nn.sigmoid(zb)
        attn_v = at_ref[...]
        dattn = dyb * (zb * sg)
        da_ref[...] = dattn.astype(BF16)
        stage[slot, :, late:late + CB] = (dyb * attn_v * (sg * (1.0 + zb * (1.0 - sg)))).astype(BF16)
        dc_ref[...] = _dot_hilo(dattn * attn_v, tot_ref)

        ba, ca, xa, za = (t[...].astype(F32) for t in (ba_ref, ca_ref, xa_ref, za_ref))
        u = ca * xa
        u1, u2 = _conv_taps(u, cah_ref[...].astype(F32) * xah_ref[...].astype(F32), i == 0)
        w0, w1, w2 = cw_ref[0:1, :], cw_ref[1:2, :], cw_ref[2:3, :]
        conv = w0 * u2 + w1 * u1 + w2 * u
        sga = jax.nn.sigmoid(za)
        sza = za * sga
        dconv = dya * ba * sza
        dcn = dconv_next[...]
        rowi = lax.broadcasted_iota(jnp.int32, (tm, 1), 0)
        d1 = jnp.where(rowi == tm - 1, dcn[0:1, :], pltpu.roll(dconv, tm - 1, 0))
        d2 = jnp.where(rowi == tm - 2, dcn[0:1, :],
                       jnp.where(rowi == tm - 1, dcn[1:2, :], pltpu.roll(dconv, tm - 2, 0)))
        du = w2 * dconv + w1 * d1 + w0 * d2
        stage[slot, :, 0:D] = (dya * conv * sza).astype(BF16)
        stage[slot, :, D:2 * D] = (du * xa).astype(BF16)
        stage[slot, :, 2 * D:3 * D] = (du * ca).astype(BF16)
        stage[slot, :, 3 * D:4 * D] = (dya * ba * conv * (sga * (1.0 + za * (1.0 - sga)))).astype(BF16)
        gwc_ref[0:1, :] += jnp.sum(dconv * u2, axis=0, keepdims=True)
        gwc_ref[1:2, :] += jnp.sum(dconv * u1, axis=0, keepdims=True)
        gwc_ref[2:3, :] += jnp.sum(dconv * u, axis=0, keepdims=True)
        dconv_next[...] = dconv[0:8, :]

        for cp in slabs(step, slot):
            cp.start()

        @pl.when(step == ni - 1)
        def _():
            for cp in slabs(step - 1, 1 - slot) + slabs(step, slot):
                cp.wait()

    rev = lambda st: ni - 1 - st
    row = lambda w: pl.BlockSpec((tm, w), lambda st: (rev(st), 0))
    pcol = lambda w, jb: pl.BlockSpec((tm, w), lambda st: (rev(st), jb))
    halo = lambda jb: pl.BlockSpec((HALO, D), lambda st: (jnp.maximum(rev(st) * hb - 1, 0), jb))
    const = lambda a: pl.BlockSpec(a.shape, lambda st: (0, 0), pipeline_mode=pl.Buffered(1))
    acc = pl.BlockSpec((8, D), lambda st: (0, 0))
    return pl.pallas_call(
        body, name="tail", grid=(ni,),
        in_specs=[row(D), row(CB), pcol(D, 9), pcol(D, 10), pcol(CB, CB_ZB), row(CB), row(D), row(D),
                  pl.BlockSpec((1, D), lambda st: (0, 0)), const(pa_w), const(pb_w), const(wo_w), const(total),
                  pcol(D, 0), pcol(D, 1), pcol(D, 2), pcol(D, 3), halo(1), halo(2),
                  pl.BlockSpec((3, D), lambda st: (0, 0))],
        out_specs=[pl.BlockSpec(memory_space=pl.ANY),
                   row(D), row(CB), row(LANES), row(D), row(D), row(D), row(D), acc, acc],
        out_shape=[jax.ShapeDtypeStruct((s, NIN), BF16), jax.ShapeDtypeStruct((s, D), F32),
                   jax.ShapeDtypeStruct((s, CB), BF16), jax.ShapeDtypeStruct((s, LANES), F32)]
                  + [jax.ShapeDtypeStruct((s, D), BF16)] * 4 + [jax.ShapeDtypeStruct((8, D), F32)] * 2,
        scratch_shapes=[pltpu.VMEM((2, tm, nearly + nlate), BF16), pltpu.VMEM((8, D), F32),
                        pltpu.SemaphoreType.DMA((2, 2))],
        compiler_params=_cp(("arbitrary",), 60))(
            ya, yb, proj, proj, proj, attn, x, target, gate, pa_w, pb_w, wo_w, total,
            proj, proj, proj, proj, proj, proj, conv_w)


def _local_step(x, target, shift, scale, gate, norm_w, conv_w, qw, kw, w_shard, small_shards, me_xyc):
    qw8, kw8 = jnp.tile(qw, (1, NH)), jnp.tile(kw, (1, NH))
    same, total, expand = _head_matrices()
    proj, ht, wg, (pa_g, pb_g, wo_g) = proj_fwd_gather(
        x, norm_w, scale, shift, w_shard, small_shards, gather_order(me_xyc), 1024)
    pa_w, wo_w = pa_g.reshape(D, D), wo_g.reshape(D, D)
    pb_w = pb_g.transpose(1, 0, 2).reshape(CB, D)
    srcs = qkv_prep(proj, qw8, kw8, same, 512)
    o_g, lse_g = zip(*[attn_fwd(*srcs[g], g, d) for g, d in enumerate(DILATIONS)])
    ya, yb, attn, lc = mid_fwd(proj, o_g, lse_g, conv_w, expand, 512)
    dproj, dy, da, dc, merged, dout, dpa, dpb, st_tail, st_conv = tail(
        proj, ya, yb, attn, x, target, gate, pa_w, pb_w, wo_w, total, conv_w, 256)
    g_wo = matmul_tn(merged, dout, "grad_w_out", 1024)
    g_pa = matmul_tn(ya, dpa, "grad_w_br_conv", 1024)
    g_pb = matmul_tn(yb, dpb, "grad_w_br_attn", 1024)
    grads = []
    for g, d in enumerate(DILATIONS):
        da_p, lc_p, dc_p, lt, dt = stats_prep(da, lc, dc, g, d, 2048)
        dq = attn_bwd_q(*srcs[g], da_p, lc_p, dc_p, g, d)
        dk, dv = attn_bwd_kv(*srcs[g], da_p, lt, dt, g, d)
        grads.append((dq, dk, dv))
    dproj, gw_qk = qkv_grads_to_dproj(dproj, proj, grads, qw8, kw8, same, 512)
    slabs = [g_pa.reshape(NDEV, 128, D), g_pb.reshape(CB, NDEV, 128).transpose(1, 0, 2), g_wo.reshape(NDEV, 128, D)]
    grad_x, st_norm, r_win, (r_pa, r_pb, r_wo) = proj_bwd(
        ht, dproj, wg, slabs, scatter_order(me_xyc), x, dy, norm_w, scale, 1024)
    dmod = jnp.concatenate([st_norm[0:1], st_norm[1:2], st_tail[0:1]], axis=1)
    loss_part = (0.5 / D) * jnp.sum(st_tail[1])
    gw_heads = gw_qk[0:2].reshape(2, NH, HD).sum(axis=1)
    small = dict(dmod=dmod, norm_w=st_norm[2:3], conv_w=st_conv[0:3],
                 q_norm_w=gw_heads[0:1], k_norm_w=gw_heads[1:2], loss=loss_part)
    return grad_x, small, (r_win, r_pa, r_pb, r_wo)


def kernel(x, c, w_ada, b_ada, norm_w, w_in, conv_w, q_norm_w, k_norm_w, w_br_conv, w_br_attn, w_out, loss_target, m_w_ada, m_b_ada, m_norm_w, m_w_in, m_conv_w, m_q_norm_w, m_k_norm_w, m_w_br_conv, m_w_br_attn, m_w_out, v_w_ada, v_b_ada, v_norm_w, v_w_in, v_conv_w, v_q_norm_w, v_k_norm_w, v_w_br_conv, v_w_br_attn, v_w_out):
    me_xyc = (lax.axis_index("x"), lax.axis_index("y"), lax.axis_index("c"))
    me = _dev_index(me_xyc)
    ncol = w_ada.shape[2]

    conv_pad = jnp.zeros((8, 128), F32).at[0:3].set(conv_w[0])
    c_all, conv_all = all_gather([c, conv_pad], "gather_cond")
    conv_full = conv_all[:, 0:3].transpose(1, 0, 2).reshape(3, D)
    c_all = c_all.reshape(NDEV, D)

    b_cols = lax.dynamic_slice(b_ada, (0, me * ncol), (1, ncol))
    mod_cols = ada_fwd(c_all, w_ada[0], b_cols)
    (mod_all,) = all_gather([mod_cols], "gather_mod")
    mod = lax.dynamic_index_in_dim(mod_all, me, axis=1, keepdims=False).reshape(1, 3 * D)
    shift, scale, gate = mod[:, 0:D], mod[:, D:2 * D], mod[:, 2 * D:3 * D]

    grad_x, small, (r_win, r_pa, r_pb, r_wo) = _local_step(
        x[0], loss_target[0], shift, scale, gate, norm_w, conv_full, q_norm_w, k_norm_w,
        w_in[0].astype(BF16), [w_br_conv[0].astype(BF16), w_br_attn[0].astype(BF16), w_out[0].astype(BF16)], me_xyc)

    packed = jnp.concatenate(
        [small["dmod"], small["norm_w"], small["conv_w"].reshape(1, 3 * D), small["q_norm_w"], small["k_norm_w"],
         jnp.full((1, 128), small["loss"], F32)], axis=1)
    (packed_all,) = all_gather([packed], "gather_small")
    tot = sum_parts(packed_all)
    loss = tot[0, 7 * D + 2 * HD]
    dmod_all = packed_all[:, 0, 0:3 * D]
    g_b_ada = tot[:, 0:3 * D]
    g_norm_w = tot[:, 3 * D:4 * D]
    g_conv = lax.dynamic_slice(tot[:, 4 * D:7 * D].reshape(3, D), (0, me * 128), (3, 128))
    g_qn = tot[:, 7 * D:7 * D + HD]
    g_kn = tot[:, 7 * D + HD:7 * D + 2 * HD]
    g_w_ada = ada_bwd(c_all.T, lax.dynamic_slice(dmod_all, (0, me * ncol), (NDEV, ncol)))

    def upd(parts, w, m, v, name, rows):
        shape = w.shape
        w2, m2, v2 = (t.reshape(shape[-2:]) for t in (w, m, v))
        return [t.reshape(shape) for t in adamw(parts, w2, m2, v2, name, rows)]

    res = {
        "w_ada": upd(g_w_ada[None], w_ada, m_w_ada, v_w_ada, "adamw_w_ada", 256),
        "b_ada": upd(g_b_ada[None], b_ada, m_b_ada, v_b_ada, "adamw_b_ada", 1),
        "norm_w": upd(g_norm_w[None], norm_w, m_norm_w, v_norm_w, "adamw_norm_w", 1),
        "w_in": upd(r_win, w_in, m_w_in, v_w_in, "adamw_w_in", 128),
        "conv_w": upd(g_conv[None], conv_w, m_conv_w, v_conv_w, "adamw_conv_w", 3),
        "q_norm_w": upd(g_qn[None], q_norm_w, m_q_norm_w, v_q_norm_w, "adamw_q_norm_w", 1),
        "k_norm_w": upd(g_kn[None], k_norm_w, m_k_norm_w, v_k_norm_w, "adamw_k_norm_w", 1),
        "w_br_conv": upd(r_pa, w_br_conv, m_w_br_conv, v_w_br_conv, "adamw_w_br_conv", 128),
        "w_br_attn": upd(r_pb, w_br_attn, m_w_br_attn, v_w_br_attn, "adamw_w_br_attn", 512),
        "w_out": upd(r_wo, w_out, m_w_out, v_w_out, "adamw_w_out", 128),
    }
    names = ["w_ada", "b_ada", "norm_w", "w_in", "conv_w", "q_norm_w", "k_norm_w", "w_br_conv", "w_br_attn", "w_out"]
    return (loss, grad_x[None], *[res[n][0] for n in names], *[res[n][1] for n in names],
            *[res[n][2] for n in names], *[res[n][3] for n in names])
```

```python
import jax
import jax.numpy as jnp
from jax import lax
from jax.experimental import pallas as pl
from jax.experimental.pallas import tpu as pltpu

F32, BF16 = jnp.float32, jnp.bfloat16
D = 1024
NIN = 11264
NDEV = 8
SHARD = NIN // NDEV
HD = 64
NH = 8
QB = 128
CB = 512
CB_Q, CB_K, CB_V, CB_ZB = 8, 11, 14, 17
DILATIONS = (1, 4, 16)
EPS = 1e-6
NEG = -1e30
HALO = 16
LANES = 128
MESH = pl.DeviceIdType.MESH

ADAM_LR, ADAM_B1, ADAM_B2, ADAM_EPS, ADAM_WD, ADAM_STEP = 0.001, 0.9, 0.999, 1e-08, 0.01, 10

NT = (((1,), (1,)), ((), ()))
TN = (((0,), (0,)), ((), ()))


def _cp(sem, vmem_mb=48):
    return pltpu.CompilerParams(dimension_semantics=sem, vmem_limit_bytes=vmem_mb << 20)


def _silu(z):
    return z * jax.nn.sigmoid(z)


def _coords():
    return lax.axis_index("x"), lax.axis_index("y"), lax.axis_index("c")


FLIPS = [(fx, fy, fc) for fx in (0, 1) for fy in (0, 1) for fc in (0, 1)][1:]


def all_gather(arrs, name):
    n = len(arrs)

    def body(*refs):
        ins, outs = refs[:n], refs[n:2 * n]
        send_sems, recv_sems, local_sems = refs[2 * n:]
        me_xyc = _coords()
        me = _dev_index(me_xyc)
        peers = [_flip(me_xyc, f) for f in FLIPS]

        def copy(a, k, block):
            return pltpu.make_async_remote_copy(
                src_ref=ins[a], dst_ref=outs[a].at[block], send_sem=send_sems.at[a, k], recv_sem=recv_sems.at[a, k],
                device_id=peers[k], device_id_type=MESH)

        mine = [pltpu.make_async_copy(ins[a], outs[a].at[me], local_sems.at[a]) for a in range(n)]
        sends = [copy(a, k, me) for k in range(7) for a in range(n)]
        for cp in mine + sends:
            cp.start()
        for k in range(7):
            for a in range(n):
                copy(a, k, _dev_index(peers[k])).wait_recv()
        for cp in sends:
            cp.wait_send()
        for cp in mine:
            cp.wait()

    any_spec = pl.BlockSpec(memory_space=pl.ANY)
    return pl.pallas_call(
        body, name=name,
        out_shape=[jax.ShapeDtypeStruct((NDEV,) + a.shape, a.dtype) for a in arrs],
        in_specs=[any_spec] * n, out_specs=[any_spec] * n,
        scratch_shapes=[pltpu.SemaphoreType.DMA((n, 7)), pltpu.SemaphoreType.DMA((n, 7)),
                        pltpu.SemaphoreType.DMA((n,))],
    )(*arrs)


def _flip(dev, f):
    return tuple(1 - v if b else v for v, b in zip(dev, f))


def _dev_index(dev):
    return 4 * dev[0] + 2 * dev[1] + dev[2]


def _chip_order(x, y, c):
    xor = lambda a, b: a + b - 2 * a * b
    return [(xor(x, 1 - c), xor(y, c)), (xor(x, c), xor(y, 1 - c)), (1 - x, 1 - y)]


def gather_order(me_xyc):
    x, y, c = me_xyc
    chips = _chip_order(x, y, c)
    devs = [(x, y, c), (x, y, 1 - c), (*chips[0], c), (*chips[1], c),
            (*chips[1], 1 - c), (*chips[0], 1 - c), (*chips[2], c), (*chips[2], 1 - c)]
    return jnp.stack([_dev_index(d) for d in devs]).astype(jnp.int32)


def scatter_order(me_xyc):
    devs = [_flip(me_xyc, f) for f in FLIPS] + [me_xyc]
    return jnp.stack([_dev_index(d) for d in devs]).astype(jnp.int32)


def ada_fwd(c_all, w_ada, b_cols):
    def body(c_ref, w_ref, b_ref, o_ref):
        a = _silu(c_ref[...]).astype(BF16)
        o_ref[...] = jnp.dot(a, w_ref[...].astype(BF16), preferred_element_type=F32) + b_ref[...]

    return pl.pallas_call(body, name="ada_fwd",
                          out_shape=jax.ShapeDtypeStruct((NDEV, w_ada.shape[1]), F32))(c_all, w_ada, b_cols)


def ada_bwd(c_all_t, dmod_cols):
    def body(c_ref, d_ref, o_ref):
        at = _silu(c_ref[...])
        acc = at[:, 0:1] * d_ref[0:1, :]
        for b in range(1, NDEV):
            acc = acc + at[:, b:b + 1] * d_ref[b:b + 1, :]
        o_ref[...] = acc

    return pl.pallas_call(body, name="ada_bwd",
                          out_shape=jax.ShapeDtypeStruct((D, dmod_cols.shape[1]), F32))(c_all_t, dmod_cols)


def sum_parts(parts):
    def body(p_ref, o_ref):
        acc = p_ref[0]
        for b in range(1, NDEV):
            acc = acc + p_ref[b]
        o_ref[...] = acc

    return pl.pallas_call(body, name="sum_parts",
                          out_shape=jax.ShapeDtypeStruct(parts.shape[1:], F32))(parts)


def _adamw_update(g, w_ref, m_ref, v_ref, g_ref, d_ref, nm_ref, nv_ref):
    nm = ADAM_B1 * m_ref[...] + (1.0 - ADAM_B1) * g
    nv = ADAM_B2 * v_ref[...] + (1.0 - ADAM_B2) * (g * g)
    g_ref[...] = g
    nm_ref[...] = nm
    nv_ref[...] = nv
    m_hat = nm / (1.0 - ADAM_B1 ** ADAM_STEP)
    v_hat = nv / (1.0 - ADAM_B2 ** ADAM_STEP)
    d_ref[...] = -ADAM_LR * (m_hat / (jnp.sqrt(v_hat) + ADAM_EPS) + ADAM_WD * w_ref[...])


def adamw_small(items):
    n = len(items)

    def body(*refs):
        ins, outs = refs[:4 * n], refs[4 * n:]
        for a in range(n):
            g_in, w_ref, m_ref, v_ref = ins[4 * a:4 * a + 4]
            _adamw_update(g_in[...], w_ref, m_ref, v_ref, *outs[4 * a:4 * a + 4])

    out = pl.pallas_call(
        body, name="adamw_small",
        out_shape=[jax.ShapeDtypeStruct(it[1].shape, F32) for it in items for _ in range(4)],
        compiler_params=pltpu.CompilerParams(vmem_limit_bytes=48 << 20))(*[t for it in items for t in it])
    return [out[4 * a:4 * a + 4] for a in range(n)]


def adamw(parts, w, m, v, name, rows):
    n, r, ccols = parts.shape

    def body(p_ref, w_ref, m_ref, v_ref, g_ref, d_ref, nm_ref, nv_ref):
        g = p_ref[0].astype(F32)
        for b in range(1, n):
            g = g + p_ref[b].astype(F32)
        _adamw_update(g, w_ref, m_ref, v_ref, g_ref, d_ref, nm_ref, nv_ref)

    blk = pl.BlockSpec((rows, ccols), lambda i: (i, 0))
    out = jax.ShapeDtypeStruct((r, ccols), F32)
    return pl.pallas_call(
        body, name=name, grid=(r // rows,),
        in_specs=[pl.BlockSpec((n, rows, ccols), lambda i: (0, i, 0)), blk, blk, blk],
        out_specs=[blk] * 4, out_shape=[out] * 4, compiler_params=_cp(("parallel",)))(parts, w, m, v)


def proj_fwd_gather(x, nw, scale, shift, w_shard, extras, order, tm):
    s = x.shape[0]
    ni = s // tm
    n = 1 + len(extras)
    mid = ni - 2

    def body(order_ref, x_ref, nw_ref, sc_ref, sh_ref, *refs):
        ins, o_ref, ht_ref, outs = refs[:n], refs[n], refs[n + 1], refs[n + 2:2 * n + 2]
        h_all, wbuf, send_sems, recv_sems, local_sems, load_sems = refs[2 * n + 2:]
        jj, i = pl.program_id(0), pl.program_id(1)
        x, y, c = _coords()
        me, sibling = (x, y, c), (x, y, 1 - c)
        chips = _chip_order(x, y, c)
        relayed = [(*chips[1], 1 - c), (*chips[0], 1 - c), (*chips[2], 1 - c)]

        def slot(a, dev):
            return outs[a].at[_dev_index(dev)]

        def copy(a, k, block, to, src=None):
            return pltpu.make_async_remote_copy(
                src_ref=slot(a, block) if src is None else src, dst_ref=slot(a, block),
                send_sem=send_sems.at[a, k], recv_sem=recv_sems.at[a, k], device_id=to, device_id_type=MESH)

        mine = [pltpu.make_async_copy(ins[a], slot(a, me), local_sems.at[a]) for a in range(n)]
        to_sibling = [copy(a, 0, me, sibling, src=ins[a]) for a in range(n)]
        to_chip = [[copy(a, 1 + j, me, (*chips[j], c), src=ins[a]) for a in range(n)] for j in range(2)]
        onward = [copy(a, 3, (*chips[1], c), (*chips[0], c)) for a in range(n)]
        passed = [[copy(a, 4 + j, (*ch, c), sibling) for a in range(n)] for j, ch in enumerate(chips)]
        sends = lambda a: [to_sibling[a], to_chip[0][a], to_chip[1][a], onward[a]] + [passed[j][a] for j in range(3)]

        def arrived(a, j):
            copy(a, 1 + j, (*chips[j], c), me).wait_recv()

        def load(row):
            return pltpu.make_async_copy(outs[0].at[order_ref[row]], wbuf.at[row % 2], load_sems.at[row % 2])

        @pl.when((jj == 0) & (i == 0))
        def _():
            for cp in mine:
                cp.start()
            to_sibling[0].start()
            to_chip[0][0].start()
            pltpu.make_async_copy(ins[0], wbuf.at[0], load_sems.at[0]).start()

        @pl.when((jj == 1) & (i == 0))
        def _():
            to_chip[1][0].start()

        @pl.when((jj == 4) & (i == 0))
        def _():
            for a in range(1, n):
                to_sibling[a].start()
                to_chip[0][a].start()
                to_chip[1][a].start()

        direct = {2: 0, 3: 1, 6: 2}
        relay = {4: 0, 5: 1, 7: 2}

        @pl.when((jj == 0) & (i == mid))
        def _():
            copy(0, 0, sibling, me).wait_recv()

        for row, j in direct.items():
            @pl.when((jj == row - 1) & (i == mid))
            def _(j=j):
                arrived(0, j)
                passed[j][0].start()
                if j == 1:
                    onward[0].start()

        for row, j in relay.items():
            @pl.when((jj == row - 1) & (i == mid))
            def _(j=j):
                copy(0, 4 + j, relayed[j], me).wait_recv()

        @pl.when((jj == NDEV - 1) & (i == 0))
        def _():
            for a in range(1, n):
                arrived(a, 1)
                onward[a].start()
                passed[1][a].start()
                arrived(a, 0)
                passed[0][a].start()

        @pl.when((jj < NDEV - 1) & (i == mid))
        def _():
            load(jj + 1).start()

        @pl.when(i == 0)
        def _():
            load(jj).wait()

        @pl.when(jj == 0)
        def _():
            xf = x_ref[...]
            r = lax.rsqrt(jnp.mean(xf * xf, axis=-1, keepdims=True) + EPS)
            h = (xf * r * nw_ref[...]) * (1.0 + sc_ref[...]) + sh_ref[...]
            h_all[i] = h.astype(BF16)
            ht_ref[...] = h.T.astype(BF16)

        o_ref[...] = jnp.dot(h_all[i], wbuf[jj % 2], preferred_element_type=F32).astype(BF16)

        @pl.when((jj == NDEV - 1) & (i == ni - 1))
        def _():
            for a in range(1, n):
                arrived(a, 2)
                passed[2][a].start()
            for a in range(1, n):
                copy(a, 0, sibling, me).wait_recv()
                for j in range(3):
                    copy(a, 4 + j, relayed[j], me).wait_recv()
            for a in range(n):
                mine[a].wait()
                for cp in sends(a):
                    cp.wait_send()

    any_spec = pl.BlockSpec(memory_space=pl.ANY)
    vec = pl.BlockSpec((1, D), lambda jj, i, o: (0, 0))
    outs = pl.pallas_call(
        body, name="proj_fwd_gather",
        grid_spec=pltpu.PrefetchScalarGridSpec(
            num_scalar_prefetch=1, grid=(NDEV, ni),
            in_specs=[pl.BlockSpec((tm, D), lambda jj, i, o: (jnp.where(jj == 0, i, ni - 1), 0))] + [vec] * 3
                     + [any_spec] * n,
            out_specs=[pl.BlockSpec((tm, SHARD), lambda jj, i, o: (i, o[jj])),
                       pl.BlockSpec((D, tm), lambda jj, i, o: (0, jnp.where(jj == 0, i, ni - 1)))]
                      + [any_spec] * n,
            scratch_shapes=[pltpu.VMEM((ni, tm, D), BF16), pltpu.VMEM((2, D, SHARD), BF16),
                            pltpu.SemaphoreType.DMA((n, 7)), pltpu.SemaphoreType.DMA((n, 7)),
                            pltpu.SemaphoreType.DMA((n,)), pltpu.SemaphoreType.DMA((2,))]),
        out_shape=[jax.ShapeDtypeStruct((s, NIN), BF16), jax.ShapeDtypeStruct((D, s), BF16),
                   jax.ShapeDtypeStruct((NDEV, D, SHARD), BF16)]
                  + [jax.ShapeDtypeStruct((NDEV,) + e.shape, e.dtype) for e in extras],
        compiler_params=_cp(("arbitrary", "arbitrary"), 56))(order, x, nw, scale, shift, w_shard, *extras)
    return outs[0], outs[1], outs[2], outs[3:]


def proj_bwd(ht, dproj, wg, smalls, order, x, dy, nw, scale, tt):
    s = dproj.shape[0]
    nk = s // tt
    n = len(smalls)
    rows_per_step = tt // nk
    last = 2 * NDEV

    def body(order_ref, ht_ref, dp_ref, w_ref, x_ref, dy_ref, nw_ref, sc_ref, *rest):
        small_in = rest[:n]
        gx_ref, st_ref, gw_ref, rwin_ref = rest[n:n + 4]
        small_out = rest[n + 4:2 * n + 4]
        acc, stage, dh, send_sems, recv_sems, local_sems, stage_sems = rest[2 * n + 4:]
        t, k = pl.program_id(0), pl.program_id(1)
        me_xyc = _coords()
        me = _dev_index(me_xyc)
        peers = [_flip(me_xyc, f) for f in FLIPS]

        def exchange(a, kf, src_arr, dst_arr):
            pid = _dev_index(peers[kf])
            mk = lambda dst: pltpu.make_async_remote_copy(
                src_ref=src_arr.at[pid], dst_ref=dst, send_sem=send_sems.at[a, kf], recv_sem=recv_sems.at[a, kf],
                device_id=peers[kf], device_id_type=MESH)
            return mk(dst_arr.at[me]), mk(dst_arr.at[pid])

        small_pairs = [exchange(1 + a, kf, small_in[a], small_out[a]) for kf in range(7) for a in range(n)]
        small_own = [pltpu.make_async_copy(small_in[a].at[me], small_out[a].at[me], local_sems.at[1 + a])
                     for a in range(n)]
        win_pairs = [exchange(0, kf, gw_ref, rwin_ref) for kf in range(7)]
        win_own = pltpu.make_async_copy(gw_ref.at[me], rwin_ref.at[me], local_sems.at[0])

        def to_hbm(jj):
            slab = me if jj == 7 else _dev_index(peers[jj])
            return pltpu.make_async_copy(stage.at[jj % 2], gw_ref.at[slab], stage_sems.at[jj % 2])

        @pl.when((t == 0) & (k == 0))
        def _():
            for cp in small_own:
                cp.start()
            for send, _ in small_pairs:
                send.start()

        @pl.when(t < NDEV)
        def _():
            p = jnp.dot(ht_ref[...], dp_ref[...], preferred_element_type=F32)

            @pl.when(k == 0)
            def _():
                acc[...] = p

            @pl.when(k > 0)
            def _():
                acc[...] += p

        for jj in range(NDEV):
            @pl.when((t == jj) & (k == nk - 1))
            def _(jj=jj):
                stage[jj % 2] = acc[...].astype(BF16)
                to_hbm(jj).start()

            @pl.when((t == jj + 1) & (k == 1))
            def _(jj=jj):
                to_hbm(jj).wait()
                if jj < 7:
                    win_pairs[jj][0].start()
                else:
                    win_own.start()

        def matmul_step():
            p = lax.dot_general(dp_ref[...], w_ref[...], NT, preferred_element_type=F32)
            slot = t % 2
            dh[slot] = jnp.where(k == 0, p, dh[slot] + p)

        def norm_step():
            g = dh.at[(t + 1) % 2][pl.ds(pl.multiple_of(k * rows_per_step, rows_per_step), rows_per_step), :]
            xf = x_ref[...]
            r = lax.rsqrt(jnp.mean(xf * xf, axis=-1, keepdims=True) + EPS)
            xh = xf * r
            dn = g * (1.0 + sc_ref[...])
            dxh = dn * nw_ref[...]
            gx_ref[...] = dy_ref[...] + r * (dxh - xh * jnp.mean(dxh * xh, axis=-1, keepdims=True))
            st_ref[0:1, :] += jnp.sum(g, axis=0, keepdims=True)
            st_ref[1:2, :] += jnp.sum(g * xh * nw_ref[...], axis=0, keepdims=True)
            st_ref[2:3, :] += jnp.sum(dn * xh, axis=0, keepdims=True)

        @pl.when((t == 0) & (k == 0))
        def _():
            st_ref[...] = jnp.zeros_like(st_ref)

        @pl.when(t == NDEV)
        def _():
            matmul_step()

        @pl.when((t > NDEV) & (t < last))
        def _():
            matmul_step()
            norm_step()

        @pl.when(t == last)
        def _():
            norm_step()

        @pl.when((t == last) & (k == nk - 1))
        def _():
            for _, recv in win_pairs + small_pairs:
                recv.wait_recv()
            for send, _ in win_pairs + small_pairs:
                send.wait_send()
            win_own.wait()
            for cp in small_own:
                cp.wait()

    any_spec = pl.BlockSpec(memory_space=pl.ANY)
    first = lambda t: t < NDEV
    slab = lambda t, k: jnp.where(t == last, NDEV - 1, k)
    chunk = pl.BlockSpec((rows_per_step, D), lambda t, k, o: (jnp.maximum((t - NDEV - 1) * nk + k, 0), 0))
    vec = pl.BlockSpec((1, D), lambda t, k, o: (0, 0))
    outs = pl.pallas_call(
        body, name="proj_bwd",
        grid_spec=pltpu.PrefetchScalarGridSpec(
            num_scalar_prefetch=1, grid=(last + 1, nk),
            in_specs=[pl.BlockSpec((D, tt), lambda t, k, o: (0, jnp.where(first(t), k, nk - 1))),
                      pl.BlockSpec((tt, SHARD), lambda t, k, o: (jnp.where(first(t), k, jnp.minimum(t, last - 1) - NDEV),
                                                                 jnp.where(first(t), o[jnp.minimum(t, NDEV - 1)],
                                                                           slab(t, k)))),
                      pl.BlockSpec((None, D, SHARD), lambda t, k, o: (jnp.where(first(t), 0, slab(t, k)), 0, 0)),
                      chunk, chunk, vec, vec]
                     + [any_spec] * n,
            out_specs=[chunk, pl.BlockSpec((8, D), lambda t, k, o: (0, 0))] + [any_spec] * (2 + n),
            scratch_shapes=[pltpu.VMEM((D, SHARD), F32), pltpu.VMEM((2, D, SHARD), BF16),
                            pltpu.VMEM((2, tt, D), F32),
                            pltpu.SemaphoreType.DMA((1 + n, 7)), pltpu.SemaphoreType.DMA((1 + n, 7)),
                            pltpu.SemaphoreType.DMA((1 + n,)), pltpu.SemaphoreType.DMA((2,))]),
        out_shape=[jax.ShapeDtypeStruct((s, D), F32), jax.ShapeDtypeStruct((8, D), F32),
                   jax.ShapeDtypeStruct((NDEV, D, SHARD), BF16), jax.ShapeDtypeStruct((NDEV, D, SHARD), BF16)]
                  + [jax.ShapeDtypeStruct(a.shape, a.dtype) for a in smalls],
        compiler_params=_cp(("arbitrary", "arbitrary"), 56))(order, ht, dproj, wg, x, dy, nw, scale, *smalls)
    return outs[0], outs[1], outs[3], outs[4:]


def matmul_tn(a, b, name, tk):
    s, m = a.shape
    n = b.shape[1]
    nk = s // tk

    def body(a_ref, b_ref, o_ref, acc_ref):
        k = pl.program_id(0)
        p = lax.dot_general(a_ref[...], b_ref[...], TN, preferred_element_type=F32)

        @pl.when(k == 0)
        def _():
            acc_ref[...] = p

        @pl.when(k > 0)
        def _():
            acc_ref[...] += p

        @pl.when(k == nk - 1)
        def _():
            o_ref[...] = acc_ref[...].astype(BF16)

    return pl.pallas_call(
        body, name=name, grid=(nk,),
        in_specs=[pl.BlockSpec((tk, m), lambda k: (k, 0)), pl.BlockSpec((tk, n), lambda k: (k, 0))],
        out_specs=pl.BlockSpec((m, n), lambda k: (0, 0)),
        out_shape=jax.ShapeDtypeStruct((m, n), BF16),
        scratch_shapes=[pltpu.VMEM((m, n), F32)],
        compiler_params=_cp(("arbitrary",)))(a, b)


def _head_matrices():
    lane = lax.broadcasted_iota(jnp.int32, (CB, CB), 0)
    col = lax.broadcasted_iota(jnp.int32, (CB, CB), 1)
    same = (lane // HD == col // HD).astype(BF16)
    lane_c = lax.broadcasted_iota(jnp.int32, (CB, LANES), 0)
    col_c = lax.broadcasted_iota(jnp.int32, (CB, LANES), 1)
    total = (lane_c // HD == col_c).astype(BF16)
    lane_e = lax.broadcasted_iota(jnp.int32, (LANES, CB), 0)
    col_e = lax.broadcasted_iota(jnp.int32, (LANES, CB), 1)
    expand = (lane_e == col_e // HD).astype(BF16)
    return same, total, expand


def _head_sum(x, m_ref):
    return jnp.dot(x.astype(BF16), m_ref[...], preferred_element_type=F32)


def _dot_hilo(x, m_ref):
    hi = x.astype(BF16)
    lo = (x - hi.astype(F32)).astype(BF16)
    return (jnp.dot(hi, m_ref[...], preferred_element_type=F32)
            + jnp.dot(lo, m_ref[...], preferred_element_type=F32))


def _to_residue_major(val, buf, out_ref, dil):
    rows = out_ref.shape[1]
    for k in range(val.shape[1] // LANES):
        lanes = slice(k * LANES, (k + 1) * LANES)
        buf[k] = val[:, lanes]
        for r in range(dil):
            out_ref[r, :, lanes] = buf.at[k][pl.ds(r, rows, stride=dil), :].astype(out_ref.dtype)


def _from_residue_major(ref, buf, dil):
    if dil == 1:
        return ref[0].astype(F32)
    rows, chunks = ref.shape[1], ref.shape[2] // LANES
    for k in range(chunks):
        for r in range(dil):
            buf.at[k][pl.ds(r, rows, stride=dil), :] = ref[r, :, k * LANES:(k + 1) * LANES].astype(F32)
    return jnp.concatenate([buf[k] for k in range(chunks)], axis=1)


def qkv_prep(proj, qw8, kw8, same, tm):
    s = proj.shape[0]
    items = []
    for g, d in enumerate(DILATIONS):
        items += [(g, "q", CB_Q + g, d), (g, "k", CB_K + g, d)] + ([(g, "v", CB_V + g, d)] if d > 1 else [])
    n = len(items)

    def body(*refs):
        ins, (qw_ref, kw_ref, same_ref), outs, buf = refs[:n], refs[n:n + 3], refs[n + 3:2 * n + 3], refs[-1]
        for idx, (_, kind, _, dil) in enumerate(items):
            val = ins[idx][...].astype(F32)
            if kind != "v":
                r = lax.rsqrt(_head_sum(val * val, same_ref) * (1.0 / HD) + EPS)
                val = val * r * (qw_ref if kind == "q" else kw_ref)[...]
            if dil == 1:
                outs[idx][0] = val.astype(BF16)
            else:
                _to_residue_major(val, buf, outs[idx], dil)

    full = lambda a: pl.BlockSpec(a.shape, lambda i: (0, 0))
    outs = pl.pallas_call(
        body, name="qkv_prep", grid=(s // tm,),
        in_specs=[pl.BlockSpec((tm, CB), lambda i, cb=cb: (i, cb)) for _, _, cb, _ in items]
                 + [full(qw8), full(kw8), full(same)],
        out_specs=[pl.BlockSpec((d, tm // d, CB), lambda i: (0, i, 0)) for _, _, _, d in items],
        out_shape=[jax.ShapeDtypeStruct((d, s // d, CB), BF16) for _, _, _, d in items],
        scratch_shapes=[pltpu.VMEM((CB // LANES, tm, LANES), F32)],
        compiler_params=_cp(("parallel",)))(*([proj] * n), qw8 * (HD ** -0.5), kw8, same)
    srcs = [[None, None, (proj, CB_V + g)] for g in range(len(DILATIONS))]
    for (g, kind, _, _), o in zip(items, outs):
        srcs[g]["qkv".index(kind)] = (o.reshape(s, CB), 0)
    return srcs


def stats_prep(da, lc, dc, g, dil, tm):
    s = da.shape[0]
    rows = tm // dil

    def body(da_ref, lc_ref, dc_ref, dap_ref, lcp_ref, dcp_ref, lt_ref, dt_ref, buf):
        if dil == 1:
            dap_ref[0] = da_ref[...]
        else:
            _to_residue_major(da_ref[...].astype(F32), buf, dap_ref, dil)
        for src, dst, dst_t in ((lc_ref, lcp_ref, lt_ref), (dc_ref, dcp_ref, dt_ref)):
            buf[0] = src[...]
            for r in range(dil):
                piece = buf.at[0][pl.ds(r, rows, stride=dil), :] if dil > 1 else buf[0]
                dst[r] = piece
                dst_t[r] = piece.T[0:NH, :]

    row = lambda w: pl.BlockSpec((tm, w), lambda i: (i, 0))
    rm = lambda w: pl.BlockSpec((dil, rows, w), lambda i: (0, i, 0))
    tr = pl.BlockSpec((dil, NH, rows), lambda i: (0, 0, i))
    length = s // dil
    dap, lcp, dcp, lt, dt = pl.pallas_call(
        body, name=f"stats_prep_g{g}", grid=(s // tm,),
        in_specs=[row(CB), row(LANES), row(LANES)],
        out_specs=[rm(CB), rm(LANES), rm(LANES), tr, tr],
        out_shape=[jax.ShapeDtypeStruct((dil, length, CB), BF16)]
                  + [jax.ShapeDtypeStruct((dil, length, LANES), F32)] * 2
                  + [jax.ShapeDtypeStruct((dil, NH, length), F32)] * 2,
        scratch_shapes=[pltpu.VMEM((CB // LANES, tm, LANES), F32)],
        compiler_params=_cp(("parallel",)))(da, lc, dc)
    return (dap.reshape(s, CB), lcp.reshape(s, LANES), dcp.reshape(s, LANES),
            lt.reshape(dil * NH, length), dt.reshape(dil * NH, length))


def qkv_grads_to_dproj(dproj, proj, grads, qw8, kw8, same, tm):
    s = dproj.shape[0]
    ni = s // tm
    flat = [(t.reshape(d, s // d, CB), d, kind, 3 * kind + g)
            for g, d in enumerate(DILATIONS) for kind, t in enumerate(grads[g])]
    nf = len(flat)
    nraw = 2 * len(DILATIONS)

    def body(*refs):
        dp_hbm, raws, ins = refs[nraw + nf + 4], refs[1:1 + nraw], refs[1 + nraw:1 + nraw + nf]
        qw_ref, kw_ref, same_ref = refs[1 + nraw + nf:4 + nraw + nf]
        gw_ref, stage, buf, sems = refs[5 + nraw + nf:]
        i = pl.program_id(0)
        slot = i % 2

        def slab(step, sl):
            return pltpu.make_async_copy(
                stage.at[sl], dp_hbm.at[pl.ds(pl.multiple_of(step * tm, tm), tm), pl.ds(CB_Q * CB, 9 * CB)],
                sems.at[sl])

        @pl.when(i == 0)
        def _():
            gw_ref[...] = jnp.zeros_like(gw_ref)

        @pl.when(i >= 2)
        def _():
            slab(i - 2, slot).wait()

        for ref, (_, d, kind, jj) in zip(ins, flat):
            cols = slice(jj * CB, (jj + 1) * CB)
            dn = _from_residue_major(ref, buf, d)
            if kind == 2:
                stage[slot, :, cols] = dn.astype(BF16)
                continue
            t = raws[jj][...].astype(F32)
            r = lax.rsqrt(_head_sum(t * t, same_ref) * (1.0 / HD) + EPS)
            xh = t * r
            gw_ref[kind:kind + 1, :] += jnp.sum(dn * xh, axis=0, keepdims=True)
            dxh = dn * (qw_ref if kind == 0 else kw_ref)[...]
            mean = _head_sum(dxh * xh, same_ref) * (1.0 / HD)
            stage[slot, :, cols] = (r * (dxh - xh * mean)).astype(BF16)
        slab(i, slot).start()

        @pl.when(i == ni - 1)
        def _():
            slab(i - 1, 1 - slot).wait()
            slab(i, slot).wait()

    full = lambda a: pl.BlockSpec(a.shape, lambda i: (0, 0))
    any_spec = pl.BlockSpec(memory_space=pl.ANY)
    return pl.pallas_call(
        body, name="qkv_grads_to_dproj", grid=(ni,),
        in_specs=[any_spec] + [pl.BlockSpec((tm, CB), lambda i, jb=jb: (i, CB_Q + jb)) for jb in range(nraw)]
                 + [pl.BlockSpec((d, tm // d, CB), lambda i: (0, i, 0)) for _, d, _, _ in flat]
                 + [full(qw8), full(kw8), full(same)],
        out_specs=[any_spec, pl.BlockSpec((8, CB), lambda i: (0, 0))],
        out_shape=[jax.ShapeDtypeStruct((s, NIN), BF16), jax.ShapeDtypeStruct((8, CB), F32)],
        input_output_aliases={0: 0},
        scratch_shapes=[pltpu.VMEM((2, tm, 9 * CB), BF16), pltpu.VMEM((CB // LANES, tm, LANES), F32),
                        pltpu.SemaphoreType.DMA((2,))],
        compiler_params=_cp(("arbitrary",)))(
            dproj, *([proj] * nraw), *[t for t, _, _, _ in flat], qw8, kw8, same)


def _lane_lo():
    return lax.broadcasted_iota(jnp.int32, (1, 2 * HD), 1) < HD


def _stack_heads(t, lo):
    zero = jnp.zeros_like(t)
    return jnp.concatenate([jnp.where(lo, t, zero), jnp.where(lo, zero, t)], axis=0)


def _masks(other_ok):
    qi = lax.broadcasted_iota(jnp.int32, (QB, QB), 0)
    kj = lax.broadcasted_iota(jnp.int32, (QB, QB), 1)
    return (kj >= qi) & other_ok, kj <= qi


MAX_SUB = 8


def _attn_specs(nb, dil, sub):
    steps = nb // sub
    main = lambda cb, w=CB: pl.BlockSpec((sub * QB, w), lambda r, s: (r * steps + s, cb))
    prev = lambda cb: pl.BlockSpec((QB, CB), lambda r, s: (jnp.maximum(r * nb + sub * s - 1, 0), cb))
    nxt = lambda cb: pl.BlockSpec((QB, CB), lambda r, s: (jnp.minimum(r * nb + sub * (s + 1), dil * nb - 1), cb))
    return main, prev, nxt


def attn_fwd(q_src, k_src, v_src, g, dil):
    s = q_src[0].shape[0]
    nb = s // dil // QB
    sub = min(MAX_SUB, nb)
    main, prev, _ = _attn_specs(nb, dil, sub)

    def body(q_ref, kp_ref, k_ref, vp_ref, v_ref, o_ref, l_ref, kbuf, vbuf):
        step = pl.program_id(1)
        kbuf[0:QB], kbuf[QB:] = kp_ref[...], k_ref[...]
        vbuf[0:QB], vbuf[QB:] = vp_ref[...], v_ref[...]
        lo = _lane_lo()
        head_lane = lax.broadcasted_iota(jnp.int32, (1, LANES), 1)

        def block(j, carry):
            r0 = pl.multiple_of(j * QB, QB)
            rows, krows = pl.ds(r0, QB), pl.ds(r0, 2 * QB)
            m_prev, m_cur = _masks(step * sub + j > 0)
            mask = jnp.concatenate([m_prev, m_cur], axis=1)
            mask = jnp.concatenate([mask, mask], axis=0)
            lses = jnp.zeros((QB, LANES), F32)
            for i in range(NH // 2):
                sl = slice(2 * HD * i, 2 * HD * (i + 1))
                qs, ks, vv = q_ref[rows, sl], kbuf[krows, sl], vbuf[krows, sl]
                sc = lax.dot_general(_stack_heads(qs, lo), ks, NT, preferred_element_type=F32)
                sc = jnp.where(mask, sc, NEG)
                mx = jnp.max(sc, axis=-1, keepdims=True)
                p = jnp.exp(sc - mx)
                den = jnp.sum(p, axis=-1, keepdims=True)
                o = jnp.dot(p.astype(BF16), vv, preferred_element_type=F32) * (1.0 / den)
                lse = mx + jnp.log(den)
                o_ref[rows, sl] = jnp.where(lo, o[:QB], o[QB:]).astype(BF16)
                lses = jnp.where(head_lane == 2 * i, lse[:QB], jnp.where(head_lane == 2 * i + 1, lse[QB:], lses))
            l_ref[rows, :] = lses
            return carry

        lax.fori_loop(0, sub, block, 0, unroll=True)

    return pl.pallas_call(
        body, name=f"attn_fwd_g{g}", grid=(dil, nb // sub),
        in_specs=[main(q_src[1]), prev(k_src[1]), main(k_src[1]), prev(v_src[1]), main(v_src[1])],
        out_specs=[main(0), main(0, LANES)],
        out_shape=[jax.ShapeDtypeStruct((s, CB), BF16), jax.ShapeDtypeStruct((s, LANES), F32)],
        scratch_shapes=[pltpu.VMEM(((sub + 1) * QB, CB), BF16)] * 2,
        compiler_params=_cp(("parallel", "parallel")))(q_src[0], k_src[0], k_src[0], v_src[0], v_src[0])


def attn_bwd_q(q_src, k_src, v_src, da, lc, dc, g, dil):
    s = q_src[0].shape[0]
    nb = s // dil // QB
    sub = min(MAX_SUB, nb)
    main, prev, _ = _attn_specs(nb, dil, sub)

    def body(q_ref, kp_ref, k_ref, vp_ref, v_ref, da_ref, l_ref, d_ref, dq_ref, kbuf, vbuf):
        step = pl.program_id(1)
        kbuf[0:QB], kbuf[QB:] = kp_ref[...], k_ref[...]
        vbuf[0:QB], vbuf[QB:] = vp_ref[...], v_ref[...]
        lo = _lane_lo()

        def block(j, carry):
            r0 = pl.multiple_of(j * QB, QB)
            rows, krows = pl.ds(r0, QB), pl.ds(r0, 2 * QB)
            m_prev, m_cur = _masks(step * sub + j > 0)
            mask = jnp.concatenate([m_prev, m_cur], axis=1)
            mask = jnp.concatenate([mask, mask], axis=0)
            lcols, dcols = l_ref[rows, :], d_ref[rows, :]
            for i in range(NH // 2):
                sl = slice(2 * HD * i, 2 * HD * (i + 1))
                qs, ks, vv, da2 = q_ref[rows, sl], kbuf[krows, sl], vbuf[krows, sl], da_ref[rows, sl]
                pair = lambda t: jnp.concatenate([t[:, 2 * i:2 * i + 1], t[:, 2 * i + 1:2 * i + 2]], axis=0)
                sc = lax.dot_general(_stack_heads(qs, lo), ks, NT, preferred_element_type=F32)
                sc = jnp.where(mask, sc, NEG)
                p = jnp.exp(sc - pair(lcols))
                dp = lax.dot_general(_stack_heads(da2, lo), vv, NT, preferred_element_type=F32)
                ds = p * (dp - pair(dcols))
                dq = jnp.dot(ds.astype(BF16), ks, preferred_element_type=F32)
                dq_ref[rows, sl] = (jnp.where(lo, dq[:QB], dq[QB:]) * (HD ** -0.5)).astype(BF16)
            return carry

        lax.fori_loop(0, sub, block, 0, unroll=True)

    return pl.pallas_call(
        body, name=f"attn_bwd_q_g{g}", grid=(dil, nb // sub),
        in_specs=[main(q_src[1]), prev(k_src[1]), main(k_src[1]), prev(v_src[1]), main(v_src[1]),
                  main(0), main(0, LANES), main(0, LANES)],
        out_specs=main(0), out_shape=jax.ShapeDtypeStruct((s, CB), BF16),
        scratch_shapes=[pltpu.VMEM(((sub + 1) * QB, CB), BF16)] * 2,
        compiler_params=_cp(("parallel", "parallel")))(
            q_src[0], k_src[0], k_src[0], v_src[0], v_src[0], da, lc, dc)


def attn_bwd_kv(q_src, k_src, v_src, da, lt, dt, g, dil):
    s = q_src[0].shape[0]
    nb = s // dil // QB
    sub = min(MAX_SUB, nb)
    main, _, nxt = _attn_specs(nb, dil, sub)

    def body(k_ref, v_ref, q_ref, qn_ref, da_ref, dan_ref, l_ref, ln_ref, d_ref, dn_ref, dk_ref, dv_ref,
             qbuf, dabuf, lbuf, dbuf):
        step = pl.program_id(1)
        qbuf[0:sub * QB], qbuf[sub * QB:] = q_ref[...], qn_ref[...]
        dabuf[0:sub * QB], dabuf[sub * QB:] = da_ref[...], dan_ref[...]
        for c in range(sub):
            lbuf[c], dbuf[c] = l_ref[:, c * QB:(c + 1) * QB], d_ref[:, c * QB:(c + 1) * QB]
        lbuf[sub], dbuf[sub] = ln_ref[...], dn_ref[...]
        lo = _lane_lo()
        kj = lax.broadcasted_iota(jnp.int32, (QB, QB), 0)
        qi = lax.broadcasted_iota(jnp.int32, (QB, QB), 1)

        def block(j, carry):
            r0 = pl.multiple_of(j * QB, QB)
            rows, qrows = pl.ds(r0, QB), pl.ds(r0, 2 * QB)
            mask = jnp.concatenate([kj <= qi, (kj >= qi) & (step * sub + j < nb - 1)], axis=1)
            mask = jnp.concatenate([mask, mask], axis=1)
            lrow = jnp.concatenate([lbuf[j], lbuf[j + 1]], axis=1)
            drow = jnp.concatenate([dbuf[j], dbuf[j + 1]], axis=1)
            for i in range(NH // 2):
                sl = slice(2 * HD * i, 2 * HD * (i + 1))
                q2, da2 = _stack_heads(qbuf[qrows, sl], lo), _stack_heads(dabuf[qrows, sl], lo)
                ks, vv = k_ref[rows, sl], v_ref[rows, sl]
                pair = lambda t: jnp.concatenate([t[2 * i:2 * i + 1, :], t[2 * i + 1:2 * i + 2, :]], axis=1)
                sc = lax.dot_general(ks, q2, NT, preferred_element_type=F32)
                sc = jnp.where(mask, sc, NEG)
                p = jnp.exp(sc - pair(lrow))
                dp = lax.dot_general(vv, da2, NT, preferred_element_type=F32)
                ds = p * (dp - pair(drow))
                dv_ref[rows, sl] = jnp.dot(p.astype(BF16), da2, preferred_element_type=F32).astype(BF16)
                dk_ref[rows, sl] = jnp.dot(ds.astype(BF16), q2, preferred_element_type=F32).astype(BF16)
            return carry

        lax.fori_loop(0, sub, block, 0, unroll=True)

    steps = nb // sub
    t_main = pl.BlockSpec((NH, sub * QB), lambda r, s: (r, s))
    t_nxt = pl.BlockSpec((NH, QB), lambda r, s: (r, jnp.minimum(sub * (s + 1), nb - 1)))
    out = jax.ShapeDtypeStruct((s, CB), BF16)
    return pl.pallas_call(
        body, name=f"attn_bwd_kv_g{g}", grid=(dil, steps),
        in_specs=[main(k_src[1]), main(v_src[1]), main(q_src[1]), nxt(q_src[1]),
                  main(0), nxt(0), t_main, t_nxt, t_main, t_nxt],
        out_specs=[main(0), main(0)], out_shape=[out, out],
        scratch_shapes=[pltpu.VMEM(((sub + 1) * QB, CB), BF16)] * 2 + [pltpu.VMEM((sub + 1, NH, QB), F32)] * 2,
        compiler_params=_cp(("parallel", "parallel")))(
            k_src[0], v_src[0], q_src[0], q_src[0], da, da, lt, lt, dt, dt)


def _conv_taps(u, u_prev, first):
    tm = u.shape[0]
    row = lax.broadcasted_iota(jnp.int32, (tm, 1), 0)
    up = jnp.where(first, 0.0, u_prev)
    u1 = jnp.where(row == 0, up[HALO - 1:HALO, :], pltpu.roll(u, 1, 0))
    u2 = jnp.where(row == 0, up[HALO - 2:HALO - 1, :],
                   jnp.where(row == 1, up[HALO - 1:HALO, :], pltpu.roll(u, 2, 0)))
    return u1, u2


def mid_fwd(proj, o_g, lse_g, conv_w, expand, tm):
    s = proj.shape[0]
    hb = tm // HALO

    def body(ba_ref, ca_ref, xa_ref, za_ref, cah_ref, xah_ref, zb_ref,
             o0, o1, o2, l0, l1, l2, w_ref, exp_ref, ya_ref, yb_ref, at_ref, lc_ref, buf_o, buf_l):
        first = pl.program_id(0) == 0
        u = ca_ref[...].astype(F32) * xa_ref[...].astype(F32)
        u1, u2 = _conv_taps(u, cah_ref[...].astype(F32) * xah_ref[...].astype(F32), first)
        conv = w_ref[0:1, :] * u2 + w_ref[1:2, :] * u1 + w_ref[2:3, :] * u
        ya_ref[...] = (ba_ref[...].astype(F32) * conv * _silu(za_ref[...].astype(F32))).astype(BF16)
        ls = [_from_residue_major(l, buf_l.at[g], d) for g, (l, d) in enumerate(zip((l0, l1, l2), DILATIONS))]
        mx = jnp.maximum(jnp.maximum(ls[0], ls[1]), ls[2])
        es = [jnp.exp(l - mx) for l in ls]
        den = es[0] + es[1] + es[2]
        attn = jnp.zeros((tm, CB), F32)
        for e, o, d in zip(es, (o0, o1, o2), DILATIONS):
            attn = attn + _dot_hilo(e / den, exp_ref) * _from_residue_major(o, buf_o, d)
        at_ref[...] = attn
        lc_ref[...] = mx + jnp.log(den)
        yb_ref[...] = (attn * _silu(zb_ref[...].astype(F32))).astype(BF16)

    col = lambda j: pl.BlockSpec((tm, D), lambda i: (i, j))
    halo = lambda j: pl.BlockSpec((HALO, D), lambda i: (jnp.maximum(i * hb - 1, 0), j))
    loc = lambda w: pl.BlockSpec((tm, w), lambda i: (i, 0))
    rm = lambda w: [pl.BlockSpec((d, tm // d, w), lambda i: (0, i, 0)) for d in DILATIONS]
    rm_view = lambda ts, w: [t.reshape(d, s // d, w) for t, d in zip(ts, DILATIONS)]
    return pl.pallas_call(
        body, name="mid_fwd", grid=(s // tm,),
        in_specs=[col(0), col(1), col(2), col(3), halo(1), halo(2),
                  pl.BlockSpec((tm, CB), lambda i: (i, CB_ZB))] + rm(CB) + rm(LANES)
                 + [pl.BlockSpec((3, D), lambda i: (0, 0)), pl.BlockSpec(expand.shape, lambda i: (0, 0))],
        out_specs=[loc(D), loc(CB), loc(CB), loc(LANES)],
        out_shape=[jax.ShapeDtypeStruct((s, D), BF16), jax.ShapeDtypeStruct((s, CB), BF16),
                   jax.ShapeDtypeStruct((s, CB), F32), jax.ShapeDtypeStruct((s, LANES), F32)],
        scratch_shapes=[pltpu.VMEM((CB // LANES, tm, LANES), F32), pltpu.VMEM((3, 1, tm, LANES), F32)],
        compiler_params=_cp(("parallel",)))(
            proj, proj, proj, proj, proj, proj, proj, *rm_view(o_g, CB), *rm_view(lse_g, LANES), conv_w, expand)


def tail(proj, ya, yb, attn, x, target, gate, pa_w, pb_w, wo_w, total, conv_w, tm):
    s = proj.shape[0]
    ni = s // tm
    hb = tm // HALO
    nlate = NIN - CB_ZB * CB
    nearly = 4 * D

    def body(ya_ref, yb_ref, ga_ref, gb_ref, zb_ref, at_ref, x_ref, t_ref, gate_ref, pa_ref, pb_ref, wo_ref,
             tot_ref, ba_ref, ca_ref, xa_ref, za_ref, cah_ref, xah_ref, cw_ref,
             dp_hbm, dy_ref, da_ref, dc_ref, mg_ref, do_ref, dpa_ref, dpb_ref, st_ref, gwc_ref,
             stage, dconv_next, sems):
        step = pl.program_id(0)
        i = ni - 1 - step
        slot = step % 2

        def slabs(at_step, sl):
            rows = pl.ds(pl.multiple_of((ni - 1 - at_step) * tm, tm), tm)
            return (pltpu.make_async_copy(stage.at[sl, :, 0:nearly], dp_hbm.at[rows, pl.ds(0, nearly)],
                                          sems.at[sl, 0]),
                    pltpu.make_async_copy(stage.at[sl, :, nearly:], dp_hbm.at[rows, pl.ds(CB_ZB * CB, nlate)],
                                          sems.at[sl, 1]))

        @pl.when(step == 0)
        def _():
            st_ref[...] = jnp.zeros_like(st_ref)
            gwc_ref[...] = jnp.zeros_like(gwc_ref)
            dconv_next[...] = jnp.zeros_like(dconv_next)

        @pl.when(step >= 2)
        def _():
            for cp in slabs(step - 2, slot):
                cp.wait()

        gate_v = gate_ref[...]
        pa = jnp.dot(ya_ref[...], pa_ref[...], preferred_element_type=F32)
        pb = jnp.dot(yb_ref[...], pb_ref[...], preferred_element_type=F32)
        sa = jax.nn.sigmoid(ga_ref[...].astype(F32))
        sb = jax.nn.sigmoid(gb_ref[...].astype(F32))
        merged = (sa * pa + sb * pb).astype(BF16)
        mg_ref[...] = merged
        out = jnp.dot(merged, wo_ref[...], preferred_element_type=F32)
        err = x_ref[...] + gate_v * out - t_ref[...]
        dy = err * (1.0 / D)
        dy_ref[...] = dy
        st_ref[0:1, :] += jnp.sum(dy * out, axis=0, keepdims=True)
        st_ref[1:2, :] += jnp.sum(err * err, axis=0, keepdims=True)
        dout = (gate_v * dy).astype(BF16)
        do_ref[...] = dout
        dmg = lax.dot_general(dout, wo_ref[...], NT, preferred_element_type=F32)
        dpa = (dmg * sa).astype(BF16)
        dpb = (dmg * sb).astype(BF16)
        dpa_ref[...] = dpa
        dpb_ref[...] = dpb
        late = nearly
        stage[slot, :, late + CB:late + CB + D] = (dmg * pa * sa * (1.0 - sa)).astype(BF16)
        stage[slot, :, late + CB + D:] = (dmg * pb * sb * (1.0 - sb)).astype(BF16)
        dya = lax.dot_general(dpa, pa_ref[...], NT, preferred_element_type=F32)
        dyb = lax.dot_general(dpb, pb_ref[...], NT, preferred_element_type=F32)
        zb = zb_ref[...].astype(F32)
        sg = jax.nn.sigmoid(zb)
        attn_v = at_ref[...]
        dattn = dyb * (zb * sg)
        da_ref[...] = dattn.astype(BF16)
        stage[slot, :, late:late + CB] = (dyb * attn_v * (sg * (1.0 + zb * (1.0 - sg)))).astype(BF16)
        dc_ref[...] = _dot_hilo(dattn * attn_v, tot_ref)

        ba, ca, xa, za = (t[...].astype(F32) for t in (ba_ref, ca_ref, xa_ref, za_ref))
        u = ca * xa
        u1, u2 = _conv_taps(u, cah_ref[...].astype(F32) * xah_ref[...].astype(F32), i == 0)
        w0, w1, w2 = cw_ref[0:1, :], cw_ref[1:2, :], cw_ref[2:3, :]
        conv = w0 * u2 + w1 * u1 + w2 * u
        sga = jax.nn.sigmoid(za)
        sza = za * sga
        dconv = dya * ba * sza
        dcn = dconv_next[...]
        rowi = lax.broadcasted_iota(jnp.int32, (tm, 1), 0)
        d1 = jnp.where(rowi == tm - 1, dcn[0:1, :], pltpu.roll(dconv, tm - 1, 0))
        d2 = jnp.where(rowi == tm - 2, dcn[0:1, :],
                       jnp.where(rowi == tm - 1, dcn[1:2, :], pltpu.roll(dconv, tm - 2, 0)))
        du = w2 * dconv + w1 * d1 + w0 * d2
        stage[slot, :, 0:D] = (dya * conv * sza).astype(BF16)
        stage[slot, :, D:2 * D] = (du * xa).astype(BF16)
        stage[slot, :, 2 * D:3 * D] = (du * ca).astype(BF16)
        stage[slot, :, 3 * D:4 * D] = (dya * ba * conv * (sga * (1.0 + za * (1.0 - sga)))).astype(BF16)
        gwc_ref[0:1, :] += jnp.sum(dconv * u2, axis=0, keepdims=True)
        gwc_ref[1:2, :] += jnp.sum(dconv * u1, axis=0, keepdims=True)
        gwc_ref[2:3, :] += jnp.sum(dconv * u, axis=0, keepdims=True)
        dconv_next[...] = dconv[0:8, :]

        for cp in slabs(step, slot):
            cp.start()

        @pl.when(step == ni - 1)
        def _():
            for cp in slabs(step - 1, 1 - slot) + slabs(step, slot):
                cp.wait()

    rev = lambda st: ni - 1 - st
    row = lambda w: pl.BlockSpec((tm, w), lambda st: (rev(st), 0))
    pcol = lambda w, jb: pl.BlockSpec((tm, w), lambda st: (rev(st), jb))
    halo = lambda jb: pl.BlockSpec((HALO, D), lambda st: (jnp.maximum(rev(st) * hb - 1, 0), jb))
    const = lambda a: pl.BlockSpec(a.shape, lambda st: (0, 0), pipeline_mode=pl.Buffered(1))
    acc = pl.BlockSpec((8, D), lambda st: (0, 0))
    return pl.pallas_call(
        body, name="tail", grid=(ni,),
        in_specs=[row(D), row(CB), pcol(D, 9), pcol(D, 10), pcol(CB, CB_ZB), row(CB), row(D), row(D),
                  pl.BlockSpec((1, D), lambda st: (0, 0)), const(pa_w), const(pb_w), const(wo_w), const(total),
                  pcol(D, 0), pcol(D, 1), pcol(D, 2), pcol(D, 3), halo(1), halo(2),
                  pl.BlockSpec((3, D), lambda st: (0, 0))],
        out_specs=[pl.BlockSpec(memory_space=pl.ANY),
                   row(D), row(CB), row(LANES), row(D), row(D), row(D), row(D), acc, acc],
        out_shape=[jax.ShapeDtypeStruct((s, NIN), BF16), jax.ShapeDtypeStruct((s, D), F32),
                   jax.ShapeDtypeStruct((s, CB), BF16), jax.ShapeDtypeStruct((s, LANES), F32)]
                  + [jax.ShapeDtypeStruct((s, D), BF16)] * 4 + [jax.ShapeDtypeStruct((8, D), F32)] * 2,
        scratch_shapes=[pltpu.VMEM((2, tm, nearly + nlate), BF16), pltpu.VMEM((8, D), F32),
                        pltpu.SemaphoreType.DMA((2, 2))],
        compiler_params=_cp(("arbitrary",), 60))(
            ya, yb, proj, proj, proj, attn, x, target, gate, pa_w, pb_w, wo_w, total,
            proj, proj, proj, proj, proj, proj, conv_w)


def _local_step(x, target, shift, scale, gate, norm_w, conv_w, qw, kw, w_shard, small_shards, me_xyc):
    qw8, kw8 = jnp.tile(qw, (1, NH)), jnp.tile(kw, (1, NH))
    same, total, expand = _head_matrices()
    proj, ht, wg, (pa_g, pb_g, wo_g) = proj_fwd_gather(
        x, norm_w, scale, shift, w_shard, small_shards, gather_order(me_xyc), 1024)
    pa_w, wo_w = pa_g.reshape(D, D), wo_g.reshape(D, D)
    pb_w = pb_g.transpose(1, 0, 2).reshape(CB, D)
    srcs = qkv_prep(proj, qw8, kw8, same, 512)
    o_g, lse_g = zip(*[attn_fwd(*srcs[g], g, d) for g, d in enumerate(DILATIONS)])
    ya, yb, attn, lc = mid_fwd(proj, o_g, lse_g, conv_w, expand, 512)
    dproj, dy, da, dc, merged, dout, dpa, dpb, st_tail, st_conv = tail(
        proj, ya, yb, attn, x, target, gate, pa_w, pb_w, wo_w, total, conv_w, 256)
    g_wo = matmul_tn(merged, dout, "grad_w_out", 1024)
    g_pa = matmul_tn(ya, dpa, "grad_w_br_conv", 1024)
    g_pb = matmul_tn(yb, dpb, "grad_w_br_attn", 1024)
    grads = []
    for g, d in enumerate(DILATIONS):
        da_p, lc_p, dc_p, lt, dt = stats_prep(da, lc, dc, g, d, 2048)
        dq = attn_bwd_q(*srcs[g], da_p, lc_p, dc_p, g, d)
        dk, dv = attn_bwd_kv(*srcs[g], da_p, lt, dt, g, d)
        grads.append((dq, dk, dv))
    dproj, gw_qk = qkv_grads_to_dproj(dproj, proj, grads, qw8, kw8, same, 512)
    slabs = [g_pa.reshape(NDEV, 128, D), g_pb.reshape(CB, NDEV, 128).transpose(1, 0, 2), g_wo.reshape(NDEV, 128, D)]
    grad_x, st_norm, r_win, (r_pa, r_pb, r_wo) = proj_bwd(
        ht, dproj, wg, slabs, scatter_order(me_xyc), x, dy, norm_w, scale, 1024)
    dmod = jnp.concatenate([st_norm[0:1], st_norm[1:2], st_tail[0:1]], axis=1)
    loss_part = (0.5 / D) * jnp.sum(st_tail[1])
    gw_heads = gw_qk[0:2].reshape(2, NH, HD).sum(axis=1)
    small = dict(dmod=dmod, norm_w=st_norm[2:3], conv_w=st_conv[0:3],
                 q_norm_w=gw_heads[0:1], k_norm_w=gw_heads[1:2], loss=loss_part)
    return grad_x, small, (r_win, r_pa, r_pb, r_wo)


def kernel(x, c, w_ada, b_ada, norm_w, w_in, conv_w, q_norm_w, k_norm_w, w_br_conv, w_br_attn, w_out, loss_target, m_w_ada, m_b_ada, m_norm_w, m_w_in, m_conv_w, m_q_norm_w, m_k_norm_w, m_w_br_conv, m_w_br_attn, m_w_out, v_w_ada, v_b_ada, v_norm_w, v_w_in, v_conv_w, v_q_norm_w, v_k_norm_w, v_w_br_conv, v_w_br_attn, v_w_out):
    me_xyc = (lax.axis_index("x"), lax.axis_index("y"), lax.axis_index("c"))
    me = _dev_index(me_xyc)
    ncol = w_ada.shape[2]

    conv_pad = jnp.zeros((8, 128), F32).at[0:3].set(conv_w[0])
    c_all, conv_all = all_gather([c, conv_pad], "gather_cond")
    conv_full = conv_all[:, 0:3].transpose(1, 0, 2).reshape(3, D)
    c_all = c_all.reshape(NDEV, D)

    b_cols = lax.dynamic_slice(b_ada, (0, me * ncol), (1, ncol))
    mod_cols = ada_fwd(c_all, w_ada[0], b_cols)
    (mod_all,) = all_gather([mod_cols], "gather_mod")
    mod = lax.dynamic_index_in_dim(mod_all, me, axis=1, keepdims=False).reshape(1, 3 * D)
    shift, scale, gate = mod[:, 0:D], mod[:, D:2 * D], mod[:, 2 * D:3 * D]

    grad_x, small, (r_win, r_pa, r_pb, r_wo) = _local_step(
        x[0], loss_target[0], shift, scale, gate, norm_w, conv_full, q_norm_w, k_norm_w,
        w_in[0].astype(BF16), [w_br_conv[0].astype(BF16), w_br_attn[0].astype(BF16), w_out[0].astype(BF16)], me_xyc)

    packed = jnp.concatenate(
        [small["dmod"], small["norm_w"], small["conv_w"].reshape(1, 3 * D), small["q_norm_w"], small["k_norm_w"],
         jnp.full((1, 128), small["loss"], F32)], axis=1)
    (packed_all,) = all_gather([packed], "gather_small")
    tot = sum_parts(packed_all)
    loss = tot[0, 7 * D + 2 * HD]
    dmod_all = packed_all[:, 0, 0:3 * D]
    g_b_ada = tot[:, 0:3 * D]
    g_norm_w = tot[:, 3 * D:4 * D]
    g_conv = lax.dynamic_slice(tot[:, 4 * D:7 * D].reshape(3, D), (0, me * 128), (3, 128))
    g_qn = tot[:, 7 * D:7 * D + HD]
    g_kn = tot[:, 7 * D + HD:7 * D + 2 * HD]
    g_w_ada = ada_bwd(c_all.T, lax.dynamic_slice(dmod_all, (0, me * ncol), (NDEV, ncol)))

    def upd(parts, w, m, v, name, rows):
        shape = w.shape
        w2, m2, v2 = (t.reshape(shape[-2:]) for t in (w, m, v))
        return [t.reshape(shape) for t in adamw(parts, w2, m2, v2, name, rows)]

    res = {
        "w_in": upd(r_win, w_in, m_w_in, v_w_in, "adamw_w_in", 128),
        "w_br_conv": upd(r_pa, w_br_conv, m_w_br_conv, v_w_br_conv, "adamw_w_br_conv", 128),
        "w_br_attn": upd(r_pb, w_br_attn, m_w_br_attn, v_w_br_attn, "adamw_w_br_attn", 512),
        "w_out": upd(r_wo, w_out, m_w_out, v_w_out, "adamw_w_out", 128),
    }
    small_params = {"w_ada": (g_w_ada, w_ada, m_w_ada, v_w_ada), "b_ada": (g_b_ada, b_ada, m_b_ada, v_b_ada),
                    "norm_w": (g_norm_w, norm_w, m_norm_w, v_norm_w), "conv_w": (g_conv, conv_w, m_conv_w, v_conv_w),
                    "q_norm_w": (g_qn, q_norm_w, m_q_norm_w, v_q_norm_w),
                    "k_norm_w": (g_kn, k_norm_w, m_k_norm_w, v_k_norm_w)}
    updated = adamw_small([tuple(t.reshape(t.shape[-2:]) for t in item) for item in small_params.values()])
    for (pname, item), outs4 in zip(small_params.items(), updated):
        res[pname] = [t.reshape(item[1].shape) for t in outs4]
    names = ["w_ada", "b_ada", "norm_w", "w_in", "conv_w", "q_norm_w", "k_norm_w", "w_br_conv", "w_br_attn", "w_out"]
    return (loss, grad_x[None], *[res[n][0] for n in names], *[res[n][1] for n in names],
            *[res[n][2] for n in names], *[res[n][3] for n in names])
```

```python
import jax
import jax.numpy as jnp
from jax import lax
from jax.experimental import pallas as pl
from jax.experimental.pallas import tpu as pltpu

F32, BF16 = jnp.float32, jnp.bfloat16
D = 1024
NIN = 11264
NDEV = 8
SHARD = NIN // NDEV
HD = 64
NH = 8
QB = 128
CB = 512
CB_Q, CB_K, CB_V, CB_ZB = 8, 11, 14, 17
DILATIONS = (1, 4, 16)
EPS = 1e-6
NEG = -1e30
HALO = 16
LANES = 128
MESH = pl.DeviceIdType.MESH

ADAM_LR, ADAM_B1, ADAM_B2, ADAM_EPS, ADAM_WD, ADAM_STEP = 0.001, 0.9, 0.999, 1e-08, 0.01, 10

NT = (((1,), (1,)), ((), ()))
TN = (((0,), (0,)), ((), ()))


def _cp(sem, vmem_mb=48):
    return pltpu.CompilerParams(dimension_semantics=sem, vmem_limit_bytes=vmem_mb << 20)


def _silu(z):
    return z * jax.nn.sigmoid(z)


def _coords():
    return lax.axis_index("x"), lax.axis_index("y"), lax.axis_index("c")


FLIPS = [(fx, fy, fc) for fx in (0, 1) for fy in (0, 1) for fc in (0, 1)][1:]


def all_gather(arrs, name):
    n = len(arrs)

    def body(*refs):
        ins, outs = refs[:n], refs[n:2 * n]
        send_sems, recv_sems, local_sems = refs[2 * n:]
        me_xyc = _coords()
        me = _dev_index(me_xyc)
        peers = [_flip(me_xyc, f) for f in FLIPS]

        def copy(a, k, block):
            return pltpu.make_async_remote_copy(
                src_ref=ins[a], dst_ref=outs[a].at[block], send_sem=send_sems.at[a, k], recv_sem=recv_sems.at[a, k],
                device_id=peers[k], device_id_type=MESH)

        mine = [pltpu.make_async_copy(ins[a], outs[a].at[me], local_sems.at[a]) for a in range(n)]
        sends = [copy(a, k, me) for k in range(7) for a in range(n)]
        for cp in mine + sends:
            cp.start()
        for k in range(7):
            for a in range(n):
                copy(a, k, _dev_index(peers[k])).wait_recv()
        for cp in sends:
            cp.wait_send()
        for cp in mine:
            cp.wait()

    any_spec = pl.BlockSpec(memory_space=pl.ANY)
    return pl.pallas_call(
        body, name=name,
        out_shape=[jax.ShapeDtypeStruct((NDEV,) + a.shape, a.dtype) for a in arrs],
        in_specs=[any_spec] * n, out_specs=[any_spec] * n,
        scratch_shapes=[pltpu.SemaphoreType.DMA((n, 7)), pltpu.SemaphoreType.DMA((n, 7)),
                        pltpu.SemaphoreType.DMA((n,))],
    )(*arrs)


def _flip(dev, f):
    return tuple(1 - v if b else v for v, b in zip(dev, f))


def _dev_index(dev):
    return 4 * dev[0] + 2 * dev[1] + dev[2]


def _chip_order(x, y, c):
    xor = lambda a, b: a + b - 2 * a * b
    return [(xor(x, 1 - c), xor(y, c)), (xor(x, c), xor(y, 1 - c)), (1 - x, 1 - y)]


def gather_order(me_xyc):
    x, y, c = me_xyc
    chips = _chip_order(x, y, c)
    devs = [(x, y, c), (x, y, 1 - c), (*chips[0], c), (*chips[1], c),
            (*chips[1], 1 - c), (*chips[0], 1 - c), (*chips[2], c), (*chips[2], 1 - c)]
    return jnp.stack([_dev_index(d) for d in devs]).astype(jnp.int32)


def scatter_order(me_xyc):
    devs = [_flip(me_xyc, f) for f in FLIPS] + [me_xyc]
    return jnp.stack([_dev_index(d) for d in devs]).astype(jnp.int32)


def ada_fwd(c, conv_pad, w_ada, b_cols):
    ncol = w_ada.shape[1]

    def body(c_ref, cv_ref, w_ref, b_ref, mod_ref, call_ref, cvall_ref, rows_buf, send_sems, recv_sems, local_sems):
        me_xyc = _coords()
        me = _dev_index(me_xyc)
        peers = [_flip(me_xyc, f) for f in FLIPS]
        pids = [_dev_index(p) for p in peers]

        def copy(a, k, src, dst):
            return pltpu.make_async_remote_copy(src_ref=src, dst_ref=dst, send_sem=send_sems.at[a, k],
                                                recv_sem=recv_sems.at[a, k], device_id=peers[k], device_id_type=MESH)

        own = [pltpu.make_async_copy(c_ref, call_ref.at[me], local_sems.at[0]),
               pltpu.make_async_copy(cv_ref, cvall_ref.at[me], local_sems.at[1])]
        first = [copy(0, k, c_ref, call_ref.at[me]) for k in range(7)]
        first += [copy(1, k, cv_ref, cvall_ref.at[me]) for k in range(7)]
        for cp in own + first:
            cp.start()
        own[0].wait()
        for k in range(7):
            copy(0, k, c_ref, call_ref.at[pids[k]]).wait_recv()
        seq = lax.broadcasted_iota(jnp.int32, (NDEV, 1), 0)
        c_all = jnp.zeros((NDEV, D), F32)
        for p in range(NDEV):
            c_all = jnp.where(seq == p, call_ref[p], c_all)
        mods = jnp.dot(_silu(c_all).astype(BF16), w_ref[...].astype(BF16), preferred_element_type=F32) + b_ref[...]
        for p in range(NDEV):
            rows_buf[p] = mods[p:p + 1, :]
        mine = pltpu.make_async_copy(rows_buf.at[me], mod_ref.at[me], local_sems.at[2])
        second = [copy(2, k, rows_buf.at[pids[k]], mod_ref.at[me]) for k in range(7)]
        for cp in [mine] + second:
            cp.start()
        for k in range(7):
            copy(2, k, rows_buf.at[pids[k]], mod_ref.at[pids[k]]).wait_recv()
            copy(1, k, cv_ref, cvall_ref.at[pids[k]]).wait_recv()
        for cp in first + second:
            cp.wait_send()
        own[1].wait()
        mine.wait()

    return pl.pallas_call(
        body, name="ada_fwd",
        out_shape=[jax.ShapeDtypeStruct((NDEV, 1, ncol), F32), jax.ShapeDtypeStruct((NDEV, 1, D), F32),
                   jax.ShapeDtypeStruct((NDEV,) + conv_pad.shape, F32)],
        scratch_shapes=[pltpu.VMEM((NDEV, 1, ncol), F32), pltpu.SemaphoreType.DMA((3, 7)),
                        pltpu.SemaphoreType.DMA((3, 7)), pltpu.SemaphoreType.DMA((3,))],
    )(c, conv_pad, w_ada, b_cols)


def ada_bwd(c_all_t, dmod_cols):
    def body(c_ref, d_ref, o_ref):
        at = _silu(c_ref[...])
        acc = at[:, 0:1] * d_ref[0:1, :]
        for b in range(1, NDEV):
            acc = acc + at[:, b:b + 1] * d_ref[b:b + 1, :]
        o_ref[...] = acc

    return pl.pallas_call(body, name="ada_bwd",
                          out_shape=jax.ShapeDtypeStruct((D, dmod_cols.shape[1]), F32))(c_all_t, dmod_cols)


def sum_parts(parts):
    def body(p_ref, o_ref):
        acc = p_ref[0]
        for b in range(1, NDEV):
            acc = acc + p_ref[b]
        o_ref[...] = acc

    return pl.pallas_call(body, name="sum_parts",
                          out_shape=jax.ShapeDtypeStruct(parts.shape[1:], F32))(parts)


def _adamw_update(g, w_ref, m_ref, v_ref, g_ref, d_ref, nm_ref, nv_ref):
    nm = ADAM_B1 * m_ref[...] + (1.0 - ADAM_B1) * g
    nv = ADAM_B2 * v_ref[...] + (1.0 - ADAM_B2) * (g * g)
    g_ref[...] = g
    nm_ref[...] = nm
    nv_ref[...] = nv
    m_hat = nm / (1.0 - ADAM_B1 ** ADAM_STEP)
    v_hat = nv / (1.0 - ADAM_B2 ** ADAM_STEP)
    d_ref[...] = -ADAM_LR * (m_hat / (jnp.sqrt(v_hat) + ADAM_EPS) + ADAM_WD * w_ref[...])


def adamw_small(items):
    n = len(items)

    def body(*refs):
        ins, outs = refs[:4 * n], refs[4 * n:]
        for a in range(n):
            g_in, w_ref, m_ref, v_ref = ins[4 * a:4 * a + 4]
            _adamw_update(g_in[...], w_ref, m_ref, v_ref, *outs[4 * a:4 * a + 4])

    out = pl.pallas_call(
        body, name="adamw_small",
        out_shape=[jax.ShapeDtypeStruct(it[1].shape, F32) for it in items for _ in range(4)],
        compiler_params=pltpu.CompilerParams(vmem_limit_bytes=48 << 20))(*[t for it in items for t in it])
    return [out[4 * a:4 * a + 4] for a in range(n)]


def adamw(parts, w, m, v, name, rows):
    n, r, ccols = parts.shape

    def body(p_ref, w_ref, m_ref, v_ref, g_ref, d_ref, nm_ref, nv_ref):
        g = p_ref[0].astype(F32)
        for b in range(1, n):
            g = g + p_ref[b].astype(F32)
        _adamw_update(g, w_ref, m_ref, v_ref, g_ref, d_ref, nm_ref, nv_ref)

    blk = pl.BlockSpec((rows, ccols), lambda i: (i, 0))
    out = jax.ShapeDtypeStruct((r, ccols), F32)
    return pl.pallas_call(
        body, name=name, grid=(r // rows,),
        in_specs=[pl.BlockSpec((n, rows, ccols), lambda i: (0, i, 0)), blk, blk, blk],
        out_specs=[blk] * 4, out_shape=[out] * 4, compiler_params=_cp(("parallel",)))(parts, w, m, v)


def proj_fwd_gather(x, nw, scale, shift, w_shard, extras, order, tm):
    s = x.shape[0]
    ni = s // tm
    n = 1 + len(extras)
    mid = ni - 2

    def body(order_ref, x_ref, nw_ref, sc_ref, sh_ref, *refs):
        ins, o_ref, ht_ref, outs = refs[:n], refs[n], refs[n + 1], refs[n + 2:2 * n + 2]
        h_all, wbuf, send_sems, recv_sems, local_sems, load_sems = refs[2 * n + 2:]
        jj, i = pl.program_id(0), pl.program_id(1)
        x, y, c = _coords()
        me, sibling = (x, y, c), (x, y, 1 - c)
        chips = _chip_order(x, y, c)
        relayed = [(*chips[1], 1 - c), (*chips[0], 1 - c), (*chips[2], 1 - c)]

        def slot(a, dev):
            return outs[a].at[_dev_index(dev)]

        def copy(a, k, block, to, src=None):
            return pltpu.make_async_remote_copy(
                src_ref=slot(a, block) if src is None else src, dst_ref=slot(a, block),
                send_sem=send_sems.at[a, k], recv_sem=recv_sems.at[a, k], device_id=to, device_id_type=MESH)

        mine = [pltpu.make_async_copy(ins[a], slot(a, me), local_sems.at[a]) for a in range(n)]
        to_sibling = [copy(a, 0, me, sibling, src=ins[a]) for a in range(n)]
        to_chip = [[copy(a, 1 + j, me, (*chips[j], c), src=ins[a]) for a in range(n)] for j in range(2)]
        onward = [copy(a, 3, (*chips[1], c), (*chips[0], c)) for a in range(n)]
        passed = [[copy(a, 4 + j, (*ch, c), sibling) for a in range(n)] for j, ch in enumerate(chips)]
        sends = lambda a: [to_sibling[a], to_chip[0][a], to_chip[1][a], onward[a]] + [passed[j][a] for j in range(3)]

        def arrived(a, j):
            copy(a, 1 + j, (*chips[j], c), me).wait_recv()

        def load(row):
            return pltpu.make_async_copy(outs[0].at[order_ref[row]], wbuf.at[row % 2], load_sems.at[row % 2])

        @pl.when((jj == 0) & (i == 0))
        def _():
            for cp in mine:
                cp.start()
            to_sibling[0].start()
            to_chip[0][0].start()
            pltpu.make_async_copy(ins[0], wbuf.at[0], load_sems.at[0]).start()

        @pl.when((jj == 1) & (i == 0))
        def _():
            to_chip[1][0].start()

        @pl.when((jj == 4) & (i == 0))
        def _():
            for a in range(1, n):
                to_sibling[a].start()
                to_chip[0][a].start()
                to_chip[1][a].start()

        direct = {2: 0, 3: 1, 6: 2}
        relay = {4: 0, 5: 1, 7: 2}

        @pl.when((jj == 0) & (i == mid))
        def _():
            copy(0, 0, sibling, me).wait_recv()

        for row, j in direct.items():
            @pl.when((jj == row - 1) & (i == mid))
            def _(j=j):
                arrived(0, j)
                passed[j][0].start()
                if j == 1:
                    onward[0].start()

        for row, j in relay.items():
            @pl.when((jj == row - 1) & (i == mid))
            def _(j=j):
                copy(0, 4 + j, relayed[j], me).wait_recv()

        @pl.when((jj == NDEV - 1) & (i == 0))
        def _():
            for a in range(1, n):
                arrived(a, 1)
                onward[a].start()
                passed[1][a].start()
                arrived(a, 0)
                passed[0][a].start()

        @pl.when((jj < NDEV - 1) & (i == mid))
        def _():
            load(jj + 1).start()

        @pl.when(i == 0)
        def _():
            load(jj).wait()

        @pl.when(jj == 0)
        def _():
            xf = x_ref[...]
            r = lax.rsqrt(jnp.mean(xf * xf, axis=-1, keepdims=True) + EPS)
            h = (xf * r * nw_ref[...]) * (1.0 + sc_ref[...]) + sh_ref[...]
            h_all[i] = h.astype(BF16)
            ht_ref[...] = h.T.astype(BF16)

        o_ref[...] = jnp.dot(h_all[i], wbuf[jj % 2], preferred_element_type=F32).astype(BF16)

        @pl.when((jj == NDEV - 1) & (i == ni - 1))
        def _():
            for a in range(1, n):
                arrived(a, 2)
                passed[2][a].start()
            for a in range(1, n):
                copy(a, 0, sibling, me).wait_recv()
                for j in range(3):
                    copy(a, 4 + j, relayed[j], me).wait_recv()
            for a in range(n):
                mine[a].wait()
                for cp in sends(a):
                    cp.wait_send()

    any_spec = pl.BlockSpec(memory_space=pl.ANY)
    vec = pl.BlockSpec((1, D), lambda jj, i, o: (0, 0))
    outs = pl.pallas_call(
        body, name="proj_fwd_gather",
        grid_spec=pltpu.PrefetchScalarGridSpec(
            num_scalar_prefetch=1, grid=(NDEV, ni),
            in_specs=[pl.BlockSpec((tm, D), lambda jj, i, o: (jnp.where(jj == 0, i, ni - 1), 0))] + [vec] * 3
                     + [any_spec] * n,
            out_specs=[pl.BlockSpec((tm, SHARD), lambda jj, i, o: (i, o[jj])),
                       pl.BlockSpec((D, tm), lambda jj, i, o: (0, jnp.where(jj == 0, i, ni - 1)))]
                      + [any_spec] * n,
            scratch_shapes=[pltpu.VMEM((ni, tm, D), BF16), pltpu.VMEM((2, D, SHARD), BF16),
                            pltpu.SemaphoreType.DMA((n, 7)), pltpu.SemaphoreType.DMA((n, 7)),
                            pltpu.SemaphoreType.DMA((n,)), pltpu.SemaphoreType.DMA((2,))]),
        out_shape=[jax.ShapeDtypeStruct((s, NIN), BF16), jax.ShapeDtypeStruct((D, s), BF16),
                   jax.ShapeDtypeStruct((NDEV, D, SHARD), BF16)]
                  + [jax.ShapeDtypeStruct((NDEV,) + e.shape, e.dtype) for e in extras],
        compiler_params=_cp(("arbitrary", "arbitrary"), 56))(order, x, nw, scale, shift, w_shard, *extras)
    return outs[0], outs[1], outs[2], outs[3:]


def proj_bwd(ht, dproj, wg, smalls, order, x, dy, nw, scale, tt):
    s = dproj.shape[0]
    nk = s // tt
    n = len(smalls)
    rows_per_step = tt // nk
    last = 2 * NDEV

    def body(order_ref, ht_ref, dp_ref, w_ref, x_ref, dy_ref, nw_ref, sc_ref, *rest):
        small_in = rest[:n]
        gx_ref, st_ref, gw_ref, rwin_ref = rest[n:n + 4]
        small_out = rest[n + 4:2 * n + 4]
        acc, stage, dh, send_sems, recv_sems, local_sems, stage_sems = rest[2 * n + 4:]
        t, k = pl.program_id(0), pl.program_id(1)
        me_xyc = _coords()
        me = _dev_index(me_xyc)
        peers = [_flip(me_xyc, f) for f in FLIPS]

        def exchange(a, kf, src_arr, dst_arr):
            pid = _dev_index(peers[kf])
            mk = lambda dst: pltpu.make_async_remote_copy(
                src_ref=src_arr.at[pid], dst_ref=dst, send_sem=send_sems.at[a, kf], recv_sem=recv_sems.at[a, kf],
                device_id=peers[kf], device_id_type=MESH)
            return mk(dst_arr.at[me]), mk(dst_arr.at[pid])

        small_pairs = [exchange(1 + a, kf, small_in[a], small_out[a]) for kf in range(7) for a in range(n)]
        small_own = [pltpu.make_async_copy(small_in[a].at[me], small_out[a].at[me], local_sems.at[1 + a])
                     for a in range(n)]
        win_pairs = [exchange(0, kf, gw_ref, rwin_ref) for kf in range(7)]
        win_own = pltpu.make_async_copy(gw_ref.at[me], rwin_ref.at[me], local_sems.at[0])

        def to_hbm(jj):
            slab = me if jj == 7 else _dev_index(peers[jj])
            return pltpu.make_async_copy(stage.at[jj % 2], gw_ref.at[slab], stage_sems.at[jj % 2])

        @pl.when((t == 0) & (k == 0))
        def _():
            for cp in small_own:
                cp.start()
            for send, _ in small_pairs:
                send.start()

        @pl.when(t < NDEV)
        def _():
            p = jnp.dot(ht_ref[...], dp_ref[...], preferred_element_type=F32)

            @pl.when(k == 0)
            def _():
                acc[...] = p

            @pl.when(k > 0)
            def _():
                acc[...] += p

        for jj in range(NDEV):
            @pl.when((t == jj) & (k == nk - 1))
            def _(jj=jj):
                stage[jj % 2] = acc[...].astype(BF16)
                to_hbm(jj).start()

            @pl.when((t == jj + 1) & (k == 1))
            def _(jj=jj):
                to_hbm(jj).wait()
                if jj < 7:
                    win_pairs[jj][0].start()
                else:
                    win_own.start()

        def matmul_step():
            p = lax.dot_general(dp_ref[...], w_ref[...], NT, preferred_element_type=F32)
            slot = t % 2
            dh[slot] = jnp.where(k == 0, p, dh[slot] + p)

        def norm_step():
            g = dh.at[(t + 1) % 2][pl.ds(pl.multiple_of(k * rows_per_step, rows_per_step), rows_per_step), :]
            xf = x_ref[...]
            r = lax.rsqrt(jnp.mean(xf * xf, axis=-1, keepdims=True) + EPS)
            xh = xf * r
            dn = g * (1.0 + sc_ref[...])
            dxh = dn * nw_ref[...]
            gx_ref[...] = dy_ref[...] + r * (dxh - xh * jnp.mean(dxh * xh, axis=-1, keepdims=True))
            st_ref[0:1, :] += jnp.sum(g, axis=0, keepdims=True)
            st_ref[1:2, :] += jnp.sum(g * xh * nw_ref[...], axis=0, keepdims=True)
            st_ref[2:3, :] += jnp.sum(dn * xh, axis=0, keepdims=True)

        @pl.when((t == 0) & (k == 0))
        def _():
            st_ref[...] = jnp.zeros_like(st_ref)

        @pl.when(t == NDEV)
        def _():
            matmul_step()

        @pl.when((t > NDEV) & (t < last))
        def _():
            matmul_step()
            norm_step()

        @pl.when(t == last)
        def _():
            norm_step()

        @pl.when((t == last) & (k == nk - 1))
        def _():
            for _, recv in win_pairs + small_pairs:
                recv.wait_recv()
            for send, _ in win_pairs + small_pairs:
                send.wait_send()
            win_own.wait()
            for cp in small_own:
                cp.wait()

    any_spec = pl.BlockSpec(memory_space=pl.ANY)
    first = lambda t: t < NDEV
    slab = lambda t, k: jnp.where(t == last, NDEV - 1, k)
    chunk = pl.BlockSpec((rows_per_step, D), lambda t, k, o: (jnp.maximum((t - NDEV - 1) * nk + k, 0), 0))
    vec = pl.BlockSpec((1, D), lambda t, k, o: (0, 0))
    outs = pl.pallas_call(
        body, name="proj_bwd",
        grid_spec=pltpu.PrefetchScalarGridSpec(
            num_scalar_prefetch=1, grid=(last + 1, nk),
            in_specs=[pl.BlockSpec((D, tt), lambda t, k, o: (0, jnp.where(first(t), k, nk - 1))),
                      pl.BlockSpec((tt, SHARD), lambda t, k, o: (jnp.where(first(t), k, jnp.minimum(t, last - 1) - NDEV),
                                                                 jnp.where(first(t), o[jnp.minimum(t, NDEV - 1)],
                                                                           slab(t, k)))),
                      pl.BlockSpec((None, D, SHARD), lambda t, k, o: (jnp.where(first(t), 0, slab(t, k)), 0, 0)),
                      chunk, chunk, vec, vec]
                     + [any_spec] * n,
            out_specs=[chunk, pl.BlockSpec((8, D), lambda t, k, o: (0, 0))] + [any_spec] * (2 + n),
            scratch_shapes=[pltpu.VMEM((D, SHARD), F32), pltpu.VMEM((2, D, SHARD), BF16),
                            pltpu.VMEM((2, tt, D), F32),
                            pltpu.SemaphoreType.DMA((1 + n, 7)), pltpu.SemaphoreType.DMA((1 + n, 7)),
                            pltpu.SemaphoreType.DMA((1 + n,)), pltpu.SemaphoreType.DMA((2,))]),
        out_shape=[jax.ShapeDtypeStruct((s, D), F32), jax.ShapeDtypeStruct((8, D), F32),
                   jax.ShapeDtypeStruct((NDEV, D, SHARD), BF16), jax.ShapeDtypeStruct((NDEV, D, SHARD), BF16)]
                  + [jax.ShapeDtypeStruct(a.shape, a.dtype) for a in smalls],
        compiler_params=_cp(("arbitrary", "arbitrary"), 56))(order, ht, dproj, wg, x, dy, nw, scale, *smalls)
    return outs[0], outs[1], outs[3], outs[4:]


def matmul_tn(a, b, name, tk):
    s, m = a.shape
    n = b.shape[1]
    nk = s // tk

    def body(a_ref, b_ref, o_ref, acc_ref):
        k = pl.program_id(0)
        p = lax.dot_general(a_ref[...], b_ref[...], TN, preferred_element_type=F32)

        @pl.when(k == 0)
        def _():
            acc_ref[...] = p

        @pl.when(k > 0)
        def _():
            acc_ref[...] += p

        @pl.when(k == nk - 1)
        def _():
            o_ref[...] = acc_ref[...].astype(BF16)

    return pl.pallas_call(
        body, name=name, grid=(nk,),
        in_specs=[pl.BlockSpec((tk, m), lambda k: (k, 0)), pl.BlockSpec((tk, n), lambda k: (k, 0))],
        out_specs=pl.BlockSpec((m, n), lambda k: (0, 0)),
        out_shape=jax.ShapeDtypeStruct((m, n), BF16),
        scratch_shapes=[pltpu.VMEM((m, n), F32)],
        compiler_params=_cp(("arbitrary",)))(a, b)


def _head_matrices():
    lane = lax.broadcasted_iota(jnp.int32, (CB, CB), 0)
    col = lax.broadcasted_iota(jnp.int32, (CB, CB), 1)
    same = (lane // HD == col // HD).astype(BF16)
    lane_c = lax.broadcasted_iota(jnp.int32, (CB, LANES), 0)
    col_c = lax.broadcasted_iota(jnp.int32, (CB, LANES), 1)
    total = (lane_c // HD == col_c).astype(BF16)
    lane_e = lax.broadcasted_iota(jnp.int32, (LANES, CB), 0)
    col_e = lax.broadcasted_iota(jnp.int32, (LANES, CB), 1)
    expand = (lane_e == col_e // HD).astype(BF16)
    return same, total, expand


def _head_sum(x, m_ref):
    return jnp.dot(x.astype(BF16), m_ref[...], preferred_element_type=F32)


def _dot_hilo(x, m_ref):
    hi = x.astype(BF16)
    lo = (x - hi.astype(F32)).astype(BF16)
    return (jnp.dot(hi, m_ref[...], preferred_element_type=F32)
            + jnp.dot(lo, m_ref[...], preferred_element_type=F32))


def _to_residue_major(val, buf, out_ref, dil):
    rows = out_ref.shape[1]
    for k in range(val.shape[1] // LANES):
        lanes = slice(k * LANES, (k + 1) * LANES)
        buf[k] = val[:, lanes]
        for r in range(dil):
            out_ref[r, :, lanes] = buf.at[k][pl.ds(r, rows, stride=dil), :].astype(out_ref.dtype)


def _from_residue_major(ref, buf, dil):
    if dil == 1:
        return ref[0].astype(F32)
    rows, chunks = ref.shape[1], ref.shape[2] // LANES
    for k in range(chunks):
        for r in range(dil):
            buf.at[k][pl.ds(r, rows, stride=dil), :] = ref[r, :, k * LANES:(k + 1) * LANES].astype(F32)
    return jnp.concatenate([buf[k] for k in range(chunks)], axis=1)


def qkv_prep(proj, qw8, kw8, same, tm):
    s = proj.shape[0]
    items = []
    for g, d in enumerate(DILATIONS):
        items += [(g, "q", CB_Q + g, d), (g, "k", CB_K + g, d)] + ([(g, "v", CB_V + g, d)] if d > 1 else [])
    n = len(items)

    def body(*refs):
        ins, (qw_ref, kw_ref, same_ref), outs, buf = refs[:n], refs[n:n + 3], refs[n + 3:2 * n + 3], refs[-1]
        for idx, (_, kind, _, dil) in enumerate(items):
            val = ins[idx][...].astype(F32)
            if kind != "v":
                r = lax.rsqrt(_head_sum(val * val, same_ref) * (1.0 / HD) + EPS)
                val = val * r * (qw_ref if kind == "q" else kw_ref)[...]
            if dil == 1:
                outs[idx][0] = val.astype(BF16)
            else:
                _to_residue_major(val, buf, outs[idx], dil)

    full = lambda a: pl.BlockSpec(a.shape, lambda i: (0, 0))
    outs = pl.pallas_call(
        body, name="qkv_prep", grid=(s // tm,),
        in_specs=[pl.BlockSpec((tm, CB), lambda i, cb=cb: (i, cb)) for _, _, cb, _ in items]
                 + [full(qw8), full(kw8), full(same)],
        out_specs=[pl.BlockSpec((d, tm // d, CB), lambda i: (0, i, 0)) for _, _, _, d in items],
        out_shape=[jax.ShapeDtypeStruct((d, s // d, CB), BF16) for _, _, _, d in items],
        scratch_shapes=[pltpu.VMEM((CB // LANES, tm, LANES), F32)],
        compiler_params=_cp(("parallel",)))(*([proj] * n), qw8 * (HD ** -0.5), kw8, same)
    srcs = [[None, None, (proj, CB_V + g)] for g in range(len(DILATIONS))]
    for (g, kind, _, _), o in zip(items, outs):
        srcs[g]["qkv".index(kind)] = (o.reshape(s, CB), 0)
    return srcs


def stats_prep(da, lc, dc, g, dil, tm):
    s = da.shape[0]
    rows = tm // dil

    def body(da_ref, lc_ref, dc_ref, dap_ref, lcp_ref, dcp_ref, lt_ref, dt_ref, buf):
        if dil == 1:
            dap_ref[0] = da_ref[...]
        else:
            _to_residue_major(da_ref[...].astype(F32), buf, dap_ref, dil)
        for src, dst, dst_t in ((lc_ref, lcp_ref, lt_ref), (dc_ref, dcp_ref, dt_ref)):
            buf[0] = src[...]
            for r in range(dil):
                piece = buf.at[0][pl.ds(r, rows, stride=dil), :] if dil > 1 else buf[0]
                dst[r] = piece
                dst_t[r] = piece.T[0:NH, :]

    row = lambda w: pl.BlockSpec((tm, w), lambda i: (i, 0))
    rm = lambda w: pl.BlockSpec((dil, rows, w), lambda i: (0, i, 0))
    tr = pl.BlockSpec((dil, NH, rows), lambda i: (0, 0, i))
    length = s // dil
    dap, lcp, dcp, lt, dt = pl.pallas_call(
        body, name=f"stats_prep_g{g}", grid=(s // tm,),
        in_specs=[row(CB), row(LANES), row(LANES)],
        out_specs=[rm(CB), rm(LANES), rm(LANES), tr, tr],
        out_shape=[jax.ShapeDtypeStruct((dil, length, CB), BF16)]
                  + [jax.ShapeDtypeStruct((dil, length, LANES), F32)] * 2
                  + [jax.ShapeDtypeStruct((dil, NH, length), F32)] * 2,
        scratch_shapes=[pltpu.VMEM((CB // LANES, tm, LANES), F32)],
        compiler_params=_cp(("parallel",)))(da, lc, dc)
    return (dap.reshape(s, CB), lcp.reshape(s, LANES), dcp.reshape(s, LANES),
            lt.reshape(dil * NH, length), dt.reshape(dil * NH, length))


def qkv_grads_to_dproj(dproj, proj, grads, qw8, kw8, same, tm):
    s = dproj.shape[0]
    ni = s // tm
    flat = [(t.reshape(d, s // d, CB), d, kind, 3 * kind + g)
            for g, d in enumerate(DILATIONS) for kind, t in enumerate(grads[g])]
    nf = len(flat)
    nraw = 2 * len(DILATIONS)

    def body(*refs):
        dp_hbm, raws, ins = refs[nraw + nf + 4], refs[1:1 + nraw], refs[1 + nraw:1 + nraw + nf]
        qw_ref, kw_ref, same_ref = refs[1 + nraw + nf:4 + nraw + nf]
        gw_ref, stage, buf, sems = refs[5 + nraw + nf:]
        i = pl.program_id(0)
        slot = i % 2

        def slab(step, sl):
            return pltpu.make_async_copy(
                stage.at[sl], dp_hbm.at[pl.ds(pl.multiple_of(step * tm, tm), tm), pl.ds(CB_Q * CB, 9 * CB)],
                sems.at[sl])

        @pl.when(i == 0)
        def _():
            gw_ref[...] = jnp.zeros_like(gw_ref)

        @pl.when(i >= 2)
        def _():
            slab(i - 2, slot).wait()

        for ref, (_, d, kind, jj) in zip(ins, flat):
            cols = slice(jj * CB, (jj + 1) * CB)
            dn = _from_residue_major(ref, buf, d)
            if kind == 2:
                stage[slot, :, cols] = dn.astype(BF16)
                continue
            t = raws[jj][...].astype(F32)
            r = lax.rsqrt(_head_sum(t * t, same_ref) * (1.0 / HD) + EPS)
            xh = t * r
            gw_ref[kind:kind + 1, :] += jnp.sum(dn * xh, axis=0, keepdims=True)
            dxh = dn * (qw_ref if kind == 0 else kw_ref)[...]
            mean = _head_sum(dxh * xh, same_ref) * (1.0 / HD)
            stage[slot, :, cols] = (r * (dxh - xh * mean)).astype(BF16)
        slab(i, slot).start()

        @pl.when(i == ni - 1)
        def _():
            slab(i - 1, 1 - slot).wait()
            slab(i, slot).wait()

    full = lambda a: pl.BlockSpec(a.shape, lambda i: (0, 0))
    any_spec = pl.BlockSpec(memory_space=pl.ANY)
    return pl.pallas_call(
        body, name="qkv_grads_to_dproj", grid=(ni,),
        in_specs=[any_spec] + [pl.BlockSpec((tm, CB), lambda i, jb=jb: (i, CB_Q + jb)) for jb in range(nraw)]
                 + [pl.BlockSpec((d, tm // d, CB), lambda i: (0, i, 0)) for _, d, _, _ in flat]
                 + [full(qw8), full(kw8), full(same)],
        out_specs=[any_spec, pl.BlockSpec((8, CB), lambda i: (0, 0))],
        out_shape=[jax.ShapeDtypeStruct((s, NIN), BF16), jax.ShapeDtypeStruct((8, CB), F32)],
        input_output_aliases={0: 0},
        scratch_shapes=[pltpu.VMEM((2, tm, 9 * CB), BF16), pltpu.VMEM((CB // LANES, tm, LANES), F32),
                        pltpu.SemaphoreType.DMA((2,))],
        compiler_params=_cp(("arbitrary",)))(
            dproj, *([proj] * nraw), *[t for t, _, _, _ in flat], qw8, kw8, same)


def _lane_lo():
    return lax.broadcasted_iota(jnp.int32, (1, 2 * HD), 1) < HD


def _stack_heads(t, lo):
    zero = jnp.zeros_like(t)
    return jnp.concatenate([jnp.where(lo, t, zero), jnp.where(lo, zero, t)], axis=0)


def _masks(other_ok):
    qi = lax.broadcasted_iota(jnp.int32, (QB, QB), 0)
    kj = lax.broadcasted_iota(jnp.int32, (QB, QB), 1)
    return (kj >= qi) & other_ok, kj <= qi


MAX_SUB = 8


def _attn_specs(nb, dil, sub):
    steps = nb // sub
    main = lambda cb, w=CB: pl.BlockSpec((sub * QB, w), lambda r, s: (r * steps + s, cb))
    prev = lambda cb: pl.BlockSpec((QB, CB), lambda r, s: (jnp.maximum(r * nb + sub * s - 1, 0), cb))
    nxt = lambda cb: pl.BlockSpec((QB, CB), lambda r, s: (jnp.minimum(r * nb + sub * (s + 1), dil * nb - 1), cb))
    return main, prev, nxt


def attn_fwd(q_src, k_src, v_src, g, dil):
    s = q_src[0].shape[0]
    nb = s // dil // QB
    sub = min(MAX_SUB, nb)
    main, prev, _ = _attn_specs(nb, dil, sub)

    def body(q_ref, kp_ref, k_ref, vp_ref, v_ref, o_ref, l_ref, kbuf, vbuf):
        step = pl.program_id(1)
        kbuf[0:QB], kbuf[QB:] = kp_ref[...], k_ref[...]
        vbuf[0:QB], vbuf[QB:] = vp_ref[...], v_ref[...]
        lo = _lane_lo()
        head_lane = lax.broadcasted_iota(jnp.int32, (1, LANES), 1)

        def block(j, carry):
            r0 = pl.multiple_of(j * QB, QB)
            rows, krows = pl.ds(r0, QB), pl.ds(r0, 2 * QB)
            m_prev, m_cur = _masks(step * sub + j > 0)
            mask = jnp.concatenate([m_prev, m_cur], axis=1)
            mask = jnp.concatenate([mask, mask], axis=0)
            lses = jnp.zeros((QB, LANES), F32)
            for i in range(NH // 2):
                sl = slice(2 * HD * i, 2 * HD * (i + 1))
                qs, ks, vv = q_ref[rows, sl], kbuf[krows, sl], vbuf[krows, sl]
                sc = lax.dot_general(_stack_heads(qs, lo), ks, NT, preferred_element_type=F32)
                sc = jnp.where(mask, sc, NEG)
                mx = jnp.max(sc, axis=-1, keepdims=True)
                p = jnp.exp(sc - mx)
                den = jnp.sum(p, axis=-1, keepdims=True)
                o = jnp.dot(p.astype(BF16), vv, preferred_element_type=F32) * (1.0 / den)
                lse = mx + jnp.log(den)
                o_ref[rows, sl] = jnp.where(lo, o[:QB], o[QB:]).astype(BF16)
                lses = jnp.where(head_lane == 2 * i, lse[:QB], jnp.where(head_lane == 2 * i + 1, lse[QB:], lses))
            l_ref[rows, :] = lses
            return carry

        lax.fori_loop(0, sub, block, 0, unroll=True)

    return pl.pallas_call(
        body, name=f"attn_fwd_g{g}", grid=(dil, nb // sub),
        in_specs=[main(q_src[1]), prev(k_src[1]), main(k_src[1]), prev(v_src[1]), main(v_src[1])],
        out_specs=[main(0), main(0, LANES)],
        out_shape=[jax.ShapeDtypeStruct((s, CB), BF16), jax.ShapeDtypeStruct((s, LANES), F32)],
        scratch_shapes=[pltpu.VMEM(((sub + 1) * QB, CB), BF16)] * 2,
        compiler_params=_cp(("parallel", "parallel")))(q_src[0], k_src[0], k_src[0], v_src[0], v_src[0])


def attn_bwd_q(q_src, k_src, v_src, da, lc, dc, g, dil):
    s = q_src[0].shape[0]
    nb = s // dil // QB
    sub = min(MAX_SUB, nb)
    main, prev, _ = _attn_specs(nb, dil, sub)

    def body(q_ref, kp_ref, k_ref, vp_ref, v_ref, da_ref, l_ref, d_ref, dq_ref, kbuf, vbuf):
        step = pl.program_id(1)
        kbuf[0:QB], kbuf[QB:] = kp_ref[...], k_ref[...]
        vbuf[0:QB], vbuf[QB:] = vp_ref[...], v_ref[...]
        lo = _lane_lo()

        def block(j, carry):
            r0 = pl.multiple_of(j * QB, QB)
            rows, krows = pl.ds(r0, QB), pl.ds(r0, 2 * QB)
            m_prev, m_cur = _masks(step * sub + j > 0)
            mask = jnp.concatenate([m_prev, m_cur], axis=1)
            mask = jnp.concatenate([mask, mask], axis=0)
            lcols, dcols = l_ref[rows, :], d_ref[rows, :]
            for i in range(NH // 2):
                sl = slice(2 * HD * i, 2 * HD * (i + 1))
                qs, ks, vv, da2 = q_ref[rows, sl], kbuf[krows, sl], vbuf[krows, sl], da_ref[rows, sl]
                pair = lambda t: jnp.concatenate([t[:, 2 * i:2 * i + 1], t[:, 2 * i + 1:2 * i + 2]], axis=0)
                sc = lax.dot_general(_stack_heads(qs, lo), ks, NT, preferred_element_type=F32)
                sc = jnp.where(mask, sc, NEG)
                p = jnp.exp(sc - pair(lcols))
                dp = lax.dot_general(_stack_heads(da2, lo), vv, NT, preferred_element_type=F32)
                ds = p * (dp - pair(dcols))
                dq = jnp.dot(ds.astype(BF16), ks, preferred_element_type=F32)
                dq_ref[rows, sl] = (jnp.where(lo, dq[:QB], dq[QB:]) * (HD ** -0.5)).astype(BF16)
            return carry

        lax.fori_loop(0, sub, block, 0, unroll=True)

    return pl.pallas_call(
        body, name=f"attn_bwd_q_g{g}", grid=(dil, nb // sub),
        in_specs=[main(q_src[1]), prev(k_src[1]), main(k_src[1]), prev(v_src[1]), main(v_src[1]),
                  main(0), main(0, LANES), main(0, LANES)],
        out_specs=main(0), out_shape=jax.ShapeDtypeStruct((s, CB), BF16),
        scratch_shapes=[pltpu.VMEM(((sub + 1) * QB, CB), BF16)] * 2,
        compiler_params=_cp(("parallel", "parallel")))(
            q_src[0], k_src[0], k_src[0], v_src[0], v_src[0], da, lc, dc)


def attn_bwd_kv(q_src, k_src, v_src, da, lt, dt, g, dil):
    s = q_src[0].shape[0]
    nb = s // dil // QB
    sub = min(MAX_SUB, nb)
    main, _, nxt = _attn_specs(nb, dil, sub)

    def body(k_ref, v_ref, q_ref, qn_ref, da_ref, dan_ref, l_ref, ln_ref, d_ref, dn_ref, dk_ref, dv_ref,
             qbuf, dabuf, lbuf, dbuf):
        step = pl.program_id(1)
        qbuf[0:sub * QB], qbuf[sub * QB:] = q_ref[...], qn_ref[...]
        dabuf[0:sub * QB], dabuf[sub * QB:] = da_ref[...], dan_ref[...]
        for c in range(sub):
            lbuf[c], dbuf[c] = l_ref[:, c * QB:(c + 1) * QB], d_ref[:, c * QB:(c + 1) * QB]
        lbuf[sub], dbuf[sub] = ln_ref[...], dn_ref[...]
        lo = _lane_lo()
        kj = lax.broadcasted_iota(jnp.int32, (QB, QB), 0)
        qi = lax.broadcasted_iota(jnp.int32, (QB, QB), 1)

        def block(j, carry):
            r0 = pl.multiple_of(j * QB, QB)
            rows, qrows = pl.ds(r0, QB), pl.ds(r0, 2 * QB)
            mask = jnp.concatenate([kj <= qi, (kj >= qi) & (step * sub + j < nb - 1)], axis=1)
            mask = jnp.concatenate([mask, mask], axis=1)
            lrow = jnp.concatenate([lbuf[j], lbuf[j + 1]], axis=1)
            drow = jnp.concatenate([dbuf[j], dbuf[j + 1]], axis=1)
            for i in range(NH // 2):
                sl = slice(2 * HD * i, 2 * HD * (i + 1))
                q2, da2 = _stack_heads(qbuf[qrows, sl], lo), _stack_heads(dabuf[qrows, sl], lo)
                ks, vv = k_ref[rows, sl], v_ref[rows, sl]
                pair = lambda t: jnp.concatenate([t[2 * i:2 * i + 1, :], t[2 * i + 1:2 * i + 2, :]], axis=1)
                sc = lax.dot_general(ks, q2, NT, preferred_element_type=F32)
                sc = jnp.where(mask, sc, NEG)
                p = jnp.exp(sc - pair(lrow))
                dp = lax.dot_general(vv, da2, NT, preferred_element_type=F32)
                ds = p * (dp - pair(drow))
                dv_ref[rows, sl] = jnp.dot(p.astype(BF16), da2, preferred_element_type=F32).astype(BF16)
                dk_ref[rows, sl] = jnp.dot(ds.astype(BF16), q2, preferred_element_type=F32).astype(BF16)
            return carry

        lax.fori_loop(0, sub, block, 0, unroll=True)

    steps = nb // sub
    t_main = pl.BlockSpec((NH, sub * QB), lambda r, s: (r, s))
    t_nxt = pl.BlockSpec((NH, QB), lambda r, s: (r, jnp.minimum(sub * (s + 1), nb - 1)))
    out = jax.ShapeDtypeStruct((s, CB), BF16)
    return pl.pallas_call(
        body, name=f"attn_bwd_kv_g{g}", grid=(dil, steps),
        in_specs=[main(k_src[1]), main(v_src[1]), main(q_src[1]), nxt(q_src[1]),
                  main(0), nxt(0), t_main, t_nxt, t_main, t_nxt],
        out_specs=[main(0), main(0)], out_shape=[out, out],
        scratch_shapes=[pltpu.VMEM(((sub + 1) * QB, CB), BF16)] * 2 + [pltpu.VMEM((sub + 1, NH, QB), F32)] * 2,
        compiler_params=_cp(("parallel", "parallel")))(
            k_src[0], v_src[0], q_src[0], q_src[0], da, da, lt, lt, dt, dt)


def _conv_taps(u, u_prev, first):
    tm = u.shape[0]
    row = lax.broadcasted_iota(jnp.int32, (tm, 1), 0)
    up = jnp.where(first, 0.0, u_prev)
    u1 = jnp.where(row == 0, up[HALO - 1:HALO, :], pltpu.roll(u, 1, 0))
    u2 = jnp.where(row == 0, up[HALO - 2:HALO - 1, :],
                   jnp.where(row == 1, up[HALO - 1:HALO, :], pltpu.roll(u, 2, 0)))
    return u1, u2


def mid_fwd(proj, o_g, lse_g, conv_w, expand, tm):
    s = proj.shape[0]
    hb = tm // HALO

    def body(ba_ref, ca_ref, xa_ref, za_ref, cah_ref, xah_ref, zb_ref,
             o0, o1, o2, l0, l1, l2, w_ref, exp_ref, ya_ref, yb_ref, at_ref, lc_ref, buf_o, buf_l):
        first = pl.program_id(0) == 0
        u = ca_ref[...].astype(F32) * xa_ref[...].astype(F32)
        u1, u2 = _conv_taps(u, cah_ref[...].astype(F32) * xah_ref[...].astype(F32), first)
        conv = w_ref[0:1, :] * u2 + w_ref[1:2, :] * u1 + w_ref[2:3, :] * u
        ya_ref[...] = (ba_ref[...].astype(F32) * conv * _silu(za_ref[...].astype(F32))).astype(BF16)
        ls = [_from_residue_major(l, buf_l.at[g], d) for g, (l, d) in enumerate(zip((l0, l1, l2), DILATIONS))]
        mx = jnp.maximum(jnp.maximum(ls[0], ls[1]), ls[2])
        es = [jnp.exp(l - mx) for l in ls]
        den = es[0] + es[1] + es[2]
        attn = jnp.zeros((tm, CB), F32)
        for e, o, d in zip(es, (o0, o1, o2), DILATIONS):
            attn = attn + _dot_hilo(e / den, exp_ref) * _from_residue_major(o, buf_o, d)
        at_ref[...] = attn
        lc_ref[...] = mx + jnp.log(den)
        yb_ref[...] = (attn * _silu(zb_ref[...].astype(F32))).astype(BF16)

    col = lambda j: pl.BlockSpec((tm, D), lambda i: (i, j))
    halo = lambda j: pl.BlockSpec((HALO, D), lambda i: (jnp.maximum(i * hb - 1, 0), j))
    loc = lambda w: pl.BlockSpec((tm, w), lambda i: (i, 0))
    rm = lambda w: [pl.BlockSpec((d, tm // d, w), lambda i: (0, i, 0)) for d in DILATIONS]
    rm_view = lambda ts, w: [t.reshape(d, s // d, w) for t, d in zip(ts, DILATIONS)]
    return pl.pallas_call(
        body, name="mid_fwd", grid=(s // tm,),
        in_specs=[col(0), col(1), col(2), col(3), halo(1), halo(2),
                  pl.BlockSpec((tm, CB), lambda i: (i, CB_ZB))] + rm(CB) + rm(LANES)
                 + [pl.BlockSpec((3, D), lambda i: (0, 0)), pl.BlockSpec(expand.shape, lambda i: (0, 0))],
        out_specs=[loc(D), loc(CB), loc(CB), loc(LANES)],
        out_shape=[jax.ShapeDtypeStruct((s, D), BF16), jax.ShapeDtypeStruct((s, CB), BF16),
                   jax.ShapeDtypeStruct((s, CB), F32), jax.ShapeDtypeStruct((s, LANES), F32)],
        scratch_shapes=[pltpu.VMEM((CB // LANES, tm, LANES), F32), pltpu.VMEM((3, 1, tm, LANES), F32)],
        compiler_params=_cp(("parallel",)))(
            proj, proj, proj, proj, proj, proj, proj, *rm_view(o_g, CB), *rm_view(lse_g, LANES), conv_w, expand)


def tail(proj, ya, yb, attn, x, target, gate, pa_w, pb_w, wo_w, total, conv_w, tm):
    s = proj.shape[0]
    ni = s // tm
    hb = tm // HALO
    nlate = NIN - CB_ZB * CB
    nearly = 4 * D

    def body(ya_ref, yb_ref, ga_ref, gb_ref, zb_ref, at_ref, x_ref, t_ref, gate_ref, pa_ref, pb_ref, wo_ref,
             tot_ref, ba_ref, ca_ref, xa_ref, za_ref, cah_ref, xah_ref, cw_ref,
             dp_hbm, dy_ref, da_ref, dc_ref, mg_ref, do_ref, dpa_ref, dpb_ref, st_ref, gwc_ref,
             stage, dconv_next, sems):
        step = pl.program_id(0)
        i = ni - 1 - step
        slot = step % 2

        def slabs(at_step, sl):
            rows = pl.ds(pl.multiple_of((ni - 1 - at_step) * tm, tm), tm)
            return (pltpu.make_async_copy(stage.at[sl, :, 0:nearly], dp_hbm.at[rows, pl.ds(0, nearly)],
                                          sems.at[sl, 0]),
                    pltpu.make_async_copy(stage.at[sl, :, nearly:], dp_hbm.at[rows, pl.ds(CB_ZB * CB, nlate)],
                                          sems.at[sl, 1]))

        @pl.when(step == 0)
        def _():
            st_ref[...] = jnp.zeros_like(st_ref)
            gwc_ref[...] = jnp.zeros_like(gwc_ref)
            dconv_next[...] = jnp.zeros_like(dconv_next)

        @pl.when(step >= 2)
        def _():
            for cp in slabs(step - 2, slot):
                cp.wait()

        gate_v = gate_ref[...]
        pa = jnp.dot(ya_ref[...], pa_ref[...], preferred_element_type=F32)
        pb = jnp.dot(yb_ref[...], pb_ref[...], preferred_element_type=F32)
        sa = jax.nn.sigmoid(ga_ref[...].astype(F32))
        sb = jax.nn.sigmoid(gb_ref[...].astype(F32))
        merged = (sa * pa + sb * pb).astype(BF16)
        mg_ref[...] = merged
        out = jnp.dot(merged, wo_ref[...], preferred_element_type=F32)
        err = x_ref[...] + gate_v * out - t_ref[...]
        dy = err * (1.0 / D)
        dy_ref[...] = dy
        st_ref[0:1, :] += jnp.sum(dy * out, axis=0, keepdims=True)
        st_ref[1:2, :] += jnp.sum(err * err, axis=0, keepdims=True)
        dout = (gate_v * dy).astype(BF16)
        do_ref[...] = dout
        dmg = lax.dot_general(dout, wo_ref[...], NT, preferred_element_type=F32)
        dpa = (dmg * sa).astype(BF16)
        dpb = (dmg * sb).astype(BF16)
        dpa_ref[...] = dpa
        dpb_ref[...] = dpb
        late = nearly
        stage[slot, :, late + CB:late + CB + D] = (dmg * pa * sa * (1.0 - sa)).astype(BF16)
        stage[slot, :, late + CB + D:] = (dmg * pb * sb * (1.0 - sb)).astype(BF16)
        dya = lax.dot_general(dpa, pa_ref[...], NT, preferred_element_type=F32)
        dyb = lax.dot_general(dpb, pb_ref[...], NT, preferred_element_type=F32)
        zb = zb_ref[...].astype(F32)
        sg = jax.nn.sigmoid(zb)
        attn_v = at_ref[...]
        dattn = dyb * (zb * sg)
        da_ref[...] = dattn.astype(BF16)
        stage[slot, :, late:late + CB] = (dyb * attn_v * (sg * (1.0 + zb * (1.0 - sg)))).astype(BF16)
        dc_ref[...] = _dot_hilo(dattn * attn_v, tot_ref)

        ba, ca, xa, za = (t[...].astype(F32) for t in (ba_ref, ca_ref, xa_ref, za_ref))
        u = ca * xa
        u1, u2 = _conv_taps(u, cah_ref[...].astype(F32) * xah_ref[...].astype(F32), i == 0)
        w0, w1, w2 = cw_ref[0:1, :], cw_ref[1:2, :], cw_ref[2:3, :]
        conv = w0 * u2 + w1 * u1 + w2 * u
        sga = jax.nn.sigmoid(za)
        sza = za * sga
        dconv = dya * ba * sza
        dcn = dconv_next[...]
        rowi = lax.broadcasted_iota(jnp.int32, (tm, 1), 0)
        d1 = jnp.where(rowi == tm - 1, dcn[0:1, :], pltpu.roll(dconv, tm - 1, 0))
        d2 = jnp.where(rowi == tm - 2, dcn[0:1, :],
                       jnp.where(rowi == tm - 1, dcn[1:2, :], pltpu.roll(dconv, tm - 2, 0)))
        du = w2 * dconv + w1 * d1 + w0 * d2
        stage[slot, :, 0:D] = (dya * conv * sza).astype(BF16)
        stage[slot, :, D:2 * D] = (du * xa).astype(BF16)
        stage[slot, :, 2 * D:3 * D] = (du * ca).astype(BF16)
        stage[slot, :, 3 * D:4 * D] = (dya * ba * conv * (sga * (1.0 + za * (1.0 - sga)))).astype(BF16)
        gwc_ref[0:1, :] += jnp.sum(dconv * u2, axis=0, keepdims=True)
        gwc_ref[1:2, :] += jnp.sum(dconv * u1, axis=0, keepdims=True)
        gwc_ref[2:3, :] += jnp.sum(dconv * u, axis=0, keepdims=True)
        dconv_next[...] = dconv[0:8, :]

        for cp in slabs(step, slot):
            cp.start()

        @pl.when(step == ni - 1)
        def _():
            for cp in slabs(step - 1, 1 - slot) + slabs(step, slot):
                cp.wait()

    rev = lambda st: ni - 1 - st
    row = lambda w: pl.BlockSpec((tm, w), lambda st: (rev(st), 0))
    pcol = lambda w, jb: pl.BlockSpec((tm, w), lambda st: (rev(st), jb))
    halo = lambda jb: pl.BlockSpec((HALO, D), lambda st: (jnp.maximum(rev(st) * hb - 1, 0), jb))
    const = lambda a: pl.BlockSpec(a.shape, lambda st: (0, 0), pipeline_mode=pl.Buffered(1))
    acc = pl.BlockSpec((8, D), lambda st: (0, 0))
    return pl.pallas_call(
        body, name="tail", grid=(ni,),
        in_specs=[row(D), row(CB), pcol(D, 9), pcol(D, 10), pcol(CB, CB_ZB), row(CB), row(D), row(D),
                  pl.BlockSpec((1, D), lambda st: (0, 0)), const(pa_w), const(pb_w), const(wo_w), const(total),
                  pcol(D, 0), pcol(D, 1), pcol(D, 2), pcol(D, 3), halo(1), halo(2),
                  pl.BlockSpec((3, D), lambda st: (0, 0))],
        out_specs=[pl.BlockSpec(memory_space=pl.ANY),
                   row(D), row(CB), row(LANES), row(D), row(D), row(D), row(D), acc, acc],
        out_shape=[jax.ShapeDtypeStruct((s, NIN), BF16), jax.ShapeDtypeStruct((s, D), F32),
                   jax.ShapeDtypeStruct((s, CB), BF16), jax.ShapeDtypeStruct((s, LANES), F32)]
                  + [jax.ShapeDtypeStruct((s, D), BF16)] * 4 + [jax.ShapeDtypeStruct((8, D), F32)] * 2,
        scratch_shapes=[pltpu.VMEM((2, tm, nearly + nlate), BF16), pltpu.VMEM((8, D), F32),
                        pltpu.SemaphoreType.DMA((2, 2))],
        compiler_params=_cp(("arbitrary",), 60))(
            ya, yb, proj, proj, proj, attn, x, target, gate, pa_w, pb_w, wo_w, total,
            proj, proj, proj, proj, proj, proj, conv_w)


def _local_step(x, target, shift, scale, gate, norm_w, conv_w, qw, kw, w_shard, small_shards, me_xyc):
    qw8, kw8 = jnp.tile(qw, (1, NH)), jnp.tile(kw, (1, NH))
    same, total, expand = _head_matrices()
    proj, ht, wg, (pa_g, pb_g, wo_g) = proj_fwd_gather(
        x, norm_w, scale, shift, w_shard, small_shards, gather_order(me_xyc), 1024)
    pa_w, wo_w = pa_g.reshape(D, D), wo_g.reshape(D, D)
    pb_w = pb_g.transpose(1, 0, 2).reshape(CB, D)
    srcs = qkv_prep(proj, qw8, kw8, same, 512)
    o_g, lse_g = zip(*[attn_fwd(*srcs[g], g, d) for g, d in enumerate(DILATIONS)])
    ya, yb, attn, lc = mid_fwd(proj, o_g, lse_g, conv_w, expand, 512)
    dproj, dy, da, dc, merged, dout, dpa, dpb, st_tail, st_conv = tail(
        proj, ya, yb, attn, x, target, gate, pa_w, pb_w, wo_w, total, conv_w, 256)
    g_wo = matmul_tn(merged, dout, "grad_w_out", 1024)
    g_pa = matmul_tn(ya, dpa, "grad_w_br_conv", 1024)
    g_pb = matmul_tn(yb, dpb, "grad_w_br_attn", 1024)
    grads = []
    for g, d in enumerate(DILATIONS):
        da_p, lc_p, dc_p, lt, dt = stats_prep(da, lc, dc, g, d, 2048)
        dq = attn_bwd_q(*srcs[g], da_p, lc_p, dc_p, g, d)
        dk, dv = attn_bwd_kv(*srcs[g], da_p, lt, dt, g, d)
        grads.append((dq, dk, dv))
    dproj, gw_qk = qkv_grads_to_dproj(dproj, proj, grads, qw8, kw8, same, 512)
    slabs = [g_pa.reshape(NDEV, 128, D), g_pb.reshape(CB, NDEV, 128).transpose(1, 0, 2), g_wo.reshape(NDEV, 128, D)]
    grad_x, st_norm, r_win, (r_pa, r_pb, r_wo) = proj_bwd(
        ht, dproj, wg, slabs, scatter_order(me_xyc), x, dy, norm_w, scale, 1024)
    dmod = jnp.concatenate([st_norm[0:1], st_norm[1:2], st_tail[0:1]], axis=1)
    loss_part = (0.5 / D) * jnp.sum(st_tail[1])
    gw_heads = gw_qk[0:2].reshape(2, NH, HD).sum(axis=1)
    small = dict(dmod=dmod, norm_w=st_norm[2:3], conv_w=st_conv[0:3],
                 q_norm_w=gw_heads[0:1], k_norm_w=gw_heads[1:2], loss=loss_part)
    return grad_x, small, (r_win, r_pa, r_pb, r_wo)


def kernel(x, c, w_ada, b_ada, norm_w, w_in, conv_w, q_norm_w, k_norm_w, w_br_conv, w_br_attn, w_out, loss_target, m_w_ada, m_b_ada, m_norm_w, m_w_in, m_conv_w, m_q_norm_w, m_k_norm_w, m_w_br_conv, m_w_br_attn, m_w_out, v_w_ada, v_b_ada, v_norm_w, v_w_in, v_conv_w, v_q_norm_w, v_k_norm_w, v_w_br_conv, v_w_br_attn, v_w_out):
    me_xyc = (lax.axis_index("x"), lax.axis_index("y"), lax.axis_index("c"))
    me = _dev_index(me_xyc)
    ncol = w_ada.shape[2]

    conv_pad = jnp.zeros((8, 128), F32).at[0:3].set(conv_w[0])
    b_cols = lax.dynamic_slice(b_ada, (0, me * ncol), (1, ncol))
    mod_pieces, c_all, conv_all = ada_fwd(c, conv_pad, w_ada[0], b_cols)
    conv_full = conv_all[:, 0:3].transpose(1, 0, 2).reshape(3, D)
    c_all = c_all.reshape(NDEV, D)
    mod = mod_pieces.reshape(1, 3 * D)
    shift, scale, gate = mod[:, 0:D], mod[:, D:2 * D], mod[:, 2 * D:3 * D]

    grad_x, small, (r_win, r_pa, r_pb, r_wo) = _local_step(
        x[0], loss_target[0], shift, scale, gate, norm_w, conv_full, q_norm_w, k_norm_w,
        w_in[0].astype(BF16), [w_br_conv[0].astype(BF16), w_br_attn[0].astype(BF16), w_out[0].astype(BF16)], me_xyc)

    packed = jnp.concatenate(
        [small["dmod"], small["norm_w"], small["conv_w"].reshape(1, 3 * D), small["q_norm_w"], small["k_norm_w"],
         jnp.full((1, 128), small["loss"], F32)], axis=1)
    (packed_all,) = all_gather([packed], "gather_small")
    tot = sum_parts(packed_all)
    loss = tot[0, 7 * D + 2 * HD]
    dmod_all = packed_all[:, 0, 0:3 * D]
    g_b_ada = tot[:, 0:3 * D]
    g_norm_w = tot[:, 3 * D:4 * D]
    g_conv = lax.dynamic_slice(tot[:, 4 * D:7 * D].reshape(3, D), (0, me * 128), (3, 128))
    g_qn = tot[:, 7 * D:7 * D + HD]
    g_kn = tot[:, 7 * D + HD:7 * D + 2 * HD]
    g_w_ada = ada_bwd(c_all.T, lax.dynamic_slice(dmod_all, (0, me * ncol), (NDEV, ncol)))

    def upd(parts, w, m, v, name, rows):
        shape = w.shape
        w2, m2, v2 = (t.reshape(shape[-2:]) for t in (w, m, v))
        return [t.reshape(shape) for t in adamw(parts, w2, m2, v2, name, rows)]

    res = {
        "w_in": upd(r_win, w_in, m_w_in, v_w_in, "adamw_w_in", 128),
        "w_br_conv": upd(r_pa, w_br_conv, m_w_br_conv, v_w_br_conv, "adamw_w_br_conv", 128),
        "w_br_attn": upd(r_pb, w_br_attn, m_w_br_attn, v_w_br_attn, "adamw_w_br_attn", 512),
        "w_out": upd(r_wo, w_out, m_w_out, v_w_out, "adamw_w_out", 128),
    }
    small_params = {"w_ada": (g_w_ada, w_ada, m_w_ada, v_w_ada), "b_ada": (g_b_ada, b_ada, m_b_ada, v_b_ada),
                    "norm_w": (g_norm_w, norm_w, m_norm_w, v_norm_w), "conv_w": (g_conv, conv_w, m_conv_w, v_conv_w),
                    "q_norm_w": (g_qn, q_norm_w, m_q_norm_w, v_q_norm_w),
                    "k_norm_w": (g_kn, k_norm_w, m_k_norm_w, v_k_norm_w)}
    updated = adamw_small([tuple(t.reshape(t.shape[-2:]) for t in item) for item in small_params.values()])
    for (pname, item), outs4 in zip(small_params.items(), updated):
        res[pname] = [t.reshape(item[1].shape) for t in outs4]
    names = ["w_ada", "b_ada", "norm_w", "w_in", "conv_w", "q_norm_w", "k_norm_w", "w_br_conv", "w_br_attn", "w_out"]
    return (loss, grad_x[None], *[res[n][0] for n in names], *[res[n][1] for n in names],
            *[res[n][2] for n in names], *[res[n][3] for n in names])
```

```python
import jax
import jax.numpy as jnp
from jax import lax
from jax.experimental import pallas as pl
from jax.experimental.pallas import tpu as pltpu

F32, BF16 = jnp.float32, jnp.bfloat16
D = 1024
NIN = 11264
NDEV = 8
SHARD = NIN // NDEV
HD = 64
NH = 8
QB = 128
CB = 512
CB_Q, CB_K, CB_V, CB_ZB = 8, 11, 14, 17
DILATIONS = (1, 4, 16)
EPS = 1e-6
NEG = -1e30
HALO = 16
LANES = 128
MESH = pl.DeviceIdType.MESH

ADAM_LR, ADAM_B1, ADAM_B2, ADAM_EPS, ADAM_WD, ADAM_STEP = 0.001, 0.9, 0.999, 1e-08, 0.01, 10

NT = (((1,), (1,)), ((), ()))
TN = (((0,), (0,)), ((), ()))


def _cp(sem, vmem_mb=48):
    return pltpu.CompilerParams(dimension_semantics=sem, vmem_limit_bytes=vmem_mb << 20)


def _silu(z):
    return z * jax.nn.sigmoid(z)


def _coords():
    return lax.axis_index("x"), lax.axis_index("y"), lax.axis_index("c")


FLIPS = [(fx, fy, fc) for fx in (0, 1) for fy in (0, 1) for fc in (0, 1)][1:]


def all_gather(arrs, name):
    n = len(arrs)

    def body(*refs):
        ins, outs = refs[:n], refs[n:2 * n]
        send_sems, recv_sems, local_sems = refs[2 * n:]
        me_xyc = _coords()
        me = _dev_index(me_xyc)
        peers = [_flip(me_xyc, f) for f in FLIPS]

        def copy(a, k, block):
            return pltpu.make_async_remote_copy(
                src_ref=ins[a], dst_ref=outs[a].at[block], send_sem=send_sems.at[a, k], recv_sem=recv_sems.at[a, k],
                device_id=peers[k], device_id_type=MESH)

        mine = [pltpu.make_async_copy(ins[a], outs[a].at[me], local_sems.at[a]) for a in range(n)]
        sends = [copy(a, k, me) for k in range(7) for a in range(n)]
        for cp in mine + sends:
            cp.start()
        for k in range(7):
            for a in range(n):
                copy(a, k, _dev_index(peers[k])).wait_recv()
        for cp in sends:
            cp.wait_send()
        for cp in mine:
            cp.wait()

    any_spec = pl.BlockSpec(memory_space=pl.ANY)
    return pl.pallas_call(
        body, name=name,
        out_shape=[jax.ShapeDtypeStruct((NDEV,) + a.shape, a.dtype) for a in arrs],
        in_specs=[any_spec] * n, out_specs=[any_spec] * n,
        scratch_shapes=[pltpu.SemaphoreType.DMA((n, 7)), pltpu.SemaphoreType.DMA((n, 7)),
                        pltpu.SemaphoreType.DMA((n,))],
    )(*arrs)


def _flip(dev, f):
    return tuple(1 - v if b else v for v, b in zip(dev, f))


def _dev_index(dev):
    return 4 * dev[0] + 2 * dev[1] + dev[2]


def _chip_order(x, y, c):
    xor = lambda a, b: a + b - 2 * a * b
    return [(xor(x, 1 - c), xor(y, c)), (xor(x, c), xor(y, 1 - c)), (1 - x, 1 - y)]


def gather_order(me_xyc):
    x, y, c = me_xyc
    chips = _chip_order(x, y, c)
    devs = [(x, y, c), (x, y, 1 - c), (*chips[0], c), (*chips[1], c),
            (*chips[1], 1 - c), (*chips[0], 1 - c), (*chips[2], c), (*chips[2], 1 - c)]
    return jnp.stack([_dev_index(d) for d in devs]).astype(jnp.int32)


def scatter_order(me_xyc):
    devs = [_flip(me_xyc, f) for f in FLIPS] + [me_xyc]
    return jnp.stack([_dev_index(d) for d in devs]).astype(jnp.int32)


def ada_fwd(c, conv_pad, w_ada, b_cols):
    ncol = w_ada.shape[1]

    def body(c_ref, cv_ref, w_ref, b_ref, mod_ref, call_ref, cvall_ref, rows_buf, send_sems, recv_sems, local_sems):
        me_xyc = _coords()
        me = _dev_index(me_xyc)
        peers = [_flip(me_xyc, f) for f in FLIPS]
        pids = [_dev_index(p) for p in peers]

        def copy(a, k, src, dst):
            return pltpu.make_async_remote_copy(src_ref=src, dst_ref=dst, send_sem=send_sems.at[a, k],
                                                recv_sem=recv_sems.at[a, k], device_id=peers[k], device_id_type=MESH)

        own = [pltpu.make_async_copy(c_ref, call_ref.at[me], local_sems.at[0]),
               pltpu.make_async_copy(cv_ref, cvall_ref.at[me], local_sems.at[1])]
        first = [copy(0, k, c_ref, call_ref.at[me]) for k in range(7)]
        first += [copy(1, k, cv_ref, cvall_ref.at[me]) for k in range(7)]
        for cp in own + first:
            cp.start()
        own[0].wait()
        for k in range(7):
            copy(0, k, c_ref, call_ref.at[pids[k]]).wait_recv()
        seq = lax.broadcasted_iota(jnp.int32, (NDEV, 1), 0)
        c_all = jnp.zeros((NDEV, D), F32)
        for p in range(NDEV):
            c_all = jnp.where(seq == p, call_ref[p], c_all)
        mods = jnp.dot(_silu(c_all).astype(BF16), w_ref[...].astype(BF16), preferred_element_type=F32) + b_ref[...]
        for p in range(NDEV):
            rows_buf[p] = mods[p:p + 1, :]
        mine = pltpu.make_async_copy(rows_buf.at[me], mod_ref.at[me], local_sems.at[2])
        second = [copy(2, k, rows_buf.at[pids[k]], mod_ref.at[me]) for k in range(7)]
        for cp in [mine] + second:
            cp.start()
        for k in range(7):
            copy(2, k, rows_buf.at[pids[k]], mod_ref.at[pids[k]]).wait_recv()
            copy(1, k, cv_ref, cvall_ref.at[pids[k]]).wait_recv()
        for cp in first + second:
            cp.wait_send()
        own[1].wait()
        mine.wait()

    return pl.pallas_call(
        body, name="ada_fwd",
        out_shape=[jax.ShapeDtypeStruct((NDEV, 1, ncol), F32), jax.ShapeDtypeStruct((NDEV, 1, D), F32),
                   jax.ShapeDtypeStruct((NDEV,) + conv_pad.shape, F32)],
        scratch_shapes=[pltpu.VMEM((NDEV, 1, ncol), F32), pltpu.SemaphoreType.DMA((3, 7)),
                        pltpu.SemaphoreType.DMA((3, 7)), pltpu.SemaphoreType.DMA((3,))],
    )(c, conv_pad, w_ada, b_cols)


def ada_bwd(c_all_t, dmod_cols):
    def body(c_ref, d_ref, o_ref):
        at = _silu(c_ref[...])
        acc = at[:, 0:1] * d_ref[0:1, :]
        for b in range(1, NDEV):
            acc = acc + at[:, b:b + 1] * d_ref[b:b + 1, :]
        o_ref[...] = acc

    return pl.pallas_call(body, name="ada_bwd",
                          out_shape=jax.ShapeDtypeStruct((D, dmod_cols.shape[1]), F32))(c_all_t, dmod_cols)


def sum_parts(parts):
    def body(p_ref, o_ref):
        acc = p_ref[0]
        for b in range(1, NDEV):
            acc = acc + p_ref[b]
        o_ref[...] = acc

    return pl.pallas_call(body, name="sum_parts",
                          out_shape=jax.ShapeDtypeStruct(parts.shape[1:], F32))(parts)


def _adamw_update(g, w_ref, m_ref, v_ref, g_ref, d_ref, nm_ref, nv_ref):
    nm = ADAM_B1 * m_ref[...] + (1.0 - ADAM_B1) * g
    nv = ADAM_B2 * v_ref[...] + (1.0 - ADAM_B2) * (g * g)
    g_ref[...] = g
    nm_ref[...] = nm
    nv_ref[...] = nv
    m_hat = nm / (1.0 - ADAM_B1 ** ADAM_STEP)
    v_hat = nv / (1.0 - ADAM_B2 ** ADAM_STEP)
    d_ref[...] = -ADAM_LR * (m_hat / (jnp.sqrt(v_hat) + ADAM_EPS) + ADAM_WD * w_ref[...])


def adamw_small(items):
    n = len(items)

    def body(*refs):
        ins, outs = refs[:4 * n], refs[4 * n:]
        for a in range(n):
            g_in, w_ref, m_ref, v_ref = ins[4 * a:4 * a + 4]
            _adamw_update(g_in[...], w_ref, m_ref, v_ref, *outs[4 * a:4 * a + 4])

    out = pl.pallas_call(
        body, name="adamw_small",
        out_shape=[jax.ShapeDtypeStruct(it[1].shape, F32) for it in items for _ in range(4)],
        compiler_params=pltpu.CompilerParams(vmem_limit_bytes=48 << 20))(*[t for it in items for t in it])
    return [out[4 * a:4 * a + 4] for a in range(n)]


def adamw(parts, w, m, v, name, rows):
    n, r, ccols = parts.shape

    def body(p_ref, w_ref, m_ref, v_ref, g_ref, d_ref, nm_ref, nv_ref):
        g = p_ref[0].astype(F32)
        for b in range(1, n):
            g = g + p_ref[b].astype(F32)
        _adamw_update(g, w_ref, m_ref, v_ref, g_ref, d_ref, nm_ref, nv_ref)

    blk = pl.BlockSpec((rows, ccols), lambda i: (i, 0))
    out = jax.ShapeDtypeStruct((r, ccols), F32)
    return pl.pallas_call(
        body, name=name, grid=(r // rows,),
        in_specs=[pl.BlockSpec((n, rows, ccols), lambda i: (0, i, 0)), blk, blk, blk],
        out_specs=[blk] * 4, out_shape=[out] * 4, compiler_params=_cp(("parallel",)))(parts, w, m, v)


def proj_fwd_gather(x, nw, scale, shift, w_shard, extras, order, tm):
    s = x.shape[0]
    ni = s // tm
    n = 1 + len(extras)
    mid = ni - 2

    def body(order_ref, x_ref, nw_ref, sc_ref, sh_ref, *refs):
        ins, o_ref, ht_ref, outs = refs[:n], refs[n], refs[n + 1], refs[n + 2:2 * n + 2]
        h_all, wbuf, send_sems, recv_sems, local_sems, load_sems = refs[2 * n + 2:]
        jj, i = pl.program_id(0), pl.program_id(1)
        x, y, c = _coords()
        me, sibling = (x, y, c), (x, y, 1 - c)
        chips = _chip_order(x, y, c)
        relayed = [(*chips[1], 1 - c), (*chips[0], 1 - c), (*chips[2], 1 - c)]

        def slot(a, dev):
            return outs[a].at[_dev_index(dev)]

        def copy(a, k, block, to, src=None):
            return pltpu.make_async_remote_copy(
                src_ref=slot(a, block) if src is None else src, dst_ref=slot(a, block),
                send_sem=send_sems.at[a, k], recv_sem=recv_sems.at[a, k], device_id=to, device_id_type=MESH)

        mine = [pltpu.make_async_copy(ins[a], slot(a, me), local_sems.at[a]) for a in range(n)]
        to_sibling = [copy(a, 0, me, sibling, src=ins[a]) for a in range(n)]
        to_chip = [[copy(a, 1 + j, me, (*chips[j], c), src=ins[a]) for a in range(n)] for j in range(2)]
        onward = [copy(a, 3, (*chips[1], c), (*chips[0], c)) for a in range(n)]
        passed = [[copy(a, 4 + j, (*ch, c), sibling) for a in range(n)] for j, ch in enumerate(chips)]
        sends = lambda a: [to_sibling[a], to_chip[0][a], to_chip[1][a], onward[a]] + [passed[j][a] for j in range(3)]

        def arrived(a, j):
            copy(a, 1 + j, (*chips[j], c), me).wait_recv()

        def load(row):
            return pltpu.make_async_copy(outs[0].at[order_ref[row]], wbuf.at[row % 2], load_sems.at[row % 2])

        @pl.when((jj == 0) & (i == 0))
        def _():
            for cp in mine:
                cp.start()
            to_sibling[0].start()
            to_chip[0][0].start()
            pltpu.make_async_copy(ins[0], wbuf.at[0], load_sems.at[0]).start()

        @pl.when((jj == 1) & (i == 0))
        def _():
            to_chip[1][0].start()

        @pl.when((jj == 4) & (i == 0))
        def _():
            for a in range(1, n):
                to_sibling[a].start()
                to_chip[0][a].start()
                to_chip[1][a].start()

        direct = {2: 0, 3: 1, 6: 2}
        relay = {4: 0, 5: 1, 7: 2}

        @pl.when((jj == 0) & (i == mid))
        def _():
            copy(0, 0, sibling, me).wait_recv()

        for row, j in direct.items():
            @pl.when((jj == row - 1) & (i == mid))
            def _(j=j):
                arrived(0, j)
                passed[j][0].start()
                if j == 1:
                    onward[0].start()

        for row, j in relay.items():
            @pl.when((jj == row - 1) & (i == mid))
            def _(j=j):
                copy(0, 4 + j, relayed[j], me).wait_recv()

        @pl.when((jj == NDEV - 1) & (i == 0))
        def _():
            for a in range(1, n):
                arrived(a, 1)
                onward[a].start()
                passed[1][a].start()
                arrived(a, 0)
                passed[0][a].start()

        @pl.when((jj < NDEV - 1) & (i == mid))
        def _():
            load(jj + 1).start()

        @pl.when(i == 0)
        def _():
            load(jj).wait()

        @pl.when(jj == 0)
        def _():
            xf = x_ref[...]
            r = lax.rsqrt(jnp.mean(xf * xf, axis=-1, keepdims=True) + EPS)
            h = (xf * r * nw_ref[...]) * (1.0 + sc_ref[...]) + sh_ref[...]
            h_all[i] = h.astype(BF16)
            ht_ref[...] = h.T.astype(BF16)

        o_ref[...] = jnp.dot(h_all[i], wbuf[jj % 2], preferred_element_type=F32).astype(BF16)

        @pl.when((jj == NDEV - 1) & (i == ni - 1))
        def _():
            for a in range(1, n):
                arrived(a, 2)
                passed[2][a].start()
            for a in range(1, n):
                copy(a, 0, sibling, me).wait_recv()
                for j in range(3):
                    copy(a, 4 + j, relayed[j], me).wait_recv()
            for a in range(n):
                mine[a].wait()
                for cp in sends(a):
                    cp.wait_send()

    any_spec = pl.BlockSpec(memory_space=pl.ANY)
    vec = pl.BlockSpec((1, D), lambda jj, i, o: (0, 0))
    outs = pl.pallas_call(
        body, name="proj_fwd_gather",
        grid_spec=pltpu.PrefetchScalarGridSpec(
            num_scalar_prefetch=1, grid=(NDEV, ni),
            in_specs=[pl.BlockSpec((tm, D), lambda jj, i, o: (jnp.where(jj == 0, i, ni - 1), 0))] + [vec] * 3
                     + [any_spec] * n,
            out_specs=[pl.BlockSpec((tm, SHARD), lambda jj, i, o: (i, o[jj])),
                       pl.BlockSpec((D, tm), lambda jj, i, o: (0, jnp.where(jj == 0, i, ni - 1)))]
                      + [any_spec] * n,
            scratch_shapes=[pltpu.VMEM((ni, tm, D), BF16), pltpu.VMEM((2, D, SHARD), BF16),
                            pltpu.SemaphoreType.DMA((n, 7)), pltpu.SemaphoreType.DMA((n, 7)),
                            pltpu.SemaphoreType.DMA((n,)), pltpu.SemaphoreType.DMA((2,))]),
        out_shape=[jax.ShapeDtypeStruct((s, NIN), BF16), jax.ShapeDtypeStruct((D, s), BF16),
                   jax.ShapeDtypeStruct((NDEV, D, SHARD), BF16)]
                  + [jax.ShapeDtypeStruct((NDEV,) + e.shape, e.dtype) for e in extras],
        compiler_params=_cp(("arbitrary", "arbitrary"), 56))(order, x, nw, scale, shift, w_shard, *extras)
    return outs[0], outs[1], outs[2], outs[3:]


def proj_bwd(ht, dproj, wg, smalls, order, x, dy, nw, scale, tt):
    s = dproj.shape[0]
    nk = s // tt
    n = len(smalls)
    rows_per_step = tt // nk
    last = 2 * NDEV

    def body(order_ref, ht_ref, dp_ref, w_ref, x_ref, dy_ref, nw_ref, sc_ref, *rest):
        small_in = rest[:n]
        gx_ref, st_ref, gw_ref, rwin_ref = rest[n:n + 4]
        small_out = rest[n + 4:2 * n + 4]
        acc, stage, dh, send_sems, recv_sems, local_sems, stage_sems = rest[2 * n + 4:]
        t, k = pl.program_id(0), pl.program_id(1)
        me_xyc = _coords()
        me = _dev_index(me_xyc)
        peers = [_flip(me_xyc, f) for f in FLIPS]

        def exchange(a, kf, src_arr, dst_arr):
            pid = _dev_index(peers[kf])
            mk = lambda dst: pltpu.make_async_remote_copy(
                src_ref=src_arr.at[pid], dst_ref=dst, send_sem=send_sems.at[a, kf], recv_sem=recv_sems.at[a, kf],
                device_id=peers[kf], device_id_type=MESH)
            return mk(dst_arr.at[me]), mk(dst_arr.at[pid])

        small_pairs = [exchange(1 + a, kf, small_in[a], small_out[a]) for kf in range(7) for a in range(n)]
        small_own = [pltpu.make_async_copy(small_in[a].at[me], small_out[a].at[me], local_sems.at[1 + a])
                     for a in range(n)]
        win_pairs = [exchange(0, kf, gw_ref, rwin_ref) for kf in range(7)]
        win_own = pltpu.make_async_copy(gw_ref.at[me], rwin_ref.at[me], local_sems.at[0])

        def to_hbm(jj):
            slab = me if jj == 7 else _dev_index(peers[jj])
            return pltpu.make_async_copy(stage.at[jj % 2], gw_ref.at[slab], stage_sems.at[jj % 2])

        @pl.when((t == 0) & (k == 0))
        def _():
            for cp in small_own:
                cp.start()
            for send, _ in small_pairs:
                send.start()

        @pl.when(t < NDEV)
        def _():
            p = jnp.dot(ht_ref[...], dp_ref[...], preferred_element_type=F32)

            @pl.when(k == 0)
            def _():
                acc[...] = p

            @pl.when(k > 0)
            def _():
                acc[...] += p

        for jj in range(NDEV):
            @pl.when((t == jj) & (k == nk - 1))
            def _(jj=jj):
                stage[jj % 2] = acc[...].astype(BF16)
                to_hbm(jj).start()

            @pl.when((t == jj + 1) & (k == 1))
            def _(jj=jj):
                to_hbm(jj).wait()
                if jj < 7:
                    win_pairs[jj][0].start()
                else:
                    win_own.start()

        def matmul_step():
            p = lax.dot_general(dp_ref[...], w_ref[...], NT, preferred_element_type=F32)
            slot = t % 2
            dh[slot] = jnp.where(k == 0, p, dh[slot] + p)

        def norm_step():
            g = dh.at[(t + 1) % 2][pl.ds(pl.multiple_of(k * rows_per_step, rows_per_step), rows_per_step), :]
            xf = x_ref[...]
            r = lax.rsqrt(jnp.mean(xf * xf, axis=-1, keepdims=True) + EPS)
            xh = xf * r
            dn = g * (1.0 + sc_ref[...])
            dxh = dn * nw_ref[...]
            gx_ref[...] = dy_ref[...] + r * (dxh - xh * jnp.mean(dxh * xh, axis=-1, keepdims=True))
            st_ref[0:1, :] += jnp.sum(g, axis=0, keepdims=True)
            st_ref[1:2, :] += jnp.sum(g * xh * nw_ref[...], axis=0, keepdims=True)
            st_ref[2:3, :] += jnp.sum(dn * xh, axis=0, keepdims=True)

        @pl.when((t == 0) & (k == 0))
        def _():
            st_ref[...] = jnp.zeros_like(st_ref)

        @pl.when(t == NDEV)
        def _():
            matmul_step()

        @pl.when((t > NDEV) & (t < last))
        def _():
            matmul_step()
            norm_step()

        @pl.when(t == last)
        def _():
            norm_step()

        @pl.when((t == last) & (k == nk - 1))
        def _():
            for _, recv in win_pairs + small_pairs:
                recv.wait_recv()
            for send, _ in win_pairs + small_pairs:
                send.wait_send()
            win_own.wait()
            for cp in small_own:
                cp.wait()

    any_spec = pl.BlockSpec(memory_space=pl.ANY)
    first = lambda t: t < NDEV
    slab = lambda t, k: jnp.where(t == last, NDEV - 1, k)
    chunk = pl.BlockSpec((rows_per_step, D), lambda t, k, o: (jnp.maximum((t - NDEV - 1) * nk + k, 0), 0))
    vec = pl.BlockSpec((1, D), lambda t, k, o: (0, 0))
    outs = pl.pallas_call(
        body, name="proj_bwd",
        grid_spec=pltpu.PrefetchScalarGridSpec(
            num_scalar_prefetch=1, grid=(last + 1, nk),
            in_specs=[pl.BlockSpec((D, tt), lambda t, k, o: (0, jnp.where(first(t), k, nk - 1))),
                      pl.BlockSpec((tt, SHARD), lambda t, k, o: (jnp.where(first(t), k, jnp.minimum(t, last - 1) - NDEV),
                                                                 jnp.where(first(t), o[jnp.minimum(t, NDEV - 1)],
                                                                           slab(t, k)))),
                      pl.BlockSpec((None, D, SHARD), lambda t, k, o: (jnp.where(first(t), 0, slab(t, k)), 0, 0)),
                      chunk, chunk, vec, vec]
                     + [any_spec] * n,
            out_specs=[chunk, pl.BlockSpec((8, D), lambda t, k, o: (0, 0))] + [any_spec] * (2 + n),
            scratch_shapes=[pltpu.VMEM((D, SHARD), F32), pltpu.VMEM((2, D, SHARD), BF16),
                            pltpu.VMEM((2, tt, D), F32),
                            pltpu.SemaphoreType.DMA((1 + n, 7)), pltpu.SemaphoreType.DMA((1 + n, 7)),
                            pltpu.SemaphoreType.DMA((1 + n,)), pltpu.SemaphoreType.DMA((2,))]),
        out_shape=[jax.ShapeDtypeStruct((s, D), F32), jax.ShapeDtypeStruct((8, D), F32),
                   jax.ShapeDtypeStruct((NDEV, D, SHARD), BF16), jax.ShapeDtypeStruct((NDEV, D, SHARD), BF16)]
                  + [jax.ShapeDtypeStruct(a.shape, a.dtype) for a in smalls],
        compiler_params=_cp(("arbitrary", "arbitrary"), 56))(order, ht, dproj, wg, x, dy, nw, scale, *smalls)
    return outs[0], outs[1], outs[3], outs[4:]


def matmuls_tn(pairs, name, tk):
    s = pairs[0][0].shape[0]
    nk = s // tk
    n = len(pairs)
    shapes = [(a.shape[1], b.shape[1]) for a, b in pairs]

    def body(*refs):
        ins, outs, accs = refs[:2 * n], refs[2 * n:3 * n], refs[3 * n:]
        k = pl.program_id(0)
        for j in range(n):
            p = lax.dot_general(ins[2 * j][...], ins[2 * j + 1][...], TN, preferred_element_type=F32)
            accs[j][...] = jnp.where(k == 0, p, accs[j][...] + p)

        @pl.when(k == nk - 1)
        def _():
            for j in range(n):
                outs[j][...] = accs[j][...].astype(BF16)

    return pl.pallas_call(
        body, name=name, grid=(nk,),
        in_specs=[pl.BlockSpec((tk, t.shape[1]), lambda k: (k, 0)) for pair in pairs for t in pair],
        out_specs=[pl.BlockSpec(sh, lambda k: (0, 0)) for sh in shapes],
        out_shape=[jax.ShapeDtypeStruct(sh, BF16) for sh in shapes],
        scratch_shapes=[pltpu.VMEM(sh, F32) for sh in shapes],
        compiler_params=_cp(("arbitrary",), 56))(*[t for pair in pairs for t in pair])


def _head_matrices():
    lane = lax.broadcasted_iota(jnp.int32, (CB, CB), 0)
    col = lax.broadcasted_iota(jnp.int32, (CB, CB), 1)
    same = (lane // HD == col // HD).astype(BF16)
    lane_c = lax.broadcasted_iota(jnp.int32, (CB, LANES), 0)
    col_c = lax.broadcasted_iota(jnp.int32, (CB, LANES), 1)
    total = (lane_c // HD == col_c).astype(BF16)
    lane_e = lax.broadcasted_iota(jnp.int32, (LANES, CB), 0)
    col_e = lax.broadcasted_iota(jnp.int32, (LANES, CB), 1)
    expand = (lane_e == col_e // HD).astype(BF16)
    return same, total, expand


def _head_sum(x, m_ref):
    return jnp.dot(x.astype(BF16), m_ref[...], preferred_element_type=F32)


def _dot_hilo(x, m_ref):
    hi = x.astype(BF16)
    lo = (x - hi.astype(F32)).astype(BF16)
    return (jnp.dot(hi, m_ref[...], preferred_element_type=F32)
            + jnp.dot(lo, m_ref[...], preferred_element_type=F32))


def _to_residue_major(val, buf, out_ref, dil):
    rows = out_ref.shape[1]
    for k in range(val.shape[1] // LANES):
        lanes = slice(k * LANES, (k + 1) * LANES)
        buf[k] = val[:, lanes]
        for r in range(dil):
            out_ref[r, :, lanes] = buf.at[k][pl.ds(r, rows, stride=dil), :].astype(out_ref.dtype)


def _from_residue_major(ref, buf, dil):
    if dil == 1:
        return ref[0].astype(F32)
    rows, chunks = ref.shape[1], ref.shape[2] // LANES
    for k in range(chunks):
        for r in range(dil):
            buf.at[k][pl.ds(r, rows, stride=dil), :] = ref[r, :, k * LANES:(k + 1) * LANES].astype(F32)
    return jnp.concatenate([buf[k] for k in range(chunks)], axis=1)


def qkv_prep(proj, qw8, kw8, same, tm):
    s = proj.shape[0]
    items = []
    for g, d in enumerate(DILATIONS):
        items += [(g, "q", CB_Q + g, d), (g, "k", CB_K + g, d)] + ([(g, "v", CB_V + g, d)] if d > 1 else [])
    n = len(items)

    def body(*refs):
        ins, (qw_ref, kw_ref, same_ref), outs, buf = refs[:n], refs[n:n + 3], refs[n + 3:2 * n + 3], refs[-1]
        for idx, (_, kind, _, dil) in enumerate(items):
            val = ins[idx][...].astype(F32)
            if kind != "v":
                r = lax.rsqrt(_head_sum(val * val, same_ref) * (1.0 / HD) + EPS)
                val = val * r * (qw_ref if kind == "q" else kw_ref)[...]
            if dil == 1:
                outs[idx][0] = val.astype(BF16)
            else:
                _to_residue_major(val, buf, outs[idx], dil)

    full = lambda a: pl.BlockSpec(a.shape, lambda i: (0, 0))
    outs = pl.pallas_call(
        body, name="qkv_prep", grid=(s // tm,),
        in_specs=[pl.BlockSpec((tm, CB), lambda i, cb=cb: (i, cb)) for _, _, cb, _ in items]
                 + [full(qw8), full(kw8), full(same)],
        out_specs=[pl.BlockSpec((d, tm // d, CB), lambda i: (0, i, 0)) for _, _, _, d in items],
        out_shape=[jax.ShapeDtypeStruct((d, s // d, CB), BF16) for _, _, _, d in items],
        scratch_shapes=[pltpu.VMEM((CB // LANES, tm, LANES), F32)],
        compiler_params=_cp(("parallel",)))(*([proj] * n), qw8 * (HD ** -0.5), kw8, same)
    srcs = [[None, None, (proj, CB_V + g)] for g in range(len(DILATIONS))]
    for (g, kind, _, _), o in zip(items, outs):
        srcs[g]["qkv".index(kind)] = (o.reshape(s, CB), 0)
    return srcs


def stats_prep(da, lc, dc, tm):
    s = da.shape[0]

    def body(da_ref, lc_ref, dc_ref, *refs):
        outs, buf = list(refs[:-1]), refs[-1]
        for dil in DILATIONS:
            rows = tm // dil
            if dil > 1:
                _to_residue_major(da_ref[...].astype(F32), buf, outs.pop(0), dil)
            for src in (lc_ref, dc_ref):
                dst = outs.pop(0) if dil > 1 else None
                dst_t = outs.pop(0)
                buf[0] = src[...]
                for r in range(dil):
                    piece = buf.at[0][pl.ds(r, rows, stride=dil), :] if dil > 1 else buf[0]
                    if dil > 1:
                        dst[r] = piece
                    dst_t[r] = piece.T[0:NH, :]

    row = lambda w: pl.BlockSpec((tm, w), lambda i: (i, 0))
    out_specs, out_shape = [], []
    for dil in DILATIONS:
        rm = lambda w, dil=dil: (pl.BlockSpec((dil, tm // dil, w), lambda i: (0, i, 0)),
                                 jax.ShapeDtypeStruct((dil, s // dil, w), BF16 if w == CB else F32))
        tr = (pl.BlockSpec((dil, NH, tm // dil), lambda i: (0, 0, i)), jax.ShapeDtypeStruct((dil, NH, s // dil), F32))
        group = ([rm(CB)] if dil > 1 else []) + ([rm(LANES), tr, rm(LANES), tr] if dil > 1 else [tr, tr])
        out_specs += [sp for sp, _ in group]
        out_shape += [sh for _, sh in group]
    outs = list(pl.pallas_call(
        body, name="stats_prep", grid=(s // tm,),
        in_specs=[row(CB), row(LANES), row(LANES)], out_specs=out_specs, out_shape=out_shape,
        scratch_shapes=[pltpu.VMEM((CB // LANES, tm, LANES), F32)],
        compiler_params=_cp(("parallel",)))(da, lc, dc))
    res = []
    for dil in DILATIONS:
        flat_t = lambda t, dil=dil: t.reshape(dil * NH, s // dil)
        if dil == 1:
            lt, dt = outs.pop(0), outs.pop(0)
            res.append((da, lc, dc, flat_t(lt), flat_t(dt)))
        else:
            dap, lcp, lt, dcp, dt = (outs.pop(0) for _ in range(5))
            res.append((dap.reshape(s, CB), lcp.reshape(s, LANES), dcp.reshape(s, LANES), flat_t(lt), flat_t(dt)))
    return res


def qkv_grads_to_dproj(dproj, proj, grads, qw8, kw8, same, tm):
    s = dproj.shape[0]
    ni = s // tm
    flat = [(t.reshape(d, s // d, CB), d, kind, 3 * kind + g)
            for g, d in enumerate(DILATIONS) for kind, t in enumerate(grads[g])]
    nf = len(flat)
    nraw = 2 * len(DILATIONS)

    def body(*refs):
        dp_hbm, raws, ins = refs[nraw + nf + 4], refs[1:1 + nraw], refs[1 + nraw:1 + nraw + nf]
        qw_ref, kw_ref, same_ref = refs[1 + nraw + nf:4 + nraw + nf]
        gw_ref, stage, buf, sems = refs[5 + nraw + nf:]
        i = pl.program_id(0)
        slot = i % 2

        def slab(step, sl):
            return pltpu.make_async_copy(
                stage.at[sl], dp_hbm.at[pl.ds(pl.multiple_of(step * tm, tm), tm), pl.ds(CB_Q * CB, 9 * CB)],
                sems.at[sl])

        @pl.when(i == 0)
        def _():
            gw_ref[...] = jnp.zeros_like(gw_ref)

        @pl.when(i >= 2)
        def _():
            slab(i - 2, slot).wait()

        for ref, (_, d, kind, jj) in zip(ins, flat):
            cols = slice(jj * CB, (jj + 1) * CB)
            dn = _from_residue_major(ref, buf, d)
            if kind == 2:
                stage[slot, :, cols] = dn.astype(BF16)
                continue
            t = raws[jj][...].astype(F32)
            r = lax.rsqrt(_head_sum(t * t, same_ref) * (1.0 / HD) + EPS)
            xh = t * r
            gw_ref[kind:kind + 1, :] += jnp.sum(dn * xh, axis=0, keepdims=True)
            dxh = dn * (qw_ref if kind == 0 else kw_ref)[...]
            mean = _head_sum(dxh * xh, same_ref) * (1.0 / HD)
            stage[slot, :, cols] = (r * (dxh - xh * mean)).astype(BF16)
        slab(i, slot).start()

        @pl.when(i == ni - 1)
        def _():
            slab(i - 1, 1 - slot).wait()
            slab(i, slot).wait()

    full = lambda a: pl.BlockSpec(a.shape, lambda i: (0, 0))
    any_spec = pl.BlockSpec(memory_space=pl.ANY)
    return pl.pallas_call(
        body, name="qkv_grads_to_dproj", grid=(ni,),
        in_specs=[any_spec] + [pl.BlockSpec((tm, CB), lambda i, jb=jb: (i, CB_Q + jb)) for jb in range(nraw)]
                 + [pl.BlockSpec((d, tm // d, CB), lambda i: (0, i, 0)) for _, d, _, _ in flat]
                 + [full(qw8), full(kw8), full(same)],
        out_specs=[any_spec, pl.BlockSpec((8, CB), lambda i: (0, 0))],
        out_shape=[jax.ShapeDtypeStruct((s, NIN), BF16), jax.ShapeDtypeStruct((8, CB), F32)],
        input_output_aliases={0: 0},
        scratch_shapes=[pltpu.VMEM((2, tm, 9 * CB), BF16), pltpu.VMEM((CB // LANES, tm, LANES), F32),
                        pltpu.SemaphoreType.DMA((2,))],
        compiler_params=_cp(("arbitrary",)))(
            dproj, *([proj] * nraw), *[t for t, _, _, _ in flat], qw8, kw8, same)


def _lane_lo():
    return lax.broadcasted_iota(jnp.int32, (1, 2 * HD), 1) < HD


def _stack_heads(t, lo):
    zero = jnp.zeros_like(t)
    return jnp.concatenate([jnp.where(lo, t, zero), jnp.where(lo, zero, t)], axis=0)


def _masks(other_ok):
    qi = lax.broadcasted_iota(jnp.int32, (QB, QB), 0)
    kj = lax.broadcasted_iota(jnp.int32, (QB, QB), 1)
    return (kj >= qi) & other_ok, kj <= qi


MAX_SUB = 8


def _attn_specs(nb, dil, sub):
    steps = nb // sub
    main = lambda cb, w=CB: pl.BlockSpec((sub * QB, w), lambda r, s: (r * steps + s, cb))
    prev = lambda cb: pl.BlockSpec((QB, CB), lambda r, s: (jnp.maximum(r * nb + sub * s - 1, 0), cb))
    nxt = lambda cb: pl.BlockSpec((QB, CB), lambda r, s: (jnp.minimum(r * nb + sub * (s + 1), dil * nb - 1), cb))
    return main, prev, nxt


def attn_fwd(q_src, k_src, v_src, g, dil):
    s = q_src[0].shape[0]
    nb = s // dil // QB
    sub = min(MAX_SUB, nb)
    main, prev, _ = _attn_specs(nb, dil, sub)

    def body(q_ref, kp_ref, k_ref, vp_ref, v_ref, o_ref, l_ref, kbuf, vbuf):
        step = pl.program_id(1)
        kbuf[0:QB], kbuf[QB:] = kp_ref[...], k_ref[...]
        vbuf[0:QB], vbuf[QB:] = vp_ref[...], v_ref[...]
        lo = _lane_lo()
        head_lane = lax.broadcasted_iota(jnp.int32, (1, LANES), 1)

        def block(j, carry):
            r0 = pl.multiple_of(j * QB, QB)
            rows, krows = pl.ds(r0, QB), pl.ds(r0, 2 * QB)
            m_prev, m_cur = _masks(step * sub + j > 0)
            mask = jnp.concatenate([m_prev, m_cur], axis=1)
            mask = jnp.concatenate([mask, mask], axis=0)
            lses = jnp.zeros((QB, LANES), F32)
            for i in range(NH // 2):
                sl = slice(2 * HD * i, 2 * HD * (i + 1))
                qs, ks, vv = q_ref[rows, sl], kbuf[krows, sl], vbuf[krows, sl]
                sc = lax.dot_general(_stack_heads(qs, lo), ks, NT, preferred_element_type=F32)
                sc = jnp.where(mask, sc, NEG)
                mx = jnp.max(sc, axis=-1, keepdims=True)
                p = jnp.exp(sc - mx)
                den = jnp.sum(p, axis=-1, keepdims=True)
                o = jnp.dot(p.astype(BF16), vv, preferred_element_type=F32) * (1.0 / den)
                lse = mx + jnp.log(den)
                o_ref[rows, sl] = jnp.where(lo, o[:QB], o[QB:]).astype(BF16)
                lses = jnp.where(head_lane == 2 * i, lse[:QB], jnp.where(head_lane == 2 * i + 1, lse[QB:], lses))
            l_ref[rows, :] = lses
            return carry

        lax.fori_loop(0, sub, block, 0, unroll=True)

    return pl.pallas_call(
        body, name=f"attn_fwd_g{g}", grid=(dil, nb // sub),
        in_specs=[main(q_src[1]), prev(k_src[1]), main(k_src[1]), prev(v_src[1]), main(v_src[1])],
        out_specs=[main(0), main(0, LANES)],
        out_shape=[jax.ShapeDtypeStruct((s, CB), BF16), jax.ShapeDtypeStruct((s, LANES), F32)],
        scratch_shapes=[pltpu.VMEM(((sub + 1) * QB, CB), BF16)] * 2,
        compiler_params=_cp(("parallel", "parallel")))(q_src[0], k_src[0], k_src[0], v_src[0], v_src[0])


def attn_bwd_q(q_src, k_src, v_src, da, lc, dc, g, dil):
    s = q_src[0].shape[0]
    nb = s // dil // QB
    sub = min(MAX_SUB, nb)
    main, prev, _ = _attn_specs(nb, dil, sub)

    def body(q_ref, kp_ref, k_ref, vp_ref, v_ref, da_ref, l_ref, d_ref, dq_ref, kbuf, vbuf):
        step = pl.program_id(1)
        kbuf[0:QB], kbuf[QB:] = kp_ref[...], k_ref[...]
        vbuf[0:QB], vbuf[QB:] = vp_ref[...], v_ref[...]
        lo = _lane_lo()

        def block(j, carry):
            r0 = pl.multiple_of(j * QB, QB)
            rows, krows = pl.ds(r0, QB), pl.ds(r0, 2 * QB)
            m_prev, m_cur = _masks(step * sub + j > 0)
            mask = jnp.concatenate([m_prev, m_cur], axis=1)
            mask = jnp.concatenate([mask, mask], axis=0)
            lcols, dcols = l_ref[rows, :], d_ref[rows, :]
            for i in range(NH // 2):
                sl = slice(2 * HD * i, 2 * HD * (i + 1))
                qs, ks, vv, da2 = q_ref[rows, sl], kbuf[krows, sl], vbuf[krows, sl], da_ref[rows, sl]
                pair = lambda t: jnp.concatenate([t[:, 2 * i:2 * i + 1], t[:, 2 * i + 1:2 * i + 2]], axis=0)
                sc = lax.dot_general(_stack_heads(qs, lo), ks, NT, preferred_element_type=F32)
                sc = jnp.where(mask, sc, NEG)
                p = jnp.exp(sc - pair(lcols))
                dp = lax.dot_general(_stack_heads(da2, lo), vv, NT, preferred_element_type=F32)
                ds = p * (dp - pair(dcols))
                dq = jnp.dot(ds.astype(BF16), ks, preferred_element_type=F32)
                dq_ref[rows, sl] = (jnp.where(lo, dq[:QB], dq[QB:]) * (HD ** -0.5)).astype(BF16)
            return carry

        lax.fori_loop(0, sub, block, 0, unroll=True)

    return pl.pallas_call(
        body, name=f"attn_bwd_q_g{g}", grid=(dil, nb // sub),
        in_specs=[main(q_src[1]), prev(k_src[1]), main(k_src[1]), prev(v_src[1]), main(v_src[1]),
                  main(0), main(0, LANES), main(0, LANES)],
        out_specs=main(0), out_shape=jax.ShapeDtypeStruct((s, CB), BF16),
        scratch_shapes=[pltpu.VMEM(((sub + 1) * QB, CB), BF16)] * 2,
        compiler_params=_cp(("parallel", "parallel")))(
            q_src[0], k_src[0], k_src[0], v_src[0], v_src[0], da, lc, dc)


def attn_bwd_kv(q_src, k_src, v_src, da, lt, dt, g, dil):
    s = q_src[0].shape[0]
    nb = s // dil // QB
    sub = min(MAX_SUB, nb)
    main, _, nxt = _attn_specs(nb, dil, sub)

    def body(k_ref, v_ref, q_ref, qn_ref, da_ref, dan_ref, l_ref, ln_ref, d_ref, dn_ref, dk_ref, dv_ref,
             qbuf, dabuf, lbuf, dbuf):
        step = pl.program_id(1)
        qbuf[0:sub * QB], qbuf[sub * QB:] = q_ref[...], qn_ref[...]
        dabuf[0:sub * QB], dabuf[sub * QB:] = da_ref[...], dan_ref[...]
        for c in range(sub):
            lbuf[c], dbuf[c] = l_ref[:, c * QB:(c + 1) * QB], d_ref[:, c * QB:(c + 1) * QB]
        lbuf[sub], dbuf[sub] = ln_ref[...], dn_ref[...]
        lo = _lane_lo()
        kj = lax.broadcasted_iota(jnp.int32, (QB, QB), 0)
        qi = lax.broadcasted_iota(jnp.int32, (QB, QB), 1)

        def block(j, carry):
            r0 = pl.multiple_of(j * QB, QB)
            rows, qrows = pl.ds(r0, QB), pl.ds(r0, 2 * QB)
            mask = jnp.concatenate([kj <= qi, (kj >= qi) & (step * sub + j < nb - 1)], axis=1)
            mask = jnp.concatenate([mask, mask], axis=1)
            lrow = jnp.concatenate([lbuf[j], lbuf[j + 1]], axis=1)
            drow = jnp.concatenate([dbuf[j], dbuf[j + 1]], axis=1)
            for i in range(NH // 2):
                sl = slice(2 * HD * i, 2 * HD * (i + 1))
                q2, da2 = _stack_heads(qbuf[qrows, sl], lo), _stack_heads(dabuf[qrows, sl], lo)
                ks, vv = k_ref[rows, sl], v_ref[rows, sl]
                pair = lambda t: jnp.concatenate([t[2 * i:2 * i + 1, :], t[2 * i + 1:2 * i + 2, :]], axis=1)
                sc = lax.dot_general(ks, q2, NT, preferred_element_type=F32)
                sc = jnp.where(mask, sc, NEG)
                p = jnp.exp(sc - pair(lrow))
                dp = lax.dot_general(vv, da2, NT, preferred_element_type=F32)
                ds = p * (dp - pair(drow))
                dv_ref[rows, sl] = jnp.dot(p.astype(BF16), da2, preferred_element_type=F32).astype(BF16)
                dk_ref[rows, sl] = jnp.dot(ds.astype(BF16), q2, preferred_element_type=F32).astype(BF16)
            return carry

        lax.fori_loop(0, sub, block, 0, unroll=True)

    steps = nb // sub
    t_main = pl.BlockSpec((NH, sub * QB), lambda r, s: (r, s))
    t_nxt = pl.BlockSpec((NH, QB), lambda r, s: (r, jnp.minimum(sub * (s + 1), nb - 1)))
    out = jax.ShapeDtypeStruct((s, CB), BF16)
    return pl.pallas_call(
        body, name=f"attn_bwd_kv_g{g}", grid=(dil, steps),
        in_specs=[main(k_src[1]), main(v_src[1]), main(q_src[1]), nxt(q_src[1]),
                  main(0), nxt(0), t_main, t_nxt, t_main, t_nxt],
        out_specs=[main(0), main(0)], out_shape=[out, out],
        scratch_shapes=[pltpu.VMEM(((sub + 1) * QB, CB), BF16)] * 2 + [pltpu.VMEM((sub + 1, NH, QB), F32)] * 2,
        compiler_params=_cp(("parallel", "parallel")))(
            k_src[0], v_src[0], q_src[0], q_src[0], da, da, lt, lt, dt, dt)


def _conv_taps(u, u_prev, first):
    tm = u.shape[0]
    row = lax.broadcasted_iota(jnp.int32, (tm, 1), 0)
    up = jnp.where(first, 0.0, u_prev)
    u1 = jnp.where(row == 0, up[HALO - 1:HALO, :], pltpu.roll(u, 1, 0))
    u2 = jnp.where(row == 0, up[HALO - 2:HALO - 1, :],
                   jnp.where(row == 1, up[HALO - 1:HALO, :], pltpu.roll(u, 2, 0)))
    return u1, u2


def mid_fwd(proj, o_g, lse_g, conv_w, expand, tm):
    s = proj.shape[0]
    hb = tm // HALO

    def body(ba_ref, ca_ref, xa_ref, za_ref, cah_ref, xah_ref, zb_ref,
             o0, o1, o2, l0, l1, l2, w_ref, exp_ref, ya_ref, yb_ref, at_ref, lc_ref, buf_o, buf_l):
        first = pl.program_id(0) == 0
        u = ca_ref[...].astype(F32) * xa_ref[...].astype(F32)
        u1, u2 = _conv_taps(u, cah_ref[...].astype(F32) * xah_ref[...].astype(F32), first)
        conv = w_ref[0:1, :] * u2 + w_ref[1:2, :] * u1 + w_ref[2:3, :] * u
        ya_ref[...] = (ba_ref[...].astype(F32) * conv * _silu(za_ref[...].astype(F32))).astype(BF16)
        ls = [_from_residue_major(l, buf_l.at[g], d) for g, (l, d) in enumerate(zip((l0, l1, l2), DILATIONS))]
        mx = jnp.maximum(jnp.maximum(ls[0], ls[1]), ls[2])
        es = [jnp.exp(l - mx) for l in ls]
        den = es[0] + es[1] + es[2]
        attn = jnp.zeros((tm, CB), F32)
        for e, o, d in zip(es, (o0, o1, o2), DILATIONS):
            attn = attn + _dot_hilo(e / den, exp_ref) * _from_residue_major(o, buf_o, d)
        at_ref[...] = attn
        lc_ref[...] = mx + jnp.log(den)
        yb_ref[...] = (attn * _silu(zb_ref[...].astype(F32))).astype(BF16)

    col = lambda j: pl.BlockSpec((tm, D), lambda i: (i, j))
    halo = lambda j: pl.BlockSpec((HALO, D), lambda i: (jnp.maximum(i * hb - 1, 0), j))
    loc = lambda w: pl.BlockSpec((tm, w), lambda i: (i, 0))
    rm = lambda w: [pl.BlockSpec((d, tm // d, w), lambda i: (0, i, 0)) for d in DILATIONS]
    rm_view = lambda ts, w: [t.reshape(d, s // d, w) for t, d in zip(ts, DILATIONS)]
    return pl.pallas_call(
        body, name="mid_fwd", grid=(s // tm,),
        in_specs=[col(0), col(1), col(2), col(3), halo(1), halo(2),
                  pl.BlockSpec((tm, CB), lambda i: (i, CB_ZB))] + rm(CB) + rm(LANES)
                 + [pl.BlockSpec((3, D), lambda i: (0, 0)), pl.BlockSpec(expand.shape, lambda i: (0, 0))],
        out_specs=[loc(D), loc(CB), loc(CB), loc(LANES)],
        out_shape=[jax.ShapeDtypeStruct((s, D), BF16), jax.ShapeDtypeStruct((s, CB), BF16),
                   jax.ShapeDtypeStruct((s, CB), F32), jax.ShapeDtypeStruct((s, LANES), F32)],
        scratch_shapes=[pltpu.VMEM((CB // LANES, tm, LANES), F32), pltpu.VMEM((3, 1, tm, LANES), F32)],
        compiler_params=_cp(("parallel",)))(
            proj, proj, proj, proj, proj, proj, proj, *rm_view(o_g, CB), *rm_view(lse_g, LANES), conv_w, expand)


def tail(proj, ya, yb, attn, x, target, gate, pa_w, pb_w, wo_w, total, conv_w, tm):
    s = proj.shape[0]
    ni = s // tm
    hb = tm // HALO
    nlate = NIN - CB_ZB * CB
    nearly = 4 * D

    def body(ya_ref, yb_ref, ga_ref, gb_ref, zb_ref, at_ref, x_ref, t_ref, gate_ref, pa_ref, pb_ref, wo_ref,
             tot_ref, ba_ref, ca_ref, xa_ref, za_ref, cah_ref, xah_ref, cw_ref,
             dp_hbm, dy_ref, da_ref, dc_ref, mg_ref, do_ref, dpa_ref, dpb_ref, st_ref, gwc_ref,
             stage, dconv_next, sems):
        step = pl.program_id(0)
        i = ni - 1 - step
        slot = step % 2

        def slabs(at_step, sl):
            rows = pl.ds(pl.multiple_of((ni - 1 - at_step) * tm, tm), tm)
            return (pltpu.make_async_copy(stage.at[sl, :, 0:nearly], dp_hbm.at[rows, pl.ds(0, nearly)],
                                          sems.at[sl, 0]),
                    pltpu.make_async_copy(stage.at[sl, :, nearly:], dp_hbm.at[rows, pl.ds(CB_ZB * CB, nlate)],
                                          sems.at[sl, 1]))

        @pl.when(step == 0)
        def _():
            st_ref[...] = jnp.zeros_like(st_ref)
            gwc_ref[...] = jnp.zeros_like(gwc_ref)
            dconv_next[...] = jnp.zeros_like(dconv_next)

        @pl.when(step >= 2)
        def _():
            for cp in slabs(step - 2, slot):
                cp.wait()

        gate_v = gate_ref[...]
        pa = jnp.dot(ya_ref[...], pa_ref[...], preferred_element_type=F32)
        pb = jnp.dot(yb_ref[...], pb_ref[...], preferred_element_type=F32)
        sa = jax.nn.sigmoid(ga_ref[...].astype(F32))
        sb = jax.nn.sigmoid(gb_ref[...].astype(F32))
        merged = (sa * pa + sb * pb).astype(BF16)
        mg_ref[...] = merged
        out = jnp.dot(merged, wo_ref[...], preferred_element_type=F32)
        err = x_ref[...] + gate_v * out - t_ref[...]
        dy = err * (1.0 / D)
        dy_ref[...] = dy
        st_ref[0:1, :] += jnp.sum(dy * out, axis=0, keepdims=True)
        st_ref[1:2, :] += jnp.sum(err * err, axis=0, keepdims=True)
        dout = (gate_v * dy).astype(BF16)
        do_ref[...] = dout
        dmg = lax.dot_general(dout, wo_ref[...], NT, preferred_element_type=F32)
        dpa = (dmg * sa).astype(BF16)
        dpb = (dmg * sb).astype(BF16)
        dpa_ref[...] = dpa
        dpb_ref[...] = dpb
        late = nearly
        stage[slot, :, late + CB:late + CB + D] = (dmg * pa * sa * (1.0 - sa)).astype(BF16)
        stage[slot, :, late + CB + D:] = (dmg * pb * sb * (1.0 - sb)).astype(BF16)
        dya = lax.dot_general(dpa, pa_ref[...], NT, preferred_element_type=F32)
        dyb = lax.dot_general(dpb, pb_ref[...], NT, preferred_element_type=F32)
        zb = zb_ref[...].astype(F32)
        sg = jax.nn.sigmoid(zb)
        attn_v = at_ref[...]
        dattn = dyb * (zb * sg)
        da_ref[...] = dattn.astype(BF16)
        stage[slot, :, late:late + CB] = (dyb * attn_v * (sg * (1.0 + zb * (1.0 - sg)))).astype(BF16)
        dc_ref[...] = _dot_hilo(dattn * attn_v, tot_ref)

        ba, ca, xa, za = (t[...].astype(F32) for t in (ba_ref, ca_ref, xa_ref, za_ref))
        u = ca * xa
        u1, u2 = _conv_taps(u, cah_ref[...].astype(F32) * xah_ref[...].astype(F32), i == 0)
        w0, w1, w2 = cw_ref[0:1, :], cw_ref[1:2, :], cw_ref[2:3, :]
        conv = w0 * u2 + w1 * u1 + w2 * u
        sga = jax.nn.sigmoid(za)
        sza = za * sga
        dconv = dya * ba * sza
        dcn = dconv_next[...]
        rowi = lax.broadcasted_iota(jnp.int32, (tm, 1), 0)
        d1 = jnp.where(rowi == tm - 1, dcn[0:1, :], pltpu.roll(dconv, tm - 1, 0))
        d2 = jnp.where(rowi == tm - 2, dcn[0:1, :],
                       jnp.where(rowi == tm - 1, dcn[1:2, :], pltpu.roll(dconv, tm - 2, 0)))
        du = w2 * dconv + w1 * d1 + w0 * d2
        stage[slot, :, 0:D] = (dya * conv * sza).astype(BF16)
        stage[slot, :, D:2 * D] = (du * xa).astype(BF16)
        stage[slot, :, 2 * D:3 * D] = (du * ca).astype(BF16)
        stage[slot, :, 3 * D:4 * D] = (dya * ba * conv * (sga * (1.0 + za * (1.0 - sga)))).astype(BF16)
        gwc_ref[0:1, :] += jnp.sum(dconv * u2, axis=0, keepdims=True)
        gwc_ref[1:2, :] += jnp.sum(dconv * u1, axis=0, keepdims=True)
        gwc_ref[2:3, :] += jnp.sum(dconv * u, axis=0, keepdims=True)
        dconv_next[...] = dconv[0:8, :]

        for cp in slabs(step, slot):
            cp.start()

        @pl.when(step == ni - 1)
        def _():
            for cp in slabs(step - 1, 1 - slot) + slabs(step, slot):
                cp.wait()

    rev = lambda st: ni - 1 - st
    row = lambda w: pl.BlockSpec((tm, w), lambda st: (rev(st), 0))
    pcol = lambda w, jb: pl.BlockSpec((tm, w), lambda st: (rev(st), jb))
    halo = lambda jb: pl.BlockSpec((HALO, D), lambda st: (jnp.maximum(rev(st) * hb - 1, 0), jb))
    const = lambda a: pl.BlockSpec(a.shape, lambda st: (0, 0), pipeline_mode=pl.Buffered(1))
    acc = pl.BlockSpec((8, D), lambda st: (0, 0))
    return pl.pallas_call(
        body, name="tail", grid=(ni,),
        in_specs=[row(D), row(CB), pcol(D, 9), pcol(D, 10), pcol(CB, CB_ZB), row(CB), row(D), row(D),
                  pl.BlockSpec((1, D), lambda st: (0, 0)), const(pa_w), const(pb_w), const(wo_w), const(total),
                  pcol(D, 0), pcol(D, 1), pcol(D, 2), pcol(D, 3), halo(1), halo(2),
                  pl.BlockSpec((3, D), lambda st: (0, 0))],
        out_specs=[pl.BlockSpec(memory_space=pl.ANY),
                   row(D), row(CB), row(LANES), row(D), row(D), row(D), row(D), acc, acc],
        out_shape=[jax.ShapeDtypeStruct((s, NIN), BF16), jax.ShapeDtypeStruct((s, D), F32),
                   jax.ShapeDtypeStruct((s, CB), BF16), jax.ShapeDtypeStruct((s, LANES), F32)]
                  + [jax.ShapeDtypeStruct((s, D), BF16)] * 4 + [jax.ShapeDtypeStruct((8, D), F32)] * 2,
        scratch_shapes=[pltpu.VMEM((2, tm, nearly + nlate), BF16), pltpu.VMEM((8, D), F32),
                        pltpu.SemaphoreType.DMA((2, 2))],
        compiler_params=_cp(("arbitrary",), 60))(
            ya, yb, proj, proj, proj, attn, x, target, gate, pa_w, pb_w, wo_w, total,
            proj, proj, proj, proj, proj, proj, conv_w)


def _local_step(x, target, shift, scale, gate, norm_w, conv_w, qw, kw, w_shard, small_shards, me_xyc):
    qw8, kw8 = jnp.tile(qw, (1, NH)), jnp.tile(kw, (1, NH))
    same, total, expand = _head_matrices()
    proj, ht, wg, (pa_g, pb_g, wo_g) = proj_fwd_gather(
        x, norm_w, scale, shift, w_shard, small_shards, gather_order(me_xyc), 1024)
    pa_w, wo_w = pa_g.reshape(D, D), wo_g.reshape(D, D)
    pb_w = pb_g.transpose(1, 0, 2).reshape(CB, D)
    srcs = qkv_prep(proj, qw8, kw8, same, 512)
    o_g, lse_g = zip(*[attn_fwd(*srcs[g], g, d) for g, d in enumerate(DILATIONS)])
    ya, yb, attn, lc = mid_fwd(proj, o_g, lse_g, conv_w, expand, 512)
    dproj, dy, da, dc, merged, dout, dpa, dpb, st_tail, st_conv = tail(
        proj, ya, yb, attn, x, target, gate, pa_w, pb_w, wo_w, total, conv_w, 256)
    g_wo, g_pa, g_pb = matmuls_tn([(merged, dout), (ya, dpa), (yb, dpb)], "grad_small_weights", 1024)
    grads = []
    for g, (d, (da_p, lc_p, dc_p, lt, dt)) in enumerate(zip(DILATIONS, stats_prep(da, lc, dc, 2048))):
        dq = attn_bwd_q(*srcs[g], da_p, lc_p, dc_p, g, d)
        dk, dv = attn_bwd_kv(*srcs[g], da_p, lt, dt, g, d)
        grads.append((dq, dk, dv))
    dproj, gw_qk = qkv_grads_to_dproj(dproj, proj, grads, qw8, kw8, same, 512)
    slabs = [g_pa.reshape(NDEV, 128, D), g_pb.reshape(CB, NDEV, 128).transpose(1, 0, 2), g_wo.reshape(NDEV, 128, D)]
    grad_x, st_norm, r_win, (r_pa, r_pb, r_wo) = proj_bwd(
        ht, dproj, wg, slabs, scatter_order(me_xyc), x, dy, norm_w, scale, 1024)
    dmod = jnp.concatenate([st_norm[0:1], st_norm[1:2], st_tail[0:1]], axis=1)
    loss_part = (0.5 / D) * jnp.sum(st_tail[1])
    gw_heads = gw_qk[0:2].reshape(2, NH, HD).sum(axis=1)
    small = dict(dmod=dmod, norm_w=st_norm[2:3], conv_w=st_conv[0:3],
                 q_norm_w=gw_heads[0:1], k_norm_w=gw_heads[1:2], loss=loss_part)
    return grad_x, small, (r_win, r_pa, r_pb, r_wo)


def kernel(x, c, w_ada, b_ada, norm_w, w_in, conv_w, q_norm_w, k_norm_w, w_br_conv, w_br_attn, w_out, loss_target, m_w_ada, m_b_ada, m_norm_w, m_w_in, m_conv_w, m_q_norm_w, m_k_norm_w, m_w_br_conv, m_w_br_attn, m_w_out, v_w_ada, v_b_ada, v_norm_w, v_w_in, v_conv_w, v_q_norm_w, v_k_norm_w, v_w_br_conv, v_w_br_attn, v_w_out):
    me_xyc = (lax.axis_index("x"), lax.axis_index("y"), lax.axis_index("c"))
    me = _dev_index(me_xyc)
    ncol = w_ada.shape[2]

    conv_pad = jnp.zeros((8, 128), F32).at[0:3].set(conv_w[0])
    b_cols = lax.dynamic_slice(b_ada, (0, me * ncol), (1, ncol))
    mod_pieces, c_all, conv_all = ada_fwd(c, conv_pad, w_ada[0], b_cols)
    conv_full = conv_all[:, 0:3].transpose(1, 0, 2).reshape(3, D)
    c_all = c_all.reshape(NDEV, D)
    mod = mod_pieces.reshape(1, 3 * D)
    shift, scale, gate = mod[:, 0:D], mod[:, D:2 * D], mod[:, 2 * D:3 * D]

    grad_x, small, (r_win, r_pa, r_pb, r_wo) = _local_step(
        x[0], loss_target[0], shift, scale, gate, norm_w, conv_full, q_norm_w, k_norm_w,
        w_in[0].astype(BF16), [w_br_conv[0].astype(BF16), w_br_attn[0].astype(BF16), w_out[0].astype(BF16)], me_xyc)

    packed = jnp.concatenate(
        [small["dmod"], small["norm_w"], small["conv_w"].reshape(1, 3 * D), small["q_norm_w"], small["k_norm_w"],
         jnp.full((1, 128), small["loss"], F32)], axis=1)
    (packed_all,) = all_gather([packed], "gather_small")
    tot = sum_parts(packed_all)
    loss = tot[0, 7 * D + 2 * HD]
    dmod_all = packed_all[:, 0, 0:3 * D]
    g_b_ada = tot[:, 0:3 * D]
    g_norm_w = tot[:, 3 * D:4 * D]
    g_conv = lax.dynamic_slice(tot[:, 4 * D:7 * D].reshape(3, D), (0, me * 128), (3, 128))
    g_qn = tot[:, 7 * D:7 * D + HD]
    g_kn = tot[:, 7 * D + HD:7 * D + 2 * HD]
    g_w_ada = ada_bwd(c_all.T, lax.dynamic_slice(dmod_all, (0, me * ncol), (NDEV, ncol)))

    def upd(parts, w, m, v, name, rows):
        shape = w.shape
        w2, m2, v2 = (t.reshape(shape[-2:]) for t in (w, m, v))
        return [t.reshape(shape) for t in adamw(parts, w2, m2, v2, name, rows)]

    res = {
        "w_in": upd(r_win, w_in, m_w_in, v_w_in, "adamw_w_in", 128),
        "w_br_conv": upd(r_pa, w_br_conv, m_w_br_conv, v_w_br_conv, "adamw_w_br_conv", 128),
        "w_br_attn": upd(r_pb, w_br_attn, m_w_br_attn, v_w_br_attn, "adamw_w_br_attn", 512),
        "w_out": upd(r_wo, w_out, m_w_out, v_w_out, "adamw_w_out", 128),
    }
    small_params = {"w_ada": (g_w_ada, w_ada, m_w_ada, v_w_ada), "b_ada": (g_b_ada, b_ada, m_b_ada, v_b_ada),
                    "norm_w": (g_norm_w, norm_w, m_norm_w, v_norm_w), "conv_w": (g_conv, conv_w, m_conv_w, v_conv_w),
                    "q_norm_w": (g_qn, q_norm_w, m_q_norm_w, v_q_norm_w),
                    "k_norm_w": (g_kn, k_norm_w, m_k_norm_w, v_k_norm_w)}
    updated = adamw_small([tuple(t.reshape(t.shape[-2:]) for t in item) for item in small_params.values()])
    for (pname, item), outs4 in zip(small_params.items(), updated):
        res[pname] = [t.reshape(item[1].shape) for t in outs4]
    names = ["w_ada", "b_ada", "norm_w", "w_in", "conv_w", "q_norm_w", "k_norm_w", "w_br_conv", "w_br_attn", "w_out"]
    return (loss, grad_x[None], *[res[n][0] for n in names], *[res[n][1] for n in names],
            *[res[n][2] for n in names], *[res[n][3] for n in names])
```

```python
import jax
import jax.numpy as jnp
from jax import lax
from jax.experimental import pallas as pl
from jax.experimental.pallas import tpu as pltpu

F32, BF16 = jnp.float32, jnp.bfloat16
D = 1024
NIN = 11264
NDEV = 8
SHARD = NIN // NDEV
HD = 64
NH = 8
QB = 128
CB = 512
CB_Q, CB_K, CB_V, CB_ZB = 8, 11, 14, 17
DILATIONS = (1, 4, 16)
EPS = 1e-6
NEG = -1e30
HALO = 16
LANES = 128
MESH = pl.DeviceIdType.MESH

ADAM_LR, ADAM_B1, ADAM_B2, ADAM_EPS, ADAM_WD, ADAM_STEP = 0.001, 0.9, 0.999, 1e-08, 0.01, 10

NT = (((1,), (1,)), ((), ()))
TN = (((0,), (0,)), ((), ()))


def _cp(sem, vmem_mb=48):
    return pltpu.CompilerParams(dimension_semantics=sem, vmem_limit_bytes=vmem_mb << 20)


def _silu(z):
    return z * jax.nn.sigmoid(z)


def _coords():
    return lax.axis_index("x"), lax.axis_index("y"), lax.axis_index("c")


FLIPS = [(fx, fy, fc) for fx in (0, 1) for fy in (0, 1) for fc in (0, 1)][1:]


def gather_sum(vec):
    def body(v_ref, all_ref, sum_ref, send_sems, recv_sems, local_sem):
        me_xyc = _coords()
        me = _dev_index(me_xyc)
        peers = [_flip(me_xyc, f) for f in FLIPS]

        def copy(k, block):
            return pltpu.make_async_remote_copy(
                src_ref=v_ref, dst_ref=all_ref.at[block], send_sem=send_sems.at[k], recv_sem=recv_sems.at[k],
                device_id=peers[k], device_id_type=MESH)

        mine = pltpu.make_async_copy(v_ref, all_ref.at[me], local_sem)
        sends = [copy(k, me) for k in range(7)]
        for cp in [mine] + sends:
            cp.start()
        for k in range(7):
            copy(k, _dev_index(peers[k])).wait_recv()
        mine.wait()
        acc = all_ref[0]
        for b in range(1, NDEV):
            acc = acc + all_ref[b]
        sum_ref[...] = acc
        for cp in sends:
            cp.wait_send()

    return pl.pallas_call(
        body, name="gather_sum",
        out_shape=[jax.ShapeDtypeStruct((NDEV,) + vec.shape, F32), jax.ShapeDtypeStruct(vec.shape, F32)],
        scratch_shapes=[pltpu.SemaphoreType.DMA((7,)), pltpu.SemaphoreType.DMA((7,)), pltpu.SemaphoreType.DMA],
    )(vec)


def _flip(dev, f):
    return tuple(1 - v if b else v for v, b in zip(dev, f))


def _dev_index(dev):
    return 4 * dev[0] + 2 * dev[1] + dev[2]


def _chip_order(x, y, c):
    xor = lambda a, b: a + b - 2 * a * b
    return [(xor(x, 1 - c), xor(y, c)), (xor(x, c), xor(y, 1 - c)), (1 - x, 1 - y)]


def gather_order(me_xyc):
    x, y, c = me_xyc
    chips = _chip_order(x, y, c)
    devs = [(x, y, c), (x, y, 1 - c), (*chips[0], c), (*chips[1], c),
            (*chips[1], 1 - c), (*chips[0], 1 - c), (*chips[2], c), (*chips[2], 1 - c)]
    return jnp.stack([_dev_index(d) for d in devs]).astype(jnp.int32)


def scatter_order(me_xyc):
    devs = [_flip(me_xyc, f) for f in FLIPS] + [me_xyc]
    return jnp.stack([_dev_index(d) for d in devs]).astype(jnp.int32)


def ada_fwd(c, conv_pad, w_ada, b_cols):
    ncol = w_ada.shape[1]

    def body(c_ref, cv_ref, w_ref, b_ref, mod_ref, call_ref, cvall_ref, rows_buf, send_sems, recv_sems, local_sems):
        me_xyc = _coords()
        me = _dev_index(me_xyc)
        peers = [_flip(me_xyc, f) for f in FLIPS]
        pids = [_dev_index(p) for p in peers]

        def copy(a, k, src, dst):
            return pltpu.make_async_remote_copy(src_ref=src, dst_ref=dst, send_sem=send_sems.at[a, k],
                                                recv_sem=recv_sems.at[a, k], device_id=peers[k], device_id_type=MESH)

        own = [pltpu.make_async_copy(c_ref, call_ref.at[me], local_sems.at[0]),
               pltpu.make_async_copy(cv_ref, cvall_ref.at[me], local_sems.at[1])]
        first = [copy(0, k, c_ref, call_ref.at[me]) for k in range(7)]
        first += [copy(1, k, cv_ref, cvall_ref.at[me]) for k in range(7)]
        for cp in own + first:
            cp.start()
        own[0].wait()
        for k in range(7):
            copy(0, k, c_ref, call_ref.at[pids[k]]).wait_recv()
        seq = lax.broadcasted_iota(jnp.int32, (NDEV, 1), 0)
        c_all = jnp.zeros((NDEV, D), F32)
        for p in range(NDEV):
            c_all = jnp.where(seq == p, call_ref[p], c_all)
        mods = jnp.dot(_silu(c_all).astype(BF16), w_ref[...].astype(BF16), preferred_element_type=F32) + b_ref[...]
        for p in range(NDEV):
            rows_buf[p] = mods[p:p + 1, :]
        mine = pltpu.make_async_copy(rows_buf.at[me], mod_ref.at[me], local_sems.at[2])
        second = [copy(2, k, rows_buf.at[pids[k]], mod_ref.at[me]) for k in range(7)]
        for cp in [mine] + second:
            cp.start()
        for k in range(7):
            copy(2, k, rows_buf.at[pids[k]], mod_ref.at[pids[k]]).wait_recv()
            copy(1, k, cv_ref, cvall_ref.at[pids[k]]).wait_recv()
        for cp in first + second:
            cp.wait_send()
        own[1].wait()
        mine.wait()

    return pl.pallas_call(
        body, name="ada_fwd",
        out_shape=[jax.ShapeDtypeStruct((NDEV, 1, ncol), F32), jax.ShapeDtypeStruct((NDEV, 1, D), F32),
                   jax.ShapeDtypeStruct((NDEV,) + conv_pad.shape, F32)],
        scratch_shapes=[pltpu.VMEM((NDEV, 1, ncol), F32), pltpu.SemaphoreType.DMA((3, 7)),
                        pltpu.SemaphoreType.DMA((3, 7)), pltpu.SemaphoreType.DMA((3,))],
    )(c, conv_pad, w_ada, b_cols)


def ada_bwd(c_all_t, dmod_cols):
    def body(c_ref, d_ref, o_ref):
        at = _silu(c_ref[...])
        acc = at[:, 0:1] * d_ref[0:1, :]
        for b in range(1, NDEV):
            acc = acc + at[:, b:b + 1] * d_ref[b:b + 1, :]
        o_ref[...] = acc

    return pl.pallas_call(body, name="ada_bwd",
                          out_shape=jax.ShapeDtypeStruct((D, dmod_cols.shape[1]), F32))(c_all_t, dmod_cols)


def _adamw_update(g, w_ref, m_ref, v_ref, g_ref, d_ref, nm_ref, nv_ref):
    nm = ADAM_B1 * m_ref[...] + (1.0 - ADAM_B1) * g
    nv = ADAM_B2 * v_ref[...] + (1.0 - ADAM_B2) * (g * g)
    g_ref[...] = g
    nm_ref[...] = nm
    nv_ref[...] = nv
    m_hat = nm / (1.0 - ADAM_B1 ** ADAM_STEP)
    v_hat = nv / (1.0 - ADAM_B2 ** ADAM_STEP)
    d_ref[...] = -ADAM_LR * (m_hat / (jnp.sqrt(v_hat) + ADAM_EPS) + ADAM_WD * w_ref[...])


def adamw_small(items):
    n = len(items)

    def body(*refs):
        ins, outs = refs[:4 * n], refs[4 * n:]
        for a in range(n):
            p_ref, w_ref, m_ref, v_ref = ins[4 * a:4 * a + 4]
            g = p_ref[0].astype(F32)
            for b in range(1, p_ref.shape[0]):
                g = g + p_ref[b].astype(F32)
            _adamw_update(g, w_ref, m_ref, v_ref, *outs[4 * a:4 * a + 4])

    out = pl.pallas_call(
        body, name="adamw_small",
        out_shape=[jax.ShapeDtypeStruct(it[1].shape, F32) for it in items for _ in range(4)],
        compiler_params=pltpu.CompilerParams(vmem_limit_bytes=48 << 20))(*[t for it in items for t in it])
    return [out[4 * a:4 * a + 4] for a in range(n)]


def adamw(parts, w, m, v, name, rows):
    n, r, ccols = parts.shape

    def body(p_ref, w_ref, m_ref, v_ref, g_ref, d_ref, nm_ref, nv_ref):
        g = p_ref[0].astype(F32)
        for b in range(1, n):
            g = g + p_ref[b].astype(F32)
        _adamw_update(g, w_ref, m_ref, v_ref, g_ref, d_ref, nm_ref, nv_ref)

    blk = pl.BlockSpec((rows, ccols), lambda i: (i, 0))
    out = jax.ShapeDtypeStruct((r, ccols), F32)
    return pl.pallas_call(
        body, name=name, grid=(r // rows,),
        in_specs=[pl.BlockSpec((n, rows, ccols), lambda i: (0, i, 0)), blk, blk, blk],
        out_specs=[blk] * 4, out_shape=[out] * 4, compiler_params=_cp(("parallel",)))(parts, w, m, v)


def proj_fwd_gather(x, nw, scale, shift, w_shard, extras, order, tm):
    s = x.shape[0]
    ni = s // tm
    n = 1 + len(extras)
    mid = ni - 2

    def body(order_ref, x_ref, nw_ref, sc_ref, sh_ref, *refs):
        ins, o_ref, ht_ref, outs = refs[:n], refs[n], refs[n + 1], refs[n + 2:2 * n + 2]
        h_all, wbuf, send_sems, recv_sems, local_sems, load_sems = refs[2 * n + 2:]
        jj, i = pl.program_id(0), pl.program_id(1)
        x, y, c = _coords()
        me, sibling = (x, y, c), (x, y, 1 - c)
        chips = _chip_order(x, y, c)
        relayed = [(*chips[1], 1 - c), (*chips[0], 1 - c), (*chips[2], 1 - c)]

        def slot(a, dev):
            return outs[a].at[_dev_index(dev)]

        def copy(a, k, block, to, src=None):
            return pltpu.make_async_remote_copy(
                src_ref=slot(a, block) if src is None else src, dst_ref=slot(a, block),
                send_sem=send_sems.at[a, k], recv_sem=recv_sems.at[a, k], device_id=to, device_id_type=MESH)

        mine = [pltpu.make_async_copy(ins[a], slot(a, me), local_sems.at[a]) for a in range(n)]
        to_sibling = [copy(a, 0, me, sibling, src=ins[a]) for a in range(n)]
        to_chip = [[copy(a, 1 + j, me, (*chips[j], c), src=ins[a]) for a in range(n)] for j in range(2)]
        onward = [copy(a, 3, (*chips[1], c), (*chips[0], c)) for a in range(n)]
        passed = [[copy(a, 4 + j, (*ch, c), sibling) for a in range(n)] for j, ch in enumerate(chips)]
        sends = lambda a: [to_sibling[a], to_chip[0][a], to_chip[1][a], onward[a]] + [passed[j][a] for j in range(3)]

        def arrived(a, j):
            copy(a, 1 + j, (*chips[j], c), me).wait_recv()

        def load(row):
            return pltpu.make_async_copy(outs[0].at[order_ref[row]], wbuf.at[row % 2], load_sems.at[row % 2])

        @pl.when((jj == 0) & (i == 0))
        def _():
            for cp in mine:
                cp.start()
            to_sibling[0].start()
            to_chip[0][0].start()
            pltpu.make_async_copy(ins[0], wbuf.at[0], load_sems.at[0]).start()

        @pl.when((jj == 1) & (i == 0))
        def _():
            to_chip[1][0].start()

        @pl.when((jj == 4) & (i == 0))
        def _():
            for a in range(1, n):
                to_sibling[a].start()
                to_chip[0][a].start()
                to_chip[1][a].start()

        direct = {2: 0, 3: 1, 6: 2}
        relay = {4: 0, 5: 1, 7: 2}

        @pl.when((jj == 0) & (i == mid))
        def _():
            copy(0, 0, sibling, me).wait_recv()

        for row, j in direct.items():
            @pl.when((jj == row - 1) & (i == mid))
            def _(j=j):
                arrived(0, j)
                passed[j][0].start()
                if j == 1:
                    onward[0].start()

        for row, j in relay.items():
            @pl.when((jj == row - 1) & (i == mid))
            def _(j=j):
                copy(0, 4 + j, relayed[j], me).wait_recv()

        @pl.when((jj == NDEV - 1) & (i == 0))
        def _():
            for a in range(1, n):
                arrived(a, 1)
                onward[a].start()
                passed[1][a].start()
                arrived(a, 0)
                passed[0][a].start()

        @pl.when((jj < NDEV - 1) & (i == mid))
        def _():
            load(jj + 1).start()

        @pl.when(i == 0)
        def _():
            load(jj).wait()

        @pl.when(jj == 0)
        def _():
            xf = x_ref[...]
            r = lax.rsqrt(jnp.mean(xf * xf, axis=-1, keepdims=True) + EPS)
            h = (xf * r * nw_ref[...]) * (1.0 + sc_ref[...]) + sh_ref[...]
            h_all[i] = h.astype(BF16)
            ht_ref[...] = h.T.astype(BF16)

        o_ref[...] = jnp.dot(h_all[i], wbuf[jj % 2], preferred_element_type=F32).astype(BF16)

        @pl.when((jj == NDEV - 1) & (i == ni - 1))
        def _():
            for a in range(1, n):
                arrived(a, 2)
                passed[2][a].start()
            for a in range(1, n):
                copy(a, 0, sibling, me).wait_recv()
                for j in range(3):
                    copy(a, 4 + j, relayed[j], me).wait_recv()
            for a in range(n):
                mine[a].wait()
                for cp in sends(a):
                    cp.wait_send()

    any_spec = pl.BlockSpec(memory_space=pl.ANY)
    vec = pl.BlockSpec((1, D), lambda jj, i, o: (0, 0))
    outs = pl.pallas_call(
        body, name="proj_fwd_gather",
        grid_spec=pltpu.PrefetchScalarGridSpec(
            num_scalar_prefetch=1, grid=(NDEV, ni),
            in_specs=[pl.BlockSpec((tm, D), lambda jj, i, o: (jnp.where(jj == 0, i, ni - 1), 0))] + [vec] * 3
                     + [any_spec] * n,
            out_specs=[pl.BlockSpec((tm, SHARD), lambda jj, i, o: (i, o[jj])),
                       pl.BlockSpec((D, tm), lambda jj, i, o: (0, jnp.where(jj == 0, i, ni - 1)))]
                      + [any_spec] * n,
            scratch_shapes=[pltpu.VMEM((ni, tm, D), BF16), pltpu.VMEM((2, D, SHARD), BF16),
                            pltpu.SemaphoreType.DMA((n, 7)), pltpu.SemaphoreType.DMA((n, 7)),
                            pltpu.SemaphoreType.DMA((n,)), pltpu.SemaphoreType.DMA((2,))]),
        out_shape=[jax.ShapeDtypeStruct((s, NIN), BF16), jax.ShapeDtypeStruct((D, s), BF16),
                   jax.ShapeDtypeStruct((NDEV, D, SHARD), BF16)]
                  + [jax.ShapeDtypeStruct((NDEV,) + e.shape, e.dtype) for e in extras],
        compiler_params=_cp(("arbitrary", "arbitrary"), 56))(order, x, nw, scale, shift, w_shard, *extras)
    return outs[0], outs[1], outs[2], outs[3:]


def proj_bwd(ht, dproj, wg, smalls, order, x, dy, nw, scale, tt):
    s = dproj.shape[0]
    nk = s // tt
    n = len(smalls)
    rows_per_step = tt // nk
    last = 2 * NDEV

    def body(order_ref, ht_ref, dp_ref, w_ref, x_ref, dy_ref, nw_ref, sc_ref, *rest):
        small_in = rest[:n]
        gx_ref, st_ref, gw_ref, rwin_ref = rest[n:n + 4]
        small_out = rest[n + 4:2 * n + 4]
        acc, stage, dh, send_sems, recv_sems, local_sems, stage_sems = rest[2 * n + 4:]
        t, k = pl.program_id(0), pl.program_id(1)
        me_xyc = _coords()
        me = _dev_index(me_xyc)
        peers = [_flip(me_xyc, f) for f in FLIPS]

        def exchange(a, kf, src_arr, dst_arr):
            pid = _dev_index(peers[kf])
            mk = lambda dst: pltpu.make_async_remote_copy(
                src_ref=src_arr.at[pid], dst_ref=dst, send_sem=send_sems.at[a, kf], recv_sem=recv_sems.at[a, kf],
                device_id=peers[kf], device_id_type=MESH)
            return mk(dst_arr.at[me]), mk(dst_arr.at[pid])

        small_pairs = [exchange(1 + a, kf, small_in[a], small_out[a]) for kf in range(7) for a in range(n)]
        small_own = [pltpu.make_async_copy(small_in[a].at[me], small_out[a].at[me], local_sems.at[1 + a])
                     for a in range(n)]
        win_pairs = [exchange(0, kf, gw_ref, rwin_ref) for kf in range(7)]
        win_own = pltpu.make_async_copy(gw_ref.at[me], rwin_ref.at[me], local_sems.at[0])

        def to_hbm(jj):
            slab = me if jj == 7 else _dev_index(peers[jj])
            return pltpu.make_async_copy(stage.at[jj % 2], gw_ref.at[slab], stage_sems.at[jj % 2])

        @pl.when((t == 0) & (k == 0))
        def _():
            for cp in small_own:
                cp.start()
            for send, _ in small_pairs:
                send.start()

        @pl.when(t < NDEV)
        def _():
            p = jnp.dot(ht_ref[...], dp_ref[...], preferred_element_type=F32)

            @pl.when(k == 0)
            def _():
                acc[...] = p

            @pl.when(k > 0)
            def _():
                acc[...] += p

        for jj in range(NDEV):
            @pl.when((t == jj) & (k == nk - 1))
            def _(jj=jj):
                stage[jj % 2] = acc[...].astype(BF16)
                to_hbm(jj).start()

            @pl.when((t == jj + 1) & (k == 1))
            def _(jj=jj):
                to_hbm(jj).wait()
                if jj < 7:
                    win_pairs[jj][0].start()
                else:
                    win_own.start()

        def matmul_step():
            p = lax.dot_general(dp_ref[...], w_ref[...], NT, preferred_element_type=F32)
            slot = t % 2
            dh[slot] = jnp.where(k == 0, p, dh[slot] + p)

        def norm_step():
            g = dh.at[(t + 1) % 2][pl.ds(pl.multiple_of(k * rows_per_step, rows_per_step), rows_per_step), :]
            xf = x_ref[...]
            r = lax.rsqrt(jnp.mean(xf * xf, axis=-1, keepdims=True) + EPS)
            xh = xf * r
            dn = g * (1.0 + sc_ref[...])
            dxh = dn * nw_ref[...]
            gx_ref[...] = dy_ref[...] + r * (dxh - xh * jnp.mean(dxh * xh, axis=-1, keepdims=True))
            st_ref[0:1, :] += jnp.sum(g, axis=0, keepdims=True)
            st_ref[1:2, :] += jnp.sum(g * xh * nw_ref[...], axis=0, keepdims=True)
            st_ref[2:3, :] += jnp.sum(dn * xh, axis=0, keepdims=True)

        @pl.when((t == 0) & (k == 0))
        def _():
            st_ref[...] = jnp.zeros_like(st_ref)

        @pl.when(t == NDEV)
        def _():
            matmul_step()

        @pl.when((t > NDEV) & (t < last))
        def _():
            matmul_step()
            norm_step()

        @pl.when(t == last)
        def _():
            norm_step()

        @pl.when((t == last) & (k == nk - 1))
        def _():
            for _, recv in win_pairs + small_pairs:
                recv.wait_recv()
            for send, _ in win_pairs + small_pairs:
                send.wait_send()
            win_own.wait()
            for cp in small_own:
                cp.wait()

    any_spec = pl.BlockSpec(memory_space=pl.ANY)
    first = lambda t: t < NDEV
    slab = lambda t, k: jnp.where(t == last, NDEV - 1, k)
    chunk = pl.BlockSpec((rows_per_step, D), lambda t, k, o: (jnp.maximum((t - NDEV - 1) * nk + k, 0), 0))
    vec = pl.BlockSpec((1, D), lambda t, k, o: (0, 0))
    outs = pl.pallas_call(
        body, name="proj_bwd",
        grid_spec=pltpu.PrefetchScalarGridSpec(
            num_scalar_prefetch=1, grid=(last + 1, nk),
            in_specs=[pl.BlockSpec((D, tt), lambda t, k, o: (0, jnp.where(first(t), k, nk - 1))),
                      pl.BlockSpec((tt, SHARD), lambda t, k, o: (jnp.where(first(t), k, jnp.minimum(t, last - 1) - NDEV),
                                                                 jnp.where(first(t), o[jnp.minimum(t, NDEV - 1)],
                                                                           slab(t, k)))),
                      pl.BlockSpec((None, D, SHARD), lambda t, k, o: (jnp.where(first(t), 0, slab(t, k)), 0, 0)),
                      chunk, chunk, vec, vec]
                     + [any_spec] * n,
            out_specs=[chunk, pl.BlockSpec((8, D), lambda t, k, o: (0, 0))] + [any_spec] * (2 + n),
            scratch_shapes=[pltpu.VMEM((D, SHARD), F32), pltpu.VMEM((2, D, SHARD), BF16),
                            pltpu.VMEM((2, tt, D), F32),
                            pltpu.SemaphoreType.DMA((1 + n, 7)), pltpu.SemaphoreType.DMA((1 + n, 7)),
                            pltpu.SemaphoreType.DMA((1 + n,)), pltpu.SemaphoreType.DMA((2,))]),
        out_shape=[jax.ShapeDtypeStruct((s, D), F32), jax.ShapeDtypeStruct((8, D), F32),
                   jax.ShapeDtypeStruct((NDEV, D, SHARD), BF16), jax.ShapeDtypeStruct((NDEV, D, SHARD), BF16)]
                  + [jax.ShapeDtypeStruct(a.shape, a.dtype) for a in smalls],
        compiler_params=_cp(("arbitrary", "arbitrary"), 56))(order, ht, dproj, wg, x, dy, nw, scale, *smalls)
    return outs[0], outs[1], outs[3], outs[4:]


def matmuls_tn(pairs, name, tk):
    s = pairs[0][0].shape[0]
    nk = s // tk
    n = len(pairs)
    shapes = [(a.shape[1], b.shape[1]) for a, b in pairs]

    def body(*refs):
        ins, outs, accs = refs[:2 * n], refs[2 * n:3 * n], refs[3 * n:]
        k = pl.program_id(0)
        for j in range(n):
            p = lax.dot_general(ins[2 * j][...], ins[2 * j + 1][...], TN, preferred_element_type=F32)
            accs[j][...] = jnp.where(k == 0, p, accs[j][...] + p)

        @pl.when(k == nk - 1)
        def _():
            for j in range(n):
                outs[j][...] = accs[j][...].astype(BF16)

    return pl.pallas_call(
        body, name=name, grid=(nk,),
        in_specs=[pl.BlockSpec((tk, t.shape[1]), lambda k: (k, 0)) for pair in pairs for t in pair],
        out_specs=[pl.BlockSpec(sh, lambda k: (0, 0)) for sh in shapes],
        out_shape=[jax.ShapeDtypeStruct(sh, BF16) for sh in shapes],
        scratch_shapes=[pltpu.VMEM(sh, F32) for sh in shapes],
        compiler_params=_cp(("arbitrary",), 56))(*[t for pair in pairs for t in pair])


def _head_matrices():
    lane = lax.broadcasted_iota(jnp.int32, (CB, CB), 0)
    col = lax.broadcasted_iota(jnp.int32, (CB, CB), 1)
    same = (lane // HD == col // HD).astype(BF16)
    lane_c = lax.broadcasted_iota(jnp.int32, (CB, LANES), 0)
    col_c = lax.broadcasted_iota(jnp.int32, (CB, LANES), 1)
    total = (lane_c // HD == col_c).astype(BF16)
    lane_e = lax.broadcasted_iota(jnp.int32, (LANES, CB), 0)
    col_e = lax.broadcasted_iota(jnp.int32, (LANES, CB), 1)
    expand = (lane_e == col_e // HD).astype(BF16)
    return same, total, expand


def _head_sum(x, m_ref):
    return jnp.dot(x.astype(BF16), m_ref[...], preferred_element_type=F32)


def _dot_hilo(x, m_ref):
    hi = x.astype(BF16)
    lo = (x - hi.astype(F32)).astype(BF16)
    return (jnp.dot(hi, m_ref[...], preferred_element_type=F32)
            + jnp.dot(lo, m_ref[...], preferred_element_type=F32))


def _to_residue_major(val, buf, out_ref, dil):
    rows = out_ref.shape[1]
    for k in range(val.shape[1] // LANES):
        lanes = slice(k * LANES, (k + 1) * LANES)
        buf[k] = val[:, lanes]
        for r in range(dil):
            out_ref[r, :, lanes] = buf.at[k][pl.ds(r, rows, stride=dil), :].astype(out_ref.dtype)


def _from_residue_major(ref, buf, dil):
    if dil == 1:
        return ref[0].astype(F32)
    rows, chunks = ref.shape[1], ref.shape[2] // LANES
    for k in range(chunks):
        for r in range(dil):
            buf.at[k][pl.ds(r, rows, stride=dil), :] = ref[r, :, k * LANES:(k + 1) * LANES].astype(F32)
    return jnp.concatenate([buf[k] for k in range(chunks)], axis=1)


def qkv_prep(proj, qw8, kw8, same, tm):
    s = proj.shape[0]
    items = []
    for g, d in enumerate(DILATIONS):
        items += [(g, "q", CB_Q + g, d), (g, "k", CB_K + g, d)] + ([(g, "v", CB_V + g, d)] if d > 1 else [])
    n = len(items)

    def body(*refs):
        ins, (qw_ref, kw_ref, same_ref), outs, buf = refs[:n], refs[n:n + 3], refs[n + 3:2 * n + 3], refs[-1]
        for idx, (_, kind, _, dil) in enumerate(items):
            val = ins[idx][...].astype(F32)
            if kind != "v":
                r = lax.rsqrt(_head_sum(val * val, same_ref) * (1.0 / HD) + EPS)
                val = val * r * (qw_ref if kind == "q" else kw_ref)[...]
            if dil == 1:
                outs[idx][0] = val.astype(BF16)
            else:
                _to_residue_major(val, buf, outs[idx], dil)

    full = lambda a: pl.BlockSpec(a.shape, lambda i: (0, 0))
    outs = pl.pallas_call(
        body, name="qkv_prep", grid=(s // tm,),
        in_specs=[pl.BlockSpec((tm, CB), lambda i, cb=cb: (i, cb)) for _, _, cb, _ in items]
                 + [full(qw8), full(kw8), full(same)],
        out_specs=[pl.BlockSpec((d, tm // d, CB), lambda i: (0, i, 0)) for _, _, _, d in items],
        out_shape=[jax.ShapeDtypeStruct((d, s // d, CB), BF16) for _, _, _, d in items],
        scratch_shapes=[pltpu.VMEM((CB // LANES, tm, LANES), F32)],
        compiler_params=_cp(("parallel",)))(*([proj] * n), qw8 * (HD ** -0.5), kw8, same)
    srcs = [[None, None, (proj, CB_V + g)] for g in range(len(DILATIONS))]
    for (g, kind, _, _), o in zip(items, outs):
        srcs[g]["qkv".index(kind)] = (o.reshape(s, CB), 0)
    return srcs


def stats_prep(da, lc, dc, tm):
    s = da.shape[0]

    def body(da_ref, lc_ref, dc_ref, *refs):
        outs, buf = list(refs[:-1]), refs[-1]
        for dil in DILATIONS:
            rows = tm // dil
            if dil > 1:
                _to_residue_major(da_ref[...].astype(F32), buf, outs.pop(0), dil)
            for src in (lc_ref, dc_ref):
                dst = outs.pop(0) if dil > 1 else None
                dst_t = outs.pop(0)
                buf[0] = src[...]
                for r in range(dil):
                    piece = buf.at[0][pl.ds(r, rows, stride=dil), :] if dil > 1 else buf[0]
                    if dil > 1:
                        dst[r] = piece
                    dst_t[r] = piece.T[0:NH, :]

    row = lambda w: pl.BlockSpec((tm, w), lambda i: (i, 0))
    out_specs, out_shape = [], []
    for dil in DILATIONS:
        rm = lambda w, dil=dil: (pl.BlockSpec((dil, tm // dil, w), lambda i: (0, i, 0)),
                                 jax.ShapeDtypeStruct((dil, s // dil, w), BF16 if w == CB else F32))
        tr = (pl.BlockSpec((dil, NH, tm // dil), lambda i: (0, 0, i)), jax.ShapeDtypeStruct((dil, NH, s // dil), F32))
        group = ([rm(CB)] if dil > 1 else []) + ([rm(LANES), tr, rm(LANES), tr] if dil > 1 else [tr, tr])
        out_specs += [sp for sp, _ in group]
        out_shape += [sh for _, sh in group]
    outs = list(pl.pallas_call(
        body, name="stats_prep", grid=(s // tm,),
        in_specs=[row(CB), row(LANES), row(LANES)], out_specs=out_specs, out_shape=out_shape,
        scratch_shapes=[pltpu.VMEM((CB // LANES, tm, LANES), F32)],
        compiler_params=_cp(("parallel",)))(da, lc, dc))
    res = []
    for dil in DILATIONS:
        flat_t = lambda t, dil=dil: t.reshape(dil * NH, s // dil)
        if dil == 1:
            lt, dt = outs.pop(0), outs.pop(0)
            res.append((da, lc, dc, flat_t(lt), flat_t(dt)))
        else:
            dap, lcp, lt, dcp, dt = (outs.pop(0) for _ in range(5))
            res.append((dap.reshape(s, CB), lcp.reshape(s, LANES), dcp.reshape(s, LANES), flat_t(lt), flat_t(dt)))
    return res


def qkv_grads_to_dproj(dproj, proj, grads, qw8, kw8, same, tm):
    s = dproj.shape[0]
    ni = s // tm
    flat = [(t.reshape(d, s // d, CB), d, kind, 3 * kind + g)
            for g, d in enumerate(DILATIONS) for kind, t in enumerate(grads[g])]
    nf = len(flat)
    nraw = 2 * len(DILATIONS)

    def body(*refs):
        dp_hbm, raws, ins = refs[nraw + nf + 4], refs[1:1 + nraw], refs[1 + nraw:1 + nraw + nf]
        qw_ref, kw_ref, same_ref = refs[1 + nraw + nf:4 + nraw + nf]
        gw_ref, stage, buf, sems = refs[5 + nraw + nf:]
        i = pl.program_id(0)
        slot = i % 2

        def slab(step, sl):
            return pltpu.make_async_copy(
                stage.at[sl], dp_hbm.at[pl.ds(pl.multiple_of(step * tm, tm), tm), pl.ds(CB_Q * CB, 9 * CB)],
                sems.at[sl])

        @pl.when(i == 0)
        def _():
            gw_ref[...] = jnp.zeros_like(gw_ref)

        @pl.when(i >= 2)
        def _():
            slab(i - 2, slot).wait()

        for ref, (_, d, kind, jj) in zip(ins, flat):
            cols = slice(jj * CB, (jj + 1) * CB)
            dn = _from_residue_major(ref, buf, d)
            if kind == 2:
                stage[slot, :, cols] = dn.astype(BF16)
                continue
            t = raws[jj][...].astype(F32)
            r = lax.rsqrt(_head_sum(t * t, same_ref) * (1.0 / HD) + EPS)
            xh = t * r
            gw_ref[kind:kind + 1, :] += jnp.sum(dn * xh, axis=0, keepdims=True)
            dxh = dn * (qw_ref if kind == 0 else kw_ref)[...]
            mean = _head_sum(dxh * xh, same_ref) * (1.0 / HD)
            stage[slot, :, cols] = (r * (dxh - xh * mean)).astype(BF16)
        slab(i, slot).start()

        @pl.when(i == ni - 1)
        def _():
            slab(i - 1, 1 - slot).wait()
            slab(i, slot).wait()

    full = lambda a: pl.BlockSpec(a.shape, lambda i: (0, 0))
    any_spec = pl.BlockSpec(memory_space=pl.ANY)
    return pl.pallas_call(
        body, name="qkv_grads_to_dproj", grid=(ni,),
        in_specs=[any_spec] + [pl.BlockSpec((tm, CB), lambda i, jb=jb: (i, CB_Q + jb)) for jb in range(nraw)]
                 + [pl.BlockSpec((d, tm // d, CB), lambda i: (0, i, 0)) for _, d, _, _ in flat]
                 + [full(qw8), full(kw8), full(same)],
        out_specs=[any_spec, pl.BlockSpec((8, CB), lambda i: (0, 0))],
        out_shape=[jax.ShapeDtypeStruct((s, NIN), BF16), jax.ShapeDtypeStruct((8, CB), F32)],
        input_output_aliases={0: 0},
        scratch_shapes=[pltpu.VMEM((2, tm, 9 * CB), BF16), pltpu.VMEM((CB // LANES, tm, LANES), F32),
                        pltpu.SemaphoreType.DMA((2,))],
        compiler_params=_cp(("arbitrary",)))(
            dproj, *([proj] * nraw), *[t for t, _, _, _ in flat], qw8, kw8, same)


def _lane_lo():
    return lax.broadcasted_iota(jnp.int32, (1, 2 * HD), 1) < HD


def _stack_heads(t, lo):
    zero = jnp.zeros_like(t)
    return jnp.concatenate([jnp.where(lo, t, zero), jnp.where(lo, zero, t)], axis=0)


def _masks(other_ok):
    qi = lax.broadcasted_iota(jnp.int32, (QB, QB), 0)
    kj = lax.broadcasted_iota(jnp.int32, (QB, QB), 1)
    return (kj >= qi) & other_ok, kj <= qi


MAX_SUB = 8


def _attn_specs(nb, dil, sub):
    steps = nb // sub
    main = lambda cb, w=CB: pl.BlockSpec((sub * QB, w), lambda r, s: (r * steps + s, cb))
    prev = lambda cb: pl.BlockSpec((QB, CB), lambda r, s: (jnp.maximum(r * nb + sub * s - 1, 0), cb))
    nxt = lambda cb: pl.BlockSpec((QB, CB), lambda r, s: (jnp.minimum(r * nb + sub * (s + 1), dil * nb - 1), cb))
    return main, prev, nxt


def attn_fwd(q_src, k_src, v_src, g, dil):
    s = q_src[0].shape[0]
    nb = s // dil // QB
    sub = min(MAX_SUB, nb)
    main, prev, _ = _attn_specs(nb, dil, sub)

    def body(q_ref, kp_ref, k_ref, vp_ref, v_ref, o_ref, l_ref, kbuf, vbuf):
        step = pl.program_id(1)
        kbuf[0:QB], kbuf[QB:] = kp_ref[...], k_ref[...]
        vbuf[0:QB], vbuf[QB:] = vp_ref[...], v_ref[...]
        lo = _lane_lo()
        head_lane = lax.broadcasted_iota(jnp.int32, (1, LANES), 1)

        def block(j, carry):
            r0 = pl.multiple_of(j * QB, QB)
            rows, krows = pl.ds(r0, QB), pl.ds(r0, 2 * QB)
            m_prev, m_cur = _masks(step * sub + j > 0)
            mask = jnp.concatenate([m_prev, m_cur], axis=1)
            mask = jnp.concatenate([mask, mask], axis=0)
            lses = jnp.zeros((QB, LANES), F32)
            for i in range(NH // 2):
                sl = slice(2 * HD * i, 2 * HD * (i + 1))
                qs, ks, vv = q_ref[rows, sl], kbuf[krows, sl], vbuf[krows, sl]
                sc = lax.dot_general(_stack_heads(qs, lo), ks, NT, preferred_element_type=F32)
                sc = jnp.where(mask, sc, NEG)
                mx = jnp.max(sc, axis=-1, keepdims=True)
                p = jnp.exp(sc - mx)
                den = jnp.sum(p, axis=-1, keepdims=True)
                o = jnp.dot(p.astype(BF16), vv, preferred_element_type=F32) * (1.0 / den)
                lse = mx + jnp.log(den)
                o_ref[rows, sl] = jnp.where(lo, o[:QB], o[QB:]).astype(BF16)
                lses = jnp.where(head_lane == 2 * i, lse[:QB], jnp.where(head_lane == 2 * i + 1, lse[QB:], lses))
            l_ref[rows, :] = lses
            return carry

        lax.fori_loop(0, sub, block, 0, unroll=True)

    return pl.pallas_call(
        body, name=f"attn_fwd_g{g}", grid=(dil, nb // sub),
        in_specs=[main(q_src[1]), prev(k_src[1]), main(k_src[1]), prev(v_src[1]), main(v_src[1])],
        out_specs=[main(0), main(0, LANES)],
        out_shape=[jax.ShapeDtypeStruct((s, CB), BF16), jax.ShapeDtypeStruct((s, LANES), F32)],
        scratch_shapes=[pltpu.VMEM(((sub + 1) * QB, CB), BF16)] * 2,
        compiler_params=_cp(("parallel", "parallel")))(q_src[0], k_src[0], k_src[0], v_src[0], v_src[0])


def attn_bwd_q(q_src, k_src, v_src, da, lc, dc, g, dil):
    s = q_src[0].shape[0]
    nb = s // dil // QB
    sub = min(MAX_SUB, nb)
    main, prev, _ = _attn_specs(nb, dil, sub)

    def body(q_ref, kp_ref, k_ref, vp_ref, v_ref, da_ref, l_ref, d_ref, dq_ref, kbuf, vbuf):
        step = pl.program_id(1)
        kbuf[0:QB], kbuf[QB:] = kp_ref[...], k_ref[...]
        vbuf[0:QB], vbuf[QB:] = vp_ref[...], v_ref[...]
        lo = _lane_lo()

        def block(j, carry):
            r0 = pl.multiple_of(j * QB, QB)
            rows, krows = pl.ds(r0, QB), pl.ds(r0, 2 * QB)
            m_prev, m_cur = _masks(step * sub + j > 0)
            mask = jnp.concatenate([m_prev, m_cur], axis=1)
            mask = jnp.concatenate([mask, mask], axis=0)
            lcols, dcols = l_ref[rows, :], d_ref[rows, :]
            for i in range(NH // 2):
                sl = slice(2 * HD * i, 2 * HD * (i + 1))
                qs, ks, vv, da2 = q_ref[rows, sl], kbuf[krows, sl], vbuf[krows, sl], da_ref[rows, sl]
                pair = lambda t: jnp.concatenate([t[:, 2 * i:2 * i + 1], t[:, 2 * i + 1:2 * i + 2]], axis=0)
                sc = lax.dot_general(_stack_heads(qs, lo), ks, NT, preferred_element_type=F32)
                sc = jnp.where(mask, sc, NEG)
                p = jnp.exp(sc - pair(lcols))
                dp = lax.dot_general(_stack_heads(da2, lo), vv, NT, preferred_element_type=F32)
                ds = p * (dp - pair(dcols))
                dq = jnp.dot(ds.astype(BF16), ks, preferred_element_type=F32)
                dq_ref[rows, sl] = (jnp.where(lo, dq[:QB], dq[QB:]) * (HD ** -0.5)).astype(BF16)
            return carry

        lax.fori_loop(0, sub, block, 0, unroll=True)

    return pl.pallas_call(
        body, name=f"attn_bwd_q_g{g}", grid=(dil, nb // sub),
        in_specs=[main(q_src[1]), prev(k_src[1]), main(k_src[1]), prev(v_src[1]), main(v_src[1]),
                  main(0), main(0, LANES), main(0, LANES)],
        out_specs=main(0), out_shape=jax.ShapeDtypeStruct((s, CB), BF16),
        scratch_shapes=[pltpu.VMEM(((sub + 1) * QB, CB), BF16)] * 2,
        compiler_params=_cp(("parallel", "parallel")))(
            q_src[0], k_src[0], k_src[0], v_src[0], v_src[0], da, lc, dc)


def attn_bwd_kv(q_src, k_src, v_src, da, lt, dt, g, dil):
    s = q_src[0].shape[0]
    nb = s // dil // QB
    sub = min(MAX_SUB, nb)
    main, _, nxt = _attn_specs(nb, dil, sub)

    def body(k_ref, v_ref, q_ref, qn_ref, da_ref, dan_ref, l_ref, ln_ref, d_ref, dn_ref, dk_ref, dv_ref,
             qbuf, dabuf, lbuf, dbuf):
        step = pl.program_id(1)
        qbuf[0:sub * QB], qbuf[sub * QB:] = q_ref[...], qn_ref[...]
        dabuf[0:sub * QB], dabuf[sub * QB:] = da_ref[...], dan_ref[...]
        for c in range(sub):
            lbuf[c], dbuf[c] = l_ref[:, c * QB:(c + 1) * QB], d_ref[:, c * QB:(c + 1) * QB]
        lbuf[sub], dbuf[sub] = ln_ref[...], dn_ref[...]
        lo = _lane_lo()
        kj = lax.broadcasted_iota(jnp.int32, (QB, QB), 0)
        qi = lax.broadcasted_iota(jnp.int32, (QB, QB), 1)

        def block(j, carry):
            r0 = pl.multiple_of(j * QB, QB)
            rows, qrows = pl.ds(r0, QB), pl.ds(r0, 2 * QB)
            mask = jnp.concatenate([kj <= qi, (kj >= qi) & (step * sub + j < nb - 1)], axis=1)
            mask = jnp.concatenate([mask, mask], axis=1)
            lrow = jnp.concatenate([lbuf[j], lbuf[j + 1]], axis=1)
            drow = jnp.concatenate([dbuf[j], dbuf[j + 1]], axis=1)
            for i in range(NH // 2):
                sl = slice(2 * HD * i, 2 * HD * (i + 1))
                q2, da2 = _stack_heads(qbuf[qrows, sl], lo), _stack_heads(dabuf[qrows, sl], lo)
                ks, vv = k_ref[rows, sl], v_ref[rows, sl]
                pair = lambda t: jnp.concatenate([t[2 * i:2 * i + 1, :], t[2 * i + 1:2 * i + 2, :]], axis=1)
                sc = lax.dot_general(ks, q2, NT, preferred_element_type=F32)
                sc = jnp.where(mask, sc, NEG)
                p = jnp.exp(sc - pair(lrow))
                dp = lax.dot_general(vv, da2, NT, preferred_element_type=F32)
                ds = p * (dp - pair(drow))
                dv_ref[rows, sl] = jnp.dot(p.astype(BF16), da2, preferred_element_type=F32).astype(BF16)
                dk_ref[rows, sl] = jnp.dot(ds.astype(BF16), q2, preferred_element_type=F32).astype(BF16)
            return carry

        lax.fori_loop(0, sub, block, 0, unroll=True)

    steps = nb // sub
    t_main = pl.BlockSpec((NH, sub * QB), lambda r, s: (r, s))
    t_nxt = pl.BlockSpec((NH, QB), lambda r, s: (r, jnp.minimum(sub * (s + 1), nb - 1)))
    out = jax.ShapeDtypeStruct((s, CB), BF16)
    return pl.pallas_call(
        body, name=f"attn_bwd_kv_g{g}", grid=(dil, steps),
        in_specs=[main(k_src[1]), main(v_src[1]), main(q_src[1]), nxt(q_src[1]),
                  main(0), nxt(0), t_main, t_nxt, t_main, t_nxt],
        out_specs=[main(0), main(0)], out_shape=[out, out],
        scratch_shapes=[pltpu.VMEM(((sub + 1) * QB, CB), BF16)] * 2 + [pltpu.VMEM((sub + 1, NH, QB), F32)] * 2,
        compiler_params=_cp(("parallel", "parallel")))(
            k_src[0], v_src[0], q_src[0], q_src[0], da, da, lt, lt, dt, dt)


def _conv_taps(u, u_prev, first):
    tm = u.shape[0]
    row = lax.broadcasted_iota(jnp.int32, (tm, 1), 0)
    up = jnp.where(first, 0.0, u_prev)
    u1 = jnp.where(row == 0, up[HALO - 1:HALO, :], pltpu.roll(u, 1, 0))
    u2 = jnp.where(row == 0, up[HALO - 2:HALO - 1, :],
                   jnp.where(row == 1, up[HALO - 1:HALO, :], pltpu.roll(u, 2, 0)))
    return u1, u2


def mid_fwd(proj, o_g, lse_g, conv_w, expand, tm):
    s = proj.shape[0]
    hb = tm // HALO

    def body(ba_ref, ca_ref, xa_ref, za_ref, cah_ref, xah_ref, zb_ref,
             o0, o1, o2, l0, l1, l2, w_ref, exp_ref, ya_ref, yb_ref, at_ref, lc_ref, buf_o, buf_l):
        first = pl.program_id(0) == 0
        u = ca_ref[...].astype(F32) * xa_ref[...].astype(F32)
        u1, u2 = _conv_taps(u, cah_ref[...].astype(F32) * xah_ref[...].astype(F32), first)
        conv = w_ref[0:1, :] * u2 + w_ref[1:2, :] * u1 + w_ref[2:3, :] * u
        ya_ref[...] = (ba_ref[...].astype(F32) * conv * _silu(za_ref[...].astype(F32))).astype(BF16)
        ls = [_from_residue_major(l, buf_l.at[g], d) for g, (l, d) in enumerate(zip((l0, l1, l2), DILATIONS))]
        mx = jnp.maximum(jnp.maximum(ls[0], ls[1]), ls[2])
        es = [jnp.exp(l - mx) for l in ls]
        den = es[0] + es[1] + es[2]
        attn = jnp.zeros((tm, CB), F32)
        for e, o, d in zip(es, (o0, o1, o2), DILATIONS):
            attn = attn + _dot_hilo(e / den, exp_ref) * _from_residue_major(o, buf_o, d)
        at_ref[...] = attn
        lc_ref[...] = mx + jnp.log(den)
        yb_ref[...] = (attn * _silu(zb_ref[...].astype(F32))).astype(BF16)

    col = lambda j: pl.BlockSpec((tm, D), lambda i: (i, j))
    halo = lambda j: pl.BlockSpec((HALO, D), lambda i: (jnp.maximum(i * hb - 1, 0), j))
    loc = lambda w: pl.BlockSpec((tm, w), lambda i: (i, 0))
    rm = lambda w: [pl.BlockSpec((d, tm // d, w), lambda i: (0, i, 0)) for d in DILATIONS]
    rm_view = lambda ts, w: [t.reshape(d, s // d, w) for t, d in zip(ts, DILATIONS)]
    return pl.pallas_call(
        body, name="mid_fwd", grid=(s // tm,),
        in_specs=[col(0), col(1), col(2), col(3), halo(1), halo(2),
                  pl.BlockSpec((tm, CB), lambda i: (i, CB_ZB))] + rm(CB) + rm(LANES)
                 + [pl.BlockSpec((3, D), lambda i: (0, 0)), pl.BlockSpec(expand.shape, lambda i: (0, 0))],
        out_specs=[loc(D), loc(CB), loc(CB), loc(LANES)],
        out_shape=[jax.ShapeDtypeStruct((s, D), BF16), jax.ShapeDtypeStruct((s, CB), BF16),
                   jax.ShapeDtypeStruct((s, CB), F32), jax.ShapeDtypeStruct((s, LANES), F32)],
        scratch_shapes=[pltpu.VMEM((CB // LANES, tm, LANES), F32), pltpu.VMEM((3, 1, tm, LANES), F32)],
        compiler_params=_cp(("parallel",)))(
            proj, proj, proj, proj, proj, proj, proj, *rm_view(o_g, CB), *rm_view(lse_g, LANES), conv_w, expand)


def tail(proj, ya, yb, attn, x, target, gate, pa_w, pb_w, wo_w, total, conv_w, tm):
    s = proj.shape[0]
    ni = s // tm
    hb = tm // HALO
    nlate = NIN - CB_ZB * CB
    nearly = 4 * D

    def body(ya_ref, yb_ref, ga_ref, gb_ref, zb_ref, at_ref, x_ref, t_ref, gate_ref, pa_ref, pb_ref, wo_ref,
             tot_ref, ba_ref, ca_ref, xa_ref, za_ref, cah_ref, xah_ref, cw_ref,
             dp_hbm, dy_ref, da_ref, dc_ref, mg_ref, do_ref, dpa_ref, dpb_ref, st_ref, gwc_ref,
             stage, dconv_next, sems):
        step = pl.program_id(0)
        i = ni - 1 - step
        slot = step % 2

        def slabs(at_step, sl):
            rows = pl.ds(pl.multiple_of((ni - 1 - at_step) * tm, tm), tm)
            return (pltpu.make_async_copy(stage.at[sl, :, 0:nearly], dp_hbm.at[rows, pl.ds(0, nearly)],
                                          sems.at[sl, 0]),
                    pltpu.make_async_copy(stage.at[sl, :, nearly:], dp_hbm.at[rows, pl.ds(CB_ZB * CB, nlate)],
                                          sems.at[sl, 1]))

        @pl.when(step == 0)
        def _():
            st_ref[...] = jnp.zeros_like(st_ref)
            gwc_ref[...] = jnp.zeros_like(gwc_ref)
            dconv_next[...] = jnp.zeros_like(dconv_next)

        @pl.when(step >= 2)
        def _():
            for cp in slabs(step - 2, slot):
                cp.wait()

        gate_v = gate_ref[...]
        pa = jnp.dot(ya_ref[...], pa_ref[...], preferred_element_type=F32)
        pb = jnp.dot(yb_ref[...], pb_ref[...], preferred_element_type=F32)
        sa = jax.nn.sigmoid(ga_ref[...].astype(F32))
        sb = jax.nn.sigmoid(gb_ref[...].astype(F32))
        merged = (sa * pa + sb * pb).astype(BF16)
        mg_ref[...] = merged
        out = jnp.dot(merged, wo_ref[...], preferred_element_type=F32)
        err = x_ref[...] + gate_v * out - t_ref[...]
        dy = err * (1.0 / D)
        dy_ref[...] = dy
        st_ref[0:1, :] += jnp.sum(dy * out, axis=0, keepdims=True)
        st_ref[1:2, :] += jnp.sum(err * err, axis=0, keepdims=True)
        dout = (gate_v * dy).astype(BF16)
        do_ref[...] = dout
        dmg = lax.dot_general(dout, wo_ref[...], NT, preferred_element_type=F32)
        dpa = (dmg * sa).astype(BF16)
        dpb = (dmg * sb).astype(BF16)
        dpa_ref[...] = dpa
        dpb_ref[...] = dpb
        late = nearly
        stage[slot, :, late + CB:late + CB + D] = (dmg * pa * sa * (1.0 - sa)).astype(BF16)
        stage[slot, :, late + CB + D:] = (dmg * pb * sb * (1.0 - sb)).astype(BF16)
        dya = lax.dot_general(dpa, pa_ref[...], NT, preferred_element_type=F32)
        dyb = lax.dot_general(dpb, pb_ref[...], NT, preferred_element_type=F32)
        zb = zb_ref[...].astype(F32)
        sg = jax.nn.sigmoid(zb)
        attn_v = at_ref[...]
        dattn = dyb * (zb * sg)
        da_ref[...] = dattn.astype(BF16)
        stage[slot, :, late:late + CB] = (dyb * attn_v * (sg * (1.0 + zb * (1.0 - sg)))).astype(BF16)
        dc_ref[...] = _dot_hilo(dattn * attn_v, tot_ref)

        ba, ca, xa, za = (t[...].astype(F32) for t in (ba_ref, ca_ref, xa_ref, za_ref))
        u = ca * xa
        u1, u2 = _conv_taps(u, cah_ref[...].astype(F32) * xah_ref[...].astype(F32), i == 0)
        w0, w1, w2 = cw_ref[0:1, :], cw_ref[1:2, :], cw_ref[2:3, :]
        conv = w0 * u2 + w1 * u1 + w2 * u
        sga = jax.nn.sigmoid(za)
        sza = za * sga
        dconv = dya * ba * sza
        dcn = dconv_next[...]
        rowi = lax.broadcasted_iota(jnp.int32, (tm, 1), 0)
        d1 = jnp.where(rowi == tm - 1, dcn[0:1, :], pltpu.roll(dconv, tm - 1, 0))
        d2 = jnp.where(rowi == tm - 2, dcn[0:1, :],
                       jnp.where(rowi == tm - 1, dcn[1:2, :], pltpu.roll(dconv, tm - 2, 0)))
        du = w2 * dconv + w1 * d1 + w0 * d2
        stage[slot, :, 0:D] = (dya * conv * sza).astype(BF16)
        stage[slot, :, D:2 * D] = (du * xa).astype(BF16)
        stage[slot, :, 2 * D:3 * D] = (du * ca).astype(BF16)
        stage[slot, :, 3 * D:4 * D] = (dya * ba * conv * (sga * (1.0 + za * (1.0 - sga)))).astype(BF16)
        gwc_ref[0:1, :] += jnp.sum(dconv * u2, axis=0, keepdims=True)
        gwc_ref[1:2, :] += jnp.sum(dconv * u1, axis=0, keepdims=True)
        gwc_ref[2:3, :] += jnp.sum(dconv * u, axis=0, keepdims=True)
        dconv_next[...] = dconv[0:8, :]

        for cp in slabs(step, slot):
            cp.start()

        @pl.when(step == ni - 1)
        def _():
            for cp in slabs(step - 1, 1 - slot) + slabs(step, slot):
                cp.wait()

    rev = lambda st: ni - 1 - st
    row = lambda w: pl.BlockSpec((tm, w), lambda st: (rev(st), 0))
    pcol = lambda w, jb: pl.BlockSpec((tm, w), lambda st: (rev(st), jb))
    halo = lambda jb: pl.BlockSpec((HALO, D), lambda st: (jnp.maximum(rev(st) * hb - 1, 0), jb))
    const = lambda a: pl.BlockSpec(a.shape, lambda st: (0, 0), pipeline_mode=pl.Buffered(1))
    acc = pl.BlockSpec((8, D), lambda st: (0, 0))
    return pl.pallas_call(
        body, name="tail", grid=(ni,),
        in_specs=[row(D), row(CB), pcol(D, 9), pcol(D, 10), pcol(CB, CB_ZB), row(CB), row(D), row(D),
                  pl.BlockSpec((1, D), lambda st: (0, 0)), const(pa_w), const(pb_w), const(wo_w), const(total),
                  pcol(D, 0), pcol(D, 1), pcol(D, 2), pcol(D, 3), halo(1), halo(2),
                  pl.BlockSpec((3, D), lambda st: (0, 0))],
        out_specs=[pl.BlockSpec(memory_space=pl.ANY),
                   row(D), row(CB), row(LANES), row(D), row(D), row(D), row(D), acc, acc],
        out_shape=[jax.ShapeDtypeStruct((s, NIN), BF16), jax.ShapeDtypeStruct((s, D), F32),
                   jax.ShapeDtypeStruct((s, CB), BF16), jax.ShapeDtypeStruct((s, LANES), F32)]
                  + [jax.ShapeDtypeStruct((s, D), BF16)] * 4 + [jax.ShapeDtypeStruct((8, D), F32)] * 2,
        scratch_shapes=[pltpu.VMEM((2, tm, nearly + nlate), BF16), pltpu.VMEM((8, D), F32),
                        pltpu.SemaphoreType.DMA((2, 2))],
        compiler_params=_cp(("arbitrary",), 60))(
            ya, yb, proj, proj, proj, attn, x, target, gate, pa_w, pb_w, wo_w, total,
            proj, proj, proj, proj, proj, proj, conv_w)


def _local_step(x, target, shift, scale, gate, norm_w, conv_w, qw, kw, w_shard, small_shards, me_xyc):
    qw8, kw8 = jnp.tile(qw, (1, NH)), jnp.tile(kw, (1, NH))
    same, total, expand = _head_matrices()
    proj, ht, wg, (pa_g, pb_g, wo_g) = proj_fwd_gather(
        x, norm_w, scale, shift, w_shard, small_shards, gather_order(me_xyc), 1024)
    pa_w, wo_w = pa_g.reshape(D, D), wo_g.reshape(D, D)
    pb_w = pb_g.transpose(1, 0, 2).reshape(CB, D)
    srcs = qkv_prep(proj, qw8, kw8, same, 512)
    o_g, lse_g = zip(*[attn_fwd(*srcs[g], g, d) for g, d in enumerate(DILATIONS)])
    ya, yb, attn, lc = mid_fwd(proj, o_g, lse_g, conv_w, expand, 512)
    dproj, dy, da, dc, merged, dout, dpa, dpb, st_tail, st_conv = tail(
        proj, ya, yb, attn, x, target, gate, pa_w, pb_w, wo_w, total, conv_w, 256)
    g_wo, g_pa, g_pb = matmuls_tn([(merged, dout), (ya, dpa), (yb, dpb)], "grad_small_weights", 1024)
    grads = []
    for g, (d, (da_p, lc_p, dc_p, lt, dt)) in enumerate(zip(DILATIONS, stats_prep(da, lc, dc, 2048))):
        dq = attn_bwd_q(*srcs[g], da_p, lc_p, dc_p, g, d)
        dk, dv = attn_bwd_kv(*srcs[g], da_p, lt, dt, g, d)
        grads.append((dq, dk, dv))
    dproj, gw_qk = qkv_grads_to_dproj(dproj, proj, grads, qw8, kw8, same, 512)
    slabs = [g_pa.reshape(NDEV, 128, D), g_pb.reshape(CB, NDEV, 128).transpose(1, 0, 2), g_wo.reshape(NDEV, 128, D)]
    grad_x, st_norm, r_win, (r_pa, r_pb, r_wo) = proj_bwd(
        ht, dproj, wg, slabs, scatter_order(me_xyc), x, dy, norm_w, scale, 1024)
    dmod = jnp.concatenate([st_norm[0:1], st_norm[1:2], st_tail[0:1]], axis=1)
    loss_part = (0.5 / D) * jnp.sum(st_tail[1])
    gw_heads = gw_qk[0:2].reshape(2, NH, HD).sum(axis=1)
    small = dict(dmod=dmod, norm_w=st_norm[2:3], conv_w=st_conv[0:3],
                 q_norm_w=gw_heads[0:1], k_norm_w=gw_heads[1:2], loss=loss_part)
    return grad_x, small, (r_win, r_pa, r_pb, r_wo)


def kernel(x, c, w_ada, b_ada, norm_w, w_in, conv_w, q_norm_w, k_norm_w, w_br_conv, w_br_attn, w_out, loss_target, m_w_ada, m_b_ada, m_norm_w, m_w_in, m_conv_w, m_q_norm_w, m_k_norm_w, m_w_br_conv, m_w_br_attn, m_w_out, v_w_ada, v_b_ada, v_norm_w, v_w_in, v_conv_w, v_q_norm_w, v_k_norm_w, v_w_br_conv, v_w_br_attn, v_w_out):
    me_xyc = (lax.axis_index("x"), lax.axis_index("y"), lax.axis_index("c"))
    me = _dev_index(me_xyc)
    ncol = w_ada.shape[2]

    conv_pad = jnp.zeros((8, 128), F32).at[0:3].set(conv_w[0])
    b_cols = lax.dynamic_slice(b_ada, (0, me * ncol), (1, ncol))
    mod_pieces, c_all, conv_all = ada_fwd(c, conv_pad, w_ada[0], b_cols)
    conv_full = conv_all[:, 0:3].transpose(1, 0, 2).reshape(3, D)
    c_all = c_all.reshape(NDEV, D)
    mod = mod_pieces.reshape(1, 3 * D)
    shift, scale, gate = mod[:, 0:D], mod[:, D:2 * D], mod[:, 2 * D:3 * D]

    grad_x, small, (r_win, r_pa, r_pb, r_wo) = _local_step(
        x[0], loss_target[0], shift, scale, gate, norm_w, conv_full, q_norm_w, k_norm_w,
        w_in[0].astype(BF16), [w_br_conv[0].astype(BF16), w_br_attn[0].astype(BF16), w_out[0].astype(BF16)], me_xyc)

    packed = jnp.concatenate(
        [small["dmod"], small["norm_w"], small["conv_w"].reshape(1, 3 * D), small["q_norm_w"], small["k_norm_w"],
         jnp.full((1, 128), small["loss"], F32)], axis=1)
    packed_all, tot = gather_sum(packed)
    loss = tot[0, 7 * D + 2 * HD]
    dmod_all = packed_all[:, 0, 0:3 * D]
    g_b_ada = tot[:, 0:3 * D]
    g_norm_w = tot[:, 3 * D:4 * D]
    g_conv = lax.dynamic_slice(tot[:, 4 * D:7 * D].reshape(3, D), (0, me * 128), (3, 128))
    g_qn = tot[:, 7 * D:7 * D + HD]
    g_kn = tot[:, 7 * D + HD:7 * D + 2 * HD]
    g_w_ada = ada_bwd(c_all.T, lax.dynamic_slice(dmod_all, (0, me * ncol), (NDEV, ncol)))

    def upd(parts, w, m, v, name, rows):
        shape = w.shape
        w2, m2, v2 = (t.reshape(shape[-2:]) for t in (w, m, v))
        return [t.reshape(shape) for t in adamw(parts, w2, m2, v2, name, rows)]

    res = {"w_in": upd(r_win, w_in, m_w_in, v_w_in, "adamw_w_in", 128)}
    small_params = {"w_ada": (g_w_ada[None], w_ada, m_w_ada, v_w_ada), "b_ada": (g_b_ada[None], b_ada, m_b_ada, v_b_ada),
                    "norm_w": (g_norm_w[None], norm_w, m_norm_w, v_norm_w),
                    "conv_w": (g_conv[None], conv_w, m_conv_w, v_conv_w),
                    "q_norm_w": (g_qn[None], q_norm_w, m_q_norm_w, v_q_norm_w),
                    "k_norm_w": (g_kn[None], k_norm_w, m_k_norm_w, v_k_norm_w),
                    "w_br_conv": (r_pa, w_br_conv, m_w_br_conv, v_w_br_conv),
                    "w_br_attn": (r_pb, w_br_attn, m_w_br_attn, v_w_br_attn),
                    "w_out": (r_wo, w_out, m_w_out, v_w_out)}
    updated = adamw_small([(item[0],) + tuple(t.reshape(t.shape[-2:]) for t in item[1:])
                           for item in small_params.values()])
    for (pname, item), outs4 in zip(small_params.items(), updated):
        res[pname] = [t.reshape(item[1].shape) for t in outs4]
    names = ["w_ada", "b_ada", "norm_w", "w_in", "conv_w", "q_norm_w", "k_norm_w", "w_br_conv", "w_br_attn", "w_out"]
    return (loss, grad_x[None], *[res[n][0] for n in names], *[res[n][1] for n in names],
            *[res[n][2] for n in names], *[res[n][3] for n in names])
```

```python
import jax
import jax.numpy as jnp
from jax import lax
from jax.experimental import pallas as pl
from jax.experimental.pallas import tpu as pltpu

F32, BF16 = jnp.float32, jnp.bfloat16
D = 1024
NIN = 11264
NDEV = 8
SHARD = NIN // NDEV
HD = 64
NH = 8
QB = 128
CB = 512
CB_Q, CB_K, CB_V, CB_ZB = 8, 11, 14, 17
DILATIONS = (1, 4, 16)
EPS = 1e-6
NEG = -1e30
HALO = 16
LANES = 128
MESH = pl.DeviceIdType.MESH

ADAM_LR, ADAM_B1, ADAM_B2, ADAM_EPS, ADAM_WD, ADAM_STEP = 0.001, 0.9, 0.999, 1e-08, 0.01, 10

NT = (((1,), (1,)), ((), ()))
TN = (((0,), (0,)), ((), ()))


def _cp(sem, vmem_mb=48):
    return pltpu.CompilerParams(dimension_semantics=sem, vmem_limit_bytes=vmem_mb << 20)


def _silu(z):
    return z * jax.nn.sigmoid(z)


def _coords():
    return lax.axis_index("x"), lax.axis_index("y"), lax.axis_index("c")


FLIPS = [(fx, fy, fc) for fx in (0, 1) for fy in (0, 1) for fc in (0, 1)][1:]


def gather_sum(vec):
    def body(v_ref, all_ref, sum_ref, send_sems, recv_sems, local_sem):
        me_xyc = _coords()
        me = _dev_index(me_xyc)
        peers = [_flip(me_xyc, f) for f in FLIPS]

        def copy(k, block):
            return pltpu.make_async_remote_copy(
                src_ref=v_ref, dst_ref=all_ref.at[block], send_sem=send_sems.at[k], recv_sem=recv_sems.at[k],
                device_id=peers[k], device_id_type=MESH)

        mine = pltpu.make_async_copy(v_ref, all_ref.at[me], local_sem)
        sends = [copy(k, me) for k in range(7)]
        for cp in [mine] + sends:
            cp.start()
        for k in range(7):
            copy(k, _dev_index(peers[k])).wait_recv()
        mine.wait()
        acc = all_ref[0]
        for b in range(1, NDEV):
            acc = acc + all_ref[b]
        sum_ref[...] = acc
        for cp in sends:
            cp.wait_send()

    return pl.pallas_call(
        body, name="gather_sum",
        out_shape=[jax.ShapeDtypeStruct((NDEV,) + vec.shape, F32), jax.ShapeDtypeStruct(vec.shape, F32)],
        scratch_shapes=[pltpu.SemaphoreType.DMA((7,)), pltpu.SemaphoreType.DMA((7,)), pltpu.SemaphoreType.DMA],
    )(vec)


def _flip(dev, f):
    return tuple(1 - v if b else v for v, b in zip(dev, f))


def _dev_index(dev):
    return 4 * dev[0] + 2 * dev[1] + dev[2]


def _chip_order(x, y, c):
    xor = lambda a, b: a + b - 2 * a * b
    return [(xor(x, 1 - c), xor(y, c)), (xor(x, c), xor(y, 1 - c)), (1 - x, 1 - y)]


def gather_order(me_xyc):
    x, y, c = me_xyc
    chips = _chip_order(x, y, c)
    devs = [(x, y, c), (x, y, 1 - c), (*chips[0], c), (*chips[1], c),
            (*chips[1], 1 - c), (*chips[0], 1 - c), (*chips[2], c), (*chips[2], 1 - c)]
    return jnp.stack([_dev_index(d) for d in devs]).astype(jnp.int32)


def scatter_order(me_xyc):
    devs = [_flip(me_xyc, f) for f in FLIPS] + [me_xyc]
    return jnp.stack([_dev_index(d) for d in devs]).astype(jnp.int32)


def ada_fwd(c, conv_pad, w_ada, b_cols):
    ncol = w_ada.shape[1]

    def body(c_ref, cv_ref, w_ref, b_ref, mod_ref, call_ref, cvall_ref, rows_buf, send_sems, recv_sems, local_sems):
        me_xyc = _coords()
        me = _dev_index(me_xyc)
        peers = [_flip(me_xyc, f) for f in FLIPS]
        pids = [_dev_index(p) for p in peers]

        def copy(a, k, src, dst):
            return pltpu.make_async_remote_copy(src_ref=src, dst_ref=dst, send_sem=send_sems.at[a, k],
                                                recv_sem=recv_sems.at[a, k], device_id=peers[k], device_id_type=MESH)

        own = [pltpu.make_async_copy(c_ref, call_ref.at[me], local_sems.at[0]),
               pltpu.make_async_copy(cv_ref, cvall_ref.at[me], local_sems.at[1])]
        first = [copy(0, k, c_ref, call_ref.at[me]) for k in range(7)]
        first += [copy(1, k, cv_ref, cvall_ref.at[me]) for k in range(7)]
        for cp in own + first:
            cp.start()
        own[0].wait()
        for k in range(7):
            copy(0, k, c_ref, call_ref.at[pids[k]]).wait_recv()
        seq = lax.broadcasted_iota(jnp.int32, (NDEV, 1), 0)
        c_all = jnp.zeros((NDEV, D), F32)
        for p in range(NDEV):
            c_all = jnp.where(seq == p, call_ref[p], c_all)
        mods = jnp.dot(_silu(c_all).astype(BF16), w_ref[...].astype(BF16), preferred_element_type=F32) + b_ref[...]
        for p in range(NDEV):
            rows_buf[p] = mods[p:p + 1, :]
        mine = pltpu.make_async_copy(rows_buf.at[me], mod_ref.at[me], local_sems.at[2])
        second = [copy(2, k, rows_buf.at[pids[k]], mod_ref.at[me]) for k in range(7)]
        for cp in [mine] + second:
            cp.start()
        for k in range(7):
            copy(2, k, rows_buf.at[pids[k]], mod_ref.at[pids[k]]).wait_recv()
            copy(1, k, cv_ref, cvall_ref.at[pids[k]]).wait_recv()
        for cp in first + second:
            cp.wait_send()
        own[1].wait()
        mine.wait()

    return pl.pallas_call(
        body, name="ada_fwd",
        out_shape=[jax.ShapeDtypeStruct((NDEV, 1, ncol), F32), jax.ShapeDtypeStruct((NDEV, 1, D), F32),
                   jax.ShapeDtypeStruct((NDEV,) + conv_pad.shape, F32)],
        scratch_shapes=[pltpu.VMEM((NDEV, 1, ncol), F32), pltpu.SemaphoreType.DMA((3, 7)),
                        pltpu.SemaphoreType.DMA((3, 7)), pltpu.SemaphoreType.DMA((3,))],
    )(c, conv_pad, w_ada, b_cols)


def ada_bwd(c_all_t, dmod_cols):
    def body(c_ref, d_ref, o_ref):
        at = _silu(c_ref[...])
        acc = at[:, 0:1] * d_ref[0:1, :]
        for b in range(1, NDEV):
            acc = acc + at[:, b:b + 1] * d_ref[b:b + 1, :]
        o_ref[...] = acc

    return pl.pallas_call(body, name="ada_bwd",
                          out_shape=jax.ShapeDtypeStruct((D, dmod_cols.shape[1]), F32))(c_all_t, dmod_cols)


def _adamw_update(g, w_ref, m_ref, v_ref, g_ref, d_ref, nm_ref, nv_ref):
    nm = ADAM_B1 * m_ref[...] + (1.0 - ADAM_B1) * g
    nv = ADAM_B2 * v_ref[...] + (1.0 - ADAM_B2) * (g * g)
    g_ref[...] = g
    nm_ref[...] = nm
    nv_ref[...] = nv
    m_hat = nm / (1.0 - ADAM_B1 ** ADAM_STEP)
    v_hat = nv / (1.0 - ADAM_B2 ** ADAM_STEP)
    d_ref[...] = -ADAM_LR * (m_hat / (jnp.sqrt(v_hat) + ADAM_EPS) + ADAM_WD * w_ref[...])


def adamw_small(items):
    n = len(items)

    def body(*refs):
        ins, outs = refs[:4 * n], refs[4 * n:]
        for a in range(n):
            p_ref, w_ref, m_ref, v_ref = ins[4 * a:4 * a + 4]
            g = p_ref[0].astype(F32)
            for b in range(1, p_ref.shape[0]):
                g = g + p_ref[b].astype(F32)
            _adamw_update(g, w_ref, m_ref, v_ref, *outs[4 * a:4 * a + 4])

    out = pl.pallas_call(
        body, name="adamw_small",
        out_shape=[jax.ShapeDtypeStruct(it[1].shape, F32) for it in items for _ in range(4)],
        compiler_params=pltpu.CompilerParams(vmem_limit_bytes=48 << 20))(*[t for it in items for t in it])
    return [out[4 * a:4 * a + 4] for a in range(n)]


def adamw(parts, w, m, v, name, rows):
    n, r, ccols = parts.shape

    def body(p_ref, w_ref, m_ref, v_ref, g_ref, d_ref, nm_ref, nv_ref):
        g = p_ref[0].astype(F32)
        for b in range(1, n):
            g = g + p_ref[b].astype(F32)
        _adamw_update(g, w_ref, m_ref, v_ref, g_ref, d_ref, nm_ref, nv_ref)

    blk = pl.BlockSpec((rows, ccols), lambda i: (i, 0))
    out = jax.ShapeDtypeStruct((r, ccols), F32)
    return pl.pallas_call(
        body, name=name, grid=(r // rows,),
        in_specs=[pl.BlockSpec((n, rows, ccols), lambda i: (0, i, 0)), blk, blk, blk],
        out_specs=[blk] * 4, out_shape=[out] * 4, compiler_params=_cp(("parallel",)))(parts, w, m, v)


def proj_fwd_gather(x, nw, scale, shift, w_shard, extras, order, tm):
    s = x.shape[0]
    ni = s // tm
    n = 1 + len(extras)
    mid = ni - 2

    def body(order_ref, x_ref, nw_ref, sc_ref, sh_ref, *refs):
        ins, o_ref, ht_ref, outs = refs[:n], refs[n], refs[n + 1], refs[n + 2:2 * n + 2]
        h_all, wbuf, send_sems, recv_sems, local_sems, load_sems = refs[2 * n + 2:]
        jj, i = pl.program_id(0), pl.program_id(1)
        x, y, c = _coords()
        me, sibling = (x, y, c), (x, y, 1 - c)
        chips = _chip_order(x, y, c)
        relayed = [(*chips[1], 1 - c), (*chips[0], 1 - c), (*chips[2], 1 - c)]

        def slot(a, dev):
            return outs[a].at[_dev_index(dev)]

        def copy(a, k, block, to, src=None):
            return pltpu.make_async_remote_copy(
                src_ref=slot(a, block) if src is None else src, dst_ref=slot(a, block),
                send_sem=send_sems.at[a, k], recv_sem=recv_sems.at[a, k], device_id=to, device_id_type=MESH)

        mine = [pltpu.make_async_copy(ins[a], slot(a, me), local_sems.at[a]) for a in range(n)]
        to_sibling = [copy(a, 0, me, sibling, src=ins[a]) for a in range(n)]
        to_chip = [[copy(a, 1 + j, me, (*chips[j], c), src=ins[a]) for a in range(n)] for j in range(2)]
        onward = [copy(a, 3, (*chips[1], c), (*chips[0], c)) for a in range(n)]
        passed = [[copy(a, 4 + j, (*ch, c), sibling) for a in range(n)] for j, ch in enumerate(chips)]
        sends = lambda a: [to_sibling[a], to_chip[0][a], to_chip[1][a], onward[a]] + [passed[j][a] for j in range(3)]

        def arrived(a, j):
            copy(a, 1 + j, (*chips[j], c), me).wait_recv()

        def load(row):
            return pltpu.make_async_copy(outs[0].at[order_ref[row]], wbuf.at[row % 2], load_sems.at[row % 2])

        @pl.when((jj == 0) & (i == 0))
        def _():
            for cp in mine:
                cp.start()
            to_sibling[0].start()
            to_chip[0][0].start()
            pltpu.make_async_copy(ins[0], wbuf.at[0], load_sems.at[0]).start()

        @pl.when((jj == 1) & (i == 0))
        def _():
            to_chip[1][0].start()

        @pl.when((jj == 4) & (i == 0))
        def _():
            for a in range(1, n):
                to_sibling[a].start()
                to_chip[0][a].start()
                to_chip[1][a].start()

        direct = {2: 0, 3: 1, 6: 2}
        relay = {4: 0, 5: 1, 7: 2}

        @pl.when((jj == 0) & (i == mid))
        def _():
            copy(0, 0, sibling, me).wait_recv()

        for row, j in direct.items():
            @pl.when((jj == row - 1) & (i == mid))
            def _(j=j):
                arrived(0, j)
                passed[j][0].start()
                if j == 1:
                    onward[0].start()

        for row, j in relay.items():
            @pl.when((jj == row - 1) & (i == mid))
            def _(j=j):
                copy(0, 4 + j, relayed[j], me).wait_recv()

        @pl.when((jj == NDEV - 1) & (i == 0))
        def _():
            for a in range(1, n):
                arrived(a, 1)
                onward[a].start()
                passed[1][a].start()
                arrived(a, 0)
                passed[0][a].start()

        @pl.when((jj < NDEV - 1) & (i == mid))
        def _():
            load(jj + 1).start()

        @pl.when(i == 0)
        def _():
            load(jj).wait()

        @pl.when(jj == 0)
        def _():
            xf = x_ref[...]
            r = lax.rsqrt(jnp.mean(xf * xf, axis=-1, keepdims=True) + EPS)
            h = (xf * r * nw_ref[...]) * (1.0 + sc_ref[...]) + sh_ref[...]
            h_all[i] = h.astype(BF16)
            ht_ref[...] = h.T.astype(BF16)

        o_ref[...] = jnp.dot(h_all[i], wbuf[jj % 2], preferred_element_type=F32).astype(BF16)

        @pl.when((jj == NDEV - 1) & (i == ni - 1))
        def _():
            for a in range(1, n):
                arrived(a, 2)
                passed[2][a].start()
            for a in range(1, n):
                copy(a, 0, sibling, me).wait_recv()
                for j in range(3):
                    copy(a, 4 + j, relayed[j], me).wait_recv()
            for a in range(n):
                mine[a].wait()
                for cp in sends(a):
                    cp.wait_send()

    any_spec = pl.BlockSpec(memory_space=pl.ANY)
    vec = pl.BlockSpec((1, D), lambda jj, i, o: (0, 0))
    outs = pl.pallas_call(
        body, name="proj_fwd_gather",
        grid_spec=pltpu.PrefetchScalarGridSpec(
            num_scalar_prefetch=1, grid=(NDEV, ni),
            in_specs=[pl.BlockSpec((tm, D), lambda jj, i, o: (jnp.where(jj == 0, i, ni - 1), 0))] + [vec] * 3
                     + [any_spec] * n,
            out_specs=[pl.BlockSpec((tm, SHARD), lambda jj, i, o: (i, o[jj])),
                       pl.BlockSpec((D, tm), lambda jj, i, o: (0, jnp.where(jj == 0, i, ni - 1)))]
                      + [any_spec] * n,
            scratch_shapes=[pltpu.VMEM((ni, tm, D), BF16), pltpu.VMEM((2, D, SHARD), BF16),
                            pltpu.SemaphoreType.DMA((n, 7)), pltpu.SemaphoreType.DMA((n, 7)),
                            pltpu.SemaphoreType.DMA((n,)), pltpu.SemaphoreType.DMA((2,))]),
        out_shape=[jax.ShapeDtypeStruct((s, NIN), BF16), jax.ShapeDtypeStruct((D, s), BF16),
                   jax.ShapeDtypeStruct((NDEV, D, SHARD), BF16)]
                  + [jax.ShapeDtypeStruct((NDEV,) + e.shape, e.dtype) for e in extras],
        compiler_params=_cp(("arbitrary", "arbitrary"), 56))(order, x, nw, scale, shift, w_shard, *extras)
    return outs[0], outs[1], outs[2], outs[3:]


def proj_bwd(ht, dproj, wg, smalls, order, x, dy, nw, scale, tt):
    s = dproj.shape[0]
    nk = s // tt
    n = len(smalls)
    rows_per_step = tt // nk
    last = 2 * NDEV

    def body(order_ref, ht_ref, dp_ref, w_ref, x_ref, dy_ref, nw_ref, sc_ref, *rest):
        small_in = rest[:n]
        gx_ref, st_ref, gw_ref, rwin_ref = rest[n:n + 4]
        small_out = rest[n + 4:2 * n + 4]
        acc, stage, dh, send_sems, recv_sems, local_sems, stage_sems = rest[2 * n + 4:]
        t, k = pl.program_id(0), pl.program_id(1)
        me_xyc = _coords()
        me = _dev_index(me_xyc)
        peers = [_flip(me_xyc, f) for f in FLIPS]

        def exchange(a, kf, src_arr, dst_arr):
            pid = _dev_index(peers[kf])
            mk = lambda dst: pltpu.make_async_remote_copy(
                src_ref=src_arr.at[pid], dst_ref=dst, send_sem=send_sems.at[a, kf], recv_sem=recv_sems.at[a, kf],
                device_id=peers[kf], device_id_type=MESH)
            return mk(dst_arr.at[me]), mk(dst_arr.at[pid])

        small_pairs = [exchange(1 + a, kf, small_in[a], small_out[a]) for kf in range(7) for a in range(n)]
        small_own = [pltpu.make_async_copy(small_in[a].at[me], small_out[a].at[me], local_sems.at[1 + a])
                     for a in range(n)]
        win_pairs = [exchange(0, kf, gw_ref, rwin_ref) for kf in range(7)]
        win_own = pltpu.make_async_copy(gw_ref.at[me], rwin_ref.at[me], local_sems.at[0])

        def to_hbm(jj):
            slab = me if jj == 7 else _dev_index(peers[jj])
            return pltpu.make_async_copy(stage.at[jj % 2], gw_ref.at[slab], stage_sems.at[jj % 2])

        @pl.when((t == 0) & (k == 0))
        def _():
            for cp in small_own:
                cp.start()
            for send, _ in small_pairs:
                send.start()

        @pl.when(t < NDEV)
        def _():
            p = jnp.dot(ht_ref[...], dp_ref[...], preferred_element_type=F32)

            @pl.when(k == 0)
            def _():
                acc[...] = p

            @pl.when(k > 0)
            def _():
                acc[...] += p

        for jj in range(NDEV):
            @pl.when((t == jj) & (k == nk - 1))
            def _(jj=jj):
                stage[jj % 2] = acc[...].astype(BF16)
                to_hbm(jj).start()

            @pl.when((t == jj + 1) & (k == 1))
            def _(jj=jj):
                to_hbm(jj).wait()
                if jj < 7:
                    win_pairs[jj][0].start()
                else:
                    win_own.start()

        def matmul_step():
            p = lax.dot_general(dp_ref[...], w_ref[...], NT, preferred_element_type=F32)
            slot = t % 2
            dh[slot] = jnp.where(k == 0, p, dh[slot] + p)

        def norm_step():
            g = dh.at[(t + 1) % 2][pl.ds(pl.multiple_of(k * rows_per_step, rows_per_step), rows_per_step), :]
            xf = x_ref[...]
            r = lax.rsqrt(jnp.mean(xf * xf, axis=-1, keepdims=True) + EPS)
            xh = xf * r
            dn = g * (1.0 + sc_ref[...])
            dxh = dn * nw_ref[...]
            gx_ref[...] = dy_ref[...] + r * (dxh - xh * jnp.mean(dxh * xh, axis=-1, keepdims=True))
            st_ref[0:1, :] += jnp.sum(g, axis=0, keepdims=True)
            st_ref[1:2, :] += jnp.sum(g * xh * nw_ref[...], axis=0, keepdims=True)
            st_ref[2:3, :] += jnp.sum(dn * xh, axis=0, keepdims=True)

        @pl.when((t == 0) & (k == 0))
        def _():
            st_ref[...] = jnp.zeros_like(st_ref)

        @pl.when(t == NDEV)
        def _():
            matmul_step()

        @pl.when((t > NDEV) & (t < last))
        def _():
            matmul_step()
            norm_step()

        @pl.when(t == last)
        def _():
            norm_step()

        @pl.when((t == last) & (k == nk - 1))
        def _():
            for _, recv in win_pairs + small_pairs:
                recv.wait_recv()
            for send, _ in win_pairs + small_pairs:
                send.wait_send()
            win_own.wait()
            for cp in small_own:
                cp.wait()

    any_spec = pl.BlockSpec(memory_space=pl.ANY)
    first = lambda t: t < NDEV
    slab = lambda t, k: jnp.where(t == last, NDEV - 1, k)
    chunk = pl.BlockSpec((rows_per_step, D), lambda t, k, o: (jnp.maximum((t - NDEV - 1) * nk + k, 0), 0))
    vec = pl.BlockSpec((1, D), lambda t, k, o: (0, 0))
    outs = pl.pallas_call(
        body, name="proj_bwd",
        grid_spec=pltpu.PrefetchScalarGridSpec(
            num_scalar_prefetch=1, grid=(last + 1, nk),
            in_specs=[pl.BlockSpec((D, tt), lambda t, k, o: (0, jnp.where(first(t), k, nk - 1))),
                      pl.BlockSpec((tt, SHARD), lambda t, k, o: (jnp.where(first(t), k, jnp.minimum(t, last - 1) - NDEV),
                                                                 jnp.where(first(t), o[jnp.minimum(t, NDEV - 1)],
                                                                           slab(t, k)))),
                      pl.BlockSpec((None, D, SHARD), lambda t, k, o: (jnp.where(first(t), 0, slab(t, k)), 0, 0)),
                      chunk, chunk, vec, vec]
                     + [any_spec] * n,
            out_specs=[chunk, pl.BlockSpec((8, D), lambda t, k, o: (0, 0))] + [any_spec] * (2 + n),
            scratch_shapes=[pltpu.VMEM((D, SHARD), F32), pltpu.VMEM((2, D, SHARD), BF16),
                            pltpu.VMEM((2, tt, D), F32),
                            pltpu.SemaphoreType.DMA((1 + n, 7)), pltpu.SemaphoreType.DMA((1 + n, 7)),
                            pltpu.SemaphoreType.DMA((1 + n,)), pltpu.SemaphoreType.DMA((2,))]),
        out_shape=[jax.ShapeDtypeStruct((s, D), F32), jax.ShapeDtypeStruct((8, D), F32),
                   jax.ShapeDtypeStruct((NDEV, D, SHARD), BF16), jax.ShapeDtypeStruct((NDEV, D, SHARD), BF16)]
                  + [jax.ShapeDtypeStruct(a.shape, a.dtype) for a in smalls],
        compiler_params=_cp(("arbitrary", "arbitrary"), 56))(order, ht, dproj, wg, x, dy, nw, scale, *smalls)
    return outs[0], outs[1], outs[3], outs[4:]


def matmuls_tn(pairs, name, tk):
    s = pairs[0][0].shape[0]
    nk = s // tk
    n = len(pairs)
    shapes = [(a.shape[1], b.shape[1]) for a, b in pairs]

    def body(*refs):
        ins, outs, accs = refs[:2 * n], refs[2 * n:3 * n], refs[3 * n:]
        k = pl.program_id(0)
        for j in range(n):
            p = lax.dot_general(ins[2 * j][...], ins[2 * j + 1][...], TN, preferred_element_type=F32)
            accs[j][...] = jnp.where(k == 0, p, accs[j][...] + p)

        @pl.when(k == nk - 1)
        def _():
            for j in range(n):
                outs[j][...] = accs[j][...].astype(BF16)

    return pl.pallas_call(
        body, name=name, grid=(nk,),
        in_specs=[pl.BlockSpec((tk, t.shape[1]), lambda k: (k, 0)) for pair in pairs for t in pair],
        out_specs=[pl.BlockSpec(sh, lambda k: (0, 0)) for sh in shapes],
        out_shape=[jax.ShapeDtypeStruct(sh, BF16) for sh in shapes],
        scratch_shapes=[pltpu.VMEM(sh, F32) for sh in shapes],
        compiler_params=_cp(("arbitrary",), 56))(*[t for pair in pairs for t in pair])


def _head_matrices():
    lane = lax.broadcasted_iota(jnp.int32, (CB, CB), 0)
    col = lax.broadcasted_iota(jnp.int32, (CB, CB), 1)
    same = (lane // HD == col // HD).astype(BF16)
    lane_c = lax.broadcasted_iota(jnp.int32, (CB, LANES), 0)
    col_c = lax.broadcasted_iota(jnp.int32, (CB, LANES), 1)
    total = (lane_c // HD == col_c).astype(BF16)
    lane_e = lax.broadcasted_iota(jnp.int32, (LANES, CB), 0)
    col_e = lax.broadcasted_iota(jnp.int32, (LANES, CB), 1)
    expand = (lane_e == col_e // HD).astype(BF16)
    return same, total, expand


def _head_sum(x, m_ref):
    return jnp.dot(x.astype(BF16), m_ref[...], preferred_element_type=F32)


def _dot_hilo(x, m_ref):
    hi = x.astype(BF16)
    lo = (x - hi.astype(F32)).astype(BF16)
    return (jnp.dot(hi, m_ref[...], preferred_element_type=F32)
            + jnp.dot(lo, m_ref[...], preferred_element_type=F32))


def _to_residue_major(val, buf, out_ref, dil):
    rows = out_ref.shape[1]
    for k in range(val.shape[1] // LANES):
        lanes = slice(k * LANES, (k + 1) * LANES)
        buf[k] = val[:, lanes]
        for r in range(dil):
            out_ref[r, :, lanes] = buf.at[k][pl.ds(r, rows, stride=dil), :].astype(out_ref.dtype)


def _from_residue_major(ref, buf, dil):
    if dil == 1:
        return ref[0].astype(F32)
    rows, chunks = ref.shape[1], ref.shape[2] // LANES
    for k in range(chunks):
        for r in range(dil):
            buf.at[k][pl.ds(r, rows, stride=dil), :] = ref[r, :, k * LANES:(k + 1) * LANES].astype(F32)
    return jnp.concatenate([buf[k] for k in range(chunks)], axis=1)


def qkv_prep(proj, qw8, kw8, same, tm):
    s = proj.shape[0]
    items = []
    for g, d in enumerate(DILATIONS):
        items += [(g, "q", CB_Q + g, d), (g, "k", CB_K + g, d)] + ([(g, "v", CB_V + g, d)] if d > 1 else [])
    n = len(items)

    def body(*refs):
        ins, (qw_ref, kw_ref, same_ref), outs, buf = refs[:n], refs[n:n + 3], refs[n + 3:2 * n + 3], refs[-1]
        for idx, (_, kind, _, dil) in enumerate(items):
            val = ins[idx][...].astype(F32)
            if kind != "v":
                r = lax.rsqrt(_head_sum(val * val, same_ref) * (1.0 / HD) + EPS)
                val = val * r * (qw_ref if kind == "q" else kw_ref)[...]
            if dil == 1:
                outs[idx][0] = val.astype(BF16)
            else:
                _to_residue_major(val, buf, outs[idx], dil)

    full = lambda a: pl.BlockSpec(a.shape, lambda i: (0, 0))
    outs = pl.pallas_call(
        body, name="qkv_prep", grid=(s // tm,),
        in_specs=[pl.BlockSpec((tm, CB), lambda i, cb=cb: (i, cb)) for _, _, cb, _ in items]
                 + [full(qw8), full(kw8), full(same)],
        out_specs=[pl.BlockSpec((d, tm // d, CB), lambda i: (0, i, 0)) for _, _, _, d in items],
        out_shape=[jax.ShapeDtypeStruct((d, s // d, CB), BF16) for _, _, _, d in items],
        scratch_shapes=[pltpu.VMEM((CB // LANES, tm, LANES), F32)],
        compiler_params=_cp(("parallel",)))(*([proj] * n), qw8 * (HD ** -0.5), kw8, same)
    srcs = [[None, None, (proj, CB_V + g)] for g in range(len(DILATIONS))]
    for (g, kind, _, _), o in zip(items, outs):
        srcs[g]["qkv".index(kind)] = (o.reshape(s, CB), 0)
    return srcs


def stats_prep(da, lc, dc, tm):
    s = da.shape[0]

    def body(da_ref, lc_ref, dc_ref, *refs):
        outs, buf = list(refs[:-1]), refs[-1]
        for dil in DILATIONS:
            rows = tm // dil
            if dil > 1:
                _to_residue_major(da_ref[...].astype(F32), buf, outs.pop(0), dil)
            for src in (lc_ref, dc_ref):
                dst = outs.pop(0) if dil > 1 else None
                dst_t = outs.pop(0)
                buf[0] = src[...]
                for r in range(dil):
                    piece = buf.at[0][pl.ds(r, rows, stride=dil), :] if dil > 1 else buf[0]
                    if dil > 1:
                        dst[r] = piece
                    dst_t[r] = piece.T[0:NH, :]

    row = lambda w: pl.BlockSpec((tm, w), lambda i: (i, 0))
    out_specs, out_shape = [], []
    for dil in DILATIONS:
        rm = lambda w, dil=dil: (pl.BlockSpec((dil, tm // dil, w), lambda i: (0, i, 0)),
                                 jax.ShapeDtypeStruct((dil, s // dil, w), BF16 if w == CB else F32))
        tr = (pl.BlockSpec((dil, NH, tm // dil), lambda i: (0, 0, i)), jax.ShapeDtypeStruct((dil, NH, s // dil), F32))
        group = ([rm(CB)] if dil > 1 else []) + ([rm(LANES), tr, rm(LANES), tr] if dil > 1 else [tr, tr])
        out_specs += [sp for sp, _ in group]
        out_shape += [sh for _, sh in group]
    outs = list(pl.pallas_call(
        body, name="stats_prep", grid=(s // tm,),
        in_specs=[row(CB), row(LANES), row(LANES)], out_specs=out_specs, out_shape=out_shape,
        scratch_shapes=[pltpu.VMEM((CB // LANES, tm, LANES), F32)],
        compiler_params=_cp(("parallel",)))(da, lc, dc))
    res = []
    for dil in DILATIONS:
        flat_t = lambda t, dil=dil: t.reshape(dil * NH, s // dil)
        if dil == 1:
            lt, dt = outs.pop(0), outs.pop(0)
            res.append((da, lc, dc, flat_t(lt), flat_t(dt)))
        else:
            dap, lcp, lt, dcp, dt = (outs.pop(0) for _ in range(5))
            res.append((dap.reshape(s, CB), lcp.reshape(s, LANES), dcp.reshape(s, LANES), flat_t(lt), flat_t(dt)))
    return res


def qkv_grads_to_dproj(dproj, proj, grads, qw8, kw8, same, tm):
    s = dproj.shape[0]
    ni = s // tm
    flat = [(t.reshape(d, s // d, CB), d, kind, 3 * kind + g)
            for g, d in enumerate(DILATIONS) for kind, t in enumerate(grads[g])]
    nf = len(flat)
    nraw = 2 * len(DILATIONS)

    def body(*refs):
        dp_hbm, raws, ins = refs[nraw + nf + 4], refs[1:1 + nraw], refs[1 + nraw:1 + nraw + nf]
        qw_ref, kw_ref, same_ref = refs[1 + nraw + nf:4 + nraw + nf]
        gw_ref, stage, buf, sems = refs[5 + nraw + nf:]
        i = pl.program_id(0)
        slot = i % 2

        def slab(step, sl):
            return pltpu.make_async_copy(
                stage.at[sl], dp_hbm.at[pl.ds(pl.multiple_of(step * tm, tm), tm), pl.ds(CB_Q * CB, 9 * CB)],
                sems.at[sl])

        @pl.when(i == 0)
        def _():
            gw_ref[...] = jnp.zeros_like(gw_ref)

        @pl.when(i >= 2)
        def _():
            slab(i - 2, slot).wait()

        for ref, (_, d, kind, jj) in zip(ins, flat):
            cols = slice(jj * CB, (jj + 1) * CB)
            dn = _from_residue_major(ref, buf, d)
            if kind == 2:
                stage[slot, :, cols] = dn.astype(BF16)
                continue
            t = raws[jj][...].astype(F32)
            r = lax.rsqrt(_head_sum(t * t, same_ref) * (1.0 / HD) + EPS)
            xh = t * r
            gw_ref[kind:kind + 1, :] += jnp.sum(dn * xh, axis=0, keepdims=True)
            dxh = dn * (qw_ref if kind == 0 else kw_ref)[...]
            mean = _head_sum(dxh * xh, same_ref) * (1.0 / HD)
            stage[slot, :, cols] = (r * (dxh - xh * mean)).astype(BF16)
        slab(i, slot).start()

        @pl.when(i == ni - 1)
        def _():
            slab(i - 1, 1 - slot).wait()
            slab(i, slot).wait()

    full = lambda a: pl.BlockSpec(a.shape, lambda i: (0, 0))
    any_spec = pl.BlockSpec(memory_space=pl.ANY)
    return pl.pallas_call(
        body, name="qkv_grads_to_dproj", grid=(ni,),
        in_specs=[any_spec] + [pl.BlockSpec((tm, CB), lambda i, jb=jb: (i, CB_Q + jb)) for jb in range(nraw)]
                 + [pl.BlockSpec((d, tm // d, CB), lambda i: (0, i, 0)) for _, d, _, _ in flat]
                 + [full(qw8), full(kw8), full(same)],
        out_specs=[any_spec, pl.BlockSpec((8, CB), lambda i: (0, 0))],
        out_shape=[jax.ShapeDtypeStruct((s, NIN), BF16), jax.ShapeDtypeStruct((8, CB), F32)],
        input_output_aliases={0: 0},
        scratch_shapes=[pltpu.VMEM((2, tm, 9 * CB), BF16), pltpu.VMEM((CB // LANES, tm, LANES), F32),
                        pltpu.SemaphoreType.DMA((2,))],
        compiler_params=_cp(("arbitrary",)))(
            dproj, *([proj] * nraw), *[t for t, _, _, _ in flat], qw8, kw8, same)


def _lane_lo():
    return lax.broadcasted_iota(jnp.int32, (1, 2 * HD), 1) < HD


def _stack_heads(t, lo):
    zero = jnp.zeros_like(t)
    return jnp.concatenate([jnp.where(lo, t, zero), jnp.where(lo, zero, t)], axis=0)


def _masks(other_ok):
    qi = lax.broadcasted_iota(jnp.int32, (QB, QB), 0)
    kj = lax.broadcasted_iota(jnp.int32, (QB, QB), 1)
    return (kj >= qi) & other_ok, kj <= qi


MAX_SUB = 8


def _attn_specs(nb, dil, sub):
    steps = nb // sub
    main = lambda cb, w=CB: pl.BlockSpec((sub * QB, w), lambda r, s: (r * steps + s, cb))
    prev = lambda cb: pl.BlockSpec((QB, CB), lambda r, s: (jnp.maximum(r * nb + sub * s - 1, 0), cb))
    nxt = lambda cb: pl.BlockSpec((QB, CB), lambda r, s: (jnp.minimum(r * nb + sub * (s + 1), dil * nb - 1), cb))
    return main, prev, nxt


def attn_fwd(q_src, k_src, v_src, g, dil):
    s = q_src[0].shape[0]
    nb = s // dil // QB
    sub = min(MAX_SUB, nb)
    main, prev, _ = _attn_specs(nb, dil, sub)

    def body(q_ref, kp_ref, k_ref, vp_ref, v_ref, o_ref, l_ref, kbuf, vbuf):
        step = pl.program_id(1)
        kbuf[0:QB], kbuf[QB:] = kp_ref[...], k_ref[...]
        vbuf[0:QB], vbuf[QB:] = vp_ref[...], v_ref[...]
        lo = _lane_lo()
        head_lane = lax.broadcasted_iota(jnp.int32, (1, LANES), 1)

        def block(j, carry):
            r0 = pl.multiple_of(j * QB, QB)
            rows, krows = pl.ds(r0, QB), pl.ds(r0, 2 * QB)
            m_prev, m_cur = _masks(step * sub + j > 0)
            mask = jnp.concatenate([m_prev, m_cur], axis=1)
            mask = jnp.concatenate([mask, mask], axis=0)
            lses = jnp.zeros((QB, LANES), F32)
            for i in range(NH // 2):
                sl = slice(2 * HD * i, 2 * HD * (i + 1))
                qs, ks, vv = q_ref[rows, sl], kbuf[krows, sl], vbuf[krows, sl]
                sc = lax.dot_general(_stack_heads(qs, lo), ks, NT, preferred_element_type=F32)
                sc = jnp.where(mask, sc, NEG)
                mx = jnp.max(sc, axis=-1, keepdims=True)
                p = jnp.exp(sc - mx)
                den = jnp.sum(p, axis=-1, keepdims=True)
                o = jnp.dot(p.astype(BF16), vv, preferred_element_type=F32) * (1.0 / den)
                lse = mx + jnp.log(den)
                o_ref[rows, sl] = jnp.where(lo, o[:QB], o[QB:]).astype(BF16)
                lses = jnp.where(head_lane == 2 * i, lse[:QB], jnp.where(head_lane == 2 * i + 1, lse[QB:], lses))
            l_ref[rows, :] = lses
            return carry

        lax.fori_loop(0, sub, block, 0, unroll=True)

    return pl.pallas_call(
        body, name=f"attn_fwd_g{g}", grid=(dil, nb // sub),
        in_specs=[main(q_src[1]), prev(k_src[1]), main(k_src[1]), prev(v_src[1]), main(v_src[1])],
        out_specs=[main(0), main(0, LANES)],
        out_shape=[jax.ShapeDtypeStruct((s, CB), BF16), jax.ShapeDtypeStruct((s, LANES), F32)],
        scratch_shapes=[pltpu.VMEM(((sub + 1) * QB, CB), BF16)] * 2,
        compiler_params=_cp(("parallel", "parallel")))(q_src[0], k_src[0], k_src[0], v_src[0], v_src[0])


def attn_bwd(q_src, k_src, v_src, da, lc, dc, lt, dt, g, dil):
    s = q_src[0].shape[0]
    nb = s // dil // QB
    sub = min(MAX_SUB, nb)
    main, prev, nxt = _attn_specs(nb, dil, sub)

    def body(q_ref, kp_ref, k_ref, vp_ref, v_ref, qn_ref, da_ref, dan_ref, lc_ref, dc_ref, l_ref, ln_ref,
             d_ref, dn_ref, dq_ref, dk_ref, dv_ref, kbuf, vbuf, qbuf, dabuf, lbuf, dbuf):
        step = pl.program_id(1)
        kbuf[0:QB], kbuf[QB:] = kp_ref[...], k_ref[...]
        vbuf[0:QB], vbuf[QB:] = vp_ref[...], v_ref[...]
        qbuf[0:sub * QB], qbuf[sub * QB:] = q_ref[...], qn_ref[...]
        dabuf[0:sub * QB], dabuf[sub * QB:] = da_ref[...], dan_ref[...]
        for c in range(sub):
            lbuf[c], dbuf[c] = l_ref[:, c * QB:(c + 1) * QB], d_ref[:, c * QB:(c + 1) * QB]
        lbuf[sub], dbuf[sub] = ln_ref[...], dn_ref[...]
        lo = _lane_lo()
        kj = lax.broadcasted_iota(jnp.int32, (QB, QB), 0)
        qi = lax.broadcasted_iota(jnp.int32, (QB, QB), 1)

        def block(j, carry):
            r0 = pl.multiple_of(j * QB, QB)
            rows, two = pl.ds(r0, QB), pl.ds(r0, 2 * QB)
            m_prev, m_cur = _masks(step * sub + j > 0)
            qmask = jnp.concatenate([m_prev, m_cur], axis=1)
            qmask = jnp.concatenate([qmask, qmask], axis=0)
            lcols, dcols = lc_ref[rows, :], dc_ref[rows, :]
            kmask = jnp.concatenate([kj <= qi, (kj >= qi) & (step * sub + j < nb - 1)], axis=1)
            kmask = jnp.concatenate([kmask, kmask], axis=1)
            lrow = jnp.concatenate([lbuf[j], lbuf[j + 1]], axis=1)
            drow = jnp.concatenate([dbuf[j], dbuf[j + 1]], axis=1)
            for i in range(NH // 2):
                sl = slice(2 * HD * i, 2 * HD * (i + 1))
                col_pair = lambda t: jnp.concatenate([t[:, 2 * i:2 * i + 1], t[:, 2 * i + 1:2 * i + 2]], axis=0)
                row_pair = lambda t: jnp.concatenate([t[2 * i:2 * i + 1, :], t[2 * i + 1:2 * i + 2, :]], axis=1)
                ks2, vv2 = kbuf[two, sl], vbuf[two, sl]
                sc = lax.dot_general(_stack_heads(q_ref[rows, sl], lo), ks2, NT, preferred_element_type=F32)
                p = jnp.exp(jnp.where(qmask, sc, NEG) - col_pair(lcols))
                dp = lax.dot_general(_stack_heads(da_ref[rows, sl], lo), vv2, NT, preferred_element_type=F32)
                ds = p * (dp - col_pair(dcols))
                dq = jnp.dot(ds.astype(BF16), ks2, preferred_element_type=F32)
                dq_ref[rows, sl] = (jnp.where(lo, dq[:QB], dq[QB:]) * (HD ** -0.5)).astype(BF16)

                q2, da2 = _stack_heads(qbuf[two, sl], lo), _stack_heads(dabuf[two, sl], lo)
                ks, vv = k_ref[rows, sl], v_ref[rows, sl]
                sct = lax.dot_general(ks, q2, NT, preferred_element_type=F32)
                pt = jnp.exp(jnp.where(kmask, sct, NEG) - row_pair(lrow))
                dpt = lax.dot_general(vv, da2, NT, preferred_element_type=F32)
                dst = pt * (dpt - row_pair(drow))
                dv_ref[rows, sl] = jnp.dot(pt.astype(BF16), da2, preferred_element_type=F32).astype(BF16)
                dk_ref[rows, sl] = jnp.dot(dst.astype(BF16), q2, preferred_element_type=F32).astype(BF16)
            return carry

        lax.fori_loop(0, sub, block, 0, unroll=True)

    t_main = pl.BlockSpec((NH, sub * QB), lambda r, st: (r, st))
    t_nxt = pl.BlockSpec((NH, QB), lambda r, st: (r, jnp.minimum(sub * (st + 1), nb - 1)))
    out = jax.ShapeDtypeStruct((s, CB), BF16)
    big = pltpu.VMEM(((sub + 1) * QB, CB), BF16)
    return pl.pallas_call(
        body, name=f"attn_bwd_g{g}", grid=(dil, nb // sub),
        in_specs=[main(q_src[1]), prev(k_src[1]), main(k_src[1]), prev(v_src[1]), main(v_src[1]), nxt(q_src[1]),
                  main(0), nxt(0), main(0, LANES), main(0, LANES), t_main, t_nxt, t_main, t_nxt],
        out_specs=[main(0)] * 3, out_shape=[out] * 3,
        scratch_shapes=[big] * 4 + [pltpu.VMEM((sub + 1, NH, QB), F32)] * 2,
        compiler_params=_cp(("parallel", "parallel")))(
            q_src[0], k_src[0], k_src[0], v_src[0], v_src[0], q_src[0], da, da, lc, dc, lt, lt, dt, dt)


def _conv_taps(u, u_prev, first):
    tm = u.shape[0]
    row = lax.broadcasted_iota(jnp.int32, (tm, 1), 0)
    up = jnp.where(first, 0.0, u_prev)
    u1 = jnp.where(row == 0, up[HALO - 1:HALO, :], pltpu.roll(u, 1, 0))
    u2 = jnp.where(row == 0, up[HALO - 2:HALO - 1, :],
                   jnp.where(row == 1, up[HALO - 1:HALO, :], pltpu.roll(u, 2, 0)))
    return u1, u2


def mid_fwd(proj, o_g, lse_g, conv_w, expand, tm):
    s = proj.shape[0]
    hb = tm // HALO

    def body(ba_ref, ca_ref, xa_ref, za_ref, cah_ref, xah_ref, zb_ref,
             o0, o1, o2, l0, l1, l2, w_ref, exp_ref, ya_ref, yb_ref, at_ref, lc_ref, buf_o, buf_l):
        first = pl.program_id(0) == 0
        u = ca_ref[...].astype(F32) * xa_ref[...].astype(F32)
        u1, u2 = _conv_taps(u, cah_ref[...].astype(F32) * xah_ref[...].astype(F32), first)
        conv = w_ref[0:1, :] * u2 + w_ref[1:2, :] * u1 + w_ref[2:3, :] * u
        ya_ref[...] = (ba_ref[...].astype(F32) * conv * _silu(za_ref[...].astype(F32))).astype(BF16)
        ls = [_from_residue_major(l, buf_l.at[g], d) for g, (l, d) in enumerate(zip((l0, l1, l2), DILATIONS))]
        mx = jnp.maximum(jnp.maximum(ls[0], ls[1]), ls[2])
        es = [jnp.exp(l - mx) for l in ls]
        den = es[0] + es[1] + es[2]
        attn = jnp.zeros((tm, CB), F32)
        for e, o, d in zip(es, (o0, o1, o2), DILATIONS):
            attn = attn + _dot_hilo(e / den, exp_ref) * _from_residue_major(o, buf_o, d)
        at_ref[...] = attn
        lc_ref[...] = mx + jnp.log(den)
        yb_ref[...] = (attn * _silu(zb_ref[...].astype(F32))).astype(BF16)

    col = lambda j: pl.BlockSpec((tm, D), lambda i: (i, j))
    halo = lambda j: pl.BlockSpec((HALO, D), lambda i: (jnp.maximum(i * hb - 1, 0), j))
    loc = lambda w: pl.BlockSpec((tm, w), lambda i: (i, 0))
    rm = lambda w: [pl.BlockSpec((d, tm // d, w), lambda i: (0, i, 0)) for d in DILATIONS]
    rm_view = lambda ts, w: [t.reshape(d, s // d, w) for t, d in zip(ts, DILATIONS)]
    return pl.pallas_call(
        body, name="mid_fwd", grid=(s // tm,),
        in_specs=[col(0), col(1), col(2), col(3), halo(1), halo(2),
                  pl.BlockSpec((tm, CB), lambda i: (i, CB_ZB))] + rm(CB) + rm(LANES)
                 + [pl.BlockSpec((3, D), lambda i: (0, 0)), pl.BlockSpec(expand.shape, lambda i: (0, 0))],
        out_specs=[loc(D), loc(CB), loc(CB), loc(LANES)],
        out_shape=[jax.ShapeDtypeStruct((s, D), BF16), jax.ShapeDtypeStruct((s, CB), BF16),
                   jax.ShapeDtypeStruct((s, CB), F32), jax.ShapeDtypeStruct((s, LANES), F32)],
        scratch_shapes=[pltpu.VMEM((CB // LANES, tm, LANES), F32), pltpu.VMEM((3, 1, tm, LANES), F32)],
        compiler_params=_cp(("parallel",)))(
            proj, proj, proj, proj, proj, proj, proj, *rm_view(o_g, CB), *rm_view(lse_g, LANES), conv_w, expand)


def tail(proj, ya, yb, attn, x, target, gate, pa_w, pb_w, wo_w, total, conv_w, tm):
    s = proj.shape[0]
    ni = s // tm
    hb = tm // HALO
    nlate = NIN - CB_ZB * CB
    nearly = 4 * D

    def body(ya_ref, yb_ref, ga_ref, gb_ref, zb_ref, at_ref, x_ref, t_ref, gate_ref, pa_ref, pb_ref, wo_ref,
             tot_ref, ba_ref, ca_ref, xa_ref, za_ref, cah_ref, xah_ref, cw_ref,
             dp_hbm, dy_ref, da_ref, dc_ref, mg_ref, do_ref, dpa_ref, dpb_ref, st_ref, gwc_ref,
             stage, dconv_next, sems):
        step = pl.program_id(0)
        i = ni - 1 - step
        slot = step % 2

        def slabs(at_step, sl):
            rows = pl.ds(pl.multiple_of((ni - 1 - at_step) * tm, tm), tm)
            return (pltpu.make_async_copy(stage.at[sl, :, 0:nearly], dp_hbm.at[rows, pl.ds(0, nearly)],
                                          sems.at[sl, 0]),
                    pltpu.make_async_copy(stage.at[sl, :, nearly:], dp_hbm.at[rows, pl.ds(CB_ZB * CB, nlate)],
                                          sems.at[sl, 1]))

        @pl.when(step == 0)
        def _():
            st_ref[...] = jnp.zeros_like(st_ref)
            gwc_ref[...] = jnp.zeros_like(gwc_ref)
            dconv_next[...] = jnp.zeros_like(dconv_next)

        @pl.when(step >= 2)
        def _():
            for cp in slabs(step - 2, slot):
                cp.wait()

        gate_v = gate_ref[...]
        pa = jnp.dot(ya_ref[...], pa_ref[...], preferred_element_type=F32)
        pb = jnp.dot(yb_ref[...], pb_ref[...], preferred_element_type=F32)
        sa = jax.nn.sigmoid(ga_ref[...].astype(F32))
        sb = jax.nn.sigmoid(gb_ref[...].astype(F32))
        merged = (sa * pa + sb * pb).astype(BF16)
        mg_ref[...] = merged
        out = jnp.dot(merged, wo_ref[...], preferred_element_type=F32)
        err = x_ref[...] + gate_v * out - t_ref[...]
        dy = err * (1.0 / D)
        dy_ref[...] = dy
        st_ref[0:1, :] += jnp.sum(dy * out, axis=0, keepdims=True)
        st_ref[1:2, :] += jnp.sum(err * err, axis=0, keepdims=True)
        dout = (gate_v * dy).astype(BF16)
        do_ref[...] = dout
        dmg = lax.dot_general(dout, wo_ref[...], NT, preferred_element_type=F32)
        dpa = (dmg * sa).astype(BF16)
        dpb = (dmg * sb).astype(BF16)
        dpa_ref[...] = dpa
        dpb_ref[...] = dpb
        late = nearly
        stage[slot, :, late + CB:late + CB + D] = (dmg * pa * sa * (1.0 - sa)).astype(BF16)
        stage[slot, :, late + CB + D:] = (dmg * pb * sb * (1.0 - sb)).astype(BF16)
        dya = lax.dot_general(dpa, pa_ref[...], NT, preferred_element_type=F32)
        dyb = lax.dot_general(dpb, pb_ref[...], NT, preferred_element_type=F32)
        zb = zb_ref[...].astype(F32)
        sg = jax.nn.sigmoid(zb)
        attn_v = at_ref[...]
        dattn = dyb * (zb * sg)
        da_ref[...] = dattn.astype(BF16)
        stage[slot, :, late:late + CB] = (dyb * attn_v * (sg * (1.0 + zb * (1.0 - sg)))).astype(BF16)
        dc_ref[...] = _dot_hilo(dattn * attn_v, tot_ref)

        ba, ca, xa, za = (t[...].astype(F32) for t in (ba_ref, ca_ref, xa_ref, za_ref))
        u = ca * xa
        u1, u2 = _conv_taps(u, cah_ref[...].astype(F32) * xah_ref[...].astype(F32), i == 0)
        w0, w1, w2 = cw_ref[0:1, :], cw_ref[1:2, :], cw_ref[2:3, :]
        conv = w0 * u2 + w1 * u1 + w2 * u
        sga = jax.nn.sigmoid(za)
        sza = za * sga
        dconv = dya * ba * sza
        dcn = dconv_next[...]
        rowi = lax.broadcasted_iota(jnp.int32, (tm, 1), 0)
        d1 = jnp.where(rowi == tm - 1, dcn[0:1, :], pltpu.roll(dconv, tm - 1, 0))
        d2 = jnp.where(rowi == tm - 2, dcn[0:1, :],
                       jnp.where(rowi == tm - 1, dcn[1:2, :], pltpu.roll(dconv, tm - 2, 0)))
        du = w2 * dconv + w1 * d1 + w0 * d2
        stage[slot, :, 0:D] = (dya * conv * sza).astype(BF16)
        stage[slot, :, D:2 * D] = (du * xa).astype(BF16)
        stage[slot, :, 2 * D:3 * D] = (du * ca).astype(BF16)
        stage[slot, :, 3 * D:4 * D] = (dya * ba * conv * (sga * (1.0 + za * (1.0 - sga)))).astype(BF16)
        gwc_ref[0:1, :] += jnp.sum(dconv * u2, axis=0, keepdims=True)
        gwc_ref[1:2, :] += jnp.sum(dconv * u1, axis=0, keepdims=True)
        gwc_ref[2:3, :] += jnp.sum(dconv * u, axis=0, keepdims=True)
        dconv_next[...] = dconv[0:8, :]

        for cp in slabs(step, slot):
            cp.start()

        @pl.when(step == ni - 1)
        def _():
            for cp in slabs(step - 1, 1 - slot) + slabs(step, slot):
                cp.wait()

    rev = lambda st: ni - 1 - st
    row = lambda w: pl.BlockSpec((tm, w), lambda st: (rev(st), 0))
    pcol = lambda w, jb: pl.BlockSpec((tm, w), lambda st: (rev(st), jb))
    halo = lambda jb: pl.BlockSpec((HALO, D), lambda st: (jnp.maximum(rev(st) * hb - 1, 0), jb))
    const = lambda a: pl.BlockSpec(a.shape, lambda st: (0, 0), pipeline_mode=pl.Buffered(1))
    acc = pl.BlockSpec((8, D), lambda st: (0, 0))
    return pl.pallas_call(
        body, name="tail", grid=(ni,),
        in_specs=[row(D), row(CB), pcol(D, 9), pcol(D, 10), pcol(CB, CB_ZB), row(CB), row(D), row(D),
                  pl.BlockSpec((1, D), lambda st: (0, 0)), const(pa_w), const(pb_w), const(wo_w), const(total),
                  pcol(D, 0), pcol(D, 1), pcol(D, 2), pcol(D, 3), halo(1), halo(2),
                  pl.BlockSpec((3, D), lambda st: (0, 0))],
        out_specs=[pl.BlockSpec(memory_space=pl.ANY),
                   row(D), row(CB), row(LANES), row(D), row(D), row(D), row(D), acc, acc],
        out_shape=[jax.ShapeDtypeStruct((s, NIN), BF16), jax.ShapeDtypeStruct((s, D), F32),
                   jax.ShapeDtypeStruct((s, CB), BF16), jax.ShapeDtypeStruct((s, LANES), F32)]
                  + [jax.ShapeDtypeStruct((s, D), BF16)] * 4 + [jax.ShapeDtypeStruct((8, D), F32)] * 2,
        scratch_shapes=[pltpu.VMEM((2, tm, nearly + nlate), BF16), pltpu.VMEM((8, D), F32),
                        pltpu.SemaphoreType.DMA((2, 2))],
        compiler_params=_cp(("arbitrary",), 60))(
            ya, yb, proj, proj, proj, attn, x, target, gate, pa_w, pb_w, wo_w, total,
            proj, proj, proj, proj, proj, proj, conv_w)


def _local_step(x, target, shift, scale, gate, norm_w, conv_w, qw, kw, w_shard, small_shards, me_xyc):
    qw8, kw8 = jnp.tile(qw, (1, NH)), jnp.tile(kw, (1, NH))
    same, total, expand = _head_matrices()
    proj, ht, wg, (pa_g, pb_g, wo_g) = proj_fwd_gather(
        x, norm_w, scale, shift, w_shard, small_shards, gather_order(me_xyc), 1024)
    pa_w, wo_w = pa_g.reshape(D, D), wo_g.reshape(D, D)
    pb_w = pb_g.transpose(1, 0, 2).reshape(CB, D)
    srcs = qkv_prep(proj, qw8, kw8, same, 512)
    o_g, lse_g = zip(*[attn_fwd(*srcs[g], g, d) for g, d in enumerate(DILATIONS)])
    ya, yb, attn, lc = mid_fwd(proj, o_g, lse_g, conv_w, expand, 512)
    dproj, dy, da, dc, merged, dout, dpa, dpb, st_tail, st_conv = tail(
        proj, ya, yb, attn, x, target, gate, pa_w, pb_w, wo_w, total, conv_w, 256)
    g_wo, g_pa, g_pb = matmuls_tn([(merged, dout), (ya, dpa), (yb, dpb)], "grad_small_weights", 1024)
    grads = []
    for g, (d, (da_p, lc_p, dc_p, lt, dt)) in enumerate(zip(DILATIONS, stats_prep(da, lc, dc, 2048))):
        grads.append(attn_bwd(*srcs[g], da_p, lc_p, dc_p, lt, dt, g, d))
    dproj, gw_qk = qkv_grads_to_dproj(dproj, proj, grads, qw8, kw8, same, 512)
    slabs = [g_pa.reshape(NDEV, 128, D), g_pb.reshape(CB, NDEV, 128).transpose(1, 0, 2), g_wo.reshape(NDEV, 128, D)]
    grad_x, st_norm, r_win, (r_pa, r_pb, r_wo) = proj_bwd(
        ht, dproj, wg, slabs, scatter_order(me_xyc), x, dy, norm_w, scale, 1024)
    dmod = jnp.concatenate([st_norm[0:1], st_norm[1:2], st_tail[0:1]], axis=1)
    loss_part = (0.5 / D) * jnp.sum(st_tail[1])
    gw_heads = gw_qk[0:2].reshape(2, NH, HD).sum(axis=1)
    small = dict(dmod=dmod, norm_w=st_norm[2:3], conv_w=st_conv[0:3],
                 q_norm_w=gw_heads[0:1], k_norm_w=gw_heads[1:2], loss=loss_part)
    return grad_x, small, (r_win, r_pa, r_pb, r_wo)


def kernel(x, c, w_ada, b_ada, norm_w, w_in, conv_w, q_norm_w, k_norm_w, w_br_conv, w_br_attn, w_out, loss_target, m_w_ada, m_b_ada, m_norm_w, m_w_in, m_conv_w, m_q_norm_w, m_k_norm_w, m_w_br_conv, m_w_br_attn, m_w_out, v_w_ada, v_b_ada, v_norm_w, v_w_in, v_conv_w, v_q_norm_w, v_k_norm_w, v_w_br_conv, v_w_br_attn, v_w_out):
    me_xyc = (lax.axis_index("x"), lax.axis_index("y"), lax.axis_index("c"))
    me = _dev_index(me_xyc)
    ncol = w_ada.shape[2]

    conv_pad = jnp.zeros((8, 128), F32).at[0:3].set(conv_w[0])
    b_cols = lax.dynamic_slice(b_ada, (0, me * ncol), (1, ncol))
    mod_pieces, c_all, conv_all = ada_fwd(c, conv_pad, w_ada[0], b_cols)
    conv_full = conv_all[:, 0:3].transpose(1, 0, 2).reshape(3, D)
    c_all = c_all.reshape(NDEV, D)
    mod = mod_pieces.reshape(1, 3 * D)
    shift, scale, gate = mod[:, 0:D], mod[:, D:2 * D], mod[:, 2 * D:3 * D]

    grad_x, small, (r_win, r_pa, r_pb, r_wo) = _local_step(
        x[0], loss_target[0], shift, scale, gate, norm_w, conv_full, q_norm_w, k_norm_w,
        w_in[0].astype(BF16), [w_br_conv[0].astype(BF16), w_br_attn[0].astype(BF16), w_out[0].astype(BF16)], me_xyc)

    packed = jnp.concatenate(
        [small["dmod"], small["norm_w"], small["conv_w"].reshape(1, 3 * D), small["q_norm_w"], small["k_norm_w"],
         jnp.full((1, 128), small["loss"], F32)], axis=1)
    packed_all, tot = gather_sum(packed)
    loss = tot[0, 7 * D + 2 * HD]
    dmod_all = packed_all[:, 0, 0:3 * D]
    g_b_ada = tot[:, 0:3 * D]
    g_norm_w = tot[:, 3 * D:4 * D]
    g_conv = lax.dynamic_slice(tot[:, 4 * D:7 * D].reshape(3, D), (0, me * 128), (3, 128))
    g_qn = tot[:, 7 * D:7 * D + HD]
    g_kn = tot[:, 7 * D + HD:7 * D + 2 * HD]
    g_w_ada = ada_bwd(c_all.T, lax.dynamic_slice(dmod_all, (0, me * ncol), (NDEV, ncol)))

    def upd(parts, w, m, v, name, rows):
        shape = w.shape
        w2, m2, v2 = (t.reshape(shape[-2:]) for t in (w, m, v))
        return [t.reshape(shape) for t in adamw(parts, w2, m2, v2, name, rows)]

    res = {"w_in": upd(r_win, w_in, m_w_in, v_w_in, "adamw_w_in", 128)}
    small_params = {"w_ada": (g_w_ada[None], w_ada, m_w_ada, v_w_ada), "b_ada": (g_b_ada[None], b_ada, m_b_ada, v_b_ada),
                    "norm_w": (g_norm_w[None], norm_w, m_norm_w, v_norm_w),
                    "conv_w": (g_conv[None], conv_w, m_conv_w, v_conv_w),
                    "q_norm_w": (g_qn[None], q_norm_w, m_q_norm_w, v_q_norm_w),
                    "k_norm_w": (g_kn[None], k_norm_w, m_k_norm_w, v_k_norm_w),
                    "w_br_conv": (r_pa, w_br_conv, m_w_br_conv, v_w_br_conv),
                    "w_br_attn": (r_pb, w_br_attn, m_w_br_attn, v_w_br_attn),
                    "w_out": (r_wo, w_out, m_w_out, v_w_out)}
    updated = adamw_small([(item[0],) + tuple(t.reshape(t.shape[-2:]) for t in item[1:])
                           for item in small_params.values()])
    for (pname, item), outs4 in zip(small_params.items(), updated):
        res[pname] = [t.reshape(item[1].shape) for t in outs4]
    names = ["w_ada", "b_ada", "norm_w", "w_in", "conv_w", "q_norm_w", "k_norm_w", "w_br_conv", "w_br_attn", "w_out"]
    return (loss, grad_x[None], *[res[n][0] for n in names], *[res[n][1] for n in names],
            *[res[n][2] for n in names], *[res[n][3] for n in names])
```

```python
import jax
import jax.numpy as jnp
from jax import lax
from jax.experimental import pallas as pl
from jax.experimental.pallas import tpu as pltpu

F32, BF16 = jnp.float32, jnp.bfloat16
D = 1024
NIN = 11264
NDEV = 8
SHARD = NIN // NDEV
HD = 64
NH = 8
QB = 128
CB = 512
CB_Q, CB_K, CB_V, CB_ZB = 8, 11, 14, 17
DILATIONS = (1, 4, 16)
EPS = 1e-6
NEG = -1e30
HALO = 16
LANES = 128
MESH = pl.DeviceIdType.MESH

ADAM_LR, ADAM_B1, ADAM_B2, ADAM_EPS, ADAM_WD, ADAM_STEP = 0.001, 0.9, 0.999, 1e-08, 0.01, 10

NT = (((1,), (1,)), ((), ()))
TN = (((0,), (0,)), ((), ()))


def _cp(sem, vmem_mb=48):
    return pltpu.CompilerParams(dimension_semantics=sem, vmem_limit_bytes=vmem_mb << 20)


def _silu(z):
    return z * jax.nn.sigmoid(z)


def _coords():
    return lax.axis_index("x"), lax.axis_index("y"), lax.axis_index("c")


FLIPS = [(fx, fy, fc) for fx in (0, 1) for fy in (0, 1) for fc in (0, 1)][1:]


def gather_sum(vec):
    def body(v_ref, all_ref, sum_ref, send_sems, recv_sems, local_sem):
        me_xyc = _coords()
        me = _dev_index(me_xyc)
        peers = [_flip(me_xyc, f) for f in FLIPS]

        def copy(k, block):
            return pltpu.make_async_remote_copy(
                src_ref=v_ref, dst_ref=all_ref.at[block], send_sem=send_sems.at[k], recv_sem=recv_sems.at[k],
                device_id=peers[k], device_id_type=MESH)

        mine = pltpu.make_async_copy(v_ref, all_ref.at[me], local_sem)
        sends = [copy(k, me) for k in range(7)]
        for cp in [mine] + sends:
            cp.start()
        for k in range(7):
            copy(k, _dev_index(peers[k])).wait_recv()
        mine.wait()
        acc = all_ref[0]
        for b in range(1, NDEV):
            acc = acc + all_ref[b]
        sum_ref[...] = acc
        for cp in sends:
            cp.wait_send()

    return pl.pallas_call(
        body, name="gather_sum",
        out_shape=[jax.ShapeDtypeStruct((NDEV,) + vec.shape, F32), jax.ShapeDtypeStruct(vec.shape, F32)],
        scratch_shapes=[pltpu.SemaphoreType.DMA((7,)), pltpu.SemaphoreType.DMA((7,)), pltpu.SemaphoreType.DMA],
    )(vec)


def _flip(dev, f):
    return tuple(1 - v if b else v for v, b in zip(dev, f))


def _dev_index(dev):
    return 4 * dev[0] + 2 * dev[1] + dev[2]


def _chip_order(x, y, c):
    xor = lambda a, b: a + b - 2 * a * b
    return [(xor(x, 1 - c), xor(y, c)), (xor(x, c), xor(y, 1 - c)), (1 - x, 1 - y)]


def gather_order(me_xyc):
    x, y, c = me_xyc
    chips = _chip_order(x, y, c)
    devs = [(x, y, c), (x, y, 1 - c), (*chips[0], c), (*chips[1], c),
            (*chips[1], 1 - c), (*chips[0], 1 - c), (*chips[2], c), (*chips[2], 1 - c)]
    return jnp.stack([_dev_index(d) for d in devs]).astype(jnp.int32)


def scatter_order(me_xyc):
    devs = [_flip(me_xyc, f) for f in FLIPS] + [me_xyc]
    return jnp.stack([_dev_index(d) for d in devs]).astype(jnp.int32)


def ada_fwd(c, conv_pad, w_ada, b_cols):
    ncol = w_ada.shape[1]

    def body(c_ref, cv_ref, w_ref, b_ref, mod_ref, call_ref, cvall_ref, rows_buf, send_sems, recv_sems, local_sems):
        me_xyc = _coords()
        me = _dev_index(me_xyc)
        peers = [_flip(me_xyc, f) for f in FLIPS]
        pids = [_dev_index(p) for p in peers]

        def copy(a, k, src, dst):
            return pltpu.make_async_remote_copy(src_ref=src, dst_ref=dst, send_sem=send_sems.at[a, k],
                                                recv_sem=recv_sems.at[a, k], device_id=peers[k], device_id_type=MESH)

        own = [pltpu.make_async_copy(c_ref, call_ref.at[me], local_sems.at[0]),
               pltpu.make_async_copy(cv_ref, cvall_ref.at[me], local_sems.at[1])]
        first = [copy(0, k, c_ref, call_ref.at[me]) for k in range(7)]
        first += [copy(1, k, cv_ref, cvall_ref.at[me]) for k in range(7)]
        for cp in own + first:
            cp.start()
        own[0].wait()
        for k in range(7):
            copy(0, k, c_ref, call_ref.at[pids[k]]).wait_recv()
        seq = lax.broadcasted_iota(jnp.int32, (NDEV, 1), 0)
        c_all = jnp.zeros((NDEV, D), F32)
        for p in range(NDEV):
            c_all = jnp.where(seq == p, call_ref[p], c_all)
        mods = jnp.dot(_silu(c_all).astype(BF16), w_ref[...].astype(BF16), preferred_element_type=F32) + b_ref[...]
        for p in range(NDEV):
            rows_buf[p] = mods[p:p + 1, :]
        mine = pltpu.make_async_copy(rows_buf.at[me], mod_ref.at[me], local_sems.at[2])
        second = [copy(2, k, rows_buf.at[pids[k]], mod_ref.at[me]) for k in range(7)]
        for cp in [mine] + second:
            cp.start()
        for k in range(7):
            copy(2, k, rows_buf.at[pids[k]], mod_ref.at[pids[k]]).wait_recv()
            copy(1, k, cv_ref, cvall_ref.at[pids[k]]).wait_recv()
        for cp in first + second:
            cp.wait_send()
        own[1].wait()
        mine.wait()

    return pl.pallas_call(
        body, name="ada_fwd",
        out_shape=[jax.ShapeDtypeStruct((NDEV, 1, ncol), F32), jax.ShapeDtypeStruct((NDEV, 1, D), F32),
                   jax.ShapeDtypeStruct((NDEV,) + conv_pad.shape, F32)],
        scratch_shapes=[pltpu.VMEM((NDEV, 1, ncol), F32), pltpu.SemaphoreType.DMA((3, 7)),
                        pltpu.SemaphoreType.DMA((3, 7)), pltpu.SemaphoreType.DMA((3,))],
    )(c, conv_pad, w_ada, b_cols)


def ada_bwd(c_all_t, dmod_cols):
    def body(c_ref, d_ref, o_ref):
        at = _silu(c_ref[...])
        acc = at[:, 0:1] * d_ref[0:1, :]
        for b in range(1, NDEV):
            acc = acc + at[:, b:b + 1] * d_ref[b:b + 1, :]
        o_ref[...] = acc

    return pl.pallas_call(body, name="ada_bwd",
                          out_shape=jax.ShapeDtypeStruct((D, dmod_cols.shape[1]), F32))(c_all_t, dmod_cols)


def _adamw_update(g, w_ref, m_ref, v_ref, g_ref, d_ref, nm_ref, nv_ref):
    nm = ADAM_B1 * m_ref[...] + (1.0 - ADAM_B1) * g
    nv = ADAM_B2 * v_ref[...] + (1.0 - ADAM_B2) * (g * g)
    g_ref[...] = g
    nm_ref[...] = nm
    nv_ref[...] = nv
    m_hat = nm / (1.0 - ADAM_B1 ** ADAM_STEP)
    v_hat = nv / (1.0 - ADAM_B2 ** ADAM_STEP)
    d_ref[...] = -ADAM_LR * (m_hat / (jnp.sqrt(v_hat) + ADAM_EPS) + ADAM_WD * w_ref[...])


def adamw_small(items):
    n = len(items)

    def body(*refs):
        ins, outs = refs[:4 * n], refs[4 * n:]
        for a in range(n):
            p_ref, w_ref, m_ref, v_ref = ins[4 * a:4 * a + 4]
            g = p_ref[0].astype(F32)
            for b in range(1, p_ref.shape[0]):
                g = g + p_ref[b].astype(F32)
            _adamw_update(g, w_ref, m_ref, v_ref, *outs[4 * a:4 * a + 4])

    out = pl.pallas_call(
        body, name="adamw_small",
        out_shape=[jax.ShapeDtypeStruct(it[1].shape, F32) for it in items for _ in range(4)],
        compiler_params=pltpu.CompilerParams(vmem_limit_bytes=48 << 20))(*[t for it in items for t in it])
    return [out[4 * a:4 * a + 4] for a in range(n)]


def adamw(parts, w, m, v, name, rows):
    n, r, ccols = parts.shape

    def body(p_ref, w_ref, m_ref, v_ref, g_ref, d_ref, nm_ref, nv_ref):
        g = p_ref[0].astype(F32)
        for b in range(1, n):
            g = g + p_ref[b].astype(F32)
        _adamw_update(g, w_ref, m_ref, v_ref, g_ref, d_ref, nm_ref, nv_ref)

    blk = pl.BlockSpec((rows, ccols), lambda i: (i, 0))
    out = jax.ShapeDtypeStruct((r, ccols), F32)
    return pl.pallas_call(
        body, name=name, grid=(r // rows,),
        in_specs=[pl.BlockSpec((n, rows, ccols), lambda i: (0, i, 0)), blk, blk, blk],
        out_specs=[blk] * 4, out_shape=[out] * 4, compiler_params=_cp(("parallel",)))(parts, w, m, v)


def proj_fwd_gather(x, nw, scale, shift, w_shard, extras, order, tm):
    s = x.shape[0]
    ni = s // tm
    n = 1 + len(extras)
    mid = ni - 2

    def body(order_ref, x_ref, nw_ref, sc_ref, sh_ref, *refs):
        ins, o_ref, ht_ref, outs = refs[:n], refs[n], refs[n + 1], refs[n + 2:2 * n + 2]
        h_all, wbuf, send_sems, recv_sems, local_sems, load_sems = refs[2 * n + 2:]
        jj, i = pl.program_id(0), pl.program_id(1)
        x, y, c = _coords()
        me, sibling = (x, y, c), (x, y, 1 - c)
        chips = _chip_order(x, y, c)
        relayed = [(*chips[1], 1 - c), (*chips[0], 1 - c), (*chips[2], 1 - c)]

        def slot(a, dev):
            return outs[a].at[_dev_index(dev)]

        def copy(a, k, block, to, src=None):
            return pltpu.make_async_remote_copy(
                src_ref=slot(a, block) if src is None else src, dst_ref=slot(a, block),
                send_sem=send_sems.at[a, k], recv_sem=recv_sems.at[a, k], device_id=to, device_id_type=MESH)

        mine = [pltpu.make_async_copy(ins[a], slot(a, me), local_sems.at[a]) for a in range(n)]
        to_sibling = [copy(a, 0, me, sibling, src=ins[a]) for a in range(n)]
        to_chip = [[copy(a, 1 + j, me, (*chips[j], c), src=ins[a]) for a in range(n)] for j in range(2)]
        onward = [copy(a, 3, (*chips[1], c), (*chips[0], c)) for a in range(n)]
        passed = [[copy(a, 4 + j, (*ch, c), sibling) for a in range(n)] for j, ch in enumerate(chips)]
        sends = lambda a: [to_sibling[a], to_chip[0][a], to_chip[1][a], onward[a]] + [passed[j][a] for j in range(3)]

        def arrived(a, j):
            copy(a, 1 + j, (*chips[j], c), me).wait_recv()

        def load(row):
            return pltpu.make_async_copy(outs[0].at[order_ref[row]], wbuf.at[row % 2], load_sems.at[row % 2])

        @pl.when((jj == 0) & (i == 0))
        def _():
            for cp in mine:
                cp.start()
            to_sibling[0].start()
            to_chip[0][0].start()
            pltpu.make_async_copy(ins[0], wbuf.at[0], load_sems.at[0]).start()

        @pl.when((jj == 1) & (i == 0))
        def _():
            to_chip[1][0].start()

        @pl.when((jj == 4) & (i == 0))
        def _():
            for a in range(1, n):
                to_sibling[a].start()
                to_chip[0][a].start()
                to_chip[1][a].start()

        direct = {2: 0, 3: 1, 6: 2}
        relay = {4: 0, 5: 1, 7: 2}

        @pl.when((jj == 0) & (i == mid))
        def _():
            copy(0, 0, sibling, me).wait_recv()

        for row, j in direct.items():
            @pl.when((jj == row - 1) & (i == mid))
            def _(j=j):
                arrived(0, j)
                passed[j][0].start()
                if j == 1:
                    onward[0].start()

        for row, j in relay.items():
            @pl.when((jj == row - 1) & (i == mid))
            def _(j=j):
                copy(0, 4 + j, relayed[j], me).wait_recv()

        @pl.when((jj == NDEV - 1) & (i == 0))
        def _():
            for a in range(1, n):
                arrived(a, 1)
                onward[a].start()
                passed[1][a].start()
                arrived(a, 0)
                passed[0][a].start()

        @pl.when((jj < NDEV - 1) & (i == mid))
        def _():
            load(jj + 1).start()

        @pl.when(i == 0)
        def _():
            load(jj).wait()

        @pl.when(jj == 0)
        def _():
            xf = x_ref[...]
            r = lax.rsqrt(jnp.mean(xf * xf, axis=-1, keepdims=True) + EPS)
            h = (xf * r * nw_ref[...]) * (1.0 + sc_ref[...]) + sh_ref[...]
            h_all[i] = h.astype(BF16)
            ht_ref[...] = h.T.astype(BF16)

        o_ref[...] = jnp.dot(h_all[i], wbuf[jj % 2], preferred_element_type=F32).astype(BF16)

        @pl.when((jj == NDEV - 1) & (i == ni - 1))
        def _():
            for a in range(1, n):
                arrived(a, 2)
                passed[2][a].start()
            for a in range(1, n):
                copy(a, 0, sibling, me).wait_recv()
                for j in range(3):
                    copy(a, 4 + j, relayed[j], me).wait_recv()
            for a in range(n):
                mine[a].wait()
                for cp in sends(a):
                    cp.wait_send()

    any_spec = pl.BlockSpec(memory_space=pl.ANY)
    vec = pl.BlockSpec((1, D), lambda jj, i, o: (0, 0))
    outs = pl.pallas_call(
        body, name="proj_fwd_gather",
        grid_spec=pltpu.PrefetchScalarGridSpec(
            num_scalar_prefetch=1, grid=(NDEV, ni),
            in_specs=[pl.BlockSpec((tm, D), lambda jj, i, o: (jnp.where(jj == 0, i, ni - 1), 0))] + [vec] * 3
                     + [any_spec] * n,
            out_specs=[pl.BlockSpec((tm, SHARD), lambda jj, i, o: (i, o[jj])),
                       pl.BlockSpec((D, tm), lambda jj, i, o: (0, jnp.where(jj == 0, i, ni - 1)))]
                      + [any_spec] * n,
            scratch_shapes=[pltpu.VMEM((ni, tm, D), BF16), pltpu.VMEM((2, D, SHARD), BF16),
                            pltpu.SemaphoreType.DMA((n, 7)), pltpu.SemaphoreType.DMA((n, 7)),
                            pltpu.SemaphoreType.DMA((n,)), pltpu.SemaphoreType.DMA((2,))]),
        out_shape=[jax.ShapeDtypeStruct((s, NIN), BF16), jax.ShapeDtypeStruct((D, s), BF16),
                   jax.ShapeDtypeStruct((NDEV, D, SHARD), BF16)]
                  + [jax.ShapeDtypeStruct((NDEV,) + e.shape, e.dtype) for e in extras],
        compiler_params=_cp(("arbitrary", "arbitrary"), 56))(order, x, nw, scale, shift, w_shard, *extras)
    return outs[0], outs[1], outs[2], outs[3:]


def proj_bwd(ht, dproj, wg, smalls, order, x, dy, nw, scale, tt):
    s = dproj.shape[0]
    nk = s // tt
    n = len(smalls)
    rows_per_step = tt // nk
    last = 2 * NDEV

    def body(order_ref, ht_ref, dp_ref, w_ref, x_ref, dy_ref, nw_ref, sc_ref, *rest):
        small_in = rest[:n]
        gx_ref, st_ref, gw_ref, rwin_ref = rest[n:n + 4]
        small_out = rest[n + 4:2 * n + 4]
        acc, stage, dh, send_sems, recv_sems, local_sems, stage_sems = rest[2 * n + 4:]
        t, k = pl.program_id(0), pl.program_id(1)
        me_xyc = _coords()
        me = _dev_index(me_xyc)
        peers = [_flip(me_xyc, f) for f in FLIPS]

        def exchange(a, kf, src_arr, dst_arr):
            pid = _dev_index(peers[kf])
            mk = lambda dst: pltpu.make_async_remote_copy(
                src_ref=src_arr.at[pid], dst_ref=dst, send_sem=send_sems.at[a, kf], recv_sem=recv_sems.at[a, kf],
                device_id=peers[kf], device_id_type=MESH)
            return mk(dst_arr.at[me]), mk(dst_arr.at[pid])

        small_pairs = [exchange(1 + a, kf, small_in[a], small_out[a]) for kf in range(7) for a in range(n)]
        small_own = [pltpu.make_async_copy(small_in[a].at[me], small_out[a].at[me], local_sems.at[1 + a])
                     for a in range(n)]
        win_pairs = [exchange(0, kf, gw_ref, rwin_ref) for kf in range(7)]
        win_own = pltpu.make_async_copy(gw_ref.at[me], rwin_ref.at[me], local_sems.at[0])

        def to_hbm(jj):
            slab = me if jj == 7 else _dev_index(peers[jj])
            return pltpu.make_async_copy(stage.at[jj % 2], gw_ref.at[slab], stage_sems.at[jj % 2])

        @pl.when((t == 0) & (k == 0))
        def _():
            for cp in small_own:
                cp.start()
            for send, _ in small_pairs:
                send.start()

        @pl.when(t < NDEV)
        def _():
            p = jnp.dot(ht_ref[...], dp_ref[...], preferred_element_type=F32)

            @pl.when(k == 0)
            def _():
                acc[...] = p

            @pl.when(k > 0)
            def _():
                acc[...] += p

        for jj in range(NDEV):
            @pl.when((t == jj) & (k == nk - 1))
            def _(jj=jj):
                stage[jj % 2] = acc[...].astype(BF16)
                to_hbm(jj).start()

            @pl.when((t == jj + 1) & (k == 1))
            def _(jj=jj):
                to_hbm(jj).wait()
                if jj < 7:
                    win_pairs[jj][0].start()
                else:
                    win_own.start()

        def matmul_step():
            p = lax.dot_general(dp_ref[...], w_ref[...], NT, preferred_element_type=F32)
            slot = t % 2
            dh[slot] = jnp.where(k == 0, p, dh[slot] + p)

        def norm_step():
            g = dh.at[(t + 1) % 2][pl.ds(pl.multiple_of(k * rows_per_step, rows_per_step), rows_per_step), :]
            xf = x_ref[...]
            r = lax.rsqrt(jnp.mean(xf * xf, axis=-1, keepdims=True) + EPS)
            xh = xf * r
            dn = g * (1.0 + sc_ref[...])
            dxh = dn * nw_ref[...]
            gx_ref[...] = dy_ref[...] + r * (dxh - xh * jnp.mean(dxh * xh, axis=-1, keepdims=True))
            st_ref[0:1, :] += jnp.sum(g, axis=0, keepdims=True)
            st_ref[1:2, :] += jnp.sum(g * xh * nw_ref[...], axis=0, keepdims=True)
            st_ref[2:3, :] += jnp.sum(dn * xh, axis=0, keepdims=True)

        @pl.when((t == 0) & (k == 0))
        def _():
            st_ref[...] = jnp.zeros_like(st_ref)

        @pl.when(t == NDEV)
        def _():
            matmul_step()

        @pl.when((t > NDEV) & (t < last))
        def _():
            matmul_step()
            norm_step()

        @pl.when(t == last)
        def _():
            norm_step()

        @pl.when((t == last) & (k == nk - 1))
        def _():
            for _, recv in win_pairs + small_pairs:
                recv.wait_recv()
            for send, _ in win_pairs + small_pairs:
                send.wait_send()
            win_own.wait()
            for cp in small_own:
                cp.wait()

    any_spec = pl.BlockSpec(memory_space=pl.ANY)
    first = lambda t: t < NDEV
    slab = lambda t, k: jnp.where(t == last, NDEV - 1, k)
    chunk = pl.BlockSpec((rows_per_step, D), lambda t, k, o: (jnp.maximum((t - NDEV - 1) * nk + k, 0), 0))
    vec = pl.BlockSpec((1, D), lambda t, k, o: (0, 0))
    outs = pl.pallas_call(
        body, name="proj_bwd",
        grid_spec=pltpu.PrefetchScalarGridSpec(
            num_scalar_prefetch=1, grid=(last + 1, nk),
            in_specs=[pl.BlockSpec((D, tt), lambda t, k, o: (0, jnp.where(first(t), k, nk - 1))),
                      pl.BlockSpec((tt, SHARD), lambda t, k, o: (jnp.where(first(t), k, jnp.minimum(t, last - 1) - NDEV),
                                                                 jnp.where(first(t), o[jnp.minimum(t, NDEV - 1)],
                                                                           slab(t, k)))),
                      pl.BlockSpec((None, D, SHARD), lambda t, k, o: (jnp.where(first(t), 0, slab(t, k)), 0, 0)),
                      chunk, chunk, vec, vec]
                     + [any_spec] * n,
            out_specs=[chunk, pl.BlockSpec((8, D), lambda t, k, o: (0, 0))] + [any_spec] * (2 + n),
            scratch_shapes=[pltpu.VMEM((D, SHARD), F32), pltpu.VMEM((2, D, SHARD), BF16),
                            pltpu.VMEM((2, tt, D), F32),
                            pltpu.SemaphoreType.DMA((1 + n, 7)), pltpu.SemaphoreType.DMA((1 + n, 7)),
                            pltpu.SemaphoreType.DMA((1 + n,)), pltpu.SemaphoreType.DMA((2,))]),
        out_shape=[jax.ShapeDtypeStruct((s, D), F32), jax.ShapeDtypeStruct((8, D), F32),
                   jax.ShapeDtypeStruct((NDEV, D, SHARD), BF16), jax.ShapeDtypeStruct((NDEV, D, SHARD), BF16)]
                  + [jax.ShapeDtypeStruct(a.shape, a.dtype) for a in smalls],
        compiler_params=_cp(("arbitrary", "arbitrary"), 56))(order, ht, dproj, wg, x, dy, nw, scale, *smalls)
    return outs[0], outs[1], outs[3], outs[4:]


def matmuls_tn(pairs, name, tk):
    s = pairs[0][0].shape[0]
    nk = s // tk
    n = len(pairs)
    shapes = [(a.shape[1], b.shape[1]) for a, b in pairs]

    def body(*refs):
        ins, outs, accs = refs[:2 * n], refs[2 * n:3 * n], refs[3 * n:]
        k = pl.program_id(0)
        for j in range(n):
            p = lax.dot_general(ins[2 * j][...], ins[2 * j + 1][...], TN, preferred_element_type=F32)
            accs[j][...] = jnp.where(k == 0, p, accs[j][...] + p)

        @pl.when(k == nk - 1)
        def _():
            for j in range(n):
                outs[j][...] = accs[j][...].astype(BF16)

    return pl.pallas_call(
        body, name=name, grid=(nk,),
        in_specs=[pl.BlockSpec((tk, t.shape[1]), lambda k: (k, 0)) for pair in pairs for t in pair],
        out_specs=[pl.BlockSpec(sh, lambda k: (0, 0)) for sh in shapes],
        out_shape=[jax.ShapeDtypeStruct(sh, BF16) for sh in shapes],
        scratch_shapes=[pltpu.VMEM(sh, F32) for sh in shapes],
        compiler_params=_cp(("arbitrary",), 56))(*[t for pair in pairs for t in pair])


def _head_matrices():
    lane = lax.broadcasted_iota(jnp.int32, (CB, CB), 0)
    col = lax.broadcasted_iota(jnp.int32, (CB, CB), 1)
    same = (lane // HD == col // HD).astype(BF16)
    lane_c = lax.broadcasted_iota(jnp.int32, (CB, LANES), 0)
    col_c = lax.broadcasted_iota(jnp.int32, (CB, LANES), 1)
    total = (lane_c // HD == col_c).astype(BF16)
    lane_e = lax.broadcasted_iota(jnp.int32, (LANES, CB), 0)
    col_e = lax.broadcasted_iota(jnp.int32, (LANES, CB), 1)
    expand = (lane_e == col_e // HD).astype(BF16)
    return same, total, expand


def _head_sum(x, m_ref):
    return jnp.dot(x.astype(BF16), m_ref[...], preferred_element_type=F32)


def _dot_hilo(x, m_ref):
    hi = x.astype(BF16)
    lo = (x - hi.astype(F32)).astype(BF16)
    return (jnp.dot(hi, m_ref[...], preferred_element_type=F32)
            + jnp.dot(lo, m_ref[...], preferred_element_type=F32))


def _to_residue_major(val, buf, out_ref, dil):
    rows = out_ref.shape[1]
    for k in range(val.shape[1] // LANES):
        lanes = slice(k * LANES, (k + 1) * LANES)
        buf[k] = val[:, lanes]
        for r in range(dil):
            out_ref[r, :, lanes] = buf.at[k][pl.ds(r, rows, stride=dil), :].astype(out_ref.dtype)


def _from_residue_major(ref, buf, dil):
    if dil == 1:
        return ref[0].astype(F32)
    rows, chunks = ref.shape[1], ref.shape[2] // LANES
    for k in range(chunks):
        for r in range(dil):
            buf.at[k][pl.ds(r, rows, stride=dil), :] = ref[r, :, k * LANES:(k + 1) * LANES].astype(F32)
    return jnp.concatenate([buf[k] for k in range(chunks)], axis=1)


def qkv_prep(proj, qw8, kw8, same, tm):
    s = proj.shape[0]
    items = []
    for g, d in enumerate(DILATIONS):
        items += [(g, "q", CB_Q + g, d), (g, "k", CB_K + g, d)] + ([(g, "v", CB_V + g, d)] if d > 1 else [])
    n = len(items)

    def body(*refs):
        ins, (qw_ref, kw_ref, same_ref), outs, buf = refs[:n], refs[n:n + 3], refs[n + 3:2 * n + 3], refs[-1]
        for idx, (_, kind, _, dil) in enumerate(items):
            val = ins[idx][...].astype(F32)
            if kind != "v":
                r = lax.rsqrt(_head_sum(val * val, same_ref) * (1.0 / HD) + EPS)
                val = val * r * (qw_ref if kind == "q" else kw_ref)[...]
            if dil == 1:
                outs[idx][0] = val.astype(BF16)
            else:
                _to_residue_major(val, buf, outs[idx], dil)

    full = lambda a: pl.BlockSpec(a.shape, lambda i: (0, 0))
    outs = pl.pallas_call(
        body, name="qkv_prep", grid=(s // tm,),
        in_specs=[pl.BlockSpec((tm, CB), lambda i, cb=cb: (i, cb)) for _, _, cb, _ in items]
                 + [full(qw8), full(kw8), full(same)],
        out_specs=[pl.BlockSpec((d, tm // d, CB), lambda i: (0, i, 0)) for _, _, _, d in items],
        out_shape=[jax.ShapeDtypeStruct((d, s // d, CB), BF16) for _, _, _, d in items],
        scratch_shapes=[pltpu.VMEM((CB // LANES, tm, LANES), F32)],
        compiler_params=_cp(("parallel",)))(*([proj] * n), qw8 * (HD ** -0.5), kw8, same)
    srcs = [[None, None, (proj, CB_V + g)] for g in range(len(DILATIONS))]
    for (g, kind, _, _), o in zip(items, outs):
        srcs[g]["qkv".index(kind)] = (o.reshape(s, CB), 0)
    return srcs


def stats_prep(da, lc, dc, tm):
    s = da.shape[0]

    def body(da_ref, lc_ref, dc_ref, *refs):
        outs, buf = list(refs[:-1]), refs[-1]
        for dil in DILATIONS:
            rows = tm // dil
            if dil > 1:
                _to_residue_major(da_ref[...].astype(F32), buf, outs.pop(0), dil)
            for src in (lc_ref, dc_ref):
                dst = outs.pop(0) if dil > 1 else None
                dst_t = outs.pop(0)
                buf[0] = src[...]
                for r in range(dil):
                    piece = buf.at[0][pl.ds(r, rows, stride=dil), :] if dil > 1 else buf[0]
                    if dil > 1:
                        dst[r] = piece
                    dst_t[r] = piece.T[0:NH, :]

    row = lambda w: pl.BlockSpec((tm, w), lambda i: (i, 0))
    out_specs, out_shape = [], []
    for dil in DILATIONS:
        rm = lambda w, dil=dil: (pl.BlockSpec((dil, tm // dil, w), lambda i: (0, i, 0)),
                                 jax.ShapeDtypeStruct((dil, s // dil, w), BF16 if w == CB else F32))
        tr = (pl.BlockSpec((dil, NH, tm // dil), lambda i: (0, 0, i)), jax.ShapeDtypeStruct((dil, NH, s // dil), F32))
        group = ([rm(CB)] if dil > 1 else []) + ([rm(LANES), tr, rm(LANES), tr] if dil > 1 else [tr, tr])
        out_specs += [sp for sp, _ in group]
        out_shape += [sh for _, sh in group]
    outs = list(pl.pallas_call(
        body, name="stats_prep", grid=(s // tm,),
        in_specs=[row(CB), row(LANES), row(LANES)], out_specs=out_specs, out_shape=out_shape,
        scratch_shapes=[pltpu.VMEM((CB // LANES, tm, LANES), F32)],
        compiler_params=_cp(("parallel",)))(da, lc, dc))
    res = []
    for dil in DILATIONS:
        flat_t = lambda t, dil=dil: t.reshape(dil * NH, s // dil)
        if dil == 1:
            lt, dt = outs.pop(0), outs.pop(0)
            res.append((da, lc, dc, flat_t(lt), flat_t(dt)))
        else:
            dap, lcp, lt, dcp, dt = (outs.pop(0) for _ in range(5))
            res.append((dap.reshape(s, CB), lcp.reshape(s, LANES), dcp.reshape(s, LANES), flat_t(lt), flat_t(dt)))
    return res


def qkv_grads_to_dproj(dproj, proj, grads, qw8, kw8, same, tm):
    s = dproj.shape[0]
    ni = s // tm
    flat = [(t.reshape(d, s // d, CB), d, kind, 3 * kind + g)
            for g, d in enumerate(DILATIONS) for kind, t in enumerate(grads[g])]
    nf = len(flat)
    nraw = 2 * len(DILATIONS)

    def body(*refs):
        dp_hbm, raws, ins = refs[nraw + nf + 4], refs[1:1 + nraw], refs[1 + nraw:1 + nraw + nf]
        qw_ref, kw_ref, same_ref = refs[1 + nraw + nf:4 + nraw + nf]
        gw_ref, stage, buf, sems = refs[5 + nraw + nf:]
        i = pl.program_id(0)
        slot = i % 2

        def slab(step, sl):
            return pltpu.make_async_copy(
                stage.at[sl], dp_hbm.at[pl.ds(pl.multiple_of(step * tm, tm), tm), pl.ds(CB_Q * CB, 9 * CB)],
                sems.at[sl])

        @pl.when(i == 0)
        def _():
            gw_ref[...] = jnp.zeros_like(gw_ref)

        @pl.when(i >= 2)
        def _():
            slab(i - 2, slot).wait()

        for ref, (_, d, kind, jj) in zip(ins, flat):
            cols = slice(jj * CB, (jj + 1) * CB)
            dn = _from_residue_major(ref, buf, d)
            if kind == 2:
                stage[slot, :, cols] = dn.astype(BF16)
                continue
            t = raws[jj][...].astype(F32)
            r = lax.rsqrt(_head_sum(t * t, same_ref) * (1.0 / HD) + EPS)
            xh = t * r
            gw_ref[kind:kind + 1, :] += jnp.sum(dn * xh, axis=0, keepdims=True)
            dxh = dn * (qw_ref if kind == 0 else kw_ref)[...]
            mean = _head_sum(dxh * xh, same_ref) * (1.0 / HD)
            stage[slot, :, cols] = (r * (dxh - xh * mean)).astype(BF16)
        slab(i, slot).start()

        @pl.when(i == ni - 1)
        def _():
            slab(i - 1, 1 - slot).wait()
            slab(i, slot).wait()

    full = lambda a: pl.BlockSpec(a.shape, lambda i: (0, 0))
    any_spec = pl.BlockSpec(memory_space=pl.ANY)
    return pl.pallas_call(
        body, name="qkv_grads_to_dproj", grid=(ni,),
        in_specs=[any_spec] + [pl.BlockSpec((tm, CB), lambda i, jb=jb: (i, CB_Q + jb)) for jb in range(nraw)]
                 + [pl.BlockSpec((d, tm // d, CB), lambda i: (0, i, 0)) for _, d, _, _ in flat]
                 + [full(qw8), full(kw8), full(same)],
        out_specs=[any_spec, pl.BlockSpec((8, CB), lambda i: (0, 0))],
        out_shape=[jax.ShapeDtypeStruct((s, NIN), BF16), jax.ShapeDtypeStruct((8, CB), F32)],
        input_output_aliases={0: 0},
        scratch_shapes=[pltpu.VMEM((2, tm, 9 * CB), BF16), pltpu.VMEM((CB // LANES, tm, LANES), F32),
                        pltpu.SemaphoreType.DMA((2,))],
        compiler_params=_cp(("arbitrary",)))(
            dproj, *([proj] * nraw), *[t for t, _, _, _ in flat], qw8, kw8, same)


def _lane_lo():
    return lax.broadcasted_iota(jnp.int32, (1, 2 * HD), 1) < HD


def _stack_heads(t, lo):
    zero = jnp.zeros_like(t)
    return jnp.concatenate([jnp.where(lo, t, zero), jnp.where(lo, zero, t)], axis=0)


def _masks(other_ok):
    qi = lax.broadcasted_iota(jnp.int32, (QB, QB), 0)
    kj = lax.broadcasted_iota(jnp.int32, (QB, QB), 1)
    return (kj >= qi) & other_ok, kj <= qi


MAX_SUB = 8


def _attn_specs(nb, dil, sub):
    steps = nb // sub
    main = lambda cb, w=CB: pl.BlockSpec((sub * QB, w), lambda r, s: (r * steps + s, cb))
    prev = lambda cb: pl.BlockSpec((QB, CB), lambda r, s: (jnp.maximum(r * nb + sub * s - 1, 0), cb))
    nxt = lambda cb: pl.BlockSpec((QB, CB), lambda r, s: (jnp.minimum(r * nb + sub * (s + 1), dil * nb - 1), cb))
    return main, prev, nxt


FWD_SUB = 4


def attn_fwd(srcs):
    s = srcs[0][0][0].shape[0]
    ng = len(DILATIONS)

    def body(*refs):
        ins, outs, bufs = refs[:5 * ng], refs[5 * ng:7 * ng], refs[7 * ng:]
        step = pl.program_id(0)
        lo = _lane_lo()
        head_lane = lax.broadcasted_iota(jnp.int32, (1, LANES), 1)
        for g, dil in enumerate(DILATIONS):
            nb = s // dil // QB
            q_ref, kp_ref, k_ref, vp_ref, v_ref = ins[5 * g:5 * g + 5]
            (o_ref, l_ref), (kbuf, vbuf) = outs[2 * g:2 * g + 2], bufs[2 * g:2 * g + 2]
            kbuf[0:QB], kbuf[QB:] = kp_ref[...], k_ref[...]
            vbuf[0:QB], vbuf[QB:] = vp_ref[...], v_ref[...]
            for j in range(FWD_SUB):
                rows, krows = slice(j * QB, (j + 1) * QB), slice(j * QB, (j + 2) * QB)
                m_prev, m_cur = _masks((step * FWD_SUB + j) % nb > 0)
                mask = jnp.concatenate([m_prev, m_cur], axis=1)
                mask = jnp.concatenate([mask, mask], axis=0)
                lses = jnp.zeros((QB, LANES), F32)
                for i in range(NH // 2):
                    sl = slice(2 * HD * i, 2 * HD * (i + 1))
                    qs, ks, vv = q_ref[rows, sl], kbuf[krows, sl], vbuf[krows, sl]
                    sc = lax.dot_general(_stack_heads(qs, lo), ks, NT, preferred_element_type=F32)
                    sc = jnp.where(mask, sc, NEG)
                    mx = jnp.max(sc, axis=-1, keepdims=True)
                    p = jnp.exp(sc - mx)
                    den = jnp.sum(p, axis=-1, keepdims=True)
                    o = jnp.dot(p.astype(BF16), vv, preferred_element_type=F32) * (1.0 / den)
                    lse = mx + jnp.log(den)
                    o_ref[rows, sl] = jnp.where(lo, o[:QB], o[QB:]).astype(BF16)
                    lses = jnp.where(head_lane == 2 * i, lse[:QB], jnp.where(head_lane == 2 * i + 1, lse[QB:], lses))
                l_ref[rows, :] = lses

    main = lambda cb, w=CB: pl.BlockSpec((FWD_SUB * QB, w), lambda st: (st, cb))
    prev = lambda cb: pl.BlockSpec((QB, CB), lambda st: (jnp.maximum(FWD_SUB * st - 1, 0), cb))
    in_specs, args = [], []
    for q_src, k_src, v_src in srcs:
        in_specs += [main(q_src[1]), prev(k_src[1]), main(k_src[1]), prev(v_src[1]), main(v_src[1])]
        args += [q_src[0], k_src[0], k_src[0], v_src[0], v_src[0]]
    outs = pl.pallas_call(
        body, name="attn_fwd", grid=(s // (FWD_SUB * QB),),
        in_specs=in_specs, out_specs=[main(0), main(0, LANES)] * ng,
        out_shape=[jax.ShapeDtypeStruct((s, CB), BF16), jax.ShapeDtypeStruct((s, LANES), F32)] * ng,
        scratch_shapes=[pltpu.VMEM(((FWD_SUB + 1) * QB, CB), BF16)] * (2 * ng),
        compiler_params=_cp(("parallel",)))(*args)
    return outs[0::2], outs[1::2]


def attn_bwd(q_src, k_src, v_src, da, lc, dc, lt, dt, g, dil):
    s = q_src[0].shape[0]
    nb = s // dil // QB
    sub = min(MAX_SUB, nb)
    main, prev, nxt = _attn_specs(nb, dil, sub)

    def body(q_ref, kp_ref, k_ref, vp_ref, v_ref, qn_ref, da_ref, dan_ref, lc_ref, dc_ref, l_ref, ln_ref,
             d_ref, dn_ref, dq_ref, dk_ref, dv_ref, kbuf, vbuf, qbuf, dabuf, lbuf, dbuf):
        step = pl.program_id(1)
        kbuf[0:QB], kbuf[QB:] = kp_ref[...], k_ref[...]
        vbuf[0:QB], vbuf[QB:] = vp_ref[...], v_ref[...]
        qbuf[0:sub * QB], qbuf[sub * QB:] = q_ref[...], qn_ref[...]
        dabuf[0:sub * QB], dabuf[sub * QB:] = da_ref[...], dan_ref[...]
        for c in range(sub):
            lbuf[c], dbuf[c] = l_ref[:, c * QB:(c + 1) * QB], d_ref[:, c * QB:(c + 1) * QB]
        lbuf[sub], dbuf[sub] = ln_ref[...], dn_ref[...]
        lo = _lane_lo()
        kj = lax.broadcasted_iota(jnp.int32, (QB, QB), 0)
        qi = lax.broadcasted_iota(jnp.int32, (QB, QB), 1)

        def block(j, carry):
            r0 = pl.multiple_of(j * QB, QB)
            rows, two = pl.ds(r0, QB), pl.ds(r0, 2 * QB)
            m_prev, m_cur = _masks(step * sub + j > 0)
            qmask = jnp.concatenate([m_prev, m_cur], axis=1)
            qmask = jnp.concatenate([qmask, qmask], axis=0)
            lcols, dcols = lc_ref[rows, :], dc_ref[rows, :]
            kmask = jnp.concatenate([kj <= qi, (kj >= qi) & (step * sub + j < nb - 1)], axis=1)
            kmask = jnp.concatenate([kmask, kmask], axis=1)
            lrow = jnp.concatenate([lbuf[j], lbuf[j + 1]], axis=1)
            drow = jnp.concatenate([dbuf[j], dbuf[j + 1]], axis=1)
            for i in range(NH // 2):
                sl = slice(2 * HD * i, 2 * HD * (i + 1))
                col_pair = lambda t: jnp.concatenate([t[:, 2 * i:2 * i + 1], t[:, 2 * i + 1:2 * i + 2]], axis=0)
                row_pair = lambda t: jnp.concatenate([t[2 * i:2 * i + 1, :], t[2 * i + 1:2 * i + 2, :]], axis=1)
                ks2, vv2 = kbuf[two, sl], vbuf[two, sl]
                sc = lax.dot_general(_stack_heads(q_ref[rows, sl], lo), ks2, NT, preferred_element_type=F32)
                p = jnp.exp(jnp.where(qmask, sc, NEG) - col_pair(lcols))
                dp = lax.dot_general(_stack_heads(da_ref[rows, sl], lo), vv2, NT, preferred_element_type=F32)
                ds = p * (dp - col_pair(dcols))
                dq = jnp.dot(ds.astype(BF16), ks2, preferred_element_type=F32)
                dq_ref[rows, sl] = (jnp.where(lo, dq[:QB], dq[QB:]) * (HD ** -0.5)).astype(BF16)

                q2, da2 = _stack_heads(qbuf[two, sl], lo), _stack_heads(dabuf[two, sl], lo)
                ks, vv = k_ref[rows, sl], v_ref[rows, sl]
                sct = lax.dot_general(ks, q2, NT, preferred_element_type=F32)
                pt = jnp.exp(jnp.where(kmask, sct, NEG) - row_pair(lrow))
                dpt = lax.dot_general(vv, da2, NT, preferred_element_type=F32)
                dst = pt * (dpt - row_pair(drow))
                dv_ref[rows, sl] = jnp.dot(pt.astype(BF16), da2, preferred_element_type=F32).astype(BF16)
                dk_ref[rows, sl] = jnp.dot(dst.astype(BF16), q2, preferred_element_type=F32).astype(BF16)
            return carry

        lax.fori_loop(0, sub, block, 0, unroll=True)

    t_main = pl.BlockSpec((NH, sub * QB), lambda r, st: (r, st))
    t_nxt = pl.BlockSpec((NH, QB), lambda r, st: (r, jnp.minimum(sub * (st + 1), nb - 1)))
    out = jax.ShapeDtypeStruct((s, CB), BF16)
    big = pltpu.VMEM(((sub + 1) * QB, CB), BF16)
    return pl.pallas_call(
        body, name=f"attn_bwd_g{g}", grid=(dil, nb // sub),
        in_specs=[main(q_src[1]), prev(k_src[1]), main(k_src[1]), prev(v_src[1]), main(v_src[1]), nxt(q_src[1]),
                  main(0), nxt(0), main(0, LANES), main(0, LANES), t_main, t_nxt, t_main, t_nxt],
        out_specs=[main(0)] * 3, out_shape=[out] * 3,
        scratch_shapes=[big] * 4 + [pltpu.VMEM((sub + 1, NH, QB), F32)] * 2,
        compiler_params=_cp(("parallel", "parallel")))(
            q_src[0], k_src[0], k_src[0], v_src[0], v_src[0], q_src[0], da, da, lc, dc, lt, lt, dt, dt)


def _conv_taps(u, u_prev, first):
    tm = u.shape[0]
    row = lax.broadcasted_iota(jnp.int32, (tm, 1), 0)
    up = jnp.where(first, 0.0, u_prev)
    u1 = jnp.where(row == 0, up[HALO - 1:HALO, :], pltpu.roll(u, 1, 0))
    u2 = jnp.where(row == 0, up[HALO - 2:HALO - 1, :],
                   jnp.where(row == 1, up[HALO - 1:HALO, :], pltpu.roll(u, 2, 0)))
    return u1, u2


def mid_fwd(proj, o_g, lse_g, conv_w, expand, tm):
    s = proj.shape[0]
    hb = tm // HALO

    def body(ba_ref, ca_ref, xa_ref, za_ref, cah_ref, xah_ref, zb_ref,
             o0, o1, o2, l0, l1, l2, w_ref, exp_ref, ya_ref, yb_ref, at_ref, lc_ref, buf_o, buf_l):
        first = pl.program_id(0) == 0
        u = ca_ref[...].astype(F32) * xa_ref[...].astype(F32)
        u1, u2 = _conv_taps(u, cah_ref[...].astype(F32) * xah_ref[...].astype(F32), first)
        conv = w_ref[0:1, :] * u2 + w_ref[1:2, :] * u1 + w_ref[2:3, :] * u
        ya_ref[...] = (ba_ref[...].astype(F32) * conv * _silu(za_ref[...].astype(F32))).astype(BF16)
        ls = [_from_residue_major(l, buf_l.at[g], d) for g, (l, d) in enumerate(zip((l0, l1, l2), DILATIONS))]
        mx = jnp.maximum(jnp.maximum(ls[0], ls[1]), ls[2])
        es = [jnp.exp(l - mx) for l in ls]
        den = es[0] + es[1] + es[2]
        attn = jnp.zeros((tm, CB), F32)
        for e, o, d in zip(es, (o0, o1, o2), DILATIONS):
            attn = attn + _dot_hilo(e / den, exp_ref) * _from_residue_major(o, buf_o, d)
        at_ref[...] = attn
        lc_ref[...] = mx + jnp.log(den)
        yb_ref[...] = (attn * _silu(zb_ref[...].astype(F32))).astype(BF16)

    col = lambda j: pl.BlockSpec((tm, D), lambda i: (i, j))
    halo = lambda j: pl.BlockSpec((HALO, D), lambda i: (jnp.maximum(i * hb - 1, 0), j))
    loc = lambda w: pl.BlockSpec((tm, w), lambda i: (i, 0))
    rm = lambda w: [pl.BlockSpec((d, tm // d, w), lambda i: (0, i, 0)) for d in DILATIONS]
    rm_view = lambda ts, w: [t.reshape(d, s // d, w) for t, d in zip(ts, DILATIONS)]
    return pl.pallas_call(
        body, name="mid_fwd", grid=(s // tm,),
        in_specs=[col(0), col(1), col(2), col(3), halo(1), halo(2),
                  pl.BlockSpec((tm, CB), lambda i: (i, CB_ZB))] + rm(CB) + rm(LANES)
                 + [pl.BlockSpec((3, D), lambda i: (0, 0)), pl.BlockSpec(expand.shape, lambda i: (0, 0))],
        out_specs=[loc(D), loc(CB), loc(CB), loc(LANES)],
        out_shape=[jax.ShapeDtypeStruct((s, D), BF16), jax.ShapeDtypeStruct((s, CB), BF16),
                   jax.ShapeDtypeStruct((s, CB), F32), jax.ShapeDtypeStruct((s, LANES), F32)],
        scratch_shapes=[pltpu.VMEM((CB // LANES, tm, LANES), F32), pltpu.VMEM((3, 1, tm, LANES), F32)],
        compiler_params=_cp(("parallel",)))(
            proj, proj, proj, proj, proj, proj, proj, *rm_view(o_g, CB), *rm_view(lse_g, LANES), conv_w, expand)


def tail(proj, ya, yb, attn, x, target, gate, pa_w, pb_w, wo_w, total, conv_w, tm):
    s = proj.shape[0]
    ni = s // tm
    hb = tm // HALO
    nlate = NIN - CB_ZB * CB
    nearly = 4 * D

    def body(ya_ref, yb_ref, ga_ref, gb_ref, zb_ref, at_ref, x_ref, t_ref, gate_ref, pa_ref, pb_ref, wo_ref,
             tot_ref, ba_ref, ca_ref, xa_ref, za_ref, cah_ref, xah_ref, cw_ref,
             dp_hbm, dy_ref, da_ref, dc_ref, mg_ref, do_ref, dpa_ref, dpb_ref, st_ref, gwc_ref,
             stage, dconv_next, sems):
        step = pl.program_id(0)
        i = ni - 1 - step
        slot = step % 2

        def slabs(at_step, sl):
            rows = pl.ds(pl.multiple_of((ni - 1 - at_step) * tm, tm), tm)
            return (pltpu.make_async_copy(stage.at[sl, :, 0:nearly], dp_hbm.at[rows, pl.ds(0, nearly)],
                                          sems.at[sl, 0]),
                    pltpu.make_async_copy(stage.at[sl, :, nearly:], dp_hbm.at[rows, pl.ds(CB_ZB * CB, nlate)],
                                          sems.at[sl, 1]))

        @pl.when(step == 0)
        def _():
            st_ref[...] = jnp.zeros_like(st_ref)
            gwc_ref[...] = jnp.zeros_like(gwc_ref)
            dconv_next[...] = jnp.zeros_like(dconv_next)

        @pl.when(step >= 2)
        def _():
            for cp in slabs(step - 2, slot):
                cp.wait()

        gate_v = gate_ref[...]
        pa = jnp.dot(ya_ref[...], pa_ref[...], preferred_element_type=F32)
        pb = jnp.dot(yb_ref[...], pb_ref[...], preferred_element_type=F32)
        sa = jax.nn.sigmoid(ga_ref[...].astype(F32))
        sb = jax.nn.sigmoid(gb_ref[...].astype(F32))
        merged = (sa * pa + sb * pb).astype(BF16)
        mg_ref[...] = merged
        out = jnp.dot(merged, wo_ref[...], preferred_element_type=F32)
        err = x_ref[...] + gate_v * out - t_ref[...]
        dy = err * (1.0 / D)
        dy_ref[...] = dy
        st_ref[0:1, :] += jnp.sum(dy * out, axis=0, keepdims=True)
        st_ref[1:2, :] += jnp.sum(err * err, axis=0, keepdims=True)
        dout = (gate_v * dy).astype(BF16)
        do_ref[...] = dout
        dmg = lax.dot_general(dout, wo_ref[...], NT, preferred_element_type=F32)
        dpa = (dmg * sa).astype(BF16)
        dpb = (dmg * sb).astype(BF16)
        dpa_ref[...] = dpa
        dpb_ref[...] = dpb
        late = nearly
        stage[slot, :, late + CB:late + CB + D] = (dmg * pa * sa * (1.0 - sa)).astype(BF16)
        stage[slot, :, late + CB + D:] = (dmg * pb * sb * (1.0 - sb)).astype(BF16)
        dya = lax.dot_general(dpa, pa_ref[...], NT, preferred_element_type=F32)
        dyb = lax.dot_general(dpb, pb_ref[...], NT, preferred_element_type=F32)
        zb = zb_ref[...].astype(F32)
        sg = jax.nn.sigmoid(zb)
        attn_v = at_ref[...]
        dattn = dyb * (zb * sg)
        da_ref[...] = dattn.astype(BF16)
        stage[slot, :, late:late + CB] = (dyb * attn_v * (sg * (1.0 + zb * (1.0 - sg)))).astype(BF16)
        dc_ref[...] = _dot_hilo(dattn * attn_v, tot_ref)

        ba, ca, xa, za = (t[...].astype(F32) for t in (ba_ref, ca_ref, xa_ref, za_ref))
        u = ca * xa
        u1, u2 = _conv_taps(u, cah_ref[...].astype(F32) * xah_ref[...].astype(F32), i == 0)
        w0, w1, w2 = cw_ref[0:1, :], cw_ref[1:2, :], cw_ref[2:3, :]
        conv = w0 * u2 + w1 * u1 + w2 * u
        sga = jax.nn.sigmoid(za)
        sza = za * sga
        dconv = dya * ba * sza
        dcn = dconv_next[...]
        rowi = lax.broadcasted_iota(jnp.int32, (tm, 1), 0)
        d1 = jnp.where(rowi == tm - 1, dcn[0:1, :], pltpu.roll(dconv, tm - 1, 0))
        d2 = jnp.where(rowi == tm - 2, dcn[0:1, :],
                       jnp.where(rowi == tm - 1, dcn[1:2, :], pltpu.roll(dconv, tm - 2, 0)))
        du = w2 * dconv + w1 * d1 + w0 * d2
        stage[slot, :, 0:D] = (dya * conv * sza).astype(BF16)
        stage[slot, :, D:2 * D] = (du * xa).astype(BF16)
        stage[slot, :, 2 * D:3 * D] = (du * ca).astype(BF16)
        stage[slot, :, 3 * D:4 * D] = (dya * ba * conv * (sga * (1.0 + za * (1.0 - sga)))).astype(BF16)
        gwc_ref[0:1, :] += jnp.sum(dconv * u2, axis=0, keepdims=True)
        gwc_ref[1:2, :] += jnp.sum(dconv * u1, axis=0, keepdims=True)
        gwc_ref[2:3, :] += jnp.sum(dconv * u, axis=0, keepdims=True)
        dconv_next[...] = dconv[0:8, :]

        for cp in slabs(step, slot):
            cp.start()

        @pl.when(step == ni - 1)
        def _():
            for cp in slabs(step - 1, 1 - slot) + slabs(step, slot):
                cp.wait()

    rev = lambda st: ni - 1 - st
    row = lambda w: pl.BlockSpec((tm, w), lambda st: (rev(st), 0))
    pcol = lambda w, jb: pl.BlockSpec((tm, w), lambda st: (rev(st), jb))
    halo = lambda jb: pl.BlockSpec((HALO, D), lambda st: (jnp.maximum(rev(st) * hb - 1, 0), jb))
    const = lambda a: pl.BlockSpec(a.shape, lambda st: (0, 0), pipeline_mode=pl.Buffered(1))
    acc = pl.BlockSpec((8, D), lambda st: (0, 0))
    return pl.pallas_call(
        body, name="tail", grid=(ni,),
        in_specs=[row(D), row(CB), pcol(D, 9), pcol(D, 10), pcol(CB, CB_ZB), row(CB), row(D), row(D),
                  pl.BlockSpec((1, D), lambda st: (0, 0)), const(pa_w), const(pb_w), const(wo_w), const(total),
                  pcol(D, 0), pcol(D, 1), pcol(D, 2), pcol(D, 3), halo(1), halo(2),
                  pl.BlockSpec((3, D), lambda st: (0, 0))],
        out_specs=[pl.BlockSpec(memory_space=pl.ANY),
                   row(D), row(CB), row(LANES), row(D), row(D), row(D), row(D), acc, acc],
        out_shape=[jax.ShapeDtypeStruct((s, NIN), BF16), jax.ShapeDtypeStruct((s, D), F32),
                   jax.ShapeDtypeStruct((s, CB), BF16), jax.ShapeDtypeStruct((s, LANES), F32)]
                  + [jax.ShapeDtypeStruct((s, D), BF16)] * 4 + [jax.ShapeDtypeStruct((8, D), F32)] * 2,
        scratch_shapes=[pltpu.VMEM((2, tm, nearly + nlate), BF16), pltpu.VMEM((8, D), F32),
                        pltpu.SemaphoreType.DMA((2, 2))],
        compiler_params=_cp(("arbitrary",), 60))(
            ya, yb, proj, proj, proj, attn, x, target, gate, pa_w, pb_w, wo_w, total,
            proj, proj, proj, proj, proj, proj, conv_w)


def _local_step(x, target, shift, scale, gate, norm_w, conv_w, qw, kw, w_shard, small_shards, me_xyc):
    qw8, kw8 = jnp.tile(qw, (1, NH)), jnp.tile(kw, (1, NH))
    same, total, expand = _head_matrices()
    proj, ht, wg, (pa_g, pb_g, wo_g) = proj_fwd_gather(
        x, norm_w, scale, shift, w_shard, small_shards, gather_order(me_xyc), 1024)
    pa_w, wo_w = pa_g.reshape(D, D), wo_g.reshape(D, D)
    pb_w = pb_g.transpose(1, 0, 2).reshape(CB, D)
    srcs = qkv_prep(proj, qw8, kw8, same, 512)
    o_g, lse_g = attn_fwd(srcs)
    ya, yb, attn, lc = mid_fwd(proj, o_g, lse_g, conv_w, expand, 512)
    dproj, dy, da, dc, merged, dout, dpa, dpb, st_tail, st_conv = tail(
        proj, ya, yb, attn, x, target, gate, pa_w, pb_w, wo_w, total, conv_w, 256)
    g_wo, g_pa, g_pb = matmuls_tn([(merged, dout), (ya, dpa), (yb, dpb)], "grad_small_weights", 1024)
    grads = []
    for g, (d, (da_p, lc_p, dc_p, lt, dt)) in enumerate(zip(DILATIONS, stats_prep(da, lc, dc, 2048))):
        grads.append(attn_bwd(*srcs[g], da_p, lc_p, dc_p, lt, dt, g, d))
    dproj, gw_qk = qkv_grads_to_dproj(dproj, proj, grads, qw8, kw8, same, 512)
    slabs = [g_pa.reshape(NDEV, 128, D), g_pb.reshape(CB, NDEV, 128).transpose(1, 0, 2), g_wo.reshape(NDEV, 128, D)]
    grad_x, st_norm, r_win, (r_pa, r_pb, r_wo) = proj_bwd(
        ht, dproj, wg, slabs, scatter_order(me_xyc), x, dy, norm_w, scale, 1024)
    dmod = jnp.concatenate([st_norm[0:1], st_norm[1:2], st_tail[0:1]], axis=1)
    loss_part = (0.5 / D) * jnp.sum(st_tail[1])
    gw_heads = gw_qk[0:2].reshape(2, NH, HD).sum(axis=1)
    small = dict(dmod=dmod, norm_w=st_norm[2:3], conv_w=st_conv[0:3],
                 q_norm_w=gw_heads[0:1], k_norm_w=gw_heads[1:2], loss=loss_part)
    return grad_x, small, (r_win, r_pa, r_pb, r_wo)


def kernel(x, c, w_ada, b_ada, norm_w, w_in, conv_w, q_norm_w, k_norm_w, w_br_conv, w_br_attn, w_out, loss_target, m_w_ada, m_b_ada, m_norm_w, m_w_in, m_conv_w, m_q_norm_w, m_k_norm_w, m_w_br_conv, m_w_br_attn, m_w_out, v_w_ada, v_b_ada, v_norm_w, v_w_in, v_conv_w, v_q_norm_w, v_k_norm_w, v_w_br_conv, v_w_br_attn, v_w_out):
    me_xyc = (lax.axis_index("x"), lax.axis_index("y"), lax.axis_index("c"))
    me = _dev_index(me_xyc)
    ncol = w_ada.shape[2]

    conv_pad = jnp.zeros((8, 128), F32).at[0:3].set(conv_w[0])
    b_cols = lax.dynamic_slice(b_ada, (0, me * ncol), (1, ncol))
    mod_pieces, c_all, conv_all = ada_fwd(c, conv_pad, w_ada[0], b_cols)
    conv_full = conv_all[:, 0:3].transpose(1, 0, 2).reshape(3, D)
    c_all = c_all.reshape(NDEV, D)
    mod = mod_pieces.reshape(1, 3 * D)
    shift, scale, gate = mod[:, 0:D], mod[:, D:2 * D], mod[:, 2 * D:3 * D]

    grad_x, small, (r_win, r_pa, r_pb, r_wo) = _local_step(
        x[0], loss_target[0], shift, scale, gate, norm_w, conv_full, q_norm_w, k_norm_w,
        w_in[0].astype(BF16), [w_br_conv[0].astype(BF16), w_br_attn[0].astype(BF16), w_out[0].astype(BF16)], me_xyc)

    packed = jnp.concatenate(
        [small["dmod"], small["norm_w"], small["conv_w"].reshape(1, 3 * D), small["q_norm_w"], small["k_norm_w"],
         jnp.full((1, 128), small["loss"], F32)], axis=1)
    packed_all, tot = gather_sum(packed)
    loss = tot[0, 7 * D + 2 * HD]
    dmod_all = packed_all[:, 0, 0:3 * D]
    g_b_ada = tot[:, 0:3 * D]
    g_norm_w = tot[:, 3 * D:4 * D]
    g_conv = lax.dynamic_slice(tot[:, 4 * D:7 * D].reshape(3, D), (0, me * 128), (3, 128))
    g_qn = tot[:, 7 * D:7 * D + HD]
    g_kn = tot[:, 7 * D + HD:7 * D + 2 * HD]
    g_w_ada = ada_bwd(c_all.T, lax.dynamic_slice(dmod_all, (0, me * ncol), (NDEV, ncol)))

    def upd(parts, w, m, v, name, rows):
        shape = w.shape
        w2, m2, v2 = (t.reshape(shape[-2:]) for t in (w, m, v))
        return [t.reshape(shape) for t in adamw(parts, w2, m2, v2, name, rows)]

    res = {"w_in": upd(r_win, w_in, m_w_in, v_w_in, "adamw_w_in", 128)}
    small_params = {"w_ada": (g_w_ada[None], w_ada, m_w_ada, v_w_ada), "b_ada": (g_b_ada[None], b_ada, m_b_ada, v_b_ada),
                    "norm_w": (g_norm_w[None], norm_w, m_norm_w, v_norm_w),
                    "conv_w": (g_conv[None], conv_w, m_conv_w, v_conv_w),
                    "q_norm_w": (g_qn[None], q_norm_w, m_q_norm_w, v_q_norm_w),
                    "k_norm_w": (g_kn[None], k_norm_w, m_k_norm_w, v_k_norm_w),
                    "w_br_conv": (r_pa, w_br_conv, m_w_br_conv, v_w_br_conv),
                    "w_br_attn": (r_pb, w_br_attn, m_w_br_attn, v_w_br_attn),
                    "w_out": (r_wo, w_out, m_w_out, v_w_out)}
    updated = adamw_small([(item[0],) + tuple(t.reshape(t.shape[-2:]) for t in item[1:])
                           for item in small_params.values()])
    for (pname, item), outs4 in zip(small_params.items(), updated):
        res[pname] = [t.reshape(item[1].shape) for t in outs4]
    names = ["w_ada", "b_ada", "norm_w", "w_in", "conv_w", "q_norm_w", "k_norm_w", "w_br_conv", "w_br_attn", "w_out"]
    return (loss, grad_x[None], *[res[n][0] for n in names], *[res[n][1] for n in names],
            *[res[n][2] for n in names], *[res[n][3] for n in names])
```

```python
import jax
import jax.numpy as jnp
from jax import lax
from jax.experimental import pallas as pl
from jax.experimental.pallas import tpu as pltpu

F32, BF16 = jnp.float32, jnp.bfloat16
D = 1024
NIN = 11264
NDEV = 8
SHARD = NIN // NDEV
HD = 64
NH = 8
QB = 128
CB = 512
CB_Q, CB_K, CB_V, CB_ZB = 8, 11, 14, 17
DILATIONS = (1, 4, 16)
EPS = 1e-6
NEG = -1e30
HALO = 16
LANES = 128
MESH = pl.DeviceIdType.MESH

ADAM_LR, ADAM_B1, ADAM_B2, ADAM_EPS, ADAM_WD, ADAM_STEP = 0.001, 0.9, 0.999, 1e-08, 0.01, 10

NT = (((1,), (1,)), ((), ()))
TN = (((0,), (0,)), ((), ()))


def _cp(sem, vmem_mb=48):
    return pltpu.CompilerParams(dimension_semantics=sem, vmem_limit_bytes=vmem_mb << 20)


def _silu(z):
    return z * jax.nn.sigmoid(z)


def _coords():
    return lax.axis_index("x"), lax.axis_index("y"), lax.axis_index("c")


FLIPS = [(fx, fy, fc) for fx in (0, 1) for fy in (0, 1) for fc in (0, 1)][1:]


def gather_sum(vec):
    def body(v_ref, all_ref, sum_ref, send_sems, recv_sems, local_sem):
        me_xyc = _coords()
        me = _dev_index(me_xyc)
        peers = [_flip(me_xyc, f) for f in FLIPS]

        def copy(k, block):
            return pltpu.make_async_remote_copy(
                src_ref=v_ref, dst_ref=all_ref.at[block], send_sem=send_sems.at[k], recv_sem=recv_sems.at[k],
                device_id=peers[k], device_id_type=MESH)

        mine = pltpu.make_async_copy(v_ref, all_ref.at[me], local_sem)
        sends = [copy(k, me) for k in range(7)]
        for cp in [mine] + sends:
            cp.start()
        for k in range(7):
            copy(k, _dev_index(peers[k])).wait_recv()
        mine.wait()
        acc = all_ref[0]
        for b in range(1, NDEV):
            acc = acc + all_ref[b]
        sum_ref[...] = acc
        for cp in sends:
            cp.wait_send()

    return pl.pallas_call(
        body, name="gather_sum",
        out_shape=[jax.ShapeDtypeStruct((NDEV,) + vec.shape, F32), jax.ShapeDtypeStruct(vec.shape, F32)],
        scratch_shapes=[pltpu.SemaphoreType.DMA((7,)), pltpu.SemaphoreType.DMA((7,)), pltpu.SemaphoreType.DMA],
    )(vec)


def _flip(dev, f):
    return tuple(1 - v if b else v for v, b in zip(dev, f))


def _dev_index(dev):
    return 4 * dev[0] + 2 * dev[1] + dev[2]


def _chip_order(x, y, c):
    xor = lambda a, b: a + b - 2 * a * b
    return [(xor(x, 1 - c), xor(y, c)), (xor(x, c), xor(y, 1 - c)), (1 - x, 1 - y)]


def gather_order(me_xyc):
    x, y, c = me_xyc
    chips = _chip_order(x, y, c)
    devs = [(x, y, c), (x, y, 1 - c), (*chips[0], c), (*chips[1], c),
            (*chips[1], 1 - c), (*chips[0], 1 - c), (*chips[2], c), (*chips[2], 1 - c)]
    return jnp.stack([_dev_index(d) for d in devs]).astype(jnp.int32)


def scatter_order(me_xyc):
    devs = [_flip(me_xyc, f) for f in FLIPS] + [me_xyc]
    return jnp.stack([_dev_index(d) for d in devs]).astype(jnp.int32)


def ada_fwd(c, conv_pad, w_ada, b_cols):
    ncol = w_ada.shape[1]

    def body(c_ref, cv_ref, w_ref, b_ref, mod_ref, call_ref, cvall_ref, rows_buf, send_sems, recv_sems, local_sems):
        me_xyc = _coords()
        me = _dev_index(me_xyc)
        peers = [_flip(me_xyc, f) for f in FLIPS]
        pids = [_dev_index(p) for p in peers]

        def copy(a, k, src, dst):
            return pltpu.make_async_remote_copy(src_ref=src, dst_ref=dst, send_sem=send_sems.at[a, k],
                                                recv_sem=recv_sems.at[a, k], device_id=peers[k], device_id_type=MESH)

        own = [pltpu.make_async_copy(c_ref, call_ref.at[me], local_sems.at[0]),
               pltpu.make_async_copy(cv_ref, cvall_ref.at[me], local_sems.at[1])]
        first = [copy(0, k, c_ref, call_ref.at[me]) for k in range(7)]
        first += [copy(1, k, cv_ref, cvall_ref.at[me]) for k in range(7)]
        for cp in own + first:
            cp.start()
        own[0].wait()
        for k in range(7):
            copy(0, k, c_ref, call_ref.at[pids[k]]).wait_recv()
        seq = lax.broadcasted_iota(jnp.int32, (NDEV, 1), 0)
        c_all = jnp.zeros((NDEV, D), F32)
        for p in range(NDEV):
            c_all = jnp.where(seq == p, call_ref[p], c_all)
        mods = jnp.dot(_silu(c_all).astype(BF16), w_ref[...].astype(BF16), preferred_element_type=F32) + b_ref[...]
        for p in range(NDEV):
            rows_buf[p] = mods[p:p + 1, :]
        mine = pltpu.make_async_copy(rows_buf.at[me], mod_ref.at[me], local_sems.at[2])
        second = [copy(2, k, rows_buf.at[pids[k]], mod_ref.at[me]) for k in range(7)]
        for cp in [mine] + second:
            cp.start()
        for k in range(7):
            copy(2, k, rows_buf.at[pids[k]], mod_ref.at[pids[k]]).wait_recv()
            copy(1, k, cv_ref, cvall_ref.at[pids[k]]).wait_recv()
        for cp in first + second:
            cp.wait_send()
        own[1].wait()
        mine.wait()

    return pl.pallas_call(
        body, name="ada_fwd",
        out_shape=[jax.ShapeDtypeStruct((NDEV, 1, ncol), F32), jax.ShapeDtypeStruct((NDEV, 1, D), F32),
                   jax.ShapeDtypeStruct((NDEV,) + conv_pad.shape, F32)],
        scratch_shapes=[pltpu.VMEM((NDEV, 1, ncol), F32), pltpu.SemaphoreType.DMA((3, 7)),
                        pltpu.SemaphoreType.DMA((3, 7)), pltpu.SemaphoreType.DMA((3,))],
    )(c, conv_pad, w_ada, b_cols)


def ada_bwd(c_all_t, dmod_cols):
    def body(c_ref, d_ref, o_ref):
        at = _silu(c_ref[...])
        acc = at[:, 0:1] * d_ref[0:1, :]
        for b in range(1, NDEV):
            acc = acc + at[:, b:b + 1] * d_ref[b:b + 1, :]
        o_ref[...] = acc

    return pl.pallas_call(body, name="ada_bwd",
                          out_shape=jax.ShapeDtypeStruct((D, dmod_cols.shape[1]), F32))(c_all_t, dmod_cols)


def _adamw_update(g, w_ref, m_ref, v_ref, g_ref, d_ref, nm_ref, nv_ref):
    nm = ADAM_B1 * m_ref[...] + (1.0 - ADAM_B1) * g
    nv = ADAM_B2 * v_ref[...] + (1.0 - ADAM_B2) * (g * g)
    g_ref[...] = g
    nm_ref[...] = nm
    nv_ref[...] = nv
    m_hat = nm / (1.0 - ADAM_B1 ** ADAM_STEP)
    v_hat = nv / (1.0 - ADAM_B2 ** ADAM_STEP)
    d_ref[...] = -ADAM_LR * (m_hat / (jnp.sqrt(v_hat) + ADAM_EPS) + ADAM_WD * w_ref[...])


def adamw_small(items):
    n = len(items)

    def body(*refs):
        ins, outs = refs[:4 * n], refs[4 * n:]
        for a in range(n):
            p_ref, w_ref, m_ref, v_ref = ins[4 * a:4 * a + 4]
            g = p_ref[0].astype(F32)
            for b in range(1, p_ref.shape[0]):
                g = g + p_ref[b].astype(F32)
            _adamw_update(g, w_ref, m_ref, v_ref, *outs[4 * a:4 * a + 4])

    out = pl.pallas_call(
        body, name="adamw_small",
        out_shape=[jax.ShapeDtypeStruct(it[1].shape, F32) for it in items for _ in range(4)],
        compiler_params=pltpu.CompilerParams(vmem_limit_bytes=48 << 20))(*[t for it in items for t in it])
    return [out[4 * a:4 * a + 4] for a in range(n)]


def adamw(parts, w, m, v, name, rows):
    n, r, ccols = parts.shape

    def body(p_ref, w_ref, m_ref, v_ref, g_ref, d_ref, nm_ref, nv_ref):
        g = p_ref[0].astype(F32)
        for b in range(1, n):
            g = g + p_ref[b].astype(F32)
        _adamw_update(g, w_ref, m_ref, v_ref, g_ref, d_ref, nm_ref, nv_ref)

    blk = pl.BlockSpec((rows, ccols), lambda i: (i, 0))
    out = jax.ShapeDtypeStruct((r, ccols), F32)
    return pl.pallas_call(
        body, name=name, grid=(r // rows,),
        in_specs=[pl.BlockSpec((n, rows, ccols), lambda i: (0, i, 0)), blk, blk, blk],
        out_specs=[blk] * 4, out_shape=[out] * 4, compiler_params=_cp(("parallel",)))(parts, w, m, v)


def proj_fwd_gather(x, nw, scale, shift, w_shard, extras, order, tm):
    s = x.shape[0]
    ni = s // tm
    n = 1 + len(extras)
    mid = ni - 2

    def body(order_ref, x_ref, nw_ref, sc_ref, sh_ref, *refs):
        ins, o_ref, ht_ref, outs = refs[:n], refs[n], refs[n + 1], refs[n + 2:2 * n + 2]
        h_all, wbuf, send_sems, recv_sems, local_sems, load_sems = refs[2 * n + 2:]
        jj, i = pl.program_id(0), pl.program_id(1)
        x, y, c = _coords()
        me, sibling = (x, y, c), (x, y, 1 - c)
        chips = _chip_order(x, y, c)
        relayed = [(*chips[1], 1 - c), (*chips[0], 1 - c), (*chips[2], 1 - c)]

        def slot(a, dev):
            return outs[a].at[_dev_index(dev)]

        def copy(a, k, block, to, src=None):
            return pltpu.make_async_remote_copy(
                src_ref=slot(a, block) if src is None else src, dst_ref=slot(a, block),
                send_sem=send_sems.at[a, k], recv_sem=recv_sems.at[a, k], device_id=to, device_id_type=MESH)

        mine = [pltpu.make_async_copy(ins[a], slot(a, me), local_sems.at[a]) for a in range(n)]
        to_sibling = [copy(a, 0, me, sibling, src=ins[a]) for a in range(n)]
        to_chip = [[copy(a, 1 + j, me, (*chips[j], c), src=ins[a]) for a in range(n)] for j in range(2)]
        onward = [copy(a, 3, (*chips[1], c), (*chips[0], c)) for a in range(n)]
        passed = [[copy(a, 4 + j, (*ch, c), sibling) for a in range(n)] for j, ch in enumerate(chips)]
        sends = lambda a: [to_sibling[a], to_chip[0][a], to_chip[1][a], onward[a]] + [passed[j][a] for j in range(3)]

        def arrived(a, j):
            copy(a, 1 + j, (*chips[j], c), me).wait_recv()

        def load(row):
            return pltpu.make_async_copy(outs[0].at[order_ref[row]], wbuf.at[row % 2], load_sems.at[row % 2])

        @pl.when((jj == 0) & (i == 0))
        def _():
            for cp in mine:
                cp.start()
            to_sibling[0].start()
            to_chip[0][0].start()
            pltpu.make_async_copy(ins[0], wbuf.at[0], load_sems.at[0]).start()

        @pl.when((jj == 1) & (i == 0))
        def _():
            to_chip[1][0].start()

        @pl.when((jj == 4) & (i == 0))
        def _():
            for a in range(1, n):
                to_sibling[a].start()
                to_chip[0][a].start()
                to_chip[1][a].start()

        direct = {2: 0, 3: 1, 6: 2}
        relay = {4: 0, 5: 1, 7: 2}

        @pl.when((jj == 0) & (i == mid))
        def _():
            copy(0, 0, sibling, me).wait_recv()

        for row, j in direct.items():
            @pl.when((jj == row - 1) & (i == mid))
            def _(j=j):
                arrived(0, j)
                passed[j][0].start()
                if j == 1:
                    onward[0].start()

        for row, j in relay.items():
            @pl.when((jj == row - 1) & (i == mid))
            def _(j=j):
                copy(0, 4 + j, relayed[j], me).wait_recv()

        @pl.when((jj == NDEV - 1) & (i == 0))
        def _():
            for a in range(1, n):
                arrived(a, 1)
                onward[a].start()
                passed[1][a].start()
                arrived(a, 0)
                passed[0][a].start()

        @pl.when((jj < NDEV - 1) & (i == mid))
        def _():
            load(jj + 1).start()

        @pl.when(i == 0)
        def _():
            load(jj).wait()

        @pl.when(jj == 0)
        def _():
            xf = x_ref[...]
            r = lax.rsqrt(jnp.mean(xf * xf, axis=-1, keepdims=True) + EPS)
            h = (xf * r * nw_ref[...]) * (1.0 + sc_ref[...]) + sh_ref[...]
            h_all[i] = h.astype(BF16)
            ht_ref[...] = h.T.astype(BF16)

        o_ref[...] = jnp.dot(h_all[i], wbuf[jj % 2], preferred_element_type=F32).astype(BF16)

        @pl.when((jj == NDEV - 1) & (i == ni - 1))
        def _():
            for a in range(1, n):
                arrived(a, 2)
                passed[2][a].start()
            for a in range(1, n):
                copy(a, 0, sibling, me).wait_recv()
                for j in range(3):
                    copy(a, 4 + j, relayed[j], me).wait_recv()
            for a in range(n):
                mine[a].wait()
                for cp in sends(a):
                    cp.wait_send()

    any_spec = pl.BlockSpec(memory_space=pl.ANY)
    vec = pl.BlockSpec((1, D), lambda jj, i, o: (0, 0))
    outs = pl.pallas_call(
        body, name="proj_fwd_gather",
        grid_spec=pltpu.PrefetchScalarGridSpec(
            num_scalar_prefetch=1, grid=(NDEV, ni),
            in_specs=[pl.BlockSpec((tm, D), lambda jj, i, o: (jnp.where(jj == 0, i, ni - 1), 0))] + [vec] * 3
                     + [any_spec] * n,
            out_specs=[pl.BlockSpec((tm, SHARD), lambda jj, i, o: (i, o[jj])),
                       pl.BlockSpec((D, tm), lambda jj, i, o: (0, jnp.where(jj == 0, i, ni - 1)))]
                      + [any_spec] * n,
            scratch_shapes=[pltpu.VMEM((ni, tm, D), BF16), pltpu.VMEM((2, D, SHARD), BF16),
                            pltpu.SemaphoreType.DMA((n, 7)), pltpu.SemaphoreType.DMA((n, 7)),
                            pltpu.SemaphoreType.DMA((n,)), pltpu.SemaphoreType.DMA((2,))]),
        out_shape=[jax.ShapeDtypeStruct((s, NIN), BF16), jax.ShapeDtypeStruct((D, s), BF16),
                   jax.ShapeDtypeStruct((NDEV, D, SHARD), BF16)]
                  + [jax.ShapeDtypeStruct((NDEV,) + e.shape, e.dtype) for e in extras],
        compiler_params=_cp(("arbitrary", "arbitrary"), 56))(order, x, nw, scale, shift, w_shard, *extras)
    return outs[0], outs[1], outs[2], outs[3:]


def proj_bwd(ht, dproj, wg, smalls, order, x, dy, nw, scale, tt):
    s = dproj.shape[0]
    nk = s // tt
    n = len(smalls)
    rows_per_step = tt // nk
    last = 2 * NDEV

    def body(order_ref, ht_ref, dp_ref, w_ref, x_ref, dy_ref, nw_ref, sc_ref, *rest):
        small_in = rest[:n]
        gx_ref, st_ref, gw_ref, rwin_ref = rest[n:n + 4]
        small_out = rest[n + 4:2 * n + 4]
        acc, stage, dh, send_sems, recv_sems, local_sems, stage_sems = rest[2 * n + 4:]
        t, k = pl.program_id(0), pl.program_id(1)
        me_xyc = _coords()
        me = _dev_index(me_xyc)
        peers = [_flip(me_xyc, f) for f in FLIPS]

        def exchange(a, kf, src_arr, dst_arr):
            pid = _dev_index(peers[kf])
            mk = lambda dst: pltpu.make_async_remote_copy(
                src_ref=src_arr.at[pid], dst_ref=dst, send_sem=send_sems.at[a, kf], recv_sem=recv_sems.at[a, kf],
                device_id=peers[kf], device_id_type=MESH)
            return mk(dst_arr.at[me]), mk(dst_arr.at[pid])

        small_pairs = [exchange(1 + a, kf, small_in[a], small_out[a]) for kf in range(7) for a in range(n)]
        small_own = [pltpu.make_async_copy(small_in[a].at[me], small_out[a].at[me], local_sems.at[1 + a])
                     for a in range(n)]
        win_pairs = [exchange(0, kf, gw_ref, rwin_ref) for kf in range(7)]
        win_own = pltpu.make_async_copy(gw_ref.at[me], rwin_ref.at[me], local_sems.at[0])

        def to_hbm(jj):
            slab = me if jj == 7 else _dev_index(peers[jj])
            return pltpu.make_async_copy(stage.at[jj % 2], gw_ref.at[slab], stage_sems.at[jj % 2])

        @pl.when((t == 0) & (k == 0))
        def _():
            for cp in small_own:
                cp.start()
            for send, _ in small_pairs:
                send.start()

        @pl.when(t < NDEV)
        def _():
            p = jnp.dot(ht_ref[...], dp_ref[...], preferred_element_type=F32)

            @pl.when(k == 0)
            def _():
                acc[...] = p

            @pl.when(k > 0)
            def _():
                acc[...] += p

        for jj in range(NDEV):
            @pl.when((t == jj) & (k == nk - 1))
            def _(jj=jj):
                stage[jj % 2] = acc[...].astype(BF16)
                to_hbm(jj).start()

            @pl.when((t == jj + 1) & (k == 1))
            def _(jj=jj):
                to_hbm(jj).wait()
                if jj < 7:
                    win_pairs[jj][0].start()
                else:
                    win_own.start()

        def matmul_step():
            p = lax.dot_general(dp_ref[...], w_ref[...], NT, preferred_element_type=F32)
            slot = t % 2
            dh[slot] = jnp.where(k == 0, p, dh[slot] + p)

        def norm_step():
            g = dh.at[(t + 1) % 2][pl.ds(pl.multiple_of(k * rows_per_step, rows_per_step), rows_per_step), :]
            xf = x_ref[...]
            r = lax.rsqrt(jnp.mean(xf * xf, axis=-1, keepdims=True) + EPS)
            xh = xf * r
            dn = g * (1.0 + sc_ref[...])
            dxh = dn * nw_ref[...]
            gx_ref[...] = dy_ref[...] + r * (dxh - xh * jnp.mean(dxh * xh, axis=-1, keepdims=True))
            st_ref[0:1, :] += jnp.sum(g, axis=0, keepdims=True)
            st_ref[1:2, :] += jnp.sum(g * xh * nw_ref[...], axis=0, keepdims=True)
            st_ref[2:3, :] += jnp.sum(dn * xh, axis=0, keepdims=True)

        @pl.when((t == 0) & (k == 0))
        def _():
            st_ref[...] = jnp.zeros_like(st_ref)

        @pl.when(t == NDEV)
        def _():
            matmul_step()

        @pl.when((t > NDEV) & (t < last))
        def _():
            matmul_step()
            norm_step()

        @pl.when(t == last)
        def _():
            norm_step()

        @pl.when((t == last) & (k == nk - 1))
        def _():
            for _, recv in win_pairs + small_pairs:
                recv.wait_recv()
            for send, _ in win_pairs + small_pairs:
                send.wait_send()
            win_own.wait()
            for cp in small_own:
                cp.wait()

    any_spec = pl.BlockSpec(memory_space=pl.ANY)
    first = lambda t: t < NDEV
    slab = lambda t, k: jnp.where(t == last, NDEV - 1, k)
    chunk = pl.BlockSpec((rows_per_step, D), lambda t, k, o: (jnp.maximum((t - NDEV - 1) * nk + k, 0), 0))
    vec = pl.BlockSpec((1, D), lambda t, k, o: (0, 0))
    outs = pl.pallas_call(
        body, name="proj_bwd",
        grid_spec=pltpu.PrefetchScalarGridSpec(
            num_scalar_prefetch=1, grid=(last + 1, nk),
            in_specs=[pl.BlockSpec((D, tt), lambda t, k, o: (0, jnp.where(first(t), k, nk - 1))),
                      pl.BlockSpec((tt, SHARD), lambda t, k, o: (jnp.where(first(t), k, jnp.minimum(t, last - 1) - NDEV),
                                                                 jnp.where(first(t), o[jnp.minimum(t, NDEV - 1)],
                                                                           slab(t, k)))),
                      pl.BlockSpec((None, D, SHARD), lambda t, k, o: (jnp.where(first(t), 0, slab(t, k)), 0, 0)),
                      chunk, chunk, vec, vec]
                     + [any_spec] * n,
            out_specs=[chunk, pl.BlockSpec((8, D), lambda t, k, o: (0, 0))] + [any_spec] * (2 + n),
            scratch_shapes=[pltpu.VMEM((D, SHARD), F32), pltpu.VMEM((2, D, SHARD), BF16),
                            pltpu.VMEM((2, tt, D), F32),
                            pltpu.SemaphoreType.DMA((1 + n, 7)), pltpu.SemaphoreType.DMA((1 + n, 7)),
                            pltpu.SemaphoreType.DMA((1 + n,)), pltpu.SemaphoreType.DMA((2,))]),
        out_shape=[jax.ShapeDtypeStruct((s, D), F32), jax.ShapeDtypeStruct((8, D), F32),
                   jax.ShapeDtypeStruct((NDEV, D, SHARD), BF16), jax.ShapeDtypeStruct((NDEV, D, SHARD), BF16)]
                  + [jax.ShapeDtypeStruct(a.shape, a.dtype) for a in smalls],
        compiler_params=_cp(("arbitrary", "arbitrary"), 56))(order, ht, dproj, wg, x, dy, nw, scale, *smalls)
    return outs[0], outs[1], outs[3], outs[4:]


def matmuls_tn(pairs, name, tk):
    s = pairs[0][0].shape[0]
    nk = s // tk
    n = len(pairs)
    shapes = [(a.shape[1], b.shape[1]) for a, b in pairs]

    def body(*refs):
        ins, outs, accs = refs[:2 * n], refs[2 * n:3 * n], refs[3 * n:]
        k = pl.program_id(0)
        for j in range(n):
            p = lax.dot_general(ins[2 * j][...], ins[2 * j + 1][...], TN, preferred_element_type=F32)
            accs[j][...] = jnp.where(k == 0, p, accs[j][...] + p)

        @pl.when(k == nk - 1)
        def _():
            for j in range(n):
                outs[j][...] = accs[j][...].astype(BF16)

    return pl.pallas_call(
        body, name=name, grid=(nk,),
        in_specs=[pl.BlockSpec((tk, t.shape[1]), lambda k: (k, 0)) for pair in pairs for t in pair],
        out_specs=[pl.BlockSpec(sh, lambda k: (0, 0)) for sh in shapes],
        out_shape=[jax.ShapeDtypeStruct(sh, BF16) for sh in shapes],
        scratch_shapes=[pltpu.VMEM(sh, F32) for sh in shapes],
        compiler_params=_cp(("arbitrary",), 56))(*[t for pair in pairs for t in pair])


def _head_matrices():
    lane = lax.broadcasted_iota(jnp.int32, (CB, CB), 0)
    col = lax.broadcasted_iota(jnp.int32, (CB, CB), 1)
    same = (lane // HD == col // HD).astype(BF16)
    lane_c = lax.broadcasted_iota(jnp.int32, (CB, LANES), 0)
    col_c = lax.broadcasted_iota(jnp.int32, (CB, LANES), 1)
    total = (lane_c // HD == col_c).astype(BF16)
    lane_e = lax.broadcasted_iota(jnp.int32, (LANES, CB), 0)
    col_e = lax.broadcasted_iota(jnp.int32, (LANES, CB), 1)
    expand = (lane_e == col_e // HD).astype(BF16)
    return same, total, expand


def _head_sum(x, m_ref):
    return jnp.dot(x.astype(BF16), m_ref[...], preferred_element_type=F32)


def _dot_hilo(x, m_ref):
    hi = x.astype(BF16)
    lo = (x - hi.astype(F32)).astype(BF16)
    return (jnp.dot(hi, m_ref[...], preferred_element_type=F32)
            + jnp.dot(lo, m_ref[...], preferred_element_type=F32))


def _to_residue_major(val, buf, out_ref, dil):
    rows = out_ref.shape[1]
    for k in range(val.shape[1] // LANES):
        lanes = slice(k * LANES, (k + 1) * LANES)
        buf[k] = val[:, lanes]
        for r in range(dil):
            out_ref[r, :, lanes] = buf.at[k][pl.ds(r, rows, stride=dil), :].astype(out_ref.dtype)


def _from_residue_major(ref, buf, dil):
    if dil == 1:
        return ref[0].astype(F32)
    rows, chunks = ref.shape[1], ref.shape[2] // LANES
    for k in range(chunks):
        for r in range(dil):
            buf.at[k][pl.ds(r, rows, stride=dil), :] = ref[r, :, k * LANES:(k + 1) * LANES].astype(F32)
    return jnp.concatenate([buf[k] for k in range(chunks)], axis=1)


def qkv_prep(proj, qw8, kw8, same, tm):
    s = proj.shape[0]
    items = []
    for g, d in enumerate(DILATIONS):
        items += [(g, "q", CB_Q + g, d), (g, "k", CB_K + g, d)] + ([(g, "v", CB_V + g, d)] if d > 1 else [])
    n = len(items)

    def body(*refs):
        ins, (qw_ref, kw_ref, same_ref), outs, buf = refs[:n], refs[n:n + 3], refs[n + 3:2 * n + 3], refs[-1]
        for idx, (_, kind, _, dil) in enumerate(items):
            val = ins[idx][...].astype(F32)
            if kind != "v":
                r = lax.rsqrt(_head_sum(val * val, same_ref) * (1.0 / HD) + EPS)
                val = val * r * (qw_ref if kind == "q" else kw_ref)[...]
            if dil == 1:
                outs[idx][0] = val.astype(BF16)
            else:
                _to_residue_major(val, buf, outs[idx], dil)

    full = lambda a: pl.BlockSpec(a.shape, lambda i: (0, 0))
    outs = pl.pallas_call(
        body, name="qkv_prep", grid=(s // tm,),
        in_specs=[pl.BlockSpec((tm, CB), lambda i, cb=cb: (i, cb)) for _, _, cb, _ in items]
                 + [full(qw8), full(kw8), full(same)],
        out_specs=[pl.BlockSpec((d, tm // d, CB), lambda i: (0, i, 0)) for _, _, _, d in items],
        out_shape=[jax.ShapeDtypeStruct((d, s // d, CB), BF16) for _, _, _, d in items],
        scratch_shapes=[pltpu.VMEM((CB // LANES, tm, LANES), F32)],
        compiler_params=_cp(("parallel",)))(*([proj] * n), qw8 * (HD ** -0.5), kw8, same)
    srcs = [[None, None, (proj, CB_V + g)] for g in range(len(DILATIONS))]
    for (g, kind, _, _), o in zip(items, outs):
        srcs[g]["qkv".index(kind)] = (o.reshape(s, CB), 0)
    return srcs


def stats_prep(da, lc, dc, tm):
    s = da.shape[0]

    def body(da_ref, lc_ref, dc_ref, *refs):
        outs, buf = list(refs[:-1]), refs[-1]
        for dil in DILATIONS:
            rows = tm // dil
            if dil > 1:
                _to_residue_major(da_ref[...].astype(F32), buf, outs.pop(0), dil)
            for src in (lc_ref, dc_ref):
                dst = outs.pop(0) if dil > 1 else None
                dst_t = outs.pop(0)
                buf[0] = src[...]
                for r in range(dil):
                    piece = buf.at[0][pl.ds(r, rows, stride=dil), :] if dil > 1 else buf[0]
                    if dil > 1:
                        dst[r] = piece
                    dst_t[r] = piece.T[0:NH, :]

    row = lambda w: pl.BlockSpec((tm, w), lambda i: (i, 0))
    out_specs, out_shape = [], []
    for dil in DILATIONS:
        rm = lambda w, dil=dil: (pl.BlockSpec((dil, tm // dil, w), lambda i: (0, i, 0)),
                                 jax.ShapeDtypeStruct((dil, s // dil, w), BF16 if w == CB else F32))
        tr = (pl.BlockSpec((dil, NH, tm // dil), lambda i: (0, 0, i)), jax.ShapeDtypeStruct((dil, NH, s // dil), F32))
        group = ([rm(CB)] if dil > 1 else []) + ([rm(LANES), tr, rm(LANES), tr] if dil > 1 else [tr, tr])
        out_specs += [sp for sp, _ in group]
        out_shape += [sh for _, sh in group]
    outs = list(pl.pallas_call(
        body, name="stats_prep", grid=(s // tm,),
        in_specs=[row(CB), row(LANES), row(LANES)], out_specs=out_specs, out_shape=out_shape,
        scratch_shapes=[pltpu.VMEM((CB // LANES, tm, LANES), F32)],
        compiler_params=_cp(("parallel",)))(da, lc, dc))
    res = []
    for dil in DILATIONS:
        flat_t = lambda t, dil=dil: t.reshape(dil * NH, s // dil)
        if dil == 1:
            lt, dt = outs.pop(0), outs.pop(0)
            res.append((da, lc, dc, flat_t(lt), flat_t(dt)))
        else:
            dap, lcp, lt, dcp, dt = (outs.pop(0) for _ in range(5))
            res.append((dap.reshape(s, CB), lcp.reshape(s, LANES), dcp.reshape(s, LANES), flat_t(lt), flat_t(dt)))
    return res


def qkv_grads_to_dproj(dproj, proj, grads, qw8, kw8, same, tm):
    s = dproj.shape[0]
    ni = s // tm
    flat = [(t.reshape(d, s // d, CB), d, kind, 3 * kind + g)
            for g, d in enumerate(DILATIONS) for kind, t in enumerate(grads[g])]
    nf = len(flat)
    nraw = 2 * len(DILATIONS)

    def body(*refs):
        dp_hbm, raws, ins = refs[nraw + nf + 4], refs[1:1 + nraw], refs[1 + nraw:1 + nraw + nf]
        qw_ref, kw_ref, same_ref = refs[1 + nraw + nf:4 + nraw + nf]
        gw_ref, stage, buf, sems = refs[5 + nraw + nf:]
        i = pl.program_id(0)
        slot = i % 2

        def slab(step, sl):
            return pltpu.make_async_copy(
                stage.at[sl], dp_hbm.at[pl.ds(pl.multiple_of(step * tm, tm), tm), pl.ds(CB_Q * CB, 9 * CB)],
                sems.at[sl])

        @pl.when(i == 0)
        def _():
            gw_ref[...] = jnp.zeros_like(gw_ref)

        @pl.when(i >= 2)
        def _():
            slab(i - 2, slot).wait()

        for ref, (_, d, kind, jj) in zip(ins, flat):
            cols = slice(jj * CB, (jj + 1) * CB)
            dn = _from_residue_major(ref, buf, d)
            if kind == 2:
                stage[slot, :, cols] = dn.astype(BF16)
                continue
            t = raws[jj][...].astype(F32)
            r = lax.rsqrt(_head_sum(t * t, same_ref) * (1.0 / HD) + EPS)
            xh = t * r
            gw_ref[kind:kind + 1, :] += jnp.sum(dn * xh, axis=0, keepdims=True)
            dxh = dn * (qw_ref if kind == 0 else kw_ref)[...]
            mean = _head_sum(dxh * xh, same_ref) * (1.0 / HD)
            stage[slot, :, cols] = (r * (dxh - xh * mean)).astype(BF16)
        slab(i, slot).start()

        @pl.when(i == ni - 1)
        def _():
            slab(i - 1, 1 - slot).wait()
            slab(i, slot).wait()

    full = lambda a: pl.BlockSpec(a.shape, lambda i: (0, 0))
    any_spec = pl.BlockSpec(memory_space=pl.ANY)
    return pl.pallas_call(
        body, name="qkv_grads_to_dproj", grid=(ni,),
        in_specs=[any_spec] + [pl.BlockSpec((tm, CB), lambda i, jb=jb: (i, CB_Q + jb)) for jb in range(nraw)]
                 + [pl.BlockSpec((d, tm // d, CB), lambda i: (0, i, 0)) for _, d, _, _ in flat]
                 + [full(qw8), full(kw8), full(same)],
        out_specs=[any_spec, pl.BlockSpec((8, CB), lambda i: (0, 0))],
        out_shape=[jax.ShapeDtypeStruct((s, NIN), BF16), jax.ShapeDtypeStruct((8, CB), F32)],
        input_output_aliases={0: 0},
        scratch_shapes=[pltpu.VMEM((2, tm, 9 * CB), BF16), pltpu.VMEM((CB // LANES, tm, LANES), F32),
                        pltpu.SemaphoreType.DMA((2,))],
        compiler_params=_cp(("arbitrary",)))(
            dproj, *([proj] * nraw), *[t for t, _, _, _ in flat], qw8, kw8, same)


def _lane_lo():
    return lax.broadcasted_iota(jnp.int32, (1, 2 * HD), 1) < HD


def _stack_heads(t, lo):
    zero = jnp.zeros_like(t)
    return jnp.concatenate([jnp.where(lo, t, zero), jnp.where(lo, zero, t)], axis=0)


def _masks(other_ok):
    qi = lax.broadcasted_iota(jnp.int32, (QB, QB), 0)
    kj = lax.broadcasted_iota(jnp.int32, (QB, QB), 1)
    return (kj >= qi) & other_ok, kj <= qi


FWD_SUB = 4


def attn_fwd(srcs):
    s = srcs[0][0][0].shape[0]
    ng = len(DILATIONS)

    def body(*refs):
        ins, outs, bufs = refs[:5 * ng], refs[5 * ng:7 * ng], refs[7 * ng:]
        step = pl.program_id(0)
        lo = _lane_lo()
        head_lane = lax.broadcasted_iota(jnp.int32, (1, LANES), 1)
        for g, dil in enumerate(DILATIONS):
            nb = s // dil // QB
            q_ref, kp_ref, k_ref, vp_ref, v_ref = ins[5 * g:5 * g + 5]
            (o_ref, l_ref), (kbuf, vbuf) = outs[2 * g:2 * g + 2], bufs[2 * g:2 * g + 2]
            kbuf[0:QB], kbuf[QB:] = kp_ref[...], k_ref[...]
            vbuf[0:QB], vbuf[QB:] = vp_ref[...], v_ref[...]
            for j in range(FWD_SUB):
                rows, krows = slice(j * QB, (j + 1) * QB), slice(j * QB, (j + 2) * QB)
                m_prev, m_cur = _masks((step * FWD_SUB + j) % nb > 0)
                mask = jnp.concatenate([m_prev, m_cur], axis=1)
                mask = jnp.concatenate([mask, mask], axis=0)
                lses = jnp.zeros((QB, LANES), F32)
                for i in range(NH // 2):
                    sl = slice(2 * HD * i, 2 * HD * (i + 1))
                    qs, ks, vv = q_ref[rows, sl], kbuf[krows, sl], vbuf[krows, sl]
                    sc = lax.dot_general(_stack_heads(qs, lo), ks, NT, preferred_element_type=F32)
                    sc = jnp.where(mask, sc, NEG)
                    mx = jnp.max(sc, axis=-1, keepdims=True)
                    p = jnp.exp(sc - mx)
                    den = jnp.sum(p, axis=-1, keepdims=True)
                    o = jnp.dot(p.astype(BF16), vv, preferred_element_type=F32) * (1.0 / den)
                    lse = mx + jnp.log(den)
                    o_ref[rows, sl] = jnp.where(lo, o[:QB], o[QB:]).astype(BF16)
                    lses = jnp.where(head_lane == 2 * i, lse[:QB], jnp.where(head_lane == 2 * i + 1, lse[QB:], lses))
                l_ref[rows, :] = lses

    main = lambda cb, w=CB: pl.BlockSpec((FWD_SUB * QB, w), lambda st: (st, cb))
    prev = lambda cb: pl.BlockSpec((QB, CB), lambda st: (jnp.maximum(FWD_SUB * st - 1, 0), cb))
    in_specs, args = [], []
    for q_src, k_src, v_src in srcs:
        in_specs += [main(q_src[1]), prev(k_src[1]), main(k_src[1]), prev(v_src[1]), main(v_src[1])]
        args += [q_src[0], k_src[0], k_src[0], v_src[0], v_src[0]]
    outs = pl.pallas_call(
        body, name="attn_fwd", grid=(s // (FWD_SUB * QB),),
        in_specs=in_specs, out_specs=[main(0), main(0, LANES)] * ng,
        out_shape=[jax.ShapeDtypeStruct((s, CB), BF16), jax.ShapeDtypeStruct((s, LANES), F32)] * ng,
        scratch_shapes=[pltpu.VMEM(((FWD_SUB + 1) * QB, CB), BF16)] * (2 * ng),
        compiler_params=_cp(("parallel",)))(*args)
    return outs[0::2], outs[1::2]


def attn_bwd(srcs, stats):
    s = srcs[0][0][0].shape[0]
    ng = len(DILATIONS)
    sub = FWD_SUB
    nin, nout, nbuf = 14, 3, 6

    def body(*refs):
        ins, outs, bufs = refs[:nin * ng], refs[nin * ng:(nin + nout) * ng], refs[(nin + nout) * ng:]
        step = pl.program_id(0)
        lo = _lane_lo()
        kj = lax.broadcasted_iota(jnp.int32, (QB, QB), 0)
        qi = lax.broadcasted_iota(jnp.int32, (QB, QB), 1)

        def block(group_refs, nb, j):
            (q_ref, _, k_ref, _, v_ref, _, da_ref, _, lc_ref, dc_ref, _, _, _, _,
             dq_ref, dk_ref, dv_ref, kbuf, vbuf, qbuf, dabuf, lbuf, dbuf) = group_refs
            rows, two = slice(j * QB, (j + 1) * QB), slice(j * QB, (j + 2) * QB)
            place = (step * sub + j) % nb
            m_prev, m_cur = _masks(place > 0)
            qmask = jnp.concatenate([m_prev, m_cur], axis=1)
            qmask = jnp.concatenate([qmask, qmask], axis=0)
            lcols, dcols = lc_ref[rows, :], dc_ref[rows, :]
            kmask = jnp.concatenate([kj <= qi, (kj >= qi) & (place < nb - 1)], axis=1)
            kmask = jnp.concatenate([kmask, kmask], axis=1)
            lrow = jnp.concatenate([lbuf[j], lbuf[j + 1]], axis=1)
            drow = jnp.concatenate([dbuf[j], dbuf[j + 1]], axis=1)
            for i in range(NH // 2):
                sl = slice(2 * HD * i, 2 * HD * (i + 1))
                col_pair = lambda t: jnp.concatenate([t[:, 2 * i:2 * i + 1], t[:, 2 * i + 1:2 * i + 2]], axis=0)
                row_pair = lambda t: jnp.concatenate([t[2 * i:2 * i + 1, :], t[2 * i + 1:2 * i + 2, :]], axis=1)
                ks2, vv2 = kbuf[two, sl], vbuf[two, sl]
                sc = lax.dot_general(_stack_heads(q_ref[rows, sl], lo), ks2, NT, preferred_element_type=F32)
                p = jnp.exp(jnp.where(qmask, sc, NEG) - col_pair(lcols))
                dp = lax.dot_general(_stack_heads(da_ref[rows, sl], lo), vv2, NT, preferred_element_type=F32)
                ds = p * (dp - col_pair(dcols))
                dq = jnp.dot(ds.astype(BF16), ks2, preferred_element_type=F32)
                dq_ref[rows, sl] = (jnp.where(lo, dq[:QB], dq[QB:]) * (HD ** -0.5)).astype(BF16)

                q2, da2 = _stack_heads(qbuf[two, sl], lo), _stack_heads(dabuf[two, sl], lo)
                ks, vv = k_ref[rows, sl], v_ref[rows, sl]
                sct = lax.dot_general(ks, q2, NT, preferred_element_type=F32)
                pt = jnp.exp(jnp.where(kmask, sct, NEG) - row_pair(lrow))
                dpt = lax.dot_general(vv, da2, NT, preferred_element_type=F32)
                dst = pt * (dpt - row_pair(drow))
                dv_ref[rows, sl] = jnp.dot(pt.astype(BF16), da2, preferred_element_type=F32).astype(BF16)
                dk_ref[rows, sl] = jnp.dot(dst.astype(BF16), q2, preferred_element_type=F32).astype(BF16)

        for g, dil in enumerate(DILATIONS):
            group_refs = (ins[nin * g:nin * (g + 1)] + outs[nout * g:nout * (g + 1)] + bufs[nbuf * g:nbuf * (g + 1)])
            (q_ref, kp_ref, k_ref, vp_ref, v_ref, qn_ref, da_ref, dan_ref, _, _, l_ref, ln_ref, d_ref, dn_ref,
             _, _, _, kbuf, vbuf, qbuf, dabuf, lbuf, dbuf) = group_refs
            kbuf[0:QB], kbuf[QB:] = kp_ref[...], k_ref[...]
            vbuf[0:QB], vbuf[QB:] = vp_ref[...], v_ref[...]
            qbuf[0:sub * QB], qbuf[sub * QB:] = q_ref[...], qn_ref[...]
            dabuf[0:sub * QB], dabuf[sub * QB:] = da_ref[...], dan_ref[...]
            for c in range(sub):
                lbuf[c], dbuf[c] = l_ref[:, c * QB:(c + 1) * QB], d_ref[:, c * QB:(c + 1) * QB]
            lbuf[sub], dbuf[sub] = ln_ref[...], dn_ref[...]
            for j in range(sub):
                block(group_refs, s // dil // QB, j)

    last = s // QB - 1
    main = lambda cb, w=CB: pl.BlockSpec((sub * QB, w), lambda st: (st, cb))
    prev = lambda cb: pl.BlockSpec((QB, CB), lambda st: (jnp.maximum(sub * st - 1, 0), cb))
    nxt = lambda cb: pl.BlockSpec((QB, CB), lambda st: (jnp.minimum(sub * (st + 1), last), cb))
    in_specs, args = [], []
    for (q_src, k_src, v_src), (da, lc, dc, lt, dt), dil in zip(srcs, stats, DILATIONS):
        nb = s // dil // QB
        t_main = pl.BlockSpec((NH, sub * QB), lambda st, nb=nb: (sub * st // nb, (sub * st % nb) // sub))
        t_nxt = pl.BlockSpec((NH, QB), lambda st, nb=nb: (sub * st // nb, jnp.minimum(sub * st % nb + sub, nb - 1)))
        in_specs += [main(q_src[1]), prev(k_src[1]), main(k_src[1]), prev(v_src[1]), main(v_src[1]), nxt(q_src[1]),
                     main(0), nxt(0), main(0, LANES), main(0, LANES), t_main, t_nxt, t_main, t_nxt]
        args += [q_src[0], k_src[0], k_src[0], v_src[0], v_src[0], q_src[0], da, da, lc, dc, lt, lt, dt, dt]
    out = jax.ShapeDtypeStruct((s, CB), BF16)
    big = pltpu.VMEM(((sub + 1) * QB, CB), BF16)
    outs = pl.pallas_call(
        body, name="attn_bwd", grid=(s // (sub * QB),),
        in_specs=in_specs, out_specs=[main(0)] * (nout * ng), out_shape=[out] * (nout * ng),
        scratch_shapes=([big] * 4 + [pltpu.VMEM((sub + 1, NH, QB), F32)] * 2) * ng,
        compiler_params=_cp(("parallel",), 56))(*args)
    return [tuple(outs[nout * g:nout * (g + 1)]) for g in range(ng)]


def _conv_taps(u, u_prev, first):
    tm = u.shape[0]
    row = lax.broadcasted_iota(jnp.int32, (tm, 1), 0)
    up = jnp.where(first, 0.0, u_prev)
    u1 = jnp.where(row == 0, up[HALO - 1:HALO, :], pltpu.roll(u, 1, 0))
    u2 = jnp.where(row == 0, up[HALO - 2:HALO - 1, :],
                   jnp.where(row == 1, up[HALO - 1:HALO, :], pltpu.roll(u, 2, 0)))
    return u1, u2


def mid_fwd(proj, o_g, lse_g, conv_w, expand, tm):
    s = proj.shape[0]
    hb = tm // HALO

    def body(ba_ref, ca_ref, xa_ref, za_ref, cah_ref, xah_ref, zb_ref,
             o0, o1, o2, l0, l1, l2, w_ref, exp_ref, ya_ref, yb_ref, at_ref, lc_ref, buf_o, buf_l):
        first = pl.program_id(0) == 0
        u = ca_ref[...].astype(F32) * xa_ref[...].astype(F32)
        u1, u2 = _conv_taps(u, cah_ref[...].astype(F32) * xah_ref[...].astype(F32), first)
        conv = w_ref[0:1, :] * u2 + w_ref[1:2, :] * u1 + w_ref[2:3, :] * u
        ya_ref[...] = (ba_ref[...].astype(F32) * conv * _silu(za_ref[...].astype(F32))).astype(BF16)
        ls = [_from_residue_major(l, buf_l.at[g], d) for g, (l, d) in enumerate(zip((l0, l1, l2), DILATIONS))]
        mx = jnp.maximum(jnp.maximum(ls[0], ls[1]), ls[2])
        es = [jnp.exp(l - mx) for l in ls]
        den = es[0] + es[1] + es[2]
        attn = jnp.zeros((tm, CB), F32)
        for e, o, d in zip(es, (o0, o1, o2), DILATIONS):
            attn = attn + _dot_hilo(e / den, exp_ref) * _from_residue_major(o, buf_o, d)
        at_ref[...] = attn
        lc_ref[...] = mx + jnp.log(den)
        yb_ref[...] = (attn * _silu(zb_ref[...].astype(F32))).astype(BF16)

    col = lambda j: pl.BlockSpec((tm, D), lambda i: (i, j))
    halo = lambda j: pl.BlockSpec((HALO, D), lambda i: (jnp.maximum(i * hb - 1, 0), j))
    loc = lambda w: pl.BlockSpec((tm, w), lambda i: (i, 0))
    rm = lambda w: [pl.BlockSpec((d, tm // d, w), lambda i: (0, i, 0)) for d in DILATIONS]
    rm_view = lambda ts, w: [t.reshape(d, s // d, w) for t, d in zip(ts, DILATIONS)]
    return pl.pallas_call(
        body, name="mid_fwd", grid=(s // tm,),
        in_specs=[col(0), col(1), col(2), col(3), halo(1), halo(2),
                  pl.BlockSpec((tm, CB), lambda i: (i, CB_ZB))] + rm(CB) + rm(LANES)
                 + [pl.BlockSpec((3, D), lambda i: (0, 0)), pl.BlockSpec(expand.shape, lambda i: (0, 0))],
        out_specs=[loc(D), loc(CB), loc(CB), loc(LANES)],
        out_shape=[jax.ShapeDtypeStruct((s, D), BF16), jax.ShapeDtypeStruct((s, CB), BF16),
                   jax.ShapeDtypeStruct((s, CB), F32), jax.ShapeDtypeStruct((s, LANES), F32)],
        scratch_shapes=[pltpu.VMEM((CB // LANES, tm, LANES), F32), pltpu.VMEM((3, 1, tm, LANES), F32)],
        compiler_params=_cp(("parallel",)))(
            proj, proj, proj, proj, proj, proj, proj, *rm_view(o_g, CB), *rm_view(lse_g, LANES), conv_w, expand)


def tail(proj, ya, yb, attn, x, target, gate, pa_w, pb_w, wo_w, total, conv_w, tm):
    s = proj.shape[0]
    ni = s // tm
    hb = tm // HALO
    nlate = NIN - CB_ZB * CB
    nearly = 4 * D

    def body(ya_ref, yb_ref, ga_ref, gb_ref, zb_ref, at_ref, x_ref, t_ref, gate_ref, pa_ref, pb_ref, wo_ref,
             tot_ref, ba_ref, ca_ref, xa_ref, za_ref, cah_ref, xah_ref, cw_ref,
             dp_hbm, dy_ref, da_ref, dc_ref, mg_ref, do_ref, dpa_ref, dpb_ref, st_ref, gwc_ref,
             stage, dconv_next, sems):
        step = pl.program_id(0)
        i = ni - 1 - step
        slot = step % 2

        def slabs(at_step, sl):
            rows = pl.ds(pl.multiple_of((ni - 1 - at_step) * tm, tm), tm)
            return (pltpu.make_async_copy(stage.at[sl, :, 0:nearly], dp_hbm.at[rows, pl.ds(0, nearly)],
                                          sems.at[sl, 0]),
                    pltpu.make_async_copy(stage.at[sl, :, nearly:], dp_hbm.at[rows, pl.ds(CB_ZB * CB, nlate)],
                                          sems.at[sl, 1]))

        @pl.when(step == 0)
        def _():
            st_ref[...] = jnp.zeros_like(st_ref)
            gwc_ref[...] = jnp.zeros_like(gwc_ref)
            dconv_next[...] = jnp.zeros_like(dconv_next)

        @pl.when(step >= 2)
        def _():
            for cp in slabs(step - 2, slot):
                cp.wait()

        gate_v = gate_ref[...]
        pa = jnp.dot(ya_ref[...], pa_ref[...], preferred_element_type=F32)
        pb = jnp.dot(yb_ref[...], pb_ref[...], preferred_element_type=F32)
        sa = jax.nn.sigmoid(ga_ref[...].astype(F32))
        sb = jax.nn.sigmoid(gb_ref[...].astype(F32))
        merged = (sa * pa + sb * pb).astype(BF16)
        mg_ref[...] = merged
        out = jnp.dot(merged, wo_ref[...], preferred_element_type=F32)
        err = x_ref[...] + gate_v * out - t_ref[...]
        dy = err * (1.0 / D)
        dy_ref[...] = dy
        st_ref[0:1, :] += jnp.sum(dy * out, axis=0, keepdims=True)
        st_ref[1:2, :] += jnp.sum(err * err, axis=0, keepdims=True)
        dout = (gate_v * dy).astype(BF16)
        do_ref[...] = dout
        dmg = lax.dot_general(dout, wo_ref[...], NT, preferred_element_type=F32)
        dpa = (dmg * sa).astype(BF16)
        dpb = (dmg * sb).astype(BF16)
        dpa_ref[...] = dpa
        dpb_ref[...] = dpb
        late = nearly
        stage[slot, :, late + CB:late + CB + D] = (dmg * pa * sa * (1.0 - sa)).astype(BF16)
        stage[slot, :, late + CB + D:] = (dmg * pb * sb * (1.0 - sb)).astype(BF16)
        dya = lax.dot_general(dpa, pa_ref[...], NT, preferred_element_type=F32)
        dyb = lax.dot_general(dpb, pb_ref[...], NT, preferred_element_type=F32)
        zb = zb_ref[...].astype(F32)
        sg = jax.nn.sigmoid(zb)
        attn_v = at_ref[...]
        dattn = dyb * (zb * sg)
        da_ref[...] = dattn.astype(BF16)
        stage[slot, :, late:late + CB] = (dyb * attn_v * (sg * (1.0 + zb * (1.0 - sg)))).astype(BF16)
        dc_ref[...] = _dot_hilo(dattn * attn_v, tot_ref)

        ba, ca, xa, za = (t[...].astype(F32) for t in (ba_ref, ca_ref, xa_ref, za_ref))
        u = ca * xa
        u1, u2 = _conv_taps(u, cah_ref[...].astype(F32) * xah_ref[...].astype(F32), i == 0)
        w0, w1, w2 = cw_ref[0:1, :], cw_ref[1:2, :], cw_ref[2:3, :]
        conv = w0 * u2 + w1 * u1 + w2 * u
        sga = jax.nn.sigmoid(za)
        sza = za * sga
        dconv = dya * ba * sza
        dcn = dconv_next[...]
        rowi = lax.broadcasted_iota(jnp.int32, (tm, 1), 0)
        d1 = jnp.where(rowi == tm - 1, dcn[0:1, :], pltpu.roll(dconv, tm - 1, 0))
        d2 = jnp.where(rowi == tm - 2, dcn[0:1, :],
                       jnp.where(rowi == tm - 1, dcn[1:2, :], pltpu.roll(dconv, tm - 2, 0)))
        du = w2 * dconv + w1 * d1 + w0 * d2
        stage[slot, :, 0:D] = (dya * conv * sza).astype(BF16)
        stage[slot, :, D:2 * D] = (du * xa).astype(BF16)
        stage[slot, :, 2 * D:3 * D] = (du * ca).astype(BF16)
        stage[slot, :, 3 * D:4 * D] = (dya * ba * conv * (sga * (1.0 + za * (1.0 - sga)))).astype(BF16)
        gwc_ref[0:1, :] += jnp.sum(dconv * u2, axis=0, keepdims=True)
        gwc_ref[1:2, :] += jnp.sum(dconv * u1, axis=0, keepdims=True)
        gwc_ref[2:3, :] += jnp.sum(dconv * u, axis=0, keepdims=True)
        dconv_next[...] = dconv[0:8, :]

        for cp in slabs(step, slot):
            cp.start()

        @pl.when(step == ni - 1)
        def _():
            for cp in slabs(step - 1, 1 - slot) + slabs(step, slot):
                cp.wait()

    rev = lambda st: ni - 1 - st
    row = lambda w: pl.BlockSpec((tm, w), lambda st: (rev(st), 0))
    pcol = lambda w, jb: pl.BlockSpec((tm, w), lambda st: (rev(st), jb))
    halo = lambda jb: pl.BlockSpec((HALO, D), lambda st: (jnp.maximum(rev(st) * hb - 1, 0), jb))
    const = lambda a: pl.BlockSpec(a.shape, lambda st: (0, 0), pipeline_mode=pl.Buffered(1))
    acc = pl.BlockSpec((8, D), lambda st: (0, 0))
    return pl.pallas_call(
        body, name="tail", grid=(ni,),
        in_specs=[row(D), row(CB), pcol(D, 9), pcol(D, 10), pcol(CB, CB_ZB), row(CB), row(D), row(D),
                  pl.BlockSpec((1, D), lambda st: (0, 0)), const(pa_w), const(pb_w), const(wo_w), const(total),
                  pcol(D, 0), pcol(D, 1), pcol(D, 2), pcol(D, 3), halo(1), halo(2),
                  pl.BlockSpec((3, D), lambda st: (0, 0))],
        out_specs=[pl.BlockSpec(memory_space=pl.ANY),
                   row(D), row(CB), row(LANES), row(D), row(D), row(D), row(D), acc, acc],
        out_shape=[jax.ShapeDtypeStruct((s, NIN), BF16), jax.ShapeDtypeStruct((s, D), F32),
                   jax.ShapeDtypeStruct((s, CB), BF16), jax.ShapeDtypeStruct((s, LANES), F32)]
                  + [jax.ShapeDtypeStruct((s, D), BF16)] * 4 + [jax.ShapeDtypeStruct((8, D), F32)] * 2,
        scratch_shapes=[pltpu.VMEM((2, tm, nearly + nlate), BF16), pltpu.VMEM((8, D), F32),
                        pltpu.SemaphoreType.DMA((2, 2))],
        compiler_params=_cp(("arbitrary",), 60))(
            ya, yb, proj, proj, proj, attn, x, target, gate, pa_w, pb_w, wo_w, total,
            proj, proj, proj, proj, proj, proj, conv_w)


def _local_step(x, target, shift, scale, gate, norm_w, conv_w, qw, kw, w_shard, small_shards, me_xyc):
    qw8, kw8 = jnp.tile(qw, (1, NH)), jnp.tile(kw, (1, NH))
    same, total, expand = _head_matrices()
    proj, ht, wg, (pa_g, pb_g, wo_g) = proj_fwd_gather(
        x, norm_w, scale, shift, w_shard, small_shards, gather_order(me_xyc), 1024)
    pa_w, wo_w = pa_g.reshape(D, D), wo_g.reshape(D, D)
    pb_w = pb_g.transpose(1, 0, 2).reshape(CB, D)
    srcs = qkv_prep(proj, qw8, kw8, same, 512)
    o_g, lse_g = attn_fwd(srcs)
    ya, yb, attn, lc = mid_fwd(proj, o_g, lse_g, conv_w, expand, 512)
    dproj, dy, da, dc, merged, dout, dpa, dpb, st_tail, st_conv = tail(
        proj, ya, yb, attn, x, target, gate, pa_w, pb_w, wo_w, total, conv_w, 256)
    g_wo, g_pa, g_pb = matmuls_tn([(merged, dout), (ya, dpa), (yb, dpb)], "grad_small_weights", 1024)
    grads = attn_bwd(srcs, stats_prep(da, lc, dc, 2048))
    dproj, gw_qk = qkv_grads_to_dproj(dproj, proj, grads, qw8, kw8, same, 512)
    slabs = [g_pa.reshape(NDEV, 128, D), g_pb.reshape(CB, NDEV, 128).transpose(1, 0, 2), g_wo.reshape(NDEV, 128, D)]
    grad_x, st_norm, r_win, (r_pa, r_pb, r_wo) = proj_bwd(
        ht, dproj, wg, slabs, scatter_order(me_xyc), x, dy, norm_w, scale, 1024)
    dmod = jnp.concatenate([st_norm[0:1], st_norm[1:2], st_tail[0:1]], axis=1)
    loss_part = (0.5 / D) * jnp.sum(st_tail[1])
    gw_heads = gw_qk[0:2].reshape(2, NH, HD).sum(axis=1)
    small = dict(dmod=dmod, norm_w=st_norm[2:3], conv_w=st_conv[0:3],
                 q_norm_w=gw_heads[0:1], k_norm_w=gw_heads[1:2], loss=loss_part)
    return grad_x, small, (r_win, r_pa, r_pb, r_wo)


def kernel(x, c, w_ada, b_ada, norm_w, w_in, conv_w, q_norm_w, k_norm_w, w_br_conv, w_br_attn, w_out, loss_target, m_w_ada, m_b_ada, m_norm_w, m_w_in, m_conv_w, m_q_norm_w, m_k_norm_w, m_w_br_conv, m_w_br_attn, m_w_out, v_w_ada, v_b_ada, v_norm_w, v_w_in, v_conv_w, v_q_norm_w, v_k_norm_w, v_w_br_conv, v_w_br_attn, v_w_out):
    me_xyc = (lax.axis_index("x"), lax.axis_index("y"), lax.axis_index("c"))
    me = _dev_index(me_xyc)
    ncol = w_ada.shape[2]

    conv_pad = jnp.zeros((8, 128), F32).at[0:3].set(conv_w[0])
    b_cols = lax.dynamic_slice(b_ada, (0, me * ncol), (1, ncol))
    mod_pieces, c_all, conv_all = ada_fwd(c, conv_pad, w_ada[0], b_cols)
    conv_full = conv_all[:, 0:3].transpose(1, 0, 2).reshape(3, D)
    c_all = c_all.reshape(NDEV, D)
    mod = mod_pieces.reshape(1, 3 * D)
    shift, scale, gate = mod[:, 0:D], mod[:, D:2 * D], mod[:, 2 * D:3 * D]

    grad_x, small, (r_win, r_pa, r_pb, r_wo) = _local_step(
        x[0], loss_target[0], shift, scale, gate, norm_w, conv_full, q_norm_w, k_norm_w,
        w_in[0].astype(BF16), [w_br_conv[0].astype(BF16), w_br_attn[0].astype(BF16), w_out[0].astype(BF16)], me_xyc)

    packed = jnp.concatenate(
        [small["dmod"], small["norm_w"], small["conv_w"].reshape(1, 3 * D), small["q_norm_w"], small["k_norm_w"],
         jnp.full((1, 128), small["loss"], F32)], axis=1)
    packed_all, tot = gather_sum(packed)
    loss = tot[0, 7 * D + 2 * HD]
    dmod_all = packed_all[:, 0, 0:3 * D]
    g_b_ada = tot[:, 0:3 * D]
    g_norm_w = tot[:, 3 * D:4 * D]
    g_conv = lax.dynamic_slice(tot[:, 4 * D:7 * D].reshape(3, D), (0, me * 128), (3, 128))
    g_qn = tot[:, 7 * D:7 * D + HD]
    g_kn = tot[:, 7 * D + HD:7 * D + 2 * HD]
    g_w_ada = ada_bwd(c_all.T, lax.dynamic_slice(dmod_all, (0, me * ncol), (NDEV, ncol)))

    def upd(parts, w, m, v, name, rows):
        shape = w.shape
        w2, m2, v2 = (t.reshape(shape[-2:]) for t in (w, m, v))
        return [t.reshape(shape) for t in adamw(parts, w2, m2, v2, name, rows)]

    res = {"w_in": upd(r_win, w_in, m_w_in, v_w_in, "adamw_w_in", 128)}
    small_params = {"w_ada": (g_w_ada[None], w_ada, m_w_ada, v_w_ada), "b_ada": (g_b_ada[None], b_ada, m_b_ada, v_b_ada),
                    "norm_w": (g_norm_w[None], norm_w, m_norm_w, v_norm_w),
                    "conv_w": (g_conv[None], conv_w, m_conv_w, v_conv_w),
                    "q_norm_w": (g_qn[None], q_norm_w, m_q_norm_w, v_q_norm_w),
                    "k_norm_w": (g_kn[None], k_norm_w, m_k_norm_w, v_k_norm_w),
                    "w_br_conv": (r_pa, w_br_conv, m_w_br_conv, v_w_br_conv),
                    "w_br_attn": (r_pb, w_br_attn, m_w_br_attn, v_w_br_attn),
                    "w_out": (r_wo, w_out, m_w_out, v_w_out)}
    updated = adamw_small([(item[0],) + tuple(t.reshape(t.shape[-2:]) for t in item[1:])
                           for item in small_params.values()])
    for (pname, item), outs4 in zip(small_params.items(), updated):
        res[pname] = [t.reshape(item[1].shape) for t in outs4]
    names = ["w_ada", "b_ada", "norm_w", "w_in", "conv_w", "q_norm_w", "k_norm_w", "w_br_conv", "w_br_attn", "w_out"]
    return (loss, grad_x[None], *[res[n][0] for n in names], *[res[n][1] for n in names],
            *[res[n][2] for n in names], *[res[n][3] for n in names])
```

```python
import jax
import jax.numpy as jnp
from jax import lax
from jax.experimental import pallas as pl
from jax.experimental.pallas import tpu as pltpu

F32, BF16 = jnp.float32, jnp.bfloat16
D = 1024
NIN = 11264
NDEV = 8
SHARD = NIN // NDEV
HD = 64
NH = 8
QB = 128
CB = 512
CB_Q, CB_K, CB_V, CB_ZB = 8, 11, 14, 17
DILATIONS = (1, 4, 16)
EPS = 1e-6
NEG = -1e30
HALO = 16
LANES = 128
MESH = pl.DeviceIdType.MESH

ADAM_LR, ADAM_B1, ADAM_B2, ADAM_EPS, ADAM_WD, ADAM_STEP = 0.001, 0.9, 0.999, 1e-08, 0.01, 10

NT = (((1,), (1,)), ((), ()))
TN = (((0,), (0,)), ((), ()))


def _cp(sem, vmem_mb=48):
    return pltpu.CompilerParams(dimension_semantics=sem, vmem_limit_bytes=vmem_mb << 20)


def _silu(z):
    return z * jax.nn.sigmoid(z)


def _coords():
    return lax.axis_index("x"), lax.axis_index("y"), lax.axis_index("c")


FLIPS = [(fx, fy, fc) for fx in (0, 1) for fy in (0, 1) for fc in (0, 1)][1:]


def gather_sum(vec):
    def body(v_ref, all_ref, sum_ref, send_sems, recv_sems, local_sem):
        me_xyc = _coords()
        me = _dev_index(me_xyc)
        peers = [_flip(me_xyc, f) for f in FLIPS]

        def copy(k, block):
            return pltpu.make_async_remote_copy(
                src_ref=v_ref, dst_ref=all_ref.at[block], send_sem=send_sems.at[k], recv_sem=recv_sems.at[k],
                device_id=peers[k], device_id_type=MESH)

        mine = pltpu.make_async_copy(v_ref, all_ref.at[me], local_sem)
        sends = [copy(k, me) for k in range(7)]
        for cp in [mine] + sends:
            cp.start()
        for k in range(7):
            copy(k, _dev_index(peers[k])).wait_recv()
        mine.wait()
        acc = all_ref[0]
        for b in range(1, NDEV):
            acc = acc + all_ref[b]
        sum_ref[...] = acc
        for cp in sends:
            cp.wait_send()

    return pl.pallas_call(
        body, name="gather_sum",
        out_shape=[jax.ShapeDtypeStruct((NDEV,) + vec.shape, F32), jax.ShapeDtypeStruct(vec.shape, F32)],
        scratch_shapes=[pltpu.SemaphoreType.DMA((7,)), pltpu.SemaphoreType.DMA((7,)), pltpu.SemaphoreType.DMA],
    )(vec)


def _flip(dev, f):
    return tuple(1 - v if b else v for v, b in zip(dev, f))


def _dev_index(dev):
    return 4 * dev[0] + 2 * dev[1] + dev[2]


def _chip_order(x, y, c):
    xor = lambda a, b: a + b - 2 * a * b
    return [(xor(x, 1 - c), xor(y, c)), (xor(x, c), xor(y, 1 - c)), (1 - x, 1 - y)]


def gather_order(me_xyc):
    x, y, c = me_xyc
    chips = _chip_order(x, y, c)
    devs = [(x, y, c), (x, y, 1 - c), (*chips[0], c), (*chips[1], c),
            (*chips[1], 1 - c), (*chips[0], 1 - c), (*chips[2], c), (*chips[2], 1 - c)]
    return jnp.stack([_dev_index(d) for d in devs]).astype(jnp.int32)


def scatter_order(me_xyc):
    devs = [_flip(me_xyc, f) for f in FLIPS] + [me_xyc]
    return jnp.stack([_dev_index(d) for d in devs]).astype(jnp.int32)


def ada_fwd(c, conv_pad, w_ada, b_cols):
    ncol = w_ada.shape[1]

    def body(c_ref, cv_ref, w_ref, b_ref, mod_ref, call_ref, cvall_ref, rows_buf, send_sems, recv_sems, local_sems):
        me_xyc = _coords()
        me = _dev_index(me_xyc)
        peers = [_flip(me_xyc, f) for f in FLIPS]
        pids = [_dev_index(p) for p in peers]

        def copy(a, k, src, dst):
            return pltpu.make_async_remote_copy(src_ref=src, dst_ref=dst, send_sem=send_sems.at[a, k],
                                                recv_sem=recv_sems.at[a, k], device_id=peers[k], device_id_type=MESH)

        own = [pltpu.make_async_copy(c_ref, call_ref.at[me], local_sems.at[0]),
               pltpu.make_async_copy(cv_ref, cvall_ref.at[me], local_sems.at[1])]
        first = [copy(0, k, c_ref, call_ref.at[me]) for k in range(7)]
        first += [copy(1, k, cv_ref, cvall_ref.at[me]) for k in range(7)]
        for cp in own + first:
            cp.start()
        own[0].wait()
        for k in range(7):
            copy(0, k, c_ref, call_ref.at[pids[k]]).wait_recv()
        seq = lax.broadcasted_iota(jnp.int32, (NDEV, 1), 0)
        c_all = jnp.zeros((NDEV, D), F32)
        for p in range(NDEV):
            c_all = jnp.where(seq == p, call_ref[p], c_all)
        mods = jnp.dot(_silu(c_all).astype(BF16), w_ref[...].astype(BF16), preferred_element_type=F32) + b_ref[...]
        for p in range(NDEV):
            rows_buf[p] = mods[p:p + 1, :]
        mine = pltpu.make_async_copy(rows_buf.at[me], mod_ref.at[me], local_sems.at[2])
        second = [copy(2, k, rows_buf.at[pids[k]], mod_ref.at[me]) for k in range(7)]
        for cp in [mine] + second:
            cp.start()
        for k in range(7):
            copy(2, k, rows_buf.at[pids[k]], mod_ref.at[pids[k]]).wait_recv()
            copy(1, k, cv_ref, cvall_ref.at[pids[k]]).wait_recv()
        for cp in first + second:
            cp.wait_send()
        own[1].wait()
        mine.wait()

    return pl.pallas_call(
        body, name="ada_fwd",
        out_shape=[jax.ShapeDtypeStruct((NDEV, 1, ncol), F32), jax.ShapeDtypeStruct((NDEV, 1, D), F32),
                   jax.ShapeDtypeStruct((NDEV,) + conv_pad.shape, F32)],
        scratch_shapes=[pltpu.VMEM((NDEV, 1, ncol), F32), pltpu.SemaphoreType.DMA((3, 7)),
                        pltpu.SemaphoreType.DMA((3, 7)), pltpu.SemaphoreType.DMA((3,))],
    )(c, conv_pad, w_ada, b_cols)


def _adamw_update(g, w_ref, m_ref, v_ref, g_ref, d_ref, nm_ref, nv_ref):
    nm = ADAM_B1 * m_ref[...] + (1.0 - ADAM_B1) * g
    nv = ADAM_B2 * v_ref[...] + (1.0 - ADAM_B2) * (g * g)
    g_ref[...] = g
    nm_ref[...] = nm
    nv_ref[...] = nv
    m_hat = nm / (1.0 - ADAM_B1 ** ADAM_STEP)
    v_hat = nv / (1.0 - ADAM_B2 ** ADAM_STEP)
    d_ref[...] = -ADAM_LR * (m_hat / (jnp.sqrt(v_hat) + ADAM_EPS) + ADAM_WD * w_ref[...])


def adamw_small(items, ada):
    n = len(items)

    def body(c_ref, d_ref, wa_ref, ma_ref, va_ref, *refs):
        ins, outs = refs[:4 * n], refs[4 * n:]
        at = _silu(c_ref[...])
        g = at[:, 0:1] * d_ref[0:1, :]
        for b in range(1, NDEV):
            g = g + at[:, b:b + 1] * d_ref[b:b + 1, :]
        _adamw_update(g, wa_ref, ma_ref, va_ref, *outs[0:4])
        for a in range(n):
            p_ref, w_ref, m_ref, v_ref = ins[4 * a:4 * a + 4]
            g = p_ref[0].astype(F32)
            for b in range(1, p_ref.shape[0]):
                g = g + p_ref[b].astype(F32)
            _adamw_update(g, w_ref, m_ref, v_ref, *outs[4 * a + 4:4 * a + 8])

    out = pl.pallas_call(
        body, name="adamw_small",
        out_shape=[jax.ShapeDtypeStruct(ada[2].shape, F32)] * 4
                  + [jax.ShapeDtypeStruct(it[1].shape, F32) for it in items for _ in range(4)],
        compiler_params=pltpu.CompilerParams(vmem_limit_bytes=48 << 20))(*ada, *[t for it in items for t in it])
    return [out[4 * a:4 * a + 4] for a in range(n + 1)]


def adamw(parts, w, m, v, name, rows):
    n, r, ccols = parts.shape

    def body(p_ref, w_ref, m_ref, v_ref, g_ref, d_ref, nm_ref, nv_ref):
        g = p_ref[0].astype(F32)
        for b in range(1, n):
            g = g + p_ref[b].astype(F32)
        _adamw_update(g, w_ref, m_ref, v_ref, g_ref, d_ref, nm_ref, nv_ref)

    blk = pl.BlockSpec((rows, ccols), lambda i: (i, 0))
    out = jax.ShapeDtypeStruct((r, ccols), F32)
    return pl.pallas_call(
        body, name=name, grid=(r // rows,),
        in_specs=[pl.BlockSpec((n, rows, ccols), lambda i: (0, i, 0)), blk, blk, blk],
        out_specs=[blk] * 4, out_shape=[out] * 4, compiler_params=_cp(("parallel",)))(parts, w, m, v)


def proj_fwd_gather(x, nw, scale, shift, w_shard, extras, order, tm):
    s = x.shape[0]
    ni = s // tm
    n = 1 + len(extras)
    mid = ni - 2

    def body(order_ref, x_ref, nw_ref, sc_ref, sh_ref, *refs):
        ins, o_ref, ht_ref, outs = refs[:n], refs[n], refs[n + 1], refs[n + 2:2 * n + 2]
        h_all, wbuf, send_sems, recv_sems, local_sems, load_sems = refs[2 * n + 2:]
        jj, i = pl.program_id(0), pl.program_id(1)
        x, y, c = _coords()
        me, sibling = (x, y, c), (x, y, 1 - c)
        chips = _chip_order(x, y, c)
        relayed = [(*chips[1], 1 - c), (*chips[0], 1 - c), (*chips[2], 1 - c)]

        def slot(a, dev):
            return outs[a].at[_dev_index(dev)]

        def copy(a, k, block, to, src=None):
            return pltpu.make_async_remote_copy(
                src_ref=slot(a, block) if src is None else src, dst_ref=slot(a, block),
                send_sem=send_sems.at[a, k], recv_sem=recv_sems.at[a, k], device_id=to, device_id_type=MESH)

        mine = [pltpu.make_async_copy(ins[a], slot(a, me), local_sems.at[a]) for a in range(n)]
        to_sibling = [copy(a, 0, me, sibling, src=ins[a]) for a in range(n)]
        to_chip = [[copy(a, 1 + j, me, (*chips[j], c), src=ins[a]) for a in range(n)] for j in range(2)]
        onward = [copy(a, 3, (*chips[1], c), (*chips[0], c)) for a in range(n)]
        passed = [[copy(a, 4 + j, (*ch, c), sibling) for a in range(n)] for j, ch in enumerate(chips)]
        sends = lambda a: [to_sibling[a], to_chip[0][a], to_chip[1][a], onward[a]] + [passed[j][a] for j in range(3)]

        def arrived(a, j):
            copy(a, 1 + j, (*chips[j], c), me).wait_recv()

        def load(row):
            return pltpu.make_async_copy(outs[0].at[order_ref[row]], wbuf.at[row % 2], load_sems.at[row % 2])

        @pl.when((jj == 0) & (i == 0))
        def _():
            for cp in mine:
                cp.start()
            to_sibling[0].start()
            to_chip[0][0].start()
            pltpu.make_async_copy(ins[0], wbuf.at[0], load_sems.at[0]).start()

        @pl.when((jj == 1) & (i == 0))
        def _():
            to_chip[1][0].start()

        @pl.when((jj == 4) & (i == 0))
        def _():
            for a in range(1, n):
                to_sibling[a].start()
                to_chip[0][a].start()
                to_chip[1][a].start()

        direct = {2: 0, 3: 1, 6: 2}
        relay = {4: 0, 5: 1, 7: 2}

        @pl.when((jj == 0) & (i == mid))
        def _():
            copy(0, 0, sibling, me).wait_recv()

        for row, j in direct.items():
            @pl.when((jj == row - 1) & (i == mid))
            def _(j=j):
                arrived(0, j)
                passed[j][0].start()
                if j == 1:
                    onward[0].start()

        for row, j in relay.items():
            @pl.when((jj == row - 1) & (i == mid))
            def _(j=j):
                copy(0, 4 + j, relayed[j], me).wait_recv()

        @pl.when((jj == NDEV - 1) & (i == 0))
        def _():
            for a in range(1, n):
                arrived(a, 1)
                onward[a].start()
                passed[1][a].start()
                arrived(a, 0)
                passed[0][a].start()

        @pl.when((jj < NDEV - 1) & (i == mid))
        def _():
            load(jj + 1).start()

        @pl.when(i == 0)
        def _():
            load(jj).wait()

        @pl.when(jj == 0)
        def _():
            xf = x_ref[...]
            r = lax.rsqrt(jnp.mean(xf * xf, axis=-1, keepdims=True) + EPS)
            h = (xf * r * nw_ref[...]) * (1.0 + sc_ref[...]) + sh_ref[...]
            h_all[i] = h.astype(BF16)
            ht_ref[...] = h.T.astype(BF16)

        o_ref[...] = jnp.dot(h_all[i], wbuf[jj % 2], preferred_element_type=F32).astype(BF16)

        @pl.when((jj == NDEV - 1) & (i == ni - 1))
        def _():
            for a in range(1, n):
                arrived(a, 2)
                passed[2][a].start()
            for a in range(1, n):
                copy(a, 0, sibling, me).wait_recv()
                for j in range(3):
                    copy(a, 4 + j, relayed[j], me).wait_recv()
            for a in range(n):
                mine[a].wait()
                for cp in sends(a):
                    cp.wait_send()

    any_spec = pl.BlockSpec(memory_space=pl.ANY)
    vec = pl.BlockSpec((1, D), lambda jj, i, o: (0, 0))
    outs = pl.pallas_call(
        body, name="proj_fwd_gather",
        grid_spec=pltpu.PrefetchScalarGridSpec(
            num_scalar_prefetch=1, grid=(NDEV, ni),
            in_specs=[pl.BlockSpec((tm, D), lambda jj, i, o: (jnp.where(jj == 0, i, ni - 1), 0))] + [vec] * 3
                     + [any_spec] * n,
            out_specs=[pl.BlockSpec((tm, SHARD), lambda jj, i, o: (i, o[jj])),
                       pl.BlockSpec((D, tm), lambda jj, i, o: (0, jnp.where(jj == 0, i, ni - 1)))]
                      + [any_spec] * n,
            scratch_shapes=[pltpu.VMEM((ni, tm, D), BF16), pltpu.VMEM((2, D, SHARD), BF16),
                            pltpu.SemaphoreType.DMA((n, 7)), pltpu.SemaphoreType.DMA((n, 7)),
                            pltpu.SemaphoreType.DMA((n,)), pltpu.SemaphoreType.DMA((2,))]),
        out_shape=[jax.ShapeDtypeStruct((s, NIN), BF16), jax.ShapeDtypeStruct((D, s), BF16),
                   jax.ShapeDtypeStruct((NDEV, D, SHARD), BF16)]
                  + [jax.ShapeDtypeStruct((NDEV,) + e.shape, e.dtype) for e in extras],
        compiler_params=_cp(("arbitrary", "arbitrary"), 56))(order, x, nw, scale, shift, w_shard, *extras)
    return outs[0], outs[1], outs[2], outs[3:]


def proj_bwd(ht, dproj, wg, smalls, order, x, dy, nw, scale, tt):
    s = dproj.shape[0]
    nk = s // tt
    n = len(smalls)
    rows_per_step = tt // nk
    last = 2 * NDEV

    def body(order_ref, ht_ref, dp_ref, w_ref, x_ref, dy_ref, nw_ref, sc_ref, *rest):
        small_in = rest[:n]
        gx_ref, st_ref, gw_ref, rwin_ref = rest[n:n + 4]
        small_out = rest[n + 4:2 * n + 4]
        acc, stage, dh, send_sems, recv_sems, local_sems, stage_sems = rest[2 * n + 4:]
        t, k = pl.program_id(0), pl.program_id(1)
        me_xyc = _coords()
        me = _dev_index(me_xyc)
        peers = [_flip(me_xyc, f) for f in FLIPS]

        def exchange(a, kf, src_arr, dst_arr):
            pid = _dev_index(peers[kf])
            mk = lambda dst: pltpu.make_async_remote_copy(
                src_ref=src_arr.at[pid], dst_ref=dst, send_sem=send_sems.at[a, kf], recv_sem=recv_sems.at[a, kf],
                device_id=peers[kf], device_id_type=MESH)
            return mk(dst_arr.at[me]), mk(dst_arr.at[pid])

        small_pairs = [exchange(1 + a, kf, small_in[a], small_out[a]) for kf in range(7) for a in range(n)]
        small_own = [pltpu.make_async_copy(small_in[a].at[me], small_out[a].at[me], local_sems.at[1 + a])
                     for a in range(n)]
        win_pairs = [exchange(0, kf, gw_ref, rwin_ref) for kf in range(7)]
        win_own = pltpu.make_async_copy(gw_ref.at[me], rwin_ref.at[me], local_sems.at[0])

        def to_hbm(jj):
            slab = me if jj == 7 else _dev_index(peers[jj])
            return pltpu.make_async_copy(stage.at[jj % 2], gw_ref.at[slab], stage_sems.at[jj % 2])

        @pl.when((t == 0) & (k == 0))
        def _():
            for cp in small_own:
                cp.start()
            for send, _ in small_pairs:
                send.start()

        @pl.when(t < NDEV)
        def _():
            p = jnp.dot(ht_ref[...], dp_ref[...], preferred_element_type=F32)

            @pl.when(k == 0)
            def _():
                acc[...] = p

            @pl.when(k > 0)
            def _():
                acc[...] += p

        for jj in range(NDEV):
            @pl.when((t == jj) & (k == nk - 1))
            def _(jj=jj):
                stage[jj % 2] = acc[...].astype(BF16)
                to_hbm(jj).start()

            @pl.when((t == jj + 1) & (k == 1))
            def _(jj=jj):
                to_hbm(jj).wait()
                if jj < 7:
                    win_pairs[jj][0].start()
                else:
                    win_own.start()

        def matmul_step():
            p = lax.dot_general(dp_ref[...], w_ref[...], NT, preferred_element_type=F32)
            slot = t % 2
            dh[slot] = jnp.where(k == 0, p, dh[slot] + p)

        def norm_step():
            g = dh.at[(t + 1) % 2][pl.ds(pl.multiple_of(k * rows_per_step, rows_per_step), rows_per_step), :]
            xf = x_ref[...]
            r = lax.rsqrt(jnp.mean(xf * xf, axis=-1, keepdims=True) + EPS)
            xh = xf * r
            dn = g * (1.0 + sc_ref[...])
            dxh = dn * nw_ref[...]
            gx_ref[...] = dy_ref[...] + r * (dxh - xh * jnp.mean(dxh * xh, axis=-1, keepdims=True))
            st_ref[0:1, :] += jnp.sum(g, axis=0, keepdims=True)
            st_ref[1:2, :] += jnp.sum(g * xh * nw_ref[...], axis=0, keepdims=True)
            st_ref[2:3, :] += jnp.sum(dn * xh, axis=0, keepdims=True)

        @pl.when((t == 0) & (k == 0))
        def _():
            st_ref[...] = jnp.zeros_like(st_ref)

        @pl.when(t == NDEV)
        def _():
            matmul_step()

        @pl.when((t > NDEV) & (t < last))
        def _():
            matmul_step()
            norm_step()

        @pl.when(t == last)
        def _():
            norm_step()

        @pl.when((t == last) & (k == nk - 1))
        def _():
            for _, recv in win_pairs + small_pairs:
                recv.wait_recv()
            for send, _ in win_pairs + small_pairs:
                send.wait_send()
            win_own.wait()
            for cp in small_own:
                cp.wait()

    any_spec = pl.BlockSpec(memory_space=pl.ANY)
    first = lambda t: t < NDEV
    slab = lambda t, k: jnp.where(t == last, NDEV - 1, k)
    chunk = pl.BlockSpec((rows_per_step, D), lambda t, k, o: (jnp.maximum((t - NDEV - 1) * nk + k, 0), 0))
    vec = pl.BlockSpec((1, D), lambda t, k, o: (0, 0))
    outs = pl.pallas_call(
        body, name="proj_bwd",
        grid_spec=pltpu.PrefetchScalarGridSpec(
            num_scalar_prefetch=1, grid=(last + 1, nk),
            in_specs=[pl.BlockSpec((D, tt), lambda t, k, o: (0, jnp.where(first(t), k, nk - 1))),
                      pl.BlockSpec((tt, SHARD), lambda t, k, o: (jnp.where(first(t), k, jnp.minimum(t, last - 1) - NDEV),
                                                                 jnp.where(first(t), o[jnp.minimum(t, NDEV - 1)],
                                                                           slab(t, k)))),
                      pl.BlockSpec((None, D, SHARD), lambda t, k, o: (jnp.where(first(t), 0, slab(t, k)), 0, 0)),
                      chunk, chunk, vec, vec]
                     + [any_spec] * n,
            out_specs=[chunk, pl.BlockSpec((8, D), lambda t, k, o: (0, 0))] + [any_spec] * (2 + n),
            scratch_shapes=[pltpu.VMEM((D, SHARD), F32), pltpu.VMEM((2, D, SHARD), BF16),
                            pltpu.VMEM((2, tt, D), F32),
                            pltpu.SemaphoreType.DMA((1 + n, 7)), pltpu.SemaphoreType.DMA((1 + n, 7)),
                            pltpu.SemaphoreType.DMA((1 + n,)), pltpu.SemaphoreType.DMA((2,))]),
        out_shape=[jax.ShapeDtypeStruct((s, D), F32), jax.ShapeDtypeStruct((8, D), F32),
                   jax.ShapeDtypeStruct((NDEV, D, SHARD), BF16), jax.ShapeDtypeStruct((NDEV, D, SHARD), BF16)]
                  + [jax.ShapeDtypeStruct(a.shape, a.dtype) for a in smalls],
        compiler_params=_cp(("arbitrary", "arbitrary"), 56))(order, ht, dproj, wg, x, dy, nw, scale, *smalls)
    return outs[0], outs[1], outs[3], outs[4:]


def matmuls_tn(pairs, name, tk):
    s = pairs[0][0].shape[0]
    nk = s // tk
    n = len(pairs)
    shapes = [(a.shape[1], b.shape[1]) for a, b in pairs]

    def body(*refs):
        ins, outs, accs = refs[:2 * n], refs[2 * n:3 * n], refs[3 * n:]
        k = pl.program_id(0)
        for j in range(n):
            p = lax.dot_general(ins[2 * j][...], ins[2 * j + 1][...], TN, preferred_element_type=F32)
            accs[j][...] = jnp.where(k == 0, p, accs[j][...] + p)

        @pl.when(k == nk - 1)
        def _():
            for j in range(n):
                outs[j][...] = accs[j][...].astype(BF16)

    return pl.pallas_call(
        body, name=name, grid=(nk,),
        in_specs=[pl.BlockSpec((tk, t.shape[1]), lambda k: (k, 0)) for pair in pairs for t in pair],
        out_specs=[pl.BlockSpec(sh, lambda k: (0, 0)) for sh in shapes],
        out_shape=[jax.ShapeDtypeStruct(sh, BF16) for sh in shapes],
        scratch_shapes=[pltpu.VMEM(sh, F32) for sh in shapes],
        compiler_params=_cp(("arbitrary",), 56))(*[t for pair in pairs for t in pair])


def _head_matrices():
    lane = lax.broadcasted_iota(jnp.int32, (CB, CB), 0)
    col = lax.broadcasted_iota(jnp.int32, (CB, CB), 1)
    same = (lane // HD == col // HD).astype(BF16)
    lane_c = lax.broadcasted_iota(jnp.int32, (CB, LANES), 0)
    col_c = lax.broadcasted_iota(jnp.int32, (CB, LANES), 1)
    total = (lane_c // HD == col_c).astype(BF16)
    lane_e = lax.broadcasted_iota(jnp.int32, (LANES, CB), 0)
    col_e = lax.broadcasted_iota(jnp.int32, (LANES, CB), 1)
    expand = (lane_e == col_e // HD).astype(BF16)
    return same, total, expand


def _head_sum(x, m_ref):
    return jnp.dot(x.astype(BF16), m_ref[...], preferred_element_type=F32)


def _dot_hilo(x, m_ref):
    hi = x.astype(BF16)
    lo = (x - hi.astype(F32)).astype(BF16)
    return (jnp.dot(hi, m_ref[...], preferred_element_type=F32)
            + jnp.dot(lo, m_ref[...], preferred_element_type=F32))


def _to_residue_major(val, buf, out_ref, dil):
    rows = out_ref.shape[1]
    for k in range(val.shape[1] // LANES):
        lanes = slice(k * LANES, (k + 1) * LANES)
        buf[k] = val[:, lanes]
        for r in range(dil):
            out_ref[r, :, lanes] = buf.at[k][pl.ds(r, rows, stride=dil), :].astype(out_ref.dtype)


def _from_residue_major(ref, buf, dil):
    if dil == 1:
        return ref[0].astype(F32)
    rows, chunks = ref.shape[1], ref.shape[2] // LANES
    for k in range(chunks):
        for r in range(dil):
            buf.at[k][pl.ds(r, rows, stride=dil), :] = ref[r, :, k * LANES:(k + 1) * LANES].astype(F32)
    return jnp.concatenate([buf[k] for k in range(chunks)], axis=1)


def qkv_prep(proj, qw8, kw8, same, tm):
    s = proj.shape[0]
    items = []
    for g, d in enumerate(DILATIONS):
        items += [(g, "q", CB_Q + g, d), (g, "k", CB_K + g, d)] + ([(g, "v", CB_V + g, d)] if d > 1 else [])
    n = len(items)

    def body(*refs):
        ins, (qw_ref, kw_ref, same_ref), outs, buf = refs[:n], refs[n:n + 3], refs[n + 3:2 * n + 3], refs[-1]
        for idx, (_, kind, _, dil) in enumerate(items):
            val = ins[idx][...].astype(F32)
            if kind != "v":
                r = lax.rsqrt(_head_sum(val * val, same_ref) * (1.0 / HD) + EPS)
                val = val * r * (qw_ref if kind == "q" else kw_ref)[...]
            if dil == 1:
                outs[idx][0] = val.astype(BF16)
            else:
                _to_residue_major(val, buf, outs[idx], dil)

    full = lambda a: pl.BlockSpec(a.shape, lambda i: (0, 0))
    outs = pl.pallas_call(
        body, name="qkv_prep", grid=(s // tm,),
        in_specs=[pl.BlockSpec((tm, CB), lambda i, cb=cb: (i, cb)) for _, _, cb, _ in items]
                 + [full(qw8), full(kw8), full(same)],
        out_specs=[pl.BlockSpec((d, tm // d, CB), lambda i: (0, i, 0)) for _, _, _, d in items],
        out_shape=[jax.ShapeDtypeStruct((d, s // d, CB), BF16) for _, _, _, d in items],
        scratch_shapes=[pltpu.VMEM((CB // LANES, tm, LANES), F32)],
        compiler_params=_cp(("parallel",)))(*([proj] * n), qw8 * (HD ** -0.5), kw8, same)
    srcs = [[None, None, (proj, CB_V + g)] for g in range(len(DILATIONS))]
    for (g, kind, _, _), o in zip(items, outs):
        srcs[g]["qkv".index(kind)] = (o.reshape(s, CB), 0)
    return srcs


def stats_prep(da, lc, dc, tm):
    s = da.shape[0]

    def body(da_ref, lc_ref, dc_ref, *refs):
        outs, buf = list(refs[:-1]), refs[-1]
        for dil in DILATIONS:
            rows = tm // dil
            if dil > 1:
                _to_residue_major(da_ref[...].astype(F32), buf, outs.pop(0), dil)
            for src in (lc_ref, dc_ref):
                dst = outs.pop(0) if dil > 1 else None
                dst_t = outs.pop(0)
                buf[0] = src[...]
                for r in range(dil):
                    piece = buf.at[0][pl.ds(r, rows, stride=dil), :] if dil > 1 else buf[0]
                    if dil > 1:
                        dst[r] = piece
                    dst_t[r] = piece.T[0:NH, :]

    row = lambda w: pl.BlockSpec((tm, w), lambda i: (i, 0))
    out_specs, out_shape = [], []
    for dil in DILATIONS:
        rm = lambda w, dil=dil: (pl.BlockSpec((dil, tm // dil, w), lambda i: (0, i, 0)),
                                 jax.ShapeDtypeStruct((dil, s // dil, w), BF16 if w == CB else F32))
        tr = (pl.BlockSpec((dil, NH, tm // dil), lambda i: (0, 0, i)), jax.ShapeDtypeStruct((dil, NH, s // dil), F32))
        group = ([rm(CB)] if dil > 1 else []) + ([rm(LANES), tr, rm(LANES), tr] if dil > 1 else [tr, tr])
        out_specs += [sp for sp, _ in group]
        out_shape += [sh for _, sh in group]
    outs = list(pl.pallas_call(
        body, name="stats_prep", grid=(s // tm,),
        in_specs=[row(CB), row(LANES), row(LANES)], out_specs=out_specs, out_shape=out_shape,
        scratch_shapes=[pltpu.VMEM((CB // LANES, tm, LANES), F32)],
        compiler_params=_cp(("parallel",)))(da, lc, dc))
    res = []
    for dil in DILATIONS:
        flat_t = lambda t, dil=dil: t.reshape(dil * NH, s // dil)
        if dil == 1:
            lt, dt = outs.pop(0), outs.pop(0)
            res.append((da, lc, dc, flat_t(lt), flat_t(dt)))
        else:
            dap, lcp, lt, dcp, dt = (outs.pop(0) for _ in range(5))
            res.append((dap.reshape(s, CB), lcp.reshape(s, LANES), dcp.reshape(s, LANES), flat_t(lt), flat_t(dt)))
    return res


def qkv_grads_to_dproj(dproj, proj, grads, qw8, kw8, same, tm):
    s = dproj.shape[0]
    ni = s // tm
    flat = [(t.reshape(d, s // d, CB), d, kind, 3 * kind + g)
            for g, d in enumerate(DILATIONS) for kind, t in enumerate(grads[g])]
    nf = len(flat)
    nraw = 2 * len(DILATIONS)

    def body(*refs):
        dp_hbm, raws, ins = refs[nraw + nf + 4], refs[1:1 + nraw], refs[1 + nraw:1 + nraw + nf]
        qw_ref, kw_ref, same_ref = refs[1 + nraw + nf:4 + nraw + nf]
        gw_ref, stage, buf, sems = refs[5 + nraw + nf:]
        i = pl.program_id(0)
        slot = i % 2

        def slab(step, sl):
            return pltpu.make_async_copy(
                stage.at[sl], dp_hbm.at[pl.ds(pl.multiple_of(step * tm, tm), tm), pl.ds(CB_Q * CB, 9 * CB)],
                sems.at[sl])

        @pl.when(i == 0)
        def _():
            gw_ref[...] = jnp.zeros_like(gw_ref)

        @pl.when(i >= 2)
        def _():
            slab(i - 2, slot).wait()

        for ref, (_, d, kind, jj) in zip(ins, flat):
            cols = slice(jj * CB, (jj + 1) * CB)
            dn = _from_residue_major(ref, buf, d)
            if kind == 2:
                stage[slot, :, cols] = dn.astype(BF16)
                continue
            t = raws[jj][...].astype(F32)
            r = lax.rsqrt(_head_sum(t * t, same_ref) * (1.0 / HD) + EPS)
            xh = t * r
            gw_ref[kind:kind + 1, :] += jnp.sum(dn * xh, axis=0, keepdims=True)
            dxh = dn * (qw_ref if kind == 0 else kw_ref)[...]
            mean = _head_sum(dxh * xh, same_ref) * (1.0 / HD)
            stage[slot, :, cols] = (r * (dxh - xh * mean)).astype(BF16)
        slab(i, slot).start()

        @pl.when(i == ni - 1)
        def _():
            slab(i - 1, 1 - slot).wait()
            slab(i, slot).wait()

    full = lambda a: pl.BlockSpec(a.shape, lambda i: (0, 0))
    any_spec = pl.BlockSpec(memory_space=pl.ANY)
    return pl.pallas_call(
        body, name="qkv_grads_to_dproj", grid=(ni,),
        in_specs=[any_spec] + [pl.BlockSpec((tm, CB), lambda i, jb=jb: (i, CB_Q + jb)) for jb in range(nraw)]
                 + [pl.BlockSpec((d, tm // d, CB), lambda i: (0, i, 0)) for _, d, _, _ in flat]
                 + [full(qw8), full(kw8), full(same)],
        out_specs=[any_spec, pl.BlockSpec((8, CB), lambda i: (0, 0))],
        out_shape=[jax.ShapeDtypeStruct((s, NIN), BF16), jax.ShapeDtypeStruct((8, CB), F32)],
        input_output_aliases={0: 0},
        scratch_shapes=[pltpu.VMEM((2, tm, 9 * CB), BF16), pltpu.VMEM((CB // LANES, tm, LANES), F32),
                        pltpu.SemaphoreType.DMA((2,))],
        compiler_params=_cp(("arbitrary",)))(
            dproj, *([proj] * nraw), *[t for t, _, _, _ in flat], qw8, kw8, same)


def _lane_lo():
    return lax.broadcasted_iota(jnp.int32, (1, 2 * HD), 1) < HD


def _stack_heads(t, lo):
    zero = jnp.zeros_like(t)
    return jnp.concatenate([jnp.where(lo, t, zero), jnp.where(lo, zero, t)], axis=0)


def _masks(other_ok):
    qi = lax.broadcasted_iota(jnp.int32, (QB, QB), 0)
    kj = lax.broadcasted_iota(jnp.int32, (QB, QB), 1)
    return (kj >= qi) & other_ok, kj <= qi


ATTN_SUB = 4


def attn_fwd(srcs):
    s = srcs[0][0][0].shape[0]
    ng = len(DILATIONS)

    def body(*refs):
        ins, outs, bufs = refs[:5 * ng], refs[5 * ng:7 * ng], refs[7 * ng:]
        step = pl.program_id(0)
        lo = _lane_lo()
        head_lane = lax.broadcasted_iota(jnp.int32, (1, LANES), 1)
        for g, dil in enumerate(DILATIONS):
            nb = s // dil // QB
            q_ref, kp_ref, k_ref, vp_ref, v_ref = ins[5 * g:5 * g + 5]
            (o_ref, l_ref), (kbuf, vbuf) = outs[2 * g:2 * g + 2], bufs[2 * g:2 * g + 2]
            kbuf[0:QB], kbuf[QB:] = kp_ref[...], k_ref[...]
            vbuf[0:QB], vbuf[QB:] = vp_ref[...], v_ref[...]
            for j in range(ATTN_SUB):
                rows, krows = slice(j * QB, (j + 1) * QB), slice(j * QB, (j + 2) * QB)
                m_prev, m_cur = _masks((step * ATTN_SUB + j) % nb > 0)
                mask = jnp.concatenate([m_prev, m_cur], axis=1)
                mask = jnp.concatenate([mask, mask], axis=0)
                lses = jnp.zeros((QB, LANES), F32)
                for i in range(NH // 2):
                    sl = slice(2 * HD * i, 2 * HD * (i + 1))
                    qs, ks, vv = q_ref[rows, sl], kbuf[krows, sl], vbuf[krows, sl]
                    sc = lax.dot_general(_stack_heads(qs, lo), ks, NT, preferred_element_type=F32)
                    sc = jnp.where(mask, sc, NEG)
                    mx = jnp.max(sc, axis=-1, keepdims=True)
                    p = jnp.exp(sc - mx)
                    den = jnp.sum(p, axis=-1, keepdims=True)
                    o = jnp.dot(p.astype(BF16), vv, preferred_element_type=F32) * (1.0 / den)
                    lse = mx + jnp.log(den)
                    o_ref[rows, sl] = jnp.where(lo, o[:QB], o[QB:]).astype(BF16)
                    lses = jnp.where(head_lane == 2 * i, lse[:QB], jnp.where(head_lane == 2 * i + 1, lse[QB:], lses))
                l_ref[rows, :] = lses

    main = lambda cb, w=CB: pl.BlockSpec((ATTN_SUB * QB, w), lambda st: (st, cb))
    prev = lambda cb: pl.BlockSpec((QB, CB), lambda st: (jnp.maximum(ATTN_SUB * st - 1, 0), cb))
    in_specs, args = [], []
    for q_src, k_src, v_src in srcs:
        in_specs += [main(q_src[1]), prev(k_src[1]), main(k_src[1]), prev(v_src[1]), main(v_src[1])]
        args += [q_src[0], k_src[0], k_src[0], v_src[0], v_src[0]]
    outs = pl.pallas_call(
        body, name="attn_fwd", grid=(s // (ATTN_SUB * QB),),
        in_specs=in_specs, out_specs=[main(0), main(0, LANES)] * ng,
        out_shape=[jax.ShapeDtypeStruct((s, CB), BF16), jax.ShapeDtypeStruct((s, LANES), F32)] * ng,
        scratch_shapes=[pltpu.VMEM(((ATTN_SUB + 1) * QB, CB), BF16)] * (2 * ng),
        compiler_params=_cp(("parallel",)))(*args)
    return outs[0::2], outs[1::2]


def attn_bwd(srcs, stats):
    s = srcs[0][0][0].shape[0]
    ng = len(DILATIONS)
    sub = ATTN_SUB
    nin, nout, nbuf = 14, 3, 6

    def body(*refs):
        ins, outs, bufs = refs[:nin * ng], refs[nin * ng:(nin + nout) * ng], refs[(nin + nout) * ng:]
        step = pl.program_id(0)
        lo = _lane_lo()
        kj = lax.broadcasted_iota(jnp.int32, (QB, QB), 0)
        qi = lax.broadcasted_iota(jnp.int32, (QB, QB), 1)

        def block(group_refs, nb, j):
            (q_ref, _, k_ref, _, v_ref, _, da_ref, _, lc_ref, dc_ref, _, _, _, _,
             dq_ref, dk_ref, dv_ref, kbuf, vbuf, qbuf, dabuf, lbuf, dbuf) = group_refs
            rows, two = slice(j * QB, (j + 1) * QB), slice(j * QB, (j + 2) * QB)
            place = (step * sub + j) % nb
            m_prev, m_cur = _masks(place > 0)
            qmask = jnp.concatenate([m_prev, m_cur], axis=1)
            qmask = jnp.concatenate([qmask, qmask], axis=0)
            lcols, dcols = lc_ref[rows, :], dc_ref[rows, :]
            kmask = jnp.concatenate([kj <= qi, (kj >= qi) & (place < nb - 1)], axis=1)
            kmask = jnp.concatenate([kmask, kmask], axis=1)
            lrow = jnp.concatenate([lbuf[j], lbuf[j + 1]], axis=1)
            drow = jnp.concatenate([dbuf[j], dbuf[j + 1]], axis=1)
            for i in range(NH // 2):
                sl = slice(2 * HD * i, 2 * HD * (i + 1))
                col_pair = lambda t: jnp.concatenate([t[:, 2 * i:2 * i + 1], t[:, 2 * i + 1:2 * i + 2]], axis=0)
                row_pair = lambda t: jnp.concatenate([t[2 * i:2 * i + 1, :], t[2 * i + 1:2 * i + 2, :]], axis=1)
                ks2, vv2 = kbuf[two, sl], vbuf[two, sl]
                sc = lax.dot_general(_stack_heads(q_ref[rows, sl], lo), ks2, NT, preferred_element_type=F32)
                p = jnp.exp(jnp.where(qmask, sc, NEG) - col_pair(lcols))
                dp = lax.dot_general(_stack_heads(da_ref[rows, sl], lo), vv2, NT, preferred_element_type=F32)
                ds = p * (dp - col_pair(dcols))
                dq = jnp.dot(ds.astype(BF16), ks2, preferred_element_type=F32)
                dq_ref[rows, sl] = (jnp.where(lo, dq[:QB], dq[QB:]) * (HD ** -0.5)).astype(BF16)

                q2, da2 = _stack_heads(qbuf[two, sl], lo), _stack_heads(dabuf[two, sl], lo)
                ks, vv = k_ref[rows, sl], v_ref[rows, sl]
                sct = lax.dot_general(ks, q2, NT, preferred_element_type=F32)
                pt = jnp.exp(jnp.where(kmask, sct, NEG) - row_pair(lrow))
                dpt = lax.dot_general(vv, da2, NT, preferred_element_type=F32)
                dst = pt * (dpt - row_pair(drow))
                dv_ref[rows, sl] = jnp.dot(pt.astype(BF16), da2, preferred_element_type=F32).astype(BF16)
                dk_ref[rows, sl] = jnp.dot(dst.astype(BF16), q2, preferred_element_type=F32).astype(BF16)

        for g, dil in enumerate(DILATIONS):
            group_refs = (ins[nin * g:nin * (g + 1)] + outs[nout * g:nout * (g + 1)] + bufs[nbuf * g:nbuf * (g + 1)])
            (q_ref, kp_ref, k_ref, vp_ref, v_ref, qn_ref, da_ref, dan_ref, _, _, l_ref, ln_ref, d_ref, dn_ref,
             _, _, _, kbuf, vbuf, qbuf, dabuf, lbuf, dbuf) = group_refs
            kbuf[0:QB], kbuf[QB:] = kp_ref[...], k_ref[...]
            vbuf[0:QB], vbuf[QB:] = vp_ref[...], v_ref[...]
            qbuf[0:sub * QB], qbuf[sub * QB:] = q_ref[...], qn_ref[...]
            dabuf[0:sub * QB], dabuf[sub * QB:] = da_ref[...], dan_ref[...]
            for c in range(sub):
                lbuf[c], dbuf[c] = l_ref[:, c * QB:(c + 1) * QB], d_ref[:, c * QB:(c + 1) * QB]
            lbuf[sub], dbuf[sub] = ln_ref[...], dn_ref[...]
            for j in range(sub):
                block(group_refs, s // dil // QB, j)

    last = s // QB - 1
    main = lambda cb, w=CB: pl.BlockSpec((sub * QB, w), lambda st: (st, cb))
    prev = lambda cb: pl.BlockSpec((QB, CB), lambda st: (jnp.maximum(sub * st - 1, 0), cb))
    nxt = lambda cb: pl.BlockSpec((QB, CB), lambda st: (jnp.minimum(sub * (st + 1), last), cb))
    in_specs, args = [], []
    for (q_src, k_src, v_src), (da, lc, dc, lt, dt), dil in zip(srcs, stats, DILATIONS):
        nb = s // dil // QB
        t_main = pl.BlockSpec((NH, sub * QB), lambda st, nb=nb: (sub * st // nb, (sub * st % nb) // sub))
        t_nxt = pl.BlockSpec((NH, QB), lambda st, nb=nb: (sub * st // nb, jnp.minimum(sub * st % nb + sub, nb - 1)))
        in_specs += [main(q_src[1]), prev(k_src[1]), main(k_src[1]), prev(v_src[1]), main(v_src[1]), nxt(q_src[1]),
                     main(0), nxt(0), main(0, LANES), main(0, LANES), t_main, t_nxt, t_main, t_nxt]
        args += [q_src[0], k_src[0], k_src[0], v_src[0], v_src[0], q_src[0], da, da, lc, dc, lt, lt, dt, dt]
    out = jax.ShapeDtypeStruct((s, CB), BF16)
    big = pltpu.VMEM(((sub + 1) * QB, CB), BF16)
    outs = pl.pallas_call(
        body, name="attn_bwd", grid=(s // (sub * QB),),
        in_specs=in_specs, out_specs=[main(0)] * (nout * ng), out_shape=[out] * (nout * ng),
        scratch_shapes=([big] * 4 + [pltpu.VMEM((sub + 1, NH, QB), F32)] * 2) * ng,
        compiler_params=_cp(("parallel",), 56))(*args)
    return [tuple(outs[nout * g:nout * (g + 1)]) for g in range(ng)]


def _conv_taps(u, u_prev, first):
    tm = u.shape[0]
    row = lax.broadcasted_iota(jnp.int32, (tm, 1), 0)
    up = jnp.where(first, 0.0, u_prev)
    u1 = jnp.where(row == 0, up[HALO - 1:HALO, :], pltpu.roll(u, 1, 0))
    u2 = jnp.where(row == 0, up[HALO - 2:HALO - 1, :],
                   jnp.where(row == 1, up[HALO - 1:HALO, :], pltpu.roll(u, 2, 0)))
    return u1, u2


def mid_fwd(proj, o_g, lse_g, conv_w, expand, tm):
    s = proj.shape[0]
    hb = tm // HALO

    def body(ba_ref, ca_ref, xa_ref, za_ref, cah_ref, xah_ref, zb_ref,
             o0, o1, o2, l0, l1, l2, w_ref, exp_ref, ya_ref, yb_ref, at_ref, lc_ref, buf_o, buf_l):
        first = pl.program_id(0) == 0
        u = ca_ref[...].astype(F32) * xa_ref[...].astype(F32)
        u1, u2 = _conv_taps(u, cah_ref[...].astype(F32) * xah_ref[...].astype(F32), first)
        conv = w_ref[0:1, :] * u2 + w_ref[1:2, :] * u1 + w_ref[2:3, :] * u
        ya_ref[...] = (ba_ref[...].astype(F32) * conv * _silu(za_ref[...].astype(F32))).astype(BF16)
        ls = [_from_residue_major(l, buf_l.at[g], d) for g, (l, d) in enumerate(zip((l0, l1, l2), DILATIONS))]
        mx = jnp.maximum(jnp.maximum(ls[0], ls[1]), ls[2])
        es = [jnp.exp(l - mx) for l in ls]
        den = es[0] + es[1] + es[2]
        attn = jnp.zeros((tm, CB), F32)
        for e, o, d in zip(es, (o0, o1, o2), DILATIONS):
            attn = attn + _dot_hilo(e / den, exp_ref) * _from_residue_major(o, buf_o, d)
        at_ref[...] = attn
        lc_ref[...] = mx + jnp.log(den)
        yb_ref[...] = (attn * _silu(zb_ref[...].astype(F32))).astype(BF16)

    col = lambda j: pl.BlockSpec((tm, D), lambda i: (i, j))
    halo = lambda j: pl.BlockSpec((HALO, D), lambda i: (jnp.maximum(i * hb - 1, 0), j))
    loc = lambda w: pl.BlockSpec((tm, w), lambda i: (i, 0))
    rm = lambda w: [pl.BlockSpec((d, tm // d, w), lambda i: (0, i, 0)) for d in DILATIONS]
    rm_view = lambda ts, w: [t.reshape(d, s // d, w) for t, d in zip(ts, DILATIONS)]
    return pl.pallas_call(
        body, name="mid_fwd", grid=(s // tm,),
        in_specs=[col(0), col(1), col(2), col(3), halo(1), halo(2),
                  pl.BlockSpec((tm, CB), lambda i: (i, CB_ZB))] + rm(CB) + rm(LANES)
                 + [pl.BlockSpec((3, D), lambda i: (0, 0)), pl.BlockSpec(expand.shape, lambda i: (0, 0))],
        out_specs=[loc(D), loc(CB), loc(CB), loc(LANES)],
        out_shape=[jax.ShapeDtypeStruct((s, D), BF16), jax.ShapeDtypeStruct((s, CB), BF16),
                   jax.ShapeDtypeStruct((s, CB), F32), jax.ShapeDtypeStruct((s, LANES), F32)],
        scratch_shapes=[pltpu.VMEM((CB // LANES, tm, LANES), F32), pltpu.VMEM((3, 1, tm, LANES), F32)],
        compiler_params=_cp(("parallel",)))(
            proj, proj, proj, proj, proj, proj, proj, *rm_view(o_g, CB), *rm_view(lse_g, LANES), conv_w, expand)


def tail(proj, ya, yb, attn, x, target, gate, pa_w, pb_w, wo_w, total, conv_w, tm):
    s = proj.shape[0]
    ni = s // tm
    hb = tm // HALO
    nlate = NIN - CB_ZB * CB
    nearly = 4 * D

    def body(ya_ref, yb_ref, ga_ref, gb_ref, zb_ref, at_ref, x_ref, t_ref, gate_ref, pa_ref, pb_ref, wo_ref,
             tot_ref, ba_ref, ca_ref, xa_ref, za_ref, cah_ref, xah_ref, cw_ref,
             dp_hbm, dy_ref, da_ref, dc_ref, mg_ref, do_ref, dpa_ref, dpb_ref, st_ref, gwc_ref,
             stage, dconv_next, sems):
        step = pl.program_id(0)
        i = ni - 1 - step
        slot = step % 2

        def slabs(at_step, sl):
            rows = pl.ds(pl.multiple_of((ni - 1 - at_step) * tm, tm), tm)
            return (pltpu.make_async_copy(stage.at[sl, :, 0:nearly], dp_hbm.at[rows, pl.ds(0, nearly)],
                                          sems.at[sl, 0]),
                    pltpu.make_async_copy(stage.at[sl, :, nearly:], dp_hbm.at[rows, pl.ds(CB_ZB * CB, nlate)],
                                          sems.at[sl, 1]))

        @pl.when(step == 0)
        def _():
            st_ref[...] = jnp.zeros_like(st_ref)
            gwc_ref[...] = jnp.zeros_like(gwc_ref)
            dconv_next[...] = jnp.zeros_like(dconv_next)

        @pl.when(step >= 2)
        def _():
            for cp in slabs(step - 2, slot):
                cp.wait()

        gate_v = gate_ref[...]
        pa = jnp.dot(ya_ref[...], pa_ref[...], preferred_element_type=F32)
        pb = jnp.dot(yb_ref[...], pb_ref[...], preferred_element_type=F32)
        sa = jax.nn.sigmoid(ga_ref[...].astype(F32))
        sb = jax.nn.sigmoid(gb_ref[...].astype(F32))
        merged = (sa * pa + sb * pb).astype(BF16)
        mg_ref[...] = merged
        out = jnp.dot(merged, wo_ref[...], preferred_element_type=F32)
        err = x_ref[...] + gate_v * out - t_ref[...]
        dy = err * (1.0 / D)
        dy_ref[...] = dy
        st_ref[0:1, :] += jnp.sum(dy * out, axis=0, keepdims=True)
        st_ref[1:2, :] += jnp.sum(err * err, axis=0, keepdims=True)
        dout = (gate_v * dy).astype(BF16)
        do_ref[...] = dout
        dmg = lax.dot_general(dout, wo_ref[...], NT, preferred_element_type=F32)
        dpa = (dmg * sa).astype(BF16)
        dpb = (dmg * sb).astype(BF16)
        dpa_ref[...] = dpa
        dpb_ref[...] = dpb
        late = nearly
        stage[slot, :, late + CB:late + CB + D] = (dmg * pa * sa * (1.0 - sa)).astype(BF16)
        stage[slot, :, late + CB + D:] = (dmg * pb * sb * (1.0 - sb)).astype(BF16)
        dya = lax.dot_general(dpa, pa_ref[...], NT, preferred_element_type=F32)
        dyb = lax.dot_general(dpb, pb_ref[...], NT, preferred_element_type=F32)
        zb = zb_ref[...].astype(F32)
        sg = jax.nn.sigmoid(zb)
        attn_v = at_ref[...]
        dattn = dyb * (zb * sg)
        da_ref[...] = dattn.astype(BF16)
        stage[slot, :, late:late + CB] = (dyb * attn_v * (sg * (1.0 + zb * (1.0 - sg)))).astype(BF16)
        dc_ref[...] = _dot_hilo(dattn * attn_v, tot_ref)

        ba, ca, xa, za = (t[...].astype(F32) for t in (ba_ref, ca_ref, xa_ref, za_ref))
        u = ca * xa
        u1, u2 = _conv_taps(u, cah_ref[...].astype(F32) * xah_ref[...].astype(F32), i == 0)
        w0, w1, w2 = cw_ref[0:1, :], cw_ref[1:2, :], cw_ref[2:3, :]
        conv = w0 * u2 + w1 * u1 + w2 * u
        sga = jax.nn.sigmoid(za)
        sza = za * sga
        dconv = dya * ba * sza
        dcn = dconv_next[...]
        rowi = lax.broadcasted_iota(jnp.int32, (tm, 1), 0)
        d1 = jnp.where(rowi == tm - 1, dcn[0:1, :], pltpu.roll(dconv, tm - 1, 0))
        d2 = jnp.where(rowi == tm - 2, dcn[0:1, :],
                       jnp.where(rowi == tm - 1, dcn[1:2, :], pltpu.roll(dconv, tm - 2, 0)))
        du = w2 * dconv + w1 * d1 + w0 * d2
        stage[slot, :, 0:D] = (dya * conv * sza).astype(BF16)
        stage[slot, :, D:2 * D] = (du * xa).astype(BF16)
        stage[slot, :, 2 * D:3 * D] = (du * ca).astype(BF16)
        stage[slot, :, 3 * D:4 * D] = (dya * ba * conv * (sga * (1.0 + za * (1.0 - sga)))).astype(BF16)
        gwc_ref[0:1, :] += jnp.sum(dconv * u2, axis=0, keepdims=True)
        gwc_ref[1:2, :] += jnp.sum(dconv * u1, axis=0, keepdims=True)
        gwc_ref[2:3, :] += jnp.sum(dconv * u, axis=0, keepdims=True)
        dconv_next[...] = dconv[0:8, :]

        for cp in slabs(step, slot):
            cp.start()

        @pl.when(step == ni - 1)
        def _():
            for cp in slabs(step - 1, 1 - slot) + slabs(step, slot):
                cp.wait()

    rev = lambda st: ni - 1 - st
    row = lambda w: pl.BlockSpec((tm, w), lambda st: (rev(st), 0))
    pcol = lambda w, jb: pl.BlockSpec((tm, w), lambda st: (rev(st), jb))
    halo = lambda jb: pl.BlockSpec((HALO, D), lambda st: (jnp.maximum(rev(st) * hb - 1, 0), jb))
    const = lambda a: pl.BlockSpec(a.shape, lambda st: (0, 0), pipeline_mode=pl.Buffered(1))
    acc = pl.BlockSpec((8, D), lambda st: (0, 0))
    return pl.pallas_call(
        body, name="tail", grid=(ni,),
        in_specs=[row(D), row(CB), pcol(D, 9), pcol(D, 10), pcol(CB, CB_ZB), row(CB), row(D), row(D),
                  pl.BlockSpec((1, D), lambda st: (0, 0)), const(pa_w), const(pb_w), const(wo_w), const(total),
                  pcol(D, 0), pcol(D, 1), pcol(D, 2), pcol(D, 3), halo(1), halo(2),
                  pl.BlockSpec((3, D), lambda st: (0, 0))],
        out_specs=[pl.BlockSpec(memory_space=pl.ANY),
                   row(D), row(CB), row(LANES), row(D), row(D), row(D), row(D), acc, acc],
        out_shape=[jax.ShapeDtypeStruct((s, NIN), BF16), jax.ShapeDtypeStruct((s, D), F32),
                   jax.ShapeDtypeStruct((s, CB), BF16), jax.ShapeDtypeStruct((s, LANES), F32)]
                  + [jax.ShapeDtypeStruct((s, D), BF16)] * 4 + [jax.ShapeDtypeStruct((8, D), F32)] * 2,
        scratch_shapes=[pltpu.VMEM((2, tm, nearly + nlate), BF16), pltpu.VMEM((8, D), F32),
                        pltpu.SemaphoreType.DMA((2, 2))],
        compiler_params=_cp(("arbitrary",), 60))(
            ya, yb, proj, proj, proj, attn, x, target, gate, pa_w, pb_w, wo_w, total,
            proj, proj, proj, proj, proj, proj, conv_w)


def _local_step(x, target, shift, scale, gate, norm_w, conv_w, qw, kw, w_shard, small_shards, me_xyc):
    qw8, kw8 = jnp.tile(qw, (1, NH)), jnp.tile(kw, (1, NH))
    same, total, expand = _head_matrices()
    proj, ht, wg, (pa_g, pb_g, wo_g) = proj_fwd_gather(
        x, norm_w, scale, shift, w_shard, small_shards, gather_order(me_xyc), 1024)
    pa_w, wo_w = pa_g.reshape(D, D), wo_g.reshape(D, D)
    pb_w = pb_g.transpose(1, 0, 2).reshape(CB, D)
    srcs = qkv_prep(proj, qw8, kw8, same, 512)
    o_g, lse_g = attn_fwd(srcs)
    ya, yb, attn, lc = mid_fwd(proj, o_g, lse_g, conv_w, expand, 512)
    dproj, dy, da, dc, merged, dout, dpa, dpb, st_tail, st_conv = tail(
        proj, ya, yb, attn, x, target, gate, pa_w, pb_w, wo_w, total, conv_w, 256)
    g_wo, g_pa, g_pb = matmuls_tn([(merged, dout), (ya, dpa), (yb, dpb)], "grad_small_weights", 1024)
    grads = attn_bwd(srcs, stats_prep(da, lc, dc, 2048))
    dproj, gw_qk = qkv_grads_to_dproj(dproj, proj, grads, qw8, kw8, same, 512)
    slabs = [g_pa.reshape(NDEV, 128, D), g_pb.reshape(CB, NDEV, 128).transpose(1, 0, 2), g_wo.reshape(NDEV, 128, D)]
    grad_x, st_norm, r_win, (r_pa, r_pb, r_wo) = proj_bwd(
        ht, dproj, wg, slabs, scatter_order(me_xyc), x, dy, norm_w, scale, 1024)
    dmod = jnp.concatenate([st_norm[0:1], st_norm[1:2], st_tail[0:1]], axis=1)
    loss_part = (0.5 / D) * jnp.sum(st_tail[1])
    gw_heads = gw_qk[0:2].reshape(2, NH, HD).sum(axis=1)
    small = dict(dmod=dmod, norm_w=st_norm[2:3], conv_w=st_conv[0:3],
                 q_norm_w=gw_heads[0:1], k_norm_w=gw_heads[1:2], loss=loss_part)
    return grad_x, small, (r_win, r_pa, r_pb, r_wo)


def kernel(x, c, w_ada, b_ada, norm_w, w_in, conv_w, q_norm_w, k_norm_w, w_br_conv, w_br_attn, w_out, loss_target, m_w_ada, m_b_ada, m_norm_w, m_w_in, m_conv_w, m_q_norm_w, m_k_norm_w, m_w_br_conv, m_w_br_attn, m_w_out, v_w_ada, v_b_ada, v_norm_w, v_w_in, v_conv_w, v_q_norm_w, v_k_norm_w, v_w_br_conv, v_w_br_attn, v_w_out):
    me_xyc = (lax.axis_index("x"), lax.axis_index("y"), lax.axis_index("c"))
    me = _dev_index(me_xyc)
    ncol = w_ada.shape[2]

    conv_pad = jnp.zeros((8, 128), F32).at[0:3].set(conv_w[0])
    b_cols = lax.dynamic_slice(b_ada, (0, me * ncol), (1, ncol))
    mod_pieces, c_all, conv_all = ada_fwd(c, conv_pad, w_ada[0], b_cols)
    conv_full = conv_all[:, 0:3].transpose(1, 0, 2).reshape(3, D)
    c_all = c_all.reshape(NDEV, D)
    mod = mod_pieces.reshape(1, 3 * D)
    shift, scale, gate = mod[:, 0:D], mod[:, D:2 * D], mod[:, 2 * D:3 * D]

    grad_x, small, (r_win, r_pa, r_pb, r_wo) = _local_step(
        x[0], loss_target[0], shift, scale, gate, norm_w, conv_full, q_norm_w, k_norm_w,
        w_in[0].astype(BF16), [w_br_conv[0].astype(BF16), w_br_attn[0].astype(BF16), w_out[0].astype(BF16)], me_xyc)

    packed = jnp.concatenate(
        [small["dmod"], small["norm_w"], small["conv_w"].reshape(1, 3 * D), small["q_norm_w"], small["k_norm_w"],
         jnp.full((1, 128), small["loss"], F32)], axis=1)
    packed_all, tot = gather_sum(packed)
    loss = tot[0, 7 * D + 2 * HD]
    dmod_all = packed_all[:, 0, 0:3 * D]
    g_b_ada = tot[:, 0:3 * D]
    g_norm_w = tot[:, 3 * D:4 * D]
    g_conv = lax.dynamic_slice(tot[:, 4 * D:7 * D].reshape(3, D), (0, me * 128), (3, 128))
    g_qn = tot[:, 7 * D:7 * D + HD]
    g_kn = tot[:, 7 * D + HD:7 * D + 2 * HD]
    dmod_cols = lax.dynamic_slice(dmod_all, (0, me * ncol), (NDEV, ncol))

    def upd(parts, w, m, v, name, rows):
        shape = w.shape
        w2, m2, v2 = (t.reshape(shape[-2:]) for t in (w, m, v))
        return [t.reshape(shape) for t in adamw(parts, w2, m2, v2, name, rows)]

    res = {"w_in": upd(r_win, w_in, m_w_in, v_w_in, "adamw_w_in", 128)}
    small_params = {"b_ada": (g_b_ada[None], b_ada, m_b_ada, v_b_ada),
                    "norm_w": (g_norm_w[None], norm_w, m_norm_w, v_norm_w),
                    "conv_w": (g_conv[None], conv_w, m_conv_w, v_conv_w),
                    "q_norm_w": (g_qn[None], q_norm_w, m_q_norm_w, v_q_norm_w),
                    "k_norm_w": (g_kn[None], k_norm_w, m_k_norm_w, v_k_norm_w),
                    "w_br_conv": (r_pa, w_br_conv, m_w_br_conv, v_w_br_conv),
                    "w_br_attn": (r_pb, w_br_attn, m_w_br_attn, v_w_br_attn),
                    "w_out": (r_wo, w_out, m_w_out, v_w_out)}
    updated = adamw_small([(item[0],) + tuple(t.reshape(t.shape[-2:]) for t in item[1:])
                           for item in small_params.values()],
                          (c_all.T, dmod_cols, w_ada[0], m_w_ada[0], v_w_ada[0]))
    res["w_ada"] = [t[None] for t in updated[0]]
    for (pname, item), outs4 in zip(small_params.items(), updated[1:]):
        res[pname] = [t.reshape(item[1].shape) for t in outs4]
    names = ["w_ada", "b_ada", "norm_w", "w_in", "conv_w", "q_norm_w", "k_norm_w", "w_br_conv", "w_br_attn", "w_out"]
    return (loss, grad_x[None], *[res[n][0] for n in names], *[res[n][1] for n in names],
            *[res[n][2] for n in names], *[res[n][3] for n in names])
```

```python
import jax
import jax.numpy as jnp
from jax import lax
from jax.experimental import pallas as pl
from jax.experimental.pallas import tpu as pltpu

F32, BF16 = jnp.float32, jnp.bfloat16
D = 1024
NIN = 11264
NDEV = 8
SHARD = NIN // NDEV
HD = 64
NH = 8
QB = 128
CB = 512
CB_Q, CB_K, CB_V, CB_ZB = 8, 11, 14, 17
DILATIONS = (1, 4, 16)
EPS = 1e-6
NEG = -1e30
HALO = 16
LANES = 128
MESH = pl.DeviceIdType.MESH

ADAM_LR, ADAM_B1, ADAM_B2, ADAM_EPS, ADAM_WD, ADAM_STEP = 0.001, 0.9, 0.999, 1e-08, 0.01, 10

NT = (((1,), (1,)), ((), ()))
TN = (((0,), (0,)), ((), ()))


def _cp(sem, vmem_mb=48):
    return pltpu.CompilerParams(dimension_semantics=sem, vmem_limit_bytes=vmem_mb << 20)


def _silu(z):
    return z * jax.nn.sigmoid(z)


def _coords():
    return lax.axis_index("x"), lax.axis_index("y"), lax.axis_index("c")


FLIPS = [(fx, fy, fc) for fx in (0, 1) for fy in (0, 1) for fc in (0, 1)][1:]


def gather_sum(vec):
    def body(v_ref, all_ref, sum_ref, send_sems, recv_sems, local_sem):
        me_xyc = _coords()
        me = _dev_index(me_xyc)
        peers = [_flip(me_xyc, f) for f in FLIPS]

        def copy(k, block):
            return pltpu.make_async_remote_copy(
                src_ref=v_ref, dst_ref=all_ref.at[block], send_sem=send_sems.at[k], recv_sem=recv_sems.at[k],
                device_id=peers[k], device_id_type=MESH)

        mine = pltpu.make_async_copy(v_ref, all_ref.at[me], local_sem)
        sends = [copy(k, me) for k in range(7)]
        for cp in [mine] + sends:
            cp.start()
        for k in range(7):
            copy(k, _dev_index(peers[k])).wait_recv()
        mine.wait()
        acc = all_ref[0]
        for b in range(1, NDEV):
            acc = acc + all_ref[b]
        sum_ref[...] = acc
        for cp in sends:
            cp.wait_send()

    return pl.pallas_call(
        body, name="gather_sum",
        out_shape=[jax.ShapeDtypeStruct((NDEV,) + vec.shape, F32), jax.ShapeDtypeStruct(vec.shape, F32)],
        scratch_shapes=[pltpu.SemaphoreType.DMA((7,)), pltpu.SemaphoreType.DMA((7,)), pltpu.SemaphoreType.DMA],
    )(vec)


def _flip(dev, f):
    return tuple(1 - v if b else v for v, b in zip(dev, f))


def _dev_index(dev):
    return 4 * dev[0] + 2 * dev[1] + dev[2]


def _chip_order(x, y, c):
    xor = lambda a, b: a + b - 2 * a * b
    return [(xor(x, 1 - c), xor(y, c)), (xor(x, c), xor(y, 1 - c)), (1 - x, 1 - y)]


def gather_order(me_xyc):
    x, y, c = me_xyc
    chips = _chip_order(x, y, c)
    devs = [(x, y, c), (x, y, 1 - c), (*chips[0], c), (*chips[1], c),
            (*chips[1], 1 - c), (*chips[0], 1 - c), (*chips[2], c), (*chips[2], 1 - c)]
    return jnp.stack([_dev_index(d) for d in devs]).astype(jnp.int32)


def scatter_order(me_xyc):
    devs = [_flip(me_xyc, f) for f in FLIPS] + [me_xyc]
    return jnp.stack([_dev_index(d) for d in devs]).astype(jnp.int32)


def ada_fwd(c, conv_pad, w_ada, b_cols):
    ncol = w_ada.shape[1]

    def body(c_ref, cv_ref, w_ref, b_ref, mod_ref, call_ref, cvall_ref, rows_buf, send_sems, recv_sems, local_sems):
        me_xyc = _coords()
        me = _dev_index(me_xyc)
        peers = [_flip(me_xyc, f) for f in FLIPS]
        pids = [_dev_index(p) for p in peers]

        def copy(a, k, src, dst):
            return pltpu.make_async_remote_copy(src_ref=src, dst_ref=dst, send_sem=send_sems.at[a, k],
                                                recv_sem=recv_sems.at[a, k], device_id=peers[k], device_id_type=MESH)

        own = [pltpu.make_async_copy(c_ref, call_ref.at[me], local_sems.at[0]),
               pltpu.make_async_copy(cv_ref, cvall_ref.at[me], local_sems.at[1])]
        first = [copy(0, k, c_ref, call_ref.at[me]) for k in range(7)]
        first += [copy(1, k, cv_ref, cvall_ref.at[me]) for k in range(7)]
        for cp in own + first:
            cp.start()
        own[0].wait()
        for k in range(7):
            copy(0, k, c_ref, call_ref.at[pids[k]]).wait_recv()
        seq = lax.broadcasted_iota(jnp.int32, (NDEV, 1), 0)
        c_all = jnp.zeros((NDEV, D), F32)
        for p in range(NDEV):
            c_all = jnp.where(seq == p, call_ref[p], c_all)
        mods = jnp.dot(_silu(c_all).astype(BF16), w_ref[...].astype(BF16), preferred_element_type=F32) + b_ref[...]
        for p in range(NDEV):
            rows_buf[p] = mods[p:p + 1, :]
        mine = pltpu.make_async_copy(rows_buf.at[me], mod_ref.at[me], local_sems.at[2])
        second = [copy(2, k, rows_buf.at[pids[k]], mod_ref.at[me]) for k in range(7)]
        for cp in [mine] + second:
            cp.start()
        for k in range(7):
            copy(2, k, rows_buf.at[pids[k]], mod_ref.at[pids[k]]).wait_recv()
            copy(1, k, cv_ref, cvall_ref.at[pids[k]]).wait_recv()
        for cp in first + second:
            cp.wait_send()
        own[1].wait()
        mine.wait()

    return pl.pallas_call(
        body, name="ada_fwd",
        out_shape=[jax.ShapeDtypeStruct((NDEV, 1, ncol), F32), jax.ShapeDtypeStruct((NDEV, 1, D), F32),
                   jax.ShapeDtypeStruct((NDEV,) + conv_pad.shape, F32)],
        scratch_shapes=[pltpu.VMEM((NDEV, 1, ncol), F32), pltpu.SemaphoreType.DMA((3, 7)),
                        pltpu.SemaphoreType.DMA((3, 7)), pltpu.SemaphoreType.DMA((3,))],
    )(c, conv_pad, w_ada, b_cols)


def ada_bwd(c_all_t, dmod_cols):
    def body(c_ref, d_ref, o_ref):
        at = _silu(c_ref[...])
        acc = at[:, 0:1] * d_ref[0:1, :]
        for b in range(1, NDEV):
            acc = acc + at[:, b:b + 1] * d_ref[b:b + 1, :]
        o_ref[...] = acc

    return pl.pallas_call(body, name="ada_bwd",
                          out_shape=jax.ShapeDtypeStruct((D, dmod_cols.shape[1]), F32))(c_all_t, dmod_cols)


def _adamw_update(g, w_ref, m_ref, v_ref, g_ref, d_ref, nm_ref, nv_ref):
    nm = ADAM_B1 * m_ref[...] + (1.0 - ADAM_B1) * g
    nv = ADAM_B2 * v_ref[...] + (1.0 - ADAM_B2) * (g * g)
    g_ref[...] = g
    nm_ref[...] = nm
    nv_ref[...] = nv
    m_hat = nm / (1.0 - ADAM_B1 ** ADAM_STEP)
    v_hat = nv / (1.0 - ADAM_B2 ** ADAM_STEP)
    d_ref[...] = -ADAM_LR * (m_hat / (jnp.sqrt(v_hat) + ADAM_EPS) + ADAM_WD * w_ref[...])


def adamw_small(items):
    n = len(items)

    def body(*refs):
        ins, outs = refs[:4 * n], refs[4 * n:]
        for a in range(n):
            p_ref, w_ref, m_ref, v_ref = ins[4 * a:4 * a + 4]
            g = p_ref[0].astype(F32)
            for b in range(1, p_ref.shape[0]):
                g = g + p_ref[b].astype(F32)
            _adamw_update(g, w_ref, m_ref, v_ref, *outs[4 * a:4 * a + 4])

    out = pl.pallas_call(
        body, name="adamw_small",
        out_shape=[jax.ShapeDtypeStruct(it[1].shape, F32) for it in items for _ in range(4)],
        compiler_params=pltpu.CompilerParams(vmem_limit_bytes=48 << 20))(*[t for it in items for t in it])
    return [out[4 * a:4 * a + 4] for a in range(n)]


def adamw(parts, w, m, v, name, rows):
    n, r, ccols = parts.shape

    def body(p_ref, w_ref, m_ref, v_ref, g_ref, d_ref, nm_ref, nv_ref):
        g = p_ref[0].astype(F32)
        for b in range(1, n):
            g = g + p_ref[b].astype(F32)
        _adamw_update(g, w_ref, m_ref, v_ref, g_ref, d_ref, nm_ref, nv_ref)

    blk = pl.BlockSpec((rows, ccols), lambda i: (i, 0))
    out = jax.ShapeDtypeStruct((r, ccols), F32)
    return pl.pallas_call(
        body, name=name, grid=(r // rows,),
        in_specs=[pl.BlockSpec((n, rows, ccols), lambda i: (0, i, 0)), blk, blk, blk],
        out_specs=[blk] * 4, out_shape=[out] * 4, compiler_params=_cp(("parallel",)))(parts, w, m, v)


def proj_fwd_gather(x, nw, scale, shift, w_shard, extras, order, tm):
    s = x.shape[0]
    ni = s // tm
    n = 1 + len(extras)
    mid = ni - 2

    def body(order_ref, x_ref, nw_ref, sc_ref, sh_ref, *refs):
        ins, o_ref, ht_ref, outs = refs[:n], refs[n], refs[n + 1], refs[n + 2:2 * n + 2]
        h_all, wbuf, send_sems, recv_sems, local_sems, load_sems = refs[2 * n + 2:]
        jj, i = pl.program_id(0), pl.program_id(1)
        x, y, c = _coords()
        me, sibling = (x, y, c), (x, y, 1 - c)
        chips = _chip_order(x, y, c)
        relayed = [(*chips[1], 1 - c), (*chips[0], 1 - c), (*chips[2], 1 - c)]

        def slot(a, dev):
            return outs[a].at[_dev_index(dev)]

        def copy(a, k, block, to, src=None):
            return pltpu.make_async_remote_copy(
                src_ref=slot(a, block) if src is None else src, dst_ref=slot(a, block),
                send_sem=send_sems.at[a, k], recv_sem=recv_sems.at[a, k], device_id=to, device_id_type=MESH)

        mine = [pltpu.make_async_copy(ins[a], slot(a, me), local_sems.at[a]) for a in range(n)]
        to_sibling = [copy(a, 0, me, sibling, src=ins[a]) for a in range(n)]
        to_chip = [[copy(a, 1 + j, me, (*chips[j], c), src=ins[a]) for a in range(n)] for j in range(2)]
        onward = [copy(a, 3, (*chips[1], c), (*chips[0], c)) for a in range(n)]
        passed = [[copy(a, 4 + j, (*ch, c), sibling) for a in range(n)] for j, ch in enumerate(chips)]
        sends = lambda a: [to_sibling[a], to_chip[0][a], to_chip[1][a], onward[a]] + [passed[j][a] for j in range(3)]

        def arrived(a, j):
            copy(a, 1 + j, (*chips[j], c), me).wait_recv()

        def load(row):
            return pltpu.make_async_copy(outs[0].at[order_ref[row]], wbuf.at[row % 2], load_sems.at[row % 2])

        @pl.when((jj == 0) & (i == 0))
        def _():
            for cp in mine:
                cp.start()
            to_sibling[0].start()
            to_chip[0][0].start()
            pltpu.make_async_copy(ins[0], wbuf.at[0], load_sems.at[0]).start()

        @pl.when((jj == 1) & (i == 0))
        def _():
            to_chip[1][0].start()

        @pl.when((jj == 4) & (i == 0))
        def _():
            for a in range(1, n):
                to_sibling[a].start()
                to_chip[0][a].start()
                to_chip[1][a].start()

        direct = {2: 0, 3: 1, 6: 2}
        relay = {4: 0, 5: 1, 7: 2}

        @pl.when((jj == 0) & (i == mid))
        def _():
            copy(0, 0, sibling, me).wait_recv()

        for row, j in direct.items():
            @pl.when((jj == row - 1) & (i == mid))
            def _(j=j):
                arrived(0, j)
                passed[j][0].start()
                if j == 1:
                    onward[0].start()

        for row, j in relay.items():
            @pl.when((jj == row - 1) & (i == mid))
            def _(j=j):
                copy(0, 4 + j, relayed[j], me).wait_recv()

        @pl.when((jj == NDEV - 1) & (i == 0))
        def _():
            for a in range(1, n):
                arrived(a, 1)
                onward[a].start()
                passed[1][a].start()
                arrived(a, 0)
                passed[0][a].start()

        @pl.when((jj < NDEV - 1) & (i == mid))
        def _():
            load(jj + 1).start()

        @pl.when(i == 0)
        def _():
            load(jj).wait()

        @pl.when(jj == 0)
        def _():
            xf = x_ref[...]
            r = lax.rsqrt(jnp.mean(xf * xf, axis=-1, keepdims=True) + EPS)
            h = (xf * r * nw_ref[...]) * (1.0 + sc_ref[...]) + sh_ref[...]
            h_all[i] = h.astype(BF16)
            ht_ref[...] = h.T.astype(BF16)

        o_ref[...] = jnp.dot(h_all[i], wbuf[jj % 2], preferred_element_type=F32).astype(BF16)

        @pl.when((jj == NDEV - 1) & (i == ni - 1))
        def _():
            for a in range(1, n):
                arrived(a, 2)
                passed[2][a].start()
            for a in range(1, n):
                copy(a, 0, sibling, me).wait_recv()
                for j in range(3):
                    copy(a, 4 + j, relayed[j], me).wait_recv()
            for a in range(n):
                mine[a].wait()
                for cp in sends(a):
                    cp.wait_send()

    any_spec = pl.BlockSpec(memory_space=pl.ANY)
    vec = pl.BlockSpec((1, D), lambda jj, i, o: (0, 0))
    outs = pl.pallas_call(
        body, name="proj_fwd_gather",
        grid_spec=pltpu.PrefetchScalarGridSpec(
            num_scalar_prefetch=1, grid=(NDEV, ni),
            in_specs=[pl.BlockSpec((tm, D), lambda jj, i, o: (jnp.where(jj == 0, i, ni - 1), 0))] + [vec] * 3
                     + [any_spec] * n,
            out_specs=[pl.BlockSpec((tm, SHARD), lambda jj, i, o: (i, o[jj])),
                       pl.BlockSpec((D, tm), lambda jj, i, o: (0, jnp.where(jj == 0, i, ni - 1)))]
                      + [any_spec] * n,
            scratch_shapes=[pltpu.VMEM((ni, tm, D), BF16), pltpu.VMEM((2, D, SHARD), BF16),
                            pltpu.SemaphoreType.DMA((n, 7)), pltpu.SemaphoreType.DMA((n, 7)),
                            pltpu.SemaphoreType.DMA((n,)), pltpu.SemaphoreType.DMA((2,))]),
        out_shape=[jax.ShapeDtypeStruct((s, NIN), BF16), jax.ShapeDtypeStruct((D, s), BF16),
                   jax.ShapeDtypeStruct((NDEV, D, SHARD), BF16)]
                  + [jax.ShapeDtypeStruct((NDEV,) + e.shape, e.dtype) for e in extras],
        compiler_params=_cp(("arbitrary", "arbitrary"), 56))(order, x, nw, scale, shift, w_shard, *extras)
    return outs[0], outs[1], outs[2], outs[3:]


def proj_bwd(ht, dproj, wg, smalls, order, x, dy, nw, scale, tt):
    s = dproj.shape[0]
    nk = s // tt
    n = len(smalls)
    rows_per_step = tt // nk
    last = 2 * NDEV

    def body(order_ref, ht_ref, dp_ref, w_ref, x_ref, dy_ref, nw_ref, sc_ref, *rest):
        small_in = rest[:n]
        gx_ref, st_ref, gw_ref, rwin_ref = rest[n:n + 4]
        small_out = rest[n + 4:2 * n + 4]
        acc, stage, dh, send_sems, recv_sems, local_sems, stage_sems = rest[2 * n + 4:]
        t, k = pl.program_id(0), pl.program_id(1)
        me_xyc = _coords()
        me = _dev_index(me_xyc)
        peers = [_flip(me_xyc, f) for f in FLIPS]

        def exchange(a, kf, src_arr, dst_arr):
            pid = _dev_index(peers[kf])
            mk = lambda dst: pltpu.make_async_remote_copy(
                src_ref=src_arr.at[pid], dst_ref=dst, send_sem=send_sems.at[a, kf], recv_sem=recv_sems.at[a, kf],
                device_id=peers[kf], device_id_type=MESH)
            return mk(dst_arr.at[me]), mk(dst_arr.at[pid])

        small_pairs = [exchange(1 + a, kf, small_in[a], small_out[a]) for kf in range(7) for a in range(n)]
        small_own = [pltpu.make_async_copy(small_in[a].at[me], small_out[a].at[me], local_sems.at[1 + a])
                     for a in range(n)]
        win_pairs = [exchange(0, kf, gw_ref, rwin_ref) for kf in range(7)]
        win_own = pltpu.make_async_copy(gw_ref.at[me], rwin_ref.at[me], local_sems.at[0])

        def to_hbm(jj):
            slab = me if jj == 7 else _dev_index(peers[jj])
            return pltpu.make_async_copy(stage.at[jj % 2], gw_ref.at[slab], stage_sems.at[jj % 2])

        @pl.when((t == 0) & (k == 0))
        def _():
            for cp in small_own:
                cp.start()
            for send, _ in small_pairs:
                send.start()

        @pl.when(t < NDEV)
        def _():
            p = jnp.dot(ht_ref[...], dp_ref[...], preferred_element_type=F32)

            @pl.when(k == 0)
            def _():
                acc[...] = p

            @pl.when(k > 0)
            def _():
                acc[...] += p

        for jj in range(NDEV):
            @pl.when((t == jj) & (k == nk - 1))
            def _(jj=jj):
                stage[jj % 2] = acc[...].astype(BF16)
                to_hbm(jj).start()

            @pl.when((t == jj + 1) & (k == 1))
            def _(jj=jj):
                to_hbm(jj).wait()
                if jj < 7:
                    win_pairs[jj][0].start()
                else:
                    win_own.start()

        def matmul_step():
            p = lax.dot_general(dp_ref[...], w_ref[...], NT, preferred_element_type=F32)
            slot = t % 2
            dh[slot] = jnp.where(k == 0, p, dh[slot] + p)

        def norm_step():
            g = dh.at[(t + 1) % 2][pl.ds(pl.multiple_of(k * rows_per_step, rows_per_step), rows_per_step), :]
            xf = x_ref[...]
            r = lax.rsqrt(jnp.mean(xf * xf, axis=-1, keepdims=True) + EPS)
            xh = xf * r
            dn = g * (1.0 + sc_ref[...])
            dxh = dn * nw_ref[...]
            gx_ref[...] = dy_ref[...] + r * (dxh - xh * jnp.mean(dxh * xh, axis=-1, keepdims=True))
            st_ref[0:1, :] += jnp.sum(g, axis=0, keepdims=True)
            st_ref[1:2, :] += jnp.sum(g * xh * nw_ref[...], axis=0, keepdims=True)
            st_ref[2:3, :] += jnp.sum(dn * xh, axis=0, keepdims=True)

        @pl.when((t == 0) & (k == 0))
        def _():
            st_ref[...] = jnp.zeros_like(st_ref)

        @pl.when(t == NDEV)
        def _():
            matmul_step()

        @pl.when((t > NDEV) & (t < last))
        def _():
            matmul_step()
            norm_step()

        @pl.when(t == last)
        def _():
            norm_step()

        @pl.when((t == last) & (k == nk - 1))
        def _():
            for _, recv in win_pairs + small_pairs:
                recv.wait_recv()
            for send, _ in win_pairs + small_pairs:
                send.wait_send()
            win_own.wait()
            for cp in small_own:
                cp.wait()

    any_spec = pl.BlockSpec(memory_space=pl.ANY)
    first = lambda t: t < NDEV
    slab = lambda t, k: jnp.where(t == last, NDEV - 1, k)
    chunk = pl.BlockSpec((rows_per_step, D), lambda t, k, o: (jnp.maximum((t - NDEV - 1) * nk + k, 0), 0))
    vec = pl.BlockSpec((1, D), lambda t, k, o: (0, 0))
    outs = pl.pallas_call(
        body, name="proj_bwd",
        grid_spec=pltpu.PrefetchScalarGridSpec(
            num_scalar_prefetch=1, grid=(last + 1, nk),
            in_specs=[pl.BlockSpec((D, tt), lambda t, k, o: (0, jnp.where(first(t), k, nk - 1))),
                      pl.BlockSpec((tt, SHARD), lambda t, k, o: (jnp.where(first(t), k, jnp.minimum(t, last - 1) - NDEV),
                                                                 jnp.where(first(t), o[jnp.minimum(t, NDEV - 1)],
                                                                           slab(t, k)))),
                      pl.BlockSpec((None, D, SHARD), lambda t, k, o: (jnp.where(first(t), 0, slab(t, k)), 0, 0)),
                      chunk, chunk, vec, vec]
                     + [any_spec] * n,
            out_specs=[chunk, pl.BlockSpec((8, D), lambda t, k, o: (0, 0))] + [any_spec] * (2 + n),
            scratch_shapes=[pltpu.VMEM((D, SHARD), F32), pltpu.VMEM((2, D, SHARD), BF16),
                            pltpu.VMEM((2, tt, D), F32),
                            pltpu.SemaphoreType.DMA((1 + n, 7)), pltpu.SemaphoreType.DMA((1 + n, 7)),
                            pltpu.SemaphoreType.DMA((1 + n,)), pltpu.SemaphoreType.DMA((2,))]),
        out_shape=[jax.ShapeDtypeStruct((s, D), F32), jax.ShapeDtypeStruct((8, D), F32),
                   jax.ShapeDtypeStruct((NDEV, D, SHARD), BF16), jax.ShapeDtypeStruct((NDEV, D, SHARD), BF16)]
                  + [jax.ShapeDtypeStruct(a.shape, a.dtype) for a in smalls],
        compiler_params=_cp(("arbitrary", "arbitrary"), 56))(order, ht, dproj, wg, x, dy, nw, scale, *smalls)
    return outs[0], outs[1], outs[3], outs[4:]


def _head_matrices():
    lane = lax.broadcasted_iota(jnp.int32, (CB, CB), 0)
    col = lax.broadcasted_iota(jnp.int32, (CB, CB), 1)
    same = (lane // HD == col // HD).astype(BF16)
    lane_c = lax.broadcasted_iota(jnp.int32, (CB, LANES), 0)
    col_c = lax.broadcasted_iota(jnp.int32, (CB, LANES), 1)
    total = (lane_c // HD == col_c).astype(BF16)
    lane_e = lax.broadcasted_iota(jnp.int32, (LANES, CB), 0)
    col_e = lax.broadcasted_iota(jnp.int32, (LANES, CB), 1)
    expand = (lane_e == col_e // HD).astype(BF16)
    return same, total, expand


def _head_sum(x, m_ref):
    return jnp.dot(x.astype(BF16), m_ref[...], preferred_element_type=F32)


def _dot_hilo(x, m_ref):
    hi = x.astype(BF16)
    lo = (x - hi.astype(F32)).astype(BF16)
    return (jnp.dot(hi, m_ref[...], preferred_element_type=F32)
            + jnp.dot(lo, m_ref[...], preferred_element_type=F32))


def _to_residue_major(val, buf, out_ref, dil):
    rows = out_ref.shape[1]
    for k in range(val.shape[1] // LANES):
        lanes = slice(k * LANES, (k + 1) * LANES)
        buf[k] = val[:, lanes]
        for r in range(dil):
            out_ref[r, :, lanes] = buf.at[k][pl.ds(r, rows, stride=dil), :].astype(out_ref.dtype)


def _from_residue_major(ref, buf, dil):
    if dil == 1:
        return ref[0].astype(F32)
    rows, chunks = ref.shape[1], ref.shape[2] // LANES
    for k in range(chunks):
        for r in range(dil):
            buf.at[k][pl.ds(r, rows, stride=dil), :] = ref[r, :, k * LANES:(k + 1) * LANES].astype(F32)
    return jnp.concatenate([buf[k] for k in range(chunks)], axis=1)


def qkv_prep(proj, qw8, kw8, same, tm):
    s = proj.shape[0]
    items = []
    for g, d in enumerate(DILATIONS):
        items += [(g, "q", CB_Q + g, d), (g, "k", CB_K + g, d)] + ([(g, "v", CB_V + g, d)] if d > 1 else [])
    n = len(items)

    def body(*refs):
        ins, (qw_ref, kw_ref, same_ref), outs, buf = refs[:n], refs[n:n + 3], refs[n + 3:2 * n + 3], refs[-1]
        for idx, (_, kind, _, dil) in enumerate(items):
            val = ins[idx][...].astype(F32)
            if kind != "v":
                r = lax.rsqrt(_head_sum(val * val, same_ref) * (1.0 / HD) + EPS)
                val = val * r * (qw_ref if kind == "q" else kw_ref)[...]
            if dil == 1:
                outs[idx][0] = val.astype(BF16)
            else:
                _to_residue_major(val, buf, outs[idx], dil)

    full = lambda a: pl.BlockSpec(a.shape, lambda i: (0, 0))
    outs = pl.pallas_call(
        body, name="qkv_prep", grid=(s // tm,),
        in_specs=[pl.BlockSpec((tm, CB), lambda i, cb=cb: (i, cb)) for _, _, cb, _ in items]
                 + [full(qw8), full(kw8), full(same)],
        out_specs=[pl.BlockSpec((d, tm // d, CB), lambda i: (0, i, 0)) for _, _, _, d in items],
        out_shape=[jax.ShapeDtypeStruct((d, s // d, CB), BF16) for _, _, _, d in items],
        scratch_shapes=[pltpu.VMEM((CB // LANES, tm, LANES), F32)],
        compiler_params=_cp(("parallel",)))(*([proj] * n), qw8 * (HD ** -0.5), kw8, same)
    srcs = [[None, None, (proj, CB_V + g)] for g in range(len(DILATIONS))]
    for (g, kind, _, _), o in zip(items, outs):
        srcs[g]["qkv".index(kind)] = (o.reshape(s, CB), 0)
    return srcs


def grad_small_weights_and_stats(pairs, da, lc, dc, tk):
    s = da.shape[0]
    nk = s // tk
    n = len(pairs)
    shapes = [(a.shape[1], b.shape[1]) for a, b in pairs]
    tt = LANES * max(DILATIONS)
    per = tt // tk
    assert per >= len(DILATIONS)
    nda = len(DILATIONS) - 1

    def stats_of_group(dil, src_refs, dst_refs, sbuf):
        rows = tt // dil
        dst_refs = list(dst_refs)
        for src in src_refs:
            dst = dst_refs.pop(0) if dil > 1 else None
            dst_t = dst_refs.pop(0)
            sbuf[0] = src[...]
            for r in range(dil):
                piece = sbuf.at[0][pl.ds(r, rows, stride=dil), :] if dil > 1 else sbuf[0]
                if dil > 1:
                    dst[r] = piece
                dst_t[r] = piece.T[0:NH, :]

    def body(*refs):
        ins, (da_ref, lc_ref, dc_ref) = refs[:2 * n], refs[2 * n:2 * n + 3]
        rest = refs[2 * n + 3:]
        outs, dap, souts = rest[:n], rest[n:n + nda], list(rest[n + nda:-(n + 2)])
        accs, buf, sbuf = rest[-(n + 2):-2], rest[-2], rest[-1]
        k = pl.program_id(0)
        for j in range(n):
            p = lax.dot_general(ins[2 * j][...], ins[2 * j + 1][...], TN, preferred_element_type=F32)
            accs[j][...] = jnp.where(k == 0, p, accs[j][...] + p)
        val = da_ref[...].astype(F32)
        for j, dil in enumerate(DILATIONS[1:]):
            _to_residue_major(val, buf, dap[j], dil)
        for phase, dil in enumerate(DILATIONS):
            dsts = [souts.pop(0) for _ in range(4 if dil > 1 else 2)]

            @pl.when(k % per == phase)
            def _(dil=dil, dsts=dsts):
                stats_of_group(dil, (lc_ref, dc_ref), dsts, sbuf)

        @pl.when(k == nk - 1)
        def _():
            for j in range(n):
                outs[j][...] = accs[j][...].astype(BF16)

    out_specs = [pl.BlockSpec(sh, lambda k: (0, 0)) for sh in shapes]
    out_shape = [jax.ShapeDtypeStruct(sh, BF16) for sh in shapes]
    for dil in DILATIONS[1:]:
        out_specs.append(pl.BlockSpec((dil, tk // dil, CB), lambda k: (0, k, 0)))
        out_shape.append(jax.ShapeDtypeStruct((dil, s // dil, CB), BF16))
    for dil in DILATIONS:
        rm = (pl.BlockSpec((dil, tt // dil, LANES), lambda k: (0, k // per, 0)),
              jax.ShapeDtypeStruct((dil, s // dil, LANES), F32))
        tr = (pl.BlockSpec((dil, NH, tt // dil), lambda k: (0, 0, k // per)),
              jax.ShapeDtypeStruct((dil, NH, s // dil), F32))
        group = [rm, tr, rm, tr] if dil > 1 else [tr, tr]
        out_specs += [sp for sp, _ in group]
        out_shape += [sh for _, sh in group]
    outs = list(pl.pallas_call(
        body, name="grad_small_weights_and_stats", grid=(nk,),
        in_specs=[pl.BlockSpec((tk, t.shape[1]), lambda k: (k, 0)) for pair in pairs for t in pair]
                 + [pl.BlockSpec((tk, CB), lambda k: (k, 0))]
                 + [pl.BlockSpec((tt, LANES), lambda k: (k // per, 0))] * 2,
        out_specs=out_specs, out_shape=out_shape,
        scratch_shapes=[pltpu.VMEM(sh, F32) for sh in shapes]
                       + [pltpu.VMEM((CB // LANES, tk, LANES), F32), pltpu.VMEM((1, tt, LANES), F32)],
        compiler_params=_cp(("arbitrary",), 56))(*[t for pair in pairs for t in pair], da, lc, dc))
    grads_w, daps, outs = outs[:n], outs[n:n + nda], outs[n + nda:]
    res = []
    for dil in DILATIONS:
        flat_t = lambda t, dil=dil: t.reshape(dil * NH, s // dil)
        if dil == 1:
            lt, dt = outs.pop(0), outs.pop(0)
            res.append((da, lc, dc, flat_t(lt), flat_t(dt)))
        else:
            lcp, lt, dcp, dt = (outs.pop(0) for _ in range(4))
            res.append((daps.pop(0).reshape(s, CB), lcp.reshape(s, LANES), dcp.reshape(s, LANES),
                        flat_t(lt), flat_t(dt)))
    return grads_w, res


def qkv_grads_to_dproj(dproj, proj, grads, qw8, kw8, same, tm):
    s = dproj.shape[0]
    ni = s // tm
    flat = [(t.reshape(d, s // d, CB), d, kind, 3 * kind + g)
            for g, d in enumerate(DILATIONS) for kind, t in enumerate(grads[g])]
    nf = len(flat)
    nraw = 2 * len(DILATIONS)

    def body(*refs):
        dp_hbm, raws, ins = refs[nraw + nf + 4], refs[1:1 + nraw], refs[1 + nraw:1 + nraw + nf]
        qw_ref, kw_ref, same_ref = refs[1 + nraw + nf:4 + nraw + nf]
        gw_ref, stage, buf, sems = refs[5 + nraw + nf:]
        i = pl.program_id(0)
        slot = i % 2

        def slab(step, sl):
            return pltpu.make_async_copy(
                stage.at[sl], dp_hbm.at[pl.ds(pl.multiple_of(step * tm, tm), tm), pl.ds(CB_Q * CB, 9 * CB)],
                sems.at[sl])

        @pl.when(i == 0)
        def _():
            gw_ref[...] = jnp.zeros_like(gw_ref)

        @pl.when(i >= 2)
        def _():
            slab(i - 2, slot).wait()

        for ref, (_, d, kind, jj) in zip(ins, flat):
            cols = slice(jj * CB, (jj + 1) * CB)
            dn = _from_residue_major(ref, buf, d)
            if kind == 2:
                stage[slot, :, cols] = dn.astype(BF16)
                continue
            t = raws[jj][...].astype(F32)
            r = lax.rsqrt(_head_sum(t * t, same_ref) * (1.0 / HD) + EPS)
            xh = t * r
            gw_ref[kind:kind + 1, :] += jnp.sum(dn * xh, axis=0, keepdims=True)
            dxh = dn * (qw_ref if kind == 0 else kw_ref)[...]
            mean = _head_sum(dxh * xh, same_ref) * (1.0 / HD)
            stage[slot, :, cols] = (r * (dxh - xh * mean)).astype(BF16)
        slab(i, slot).start()

        @pl.when(i == ni - 1)
        def _():
            slab(i - 1, 1 - slot).wait()
            slab(i, slot).wait()

    full = lambda a: pl.BlockSpec(a.shape, lambda i: (0, 0))
    any_spec = pl.BlockSpec(memory_space=pl.ANY)
    return pl.pallas_call(
        body, name="qkv_grads_to_dproj", grid=(ni,),
        in_specs=[any_spec] + [pl.BlockSpec((tm, CB), lambda i, jb=jb: (i, CB_Q + jb)) for jb in range(nraw)]
                 + [pl.BlockSpec((d, tm // d, CB), lambda i: (0, i, 0)) for _, d, _, _ in flat]
                 + [full(qw8), full(kw8), full(same)],
        out_specs=[any_spec, pl.BlockSpec((8, CB), lambda i: (0, 0))],
        out_shape=[jax.ShapeDtypeStruct((s, NIN), BF16), jax.ShapeDtypeStruct((8, CB), F32)],
        input_output_aliases={0: 0},
        scratch_shapes=[pltpu.VMEM((2, tm, 9 * CB), BF16), pltpu.VMEM((CB // LANES, tm, LANES), F32),
                        pltpu.SemaphoreType.DMA((2,))],
        compiler_params=_cp(("arbitrary",)))(
            dproj, *([proj] * nraw), *[t for t, _, _, _ in flat], qw8, kw8, same)


def _lane_lo():
    return lax.broadcasted_iota(jnp.int32, (1, 2 * HD), 1) < HD


def _stack_heads(t, lo):
    zero = jnp.zeros_like(t)
    return jnp.concatenate([jnp.where(lo, t, zero), jnp.where(lo, zero, t)], axis=0)


def _masks(other_ok):
    qi = lax.broadcasted_iota(jnp.int32, (QB, QB), 0)
    kj = lax.broadcasted_iota(jnp.int32, (QB, QB), 1)
    return (kj >= qi) & other_ok, kj <= qi


ATTN_SUB = 4


def attn_fwd(srcs):
    s = srcs[0][0][0].shape[0]
    ng = len(DILATIONS)

    def body(*refs):
        ins, outs, bufs = refs[:5 * ng], refs[5 * ng:7 * ng], refs[7 * ng:]
        step = pl.program_id(0)
        lo = _lane_lo()
        head_lane = lax.broadcasted_iota(jnp.int32, (1, LANES), 1)
        for g, dil in enumerate(DILATIONS):
            nb = s // dil // QB
            q_ref, kp_ref, k_ref, vp_ref, v_ref = ins[5 * g:5 * g + 5]
            (o_ref, l_ref), (kbuf, vbuf) = outs[2 * g:2 * g + 2], bufs[2 * g:2 * g + 2]
            kbuf[0:QB], kbuf[QB:] = kp_ref[...], k_ref[...]
            vbuf[0:QB], vbuf[QB:] = vp_ref[...], v_ref[...]
            for j in range(ATTN_SUB):
                rows, krows = slice(j * QB, (j + 1) * QB), slice(j * QB, (j + 2) * QB)
                m_prev, m_cur = _masks((step * ATTN_SUB + j) % nb > 0)
                mask = jnp.concatenate([m_prev, m_cur], axis=1)
                mask = jnp.concatenate([mask, mask], axis=0)
                lses = jnp.zeros((QB, LANES), F32)
                for i in range(NH // 2):
                    sl = slice(2 * HD * i, 2 * HD * (i + 1))
                    qs, ks, vv = q_ref[rows, sl], kbuf[krows, sl], vbuf[krows, sl]
                    sc = lax.dot_general(_stack_heads(qs, lo), ks, NT, preferred_element_type=F32)
                    sc = jnp.where(mask, sc, NEG)
                    mx = jnp.max(sc, axis=-1, keepdims=True)
                    p = jnp.exp(sc - mx)
                    den = jnp.sum(p, axis=-1, keepdims=True)
                    o = jnp.dot(p.astype(BF16), vv, preferred_element_type=F32) * (1.0 / den)
                    lse = mx + jnp.log(den)
                    o_ref[rows, sl] = jnp.where(lo, o[:QB], o[QB:]).astype(BF16)
                    lses = jnp.where(head_lane == 2 * i, lse[:QB], jnp.where(head_lane == 2 * i + 1, lse[QB:], lses))
                l_ref[rows, :] = lses

    main = lambda cb, w=CB: pl.BlockSpec((ATTN_SUB * QB, w), lambda st: (st, cb))
    prev = lambda cb: pl.BlockSpec((QB, CB), lambda st: (jnp.maximum(ATTN_SUB * st - 1, 0), cb))
    in_specs, args = [], []
    for q_src, k_src, v_src in srcs:
        in_specs += [main(q_src[1]), prev(k_src[1]), main(k_src[1]), prev(v_src[1]), main(v_src[1])]
        args += [q_src[0], k_src[0], k_src[0], v_src[0], v_src[0]]
    outs = pl.pallas_call(
        body, name="attn_fwd", grid=(s // (ATTN_SUB * QB),),
        in_specs=in_specs, out_specs=[main(0), main(0, LANES)] * ng,
        out_shape=[jax.ShapeDtypeStruct((s, CB), BF16), jax.ShapeDtypeStruct((s, LANES), F32)] * ng,
        scratch_shapes=[pltpu.VMEM(((ATTN_SUB + 1) * QB, CB), BF16)] * (2 * ng),
        compiler_params=_cp(("parallel",)))(*args)
    return outs[0::2], outs[1::2]


def attn_bwd(srcs, stats):
    s = srcs[0][0][0].shape[0]
    ng = len(DILATIONS)
    sub = ATTN_SUB
    nin, nout, nbuf = 14, 3, 6

    def body(*refs):
        ins, outs, bufs = refs[:nin * ng], refs[nin * ng:(nin + nout) * ng], refs[(nin + nout) * ng:]
        step = pl.program_id(0)
        lo = _lane_lo()
        kj = lax.broadcasted_iota(jnp.int32, (QB, QB), 0)
        qi = lax.broadcasted_iota(jnp.int32, (QB, QB), 1)

        def block(group_refs, nb, j):
            (q_ref, _, k_ref, _, v_ref, _, da_ref, _, lc_ref, dc_ref, _, _, _, _,
             dq_ref, dk_ref, dv_ref, kbuf, vbuf, qbuf, dabuf, lbuf, dbuf) = group_refs
            rows, two = slice(j * QB, (j + 1) * QB), slice(j * QB, (j + 2) * QB)
            place = (step * sub + j) % nb
            m_prev, m_cur = _masks(place > 0)
            qmask = jnp.concatenate([m_prev, m_cur], axis=1)
            qmask = jnp.concatenate([qmask, qmask], axis=0)
            lcols, dcols = lc_ref[rows, :], dc_ref[rows, :]
            kmask = jnp.concatenate([kj <= qi, (kj >= qi) & (place < nb - 1)], axis=1)
            kmask = jnp.concatenate([kmask, kmask], axis=1)
            lrow = jnp.concatenate([lbuf[j], lbuf[j + 1]], axis=1)
            drow = jnp.concatenate([dbuf[j], dbuf[j + 1]], axis=1)
            for i in range(NH // 2):
                sl = slice(2 * HD * i, 2 * HD * (i + 1))
                col_pair = lambda t: jnp.concatenate([t[:, 2 * i:2 * i + 1], t[:, 2 * i + 1:2 * i + 2]], axis=0)
                row_pair = lambda t: jnp.concatenate([t[2 * i:2 * i + 1, :], t[2 * i + 1:2 * i + 2, :]], axis=1)
                ks2, vv2 = kbuf[two, sl], vbuf[two, sl]
                sc = lax.dot_general(_stack_heads(q_ref[rows, sl], lo), ks2, NT, preferred_element_type=F32)
                p = jnp.exp(jnp.where(qmask, sc, NEG) - col_pair(lcols))
                dp = lax.dot_general(_stack_heads(da_ref[rows, sl], lo), vv2, NT, preferred_element_type=F32)
                ds = p * (dp - col_pair(dcols))
                dq = jnp.dot(ds.astype(BF16), ks2, preferred_element_type=F32)
                dq_ref[rows, sl] = (jnp.where(lo, dq[:QB], dq[QB:]) * (HD ** -0.5)).astype(BF16)

                q2, da2 = _stack_heads(qbuf[two, sl], lo), _stack_heads(dabuf[two, sl], lo)
                ks, vv = k_ref[rows, sl], v_ref[rows, sl]
                sct = lax.dot_general(ks, q2, NT, preferred_element_type=F32)
                pt = jnp.exp(jnp.where(kmask, sct, NEG) - row_pair(lrow))
                dpt = lax.dot_general(vv, da2, NT, preferred_element_type=F32)
                dst = pt * (dpt - row_pair(drow))
                dv_ref[rows, sl] = jnp.dot(pt.astype(BF16), da2, preferred_element_type=F32).astype(BF16)
                dk_ref[rows, sl] = jnp.dot(dst.astype(BF16), q2, preferred_element_type=F32).astype(BF16)

        for g, dil in enumerate(DILATIONS):
            group_refs = (ins[nin * g:nin * (g + 1)] + outs[nout * g:nout * (g + 1)] + bufs[nbuf * g:nbuf * (g + 1)])
            (q_ref, kp_ref, k_ref, vp_ref, v_ref, qn_ref, da_ref, dan_ref, _, _, l_ref, ln_ref, d_ref, dn_ref,
             _, _, _, kbuf, vbuf, qbuf, dabuf, lbuf, dbuf) = group_refs
            kbuf[0:QB], kbuf[QB:] = kp_ref[...], k_ref[...]
            vbuf[0:QB], vbuf[QB:] = vp_ref[...], v_ref[...]
            qbuf[0:sub * QB], qbuf[sub * QB:] = q_ref[...], qn_ref[...]
            dabuf[0:sub * QB], dabuf[sub * QB:] = da_ref[...], dan_ref[...]
            for c in range(sub):
                lbuf[c], dbuf[c] = l_ref[:, c * QB:(c + 1) * QB], d_ref[:, c * QB:(c + 1) * QB]
            lbuf[sub], dbuf[sub] = ln_ref[...], dn_ref[...]
            for j in range(sub):
                block(group_refs, s // dil // QB, j)

    last = s // QB - 1
    main = lambda cb, w=CB: pl.BlockSpec((sub * QB, w), lambda st: (st, cb))
    prev = lambda cb: pl.BlockSpec((QB, CB), lambda st: (jnp.maximum(sub * st - 1, 0), cb))
    nxt = lambda cb: pl.BlockSpec((QB, CB), lambda st: (jnp.minimum(sub * (st + 1), last), cb))
    in_specs, args = [], []
    for (q_src, k_src, v_src), (da, lc, dc, lt, dt), dil in zip(srcs, stats, DILATIONS):
        nb = s // dil // QB
        t_main = pl.BlockSpec((NH, sub * QB), lambda st, nb=nb: (sub * st // nb, (sub * st % nb) // sub))
        t_nxt = pl.BlockSpec((NH, QB), lambda st, nb=nb: (sub * st // nb, jnp.minimum(sub * st % nb + sub, nb - 1)))
        in_specs += [main(q_src[1]), prev(k_src[1]), main(k_src[1]), prev(v_src[1]), main(v_src[1]), nxt(q_src[1]),
                     main(0), nxt(0), main(0, LANES), main(0, LANES), t_main, t_nxt, t_main, t_nxt]
        args += [q_src[0], k_src[0], k_src[0], v_src[0], v_src[0], q_src[0], da, da, lc, dc, lt, lt, dt, dt]
    out = jax.ShapeDtypeStruct((s, CB), BF16)
    big = pltpu.VMEM(((sub + 1) * QB, CB), BF16)
    outs = pl.pallas_call(
        body, name="attn_bwd", grid=(s // (sub * QB),),
        in_specs=in_specs, out_specs=[main(0)] * (nout * ng), out_shape=[out] * (nout * ng),
        scratch_shapes=([big] * 4 + [pltpu.VMEM((sub + 1, NH, QB), F32)] * 2) * ng,
        compiler_params=_cp(("parallel",), 56))(*args)
    return [tuple(outs[nout * g:nout * (g + 1)]) for g in range(ng)]


def _conv_taps(u, u_prev, first):
    tm = u.shape[0]
    row = lax.broadcasted_iota(jnp.int32, (tm, 1), 0)
    up = jnp.where(first, 0.0, u_prev)
    u1 = jnp.where(row == 0, up[HALO - 1:HALO, :], pltpu.roll(u, 1, 0))
    u2 = jnp.where(row == 0, up[HALO - 2:HALO - 1, :],
                   jnp.where(row == 1, up[HALO - 1:HALO, :], pltpu.roll(u, 2, 0)))
    return u1, u2


def mid_fwd(proj, o_g, lse_g, conv_w, expand, tm):
    s = proj.shape[0]
    hb = tm // HALO

    def body(ba_ref, ca_ref, xa_ref, za_ref, cah_ref, xah_ref, zb_ref,
             o0, o1, o2, l0, l1, l2, w_ref, exp_ref, ya_ref, yb_ref, at_ref, lc_ref, buf_o, buf_l):
        first = pl.program_id(0) == 0
        u = ca_ref[...].astype(F32) * xa_ref[...].astype(F32)
        u1, u2 = _conv_taps(u, cah_ref[...].astype(F32) * xah_ref[...].astype(F32), first)
        conv = w_ref[0:1, :] * u2 + w_ref[1:2, :] * u1 + w_ref[2:3, :] * u
        ya_ref[...] = (ba_ref[...].astype(F32) * conv * _silu(za_ref[...].astype(F32))).astype(BF16)
        ls = [_from_residue_major(l, buf_l.at[g], d) for g, (l, d) in enumerate(zip((l0, l1, l2), DILATIONS))]
        mx = jnp.maximum(jnp.maximum(ls[0], ls[1]), ls[2])
        es = [jnp.exp(l - mx) for l in ls]
        den = es[0] + es[1] + es[2]
        attn = jnp.zeros((tm, CB), F32)
        for e, o, d in zip(es, (o0, o1, o2), DILATIONS):
            attn = attn + _dot_hilo(e / den, exp_ref) * _from_residue_major(o, buf_o, d)
        at_ref[...] = attn
        lc_ref[...] = mx + jnp.log(den)
        yb_ref[...] = (attn * _silu(zb_ref[...].astype(F32))).astype(BF16)

    col = lambda j: pl.BlockSpec((tm, D), lambda i: (i, j))
    halo = lambda j: pl.BlockSpec((HALO, D), lambda i: (jnp.maximum(i * hb - 1, 0), j))
    loc = lambda w: pl.BlockSpec((tm, w), lambda i: (i, 0))
    rm = lambda w: [pl.BlockSpec((d, tm // d, w), lambda i: (0, i, 0)) for d in DILATIONS]
    rm_view = lambda ts, w: [t.reshape(d, s // d, w) for t, d in zip(ts, DILATIONS)]
    return pl.pallas_call(
        body, name="mid_fwd", grid=(s // tm,),
        in_specs=[col(0), col(1), col(2), col(3), halo(1), halo(2),
                  pl.BlockSpec((tm, CB), lambda i: (i, CB_ZB))] + rm(CB) + rm(LANES)
                 + [pl.BlockSpec((3, D), lambda i: (0, 0)), pl.BlockSpec(expand.shape, lambda i: (0, 0))],
        out_specs=[loc(D), loc(CB), loc(CB), loc(LANES)],
        out_shape=[jax.ShapeDtypeStruct((s, D), BF16), jax.ShapeDtypeStruct((s, CB), BF16),
                   jax.ShapeDtypeStruct((s, CB), F32), jax.ShapeDtypeStruct((s, LANES), F32)],
        scratch_shapes=[pltpu.VMEM((CB // LANES, tm, LANES), F32), pltpu.VMEM((3, 1, tm, LANES), F32)],
        compiler_params=_cp(("parallel",)))(
            proj, proj, proj, proj, proj, proj, proj, *rm_view(o_g, CB), *rm_view(lse_g, LANES), conv_w, expand)


def tail(proj, ya, yb, attn, x, target, gate, pa_w, pb_w, wo_w, total, conv_w, tm):
    s = proj.shape[0]
    ni = s // tm
    hb = tm // HALO
    nlate = NIN - CB_ZB * CB
    nearly = 4 * D

    def body(ya_ref, yb_ref, ga_ref, gb_ref, zb_ref, at_ref, x_ref, t_ref, gate_ref, pa_ref, pb_ref, wo_ref,
             tot_ref, ba_ref, ca_ref, xa_ref, za_ref, cah_ref, xah_ref, cw_ref,
             dp_hbm, dy_ref, da_ref, dc_ref, mg_ref, do_ref, dpa_ref, dpb_ref, st_ref, gwc_ref,
             stage, dconv_next, sems):
        step = pl.program_id(0)
        i = ni - 1 - step
        slot = step % 2

        def slabs(at_step, sl):
            rows = pl.ds(pl.multiple_of((ni - 1 - at_step) * tm, tm), tm)
            return (pltpu.make_async_copy(stage.at[sl, :, 0:nearly], dp_hbm.at[rows, pl.ds(0, nearly)],
                                          sems.at[sl, 0]),
                    pltpu.make_async_copy(stage.at[sl, :, nearly:], dp_hbm.at[rows, pl.ds(CB_ZB * CB, nlate)],
                                          sems.at[sl, 1]))

        @pl.when(step == 0)
        def _():
            st_ref[...] = jnp.zeros_like(st_ref)
            gwc_ref[...] = jnp.zeros_like(gwc_ref)
            dconv_next[...] = jnp.zeros_like(dconv_next)

        @pl.when(step >= 2)
        def _():
            for cp in slabs(step - 2, slot):
                cp.wait()

        gate_v = gate_ref[...]
        pa = jnp.dot(ya_ref[...], pa_ref[...], preferred_element_type=F32)
        pb = jnp.dot(yb_ref[...], pb_ref[...], preferred_element_type=F32)
        sa = jax.nn.sigmoid(ga_ref[...].astype(F32))
        sb = jax.nn.sigmoid(gb_ref[...].astype(F32))
        merged = (sa * pa + sb * pb).astype(BF16)
        mg_ref[...] = merged
        out = jnp.dot(merged, wo_ref[...], preferred_element_type=F32)
        err = x_ref[...] + gate_v * out - t_ref[...]
        dy = err * (1.0 / D)
        dy_ref[...] = dy
        st_ref[0:1, :] += jnp.sum(dy * out, axis=0, keepdims=True)
        st_ref[1:2, :] += jnp.sum(err * err, axis=0, keepdims=True)
        dout = (gate_v * dy).astype(BF16)
        do_ref[...] = dout
        dmg = lax.dot_general(dout, wo_ref[...], NT, preferred_element_type=F32)
        dpa = (dmg * sa).astype(BF16)
        dpb = (dmg * sb).astype(BF16)
        dpa_ref[...] = dpa
        dpb_ref[...] = dpb
        late = nearly
        stage[slot, :, late + CB:late + CB + D] = (dmg * pa * sa * (1.0 - sa)).astype(BF16)
        stage[slot, :, late + CB + D:] = (dmg * pb * sb * (1.0 - sb)).astype(BF16)
        dya = lax.dot_general(dpa, pa_ref[...], NT, preferred_element_type=F32)
        dyb = lax.dot_general(dpb, pb_ref[...], NT, preferred_element_type=F32)
        zb = zb_ref[...].astype(F32)
        sg = jax.nn.sigmoid(zb)
        attn_v = at_ref[...]
        dattn = dyb * (zb * sg)
        da_ref[...] = dattn.astype(BF16)
        stage[slot, :, late:late + CB] = (dyb * attn_v * (sg * (1.0 + zb * (1.0 - sg)))).astype(BF16)
        dc_ref[...] = _dot_hilo(dattn * attn_v, tot_ref)

        ba, ca, xa, za = (t[...].astype(F32) for t in (ba_ref, ca_ref, xa_ref, za_ref))
        u = ca * xa
        u1, u2 = _conv_taps(u, cah_ref[...].astype(F32) * xah_ref[...].astype(F32), i == 0)
        w0, w1, w2 = cw_ref[0:1, :], cw_ref[1:2, :], cw_ref[2:3, :]
        conv = w0 * u2 + w1 * u1 + w2 * u
        sga = jax.nn.sigmoid(za)
        sza = za * sga
        dconv = dya * ba * sza
        dcn = dconv_next[...]
        rowi = lax.broadcasted_iota(jnp.int32, (tm, 1), 0)
        d1 = jnp.where(rowi == tm - 1, dcn[0:1, :], pltpu.roll(dconv, tm - 1, 0))
        d2 = jnp.where(rowi == tm - 2, dcn[0:1, :],
                       jnp.where(rowi == tm - 1, dcn[1:2, :], pltpu.roll(dconv, tm - 2, 0)))
        du = w2 * dconv + w1 * d1 + w0 * d2
        stage[slot, :, 0:D] = (dya * conv * sza).astype(BF16)
        stage[slot, :, D:2 * D] = (du * xa).astype(BF16)
        stage[slot, :, 2 * D:3 * D] = (du * ca).astype(BF16)
        stage[slot, :, 3 * D:4 * D] = (dya * ba * conv * (sga * (1.0 + za * (1.0 - sga)))).astype(BF16)
        gwc_ref[0:1, :] += jnp.sum(dconv * u2, axis=0, keepdims=True)
        gwc_ref[1:2, :] += jnp.sum(dconv * u1, axis=0, keepdims=True)
        gwc_ref[2:3, :] += jnp.sum(dconv * u, axis=0, keepdims=True)
        dconv_next[...] = dconv[0:8, :]

        for cp in slabs(step, slot):
            cp.start()

        @pl.when(step == ni - 1)
        def _():
            for cp in slabs(step - 1, 1 - slot) + slabs(step, slot):
                cp.wait()

    rev = lambda st: ni - 1 - st
    row = lambda w: pl.BlockSpec((tm, w), lambda st: (rev(st), 0))
    pcol = lambda w, jb: pl.BlockSpec((tm, w), lambda st: (rev(st), jb))
    halo = lambda jb: pl.BlockSpec((HALO, D), lambda st: (jnp.maximum(rev(st) * hb - 1, 0), jb))
    const = lambda a: pl.BlockSpec(a.shape, lambda st: (0, 0), pipeline_mode=pl.Buffered(1))
    acc = pl.BlockSpec((8, D), lambda st: (0, 0))
    return pl.pallas_call(
        body, name="tail", grid=(ni,),
        in_specs=[row(D), row(CB), pcol(D, 9), pcol(D, 10), pcol(CB, CB_ZB), row(CB), row(D), row(D),
                  pl.BlockSpec((1, D), lambda st: (0, 0)), const(pa_w), const(pb_w), const(wo_w), const(total),
                  pcol(D, 0), pcol(D, 1), pcol(D, 2), pcol(D, 3), halo(1), halo(2),
                  pl.BlockSpec((3, D), lambda st: (0, 0))],
        out_specs=[pl.BlockSpec(memory_space=pl.ANY),
                   row(D), row(CB), row(LANES), row(D), row(D), row(D), row(D), acc, acc],
        out_shape=[jax.ShapeDtypeStruct((s, NIN), BF16), jax.ShapeDtypeStruct((s, D), F32),
                   jax.ShapeDtypeStruct((s, CB), BF16), jax.ShapeDtypeStruct((s, LANES), F32)]
                  + [jax.ShapeDtypeStruct((s, D), BF16)] * 4 + [jax.ShapeDtypeStruct((8, D), F32)] * 2,
        scratch_shapes=[pltpu.VMEM((2, tm, nearly + nlate), BF16), pltpu.VMEM((8, D), F32),
                        pltpu.SemaphoreType.DMA((2, 2))],
        compiler_params=_cp(("arbitrary",), 60))(
            ya, yb, proj, proj, proj, attn, x, target, gate, pa_w, pb_w, wo_w, total,
            proj, proj, proj, proj, proj, proj, conv_w)


def _local_step(x, target, shift, scale, gate, norm_w, conv_w, qw, kw, w_shard, small_shards, me_xyc):
    qw8, kw8 = jnp.tile(qw, (1, NH)), jnp.tile(kw, (1, NH))
    same, total, expand = _head_matrices()
    proj, ht, wg, (pa_g, pb_g, wo_g) = proj_fwd_gather(
        x, norm_w, scale, shift, w_shard, small_shards, gather_order(me_xyc), 1024)
    pa_w, wo_w = pa_g.reshape(D, D), wo_g.reshape(D, D)
    pb_w = pb_g.transpose(1, 0, 2).reshape(CB, D)
    srcs = qkv_prep(proj, qw8, kw8, same, 512)
    o_g, lse_g = attn_fwd(srcs)
    ya, yb, attn, lc = mid_fwd(proj, o_g, lse_g, conv_w, expand, 512)
    dproj, dy, da, dc, merged, dout, dpa, dpb, st_tail, st_conv = tail(
        proj, ya, yb, attn, x, target, gate, pa_w, pb_w, wo_w, total, conv_w, 256)
    (g_wo, g_pa, g_pb), stats = grad_small_weights_and_stats([(merged, dout), (ya, dpa), (yb, dpb)], da, lc, dc, 512)
    grads = attn_bwd(srcs, stats)
    dproj, gw_qk = qkv_grads_to_dproj(dproj, proj, grads, qw8, kw8, same, 512)
    slabs = [g_pa.reshape(NDEV, 128, D), g_pb.reshape(CB, NDEV, 128).transpose(1, 0, 2), g_wo.reshape(NDEV, 128, D)]
    grad_x, st_norm, r_win, (r_pa, r_pb, r_wo) = proj_bwd(
        ht, dproj, wg, slabs, scatter_order(me_xyc), x, dy, norm_w, scale, 1024)
    dmod = jnp.concatenate([st_norm[0:1], st_norm[1:2], st_tail[0:1]], axis=1)
    loss_part = (0.5 / D) * jnp.sum(st_tail[1])
    gw_heads = gw_qk[0:2].reshape(2, NH, HD).sum(axis=1)
    small = dict(dmod=dmod, norm_w=st_norm[2:3], conv_w=st_conv[0:3],
                 q_norm_w=gw_heads[0:1], k_norm_w=gw_heads[1:2], loss=loss_part)
    return grad_x, small, (r_win, r_pa, r_pb, r_wo)


def kernel(x, c, w_ada, b_ada, norm_w, w_in, conv_w, q_norm_w, k_norm_w, w_br_conv, w_br_attn, w_out, loss_target, m_w_ada, m_b_ada, m_norm_w, m_w_in, m_conv_w, m_q_norm_w, m_k_norm_w, m_w_br_conv, m_w_br_attn, m_w_out, v_w_ada, v_b_ada, v_norm_w, v_w_in, v_conv_w, v_q_norm_w, v_k_norm_w, v_w_br_conv, v_w_br_attn, v_w_out):
    me_xyc = (lax.axis_index("x"), lax.axis_index("y"), lax.axis_index("c"))
    me = _dev_index(me_xyc)
    ncol = w_ada.shape[2]

    conv_pad = jnp.zeros((8, 128), F32).at[0:3].set(conv_w[0])
    b_cols = lax.dynamic_slice(b_ada, (0, me * ncol), (1, ncol))
    mod_pieces, c_all, conv_all = ada_fwd(c, conv_pad, w_ada[0], b_cols)
    conv_full = conv_all[:, 0:3].transpose(1, 0, 2).reshape(3, D)
    c_all = c_all.reshape(NDEV, D)
    mod = mod_pieces.reshape(1, 3 * D)
    shift, scale, gate = mod[:, 0:D], mod[:, D:2 * D], mod[:, 2 * D:3 * D]

    grad_x, small, (r_win, r_pa, r_pb, r_wo) = _local_step(
        x[0], loss_target[0], shift, scale, gate, norm_w, conv_full, q_norm_w, k_norm_w,
        w_in[0].astype(BF16), [w_br_conv[0].astype(BF16), w_br_attn[0].astype(BF16), w_out[0].astype(BF16)], me_xyc)

    packed = jnp.concatenate(
        [small["dmod"], small["norm_w"], small["conv_w"].reshape(1, 3 * D), small["q_norm_w"], small["k_norm_w"],
         jnp.full((1, 128), small["loss"], F32)], axis=1)
    packed_all, tot = gather_sum(packed)
    loss = tot[0, 7 * D + 2 * HD]
    dmod_all = packed_all[:, 0, 0:3 * D]
    g_b_ada = tot[:, 0:3 * D]
    g_norm_w = tot[:, 3 * D:4 * D]
    g_conv = lax.dynamic_slice(tot[:, 4 * D:7 * D].reshape(3, D), (0, me * 128), (3, 128))
    g_qn = tot[:, 7 * D:7 * D + HD]
    g_kn = tot[:, 7 * D + HD:7 * D + 2 * HD]
    g_w_ada = ada_bwd(c_all.T, lax.dynamic_slice(dmod_all, (0, me * ncol), (NDEV, ncol)))

    def upd(parts, w, m, v, name, rows):
        shape = w.shape
        w2, m2, v2 = (t.reshape(shape[-2:]) for t in (w, m, v))
        return [t.reshape(shape) for t in adamw(parts, w2, m2, v2, name, rows)]

    res = {"w_in": upd(r_win, w_in, m_w_in, v_w_in, "adamw_w_in", 128)}
    small_params = {"w_ada": (g_w_ada[None], w_ada, m_w_ada, v_w_ada), "b_ada": (g_b_ada[None], b_ada, m_b_ada, v_b_ada),
                    "norm_w": (g_norm_w[None], norm_w, m_norm_w, v_norm_w),
                    "conv_w": (g_conv[None], conv_w, m_conv_w, v_conv_w),
                    "q_norm_w": (g_qn[None], q_norm_w, m_q_norm_w, v_q_norm_w),
                    "k_norm_w": (g_kn[None], k_norm_w, m_k_norm_w, v_k_norm_w),
                    "w_br_conv": (r_pa, w_br_conv, m_w_br_conv, v_w_br_conv),
                    "w_br_attn": (r_pb, w_br_attn, m_w_br_attn, v_w_br_attn),
                    "w_out": (r_wo, w_out, m_w_out, v_w_out)}
    updated = adamw_small([(item[0],) + tuple(t.reshape(t.shape[-2:]) for t in item[1:])
                           for item in small_params.values()])
    for (pname, item), outs4 in zip(small_params.items(), updated):
        res[pname] = [t.reshape(item[1].shape) for t in outs4]
    names = ["w_ada", "b_ada", "norm_w", "w_in", "conv_w", "q_norm_w", "k_norm_w", "w_br_conv", "w_br_attn", "w_out"]
    return (loss, grad_x[None], *[res[n][0] for n in names], *[res[n][1] for n in names],
            *[res[n][2] for n in names], *[res[n][3] for n in names])
```

```python
import jax
import jax.numpy as jnp
from jax import lax
from jax.experimental import pallas as pl
from jax.experimental.pallas import tpu as pltpu

F32, BF16 = jnp.float32, jnp.bfloat16
D = 1024
NIN = 11264
NDEV = 8
SHARD = NIN // NDEV
HD = 64
NH = 8
QB = 128
CB = 512
CB_Q, CB_K, CB_V, CB_ZB = 8, 11, 14, 17
DILATIONS = (1, 4, 16)
EPS = 1e-6
NEG = -1e30
HALO = 16
LANES = 128
MESH = pl.DeviceIdType.MESH

ADAM_LR, ADAM_B1, ADAM_B2, ADAM_EPS, ADAM_WD, ADAM_STEP = 0.001, 0.9, 0.999, 1e-08, 0.01, 10

NT = (((1,), (1,)), ((), ()))
TN = (((0,), (0,)), ((), ()))


def _cp(sem, vmem_mb=48):
    return pltpu.CompilerParams(dimension_semantics=sem, vmem_limit_bytes=vmem_mb << 20)


def _silu(z):
    return z * jax.nn.sigmoid(z)


def _coords():
    return lax.axis_index("x"), lax.axis_index("y"), lax.axis_index("c")


FLIPS = [(fx, fy, fc) for fx in (0, 1) for fy in (0, 1) for fc in (0, 1)][1:]


def gather_sum(vec):
    def body(v_ref, all_ref, sum_ref, send_sems, recv_sems, local_sem):
        me_xyc = _coords()
        me = _dev_index(me_xyc)
        peers = [_flip(me_xyc, f) for f in FLIPS]

        def copy(k, block):
            return pltpu.make_async_remote_copy(
                src_ref=v_ref, dst_ref=all_ref.at[block], send_sem=send_sems.at[k], recv_sem=recv_sems.at[k],
                device_id=peers[k], device_id_type=MESH)

        mine = pltpu.make_async_copy(v_ref, all_ref.at[me], local_sem)
        sends = [copy(k, me) for k in range(7)]
        for cp in [mine] + sends:
            cp.start()
        for k in range(7):
            copy(k, _dev_index(peers[k])).wait_recv()
        mine.wait()
        acc = all_ref[0]
        for b in range(1, NDEV):
            acc = acc + all_ref[b]
        sum_ref[...] = acc
        for cp in sends:
            cp.wait_send()

    return pl.pallas_call(
        body, name="gather_sum",
        out_shape=[jax.ShapeDtypeStruct((NDEV,) + vec.shape, F32), jax.ShapeDtypeStruct(vec.shape, F32)],
        scratch_shapes=[pltpu.SemaphoreType.DMA((7,)), pltpu.SemaphoreType.DMA((7,)), pltpu.SemaphoreType.DMA],
    )(vec)


def _flip(dev, f):
    return tuple(1 - v if b else v for v, b in zip(dev, f))


def _dev_index(dev):
    return 4 * dev[0] + 2 * dev[1] + dev[2]


def _chip_order(x, y, c):
    xor = lambda a, b: a + b - 2 * a * b
    return [(xor(x, 1 - c), xor(y, c)), (xor(x, c), xor(y, 1 - c)), (1 - x, 1 - y)]


def gather_order(me_xyc):
    x, y, c = me_xyc
    chips = _chip_order(x, y, c)
    devs = [(x, y, c), (x, y, 1 - c), (*chips[0], c), (*chips[1], c),
            (*chips[1], 1 - c), (*chips[0], 1 - c), (*chips[2], c), (*chips[2], 1 - c)]
    return jnp.stack([_dev_index(d) for d in devs]).astype(jnp.int32)


def scatter_order(me_xyc):
    devs = [_flip(me_xyc, f) for f in FLIPS] + [me_xyc]
    return jnp.stack([_dev_index(d) for d in devs]).astype(jnp.int32)


def ada_fwd(c, conv_pad, w_ada, b_cols):
    ncol = w_ada.shape[1]

    def body(c_ref, cv_ref, w_ref, b_ref, mod_ref, call_ref, cvall_ref, rows_buf, send_sems, recv_sems, local_sems):
        me_xyc = _coords()
        me = _dev_index(me_xyc)
        peers = [_flip(me_xyc, f) for f in FLIPS]
        pids = [_dev_index(p) for p in peers]

        def copy(a, k, src, dst):
            return pltpu.make_async_remote_copy(src_ref=src, dst_ref=dst, send_sem=send_sems.at[a, k],
                                                recv_sem=recv_sems.at[a, k], device_id=peers[k], device_id_type=MESH)

        own = [pltpu.make_async_copy(c_ref, call_ref.at[me], local_sems.at[0]),
               pltpu.make_async_copy(cv_ref, cvall_ref.at[me], local_sems.at[1])]
        first = [copy(0, k, c_ref, call_ref.at[me]) for k in range(7)]
        first += [copy(1, k, cv_ref, cvall_ref.at[me]) for k in range(7)]
        for cp in own + first:
            cp.start()
        own[0].wait()
        for k in range(7):
            copy(0, k, c_ref, call_ref.at[pids[k]]).wait_recv()
        seq = lax.broadcasted_iota(jnp.int32, (NDEV, 1), 0)
        c_all = jnp.zeros((NDEV, D), F32)
        for p in range(NDEV):
            c_all = jnp.where(seq == p, call_ref[p], c_all)
        mods = jnp.dot(_silu(c_all).astype(BF16), w_ref[...].astype(BF16), preferred_element_type=F32) + b_ref[...]
        for p in range(NDEV):
            rows_buf[p] = mods[p:p + 1, :]
        mine = pltpu.make_async_copy(rows_buf.at[me], mod_ref.at[me], local_sems.at[2])
        second = [copy(2, k, rows_buf.at[pids[k]], mod_ref.at[me]) for k in range(7)]
        for cp in [mine] + second:
            cp.start()
        for k in range(7):
            copy(2, k, rows_buf.at[pids[k]], mod_ref.at[pids[k]]).wait_recv()
            copy(1, k, cv_ref, cvall_ref.at[pids[k]]).wait_recv()
        for cp in first + second:
            cp.wait_send()
        own[1].wait()
        mine.wait()

    return pl.pallas_call(
        body, name="ada_fwd",
        out_shape=[jax.ShapeDtypeStruct((NDEV, 1, ncol), F32), jax.ShapeDtypeStruct((NDEV, 1, D), F32),
                   jax.ShapeDtypeStruct((NDEV,) + conv_pad.shape, F32)],
        scratch_shapes=[pltpu.VMEM((NDEV, 1, ncol), F32), pltpu.SemaphoreType.DMA((3, 7)),
                        pltpu.SemaphoreType.DMA((3, 7)), pltpu.SemaphoreType.DMA((3,))],
    )(c, conv_pad, w_ada, b_cols)


def ada_bwd(c_all_t, dmod_cols):
    def body(c_ref, d_ref, o_ref):
        at = _silu(c_ref[...])
        acc = at[:, 0:1] * d_ref[0:1, :]
        for b in range(1, NDEV):
            acc = acc + at[:, b:b + 1] * d_ref[b:b + 1, :]
        o_ref[...] = acc

    return pl.pallas_call(body, name="ada_bwd",
                          out_shape=jax.ShapeDtypeStruct((D, dmod_cols.shape[1]), F32))(c_all_t, dmod_cols)


def _adamw_update(g, w_ref, m_ref, v_ref, g_ref, d_ref, nm_ref, nv_ref):
    nm = ADAM_B1 * m_ref[...] + (1.0 - ADAM_B1) * g
    nv = ADAM_B2 * v_ref[...] + (1.0 - ADAM_B2) * (g * g)
    g_ref[...] = g
    nm_ref[...] = nm
    nv_ref[...] = nv
    m_hat = nm / (1.0 - ADAM_B1 ** ADAM_STEP)
    v_hat = nv / (1.0 - ADAM_B2 ** ADAM_STEP)
    d_ref[...] = -ADAM_LR * (m_hat / (jnp.sqrt(v_hat) + ADAM_EPS) + ADAM_WD * w_ref[...])


def adamw_small(items):
    n = len(items)

    def body(*refs):
        ins, outs = refs[:4 * n], refs[4 * n:]
        for a in range(n):
            p_ref, w_ref, m_ref, v_ref = ins[4 * a:4 * a + 4]
            g = p_ref[0].astype(F32)
            for b in range(1, p_ref.shape[0]):
                g = g + p_ref[b].astype(F32)
            _adamw_update(g, w_ref, m_ref, v_ref, *outs[4 * a:4 * a + 4])

    out = pl.pallas_call(
        body, name="adamw_small",
        out_shape=[jax.ShapeDtypeStruct(it[1].shape, F32) for it in items for _ in range(4)],
        compiler_params=pltpu.CompilerParams(vmem_limit_bytes=48 << 20))(*[t for it in items for t in it])
    return [out[4 * a:4 * a + 4] for a in range(n)]


def adamw(parts, w, m, v, name, rows):
    n, r, ccols = parts.shape

    def body(p_ref, w_ref, m_ref, v_ref, g_ref, d_ref, nm_ref, nv_ref):
        g = p_ref[0].astype(F32)
        for b in range(1, n):
            g = g + p_ref[b].astype(F32)
        _adamw_update(g, w_ref, m_ref, v_ref, g_ref, d_ref, nm_ref, nv_ref)

    blk = pl.BlockSpec((rows, ccols), lambda i: (i, 0))
    out = jax.ShapeDtypeStruct((r, ccols), F32)
    return pl.pallas_call(
        body, name=name, grid=(r // rows,),
        in_specs=[pl.BlockSpec((n, rows, ccols), lambda i: (0, i, 0)), blk, blk, blk],
        out_specs=[blk] * 4, out_shape=[out] * 4, compiler_params=_cp(("parallel",)))(parts, w, m, v)


def proj_fwd_gather(x, nw, scale, shift, w_shard, extras, order, tm):
    s = x.shape[0]
    ni = s // tm
    n = 1 + len(extras)
    mid = ni - 2

    def body(order_ref, x_ref, nw_ref, sc_ref, sh_ref, *refs):
        ins, o_ref, ht_ref, outs = refs[:n], refs[n], refs[n + 1], refs[n + 2:2 * n + 2]
        h_all, wbuf, send_sems, recv_sems, local_sems, load_sems = refs[2 * n + 2:]
        jj, i = pl.program_id(0), pl.program_id(1)
        x, y, c = _coords()
        me, sibling = (x, y, c), (x, y, 1 - c)
        chips = _chip_order(x, y, c)
        relayed = [(*chips[1], 1 - c), (*chips[0], 1 - c), (*chips[2], 1 - c)]

        def slot(a, dev):
            return outs[a].at[_dev_index(dev)]

        def copy(a, k, block, to, src=None):
            return pltpu.make_async_remote_copy(
                src_ref=slot(a, block) if src is None else src, dst_ref=slot(a, block),
                send_sem=send_sems.at[a, k], recv_sem=recv_sems.at[a, k], device_id=to, device_id_type=MESH)

        mine = [pltpu.make_async_copy(ins[a], slot(a, me), local_sems.at[a]) for a in range(n)]
        to_sibling = [copy(a, 0, me, sibling, src=ins[a]) for a in range(n)]
        to_chip = [[copy(a, 1 + j, me, (*chips[j], c), src=ins[a]) for a in range(n)] for j in range(2)]
        onward = [copy(a, 3, (*chips[1], c), (*chips[0], c)) for a in range(n)]
        passed = [[copy(a, 4 + j, (*ch, c), sibling) for a in range(n)] for j, ch in enumerate(chips)]
        sends = lambda a: [to_sibling[a], to_chip[0][a], to_chip[1][a], onward[a]] + [passed[j][a] for j in range(3)]

        def arrived(a, j):
            copy(a, 1 + j, (*chips[j], c), me).wait_recv()

        def load(row):
            return pltpu.make_async_copy(outs[0].at[order_ref[row]], wbuf.at[row % 2], load_sems.at[row % 2])

        @pl.when((jj == 0) & (i == 0))
        def _():
            for cp in mine:
                cp.start()
            to_sibling[0].start()
            to_chip[0][0].start()
            pltpu.make_async_copy(ins[0], wbuf.at[0], load_sems.at[0]).start()

        @pl.when((jj == 1) & (i == 0))
        def _():
            to_chip[1][0].start()

        @pl.when((jj == 4) & (i == 0))
        def _():
            for a in range(1, n):
                to_sibling[a].start()
                to_chip[0][a].start()
                to_chip[1][a].start()

        direct = {2: 0, 3: 1, 6: 2}
        relay = {4: 0, 5: 1, 7: 2}

        @pl.when((jj == 0) & (i == mid))
        def _():
            copy(0, 0, sibling, me).wait_recv()

        for row, j in direct.items():
            @pl.when((jj == row - 1) & (i == mid))
            def _(j=j):
                arrived(0, j)
                passed[j][0].start()
                if j == 1:
                    onward[0].start()

        for row, j in relay.items():
            @pl.when((jj == row - 1) & (i == mid))
            def _(j=j):
                copy(0, 4 + j, relayed[j], me).wait_recv()

        @pl.when((jj == NDEV - 1) & (i == 0))
        def _():
            for a in range(1, n):
                arrived(a, 1)
                onward[a].start()
                passed[1][a].start()
                arrived(a, 0)
                passed[0][a].start()

        @pl.when((jj < NDEV - 1) & (i == mid))
        def _():
            load(jj + 1).start()

        @pl.when(i == 0)
        def _():
            load(jj).wait()

        @pl.when(jj == 0)
        def _():
            xf = x_ref[...]
            r = lax.rsqrt(jnp.mean(xf * xf, axis=-1, keepdims=True) + EPS)
            h = (xf * r * nw_ref[...]) * (1.0 + sc_ref[...]) + sh_ref[...]
            h_all[i] = h.astype(BF16)
            ht_ref[...] = h.T.astype(BF16)

        o_ref[...] = jnp.dot(h_all[i], wbuf[jj % 2], preferred_element_type=F32).astype(BF16)

        @pl.when((jj == NDEV - 1) & (i == ni - 1))
        def _():
            for a in range(1, n):
                arrived(a, 2)
                passed[2][a].start()
            for a in range(1, n):
                copy(a, 0, sibling, me).wait_recv()
                for j in range(3):
                    copy(a, 4 + j, relayed[j], me).wait_recv()
            for a in range(n):
                mine[a].wait()
                for cp in sends(a):
                    cp.wait_send()

    any_spec = pl.BlockSpec(memory_space=pl.ANY)
    vec = pl.BlockSpec((1, D), lambda jj, i, o: (0, 0))
    outs = pl.pallas_call(
        body, name="proj_fwd_gather",
        grid_spec=pltpu.PrefetchScalarGridSpec(
            num_scalar_prefetch=1, grid=(NDEV, ni),
            in_specs=[pl.BlockSpec((tm, D), lambda jj, i, o: (jnp.where(jj == 0, i, ni - 1), 0))] + [vec] * 3
                     + [any_spec] * n,
            out_specs=[pl.BlockSpec((tm, SHARD), lambda jj, i, o: (i, o[jj])),
                       pl.BlockSpec((D, tm), lambda jj, i, o: (0, jnp.where(jj == 0, i, ni - 1)))]
                      + [any_spec] * n,
            scratch_shapes=[pltpu.VMEM((ni, tm, D), BF16), pltpu.VMEM((2, D, SHARD), BF16),
                            pltpu.SemaphoreType.DMA((n, 7)), pltpu.SemaphoreType.DMA((n, 7)),
                            pltpu.SemaphoreType.DMA((n,)), pltpu.SemaphoreType.DMA((2,))]),
        out_shape=[jax.ShapeDtypeStruct((s, NIN), BF16), jax.ShapeDtypeStruct((D, s), BF16),
                   jax.ShapeDtypeStruct((NDEV, D, SHARD), BF16)]
                  + [jax.ShapeDtypeStruct((NDEV,) + e.shape, e.dtype) for e in extras],
        compiler_params=_cp(("arbitrary", "arbitrary"), 56))(order, x, nw, scale, shift, w_shard, *extras)
    return outs[0], outs[1], outs[2], outs[3:]


def proj_bwd(ht, dproj, wg, smalls, order, x, dy, nw, scale, tt):
    s = dproj.shape[0]
    nk = s // tt
    n = len(smalls)
    rows_per_step = tt // nk
    last = 2 * NDEV

    def body(order_ref, ht_ref, dp_ref, w_ref, x_ref, dy_ref, nw_ref, sc_ref, *rest):
        small_in = rest[:n]
        gx_ref, st_ref, gw_ref, rwin_ref = rest[n:n + 4]
        small_out = rest[n + 4:2 * n + 4]
        acc, stage, dh, send_sems, recv_sems, local_sems, stage_sems = rest[2 * n + 4:]
        t, k = pl.program_id(0), pl.program_id(1)
        me_xyc = _coords()
        me = _dev_index(me_xyc)
        peers = [_flip(me_xyc, f) for f in FLIPS]

        def exchange(a, kf, src_arr, dst_arr):
            pid = _dev_index(peers[kf])
            mk = lambda dst: pltpu.make_async_remote_copy(
                src_ref=src_arr.at[pid], dst_ref=dst, send_sem=send_sems.at[a, kf], recv_sem=recv_sems.at[a, kf],
                device_id=peers[kf], device_id_type=MESH)
            return mk(dst_arr.at[me]), mk(dst_arr.at[pid])

        small_pairs = [exchange(1 + a, kf, small_in[a], small_out[a]) for kf in range(7) for a in range(n)]
        small_own = [pltpu.make_async_copy(small_in[a].at[me], small_out[a].at[me], local_sems.at[1 + a])
                     for a in range(n)]
        win_pairs = [exchange(0, kf, gw_ref, rwin_ref) for kf in range(7)]
        win_own = pltpu.make_async_copy(gw_ref.at[me], rwin_ref.at[me], local_sems.at[0])

        def to_hbm(jj):
            slab = me if jj == 7 else _dev_index(peers[jj])
            return pltpu.make_async_copy(stage.at[jj % 2], gw_ref.at[slab], stage_sems.at[jj % 2])

        @pl.when((t == 0) & (k == 0))
        def _():
            for cp in small_own:
                cp.start()
            for send, _ in small_pairs:
                send.start()

        @pl.when(t < NDEV)
        def _():
            p = jnp.dot(ht_ref[...], dp_ref[...], preferred_element_type=F32)

            @pl.when(k == 0)
            def _():
                acc[...] = p

            @pl.when(k > 0)
            def _():
                acc[...] += p

        for jj in range(NDEV):
            @pl.when((t == jj) & (k == nk - 1))
            def _(jj=jj):
                stage[jj % 2] = acc[...].astype(BF16)
                to_hbm(jj).start()

            @pl.when((t == jj + 1) & (k == 1))
            def _(jj=jj):
                to_hbm(jj).wait()
                if jj < 7:
                    win_pairs[jj][0].start()
                else:
                    win_own.start()

        def matmul_step():
            p = lax.dot_general(dp_ref[...], w_ref[...], NT, preferred_element_type=F32)
            slot = t % 2
            dh[slot] = jnp.where(k == 0, p, dh[slot] + p)

        def norm_step():
            g = dh.at[(t + 1) % 2][pl.ds(pl.multiple_of(k * rows_per_step, rows_per_step), rows_per_step), :]
            xf = x_ref[...]
            r = lax.rsqrt(jnp.mean(xf * xf, axis=-1, keepdims=True) + EPS)
            xh = xf * r
            dn = g * (1.0 + sc_ref[...])
            dxh = dn * nw_ref[...]
            gx_ref[...] = dy_ref[...] + r * (dxh - xh * jnp.mean(dxh * xh, axis=-1, keepdims=True))
            st_ref[0:1, :] += jnp.sum(g, axis=0, keepdims=True)
            st_ref[1:2, :] += jnp.sum(g * xh * nw_ref[...], axis=0, keepdims=True)
            st_ref[2:3, :] += jnp.sum(dn * xh, axis=0, keepdims=True)

        @pl.when((t == 0) & (k == 0))
        def _():
            st_ref[...] = jnp.zeros_like(st_ref)

        @pl.when(t == NDEV)
        def _():
            matmul_step()

        @pl.when((t > NDEV) & (t < last))
        def _():
            matmul_step()
            norm_step()

        @pl.when(t == last)
        def _():
            norm_step()

        @pl.when((t == last) & (k == nk - 1))
        def _():
            for _, recv in win_pairs + small_pairs:
                recv.wait_recv()
            for send, _ in win_pairs + small_pairs:
                send.wait_send()
            win_own.wait()
            for cp in small_own:
                cp.wait()

    any_spec = pl.BlockSpec(memory_space=pl.ANY)
    first = lambda t: t < NDEV
    slab = lambda t, k: jnp.where(t == last, NDEV - 1, k)
    chunk = pl.BlockSpec((rows_per_step, D), lambda t, k, o: (jnp.maximum((t - NDEV - 1) * nk + k, 0), 0))
    vec = pl.BlockSpec((1, D), lambda t, k, o: (0, 0))
    outs = pl.pallas_call(
        body, name="proj_bwd",
        grid_spec=pltpu.PrefetchScalarGridSpec(
            num_scalar_prefetch=1, grid=(last + 1, nk),
            in_specs=[pl.BlockSpec((D, tt), lambda t, k, o: (0, jnp.where(first(t), k, nk - 1))),
                      pl.BlockSpec((tt, SHARD), lambda t, k, o: (jnp.where(first(t), k, jnp.minimum(t, last - 1) - NDEV),
                                                                 jnp.where(first(t), o[jnp.minimum(t, NDEV - 1)],
                                                                           slab(t, k)))),
                      pl.BlockSpec((None, D, SHARD), lambda t, k, o: (jnp.where(first(t), 0, slab(t, k)), 0, 0)),
                      chunk, chunk, vec, vec]
                     + [any_spec] * n,
            out_specs=[chunk, pl.BlockSpec((8, D), lambda t, k, o: (0, 0))] + [any_spec] * (2 + n),
            scratch_shapes=[pltpu.VMEM((D, SHARD), F32), pltpu.VMEM((2, D, SHARD), BF16),
                            pltpu.VMEM((2, tt, D), F32),
                            pltpu.SemaphoreType.DMA((1 + n, 7)), pltpu.SemaphoreType.DMA((1 + n, 7)),
                            pltpu.SemaphoreType.DMA((1 + n,)), pltpu.SemaphoreType.DMA((2,))]),
        out_shape=[jax.ShapeDtypeStruct((s, D), F32), jax.ShapeDtypeStruct((8, D), F32),
                   jax.ShapeDtypeStruct((NDEV, D, SHARD), BF16), jax.ShapeDtypeStruct((NDEV, D, SHARD), BF16)]
                  + [jax.ShapeDtypeStruct(a.shape, a.dtype) for a in smalls],
        compiler_params=_cp(("arbitrary", "arbitrary"), 56))(order, ht, dproj, wg, x, dy, nw, scale, *smalls)
    return outs[0], outs[1], outs[3], outs[4:]


def _head_matrices():
    lane = lax.broadcasted_iota(jnp.int32, (CB, CB), 0)
    col = lax.broadcasted_iota(jnp.int32, (CB, CB), 1)
    same = (lane // HD == col // HD).astype(BF16)
    lane_c = lax.broadcasted_iota(jnp.int32, (CB, LANES), 0)
    col_c = lax.broadcasted_iota(jnp.int32, (CB, LANES), 1)
    total = (lane_c // HD == col_c).astype(BF16)
    lane_e = lax.broadcasted_iota(jnp.int32, (LANES, CB), 0)
    col_e = lax.broadcasted_iota(jnp.int32, (LANES, CB), 1)
    expand = (lane_e == col_e // HD).astype(BF16)
    return same, total, expand


def _head_sum(x, m_ref):
    return jnp.dot(x.astype(BF16), m_ref[...], preferred_element_type=F32)


def _dot_hilo(x, m_ref):
    hi = x.astype(BF16)
    lo = (x - hi.astype(F32)).astype(BF16)
    return (jnp.dot(hi, m_ref[...], preferred_element_type=F32)
            + jnp.dot(lo, m_ref[...], preferred_element_type=F32))


def _to_residue_major(val, buf, out_ref, dil):
    rows = out_ref.shape[1]
    for k in range(val.shape[1] // LANES):
        lanes = slice(k * LANES, (k + 1) * LANES)
        buf[k] = val[:, lanes]
        for r in range(dil):
            out_ref[r, :, lanes] = buf.at[k][pl.ds(r, rows, stride=dil), :].astype(out_ref.dtype)


def _from_residue_major(ref, buf, dil):
    if dil == 1:
        return ref[0].astype(F32)
    rows, chunks = ref.shape[1], ref.shape[2] // LANES
    for k in range(chunks):
        for r in range(dil):
            buf.at[k][pl.ds(r, rows, stride=dil), :] = ref[r, :, k * LANES:(k + 1) * LANES].astype(F32)
    return jnp.concatenate([buf[k] for k in range(chunks)], axis=1)


def qkv_prep(proj, qw8, kw8, same, tm):
    s = proj.shape[0]
    items = []
    for g, d in enumerate(DILATIONS):
        items += [(g, "q", CB_Q + g, d), (g, "k", CB_K + g, d)] + ([(g, "v", CB_V + g, d)] if d > 1 else [])
    n = len(items)

    def body(*refs):
        ins, (qw_ref, kw_ref, same_ref), outs, buf = refs[:n], refs[n:n + 3], refs[n + 3:2 * n + 3], refs[-1]
        for idx, (_, kind, _, dil) in enumerate(items):
            val = ins[idx][...].astype(F32)
            if kind != "v":
                r = lax.rsqrt(_head_sum(val * val, same_ref) * (1.0 / HD) + EPS)
                val = val * r * (qw_ref if kind == "q" else kw_ref)[...]
            if dil == 1:
                outs[idx][0] = val.astype(BF16)
            else:
                _to_residue_major(val, buf, outs[idx], dil)

    full = lambda a: pl.BlockSpec(a.shape, lambda i: (0, 0))
    outs = pl.pallas_call(
        body, name="qkv_prep", grid=(s // tm,),
        in_specs=[pl.BlockSpec((tm, CB), lambda i, cb=cb: (i, cb)) for _, _, cb, _ in items]
                 + [full(qw8), full(kw8), full(same)],
        out_specs=[pl.BlockSpec((d, tm // d, CB), lambda i: (0, i, 0)) for _, _, _, d in items],
        out_shape=[jax.ShapeDtypeStruct((d, s // d, CB), BF16) for _, _, _, d in items],
        scratch_shapes=[pltpu.VMEM((CB // LANES, tm, LANES), F32)],
        compiler_params=_cp(("parallel",)))(*([proj] * n), qw8 * (HD ** -0.5), kw8, same)
    srcs = [[None, None, (proj, CB_V + g)] for g in range(len(DILATIONS))]
    for (g, kind, _, _), o in zip(items, outs):
        srcs[g]["qkv".index(kind)] = (o.reshape(s, CB), 0)
    return srcs


def grad_small_weights_and_stats(pairs, da, lc, dc, tk):
    s = da.shape[0]
    nk = s // tk
    n = len(pairs)
    shapes = [(a.shape[1], b.shape[1]) for a, b in pairs]
    tt = LANES * max(DILATIONS)
    per = tt // tk
    assert per >= len(DILATIONS)
    nda = len(DILATIONS) - 1

    def stats_of_group(dil, src_refs, dst_refs):
        rows = tt // dil
        dst_refs = list(dst_refs)
        for src in src_refs:
            dst = dst_refs.pop(0) if dil > 1 else None
            dst_t = dst_refs.pop(0)
            for r in range(dil):
                piece = src[pl.ds(r, rows, stride=dil), :] if dil > 1 else src[...]
                if dil > 1:
                    dst[r] = piece
                dst_t[r] = piece.T[0:NH, :]

    def body(*refs):
        ins, (da_ref, lc_ref, dc_ref) = refs[:2 * n], refs[2 * n:2 * n + 3]
        rest = refs[2 * n + 3:]
        outs, dap, souts = rest[:n], rest[n:n + nda], list(rest[n + nda:-(n + 1)])
        accs, buf = rest[-(n + 1):-1], rest[-1]
        k = pl.program_id(0)
        for j in range(n):
            p = lax.dot_general(ins[2 * j][...], ins[2 * j + 1][...], TN, preferred_element_type=F32)
            accs[j][...] = jnp.where(k == 0, p, accs[j][...] + p)
        val = da_ref[...].astype(F32)
        for j, dil in enumerate(DILATIONS[1:]):
            _to_residue_major(val, buf, dap[j], dil)
        for phase, dil in enumerate(DILATIONS):
            dsts = [souts.pop(0) for _ in range(4 if dil > 1 else 2)]

            @pl.when(k % per == phase)
            def _(dil=dil, dsts=dsts):
                stats_of_group(dil, (lc_ref, dc_ref), dsts)

        @pl.when(k == nk - 1)
        def _():
            for j in range(n):
                outs[j][...] = accs[j][...].astype(BF16)

    out_specs = [pl.BlockSpec(sh, lambda k: (0, 0)) for sh in shapes]
    out_shape = [jax.ShapeDtypeStruct(sh, BF16) for sh in shapes]
    for dil in DILATIONS[1:]:
        out_specs.append(pl.BlockSpec((dil, tk // dil, CB), lambda k: (0, k, 0)))
        out_shape.append(jax.ShapeDtypeStruct((dil, s // dil, CB), BF16))
    for dil in DILATIONS:
        rm = (pl.BlockSpec((dil, tt // dil, LANES), lambda k: (0, k // per, 0)),
              jax.ShapeDtypeStruct((dil, s // dil, LANES), F32))
        tr = (pl.BlockSpec((dil, NH, tt // dil), lambda k: (0, 0, k // per)),
              jax.ShapeDtypeStruct((dil, NH, s // dil), F32))
        group = [rm, tr, rm, tr] if dil > 1 else [tr, tr]
        out_specs += [sp for sp, _ in group]
        out_shape += [sh for _, sh in group]
    outs = list(pl.pallas_call(
        body, name="grad_small_weights_and_stats", grid=(nk,),
        in_specs=[pl.BlockSpec((tk, t.shape[1]), lambda k: (k, 0)) for pair in pairs for t in pair]
                 + [pl.BlockSpec((tk, CB), lambda k: (k, 0))]
                 + [pl.BlockSpec((tt, LANES), lambda k: (k // per, 0))] * 2,
        out_specs=out_specs, out_shape=out_shape,
        scratch_shapes=[pltpu.VMEM(sh, F32) for sh in shapes]
                       + [pltpu.VMEM((CB // LANES, tk, LANES), F32)],
        compiler_params=_cp(("arbitrary",), 56))(*[t for pair in pairs for t in pair], da, lc, dc))
    grads_w, daps, outs = outs[:n], outs[n:n + nda], outs[n + nda:]
    res = []
    for dil in DILATIONS:
        flat_t = lambda t, dil=dil: t.reshape(dil * NH, s // dil)
        if dil == 1:
            lt, dt = outs.pop(0), outs.pop(0)
            res.append((da, lc, dc, flat_t(lt), flat_t(dt)))
        else:
            lcp, lt, dcp, dt = (outs.pop(0) for _ in range(4))
            res.append((daps.pop(0).reshape(s, CB), lcp.reshape(s, LANES), dcp.reshape(s, LANES),
                        flat_t(lt), flat_t(dt)))
    return grads_w, res


def qkv_grads_to_dproj(dproj, proj, grads, qw8, kw8, same, tm):
    s = dproj.shape[0]
    ni = s // tm
    flat = [(t.reshape(d, s // d, CB), d, kind, 3 * kind + g)
            for g, d in enumerate(DILATIONS) for kind, t in enumerate(grads[g])]
    nf = len(flat)
    nraw = 2 * len(DILATIONS)

    def body(*refs):
        dp_hbm, raws, ins = refs[nraw + nf + 4], refs[1:1 + nraw], refs[1 + nraw:1 + nraw + nf]
        qw_ref, kw_ref, same_ref = refs[1 + nraw + nf:4 + nraw + nf]
        gw_ref, stage, buf, sems = refs[5 + nraw + nf:]
        i = pl.program_id(0)
        slot = i % 2

        def slab(step, sl):
            return pltpu.make_async_copy(
                stage.at[sl], dp_hbm.at[pl.ds(pl.multiple_of(step * tm, tm), tm), pl.ds(CB_Q * CB, 9 * CB)],
                sems.at[sl])

        @pl.when(i == 0)
        def _():
            gw_ref[...] = jnp.zeros_like(gw_ref)

        @pl.when(i >= 2)
        def _():
            slab(i - 2, slot).wait()

        for ref, (_, d, kind, jj) in zip(ins, flat):
            cols = slice(jj * CB, (jj + 1) * CB)
            dn = _from_residue_major(ref, buf, d)
            if kind == 2:
                stage[slot, :, cols] = dn.astype(BF16)
                continue
            t = raws[jj][...].astype(F32)
            r = lax.rsqrt(_head_sum(t * t, same_ref) * (1.0 / HD) + EPS)
            xh = t * r
            gw_ref[kind:kind + 1, :] += jnp.sum(dn * xh, axis=0, keepdims=True)
            dxh = dn * (qw_ref if kind == 0 else kw_ref)[...]
            mean = _head_sum(dxh * xh, same_ref) * (1.0 / HD)
            stage[slot, :, cols] = (r * (dxh - xh * mean)).astype(BF16)
        slab(i, slot).start()

        @pl.when(i == ni - 1)
        def _():
            slab(i - 1, 1 - slot).wait()
            slab(i, slot).wait()

    full = lambda a: pl.BlockSpec(a.shape, lambda i: (0, 0))
    any_spec = pl.BlockSpec(memory_space=pl.ANY)
    return pl.pallas_call(
        body, name="qkv_grads_to_dproj", grid=(ni,),
        in_specs=[any_spec] + [pl.BlockSpec((tm, CB), lambda i, jb=jb: (i, CB_Q + jb)) for jb in range(nraw)]
                 + [pl.BlockSpec((d, tm // d, CB), lambda i: (0, i, 0)) for _, d, _, _ in flat]
                 + [full(qw8), full(kw8), full(same)],
        out_specs=[any_spec, pl.BlockSpec((8, CB), lambda i: (0, 0))],
        out_shape=[jax.ShapeDtypeStruct((s, NIN), BF16), jax.ShapeDtypeStruct((8, CB), F32)],
        input_output_aliases={0: 0},
        scratch_shapes=[pltpu.VMEM((2, tm, 9 * CB), BF16), pltpu.VMEM((CB // LANES, tm, LANES), F32),
                        pltpu.SemaphoreType.DMA((2,))],
        compiler_params=_cp(("arbitrary",)))(
            dproj, *([proj] * nraw), *[t for t, _, _, _ in flat], qw8, kw8, same)


def _lane_lo():
    return lax.broadcasted_iota(jnp.int32, (1, 2 * HD), 1) < HD


def _stack_heads(t, lo):
    zero = jnp.zeros_like(t)
    return jnp.concatenate([jnp.where(lo, t, zero), jnp.where(lo, zero, t)], axis=0)


def _masks(other_ok):
    qi = lax.broadcasted_iota(jnp.int32, (QB, QB), 0)
    kj = lax.broadcasted_iota(jnp.int32, (QB, QB), 1)
    return (kj >= qi) & other_ok, kj <= qi


ATTN_SUB = 4


def attn_fwd(srcs):
    s = srcs[0][0][0].shape[0]
    ng = len(DILATIONS)

    def body(*refs):
        ins, outs, bufs = refs[:5 * ng], refs[5 * ng:7 * ng], refs[7 * ng:]
        step = pl.program_id(0)
        lo = _lane_lo()
        head_lane = lax.broadcasted_iota(jnp.int32, (1, LANES), 1)
        for g, dil in enumerate(DILATIONS):
            nb = s // dil // QB
            q_ref, kp_ref, k_ref, vp_ref, v_ref = ins[5 * g:5 * g + 5]
            (o_ref, l_ref), (kbuf, vbuf) = outs[2 * g:2 * g + 2], bufs[2 * g:2 * g + 2]
            kbuf[0:QB], kbuf[QB:] = kp_ref[...], k_ref[...]
            vbuf[0:QB], vbuf[QB:] = vp_ref[...], v_ref[...]
            for j in range(ATTN_SUB):
                rows, krows = slice(j * QB, (j + 1) * QB), slice(j * QB, (j + 2) * QB)
                m_prev, m_cur = _masks((step * ATTN_SUB + j) % nb > 0)
                mask = jnp.concatenate([m_prev, m_cur], axis=1)
                mask = jnp.concatenate([mask, mask], axis=0)
                lses = jnp.zeros((QB, LANES), F32)
                for i in range(NH // 2):
                    sl = slice(2 * HD * i, 2 * HD * (i + 1))
                    qs, ks, vv = q_ref[rows, sl], kbuf[krows, sl], vbuf[krows, sl]
                    sc = lax.dot_general(_stack_heads(qs, lo), ks, NT, preferred_element_type=F32)
                    sc = jnp.where(mask, sc, NEG)
                    mx = jnp.max(sc, axis=-1, keepdims=True)
                    p = jnp.exp(sc - mx)
                    den = jnp.sum(p, axis=-1, keepdims=True)
                    o = jnp.dot(p.astype(BF16), vv, preferred_element_type=F32) * (1.0 / den)
                    lse = mx + jnp.log(den)
                    o_ref[rows, sl] = jnp.where(lo, o[:QB], o[QB:]).astype(BF16)
                    lses = jnp.where(head_lane == 2 * i, lse[:QB], jnp.where(head_lane == 2 * i + 1, lse[QB:], lses))
                l_ref[rows, :] = lses

    main = lambda cb, w=CB: pl.BlockSpec((ATTN_SUB * QB, w), lambda st: (st, cb))
    prev = lambda cb: pl.BlockSpec((QB, CB), lambda st: (jnp.maximum(ATTN_SUB * st - 1, 0), cb))
    in_specs, args = [], []
    for q_src, k_src, v_src in srcs:
        in_specs += [main(q_src[1]), prev(k_src[1]), main(k_src[1]), prev(v_src[1]), main(v_src[1])]
        args += [q_src[0], k_src[0], k_src[0], v_src[0], v_src[0]]
    outs = pl.pallas_call(
        body, name="attn_fwd", grid=(s // (ATTN_SUB * QB),),
        in_specs=in_specs, out_specs=[main(0), main(0, LANES)] * ng,
        out_shape=[jax.ShapeDtypeStruct((s, CB), BF16), jax.ShapeDtypeStruct((s, LANES), F32)] * ng,
        scratch_shapes=[pltpu.VMEM(((ATTN_SUB + 1) * QB, CB), BF16)] * (2 * ng),
        compiler_params=_cp(("parallel",)))(*args)
    return outs[0::2], outs[1::2]


def attn_bwd(srcs, stats):
    s = srcs[0][0][0].shape[0]
    ng = len(DILATIONS)
    sub = ATTN_SUB
    nin, nout, nbuf = 14, 3, 6

    def body(*refs):
        ins, outs, bufs = refs[:nin * ng], refs[nin * ng:(nin + nout) * ng], refs[(nin + nout) * ng:]
        step = pl.program_id(0)
        lo = _lane_lo()
        kj = lax.broadcasted_iota(jnp.int32, (QB, QB), 0)
        qi = lax.broadcasted_iota(jnp.int32, (QB, QB), 1)

        def block(group_refs, nb, j):
            (q_ref, _, k_ref, _, v_ref, _, da_ref, _, lc_ref, dc_ref, _, _, _, _,
             dq_ref, dk_ref, dv_ref, kbuf, vbuf, qbuf, dabuf, lbuf, dbuf) = group_refs
            rows, two = slice(j * QB, (j + 1) * QB), slice(j * QB, (j + 2) * QB)
            place = (step * sub + j) % nb
            m_prev, m_cur = _masks(place > 0)
            qmask = jnp.concatenate([m_prev, m_cur], axis=1)
            qmask = jnp.concatenate([qmask, qmask], axis=0)
            lcols, dcols = lc_ref[rows, :], dc_ref[rows, :]
            kmask = jnp.concatenate([kj <= qi, (kj >= qi) & (place < nb - 1)], axis=1)
            kmask = jnp.concatenate([kmask, kmask], axis=1)
            lrow = jnp.concatenate([lbuf[j], lbuf[j + 1]], axis=1)
            drow = jnp.concatenate([dbuf[j], dbuf[j + 1]], axis=1)
            for i in range(NH // 2):
                sl = slice(2 * HD * i, 2 * HD * (i + 1))
                col_pair = lambda t: jnp.concatenate([t[:, 2 * i:2 * i + 1], t[:, 2 * i + 1:2 * i + 2]], axis=0)
                row_pair = lambda t: jnp.concatenate([t[2 * i:2 * i + 1, :], t[2 * i + 1:2 * i + 2, :]], axis=1)
                ks2, vv2 = kbuf[two, sl], vbuf[two, sl]
                sc = lax.dot_general(_stack_heads(q_ref[rows, sl], lo), ks2, NT, preferred_element_type=F32)
                p = jnp.exp(jnp.where(qmask, sc, NEG) - col_pair(lcols))
                dp = lax.dot_general(_stack_heads(da_ref[rows, sl], lo), vv2, NT, preferred_element_type=F32)
                ds = p * (dp - col_pair(dcols))
                dq = jnp.dot(ds.astype(BF16), ks2, preferred_element_type=F32)
                dq_ref[rows, sl] = (jnp.where(lo, dq[:QB], dq[QB:]) * (HD ** -0.5)).astype(BF16)

                q2, da2 = _stack_heads(qbuf[two, sl], lo), _stack_heads(dabuf[two, sl], lo)
                ks, vv = k_ref[rows, sl], v_ref[rows, sl]
                sct = lax.dot_general(ks, q2, NT, preferred_element_type=F32)
                pt = jnp.exp(jnp.where(kmask, sct, NEG) - row_pair(lrow))
                dpt = lax.dot_general(vv, da2, NT, preferred_element_type=F32)
                dst = pt * (dpt - row_pair(drow))
                dv_ref[rows, sl] = jnp.dot(pt.astype(BF16), da2, preferred_element_type=F32).astype(BF16)
                dk_ref[rows, sl] = jnp.dot(dst.astype(BF16), q2, preferred_element_type=F32).astype(BF16)

        for g, dil in enumerate(DILATIONS):
            group_refs = (ins[nin * g:nin * (g + 1)] + outs[nout * g:nout * (g + 1)] + bufs[nbuf * g:nbuf * (g + 1)])
            (q_ref, kp_ref, k_ref, vp_ref, v_ref, qn_ref, da_ref, dan_ref, _, _, l_ref, ln_ref, d_ref, dn_ref,
             _, _, _, kbuf, vbuf, qbuf, dabuf, lbuf, dbuf) = group_refs
            kbuf[0:QB], kbuf[QB:] = kp_ref[...], k_ref[...]
            vbuf[0:QB], vbuf[QB:] = vp_ref[...], v_ref[...]
            qbuf[0:sub * QB], qbuf[sub * QB:] = q_ref[...], qn_ref[...]
            dabuf[0:sub * QB], dabuf[sub * QB:] = da_ref[...], dan_ref[...]
            for c in range(sub):
                lbuf[c], dbuf[c] = l_ref[:, c * QB:(c + 1) * QB], d_ref[:, c * QB:(c + 1) * QB]
            lbuf[sub], dbuf[sub] = ln_ref[...], dn_ref[...]
            for j in range(sub):
                block(group_refs, s // dil // QB, j)

    last = s // QB - 1
    main = lambda cb, w=CB: pl.BlockSpec((sub * QB, w), lambda st: (st, cb))
    prev = lambda cb: pl.BlockSpec((QB, CB), lambda st: (jnp.maximum(sub * st - 1, 0), cb))
    nxt = lambda cb: pl.BlockSpec((QB, CB), lambda st: (jnp.minimum(sub * (st + 1), last), cb))
    in_specs, args = [], []
    for (q_src, k_src, v_src), (da, lc, dc, lt, dt), dil in zip(srcs, stats, DILATIONS):
        nb = s // dil // QB
        t_main = pl.BlockSpec((NH, sub * QB), lambda st, nb=nb: (sub * st // nb, (sub * st % nb) // sub))
        t_nxt = pl.BlockSpec((NH, QB), lambda st, nb=nb: (sub * st // nb, jnp.minimum(sub * st % nb + sub, nb - 1)))
        in_specs += [main(q_src[1]), prev(k_src[1]), main(k_src[1]), prev(v_src[1]), main(v_src[1]), nxt(q_src[1]),
                     main(0), nxt(0), main(0, LANES), main(0, LANES), t_main, t_nxt, t_main, t_nxt]
        args += [q_src[0], k_src[0], k_src[0], v_src[0], v_src[0], q_src[0], da, da, lc, dc, lt, lt, dt, dt]
    out = jax.ShapeDtypeStruct((s, CB), BF16)
    big = pltpu.VMEM(((sub + 1) * QB, CB), BF16)
    outs = pl.pallas_call(
        body, name="attn_bwd", grid=(s // (sub * QB),),
        in_specs=in_specs, out_specs=[main(0)] * (nout * ng), out_shape=[out] * (nout * ng),
        scratch_shapes=([big] * 4 + [pltpu.VMEM((sub + 1, NH, QB), F32)] * 2) * ng,
        compiler_params=_cp(("parallel",), 56))(*args)
    return [tuple(outs[nout * g:nout * (g + 1)]) for g in range(ng)]


def _conv_taps(u, u_prev, first):
    tm = u.shape[0]
    row = lax.broadcasted_iota(jnp.int32, (tm, 1), 0)
    up = jnp.where(first, 0.0, u_prev)
    u1 = jnp.where(row == 0, up[HALO - 1:HALO, :], pltpu.roll(u, 1, 0))
    u2 = jnp.where(row == 0, up[HALO - 2:HALO - 1, :],
                   jnp.where(row == 1, up[HALO - 1:HALO, :], pltpu.roll(u, 2, 0)))
    return u1, u2


def mid_fwd(proj, o_g, lse_g, conv_w, expand, tm):
    s = proj.shape[0]
    hb = tm // HALO

    def body(ba_ref, ca_ref, xa_ref, za_ref, cah_ref, xah_ref, zb_ref,
             o0, o1, o2, l0, l1, l2, w_ref, exp_ref, ya_ref, yb_ref, at_ref, lc_ref, buf_o, buf_l):
        first = pl.program_id(0) == 0
        u = ca_ref[...].astype(F32) * xa_ref[...].astype(F32)
        u1, u2 = _conv_taps(u, cah_ref[...].astype(F32) * xah_ref[...].astype(F32), first)
        conv = w_ref[0:1, :] * u2 + w_ref[1:2, :] * u1 + w_ref[2:3, :] * u
        ya_ref[...] = (ba_ref[...].astype(F32) * conv * _silu(za_ref[...].astype(F32))).astype(BF16)
        ls = [_from_residue_major(l, buf_l.at[g], d) for g, (l, d) in enumerate(zip((l0, l1, l2), DILATIONS))]
        mx = jnp.maximum(jnp.maximum(ls[0], ls[1]), ls[2])
        es = [jnp.exp(l - mx) for l in ls]
        den = es[0] + es[1] + es[2]
        attn = jnp.zeros((tm, CB), F32)
        for e, o, d in zip(es, (o0, o1, o2), DILATIONS):
            attn = attn + _dot_hilo(e / den, exp_ref) * _from_residue_major(o, buf_o, d)
        at_ref[...] = attn
        lc_ref[...] = mx + jnp.log(den)
        yb_ref[...] = (attn * _silu(zb_ref[...].astype(F32))).astype(BF16)

    col = lambda j: pl.BlockSpec((tm, D), lambda i: (i, j))
    halo = lambda j: pl.BlockSpec((HALO, D), lambda i: (jnp.maximum(i * hb - 1, 0), j))
    loc = lambda w: pl.BlockSpec((tm, w), lambda i: (i, 0))
    rm = lambda w: [pl.BlockSpec((d, tm // d, w), lambda i: (0, i, 0)) for d in DILATIONS]
    rm_view = lambda ts, w: [t.reshape(d, s // d, w) for t, d in zip(ts, DILATIONS)]
    return pl.pallas_call(
        body, name="mid_fwd", grid=(s // tm,),
        in_specs=[col(0), col(1), col(2), col(3), halo(1), halo(2),
                  pl.BlockSpec((tm, CB), lambda i: (i, CB_ZB))] + rm(CB) + rm(LANES)
                 + [pl.BlockSpec((3, D), lambda i: (0, 0)), pl.BlockSpec(expand.shape, lambda i: (0, 0))],
        out_specs=[loc(D), loc(CB), loc(CB), loc(LANES)],
        out_shape=[jax.ShapeDtypeStruct((s, D), BF16), jax.ShapeDtypeStruct((s, CB), BF16),
                   jax.ShapeDtypeStruct((s, CB), F32), jax.ShapeDtypeStruct((s, LANES), F32)],
        scratch_shapes=[pltpu.VMEM((CB // LANES, tm, LANES), F32), pltpu.VMEM((3, 1, tm, LANES), F32)],
        compiler_params=_cp(("parallel",)))(
            proj, proj, proj, proj, proj, proj, proj, *rm_view(o_g, CB), *rm_view(lse_g, LANES), conv_w, expand)


def tail(proj, ya, yb, attn, x, target, gate, pa_w, pb_w, wo_w, total, conv_w, tm):
    s = proj.shape[0]
    ni = s // tm
    hb = tm // HALO
    nlate = NIN - CB_ZB * CB
    nearly = 4 * D

    def body(ya_ref, yb_ref, ga_ref, gb_ref, zb_ref, at_ref, x_ref, t_ref, gate_ref, pa_ref, pb_ref, wo_ref,
             tot_ref, ba_ref, ca_ref, xa_ref, za_ref, cah_ref, xah_ref, cw_ref,
             dp_hbm, dy_ref, da_ref, dc_ref, mg_ref, do_ref, dpa_ref, dpb_ref, st_ref, gwc_ref,
             stage, dconv_next, sems):
        step = pl.program_id(0)
        i = ni - 1 - step
        slot = step % 2

        def slabs(at_step, sl):
            rows = pl.ds(pl.multiple_of((ni - 1 - at_step) * tm, tm), tm)
            return (pltpu.make_async_copy(stage.at[sl, :, 0:nearly], dp_hbm.at[rows, pl.ds(0, nearly)],
                                          sems.at[sl, 0]),
                    pltpu.make_async_copy(stage.at[sl, :, nearly:], dp_hbm.at[rows, pl.ds(CB_ZB * CB, nlate)],
                                          sems.at[sl, 1]))

        @pl.when(step == 0)
        def _():
            st_ref[...] = jnp.zeros_like(st_ref)
            gwc_ref[...] = jnp.zeros_like(gwc_ref)
            dconv_next[...] = jnp.zeros_like(dconv_next)

        @pl.when(step >= 2)
        def _():
            for cp in slabs(step - 2, slot):
                cp.wait()

        gate_v = gate_ref[...]
        pa = jnp.dot(ya_ref[...], pa_ref[...], preferred_element_type=F32)
        pb = jnp.dot(yb_ref[...], pb_ref[...], preferred_element_type=F32)
        sa = jax.nn.sigmoid(ga_ref[...].astype(F32))
        sb = jax.nn.sigmoid(gb_ref[...].astype(F32))
        merged = (sa * pa + sb * pb).astype(BF16)
        mg_ref[...] = merged
        out = jnp.dot(merged, wo_ref[...], preferred_element_type=F32)
        err = x_ref[...] + gate_v * out - t_ref[...]
        dy = err * (1.0 / D)
        dy_ref[...] = dy
        st_ref[0:1, :] += jnp.sum(dy * out, axis=0, keepdims=True)
        st_ref[1:2, :] += jnp.sum(err * err, axis=0, keepdims=True)
        dout = (gate_v * dy).astype(BF16)
        do_ref[...] = dout
        dmg = lax.dot_general(dout, wo_ref[...], NT, preferred_element_type=F32)
        dpa = (dmg * sa).astype(BF16)
        dpb = (dmg * sb).astype(BF16)
        dpa_ref[...] = dpa
        dpb_ref[...] = dpb
        late = nearly
        stage[slot, :, late + CB:late + CB + D] = (dmg * pa * sa * (1.0 - sa)).astype(BF16)
        stage[slot, :, late + CB + D:] = (dmg * pb * sb * (1.0 - sb)).astype(BF16)
        dya = lax.dot_general(dpa, pa_ref[...], NT, preferred_element_type=F32)
        dyb = lax.dot_general(dpb, pb_ref[...], NT, preferred_element_type=F32)
        zb = zb_ref[...].astype(F32)
        sg = jax.nn.sigmoid(zb)
        attn_v = at_ref[...]
        dattn = dyb * (zb * sg)
        da_ref[...] = dattn.astype(BF16)
        stage[slot, :, late:late + CB] = (dyb * attn_v * (sg * (1.0 + zb * (1.0 - sg)))).astype(BF16)
        dc_ref[...] = _dot_hilo(dattn * attn_v, tot_ref)

        ba, ca, xa, za = (t[...].astype(F32) for t in (ba_ref, ca_ref, xa_ref, za_ref))
        u = ca * xa
        u1, u2 = _conv_taps(u, cah_ref[...].astype(F32) * xah_ref[...].astype(F32), i == 0)
        w0, w1, w2 = cw_ref[0:1, :], cw_ref[1:2, :], cw_ref[2:3, :]
        conv = w0 * u2 + w1 * u1 + w2 * u
        sga = jax.nn.sigmoid(za)
        sza = za * sga
        dconv = dya * ba * sza
        dcn = dconv_next[...]
        rowi = lax.broadcasted_iota(jnp.int32, (tm, 1), 0)
        d1 = jnp.where(rowi == tm - 1, dcn[0:1, :], pltpu.roll(dconv, tm - 1, 0))
        d2 = jnp.where(rowi == tm - 2, dcn[0:1, :],
                       jnp.where(rowi == tm - 1, dcn[1:2, :], pltpu.roll(dconv, tm - 2, 0)))
        du = w2 * dconv + w1 * d1 + w0 * d2
        stage[slot, :, 0:D] = (dya * conv * sza).astype(BF16)
        stage[slot, :, D:2 * D] = (du * xa).astype(BF16)
        stage[slot, :, 2 * D:3 * D] = (du * ca).astype(BF16)
        stage[slot, :, 3 * D:4 * D] = (dya * ba * conv * (sga * (1.0 + za * (1.0 - sga)))).astype(BF16)
        gwc_ref[0:1, :] += jnp.sum(dconv * u2, axis=0, keepdims=True)
        gwc_ref[1:2, :] += jnp.sum(dconv * u1, axis=0, keepdims=True)
        gwc_ref[2:3, :] += jnp.sum(dconv * u, axis=0, keepdims=True)
        dconv_next[...] = dconv[0:8, :]

        for cp in slabs(step, slot):
            cp.start()

        @pl.when(step == ni - 1)
        def _():
            for cp in slabs(step - 1, 1 - slot) + slabs(step, slot):
                cp.wait()

    rev = lambda st: ni - 1 - st
    row = lambda w: pl.BlockSpec((tm, w), lambda st: (rev(st), 0))
    pcol = lambda w, jb: pl.BlockSpec((tm, w), lambda st: (rev(st), jb))
    halo = lambda jb: pl.BlockSpec((HALO, D), lambda st: (jnp.maximum(rev(st) * hb - 1, 0), jb))
    const = lambda a: pl.BlockSpec(a.shape, lambda st: (0, 0), pipeline_mode=pl.Buffered(1))
    acc = pl.BlockSpec((8, D), lambda st: (0, 0))
    return pl.pallas_call(
        body, name="tail", grid=(ni,),
        in_specs=[row(D), row(CB), pcol(D, 9), pcol(D, 10), pcol(CB, CB_ZB), row(CB), row(D), row(D),
                  pl.BlockSpec((1, D), lambda st: (0, 0)), const(pa_w), const(pb_w), const(wo_w), const(total),
                  pcol(D, 0), pcol(D, 1), pcol(D, 2), pcol(D, 3), halo(1), halo(2),
                  pl.BlockSpec((3, D), lambda st: (0, 0))],
        out_specs=[pl.BlockSpec(memory_space=pl.ANY),
                   row(D), row(CB), row(LANES), row(D), row(D), row(D), row(D), acc, acc],
        out_shape=[jax.ShapeDtypeStruct((s, NIN), BF16), jax.ShapeDtypeStruct((s, D), F32),
                   jax.ShapeDtypeStruct((s, CB), BF16), jax.ShapeDtypeStruct((s, LANES), F32)]
                  + [jax.ShapeDtypeStruct((s, D), BF16)] * 4 + [jax.ShapeDtypeStruct((8, D), F32)] * 2,
        scratch_shapes=[pltpu.VMEM((2, tm, nearly + nlate), BF16), pltpu.VMEM((8, D), F32),
                        pltpu.SemaphoreType.DMA((2, 2))],
        compiler_params=_cp(("arbitrary",), 60))(
            ya, yb, proj, proj, proj, attn, x, target, gate, pa_w, pb_w, wo_w, total,
            proj, proj, proj, proj, proj, proj, conv_w)


def _local_step(x, target, shift, scale, gate, norm_w, conv_w, qw, kw, w_shard, small_shards, me_xyc):
    qw8, kw8 = jnp.tile(qw, (1, NH)), jnp.tile(kw, (1, NH))
    same, total, expand = _head_matrices()
    proj, ht, wg, (pa_g, pb_g, wo_g) = proj_fwd_gather(
        x, norm_w, scale, shift, w_shard, small_shards, gather_order(me_xyc), 1024)
    pa_w, wo_w = pa_g.reshape(D, D), wo_g.reshape(D, D)
    pb_w = pb_g.transpose(1, 0, 2).reshape(CB, D)
    srcs = qkv_prep(proj, qw8, kw8, same, 512)
    o_g, lse_g = attn_fwd(srcs)
    ya, yb, attn, lc = mid_fwd(proj, o_g, lse_g, conv_w, expand, 512)
    dproj, dy, da, dc, merged, dout, dpa, dpb, st_tail, st_conv = tail(
        proj, ya, yb, attn, x, target, gate, pa_w, pb_w, wo_w, total, conv_w, 256)
    (g_wo, g_pa, g_pb), stats = grad_small_weights_and_stats([(merged, dout), (ya, dpa), (yb, dpb)], da, lc, dc, 512)
    grads = attn_bwd(srcs, stats)
    dproj, gw_qk = qkv_grads_to_dproj(dproj, proj, grads, qw8, kw8, same, 512)
    slabs = [g_pa.reshape(NDEV, 128, D), g_pb.reshape(CB, NDEV, 128).transpose(1, 0, 2), g_wo.reshape(NDEV, 128, D)]
    grad_x, st_norm, r_win, (r_pa, r_pb, r_wo) = proj_bwd(
        ht, dproj, wg, slabs, scatter_order(me_xyc), x, dy, norm_w, scale, 1024)
    dmod = jnp.concatenate([st_norm[0:1], st_norm[1:2], st_tail[0:1]], axis=1)
    loss_part = (0.5 / D) * jnp.sum(st_tail[1])
    gw_heads = gw_qk[0:2].reshape(2, NH, HD).sum(axis=1)
    small = dict(dmod=dmod, norm_w=st_norm[2:3], conv_w=st_conv[0:3],
                 q_norm_w=gw_heads[0:1], k_norm_w=gw_heads[1:2], loss=loss_part)
    return grad_x, small, (r_win, r_pa, r_pb, r_wo)


def kernel(x, c, w_ada, b_ada, norm_w, w_in, conv_w, q_norm_w, k_norm_w, w_br_conv, w_br_attn, w_out, loss_target, m_w_ada, m_b_ada, m_norm_w, m_w_in, m_conv_w, m_q_norm_w, m_k_norm_w, m_w_br_conv, m_w_br_attn, m_w_out, v_w_ada, v_b_ada, v_norm_w, v_w_in, v_conv_w, v_q_norm_w, v_k_norm_w, v_w_br_conv, v_w_br_attn, v_w_out):
    me_xyc = (lax.axis_index("x"), lax.axis_index("y"), lax.axis_index("c"))
    me = _dev_index(me_xyc)
    ncol = w_ada.shape[2]

    conv_pad = jnp.zeros((8, 128), F32).at[0:3].set(conv_w[0])
    b_cols = lax.dynamic_slice(b_ada, (0, me * ncol), (1, ncol))
    mod_pieces, c_all, conv_all = ada_fwd(c, conv_pad, w_ada[0], b_cols)
    conv_full = conv_all[:, 0:3].transpose(1, 0, 2).reshape(3, D)
    c_all = c_all.reshape(NDEV, D)
    mod = mod_pieces.reshape(1, 3 * D)
    shift, scale, gate = mod[:, 0:D], mod[:, D:2 * D], mod[:, 2 * D:3 * D]

    grad_x, small, (r_win, r_pa, r_pb, r_wo) = _local_step(
        x[0], loss_target[0], shift, scale, gate, norm_w, conv_full, q_norm_w, k_norm_w,
        w_in[0].astype(BF16), [w_br_conv[0].astype(BF16), w_br_attn[0].astype(BF16), w_out[0].astype(BF16)], me_xyc)

    packed = jnp.concatenate(
        [small["dmod"], small["norm_w"], small["conv_w"].reshape(1, 3 * D), small["q_norm_w"], small["k_norm_w"],
         jnp.full((1, 128), small["loss"], F32)], axis=1)
    packed_all, tot = gather_sum(packed)
    loss = tot[0, 7 * D + 2 * HD]
    dmod_all = packed_all[:, 0, 0:3 * D]
    g_b_ada = tot[:, 0:3 * D]
    g_norm_w = tot[:, 3 * D:4 * D]
    g_conv = lax.dynamic_slice(tot[:, 4 * D:7 * D].reshape(3, D), (0, me * 128), (3, 128))
    g_qn = tot[:, 7 * D:7 * D + HD]
    g_kn = tot[:, 7 * D + HD:7 * D + 2 * HD]
    g_w_ada = ada_bwd(c_all.T, lax.dynamic_slice(dmod_all, (0, me * ncol), (NDEV, ncol)))

    def upd(parts, w, m, v, name, rows):
        shape = w.shape
        w2, m2, v2 = (t.reshape(shape[-2:]) for t in (w, m, v))
        return [t.reshape(shape) for t in adamw(parts, w2, m2, v2, name, rows)]

    res = {"w_in": upd(r_win, w_in, m_w_in, v_w_in, "adamw_w_in", 128)}
    small_params = {"w_ada": (g_w_ada[None], w_ada, m_w_ada, v_w_ada), "b_ada": (g_b_ada[None], b_ada, m_b_ada, v_b_ada),
                    "norm_w": (g_norm_w[None], norm_w, m_norm_w, v_norm_w),
                    "conv_w": (g_conv[None], conv_w, m_conv_w, v_conv_w),
                    "q_norm_w": (g_qn[None], q_norm_w, m_q_norm_w, v_q_norm_w),
                    "k_norm_w": (g_kn[None], k_norm_w, m_k_norm_w, v_k_norm_w),
                    "w_br_conv": (r_pa, w_br_conv, m_w_br_conv, v_w_br_conv),
                    "w_br_attn": (r_pb, w_br_attn, m_w_br_attn, v_w_br_attn),
                    "w_out": (r_wo, w_out, m_w_out, v_w_out)}
    updated = adamw_small([(item[0],) + tuple(t.reshape(t.shape[-2:]) for t in item[1:])
                           for item in small_params.values()])
    for (pname, item), outs4 in zip(small_params.items(), updated):
        res[pname] = [t.reshape(item[1].shape) for t in outs4]
    names = ["w_ada", "b_ada", "norm_w", "w_in", "conv_w", "q_norm_w", "k_norm_w", "w_br_conv", "w_br_attn", "w_out"]
    return (loss, grad_x[None], *[res[n][0] for n in names], *[res[n][1] for n in names],
            *[res[n][2] for n in names], *[res[n][3] for n in names])
```

```python
import jax
import jax.numpy as jnp
from jax import lax
from jax.experimental import pallas as pl
from jax.experimental.pallas import tpu as pltpu

F32, BF16 = jnp.float32, jnp.bfloat16
D = 1024
NIN = 11264
NDEV = 8
SHARD = NIN // NDEV
HD = 64
NH = 8
QB = 128
CB = 512
CB_Q, CB_K, CB_V, CB_ZB = 8, 11, 14, 17
DILATIONS = (1, 4, 16)
EPS = 1e-6
NEG = -1e30
HALO = 16
LANES = 128
MESH = pl.DeviceIdType.MESH

ADAM_LR, ADAM_B1, ADAM_B2, ADAM_EPS, ADAM_WD, ADAM_STEP = 0.001, 0.9, 0.999, 1e-08, 0.01, 10

NT = (((1,), (1,)), ((), ()))
TN = (((0,), (0,)), ((), ()))


def _cp(sem, vmem_mb=48):
    return pltpu.CompilerParams(dimension_semantics=sem, vmem_limit_bytes=vmem_mb << 20)


def _silu(z):
    return z * jax.nn.sigmoid(z)


def _coords():
    return lax.axis_index("x"), lax.axis_index("y"), lax.axis_index("c")


FLIPS = [(fx, fy, fc) for fx in (0, 1) for fy in (0, 1) for fc in (0, 1)][1:]


def gather_sum(vec):
    def body(v_ref, all_ref, sum_ref, send_sems, recv_sems, local_sem):
        me_xyc = _coords()
        me = _dev_index(me_xyc)
        peers = [_flip(me_xyc, f) for f in FLIPS]

        def copy(k, block):
            return pltpu.make_async_remote_copy(
                src_ref=v_ref, dst_ref=all_ref.at[block], send_sem=send_sems.at[k], recv_sem=recv_sems.at[k],
                device_id=peers[k], device_id_type=MESH)

        mine = pltpu.make_async_copy(v_ref, all_ref.at[me], local_sem)
        sends = [copy(k, me) for k in range(7)]
        for cp in [mine] + sends:
            cp.start()
        for k in range(7):
            copy(k, _dev_index(peers[k])).wait_recv()
        mine.wait()
        acc = all_ref[0]
        for b in range(1, NDEV):
            acc = acc + all_ref[b]
        sum_ref[...] = acc
        for cp in sends:
            cp.wait_send()

    return pl.pallas_call(
        body, name="gather_sum",
        out_shape=[jax.ShapeDtypeStruct((NDEV,) + vec.shape, F32), jax.ShapeDtypeStruct(vec.shape, F32)],
        scratch_shapes=[pltpu.SemaphoreType.DMA((7,)), pltpu.SemaphoreType.DMA((7,)), pltpu.SemaphoreType.DMA],
    )(vec)


def _flip(dev, f):
    return tuple(1 - v if b else v for v, b in zip(dev, f))


def _dev_index(dev):
    return 4 * dev[0] + 2 * dev[1] + dev[2]


def _chip_order(x, y, c):
    xor = lambda a, b: a + b - 2 * a * b
    return [(xor(x, 1 - c), xor(y, c)), (xor(x, c), xor(y, 1 - c)), (1 - x, 1 - y)]


def gather_order(me_xyc):
    x, y, c = me_xyc
    chips = _chip_order(x, y, c)
    devs = [(x, y, c), (x, y, 1 - c), (*chips[0], c), (*chips[1], c),
            (*chips[1], 1 - c), (*chips[0], 1 - c), (*chips[2], c), (*chips[2], 1 - c)]
    return jnp.stack([_dev_index(d) for d in devs]).astype(jnp.int32)


def scatter_order(me_xyc):
    devs = [_flip(me_xyc, f) for f in FLIPS] + [me_xyc]
    return jnp.stack([_dev_index(d) for d in devs]).astype(jnp.int32)


def ada_fwd(c, conv_pad, w_ada, b_cols):
    ncol = w_ada.shape[1]

    def body(c_ref, cv_ref, w_ref, b_ref, mod_ref, call_ref, cvall_ref, rows_buf, send_sems, recv_sems, local_sems):
        me_xyc = _coords()
        me = _dev_index(me_xyc)
        peers = [_flip(me_xyc, f) for f in FLIPS]
        pids = [_dev_index(p) for p in peers]

        def copy(a, k, src, dst):
            return pltpu.make_async_remote_copy(src_ref=src, dst_ref=dst, send_sem=send_sems.at[a, k],
                                                recv_sem=recv_sems.at[a, k], device_id=peers[k], device_id_type=MESH)

        own = [pltpu.make_async_copy(c_ref, call_ref.at[me], local_sems.at[0]),
               pltpu.make_async_copy(cv_ref, cvall_ref.at[me], local_sems.at[1])]
        first = [copy(0, k, c_ref, call_ref.at[me]) for k in range(7)]
        first += [copy(1, k, cv_ref, cvall_ref.at[me]) for k in range(7)]
        for cp in own + first:
            cp.start()
        own[0].wait()
        for k in range(7):
            copy(0, k, c_ref, call_ref.at[pids[k]]).wait_recv()
        seq = lax.broadcasted_iota(jnp.int32, (NDEV, 1), 0)
        c_all = jnp.zeros((NDEV, D), F32)
        for p in range(NDEV):
            c_all = jnp.where(seq == p, call_ref[p], c_all)
        mods = jnp.dot(_silu(c_all).astype(BF16), w_ref[...].astype(BF16), preferred_element_type=F32) + b_ref[...]
        for p in range(NDEV):
            rows_buf[p] = mods[p:p + 1, :]
        mine = pltpu.make_async_copy(rows_buf.at[me], mod_ref.at[me], local_sems.at[2])
        second = [copy(2, k, rows_buf.at[pids[k]], mod_ref.at[me]) for k in range(7)]
        for cp in [mine] + second:
            cp.start()
        for k in range(7):
            copy(2, k, rows_buf.at[pids[k]], mod_ref.at[pids[k]]).wait_recv()
            copy(1, k, cv_ref, cvall_ref.at[pids[k]]).wait_recv()
        for cp in first + second:
            cp.wait_send()
        own[1].wait()
        mine.wait()

    return pl.pallas_call(
        body, name="ada_fwd",
        out_shape=[jax.ShapeDtypeStruct((NDEV, 1, ncol), F32), jax.ShapeDtypeStruct((NDEV, 1, D), F32),
                   jax.ShapeDtypeStruct((NDEV,) + conv_pad.shape, F32)],
        scratch_shapes=[pltpu.VMEM((NDEV, 1, ncol), F32), pltpu.SemaphoreType.DMA((3, 7)),
                        pltpu.SemaphoreType.DMA((3, 7)), pltpu.SemaphoreType.DMA((3,))],
    )(c, conv_pad, w_ada, b_cols)


def ada_bwd(c_all_t, dmod_cols):
    def body(c_ref, d_ref, o_ref):
        at = _silu(c_ref[...])
        acc = at[:, 0:1] * d_ref[0:1, :]
        for b in range(1, NDEV):
            acc = acc + at[:, b:b + 1] * d_ref[b:b + 1, :]
        o_ref[...] = acc

    return pl.pallas_call(body, name="ada_bwd",
                          out_shape=jax.ShapeDtypeStruct((D, dmod_cols.shape[1]), F32))(c_all_t, dmod_cols)


def _adamw_update(g, w_ref, m_ref, v_ref, g_ref, d_ref, nm_ref, nv_ref):
    nm = ADAM_B1 * m_ref[...] + (1.0 - ADAM_B1) * g
    nv = ADAM_B2 * v_ref[...] + (1.0 - ADAM_B2) * (g * g)
    g_ref[...] = g
    nm_ref[...] = nm
    nv_ref[...] = nv
    m_hat = nm / (1.0 - ADAM_B1 ** ADAM_STEP)
    v_hat = nv / (1.0 - ADAM_B2 ** ADAM_STEP)
    d_ref[...] = -ADAM_LR * (m_hat / (jnp.sqrt(v_hat) + ADAM_EPS) + ADAM_WD * w_ref[...])


def adamw_small(items):
    n = len(items)

    def body(*refs):
        ins, outs = refs[:4 * n], refs[4 * n:]
        for a in range(n):
            p_ref, w_ref, m_ref, v_ref = ins[4 * a:4 * a + 4]
            g = p_ref[0].astype(F32)
            for b in range(1, p_ref.shape[0]):
                g = g + p_ref[b].astype(F32)
            _adamw_update(g, w_ref, m_ref, v_ref, *outs[4 * a:4 * a + 4])

    out = pl.pallas_call(
        body, name="adamw_small",
        out_shape=[jax.ShapeDtypeStruct(it[1].shape, F32) for it in items for _ in range(4)],
        compiler_params=pltpu.CompilerParams(vmem_limit_bytes=48 << 20))(*[t for it in items for t in it])
    return [out[4 * a:4 * a + 4] for a in range(n)]


def adamw(parts, w, m, v, name, rows):
    n, r, ccols = parts.shape

    def body(p_ref, w_ref, m_ref, v_ref, g_ref, d_ref, nm_ref, nv_ref):
        g = p_ref[0].astype(F32)
        for b in range(1, n):
            g = g + p_ref[b].astype(F32)
        _adamw_update(g, w_ref, m_ref, v_ref, g_ref, d_ref, nm_ref, nv_ref)

    blk = pl.BlockSpec((rows, ccols), lambda i: (i, 0))
    out = jax.ShapeDtypeStruct((r, ccols), F32)
    return pl.pallas_call(
        body, name=name, grid=(r // rows,),
        in_specs=[pl.BlockSpec((n, rows, ccols), lambda i: (0, i, 0)), blk, blk, blk],
        out_specs=[blk] * 4, out_shape=[out] * 4, compiler_params=_cp(("parallel",)))(parts, w, m, v)


def proj_fwd_gather(x, nw, scale, shift, w_shard, extras, order, tm):
    s = x.shape[0]
    ni = s // tm
    n = 1 + len(extras)
    mid = ni - 2

    def body(order_ref, x_ref, nw_ref, sc_ref, sh_ref, *refs):
        ins, o_ref, ht_ref, outs = refs[:n], refs[n], refs[n + 1], refs[n + 2:2 * n + 2]
        h_all, wbuf, send_sems, recv_sems, local_sems, load_sems = refs[2 * n + 2:]
        jj, i = pl.program_id(0), pl.program_id(1)
        x, y, c = _coords()
        me, sibling = (x, y, c), (x, y, 1 - c)
        chips = _chip_order(x, y, c)
        relayed = [(*chips[1], 1 - c), (*chips[0], 1 - c), (*chips[2], 1 - c)]

        def slot(a, dev):
            return outs[a].at[_dev_index(dev)]

        def copy(a, k, block, to, src=None):
            return pltpu.make_async_remote_copy(
                src_ref=slot(a, block) if src is None else src, dst_ref=slot(a, block),
                send_sem=send_sems.at[a, k], recv_sem=recv_sems.at[a, k], device_id=to, device_id_type=MESH)

        mine = [pltpu.make_async_copy(ins[a], slot(a, me), local_sems.at[a]) for a in range(n)]
        to_sibling = [copy(a, 0, me, sibling, src=ins[a]) for a in range(n)]
        to_chip = [[copy(a, 1 + j, me, (*chips[j], c), src=ins[a]) for a in range(n)] for j in range(2)]
        onward = [copy(a, 3, (*chips[1], c), (*chips[0], c)) for a in range(n)]
        passed = [[copy(a, 4 + j, (*ch, c), sibling) for a in range(n)] for j, ch in enumerate(chips)]
        sends = lambda a: [to_sibling[a], to_chip[0][a], to_chip[1][a], onward[a]] + [passed[j][a] for j in range(3)]

        def arrived(a, j):
            copy(a, 1 + j, (*chips[j], c), me).wait_recv()

        def load(row):
            return pltpu.make_async_copy(outs[0].at[order_ref[row]], wbuf.at[row % 2], load_sems.at[row % 2])

        @pl.when((jj == 0) & (i == 0))
        def _():
            for cp in mine:
                cp.start()
            to_sibling[0].start()
            to_chip[0][0].start()
            pltpu.make_async_copy(ins[0], wbuf.at[0], load_sems.at[0]).start()

        @pl.when((jj == 1) & (i == 0))
        def _():
            to_chip[1][0].start()

        @pl.when((jj == 4) & (i == 0))
        def _():
            for a in range(1, n):
                to_sibling[a].start()
                to_chip[0][a].start()
                to_chip[1][a].start()

        direct = {2: 0, 3: 1, 6: 2}
        relay = {4: 0, 5: 1, 7: 2}

        @pl.when((jj == 0) & (i == mid))
        def _():
            copy(0, 0, sibling, me).wait_recv()

        for row, j in direct.items():
            @pl.when((jj == row - 1) & (i == mid))
            def _(j=j):
                arrived(0, j)
                passed[j][0].start()
                if j == 1:
                    onward[0].start()

        for row, j in relay.items():
            @pl.when((jj == row - 1) & (i == mid))
            def _(j=j):
                copy(0, 4 + j, relayed[j], me).wait_recv()

        @pl.when((jj == NDEV - 1) & (i == 0))
        def _():
            for a in range(1, n):
                arrived(a, 1)
                onward[a].start()
                passed[1][a].start()
                arrived(a, 0)
                passed[0][a].start()

        @pl.when((jj < NDEV - 1) & (i == mid))
        def _():
            load(jj + 1).start()

        @pl.when(i == 0)
        def _():
            load(jj).wait()

        @pl.when(jj == 0)
        def _():
            xf = x_ref[...]
            r = lax.rsqrt(jnp.mean(xf * xf, axis=-1, keepdims=True) + EPS)
            h = (xf * r * nw_ref[...]) * (1.0 + sc_ref[...]) + sh_ref[...]
            h_all[i] = h.astype(BF16)
            ht_ref[...] = h.T.astype(BF16)

        o_ref[...] = jnp.dot(h_all[i], wbuf[jj % 2], preferred_element_type=F32).astype(BF16)

        @pl.when((jj == NDEV - 1) & (i == ni - 1))
        def _():
            for a in range(1, n):
                arrived(a, 2)
                passed[2][a].start()
            for a in range(1, n):
                copy(a, 0, sibling, me).wait_recv()
                for j in range(3):
                    copy(a, 4 + j, relayed[j], me).wait_recv()
            for a in range(n):
                mine[a].wait()
                for cp in sends(a):
                    cp.wait_send()

    any_spec = pl.BlockSpec(memory_space=pl.ANY)
    vec = pl.BlockSpec((1, D), lambda jj, i, o: (0, 0))
    outs = pl.pallas_call(
        body, name="proj_fwd_gather",
        grid_spec=pltpu.PrefetchScalarGridSpec(
            num_scalar_prefetch=1, grid=(NDEV, ni),
            in_specs=[pl.BlockSpec((tm, D), lambda jj, i, o: (jnp.where(jj == 0, i, ni - 1), 0))] + [vec] * 3
                     + [any_spec] * n,
            out_specs=[pl.BlockSpec((tm, SHARD), lambda jj, i, o: (i, o[jj])),
                       pl.BlockSpec((D, tm), lambda jj, i, o: (0, jnp.where(jj == 0, i, ni - 1)))]
                      + [any_spec] * n,
            scratch_shapes=[pltpu.VMEM((ni, tm, D), BF16), pltpu.VMEM((2, D, SHARD), BF16),
                            pltpu.SemaphoreType.DMA((n, 7)), pltpu.SemaphoreType.DMA((n, 7)),
                            pltpu.SemaphoreType.DMA((n,)), pltpu.SemaphoreType.DMA((2,))]),
        out_shape=[jax.ShapeDtypeStruct((s, NIN), BF16), jax.ShapeDtypeStruct((D, s), BF16),
                   jax.ShapeDtypeStruct((NDEV, D, SHARD), BF16)]
                  + [jax.ShapeDtypeStruct((NDEV,) + e.shape, e.dtype) for e in extras],
        compiler_params=_cp(("arbitrary", "arbitrary"), 56))(order, x, nw, scale, shift, w_shard, *extras)
    return outs[0], outs[1], outs[2], outs[3:]


def proj_bwd(ht, dproj, wg, smalls, order, x, dy, nw, scale, tt):
    s = dproj.shape[0]
    nk = s // tt
    n = len(smalls)
    rows_per_step = tt // nk
    last = 2 * NDEV

    def body(order_ref, ht_ref, dp_ref, w_ref, x_ref, dy_ref, nw_ref, sc_ref, *rest):
        small_in = rest[:n]
        gx_ref, st_ref, gw_ref, rwin_ref = rest[n:n + 4]
        small_out = rest[n + 4:2 * n + 4]
        acc, stage, dh, send_sems, recv_sems, local_sems, stage_sems = rest[2 * n + 4:]
        t, k = pl.program_id(0), pl.program_id(1)
        me_xyc = _coords()
        me = _dev_index(me_xyc)
        peers = [_flip(me_xyc, f) for f in FLIPS]

        def exchange(a, kf, src_arr, dst_arr):
            pid = _dev_index(peers[kf])
            mk = lambda dst: pltpu.make_async_remote_copy(
                src_ref=src_arr.at[pid], dst_ref=dst, send_sem=send_sems.at[a, kf], recv_sem=recv_sems.at[a, kf],
                device_id=peers[kf], device_id_type=MESH)
            return mk(dst_arr.at[me]), mk(dst_arr.at[pid])

        small_pairs = [exchange(1 + a, kf, small_in[a], small_out[a]) for kf in range(7) for a in range(n)]
        small_own = [pltpu.make_async_copy(small_in[a].at[me], small_out[a].at[me], local_sems.at[1 + a])
                     for a in range(n)]
        win_pairs = [exchange(0, kf, gw_ref, rwin_ref) for kf in range(7)]
        win_own = pltpu.make_async_copy(gw_ref.at[me], rwin_ref.at[me], local_sems.at[0])

        def to_hbm(jj):
            slab = me if jj == 7 else _dev_index(peers[jj])
            return pltpu.make_async_copy(stage.at[jj % 2], gw_ref.at[slab], stage_sems.at[jj % 2])

        @pl.when((t == 0) & (k == 0))
        def _():
            for cp in small_own:
                cp.start()
            for send, _ in small_pairs:
                send.start()

        @pl.when(t < NDEV)
        def _():
            p = jnp.dot(ht_ref[...], dp_ref[...], preferred_element_type=F32)

            @pl.when(k == 0)
            def _():
                acc[...] = p

            @pl.when(k > 0)
            def _():
                acc[...] += p

        for jj in range(NDEV):
            @pl.when((t == jj) & (k == nk - 1))
            def _(jj=jj):
                stage[jj % 2] = acc[...].astype(BF16)
                to_hbm(jj).start()

            @pl.when((t == jj + 1) & (k == 1))
            def _(jj=jj):
                to_hbm(jj).wait()
                if jj < 7:
                    win_pairs[jj][0].start()
                else:
                    win_own.start()

        def matmul_step():
            p = lax.dot_general(dp_ref[...], w_ref[...], NT, preferred_element_type=F32)
            slot = t % 2
            dh[slot] = jnp.where(k == 0, p, dh[slot] + p)

        def norm_step():
            g = dh.at[(t + 1) % 2][pl.ds(pl.multiple_of(k * rows_per_step, rows_per_step), rows_per_step), :]
            xf = x_ref[...]
            r = lax.rsqrt(jnp.mean(xf * xf, axis=-1, keepdims=True) + EPS)
            xh = xf * r
            dn = g * (1.0 + sc_ref[...])
            dxh = dn * nw_ref[...]
            gx_ref[...] = dy_ref[...] + r * (dxh - xh * jnp.mean(dxh * xh, axis=-1, keepdims=True))
            st_ref[0:1, :] += jnp.sum(g, axis=0, keepdims=True)
            st_ref[1:2, :] += jnp.sum(g * xh * nw_ref[...], axis=0, keepdims=True)
            st_ref[2:3, :] += jnp.sum(dn * xh, axis=0, keepdims=True)

        @pl.when((t == 0) & (k == 0))
        def _():
            st_ref[...] = jnp.zeros_like(st_ref)

        @pl.when(t == NDEV)
        def _():
            matmul_step()

        @pl.when((t > NDEV) & (t < last))
        def _():
            matmul_step()
            norm_step()

        @pl.when(t == last)
        def _():
            norm_step()

        @pl.when((t == last) & (k == nk - 1))
        def _():
            for _, recv in win_pairs + small_pairs:
                recv.wait_recv()
            for send, _ in win_pairs + small_pairs:
                send.wait_send()
            win_own.wait()
            for cp in small_own:
                cp.wait()

    any_spec = pl.BlockSpec(memory_space=pl.ANY)
    first = lambda t: t < NDEV
    slab = lambda t, k: jnp.where(t == last, NDEV - 1, k)
    chunk = pl.BlockSpec((rows_per_step, D), lambda t, k, o: (jnp.maximum((t - NDEV - 1) * nk + k, 0), 0))
    vec = pl.BlockSpec((1, D), lambda t, k, o: (0, 0))
    outs = pl.pallas_call(
        body, name="proj_bwd",
        grid_spec=pltpu.PrefetchScalarGridSpec(
            num_scalar_prefetch=1, grid=(last + 1, nk),
            in_specs=[pl.BlockSpec((D, tt), lambda t, k, o: (0, jnp.where(first(t), k, nk - 1))),
                      pl.BlockSpec((tt, SHARD), lambda t, k, o: (jnp.where(first(t), k, jnp.minimum(t, last - 1) - NDEV),
                                                                 jnp.where(first(t), o[jnp.minimum(t, NDEV - 1)],
                                                                           slab(t, k)))),
                      pl.BlockSpec((None, D, SHARD), lambda t, k, o: (jnp.where(first(t), 0, slab(t, k)), 0, 0)),
                      chunk, chunk, vec, vec]
                     + [any_spec] * n,
            out_specs=[chunk, pl.BlockSpec((8, D), lambda t, k, o: (0, 0))] + [any_spec] * (2 + n),
            scratch_shapes=[pltpu.VMEM((D, SHARD), F32), pltpu.VMEM((2, D, SHARD), BF16),
                            pltpu.VMEM((2, tt, D), F32),
                            pltpu.SemaphoreType.DMA((1 + n, 7)), pltpu.SemaphoreType.DMA((1 + n, 7)),
                            pltpu.SemaphoreType.DMA((1 + n,)), pltpu.SemaphoreType.DMA((2,))]),
        out_shape=[jax.ShapeDtypeStruct((s, D), F32), jax.ShapeDtypeStruct((8, D), F32),
                   jax.ShapeDtypeStruct((NDEV, D, SHARD), BF16), jax.ShapeDtypeStruct((NDEV, D, SHARD), BF16)]
                  + [jax.ShapeDtypeStruct(a.shape, a.dtype) for a in smalls],
        compiler_params=_cp(("arbitrary", "arbitrary"), 56))(order, ht, dproj, wg, x, dy, nw, scale, *smalls)
    return outs[0], outs[1], outs[3], outs[4:]


def _head_matrices():
    lane = lax.broadcasted_iota(jnp.int32, (CB, CB), 0)
    col = lax.broadcasted_iota(jnp.int32, (CB, CB), 1)
    same = (lane // HD == col // HD).astype(BF16)
    lane_c = lax.broadcasted_iota(jnp.int32, (CB, LANES), 0)
    col_c = lax.broadcasted_iota(jnp.int32, (CB, LANES), 1)
    total = (lane_c // HD == col_c).astype(BF16)
    lane_e = lax.broadcasted_iota(jnp.int32, (LANES, CB), 0)
    col_e = lax.broadcasted_iota(jnp.int32, (LANES, CB), 1)
    expand = (lane_e == col_e // HD).astype(BF16)
    return same, total, expand


def _head_sum(x, m_ref):
    return jnp.dot(x.astype(BF16), m_ref[...], preferred_element_type=F32)


def _dot_hilo(x, m_ref):
    hi = x.astype(BF16)
    lo = (x - hi.astype(F32)).astype(BF16)
    return (jnp.dot(hi, m_ref[...], preferred_element_type=F32)
            + jnp.dot(lo, m_ref[...], preferred_element_type=F32))


def _to_residue_major(val, buf, out_ref, dil):
    rows = out_ref.shape[1]
    for k in range(val.shape[1] // LANES):
        lanes = slice(k * LANES, (k + 1) * LANES)
        buf[k] = val[:, lanes]
        for r in range(dil):
            out_ref[r, :, lanes] = buf.at[k][pl.ds(r, rows, stride=dil), :].astype(out_ref.dtype)


def _from_residue_major(ref, buf, dil):
    if dil == 1:
        return ref[0].astype(F32)
    rows, chunks = ref.shape[1], ref.shape[2] // LANES
    for k in range(chunks):
        for r in range(dil):
            buf.at[k][pl.ds(r, rows, stride=dil), :] = ref[r, :, k * LANES:(k + 1) * LANES].astype(F32)
    return jnp.concatenate([buf[k] for k in range(chunks)], axis=1)


def qkv_prep(proj, qw8, kw8, same, tm):
    s = proj.shape[0]
    items = []
    for g, d in enumerate(DILATIONS):
        items += [(g, "q", CB_Q + g, d), (g, "k", CB_K + g, d)] + ([(g, "v", CB_V + g, d)] if d > 1 else [])
    n = len(items)

    def body(*refs):
        ins, (qw_ref, kw_ref, same_ref), outs, buf = refs[:n], refs[n:n + 3], refs[n + 3:2 * n + 3], refs[-1]
        for idx, (_, kind, _, dil) in enumerate(items):
            val = ins[idx][...].astype(F32)
            if kind != "v":
                r = lax.rsqrt(_head_sum(val * val, same_ref) * (1.0 / HD) + EPS)
                val = val * r * (qw_ref if kind == "q" else kw_ref)[...]
            if dil == 1:
                outs[idx][0] = val.astype(BF16)
            else:
                _to_residue_major(val, buf, outs[idx], dil)

    full = lambda a: pl.BlockSpec(a.shape, lambda i: (0, 0))
    outs = pl.pallas_call(
        body, name="qkv_prep", grid=(s // tm,),
        in_specs=[pl.BlockSpec((tm, CB), lambda i, cb=cb: (i, cb)) for _, _, cb, _ in items]
                 + [full(qw8), full(kw8), full(same)],
        out_specs=[pl.BlockSpec((d, tm // d, CB), lambda i: (0, i, 0)) for _, _, _, d in items],
        out_shape=[jax.ShapeDtypeStruct((d, s // d, CB), BF16) for _, _, _, d in items],
        scratch_shapes=[pltpu.VMEM((CB // LANES, tm, LANES), F32)],
        compiler_params=_cp(("parallel",)))(*([proj] * n), qw8 * (HD ** -0.5), kw8, same)
    srcs = [[None, None, (proj, CB_V + g)] for g in range(len(DILATIONS))]
    for (g, kind, _, _), o in zip(items, outs):
        srcs[g]["qkv".index(kind)] = (o.reshape(s, CB), 0)
    return srcs


def grad_small_weights_and_stats(pairs, da, lc, dc, tk):
    s = da.shape[0]
    nk = s // tk
    n = len(pairs)
    shapes = [(a.shape[1], b.shape[1]) for a, b in pairs]
    tt = LANES * max(DILATIONS)
    per = tt // tk
    assert per >= len(DILATIONS)
    nda = len(DILATIONS) - 1

    def stats_of_group(dil, src_refs, dst_refs, sbuf):
        rows = tt // dil
        dst_refs = list(dst_refs)
        for src in src_refs:
            dst = dst_refs.pop(0) if dil > 1 else None
            dst_t = dst_refs.pop(0)
            sbuf[0] = src[...]
            for r in range(dil):
                piece = sbuf.at[0][pl.ds(r, rows, stride=dil), :] if dil > 1 else sbuf[0]
                if dil > 1:
                    dst[r] = piece
                dst_t[r] = piece.T[0:NH, :]

    def body(*refs):
        ins, (da_ref, lc_ref, dc_ref) = refs[:2 * n], refs[2 * n:2 * n + 3]
        rest = refs[2 * n + 3:]
        outs, dap, souts = rest[:n], rest[n:n + nda], list(rest[n + nda:-(n + 2)])
        accs, buf, sbuf = rest[-(n + 2):-2], rest[-2], rest[-1]
        k = pl.program_id(0)

        @pl.when(k == 0)
        def _():
            for j in range(n):
                accs[j][...] = jnp.zeros(shapes[j], F32)

        for j in range(n):
            accs[j][...] += lax.dot_general(ins[2 * j][...], ins[2 * j + 1][...], TN, preferred_element_type=F32)
        val = da_ref[...].astype(F32)
        for j, dil in enumerate(DILATIONS[1:]):
            _to_residue_major(val, buf, dap[j], dil)
        for phase, dil in enumerate(DILATIONS):
            dsts = [souts.pop(0) for _ in range(4 if dil > 1 else 2)]

            @pl.when(k % per == phase)
            def _(dil=dil, dsts=dsts):
                stats_of_group(dil, (lc_ref, dc_ref), dsts, sbuf)

        @pl.when(k == nk - 1)
        def _():
            for j in range(n):
                outs[j][...] = accs[j][...].astype(BF16)

    out_specs = [pl.BlockSpec(sh, lambda k: (0, 0)) for sh in shapes]
    out_shape = [jax.ShapeDtypeStruct(sh, BF16) for sh in shapes]
    for dil in DILATIONS[1:]:
        out_specs.append(pl.BlockSpec((dil, tk // dil, CB), lambda k: (0, k, 0)))
        out_shape.append(jax.ShapeDtypeStruct((dil, s // dil, CB), BF16))
    for dil in DILATIONS:
        rm = (pl.BlockSpec((dil, tt // dil, LANES), lambda k: (0, k // per, 0)),
              jax.ShapeDtypeStruct((dil, s // dil, LANES), F32))
        tr = (pl.BlockSpec((dil, NH, tt // dil), lambda k: (0, 0, k // per)),
              jax.ShapeDtypeStruct((dil, NH, s // dil), F32))
        group = [rm, tr, rm, tr] if dil > 1 else [tr, tr]
        out_specs += [sp for sp, _ in group]
        out_shape += [sh for _, sh in group]
    outs = list(pl.pallas_call(
        body, name="grad_small_weights_and_stats", grid=(nk,),
        in_specs=[pl.BlockSpec((tk, t.shape[1]), lambda k: (k, 0)) for pair in pairs for t in pair]
                 + [pl.BlockSpec((tk, CB), lambda k: (k, 0))]
                 + [pl.BlockSpec((tt, LANES), lambda k: (k // per, 0))] * 2,
        out_specs=out_specs, out_shape=out_shape,
        scratch_shapes=[pltpu.VMEM(sh, F32) for sh in shapes]
                       + [pltpu.VMEM((CB // LANES, tk, LANES), F32), pltpu.VMEM((1, tt, LANES), F32)],
        compiler_params=_cp(("arbitrary",), 56))(*[t for pair in pairs for t in pair], da, lc, dc))
    grads_w, daps, outs = outs[:n], outs[n:n + nda], outs[n + nda:]
    res = []
    for dil in DILATIONS:
        flat_t = lambda t, dil=dil: t.reshape(dil * NH, s // dil)
        if dil == 1:
            lt, dt = outs.pop(0), outs.pop(0)
            res.append((da, lc, dc, flat_t(lt), flat_t(dt)))
        else:
            lcp, lt, dcp, dt = (outs.pop(0) for _ in range(4))
            res.append((daps.pop(0).reshape(s, CB), lcp.reshape(s, LANES), dcp.reshape(s, LANES),
                        flat_t(lt), flat_t(dt)))
    return grads_w, res


def qkv_grads_to_dproj(dproj, proj, grads, qw8, kw8, same, tm):
    s = dproj.shape[0]
    ni = s // tm
    flat = [(t.reshape(d, s // d, CB), d, kind, 3 * kind + g)
            for g, d in enumerate(DILATIONS) for kind, t in enumerate(grads[g])]
    nf = len(flat)
    nraw = 2 * len(DILATIONS)

    def body(*refs):
        dp_hbm, raws, ins = refs[nraw + nf + 4], refs[1:1 + nraw], refs[1 + nraw:1 + nraw + nf]
        qw_ref, kw_ref, same_ref = refs[1 + nraw + nf:4 + nraw + nf]
        gw_ref, stage, buf, sems = refs[5 + nraw + nf:]
        i = pl.program_id(0)
        slot = i % 2

        def slab(step, sl):
            return pltpu.make_async_copy(
                stage.at[sl], dp_hbm.at[pl.ds(pl.multiple_of(step * tm, tm), tm), pl.ds(CB_Q * CB, 9 * CB)],
                sems.at[sl])

        @pl.when(i == 0)
        def _():
            gw_ref[...] = jnp.zeros_like(gw_ref)

        @pl.when(i >= 2)
        def _():
            slab(i - 2, slot).wait()

        for ref, (_, d, kind, jj) in zip(ins, flat):
            cols = slice(jj * CB, (jj + 1) * CB)
            dn = _from_residue_major(ref, buf, d)
            if kind == 2:
                stage[slot, :, cols] = dn.astype(BF16)
                continue
            t = raws[jj][...].astype(F32)
            r = lax.rsqrt(_head_sum(t * t, same_ref) * (1.0 / HD) + EPS)
            xh = t * r
            gw_ref[kind:kind + 1, :] += jnp.sum(dn * xh, axis=0, keepdims=True)
            dxh = dn * (qw_ref if kind == 0 else kw_ref)[...]
            mean = _head_sum(dxh * xh, same_ref) * (1.0 / HD)
            stage[slot, :, cols] = (r * (dxh - xh * mean)).astype(BF16)
        slab(i, slot).start()

        @pl.when(i == ni - 1)
        def _():
            slab(i - 1, 1 - slot).wait()
            slab(i, slot).wait()

    full = lambda a: pl.BlockSpec(a.shape, lambda i: (0, 0))
    any_spec = pl.BlockSpec(memory_space=pl.ANY)
    return pl.pallas_call(
        body, name="qkv_grads_to_dproj", grid=(ni,),
        in_specs=[any_spec] + [pl.BlockSpec((tm, CB), lambda i, jb=jb: (i, CB_Q + jb)) for jb in range(nraw)]
                 + [pl.BlockSpec((d, tm // d, CB), lambda i: (0, i, 0)) for _, d, _, _ in flat]
                 + [full(qw8), full(kw8), full(same)],
        out_specs=[any_spec, pl.BlockSpec((8, CB), lambda i: (0, 0))],
        out_shape=[jax.ShapeDtypeStruct((s, NIN), BF16), jax.ShapeDtypeStruct((8, CB), F32)],
        input_output_aliases={0: 0},
        scratch_shapes=[pltpu.VMEM((2, tm, 9 * CB), BF16), pltpu.VMEM((CB // LANES, tm, LANES), F32),
                        pltpu.SemaphoreType.DMA((2,))],
        compiler_params=_cp(("arbitrary",)))(
            dproj, *([proj] * nraw), *[t for t, _, _, _ in flat], qw8, kw8, same)


def _lane_lo():
    return lax.broadcasted_iota(jnp.int32, (1, 2 * HD), 1) < HD


def _stack_heads(t, lo):
    zero = jnp.zeros_like(t)
    return jnp.concatenate([jnp.where(lo, t, zero), jnp.where(lo, zero, t)], axis=0)


def _masks(other_ok):
    qi = lax.broadcasted_iota(jnp.int32, (QB, QB), 0)
    kj = lax.broadcasted_iota(jnp.int32, (QB, QB), 1)
    return (kj >= qi) & other_ok, kj <= qi


ATTN_SUB = 4


def attn_fwd(srcs):
    s = srcs[0][0][0].shape[0]
    ng = len(DILATIONS)

    def body(*refs):
        ins, outs, bufs = refs[:5 * ng], refs[5 * ng:7 * ng], refs[7 * ng:]
        step = pl.program_id(0)
        lo = _lane_lo()
        head_lane = lax.broadcasted_iota(jnp.int32, (1, LANES), 1)
        for g, dil in enumerate(DILATIONS):
            nb = s // dil // QB
            q_ref, kp_ref, k_ref, vp_ref, v_ref = ins[5 * g:5 * g + 5]
            (o_ref, l_ref), (kbuf, vbuf) = outs[2 * g:2 * g + 2], bufs[2 * g:2 * g + 2]
            kbuf[0:QB], kbuf[QB:] = kp_ref[...], k_ref[...]
            vbuf[0:QB], vbuf[QB:] = vp_ref[...], v_ref[...]
            for j in range(ATTN_SUB):
                rows, krows = slice(j * QB, (j + 1) * QB), slice(j * QB, (j + 2) * QB)
                m_prev, m_cur = _masks((step * ATTN_SUB + j) % nb > 0)
                mask = jnp.concatenate([m_prev, m_cur], axis=1)
                mask = jnp.concatenate([mask, mask], axis=0)
                lses = jnp.zeros((QB, LANES), F32)
                for i in range(NH // 2):
                    sl = slice(2 * HD * i, 2 * HD * (i + 1))
                    qs, ks, vv = q_ref[rows, sl], kbuf[krows, sl], vbuf[krows, sl]
                    sc = lax.dot_general(_stack_heads(qs, lo), ks, NT, preferred_element_type=F32)
                    sc = jnp.where(mask, sc, NEG)
                    mx = jnp.max(sc, axis=-1, keepdims=True)
                    p = jnp.exp(sc - mx)
                    den = jnp.sum(p, axis=-1, keepdims=True)
                    o = jnp.dot(p.astype(BF16), vv, preferred_element_type=F32) * (1.0 / den)
                    lse = mx + jnp.log(den)
                    o_ref[rows, sl] = jnp.where(lo, o[:QB], o[QB:]).astype(BF16)
                    lses = jnp.where(head_lane == 2 * i, lse[:QB], jnp.where(head_lane == 2 * i + 1, lse[QB:], lses))
                l_ref[rows, :] = lses

    main = lambda cb, w=CB: pl.BlockSpec((ATTN_SUB * QB, w), lambda st: (st, cb))
    prev = lambda cb: pl.BlockSpec((QB, CB), lambda st: (jnp.maximum(ATTN_SUB * st - 1, 0), cb))
    in_specs, args = [], []
    for q_src, k_src, v_src in srcs:
        in_specs += [main(q_src[1]), prev(k_src[1]), main(k_src[1]), prev(v_src[1]), main(v_src[1])]
        args += [q_src[0], k_src[0], k_src[0], v_src[0], v_src[0]]
    outs = pl.pallas_call(
        body, name="attn_fwd", grid=(s // (ATTN_SUB * QB),),
        in_specs=in_specs, out_specs=[main(0), main(0, LANES)] * ng,
        out_shape=[jax.ShapeDtypeStruct((s, CB), BF16), jax.ShapeDtypeStruct((s, LANES), F32)] * ng,
        scratch_shapes=[pltpu.VMEM(((ATTN_SUB + 1) * QB, CB), BF16)] * (2 * ng),
        compiler_params=_cp(("parallel",)))(*args)
    return outs[0::2], outs[1::2]


def attn_bwd(srcs, stats):
    s = srcs[0][0][0].shape[0]
    ng = len(DILATIONS)
    sub = ATTN_SUB
    nin, nout, nbuf = 14, 3, 6

    def body(*refs):
        ins, outs, bufs = refs[:nin * ng], refs[nin * ng:(nin + nout) * ng], refs[(nin + nout) * ng:]
        step = pl.program_id(0)
        lo = _lane_lo()
        kj = lax.broadcasted_iota(jnp.int32, (QB, QB), 0)
        qi = lax.broadcasted_iota(jnp.int32, (QB, QB), 1)

        def block(group_refs, nb, j):
            (q_ref, _, k_ref, _, v_ref, _, da_ref, _, lc_ref, dc_ref, _, _, _, _,
             dq_ref, dk_ref, dv_ref, kbuf, vbuf, qbuf, dabuf, lbuf, dbuf) = group_refs
            rows, two = slice(j * QB, (j + 1) * QB), slice(j * QB, (j + 2) * QB)
            place = (step * sub + j) % nb
            m_prev, m_cur = _masks(place > 0)
            qmask = jnp.concatenate([m_prev, m_cur], axis=1)
            qmask = jnp.concatenate([qmask, qmask], axis=0)
            lcols, dcols = lc_ref[rows, :], dc_ref[rows, :]
            kmask = jnp.concatenate([kj <= qi, (kj >= qi) & (place < nb - 1)], axis=1)
            kmask = jnp.concatenate([kmask, kmask], axis=1)
            lrow = jnp.concatenate([lbuf[j], lbuf[j + 1]], axis=1)
            drow = jnp.concatenate([dbuf[j], dbuf[j + 1]], axis=1)
            for i in range(NH // 2):
                sl = slice(2 * HD * i, 2 * HD * (i + 1))
                col_pair = lambda t: jnp.concatenate([t[:, 2 * i:2 * i + 1], t[:, 2 * i + 1:2 * i + 2]], axis=0)
                row_pair = lambda t: jnp.concatenate([t[2 * i:2 * i + 1, :], t[2 * i + 1:2 * i + 2, :]], axis=1)
                ks2, vv2 = kbuf[two, sl], vbuf[two, sl]
                sc = lax.dot_general(_stack_heads(q_ref[rows, sl], lo), ks2, NT, preferred_element_type=F32)
                p = jnp.exp(jnp.where(qmask, sc, NEG) - col_pair(lcols))
                dp = lax.dot_general(_stack_heads(da_ref[rows, sl], lo), vv2, NT, preferred_element_type=F32)
                ds = p * (dp - col_pair(dcols))
                dq = jnp.dot(ds.astype(BF16), ks2, preferred_element_type=F32)
                dq_ref[rows, sl] = (jnp.where(lo, dq[:QB], dq[QB:]) * (HD ** -0.5)).astype(BF16)

                q2, da2 = _stack_heads(qbuf[two, sl], lo), _stack_heads(dabuf[two, sl], lo)
                ks, vv = k_ref[rows, sl], v_ref[rows, sl]
                sct = lax.dot_general(ks, q2, NT, preferred_element_type=F32)
                pt = jnp.exp(jnp.where(kmask, sct, NEG) - row_pair(lrow))
                dpt = lax.dot_general(vv, da2, NT, preferred_element_type=F32)
                dst = pt * (dpt - row_pair(drow))
                dv_ref[rows, sl] = jnp.dot(pt.astype(BF16), da2, preferred_element_type=F32).astype(BF16)
                dk_ref[rows, sl] = jnp.dot(dst.astype(BF16), q2, preferred_element_type=F32).astype(BF16)

        for g, dil in enumerate(DILATIONS):
            group_refs = (ins[nin * g:nin * (g + 1)] + outs[nout * g:nout * (g + 1)] + bufs[nbuf * g:nbuf * (g + 1)])
            (q_ref, kp_ref, k_ref, vp_ref, v_ref, qn_ref, da_ref, dan_ref, _, _, l_ref, ln_ref, d_ref, dn_ref,
             _, _, _, kbuf, vbuf, qbuf, dabuf, lbuf, dbuf) = group_refs
            kbuf[0:QB], kbuf[QB:] = kp_ref[...], k_ref[...]
            vbuf[0:QB], vbuf[QB:] = vp_ref[...], v_ref[...]
            qbuf[0:sub * QB], qbuf[sub * QB:] = q_ref[...], qn_ref[...]
            dabuf[0:sub * QB], dabuf[sub * QB:] = da_ref[...], dan_ref[...]
            for c in range(sub):
                lbuf[c], dbuf[c] = l_ref[:, c * QB:(c + 1) * QB], d_ref[:, c * QB:(c + 1) * QB]
            lbuf[sub], dbuf[sub] = ln_ref[...], dn_ref[...]
            for j in range(sub):
                block(group_refs, s // dil // QB, j)

    last = s // QB - 1
    main = lambda cb, w=CB: pl.BlockSpec((sub * QB, w), lambda st: (st, cb))
    prev = lambda cb: pl.BlockSpec((QB, CB), lambda st: (jnp.maximum(sub * st - 1, 0), cb))
    nxt = lambda cb: pl.BlockSpec((QB, CB), lambda st: (jnp.minimum(sub * (st + 1), last), cb))
    in_specs, args = [], []
    for (q_src, k_src, v_src), (da, lc, dc, lt, dt), dil in zip(srcs, stats, DILATIONS):
        nb = s // dil // QB
        t_main = pl.BlockSpec((NH, sub * QB), lambda st, nb=nb: (sub * st // nb, (sub * st % nb) // sub))
        t_nxt = pl.BlockSpec((NH, QB), lambda st, nb=nb: (sub * st // nb, jnp.minimum(sub * st % nb + sub, nb - 1)))
        in_specs += [main(q_src[1]), prev(k_src[1]), main(k_src[1]), prev(v_src[1]), main(v_src[1]), nxt(q_src[1]),
                     main(0), nxt(0), main(0, LANES), main(0, LANES), t_main, t_nxt, t_main, t_nxt]
        args += [q_src[0], k_src[0], k_src[0], v_src[0], v_src[0], q_src[0], da, da, lc, dc, lt, lt, dt, dt]
    out = jax.ShapeDtypeStruct((s, CB), BF16)
    big = pltpu.VMEM(((sub + 1) * QB, CB), BF16)
    outs = pl.pallas_call(
        body, name="attn_bwd", grid=(s // (sub * QB),),
        in_specs=in_specs, out_specs=[main(0)] * (nout * ng), out_shape=[out] * (nout * ng),
        scratch_shapes=([big] * 4 + [pltpu.VMEM((sub + 1, NH, QB), F32)] * 2) * ng,
        compiler_params=_cp(("parallel",), 56))(*args)
    return [tuple(outs[nout * g:nout * (g + 1)]) for g in range(ng)]


def _conv_taps(u, u_prev, first):
    tm = u.shape[0]
    row = lax.broadcasted_iota(jnp.int32, (tm, 1), 0)
    up = jnp.where(first, 0.0, u_prev)
    u1 = jnp.where(row == 0, up[HALO - 1:HALO, :], pltpu.roll(u, 1, 0))
    u2 = jnp.where(row == 0, up[HALO - 2:HALO - 1, :],
                   jnp.where(row == 1, up[HALO - 1:HALO, :], pltpu.roll(u, 2, 0)))
    return u1, u2


def mid_fwd(proj, o_g, lse_g, conv_w, expand, tm):
    s = proj.shape[0]
    hb = tm // HALO

    def body(ba_ref, ca_ref, xa_ref, za_ref, cah_ref, xah_ref, zb_ref,
             o0, o1, o2, l0, l1, l2, w_ref, exp_ref, ya_ref, yb_ref, at_ref, lc_ref, buf_o, buf_l):
        first = pl.program_id(0) == 0
        u = ca_ref[...].astype(F32) * xa_ref[...].astype(F32)
        u1, u2 = _conv_taps(u, cah_ref[...].astype(F32) * xah_ref[...].astype(F32), first)
        conv = w_ref[0:1, :] * u2 + w_ref[1:2, :] * u1 + w_ref[2:3, :] * u
        ya_ref[...] = (ba_ref[...].astype(F32) * conv * _silu(za_ref[...].astype(F32))).astype(BF16)
        ls = [_from_residue_major(l, buf_l.at[g], d) for g, (l, d) in enumerate(zip((l0, l1, l2), DILATIONS))]
        mx = jnp.maximum(jnp.maximum(ls[0], ls[1]), ls[2])
        es = [jnp.exp(l - mx) for l in ls]
        den = es[0] + es[1] + es[2]
        attn = jnp.zeros((tm, CB), F32)
        for e, o, d in zip(es, (o0, o1, o2), DILATIONS):
            attn = attn + _dot_hilo(e / den, exp_ref) * _from_residue_major(o, buf_o, d)
        at_ref[...] = attn
        lc_ref[...] = mx + jnp.log(den)
        yb_ref[...] = (attn * _silu(zb_ref[...].astype(F32))).astype(BF16)

    col = lambda j: pl.BlockSpec((tm, D), lambda i: (i, j))
    halo = lambda j: pl.BlockSpec((HALO, D), lambda i: (jnp.maximum(i * hb - 1, 0), j))
    loc = lambda w: pl.BlockSpec((tm, w), lambda i: (i, 0))
    rm = lambda w: [pl.BlockSpec((d, tm // d, w), lambda i: (0, i, 0)) for d in DILATIONS]
    rm_view = lambda ts, w: [t.reshape(d, s // d, w) for t, d in zip(ts, DILATIONS)]
    return pl.pallas_call(
        body, name="mid_fwd", grid=(s // tm,),
        in_specs=[col(0), col(1), col(2), col(3), halo(1), halo(2),
                  pl.BlockSpec((tm, CB), lambda i: (i, CB_ZB))] + rm(CB) + rm(LANES)
                 + [pl.BlockSpec((3, D), lambda i: (0, 0)), pl.BlockSpec(expand.shape, lambda i: (0, 0))],
        out_specs=[loc(D), loc(CB), loc(CB), loc(LANES)],
        out_shape=[jax.ShapeDtypeStruct((s, D), BF16), jax.ShapeDtypeStruct((s, CB), BF16),
                   jax.ShapeDtypeStruct((s, CB), F32), jax.ShapeDtypeStruct((s, LANES), F32)],
        scratch_shapes=[pltpu.VMEM((CB // LANES, tm, LANES), F32), pltpu.VMEM((3, 1, tm, LANES), F32)],
        compiler_params=_cp(("parallel",)))(
            proj, proj, proj, proj, proj, proj, proj, *rm_view(o_g, CB), *rm_view(lse_g, LANES), conv_w, expand)


def tail(proj, ya, yb, attn, x, target, gate, pa_w, pb_w, wo_w, total, conv_w, tm):
    s = proj.shape[0]
    ni = s // tm
    hb = tm // HALO
    nlate = NIN - CB_ZB * CB
    nearly = 4 * D

    def body(ya_ref, yb_ref, ga_ref, gb_ref, zb_ref, at_ref, x_ref, t_ref, gate_ref, pa_ref, pb_ref, wo_ref,
             tot_ref, ba_ref, ca_ref, xa_ref, za_ref, cah_ref, xah_ref, cw_ref,
             dp_hbm, dy_ref, da_ref, dc_ref, mg_ref, do_ref, dpa_ref, dpb_ref, st_ref, gwc_ref,
             stage, dconv_next, sems):
        step = pl.program_id(0)
        i = ni - 1 - step
        slot = step % 2

        def slabs(at_step, sl):
            rows = pl.ds(pl.multiple_of((ni - 1 - at_step) * tm, tm), tm)
            return (pltpu.make_async_copy(stage.at[sl, :, 0:nearly], dp_hbm.at[rows, pl.ds(0, nearly)],
                                          sems.at[sl, 0]),
                    pltpu.make_async_copy(stage.at[sl, :, nearly:], dp_hbm.at[rows, pl.ds(CB_ZB * CB, nlate)],
                                          sems.at[sl, 1]))

        @pl.when(step == 0)
        def _():
            st_ref[...] = jnp.zeros_like(st_ref)
            gwc_ref[...] = jnp.zeros_like(gwc_ref)
            dconv_next[...] = jnp.zeros_like(dconv_next)

        @pl.when(step >= 2)
        def _():
            for cp in slabs(step - 2, slot):
                cp.wait()

        gate_v = gate_ref[...]
        pa = jnp.dot(ya_ref[...], pa_ref[...], preferred_element_type=F32)
        pb = jnp.dot(yb_ref[...], pb_ref[...], preferred_element_type=F32)
        sa = jax.nn.sigmoid(ga_ref[...].astype(F32))
        sb = jax.nn.sigmoid(gb_ref[...].astype(F32))
        merged = (sa * pa + sb * pb).astype(BF16)
        mg_ref[...] = merged
        out = jnp.dot(merged, wo_ref[...], preferred_element_type=F32)
        err = x_ref[...] + gate_v * out - t_ref[...]
        dy = err * (1.0 / D)
        dy_ref[...] = dy
        st_ref[0:1, :] += jnp.sum(dy * out, axis=0, keepdims=True)
        st_ref[1:2, :] += jnp.sum(err * err, axis=0, keepdims=True)
        dout = (gate_v * dy).astype(BF16)
        do_ref[...] = dout
        dmg = lax.dot_general(dout, wo_ref[...], NT, preferred_element_type=F32)
        dpa = (dmg * sa).astype(BF16)
        dpb = (dmg * sb).astype(BF16)
        dpa_ref[...] = dpa
        dpb_ref[...] = dpb
        late = nearly
        stage[slot, :, late + CB:late + CB + D] = (dmg * pa * sa * (1.0 - sa)).astype(BF16)
        stage[slot, :, late + CB + D:] = (dmg * pb * sb * (1.0 - sb)).astype(BF16)
        dya = lax.dot_general(dpa, pa_ref[...], NT, preferred_element_type=F32)
        dyb = lax.dot_general(dpb, pb_ref[...], NT, preferred_element_type=F32)
        zb = zb_ref[...].astype(F32)
        sg = jax.nn.sigmoid(zb)
        attn_v = at_ref[...]
        dattn = dyb * (zb * sg)
        da_ref[...] = dattn.astype(BF16)
        stage[slot, :, late:late + CB] = (dyb * attn_v * (sg * (1.0 + zb * (1.0 - sg)))).astype(BF16)
        dc_ref[...] = _dot_hilo(dattn * attn_v, tot_ref)

        ba, ca, xa, za = (t[...].astype(F32) for t in (ba_ref, ca_ref, xa_ref, za_ref))
        u = ca * xa
        u1, u2 = _conv_taps(u, cah_ref[...].astype(F32) * xah_ref[...].astype(F32), i == 0)
        w0, w1, w2 = cw_ref[0:1, :], cw_ref[1:2, :], cw_ref[2:3, :]
        conv = w0 * u2 + w1 * u1 + w2 * u
        sga = jax.nn.sigmoid(za)
        sza = za * sga
        dconv = dya * ba * sza
        dcn = dconv_next[...]
        rowi = lax.broadcasted_iota(jnp.int32, (tm, 1), 0)
        d1 = jnp.where(rowi == tm - 1, dcn[0:1, :], pltpu.roll(dconv, tm - 1, 0))
        d2 = jnp.where(rowi == tm - 2, dcn[0:1, :],
                       jnp.where(rowi == tm - 1, dcn[1:2, :], pltpu.roll(dconv, tm - 2, 0)))
        du = w2 * dconv + w1 * d1 + w0 * d2
        stage[slot, :, 0:D] = (dya * conv * sza).astype(BF16)
        stage[slot, :, D:2 * D] = (du * xa).astype(BF16)
        stage[slot, :, 2 * D:3 * D] = (du * ca).astype(BF16)
        stage[slot, :, 3 * D:4 * D] = (dya * ba * conv * (sga * (1.0 + za * (1.0 - sga)))).astype(BF16)
        gwc_ref[0:1, :] += jnp.sum(dconv * u2, axis=0, keepdims=True)
        gwc_ref[1:2, :] += jnp.sum(dconv * u1, axis=0, keepdims=True)
        gwc_ref[2:3, :] += jnp.sum(dconv * u, axis=0, keepdims=True)
        dconv_next[...] = dconv[0:8, :]

        for cp in slabs(step, slot):
            cp.start()

        @pl.when(step == ni - 1)
        def _():
            for cp in slabs(step - 1, 1 - slot) + slabs(step, slot):
                cp.wait()

    rev = lambda st: ni - 1 - st
    row = lambda w: pl.BlockSpec((tm, w), lambda st: (rev(st), 0))
    pcol = lambda w, jb: pl.BlockSpec((tm, w), lambda st: (rev(st), jb))
    halo = lambda jb: pl.BlockSpec((HALO, D), lambda st: (jnp.maximum(rev(st) * hb - 1, 0), jb))
    const = lambda a: pl.BlockSpec(a.shape, lambda st: (0, 0), pipeline_mode=pl.Buffered(1))
    acc = pl.BlockSpec((8, D), lambda st: (0, 0))
    return pl.pallas_call(
        body, name="tail", grid=(ni,),
        in_specs=[row(D), row(CB), pcol(D, 9), pcol(D, 10), pcol(CB, CB_ZB), row(CB), row(D), row(D),
                  pl.BlockSpec((1, D), lambda st: (0, 0)), const(pa_w), const(pb_w), const(wo_w), const(total),
                  pcol(D, 0), pcol(D, 1), pcol(D, 2), pcol(D, 3), halo(1), halo(2),
                  pl.BlockSpec((3, D), lambda st: (0, 0))],
        out_specs=[pl.BlockSpec(memory_space=pl.ANY),
                   row(D), row(CB), row(LANES), row(D), row(D), row(D), row(D), acc, acc],
        out_shape=[jax.ShapeDtypeStruct((s, NIN), BF16), jax.ShapeDtypeStruct((s, D), F32),
                   jax.ShapeDtypeStruct((s, CB), BF16), jax.ShapeDtypeStruct((s, LANES), F32)]
                  + [jax.ShapeDtypeStruct((s, D), BF16)] * 4 + [jax.ShapeDtypeStruct((8, D), F32)] * 2,
        scratch_shapes=[pltpu.VMEM((2, tm, nearly + nlate), BF16), pltpu.VMEM((8, D), F32),
                        pltpu.SemaphoreType.DMA((2, 2))],
        compiler_params=_cp(("arbitrary",), 60))(
            ya, yb, proj, proj, proj, attn, x, target, gate, pa_w, pb_w, wo_w, total,
            proj, proj, proj, proj, proj, proj, conv_w)


def _local_step(x, target, shift, scale, gate, norm_w, conv_w, qw, kw, w_shard, small_shards, me_xyc):
    qw8, kw8 = jnp.tile(qw, (1, NH)), jnp.tile(kw, (1, NH))
    same, total, expand = _head_matrices()
    proj, ht, wg, (pa_g, pb_g, wo_g) = proj_fwd_gather(
        x, norm_w, scale, shift, w_shard, small_shards, gather_order(me_xyc), 1024)
    pa_w, wo_w = pa_g.reshape(D, D), wo_g.reshape(D, D)
    pb_w = pb_g.transpose(1, 0, 2).reshape(CB, D)
    srcs = qkv_prep(proj, qw8, kw8, same, 512)
    o_g, lse_g = attn_fwd(srcs)
    ya, yb, attn, lc = mid_fwd(proj, o_g, lse_g, conv_w, expand, 512)
    dproj, dy, da, dc, merged, dout, dpa, dpb, st_tail, st_conv = tail(
        proj, ya, yb, attn, x, target, gate, pa_w, pb_w, wo_w, total, conv_w, 256)
    (g_wo, g_pa, g_pb), stats = grad_small_weights_and_stats([(merged, dout), (ya, dpa), (yb, dpb)], da, lc, dc, 512)
    grads = attn_bwd(srcs, stats)
    dproj, gw_qk = qkv_grads_to_dproj(dproj, proj, grads, qw8, kw8, same, 512)
    slabs = [g_pa.reshape(NDEV, 128, D), g_pb.reshape(CB, NDEV, 128).transpose(1, 0, 2), g_wo.reshape(NDEV, 128, D)]
    grad_x, st_norm, r_win, (r_pa, r_pb, r_wo) = proj_bwd(
        ht, dproj, wg, slabs, scatter_order(me_xyc), x, dy, norm_w, scale, 1024)
    dmod = jnp.concatenate([st_norm[0:1], st_norm[1:2], st_tail[0:1]], axis=1)
    loss_part = (0.5 / D) * jnp.sum(st_tail[1])
    gw_heads = gw_qk[0:2].reshape(2, NH, HD).sum(axis=1)
    small = dict(dmod=dmod, norm_w=st_norm[2:3], conv_w=st_conv[0:3],
                 q_norm_w=gw_heads[0:1], k_norm_w=gw_heads[1:2], loss=loss_part)
    return grad_x, small, (r_win, r_pa, r_pb, r_wo)


def kernel(x, c, w_ada, b_ada, norm_w, w_in, conv_w, q_norm_w, k_norm_w, w_br_conv, w_br_attn, w_out, loss_target, m_w_ada, m_b_ada, m_norm_w, m_w_in, m_conv_w, m_q_norm_w, m_k_norm_w, m_w_br_conv, m_w_br_attn, m_w_out, v_w_ada, v_b_ada, v_norm_w, v_w_in, v_conv_w, v_q_norm_w, v_k_norm_w, v_w_br_conv, v_w_br_attn, v_w_out):
    me_xyc = (lax.axis_index("x"), lax.axis_index("y"), lax.axis_index("c"))
    me = _dev_index(me_xyc)
    ncol = w_ada.shape[2]

    conv_pad = jnp.zeros((8, 128), F32).at[0:3].set(conv_w[0])
    b_cols = lax.dynamic_slice(b_ada, (0, me * ncol), (1, ncol))
    mod_pieces, c_all, conv_all = ada_fwd(c, conv_pad, w_ada[0], b_cols)
    conv_full = conv_all[:, 0:3].transpose(1, 0, 2).reshape(3, D)
    c_all = c_all.reshape(NDEV, D)
    mod = mod_pieces.reshape(1, 3 * D)
    shift, scale, gate = mod[:, 0:D], mod[:, D:2 * D], mod[:, 2 * D:3 * D]

    grad_x, small, (r_win, r_pa, r_pb, r_wo) = _local_step(
        x[0], loss_target[0], shift, scale, gate, norm_w, conv_full, q_norm_w, k_norm_w,
        w_in[0].astype(BF16), [w_br_conv[0].astype(BF16), w_br_attn[0].astype(BF16), w_out[0].astype(BF16)], me_xyc)

    packed = jnp.concatenate(
        [small["dmod"], small["norm_w"], small["conv_w"].reshape(1, 3 * D), small["q_norm_w"], small["k_norm_w"],
         jnp.full((1, 128), small["loss"], F32)], axis=1)
    packed_all, tot = gather_sum(packed)
    loss = tot[0, 7 * D + 2 * HD]
    dmod_all = packed_all[:, 0, 0:3 * D]
    g_b_ada = tot[:, 0:3 * D]
    g_norm_w = tot[:, 3 * D:4 * D]
    g_conv = lax.dynamic_slice(tot[:, 4 * D:7 * D].reshape(3, D), (0, me * 128), (3, 128))
    g_qn = tot[:, 7 * D:7 * D + HD]
    g_kn = tot[:, 7 * D + HD:7 * D + 2 * HD]
    g_w_ada = ada_bwd(c_all.T, lax.dynamic_slice(dmod_all, (0, me * ncol), (NDEV, ncol)))

    def upd(parts, w, m, v, name, rows):
        shape = w.shape
        w2, m2, v2 = (t.reshape(shape[-2:]) for t in (w, m, v))
        return [t.reshape(shape) for t in adamw(parts, w2, m2, v2, name, rows)]

    res = {"w_in": upd(r_win, w_in, m_w_in, v_w_in, "adamw_w_in", 128)}
    small_params = {"w_ada": (g_w_ada[None], w_ada, m_w_ada, v_w_ada), "b_ada": (g_b_ada[None], b_ada, m_b_ada, v_b_ada),
                    "norm_w": (g_norm_w[None], norm_w, m_norm_w, v_norm_w),
                    "conv_w": (g_conv[None], conv_w, m_conv_w, v_conv_w),
                    "q_norm_w": (g_qn[None], q_norm_w, m_q_norm_w, v_q_norm_w),
                    "k_norm_w": (g_kn[None], k_norm_w, m_k_norm_w, v_k_norm_w),
                    "w_br_conv": (r_pa, w_br_conv, m_w_br_conv, v_w_br_conv),
                    "w_br_attn": (r_pb, w_br_attn, m_w_br_attn, v_w_br_attn),
                    "w_out": (r_wo, w_out, m_w_out, v_w_out)}
    updated = adamw_small([(item[0],) + tuple(t.reshape(t.shape[-2:]) for t in item[1:])
                           for item in small_params.values()])
    for (pname, item), outs4 in zip(small_params.items(), updated):
        res[pname] = [t.reshape(item[1].shape) for t in outs4]
    names = ["w_ada", "b_ada", "norm_w", "w_in", "conv_w", "q_norm_w", "k_norm_w", "w_br_conv", "w_br_attn", "w_out"]
    return (loss, grad_x[None], *[res[n][0] for n in names], *[res[n][1] for n in names],
            *[res[n][2] for n in names], *[res[n][3] for n in names])
```
